```python
import math
import jax, jax.numpy as jnp
from jax import lax
import numpy as np

D_MODEL = 2048
BATCH = 8
SEQ = 4096
DEPTH = 1

MEM_LEN = 256
EPS = 1e-6
Q_BLOCK = 128
NEG_BIG = 1e9

MLA_HEADS = 8
MLA_NOPE = 128
MLA_ROPE = 64
MLA_V = 128
MLA_Q_RANK = 512
MLA_KV_RANK = 512
ROPE_THETA = 10000.0

NSA_HEADS = 4
NSA_DK = 192
NSA_DV = 128
CMP_LEN = 32
CMP_STRIDE = 16
SLC_LEN = 64
SLC_TOPN = 16
WIN = 512

MEM_HEADS = 4
MEM_DH = 128

MLA_WIDTH = MLA_HEADS * MLA_V
NSA_WIDTH = NSA_HEADS * NSA_DV
MEM_WIDTH = MEM_HEADS * MEM_DH
MIX_WIDTH = MLA_WIDTH + NSA_WIDTH + MEM_WIDTH

IN_SPLITS = (
    MLA_Q_RANK, MLA_KV_RANK, MLA_ROPE, MLA_WIDTH,
    NSA_HEADS * NSA_DK, NSA_DK, NSA_DV, NSA_DK, NSA_DV,
    NSA_DK, NSA_DV, 3 * NSA_HEADS, NSA_WIDTH,
    MEM_WIDTH, MEM_WIDTH,
)
D_IN = sum(IN_SPLITS)

kernel_name = "hybrid_mla_nsa_memory_block"


def rmsnorm(x, g):
    xf = x.astype(jnp.float32)
    y = xf * lax.rsqrt(jnp.mean(xf * xf, axis=-1, keepdims=True) + EPS)
    return (y * g.astype(jnp.float32)).astype(x.dtype)


def masked_softmax(s, mask):
    s = jnp.where(mask, s.astype(jnp.float32), -1e30)
    m = jnp.max(s, axis=-1, keepdims=True)
    p = jnp.exp(s - m) * mask
    return p / (jnp.sum(p, axis=-1, keepdims=True) + 1e-20)


def alibi_slopes(n):
    return 2.0 ** (-8.0 * jnp.arange(1, n + 1, dtype=jnp.float32) / n)


def apply_rope(x, cos, sin):
    x1, x2 = jnp.split(x.astype(jnp.float32), 2, axis=-1)
    return jnp.concatenate([x1 * cos - x2 * sin, x1 * sin + x2 * cos], axis=-1).astype(x.dtype)


def to_blocks(a):
    b, s = a.shape[:2]
    return jnp.moveaxis(a.reshape(b, s // Q_BLOCK, Q_BLOCK, *a.shape[2:]), 1, 0)


def from_blocks(a):
    a = jnp.moveaxis(a, 0, 1)
    return a.reshape(a.shape[0], -1, *a.shape[3:])


def mla_mixer(c_q, c_kv, k_rope, cos, sin, q_norm_g, w_uq, kv_norm_g, w_ukv):
    b, s, _ = c_q.shape
    q = (rmsnorm(c_q, q_norm_g) @ w_uq).reshape(b, s, MLA_HEADS, MLA_NOPE + MLA_ROPE)
    q = jnp.concatenate([q[..., :MLA_NOPE],
                         apply_rope(q[..., MLA_NOPE:], cos[:, None], sin[:, None])], axis=-1)
    kv = (rmsnorm(c_kv, kv_norm_g) @ w_ukv).reshape(b, s, MLA_HEADS, MLA_NOPE + MLA_V)
    k_pe = apply_rope(k_rope, cos, sin)
    k = jnp.concatenate([kv[..., :MLA_NOPE],
                         jnp.broadcast_to(k_pe[:, :, None], (b, s, MLA_HEADS, MLA_ROPE))], axis=-1)
    v = kv[..., MLA_NOPE:]
    scale = (MLA_NOPE + MLA_ROPE) ** -0.5
    kpos = jnp.arange(s)

    def block(args):
        qb, i = args
        qpos = i * Q_BLOCK + jnp.arange(Q_BLOCK)
        sc = jnp.einsum('bqhd,bkhd->bhqk', qb, k, preferred_element_type=jnp.float32) * scale
        p = masked_softmax(sc, kpos[None, :] <= qpos[:, None])
        return jnp.einsum('bhqk,bkhd->bqhd', p.astype(v.dtype), v)

    o = lax.map(block, (to_blocks(q), jnp.arange(s // Q_BLOCK)))
    return from_blocks(o).reshape(b, s, MLA_WIDTH)


def compress(a, pe, w1, w2):
    b, s, d = a.shape
    ch = a.reshape(b, s // CMP_STRIDE, CMP_STRIDE, d)
    blocks = jnp.concatenate([ch[:, :-1], ch[:, 1:]], axis=2) + pe
    return jax.nn.silu(blocks.reshape(b, -1, CMP_LEN * d) @ w1) @ w2


def nsa_mixer(q, k_c, v_c, k_s, v_s, k_w, v_w, gate_logits,
              cmp_pe_k, cmp_pe_v, cmp_w1k, cmp_w2k, cmp_w1v, cmp_w2v):
    b, s, _ = q.shape
    q = q.reshape(b, s, NSA_HEADS, NSA_DK)
    gates = jax.nn.sigmoid(gate_logits.astype(jnp.float32)).reshape(b, s, NSA_HEADS, 3)
    scale = NSA_DK ** -0.5
    slopes = alibi_slopes(NSA_HEADS)[None, :, None, None]

    k_cmp = compress(k_c, cmp_pe_k, cmp_w1k, cmp_w2k)
    v_cmp = compress(v_c, cmp_pe_v, cmp_w1v, cmp_w2v)
    n_c = k_cmp.shape[1]
    c_start = jnp.arange(n_c) * CMP_STRIDE
    cmp_end = c_start + CMP_LEN - 1
    cmp_pos = c_start.astype(jnp.float32) + (CMP_LEN - 1) / 2.0

    n_s = s // SLC_LEN
    top_n = min(SLC_TOPN, n_s)
    k_blk = k_s.reshape(b, n_s, SLC_LEN, NSA_DK)
    v_blk = v_s.reshape(b, n_s, SLC_LEN, NSA_DV)
    s_start = jnp.arange(n_s) * SLC_LEN
    overlap = ((c_start[:, None] < s_start[None, :] + SLC_LEN) &
               (c_start[:, None] + CMP_LEN > s_start[None, :])).astype(jnp.float32)
    j = jnp.arange(n_s)

    k_wp = jnp.pad(k_w, ((0, 0), (WIN, 0), (0, 0)))
    v_wp = jnp.pad(v_w, ((0, 0), (WIN, 0), (0, 0)))

    def block(args):
        qb, gb, i = args
        t = i * Q_BLOCK + jnp.arange(Q_BLOCK)
        tf = t.astype(jnp.float32)

        sc = jnp.einsum('bqhd,bnd->bhqn', qb, k_cmp, preferred_element_type=jnp.float32) * scale
        sc = sc - slopes * (tf[:, None] - cmp_pos[None, :])
        p_cmp = masked_softmax(sc, cmp_end[None, :] <= t[:, None])
        o_cmp = jnp.einsum('bhqn,bnd->bqhd', p_cmp.astype(v_cmp.dtype), v_cmp)

        imp = jnp.einsum('bhqn,nm->bqm', p_cmp, overlap)
        cur = t // SLC_LEN
        forced = (j[None, :] == 0) | (j[None, :] == cur[:, None]) | (j[None, :] == cur[:, None] - 1)
        imp = jnp.where(forced, NEG_BIG, imp)
        imp = jnp.where(j[None, :] > cur[:, None], -NEG_BIG, imp)
        _, idx = lax.top_k(imp, top_n)
        ks = jax.vmap(lambda kb, ix: kb[ix])(k_blk, idx)
        vs = jax.vmap(lambda vb, ix: vb[ix])(v_blk, idx).reshape(b, Q_BLOCK, top_n * SLC_LEN, NSA_DV)
        spos = (idx[..., None] * SLC_LEN + jnp.arange(SLC_LEN)).reshape(b, Q_BLOCK, top_n * SLC_LEN)
        ss = jnp.einsum('bqhd,bqnld->bhqnl', qb, ks, preferred_element_type=jnp.float32)
        ss = ss.reshape(b, NSA_HEADS, Q_BLOCK, top_n * SLC_LEN) * scale
        ss = ss - slopes * (tf[None, None, :, None] - spos[:, None].astype(jnp.float32))
        p_s = masked_softmax(ss, (spos <= t[None, :, None])[:, None])
        o_slc = jnp.einsum('bhqk,bqkd->bqhd', p_s.astype(vs.dtype), vs)

        kw = lax.dynamic_slice_in_dim(k_wp, i * Q_BLOCK, WIN + Q_BLOCK, axis=1)
        vw = lax.dynamic_slice_in_dim(v_wp, i * Q_BLOCK, WIN + Q_BLOCK, axis=1)
        wpos = i * Q_BLOCK - WIN + jnp.arange(WIN + Q_BLOCK)
        rel = t[:, None] - wpos[None, :]
        sw = jnp.einsum('bqhd,bkd->bhqk', qb, kw, preferred_element_type=jnp.float32) * scale
        sw = sw - slopes * rel.astype(jnp.float32)
        p_w = masked_softmax(sw, (rel >= 0) & (rel < WIN) & (wpos[None, :] >= 0))
        o_win = jnp.einsum('bhqk,bkd->bqhd', p_w.astype(vw.dtype), vw)

        o = gb[..., 0:1] * o_cmp + gb[..., 1:2] * o_slc + gb[..., 2:3] * o_win
        return o.astype(qb.dtype)

    o = lax.map(block, (to_blocks(q), to_blocks(gates), jnp.arange(s // Q_BLOCK)))
    return from_blocks(o).reshape(b, s, NSA_WIDTH)


def memory_mixer(q, mem, mem_norm_g, w_mem_kv):
    b, s, _ = q.shape
    q = q.reshape(b, s, MEM_HEADS, MEM_DH)
    kv = (rmsnorm(mem, mem_norm_g) @ w_mem_kv).reshape(b, mem.shape[1], 2, MEM_HEADS, MEM_DH)
    k, v = kv[:, :, 0], kv[:, :, 1]
    sc = jnp.einsum('bshd,bmhd->bhsm', q, k, preferred_element_type=jnp.float32) * MEM_DH ** -0.5
    p = jax.nn.softmax(sc, axis=-1)
    return jnp.einsum('bhsm,bmhd->bshd', p.astype(v.dtype), v).reshape(b, s, MEM_WIDTH)


def hybrid_layer(x, mem, cos, sin, norm_g, w_in, q_norm_g, w_uq, kv_norm_g, w_ukv,
                 cmp_pe_k, cmp_pe_v, cmp_w1k, cmp_w2k, cmp_w1v, cmp_w2v,
                 mem_norm_g, w_mem_kv, w_out):
    h = rmsnorm(x, norm_g) @ w_in
    offsets = np.cumsum(IN_SPLITS)[:-1].tolist()
    (c_q, c_kv, k_rope, z_mla, q_nsa, k_c, v_c, k_s, v_s, k_w, v_w, g_nsa, z_nsa,
     q_mem, z_mem) = jnp.split(h, offsets, axis=-1)
    o_mla = mla_mixer(c_q, c_kv, k_rope, cos, sin, q_norm_g, w_uq, kv_norm_g, w_ukv) * jax.nn.silu(z_mla)
    o_nsa = nsa_mixer(q_nsa, k_c, v_c, k_s, v_s, k_w, v_w, g_nsa,
                      cmp_pe_k, cmp_pe_v, cmp_w1k, cmp_w2k, cmp_w1v, cmp_w2v) * jax.nn.silu(z_nsa)
    o_mem = memory_mixer(q_mem, mem, mem_norm_g, w_mem_kv) * jax.nn.silu(z_mem)
    o = jnp.concatenate([o_mla, o_nsa, o_mem], axis=-1) @ w_out
    return x + o.astype(x.dtype)


def _fwd_setup_inputs(seed: int = 0) -> dict:
    key = jax.random.key(seed)
    ks = jax.random.split(key, 20)

    def nrm(k, shape, scale):
        return jax.random.normal(k, shape, jnp.float32) * scale

    def gain(k, shape):
        return 1.0 + 0.01 * jax.random.normal(k, shape, jnp.float32)

    return {
        "x": nrm(ks[0], (BATCH, SEQ, D_MODEL), 1.0),
        "mem": nrm(ks[1], (BATCH, MEM_LEN, D_MODEL), 1.0),
        "norm_g": gain(ks[2], (DEPTH, D_MODEL)),
        "w_in": nrm(ks[3], (DEPTH, D_MODEL, D_IN), D_MODEL ** -0.5),
        "q_norm_g": gain(ks[4], (DEPTH, MLA_Q_RANK)),
        "w_uq": nrm(ks[5], (DEPTH, MLA_Q_RANK, MLA_HEADS * (MLA_NOPE + MLA_ROPE)), MLA_Q_RANK ** -0.5),
        "kv_norm_g": gain(ks[6], (DEPTH, MLA_KV_RANK)),
        "w_ukv": nrm(ks[7], (DEPTH, MLA_KV_RANK, MLA_HEADS * (MLA_NOPE + MLA_V)), MLA_KV_RANK ** -0.5),
        "cmp_pe_k": nrm(ks[8], (DEPTH, CMP_LEN, NSA_DK), 0.02),
        "cmp_pe_v": nrm(ks[9], (DEPTH, CMP_LEN, NSA_DV), 0.02),
        "cmp_w1k": nrm(ks[10], (DEPTH, CMP_LEN * NSA_DK, NSA_DK), (CMP_LEN * NSA_DK) ** -0.5),
        "cmp_w2k": nrm(ks[11], (DEPTH, NSA_DK, NSA_DK), NSA_DK ** -0.5),
        "cmp_w1v": nrm(ks[12], (DEPTH, CMP_LEN * NSA_DV, NSA_DV), (CMP_LEN * NSA_DV) ** -0.5),
        "cmp_w2v": nrm(ks[13], (DEPTH, NSA_DV, NSA_DV), NSA_DV ** -0.5),
        "mem_norm_g": gain(ks[14], (DEPTH, D_MODEL)),
        "w_mem_kv": nrm(ks[15], (DEPTH, D_MODEL, 2 * MEM_WIDTH), D_MODEL ** -0.5),
        "w_out": nrm(ks[16], (DEPTH, MIX_WIDTH, D_MODEL), MIX_WIDTH ** -0.5),
        "final_norm_g": gain(ks[17], (D_MODEL,)),
    }


def _fwd_reference(x, mem, norm_g, w_in, q_norm_g, w_uq, kv_norm_g, w_ukv,
              cmp_pe_k, cmp_pe_v, cmp_w1k, cmp_w2k, cmp_w1v, cmp_w2v,
              mem_norm_g, w_mem_kv, w_out, final_norm_g):
    s = x.shape[1]
    pos = jnp.arange(s, dtype=jnp.float32)
    inv_freq = ROPE_THETA ** (-jnp.arange(0, MLA_ROPE, 2, dtype=jnp.float32) / MLA_ROPE)
    ang = pos[:, None] * inv_freq[None, :]
    cos, sin = jnp.cos(ang), jnp.sin(ang)
    for l in range(DEPTH):
        x = hybrid_layer(x, mem, cos, sin, norm_g[l], w_in[l], q_norm_g[l], w_uq[l],
                         kv_norm_g[l], w_ukv[l], cmp_pe_k[l], cmp_pe_v[l], cmp_w1k[l],
                         cmp_w2k[l], cmp_w1v[l], cmp_w2v[l], mem_norm_g[l], w_mem_kv[l], w_out[l])
    return rmsnorm(x, final_norm_g)


import jax as _jax
import jax.numpy as _jnp

TWIN_FORMAT = 'train_step'
FWD_PARAMS = ['x', 'mem', 'norm_g', 'w_in', 'q_norm_g', 'w_uq', 'kv_norm_g', 'w_ukv', 'cmp_pe_k', 'cmp_pe_v', 'cmp_w1k', 'cmp_w2k', 'cmp_w1v', 'cmp_w2v', 'mem_norm_g', 'w_mem_kv', 'w_out', 'final_norm_g']
TWIN_WEIGHTS = ['norm_g', 'w_in', 'q_norm_g', 'w_uq', 'kv_norm_g', 'w_ukv', 'cmp_pe_k', 'cmp_pe_v', 'cmp_w1k', 'cmp_w2k', 'cmp_w1v', 'cmp_w2v', 'mem_norm_g', 'w_mem_kv', 'w_out', 'final_norm_g']
TWIN_DIFF_INPUT = 'x'
TWIN_INPUTS = ['x', 'mem', 'norm_g', 'w_in', 'q_norm_g', 'w_uq', 'kv_norm_g', 'w_ukv', 'cmp_pe_k', 'cmp_pe_v', 'cmp_w1k', 'cmp_w2k', 'cmp_w1v', 'cmp_w2v', 'mem_norm_g', 'w_mem_kv', 'w_out', 'final_norm_g', 'loss_target', 'm_norm_g', 'm_w_in', 'm_q_norm_g', 'm_w_uq', 'm_kv_norm_g', 'm_w_ukv', 'm_cmp_pe_k', 'm_cmp_pe_v', 'm_cmp_w1k', 'm_cmp_w2k', 'm_cmp_w1v', 'm_cmp_w2v', 'm_mem_norm_g', 'm_w_mem_kv', 'm_w_out', 'm_final_norm_g', 'v_norm_g', 'v_w_in', 'v_q_norm_g', 'v_w_uq', 'v_kv_norm_g', 'v_w_ukv', 'v_cmp_pe_k', 'v_cmp_pe_v', 'v_cmp_w1k', 'v_cmp_w2k', 'v_cmp_w1v', 'v_cmp_w2v', 'v_mem_norm_g', 'v_w_mem_kv', 'v_w_out', 'v_final_norm_g']
TWIN_OUTPUTS = ['loss', 'grad_x', 'grad_norm_g', 'grad_w_in', 'grad_q_norm_g', 'grad_w_uq', 'grad_kv_norm_g', 'grad_w_ukv', 'grad_cmp_pe_k', 'grad_cmp_pe_v', 'grad_cmp_w1k', 'grad_cmp_w2k', 'grad_cmp_w1v', 'grad_cmp_w2v', 'grad_mem_norm_g', 'grad_w_mem_kv', 'grad_w_out', 'grad_final_norm_g', 'delta_norm_g', 'delta_w_in', 'delta_q_norm_g', 'delta_w_uq', 'delta_kv_norm_g', 'delta_w_ukv', 'delta_cmp_pe_k', 'delta_cmp_pe_v', 'delta_cmp_w1k', 'delta_cmp_w2k', 'delta_cmp_w1v', 'delta_cmp_w2v', 'delta_mem_norm_g', 'delta_w_mem_kv', 'delta_w_out', 'delta_final_norm_g', 'new_m_norm_g', 'new_m_w_in', 'new_m_q_norm_g', 'new_m_w_uq', 'new_m_kv_norm_g', 'new_m_w_ukv', 'new_m_cmp_pe_k', 'new_m_cmp_pe_v', 'new_m_cmp_w1k', 'new_m_cmp_w2k', 'new_m_cmp_w1v', 'new_m_cmp_w2v', 'new_m_mem_norm_g', 'new_m_w_mem_kv', 'new_m_w_out', 'new_m_final_norm_g', 'new_v_norm_g', 'new_v_w_in', 'new_v_q_norm_g', 'new_v_w_uq', 'new_v_kv_norm_g', 'new_v_w_ukv', 'new_v_cmp_pe_k', 'new_v_cmp_pe_v', 'new_v_cmp_w1k', 'new_v_cmp_w2k', 'new_v_cmp_w1v', 'new_v_cmp_w2v', 'new_v_mem_norm_g', 'new_v_w_mem_kv', 'new_v_w_out', 'new_v_final_norm_g']
TWIN_LEAF_KINDS = {'loss': 'loss', 'grad_x': 'grad_x', 'grad_norm_g': 'grad_w', 'grad_w_in': 'grad_w', 'grad_q_norm_g': 'grad_w', 'grad_w_uq': 'grad_w', 'grad_kv_norm_g': 'grad_w', 'grad_w_ukv': 'grad_w', 'grad_cmp_pe_k': 'grad_w', 'grad_cmp_pe_v': 'grad_w', 'grad_cmp_w1k': 'grad_w', 'grad_cmp_w2k': 'grad_w', 'grad_cmp_w1v': 'grad_w', 'grad_cmp_w2v': 'grad_w', 'grad_mem_norm_g': 'grad_w', 'grad_w_mem_kv': 'grad_w', 'grad_w_out': 'grad_w', 'grad_final_norm_g': 'grad_w', 'delta_norm_g': 'delta_w', 'delta_w_in': 'delta_w', 'delta_q_norm_g': 'delta_w', 'delta_w_uq': 'delta_w', 'delta_kv_norm_g': 'delta_w', 'delta_w_ukv': 'delta_w', 'delta_cmp_pe_k': 'delta_w', 'delta_cmp_pe_v': 'delta_w', 'delta_cmp_w1k': 'delta_w', 'delta_cmp_w2k': 'delta_w', 'delta_cmp_w1v': 'delta_w', 'delta_cmp_w2v': 'delta_w', 'delta_mem_norm_g': 'delta_w', 'delta_w_mem_kv': 'delta_w', 'delta_w_out': 'delta_w', 'delta_final_norm_g': 'delta_w', 'new_m_norm_g': 'new_m', 'new_m_w_in': 'new_m', 'new_m_q_norm_g': 'new_m', 'new_m_w_uq': 'new_m', 'new_m_kv_norm_g': 'new_m', 'new_m_w_ukv': 'new_m', 'new_m_cmp_pe_k': 'new_m', 'new_m_cmp_pe_v': 'new_m', 'new_m_cmp_w1k': 'new_m', 'new_m_cmp_w2k': 'new_m', 'new_m_cmp_w1v': 'new_m', 'new_m_cmp_w2v': 'new_m', 'new_m_mem_norm_g': 'new_m', 'new_m_w_mem_kv': 'new_m', 'new_m_w_out': 'new_m', 'new_m_final_norm_g': 'new_m', 'new_v_norm_g': 'new_v', 'new_v_w_in': 'new_v', 'new_v_q_norm_g': 'new_v', 'new_v_w_uq': 'new_v', 'new_v_kv_norm_g': 'new_v', 'new_v_w_ukv': 'new_v', 'new_v_cmp_pe_k': 'new_v', 'new_v_cmp_pe_v': 'new_v', 'new_v_cmp_w1k': 'new_v', 'new_v_cmp_w2k': 'new_v', 'new_v_cmp_w1v': 'new_v', 'new_v_cmp_w2v': 'new_v', 'new_v_mem_norm_g': 'new_v', 'new_v_w_mem_kv': 'new_v', 'new_v_w_out': 'new_v', 'new_v_final_norm_g': 'new_v'}


def _forward(args):
    return _fwd_reference(*[args[k] for k in FWD_PARAMS])


def _output_shape():
    def fwd():
        inp = _fwd_setup_inputs(0)
        return _fwd_reference(*[inp[k] for k in FWD_PARAMS])
    out = _jax.eval_shape(fwd)
    return out.shape, out.dtype

N_MICROBATCH = 1
ADAM_LR = 0.001
ADAM_B1 = 0.9
ADAM_B2 = 0.999
ADAM_EPS = 1e-08
ADAM_WD = 0.01
ADAM_STEP = 10
PER_EXAMPLE_BATCH_AXIS = {'x': 0, 'mem': 0, 'loss_target': 0}
SHARED_INPUTS = []
_WEIGHT_DTYPES = {'norm_g': _jnp.float32, 'w_in': _jnp.float32, 'q_norm_g': _jnp.float32, 'w_uq': _jnp.float32, 'kv_norm_g': _jnp.float32, 'w_ukv': _jnp.float32, 'cmp_pe_k': _jnp.float32, 'cmp_pe_v': _jnp.float32, 'cmp_w1k': _jnp.float32, 'cmp_w2k': _jnp.float32, 'cmp_w1v': _jnp.float32, 'cmp_w2v': _jnp.float32, 'mem_norm_g': _jnp.float32, 'w_mem_kv': _jnp.float32, 'w_out': _jnp.float32, 'final_norm_g': _jnp.float32}
MOMENT_SCALE = {'norm_g': 2.148046e-02, 'w_in': 1.314912e-02, 'q_norm_g': 1.313871e-02, 'w_uq': 7.611111e-03, 'kv_norm_g': 1.917006e-02, 'w_ukv': 9.592997e-03, 'cmp_pe_k': 5.244050e-04, 'cmp_pe_v': 6.206915e-03, 'cmp_w1k': 4.942237e-03, 'cmp_w2k': 4.462542e-03, 'cmp_w1v': 2.420942e-02, 'cmp_w2v': 2.443545e-02, 'mem_norm_g': 3.988942e-03, 'w_mem_kv': 5.442826e-03, 'w_out': 1.218171e-02, 'final_norm_g': 1.598762e+01}


def _to_microbatches(a, axis):
    t = _jnp.moveaxis(a, axis, 0)
    t = t.reshape((N_MICROBATCH, t.shape[0] // N_MICROBATCH) + t.shape[1:])
    return _jnp.moveaxis(t, 1, axis + 1)


def setup_inputs(seed: int = 0) -> dict:
    inp = _fwd_setup_inputs(seed)
    key = _jax.random.fold_in(_jax.random.key(seed), 7919)
    shape, _ = _output_shape()
    out = dict(inp)
    out["loss_target"] = _jax.random.normal(_jax.random.fold_in(key, 0), shape, _jnp.float32)
    for i, name in enumerate(TWIN_WEIGHTS):
        w = inp[name].astype(_jnp.float32)
        if MOMENT_SCALE is None:
            s = _jnp.sqrt(_jnp.mean(_jnp.square(w)) + 1e-30)
        else:
            s = MOMENT_SCALE[name]
        km, kv = _jax.random.split(_jax.random.fold_in(key, i + 1))
        out[name] = w
        out["m_" + name] = s * _jax.random.normal(km, w.shape, _jnp.float32)
        out["v_" + name] = (s * s) * _jax.random.uniform(kv, w.shape, _jnp.float32, 0.5, 1.5)
    if N_MICROBATCH > 1:
        for name, axis in PER_EXAMPLE_BATCH_AXIS.items():
            out[name] = _to_microbatches(out[name], axis)
    return {'x': out['x'], 'mem': out['mem'], 'norm_g': out['norm_g'], 'w_in': out['w_in'], 'q_norm_g': out['q_norm_g'], 'w_uq': out['w_uq'], 'kv_norm_g': out['kv_norm_g'], 'w_ukv': out['w_ukv'], 'cmp_pe_k': out['cmp_pe_k'], 'cmp_pe_v': out['cmp_pe_v'], 'cmp_w1k': out['cmp_w1k'], 'cmp_w2k': out['cmp_w2k'], 'cmp_w1v': out['cmp_w1v'], 'cmp_w2v': out['cmp_w2v'], 'mem_norm_g': out['mem_norm_g'], 'w_mem_kv': out['w_mem_kv'], 'w_out': out['w_out'], 'final_norm_g': out['final_norm_g'], 'loss_target': out['loss_target'], 'm_norm_g': out['m_norm_g'], 'm_w_in': out['m_w_in'], 'm_q_norm_g': out['m_q_norm_g'], 'm_w_uq': out['m_w_uq'], 'm_kv_norm_g': out['m_kv_norm_g'], 'm_w_ukv': out['m_w_ukv'], 'm_cmp_pe_k': out['m_cmp_pe_k'], 'm_cmp_pe_v': out['m_cmp_pe_v'], 'm_cmp_w1k': out['m_cmp_w1k'], 'm_cmp_w2k': out['m_cmp_w2k'], 'm_cmp_w1v': out['m_cmp_w1v'], 'm_cmp_w2v': out['m_cmp_w2v'], 'm_mem_norm_g': out['m_mem_norm_g'], 'm_w_mem_kv': out['m_w_mem_kv'], 'm_w_out': out['m_w_out'], 'm_final_norm_g': out['m_final_norm_g'], 'v_norm_g': out['v_norm_g'], 'v_w_in': out['v_w_in'], 'v_q_norm_g': out['v_q_norm_g'], 'v_w_uq': out['v_w_uq'], 'v_kv_norm_g': out['v_kv_norm_g'], 'v_w_ukv': out['v_w_ukv'], 'v_cmp_pe_k': out['v_cmp_pe_k'], 'v_cmp_pe_v': out['v_cmp_pe_v'], 'v_cmp_w1k': out['v_cmp_w1k'], 'v_cmp_w2k': out['v_cmp_w2k'], 'v_cmp_w1v': out['v_cmp_w1v'], 'v_cmp_w2v': out['v_cmp_w2v'], 'v_mem_norm_g': out['v_mem_norm_g'], 'v_w_mem_kv': out['v_w_mem_kv'], 'v_w_out': out['v_w_out'], 'v_final_norm_g': out['v_final_norm_g']}


def _loss(weights, diff, rest, loss_target):
    with _jax.named_scope("forward"):
        args = {**rest, TWIN_DIFF_INPUT: diff, **{k: w.astype(_WEIGHT_DTYPES[k]) for k, w in weights.items()}}
        y = _forward(args)
    with _jax.named_scope("loss_head"):
        err = _jnp.square(y.astype(_jnp.float32) - loss_target)
        return 0.5 * _jnp.sum(_jnp.mean(err, axis=-1)) if err.ndim else 0.5 * err


def _adamw(w, g, m, v):
    m = ADAM_B1 * m + (1.0 - ADAM_B1) * g
    v = ADAM_B2 * v + (1.0 - ADAM_B2) * _jnp.square(g)
    m_hat = m / (1.0 - ADAM_B1 ** ADAM_STEP)
    v_hat = v / (1.0 - ADAM_B2 ** ADAM_STEP)
    delta = -ADAM_LR * (m_hat / (_jnp.sqrt(v_hat) + ADAM_EPS) + ADAM_WD * w)
    return delta, m, v


def reference(x, mem, norm_g, w_in, q_norm_g, w_uq, kv_norm_g, w_ukv, cmp_pe_k, cmp_pe_v, cmp_w1k, cmp_w2k, cmp_w1v, cmp_w2v, mem_norm_g, w_mem_kv, w_out, final_norm_g, loss_target, m_norm_g, m_w_in, m_q_norm_g, m_w_uq, m_kv_norm_g, m_w_ukv, m_cmp_pe_k, m_cmp_pe_v, m_cmp_w1k, m_cmp_w2k, m_cmp_w1v, m_cmp_w2v, m_mem_norm_g, m_w_mem_kv, m_w_out, m_final_norm_g, v_norm_g, v_w_in, v_q_norm_g, v_w_uq, v_kv_norm_g, v_w_ukv, v_cmp_pe_k, v_cmp_pe_v, v_cmp_w1k, v_cmp_w2k, v_cmp_w1v, v_cmp_w2v, v_mem_norm_g, v_w_mem_kv, v_w_out, v_final_norm_g):
    given = dict(x=x, mem=mem, norm_g=norm_g, w_in=w_in, q_norm_g=q_norm_g, w_uq=w_uq, kv_norm_g=kv_norm_g, w_ukv=w_ukv, cmp_pe_k=cmp_pe_k, cmp_pe_v=cmp_pe_v, cmp_w1k=cmp_w1k, cmp_w2k=cmp_w2k, cmp_w1v=cmp_w1v, cmp_w2v=cmp_w2v, mem_norm_g=mem_norm_g, w_mem_kv=w_mem_kv, w_out=w_out, final_norm_g=final_norm_g, loss_target=loss_target, m_norm_g=m_norm_g, m_w_in=m_w_in, m_q_norm_g=m_q_norm_g, m_w_uq=m_w_uq, m_kv_norm_g=m_kv_norm_g, m_w_ukv=m_w_ukv, m_cmp_pe_k=m_cmp_pe_k, m_cmp_pe_v=m_cmp_pe_v, m_cmp_w1k=m_cmp_w1k, m_cmp_w2k=m_cmp_w2k, m_cmp_w1v=m_cmp_w1v, m_cmp_w2v=m_cmp_w2v, m_mem_norm_g=m_mem_norm_g, m_w_mem_kv=m_w_mem_kv, m_w_out=m_w_out, m_final_norm_g=m_final_norm_g, v_norm_g=v_norm_g, v_w_in=v_w_in, v_q_norm_g=v_q_norm_g, v_w_uq=v_w_uq, v_kv_norm_g=v_kv_norm_g, v_w_ukv=v_w_ukv, v_cmp_pe_k=v_cmp_pe_k, v_cmp_pe_v=v_cmp_pe_v, v_cmp_w1k=v_cmp_w1k, v_cmp_w2k=v_cmp_w2k, v_cmp_w1v=v_cmp_w1v, v_cmp_w2v=v_cmp_w2v, v_mem_norm_g=v_mem_norm_g, v_w_mem_kv=v_w_mem_kv, v_w_out=v_w_out, v_final_norm_g=v_final_norm_g)
    weights = {n: given[n] for n in TWIN_WEIGHTS}
    shared = {n: given[n] for n in SHARED_INPUTS}
    per_example = {n: given[n] for n in ['x', 'mem']}
    grad_fn = _jax.value_and_grad(_loss, argnums=(0, 1))

    def one_microbatch(ex, loss_target):
        ex = dict(ex)
        diff = ex.pop(TWIN_DIFF_INPUT)
        return grad_fn(weights, diff, {**shared, **ex}, loss_target)

    if N_MICROBATCH == 1:
        loss, (grad_w, grad_x) = one_microbatch(per_example, given["loss_target"])
    else:
        def body(carry, xs):
            loss_sum, grad_sum = carry
            l_k, (gw_k, gx_k) = one_microbatch(xs[0], xs[1])
            with _jax.named_scope("update"):
                return (loss_sum + l_k, _jax.tree.map(_jnp.add, grad_sum, gw_k)), gx_k

        init = (_jnp.zeros((), _jnp.float32), _jax.tree.map(_jnp.zeros_like, weights))
        (loss, grad_w), grad_x = _jax.lax.scan(body, init, (per_example, given["loss_target"]))
    with _jax.named_scope("update"):
        delta_w, new_m, new_v = {}, {}, {}
        for n in TWIN_WEIGHTS:
            delta_w[n], new_m[n], new_v[n] = _adamw(weights[n], grad_w[n], given["m_" + n], given["v_" + n])
    return (loss, grad_x, *[grad_w[n] for n in TWIN_WEIGHTS], *[delta_w[n] for n in TWIN_WEIGHTS],
            *[new_m[n] for n in TWIN_WEIGHTS], *[new_v[n] for n in TWIN_WEIGHTS])
```

```python
import functools

import numpy as np
import jax
import jax.numpy as jnp
from jax import lax
from jax.experimental import pallas as pl
from jax.experimental.pallas import tpu as pltpu

F32 = jnp.float32
BF16 = jnp.bfloat16
MESH = pl.DeviceIdType.MESH

D_MODEL = 2048
EPS = 1e-6
LANE = 128
HEAD_V = 128
MLA_HEADS = 8
NSA_HEADS = 4
MEM_HEADS = 4
NSA_DK = 192
CMP_STRIDE = 16
CMP_LEN = 32
SLC_LEN = 64
SLC_TOPN = 16
WIN = 512
NEG = -1e30
ROPE_THETA = 10000.0

ORIG = dict(c_q=(0, 512), c_kv=(512, 512), k_rope=(1024, 64), z_mla=(1088, 1024),
            q_nsa=(2112, 768), k_c=(2880, 192), v_c=(3072, 128), k_s=(3200, 192),
            v_s=(3392, 128), k_w=(3520, 192), v_w=(3712, 128), g_nsa=(3840, 12),
            z_nsa=(3852, 512), q_mem=(4364, 512), z_mem=(4876, 512))
D_IN = 5388
PAD = dict(c_q=0, c_kv=512, k_rope=1024, q_nsa=1152, k_c=2176, v_c=2432, k_s=2560,
           v_s=2816, k_w=2944, v_w=3200, g_nsa=3328, q_mem=3456, z=4096)
D_PAD = 6144

ADAM_LR, ADAM_B1, ADAM_B2, ADAM_EPS, ADAM_WD, ADAM_STEP = 0.001, 0.9, 0.999, 1e-08, 0.01, 10

PACK_COLS = 1024
SHARDED = ("w_in", "w_uq", "w_ukv", "cmp_w1k", "cmp_w1v", "w_mem_kv", "w_out")
SHARD_AXIS = dict(w_in=1, w_uq=1, w_ukv=1, cmp_w1k=0, cmp_w1v=0, w_mem_kv=0, w_out=0)
REPLICATED = ("norm_g", "q_norm_g", "kv_norm_g", "cmp_pe_k", "cmp_pe_v", "cmp_w2k", "cmp_w2v",
              "mem_norm_g", "final_norm_g")
WEIGHTS = ("norm_g", "w_in", "q_norm_g", "w_uq", "kv_norm_g", "w_ukv", "cmp_pe_k", "cmp_pe_v",
           "cmp_w1k", "cmp_w2k", "cmp_w1v", "cmp_w2v", "mem_norm_g", "w_mem_kv", "w_out",
           "final_norm_g")


def _pcall(kernel, **kw):
    return pl.pallas_call(kernel, **kw)


def _tile(n, pref):
    if n <= pref:
        return n
    for t in range(pref, LANE - 1, -LANE):
        if n % t == 0:
            return t
    raise ValueError((n, pref))


def _nt(a, b):
    return lax.dot_general(a, b, (((1,), (1,)), ((), ())), preferred_element_type=F32)


def _nn(a, b):
    return jnp.dot(a, b, preferred_element_type=F32)


def _sigmoid(x):
    return 1.0 / (1.0 + jnp.exp(-x))


def _mm(a, b, name, out_dtype=F32):
    m, k = a.shape
    k2, n = b.shape
    assert k == k2
    tm, tn, tk = _tile(m, 1024), _tile(n, 1024), _tile(k, 2048)
    nk = k // tk

    def kern(a_ref, b_ref, o_ref):
        r = _nn(a_ref[...].astype(BF16), b_ref[...].astype(BF16))
        if nk == 1:
            o_ref[...] = r.astype(out_dtype)
        else:
            kk = pl.program_id(2)

            @pl.when(kk == 0)
            def _():
                o_ref[...] = r

            @pl.when(kk > 0)
            def _():
                o_ref[...] += r

    assert nk == 1 or out_dtype == F32
    return _pcall(
        kern, name=name, grid=(m // tm, n // tn, nk),
        in_specs=[pl.BlockSpec((tm, tk), lambda i, j, kk: (i, kk)),
                  pl.BlockSpec((tk, tn), lambda i, j, kk: (kk, j))],
        out_specs=pl.BlockSpec((tm, tn), lambda i, j, kk: (i, j)),
        out_shape=jax.ShapeDtypeStruct((m, n), out_dtype),
        compiler_params=pltpu.CompilerParams(
            dimension_semantics=("parallel", "parallel", "arbitrary")),
    )(a, b)


def _rms_fwd(x, g, name):
    r, d = x.shape
    tr = _tile(r, 512)

    def kern(x_ref, g_ref, y_ref, r_ref):
        xv = x_ref[...]
        rstd = lax.rsqrt(jnp.mean(xv * xv, axis=-1, keepdims=True) + EPS)
        y_ref[...] = (xv * rstd * g_ref[...]).astype(BF16)
        r_ref[...] = rstd

    return _pcall(
        kern, name=name, grid=(r // tr,),
        in_specs=[pl.BlockSpec((tr, d), lambda i: (i, 0)), pl.BlockSpec((1, d), lambda i: (0, 0))],
        out_specs=[pl.BlockSpec((tr, d), lambda i: (i, 0)), pl.BlockSpec((tr, 1), lambda i: (i, 0))],
        out_shape=[jax.ShapeDtypeStruct((r, d), BF16), jax.ShapeDtypeStruct((r, 1), F32)],
    )(x, g)


def _rms_bwd(x, g, rstd, dy, add, name):
    r, d = x.shape
    tr = _tile(r, 256)
    has_add = add is not None

    def kern(*refs):
        if has_add:
            x_ref, g_ref, r_ref, dy_ref, add_ref, dx_ref, dg_ref = refs
        else:
            x_ref, g_ref, r_ref, dy_ref, dx_ref, dg_ref = refs
        rs = r_ref[...]
        xhat = x_ref[...] * rs
        dyv = dy_ref[...]
        dyg = dyv * g_ref[...]
        c = jnp.mean(dyg * xhat, axis=-1, keepdims=True)
        dx = rs * (dyg - xhat * c)
        if has_add:
            dx = dx + add_ref[...]
        dx_ref[...] = dx
        part = jnp.sum(dyv * xhat, axis=0, keepdims=True)

        @pl.when(pl.program_id(0) == 0)
        def _():
            dg_ref[...] = part

        @pl.when(pl.program_id(0) > 0)
        def _():
            dg_ref[...] += part

    row = pl.BlockSpec((tr, d), lambda i: (i, 0))
    vec = pl.BlockSpec((1, d), lambda i: (0, 0))
    ins = [row, vec, pl.BlockSpec((tr, 1), lambda i: (i, 0)), row] + ([row] if has_add else [])
    args = (x, g, rstd, dy) + ((add,) if has_add else ())
    return _pcall(
        kern, name=name, grid=(r // tr,), in_specs=ins, out_specs=[row, vec],
        out_shape=[jax.ShapeDtypeStruct((r, d), F32), jax.ShapeDtypeStruct((1, d), F32)],
        compiler_params=pltpu.CompilerParams(dimension_semantics=("arbitrary",)),
    )(*args)


def _final_loss(x, proj, g, target):
    r, d = x.shape
    tr = _tile(r, 256)

    def kern(x_ref, p_ref, g_ref, t_ref, dy_ref, dg_ref, loss_ref):
        y = x_ref[...] + p_ref[...]
        rs = lax.rsqrt(jnp.mean(y * y, axis=-1, keepdims=True) + EPS)
        yhat = y * rs
        gv = g_ref[...]
        e = yhat * gv - t_ref[...]
        lpart = 0.5 * jnp.sum(jnp.mean(e * e, axis=-1, keepdims=True), axis=0, keepdims=True)
        dout = e * (1.0 / d)
        dyg = dout * gv
        c = jnp.mean(dyg * yhat, axis=-1, keepdims=True)
        dy_ref[...] = rs * (dyg - yhat * c)
        gpart = jnp.sum(dout * yhat, axis=0, keepdims=True)
        lrow = jnp.broadcast_to(lpart, (1, LANE))

        @pl.when(pl.program_id(0) == 0)
        def _():
            dg_ref[...] = gpart
            loss_ref[...] = lrow

        @pl.when(pl.program_id(0) > 0)
        def _():
            dg_ref[...] += gpart
            loss_ref[...] += lrow

    row = pl.BlockSpec((tr, d), lambda i: (i, 0))
    vec = pl.BlockSpec((1, d), lambda i: (0, 0))
    return _pcall(
        kern, name="final_loss", grid=(r // tr,), in_specs=[row, row, vec, row],
        out_specs=[row, vec, pl.BlockSpec((1, LANE), lambda i: (0, 0))],
        out_shape=[jax.ShapeDtypeStruct((r, d), F32), jax.ShapeDtypeStruct((1, d), F32),
                   jax.ShapeDtypeStruct((1, LANE), F32)],
        compiler_params=pltpu.CompilerParams(dimension_semantics=("arbitrary",)),
    )(x, proj, g, target)


def _rope_fwd(x, cs, sn, nh, width, off, name):
    s = x.shape[0]
    tr = _tile(s, 512)

    def kern(x_ref, c_ref, s_ref, o_ref):
        cv, sv = c_ref[...], s_ref[...]
        for h in range(nh):
            b = h * width
            if off:
                o_ref[:, b:b + off] = x_ref[:, b:b + off].astype(BF16)
            xr = x_ref[:, b + off:b + off + LANE]
            o_ref[:, b + off:b + off + LANE] = (xr * cv + pltpu.roll(xr, 32, 1) * sv).astype(BF16)

    return _pcall(
        kern, name=name, grid=(s // tr,),
        in_specs=[pl.BlockSpec((tr, nh * width), lambda i: (i, 0)),
                  pl.BlockSpec((tr, LANE), lambda i: (i, 0)), pl.BlockSpec((tr, LANE), lambda i: (i, 0))],
        out_specs=pl.BlockSpec((tr, nh * width), lambda i: (i, 0)),
        out_shape=jax.ShapeDtypeStruct((s, nh * width), BF16),
    )(x, cs, sn)


def _rope_grad(d, cv, sv):
    g2 = d * sv
    g2 = g2 + pltpu.roll(g2, 64, 1)
    lane = lax.broadcasted_iota(jnp.int32, d.shape, 1)
    return jnp.where(lane < 64, d * cv + pltpu.roll(g2, 32, 1), 0.0)


def _rope_bwd_q(dq, cs, sn):
    s, w = dq.shape
    tr = _tile(s, 512)
    nh = w // 256

    def kern(d_ref, c_ref, s_ref, o_ref):
        cv, sv = c_ref[...], s_ref[...]
        for h in range(nh):
            b = h * 256
            o_ref[:, b:b + LANE] = d_ref[:, b:b + LANE]
            o_ref[:, b + LANE:b + 256] = _rope_grad(d_ref[:, b + LANE:b + 256], cv, sv)

    row = pl.BlockSpec((tr, w), lambda i: (i, 0))
    tab = pl.BlockSpec((tr, LANE), lambda i: (i, 0))
    return _pcall(kern, name="rope_bwd_q", grid=(s // tr,), in_specs=[row, tab, tab], out_specs=row,
                  out_shape=jax.ShapeDtypeStruct((s, w), F32))(dq, cs, sn)


def _rope_bwd_k(dk, dv, cs, sn):
    s = dk.shape[0]
    tr = _tile(s, 512)
    nh = MLA_HEADS

    def kern(dk_ref, dv_ref, c_ref, s_ref, okv_ref, okr_ref):
        acc = dk_ref[:, LANE:256]
        for h in range(nh):
            okv_ref[:, h * LANE:(h + 1) * LANE] = dk_ref[:, h * 256:h * 256 + LANE]
            if h:
                acc = acc + dk_ref[:, h * 256 + LANE:(h + 1) * 256]
        okv_ref[:, nh * LANE:] = dv_ref[...]
        okr_ref[...] = _rope_grad(acc, c_ref[...], s_ref[...])

    tab = pl.BlockSpec((tr, LANE), lambda i: (i, 0))
    return _pcall(
        kern, name="rope_bwd_k", grid=(s // tr,),
        in_specs=[pl.BlockSpec((tr, nh * 256), lambda i: (i, 0)), pl.BlockSpec((tr, nh * LANE), lambda i: (i, 0)),
                  tab, tab],
        out_specs=[pl.BlockSpec((tr, 2 * nh * LANE), lambda i: (i, 0)), tab],
        out_shape=[jax.ShapeDtypeStruct((s, 2 * nh * LANE), F32), jax.ShapeDtypeStruct((s, LANE), F32)],
    )(dk, dv, cs, sn)


class _Attn:
    def __init__(self, mode, s, sk, heads, dk, per_head_kv):
        self.mode, self.s, self.sk, self.h, self.dk, self.per_head = mode, s, sk, heads, dk, per_head_kv
        self.scale = {"mla": 192 ** -0.5, "mem": 128 ** -0.5}.get(mode, NSA_DK ** -0.5)
        self.tq = min(256, s)
        self.nq = s // self.tq
        if mode in ("mla", "slc"):
            self.tk = min(512, s)
        elif mode == "win":
            self.tk = WIN + self.tq
        else:
            self.tk = sk
        self.tkb = min(512, sk)
        self.ncmp = s // CMP_STRIDE - 1

    def chunks(self, i):
        if self.mode in ("mla", "slc"):
            return (i * self.tq + self.tq + self.tk - 1) // self.tk
        return 1

    def chunk_start(self, i, c):
        if self.mode in ("mla", "slc"):
            return pl.multiple_of(c * self.tk, self.tk)
        if self.mode == "win":
            return pl.multiple_of(jnp.maximum(i * self.tq - WIN, 0), self.tq)
        return 0

    def q_range(self, j):
        k0 = j * self.tkb
        if self.mode in ("mla", "slc"):
            return k0 // self.tq, self.nq
        if self.mode == "win":
            return k0 // self.tq, jnp.minimum((k0 + self.tkb + WIN - 2) // self.tq + 1, self.nq)
        return 0, self.nq

    def mask_bias(self, t, n, h, selx):
        m = self.mode
        if m == "mla":
            return n <= t, None
        if m == "mem":
            return None, None
        slope = jnp.where(h == 0, 0.25, jnp.where(h == 1, 0.0625, jnp.where(h == 2, 0.015625, 0.00390625)))
        slope = slope.astype(F32)
        if m == "cmp":
            mask = (n * CMP_STRIDE + (CMP_LEN - 1) <= t) & (n < self.ncmp)
            pos = n.astype(F32) * float(CMP_STRIDE) + (CMP_LEN - 1) / 2.0
            return mask, -slope * (t.astype(F32) - pos)
        rel = t - n
        bias = -slope * rel.astype(F32)
        if m == "slc":
            return (selx > 0.5) & (rel >= 0), bias
        return (rel >= 0) & (rel < WIN), bias


def _scores(cfg, s_raw, t, n, h, selx, lse=None):
    s = s_raw * cfg.scale
    mask, bias = cfg.mask_bias(t, n, h, selx)
    if bias is not None:
        s = s + bias
    if lse is None:
        if mask is not None:
            s = jnp.where(mask, s, NEG)
        return s, mask
    p = jnp.exp(jnp.minimum(s - lse, 0.0))
    if mask is not None:
        p = jnp.where(mask, p, 0.0)
    return p, mask


def _expand_rows(sel, k0, tk):
    j = lax.broadcasted_iota(jnp.int32, (LANE, tk), 0)
    n = lax.broadcasted_iota(jnp.int32, (LANE, tk), 1) + k0
    e = jnp.where((n >> 6) == j, 1.0, 0.0).astype(BF16)
    return _nn(sel.astype(BF16), e)


def _expand_cols(selt, k0, tk):
    n = lax.broadcasted_iota(jnp.int32, (tk, LANE), 0) + k0
    j = lax.broadcasted_iota(jnp.int32, (tk, LANE), 1)
    e = jnp.where((n >> 6) == j, 1.0, 0.0).astype(BF16)
    return _nn(e, selt.astype(BF16))


def _kv_col(cfg):
    return (lambda h: h) if cfg.per_head else (lambda h: 0)


def _attn_fwd(cfg, q, k, v, sel, name):
    s, tq, tk, dk = cfg.s, cfg.tq, cfg.tk, cfg.dk
    has_sel = sel is not None
    col = _kv_col(cfg)

    def kern(*refs):
        if has_sel:
            q_ref, k_ref, v_ref, sel_ref, o_ref, l_ref = refs
        else:
            q_ref, k_ref, v_ref, o_ref, l_ref = refs
        h, i = pl.program_id(0), pl.program_id(1)
        qv = q_ref[...]
        t = i * tq + lax.broadcasted_iota(jnp.int32, (tq, 1), 0)

        def chunk(c, carry):
            m, l, acc = carry
            k0 = cfg.chunk_start(i, c)
            kk = k_ref[pl.ds(k0, tk), :]
            vv = v_ref[pl.ds(k0, tk), :]
            n = k0 + lax.broadcasted_iota(jnp.int32, (1, tk), 1)
            selx = _expand_rows(sel_ref[...], k0, tk) if has_sel else None
            sc, mask = _scores(cfg, _nt(qv, kk), t, n, h, selx)
            m_new = jnp.maximum(m, jnp.max(sc, axis=1, keepdims=True))
            alpha = jnp.exp(m - m_new)
            p = jnp.exp(sc - m_new)
            if mask is not None:
                p = jnp.where(mask, p, 0.0)
            l = alpha * l + jnp.sum(p, axis=1, keepdims=True)
            acc = alpha * acc + _nn(p.astype(BF16), vv)
            return m_new, l, acc

        init = (jnp.full((tq, 1), NEG, F32), jnp.zeros((tq, 1), F32), jnp.zeros((tq, HEAD_V), F32))
        if cfg.mode in ("mla", "slc"):
            m, l, acc = lax.fori_loop(0, cfg.chunks(i), chunk, init)
        else:
            m, l, acc = chunk(0, init)
        o_ref[...] = acc / (l + 1e-20)
        l_ref[0] = m + jnp.log(l + 1e-20)

    ins = [pl.BlockSpec((tq, dk), lambda h, i: (i, h)),
           pl.BlockSpec((cfg.sk, dk), lambda h, i: (0, col(h))),
           pl.BlockSpec((cfg.sk, HEAD_V), lambda h, i: (0, col(h)))]
    args = [q, k, v]
    if has_sel:
        ins.append(pl.BlockSpec((tq, LANE), lambda h, i: (i, 0)))
        args.append(sel)
    return _pcall(
        kern, name=name, grid=(cfg.h, cfg.nq), in_specs=ins,
        out_specs=[pl.BlockSpec((tq, HEAD_V), lambda h, i: (i, h)),
                   pl.BlockSpec((1, tq, 1), lambda h, i: (h, i, 0))],
        out_shape=[jax.ShapeDtypeStruct((s, cfg.h * HEAD_V), F32),
                   jax.ShapeDtypeStruct((cfg.h, s, 1), F32)],
        compiler_params=pltpu.CompilerParams(dimension_semantics=("parallel", "parallel")),
    )(*args)


def _attn_dq(cfg, q, k, v, sel, o, lse, do, dq_in, name):
    s, tq, tk, dk = cfg.s, cfg.tq, cfg.tk, cfg.dk
    has_sel = sel is not None
    has_in = dq_in is not None
    col = _kv_col(cfg)

    def kern(*refs):
        refs = list(refs)
        q_ref, k_ref, v_ref = refs[:3]
        p0 = 3
        sel_ref = refs[p0] if has_sel else None
        p0 += has_sel
        o_ref, l_ref, do_ref = refs[p0:p0 + 3]
        p0 += 3
        in_ref = refs[p0] if has_in else None
        p0 += has_in
        dq_ref, d_ref = refs[p0:p0 + 2]
        h, i = pl.program_id(0), pl.program_id(1)
        qv = q_ref[...]
        dov = do_ref[...]
        dvec = jnp.sum(dov * o_ref[...], axis=1, keepdims=True)
        d_ref[0] = dvec
        dob = dov.astype(BF16)
        lse_v = l_ref[0]
        t = i * tq + lax.broadcasted_iota(jnp.int32, (tq, 1), 0)

        def chunk(c, acc):
            k0 = cfg.chunk_start(i, c)
            kk = k_ref[pl.ds(k0, tk), :]
            vv = v_ref[pl.ds(k0, tk), :]
            n = k0 + lax.broadcasted_iota(jnp.int32, (1, tk), 1)
            selx = _expand_rows(sel_ref[...], k0, tk) if has_sel else None
            p, _ = _scores(cfg, _nt(qv, kk), t, n, h, selx, lse_v)
            ds = p * (_nt(dob, vv) - dvec) * cfg.scale
            return acc + _nn(ds.astype(BF16), kk)

        init = in_ref[...] if has_in else jnp.zeros((tq, dk), F32)
        if cfg.mode in ("mla", "slc"):
            dq_ref[...] = lax.fori_loop(0, cfg.chunks(i), chunk, init)
        else:
            dq_ref[...] = chunk(0, init)

    qs = pl.BlockSpec((tq, dk), lambda h, i: (i, h))
    os_ = pl.BlockSpec((tq, HEAD_V), lambda h, i: (i, h))
    ls = pl.BlockSpec((1, tq, 1), lambda h, i: (h, i, 0))
    ins = [qs, pl.BlockSpec((cfg.sk, dk), lambda h, i: (0, col(h))),
           pl.BlockSpec((cfg.sk, HEAD_V), lambda h, i: (0, col(h)))]
    args = [q, k, v]
    if has_sel:
        ins.append(pl.BlockSpec((tq, LANE), lambda h, i: (i, 0)))
        args.append(sel)
    ins += [os_, ls, os_]
    args += [o, lse, do]
    if has_in:
        ins.append(qs)
        args.append(dq_in)
    return _pcall(
        kern, name=name, grid=(cfg.h, cfg.nq), in_specs=ins, out_specs=[qs, ls],
        out_shape=[jax.ShapeDtypeStruct((s, cfg.h * dk), F32), jax.ShapeDtypeStruct((cfg.h, s, 1), F32)],
        compiler_params=pltpu.CompilerParams(dimension_semantics=("parallel", "parallel")),
    )(*args)


def _attn_dkv(cfg, q, k, v, selt, lse_t, d_t, do, name):
    s, tq, tkb, dk = cfg.s, cfg.tq, cfg.tkb, cfg.dk
    nq = cfg.nq
    has_sel = selt is not None
    col = _kv_col(cfg)
    hk = cfg.h if cfg.per_head else 1

    def kern(*refs):
        if has_sel:
            k_ref, v_ref, q_ref, do_ref, lt_ref, dt_ref, st_ref, dk_ref, dv_ref = refs
        else:
            k_ref, v_ref, q_ref, do_ref, lt_ref, dt_ref, dk_ref, dv_ref = refs
        j, h = pl.program_id(0), pl.program_id(1)
        k0 = j * tkb
        kk = k_ref[...]
        vv = v_ref[...]
        n = k0 + lax.broadcasted_iota(jnp.int32, (tkb, 1), 0)

        def qblock(i, carry):
            dk_acc, dv_acc = carry
            r0 = pl.multiple_of(i * tq, tq)
            qi = q_ref[pl.ds(r0, tq), :]
            doi = do_ref[pl.ds(r0, tq), :].astype(BF16)
            t = i * tq + lax.broadcasted_iota(jnp.int32, (1, tq), 1)
            selx = _expand_cols(st_ref[i], k0, tkb) if has_sel else None
            pt, _ = _scores(cfg, _nt(kk, qi), t, n, h, selx, lt_ref[0, i])
            dv_acc = dv_acc + _nn(pt.astype(BF16), doi)
            dst = pt * (_nt(vv, doi) - dt_ref[0, i]) * cfg.scale
            dk_acc = dk_acc + _nn(dst.astype(BF16), qi)
            return dk_acc, dv_acc

        lo, hi = cfg.q_range(j)
        dk_acc, dv_acc = lax.fori_loop(lo, hi, qblock,
                                       (jnp.zeros((tkb, dk), F32), jnp.zeros((tkb, HEAD_V), F32)))
        if cfg.per_head:
            dk_ref[...] = dk_acc
            dv_ref[...] = dv_acc
        else:
            @pl.when(h == 0)
            def _():
                dk_ref[...] = dk_acc
                dv_ref[...] = dv_acc

            @pl.when(h > 0)
            def _():
                dk_ref[...] += dk_acc
                dv_ref[...] += dv_acc

    ks = pl.BlockSpec((tkb, dk), lambda j, h: (j, col(h)))
    vs = pl.BlockSpec((tkb, HEAD_V), lambda j, h: (j, col(h)))
    rowv = pl.BlockSpec((1, nq, 1, tq), lambda j, h: (h, 0, 0, 0))
    ins = [ks, vs, pl.BlockSpec((s, dk), lambda j, h: (0, h)),
           pl.BlockSpec((s, HEAD_V), lambda j, h: (0, h)), rowv, rowv]
    args = [k, v, q, do, lse_t, d_t]
    if has_sel:
        ins.append(pl.BlockSpec((nq, LANE, tq), lambda j, h: (0, 0, 0)))
        args.append(selt)
    return _pcall(
        kern, name=name, grid=(cfg.sk // tkb, cfg.h), in_specs=ins, out_specs=[ks, vs],
        out_shape=[jax.ShapeDtypeStruct((cfg.sk, hk * dk), F32),
                   jax.ShapeDtypeStruct((cfg.sk, hk * HEAD_V), F32)],
        compiler_params=pltpu.CompilerParams(dimension_semantics=("parallel", "arbitrary")),
    )(*args)


def _attn_bwd(cfg, q, k, v, sel, o, lse, do, dq_in, name):
    dq, dvec = _attn_dq(cfg, q, k, v, sel, o, lse, do, dq_in, name + "_dq")
    to_rows = lambda a: a.reshape(cfg.h, cfg.nq, 1, cfg.tq)
    selt = None
    if sel is not None:
        selt = jnp.transpose(sel.reshape(cfg.nq, cfg.tq, LANE), (0, 2, 1))
    dk, dv = _attn_dkv(cfg, q, k, v, selt, to_rows(lse), to_rows(dvec), do, name + "_dkv")
    return dq, dk, dv


def _select(cfg, q, k_cmp, overlap):
    s, tq, sk = cfg.s, cfg.tq, cfg.sk
    n_s = s // SLC_LEN
    top_n = min(SLC_TOPN, n_s)

    def kern(q_ref, k_ref, ov_ref, sel_ref):
        i = pl.program_id(0)
        t = i * tq + lax.broadcasted_iota(jnp.int32, (tq, 1), 0)
        n = lax.broadcasted_iota(jnp.int32, (1, sk), 1)
        kk = k_ref[...]
        imp = jnp.zeros((tq, LANE), F32)
        for h in range(NSA_HEADS):
            sc, mask = _scores(cfg, _nt(q_ref[:, h * 256:(h + 1) * 256], kk), t, n, h, None)
            m = jnp.max(sc, axis=1, keepdims=True)
            e = jnp.where(mask, jnp.exp(sc - m), 0.0)
            p = e / (jnp.sum(e, axis=1, keepdims=True) + 1e-20)
            imp = imp + _nn(p.astype(BF16), ov_ref[...])
        j = lax.broadcasted_iota(jnp.int32, (tq, LANE), 1)
        cur = t >> 6
        forced = (j == 0) | (j == cur) | (j == cur - 1)
        imp = jnp.where(forced, 1e9, imp)
        imp = jnp.where(j > cur, -1e9, imp)
        imp = jnp.where(j >= n_s, -3e38, imp)

        def pick(_, carry):
            work, chosen = carry
            mx = jnp.max(work, axis=1, keepdims=True)
            first = jnp.min(jnp.where(work == mx, j, LANE), axis=1, keepdims=True)
            hit = j == first
            return jnp.where(hit, -3e38, work), jnp.where(hit, 1.0, chosen)

        _, chosen = lax.fori_loop(0, top_n, pick, (imp, jnp.zeros((tq, LANE), F32)))
        sel_ref[...] = jnp.where(j <= cur, chosen, 0.0)

    return _pcall(
        kern, name="nsa_select", grid=(cfg.nq,),
        in_specs=[pl.BlockSpec((tq, NSA_HEADS * 256), lambda i: (i, 0)),
                  pl.BlockSpec((sk, 256), lambda i: (0, 0)), pl.BlockSpec((sk, LANE), lambda i: (0, 0))],
        out_specs=pl.BlockSpec((tq, LANE), lambda i: (i, 0)),
        out_shape=jax.ShapeDtypeStruct((s, LANE), F32),
    )(q, k_cmp, overlap)


def _silu_grad(pre):
    sg = _sigmoid(pre)
    return sg * (1.0 + pre * (1.0 - sg))


def _compress_fwd(a_lo, a_hi, pe_lo, pe_hi, w1_lo, w1_hi, w2, name):
    n, dp = a_lo.shape[0], w2.shape[1]

    def kern(alo, ahi, plo, phi, w1l, w1h, w2r, out_ref, pre_ref):
        xl = (alo[...] + plo[...]).astype(BF16)
        xh = (ahi[...] + phi[...]).astype(BF16)
        pre = _nn(xl, w1l[...]) + _nn(xh, w1h[...])
        act = pre * _sigmoid(pre)
        out_ref[...] = _nn(act.astype(BF16), w2r[...])
        pre_ref[...] = pre

    return _pcall(kern, name=name,
                  out_shape=[jax.ShapeDtypeStruct((n, dp), F32), jax.ShapeDtypeStruct((n, dp), F32)],
                  )(a_lo, a_hi, pe_lo, pe_hi, w1_lo, w1_hi, w2)


def _compress_bwd(a_lo, a_hi, pe_lo, pe_hi, w1_lo, w1_hi, w2, pre, pre_sh, dout, dout_sh, name):
    n, ln = a_lo.shape
    dp = w2.shape[1]

    def kern(alo, ahi, plo, phi, w1l, w1h, w2r, pre_ref, presh_ref, do_ref, dosh_ref,
             da_ref, dpl_ref, dph_ref, dw1l_ref, dw1h_ref, dw2_ref):
        prev = pre_ref[...]
        act = prev * _sigmoid(prev)
        dob = do_ref[...].astype(BF16)
        w2v = w2r[...]
        dpre = (_nt(dob, w2v) * _silu_grad(prev)).astype(BF16)
        dpre_sh = (_nt(dosh_ref[...].astype(BF16), w2v) * _silu_grad(presh_ref[...])).astype(BF16)
        dw2_ref[...] = _nn(act.T.astype(BF16), dob)
        xl = alo[...] + plo[...]
        xh = ahi[...] + phi[...]
        dw1l_ref[...] = _nn(xl.T.astype(BF16), dpre)
        dw1h_ref[...] = _nn(xh.T.astype(BF16), dpre)
        dal = _nt(dpre, w1l[...])
        dah_sh = _nt(dpre_sh, w1h[...])
        da_ref[...] = dal + dah_sh
        dpl_ref[...] = jnp.sum(dal, axis=0, keepdims=True)
        dph_ref[...] = jnp.sum(dah_sh, axis=0, keepdims=True)

    return _pcall(
        kern, name=name,
        out_shape=[jax.ShapeDtypeStruct((n, ln), F32), jax.ShapeDtypeStruct((1, ln), F32),
                   jax.ShapeDtypeStruct((1, ln), F32), jax.ShapeDtypeStruct((ln, dp), F32),
                   jax.ShapeDtypeStruct((ln, dp), F32), jax.ShapeDtypeStruct((dp, dp), F32)],
    )(a_lo, a_hi, pe_lo, pe_hi, w1_lo, w1_hi, w2, pre, pre_sh, dout, dout_sh)


def _nsa_combine(o_cmp, o_slc, o_win, gl):
    s, w = o_cmp.shape
    tr = _tile(s, 512)

    def kern(a_ref, b_ref, c_ref, g_ref, o_ref):
        g = _sigmoid(g_ref[...])
        for h in range(NSA_HEADS):
            cs = slice(h * HEAD_V, (h + 1) * HEAD_V)
            o_ref[:, cs] = (g[:, 3 * h:3 * h + 1] * a_ref[:, cs] + g[:, 3 * h + 1:3 * h + 2] * b_ref[:, cs]
                            + g[:, 3 * h + 2:3 * h + 3] * c_ref[:, cs])

    row = pl.BlockSpec((tr, w), lambda i: (i, 0))
    return _pcall(kern, name="nsa_combine", grid=(s // tr,),
                  in_specs=[row, row, row, pl.BlockSpec((tr, LANE), lambda i: (i, 0))], out_specs=row,
                  out_shape=jax.ShapeDtypeStruct((s, w), F32))(o_cmp, o_slc, o_win, gl)


def _nsa_combine_bwd(do_cat, o_cmp, o_slc, o_win, gl):
    s, w = o_cmp.shape
    tr = _tile(s, 512)

    def kern(d_ref, a_ref, b_ref, c_ref, g_ref, da_ref, db_ref, dc_ref, dg_ref):
        g = _sigmoid(g_ref[...])
        lane = lax.broadcasted_iota(jnp.int32, (tr, LANE), 1)
        dgl = jnp.zeros((tr, LANE), F32)
        for h in range(NSA_HEADS):
            cs = slice(h * HEAD_V, (h + 1) * HEAD_V)
            dv = d_ref[:, cs]
            for b, (src, dst) in enumerate(((a_ref, da_ref), (b_ref, db_ref), (c_ref, dc_ref))):
                gate = g[:, 3 * h + b:3 * h + b + 1]
                dst[:, cs] = gate * dv
                dgate = jnp.sum(dv * src[:, cs], axis=1, keepdims=True)
                dgl = jnp.where(lane == 3 * h + b, dgate * gate * (1.0 - gate), dgl)
        dg_ref[...] = dgl

    row = pl.BlockSpec((tr, w), lambda i: (i, 0))
    tab = pl.BlockSpec((tr, LANE), lambda i: (i, 0))
    return _pcall(kern, name="nsa_combine_bwd", grid=(s // tr,),
                  in_specs=[pl.BlockSpec((tr, w), lambda i: (i, 2)), row, row, row, tab],
                  out_specs=[row, row, row, tab],
                  out_shape=[jax.ShapeDtypeStruct((s, w), F32)] * 3 + [jax.ShapeDtypeStruct((s, LANE), F32)],
                  )(do_cat, o_cmp, o_slc, o_win, gl)


def _gate_fwd(o_mla, o_nsa, o_mem, hp):
    s = o_mla.shape[0]
    tr = _tile(s, 256)

    def kern(a_ref, b_ref, c_ref, z_ref, u_ref):
        z = z_ref[...]
        sz = z * _sigmoid(z)
        u_ref[:, 0:1024] = (a_ref[...] * sz[:, 0:1024]).astype(BF16)
        u_ref[:, 1024:1536] = (b_ref[...] * sz[:, 1024:1536]).astype(BF16)
        u_ref[:, 1536:2048] = (c_ref[...] * sz[:, 1536:2048]).astype(BF16)

    return _pcall(
        kern, name="gate_fwd", grid=(s // tr,),
        in_specs=[pl.BlockSpec((tr, 1024), lambda i: (i, 0)), pl.BlockSpec((tr, 512), lambda i: (i, 0)),
                  pl.BlockSpec((tr, 512), lambda i: (i, 0)), pl.BlockSpec((tr, 2048), lambda i: (i, 2))],
        out_specs=pl.BlockSpec((tr, 2048), lambda i: (i, 0)),
        out_shape=jax.ShapeDtypeStruct((s, 2048), BF16))(o_mla, o_nsa, o_mem, hp)


def _gate_bwd(du, o_mla, o_nsa, o_mem, hp):
    s = du.shape[0]
    tr = _tile(s, 256)

    def kern(d_ref, a_ref, b_ref, c_ref, z_ref, do_ref, dz_ref):
        z = z_ref[...]
        sg = _sigmoid(z)
        sz = z * sg
        dsz = sg * (1.0 + z * (1.0 - sg))
        d = d_ref[...]
        do_ref[...] = d * sz
        dz_ref[:, 0:1024] = d[:, 0:1024] * a_ref[...] * dsz[:, 0:1024]
        dz_ref[:, 1024:1536] = d[:, 1024:1536] * b_ref[...] * dsz[:, 1024:1536]
        dz_ref[:, 1536:2048] = d[:, 1536:2048] * c_ref[...] * dsz[:, 1536:2048]

    wide = pl.BlockSpec((tr, 2048), lambda i: (i, 0))
    return _pcall(
        kern, name="gate_bwd", grid=(s // tr,),
        in_specs=[wide, pl.BlockSpec((tr, 1024), lambda i: (i, 0)), pl.BlockSpec((tr, 512), lambda i: (i, 0)),
                  pl.BlockSpec((tr, 512), lambda i: (i, 0)), pl.BlockSpec((tr, 2048), lambda i: (i, 2))],
        out_specs=[wide, wide],
        out_shape=[jax.ShapeDtypeStruct((s, 2048), F32)] * 2)(du, o_mla, o_nsa, o_mem, hp)


def _row_tile(rows):
    if rows <= 1024:
        return rows
    for cand in (512, 256, 128, 64, 32, 16, 8):
        if rows % cand == 0:
            return cand
    raise ValueError(rows)


def _sum_slots(buf, name):
    n, rows, cols = buf.shape
    tr = _row_tile(rows)

    def kern(b_ref, o_ref):
        acc = b_ref[0]
        for i in range(1, n):
            acc = acc + b_ref[i]
        o_ref[...] = acc

    return _pcall(kern, name=name, grid=(rows // tr,),
                  in_specs=[pl.BlockSpec((n, tr, cols), lambda i: (0, i, 0))],
                  out_specs=pl.BlockSpec((tr, cols), lambda i: (i, 0)),
                  out_shape=jax.ShapeDtypeStruct((rows, cols), F32))(buf)


def _add(a, b, name):
    rows, cols = a.shape
    tr = _row_tile(rows)

    def kern(a_ref, b_ref, o_ref):
        o_ref[...] = a_ref[...] + b_ref[...]

    row = pl.BlockSpec((tr, cols), lambda i: (i, 0))
    return _pcall(kern, name=name, grid=(rows // tr,), in_specs=[row, row], out_specs=row,
                  out_shape=jax.ShapeDtypeStruct((rows, cols), F32))(a, b)


def _adamw(w, g, m, v, name):
    rows, cols = w.shape
    tr = _row_tile(rows)
    bc1 = 1.0 - ADAM_B1 ** ADAM_STEP
    bc2 = 1.0 - ADAM_B2 ** ADAM_STEP

    def kern(w_ref, g_ref, m_ref, v_ref, d_ref, mo_ref, vo_ref):
        gv = g_ref[...]
        mn = ADAM_B1 * m_ref[...] + (1.0 - ADAM_B1) * gv
        vn = ADAM_B2 * v_ref[...] + (1.0 - ADAM_B2) * (gv * gv)
        d_ref[...] = -ADAM_LR * ((mn / bc1) / (jnp.sqrt(vn / bc2) + ADAM_EPS) + ADAM_WD * w_ref[...])
        mo_ref[...] = mn
        vo_ref[...] = vn

    row = pl.BlockSpec((tr, cols), lambda i: (i, 0))
    return _pcall(kern, name=name, grid=(rows // tr,), in_specs=[row] * 4, out_specs=[row] * 3,
                  out_shape=[jax.ShapeDtypeStruct((rows, cols), F32)] * 3)(w, g, m, v)


ANY = pl.BlockSpec(memory_space=pl.ANY)


def _place():
    x, y, c = lax.axis_index("x"), lax.axis_index("y"), lax.axis_index("c")
    chips = [(1 - x, y), (x, 1 - y), (1 - x, 1 - y)]
    return x, y, c, chips


def _gather_shards(wpack):
    rows, cols = wpack.shape
    half = rows // 2

    def body(w_ref, out_ref, send_sems, recv_sems, local_sem):
        x, y, c, chips = _place()
        me = 2 * x + y
        sibling = (x, y, 1 - c)

        def part(slot, core):
            return out_ref.at[slot, pl.ds(core * half, half), :]

        def copy(sem, src, dst, to):
            return pltpu.make_async_remote_copy(src_ref=src, dst_ref=dst, send_sem=send_sems.at[sem],
                                                recv_sem=recv_sems.at[sem], device_id=to, device_id_type=MESH)

        mine = pltpu.make_async_copy(w_ref, out_ref.at[me], local_sem)
        mine.start()
        first = [copy(j, w_ref.at[pl.ds(c * half, half), :], part(me, c), (*chip, c))
                 for j, chip in enumerate(chips)]
        for cp in first:
            cp.start()
        passed = []
        for j, (cx, cy) in enumerate(chips):
            slot = 2 * cx + cy
            copy(j, part(slot, c), part(slot, c), (x, y, c)).wait_recv()
            fwd = copy(3 + j, part(slot, c), part(slot, c), sibling)
            fwd.start()
            passed.append(fwd)
        for j, (cx, cy) in enumerate(chips):
            slot = 2 * cx + cy
            copy(3 + j, part(slot, 1 - c), part(slot, 1 - c), (x, y, c)).wait_recv()
        for cp in first + passed:
            cp.wait_send()
        mine.wait()

    return _pcall(
        body, name="gather_shards", in_specs=[ANY], out_specs=ANY,
        out_shape=jax.ShapeDtypeStruct((4, rows, cols), wpack.dtype),
        scratch_shapes=[pltpu.SemaphoreType.DMA((6,)), pltpu.SemaphoreType.DMA((6,)), pltpu.SemaphoreType.DMA],
    )(wpack)


def _pair_exchange(g):
    n, rows, cols = g.shape
    half = rows // 2

    def body(g_ref, out_ref, send_sem, recv_sem):
        x, y, c, _ = _place()
        cp = pltpu.make_async_remote_copy(
            src_ref=g_ref.at[:, pl.ds((1 - c) * half, half), :], dst_ref=out_ref,
            send_sem=send_sem, recv_sem=recv_sem, device_id=(x, y, 1 - c), device_id_type=MESH)
        cp.start()
        cp.wait()

    return _pcall(body, name="pair_exchange", in_specs=[ANY], out_specs=ANY,
                  out_shape=jax.ShapeDtypeStruct((n, half, cols), g.dtype),
                  scratch_shapes=[pltpu.SemaphoreType.DMA, pltpu.SemaphoreType.DMA])(g)


def _chip_exchange(p):
    n, rows, cols = p.shape

    def body(p_ref, out_ref, send_sems, recv_sems, local_sem):
        x, y, c, chips = _place()
        me = 2 * x + y
        mine = pltpu.make_async_copy(p_ref.at[me], out_ref.at[me], local_sem)
        mine.start()
        sends = []
        for j, (cx, cy) in enumerate(chips):
            cp = pltpu.make_async_remote_copy(
                src_ref=p_ref.at[2 * cx + cy], dst_ref=out_ref.at[me], send_sem=send_sems.at[j],
                recv_sem=recv_sems.at[j], device_id=(cx, cy, c), device_id_type=MESH)
            cp.start()
            sends.append(cp)
        for j, (cx, cy) in enumerate(chips):
            slot = 2 * cx + cy
            pltpu.make_async_remote_copy(
                src_ref=out_ref.at[slot], dst_ref=out_ref.at[slot], send_sem=send_sems.at[j],
                recv_sem=recv_sems.at[j], device_id=(x, y, c), device_id_type=MESH).wait_recv()
        for cp in sends:
            cp.wait_send()
        mine.wait()

    return _pcall(body, name="chip_exchange", in_specs=[ANY], out_specs=ANY,
                  out_shape=jax.ShapeDtypeStruct((n, rows, cols), p.dtype),
                  scratch_shapes=[pltpu.SemaphoreType.DMA((3,)), pltpu.SemaphoreType.DMA((3,)),
                                  pltpu.SemaphoreType.DMA])(p)


def _half_exchange(t):
    half, cols = t.shape

    def body(t_ref, out_ref, send_sem, recv_sem, local_sem):
        x, y, c, _ = _place()
        mine = pltpu.make_async_copy(t_ref, out_ref.at[pl.ds(c * half, half), :], local_sem)
        mine.start()
        cp = pltpu.make_async_remote_copy(
            src_ref=t_ref, dst_ref=out_ref.at[pl.ds(c * half, half), :], send_sem=send_sem, recv_sem=recv_sem,
            device_id=(x, y, 1 - c), device_id_type=MESH)
        cp.start()
        pltpu.make_async_remote_copy(
            src_ref=t_ref, dst_ref=out_ref.at[pl.ds((1 - c) * half, half), :], send_sem=send_sem,
            recv_sem=recv_sem, device_id=(x, y, c), device_id_type=MESH).wait_recv()
        cp.wait_send()
        mine.wait()

    return _pcall(body, name="half_exchange", in_specs=[ANY], out_specs=ANY,
                  out_shape=jax.ShapeDtypeStruct((2 * half, cols), t.dtype),
                  scratch_shapes=[pltpu.SemaphoreType.DMA, pltpu.SemaphoreType.DMA, pltpu.SemaphoreType.DMA])(t)


def _gather_all(v):
    rows, cols = v.shape

    def body(v_ref, out_ref, send_sems, recv_sems, local_sem):
        x, y, c, _ = _place()
        me = 4 * x + 2 * y + c
        mine = pltpu.make_async_copy(v_ref, out_ref.at[me], local_sem)
        mine.start()
        sends = []
        for d in range(1, 8):
            peer = (x ^ (d >> 2), y ^ ((d >> 1) & 1), c ^ (d & 1))
            cp = pltpu.make_async_remote_copy(
                src_ref=v_ref, dst_ref=out_ref.at[me], send_sem=send_sems.at[d - 1],
                recv_sem=recv_sems.at[d - 1], device_id=peer, device_id_type=MESH)
            cp.start()
            sends.append(cp)
        for d in range(1, 8):
            slot = 4 * (x ^ (d >> 2)) + 2 * (y ^ ((d >> 1) & 1)) + (c ^ (d & 1))
            pltpu.make_async_remote_copy(
                src_ref=v_ref, dst_ref=out_ref.at[slot], send_sem=send_sems.at[d - 1],
                recv_sem=recv_sems.at[d - 1], device_id=(x, y, c), device_id_type=MESH).wait_recv()
        for cp in sends:
            cp.wait_send()
        mine.wait()

    return _pcall(body, name="gather_all", in_specs=[ANY], out_specs=ANY,
                  out_shape=jax.ShapeDtypeStruct((8, rows, cols), v.dtype),
                  scratch_shapes=[pltpu.SemaphoreType.DMA((7,)), pltpu.SemaphoreType.DMA((7,)),
                                  pltpu.SemaphoreType.DMA])(v)


def _pad_cols(a, width):
    return a if a.shape[1] == width else jnp.pad(a, ((0, 0), (0, width - a.shape[1])))


def _w_in_padded(w):
    def seg(name, width=None):
        o, n = ORIG[name]
        return _pad_cols(w[:, o:o + n], width or n)

    qn = w[:, ORIG["q_nsa"][0]:ORIG["q_nsa"][0] + 768].reshape(-1, NSA_HEADS, NSA_DK)
    qn = jnp.pad(qn, ((0, 0), (0, 0), (0, 256 - NSA_DK))).reshape(-1, NSA_HEADS * 256)
    kr = seg("k_rope")
    zeros = jnp.zeros((w.shape[0], PAD["z"] - (PAD["q_mem"] + 512)), w.dtype)
    return jnp.concatenate(
        [seg("c_q"), seg("c_kv"), kr, kr, qn, seg("k_c", 256), seg("v_c"), seg("k_s", 256), seg("v_s"),
         seg("k_w", 256), seg("v_w"), seg("g_nsa", LANE), seg("q_mem"), zeros,
         seg("z_mla"), seg("z_nsa"), seg("z_mem")], axis=1)


def _w_in_unpadded(g):
    def seg(name, n):
        return g[:, PAD[name]:PAD[name] + n]

    qn = g[:, PAD["q_nsa"]:PAD["q_nsa"] + 1024].reshape(-1, NSA_HEADS, 256)[:, :, :NSA_DK].reshape(-1, 768)
    z = PAD["z"]
    return jnp.concatenate(
        [seg("c_q", 512), seg("c_kv", 512), seg("k_rope", 64), g[:, z:z + 1024], qn, seg("k_c", 192),
         seg("v_c", 128), seg("k_s", 192), seg("v_s", 128), seg("k_w", 192), seg("v_w", 128),
         seg("g_nsa", 12), g[:, z + 1024:z + 1536], seg("q_mem", 512), g[:, z + 1536:z + 2048]], axis=1)


def _pack(parts, rows):
    flat = jnp.concatenate([p.reshape(-1) for p in parts])
    flat = jnp.pad(flat, (0, rows * PACK_COLS - flat.shape[0]))
    return flat.reshape(rows, PACK_COLS)


def _unpack(buf, shapes):
    flat = buf.reshape(-1)
    out, o = [], 0
    for shp in shapes:
        n = int(np.prod(shp))
        out.append(flat[o:o + n].reshape(shp))
        o += n
    return out


def _rope_tables(s):
    pos = jnp.arange(s, dtype=F32)
    inv_freq = ROPE_THETA ** (-jnp.arange(0, 64, 2, dtype=F32) / 64)
    ang = pos[:, None] * inv_freq[None, :]
    cos, sin = jnp.cos(ang), jnp.sin(ang)
    z = jnp.zeros((s, 64), F32)
    return jnp.concatenate([cos, cos, z], axis=1), jnp.concatenate([-sin, sin, z], axis=1)


def _overlap_table(s):
    n_c, n_s = s // CMP_STRIDE, s // SLC_LEN
    c0 = np.arange(n_c)[:, None] * CMP_STRIDE
    s0 = np.arange(LANE)[None, :] * SLC_LEN
    ov = (c0 < s0 + SLC_LEN) & (c0 + CMP_LEN > s0) & (np.arange(n_c)[:, None] < n_c - 1) & (np.arange(LANE)[None, :] < n_s)
    return jnp.asarray(ov.astype(np.float32), dtype=BF16)


def _shift_rows(a):
    return jnp.concatenate([jnp.zeros((8, a.shape[1]), a.dtype), a], axis=0)[7:7 + a.shape[0]]


def _local_step(x, mem, target, w):
    s = x.shape[0]
    cs, sn = _rope_tables(s)
    t_ = jnp.transpose

    w_in_p = _w_in_padded(w["w_in"])
    xn, rstd_x = _rms_fwd(x, w["norm_g"], "norm_x")
    hp = _mm(xn, w_in_p, "in_proj")

    def col(name, width, dtype=None):
        a = hp[:, PAD[name]:PAD[name] + width]
        return a if dtype is None else a.astype(dtype)

    w_uq3 = w["w_uq"].reshape(512, MLA_HEADS, 192)
    w_uq_p = jnp.concatenate([w_uq3, w_uq3[:, :, 128:]], axis=2).reshape(512, MLA_HEADS * 256)
    w_ukv_p = t_(w["w_ukv"].reshape(512, MLA_HEADS, 2, 128), (0, 2, 1, 3)).reshape(512, 2048)
    c_q, c_kv = col("c_q", 512), col("c_kv", 512)
    cqn, rstd_q = _rms_fwd(c_q, w["q_norm_g"], "norm_q")
    ckvn, rstd_kv = _rms_fwd(c_kv, w["kv_norm_g"], "norm_kv")
    q_lin = _mm(cqn, w_uq_p, "mla_q_proj")
    kv_lin = _mm(ckvn, w_ukv_p, "mla_kv_proj")
    q_mla = _rope_fwd(q_lin, cs, sn, MLA_HEADS, 256, LANE, "rope_q")
    k_pe = _rope_fwd(col("k_rope", LANE), cs, sn, 1, LANE, 0, "rope_k")
    k_nope = kv_lin[:, :1024].astype(BF16).reshape(s, MLA_HEADS, LANE)
    k_mla = jnp.concatenate([k_nope, jnp.broadcast_to(k_pe[:, None, :], (s, MLA_HEADS, LANE))],
                            axis=2).reshape(s, MLA_HEADS * 256)
    v_mla = kv_lin[:, 1024:].astype(BF16)
    mla = _Attn("mla", s, s, MLA_HEADS, 256, True)
    o_mla, l_mla = _attn_fwd(mla, q_mla, k_mla, v_mla, None, "mla_fwd")

    sk = s // CMP_STRIDE
    q_nsa = col("q_nsa", 1024, BF16)
    pe_k, pe_v = w["cmp_pe_k"], w["cmp_pe_v"]
    w1k = _pad_cols(w["cmp_w1k"], 256)
    w2k = jnp.pad(w["cmp_w2k"], ((0, 64), (0, 64))).astype(BF16)
    w1v, w2v = w["cmp_w1v"], w["cmp_w2v"].astype(BF16)
    half_k, half_v = CMP_STRIDE * NSA_DK, CMP_STRIDE * HEAD_V
    ak = col("k_c", NSA_DK).reshape(sk, half_k)
    av = col("v_c", HEAD_V).reshape(sk, half_v)
    up = lambda a: jnp.concatenate([a, jnp.zeros((8, a.shape[1]), a.dtype)], axis=0)[1:1 + a.shape[0]]
    ck_args = (ak, up(ak), pe_k[:CMP_STRIDE].reshape(1, half_k), pe_k[CMP_STRIDE:].reshape(1, half_k),
               w1k[:half_k], w1k[half_k:], w2k)
    cv_args = (av, up(av), pe_v[:CMP_STRIDE].reshape(1, half_v), pe_v[CMP_STRIDE:].reshape(1, half_v),
               w1v[:half_v], w1v[half_v:], w2v)
    k_cmp, pre_k = _compress_fwd(*ck_args, "compress_k")
    v_cmp, pre_v = _compress_fwd(*cv_args, "compress_v")
    k_cmp_b, v_cmp_b = k_cmp.astype(BF16), v_cmp.astype(BF16)
    cmp_ = _Attn("cmp", s, sk, NSA_HEADS, 256, False)
    slc = _Attn("slc", s, s, NSA_HEADS, 256, False)
    win = _Attn("win", s, s, NSA_HEADS, 256, False)
    o_cmp, l_cmp = _attn_fwd(cmp_, q_nsa, k_cmp_b, v_cmp_b, None, "cmp_fwd")
    sel = _select(cmp_, q_nsa, k_cmp_b, _overlap_table(s))
    k_s, v_s = col("k_s", 256, BF16), col("v_s", HEAD_V, BF16)
    k_w, v_w = col("k_w", 256, BF16), col("v_w", HEAD_V, BF16)
    o_slc, l_slc = _attn_fwd(slc, q_nsa, k_s, v_s, sel, "slc_fwd")
    o_win, l_win = _attn_fwd(win, q_nsa, k_w, v_w, None, "win_fwd")
    gl = col("g_nsa", LANE)
    o_nsa = _nsa_combine(o_cmp, o_slc, o_win, gl)

    mn, rstd_m = _rms_fwd(mem, w["mem_norm_g"], "norm_mem")
    kvm = _mm(mn, w["w_mem_kv"], "mem_kv_proj")
    k_mem, v_mem = kvm[:, :512].astype(BF16), kvm[:, 512:].astype(BF16)
    q_mem = col("q_mem", 512, BF16)
    mem_ = _Attn("mem", s, mem.shape[0], MEM_HEADS, LANE, True)
    o_mem, l_mem = _attn_fwd(mem_, q_mem, k_mem, v_mem, None, "mem_fwd")

    u = _gate_fwd(o_mla, o_nsa, o_mem, hp)
    proj = _mm(u, w["w_out"], "out_proj")
    dy, g_final, loss = _final_loss(x, proj, w["final_norm_g"].reshape(1, -1), target)

    g_w_out = _mm(t_(u), dy, "out_proj_dw")
    du = _mm(dy, t_(w["w_out"]), "out_proj_dx")
    do_cat, dz = _gate_bwd(du, o_mla, o_nsa, o_mem, hp)

    dq_mem, dk_mem, dv_mem = _attn_bwd(mem_, q_mem, k_mem, v_mem, None, o_mem, l_mem,
                                       do_cat[:, 1536:], None, "mem_bwd")
    dkvm = jnp.concatenate([dk_mem, dv_mem], axis=1)
    g_w_mem_kv = _mm(t_(mn), dkvm, "mem_kv_dw")
    dmn = _mm(dkvm, t_(w["w_mem_kv"]), "mem_kv_dx")
    _, g_mem_norm = _rms_bwd(mem, w["mem_norm_g"], rstd_m, dmn, None, "norm_mem_bwd")

    do_cmp, do_slc, do_win, dgl = _nsa_combine_bwd(do_cat, o_cmp, o_slc, o_win, gl)
    dq_n, dk_cmp, dv_cmp = _attn_bwd(cmp_, q_nsa, k_cmp_b, v_cmp_b, None, o_cmp, l_cmp, do_cmp, None, "cmp_bwd")
    dq_n, dk_s, dv_s = _attn_bwd(slc, q_nsa, k_s, v_s, sel, o_slc, l_slc, do_slc, dq_n, "slc_bwd")
    dq_n, dk_w, dv_w = _attn_bwd(win, q_nsa, k_w, v_w, None, o_win, l_win, do_win, dq_n, "win_bwd")
    dak, dpk_lo, dpk_hi, dw1k_lo, dw1k_hi, g_w2k = _compress_bwd(
        *ck_args, pre_k, _shift_rows(pre_k), dk_cmp, _shift_rows(dk_cmp), "compress_k_bwd")
    dav, dpv_lo, dpv_hi, dw1v_lo, dw1v_hi, g_w2v = _compress_bwd(
        *cv_args, pre_v, _shift_rows(pre_v), dv_cmp, _shift_rows(dv_cmp), "compress_v_bwd")
    g_pe_k = jnp.concatenate([dpk_lo.reshape(CMP_STRIDE, NSA_DK), dpk_hi.reshape(CMP_STRIDE, NSA_DK)], axis=0)
    g_pe_v = jnp.concatenate([dpv_lo.reshape(CMP_STRIDE, HEAD_V), dpv_hi.reshape(CMP_STRIDE, HEAD_V)], axis=0)
    g_w1k = jnp.concatenate([dw1k_lo, dw1k_hi], axis=0)[:, :NSA_DK]
    g_w1v = jnp.concatenate([dw1v_lo, dw1v_hi], axis=0)
    dk_c = _pad_cols(dak.reshape(s, NSA_DK), 256)
    dv_c = dav.reshape(s, HEAD_V)

    dq_m, dk_m, dv_m = _attn_bwd(mla, q_mla, k_mla, v_mla, None, o_mla, l_mla, do_cat[:, :1024], None, "mla_bwd")
    dq_lin = _rope_bwd_q(dq_m, cs, sn)
    dkv_lin, d_krope = _rope_bwd_k(dk_m, dv_m, cs, sn)
    g_w_uq_p = _mm(t_(cqn), dq_lin, "mla_q_dw")
    dcqn = _mm(dq_lin, t_(w_uq_p), "mla_q_dx")
    g_w_ukv_p = _mm(t_(ckvn), dkv_lin, "mla_kv_dw")
    dckvn = _mm(dkv_lin, t_(w_ukv_p), "mla_kv_dx")
    dc_q, g_q_norm = _rms_bwd(c_q, w["q_norm_g"], rstd_q, dcqn, None, "norm_q_bwd")
    dc_kv, g_kv_norm = _rms_bwd(c_kv, w["kv_norm_g"], rstd_kv, dckvn, None, "norm_kv_bwd")
    g_w_uq = g_w_uq_p.reshape(512, MLA_HEADS, 256)[:, :, :192].reshape(512, MLA_HEADS * 192)
    g_w_ukv = t_(g_w_ukv_p.reshape(512, 2, MLA_HEADS, 128), (0, 2, 1, 3)).reshape(512, 2048)

    dhp = jnp.concatenate(
        [dc_q, dc_kv, d_krope, dq_n, dk_c, dv_c, dk_s, dv_s, dk_w, dv_w, dgl, dq_mem,
         jnp.zeros((s, PAD["z"] - (PAD["q_mem"] + 512)), F32), dz], axis=1)
    g_w_in = _w_in_unpadded(_mm(t_(xn), dhp, "in_proj_dw"))
    dxn = _mm(dhp, t_(w_in_p), "in_proj_dx")
    grad_x, g_norm = _rms_bwd(x, w["norm_g"], rstd_x, dxn, dy, "norm_x_bwd")

    grads = dict(norm_g=g_norm, w_in=g_w_in, q_norm_g=g_q_norm, w_uq=g_w_uq, kv_norm_g=g_kv_norm,
                 w_ukv=g_w_ukv, cmp_pe_k=g_pe_k, cmp_pe_v=g_pe_v, cmp_w1k=g_w1k, cmp_w2k=g_w2k[:NSA_DK, :NSA_DK],
                 cmp_w1v=g_w1v, cmp_w2v=g_w2v, mem_norm_g=g_mem_norm, w_mem_kv=g_w_mem_kv, w_out=g_w_out,
                 final_norm_g=g_final.reshape(-1))
    return loss[0, 0], grad_x, grads


def _rows_for(n_elems, mult):
    rows = -(-n_elems // PACK_COLS)
    return -(-rows // mult) * mult


def kernel(x, mem, norm_g, w_in, q_norm_g, w_uq, kv_norm_g, w_ukv, cmp_pe_k, cmp_pe_v, cmp_w1k, cmp_w2k, cmp_w1v, cmp_w2v, mem_norm_g, w_mem_kv, w_out, final_norm_g, loss_target, m_norm_g, m_w_in, m_q_norm_g, m_w_uq, m_kv_norm_g, m_w_ukv, m_cmp_pe_k, m_cmp_pe_v, m_cmp_w1k, m_cmp_w2k, m_cmp_w1v, m_cmp_w2v, m_mem_norm_g, m_w_mem_kv, m_w_out, m_final_norm_g, v_norm_g, v_w_in, v_q_norm_g, v_w_uq, v_kv_norm_g, v_w_ukv, v_cmp_pe_k, v_cmp_pe_v, v_cmp_w1k, v_cmp_w2k, v_cmp_w1v, v_cmp_w2v, v_mem_norm_g, v_w_mem_kv, v_w_out, v_final_norm_g):
    args = dict(locals())
    wts = {n: args[n] for n in WEIGHTS}
    loc = {n: (a if n == "final_norm_g" else a[0]) for n, a in wts.items()}
    shard_shapes = [loc[n].shape for n in SHARDED]
    n_sh = sum(int(np.prod(sh)) for sh in shard_shapes)
    rows_sh = _rows_for(n_sh, 64)

    gathered = _gather_shards(_pack([loc[n].astype(BF16) for n in SHARDED], rows_sh))
    pieces = [_unpack(gathered[j], shard_shapes) for j in range(4)]
    full = {n: loc[n].reshape(1, -1) if loc[n].ndim == 1 else loc[n] for n in REPLICATED}
    for i, n in enumerate(SHARDED):
        full[n] = jnp.concatenate([pieces[j][i] for j in range(4)], axis=SHARD_AXIS[n])

    loss, grad_x, g = _local_step(x[0], mem[0], loss_target[0], full)
    loss = lax.psum(loss, ("x", "y", "c"))

    def shard_of(n, j):
        a, ax = g[n], SHARD_AXIS[n]
        width = a.shape[ax] // 4
        return lax.slice_in_dim(a, j * width, (j + 1) * width, axis=ax)

    gp = jnp.stack([_pack([shard_of(n, j) for n in SHARDED], rows_sh) for j in range(4)])
    c = lax.axis_index("c")
    half = rows_sh // 2
    theirs = _pair_exchange(gp)
    ours = lax.dynamic_slice_in_dim(gp, c * half, half, axis=1)
    pair = _add(ours.reshape(4 * half, PACK_COLS), theirs.reshape(4 * half, PACK_COLS), "pair_sum")
    from_chips = _chip_exchange(pair.reshape(4, half, PACK_COLS))
    g_sh = _half_exchange(_sum_slots(from_chips, "chip_sum"))

    rep_shapes = [loc[n].shape for n in REPLICATED]
    n_rep = sum(int(np.prod(sh)) for sh in rep_shapes)
    rows_rep = -(-n_rep // (8 * LANE)) * 8

    def rep_pack(parts):
        flat = jnp.concatenate([p.reshape(-1) for p in parts])
        return jnp.pad(flat, (0, rows_rep * LANE - n_rep)).reshape(rows_rep, LANE)

    g_rep = _sum_slots(_gather_all(rep_pack([g[n] for n in REPLICATED])), "replica_sum")

    sh_pack = lambda prefix: _pack([(args[prefix + n] if prefix else wts[n])[0] for n in SHARDED], rows_sh)
    rp_pack = lambda prefix: rep_pack([(args[prefix + n] if prefix else wts[n]) for n in REPLICATED])
    d_sh, m_sh, v_sh = _adamw(sh_pack(""), g_sh, sh_pack("m_"), sh_pack("v_"), "adamw_sharded")
    d_rp, m_rp, v_rp = _adamw(rp_pack(""), g_rep, rp_pack("m_"), rp_pack("v_"), "adamw_replicated")

    out_shapes = {n: wts[n].shape for n in WEIGHTS}

    def spread(sh_buf, rp_buf):
        sh = dict(zip(SHARDED, _unpack(sh_buf, [out_shapes[n] for n in SHARDED])))
        rp = dict(zip(REPLICATED, _unpack(rp_buf, [out_shapes[n] for n in REPLICATED])))
        return [sh[n] if n in sh else rp[n] for n in WEIGHTS]

    return (loss, grad_x[None], *spread(g_sh, g_rep), *spread(d_sh, d_rp), *spread(m_sh, m_rp),
            *spread(v_sh, v_rp))
```

```python
from typing import NamedTuple

import numpy as np
import jax
import jax.numpy as jnp
from jax import lax
from jax.experimental import pallas as pl
from jax.experimental.pallas import tpu as pltpu

F32 = jnp.float32
BF16 = jnp.bfloat16
MESH = pl.DeviceIdType.MESH

D_MODEL = 2048
EPS = 1e-6
LANE = 128
HEAD_V = 128
MLA_HEADS = 8
NSA_HEADS = 4
MEM_HEADS = 4
NSA_DK = 192
CMP_STRIDE = 16
CMP_LEN = 32
SLC_LEN = 64
SLC_TOPN = 16
WIN = 512
NEG = -1e30
ROPE_THETA = 10000.0
BLOCK_BYTES = 2 << 20

ORIG = dict(c_q=(0, 512), c_kv=(512, 512), k_rope=(1024, 64), z_mla=(1088, 1024),
            q_nsa=(2112, 768), k_c=(2880, 192), v_c=(3072, 128), k_s=(3200, 192),
            v_s=(3392, 128), k_w=(3520, 192), v_w=(3712, 128), g_nsa=(3840, 12),
            z_nsa=(3852, 512), q_mem=(4364, 512), z_mem=(4876, 512))
PAD = dict(c_q=0, c_kv=512, q_nsa=1024, k_c=2048, k_s=2304, k_w=2560, k_rope=2816, v_c=2944,
           v_s=3072, v_w=3200, g_nsa=3328, q_mem=3456, z=4096)
D_PAD = 6144

ADAM_LR, ADAM_B1, ADAM_B2, ADAM_EPS, ADAM_WD, ADAM_STEP = 0.001, 0.9, 0.999, 1e-08, 0.01, 10

SHARDED = ("w_in", "w_uq", "w_ukv", "cmp_w1k", "cmp_w1v", "w_mem_kv", "w_out")
SHARD_AXIS = dict(w_in=1, w_uq=1, w_ukv=1, cmp_w1k=0, cmp_w1v=0, w_mem_kv=0, w_out=0)
REPLICATED = ("norm_g", "q_norm_g", "kv_norm_g", "cmp_pe_k", "cmp_pe_v", "cmp_w2k", "cmp_w2v",
              "mem_norm_g", "final_norm_g")
WEIGHTS = ("norm_g", "w_in", "q_norm_g", "w_uq", "kv_norm_g", "w_ukv", "cmp_pe_k", "cmp_pe_v",
           "cmp_w1k", "cmp_w2k", "cmp_w1v", "cmp_w2v", "mem_norm_g", "w_mem_kv", "w_out",
           "final_norm_g")


def _pcall(kernel, **kw):
    return pl.pallas_call(kernel, **kw)


def _tile(n, pref):
    if n <= pref:
        return n
    for t in range(pref, LANE - 1, -LANE):
        if n % t == 0:
            return t
    raise ValueError((n, pref))


def _row_tile(rows, cols, itemsize=4):
    want = max(16, BLOCK_BYTES // (cols * itemsize))
    if rows <= want:
        return rows
    t = 16
    best = rows
    while t <= want:
        if rows % t == 0:
            best = t
        t *= 2
    return best


def _nt(a, b):
    return lax.dot_general(a, b, (((1,), (1,)), ((), ())), preferred_element_type=F32)


def _tn(a, b):
    return lax.dot_general(a, b, (((0,), (0,)), ((), ())), preferred_element_type=F32)


def _nn(a, b):
    return jnp.dot(a, b, preferred_element_type=F32)


def _sigmoid(x):
    return 1.0 / (1.0 + jnp.exp(-x))


class _Src(NamedTuple):
    arr: jax.Array
    width: int
    col0: int = 0
    per_head: bool = True

    def col(self, h):
        return self.col0 + h if self.per_head else self.col0


def _mm(a, b, name, mode="nn", out_dtype=F32, second_dtype=None):
    if mode == "tn":
        k, m = a.shape
    else:
        m, k = a.shape
    if mode == "nt":
        n, k2 = b.shape
    else:
        k2, n = b.shape
    assert k == k2, (a.shape, b.shape, mode)
    tm, tn, tk = _tile(m, 1024), _tile(n, 1024), _tile(k, 2048)
    nk = k // tk
    assert nk == 1 or (out_dtype == F32 and second_dtype is None)
    dot = {"nn": _nn, "nt": _nt, "tn": _tn}[mode]

    def kern(a_ref, b_ref, o_ref, *more):
        r = dot(a_ref[...].astype(BF16), b_ref[...].astype(BF16))
        if nk == 1:
            o_ref[...] = r.astype(out_dtype)
            if more:
                more[0][...] = r.astype(second_dtype)
        else:
            kk = pl.program_id(2)

            @pl.when(kk == 0)
            def _():
                o_ref[...] = r

            @pl.when(kk > 0)
            def _():
                o_ref[...] += r

    a_spec = (pl.BlockSpec((tk, tm), lambda i, j, kk: (kk, i)) if mode == "tn"
              else pl.BlockSpec((tm, tk), lambda i, j, kk: (i, kk)))
    b_spec = (pl.BlockSpec((tn, tk), lambda i, j, kk: (j, kk)) if mode == "nt"
              else pl.BlockSpec((tk, tn), lambda i, j, kk: (kk, j)))
    o_spec = pl.BlockSpec((tm, tn), lambda i, j, kk: (i, j))
    out_shape = jax.ShapeDtypeStruct((m, n), out_dtype)
    if second_dtype is not None:
        o_spec = [o_spec, o_spec]
        out_shape = [out_shape, jax.ShapeDtypeStruct((m, n), second_dtype)]
    return _pcall(
        kern, name=name, grid=(m // tm, n // tn, nk), in_specs=[a_spec, b_spec], out_specs=o_spec,
        out_shape=out_shape,
        compiler_params=pltpu.CompilerParams(dimension_semantics=("parallel", "parallel", "arbitrary")),
    )(a, b)


def _rms_fwd(x, g, name):
    r, d = x.arr.shape[0], x.width
    tr = _tile(r, 512)

    def kern(x_ref, g_ref, y_ref, r_ref):
        xv = x_ref[...]
        rstd = lax.rsqrt(jnp.mean(xv * xv, axis=-1, keepdims=True) + EPS)
        y_ref[...] = (xv * rstd * g_ref[...]).astype(BF16)
        r_ref[...] = rstd

    return _pcall(
        kern, name=name, grid=(r // tr,),
        in_specs=[pl.BlockSpec((tr, d), lambda i: (i, x.col0)), pl.BlockSpec((1, d), lambda i: (0, 0))],
        out_specs=[pl.BlockSpec((tr, d), lambda i: (i, 0)), pl.BlockSpec((tr, 1), lambda i: (i, 0))],
        out_shape=[jax.ShapeDtypeStruct((r, d), BF16), jax.ShapeDtypeStruct((r, 1), F32)],
    )(x.arr, g)


def _rms_bwd(x, g, rstd, dy, add, name):
    r, d = x.arr.shape[0], x.width
    tr = _tile(r, 256)
    has_add = add is not None

    def kern(*refs):
        if has_add:
            x_ref, g_ref, r_ref, dy_ref, add_ref, dx_ref, dg_ref = refs
        else:
            x_ref, g_ref, r_ref, dy_ref, dx_ref, dg_ref = refs
        rs = r_ref[...]
        xhat = x_ref[...] * rs
        dyv = dy_ref[...]
        dyg = dyv * g_ref[...]
        c = jnp.mean(dyg * xhat, axis=-1, keepdims=True)
        dx = rs * (dyg - xhat * c)
        if has_add:
            dx = dx + add_ref[...]
        dx_ref[...] = dx
        part = jnp.sum(dyv * xhat, axis=0, keepdims=True)

        @pl.when(pl.program_id(0) == 0)
        def _():
            dg_ref[...] = part

        @pl.when(pl.program_id(0) > 0)
        def _():
            dg_ref[...] += part

    row = pl.BlockSpec((tr, d), lambda i: (i, 0))
    vec = pl.BlockSpec((1, d), lambda i: (0, 0))
    ins = [pl.BlockSpec((tr, d), lambda i: (i, x.col0)), vec, pl.BlockSpec((tr, 1), lambda i: (i, 0)), row]
    ins += [row] if has_add else []
    args = (x.arr, g, rstd, dy) + ((add,) if has_add else ())
    return _pcall(
        kern, name=name, grid=(r // tr,), in_specs=ins, out_specs=[row, vec],
        out_shape=[jax.ShapeDtypeStruct((r, d), F32), jax.ShapeDtypeStruct((1, d), F32)],
        compiler_params=pltpu.CompilerParams(dimension_semantics=("arbitrary",)),
    )(*args)


def _final_loss(x, proj, g, target):
    r, d = x.shape
    tr = _tile(r, 256)

    def kern(x_ref, p_ref, g_ref, t_ref, dy_ref, dg_ref, loss_ref):
        y = x_ref[...] + p_ref[...]
        rs = lax.rsqrt(jnp.mean(y * y, axis=-1, keepdims=True) + EPS)
        yhat = y * rs
        gv = g_ref[...]
        e = yhat * gv - t_ref[...]
        lpart = 0.5 * jnp.sum(jnp.mean(e * e, axis=-1, keepdims=True), axis=0, keepdims=True)
        dout = e * (1.0 / d)
        dyg = dout * gv
        c = jnp.mean(dyg * yhat, axis=-1, keepdims=True)
        dy_ref[...] = rs * (dyg - yhat * c)
        gpart = jnp.sum(dout * yhat, axis=0, keepdims=True)
        lrow = jnp.broadcast_to(lpart, (1, LANE))

        @pl.when(pl.program_id(0) == 0)
        def _():
            dg_ref[...] = gpart
            loss_ref[...] = lrow

        @pl.when(pl.program_id(0) > 0)
        def _():
            dg_ref[...] += gpart
            loss_ref[...] += lrow

    row = pl.BlockSpec((tr, d), lambda i: (i, 0))
    vec = pl.BlockSpec((1, d), lambda i: (0, 0))
    return _pcall(
        kern, name="final_loss", grid=(r // tr,), in_specs=[row, row, vec, row],
        out_specs=[row, vec, pl.BlockSpec((1, LANE), lambda i: (0, 0))],
        out_shape=[jax.ShapeDtypeStruct((r, d), F32), jax.ShapeDtypeStruct((1, d), F32),
                   jax.ShapeDtypeStruct((1, LANE), F32)],
        compiler_params=pltpu.CompilerParams(dimension_semantics=("arbitrary",)),
    )(x, proj, g, target)


def _rope_fwd(x, cs, sn, nh, width, off, name):
    s = x.arr.shape[0]
    tr = _tile(s, 512)

    def kern(x_ref, c_ref, s_ref, o_ref):
        cv, sv = c_ref[...], s_ref[...]
        for h in range(nh):
            b = h * width
            if off:
                o_ref[:, b:b + off] = x_ref[:, b:b + off].astype(BF16)
            xr = x_ref[:, b + off:b + off + LANE]
            o_ref[:, b + off:b + off + LANE] = (xr * cv + pltpu.roll(xr, 32, 1) * sv).astype(BF16)

    tab = pl.BlockSpec((tr, LANE), lambda i: (i, 0))
    return _pcall(
        kern, name=name, grid=(s // tr,),
        in_specs=[pl.BlockSpec((tr, nh * width), lambda i: (i, x.col0)), tab, tab],
        out_specs=pl.BlockSpec((tr, nh * width), lambda i: (i, 0)),
        out_shape=jax.ShapeDtypeStruct((s, nh * width), BF16),
    )(x.arr, cs, sn)


def _rope_grad(d, cv, sv):
    g2 = d * sv
    g2 = g2 + pltpu.roll(g2, 64, 1)
    lane = lax.broadcasted_iota(jnp.int32, d.shape, 1)
    return jnp.where(lane < 64, d * cv + pltpu.roll(g2, 32, 1), 0.0)


def _rope_bwd_q(dq, cs, sn):
    s, w = dq.shape
    tr = _tile(s, 512)
    nh = w // 256

    def kern(d_ref, c_ref, s_ref, o_ref):
        cv, sv = c_ref[...], s_ref[...]
        for h in range(nh):
            b = h * 256
            o_ref[:, b:b + LANE] = d_ref[:, b:b + LANE]
            o_ref[:, b + LANE:b + 256] = _rope_grad(d_ref[:, b + LANE:b + 256], cv, sv)

    row = pl.BlockSpec((tr, w), lambda i: (i, 0))
    tab = pl.BlockSpec((tr, LANE), lambda i: (i, 0))
    return _pcall(kern, name="rope_bwd_q", grid=(s // tr,), in_specs=[row, tab, tab], out_specs=row,
                  out_shape=jax.ShapeDtypeStruct((s, w), F32))(dq, cs, sn)


def _rope_bwd_k(dk_nope, dk_pe, dv, cs, sn):
    s, w = dk_nope.shape
    tr = _tile(s, 512)

    def kern(dk_ref, dp_ref, dv_ref, c_ref, s_ref, okv_ref, okr_ref):
        okv_ref[:, :w] = dk_ref[...]
        okv_ref[:, w:] = dv_ref[...]
        okr_ref[...] = _rope_grad(dp_ref[...], c_ref[...], s_ref[...])

    tab = pl.BlockSpec((tr, LANE), lambda i: (i, 0))
    wide = pl.BlockSpec((tr, w), lambda i: (i, 0))
    return _pcall(
        kern, name="rope_bwd_k", grid=(s // tr,), in_specs=[wide, tab, wide, tab, tab],
        out_specs=[pl.BlockSpec((tr, 2 * w), lambda i: (i, 0)), tab],
        out_shape=[jax.ShapeDtypeStruct((s, 2 * w), F32), jax.ShapeDtypeStruct((s, LANE), F32)],
    )(dk_nope, dk_pe, dv, cs, sn)


class _Attn:
    def __init__(self, mode, s, sk, heads, dk):
        self.mode, self.s, self.sk, self.h, self.dk = mode, s, sk, heads, dk
        self.scale = {"mla": 192 ** -0.5, "mem": 128 ** -0.5}.get(mode, NSA_DK ** -0.5)
        self.tq = min(256, s)
        self.nq = s // self.tq
        self.causal = mode in ("mla", "slc")
        if self.causal:
            self.tk = min(512, s)
        elif mode == "win":
            self.tk = WIN + self.tq
        else:
            self.tk = sk
        self.tkb = min(512, sk)
        self.ncmp = s // CMP_STRIDE - 1

    def mask_bias(self, t, n, h, selx, diag):
        m = self.mode
        if m == "mla":
            return (n <= t) if diag else None, None
        if m == "mem":
            return None, None
        slope = jnp.where(h == 0, 0.25, jnp.where(h == 1, 0.0625, jnp.where(h == 2, 0.015625, 0.00390625)))
        slope = slope.astype(F32)
        if m == "cmp":
            mask = (n * CMP_STRIDE + (CMP_LEN - 1) <= t) & (n < self.ncmp)
            pos = n.astype(F32) * float(CMP_STRIDE) + (CMP_LEN - 1) / 2.0
            return mask, -slope * (t.astype(F32) - pos)
        rel = t - n
        bias = -slope * rel.astype(F32)
        if m == "slc":
            mask = selx > 0.5
            return (mask & (rel >= 0)) if diag else mask, bias
        return (rel >= 0) & (rel < WIN), bias


def _scores(cfg, s_raw, t, n, h, selx, diag, lse=None):
    s = s_raw * cfg.scale
    mask, bias = cfg.mask_bias(t, n, h, selx, diag)
    if bias is not None:
        s = s + bias
    if lse is None:
        if mask is not None:
            s = jnp.where(mask, s, NEG)
        return s, mask
    p = jnp.exp(jnp.minimum(s - lse, 0.0))
    if mask is not None:
        p = jnp.where(mask, p, 0.0)
    return p, mask


def _expand_rows(sel, k0, tk):
    j = lax.broadcasted_iota(jnp.int32, (LANE, tk), 0)
    n = lax.broadcasted_iota(jnp.int32, (LANE, tk), 1) + k0
    e = jnp.where((n >> 6) == j, 1.0, 0.0).astype(BF16)
    return _nn(sel.astype(BF16), e)


def _expand_cols(selt, k0, tk):
    n = lax.broadcasted_iota(jnp.int32, (tk, LANE), 0) + k0
    j = lax.broadcasted_iota(jnp.int32, (tk, LANE), 1)
    e = jnp.where((n >> 6) == j, 1.0, 0.0).astype(BF16)
    return _nn(e, selt.astype(BF16))


def _to_row(col):
    t = col.shape[0]
    return jnp.transpose(jnp.broadcast_to(col, (t, LANE)))[0:1, :]


def _load_keys(k_refs, rows):
    parts = [r[rows, :].astype(BF16) for r in k_refs]
    return parts[0] if len(parts) == 1 else jnp.concatenate(parts, axis=1)


def _attn_fwd(cfg, q, ks, v, sel, name):
    s, tq, tk = cfg.s, cfg.tq, cfg.tk
    has_sel = sel is not None
    nkp = len(ks)

    def kern(*refs):
        q_ref, k_refs, v_ref = refs[0], refs[1:1 + nkp], refs[1 + nkp]
        sel_ref = refs[2 + nkp] if has_sel else None
        o_ref, lc_ref, lr_ref = refs[-3:]
        h, i = pl.program_id(0), pl.program_id(1)
        qv = q_ref[...].astype(BF16)
        t = i * tq + lax.broadcasted_iota(jnp.int32, (tq, 1), 0)

        def chunk(k0, carry, diag):
            m, l, acc = carry
            rows = pl.ds(k0, tk)
            kk = _load_keys(k_refs, rows)
            vv = v_ref[rows, :].astype(BF16)
            n = k0 + lax.broadcasted_iota(jnp.int32, (1, tk), 1)
            selx = _expand_rows(sel_ref[...], k0, tk) if has_sel else None
            sc, mask = _scores(cfg, _nt(qv, kk), t, n, h, selx, diag)
            m_new = jnp.maximum(m, jnp.max(sc, axis=1, keepdims=True))
            alpha = jnp.exp(m - m_new)
            p = jnp.exp(sc - m_new)
            if mask is not None:
                p = jnp.where(mask, p, 0.0)
            l = alpha * l + jnp.sum(p, axis=1, keepdims=True)
            acc = alpha * acc + _nn(p.astype(BF16), vv)
            return m_new, l, acc

        carry = (jnp.full((tq, 1), NEG, F32), jnp.zeros((tq, 1), F32), jnp.zeros((tq, HEAD_V), F32))
        if cfg.causal:
            full = (i * tq) // tk
            carry = lax.fori_loop(0, full, lambda c, cr: chunk(pl.multiple_of(c * tk, tk), cr, False), carry)
            carry = chunk(pl.multiple_of(full * tk, tk), carry, True)
        elif cfg.mode == "win":
            carry = chunk(pl.multiple_of(jnp.maximum(i * tq - WIN, 0), tq), carry, True)
        else:
            carry = chunk(0, carry, True)
        m, l, acc = carry
        o_ref[...] = acc / (l + 1e-20)
        lse = m + jnp.log(l + 1e-20)
        lc_ref[0] = lse
        lr_ref[0, 0] = _to_row(lse)

    ins = [pl.BlockSpec((tq, q.width), lambda h, i: (i, q.col(h)))]
    ins += [pl.BlockSpec((cfg.sk, p.width), lambda h, i, p=p: (0, p.col(h))) for p in ks]
    ins += [pl.BlockSpec((cfg.sk, HEAD_V), lambda h, i: (0, v.col(h)))]
    args = [q.arr] + [p.arr for p in ks] + [v.arr]
    if has_sel:
        ins.append(pl.BlockSpec((tq, LANE), lambda h, i: (i, 0)))
        args.append(sel)
    return _pcall(
        kern, name=name, grid=(cfg.h, cfg.nq), in_specs=ins,
        out_specs=[pl.BlockSpec((tq, HEAD_V), lambda h, i: (i, h)),
                   pl.BlockSpec((1, tq, 1), lambda h, i: (h, i, 0)),
                   pl.BlockSpec((1, 1, 1, tq), lambda h, i: (h, i, 0, 0))],
        out_shape=[jax.ShapeDtypeStruct((s, cfg.h * HEAD_V), F32),
                   jax.ShapeDtypeStruct((cfg.h, s, 1), F32),
                   jax.ShapeDtypeStruct((cfg.h, cfg.nq, 1, tq), F32)],
        compiler_params=pltpu.CompilerParams(dimension_semantics=("parallel", "parallel")),
    )(*args)


def _attn_dq(cfg, q, ks, v, sel, o, lse, do, dq_in, name):
    s, tq, tk, dk = cfg.s, cfg.tq, cfg.tk, cfg.dk
    has_sel = sel is not None
    has_in = dq_in is not None
    nkp = len(ks)

    def kern(*refs):
        refs = list(refs)
        q_ref, k_refs, v_ref = refs[0], refs[1:1 + nkp], refs[1 + nkp]
        p0 = 2 + nkp
        sel_ref = refs[p0] if has_sel else None
        p0 += has_sel
        o_ref, l_ref, do_ref = refs[p0:p0 + 3]
        p0 += 3
        in_ref = refs[p0] if has_in else None
        dq_ref, dr_ref = refs[-2:]
        h, i = pl.program_id(0), pl.program_id(1)
        qv = q_ref[...].astype(BF16)
        dov = do_ref[...]
        dvec = jnp.sum(dov * o_ref[...], axis=1, keepdims=True)
        dr_ref[0, 0] = _to_row(dvec)
        dob = dov.astype(BF16)
        lse_v = l_ref[0]
        t = i * tq + lax.broadcasted_iota(jnp.int32, (tq, 1), 0)

        def chunk(k0, acc, diag):
            rows = pl.ds(k0, tk)
            kk = _load_keys(k_refs, rows)
            vv = v_ref[rows, :].astype(BF16)
            n = k0 + lax.broadcasted_iota(jnp.int32, (1, tk), 1)
            selx = _expand_rows(sel_ref[...], k0, tk) if has_sel else None
            p, _ = _scores(cfg, _nt(qv, kk), t, n, h, selx, diag, lse_v)
            ds = p * (_nt(dob, vv) - dvec) * cfg.scale
            return acc + _nn(ds.astype(BF16), kk)

        acc = in_ref[...] if has_in else jnp.zeros((tq, dk), F32)
        if cfg.causal:
            full = (i * tq) // tk
            acc = lax.fori_loop(0, full, lambda c, a: chunk(pl.multiple_of(c * tk, tk), a, False), acc)
            acc = chunk(pl.multiple_of(full * tk, tk), acc, True)
        elif cfg.mode == "win":
            acc = chunk(pl.multiple_of(jnp.maximum(i * tq - WIN, 0), tq), acc, True)
        else:
            acc = chunk(0, acc, True)
        dq_ref[...] = acc

    qs = pl.BlockSpec((tq, dk), lambda h, i: (i, h))
    ins = [pl.BlockSpec((tq, q.width), lambda h, i: (i, q.col(h)))]
    ins += [pl.BlockSpec((cfg.sk, p.width), lambda h, i, p=p: (0, p.col(h))) for p in ks]
    ins += [pl.BlockSpec((cfg.sk, HEAD_V), lambda h, i: (0, v.col(h)))]
    args = [q.arr] + [p.arr for p in ks] + [v.arr]
    if has_sel:
        ins.append(pl.BlockSpec((tq, LANE), lambda h, i: (i, 0)))
        args.append(sel)
    ins += [pl.BlockSpec((tq, HEAD_V), lambda h, i: (i, o.col(h))),
            pl.BlockSpec((1, tq, 1), lambda h, i: (h, i, 0)),
            pl.BlockSpec((tq, HEAD_V), lambda h, i: (i, do.col(h)))]
    args += [o.arr, lse, do.arr]
    if has_in:
        ins.append(qs)
        args.append(dq_in)
    return _pcall(
        kern, name=name, grid=(cfg.h, cfg.nq), in_specs=ins,
        out_specs=[qs, pl.BlockSpec((1, 1, 1, tq), lambda h, i: (h, i, 0, 0))],
        out_shape=[jax.ShapeDtypeStruct((s, cfg.h * dk), F32),
                   jax.ShapeDtypeStruct((cfg.h, cfg.nq, 1, tq), F32)],
        compiler_params=pltpu.CompilerParams(dimension_semantics=("parallel", "parallel")),
    )(*args)


def _attn_dkv(cfg, q, ks, v, selt, lse_r, d_r, do, name):
    s, tq, tkb, dk = cfg.s, cfg.tq, cfg.tkb, cfg.dk
    nq = cfg.nq
    has_sel = selt is not None
    nkp = len(ks)
    outs = list(ks) + [v]

    def kern(*refs):
        k_refs, v_ref = refs[:nkp], refs[nkp]
        q_ref, do_ref, lr_ref, dr_ref = refs[nkp + 1:nkp + 5]
        st_ref = refs[nkp + 5] if has_sel else None
        out_refs = refs[-(nkp + 1):]
        j, h = pl.program_id(0), pl.program_id(1)
        k0 = j * tkb
        kk = _load_keys(k_refs, slice(None))
        vv = v_ref[...].astype(BF16)
        n = k0 + lax.broadcasted_iota(jnp.int32, (tkb, 1), 0)

        def qblock(i, carry, diag):
            dk_acc, dv_acc = carry
            r0 = pl.multiple_of(i * tq, tq)
            qi = q_ref[pl.ds(r0, tq), :].astype(BF16)
            doi = do_ref[pl.ds(r0, tq), :].astype(BF16)
            t = i * tq + lax.broadcasted_iota(jnp.int32, (1, tq), 1)
            selx = _expand_cols(st_ref[i], k0, tkb) if has_sel else None
            pt, _ = _scores(cfg, _nt(kk, qi), t, n, h, selx, diag, lr_ref[0, i])
            dv_acc = dv_acc + _nn(pt.astype(BF16), doi)
            dst = pt * (_nt(vv, doi) - dr_ref[0, i]) * cfg.scale
            dk_acc = dk_acc + _nn(dst.astype(BF16), qi)
            return dk_acc, dv_acc

        carry = (jnp.zeros((tkb, dk), F32), jnp.zeros((tkb, HEAD_V), F32))
        masked = lambda i, cr: qblock(i, cr, True)
        if cfg.causal:
            first, past = k0 // tq, (k0 + tkb) // tq
            carry = lax.fori_loop(first, past, masked, carry)
            carry = lax.fori_loop(past, nq, lambda i, cr: qblock(i, cr, False), carry)
        elif cfg.mode == "win":
            carry = lax.fori_loop(k0 // tq, jnp.minimum((k0 + tkb + WIN - 2) // tq + 1, nq), masked, carry)
        else:
            carry = lax.fori_loop(0, nq, masked, carry)
        dk_acc, dv_acc = carry
        vals, off = [], 0
        for p in ks:
            vals.append(dk_acc[:, off:off + p.width])
            off += p.width
        vals.append(dv_acc)
        for src, ref, val in zip(outs, out_refs, vals):
            if src.per_head:
                ref[...] = val
            else:
                @pl.when(h == 0)
                def _(ref=ref, val=val):
                    ref[...] = val

                @pl.when(h > 0)
                def _(ref=ref, val=val):
                    ref[...] += val

    rowv = pl.BlockSpec((1, nq, 1, tq), lambda j, h: (h, 0, 0, 0))
    ins = [pl.BlockSpec((tkb, p.width), lambda j, h, p=p: (j, p.col(h))) for p in ks]
    ins += [pl.BlockSpec((tkb, HEAD_V), lambda j, h: (j, v.col(h))),
            pl.BlockSpec((s, q.width), lambda j, h: (0, q.col(h))),
            pl.BlockSpec((s, HEAD_V), lambda j, h: (0, do.col(h))), rowv, rowv]
    args = [p.arr for p in ks] + [v.arr, q.arr, do.arr, lse_r, d_r]
    if has_sel:
        ins.append(pl.BlockSpec((nq, LANE, tq), lambda j, h: (0, 0, 0)))
        args.append(selt)
    out_specs = [pl.BlockSpec((tkb, p.width), lambda j, h, p=p: (j, h if p.per_head else 0)) for p in outs]
    out_shape = [jax.ShapeDtypeStruct((cfg.sk, (cfg.h if p.per_head else 1) * p.width), F32) for p in outs]
    return _pcall(
        kern, name=name, grid=(cfg.sk // tkb, cfg.h), in_specs=ins, out_specs=out_specs, out_shape=out_shape,
        compiler_params=pltpu.CompilerParams(dimension_semantics=("parallel", "arbitrary")),
    )(*args)


def _attn_bwd(cfg, q, ks, v, sel, selt, o, lse, lse_r, do, dq_in, name):
    dq, d_r = _attn_dq(cfg, q, ks, v, sel, o, lse, do, dq_in, name + "_dq")
    res = _attn_dkv(cfg, q, ks, v, selt, lse_r, d_r, do, name + "_dkv")
    return dq, res[:-1], res[-1]


def _select(cfg, q, k_cmp, overlap):
    s, tq, sk = cfg.s, cfg.tq, cfg.sk
    n_s = s // SLC_LEN
    top_n = min(SLC_TOPN, n_s)

    def kern(q_ref, k_ref, ov_ref, sel_ref, selt_ref):
        i = pl.program_id(0)
        t = i * tq + lax.broadcasted_iota(jnp.int32, (tq, 1), 0)
        n = lax.broadcasted_iota(jnp.int32, (1, sk), 1)
        kk = k_ref[...]
        imp = jnp.zeros((tq, LANE), F32)
        for h in range(NSA_HEADS):
            sc, mask = _scores(cfg, _nt(q_ref[:, h * 256:(h + 1) * 256], kk), t, n, h, None, True)
            m = jnp.max(sc, axis=1, keepdims=True)
            e = jnp.where(mask, jnp.exp(sc - m), 0.0)
            p = e / (jnp.sum(e, axis=1, keepdims=True) + 1e-20)
            imp = imp + _nn(p.astype(BF16), ov_ref[...])
        j = lax.broadcasted_iota(jnp.int32, (tq, LANE), 1)
        cur = t >> 6
        forced = (j == 0) | (j == cur) | (j == cur - 1)
        imp = jnp.where(forced, 1e9, imp)
        imp = jnp.where(j > cur, -1e9, imp)
        imp = jnp.where(j >= n_s, -3e38, imp)

        def pick(_, carry):
            work, chosen = carry
            mx = jnp.max(work, axis=1, keepdims=True)
            first = jnp.min(jnp.where(work == mx, j, LANE), axis=1, keepdims=True)
            hit = j == first
            return jnp.where(hit, -3e38, work), jnp.where(hit, 1.0, chosen)

        _, chosen = lax.fori_loop(0, top_n, pick, (imp, jnp.zeros((tq, LANE), F32)))
        chosen = jnp.where(j <= cur, chosen, 0.0)
        sel_ref[...] = chosen
        selt_ref[0] = jnp.transpose(chosen)

    return _pcall(
        kern, name="nsa_select", grid=(cfg.nq,),
        in_specs=[pl.BlockSpec((tq, NSA_HEADS * 256), lambda i: (i, q.col0)),
                  pl.BlockSpec((sk, 256), lambda i: (0, 0)), pl.BlockSpec((sk, LANE), lambda i: (0, 0))],
        out_specs=[pl.BlockSpec((tq, LANE), lambda i: (i, 0)), pl.BlockSpec((1, LANE, tq), lambda i: (i, 0, 0))],
        out_shape=[jax.ShapeDtypeStruct((s, LANE), F32), jax.ShapeDtypeStruct((cfg.nq, LANE, tq), F32)],
    )(q.arr, k_cmp, overlap)


def _silu_grad(pre):
    sg = _sigmoid(pre)
    return sg * (1.0 + pre * (1.0 - sg))


def _compress_fwd(a_lo, a_hi, pe_lo, pe_hi, w1_lo, w1_hi, w2, name):
    n, dp = a_lo.shape[0], w2.shape[1]

    def kern(alo, ahi, plo, phi, w1l, w1h, w2r, out_ref, pre_ref):
        xl = (alo[...] + plo[...]).astype(BF16)
        xh = (ahi[...] + phi[...]).astype(BF16)
        pre = _nn(xl, w1l[...]) + _nn(xh, w1h[...])
        act = pre * _sigmoid(pre)
        out_ref[...] = _nn(act.astype(BF16), w2r[...]).astype(BF16)
        pre_ref[...] = pre

    return _pcall(kern, name=name,
                  out_shape=[jax.ShapeDtypeStruct((n, dp), BF16), jax.ShapeDtypeStruct((n, dp), F32)],
                  )(a_lo, a_hi, pe_lo, pe_hi, w1_lo, w1_hi, w2)


def _compress_bwd(a_lo, a_hi, pe_lo, pe_hi, w1_lo, w1_hi, w2, pre, pre_sh, dout, dout_sh, name):
    n, ln = a_lo.shape
    dp = w2.shape[1]

    def kern(alo, ahi, plo, phi, w1l, w1h, w2r, pre_ref, presh_ref, do_ref, dosh_ref,
             da_ref, dpl_ref, dph_ref, dw1l_ref, dw1h_ref, dw2_ref):
        prev = pre_ref[...]
        act = prev * _sigmoid(prev)
        dob = do_ref[...].astype(BF16)
        w2v = w2r[...]
        dpre = (_nt(dob, w2v) * _silu_grad(prev)).astype(BF16)
        dpre_sh = (_nt(dosh_ref[...].astype(BF16), w2v) * _silu_grad(presh_ref[...])).astype(BF16)
        dw2_ref[...] = _nn(act.T.astype(BF16), dob)
        xl = alo[...] + plo[...]
        xh = ahi[...] + phi[...]
        dw1l_ref[...] = _nn(xl.T.astype(BF16), dpre)
        dw1h_ref[...] = _nn(xh.T.astype(BF16), dpre)
        dal = _nt(dpre, w1l[...])
        dah_sh = _nt(dpre_sh, w1h[...])
        da_ref[...] = dal + dah_sh
        dpl_ref[...] = jnp.sum(dal, axis=0, keepdims=True)
        dph_ref[...] = jnp.sum(dah_sh, axis=0, keepdims=True)

    return _pcall(
        kern, name=name,
        out_shape=[jax.ShapeDtypeStruct((n, ln), F32), jax.ShapeDtypeStruct((1, ln), F32),
                   jax.ShapeDtypeStruct((1, ln), F32), jax.ShapeDtypeStruct((ln, dp), F32),
                   jax.ShapeDtypeStruct((ln, dp), F32), jax.ShapeDtypeStruct((dp, dp), F32)],
    )(a_lo, a_hi, pe_lo, pe_hi, w1_lo, w1_hi, w2, pre, pre_sh, dout, dout_sh)


def _nsa_combine(o_cmp, o_slc, o_win, gl):
    s, w = o_cmp.shape
    tr = _tile(s, 512)

    def kern(a_ref, b_ref, c_ref, g_ref, o_ref):
        g = _sigmoid(g_ref[...])
        for h in range(NSA_HEADS):
            cs = slice(h * HEAD_V, (h + 1) * HEAD_V)
            o_ref[:, cs] = (g[:, 3 * h:3 * h + 1] * a_ref[:, cs] + g[:, 3 * h + 1:3 * h + 2] * b_ref[:, cs]
                            + g[:, 3 * h + 2:3 * h + 3] * c_ref[:, cs])

    row = pl.BlockSpec((tr, w), lambda i: (i, 0))
    return _pcall(kern, name="nsa_combine", grid=(s // tr,),
                  in_specs=[row, row, row, pl.BlockSpec((tr, LANE), lambda i: (i, gl.col0))], out_specs=row,
                  out_shape=jax.ShapeDtypeStruct((s, w), F32))(o_cmp, o_slc, o_win, gl.arr)


def _nsa_combine_bwd(do_cat, o_cmp, o_slc, o_win, gl):
    s, w = o_cmp.shape
    tr = _tile(s, 512)

    def kern(d_ref, a_ref, b_ref, c_ref, g_ref, da_ref, db_ref, dc_ref, dg_ref):
        g = _sigmoid(g_ref[...])
        lane = lax.broadcasted_iota(jnp.int32, (tr, LANE), 1)
        dgl = jnp.zeros((tr, LANE), F32)
        for h in range(NSA_HEADS):
            cs = slice(h * HEAD_V, (h + 1) * HEAD_V)
            dv = d_ref[:, cs]
            for b, (src, dst) in enumerate(((a_ref, da_ref), (b_ref, db_ref), (c_ref, dc_ref))):
                gate = g[:, 3 * h + b:3 * h + b + 1]
                dst[:, cs] = gate * dv
                dgate = jnp.sum(dv * src[:, cs], axis=1, keepdims=True)
                dgl = jnp.where(lane == 3 * h + b, dgate * gate * (1.0 - gate), dgl)
        dg_ref[...] = dgl

    row = pl.BlockSpec((tr, w), lambda i: (i, 0))
    tab = pl.BlockSpec((tr, LANE), lambda i: (i, 0))
    return _pcall(kern, name="nsa_combine_bwd", grid=(s // tr,),
                  in_specs=[pl.BlockSpec((tr, w), lambda i: (i, 2)), row, row, row,
                            pl.BlockSpec((tr, LANE), lambda i: (i, gl.col0))],
                  out_specs=[row, row, row, tab],
                  out_shape=[jax.ShapeDtypeStruct((s, w), F32)] * 3 + [jax.ShapeDtypeStruct((s, LANE), F32)],
                  )(do_cat, o_cmp, o_slc, o_win, gl.arr)


def _gate_fwd(o_mla, o_nsa, o_mem, hp):
    s = o_mla.shape[0]
    tr = _tile(s, 256)

    def kern(a_ref, b_ref, c_ref, z_ref, u_ref):
        z = z_ref[...]
        sz = z * _sigmoid(z)
        u_ref[:, 0:1024] = (a_ref[...] * sz[:, 0:1024]).astype(BF16)
        u_ref[:, 1024:1536] = (b_ref[...] * sz[:, 1024:1536]).astype(BF16)
        u_ref[:, 1536:2048] = (c_ref[...] * sz[:, 1536:2048]).astype(BF16)

    return _pcall(
        kern, name="gate_fwd", grid=(s // tr,),
        in_specs=[pl.BlockSpec((tr, 1024), lambda i: (i, 0)), pl.BlockSpec((tr, 512), lambda i: (i, 0)),
                  pl.BlockSpec((tr, 512), lambda i: (i, 0)), pl.BlockSpec((tr, 2048), lambda i: (i, 2))],
        out_specs=pl.BlockSpec((tr, 2048), lambda i: (i, 0)),
        out_shape=jax.ShapeDtypeStruct((s, 2048), BF16))(o_mla, o_nsa, o_mem, hp)


def _gate_bwd(du, o_mla, o_nsa, o_mem, hp):
    s = du.shape[0]
    tr = _tile(s, 256)

    def kern(d_ref, a_ref, b_ref, c_ref, z_ref, do_ref, dz_ref):
        z = z_ref[...]
        sg = _sigmoid(z)
        sz = z * sg
        dsz = sg * (1.0 + z * (1.0 - sg))
        d = d_ref[...]
        do_ref[...] = d * sz
        dz_ref[:, 0:1024] = d[:, 0:1024] * a_ref[...] * dsz[:, 0:1024]
        dz_ref[:, 1024:1536] = d[:, 1024:1536] * b_ref[...] * dsz[:, 1024:1536]
        dz_ref[:, 1536:2048] = d[:, 1536:2048] * c_ref[...] * dsz[:, 1536:2048]

    wide = pl.BlockSpec((tr, 2048), lambda i: (i, 0))
    return _pcall(
        kern, name="gate_bwd", grid=(s // tr,),
        in_specs=[wide, pl.BlockSpec((tr, 1024), lambda i: (i, 0)), pl.BlockSpec((tr, 512), lambda i: (i, 0)),
                  pl.BlockSpec((tr, 512), lambda i: (i, 0)), pl.BlockSpec((tr, 2048), lambda i: (i, 2))],
        out_specs=[wide, wide],
        out_shape=[jax.ShapeDtypeStruct((s, 2048), F32)] * 2)(du, o_mla, o_nsa, o_mem, hp)


def _sum_slots(buf, name):
    n, rows, cols = buf.shape
    tr = _row_tile(rows, cols * n)

    def kern(b_ref, o_ref):
        acc = b_ref[0].astype(F32)
        for i in range(1, n):
            acc = acc + b_ref[i].astype(F32)
        o_ref[...] = acc

    return _pcall(kern, name=name, grid=(rows // tr,),
                  in_specs=[pl.BlockSpec((n, tr, cols), lambda i: (0, i, 0))],
                  out_specs=pl.BlockSpec((tr, cols), lambda i: (i, 0)),
                  out_shape=jax.ShapeDtypeStruct((rows, cols), F32))(buf)


def _pair_sum(g4, theirs, core, name):
    n, rows, cols = g4.shape
    half = rows // 2
    tr = _row_tile(half, cols)
    nb = half // tr

    def kern(c_ref, a_ref, b_ref, o_ref):
        o_ref[...] = (a_ref[...] + b_ref[...]).astype(BF16)

    blk = (1, tr, cols)
    grid_spec = pltpu.PrefetchScalarGridSpec(
        num_scalar_prefetch=1, grid=(n, nb),
        in_specs=[pl.BlockSpec(blk, lambda s, i, c: (s, c[0] * nb + i, 0)),
                  pl.BlockSpec(blk, lambda s, i, c: (s, i, 0))],
        out_specs=pl.BlockSpec(blk, lambda s, i, c: (s, i, 0)))
    return _pcall(kern, name=name, grid_spec=grid_spec,
                  out_shape=jax.ShapeDtypeStruct((n, half, cols), BF16))(core, g4, theirs)


def _adamw(w, g, m, v, name):
    rows, cols = w.shape
    tr = _row_tile(rows, cols * 4)
    bc1 = 1.0 - ADAM_B1 ** ADAM_STEP
    bc2 = 1.0 - ADAM_B2 ** ADAM_STEP

    def kern(w_ref, g_ref, m_ref, v_ref, d_ref, mo_ref, vo_ref):
        gv = g_ref[...]
        mn = ADAM_B1 * m_ref[...] + (1.0 - ADAM_B1) * gv
        vn = ADAM_B2 * v_ref[...] + (1.0 - ADAM_B2) * (gv * gv)
        d_ref[...] = -ADAM_LR * ((mn / bc1) / (jnp.sqrt(vn / bc2) + ADAM_EPS) + ADAM_WD * w_ref[...])
        mo_ref[...] = mn
        vo_ref[...] = vn

    row = pl.BlockSpec((tr, cols), lambda i: (i, 0))
    return _pcall(kern, name=name, grid=(rows // tr,), in_specs=[row] * 4, out_specs=[row] * 3,
                  out_shape=[jax.ShapeDtypeStruct((rows, cols), F32)] * 3)(w, g, m, v)


ANY = pl.BlockSpec(memory_space=pl.ANY)


def _place():
    x, y, c = lax.axis_index("x"), lax.axis_index("y"), lax.axis_index("c")
    chips = [(1 - x, y), (x, 1 - y), (1 - x, 1 - y)]
    return x, y, c, chips


def _remote(src, dst, send_sem, recv_sem, to):
    return pltpu.make_async_remote_copy(src_ref=src, dst_ref=dst, send_sem=send_sem, recv_sem=recv_sem,
                                        device_id=to, device_id_type=MESH)


def _gather_shards(ws):
    nw = len(ws)
    halves = [w.shape[0] // 2 for w in ws]

    def body(*refs):
        w_refs, out_refs = refs[:nw], refs[nw:2 * nw]
        send_sems, recv_sems, local_sems = refs[2 * nw:]
        x, y, c, chips = _place()
        me = 2 * x + y
        sibling = (x, y, 1 - c)

        def part(i, slot, core):
            return out_refs[i].at[slot, pl.ds(core * halves[i], halves[i]), :]

        def copy(sem, src, dst, to):
            return _remote(src, dst, send_sems.at[sem], recv_sems.at[sem], to)

        mine = [pltpu.make_async_copy(w_refs[i], out_refs[i].at[me], local_sems.at[i]) for i in range(nw)]
        for cp in mine:
            cp.start()
        first = [copy(j * nw + i, w_refs[i].at[pl.ds(c * halves[i], halves[i]), :], part(i, me, c), (*chip, c))
                 for j, chip in enumerate(chips) for i in range(nw)]
        for cp in first:
            cp.start()
        passed = []
        for j, (cx, cy) in enumerate(chips):
            slot = 2 * cx + cy
            for i in range(nw):
                copy(j * nw + i, part(i, slot, c), part(i, slot, c), (x, y, c)).wait_recv()
                fwd = copy((3 + j) * nw + i, part(i, slot, c), part(i, slot, c), sibling)
                fwd.start()
                passed.append(fwd)
        for j, (cx, cy) in enumerate(chips):
            slot = 2 * cx + cy
            for i in range(nw):
                copy((3 + j) * nw + i, part(i, slot, 1 - c), part(i, slot, 1 - c), (x, y, c)).wait_recv()
        for cp in first + passed:
            cp.wait_send()
        for cp in mine:
            cp.wait()

    return _pcall(
        body, name="gather_shards", in_specs=[ANY] * nw, out_specs=[ANY] * nw,
        out_shape=[jax.ShapeDtypeStruct((4,) + w.shape, w.dtype) for w in ws],
        scratch_shapes=[pltpu.SemaphoreType.DMA((6 * nw,)), pltpu.SemaphoreType.DMA((6 * nw,)),
                        pltpu.SemaphoreType.DMA((nw,))],
    )(*ws)


def _pair_exchange(gs):
    nw = len(gs)

    def body(*refs):
        g_refs, out_refs = refs[:nw], refs[nw:2 * nw]
        send_sems, recv_sems = refs[2 * nw:]
        x, y, c, _ = _place()
        cps = []
        for i in range(nw):
            half = gs[i].shape[1] // 2
            cp = _remote(g_refs[i].at[:, pl.ds((1 - c) * half, half), :], out_refs[i],
                         send_sems.at[i], recv_sems.at[i], (x, y, 1 - c))
            cp.start()
            cps.append(cp)
        for cp in cps:
            cp.wait()

    return _pcall(body, name="pair_exchange", in_specs=[ANY] * nw, out_specs=[ANY] * nw,
                  out_shape=[jax.ShapeDtypeStruct((4, g.shape[1] // 2, g.shape[2]), g.dtype) for g in gs],
                  scratch_shapes=[pltpu.SemaphoreType.DMA((nw,)), pltpu.SemaphoreType.DMA((nw,))])(*gs)


def _chip_exchange(ps):
    nw = len(ps)

    def body(*refs):
        p_refs, out_refs = refs[:nw], refs[nw:2 * nw]
        send_sems, recv_sems, local_sems = refs[2 * nw:]
        x, y, c, chips = _place()
        me = 2 * x + y
        mine = [pltpu.make_async_copy(p_refs[i].at[me], out_refs[i].at[me], local_sems.at[i]) for i in range(nw)]
        for cp in mine:
            cp.start()
        sends = []
        for j, (cx, cy) in enumerate(chips):
            for i in range(nw):
                cp = _remote(p_refs[i].at[2 * cx + cy], out_refs[i].at[me], send_sems.at[j * nw + i],
                             recv_sems.at[j * nw + i], (cx, cy, c))
                cp.start()
                sends.append(cp)
        for j, (cx, cy) in enumerate(chips):
            slot = 2 * cx + cy
            for i in range(nw):
                _remote(out_refs[i].at[slot], out_refs[i].at[slot], send_sems.at[j * nw + i],
                        recv_sems.at[j * nw + i], (x, y, c)).wait_recv()
        for cp in sends:
            cp.wait_send()
        for cp in mine:
            cp.wait()

    return _pcall(body, name="chip_exchange", in_specs=[ANY] * nw, out_specs=[ANY] * nw,
                  out_shape=[jax.ShapeDtypeStruct(p.shape, p.dtype) for p in ps],
                  scratch_shapes=[pltpu.SemaphoreType.DMA((3 * nw,)), pltpu.SemaphoreType.DMA((3 * nw,)),
                                  pltpu.SemaphoreType.DMA((nw,))])(*ps)


def _half_exchange(ts):
    nw = len(ts)

    def body(*refs):
        t_refs, out_refs = refs[:nw], refs[nw:2 * nw]
        send_sems, recv_sems, local_sems = refs[2 * nw:]
        x, y, c, _ = _place()
        mine, sends = [], []
        for i in range(nw):
            half = ts[i].shape[0]
            own = out_refs[i].at[pl.ds(c * half, half), :]
            cp = pltpu.make_async_copy(t_refs[i], own, local_sems.at[i])
            cp.start()
            mine.append(cp)
            cp = _remote(t_refs[i], own, send_sems.at[i], recv_sems.at[i], (x, y, 1 - c))
            cp.start()
            sends.append(cp)
        for i in range(nw):
            half = ts[i].shape[0]
            other = out_refs[i].at[pl.ds((1 - c) * half, half), :]
            _remote(t_refs[i], other, send_sems.at[i], recv_sems.at[i], (x, y, c)).wait_recv()
        for cp in sends:
            cp.wait_send()
        for cp in mine:
            cp.wait()

    return _pcall(body, name="half_exchange", in_specs=[ANY] * nw, out_specs=[ANY] * nw,
                  out_shape=[jax.ShapeDtypeStruct((2 * t.shape[0], t.shape[1]), t.dtype) for t in ts],
                  scratch_shapes=[pltpu.SemaphoreType.DMA((nw,)), pltpu.SemaphoreType.DMA((nw,)),
                                  pltpu.SemaphoreType.DMA((nw,))])(*ts)


def _gather_all(v):
    rows, cols = v.shape

    def body(v_ref, out_ref, send_sems, recv_sems, local_sem):
        x, y, c, _ = _place()
        me = 4 * x + 2 * y + c
        mine = pltpu.make_async_copy(v_ref, out_ref.at[me], local_sem)
        mine.start()
        sends = []
        for d in range(1, 8):
            peer = (x ^ (d >> 2), y ^ ((d >> 1) & 1), c ^ (d & 1))
            cp = _remote(v_ref, out_ref.at[me], send_sems.at[d - 1], recv_sems.at[d - 1], peer)
            cp.start()
            sends.append(cp)
        for d in range(1, 8):
            slot = 4 * (x ^ (d >> 2)) + 2 * (y ^ ((d >> 1) & 1)) + (c ^ (d & 1))
            _remote(v_ref, out_ref.at[slot], send_sems.at[d - 1], recv_sems.at[d - 1], (x, y, c)).wait_recv()
        for cp in sends:
            cp.wait_send()
        mine.wait()

    return _pcall(body, name="gather_all", in_specs=[ANY], out_specs=ANY,
                  out_shape=jax.ShapeDtypeStruct((8, rows, cols), v.dtype),
                  scratch_shapes=[pltpu.SemaphoreType.DMA((7,)), pltpu.SemaphoreType.DMA((7,)),
                                  pltpu.SemaphoreType.DMA])(v)


def _pad_cols(a, width):
    return a if a.shape[1] == width else jnp.pad(a, ((0, 0), (0, width - a.shape[1])))


def _w_in_padded(w):
    def seg(name, width=None):
        o, n = ORIG[name]
        return _pad_cols(w[:, o:o + n], width or n)

    qn = w[:, ORIG["q_nsa"][0]:ORIG["q_nsa"][0] + 768].reshape(-1, NSA_HEADS, NSA_DK)
    qn = jnp.pad(qn, ((0, 0), (0, 0), (0, 256 - NSA_DK))).reshape(-1, NSA_HEADS * 256)
    kr = seg("k_rope")
    zeros = jnp.zeros((w.shape[0], PAD["z"] - (PAD["q_mem"] + 512)), w.dtype)
    return jnp.concatenate(
        [seg("c_q"), seg("c_kv"), qn, seg("k_c", 256), seg("k_s", 256), seg("k_w", 256), kr, kr,
         seg("v_c"), seg("v_s"), seg("v_w"), seg("g_nsa", LANE), seg("q_mem"), zeros,
         seg("z_mla"), seg("z_nsa"), seg("z_mem")], axis=1)


def _w_in_unpadded(g):
    def seg(name, n):
        return g[:, PAD[name]:PAD[name] + n]

    qn = g[:, PAD["q_nsa"]:PAD["q_nsa"] + 1024].reshape(-1, NSA_HEADS, 256)[:, :, :NSA_DK].reshape(-1, 768)
    z = PAD["z"]
    return jnp.concatenate(
        [seg("c_q", 512), seg("c_kv", 512), seg("k_rope", 64), g[:, z:z + 1024], qn, seg("k_c", 192),
         seg("v_c", 128), seg("k_s", 192), seg("v_s", 128), seg("k_w", 192), seg("v_w", 128),
         seg("g_nsa", 12), g[:, z + 1024:z + 1536], seg("q_mem", 512), g[:, z + 1536:z + 2048]], axis=1)


def _rope_tables(s):
    pos = jnp.arange(s, dtype=F32)
    inv_freq = ROPE_THETA ** (-jnp.arange(0, 64, 2, dtype=F32) / 64)
    ang = pos[:, None] * inv_freq[None, :]
    cos, sin = jnp.cos(ang), jnp.sin(ang)
    z = jnp.zeros((s, 64), F32)
    return jnp.concatenate([cos, cos, z], axis=1), jnp.concatenate([-sin, sin, z], axis=1)


def _overlap_table(s):
    n_c, n_s = s // CMP_STRIDE, s // SLC_LEN
    c0 = np.arange(n_c)[:, None] * CMP_STRIDE
    s0 = np.arange(LANE)[None, :] * SLC_LEN
    ov = (c0 < s0 + SLC_LEN) & (c0 + CMP_LEN > s0) & (np.arange(n_c)[:, None] < n_c - 1) & (np.arange(LANE)[None, :] < n_s)
    return jnp.asarray(ov.astype(np.float32), dtype=BF16)


def _shift_down(a):
    return jnp.concatenate([jnp.zeros((8, a.shape[1]), a.dtype), a], axis=0)[7:7 + a.shape[0]]


def _shift_up(a):
    return jnp.concatenate([a, jnp.zeros((8, a.shape[1]), a.dtype)], axis=0)[1:1 + a.shape[0]]


def _local_step(x, mem, target, w):
    s = x.shape[0]
    cs, sn = _rope_tables(s)
    t_ = jnp.transpose

    w_in_p = _w_in_padded(w["w_in"])
    xn, rstd_x = _rms_fwd(_Src(x, D_MODEL), w["norm_g"], "norm_x")
    hp, hpb = _mm(xn, w_in_p, "in_proj", second_dtype=BF16)

    w_uq3 = w["w_uq"].reshape(512, MLA_HEADS, 192)
    w_uq_p = jnp.concatenate([w_uq3, w_uq3[:, :, 128:]], axis=2).reshape(512, MLA_HEADS * 256)
    w_ukv_p = t_(w["w_ukv"].reshape(512, MLA_HEADS, 2, 128), (0, 2, 1, 3)).reshape(512, 2048)
    c_q, c_kv = _Src(hp, 512, 0), _Src(hp, 512, 1)
    cqn, rstd_q = _rms_fwd(c_q, w["q_norm_g"], "norm_q")
    ckvn, rstd_kv = _rms_fwd(c_kv, w["kv_norm_g"], "norm_kv")
    q_lin = _mm(cqn, w_uq_p, "mla_q_proj")
    kvb = _mm(ckvn, w_ukv_p, "mla_kv_proj", out_dtype=BF16)
    q_mla = _rope_fwd(_Src(q_lin, MLA_HEADS * 256), cs, sn, MLA_HEADS, 256, LANE, "rope_q")
    k_pe = _rope_fwd(_Src(hp, LANE, PAD["k_rope"] // LANE), cs, sn, 1, LANE, 0, "rope_k")
    mla = _Attn("mla", s, s, MLA_HEADS, 256)
    mla_q, mla_v = _Src(q_mla, 256), _Src(kvb, LANE, MLA_HEADS)
    mla_k = [_Src(kvb, LANE), _Src(k_pe, LANE, 0, False)]
    o_mla, l_mla, lr_mla = _attn_fwd(mla, mla_q, mla_k, mla_v, None, "mla_fwd")

    sk = s // CMP_STRIDE
    pe_k, pe_v = w["cmp_pe_k"], w["cmp_pe_v"]
    w1k = _pad_cols(w["cmp_w1k"], 256)
    w2k = jnp.pad(w["cmp_w2k"], ((0, 64), (0, 64))).astype(BF16)
    w1v, w2v = w["cmp_w1v"], w["cmp_w2v"].astype(BF16)
    half_k, half_v = CMP_STRIDE * NSA_DK, CMP_STRIDE * HEAD_V
    ak = hp[:, PAD["k_c"]:PAD["k_c"] + NSA_DK].reshape(sk, half_k)
    av = hp[:, PAD["v_c"]:PAD["v_c"] + HEAD_V].reshape(sk, half_v)
    ck_args = (ak, _shift_up(ak), pe_k[:CMP_STRIDE].reshape(1, half_k), pe_k[CMP_STRIDE:].reshape(1, half_k),
               w1k[:half_k], w1k[half_k:], w2k)
    cv_args = (av, _shift_up(av), pe_v[:CMP_STRIDE].reshape(1, half_v), pe_v[CMP_STRIDE:].reshape(1, half_v),
               w1v[:half_v], w1v[half_v:], w2v)
    k_cmp, pre_k = _compress_fwd(*ck_args, "compress_k")
    v_cmp, pre_v = _compress_fwd(*cv_args, "compress_v")
    cmp_ = _Attn("cmp", s, sk, NSA_HEADS, 256)
    slc = _Attn("slc", s, s, NSA_HEADS, 256)
    win = _Attn("win", s, s, NSA_HEADS, 256)
    nsa_q = _Src(hpb, 256, PAD["q_nsa"] // 256)
    cmp_k, cmp_v = [_Src(k_cmp, 256, 0, False)], _Src(v_cmp, HEAD_V, 0, False)
    slc_k, slc_v = [_Src(hpb, 256, PAD["k_s"] // 256, False)], _Src(hpb, HEAD_V, PAD["v_s"] // HEAD_V, False)
    win_k, win_v = [_Src(hpb, 256, PAD["k_w"] // 256, False)], _Src(hpb, HEAD_V, PAD["v_w"] // HEAD_V, False)
    o_cmp, l_cmp, lr_cmp = _attn_fwd(cmp_, nsa_q, cmp_k, cmp_v, None, "cmp_fwd")
    sel, selt = _select(cmp_, _Src(hpb, NSA_HEADS * 256, PAD["q_nsa"] // (NSA_HEADS * 256)), k_cmp,
                        _overlap_table(s))
    o_slc, l_slc, lr_slc = _attn_fwd(slc, nsa_q, slc_k, slc_v, sel, "slc_fwd")
    o_win, l_win, lr_win = _attn_fwd(win, nsa_q, win_k, win_v, None, "win_fwd")
    gl = _Src(hp, LANE, PAD["g_nsa"] // LANE)
    o_nsa = _nsa_combine(o_cmp, o_slc, o_win, gl)

    mn, rstd_m = _rms_fwd(_Src(mem, D_MODEL), w["mem_norm_g"], "norm_mem")
    kvm = _mm(mn, w["w_mem_kv"], "mem_kv_proj", out_dtype=BF16)
    mem_ = _Attn("mem", s, mem.shape[0], MEM_HEADS, LANE)
    mem_q, mem_k, mem_v = _Src(hpb, LANE, PAD["q_mem"] // LANE), [_Src(kvm, LANE)], _Src(kvm, LANE, MEM_HEADS)
    o_mem, l_mem, lr_mem = _attn_fwd(mem_, mem_q, mem_k, mem_v, None, "mem_fwd")

    u = _gate_fwd(o_mla, o_nsa, o_mem, hp)
    proj = _mm(u, w["w_out"], "out_proj")
    dy, g_final, loss = _final_loss(x, proj, w["final_norm_g"].reshape(1, -1), target)

    g_w_out = _mm(u, dy, "out_proj_dw", mode="tn")
    du = _mm(dy, w["w_out"], "out_proj_dx", mode="nt")
    do_cat, dz = _gate_bwd(du, o_mla, o_nsa, o_mem, hp)

    dq_mem, (dk_mem,), dv_mem = _attn_bwd(mem_, mem_q, mem_k, mem_v, None, None, _Src(o_mem, HEAD_V), l_mem,
                                          lr_mem, _Src(do_cat, HEAD_V, 12), None, "mem_bwd")
    dkvm = jnp.concatenate([dk_mem, dv_mem], axis=1)
    g_w_mem_kv = _mm(mn, dkvm, "mem_kv_dw", mode="tn")
    dmn = _mm(dkvm, w["w_mem_kv"], "mem_kv_dx", mode="nt")
    _, g_mem_norm = _rms_bwd(_Src(mem, D_MODEL), w["mem_norm_g"], rstd_m, dmn, None, "norm_mem_bwd")

    do_cmp, do_slc, do_win, dgl = _nsa_combine_bwd(do_cat, o_cmp, o_slc, o_win, gl)
    dq_n, (dk_cmp,), dv_cmp = _attn_bwd(cmp_, nsa_q, cmp_k, cmp_v, None, None, _Src(o_cmp, HEAD_V), l_cmp,
                                        lr_cmp, _Src(do_cmp, HEAD_V), None, "cmp_bwd")
    dq_n, (dk_s,), dv_s = _attn_bwd(slc, nsa_q, slc_k, slc_v, sel, selt, _Src(o_slc, HEAD_V), l_slc, lr_slc,
                                    _Src(do_slc, HEAD_V), dq_n, "slc_bwd")
    dq_n, (dk_w,), dv_w = _attn_bwd(win, nsa_q, win_k, win_v, None, None, _Src(o_win, HEAD_V), l_win, lr_win,
                                    _Src(do_win, HEAD_V), dq_n, "win_bwd")
    dak, dpk_lo, dpk_hi, dw1k_lo, dw1k_hi, g_w2k = _compress_bwd(
        *ck_args, pre_k, _shift_down(pre_k), dk_cmp, _shift_down(dk_cmp), "compress_k_bwd")
    dav, dpv_lo, dpv_hi, dw1v_lo, dw1v_hi, g_w2v = _compress_bwd(
        *cv_args, pre_v, _shift_down(pre_v), dv_cmp, _shift_down(dv_cmp), "compress_v_bwd")
    g_pe_k = jnp.concatenate([dpk_lo.reshape(CMP_STRIDE, NSA_DK), dpk_hi.reshape(CMP_STRIDE, NSA_DK)], axis=0)
    g_pe_v = jnp.concatenate([dpv_lo.reshape(CMP_STRIDE, HEAD_V), dpv_hi.reshape(CMP_STRIDE, HEAD_V)], axis=0)
    g_w1k = jnp.concatenate([dw1k_lo, dw1k_hi], axis=0)[:, :NSA_DK]
    g_w1v = jnp.concatenate([dw1v_lo, dw1v_hi], axis=0)
    dk_c = _pad_cols(dak.reshape(s, NSA_DK), 256)
    dv_c = dav.reshape(s, HEAD_V)

    dq_m, (dk_nope, dk_pe), dv_m = _attn_bwd(mla, mla_q, mla_k, mla_v, None, None, _Src(o_mla, HEAD_V), l_mla,
                                             lr_mla, _Src(do_cat, HEAD_V), None, "mla_bwd")
    dq_lin = _rope_bwd_q(dq_m, cs, sn)
    dkv_lin, d_krope = _rope_bwd_k(dk_nope, dk_pe, dv_m, cs, sn)
    g_w_uq_p = _mm(cqn, dq_lin, "mla_q_dw", mode="tn")
    dcqn = _mm(dq_lin, w_uq_p, "mla_q_dx", mode="nt")
    g_w_ukv_p = _mm(ckvn, dkv_lin, "mla_kv_dw", mode="tn")
    dckvn = _mm(dkv_lin, w_ukv_p, "mla_kv_dx", mode="nt")
    dc_q, g_q_norm = _rms_bwd(c_q, w["q_norm_g"], rstd_q, dcqn, None, "norm_q_bwd")
    dc_kv, g_kv_norm = _rms_bwd(c_kv, w["kv_norm_g"], rstd_kv, dckvn, None, "norm_kv_bwd")
    g_w_uq = g_w_uq_p.reshape(512, MLA_HEADS, 256)[:, :, :192].reshape(512, MLA_HEADS * 192)
    g_w_ukv = t_(g_w_ukv_p.reshape(512, 2, MLA_HEADS, 128), (0, 2, 1, 3)).reshape(512, 2048)

    dhp = jnp.concatenate(
        [dc_q, dc_kv, dq_n, dk_c, dk_s, dk_w, d_krope, dv_c, dv_s, dv_w, dgl, dq_mem,
         jnp.zeros((s, PAD["z"] - (PAD["q_mem"] + 512)), F32), dz], axis=1)
    g_w_in = _w_in_unpadded(_mm(xn, dhp, "in_proj_dw", mode="tn"))
    dxn = _mm(dhp, w_in_p, "in_proj_dx", mode="nt")
    grad_x, g_norm = _rms_bwd(_Src(x, D_MODEL), w["norm_g"], rstd_x, dxn, dy, "norm_x_bwd")

    grads = dict(norm_g=g_norm, w_in=g_w_in, q_norm_g=g_q_norm, w_uq=g_w_uq, kv_norm_g=g_kv_norm,
                 w_ukv=g_w_ukv, cmp_pe_k=g_pe_k, cmp_pe_v=g_pe_v, cmp_w1k=g_w1k, cmp_w2k=g_w2k[:NSA_DK, :NSA_DK],
                 cmp_w1v=g_w1v, cmp_w2v=g_w2v, mem_norm_g=g_mem_norm, w_mem_kv=g_w_mem_kv, w_out=g_w_out,
                 final_norm_g=g_final.reshape(-1))
    return loss[0, 0], grad_x, grads


def kernel(x, mem, norm_g, w_in, q_norm_g, w_uq, kv_norm_g, w_ukv, cmp_pe_k, cmp_pe_v, cmp_w1k, cmp_w2k, cmp_w1v, cmp_w2v, mem_norm_g, w_mem_kv, w_out, final_norm_g, loss_target, m_norm_g, m_w_in, m_q_norm_g, m_w_uq, m_kv_norm_g, m_w_ukv, m_cmp_pe_k, m_cmp_pe_v, m_cmp_w1k, m_cmp_w2k, m_cmp_w1v, m_cmp_w2v, m_mem_norm_g, m_w_mem_kv, m_w_out, m_final_norm_g, v_norm_g, v_w_in, v_q_norm_g, v_w_uq, v_kv_norm_g, v_w_ukv, v_cmp_pe_k, v_cmp_pe_v, v_cmp_w1k, v_cmp_w2k, v_cmp_w1v, v_cmp_w2v, v_mem_norm_g, v_w_mem_kv, v_w_out, v_final_norm_g):
    args = dict(locals())
    wts = {n: args[n] for n in WEIGHTS}
    loc = {n: (a if n == "final_norm_g" else a[0]) for n, a in wts.items()}

    gathered = _gather_shards([loc[n].astype(BF16) for n in SHARDED])
    full = {n: loc[n].reshape(1, -1) if loc[n].ndim == 1 else loc[n] for n in REPLICATED}
    for n, gw in zip(SHARDED, gathered):
        if SHARD_AXIS[n] == 0:
            full[n] = gw.reshape(4 * gw.shape[1], gw.shape[2])
        else:
            full[n] = jnp.concatenate([gw[j] for j in range(4)], axis=1)

    loss, grad_x, g = _local_step(x[0], mem[0], loss_target[0], full)
    loss = lax.psum(loss, ("x", "y", "c"))

    def slots(n):
        a = g[n]
        if SHARD_AXIS[n] == 0:
            return a.reshape(4, a.shape[0] // 4, a.shape[1])
        width = a.shape[1] // 4
        return jnp.stack([a[:, j * width:(j + 1) * width] for j in range(4)])

    gs = [slots(n) for n in SHARDED]
    core = lax.axis_index("c").astype(jnp.int32).reshape(1)
    theirs = _pair_exchange(gs)
    pairs = [_pair_sum(a, b, core, "pair_sum_" + n) for n, a, b in zip(SHARDED, gs, theirs)]
    from_chips = _chip_exchange(pairs)
    g_sh = _half_exchange([_sum_slots(b, "chip_sum_" + n) for n, b in zip(SHARDED, from_chips)])

    n_rep = sum(int(np.prod(loc[n].shape)) for n in REPLICATED)
    rows_rep = -(-n_rep // (8 * LANE)) * 8

    def rep_pack(parts):
        flat = jnp.concatenate([p.reshape(-1) for p in parts])
        return jnp.pad(flat, (0, rows_rep * LANE - n_rep)).reshape(rows_rep, LANE)

    g_rep = _sum_slots(_gather_all(rep_pack([g[n] for n in REPLICATED])), "replica_sum")
    d_rp, m_rp, v_rp = _adamw(rep_pack([wts[n] for n in REPLICATED]), g_rep,
                              rep_pack([args["m_" + n] for n in REPLICATED]),
                              rep_pack([args["v_" + n] for n in REPLICATED]), "adamw_replicated")

    def rep_unpack(buf):
        flat, out, o = buf.reshape(-1), {}, 0
        for n in REPLICATED:
            size = int(np.prod(wts[n].shape))
            out[n] = flat[o:o + size].reshape(wts[n].shape)
            o += size
        return out

    outs = {k: rep_unpack(b) for k, b in (("g", g_rep), ("d", d_rp), ("m", m_rp), ("v", v_rp))}
    for n, gn in zip(SHARDED, g_sh):
        d, mo, vo = _adamw(loc[n], gn, args["m_" + n][0], args["v_" + n][0], "adamw_" + n)
        for k, a in (("g", gn), ("d", d), ("m", mo), ("v", vo)):
            outs[k][n] = a.reshape(wts[n].shape)

    return (loss, grad_x[None], *[outs["g"][n] for n in WEIGHTS], *[outs["d"][n] for n in WEIGHTS],
            *[outs["m"][n] for n in WEIGHTS], *[outs["v"][n] for n in WEIGHTS])
```

```python
from typing import NamedTuple

import numpy as np
import jax
import jax.numpy as jnp
from jax import lax
from jax.experimental import pallas as pl
from jax.experimental.pallas import tpu as pltpu

F32 = jnp.float32
BF16 = jnp.bfloat16
MESH = pl.DeviceIdType.MESH

D_MODEL = 2048
EPS = 1e-6
LANE = 128
HEAD_V = 128
MLA_HEADS = 8
NSA_HEADS = 4
MEM_HEADS = 4
NSA_DK = 192
CMP_STRIDE = 16
CMP_LEN = 32
SLC_LEN = 64
SLC_TOPN = 16
WIN = 512
NEG = -1e30
ROPE_THETA = 10000.0
BLOCK_BYTES = 2 << 20

ORIG = dict(c_q=(0, 512), c_kv=(512, 512), k_rope=(1024, 64), z_mla=(1088, 1024),
            q_nsa=(2112, 768), k_c=(2880, 192), v_c=(3072, 128), k_s=(3200, 192),
            v_s=(3392, 128), k_w=(3520, 192), v_w=(3712, 128), g_nsa=(3840, 12),
            z_nsa=(3852, 512), q_mem=(4364, 512), z_mem=(4876, 512))
PAD = dict(c_q=0, c_kv=512, q_nsa=1024, k_c=2048, k_s=2304, k_w=2560, k_rope=2816, v_c=2944,
           v_s=3072, v_w=3200, g_nsa=3328, q_mem=3456, z=4096)
D_PAD = 6144

ADAM_LR, ADAM_B1, ADAM_B2, ADAM_EPS, ADAM_WD, ADAM_STEP = 0.001, 0.9, 0.999, 1e-08, 0.01, 10

SHARDED = ("w_in", "w_uq", "w_ukv", "cmp_w1k", "cmp_w1v", "w_mem_kv", "w_out")
SHARD_AXIS = dict(w_in=1, w_uq=1, w_ukv=1, cmp_w1k=0, cmp_w1v=0, w_mem_kv=0, w_out=0)
REPLICATED = ("norm_g", "q_norm_g", "kv_norm_g", "cmp_pe_k", "cmp_pe_v", "cmp_w2k", "cmp_w2v",
              "mem_norm_g", "final_norm_g")
WEIGHTS = ("norm_g", "w_in", "q_norm_g", "w_uq", "kv_norm_g", "w_ukv", "cmp_pe_k", "cmp_pe_v",
           "cmp_w1k", "cmp_w2k", "cmp_w1v", "cmp_w2v", "mem_norm_g", "w_mem_kv", "w_out",
           "final_norm_g")


def _pcall(kernel, **kw):
    return pl.pallas_call(kernel, **kw)


def _tile(n, pref):
    if n <= pref:
        return n
    for t in range(pref, LANE - 1, -LANE):
        if n % t == 0:
            return t
    raise ValueError((n, pref))


def _row_tile(rows, cols, itemsize=4):
    want = max(16, BLOCK_BYTES // (cols * itemsize))
    if rows <= want:
        return rows
    t = 16
    best = rows
    while t <= want:
        if rows % t == 0:
            best = t
        t *= 2
    return best


def _nt(a, b):
    return lax.dot_general(a, b, (((1,), (1,)), ((), ())), preferred_element_type=F32)


def _tn(a, b):
    return lax.dot_general(a, b, (((0,), (0,)), ((), ())), preferred_element_type=F32)


def _nn(a, b):
    return jnp.dot(a, b, preferred_element_type=F32)


def _sigmoid(x):
    return 1.0 / (1.0 + jnp.exp(-x))


class _Src(NamedTuple):
    arr: jax.Array
    width: int
    col0: int = 0
    per_head: bool = True

    def col(self, h):
        return self.col0 + h if self.per_head else self.col0


def _mm(a, b, name, mode="nn", out_dtype=F32, second_dtype=None):
    if mode == "tn":
        k, m = a.shape
    else:
        m, k = a.shape
    if mode == "nt":
        n, k2 = b.shape
    else:
        k2, n = b.shape
    assert k == k2, (a.shape, b.shape, mode)
    tm, tn, tk = _tile(m, 1024), _tile(n, 1024), _tile(k, 2048)
    nk = k // tk
    assert nk == 1 or (out_dtype == F32 and second_dtype is None)
    dot = {"nn": _nn, "nt": _nt, "tn": _tn}[mode]

    def kern(a_ref, b_ref, o_ref, *more):
        r = dot(a_ref[...].astype(BF16), b_ref[...].astype(BF16))
        if nk == 1:
            o_ref[...] = r.astype(out_dtype)
            if more:
                more[0][...] = r.astype(second_dtype)
        else:
            kk = pl.program_id(2)

            @pl.when(kk == 0)
            def _():
                o_ref[...] = r

            @pl.when(kk > 0)
            def _():
                o_ref[...] += r

    a_spec = (pl.BlockSpec((tk, tm), lambda i, j, kk: (kk, i)) if mode == "tn"
              else pl.BlockSpec((tm, tk), lambda i, j, kk: (i, kk)))
    b_spec = (pl.BlockSpec((tn, tk), lambda i, j, kk: (j, kk)) if mode == "nt"
              else pl.BlockSpec((tk, tn), lambda i, j, kk: (kk, j)))
    o_spec = pl.BlockSpec((tm, tn), lambda i, j, kk: (i, j))
    out_shape = jax.ShapeDtypeStruct((m, n), out_dtype)
    if second_dtype is not None:
        o_spec = [o_spec, o_spec]
        out_shape = [out_shape, jax.ShapeDtypeStruct((m, n), second_dtype)]
    return _pcall(
        kern, name=name, grid=(m // tm, n // tn, nk), in_specs=[a_spec, b_spec], out_specs=o_spec,
        out_shape=out_shape,
        compiler_params=pltpu.CompilerParams(dimension_semantics=("parallel", "parallel", "arbitrary")),
    )(a, b)


def _rms_fwd(x, g, name):
    r, d = x.arr.shape[0], x.width
    tr = _tile(r, 512)

    def kern(x_ref, g_ref, y_ref, r_ref):
        xv = x_ref[...]
        rstd = lax.rsqrt(jnp.mean(xv * xv, axis=-1, keepdims=True) + EPS)
        y_ref[...] = (xv * rstd * g_ref[...]).astype(BF16)
        r_ref[...] = rstd

    return _pcall(
        kern, name=name, grid=(r // tr,),
        in_specs=[pl.BlockSpec((tr, d), lambda i: (i, x.col0)), pl.BlockSpec((1, d), lambda i: (0, 0))],
        out_specs=[pl.BlockSpec((tr, d), lambda i: (i, 0)), pl.BlockSpec((tr, 1), lambda i: (i, 0))],
        out_shape=[jax.ShapeDtypeStruct((r, d), BF16), jax.ShapeDtypeStruct((r, 1), F32)],
    )(x.arr, g)


def _rms_bwd(x, g, rstd, dy, add, name):
    r, d = x.arr.shape[0], x.width
    tr = _tile(r, 256)
    has_add = add is not None

    def kern(*refs):
        if has_add:
            x_ref, g_ref, r_ref, dy_ref, add_ref, dx_ref, dg_ref = refs
        else:
            x_ref, g_ref, r_ref, dy_ref, dx_ref, dg_ref = refs
        rs = r_ref[...]
        xhat = x_ref[...] * rs
        dyv = dy_ref[...]
        dyg = dyv * g_ref[...]
        c = jnp.mean(dyg * xhat, axis=-1, keepdims=True)
        dx = rs * (dyg - xhat * c)
        if has_add:
            dx = dx + add_ref[...]
        dx_ref[...] = dx
        part = jnp.sum(dyv * xhat, axis=0, keepdims=True)

        @pl.when(pl.program_id(0) == 0)
        def _():
            dg_ref[...] = part

        @pl.when(pl.program_id(0) > 0)
        def _():
            dg_ref[...] += part

    row = pl.BlockSpec((tr, d), lambda i: (i, 0))
    vec = pl.BlockSpec((1, d), lambda i: (0, 0))
    ins = [pl.BlockSpec((tr, d), lambda i: (i, x.col0)), vec, pl.BlockSpec((tr, 1), lambda i: (i, 0)), row]
    ins += [row] if has_add else []
    args = (x.arr, g, rstd, dy) + ((add,) if has_add else ())
    return _pcall(
        kern, name=name, grid=(r // tr,), in_specs=ins, out_specs=[row, vec],
        out_shape=[jax.ShapeDtypeStruct((r, d), F32), jax.ShapeDtypeStruct((1, d), F32)],
        compiler_params=pltpu.CompilerParams(dimension_semantics=("arbitrary",)),
    )(*args)


def _final_loss(x, proj, g, target):
    r, d = x.shape
    tr = _tile(r, 256)

    def kern(x_ref, p_ref, g_ref, t_ref, dy_ref, dg_ref, loss_ref):
        y = x_ref[...] + p_ref[...]
        rs = lax.rsqrt(jnp.mean(y * y, axis=-1, keepdims=True) + EPS)
        yhat = y * rs
        gv = g_ref[...]
        e = yhat * gv - t_ref[...]
        lpart = 0.5 * jnp.sum(jnp.mean(e * e, axis=-1, keepdims=True), axis=0, keepdims=True)
        dout = e * (1.0 / d)
        dyg = dout * gv
        c = jnp.mean(dyg * yhat, axis=-1, keepdims=True)
        dy_ref[...] = rs * (dyg - yhat * c)
        gpart = jnp.sum(dout * yhat, axis=0, keepdims=True)
        lrow = jnp.broadcast_to(lpart, (1, LANE))

        @pl.when(pl.program_id(0) == 0)
        def _():
            dg_ref[...] = gpart
            loss_ref[...] = lrow

        @pl.when(pl.program_id(0) > 0)
        def _():
            dg_ref[...] += gpart
            loss_ref[...] += lrow

    row = pl.BlockSpec((tr, d), lambda i: (i, 0))
    vec = pl.BlockSpec((1, d), lambda i: (0, 0))
    return _pcall(
        kern, name="final_loss", grid=(r // tr,), in_specs=[row, row, vec, row],
        out_specs=[row, vec, pl.BlockSpec((1, LANE), lambda i: (0, 0))],
        out_shape=[jax.ShapeDtypeStruct((r, d), F32), jax.ShapeDtypeStruct((1, d), F32),
                   jax.ShapeDtypeStruct((1, LANE), F32)],
        compiler_params=pltpu.CompilerParams(dimension_semantics=("arbitrary",)),
    )(x, proj, g, target)


def _rope_fwd(x, cs, sn, nh, width, off, name):
    s = x.arr.shape[0]
    tr = _tile(s, 512)

    def kern(x_ref, c_ref, s_ref, o_ref):
        cv, sv = c_ref[...], s_ref[...]
        for h in range(nh):
            b = h * width
            if off:
                o_ref[:, b:b + off] = x_ref[:, b:b + off].astype(BF16)
            xr = x_ref[:, b + off:b + off + LANE]
            o_ref[:, b + off:b + off + LANE] = (xr * cv + pltpu.roll(xr, 32, 1) * sv).astype(BF16)

    tab = pl.BlockSpec((tr, LANE), lambda i: (i, 0))
    return _pcall(
        kern, name=name, grid=(s // tr,),
        in_specs=[pl.BlockSpec((tr, nh * width), lambda i: (i, x.col0)), tab, tab],
        out_specs=pl.BlockSpec((tr, nh * width), lambda i: (i, 0)),
        out_shape=jax.ShapeDtypeStruct((s, nh * width), BF16),
    )(x.arr, cs, sn)


def _rope_grad(d, cv, sv):
    g2 = d * sv
    g2 = g2 + pltpu.roll(g2, 64, 1)
    lane = lax.broadcasted_iota(jnp.int32, d.shape, 1)
    return jnp.where(lane < 64, d * cv + pltpu.roll(g2, 32, 1), 0.0)


def _rope_bwd_q(dq, cs, sn):
    s, w = dq.shape
    tr = _tile(s, 512)
    nh = w // 256

    def kern(d_ref, c_ref, s_ref, o_ref):
        cv, sv = c_ref[...], s_ref[...]
        for h in range(nh):
            b = h * 256
            o_ref[:, b:b + LANE] = d_ref[:, b:b + LANE]
            o_ref[:, b + LANE:b + 256] = _rope_grad(d_ref[:, b + LANE:b + 256], cv, sv)

    row = pl.BlockSpec((tr, w), lambda i: (i, 0))
    tab = pl.BlockSpec((tr, LANE), lambda i: (i, 0))
    return _pcall(kern, name="rope_bwd_q", grid=(s // tr,), in_specs=[row, tab, tab], out_specs=row,
                  out_shape=jax.ShapeDtypeStruct((s, w), F32))(dq, cs, sn)


def _rope_bwd_k(dk_nope, dk_pe, dv, cs, sn):
    s, w = dk_nope.shape
    tr = _tile(s, 512)

    def kern(dk_ref, dp_ref, dv_ref, c_ref, s_ref, okv_ref, okr_ref):
        okv_ref[:, :w] = dk_ref[...]
        okv_ref[:, w:] = dv_ref[...]
        okr_ref[...] = _rope_grad(dp_ref[...], c_ref[...], s_ref[...])

    tab = pl.BlockSpec((tr, LANE), lambda i: (i, 0))
    wide = pl.BlockSpec((tr, w), lambda i: (i, 0))
    return _pcall(
        kern, name="rope_bwd_k", grid=(s // tr,), in_specs=[wide, tab, wide, tab, tab],
        out_specs=[pl.BlockSpec((tr, 2 * w), lambda i: (i, 0)), tab],
        out_shape=[jax.ShapeDtypeStruct((s, 2 * w), F32), jax.ShapeDtypeStruct((s, LANE), F32)],
    )(dk_nope, dk_pe, dv, cs, sn)


class _Attn:
    def __init__(self, mode, s, sk, heads, dk):
        self.mode, self.s, self.sk, self.h, self.dk = mode, s, sk, heads, dk
        self.scale = {"mla": 192 ** -0.5, "mem": 128 ** -0.5}.get(mode, NSA_DK ** -0.5)
        self.tq = min(256, s)
        self.nq = s // self.tq
        self.causal = mode in ("mla", "slc")
        if self.causal:
            self.tk = min(512, s)
        elif mode == "win":
            self.tk = WIN + self.tq
        else:
            self.tk = sk
        self.tkb = min(512, sk)
        self.ncmp = s // CMP_STRIDE - 1

    def mask_bias(self, t, n, h, selx, diag):
        m = self.mode
        if m == "mla":
            return (n <= t) if diag else None, None
        if m == "mem":
            return None, None
        slope = jnp.where(h == 0, 0.25, jnp.where(h == 1, 0.0625, jnp.where(h == 2, 0.015625, 0.00390625)))
        slope = slope.astype(F32)
        if m == "cmp":
            mask = (n * CMP_STRIDE + (CMP_LEN - 1) <= t) & (n < self.ncmp)
            pos = n.astype(F32) * float(CMP_STRIDE) + (CMP_LEN - 1) / 2.0
            return mask, -slope * (t.astype(F32) - pos)
        rel = t - n
        bias = -slope * rel.astype(F32)
        if m == "slc":
            mask = selx > 0.5
            return (mask & (rel >= 0)) if diag else mask, bias
        return (rel >= 0) & (rel < WIN), bias


def _scores(cfg, s_raw, t, n, h, selx, diag, lse=None):
    s = s_raw * cfg.scale
    mask, bias = cfg.mask_bias(t, n, h, selx, diag)
    if bias is not None:
        s = s + bias
    if lse is None:
        if mask is not None:
            s = jnp.where(mask, s, NEG)
        return s, mask
    p = jnp.exp(jnp.minimum(s - lse, 0.0))
    if mask is not None:
        p = jnp.where(mask, p, 0.0)
    return p, mask


def _expand_rows(sel, k0, tk):
    j = lax.broadcasted_iota(jnp.int32, (LANE, tk), 0)
    n = lax.broadcasted_iota(jnp.int32, (LANE, tk), 1) + k0
    e = jnp.where((n >> 6) == j, 1.0, 0.0).astype(BF16)
    return _nn(sel.astype(BF16), e)


def _expand_cols(selt, k0, tk):
    n = lax.broadcasted_iota(jnp.int32, (tk, LANE), 0) + k0
    j = lax.broadcasted_iota(jnp.int32, (tk, LANE), 1)
    e = jnp.where((n >> 6) == j, 1.0, 0.0).astype(BF16)
    return _nn(e, selt.astype(BF16))


def _to_row(col):
    t = col.shape[0]
    return jnp.transpose(jnp.broadcast_to(col, (t, LANE)))[0:1, :]


def _load_keys(k_refs, rows):
    parts = [r[rows, :].astype(BF16) for r in k_refs]
    return parts[0] if len(parts) == 1 else jnp.concatenate(parts, axis=1)


def _attn_fwd(cfg, q, ks, v, sel, name):
    s, tq, tk = cfg.s, cfg.tq, cfg.tk
    has_sel = sel is not None
    nkp = len(ks)

    def kern(*refs):
        q_ref, k_refs, v_ref = refs[0], refs[1:1 + nkp], refs[1 + nkp]
        sel_ref = refs[2 + nkp] if has_sel else None
        o_ref, lc_ref, lr_ref = refs[-3:]
        h, i = pl.program_id(0), pl.program_id(1)
        qv = q_ref[...].astype(BF16)
        t = i * tq + lax.broadcasted_iota(jnp.int32, (tq, 1), 0)

        def chunk(k0, carry, diag):
            m, l, acc = carry
            rows = pl.ds(k0, tk)
            kk = _load_keys(k_refs, rows)
            vv = v_ref[rows, :].astype(BF16)
            n = k0 + lax.broadcasted_iota(jnp.int32, (1, tk), 1)
            selx = _expand_rows(sel_ref[...], k0, tk) if has_sel else None
            sc, mask = _scores(cfg, _nt(qv, kk), t, n, h, selx, diag)
            m_new = jnp.maximum(m, jnp.max(sc, axis=1, keepdims=True))
            alpha = jnp.exp(m - m_new)
            p = jnp.exp(sc - m_new)
            if mask is not None:
                p = jnp.where(mask, p, 0.0)
            l = alpha * l + jnp.sum(p, axis=1, keepdims=True)
            acc = alpha * acc + _nn(p.astype(BF16), vv)
            return m_new, l, acc

        carry = (jnp.full((tq, 1), NEG, F32), jnp.zeros((tq, 1), F32), jnp.zeros((tq, HEAD_V), F32))
        if cfg.causal:
            full = (i * tq) // tk
            carry = lax.fori_loop(0, full, lambda c, cr: chunk(pl.multiple_of(c * tk, tk), cr, False), carry)
            carry = chunk(pl.multiple_of(full * tk, tk), carry, True)
        elif cfg.mode == "win":
            carry = chunk(pl.multiple_of(jnp.maximum(i * tq - WIN, 0), tq), carry, True)
        else:
            carry = chunk(0, carry, True)
        m, l, acc = carry
        o_ref[...] = acc / (l + 1e-20)
        lse = m + jnp.log(l + 1e-20)
        lc_ref[0] = lse
        lr_ref[0, 0] = _to_row(lse)

    ins = [pl.BlockSpec((tq, q.width), lambda h, i: (i, q.col(h)))]
    ins += [pl.BlockSpec((cfg.sk, p.width), lambda h, i, p=p: (0, p.col(h))) for p in ks]
    ins += [pl.BlockSpec((cfg.sk, HEAD_V), lambda h, i: (0, v.col(h)))]
    args = [q.arr] + [p.arr for p in ks] + [v.arr]
    if has_sel:
        ins.append(pl.BlockSpec((tq, LANE), lambda h, i: (i, 0)))
        args.append(sel)
    return _pcall(
        kern, name=name, grid=(cfg.h, cfg.nq), in_specs=ins,
        out_specs=[pl.BlockSpec((tq, HEAD_V), lambda h, i: (i, h)),
                   pl.BlockSpec((1, tq, 1), lambda h, i: (h, i, 0)),
                   pl.BlockSpec((1, 1, 1, tq), lambda h, i: (h, i, 0, 0))],
        out_shape=[jax.ShapeDtypeStruct((s, cfg.h * HEAD_V), F32),
                   jax.ShapeDtypeStruct((cfg.h, s, 1), F32),
                   jax.ShapeDtypeStruct((cfg.h, cfg.nq, 1, tq), F32)],
        compiler_params=pltpu.CompilerParams(dimension_semantics=("parallel", "parallel")),
    )(*args)


def _attn_dq(cfg, q, ks, v, sel, o, lse, do, dq_in, name):
    s, tq, tk, dk = cfg.s, cfg.tq, cfg.tk, cfg.dk
    has_sel = sel is not None
    has_in = dq_in is not None
    nkp = len(ks)

    def kern(*refs):
        refs = list(refs)
        q_ref, k_refs, v_ref = refs[0], refs[1:1 + nkp], refs[1 + nkp]
        p0 = 2 + nkp
        sel_ref = refs[p0] if has_sel else None
        p0 += has_sel
        o_ref, l_ref, do_ref = refs[p0:p0 + 3]
        p0 += 3
        in_ref = refs[p0] if has_in else None
        dq_ref, dr_ref = refs[-2:]
        h, i = pl.program_id(0), pl.program_id(1)
        qv = q_ref[...].astype(BF16)
        dov = do_ref[...]
        dvec = jnp.sum(dov * o_ref[...], axis=1, keepdims=True)
        dr_ref[0, 0] = _to_row(dvec)
        dob = dov.astype(BF16)
        lse_v = l_ref[0]
        t = i * tq + lax.broadcasted_iota(jnp.int32, (tq, 1), 0)

        def chunk(k0, acc, diag):
            rows = pl.ds(k0, tk)
            kk = _load_keys(k_refs, rows)
            vv = v_ref[rows, :].astype(BF16)
            n = k0 + lax.broadcasted_iota(jnp.int32, (1, tk), 1)
            selx = _expand_rows(sel_ref[...], k0, tk) if has_sel else None
            p, _ = _scores(cfg, _nt(qv, kk), t, n, h, selx, diag, lse_v)
            ds = p * (_nt(dob, vv) - dvec) * cfg.scale
            return acc + _nn(ds.astype(BF16), kk)

        acc = in_ref[...] if has_in else jnp.zeros((tq, dk), F32)
        if cfg.causal:
            full = (i * tq) // tk
            acc = lax.fori_loop(0, full, lambda c, a: chunk(pl.multiple_of(c * tk, tk), a, False), acc)
            acc = chunk(pl.multiple_of(full * tk, tk), acc, True)
        elif cfg.mode == "win":
            acc = chunk(pl.multiple_of(jnp.maximum(i * tq - WIN, 0), tq), acc, True)
        else:
            acc = chunk(0, acc, True)
        dq_ref[...] = acc

    qs = pl.BlockSpec((tq, dk), lambda h, i: (i, h))
    ins = [pl.BlockSpec((tq, q.width), lambda h, i: (i, q.col(h)))]
    ins += [pl.BlockSpec((cfg.sk, p.width), lambda h, i, p=p: (0, p.col(h))) for p in ks]
    ins += [pl.BlockSpec((cfg.sk, HEAD_V), lambda h, i: (0, v.col(h)))]
    args = [q.arr] + [p.arr for p in ks] + [v.arr]
    if has_sel:
        ins.append(pl.BlockSpec((tq, LANE), lambda h, i: (i, 0)))
        args.append(sel)
    ins += [pl.BlockSpec((tq, HEAD_V), lambda h, i: (i, o.col(h))),
            pl.BlockSpec((1, tq, 1), lambda h, i: (h, i, 0)),
            pl.BlockSpec((tq, HEAD_V), lambda h, i: (i, do.col(h)))]
    args += [o.arr, lse, do.arr]
    if has_in:
        ins.append(qs)
        args.append(dq_in)
    return _pcall(
        kern, name=name, grid=(cfg.h, cfg.nq), in_specs=ins,
        out_specs=[qs, pl.BlockSpec((1, 1, 1, tq), lambda h, i: (h, i, 0, 0))],
        out_shape=[jax.ShapeDtypeStruct((s, cfg.h * dk), F32),
                   jax.ShapeDtypeStruct((cfg.h, cfg.nq, 1, tq), F32)],
        compiler_params=pltpu.CompilerParams(dimension_semantics=("parallel", "parallel")),
    )(*args)


def _attn_dkv(cfg, q, ks, v, selt, lse_r, d_r, do, name):
    s, tq, tkb, dk = cfg.s, cfg.tq, cfg.tkb, cfg.dk
    nq = cfg.nq
    has_sel = selt is not None
    nkp = len(ks)
    outs = list(ks) + [v]

    def kern(*refs):
        k_refs, v_ref = refs[:nkp], refs[nkp]
        q_ref, do_ref, lr_ref, dr_ref = refs[nkp + 1:nkp + 5]
        st_ref = refs[nkp + 5] if has_sel else None
        out_refs = refs[-(nkp + 1):]
        j, h = pl.program_id(0), pl.program_id(1)
        k0 = j * tkb
        kk = _load_keys(k_refs, slice(None))
        vv = v_ref[...].astype(BF16)
        n = k0 + lax.broadcasted_iota(jnp.int32, (tkb, 1), 0)

        def qblock(i, carry, diag):
            dk_acc, dv_acc = carry
            r0 = pl.multiple_of(i * tq, tq)
            qi = q_ref[pl.ds(r0, tq), :].astype(BF16)
            doi = do_ref[pl.ds(r0, tq), :].astype(BF16)
            t = i * tq + lax.broadcasted_iota(jnp.int32, (1, tq), 1)
            selx = _expand_cols(st_ref[i], k0, tkb) if has_sel else None
            pt, _ = _scores(cfg, _nt(kk, qi), t, n, h, selx, diag, lr_ref[0, i])
            dv_acc = dv_acc + _nn(pt.astype(BF16), doi)
            dst = pt * (_nt(vv, doi) - dr_ref[0, i]) * cfg.scale
            dk_acc = dk_acc + _nn(dst.astype(BF16), qi)
            return dk_acc, dv_acc

        carry = (jnp.zeros((tkb, dk), F32), jnp.zeros((tkb, HEAD_V), F32))
        masked = lambda i, cr: qblock(i, cr, True)
        if cfg.causal:
            first, past = k0 // tq, (k0 + tkb) // tq
            carry = lax.fori_loop(first, past, masked, carry)
            carry = lax.fori_loop(past, nq, lambda i, cr: qblock(i, cr, False), carry)
        elif cfg.mode == "win":
            carry = lax.fori_loop(k0 // tq, jnp.minimum((k0 + tkb + WIN - 2) // tq + 1, nq), masked, carry)
        else:
            carry = lax.fori_loop(0, nq, masked, carry)
        dk_acc, dv_acc = carry
        vals, off = [], 0
        for p in ks:
            vals.append(dk_acc[:, off:off + p.width])
            off += p.width
        vals.append(dv_acc)
        for src, ref, val in zip(outs, out_refs, vals):
            if src.per_head:
                ref[...] = val
            else:
                @pl.when(h == 0)
                def _(ref=ref, val=val):
                    ref[...] = val

                @pl.when(h > 0)
                def _(ref=ref, val=val):
                    ref[...] += val

    rowv = pl.BlockSpec((1, nq, 1, tq), lambda j, h: (h, 0, 0, 0))
    ins = [pl.BlockSpec((tkb, p.width), lambda j, h, p=p: (j, p.col(h))) for p in ks]
    ins += [pl.BlockSpec((tkb, HEAD_V), lambda j, h: (j, v.col(h))),
            pl.BlockSpec((s, q.width), lambda j, h: (0, q.col(h))),
            pl.BlockSpec((s, HEAD_V), lambda j, h: (0, do.col(h))), rowv, rowv]
    args = [p.arr for p in ks] + [v.arr, q.arr, do.arr, lse_r, d_r]
    if has_sel:
        ins.append(pl.BlockSpec((nq, LANE, tq), lambda j, h: (0, 0, 0)))
        args.append(selt)
    out_specs = [pl.BlockSpec((tkb, p.width), lambda j, h, p=p: (j, h if p.per_head else 0)) for p in outs]
    out_shape = [jax.ShapeDtypeStruct((cfg.sk, (cfg.h if p.per_head else 1) * p.width), F32) for p in outs]
    return _pcall(
        kern, name=name, grid=(cfg.sk // tkb, cfg.h), in_specs=ins, out_specs=out_specs, out_shape=out_shape,
        compiler_params=pltpu.CompilerParams(dimension_semantics=("parallel", "arbitrary")),
    )(*args)


def _attn_bwd(cfg, q, ks, v, sel, selt, o, lse, lse_r, do, dq_in, name):
    dq, d_r = _attn_dq(cfg, q, ks, v, sel, o, lse, do, dq_in, name + "_dq")
    res = _attn_dkv(cfg, q, ks, v, selt, lse_r, d_r, do, name + "_dkv")
    return dq, res[:-1], res[-1]


def _select(cfg, q, k_cmp, overlap):
    s, tq, sk = cfg.s, cfg.tq, cfg.sk
    n_s = s // SLC_LEN
    top_n = min(SLC_TOPN, n_s)

    def kern(q_ref, k_ref, ov_ref, sel_ref, selt_ref):
        i = pl.program_id(0)
        t = i * tq + lax.broadcasted_iota(jnp.int32, (tq, 1), 0)
        n = lax.broadcasted_iota(jnp.int32, (1, sk), 1)
        kk = k_ref[...]
        imp = jnp.zeros((tq, LANE), F32)
        for h in range(NSA_HEADS):
            sc, mask = _scores(cfg, _nt(q_ref[:, h * 256:(h + 1) * 256], kk), t, n, h, None, True)
            m = jnp.max(sc, axis=1, keepdims=True)
            e = jnp.where(mask, jnp.exp(sc - m), 0.0)
            p = e / (jnp.sum(e, axis=1, keepdims=True) + 1e-20)
            imp = imp + _nn(p.astype(BF16), ov_ref[...])
        j = lax.broadcasted_iota(jnp.int32, (tq, LANE), 1)
        cur = t >> 6
        forced = (j == 0) | (j == cur) | (j == cur - 1)
        imp = jnp.where(forced, 1e9, imp)
        imp = jnp.where(j > cur, -1e9, imp)
        imp = jnp.where(j >= n_s, -3e38, imp)

        def pick(_, carry):
            work, chosen = carry
            mx = jnp.max(work, axis=1, keepdims=True)
            first = jnp.min(jnp.where(work == mx, j, LANE), axis=1, keepdims=True)
            hit = j == first
            return jnp.where(hit, -3e38, work), jnp.where(hit, 1.0, chosen)

        _, chosen = lax.fori_loop(0, top_n, pick, (imp, jnp.zeros((tq, LANE), F32)))
        chosen = jnp.where(j <= cur, chosen, 0.0)
        sel_ref[...] = chosen
        selt_ref[0] = jnp.transpose(chosen)

    return _pcall(
        kern, name="nsa_select", grid=(cfg.nq,),
        in_specs=[pl.BlockSpec((tq, NSA_HEADS * 256), lambda i: (i, q.col0)),
                  pl.BlockSpec((sk, 256), lambda i: (0, 0)), pl.BlockSpec((sk, LANE), lambda i: (0, 0))],
        out_specs=[pl.BlockSpec((tq, LANE), lambda i: (i, 0)), pl.BlockSpec((1, LANE, tq), lambda i: (i, 0, 0))],
        out_shape=[jax.ShapeDtypeStruct((s, LANE), F32), jax.ShapeDtypeStruct((cfg.nq, LANE, tq), F32)],
    )(q.arr, k_cmp, overlap)


def _silu_grad(pre):
    sg = _sigmoid(pre)
    return sg * (1.0 + pre * (1.0 - sg))


def _compress_fwd(a_lo, a_hi, pe_lo, pe_hi, w1_lo, w1_hi, w2, name):
    n, dp = a_lo.shape[0], w2.shape[1]

    def kern(alo, ahi, plo, phi, w1l, w1h, w2r, out_ref, pre_ref):
        xl = (alo[...] + plo[...]).astype(BF16)
        xh = (ahi[...] + phi[...]).astype(BF16)
        pre = _nn(xl, w1l[...]) + _nn(xh, w1h[...])
        act = pre * _sigmoid(pre)
        out_ref[...] = _nn(act.astype(BF16), w2r[...]).astype(BF16)
        pre_ref[...] = pre

    return _pcall(kern, name=name,
                  out_shape=[jax.ShapeDtypeStruct((n, dp), BF16), jax.ShapeDtypeStruct((n, dp), F32)],
                  )(a_lo, a_hi, pe_lo, pe_hi, w1_lo, w1_hi, w2)


def _compress_bwd(a_lo, a_hi, pe_lo, pe_hi, w1_lo, w1_hi, w2, pre, pre_sh, dout, dout_sh, name):
    n, ln = a_lo.shape
    dp = w2.shape[1]

    def kern(alo, ahi, plo, phi, w1l, w1h, w2r, pre_ref, presh_ref, do_ref, dosh_ref,
             da_ref, dpl_ref, dph_ref, dw1l_ref, dw1h_ref, dw2_ref):
        prev = pre_ref[...]
        act = prev * _sigmoid(prev)
        dob = do_ref[...].astype(BF16)
        w2v = w2r[...]
        dpre = (_nt(dob, w2v) * _silu_grad(prev)).astype(BF16)
        dpre_sh = (_nt(dosh_ref[...].astype(BF16), w2v) * _silu_grad(presh_ref[...])).astype(BF16)
        dw2_ref[...] = _nn(act.T.astype(BF16), dob)
        xl = alo[...] + plo[...]
        xh = ahi[...] + phi[...]
        dw1l_ref[...] = _nn(xl.T.astype(BF16), dpre)
        dw1h_ref[...] = _nn(xh.T.astype(BF16), dpre)
        dal = _nt(dpre, w1l[...])
        dah_sh = _nt(dpre_sh, w1h[...])
        da_ref[...] = dal + dah_sh
        dpl_ref[...] = jnp.sum(dal, axis=0, keepdims=True)
        dph_ref[...] = jnp.sum(dah_sh, axis=0, keepdims=True)

    return _pcall(
        kern, name=name,
        out_shape=[jax.ShapeDtypeStruct((n, ln), F32), jax.ShapeDtypeStruct((1, ln), F32),
                   jax.ShapeDtypeStruct((1, ln), F32), jax.ShapeDtypeStruct((ln, dp), F32),
                   jax.ShapeDtypeStruct((ln, dp), F32), jax.ShapeDtypeStruct((dp, dp), F32)],
    )(a_lo, a_hi, pe_lo, pe_hi, w1_lo, w1_hi, w2, pre, pre_sh, dout, dout_sh)


def _nsa_combine(o_cmp, o_slc, o_win, gl):
    s, w = o_cmp.shape
    tr = _tile(s, 512)

    def kern(a_ref, b_ref, c_ref, g_ref, o_ref):
        g = _sigmoid(g_ref[...])
        for h in range(NSA_HEADS):
            cs = slice(h * HEAD_V, (h + 1) * HEAD_V)
            o_ref[:, cs] = (g[:, 3 * h:3 * h + 1] * a_ref[:, cs] + g[:, 3 * h + 1:3 * h + 2] * b_ref[:, cs]
                            + g[:, 3 * h + 2:3 * h + 3] * c_ref[:, cs])

    row = pl.BlockSpec((tr, w), lambda i: (i, 0))
    return _pcall(kern, name="nsa_combine", grid=(s // tr,),
                  in_specs=[row, row, row, pl.BlockSpec((tr, LANE), lambda i: (i, gl.col0))], out_specs=row,
                  out_shape=jax.ShapeDtypeStruct((s, w), F32))(o_cmp, o_slc, o_win, gl.arr)


def _nsa_combine_bwd(do_cat, o_cmp, o_slc, o_win, gl):
    s, w = o_cmp.shape
    tr = _tile(s, 512)

    def kern(d_ref, a_ref, b_ref, c_ref, g_ref, da_ref, db_ref, dc_ref, dg_ref):
        g = _sigmoid(g_ref[...])
        lane = lax.broadcasted_iota(jnp.int32, (tr, LANE), 1)
        dgl = jnp.zeros((tr, LANE), F32)
        for h in range(NSA_HEADS):
            cs = slice(h * HEAD_V, (h + 1) * HEAD_V)
            dv = d_ref[:, cs]
            for b, (src, dst) in enumerate(((a_ref, da_ref), (b_ref, db_ref), (c_ref, dc_ref))):
                gate = g[:, 3 * h + b:3 * h + b + 1]
                dst[:, cs] = gate * dv
                dgate = jnp.sum(dv * src[:, cs], axis=1, keepdims=True)
                dgl = jnp.where(lane == 3 * h + b, dgate * gate * (1.0 - gate), dgl)
        dg_ref[...] = dgl

    row = pl.BlockSpec((tr, w), lambda i: (i, 0))
    tab = pl.BlockSpec((tr, LANE), lambda i: (i, 0))
    return _pcall(kern, name="nsa_combine_bwd", grid=(s // tr,),
                  in_specs=[pl.BlockSpec((tr, w), lambda i: (i, 2)), row, row, row,
                            pl.BlockSpec((tr, LANE), lambda i: (i, gl.col0))],
                  out_specs=[row, row, row, tab],
                  out_shape=[jax.ShapeDtypeStruct((s, w), F32)] * 3 + [jax.ShapeDtypeStruct((s, LANE), F32)],
                  )(do_cat, o_cmp, o_slc, o_win, gl.arr)


def _gate_fwd(o_mla, o_nsa, o_mem, hp):
    s = o_mla.shape[0]
    tr = _tile(s, 256)

    def kern(a_ref, b_ref, c_ref, z_ref, u_ref):
        z = z_ref[...]
        sz = z * _sigmoid(z)
        u_ref[:, 0:1024] = (a_ref[...] * sz[:, 0:1024]).astype(BF16)
        u_ref[:, 1024:1536] = (b_ref[...] * sz[:, 1024:1536]).astype(BF16)
        u_ref[:, 1536:2048] = (c_ref[...] * sz[:, 1536:2048]).astype(BF16)

    return _pcall(
        kern, name="gate_fwd", grid=(s // tr,),
        in_specs=[pl.BlockSpec((tr, 1024), lambda i: (i, 0)), pl.BlockSpec((tr, 512), lambda i: (i, 0)),
                  pl.BlockSpec((tr, 512), lambda i: (i, 0)), pl.BlockSpec((tr, 2048), lambda i: (i, 2))],
        out_specs=pl.BlockSpec((tr, 2048), lambda i: (i, 0)),
        out_shape=jax.ShapeDtypeStruct((s, 2048), BF16))(o_mla, o_nsa, o_mem, hp)


def _gate_bwd(du, o_mla, o_nsa, o_mem, hp):
    s = du.shape[0]
    tr = _tile(s, 256)

    def kern(d_ref, a_ref, b_ref, c_ref, z_ref, do_ref, dz_ref):
        z = z_ref[...]
        sg = _sigmoid(z)
        sz = z * sg
        dsz = sg * (1.0 + z * (1.0 - sg))
        d = d_ref[...]
        do_ref[...] = d * sz
        dz_ref[:, 0:1024] = d[:, 0:1024] * a_ref[...] * dsz[:, 0:1024]
        dz_ref[:, 1024:1536] = d[:, 1024:1536] * b_ref[...] * dsz[:, 1024:1536]
        dz_ref[:, 1536:2048] = d[:, 1536:2048] * c_ref[...] * dsz[:, 1536:2048]

    wide = pl.BlockSpec((tr, 2048), lambda i: (i, 0))
    return _pcall(
        kern, name="gate_bwd", grid=(s // tr,),
        in_specs=[wide, pl.BlockSpec((tr, 1024), lambda i: (i, 0)), pl.BlockSpec((tr, 512), lambda i: (i, 0)),
                  pl.BlockSpec((tr, 512), lambda i: (i, 0)), pl.BlockSpec((tr, 2048), lambda i: (i, 2))],
        out_specs=[wide, wide],
        out_shape=[jax.ShapeDtypeStruct((s, 2048), F32)] * 2)(du, o_mla, o_nsa, o_mem, hp)


def _sum_slots(buf, name):
    n, rows, cols = buf.shape
    tr = _row_tile(rows, cols * n)

    def kern(b_ref, o_ref):
        acc = b_ref[0].astype(F32)
        for i in range(1, n):
            acc = acc + b_ref[i].astype(F32)
        o_ref[...] = acc

    return _pcall(kern, name=name, grid=(rows // tr,),
                  in_specs=[pl.BlockSpec((n, tr, cols), lambda i: (0, i, 0))],
                  out_specs=pl.BlockSpec((tr, cols), lambda i: (i, 0)),
                  out_shape=jax.ShapeDtypeStruct((rows, cols), F32))(buf)


def _pair_sum(g4, theirs, core, name):
    n, rows, cols = g4.shape
    half = rows // 2
    tr = _row_tile(half, cols)
    nb = half // tr

    def kern(c_ref, a_ref, b_ref, o_ref):
        o_ref[...] = (a_ref[...] + b_ref[...]).astype(BF16)

    blk = (1, tr, cols)
    grid_spec = pltpu.PrefetchScalarGridSpec(
        num_scalar_prefetch=1, grid=(n, nb),
        in_specs=[pl.BlockSpec(blk, lambda s, i, c: (s, c[0] * nb + i, 0)),
                  pl.BlockSpec(blk, lambda s, i, c: (s, i, 0))],
        out_specs=pl.BlockSpec(blk, lambda s, i, c: (s, i, 0)))
    return _pcall(kern, name=name, grid_spec=grid_spec,
                  out_shape=jax.ShapeDtypeStruct((n, half, cols), BF16))(core, g4, theirs)


def _adamw(w, g, m, v, name):
    rows, cols = w.shape
    tr = _row_tile(rows, cols * 4)
    bc1 = 1.0 - ADAM_B1 ** ADAM_STEP
    bc2 = 1.0 - ADAM_B2 ** ADAM_STEP

    def kern(w_ref, g_ref, m_ref, v_ref, d_ref, mo_ref, vo_ref):
        gv = g_ref[...]
        mn = ADAM_B1 * m_ref[...] + (1.0 - ADAM_B1) * gv
        vn = ADAM_B2 * v_ref[...] + (1.0 - ADAM_B2) * (gv * gv)
        d_ref[...] = -ADAM_LR * ((mn / bc1) / (jnp.sqrt(vn / bc2) + ADAM_EPS) + ADAM_WD * w_ref[...])
        mo_ref[...] = mn
        vo_ref[...] = vn

    row = pl.BlockSpec((tr, cols), lambda i: (i, 0))
    return _pcall(kern, name=name, grid=(rows // tr,), in_specs=[row] * 4, out_specs=[row] * 3,
                  out_shape=[jax.ShapeDtypeStruct((rows, cols), F32)] * 3)(w, g, m, v)


ANY = pl.BlockSpec(memory_space=pl.ANY)


def _place():
    x, y, c = lax.axis_index("x"), lax.axis_index("y"), lax.axis_index("c")
    chips = [(1 - x, y), (x, 1 - y), (1 - x, 1 - y)]
    return x, y, c, chips


def _remote(src, dst, send_sem, recv_sem, to):
    return pltpu.make_async_remote_copy(src_ref=src, dst_ref=dst, send_sem=send_sem, recv_sem=recv_sem,
                                        device_id=to, device_id_type=MESH)


def _gather_shards(ws):
    nw = len(ws)
    halves = [w.shape[0] // 2 for w in ws]

    def body(*refs):
        w_refs, out_refs = refs[:nw], refs[nw:2 * nw]
        send_sems, recv_sems = refs[2 * nw:]
        x, y, c, chips = _place()
        me = 2 * x + y
        sibling = (x, y, 1 - c)

        def part(i, slot, core):
            return out_refs[i].at[slot, pl.ds(core * halves[i], halves[i]), :]

        def copy(sem, src, dst, to):
            return _remote(src, dst, send_sems.at[sem], recv_sems.at[sem], to)

        first = [copy(j * nw + i, w_refs[i].at[pl.ds(c * halves[i], halves[i]), :], part(i, me, c), (*chip, c))
                 for j, chip in enumerate(chips) for i in range(nw)]
        for cp in first:
            cp.start()
        passed = []
        for j, (cx, cy) in enumerate(chips):
            slot = 2 * cx + cy
            for i in range(nw):
                copy(j * nw + i, part(i, slot, c), part(i, slot, c), (x, y, c)).wait_recv()
                fwd = copy((3 + j) * nw + i, part(i, slot, c), part(i, slot, c), sibling)
                fwd.start()
                passed.append(fwd)
        for j, (cx, cy) in enumerate(chips):
            slot = 2 * cx + cy
            for i in range(nw):
                copy((3 + j) * nw + i, part(i, slot, 1 - c), part(i, slot, 1 - c), (x, y, c)).wait_recv()
        for cp in first + passed:
            cp.wait_send()

    return _pcall(
        body, name="gather_shards", in_specs=[ANY] * nw, out_specs=[ANY] * nw,
        out_shape=[jax.ShapeDtypeStruct((4,) + w.shape, w.dtype) for w in ws],
        scratch_shapes=[pltpu.SemaphoreType.DMA((6 * nw,)), pltpu.SemaphoreType.DMA((6 * nw,))],
    )(*ws)


def _pair_exchange(gs):
    nw = len(gs)

    def body(*refs):
        g_refs, out_refs = refs[:nw], refs[nw:2 * nw]
        send_sems, recv_sems = refs[2 * nw:]
        x, y, c, _ = _place()
        cps = []
        for i in range(nw):
            half = gs[i].shape[1] // 2
            cp = _remote(g_refs[i].at[:, pl.ds((1 - c) * half, half), :], out_refs[i],
                         send_sems.at[i], recv_sems.at[i], (x, y, 1 - c))
            cp.start()
            cps.append(cp)
        for cp in cps:
            cp.wait()

    return _pcall(body, name="pair_exchange", in_specs=[ANY] * nw, out_specs=[ANY] * nw,
                  out_shape=[jax.ShapeDtypeStruct((4, g.shape[1] // 2, g.shape[2]), g.dtype) for g in gs],
                  scratch_shapes=[pltpu.SemaphoreType.DMA((nw,)), pltpu.SemaphoreType.DMA((nw,))])(*gs)


def _chip_exchange(ps):
    nw = len(ps)

    def body(*refs):
        p_refs, out_refs = refs[:nw], refs[nw:2 * nw]
        send_sems, recv_sems, local_sems = refs[2 * nw:]
        x, y, c, chips = _place()
        me = 2 * x + y
        mine = [pltpu.make_async_copy(p_refs[i].at[me], out_refs[i].at[me], local_sems.at[i]) for i in range(nw)]
        for cp in mine:
            cp.start()
        sends = []
        for j, (cx, cy) in enumerate(chips):
            for i in range(nw):
                cp = _remote(p_refs[i].at[2 * cx + cy], out_refs[i].at[me], send_sems.at[j * nw + i],
                             recv_sems.at[j * nw + i], (cx, cy, c))
                cp.start()
                sends.append(cp)
        for j, (cx, cy) in enumerate(chips):
            slot = 2 * cx + cy
            for i in range(nw):
                _remote(out_refs[i].at[slot], out_refs[i].at[slot], send_sems.at[j * nw + i],
                        recv_sems.at[j * nw + i], (x, y, c)).wait_recv()
        for cp in sends:
            cp.wait_send()
        for cp in mine:
            cp.wait()

    return _pcall(body, name="chip_exchange", in_specs=[ANY] * nw, out_specs=[ANY] * nw,
                  out_shape=[jax.ShapeDtypeStruct(p.shape, p.dtype) for p in ps],
                  scratch_shapes=[pltpu.SemaphoreType.DMA((3 * nw,)), pltpu.SemaphoreType.DMA((3 * nw,)),
                                  pltpu.SemaphoreType.DMA((nw,))])(*ps)


def _half_exchange(ts):
    nw = len(ts)

    def body(*refs):
        t_refs, out_refs = refs[:nw], refs[nw:2 * nw]
        send_sems, recv_sems = refs[2 * nw:]
        x, y, c, _ = _place()
        sends = []
        for i in range(nw):
            half = ts[i].shape[0]
            own = out_refs[i].at[pl.ds(c * half, half), :]
            cp = _remote(t_refs[i], own, send_sems.at[i], recv_sems.at[i], (x, y, 1 - c))
            cp.start()
            sends.append(cp)
        for i in range(nw):
            half = ts[i].shape[0]
            other = out_refs[i].at[pl.ds((1 - c) * half, half), :]
            _remote(t_refs[i], other, send_sems.at[i], recv_sems.at[i], (x, y, c)).wait_recv()
        for cp in sends:
            cp.wait_send()

    return _pcall(body, name="half_exchange", in_specs=[ANY] * nw, out_specs=[ANY] * nw,
                  out_shape=[jax.ShapeDtypeStruct((2 * t.shape[0], t.shape[1]), t.dtype) for t in ts],
                  scratch_shapes=[pltpu.SemaphoreType.DMA((nw,)), pltpu.SemaphoreType.DMA((nw,))])(*ts)


def _gather_all(v):
    rows, cols = v.shape

    def body(v_ref, out_ref, send_sems, recv_sems, local_sem):
        x, y, c, _ = _place()
        me = 4 * x + 2 * y + c
        mine = pltpu.make_async_copy(v_ref, out_ref.at[me], local_sem)
        mine.start()
        sends = []
        for d in range(1, 8):
            peer = (x ^ (d >> 2), y ^ ((d >> 1) & 1), c ^ (d & 1))
            cp = _remote(v_ref, out_ref.at[me], send_sems.at[d - 1], recv_sems.at[d - 1], peer)
            cp.start()
            sends.append(cp)
        for d in range(1, 8):
            slot = 4 * (x ^ (d >> 2)) + 2 * (y ^ ((d >> 1) & 1)) + (c ^ (d & 1))
            _remote(v_ref, out_ref.at[slot], send_sems.at[d - 1], recv_sems.at[d - 1], (x, y, c)).wait_recv()
        for cp in sends:
            cp.wait_send()
        mine.wait()

    return _pcall(body, name="gather_all", in_specs=[ANY], out_specs=ANY,
                  out_shape=jax.ShapeDtypeStruct((8, rows, cols), v.dtype),
                  scratch_shapes=[pltpu.SemaphoreType.DMA((7,)), pltpu.SemaphoreType.DMA((7,)),
                                  pltpu.SemaphoreType.DMA])(v)


def _pad_cols(a, width):
    return a if a.shape[1] == width else jnp.pad(a, ((0, 0), (0, width - a.shape[1])))


def _w_in_padded(w):
    def seg(name, width=None):
        o, n = ORIG[name]
        return _pad_cols(w[:, o:o + n], width or n)

    qn = w[:, ORIG["q_nsa"][0]:ORIG["q_nsa"][0] + 768].reshape(-1, NSA_HEADS, NSA_DK)
    qn = jnp.pad(qn, ((0, 0), (0, 0), (0, 256 - NSA_DK))).reshape(-1, NSA_HEADS * 256)
    kr = seg("k_rope")
    zeros = jnp.zeros((w.shape[0], PAD["z"] - (PAD["q_mem"] + 512)), w.dtype)
    return jnp.concatenate(
        [seg("c_q"), seg("c_kv"), qn, seg("k_c", 256), seg("k_s", 256), seg("k_w", 256), kr, kr,
         seg("v_c"), seg("v_s"), seg("v_w"), seg("g_nsa", LANE), seg("q_mem"), zeros,
         seg("z_mla"), seg("z_nsa"), seg("z_mem")], axis=1)


def _w_in_unpadded(g):
    def seg(name, n):
        return g[:, PAD[name]:PAD[name] + n]

    qn = g[:, PAD["q_nsa"]:PAD["q_nsa"] + 1024].reshape(-1, NSA_HEADS, 256)[:, :, :NSA_DK].reshape(-1, 768)
    z = PAD["z"]
    return jnp.concatenate(
        [seg("c_q", 512), seg("c_kv", 512), seg("k_rope", 64), g[:, z:z + 1024], qn, seg("k_c", 192),
         seg("v_c", 128), seg("k_s", 192), seg("v_s", 128), seg("k_w", 192), seg("v_w", 128),
         seg("g_nsa", 12), g[:, z + 1024:z + 1536], seg("q_mem", 512), g[:, z + 1536:z + 2048]], axis=1)


def _rope_tables(s):
    pos = jnp.arange(s, dtype=F32)
    inv_freq = ROPE_THETA ** (-jnp.arange(0, 64, 2, dtype=F32) / 64)
    ang = pos[:, None] * inv_freq[None, :]
    cos, sin = jnp.cos(ang), jnp.sin(ang)
    z = jnp.zeros((s, 64), F32)
    return jnp.concatenate([cos, cos, z], axis=1), jnp.concatenate([-sin, sin, z], axis=1)


def _overlap_table(s):
    n_c, n_s = s // CMP_STRIDE, s // SLC_LEN
    c0 = np.arange(n_c)[:, None] * CMP_STRIDE
    s0 = np.arange(LANE)[None, :] * SLC_LEN
    ov = (c0 < s0 + SLC_LEN) & (c0 + CMP_LEN > s0) & (np.arange(n_c)[:, None] < n_c - 1) & (np.arange(LANE)[None, :] < n_s)
    return jnp.asarray(ov.astype(np.float32), dtype=BF16)


def _shift_down(a):
    return jnp.concatenate([jnp.zeros((8, a.shape[1]), a.dtype), a], axis=0)[7:7 + a.shape[0]]


def _shift_up(a):
    return jnp.concatenate([a, jnp.zeros((8, a.shape[1]), a.dtype)], axis=0)[1:1 + a.shape[0]]


def _local_step(x, mem, target, w):
    s = x.shape[0]
    cs, sn = _rope_tables(s)
    t_ = jnp.transpose

    w_in_p = _w_in_padded(w["w_in"])
    xn, rstd_x = _rms_fwd(_Src(x, D_MODEL), w["norm_g"], "norm_x")
    hp, hpb = _mm(xn, w_in_p, "in_proj", second_dtype=BF16)

    w_uq3 = w["w_uq"].reshape(512, MLA_HEADS, 192)
    w_uq_p = jnp.concatenate([w_uq3, w_uq3[:, :, 128:]], axis=2).reshape(512, MLA_HEADS * 256)
    w_ukv_p = t_(w["w_ukv"].reshape(512, MLA_HEADS, 2, 128), (0, 2, 1, 3)).reshape(512, 2048)
    c_q, c_kv = _Src(hp, 512, 0), _Src(hp, 512, 1)
    cqn, rstd_q = _rms_fwd(c_q, w["q_norm_g"], "norm_q")
    ckvn, rstd_kv = _rms_fwd(c_kv, w["kv_norm_g"], "norm_kv")
    q_lin = _mm(cqn, w_uq_p, "mla_q_proj")
    kvb = _mm(ckvn, w_ukv_p, "mla_kv_proj", out_dtype=BF16)
    q_mla = _rope_fwd(_Src(q_lin, MLA_HEADS * 256), cs, sn, MLA_HEADS, 256, LANE, "rope_q")
    k_pe = _rope_fwd(_Src(hp, LANE, PAD["k_rope"] // LANE), cs, sn, 1, LANE, 0, "rope_k")
    mla = _Attn("mla", s, s, MLA_HEADS, 256)
    mla_q, mla_v = _Src(q_mla, 256), _Src(kvb, LANE, MLA_HEADS)
    mla_k = [_Src(kvb, LANE), _Src(k_pe, LANE, 0, False)]
    o_mla, l_mla, lr_mla = _attn_fwd(mla, mla_q, mla_k, mla_v, None, "mla_fwd")

    sk = s // CMP_STRIDE
    pe_k, pe_v = w["cmp_pe_k"], w["cmp_pe_v"]
    w1k = _pad_cols(w["cmp_w1k"], 256)
    w2k = jnp.pad(w["cmp_w2k"], ((0, 64), (0, 64))).astype(BF16)
    w1v, w2v = w["cmp_w1v"], w["cmp_w2v"].astype(BF16)
    half_k, half_v = CMP_STRIDE * NSA_DK, CMP_STRIDE * HEAD_V
    ak = hp[:, PAD["k_c"]:PAD["k_c"] + NSA_DK].reshape(sk, half_k)
    av = hp[:, PAD["v_c"]:PAD["v_c"] + HEAD_V].reshape(sk, half_v)
    ck_args = (ak, _shift_up(ak), pe_k[:CMP_STRIDE].reshape(1, half_k), pe_k[CMP_STRIDE:].reshape(1, half_k),
               w1k[:half_k], w1k[half_k:], w2k)
    cv_args = (av, _shift_up(av), pe_v[:CMP_STRIDE].reshape(1, half_v), pe_v[CMP_STRIDE:].reshape(1, half_v),
               w1v[:half_v], w1v[half_v:], w2v)
    k_cmp, pre_k = _compress_fwd(*ck_args, "compress_k")
    v_cmp, pre_v = _compress_fwd(*cv_args, "compress_v")
    cmp_ = _Attn("cmp", s, sk, NSA_HEADS, 256)
    slc = _Attn("slc", s, s, NSA_HEADS, 256)
    win = _Attn("win", s, s, NSA_HEADS, 256)
    nsa_q = _Src(hpb, 256, PAD["q_nsa"] // 256)
    cmp_k, cmp_v = [_Src(k_cmp, 256, 0, False)], _Src(v_cmp, HEAD_V, 0, False)
    slc_k, slc_v = [_Src(hpb, 256, PAD["k_s"] // 256, False)], _Src(hpb, HEAD_V, PAD["v_s"] // HEAD_V, False)
    win_k, win_v = [_Src(hpb, 256, PAD["k_w"] // 256, False)], _Src(hpb, HEAD_V, PAD["v_w"] // HEAD_V, False)
    o_cmp, l_cmp, lr_cmp = _attn_fwd(cmp_, nsa_q, cmp_k, cmp_v, None, "cmp_fwd")
    sel, selt = _select(cmp_, _Src(hpb, NSA_HEADS * 256, PAD["q_nsa"] // (NSA_HEADS * 256)), k_cmp,
                        _overlap_table(s))
    o_slc, l_slc, lr_slc = _attn_fwd(slc, nsa_q, slc_k, slc_v, sel, "slc_fwd")
    o_win, l_win, lr_win = _attn_fwd(win, nsa_q, win_k, win_v, None, "win_fwd")
    gl = _Src(hp, LANE, PAD["g_nsa"] // LANE)
    o_nsa = _nsa_combine(o_cmp, o_slc, o_win, gl)

    mn, rstd_m = _rms_fwd(_Src(mem, D_MODEL), w["mem_norm_g"], "norm_mem")
    kvm = _mm(mn, w["w_mem_kv"], "mem_kv_proj", out_dtype=BF16)
    mem_ = _Attn("mem", s, mem.shape[0], MEM_HEADS, LANE)
    mem_q, mem_k, mem_v = _Src(hpb, LANE, PAD["q_mem"] // LANE), [_Src(kvm, LANE)], _Src(kvm, LANE, MEM_HEADS)
    o_mem, l_mem, lr_mem = _attn_fwd(mem_, mem_q, mem_k, mem_v, None, "mem_fwd")

    u = _gate_fwd(o_mla, o_nsa, o_mem, hp)
    proj = _mm(u, w["w_out"], "out_proj")
    dy, g_final, loss = _final_loss(x, proj, w["final_norm_g"].reshape(1, -1), target)

    g_w_out = _mm(u, dy, "out_proj_dw", mode="tn")
    du = _mm(dy, w["w_out"], "out_proj_dx", mode="nt")
    do_cat, dz = _gate_bwd(du, o_mla, o_nsa, o_mem, hp)

    dq_mem, (dk_mem,), dv_mem = _attn_bwd(mem_, mem_q, mem_k, mem_v, None, None, _Src(o_mem, HEAD_V), l_mem,
                                          lr_mem, _Src(do_cat, HEAD_V, 12), None, "mem_bwd")
    dkvm = jnp.concatenate([dk_mem, dv_mem], axis=1)
    g_w_mem_kv = _mm(mn, dkvm, "mem_kv_dw", mode="tn")
    dmn = _mm(dkvm, w["w_mem_kv"], "mem_kv_dx", mode="nt")
    _, g_mem_norm = _rms_bwd(_Src(mem, D_MODEL), w["mem_norm_g"], rstd_m, dmn, None, "norm_mem_bwd")

    do_cmp, do_slc, do_win, dgl = _nsa_combine_bwd(do_cat, o_cmp, o_slc, o_win, gl)
    dq_n, (dk_cmp,), dv_cmp = _attn_bwd(cmp_, nsa_q, cmp_k, cmp_v, None, None, _Src(o_cmp, HEAD_V), l_cmp,
                                        lr_cmp, _Src(do_cmp, HEAD_V), None, "cmp_bwd")
    dq_n, (dk_s,), dv_s = _attn_bwd(slc, nsa_q, slc_k, slc_v, sel, selt, _Src(o_slc, HEAD_V), l_slc, lr_slc,
                                    _Src(do_slc, HEAD_V), dq_n, "slc_bwd")
    dq_n, (dk_w,), dv_w = _attn_bwd(win, nsa_q, win_k, win_v, None, None, _Src(o_win, HEAD_V), l_win, lr_win,
                                    _Src(do_win, HEAD_V), dq_n, "win_bwd")
    dak, dpk_lo, dpk_hi, dw1k_lo, dw1k_hi, g_w2k = _compress_bwd(
        *ck_args, pre_k, _shift_down(pre_k), dk_cmp, _shift_down(dk_cmp), "compress_k_bwd")
    dav, dpv_lo, dpv_hi, dw1v_lo, dw1v_hi, g_w2v = _compress_bwd(
        *cv_args, pre_v, _shift_down(pre_v), dv_cmp, _shift_down(dv_cmp), "compress_v_bwd")
    g_pe_k = jnp.concatenate([dpk_lo.reshape(CMP_STRIDE, NSA_DK), dpk_hi.reshape(CMP_STRIDE, NSA_DK)], axis=0)
    g_pe_v = jnp.concatenate([dpv_lo.reshape(CMP_STRIDE, HEAD_V), dpv_hi.reshape(CMP_STRIDE, HEAD_V)], axis=0)
    g_w1k = jnp.concatenate([dw1k_lo, dw1k_hi], axis=0)[:, :NSA_DK]
    g_w1v = jnp.concatenate([dw1v_lo, dw1v_hi], axis=0)
    dk_c = _pad_cols(dak.reshape(s, NSA_DK), 256)
    dv_c = dav.reshape(s, HEAD_V)

    dq_m, (dk_nope, dk_pe), dv_m = _attn_bwd(mla, mla_q, mla_k, mla_v, None, None, _Src(o_mla, HEAD_V), l_mla,
                                             lr_mla, _Src(do_cat, HEAD_V), None, "mla_bwd")
    dq_lin = _rope_bwd_q(dq_m, cs, sn)
    dkv_lin, d_krope = _rope_bwd_k(dk_nope, dk_pe, dv_m, cs, sn)
    g_w_uq_p = _mm(cqn, dq_lin, "mla_q_dw", mode="tn")
    dcqn = _mm(dq_lin, w_uq_p, "mla_q_dx", mode="nt")
    g_w_ukv_p = _mm(ckvn, dkv_lin, "mla_kv_dw", mode="tn")
    dckvn = _mm(dkv_lin, w_ukv_p, "mla_kv_dx", mode="nt")
    dc_q, g_q_norm = _rms_bwd(c_q, w["q_norm_g"], rstd_q, dcqn, None, "norm_q_bwd")
    dc_kv, g_kv_norm = _rms_bwd(c_kv, w["kv_norm_g"], rstd_kv, dckvn, None, "norm_kv_bwd")
    g_w_uq = g_w_uq_p.reshape(512, MLA_HEADS, 256)[:, :, :192].reshape(512, MLA_HEADS * 192)
    g_w_ukv = t_(g_w_ukv_p.reshape(512, 2, MLA_HEADS, 128), (0, 2, 1, 3)).reshape(512, 2048)

    dhp = jnp.concatenate(
        [dc_q, dc_kv, dq_n, dk_c, dk_s, dk_w, d_krope, dv_c, dv_s, dv_w, dgl, dq_mem,
         jnp.zeros((s, PAD["z"] - (PAD["q_mem"] + 512)), F32), dz], axis=1)
    g_w_in = _w_in_unpadded(_mm(xn, dhp, "in_proj_dw", mode="tn"))
    dxn = _mm(dhp, w_in_p, "in_proj_dx", mode="nt")
    grad_x, g_norm = _rms_bwd(_Src(x, D_MODEL), w["norm_g"], rstd_x, dxn, dy, "norm_x_bwd")

    grads = dict(norm_g=g_norm, w_in=g_w_in, q_norm_g=g_q_norm, w_uq=g_w_uq, kv_norm_g=g_kv_norm,
                 w_ukv=g_w_ukv, cmp_pe_k=g_pe_k, cmp_pe_v=g_pe_v, cmp_w1k=g_w1k, cmp_w2k=g_w2k[:NSA_DK, :NSA_DK],
                 cmp_w1v=g_w1v, cmp_w2v=g_w2v, mem_norm_g=g_mem_norm, w_mem_kv=g_w_mem_kv, w_out=g_w_out,
                 final_norm_g=g_final.reshape(-1))
    return loss[0, 0], grad_x, grads


def kernel(x, mem, norm_g, w_in, q_norm_g, w_uq, kv_norm_g, w_ukv, cmp_pe_k, cmp_pe_v, cmp_w1k, cmp_w2k, cmp_w1v, cmp_w2v, mem_norm_g, w_mem_kv, w_out, final_norm_g, loss_target, m_norm_g, m_w_in, m_q_norm_g, m_w_uq, m_kv_norm_g, m_w_ukv, m_cmp_pe_k, m_cmp_pe_v, m_cmp_w1k, m_cmp_w2k, m_cmp_w1v, m_cmp_w2v, m_mem_norm_g, m_w_mem_kv, m_w_out, m_final_norm_g, v_norm_g, v_w_in, v_q_norm_g, v_w_uq, v_kv_norm_g, v_w_ukv, v_cmp_pe_k, v_cmp_pe_v, v_cmp_w1k, v_cmp_w2k, v_cmp_w1v, v_cmp_w2v, v_mem_norm_g, v_w_mem_kv, v_w_out, v_final_norm_g):
    args = dict(locals())
    wts = {n: args[n] for n in WEIGHTS}
    loc = {n: (a if n == "final_norm_g" else a[0]) for n, a in wts.items()}

    own = [loc[n].astype(BF16) for n in SHARDED]
    chip = 2 * lax.axis_index("x") + lax.axis_index("y")
    gathered = [lax.dynamic_update_slice(gw, a[None], (chip, 0, 0)) for gw, a in zip(_gather_shards(own), own)]
    full = {n: loc[n].reshape(1, -1) if loc[n].ndim == 1 else loc[n] for n in REPLICATED}
    for n, gw in zip(SHARDED, gathered):
        if SHARD_AXIS[n] == 0:
            full[n] = gw.reshape(4 * gw.shape[1], gw.shape[2])
        else:
            full[n] = jnp.concatenate([gw[j] for j in range(4)], axis=1)

    loss, grad_x, g = _local_step(x[0], mem[0], loss_target[0], full)
    loss = lax.psum(loss, ("x", "y", "c"))

    def slots(n):
        a = g[n]
        if SHARD_AXIS[n] == 0:
            return a.reshape(4, a.shape[0] // 4, a.shape[1])
        width = a.shape[1] // 4
        return jnp.stack([a[:, j * width:(j + 1) * width] for j in range(4)])

    gs = [slots(n) for n in SHARDED]
    core = lax.axis_index("c").astype(jnp.int32).reshape(1)
    theirs = _pair_exchange(gs)
    pairs = [_pair_sum(a, b, core, "pair_sum_" + n) for n, a, b in zip(SHARDED, gs, theirs)]
    from_chips = _chip_exchange(pairs)
    mine = [_sum_slots(b, "chip_sum_" + n) for n, b in zip(SHARDED, from_chips)]
    g_sh = [lax.dynamic_update_slice(o, t, (core[0] * t.shape[0], 0)) for o, t in zip(_half_exchange(mine), mine)]

    n_rep = sum(int(np.prod(loc[n].shape)) for n in REPLICATED)
    rows_rep = -(-n_rep // (8 * LANE)) * 8

    def rep_pack(parts):
        flat = jnp.concatenate([p.reshape(-1) for p in parts])
        return jnp.pad(flat, (0, rows_rep * LANE - n_rep)).reshape(rows_rep, LANE)

    g_rep = _sum_slots(_gather_all(rep_pack([g[n] for n in REPLICATED])), "replica_sum")
    d_rp, m_rp, v_rp = _adamw(rep_pack([wts[n] for n in REPLICATED]), g_rep,
                              rep_pack([args["m_" + n] for n in REPLICATED]),
                              rep_pack([args["v_" + n] for n in REPLICATED]), "adamw_replicated")

    def rep_unpack(buf):
        flat, out, o = buf.reshape(-1), {}, 0
        for n in REPLICATED:
            size = int(np.prod(wts[n].shape))
            out[n] = flat[o:o + size].reshape(wts[n].shape)
            o += size
        return out

    outs = {k: rep_unpack(b) for k, b in (("g", g_rep), ("d", d_rp), ("m", m_rp), ("v", v_rp))}
    for n, gn in zip(SHARDED, g_sh):
        d, mo, vo = _adamw(loc[n], gn, args["m_" + n][0], args["v_" + n][0], "adamw_" + n)
        for k, a in (("g", gn), ("d", d), ("m", mo), ("v", vo)):
            outs[k][n] = a.reshape(wts[n].shape)

    return (loss, grad_x[None], *[outs["g"][n] for n in WEIGHTS], *[outs["d"][n] for n in WEIGHTS],
            *[outs["m"][n] for n in WEIGHTS], *[outs["v"][n] for n in WEIGHTS])
```

```python
from typing import NamedTuple

import numpy as np
import jax
import jax.numpy as jnp
from jax import lax
from jax.experimental import pallas as pl
from jax.experimental.pallas import tpu as pltpu

F32 = jnp.float32
BF16 = jnp.bfloat16
MESH = pl.DeviceIdType.MESH

D_MODEL = 2048
EPS = 1e-6
LANE = 128
HEAD_V = 128
MLA_HEADS = 8
NSA_HEADS = 4
MEM_HEADS = 4
NSA_DK = 192
CMP_STRIDE = 16
CMP_LEN = 32
SLC_LEN = 64
SLC_TOPN = 16
WIN = 512
NEG = -1e30
ROPE_THETA = 10000.0
BLOCK_BYTES = 2 << 20

ORIG = dict(c_q=(0, 512), c_kv=(512, 512), k_rope=(1024, 64), z_mla=(1088, 1024),
            q_nsa=(2112, 768), k_c=(2880, 192), v_c=(3072, 128), k_s=(3200, 192),
            v_s=(3392, 128), k_w=(3520, 192), v_w=(3712, 128), g_nsa=(3840, 12),
            z_nsa=(3852, 512), q_mem=(4364, 512), z_mem=(4876, 512))
PAD = dict(c_q=0, c_kv=512, q_nsa=1024, k_c=2048, k_s=2304, k_w=2560, k_rope=2816, v_c=2944,
           v_s=3072, v_w=3200, g_nsa=3328, q_mem=3456, z=4096)
D_PAD = 6144

ADAM_LR, ADAM_B1, ADAM_B2, ADAM_EPS, ADAM_WD, ADAM_STEP = 0.001, 0.9, 0.999, 1e-08, 0.01, 10

SHARDED = ("w_in", "w_uq", "w_ukv", "cmp_w1k", "cmp_w1v", "w_mem_kv", "w_out")
SHARD_AXIS = dict(w_in=1, w_uq=1, w_ukv=1, cmp_w1k=0, cmp_w1v=0, w_mem_kv=0, w_out=0)
REPLICATED = ("norm_g", "q_norm_g", "kv_norm_g", "cmp_pe_k", "cmp_pe_v", "cmp_w2k", "cmp_w2v",
              "mem_norm_g", "final_norm_g")
WEIGHTS = ("norm_g", "w_in", "q_norm_g", "w_uq", "kv_norm_g", "w_ukv", "cmp_pe_k", "cmp_pe_v",
           "cmp_w1k", "cmp_w2k", "cmp_w1v", "cmp_w2v", "mem_norm_g", "w_mem_kv", "w_out",
           "final_norm_g")


def _pcall(kernel, **kw):
    return pl.pallas_call(kernel, **kw)


def _tile(n, pref):
    if n <= pref:
        return n
    for t in range(pref, LANE - 1, -LANE):
        if n % t == 0:
            return t
    raise ValueError((n, pref))


def _row_tile(rows, cols, itemsize=4):
    want = max(16, BLOCK_BYTES // (cols * itemsize))
    if rows <= want:
        return rows
    t = 16
    best = rows
    while t <= want:
        if rows % t == 0:
            best = t
        t *= 2
    return best


def _nt(a, b):
    return lax.dot_general(a, b, (((1,), (1,)), ((), ())), preferred_element_type=F32)


def _tn(a, b):
    return lax.dot_general(a, b, (((0,), (0,)), ((), ())), preferred_element_type=F32)


def _nn(a, b):
    return jnp.dot(a, b, preferred_element_type=F32)


def _sigmoid(x):
    return 1.0 / (1.0 + jnp.exp(-x))


class _Src(NamedTuple):
    arr: jax.Array
    width: int
    col0: int = 0
    per_head: bool = True

    def col(self, h):
        return self.col0 + h if self.per_head else self.col0


def _mm(a, b, name, mode="nn", out_dtype=F32, second_dtype=None):
    if mode == "tn":
        k, m = a.shape
    else:
        m, k = a.shape
    if mode == "nt":
        n, k2 = b.shape
    else:
        k2, n = b.shape
    assert k == k2, (a.shape, b.shape, mode)
    tm, tn, tk = _tile(m, 1024), _tile(n, 1024), _tile(k, 2048)
    nk = k // tk
    assert nk == 1 or (out_dtype == F32 and second_dtype is None)
    dot = {"nn": _nn, "nt": _nt, "tn": _tn}[mode]

    def kern(a_ref, b_ref, o_ref, *more):
        r = dot(a_ref[...].astype(BF16), b_ref[...].astype(BF16))
        if nk == 1:
            o_ref[...] = r.astype(out_dtype)
            if more:
                more[0][...] = r.astype(second_dtype)
        else:
            kk = pl.program_id(2)

            @pl.when(kk == 0)
            def _():
                o_ref[...] = r

            @pl.when(kk > 0)
            def _():
                o_ref[...] += r

    a_spec = (pl.BlockSpec((tk, tm), lambda i, j, kk: (kk, i)) if mode == "tn"
              else pl.BlockSpec((tm, tk), lambda i, j, kk: (i, kk)))
    b_spec = (pl.BlockSpec((tn, tk), lambda i, j, kk: (j, kk)) if mode == "nt"
              else pl.BlockSpec((tk, tn), lambda i, j, kk: (kk, j)))
    o_spec = pl.BlockSpec((tm, tn), lambda i, j, kk: (i, j))
    out_shape = jax.ShapeDtypeStruct((m, n), out_dtype)
    if second_dtype is not None:
        o_spec = [o_spec, o_spec]
        out_shape = [out_shape, jax.ShapeDtypeStruct((m, n), second_dtype)]
    return _pcall(
        kern, name=name, grid=(m // tm, n // tn, nk), in_specs=[a_spec, b_spec], out_specs=o_spec,
        out_shape=out_shape,
        compiler_params=pltpu.CompilerParams(dimension_semantics=("parallel", "parallel", "arbitrary")),
    )(a, b)


def _rms_fwd(x, g, name):
    r, d = x.arr.shape[0], x.width
    tr = _tile(r, 512)

    def kern(x_ref, g_ref, y_ref, r_ref):
        xv = x_ref[...]
        rstd = lax.rsqrt(jnp.mean(xv * xv, axis=-1, keepdims=True) + EPS)
        y_ref[...] = (xv * rstd * g_ref[...]).astype(BF16)
        r_ref[...] = rstd

    return _pcall(
        kern, name=name, grid=(r // tr,),
        in_specs=[pl.BlockSpec((tr, d), lambda i: (i, x.col0)), pl.BlockSpec((1, d), lambda i: (0, 0))],
        out_specs=[pl.BlockSpec((tr, d), lambda i: (i, 0)), pl.BlockSpec((tr, 1), lambda i: (i, 0))],
        out_shape=[jax.ShapeDtypeStruct((r, d), BF16), jax.ShapeDtypeStruct((r, 1), F32)],
    )(x.arr, g)


def _rms_bwd(x, g, rstd, dy, add, name):
    r, d = x.arr.shape[0], x.width
    tr = _tile(r, 256)
    has_add = add is not None

    def kern(*refs):
        if has_add:
            x_ref, g_ref, r_ref, dy_ref, add_ref, dx_ref, dg_ref = refs
        else:
            x_ref, g_ref, r_ref, dy_ref, dx_ref, dg_ref = refs
        rs = r_ref[...]
        xhat = x_ref[...] * rs
        dyv = dy_ref[...]
        dyg = dyv * g_ref[...]
        c = jnp.mean(dyg * xhat, axis=-1, keepdims=True)
        dx = rs * (dyg - xhat * c)
        if has_add:
            dx = dx + add_ref[...]
        dx_ref[...] = dx
        part = jnp.sum(dyv * xhat, axis=0, keepdims=True)

        @pl.when(pl.program_id(0) == 0)
        def _():
            dg_ref[...] = part

        @pl.when(pl.program_id(0) > 0)
        def _():
            dg_ref[...] += part

    row = pl.BlockSpec((tr, d), lambda i: (i, 0))
    vec = pl.BlockSpec((1, d), lambda i: (0, 0))
    ins = [pl.BlockSpec((tr, d), lambda i: (i, x.col0)), vec, pl.BlockSpec((tr, 1), lambda i: (i, 0)), row]
    ins += [row] if has_add else []
    args = (x.arr, g, rstd, dy) + ((add,) if has_add else ())
    return _pcall(
        kern, name=name, grid=(r // tr,), in_specs=ins, out_specs=[row, vec],
        out_shape=[jax.ShapeDtypeStruct((r, d), F32), jax.ShapeDtypeStruct((1, d), F32)],
        compiler_params=pltpu.CompilerParams(dimension_semantics=("arbitrary",)),
    )(*args)


def _final_loss(x, proj, g, target):
    r, d = x.shape
    tr = _tile(r, 256)

    def kern(x_ref, p_ref, g_ref, t_ref, dy_ref, dg_ref, loss_ref):
        y = x_ref[...] + p_ref[...]
        rs = lax.rsqrt(jnp.mean(y * y, axis=-1, keepdims=True) + EPS)
        yhat = y * rs
        gv = g_ref[...]
        e = yhat * gv - t_ref[...]
        lpart = 0.5 * jnp.sum(jnp.mean(e * e, axis=-1, keepdims=True), axis=0, keepdims=True)
        dout = e * (1.0 / d)
        dyg = dout * gv
        c = jnp.mean(dyg * yhat, axis=-1, keepdims=True)
        dy_ref[...] = rs * (dyg - yhat * c)
        gpart = jnp.sum(dout * yhat, axis=0, keepdims=True)
        lrow = jnp.broadcast_to(lpart, (1, LANE))

        @pl.when(pl.program_id(0) == 0)
        def _():
            dg_ref[...] = gpart
            loss_ref[...] = lrow

        @pl.when(pl.program_id(0) > 0)
        def _():
            dg_ref[...] += gpart
            loss_ref[...] += lrow

    row = pl.BlockSpec((tr, d), lambda i: (i, 0))
    vec = pl.BlockSpec((1, d), lambda i: (0, 0))
    return _pcall(
        kern, name="final_loss", grid=(r // tr,), in_specs=[row, row, vec, row],
        out_specs=[row, vec, pl.BlockSpec((1, LANE), lambda i: (0, 0))],
        out_shape=[jax.ShapeDtypeStruct((r, d), F32), jax.ShapeDtypeStruct((1, d), F32),
                   jax.ShapeDtypeStruct((1, LANE), F32)],
        compiler_params=pltpu.CompilerParams(dimension_semantics=("arbitrary",)),
    )(x, proj, g, target)


def _rope_fwd(x, cs, sn, nh, width, off, name):
    s = x.arr.shape[0]
    tr = _tile(s, 512)

    def kern(x_ref, c_ref, s_ref, o_ref):
        cv, sv = c_ref[...], s_ref[...]
        for h in range(nh):
            b = h * width
            if off:
                o_ref[:, b:b + off] = x_ref[:, b:b + off].astype(BF16)
            xr = x_ref[:, b + off:b + off + LANE]
            o_ref[:, b + off:b + off + LANE] = (xr * cv + pltpu.roll(xr, 32, 1) * sv).astype(BF16)

    tab = pl.BlockSpec((tr, LANE), lambda i: (i, 0))
    return _pcall(
        kern, name=name, grid=(s // tr,),
        in_specs=[pl.BlockSpec((tr, nh * width), lambda i: (i, x.col0)), tab, tab],
        out_specs=pl.BlockSpec((tr, nh * width), lambda i: (i, 0)),
        out_shape=jax.ShapeDtypeStruct((s, nh * width), BF16),
    )(x.arr, cs, sn)


def _rope_grad(d, cv, sv):
    g2 = d * sv
    g2 = g2 + pltpu.roll(g2, 64, 1)
    lane = lax.broadcasted_iota(jnp.int32, d.shape, 1)
    return jnp.where(lane < 64, d * cv + pltpu.roll(g2, 32, 1), 0.0)


def _rope_bwd_q(dq, cs, sn):
    s, w = dq.shape
    tr = _tile(s, 512)
    nh = w // 256

    def kern(d_ref, c_ref, s_ref, o_ref):
        cv, sv = c_ref[...], s_ref[...]
        for h in range(nh):
            b = h * 256
            o_ref[:, b:b + LANE] = d_ref[:, b:b + LANE]
            o_ref[:, b + LANE:b + 256] = _rope_grad(d_ref[:, b + LANE:b + 256], cv, sv)

    row = pl.BlockSpec((tr, w), lambda i: (i, 0))
    tab = pl.BlockSpec((tr, LANE), lambda i: (i, 0))
    return _pcall(kern, name="rope_bwd_q", grid=(s // tr,), in_specs=[row, tab, tab], out_specs=row,
                  out_shape=jax.ShapeDtypeStruct((s, w), F32))(dq, cs, sn)


def _rope_bwd_k(dk_nope, dk_pe, dv, cs, sn):
    s, w = dk_nope.shape
    tr = _tile(s, 512)

    def kern(dk_ref, dp_ref, dv_ref, c_ref, s_ref, okv_ref, okr_ref):
        okv_ref[:, :w] = dk_ref[...]
        okv_ref[:, w:] = dv_ref[...]
        okr_ref[...] = _rope_grad(dp_ref[...], c_ref[...], s_ref[...])

    tab = pl.BlockSpec((tr, LANE), lambda i: (i, 0))
    wide = pl.BlockSpec((tr, w), lambda i: (i, 0))
    return _pcall(
        kern, name="rope_bwd_k", grid=(s // tr,), in_specs=[wide, tab, wide, tab, tab],
        out_specs=[pl.BlockSpec((tr, 2 * w), lambda i: (i, 0)), tab],
        out_shape=[jax.ShapeDtypeStruct((s, 2 * w), F32), jax.ShapeDtypeStruct((s, LANE), F32)],
    )(dk_nope, dk_pe, dv, cs, sn)


class _Attn:
    def __init__(self, mode, s, sk, heads, dk):
        self.mode, self.s, self.sk, self.h, self.dk = mode, s, sk, heads, dk
        self.scale = {"mla": 192 ** -0.5, "mem": 128 ** -0.5}.get(mode, NSA_DK ** -0.5)
        self.tb = min(256, s)
        self.nb = s // self.tb
        self.nsub = 2 if self.nb % 2 == 0 else 1
        self.tq = self.tb * self.nsub
        self.nq = s // self.tq
        self.causal = mode in ("mla", "slc")
        if self.causal:
            self.tk = self.tq
        elif mode == "win":
            self.tk = WIN + self.tb
        else:
            self.tk = sk
        self.tkb = min(512, sk)
        self.ksub = 2 if self.tkb == 512 else 1
        self.kb = self.tkb // self.ksub
        self.ncmp = s // CMP_STRIDE - 1

    def mask_bias(self, t, n, h, selx, diag):
        m = self.mode
        if m == "mla":
            return (n <= t) if diag else None, None
        if m == "mem":
            return None, None
        slope = jnp.where(h == 0, 0.25, jnp.where(h == 1, 0.0625, jnp.where(h == 2, 0.015625, 0.00390625)))
        slope = slope.astype(F32)
        if m == "cmp":
            mask = (n * CMP_STRIDE + (CMP_LEN - 1) <= t) & (n < self.ncmp)
            pos = n.astype(F32) * float(CMP_STRIDE) + (CMP_LEN - 1) / 2.0
            return mask, -slope * (t.astype(F32) - pos)
        rel = t - n
        bias = -slope * rel.astype(F32)
        if m == "slc":
            mask = selx > 0.5
            return (mask & (rel >= 0)) if diag else mask, bias
        return (rel >= 0) & (rel < WIN), bias


def _scores(cfg, s_raw, t, n, h, selx, diag, lse=None):
    s = s_raw * cfg.scale
    mask, bias = cfg.mask_bias(t, n, h, selx, diag)
    if bias is not None:
        s = s + bias
    if lse is None:
        if mask is not None:
            s = jnp.where(mask, s, NEG)
        return s, mask
    p = jnp.exp(jnp.minimum(s - lse, 0.0))
    if mask is not None:
        p = jnp.where(mask, p, 0.0)
    return p, mask


def _block_of_key(k0, tk, keys_on_rows):
    shape = (tk, LANE) if keys_on_rows else (LANE, tk)
    n = lax.broadcasted_iota(jnp.int32, shape, 0 if keys_on_rows else 1) + k0
    j = lax.broadcasted_iota(jnp.int32, shape, 1 if keys_on_rows else 0)
    return jnp.where((n >> 6) == j, 1.0, 0.0).astype(BF16)


def _to_row(col):
    t = col.shape[0]
    return jnp.transpose(jnp.broadcast_to(col, (t, LANE)))[0:1, :]


def _load_keys(k_refs, rows):
    parts = [r[rows, :].astype(BF16) for r in k_refs]
    return parts[0] if len(parts) == 1 else jnp.concatenate(parts, axis=1)


def _attn_fwd(cfg, q, ks, v, sel, name):
    s, tq, tk, tb, nsub = cfg.s, cfg.tq, cfg.tk, cfg.tb, cfg.nsub
    has_sel = sel is not None
    nkp = len(ks)

    def kern(*refs):
        q_ref, k_refs, v_ref = refs[0], refs[1:1 + nkp], refs[1 + nkp]
        sel_ref = refs[2 + nkp] if has_sel else None
        o_ref, lc_ref, lr_ref = refs[-3:]
        h, i = pl.program_id(0), pl.program_id(1)
        part = [slice(r * tb, (r + 1) * tb) for r in range(nsub)]
        qs = [q_ref[p, :].astype(BF16) for p in part]
        ts = [i * tq + r * tb + lax.broadcasted_iota(jnp.int32, (tb, 1), 0) for r in range(nsub)]
        sels = [sel_ref[p, :].astype(BF16) for p in part] if has_sel else None

        def load(k0):
            rows = pl.ds(k0, tk)
            return _load_keys(k_refs, rows), v_ref[rows, :].astype(BF16)

        def step(r, k0, kk, vv, emat, carry, diag):
            m, l, acc = carry
            n = k0 + lax.broadcasted_iota(jnp.int32, (1, tk), 1)
            selx = _nn(sels[r], emat) if has_sel else None
            sc, mask = _scores(cfg, _nt(qs[r], kk), ts[r], n, h, selx, diag)
            m_new = jnp.maximum(m, jnp.max(sc, axis=1, keepdims=True))
            alpha = jnp.exp(m - m_new)
            p = jnp.exp(sc - m_new)
            if mask is not None:
                p = jnp.where(mask, p, 0.0)
            l = alpha * l + jnp.sum(p, axis=1, keepdims=True)
            acc = alpha * acc + _nn(p.astype(BF16), vv)
            return m_new, l, acc

        def chunk(k0, carry, diag):
            kk, vv = load(k0)
            emat = _block_of_key(k0, tk, False) if has_sel else None
            return tuple(step(r, k0, kk, vv, emat, carry[r], diag) for r in range(nsub))

        init = (jnp.full((tb, 1), NEG, F32), jnp.zeros((tb, 1), F32), jnp.zeros((tb, HEAD_V), F32))
        carry = (init,) * nsub
        if cfg.causal:
            full = (i * tq) // tk
            carry = lax.fori_loop(0, full, lambda c, cr: chunk(pl.multiple_of(c * tk, tk), cr, False), carry)
            carry = chunk(pl.multiple_of(full * tk, tk), carry, True)
        elif cfg.mode == "win":
            starts = [pl.multiple_of(jnp.maximum(i * tq + r * tb - WIN, 0), tb) for r in range(nsub)]
            carry = tuple(step(r, k0, *load(k0), None, carry[r], True) for r, k0 in enumerate(starts))
        else:
            carry = chunk(0, carry, True)
        for r, (m, l, acc) in enumerate(carry):
            o_ref[part[r], :] = acc / (l + 1e-20)
            lse = m + jnp.log(l + 1e-20)
            lc_ref[0, part[r], :] = lse
            lr_ref[0, r] = _to_row(lse)

    ins = [pl.BlockSpec((tq, q.width), lambda h, i: (i, q.col(h)))]
    ins += [pl.BlockSpec((cfg.sk, p.width), lambda h, i, p=p: (0, p.col(h))) for p in ks]
    ins += [pl.BlockSpec((cfg.sk, HEAD_V), lambda h, i: (0, v.col(h)))]
    args = [q.arr] + [p.arr for p in ks] + [v.arr]
    if has_sel:
        ins.append(pl.BlockSpec((tq, LANE), lambda h, i: (i, 0)))
        args.append(sel)
    return _pcall(
        kern, name=name, grid=(cfg.h, cfg.nq), in_specs=ins,
        out_specs=[pl.BlockSpec((tq, HEAD_V), lambda h, i: (i, h)),
                   pl.BlockSpec((1, tq, 1), lambda h, i: (h, i, 0)),
                   pl.BlockSpec((1, nsub, 1, tb), lambda h, i: (h, i, 0, 0))],
        out_shape=[jax.ShapeDtypeStruct((s, cfg.h * HEAD_V), F32),
                   jax.ShapeDtypeStruct((cfg.h, s, 1), F32),
                   jax.ShapeDtypeStruct((cfg.h, cfg.nb, 1, tb), F32)],
        compiler_params=pltpu.CompilerParams(dimension_semantics=("parallel", "parallel")),
    )(*args)


def _attn_dq(cfg, q, ks, v, sel, o, lse, do, dq_in, name):
    s, tq, tk, dk, tb, nsub = cfg.s, cfg.tq, cfg.tk, cfg.dk, cfg.tb, cfg.nsub
    has_sel = sel is not None
    has_in = dq_in is not None
    nkp = len(ks)

    def kern(*refs):
        refs = list(refs)
        q_ref, k_refs, v_ref = refs[0], refs[1:1 + nkp], refs[1 + nkp]
        p0 = 2 + nkp
        sel_ref = refs[p0] if has_sel else None
        p0 += has_sel
        o_ref, l_ref, do_ref = refs[p0:p0 + 3]
        p0 += 3
        in_ref = refs[p0] if has_in else None
        dq_ref, dr_ref = refs[-2:]
        h, i = pl.program_id(0), pl.program_id(1)
        part = [slice(r * tb, (r + 1) * tb) for r in range(nsub)]
        qs = [q_ref[p, :].astype(BF16) for p in part]
        ts = [i * tq + r * tb + lax.broadcasted_iota(jnp.int32, (tb, 1), 0) for r in range(nsub)]
        sels = [sel_ref[p, :].astype(BF16) for p in part] if has_sel else None
        dvecs, dobs, lses = [], [], []
        for r, p in enumerate(part):
            dov = do_ref[p, :]
            dvec = jnp.sum(dov * o_ref[p, :], axis=1, keepdims=True)
            dr_ref[0, r] = _to_row(dvec)
            dvecs.append(dvec)
            dobs.append(dov.astype(BF16))
            lses.append(l_ref[0, p, :])

        def load(k0):
            rows = pl.ds(k0, tk)
            return _load_keys(k_refs, rows), v_ref[rows, :].astype(BF16)

        def step(r, k0, kk, vv, emat, acc, diag):
            n = k0 + lax.broadcasted_iota(jnp.int32, (1, tk), 1)
            selx = _nn(sels[r], emat) if has_sel else None
            p, _ = _scores(cfg, _nt(qs[r], kk), ts[r], n, h, selx, diag, lses[r])
            ds = p * (_nt(dobs[r], vv) - dvecs[r]) * cfg.scale
            return acc + _nn(ds.astype(BF16), kk)

        def chunk(k0, accs, diag):
            kk, vv = load(k0)
            emat = _block_of_key(k0, tk, False) if has_sel else None
            return tuple(step(r, k0, kk, vv, emat, accs[r], diag) for r in range(nsub))

        accs = tuple(in_ref[p, :] if has_in else jnp.zeros((tb, dk), F32) for p in part)
        if cfg.causal:
            full = (i * tq) // tk
            accs = lax.fori_loop(0, full, lambda c, a: chunk(pl.multiple_of(c * tk, tk), a, False), accs)
            accs = chunk(pl.multiple_of(full * tk, tk), accs, True)
        elif cfg.mode == "win":
            starts = [pl.multiple_of(jnp.maximum(i * tq + r * tb - WIN, 0), tb) for r in range(nsub)]
            accs = tuple(step(r, k0, *load(k0), None, accs[r], True) for r, k0 in enumerate(starts))
        else:
            accs = chunk(0, accs, True)
        for r, p in enumerate(part):
            dq_ref[p, :] = accs[r]

    qs = pl.BlockSpec((tq, dk), lambda h, i: (i, h))
    ins = [pl.BlockSpec((tq, q.width), lambda h, i: (i, q.col(h)))]
    ins += [pl.BlockSpec((cfg.sk, p.width), lambda h, i, p=p: (0, p.col(h))) for p in ks]
    ins += [pl.BlockSpec((cfg.sk, HEAD_V), lambda h, i: (0, v.col(h)))]
    args = [q.arr] + [p.arr for p in ks] + [v.arr]
    if has_sel:
        ins.append(pl.BlockSpec((tq, LANE), lambda h, i: (i, 0)))
        args.append(sel)
    ins += [pl.BlockSpec((tq, HEAD_V), lambda h, i: (i, o.col(h))),
            pl.BlockSpec((1, tq, 1), lambda h, i: (h, i, 0)),
            pl.BlockSpec((tq, HEAD_V), lambda h, i: (i, do.col(h)))]
    args += [o.arr, lse, do.arr]
    if has_in:
        ins.append(qs)
        args.append(dq_in)
    return _pcall(
        kern, name=name, grid=(cfg.h, cfg.nq), in_specs=ins,
        out_specs=[qs, pl.BlockSpec((1, nsub, 1, tb), lambda h, i: (h, i, 0, 0))],
        out_shape=[jax.ShapeDtypeStruct((s, cfg.h * dk), F32),
                   jax.ShapeDtypeStruct((cfg.h, cfg.nb, 1, tb), F32)],
        compiler_params=pltpu.CompilerParams(dimension_semantics=("parallel", "parallel")),
    )(*args)


def _attn_dkv(cfg, q, ks, v, selt, lse_r, d_r, do, name):
    s, tq, tkb, dk, kb, ksub = cfg.s, cfg.tb, cfg.tkb, cfg.dk, cfg.kb, cfg.ksub
    nq = cfg.nb
    has_sel = selt is not None
    nkp = len(ks)
    outs = list(ks) + [v]

    def kern(*refs):
        k_refs, v_ref = refs[:nkp], refs[nkp]
        q_ref, do_ref, lr_ref, dr_ref = refs[nkp + 1:nkp + 5]
        st_ref = refs[nkp + 5] if has_sel else None
        out_refs = refs[-(nkp + 1):]
        j, h = pl.program_id(0), pl.program_id(1)
        k0 = j * tkb
        part = [slice(u * kb, (u + 1) * kb) for u in range(ksub)]
        kks = [_load_keys(k_refs, p) for p in part]
        vvs = [v_ref[p, :].astype(BF16) for p in part]
        ns = [k0 + u * kb + lax.broadcasted_iota(jnp.int32, (kb, 1), 0) for u in range(ksub)]
        emats = [_block_of_key(k0 + u * kb, kb, True) for u in range(ksub)] if has_sel else None

        def qblock(i, carry, diag):
            r0 = pl.multiple_of(i * tq, tq)
            qi = q_ref[pl.ds(r0, tq), :].astype(BF16)
            doi = do_ref[pl.ds(r0, tq), :].astype(BF16)
            t = i * tq + lax.broadcasted_iota(jnp.int32, (1, tq), 1)
            selt_i = st_ref[i].astype(BF16) if has_sel else None
            new = []
            for u in range(ksub):
                dk_acc, dv_acc = carry[u]
                selx = _nn(emats[u], selt_i) if has_sel else None
                pt, _ = _scores(cfg, _nt(kks[u], qi), t, ns[u], h, selx, diag, lr_ref[0, i])
                dv_acc = dv_acc + _nn(pt.astype(BF16), doi)
                dst = pt * (_nt(vvs[u], doi) - dr_ref[0, i]) * cfg.scale
                new.append((dk_acc + _nn(dst.astype(BF16), qi), dv_acc))
            return tuple(new)

        carry = ((jnp.zeros((kb, dk), F32), jnp.zeros((kb, HEAD_V), F32)),) * ksub
        masked = lambda i, cr: qblock(i, cr, True)
        if cfg.causal:
            first, past = k0 // tq, (k0 + tkb) // tq
            carry = lax.fori_loop(first, past, masked, carry)
            carry = lax.fori_loop(past, nq, lambda i, cr: qblock(i, cr, False), carry)
        elif cfg.mode == "win":
            carry = lax.fori_loop(k0 // tq, jnp.minimum((k0 + tkb + WIN - 2) // tq + 1, nq), masked, carry)
        else:
            carry = lax.fori_loop(0, nq, masked, carry)
        for u, (dk_acc, dv_acc) in enumerate(carry):
            vals, off = [], 0
            for p in ks:
                vals.append(dk_acc[:, off:off + p.width])
                off += p.width
            vals.append(dv_acc)
            for src, ref, val in zip(outs, out_refs, vals):
                if src.per_head:
                    ref[part[u], :] = val
                else:
                    @pl.when(h == 0)
                    def _(ref=ref, val=val, u=u):
                        ref[part[u], :] = val

                    @pl.when(h > 0)
                    def _(ref=ref, val=val, u=u):
                        ref[part[u], :] += val

    rowv = pl.BlockSpec((1, nq, 1, tq), lambda j, h: (h, 0, 0, 0))
    ins = [pl.BlockSpec((tkb, p.width), lambda j, h, p=p: (j, p.col(h))) for p in ks]
    ins += [pl.BlockSpec((tkb, HEAD_V), lambda j, h: (j, v.col(h))),
            pl.BlockSpec((s, q.width), lambda j, h: (0, q.col(h))),
            pl.BlockSpec((s, HEAD_V), lambda j, h: (0, do.col(h))), rowv, rowv]
    args = [p.arr for p in ks] + [v.arr, q.arr, do.arr, lse_r, d_r]
    if has_sel:
        ins.append(pl.BlockSpec((nq, LANE, tq), lambda j, h: (0, 0, 0)))
        args.append(selt)
    out_specs = [pl.BlockSpec((tkb, p.width), lambda j, h, p=p: (j, h if p.per_head else 0)) for p in outs]
    out_shape = [jax.ShapeDtypeStruct((cfg.sk, (cfg.h if p.per_head else 1) * p.width), F32) for p in outs]
    return _pcall(
        kern, name=name, grid=(cfg.sk // tkb, cfg.h), in_specs=ins, out_specs=out_specs, out_shape=out_shape,
        compiler_params=pltpu.CompilerParams(dimension_semantics=("parallel", "arbitrary")),
    )(*args)


def _attn_bwd(cfg, q, ks, v, sel, selt, o, lse, lse_r, do, dq_in, name):
    dq, d_r = _attn_dq(cfg, q, ks, v, sel, o, lse, do, dq_in, name + "_dq")
    res = _attn_dkv(cfg, q, ks, v, selt, lse_r, d_r, do, name + "_dkv")
    return dq, res[:-1], res[-1]


def _select(cfg, q, k_cmp, overlap):
    s, tq, sk = cfg.s, cfg.tb, cfg.sk
    n_s = s // SLC_LEN
    top_n = min(SLC_TOPN, n_s)

    def kern(q_ref, k_ref, ov_ref, sel_ref, selt_ref):
        i = pl.program_id(0)
        t = i * tq + lax.broadcasted_iota(jnp.int32, (tq, 1), 0)
        n = lax.broadcasted_iota(jnp.int32, (1, sk), 1)
        kk = k_ref[...]
        imp = jnp.zeros((tq, LANE), F32)
        for h in range(NSA_HEADS):
            sc, mask = _scores(cfg, _nt(q_ref[:, h * 256:(h + 1) * 256], kk), t, n, h, None, True)
            m = jnp.max(sc, axis=1, keepdims=True)
            e = jnp.where(mask, jnp.exp(sc - m), 0.0)
            p = e / (jnp.sum(e, axis=1, keepdims=True) + 1e-20)
            imp = imp + _nn(p.astype(BF16), ov_ref[...])
        j = lax.broadcasted_iota(jnp.int32, (tq, LANE), 1)
        cur = t >> 6
        forced = (j == 0) | (j == cur) | (j == cur - 1)
        imp = jnp.where(forced, 1e9, imp)
        imp = jnp.where(j > cur, -1e9, imp)
        imp = jnp.where(j >= n_s, -3e38, imp)

        def pick(_, carry):
            work, chosen = carry
            mx = jnp.max(work, axis=1, keepdims=True)
            first = jnp.min(jnp.where(work == mx, j, LANE), axis=1, keepdims=True)
            hit = j == first
            return jnp.where(hit, -3e38, work), jnp.where(hit, 1.0, chosen)

        _, chosen = lax.fori_loop(0, top_n, pick, (imp, jnp.zeros((tq, LANE), F32)))
        chosen = jnp.where(j <= cur, chosen, 0.0)
        sel_ref[...] = chosen
        selt_ref[0] = jnp.transpose(chosen)

    return _pcall(
        kern, name="nsa_select", grid=(cfg.nb,),
        in_specs=[pl.BlockSpec((tq, NSA_HEADS * 256), lambda i: (i, q.col0)),
                  pl.BlockSpec((sk, 256), lambda i: (0, 0)), pl.BlockSpec((sk, LANE), lambda i: (0, 0))],
        out_specs=[pl.BlockSpec((tq, LANE), lambda i: (i, 0)), pl.BlockSpec((1, LANE, tq), lambda i: (i, 0, 0))],
        out_shape=[jax.ShapeDtypeStruct((s, LANE), F32), jax.ShapeDtypeStruct((cfg.nb, LANE, tq), F32)],
    )(q.arr, k_cmp, overlap)


def _silu_grad(pre):
    sg = _sigmoid(pre)
    return sg * (1.0 + pre * (1.0 - sg))


def _compress_fwd(a_lo, a_hi, pe_lo, pe_hi, w1_lo, w1_hi, w2, name):
    n, dp = a_lo.shape[0], w2.shape[1]

    def kern(alo, ahi, plo, phi, w1l, w1h, w2r, out_ref, pre_ref):
        xl = (alo[...] + plo[...]).astype(BF16)
        xh = (ahi[...] + phi[...]).astype(BF16)
        pre = _nn(xl, w1l[...]) + _nn(xh, w1h[...])
        act = pre * _sigmoid(pre)
        out_ref[...] = _nn(act.astype(BF16), w2r[...]).astype(BF16)
        pre_ref[...] = pre

    return _pcall(kern, name=name,
                  out_shape=[jax.ShapeDtypeStruct((n, dp), BF16), jax.ShapeDtypeStruct((n, dp), F32)],
                  )(a_lo, a_hi, pe_lo, pe_hi, w1_lo, w1_hi, w2)


def _compress_bwd(a_lo, a_hi, pe_lo, pe_hi, w1_lo, w1_hi, w2, pre, pre_sh, dout, dout_sh, name):
    n, ln = a_lo.shape
    dp = w2.shape[1]

    def kern(alo, ahi, plo, phi, w1l, w1h, w2r, pre_ref, presh_ref, do_ref, dosh_ref,
             da_ref, dpl_ref, dph_ref, dw1l_ref, dw1h_ref, dw2_ref):
        prev = pre_ref[...]
        act = prev * _sigmoid(prev)
        dob = do_ref[...].astype(BF16)
        w2v = w2r[...]
        dpre = (_nt(dob, w2v) * _silu_grad(prev)).astype(BF16)
        dpre_sh = (_nt(dosh_ref[...].astype(BF16), w2v) * _silu_grad(presh_ref[...])).astype(BF16)
        dw2_ref[...] = _nn(act.T.astype(BF16), dob)
        xl = alo[...] + plo[...]
        xh = ahi[...] + phi[...]
        dw1l_ref[...] = _nn(xl.T.astype(BF16), dpre)
        dw1h_ref[...] = _nn(xh.T.astype(BF16), dpre)
        dal = _nt(dpre, w1l[...])
        dah_sh = _nt(dpre_sh, w1h[...])
        da_ref[...] = dal + dah_sh
        dpl_ref[...] = jnp.sum(dal, axis=0, keepdims=True)
        dph_ref[...] = jnp.sum(dah_sh, axis=0, keepdims=True)

    return _pcall(
        kern, name=name,
        out_shape=[jax.ShapeDtypeStruct((n, ln), F32), jax.ShapeDtypeStruct((1, ln), F32),
                   jax.ShapeDtypeStruct((1, ln), F32), jax.ShapeDtypeStruct((ln, dp), F32),
                   jax.ShapeDtypeStruct((ln, dp), F32), jax.ShapeDtypeStruct((dp, dp), F32)],
    )(a_lo, a_hi, pe_lo, pe_hi, w1_lo, w1_hi, w2, pre, pre_sh, dout, dout_sh)


def _nsa_combine(o_cmp, o_slc, o_win, gl):
    s, w = o_cmp.shape
    tr = _tile(s, 512)

    def kern(a_ref, b_ref, c_ref, g_ref, o_ref):
        g = _sigmoid(g_ref[...])
        for h in range(NSA_HEADS):
            cs = slice(h * HEAD_V, (h + 1) * HEAD_V)
            o_ref[:, cs] = (g[:, 3 * h:3 * h + 1] * a_ref[:, cs] + g[:, 3 * h + 1:3 * h + 2] * b_ref[:, cs]
                            + g[:, 3 * h + 2:3 * h + 3] * c_ref[:, cs])

    row = pl.BlockSpec((tr, w), lambda i: (i, 0))
    return _pcall(kern, name="nsa_combine", grid=(s // tr,),
                  in_specs=[row, row, row, pl.BlockSpec((tr, LANE), lambda i: (i, gl.col0))], out_specs=row,
                  out_shape=jax.ShapeDtypeStruct((s, w), F32))(o_cmp, o_slc, o_win, gl.arr)


def _nsa_combine_bwd(do_cat, o_cmp, o_slc, o_win, gl):
    s, w = o_cmp.shape
    tr = _tile(s, 512)

    def kern(d_ref, a_ref, b_ref, c_ref, g_ref, da_ref, db_ref, dc_ref, dg_ref):
        g = _sigmoid(g_ref[...])
        lane = lax.broadcasted_iota(jnp.int32, (tr, LANE), 1)
        dgl = jnp.zeros((tr, LANE), F32)
        for h in range(NSA_HEADS):
            cs = slice(h * HEAD_V, (h + 1) * HEAD_V)
            dv = d_ref[:, cs]
            for b, (src, dst) in enumerate(((a_ref, da_ref), (b_ref, db_ref), (c_ref, dc_ref))):
                gate = g[:, 3 * h + b:3 * h + b + 1]
                dst[:, cs] = gate * dv
                dgate = jnp.sum(dv * src[:, cs], axis=1, keepdims=True)
                dgl = jnp.where(lane == 3 * h + b, dgate * gate * (1.0 - gate), dgl)
        dg_ref[...] = dgl

    row = pl.BlockSpec((tr, w), lambda i: (i, 0))
    tab = pl.BlockSpec((tr, LANE), lambda i: (i, 0))
    return _pcall(kern, name="nsa_combine_bwd", grid=(s // tr,),
                  in_specs=[pl.BlockSpec((tr, w), lambda i: (i, 2)), row, row, row,
                            pl.BlockSpec((tr, LANE), lambda i: (i, gl.col0))],
                  out_specs=[row, row, row, tab],
                  out_shape=[jax.ShapeDtypeStruct((s, w), F32)] * 3 + [jax.ShapeDtypeStruct((s, LANE), F32)],
                  )(do_cat, o_cmp, o_slc, o_win, gl.arr)


def _gate_fwd(o_mla, o_nsa, o_mem, hp):
    s = o_mla.shape[0]
    tr = _tile(s, 256)

    def kern(a_ref, b_ref, c_ref, z_ref, u_ref):
        z = z_ref[...]
        sz = z * _sigmoid(z)
        u_ref[:, 0:1024] = (a_ref[...] * sz[:, 0:1024]).astype(BF16)
        u_ref[:, 1024:1536] = (b_ref[...] * sz[:, 1024:1536]).astype(BF16)
        u_ref[:, 1536:2048] = (c_ref[...] * sz[:, 1536:2048]).astype(BF16)

    return _pcall(
        kern, name="gate_fwd", grid=(s // tr,),
        in_specs=[pl.BlockSpec((tr, 1024), lambda i: (i, 0)), pl.BlockSpec((tr, 512), lambda i: (i, 0)),
                  pl.BlockSpec((tr, 512), lambda i: (i, 0)), pl.BlockSpec((tr, 2048), lambda i: (i, 2))],
        out_specs=pl.BlockSpec((tr, 2048), lambda i: (i, 0)),
        out_shape=jax.ShapeDtypeStruct((s, 2048), BF16))(o_mla, o_nsa, o_mem, hp)


def _gate_bwd(du, o_mla, o_nsa, o_mem, hp):
    s = du.shape[0]
    tr = _tile(s, 256)

    def kern(d_ref, a_ref, b_ref, c_ref, z_ref, do_ref, dz_ref):
        z = z_ref[...]
        sg = _sigmoid(z)
        sz = z * sg
        dsz = sg * (1.0 + z * (1.0 - sg))
        d = d_ref[...]
        do_ref[...] = d * sz
        dz_ref[:, 0:1024] = d[:, 0:1024] * a_ref[...] * dsz[:, 0:1024]
        dz_ref[:, 1024:1536] = d[:, 1024:1536] * b_ref[...] * dsz[:, 1024:1536]
        dz_ref[:, 1536:2048] = d[:, 1536:2048] * c_ref[...] * dsz[:, 1536:2048]

    wide = pl.BlockSpec((tr, 2048), lambda i: (i, 0))
    return _pcall(
        kern, name="gate_bwd", grid=(s // tr,),
        in_specs=[wide, pl.BlockSpec((tr, 1024), lambda i: (i, 0)), pl.BlockSpec((tr, 512), lambda i: (i, 0)),
                  pl.BlockSpec((tr, 512), lambda i: (i, 0)), pl.BlockSpec((tr, 2048), lambda i: (i, 2))],
        out_specs=[wide, wide],
        out_shape=[jax.ShapeDtypeStruct((s, 2048), F32)] * 2)(du, o_mla, o_nsa, o_mem, hp)


def _sum_slots(buf, name):
    n, rows, cols = buf.shape
    tr = _row_tile(rows, cols * n)

    def kern(b_ref, o_ref):
        acc = b_ref[0].astype(F32)
        for i in range(1, n):
            acc = acc + b_ref[i].astype(F32)
        o_ref[...] = acc

    return _pcall(kern, name=name, grid=(rows // tr,),
                  in_specs=[pl.BlockSpec((n, tr, cols), lambda i: (0, i, 0))],
                  out_specs=pl.BlockSpec((tr, cols), lambda i: (i, 0)),
                  out_shape=jax.ShapeDtypeStruct((rows, cols), F32))(buf)


def _pair_sum(g4, theirs, core, name):
    n, rows, cols = g4.shape
    half = rows // 2
    tr = _row_tile(half, cols)
    nb = half // tr

    def kern(c_ref, a_ref, b_ref, o_ref):
        o_ref[...] = (a_ref[...] + b_ref[...]).astype(BF16)

    blk = (1, tr, cols)
    grid_spec = pltpu.PrefetchScalarGridSpec(
        num_scalar_prefetch=1, grid=(n, nb),
        in_specs=[pl.BlockSpec(blk, lambda s, i, c: (s, c[0] * nb + i, 0)),
                  pl.BlockSpec(blk, lambda s, i, c: (s, i, 0))],
        out_specs=pl.BlockSpec(blk, lambda s, i, c: (s, i, 0)))
    return _pcall(kern, name=name, grid_spec=grid_spec,
                  out_shape=jax.ShapeDtypeStruct((n, half, cols), BF16))(core, g4, theirs)


def _adamw(w, g, m, v, name):
    rows, cols = w.shape
    tr = _row_tile(rows, cols * 4)
    bc1 = 1.0 - ADAM_B1 ** ADAM_STEP
    bc2 = 1.0 - ADAM_B2 ** ADAM_STEP

    def kern(w_ref, g_ref, m_ref, v_ref, d_ref, mo_ref, vo_ref):
        gv = g_ref[...]
        mn = ADAM_B1 * m_ref[...] + (1.0 - ADAM_B1) * gv
        vn = ADAM_B2 * v_ref[...] + (1.0 - ADAM_B2) * (gv * gv)
        d_ref[...] = -ADAM_LR * ((mn / bc1) / (jnp.sqrt(vn / bc2) + ADAM_EPS) + ADAM_WD * w_ref[...])
        mo_ref[...] = mn
        vo_ref[...] = vn

    row = pl.BlockSpec((tr, cols), lambda i: (i, 0))
    return _pcall(kern, name=name, grid=(rows // tr,), in_specs=[row] * 4, out_specs=[row] * 3,
                  out_shape=[jax.ShapeDtypeStruct((rows, cols), F32)] * 3)(w, g, m, v)


ANY = pl.BlockSpec(memory_space=pl.ANY)


def _place():
    x, y, c = lax.axis_index("x"), lax.axis_index("y"), lax.axis_index("c")
    chips = [(1 - x, y), (x, 1 - y), (1 - x, 1 - y)]
    return x, y, c, chips


def _remote(src, dst, send_sem, recv_sem, to):
    return pltpu.make_async_remote_copy(src_ref=src, dst_ref=dst, send_sem=send_sem, recv_sem=recv_sem,
                                        device_id=to, device_id_type=MESH)


def _gather_shards(ws):
    nw = len(ws)
    halves = [w.shape[0] // 2 for w in ws]

    def body(*refs):
        w_refs, out_refs = refs[:nw], refs[nw:2 * nw]
        send_sems, recv_sems = refs[2 * nw:]
        x, y, c, chips = _place()
        me = 2 * x + y
        sibling = (x, y, 1 - c)

        def part(i, slot, core):
            return out_refs[i].at[slot, pl.ds(core * halves[i], halves[i]), :]

        def copy(sem, src, dst, to):
            return _remote(src, dst, send_sems.at[sem], recv_sems.at[sem], to)

        first = [copy(j * nw + i, w_refs[i].at[pl.ds(c * halves[i], halves[i]), :], part(i, me, c), (*chip, c))
                 for j, chip in enumerate(chips) for i in range(nw)]
        for cp in first:
            cp.start()
        passed = []
        for j, (cx, cy) in enumerate(chips):
            slot = 2 * cx + cy
            for i in range(nw):
                copy(j * nw + i, part(i, slot, c), part(i, slot, c), (x, y, c)).wait_recv()
                fwd = copy((3 + j) * nw + i, part(i, slot, c), part(i, slot, c), sibling)
                fwd.start()
                passed.append(fwd)
        for j, (cx, cy) in enumerate(chips):
            slot = 2 * cx + cy
            for i in range(nw):
                copy((3 + j) * nw + i, part(i, slot, 1 - c), part(i, slot, 1 - c), (x, y, c)).wait_recv()
        for cp in first + passed:
            cp.wait_send()

    return _pcall(
        body, name="gather_shards", in_specs=[ANY] * nw, out_specs=[ANY] * nw,
        out_shape=[jax.ShapeDtypeStruct((4,) + w.shape, w.dtype) for w in ws],
        scratch_shapes=[pltpu.SemaphoreType.DMA((6 * nw,)), pltpu.SemaphoreType.DMA((6 * nw,))],
    )(*ws)


def _pair_exchange(gs):
    nw = len(gs)

    def body(*refs):
        g_refs, out_refs = refs[:nw], refs[nw:2 * nw]
        send_sems, recv_sems = refs[2 * nw:]
        x, y, c, _ = _place()
        cps = []
        for i in range(nw):
            half = gs[i].shape[1] // 2
            cp = _remote(g_refs[i].at[:, pl.ds((1 - c) * half, half), :], out_refs[i],
                         send_sems.at[i], recv_sems.at[i], (x, y, 1 - c))
            cp.start()
            cps.append(cp)
        for cp in cps:
            cp.wait()

    return _pcall(body, name="pair_exchange", in_specs=[ANY] * nw, out_specs=[ANY] * nw,
                  out_shape=[jax.ShapeDtypeStruct((4, g.shape[1] // 2, g.shape[2]), g.dtype) for g in gs],
                  scratch_shapes=[pltpu.SemaphoreType.DMA((nw,)), pltpu.SemaphoreType.DMA((nw,))])(*gs)


def _chip_exchange(ps):
    nw = len(ps)

    def body(*refs):
        p_refs, out_refs = refs[:nw], refs[nw:2 * nw]
        send_sems, recv_sems, local_sems = refs[2 * nw:]
        x, y, c, chips = _place()
        me = 2 * x + y
        mine = [pltpu.make_async_copy(p_refs[i].at[me], out_refs[i].at[me], local_sems.at[i]) for i in range(nw)]
        for cp in mine:
            cp.start()
        sends = []
        for j, (cx, cy) in enumerate(chips):
            for i in range(nw):
                cp = _remote(p_refs[i].at[2 * cx + cy], out_refs[i].at[me], send_sems.at[j * nw + i],
                             recv_sems.at[j * nw + i], (cx, cy, c))
                cp.start()
                sends.append(cp)
        for j, (cx, cy) in enumerate(chips):
            slot = 2 * cx + cy
            for i in range(nw):
                _remote(out_refs[i].at[slot], out_refs[i].at[slot], send_sems.at[j * nw + i],
                        recv_sems.at[j * nw + i], (x, y, c)).wait_recv()
        for cp in sends:
            cp.wait_send()
        for cp in mine:
            cp.wait()

    return _pcall(body, name="chip_exchange", in_specs=[ANY] * nw, out_specs=[ANY] * nw,
                  out_shape=[jax.ShapeDtypeStruct(p.shape, p.dtype) for p in ps],
                  scratch_shapes=[pltpu.SemaphoreType.DMA((3 * nw,)), pltpu.SemaphoreType.DMA((3 * nw,)),
                                  pltpu.SemaphoreType.DMA((nw,))])(*ps)


def _half_exchange(ts):
    nw = len(ts)

    def body(*refs):
        t_refs, out_refs = refs[:nw], refs[nw:2 * nw]
        send_sems, recv_sems = refs[2 * nw:]
        x, y, c, _ = _place()
        sends = []
        for i in range(nw):
            half = ts[i].shape[0]
            own = out_refs[i].at[pl.ds(c * half, half), :]
            cp = _remote(t_refs[i], own, send_sems.at[i], recv_sems.at[i], (x, y, 1 - c))
            cp.start()
            sends.append(cp)
        for i in range(nw):
            half = ts[i].shape[0]
            other = out_refs[i].at[pl.ds((1 - c) * half, half), :]
            _remote(t_refs[i], other, send_sems.at[i], recv_sems.at[i], (x, y, c)).wait_recv()
        for cp in sends:
            cp.wait_send()

    return _pcall(body, name="half_exchange", in_specs=[ANY] * nw, out_specs=[ANY] * nw,
                  out_shape=[jax.ShapeDtypeStruct((2 * t.shape[0], t.shape[1]), t.dtype) for t in ts],
                  scratch_shapes=[pltpu.SemaphoreType.DMA((nw,)), pltpu.SemaphoreType.DMA((nw,))])(*ts)


def _gather_all(v):
    rows, cols = v.shape

    def body(v_ref, out_ref, send_sems, recv_sems, local_sem):
        x, y, c, _ = _place()
        me = 4 * x + 2 * y + c
        mine = pltpu.make_async_copy(v_ref, out_ref.at[me], local_sem)
        mine.start()
        sends = []
        for d in range(1, 8):
            peer = (x ^ (d >> 2), y ^ ((d >> 1) & 1), c ^ (d & 1))
            cp = _remote(v_ref, out_ref.at[me], send_sems.at[d - 1], recv_sems.at[d - 1], peer)
            cp.start()
            sends.append(cp)
        for d in range(1, 8):
            slot = 4 * (x ^ (d >> 2)) + 2 * (y ^ ((d >> 1) & 1)) + (c ^ (d & 1))
            _remote(v_ref, out_ref.at[slot], send_sems.at[d - 1], recv_sems.at[d - 1], (x, y, c)).wait_recv()
        for cp in sends:
            cp.wait_send()
        mine.wait()

    return _pcall(body, name="gather_all", in_specs=[ANY], out_specs=ANY,
                  out_shape=jax.ShapeDtypeStruct((8, rows, cols), v.dtype),
                  scratch_shapes=[pltpu.SemaphoreType.DMA((7,)), pltpu.SemaphoreType.DMA((7,)),
                                  pltpu.SemaphoreType.DMA])(v)


def _pad_cols(a, width):
    return a if a.shape[1] == width else jnp.pad(a, ((0, 0), (0, width - a.shape[1])))


def _w_in_padded(w):
    def seg(name, width=None):
        o, n = ORIG[name]
        return _pad_cols(w[:, o:o + n], width or n)

    qn = w[:, ORIG["q_nsa"][0]:ORIG["q_nsa"][0] + 768].reshape(-1, NSA_HEADS, NSA_DK)
    qn = jnp.pad(qn, ((0, 0), (0, 0), (0, 256 - NSA_DK))).reshape(-1, NSA_HEADS * 256)
    kr = seg("k_rope")
    zeros = jnp.zeros((w.shape[0], PAD["z"] - (PAD["q_mem"] + 512)), w.dtype)
    return jnp.concatenate(
        [seg("c_q"), seg("c_kv"), qn, seg("k_c", 256), seg("k_s", 256), seg("k_w", 256), kr, kr,
         seg("v_c"), seg("v_s"), seg("v_w"), seg("g_nsa", LANE), seg("q_mem"), zeros,
         seg("z_mla"), seg("z_nsa"), seg("z_mem")], axis=1)


def _w_in_unpadded(g):
    def seg(name, n):
        return g[:, PAD[name]:PAD[name] + n]

    qn = g[:, PAD["q_nsa"]:PAD["q_nsa"] + 1024].reshape(-1, NSA_HEADS, 256)[:, :, :NSA_DK].reshape(-1, 768)
    z = PAD["z"]
    return jnp.concatenate(
        [seg("c_q", 512), seg("c_kv", 512), seg("k_rope", 64), g[:, z:z + 1024], qn, seg("k_c", 192),
         seg("v_c", 128), seg("k_s", 192), seg("v_s", 128), seg("k_w", 192), seg("v_w", 128),
         seg("g_nsa", 12), g[:, z + 1024:z + 1536], seg("q_mem", 512), g[:, z + 1536:z + 2048]], axis=1)


def _rope_tables(s):
    pos = jnp.arange(s, dtype=F32)
    inv_freq = ROPE_THETA ** (-jnp.arange(0, 64, 2, dtype=F32) / 64)
    ang = pos[:, None] * inv_freq[None, :]
    cos, sin = jnp.cos(ang), jnp.sin(ang)
    z = jnp.zeros((s, 64), F32)
    return jnp.concatenate([cos, cos, z], axis=1), jnp.concatenate([-sin, sin, z], axis=1)


def _overlap_table(s):
    n_c, n_s = s // CMP_STRIDE, s // SLC_LEN
    c0 = np.arange(n_c)[:, None] * CMP_STRIDE
    s0 = np.arange(LANE)[None, :] * SLC_LEN
    ov = (c0 < s0 + SLC_LEN) & (c0 + CMP_LEN > s0) & (np.arange(n_c)[:, None] < n_c - 1) & (np.arange(LANE)[None, :] < n_s)
    return jnp.asarray(ov.astype(np.float32), dtype=BF16)


def _shift_down(a):
    return jnp.concatenate([jnp.zeros((8, a.shape[1]), a.dtype), a], axis=0)[7:7 + a.shape[0]]


def _shift_up(a):
    return jnp.concatenate([a, jnp.zeros((8, a.shape[1]), a.dtype)], axis=0)[1:1 + a.shape[0]]


def _local_step(x, mem, target, w):
    s = x.shape[0]
    cs, sn = _rope_tables(s)
    t_ = jnp.transpose

    w_in_p = _w_in_padded(w["w_in"])
    xn, rstd_x = _rms_fwd(_Src(x, D_MODEL), w["norm_g"], "norm_x")
    hp, hpb = _mm(xn, w_in_p, "in_proj", second_dtype=BF16)

    w_uq3 = w["w_uq"].reshape(512, MLA_HEADS, 192)
    w_uq_p = jnp.concatenate([w_uq3, w_uq3[:, :, 128:]], axis=2).reshape(512, MLA_HEADS * 256)
    w_ukv_p = t_(w["w_ukv"].reshape(512, MLA_HEADS, 2, 128), (0, 2, 1, 3)).reshape(512, 2048)
    c_q, c_kv = _Src(hp, 512, 0), _Src(hp, 512, 1)
    cqn, rstd_q = _rms_fwd(c_q, w["q_norm_g"], "norm_q")
    ckvn, rstd_kv = _rms_fwd(c_kv, w["kv_norm_g"], "norm_kv")
    q_lin = _mm(cqn, w_uq_p, "mla_q_proj")
    kvb = _mm(ckvn, w_ukv_p, "mla_kv_proj", out_dtype=BF16)
    q_mla = _rope_fwd(_Src(q_lin, MLA_HEADS * 256), cs, sn, MLA_HEADS, 256, LANE, "rope_q")
    k_pe = _rope_fwd(_Src(hp, LANE, PAD["k_rope"] // LANE), cs, sn, 1, LANE, 0, "rope_k")
    mla = _Attn("mla", s, s, MLA_HEADS, 256)
    mla_q, mla_v = _Src(q_mla, 256), _Src(kvb, LANE, MLA_HEADS)
    mla_k = [_Src(kvb, LANE), _Src(k_pe, LANE, 0, False)]
    o_mla, l_mla, lr_mla = _attn_fwd(mla, mla_q, mla_k, mla_v, None, "mla_fwd")

    sk = s // CMP_STRIDE
    pe_k, pe_v = w["cmp_pe_k"], w["cmp_pe_v"]
    w1k = _pad_cols(w["cmp_w1k"], 256)
    w2k = jnp.pad(w["cmp_w2k"], ((0, 64), (0, 64))).astype(BF16)
    w1v, w2v = w["cmp_w1v"], w["cmp_w2v"].astype(BF16)
    half_k, half_v = CMP_STRIDE * NSA_DK, CMP_STRIDE * HEAD_V
    ak = hp[:, PAD["k_c"]:PAD["k_c"] + NSA_DK].reshape(sk, half_k)
    av = hp[:, PAD["v_c"]:PAD["v_c"] + HEAD_V].reshape(sk, half_v)
    ck_args = (ak, _shift_up(ak), pe_k[:CMP_STRIDE].reshape(1, half_k), pe_k[CMP_STRIDE:].reshape(1, half_k),
               w1k[:half_k], w1k[half_k:], w2k)
    cv_args = (av, _shift_up(av), pe_v[:CMP_STRIDE].reshape(1, half_v), pe_v[CMP_STRIDE:].reshape(1, half_v),
               w1v[:half_v], w1v[half_v:], w2v)
    k_cmp, pre_k = _compress_fwd(*ck_args, "compress_k")
    v_cmp, pre_v = _compress_fwd(*cv_args, "compress_v")
    cmp_ = _Attn("cmp", s, sk, NSA_HEADS, 256)
    slc = _Attn("slc", s, s, NSA_HEADS, 256)
    win = _Attn("win", s, s, NSA_HEADS, 256)
    nsa_q = _Src(hpb, 256, PAD["q_nsa"] // 256)
    cmp_k, cmp_v = [_Src(k_cmp, 256, 0, False)], _Src(v_cmp, HEAD_V, 0, False)
    slc_k, slc_v = [_Src(hpb, 256, PAD["k_s"] // 256, False)], _Src(hpb, HEAD_V, PAD["v_s"] // HEAD_V, False)
    win_k, win_v = [_Src(hpb, 256, PAD["k_w"] // 256, False)], _Src(hpb, HEAD_V, PAD["v_w"] // HEAD_V, False)
    o_cmp, l_cmp, lr_cmp = _attn_fwd(cmp_, nsa_q, cmp_k, cmp_v, None, "cmp_fwd")
    sel, selt = _select(cmp_, _Src(hpb, NSA_HEADS * 256, PAD["q_nsa"] // (NSA_HEADS * 256)), k_cmp,
                        _overlap_table(s))
    o_slc, l_slc, lr_slc = _attn_fwd(slc, nsa_q, slc_k, slc_v, sel, "slc_fwd")
    o_win, l_win, lr_win = _attn_fwd(win, nsa_q, win_k, win_v, None, "win_fwd")
    gl = _Src(hp, LANE, PAD["g_nsa"] // LANE)
    o_nsa = _nsa_combine(o_cmp, o_slc, o_win, gl)

    mn, rstd_m = _rms_fwd(_Src(mem, D_MODEL), w["mem_norm_g"], "norm_mem")
    kvm = _mm(mn, w["w_mem_kv"], "mem_kv_proj", out_dtype=BF16)
    mem_ = _Attn("mem", s, mem.shape[0], MEM_HEADS, LANE)
    mem_q, mem_k, mem_v = _Src(hpb, LANE, PAD["q_mem"] // LANE), [_Src(kvm, LANE)], _Src(kvm, LANE, MEM_HEADS)
    o_mem, l_mem, lr_mem = _attn_fwd(mem_, mem_q, mem_k, mem_v, None, "mem_fwd")

    u = _gate_fwd(o_mla, o_nsa, o_mem, hp)
    proj = _mm(u, w["w_out"], "out_proj")
    dy, g_final, loss = _final_loss(x, proj, w["final_norm_g"].reshape(1, -1), target)

    g_w_out = _mm(u, dy, "out_proj_dw", mode="tn")
    du = _mm(dy, w["w_out"], "out_proj_dx", mode="nt")
    do_cat, dz = _gate_bwd(du, o_mla, o_nsa, o_mem, hp)

    dq_mem, (dk_mem,), dv_mem = _attn_bwd(mem_, mem_q, mem_k, mem_v, None, None, _Src(o_mem, HEAD_V), l_mem,
                                          lr_mem, _Src(do_cat, HEAD_V, 12), None, "mem_bwd")
    dkvm = jnp.concatenate([dk_mem, dv_mem], axis=1)
    g_w_mem_kv = _mm(mn, dkvm, "mem_kv_dw", mode="tn")
    dmn = _mm(dkvm, w["w_mem_kv"], "mem_kv_dx", mode="nt")
    _, g_mem_norm = _rms_bwd(_Src(mem, D_MODEL), w["mem_norm_g"], rstd_m, dmn, None, "norm_mem_bwd")

    do_cmp, do_slc, do_win, dgl = _nsa_combine_bwd(do_cat, o_cmp, o_slc, o_win, gl)
    dq_n, (dk_cmp,), dv_cmp = _attn_bwd(cmp_, nsa_q, cmp_k, cmp_v, None, None, _Src(o_cmp, HEAD_V), l_cmp,
                                        lr_cmp, _Src(do_cmp, HEAD_V), None, "cmp_bwd")
    dq_n, (dk_s,), dv_s = _attn_bwd(slc, nsa_q, slc_k, slc_v, sel, selt, _Src(o_slc, HEAD_V), l_slc, lr_slc,
                                    _Src(do_slc, HEAD_V), dq_n, "slc_bwd")
    dq_n, (dk_w,), dv_w = _attn_bwd(win, nsa_q, win_k, win_v, None, None, _Src(o_win, HEAD_V), l_win, lr_win,
                                    _Src(do_win, HEAD_V), dq_n, "win_bwd")
    dak, dpk_lo, dpk_hi, dw1k_lo, dw1k_hi, g_w2k = _compress_bwd(
        *ck_args, pre_k, _shift_down(pre_k), dk_cmp, _shift_down(dk_cmp), "compress_k_bwd")
    dav, dpv_lo, dpv_hi, dw1v_lo, dw1v_hi, g_w2v = _compress_bwd(
        *cv_args, pre_v, _shift_down(pre_v), dv_cmp, _shift_down(dv_cmp), "compress_v_bwd")
    g_pe_k = jnp.concatenate([dpk_lo.reshape(CMP_STRIDE, NSA_DK), dpk_hi.reshape(CMP_STRIDE, NSA_DK)], axis=0)
    g_pe_v = jnp.concatenate([dpv_lo.reshape(CMP_STRIDE, HEAD_V), dpv_hi.reshape(CMP_STRIDE, HEAD_V)], axis=0)
    g_w1k = jnp.concatenate([dw1k_lo, dw1k_hi], axis=0)[:, :NSA_DK]
    g_w1v = jnp.concatenate([dw1v_lo, dw1v_hi], axis=0)
    dk_c = _pad_cols(dak.reshape(s, NSA_DK), 256)
    dv_c = dav.reshape(s, HEAD_V)

    dq_m, (dk_nope, dk_pe), dv_m = _attn_bwd(mla, mla_q, mla_k, mla_v, None, None, _Src(o_mla, HEAD_V), l_mla,
                                             lr_mla, _Src(do_cat, HEAD_V), None, "mla_bwd")
    dq_lin = _rope_bwd_q(dq_m, cs, sn)
    dkv_lin, d_krope = _rope_bwd_k(dk_nope, dk_pe, dv_m, cs, sn)
    g_w_uq_p = _mm(cqn, dq_lin, "mla_q_dw", mode="tn")
    dcqn = _mm(dq_lin, w_uq_p, "mla_q_dx", mode="nt")
    g_w_ukv_p = _mm(ckvn, dkv_lin, "mla_kv_dw", mode="tn")
    dckvn = _mm(dkv_lin, w_ukv_p, "mla_kv_dx", mode="nt")
    dc_q, g_q_norm = _rms_bwd(c_q, w["q_norm_g"], rstd_q, dcqn, None, "norm_q_bwd")
    dc_kv, g_kv_norm = _rms_bwd(c_kv, w["kv_norm_g"], rstd_kv, dckvn, None, "norm_kv_bwd")
    g_w_uq = g_w_uq_p.reshape(512, MLA_HEADS, 256)[:, :, :192].reshape(512, MLA_HEADS * 192)
    g_w_ukv = t_(g_w_ukv_p.reshape(512, 2, MLA_HEADS, 128), (0, 2, 1, 3)).reshape(512, 2048)

    dhp = jnp.concatenate(
        [dc_q, dc_kv, dq_n, dk_c, dk_s, dk_w, d_krope, dv_c, dv_s, dv_w, dgl, dq_mem,
         jnp.zeros((s, PAD["z"] - (PAD["q_mem"] + 512)), F32), dz], axis=1)
    g_w_in = _w_in_unpadded(_mm(xn, dhp, "in_proj_dw", mode="tn"))
    dxn = _mm(dhp, w_in_p, "in_proj_dx", mode="nt")
    grad_x, g_norm = _rms_bwd(_Src(x, D_MODEL), w["norm_g"], rstd_x, dxn, dy, "norm_x_bwd")

    grads = dict(norm_g=g_norm, w_in=g_w_in, q_norm_g=g_q_norm, w_uq=g_w_uq, kv_norm_g=g_kv_norm,
                 w_ukv=g_w_ukv, cmp_pe_k=g_pe_k, cmp_pe_v=g_pe_v, cmp_w1k=g_w1k, cmp_w2k=g_w2k[:NSA_DK, :NSA_DK],
                 cmp_w1v=g_w1v, cmp_w2v=g_w2v, mem_norm_g=g_mem_norm, w_mem_kv=g_w_mem_kv, w_out=g_w_out,
                 final_norm_g=g_final.reshape(-1))
    return loss[0, 0], grad_x, grads


def kernel(x, mem, norm_g, w_in, q_norm_g, w_uq, kv_norm_g, w_ukv, cmp_pe_k, cmp_pe_v, cmp_w1k, cmp_w2k, cmp_w1v, cmp_w2v, mem_norm_g, w_mem_kv, w_out, final_norm_g, loss_target, m_norm_g, m_w_in, m_q_norm_g, m_w_uq, m_kv_norm_g, m_w_ukv, m_cmp_pe_k, m_cmp_pe_v, m_cmp_w1k, m_cmp_w2k, m_cmp_w1v, m_cmp_w2v, m_mem_norm_g, m_w_mem_kv, m_w_out, m_final_norm_g, v_norm_g, v_w_in, v_q_norm_g, v_w_uq, v_kv_norm_g, v_w_ukv, v_cmp_pe_k, v_cmp_pe_v, v_cmp_w1k, v_cmp_w2k, v_cmp_w1v, v_cmp_w2v, v_mem_norm_g, v_w_mem_kv, v_w_out, v_final_norm_g):
    args = dict(locals())
    wts = {n: args[n] for n in WEIGHTS}
    loc = {n: (a if n == "final_norm_g" else a[0]) for n, a in wts.items()}

    own = [loc[n].astype(BF16) for n in SHARDED]
    chip = 2 * lax.axis_index("x") + lax.axis_index("y")
    gathered = [lax.dynamic_update_slice(gw, a[None], (chip, 0, 0)) for gw, a in zip(_gather_shards(own), own)]
    full = {n: loc[n].reshape(1, -1) if loc[n].ndim == 1 else loc[n] for n in REPLICATED}
    for n, gw in zip(SHARDED, gathered):
        if SHARD_AXIS[n] == 0:
            full[n] = gw.reshape(4 * gw.shape[1], gw.shape[2])
        else:
            full[n] = jnp.concatenate([gw[j] for j in range(4)], axis=1)

    loss, grad_x, g = _local_step(x[0], mem[0], loss_target[0], full)
    loss = lax.psum(loss, ("x", "y", "c"))

    def slots(n):
        a = g[n]
        if SHARD_AXIS[n] == 0:
            return a.reshape(4, a.shape[0] // 4, a.shape[1])
        width = a.shape[1] // 4
        return jnp.stack([a[:, j * width:(j + 1) * width] for j in range(4)])

    gs = [slots(n) for n in SHARDED]
    core = lax.axis_index("c").astype(jnp.int32).reshape(1)
    theirs = _pair_exchange(gs)
    pairs = [_pair_sum(a, b, core, "pair_sum_" + n) for n, a, b in zip(SHARDED, gs, theirs)]
    from_chips = _chip_exchange(pairs)
    mine = [_sum_slots(b, "chip_sum_" + n) for n, b in zip(SHARDED, from_chips)]
    g_sh = [lax.dynamic_update_slice(o, t, (core[0] * t.shape[0], 0)) for o, t in zip(_half_exchange(mine), mine)]

    n_rep = sum(int(np.prod(loc[n].shape)) for n in REPLICATED)
    rows_rep = -(-n_rep // (8 * LANE)) * 8

    def rep_pack(parts):
        flat = jnp.concatenate([p.reshape(-1) for p in parts])
        return jnp.pad(flat, (0, rows_rep * LANE - n_rep)).reshape(rows_rep, LANE)

    g_rep = _sum_slots(_gather_all(rep_pack([g[n] for n in REPLICATED])), "replica_sum")
    d_rp, m_rp, v_rp = _adamw(rep_pack([wts[n] for n in REPLICATED]), g_rep,
                              rep_pack([args["m_" + n] for n in REPLICATED]),
                              rep_pack([args["v_" + n] for n in REPLICATED]), "adamw_replicated")

    def rep_unpack(buf):
        flat, out, o = buf.reshape(-1), {}, 0
        for n in REPLICATED:
            size = int(np.prod(wts[n].shape))
            out[n] = flat[o:o + size].reshape(wts[n].shape)
            o += size
        return out

    outs = {k: rep_unpack(b) for k, b in (("g", g_rep), ("d", d_rp), ("m", m_rp), ("v", v_rp))}
    for n, gn in zip(SHARDED, g_sh):
        d, mo, vo = _adamw(loc[n], gn, args["m_" + n][0], args["v_" + n][0], "adamw_" + n)
        for k, a in (("g", gn), ("d", d), ("m", mo), ("v", vo)):
            outs[k][n] = a.reshape(wts[n].shape)

    return (loss, grad_x[None], *[outs["g"][n] for n in WEIGHTS], *[outs["d"][n] for n in WEIGHTS],
            *[outs["m"][n] for n in WEIGHTS], *[outs["v"][n] for n in WEIGHTS])
```

```python
from typing import NamedTuple

import numpy as np
import jax
import jax.numpy as jnp
from jax import lax
from jax.experimental import pallas as pl
from jax.experimental.pallas import tpu as pltpu

F32 = jnp.float32
BF16 = jnp.bfloat16
MESH = pl.DeviceIdType.MESH

D_MODEL = 2048
EPS = 1e-6
LANE = 128
HEAD_V = 128
MLA_HEADS = 8
NSA_HEADS = 4
MEM_HEADS = 4
NSA_DK = 192
CMP_STRIDE = 16
CMP_LEN = 32
SLC_LEN = 64
SLC_TOPN = 16
WIN = 512
NEG = -1e30
ROPE_THETA = 10000.0
BLOCK_BYTES = 2 << 20

ORIG = dict(c_q=(0, 512), c_kv=(512, 512), k_rope=(1024, 64), z_mla=(1088, 1024),
            q_nsa=(2112, 768), k_c=(2880, 192), v_c=(3072, 128), k_s=(3200, 192),
            v_s=(3392, 128), k_w=(3520, 192), v_w=(3712, 128), g_nsa=(3840, 12),
            z_nsa=(3852, 512), q_mem=(4364, 512), z_mem=(4876, 512))
PAD = dict(c_q=0, c_kv=512, q_nsa=1024, k_c=2048, k_s=2304, k_w=2560, k_rope=2816, v_c=2944,
           v_s=3072, v_w=3200, g_nsa=3328, q_mem=3456, z=4096)
D_PAD = 6144

ADAM_LR, ADAM_B1, ADAM_B2, ADAM_EPS, ADAM_WD, ADAM_STEP = 0.001, 0.9, 0.999, 1e-08, 0.01, 10

SHARDED = ("w_in", "w_uq", "w_ukv", "cmp_w1k", "cmp_w1v", "w_mem_kv", "w_out")
SHARD_AXIS = dict(w_in=1, w_uq=1, w_ukv=1, cmp_w1k=0, cmp_w1v=0, w_mem_kv=0, w_out=0)
REPLICATED = ("norm_g", "q_norm_g", "kv_norm_g", "cmp_pe_k", "cmp_pe_v", "cmp_w2k", "cmp_w2v",
              "mem_norm_g", "final_norm_g")
WEIGHTS = ("norm_g", "w_in", "q_norm_g", "w_uq", "kv_norm_g", "w_ukv", "cmp_pe_k", "cmp_pe_v",
           "cmp_w1k", "cmp_w2k", "cmp_w1v", "cmp_w2v", "mem_norm_g", "w_mem_kv", "w_out",
           "final_norm_g")


def _pcall(kernel, **kw):
    return pl.pallas_call(kernel, **kw)


def _tile(n, pref):
    if n <= pref:
        return n
    for t in range(pref, LANE - 1, -LANE):
        if n % t == 0:
            return t
    raise ValueError((n, pref))


def _row_tile(rows, cols, itemsize=4):
    want = max(16, BLOCK_BYTES // (cols * itemsize))
    if rows <= want:
        return rows
    t = 16
    best = rows
    while t <= want:
        if rows % t == 0:
            best = t
        t *= 2
    return best


def _nt(a, b):
    return lax.dot_general(a, b, (((1,), (1,)), ((), ())), preferred_element_type=F32)


def _tn(a, b):
    return lax.dot_general(a, b, (((0,), (0,)), ((), ())), preferred_element_type=F32)


def _nn(a, b):
    return jnp.dot(a, b, preferred_element_type=F32)


def _sigmoid(x):
    return 1.0 / (1.0 + jnp.exp(-x))


class _Src(NamedTuple):
    arr: jax.Array
    width: int
    col0: int = 0
    per_head: bool = True

    def col(self, h):
        return self.col0 + h if self.per_head else self.col0


def _mm(a, b, name, mode="nn", out_dtype=F32, second_dtype=None):
    if mode == "tn":
        k, m = a.shape
    else:
        m, k = a.shape
    if mode == "nt":
        n, k2 = b.shape
    else:
        k2, n = b.shape
    assert k == k2, (a.shape, b.shape, mode)
    tm, tn, tk = _tile(m, 1024), _tile(n, 1024), _tile(k, 2048)
    nk = k // tk
    assert nk == 1 or (out_dtype == F32 and second_dtype is None)
    dot = {"nn": _nn, "nt": _nt, "tn": _tn}[mode]

    def kern(a_ref, b_ref, o_ref, *more):
        r = dot(a_ref[...].astype(BF16), b_ref[...].astype(BF16))
        if nk == 1:
            o_ref[...] = r.astype(out_dtype)
            if more:
                more[0][...] = r.astype(second_dtype)
        else:
            kk = pl.program_id(2)

            @pl.when(kk == 0)
            def _():
                o_ref[...] = r

            @pl.when(kk > 0)
            def _():
                o_ref[...] += r

    a_spec = (pl.BlockSpec((tk, tm), lambda i, j, kk: (kk, i)) if mode == "tn"
              else pl.BlockSpec((tm, tk), lambda i, j, kk: (i, kk)))
    b_spec = (pl.BlockSpec((tn, tk), lambda i, j, kk: (j, kk)) if mode == "nt"
              else pl.BlockSpec((tk, tn), lambda i, j, kk: (kk, j)))
    o_spec = pl.BlockSpec((tm, tn), lambda i, j, kk: (i, j))
    out_shape = jax.ShapeDtypeStruct((m, n), out_dtype)
    if second_dtype is not None:
        o_spec = [o_spec, o_spec]
        out_shape = [out_shape, jax.ShapeDtypeStruct((m, n), second_dtype)]
    return _pcall(
        kern, name=name, grid=(m // tm, n // tn, nk), in_specs=[a_spec, b_spec], out_specs=o_spec,
        out_shape=out_shape,
        compiler_params=pltpu.CompilerParams(dimension_semantics=("parallel", "parallel", "arbitrary")),
    )(a, b)


def _rms_fwd(x, g, name):
    r, d = x.arr.shape[0], x.width
    tr = _tile(r, 512)

    def kern(x_ref, g_ref, y_ref, r_ref):
        xv = x_ref[...]
        rstd = lax.rsqrt(jnp.mean(xv * xv, axis=-1, keepdims=True) + EPS)
        y_ref[...] = (xv * rstd * g_ref[...]).astype(BF16)
        r_ref[...] = rstd

    return _pcall(
        kern, name=name, grid=(r // tr,),
        in_specs=[pl.BlockSpec((tr, d), lambda i: (i, x.col0)), pl.BlockSpec((1, d), lambda i: (0, 0))],
        out_specs=[pl.BlockSpec((tr, d), lambda i: (i, 0)), pl.BlockSpec((tr, 1), lambda i: (i, 0))],
        out_shape=[jax.ShapeDtypeStruct((r, d), BF16), jax.ShapeDtypeStruct((r, 1), F32)],
    )(x.arr, g)


def _rms_bwd(x, g, rstd, dy, add, name):
    r, d = x.arr.shape[0], x.width
    tr = _tile(r, 256)
    has_add = add is not None

    def kern(*refs):
        if has_add:
            x_ref, g_ref, r_ref, dy_ref, add_ref, dx_ref, dg_ref = refs
        else:
            x_ref, g_ref, r_ref, dy_ref, dx_ref, dg_ref = refs
        rs = r_ref[...]
        xhat = x_ref[...] * rs
        dyv = dy_ref[...]
        dyg = dyv * g_ref[...]
        c = jnp.mean(dyg * xhat, axis=-1, keepdims=True)
        dx = rs * (dyg - xhat * c)
        if has_add:
            dx = dx + add_ref[...]
        dx_ref[...] = dx
        part = jnp.sum(dyv * xhat, axis=0, keepdims=True)

        @pl.when(pl.program_id(0) == 0)
        def _():
            dg_ref[...] = part

        @pl.when(pl.program_id(0) > 0)
        def _():
            dg_ref[...] += part

    row = pl.BlockSpec((tr, d), lambda i: (i, 0))
    vec = pl.BlockSpec((1, d), lambda i: (0, 0))
    ins = [pl.BlockSpec((tr, d), lambda i: (i, x.col0)), vec, pl.BlockSpec((tr, 1), lambda i: (i, 0)), row]
    ins += [row] if has_add else []
    args = (x.arr, g, rstd, dy) + ((add,) if has_add else ())
    return _pcall(
        kern, name=name, grid=(r // tr,), in_specs=ins, out_specs=[row, vec],
        out_shape=[jax.ShapeDtypeStruct((r, d), F32), jax.ShapeDtypeStruct((1, d), F32)],
        compiler_params=pltpu.CompilerParams(dimension_semantics=("arbitrary",)),
    )(*args)


def _final_loss(x, proj, g, target):
    r, d = x.shape
    tr = _tile(r, 256)

    def kern(x_ref, p_ref, g_ref, t_ref, dy_ref, dg_ref, loss_ref):
        y = x_ref[...] + p_ref[...]
        rs = lax.rsqrt(jnp.mean(y * y, axis=-1, keepdims=True) + EPS)
        yhat = y * rs
        gv = g_ref[...]
        e = yhat * gv - t_ref[...]
        lpart = 0.5 * jnp.sum(jnp.mean(e * e, axis=-1, keepdims=True), axis=0, keepdims=True)
        dout = e * (1.0 / d)
        dyg = dout * gv
        c = jnp.mean(dyg * yhat, axis=-1, keepdims=True)
        dy_ref[...] = rs * (dyg - yhat * c)
        gpart = jnp.sum(dout * yhat, axis=0, keepdims=True)
        lrow = jnp.broadcast_to(lpart, (1, LANE))

        @pl.when(pl.program_id(0) == 0)
        def _():
            dg_ref[...] = gpart
            loss_ref[...] = lrow

        @pl.when(pl.program_id(0) > 0)
        def _():
            dg_ref[...] += gpart
            loss_ref[...] += lrow

    row = pl.BlockSpec((tr, d), lambda i: (i, 0))
    vec = pl.BlockSpec((1, d), lambda i: (0, 0))
    return _pcall(
        kern, name="final_loss", grid=(r // tr,), in_specs=[row, row, vec, row],
        out_specs=[row, vec, pl.BlockSpec((1, LANE), lambda i: (0, 0))],
        out_shape=[jax.ShapeDtypeStruct((r, d), F32), jax.ShapeDtypeStruct((1, d), F32),
                   jax.ShapeDtypeStruct((1, LANE), F32)],
        compiler_params=pltpu.CompilerParams(dimension_semantics=("arbitrary",)),
    )(x, proj, g, target)


def _rope_fwd(x, cs, sn, nh, width, off, name):
    s = x.arr.shape[0]
    tr = _tile(s, 512)

    def kern(x_ref, c_ref, s_ref, o_ref):
        cv, sv = c_ref[...], s_ref[...]
        for h in range(nh):
            b = h * width
            if off:
                o_ref[:, b:b + off] = x_ref[:, b:b + off].astype(BF16)
            xr = x_ref[:, b + off:b + off + LANE]
            o_ref[:, b + off:b + off + LANE] = (xr * cv + pltpu.roll(xr, 32, 1) * sv).astype(BF16)

    tab = pl.BlockSpec((tr, LANE), lambda i: (i, 0))
    return _pcall(
        kern, name=name, grid=(s // tr,),
        in_specs=[pl.BlockSpec((tr, nh * width), lambda i: (i, x.col0)), tab, tab],
        out_specs=pl.BlockSpec((tr, nh * width), lambda i: (i, 0)),
        out_shape=jax.ShapeDtypeStruct((s, nh * width), BF16),
    )(x.arr, cs, sn)


def _rope_grad(d, cv, sv):
    g2 = d * sv
    g2 = g2 + pltpu.roll(g2, 64, 1)
    lane = lax.broadcasted_iota(jnp.int32, d.shape, 1)
    return jnp.where(lane < 64, d * cv + pltpu.roll(g2, 32, 1), 0.0)


def _rope_bwd_q(dq, cs, sn):
    s, w = dq.shape
    tr = _tile(s, 512)
    nh = w // 256

    def kern(d_ref, c_ref, s_ref, o_ref):
        cv, sv = c_ref[...], s_ref[...]
        for h in range(nh):
            b = h * 256
            o_ref[:, b:b + LANE] = d_ref[:, b:b + LANE]
            o_ref[:, b + LANE:b + 256] = _rope_grad(d_ref[:, b + LANE:b + 256], cv, sv)

    row = pl.BlockSpec((tr, w), lambda i: (i, 0))
    tab = pl.BlockSpec((tr, LANE), lambda i: (i, 0))
    return _pcall(kern, name="rope_bwd_q", grid=(s // tr,), in_specs=[row, tab, tab], out_specs=row,
                  out_shape=jax.ShapeDtypeStruct((s, w), F32))(dq, cs, sn)


def _rope_bwd_k(dk_nope, dk_pe, dv, cs, sn):
    s, w = dk_nope.shape
    tr = _tile(s, 512)

    def kern(dk_ref, dp_ref, dv_ref, c_ref, s_ref, okv_ref, okr_ref):
        okv_ref[:, :w] = dk_ref[...]
        okv_ref[:, w:] = dv_ref[...]
        okr_ref[...] = _rope_grad(dp_ref[...], c_ref[...], s_ref[...])

    tab = pl.BlockSpec((tr, LANE), lambda i: (i, 0))
    wide = pl.BlockSpec((tr, w), lambda i: (i, 0))
    return _pcall(
        kern, name="rope_bwd_k", grid=(s // tr,), in_specs=[wide, tab, wide, tab, tab],
        out_specs=[pl.BlockSpec((tr, 2 * w), lambda i: (i, 0)), tab],
        out_shape=[jax.ShapeDtypeStruct((s, 2 * w), F32), jax.ShapeDtypeStruct((s, LANE), F32)],
    )(dk_nope, dk_pe, dv, cs, sn)


class _Attn:
    def __init__(self, mode, s, sk, heads, dk):
        self.mode, self.s, self.sk, self.h, self.dk = mode, s, sk, heads, dk
        self.scale = {"mla": 192 ** -0.5, "mem": 128 ** -0.5}.get(mode, NSA_DK ** -0.5)
        self.tb = min(256, s)
        self.nb = s // self.tb
        self.nsub = 2 if self.nb % 2 == 0 else 1
        self.tq = self.tb * self.nsub
        self.nq = s // self.tq
        self.causal = mode in ("mla", "slc")
        if self.causal:
            self.tk = self.tq
        elif mode == "win":
            self.tk = WIN + self.tb
        else:
            self.tk = sk
        self.tkb = min(512, sk)
        self.ksub = 2 if self.tkb == 512 and mode == "mla" else 1
        self.kb = self.tkb // self.ksub
        self.ncmp = s // CMP_STRIDE - 1

    def mask_bias(self, t, n, h, selx, diag):
        m = self.mode
        if m == "mla":
            return (n <= t) if diag else None, None
        if m == "mem":
            return None, None
        slope = jnp.where(h == 0, 0.25, jnp.where(h == 1, 0.0625, jnp.where(h == 2, 0.015625, 0.00390625)))
        slope = slope.astype(F32)
        if m == "cmp":
            mask = (n * CMP_STRIDE + (CMP_LEN - 1) <= t) & (n < self.ncmp)
            pos = n.astype(F32) * float(CMP_STRIDE) + (CMP_LEN - 1) / 2.0
            return mask, -slope * (t.astype(F32) - pos)
        rel = t - n
        bias = -slope * rel.astype(F32)
        if m == "slc":
            mask = selx > 0.5
            return (mask & (rel >= 0)) if diag else mask, bias
        return (rel >= 0) & (rel < WIN), bias


def _scores(cfg, s_raw, t, n, h, selx, diag, lse=None):
    s = s_raw * cfg.scale
    mask, bias = cfg.mask_bias(t, n, h, selx, diag)
    if bias is not None:
        s = s + bias
    if lse is None:
        if mask is not None:
            s = jnp.where(mask, s, NEG)
        return s, mask
    p = jnp.exp(jnp.minimum(s - lse, 0.0))
    if mask is not None:
        p = jnp.where(mask, p, 0.0)
    return p, mask


def _block_of_key(k0, tk, keys_on_rows):
    shape = (tk, LANE) if keys_on_rows else (LANE, tk)
    n = lax.broadcasted_iota(jnp.int32, shape, 0 if keys_on_rows else 1) + k0
    j = lax.broadcasted_iota(jnp.int32, shape, 1 if keys_on_rows else 0)
    return jnp.where((n >> 6) == j, 1.0, 0.0).astype(BF16)


def _to_row(col):
    t = col.shape[0]
    return jnp.transpose(jnp.broadcast_to(col, (t, LANE)))[0:1, :]


def _load_keys(k_refs, rows):
    parts = [r[rows, :].astype(BF16) for r in k_refs]
    return parts[0] if len(parts) == 1 else jnp.concatenate(parts, axis=1)


def _attn_fwd(cfg, q, ks, v, sel, name):
    s, tq, tk, tb, nsub = cfg.s, cfg.tq, cfg.tk, cfg.tb, cfg.nsub
    has_sel = sel is not None
    nkp = len(ks)

    def kern(*refs):
        q_ref, k_refs, v_ref = refs[0], refs[1:1 + nkp], refs[1 + nkp]
        sel_ref = refs[2 + nkp] if has_sel else None
        o_ref, lc_ref, lr_ref = refs[-3:]
        h, i = pl.program_id(0), pl.program_id(1)
        part = [slice(r * tb, (r + 1) * tb) for r in range(nsub)]
        qs = [q_ref[p, :].astype(BF16) for p in part]
        ts = [i * tq + r * tb + lax.broadcasted_iota(jnp.int32, (tb, 1), 0) for r in range(nsub)]
        sels = [sel_ref[p, :].astype(BF16) for p in part] if has_sel else None

        def load(k0):
            rows = pl.ds(k0, tk)
            return _load_keys(k_refs, rows), v_ref[rows, :].astype(BF16)

        def step(r, k0, kk, vv, emat, carry, diag):
            m, l, acc = carry
            n = k0 + lax.broadcasted_iota(jnp.int32, (1, tk), 1)
            selx = _nn(sels[r], emat) if has_sel else None
            sc, mask = _scores(cfg, _nt(qs[r], kk), ts[r], n, h, selx, diag)
            m_new = jnp.maximum(m, jnp.max(sc, axis=1, keepdims=True))
            alpha = jnp.exp(m - m_new)
            p = jnp.exp(sc - m_new)
            if mask is not None:
                p = jnp.where(mask, p, 0.0)
            l = alpha * l + jnp.sum(p, axis=1, keepdims=True)
            acc = alpha * acc + _nn(p.astype(BF16), vv)
            return m_new, l, acc

        def chunk(k0, carry, diag):
            kk, vv = load(k0)
            emat = _block_of_key(k0, tk, False) if has_sel else None
            return tuple(step(r, k0, kk, vv, emat, carry[r], diag) for r in range(nsub))

        init = (jnp.full((tb, 1), NEG, F32), jnp.zeros((tb, 1), F32), jnp.zeros((tb, HEAD_V), F32))
        carry = (init,) * nsub
        if cfg.causal:
            full = (i * tq) // tk
            carry = lax.fori_loop(0, full, lambda c, cr: chunk(pl.multiple_of(c * tk, tk), cr, False), carry)
            carry = chunk(pl.multiple_of(full * tk, tk), carry, True)
        elif cfg.mode == "win":
            starts = [pl.multiple_of(jnp.maximum(i * tq + r * tb - WIN, 0), tb) for r in range(nsub)]
            carry = tuple(step(r, k0, *load(k0), None, carry[r], True) for r, k0 in enumerate(starts))
        else:
            carry = chunk(0, carry, True)
        for r, (m, l, acc) in enumerate(carry):
            o_ref[part[r], :] = acc / (l + 1e-20)
            lse = m + jnp.log(l + 1e-20)
            lc_ref[0, part[r], :] = lse
            lr_ref[0, r] = _to_row(lse)

    ins = [pl.BlockSpec((tq, q.width), lambda h, i: (i, q.col(h)))]
    ins += [pl.BlockSpec((cfg.sk, p.width), lambda h, i, p=p: (0, p.col(h))) for p in ks]
    ins += [pl.BlockSpec((cfg.sk, HEAD_V), lambda h, i: (0, v.col(h)))]
    args = [q.arr] + [p.arr for p in ks] + [v.arr]
    if has_sel:
        ins.append(pl.BlockSpec((tq, LANE), lambda h, i: (i, 0)))
        args.append(sel)
    return _pcall(
        kern, name=name, grid=(cfg.h, cfg.nq), in_specs=ins,
        out_specs=[pl.BlockSpec((tq, HEAD_V), lambda h, i: (i, h)),
                   pl.BlockSpec((1, tq, 1), lambda h, i: (h, i, 0)),
                   pl.BlockSpec((1, nsub, 1, tb), lambda h, i: (h, i, 0, 0))],
        out_shape=[jax.ShapeDtypeStruct((s, cfg.h * HEAD_V), F32),
                   jax.ShapeDtypeStruct((cfg.h, s, 1), F32),
                   jax.ShapeDtypeStruct((cfg.h, cfg.nb, 1, tb), F32)],
        compiler_params=pltpu.CompilerParams(dimension_semantics=("parallel", "parallel")),
    )(*args)


def _attn_dq(cfg, q, ks, v, sel, o, lse, do, dq_in, name):
    s, tq, tk, dk, tb, nsub = cfg.s, cfg.tq, cfg.tk, cfg.dk, cfg.tb, cfg.nsub
    has_sel = sel is not None
    has_in = dq_in is not None
    nkp = len(ks)

    def kern(*refs):
        refs = list(refs)
        q_ref, k_refs, v_ref = refs[0], refs[1:1 + nkp], refs[1 + nkp]
        p0 = 2 + nkp
        sel_ref = refs[p0] if has_sel else None
        p0 += has_sel
        o_ref, l_ref, do_ref = refs[p0:p0 + 3]
        p0 += 3
        in_ref = refs[p0] if has_in else None
        dq_ref, dr_ref = refs[-2:]
        h, i = pl.program_id(0), pl.program_id(1)
        part = [slice(r * tb, (r + 1) * tb) for r in range(nsub)]
        qs = [q_ref[p, :].astype(BF16) for p in part]
        ts = [i * tq + r * tb + lax.broadcasted_iota(jnp.int32, (tb, 1), 0) for r in range(nsub)]
        sels = [sel_ref[p, :].astype(BF16) for p in part] if has_sel else None
        dvecs, dobs, lses = [], [], []
        for r, p in enumerate(part):
            dov = do_ref[p, :]
            dvec = jnp.sum(dov * o_ref[p, :], axis=1, keepdims=True)
            dr_ref[0, r] = _to_row(dvec)
            dvecs.append(dvec)
            dobs.append(dov.astype(BF16))
            lses.append(l_ref[0, p, :])

        def load(k0):
            rows = pl.ds(k0, tk)
            return _load_keys(k_refs, rows), v_ref[rows, :].astype(BF16)

        def step(r, k0, kk, vv, emat, acc, diag):
            n = k0 + lax.broadcasted_iota(jnp.int32, (1, tk), 1)
            selx = _nn(sels[r], emat) if has_sel else None
            p, _ = _scores(cfg, _nt(qs[r], kk), ts[r], n, h, selx, diag, lses[r])
            ds = p * (_nt(dobs[r], vv) - dvecs[r]) * cfg.scale
            return acc + _nn(ds.astype(BF16), kk)

        def chunk(k0, accs, diag):
            kk, vv = load(k0)
            emat = _block_of_key(k0, tk, False) if has_sel else None
            return tuple(step(r, k0, kk, vv, emat, accs[r], diag) for r in range(nsub))

        accs = tuple(in_ref[p, :] if has_in else jnp.zeros((tb, dk), F32) for p in part)
        if cfg.causal:
            full = (i * tq) // tk
            accs = lax.fori_loop(0, full, lambda c, a: chunk(pl.multiple_of(c * tk, tk), a, False), accs)
            accs = chunk(pl.multiple_of(full * tk, tk), accs, True)
        elif cfg.mode == "win":
            starts = [pl.multiple_of(jnp.maximum(i * tq + r * tb - WIN, 0), tb) for r in range(nsub)]
            accs = tuple(step(r, k0, *load(k0), None, accs[r], True) for r, k0 in enumerate(starts))
        else:
            accs = chunk(0, accs, True)
        for r, p in enumerate(part):
            dq_ref[p, :] = accs[r]

    qs = pl.BlockSpec((tq, dk), lambda h, i: (i, h))
    ins = [pl.BlockSpec((tq, q.width), lambda h, i: (i, q.col(h)))]
    ins += [pl.BlockSpec((cfg.sk, p.width), lambda h, i, p=p: (0, p.col(h))) for p in ks]
    ins += [pl.BlockSpec((cfg.sk, HEAD_V), lambda h, i: (0, v.col(h)))]
    args = [q.arr] + [p.arr for p in ks] + [v.arr]
    if has_sel:
        ins.append(pl.BlockSpec((tq, LANE), lambda h, i: (i, 0)))
        args.append(sel)
    ins += [pl.BlockSpec((tq, HEAD_V), lambda h, i: (i, o.col(h))),
            pl.BlockSpec((1, tq, 1), lambda h, i: (h, i, 0)),
            pl.BlockSpec((tq, HEAD_V), lambda h, i: (i, do.col(h)))]
    args += [o.arr, lse, do.arr]
    if has_in:
        ins.append(qs)
        args.append(dq_in)
    return _pcall(
        kern, name=name, grid=(cfg.h, cfg.nq), in_specs=ins,
        out_specs=[qs, pl.BlockSpec((1, nsub, 1, tb), lambda h, i: (h, i, 0, 0))],
        out_shape=[jax.ShapeDtypeStruct((s, cfg.h * dk), F32),
                   jax.ShapeDtypeStruct((cfg.h, cfg.nb, 1, tb), F32)],
        compiler_params=pltpu.CompilerParams(dimension_semantics=("parallel", "parallel")),
    )(*args)


def _attn_dkv(cfg, q, ks, v, selt, lse_r, d_r, do, name):
    s, tq, tkb, dk, kb, ksub = cfg.s, cfg.tb, cfg.tkb, cfg.dk, cfg.kb, cfg.ksub
    nq = cfg.nb
    has_sel = selt is not None
    nkp = len(ks)
    outs = list(ks) + [v]

    def kern(*refs):
        k_refs, v_ref = refs[:nkp], refs[nkp]
        q_ref, do_ref, lr_ref, dr_ref = refs[nkp + 1:nkp + 5]
        st_ref = refs[nkp + 5] if has_sel else None
        out_refs = refs[-(nkp + 1):]
        j, h = pl.program_id(0), pl.program_id(1)
        k0 = j * tkb
        part = [slice(u * kb, (u + 1) * kb) for u in range(ksub)]
        kks = [_load_keys(k_refs, p) for p in part]
        vvs = [v_ref[p, :].astype(BF16) for p in part]
        ns = [k0 + u * kb + lax.broadcasted_iota(jnp.int32, (kb, 1), 0) for u in range(ksub)]
        emats = [_block_of_key(k0 + u * kb, kb, True) for u in range(ksub)] if has_sel else None

        def qblock(i, carry, diag):
            r0 = pl.multiple_of(i * tq, tq)
            qi = q_ref[pl.ds(r0, tq), :].astype(BF16)
            doi = do_ref[pl.ds(r0, tq), :].astype(BF16)
            t = i * tq + lax.broadcasted_iota(jnp.int32, (1, tq), 1)
            selt_i = st_ref[i].astype(BF16) if has_sel else None
            new = []
            for u in range(ksub):
                dk_acc, dv_acc = carry[u]
                selx = _nn(emats[u], selt_i) if has_sel else None
                pt, _ = _scores(cfg, _nt(kks[u], qi), t, ns[u], h, selx, diag, lr_ref[0, i])
                dv_acc = dv_acc + _nn(pt.astype(BF16), doi)
                dst = pt * (_nt(vvs[u], doi) - dr_ref[0, i]) * cfg.scale
                new.append((dk_acc + _nn(dst.astype(BF16), qi), dv_acc))
            return tuple(new)

        carry = ((jnp.zeros((kb, dk), F32), jnp.zeros((kb, HEAD_V), F32)),) * ksub
        masked = lambda i, cr: qblock(i, cr, True)
        if cfg.causal:
            first, past = k0 // tq, (k0 + tkb) // tq
            carry = lax.fori_loop(first, past, masked, carry)
            carry = lax.fori_loop(past, nq, lambda i, cr: qblock(i, cr, False), carry)
        elif cfg.mode == "win":
            carry = lax.fori_loop(k0 // tq, jnp.minimum((k0 + tkb + WIN - 2) // tq + 1, nq), masked, carry)
        else:
            carry = lax.fori_loop(0, nq, masked, carry)
        for u, (dk_acc, dv_acc) in enumerate(carry):
            vals, off = [], 0
            for p in ks:
                vals.append(dk_acc[:, off:off + p.width])
                off += p.width
            vals.append(dv_acc)
            for src, ref, val in zip(outs, out_refs, vals):
                if src.per_head:
                    ref[part[u], :] = val
                else:
                    @pl.when(h == 0)
                    def _(ref=ref, val=val, u=u):
                        ref[part[u], :] = val

                    @pl.when(h > 0)
                    def _(ref=ref, val=val, u=u):
                        ref[part[u], :] += val

    rowv = pl.BlockSpec((1, nq, 1, tq), lambda j, h: (h, 0, 0, 0))
    ins = [pl.BlockSpec((tkb, p.width), lambda j, h, p=p: (j, p.col(h))) for p in ks]
    ins += [pl.BlockSpec((tkb, HEAD_V), lambda j, h: (j, v.col(h))),
            pl.BlockSpec((s, q.width), lambda j, h: (0, q.col(h))),
            pl.BlockSpec((s, HEAD_V), lambda j, h: (0, do.col(h))), rowv, rowv]
    args = [p.arr for p in ks] + [v.arr, q.arr, do.arr, lse_r, d_r]
    if has_sel:
        ins.append(pl.BlockSpec((nq, LANE, tq), lambda j, h: (0, 0, 0)))
        args.append(selt)
    out_specs = [pl.BlockSpec((tkb, p.width), lambda j, h, p=p: (j, h if p.per_head else 0)) for p in outs]
    out_shape = [jax.ShapeDtypeStruct((cfg.sk, (cfg.h if p.per_head else 1) * p.width), F32) for p in outs]
    return _pcall(
        kern, name=name, grid=(cfg.sk // tkb, cfg.h), in_specs=ins, out_specs=out_specs, out_shape=out_shape,
        compiler_params=pltpu.CompilerParams(dimension_semantics=("parallel", "arbitrary")),
    )(*args)


def _attn_bwd(cfg, q, ks, v, sel, selt, o, lse, lse_r, do, dq_in, name):
    dq, d_r = _attn_dq(cfg, q, ks, v, sel, o, lse, do, dq_in, name + "_dq")
    res = _attn_dkv(cfg, q, ks, v, selt, lse_r, d_r, do, name + "_dkv")
    return dq, res[:-1], res[-1]


def _select(cfg, q, k_cmp, overlap):
    s, tq, sk = cfg.s, cfg.tb, cfg.sk
    n_s = s // SLC_LEN
    top_n = min(SLC_TOPN, n_s)

    def kern(q_ref, k_ref, ov_ref, sel_ref, selt_ref):
        i = pl.program_id(0)
        t = i * tq + lax.broadcasted_iota(jnp.int32, (tq, 1), 0)
        n = lax.broadcasted_iota(jnp.int32, (1, sk), 1)
        kk = k_ref[...]
        imp = jnp.zeros((tq, LANE), F32)
        for h in range(NSA_HEADS):
            sc, mask = _scores(cfg, _nt(q_ref[:, h * 256:(h + 1) * 256], kk), t, n, h, None, True)
            m = jnp.max(sc, axis=1, keepdims=True)
            e = jnp.where(mask, jnp.exp(sc - m), 0.0)
            p = e / (jnp.sum(e, axis=1, keepdims=True) + 1e-20)
            imp = imp + _nn(p.astype(BF16), ov_ref[...])
        j = lax.broadcasted_iota(jnp.int32, (tq, LANE), 1)
        cur = t >> 6
        forced = (j == 0) | (j == cur) | (j == cur - 1)
        imp = jnp.where(forced, 1e9, imp)
        imp = jnp.where(j > cur, -1e9, imp)
        imp = jnp.where(j >= n_s, -3e38, imp)

        def pick(_, carry):
            work, chosen = carry
            mx = jnp.max(work, axis=1, keepdims=True)
            first = jnp.min(jnp.where(work == mx, j, LANE), axis=1, keepdims=True)
            hit = j == first
            return jnp.where(hit, -3e38, work), jnp.where(hit, 1.0, chosen)

        _, chosen = lax.fori_loop(0, top_n, pick, (imp, jnp.zeros((tq, LANE), F32)))
        chosen = jnp.where(j <= cur, chosen, 0.0)
        sel_ref[...] = chosen
        selt_ref[0] = jnp.transpose(chosen)

    return _pcall(
        kern, name="nsa_select", grid=(cfg.nb,),
        in_specs=[pl.BlockSpec((tq, NSA_HEADS * 256), lambda i: (i, q.col0)),
                  pl.BlockSpec((sk, 256), lambda i: (0, 0)), pl.BlockSpec((sk, LANE), lambda i: (0, 0))],
        out_specs=[pl.BlockSpec((tq, LANE), lambda i: (i, 0)), pl.BlockSpec((1, LANE, tq), lambda i: (i, 0, 0))],
        out_shape=[jax.ShapeDtypeStruct((s, LANE), F32), jax.ShapeDtypeStruct((cfg.nb, LANE, tq), F32)],
    )(q.arr, k_cmp, overlap)


def _silu_grad(pre):
    sg = _sigmoid(pre)
    return sg * (1.0 + pre * (1.0 - sg))


def _compress_fwd(a_lo, a_hi, pe_lo, pe_hi, w1_lo, w1_hi, w2, name):
    n, dp = a_lo.shape[0], w2.shape[1]

    def kern(alo, ahi, plo, phi, w1l, w1h, w2r, out_ref, pre_ref):
        xl = (alo[...] + plo[...]).astype(BF16)
        xh = (ahi[...] + phi[...]).astype(BF16)
        pre = _nn(xl, w1l[...]) + _nn(xh, w1h[...])
        act = pre * _sigmoid(pre)
        out_ref[...] = _nn(act.astype(BF16), w2r[...]).astype(BF16)
        pre_ref[...] = pre

    return _pcall(kern, name=name,
                  out_shape=[jax.ShapeDtypeStruct((n, dp), BF16), jax.ShapeDtypeStruct((n, dp), F32)],
                  )(a_lo, a_hi, pe_lo, pe_hi, w1_lo, w1_hi, w2)


def _compress_bwd(a_lo, a_hi, pe_lo, pe_hi, w1_lo, w1_hi, w2, pre, pre_sh, dout, dout_sh, name):
    n, ln = a_lo.shape
    dp = w2.shape[1]

    def kern(alo, ahi, plo, phi, w1l, w1h, w2r, pre_ref, presh_ref, do_ref, dosh_ref,
             da_ref, dpl_ref, dph_ref, dw1l_ref, dw1h_ref, dw2_ref):
        prev = pre_ref[...]
        act = prev * _sigmoid(prev)
        dob = do_ref[...].astype(BF16)
        w2v = w2r[...]
        dpre = (_nt(dob, w2v) * _silu_grad(prev)).astype(BF16)
        dpre_sh = (_nt(dosh_ref[...].astype(BF16), w2v) * _silu_grad(presh_ref[...])).astype(BF16)
        dw2_ref[...] = _nn(act.T.astype(BF16), dob)
        xl = alo[...] + plo[...]
        xh = ahi[...] + phi[...]
        dw1l_ref[...] = _nn(xl.T.astype(BF16), dpre)
        dw1h_ref[...] = _nn(xh.T.astype(BF16), dpre)
        dal = _nt(dpre, w1l[...])
        dah_sh = _nt(dpre_sh, w1h[...])
        da_ref[...] = dal + dah_sh
        dpl_ref[...] = jnp.sum(dal, axis=0, keepdims=True)
        dph_ref[...] = jnp.sum(dah_sh, axis=0, keepdims=True)

    return _pcall(
        kern, name=name,
        out_shape=[jax.ShapeDtypeStruct((n, ln), F32), jax.ShapeDtypeStruct((1, ln), F32),
                   jax.ShapeDtypeStruct((1, ln), F32), jax.ShapeDtypeStruct((ln, dp), F32),
                   jax.ShapeDtypeStruct((ln, dp), F32), jax.ShapeDtypeStruct((dp, dp), F32)],
    )(a_lo, a_hi, pe_lo, pe_hi, w1_lo, w1_hi, w2, pre, pre_sh, dout, dout_sh)


def _nsa_combine(o_cmp, o_slc, o_win, gl):
    s, w = o_cmp.shape
    tr = _tile(s, 512)

    def kern(a_ref, b_ref, c_ref, g_ref, o_ref):
        g = _sigmoid(g_ref[...])
        for h in range(NSA_HEADS):
            cs = slice(h * HEAD_V, (h + 1) * HEAD_V)
            o_ref[:, cs] = (g[:, 3 * h:3 * h + 1] * a_ref[:, cs] + g[:, 3 * h + 1:3 * h + 2] * b_ref[:, cs]
                            + g[:, 3 * h + 2:3 * h + 3] * c_ref[:, cs])

    row = pl.BlockSpec((tr, w), lambda i: (i, 0))
    return _pcall(kern, name="nsa_combine", grid=(s // tr,),
                  in_specs=[row, row, row, pl.BlockSpec((tr, LANE), lambda i: (i, gl.col0))], out_specs=row,
                  out_shape=jax.ShapeDtypeStruct((s, w), F32))(o_cmp, o_slc, o_win, gl.arr)


def _nsa_combine_bwd(do_cat, o_cmp, o_slc, o_win, gl):
    s, w = o_cmp.shape
    tr = _tile(s, 512)

    def kern(d_ref, a_ref, b_ref, c_ref, g_ref, da_ref, db_ref, dc_ref, dg_ref):
        g = _sigmoid(g_ref[...])
        lane = lax.broadcasted_iota(jnp.int32, (tr, LANE), 1)
        dgl = jnp.zeros((tr, LANE), F32)
        for h in range(NSA_HEADS):
            cs = slice(h * HEAD_V, (h + 1) * HEAD_V)
            dv = d_ref[:, cs]
            for b, (src, dst) in enumerate(((a_ref, da_ref), (b_ref, db_ref), (c_ref, dc_ref))):
                gate = g[:, 3 * h + b:3 * h + b + 1]
                dst[:, cs] = gate * dv
                dgate = jnp.sum(dv * src[:, cs], axis=1, keepdims=True)
                dgl = jnp.where(lane == 3 * h + b, dgate * gate * (1.0 - gate), dgl)
        dg_ref[...] = dgl

    row = pl.BlockSpec((tr, w), lambda i: (i, 0))
    tab = pl.BlockSpec((tr, LANE), lambda i: (i, 0))
    return _pcall(kern, name="nsa_combine_bwd", grid=(s // tr,),
                  in_specs=[pl.BlockSpec((tr, w), lambda i: (i, 2)), row, row, row,
                            pl.BlockSpec((tr, LANE), lambda i: (i, gl.col0))],
                  out_specs=[row, row, row, tab],
                  out_shape=[jax.ShapeDtypeStruct((s, w), F32)] * 3 + [jax.ShapeDtypeStruct((s, LANE), F32)],
                  )(do_cat, o_cmp, o_slc, o_win, gl.arr)


def _gate_fwd(o_mla, o_nsa, o_mem, hp):
    s = o_mla.shape[0]
    tr = _tile(s, 256)

    def kern(a_ref, b_ref, c_ref, z_ref, u_ref):
        z = z_ref[...]
        sz = z * _sigmoid(z)
        u_ref[:, 0:1024] = (a_ref[...] * sz[:, 0:1024]).astype(BF16)
        u_ref[:, 1024:1536] = (b_ref[...] * sz[:, 1024:1536]).astype(BF16)
        u_ref[:, 1536:2048] = (c_ref[...] * sz[:, 1536:2048]).astype(BF16)

    return _pcall(
        kern, name="gate_fwd", grid=(s // tr,),
        in_specs=[pl.BlockSpec((tr, 1024), lambda i: (i, 0)), pl.BlockSpec((tr, 512), lambda i: (i, 0)),
                  pl.BlockSpec((tr, 512), lambda i: (i, 0)), pl.BlockSpec((tr, 2048), lambda i: (i, 2))],
        out_specs=pl.BlockSpec((tr, 2048), lambda i: (i, 0)),
        out_shape=jax.ShapeDtypeStruct((s, 2048), BF16))(o_mla, o_nsa, o_mem, hp)


def _gate_bwd(du, o_mla, o_nsa, o_mem, hp):
    s = du.shape[0]
    tr = _tile(s, 256)

    def kern(d_ref, a_ref, b_ref, c_ref, z_ref, do_ref, dz_ref):
        z = z_ref[...]
        sg = _sigmoid(z)
        sz = z * sg
        dsz = sg * (1.0 + z * (1.0 - sg))
        d = d_ref[...]
        do_ref[...] = d * sz
        dz_ref[:, 0:1024] = d[:, 0:1024] * a_ref[...] * dsz[:, 0:1024]
        dz_ref[:, 1024:1536] = d[:, 1024:1536] * b_ref[...] * dsz[:, 1024:1536]
        dz_ref[:, 1536:2048] = d[:, 1536:2048] * c_ref[...] * dsz[:, 1536:2048]

    wide = pl.BlockSpec((tr, 2048), lambda i: (i, 0))
    return _pcall(
        kern, name="gate_bwd", grid=(s // tr,),
        in_specs=[wide, pl.BlockSpec((tr, 1024), lambda i: (i, 0)), pl.BlockSpec((tr, 512), lambda i: (i, 0)),
                  pl.BlockSpec((tr, 512), lambda i: (i, 0)), pl.BlockSpec((tr, 2048), lambda i: (i, 2))],
        out_specs=[wide, wide],
        out_shape=[jax.ShapeDtypeStruct((s, 2048), F32)] * 2)(du, o_mla, o_nsa, o_mem, hp)


def _tile2d(rows, cols, arrays):
    if rows % 16 == 0:
        return _row_tile(rows, cols * arrays), cols
    want = max(LANE, BLOCK_BYTES // (rows * 4 * arrays) // LANE * LANE)
    tc = LANE
    for t in range(LANE, cols + 1, LANE):
        if cols % t == 0 and t <= want:
            tc = t
    return rows, tc


def _sum_slots(buf, name):
    n, rows, cols = buf.shape
    tr, tc = _tile2d(rows, cols, n)

    def kern(b_ref, o_ref):
        acc = b_ref[0].astype(F32)
        for i in range(1, n):
            acc = acc + b_ref[i].astype(F32)
        o_ref[...] = acc

    return _pcall(kern, name=name, grid=(rows // tr, cols // tc),
                  in_specs=[pl.BlockSpec((n, tr, tc), lambda i, j: (0, i, j))],
                  out_specs=pl.BlockSpec((tr, tc), lambda i, j: (i, j)),
                  out_shape=jax.ShapeDtypeStruct((rows, cols), F32))(buf)


def _pair_sum(g4, theirs, core, axis, name):
    n, rows, cols = theirs.shape
    tr, tc = _tile2d(rows, cols, 1)
    nbr, nbc = rows // tr, cols // tc

    def kern(c_ref, a_ref, b_ref, o_ref):
        o_ref[...] = (a_ref[...] + b_ref[...]).astype(BF16)

    blk = (1, tr, tc)
    mine = ((lambda s, i, j, c: (s, c[0] * nbr + i, j)) if axis == 0
            else (lambda s, i, j, c: (s, i, c[0] * nbc + j)))
    grid_spec = pltpu.PrefetchScalarGridSpec(
        num_scalar_prefetch=1, grid=(n, nbr, nbc),
        in_specs=[pl.BlockSpec(blk, mine), pl.BlockSpec(blk, lambda s, i, j, c: (s, i, j))],
        out_specs=pl.BlockSpec(blk, lambda s, i, j, c: (s, i, j)))
    return _pcall(kern, name=name, grid_spec=grid_spec,
                  out_shape=jax.ShapeDtypeStruct((n, rows, cols), BF16))(core, g4, theirs)


def _adamw(w, g, m, v, name):
    rows, cols = w.shape
    tr, tc = _tile2d(rows, cols, 4)
    bc1 = 1.0 - ADAM_B1 ** ADAM_STEP
    bc2 = 1.0 - ADAM_B2 ** ADAM_STEP

    def kern(w_ref, g_ref, m_ref, v_ref, d_ref, mo_ref, vo_ref):
        gv = g_ref[...]
        mn = ADAM_B1 * m_ref[...] + (1.0 - ADAM_B1) * gv
        vn = ADAM_B2 * v_ref[...] + (1.0 - ADAM_B2) * (gv * gv)
        d_ref[...] = -ADAM_LR * ((mn / bc1) / (jnp.sqrt(vn / bc2) + ADAM_EPS) + ADAM_WD * w_ref[...])
        mo_ref[...] = mn
        vo_ref[...] = vn

    blk = pl.BlockSpec((tr, tc), lambda i, j: (i, j))
    return _pcall(kern, name=name, grid=(rows // tr, cols // tc), in_specs=[blk] * 4, out_specs=[blk] * 3,
                  out_shape=[jax.ShapeDtypeStruct((rows, cols), F32)] * 3)(w, g, m, v)


ANY = pl.BlockSpec(memory_space=pl.ANY)


def _place():
    x, y, c = lax.axis_index("x"), lax.axis_index("y"), lax.axis_index("c")
    chips = [(1 - x, y), (x, 1 - y), (1 - x, 1 - y)]
    return x, y, c, chips


def _remote(src, dst, send_sem, recv_sem, to):
    return pltpu.make_async_remote_copy(src_ref=src, dst_ref=dst, send_sem=send_sem, recv_sem=recv_sem,
                                        device_id=to, device_id_type=MESH)


def _half(ref, lead, core, axis):
    size = ref.shape[len(lead) + axis] // 2
    cut = pl.ds(core * size, size)
    return ref.at[tuple(lead) + ((cut, slice(None)) if axis == 0 else (slice(None), cut))]


def _gather_shards(ws, axes):
    nw = len(ws)

    def body(*refs):
        w_refs, out_refs = refs[:nw], refs[nw:2 * nw]
        send_sems, recv_sems = refs[2 * nw:]
        x, y, c, chips = _place()
        me = 2 * x + y
        sibling = (x, y, 1 - c)

        def part(i, slot, core):
            return _half(out_refs[i], (slot,), core, axes[i])

        def copy(sem, src, dst, to):
            return _remote(src, dst, send_sems.at[sem], recv_sems.at[sem], to)

        first = [copy(j * nw + i, _half(w_refs[i], (), c, axes[i]), part(i, me, c), (*chip, c))
                 for j, chip in enumerate(chips) for i in range(nw)]
        for cp in first:
            cp.start()
        passed = []
        for j, (cx, cy) in enumerate(chips):
            slot = 2 * cx + cy
            for i in range(nw):
                copy(j * nw + i, part(i, slot, c), part(i, slot, c), (x, y, c)).wait_recv()
                fwd = copy((3 + j) * nw + i, part(i, slot, c), part(i, slot, c), sibling)
                fwd.start()
                passed.append(fwd)
        for j, (cx, cy) in enumerate(chips):
            slot = 2 * cx + cy
            for i in range(nw):
                copy((3 + j) * nw + i, part(i, slot, 1 - c), part(i, slot, 1 - c), (x, y, c)).wait_recv()
        for cp in first + passed:
            cp.wait_send()

    return _pcall(
        body, name="gather_shards", in_specs=[ANY] * nw, out_specs=[ANY] * nw,
        out_shape=[jax.ShapeDtypeStruct((4,) + w.shape, w.dtype) for w in ws],
        scratch_shapes=[pltpu.SemaphoreType.DMA((6 * nw,)), pltpu.SemaphoreType.DMA((6 * nw,))],
    )(*ws)


def _half_shape(shape, axis):
    return tuple(d // 2 if k == len(shape) - 2 + axis else d for k, d in enumerate(shape))


def _pair_exchange(gs, axes):
    nw = len(gs)

    def body(*refs):
        g_refs, out_refs = refs[:nw], refs[nw:2 * nw]
        send_sems, recv_sems = refs[2 * nw:]
        x, y, c, _ = _place()
        cps = []
        for i in range(nw):
            cp = _remote(_half(g_refs[i], (slice(None),), 1 - c, axes[i]), out_refs[i],
                         send_sems.at[i], recv_sems.at[i], (x, y, 1 - c))
            cp.start()
            cps.append(cp)
        for cp in cps:
            cp.wait()

    return _pcall(body, name="pair_exchange", in_specs=[ANY] * nw, out_specs=[ANY] * nw,
                  out_shape=[jax.ShapeDtypeStruct(_half_shape(g.shape, a), g.dtype) for g, a in zip(gs, axes)],
                  scratch_shapes=[pltpu.SemaphoreType.DMA((nw,)), pltpu.SemaphoreType.DMA((nw,))])(*gs)


def _chip_exchange(ps):
    nw = len(ps)

    def body(*refs):
        p_refs, out_refs = refs[:nw], refs[nw:2 * nw]
        send_sems, recv_sems, local_sems = refs[2 * nw:]
        x, y, c, chips = _place()
        me = 2 * x + y
        mine = [pltpu.make_async_copy(p_refs[i].at[me], out_refs[i].at[me], local_sems.at[i]) for i in range(nw)]
        for cp in mine:
            cp.start()
        sends = []
        for j, (cx, cy) in enumerate(chips):
            for i in range(nw):
                cp = _remote(p_refs[i].at[2 * cx + cy], out_refs[i].at[me], send_sems.at[j * nw + i],
                             recv_sems.at[j * nw + i], (cx, cy, c))
                cp.start()
                sends.append(cp)
        for j, (cx, cy) in enumerate(chips):
            slot = 2 * cx + cy
            for i in range(nw):
                _remote(out_refs[i].at[slot], out_refs[i].at[slot], send_sems.at[j * nw + i],
                        recv_sems.at[j * nw + i], (x, y, c)).wait_recv()
        for cp in sends:
            cp.wait_send()
        for cp in mine:
            cp.wait()

    return _pcall(body, name="chip_exchange", in_specs=[ANY] * nw, out_specs=[ANY] * nw,
                  out_shape=[jax.ShapeDtypeStruct(p.shape, p.dtype) for p in ps],
                  scratch_shapes=[pltpu.SemaphoreType.DMA((3 * nw,)), pltpu.SemaphoreType.DMA((3 * nw,)),
                                  pltpu.SemaphoreType.DMA((nw,))])(*ps)


def _half_exchange(ts, axes):
    nw = len(ts)

    def body(*refs):
        t_refs, out_refs = refs[:nw], refs[nw:2 * nw]
        send_sems, recv_sems = refs[2 * nw:]
        x, y, c, _ = _place()
        sends = []
        for i in range(nw):
            cp = _remote(t_refs[i], _half(out_refs[i], (), c, axes[i]), send_sems.at[i], recv_sems.at[i],
                         (x, y, 1 - c))
            cp.start()
            sends.append(cp)
        for i in range(nw):
            _remote(t_refs[i], _half(out_refs[i], (), 1 - c, axes[i]), send_sems.at[i], recv_sems.at[i],
                    (x, y, c)).wait_recv()
        for cp in sends:
            cp.wait_send()

    def whole(t, a):
        return tuple(2 * d if k == a else d for k, d in enumerate(t.shape))

    return _pcall(body, name="half_exchange", in_specs=[ANY] * nw, out_specs=[ANY] * nw,
                  out_shape=[jax.ShapeDtypeStruct(whole(t, a), t.dtype) for t, a in zip(ts, axes)],
                  scratch_shapes=[pltpu.SemaphoreType.DMA((nw,)), pltpu.SemaphoreType.DMA((nw,))])(*ts)


def _gather_all(v):
    rows, cols = v.shape

    def body(v_ref, out_ref, send_sems, recv_sems, local_sem):
        x, y, c, _ = _place()
        me = 4 * x + 2 * y + c
        mine = pltpu.make_async_copy(v_ref, out_ref.at[me], local_sem)
        mine.start()
        sends = []
        for d in range(1, 8):
            peer = (x ^ (d >> 2), y ^ ((d >> 1) & 1), c ^ (d & 1))
            cp = _remote(v_ref, out_ref.at[me], send_sems.at[d - 1], recv_sems.at[d - 1], peer)
            cp.start()
            sends.append(cp)
        for d in range(1, 8):
            slot = 4 * (x ^ (d >> 2)) + 2 * (y ^ ((d >> 1) & 1)) + (c ^ (d & 1))
            _remote(v_ref, out_ref.at[slot], send_sems.at[d - 1], recv_sems.at[d - 1], (x, y, c)).wait_recv()
        for cp in sends:
            cp.wait_send()
        mine.wait()

    return _pcall(body, name="gather_all", in_specs=[ANY], out_specs=ANY,
                  out_shape=jax.ShapeDtypeStruct((8, rows, cols), v.dtype),
                  scratch_shapes=[pltpu.SemaphoreType.DMA((7,)), pltpu.SemaphoreType.DMA((7,)),
                                  pltpu.SemaphoreType.DMA])(v)


def _pad_cols(a, width):
    return a if a.shape[1] == width else jnp.pad(a, ((0, 0), (0, width - a.shape[1])))


def _pad_rows(a, height):
    return a if a.shape[0] == height else jnp.pad(a, ((0, height - a.shape[0]), (0, 0)))


def _w_in_padded(wt):
    def seg(name, height=None):
        o, n = ORIG[name]
        return _pad_rows(wt[o:o + n], height or n)

    qn = wt[ORIG["q_nsa"][0]:ORIG["q_nsa"][0] + 768].reshape(NSA_HEADS, NSA_DK, -1)
    qn = jnp.pad(qn, ((0, 0), (0, 256 - NSA_DK), (0, 0))).reshape(NSA_HEADS * 256, -1)
    kr = seg("k_rope")
    zeros = jnp.zeros((PAD["z"] - (PAD["q_mem"] + 512), wt.shape[1]), wt.dtype)
    return jnp.concatenate(
        [seg("c_q"), seg("c_kv"), qn, seg("k_c", 256), seg("k_s", 256), seg("k_w", 256), kr, kr,
         seg("v_c"), seg("v_s"), seg("v_w"), seg("g_nsa", LANE), seg("q_mem"), zeros,
         seg("z_mla"), seg("z_nsa"), seg("z_mem")], axis=0)


def _w_in_unpadded(gt):
    def seg(name, n):
        return gt[PAD[name]:PAD[name] + n]

    qn = gt[PAD["q_nsa"]:PAD["q_nsa"] + 1024].reshape(NSA_HEADS, 256, -1)[:, :NSA_DK].reshape(768, -1)
    z = PAD["z"]
    return jnp.concatenate(
        [seg("c_q", 512), seg("c_kv", 512), seg("k_rope", 64), gt[z:z + 1024], qn, seg("k_c", 192),
         seg("v_c", 128), seg("k_s", 192), seg("v_s", 128), seg("k_w", 192), seg("v_w", 128),
         seg("g_nsa", 12), gt[z + 1024:z + 1536], seg("q_mem", 512), gt[z + 1536:z + 2048]], axis=0)


def _rope_tables(s):
    pos = jnp.arange(s, dtype=F32)
    inv_freq = ROPE_THETA ** (-jnp.arange(0, 64, 2, dtype=F32) / 64)
    ang = pos[:, None] * inv_freq[None, :]
    cos, sin = jnp.cos(ang), jnp.sin(ang)
    z = jnp.zeros((s, 64), F32)
    return jnp.concatenate([cos, cos, z], axis=1), jnp.concatenate([-sin, sin, z], axis=1)


def _overlap_table(s):
    n_c, n_s = s // CMP_STRIDE, s // SLC_LEN
    c0 = np.arange(n_c)[:, None] * CMP_STRIDE
    s0 = np.arange(LANE)[None, :] * SLC_LEN
    ov = (c0 < s0 + SLC_LEN) & (c0 + CMP_LEN > s0) & (np.arange(n_c)[:, None] < n_c - 1) & (np.arange(LANE)[None, :] < n_s)
    return jnp.asarray(ov.astype(np.float32), dtype=BF16)


def _shift_down(a):
    return jnp.concatenate([jnp.zeros((8, a.shape[1]), a.dtype), a], axis=0)[7:7 + a.shape[0]]


def _shift_up(a):
    return jnp.concatenate([a, jnp.zeros((8, a.shape[1]), a.dtype)], axis=0)[1:1 + a.shape[0]]


def _local_step(x, mem, target, w):
    s = x.shape[0]
    cs, sn = _rope_tables(s)
    t_ = jnp.transpose

    w_in_p = _w_in_padded(w["w_in_t"])
    xn, rstd_x = _rms_fwd(_Src(x, D_MODEL), w["norm_g"], "norm_x")
    hp, hpb = _mm(xn, w_in_p, "in_proj", mode="nt", second_dtype=BF16)

    w_uq3 = w["w_uq"].reshape(512, MLA_HEADS, 192)
    w_uq_p = jnp.concatenate([w_uq3, w_uq3[:, :, 128:]], axis=2).reshape(512, MLA_HEADS * 256)
    w_ukv_p = t_(w["w_ukv"].reshape(512, MLA_HEADS, 2, 128), (0, 2, 1, 3)).reshape(512, 2048)
    c_q, c_kv = _Src(hp, 512, 0), _Src(hp, 512, 1)
    cqn, rstd_q = _rms_fwd(c_q, w["q_norm_g"], "norm_q")
    ckvn, rstd_kv = _rms_fwd(c_kv, w["kv_norm_g"], "norm_kv")
    q_lin = _mm(cqn, w_uq_p, "mla_q_proj")
    kvb = _mm(ckvn, w_ukv_p, "mla_kv_proj", out_dtype=BF16)
    q_mla = _rope_fwd(_Src(q_lin, MLA_HEADS * 256), cs, sn, MLA_HEADS, 256, LANE, "rope_q")
    k_pe = _rope_fwd(_Src(hp, LANE, PAD["k_rope"] // LANE), cs, sn, 1, LANE, 0, "rope_k")
    mla = _Attn("mla", s, s, MLA_HEADS, 256)
    mla_q, mla_v = _Src(q_mla, 256), _Src(kvb, LANE, MLA_HEADS)
    mla_k = [_Src(kvb, LANE), _Src(k_pe, LANE, 0, False)]
    o_mla, l_mla, lr_mla = _attn_fwd(mla, mla_q, mla_k, mla_v, None, "mla_fwd")

    sk = s // CMP_STRIDE
    pe_k, pe_v = w["cmp_pe_k"], w["cmp_pe_v"]
    w1k = _pad_cols(w["cmp_w1k"], 256)
    w2k = jnp.pad(w["cmp_w2k"], ((0, 64), (0, 64))).astype(BF16)
    w1v, w2v = w["cmp_w1v"], w["cmp_w2v"].astype(BF16)
    half_k, half_v = CMP_STRIDE * NSA_DK, CMP_STRIDE * HEAD_V
    ak = hp[:, PAD["k_c"]:PAD["k_c"] + NSA_DK].reshape(sk, half_k)
    av = hp[:, PAD["v_c"]:PAD["v_c"] + HEAD_V].reshape(sk, half_v)
    ck_args = (ak, _shift_up(ak), pe_k[:CMP_STRIDE].reshape(1, half_k), pe_k[CMP_STRIDE:].reshape(1, half_k),
               w1k[:half_k], w1k[half_k:], w2k)
    cv_args = (av, _shift_up(av), pe_v[:CMP_STRIDE].reshape(1, half_v), pe_v[CMP_STRIDE:].reshape(1, half_v),
               w1v[:half_v], w1v[half_v:], w2v)
    k_cmp, pre_k = _compress_fwd(*ck_args, "compress_k")
    v_cmp, pre_v = _compress_fwd(*cv_args, "compress_v")
    cmp_ = _Attn("cmp", s, sk, NSA_HEADS, 256)
    slc = _Attn("slc", s, s, NSA_HEADS, 256)
    win = _Attn("win", s, s, NSA_HEADS, 256)
    nsa_q = _Src(hpb, 256, PAD["q_nsa"] // 256)
    cmp_k, cmp_v = [_Src(k_cmp, 256, 0, False)], _Src(v_cmp, HEAD_V, 0, False)
    slc_k, slc_v = [_Src(hpb, 256, PAD["k_s"] // 256, False)], _Src(hpb, HEAD_V, PAD["v_s"] // HEAD_V, False)
    win_k, win_v = [_Src(hpb, 256, PAD["k_w"] // 256, False)], _Src(hpb, HEAD_V, PAD["v_w"] // HEAD_V, False)
    o_cmp, l_cmp, lr_cmp = _attn_fwd(cmp_, nsa_q, cmp_k, cmp_v, None, "cmp_fwd")
    sel, selt = _select(cmp_, _Src(hpb, NSA_HEADS * 256, PAD["q_nsa"] // (NSA_HEADS * 256)), k_cmp,
                        _overlap_table(s))
    o_slc, l_slc, lr_slc = _attn_fwd(slc, nsa_q, slc_k, slc_v, sel, "slc_fwd")
    o_win, l_win, lr_win = _attn_fwd(win, nsa_q, win_k, win_v, None, "win_fwd")
    gl = _Src(hp, LANE, PAD["g_nsa"] // LANE)
    o_nsa = _nsa_combine(o_cmp, o_slc, o_win, gl)

    mn, rstd_m = _rms_fwd(_Src(mem, D_MODEL), w["mem_norm_g"], "norm_mem")
    kvm = _mm(mn, w["w_mem_kv"], "mem_kv_proj", out_dtype=BF16)
    mem_ = _Attn("mem", s, mem.shape[0], MEM_HEADS, LANE)
    mem_q, mem_k, mem_v = _Src(hpb, LANE, PAD["q_mem"] // LANE), [_Src(kvm, LANE)], _Src(kvm, LANE, MEM_HEADS)
    o_mem, l_mem, lr_mem = _attn_fwd(mem_, mem_q, mem_k, mem_v, None, "mem_fwd")

    u = _gate_fwd(o_mla, o_nsa, o_mem, hp)
    proj = _mm(u, w["w_out"], "out_proj")
    dy, g_final, loss = _final_loss(x, proj, w["final_norm_g"].reshape(1, -1), target)

    g_w_out = _mm(u, dy, "out_proj_dw", mode="tn")
    du = _mm(dy, w["w_out"], "out_proj_dx", mode="nt")
    do_cat, dz = _gate_bwd(du, o_mla, o_nsa, o_mem, hp)

    dq_mem, (dk_mem,), dv_mem = _attn_bwd(mem_, mem_q, mem_k, mem_v, None, None, _Src(o_mem, HEAD_V), l_mem,
                                          lr_mem, _Src(do_cat, HEAD_V, 12), None, "mem_bwd")
    dkvm = jnp.concatenate([dk_mem, dv_mem], axis=1)
    g_w_mem_kv = _mm(mn, dkvm, "mem_kv_dw", mode="tn")
    dmn = _mm(dkvm, w["w_mem_kv"], "mem_kv_dx", mode="nt")
    _, g_mem_norm = _rms_bwd(_Src(mem, D_MODEL), w["mem_norm_g"], rstd_m, dmn, None, "norm_mem_bwd")

    do_cmp, do_slc, do_win, dgl = _nsa_combine_bwd(do_cat, o_cmp, o_slc, o_win, gl)
    dq_n, (dk_cmp,), dv_cmp = _attn_bwd(cmp_, nsa_q, cmp_k, cmp_v, None, None, _Src(o_cmp, HEAD_V), l_cmp,
                                        lr_cmp, _Src(do_cmp, HEAD_V), None, "cmp_bwd")
    dq_n, (dk_s,), dv_s = _attn_bwd(slc, nsa_q, slc_k, slc_v, sel, selt, _Src(o_slc, HEAD_V), l_slc, lr_slc,
                                    _Src(do_slc, HEAD_V), dq_n, "slc_bwd")
    dq_n, (dk_w,), dv_w = _attn_bwd(win, nsa_q, win_k, win_v, None, None, _Src(o_win, HEAD_V), l_win, lr_win,
                                    _Src(do_win, HEAD_V), dq_n, "win_bwd")
    dak, dpk_lo, dpk_hi, dw1k_lo, dw1k_hi, g_w2k = _compress_bwd(
        *ck_args, pre_k, _shift_down(pre_k), dk_cmp, _shift_down(dk_cmp), "compress_k_bwd")
    dav, dpv_lo, dpv_hi, dw1v_lo, dw1v_hi, g_w2v = _compress_bwd(
        *cv_args, pre_v, _shift_down(pre_v), dv_cmp, _shift_down(dv_cmp), "compress_v_bwd")
    g_pe_k = jnp.concatenate([dpk_lo.reshape(CMP_STRIDE, NSA_DK), dpk_hi.reshape(CMP_STRIDE, NSA_DK)], axis=0)
    g_pe_v = jnp.concatenate([dpv_lo.reshape(CMP_STRIDE, HEAD_V), dpv_hi.reshape(CMP_STRIDE, HEAD_V)], axis=0)
    g_w1k = jnp.concatenate([dw1k_lo, dw1k_hi], axis=0)[:, :NSA_DK]
    g_w1v = jnp.concatenate([dw1v_lo, dw1v_hi], axis=0)
    dk_c = _pad_cols(dak.reshape(s, NSA_DK), 256)
    dv_c = dav.reshape(s, HEAD_V)

    dq_m, (dk_nope, dk_pe), dv_m = _attn_bwd(mla, mla_q, mla_k, mla_v, None, None, _Src(o_mla, HEAD_V), l_mla,
                                             lr_mla, _Src(do_cat, HEAD_V), None, "mla_bwd")
    dq_lin = _rope_bwd_q(dq_m, cs, sn)
    dkv_lin, d_krope = _rope_bwd_k(dk_nope, dk_pe, dv_m, cs, sn)
    g_w_uq_p = _mm(cqn, dq_lin, "mla_q_dw", mode="tn")
    dcqn = _mm(dq_lin, w_uq_p, "mla_q_dx", mode="nt")
    g_w_ukv_p = _mm(ckvn, dkv_lin, "mla_kv_dw", mode="tn")
    dckvn = _mm(dkv_lin, w_ukv_p, "mla_kv_dx", mode="nt")
    dc_q, g_q_norm = _rms_bwd(c_q, w["q_norm_g"], rstd_q, dcqn, None, "norm_q_bwd")
    dc_kv, g_kv_norm = _rms_bwd(c_kv, w["kv_norm_g"], rstd_kv, dckvn, None, "norm_kv_bwd")
    g_w_uq = g_w_uq_p.reshape(512, MLA_HEADS, 256)[:, :, :192].reshape(512, MLA_HEADS * 192)
    g_w_ukv = t_(g_w_ukv_p.reshape(512, 2, MLA_HEADS, 128), (0, 2, 1, 3)).reshape(512, 2048)

    dhp = jnp.concatenate(
        [dc_q, dc_kv, dq_n, dk_c, dk_s, dk_w, d_krope, dv_c, dv_s, dv_w, dgl, dq_mem,
         jnp.zeros((s, PAD["z"] - (PAD["q_mem"] + 512)), F32), dz], axis=1)
    g_w_in_t = _w_in_unpadded(_mm(dhp, xn, "in_proj_dw", mode="tn"))
    dxn = _mm(dhp, w_in_p, "in_proj_dx")
    grad_x, g_norm = _rms_bwd(_Src(x, D_MODEL), w["norm_g"], rstd_x, dxn, dy, "norm_x_bwd")

    grads = dict(norm_g=g_norm, w_in_t=g_w_in_t, q_norm_g=g_q_norm, w_uq=g_w_uq, kv_norm_g=g_kv_norm,
                 w_ukv=g_w_ukv, cmp_pe_k=g_pe_k, cmp_pe_v=g_pe_v, cmp_w1k=g_w1k, cmp_w2k=g_w2k[:NSA_DK, :NSA_DK],
                 cmp_w1v=g_w1v, cmp_w2v=g_w2v, mem_norm_g=g_mem_norm, w_mem_kv=g_w_mem_kv, w_out=g_w_out,
                 final_norm_g=g_final.reshape(-1))
    return loss[0, 0], grad_x, grads


def kernel(x, mem, norm_g, w_in, q_norm_g, w_uq, kv_norm_g, w_ukv, cmp_pe_k, cmp_pe_v, cmp_w1k, cmp_w2k, cmp_w1v, cmp_w2v, mem_norm_g, w_mem_kv, w_out, final_norm_g, loss_target, m_norm_g, m_w_in, m_q_norm_g, m_w_uq, m_kv_norm_g, m_w_ukv, m_cmp_pe_k, m_cmp_pe_v, m_cmp_w1k, m_cmp_w2k, m_cmp_w1v, m_cmp_w2v, m_mem_norm_g, m_w_mem_kv, m_w_out, m_final_norm_g, v_norm_g, v_w_in, v_q_norm_g, v_w_uq, v_kv_norm_g, v_w_ukv, v_cmp_pe_k, v_cmp_pe_v, v_cmp_w1k, v_cmp_w2k, v_cmp_w1v, v_cmp_w2v, v_mem_norm_g, v_w_mem_kv, v_w_out, v_final_norm_g):
    args = dict(locals())
    wts = {n: args[n] for n in WEIGHTS}
    loc = {n: (a if n == "final_norm_g" else a[0]) for n, a in wts.items()}

    def to_x(n, a):
        return a.T if n == "w_in" else a

    split = [1 if n == "w_in" else 0 for n in SHARDED]

    own = [to_x(n, loc[n]).astype(BF16) for n in SHARDED]
    chip = 2 * lax.axis_index("x") + lax.axis_index("y")
    gathered = [lax.dynamic_update_slice(gw, a[None], (chip, 0, 0))
                for gw, a in zip(_gather_shards(own, split), own)]
    full = {n: loc[n].reshape(1, -1) if loc[n].ndim == 1 else loc[n] for n in REPLICATED}
    for n, gw in zip(SHARDED, gathered):
        if n == "w_in":
            full["w_in_t"] = gw.reshape(4 * gw.shape[1], gw.shape[2])
        elif SHARD_AXIS[n] == 0:
            full[n] = gw.reshape(4 * gw.shape[1], gw.shape[2])
        else:
            full[n] = jnp.concatenate([gw[j] for j in range(4)], axis=1)

    loss, grad_x, g = _local_step(x[0], mem[0], loss_target[0], full)
    loss = lax.psum(loss, ("x", "y", "c"))

    def slots(n):
        a = g["w_in_t"] if n == "w_in" else g[n]
        if n == "w_in" or SHARD_AXIS[n] == 0:
            return a.reshape(4, a.shape[0] // 4, a.shape[1])
        width = a.shape[1] // 4
        return jnp.stack([a[:, j * width:(j + 1) * width] for j in range(4)])

    gs = [slots(n) for n in SHARDED]
    core = lax.axis_index("c").astype(jnp.int32).reshape(1)
    theirs = _pair_exchange(gs, split)
    pairs = [_pair_sum(a, b, core, ax, "pair_sum_" + n) for n, a, b, ax in zip(SHARDED, gs, theirs, split)]
    from_chips = _chip_exchange(pairs)
    mine = [_sum_slots(b, "chip_sum_" + n) for n, b in zip(SHARDED, from_chips)]
    g_sh = [lax.dynamic_update_slice(o, t, (core[0] * t.shape[0], 0) if ax == 0 else (0, core[0] * t.shape[1]))
            for o, t, ax in zip(_half_exchange(mine, split), mine, split)]

    n_rep = sum(int(np.prod(loc[n].shape)) for n in REPLICATED)
    rows_rep = -(-n_rep // (8 * LANE)) * 8

    def rep_pack(parts):
        flat = jnp.concatenate([p.reshape(-1) for p in parts])
        return jnp.pad(flat, (0, rows_rep * LANE - n_rep)).reshape(rows_rep, LANE)

    g_rep = _sum_slots(_gather_all(rep_pack([g[n] for n in REPLICATED])), "replica_sum")
    d_rp, m_rp, v_rp = _adamw(rep_pack([wts[n] for n in REPLICATED]), g_rep,
                              rep_pack([args["m_" + n] for n in REPLICATED]),
                              rep_pack([args["v_" + n] for n in REPLICATED]), "adamw_replicated")

    def rep_unpack(buf):
        flat, out, o = buf.reshape(-1), {}, 0
        for n in REPLICATED:
            size = int(np.prod(wts[n].shape))
            out[n] = flat[o:o + size].reshape(wts[n].shape)
            o += size
        return out

    outs = {k: rep_unpack(b) for k, b in (("g", g_rep), ("d", d_rp), ("m", m_rp), ("v", v_rp))}
    for n, gn in zip(SHARDED, g_sh):
        d, mo, vo = _adamw(to_x(n, loc[n]), gn, to_x(n, args["m_" + n][0]), to_x(n, args["v_" + n][0]),
                           "adamw_" + n)
        for k, a in (("g", gn), ("d", d), ("m", mo), ("v", vo)):
            outs[k][n] = to_x(n, a).reshape(wts[n].shape)

    return (loss, grad_x[None], *[outs["g"][n] for n in WEIGHTS], *[outs["d"][n] for n in WEIGHTS],
            *[outs["m"][n] for n in WEIGHTS], *[outs["v"][n] for n in WEIGHTS])
```

```python
from typing import NamedTuple

import numpy as np
import jax
import jax.numpy as jnp
from jax import lax
from jax.experimental import pallas as pl
from jax.experimental.pallas import tpu as pltpu

F32 = jnp.float32
BF16 = jnp.bfloat16
MESH = pl.DeviceIdType.MESH

D_MODEL = 2048
EPS = 1e-6
LANE = 128
HEAD_V = 128
MLA_HEADS = 8
NSA_HEADS = 4
MEM_HEADS = 4
NSA_DK = 192
CMP_STRIDE = 16
CMP_LEN = 32
SLC_LEN = 64
SLC_TOPN = 16
WIN = 512
NEG = -1e30
LOG2E = 1.4426950408889634
ROPE_THETA = 10000.0
BLOCK_BYTES = 2 << 20

ORIG = dict(c_q=(0, 512), c_kv=(512, 512), k_rope=(1024, 64), z_mla=(1088, 1024),
            q_nsa=(2112, 768), k_c=(2880, 192), v_c=(3072, 128), k_s=(3200, 192),
            v_s=(3392, 128), k_w=(3520, 192), v_w=(3712, 128), g_nsa=(3840, 12),
            z_nsa=(3852, 512), q_mem=(4364, 512), z_mem=(4876, 512))
PAD = dict(c_q=0, c_kv=512, q_nsa=1024, k_c=2048, k_s=2304, k_w=2560, k_rope=2816, v_c=2944,
           v_s=3072, v_w=3200, g_nsa=3328, q_mem=3456, z=4096)
D_PAD = 6144

ADAM_LR, ADAM_B1, ADAM_B2, ADAM_EPS, ADAM_WD, ADAM_STEP = 0.001, 0.9, 0.999, 1e-08, 0.01, 10

SHARDED = ("w_in", "w_uq", "w_ukv", "cmp_w1k", "cmp_w1v", "w_mem_kv", "w_out")
SHARD_AXIS = dict(w_in=1, w_uq=1, w_ukv=1, cmp_w1k=0, cmp_w1v=0, w_mem_kv=0, w_out=0)
REPLICATED = ("norm_g", "q_norm_g", "kv_norm_g", "cmp_pe_k", "cmp_pe_v", "cmp_w2k", "cmp_w2v",
              "mem_norm_g", "final_norm_g")
WEIGHTS = ("norm_g", "w_in", "q_norm_g", "w_uq", "kv_norm_g", "w_ukv", "cmp_pe_k", "cmp_pe_v",
           "cmp_w1k", "cmp_w2k", "cmp_w1v", "cmp_w2v", "mem_norm_g", "w_mem_kv", "w_out",
           "final_norm_g")


def _pcall(kernel, **kw):
    return pl.pallas_call(kernel, **kw)


def _tile(n, pref):
    if n <= pref:
        return n
    for t in range(pref, LANE - 1, -LANE):
        if n % t == 0:
            return t
    raise ValueError((n, pref))


def _row_tile(rows, cols, itemsize=4):
    want = max(16, BLOCK_BYTES // (cols * itemsize))
    if rows <= want:
        return rows
    t = 16
    best = rows
    while t <= want:
        if rows % t == 0:
            best = t
        t *= 2
    return best


def _nt(a, b):
    return lax.dot_general(a, b, (((1,), (1,)), ((), ())), preferred_element_type=F32)


def _tn(a, b):
    return lax.dot_general(a, b, (((0,), (0,)), ((), ())), preferred_element_type=F32)


def _nn(a, b):
    return jnp.dot(a, b, preferred_element_type=F32)


def _sigmoid(x):
    return 1.0 / (1.0 + jnp.exp(-x))


class _Src(NamedTuple):
    arr: jax.Array
    width: int
    col0: int = 0
    per_head: bool = True

    def col(self, h):
        return self.col0 + h if self.per_head else self.col0


def _mm(a, b, name, mode="nn", out_dtype=F32, second_dtype=None):
    if mode == "tn":
        k, m = a.shape
    else:
        m, k = a.shape
    if mode == "nt":
        n, k2 = b.shape
    else:
        k2, n = b.shape
    assert k == k2, (a.shape, b.shape, mode)
    tm, tn, tk = _tile(m, 1024), _tile(n, 1024), _tile(k, 2048)
    nk = k // tk
    assert nk == 1 or (out_dtype == F32 and second_dtype is None)
    dot = {"nn": _nn, "nt": _nt, "tn": _tn}[mode]

    def kern(a_ref, b_ref, o_ref, *more):
        r = dot(a_ref[...].astype(BF16), b_ref[...].astype(BF16))
        if nk == 1:
            o_ref[...] = r.astype(out_dtype)
            if more:
                more[0][...] = r.astype(second_dtype)
        else:
            kk = pl.program_id(2)

            @pl.when(kk == 0)
            def _():
                o_ref[...] = r

            @pl.when(kk > 0)
            def _():
                o_ref[...] += r

    a_spec = (pl.BlockSpec((tk, tm), lambda i, j, kk: (kk, i)) if mode == "tn"
              else pl.BlockSpec((tm, tk), lambda i, j, kk: (i, kk)))
    b_spec = (pl.BlockSpec((tn, tk), lambda i, j, kk: (j, kk)) if mode == "nt"
              else pl.BlockSpec((tk, tn), lambda i, j, kk: (kk, j)))
    o_spec = pl.BlockSpec((tm, tn), lambda i, j, kk: (i, j))
    out_shape = jax.ShapeDtypeStruct((m, n), out_dtype)
    if second_dtype is not None:
        o_spec = [o_spec, o_spec]
        out_shape = [out_shape, jax.ShapeDtypeStruct((m, n), second_dtype)]
    return _pcall(
        kern, name=name, grid=(m // tm, n // tn, nk), in_specs=[a_spec, b_spec], out_specs=o_spec,
        out_shape=out_shape,
        compiler_params=pltpu.CompilerParams(dimension_semantics=("parallel", "parallel", "arbitrary")),
    )(a, b)


def _rms_fwd(x, g, name):
    r, d = x.arr.shape[0], x.width
    tr = _tile(r, 512)

    def kern(x_ref, g_ref, y_ref, r_ref):
        xv = x_ref[...]
        rstd = lax.rsqrt(jnp.mean(xv * xv, axis=-1, keepdims=True) + EPS)
        y_ref[...] = (xv * rstd * g_ref[...]).astype(BF16)
        r_ref[...] = rstd

    return _pcall(
        kern, name=name, grid=(r // tr,),
        in_specs=[pl.BlockSpec((tr, d), lambda i: (i, x.col0)), pl.BlockSpec((1, d), lambda i: (0, 0))],
        out_specs=[pl.BlockSpec((tr, d), lambda i: (i, 0)), pl.BlockSpec((tr, 1), lambda i: (i, 0))],
        out_shape=[jax.ShapeDtypeStruct((r, d), BF16), jax.ShapeDtypeStruct((r, 1), F32)],
    )(x.arr, g)


def _rms_bwd(x, g, rstd, dy, add, name):
    r, d = x.arr.shape[0], x.width
    tr = _tile(r, 256)
    has_add = add is not None

    def kern(*refs):
        if has_add:
            x_ref, g_ref, r_ref, dy_ref, add_ref, dx_ref, dg_ref = refs
        else:
            x_ref, g_ref, r_ref, dy_ref, dx_ref, dg_ref = refs
        rs = r_ref[...]
        xhat = x_ref[...] * rs
        dyv = dy_ref[...]
        dyg = dyv * g_ref[...]
        c = jnp.mean(dyg * xhat, axis=-1, keepdims=True)
        dx = rs * (dyg - xhat * c)
        if has_add:
            dx = dx + add_ref[...]
        dx_ref[...] = dx
        part = jnp.sum(dyv * xhat, axis=0, keepdims=True)

        @pl.when(pl.program_id(0) == 0)
        def _():
            dg_ref[...] = part

        @pl.when(pl.program_id(0) > 0)
        def _():
            dg_ref[...] += part

    row = pl.BlockSpec((tr, d), lambda i: (i, 0))
    vec = pl.BlockSpec((1, d), lambda i: (0, 0))
    ins = [pl.BlockSpec((tr, d), lambda i: (i, x.col0)), vec, pl.BlockSpec((tr, 1), lambda i: (i, 0)), row]
    ins += [row] if has_add else []
    args = (x.arr, g, rstd, dy) + ((add,) if has_add else ())
    return _pcall(
        kern, name=name, grid=(r // tr,), in_specs=ins, out_specs=[row, vec],
        out_shape=[jax.ShapeDtypeStruct((r, d), F32), jax.ShapeDtypeStruct((1, d), F32)],
        compiler_params=pltpu.CompilerParams(dimension_semantics=("arbitrary",)),
    )(*args)


def _final_loss(x, proj, g, target):
    r, d = x.shape
    tr = _tile(r, 256)

    def kern(x_ref, p_ref, g_ref, t_ref, dy_ref, dg_ref, loss_ref):
        y = x_ref[...] + p_ref[...]
        rs = lax.rsqrt(jnp.mean(y * y, axis=-1, keepdims=True) + EPS)
        yhat = y * rs
        gv = g_ref[...]
        e = yhat * gv - t_ref[...]
        lpart = 0.5 * jnp.sum(jnp.mean(e * e, axis=-1, keepdims=True), axis=0, keepdims=True)
        dout = e * (1.0 / d)
        dyg = dout * gv
        c = jnp.mean(dyg * yhat, axis=-1, keepdims=True)
        dy_ref[...] = rs * (dyg - yhat * c)
        gpart = jnp.sum(dout * yhat, axis=0, keepdims=True)
        lrow = jnp.broadcast_to(lpart, (1, LANE))

        @pl.when(pl.program_id(0) == 0)
        def _():
            dg_ref[...] = gpart
            loss_ref[...] = lrow

        @pl.when(pl.program_id(0) > 0)
        def _():
            dg_ref[...] += gpart
            loss_ref[...] += lrow

    row = pl.BlockSpec((tr, d), lambda i: (i, 0))
    vec = pl.BlockSpec((1, d), lambda i: (0, 0))
    return _pcall(
        kern, name="final_loss", grid=(r // tr,), in_specs=[row, row, vec, row],
        out_specs=[row, vec, pl.BlockSpec((1, LANE), lambda i: (0, 0))],
        out_shape=[jax.ShapeDtypeStruct((r, d), F32), jax.ShapeDtypeStruct((1, d), F32),
                   jax.ShapeDtypeStruct((1, LANE), F32)],
        compiler_params=pltpu.CompilerParams(dimension_semantics=("arbitrary",)),
    )(x, proj, g, target)


def _rope_fwd(x, cs, sn, nh, width, off, name):
    s = x.arr.shape[0]
    tr = _tile(s, 512)

    def kern(x_ref, c_ref, s_ref, o_ref):
        cv, sv = c_ref[...], s_ref[...]
        for h in range(nh):
            b = h * width
            if off:
                o_ref[:, b:b + off] = x_ref[:, b:b + off].astype(BF16)
            xr = x_ref[:, b + off:b + off + LANE]
            o_ref[:, b + off:b + off + LANE] = (xr * cv + pltpu.roll(xr, 32, 1) * sv).astype(BF16)

    tab = pl.BlockSpec((tr, LANE), lambda i: (i, 0))
    return _pcall(
        kern, name=name, grid=(s // tr,),
        in_specs=[pl.BlockSpec((tr, nh * width), lambda i: (i, x.col0)), tab, tab],
        out_specs=pl.BlockSpec((tr, nh * width), lambda i: (i, 0)),
        out_shape=jax.ShapeDtypeStruct((s, nh * width), BF16),
    )(x.arr, cs, sn)


def _rope_grad(d, cv, sv):
    g2 = d * sv
    g2 = g2 + pltpu.roll(g2, 64, 1)
    lane = lax.broadcasted_iota(jnp.int32, d.shape, 1)
    return jnp.where(lane < 64, d * cv + pltpu.roll(g2, 32, 1), 0.0)


def _rope_bwd_q(dq, cs, sn):
    s, w = dq.shape
    tr = _tile(s, 512)
    nh = w // 256

    def kern(d_ref, c_ref, s_ref, o_ref):
        cv, sv = c_ref[...], s_ref[...]
        for h in range(nh):
            b = h * 256
            o_ref[:, b:b + LANE] = d_ref[:, b:b + LANE]
            o_ref[:, b + LANE:b + 256] = _rope_grad(d_ref[:, b + LANE:b + 256], cv, sv)

    row = pl.BlockSpec((tr, w), lambda i: (i, 0))
    tab = pl.BlockSpec((tr, LANE), lambda i: (i, 0))
    return _pcall(kern, name="rope_bwd_q", grid=(s // tr,), in_specs=[row, tab, tab], out_specs=row,
                  out_shape=jax.ShapeDtypeStruct((s, w), F32))(dq, cs, sn)


def _rope_bwd_k(dk_nope, dk_pe, dv, cs, sn):
    s, w = dk_nope.shape
    tr = _tile(s, 512)

    def kern(dk_ref, dp_ref, dv_ref, c_ref, s_ref, okv_ref, okr_ref):
        okv_ref[:, :w] = dk_ref[...]
        okv_ref[:, w:] = dv_ref[...]
        okr_ref[...] = _rope_grad(dp_ref[...], c_ref[...], s_ref[...])

    tab = pl.BlockSpec((tr, LANE), lambda i: (i, 0))
    wide = pl.BlockSpec((tr, w), lambda i: (i, 0))
    return _pcall(
        kern, name="rope_bwd_k", grid=(s // tr,), in_specs=[wide, tab, wide, tab, tab],
        out_specs=[pl.BlockSpec((tr, 2 * w), lambda i: (i, 0)), tab],
        out_shape=[jax.ShapeDtypeStruct((s, 2 * w), F32), jax.ShapeDtypeStruct((s, LANE), F32)],
    )(dk_nope, dk_pe, dv, cs, sn)


class _Attn:
    def __init__(self, mode, s, sk, heads, dk):
        self.mode, self.s, self.sk, self.h, self.dk = mode, s, sk, heads, dk
        self.scale = {"mla": 192 ** -0.5, "mem": 128 ** -0.5}.get(mode, NSA_DK ** -0.5)
        self.tb = min(256, s)
        self.nb = s // self.tb
        self.nsub = 2 if self.nb % 2 == 0 else 1
        self.tq = self.tb * self.nsub
        self.nq = s // self.tq
        self.causal = mode in ("mla", "slc")
        if self.causal:
            self.tk = self.tq
        elif mode == "win":
            self.tk = WIN + self.tb
        else:
            self.tk = sk
        self.tkb = min(512, sk)
        self.ksub = 2 if self.tkb == 512 and mode == "mla" else 1
        self.kb = self.tkb // self.ksub
        self.ncmp = s // CMP_STRIDE - 1

    def mask_bias(self, t, n, h, selx, diag):
        m = self.mode
        if m == "mla":
            return (n <= t) if diag else None, None
        if m == "mem":
            return None, None
        slope = jnp.where(h == 0, 0.25, jnp.where(h == 1, 0.0625, jnp.where(h == 2, 0.015625, 0.00390625)))
        slope = slope.astype(F32) * LOG2E
        if m == "cmp":
            mask = (n * CMP_STRIDE + (CMP_LEN - 1) <= t) & (n < self.ncmp)
            pos = n.astype(F32) * float(CMP_STRIDE) + (CMP_LEN - 1) / 2.0
            return mask, slope * pos
        rel = t - n
        if m == "slc":
            mask = selx > 0.5
            return (mask & (rel >= 0)) if diag else mask, slope * n.astype(F32)
        return (rel >= 0) & (rel < WIN), slope * n.astype(F32)


def _scores(cfg, s_raw, t, n, h, selx, diag, lse=None):
    s = s_raw * (cfg.scale * LOG2E)
    mask, key_term = cfg.mask_bias(t, n, h, selx, diag)
    if key_term is not None:
        s = s + key_term
    if lse is None:
        if mask is not None:
            s = jnp.where(mask, s, NEG)
        return s, mask
    p = jnp.exp2(jnp.minimum(s - lse, 0.0))
    if mask is not None:
        p = jnp.where(mask, p, 0.0)
    return p, mask


def _block_of_key(k0, tk, keys_on_rows):
    shape = (tk, LANE) if keys_on_rows else (LANE, tk)
    n = lax.broadcasted_iota(jnp.int32, shape, 0 if keys_on_rows else 1) + k0
    j = lax.broadcasted_iota(jnp.int32, shape, 1 if keys_on_rows else 0)
    return jnp.where((n >> 6) == j, 1.0, 0.0).astype(BF16)


def _to_row(col):
    t = col.shape[0]
    return jnp.transpose(jnp.broadcast_to(col, (t, LANE)))[0:1, :]


def _load_keys(k_refs, rows):
    parts = [r[rows, :].astype(BF16) for r in k_refs]
    return parts[0] if len(parts) == 1 else jnp.concatenate(parts, axis=1)


def _attn_fwd(cfg, q, ks, v, sel, name):
    s, tq, tk, tb, nsub = cfg.s, cfg.tq, cfg.tk, cfg.tb, cfg.nsub
    has_sel = sel is not None
    nkp = len(ks)

    def kern(*refs):
        q_ref, k_refs, v_ref = refs[0], refs[1:1 + nkp], refs[1 + nkp]
        sel_ref = refs[2 + nkp] if has_sel else None
        o_ref, lc_ref, lr_ref = refs[-3:]
        h, i = pl.program_id(0), pl.program_id(1)
        part = [slice(r * tb, (r + 1) * tb) for r in range(nsub)]
        qs = [q_ref[p, :].astype(BF16) for p in part]
        ts = [i * tq + r * tb + lax.broadcasted_iota(jnp.int32, (tb, 1), 0) for r in range(nsub)]
        sels = [sel_ref[p, :].astype(BF16) for p in part] if has_sel else None

        def load(k0):
            rows = pl.ds(k0, tk)
            return _load_keys(k_refs, rows), v_ref[rows, :].astype(BF16)

        def step(r, k0, kk, vv, emat, carry, diag):
            m, l, acc = carry
            n = k0 + lax.broadcasted_iota(jnp.int32, (1, tk), 1)
            selx = _nn(sels[r], emat) if has_sel else None
            sc, mask = _scores(cfg, _nt(qs[r], kk), ts[r], n, h, selx, diag)
            m_new = jnp.maximum(m, jnp.max(sc, axis=1, keepdims=True))
            alpha = jnp.exp2(m - m_new)
            p = jnp.exp2(sc - m_new)
            if mask is not None:
                p = jnp.where(mask, p, 0.0)
            l = alpha * l + jnp.sum(p, axis=1, keepdims=True)
            acc = alpha * acc + _nn(p.astype(BF16), vv)
            return m_new, l, acc

        def chunk(k0, carry, diag):
            kk, vv = load(k0)
            emat = _block_of_key(k0, tk, False) if has_sel else None
            return tuple(step(r, k0, kk, vv, emat, carry[r], diag) for r in range(nsub))

        init = (jnp.full((tb, 1), NEG, F32), jnp.zeros((tb, 1), F32), jnp.zeros((tb, HEAD_V), F32))
        carry = (init,) * nsub
        if cfg.causal:
            full = (i * tq) // tk
            carry = lax.fori_loop(0, full, lambda c, cr: chunk(pl.multiple_of(c * tk, tk), cr, False), carry)
            carry = chunk(pl.multiple_of(full * tk, tk), carry, True)
        elif cfg.mode == "win":
            starts = [pl.multiple_of(jnp.maximum(i * tq + r * tb - WIN, 0), tb) for r in range(nsub)]
            carry = tuple(step(r, k0, *load(k0), None, carry[r], True) for r, k0 in enumerate(starts))
        else:
            carry = chunk(0, carry, True)
        for r, (m, l, acc) in enumerate(carry):
            o_ref[part[r], :] = acc / (l + 1e-20)
            lse = m + jnp.log(l + 1e-20) * LOG2E
            lc_ref[0, part[r], :] = lse
            lr_ref[0, r] = _to_row(lse)

    ins = [pl.BlockSpec((tq, q.width), lambda h, i: (i, q.col(h)))]
    ins += [pl.BlockSpec((cfg.sk, p.width), lambda h, i, p=p: (0, p.col(h))) for p in ks]
    ins += [pl.BlockSpec((cfg.sk, HEAD_V), lambda h, i: (0, v.col(h)))]
    args = [q.arr] + [p.arr for p in ks] + [v.arr]
    if has_sel:
        ins.append(pl.BlockSpec((tq, LANE), lambda h, i: (i, 0)))
        args.append(sel)
    return _pcall(
        kern, name=name, grid=(cfg.h, cfg.nq), in_specs=ins,
        out_specs=[pl.BlockSpec((tq, HEAD_V), lambda h, i: (i, h)),
                   pl.BlockSpec((1, tq, 1), lambda h, i: (h, i, 0)),
                   pl.BlockSpec((1, nsub, 1, tb), lambda h, i: (h, i, 0, 0))],
        out_shape=[jax.ShapeDtypeStruct((s, cfg.h * HEAD_V), F32),
                   jax.ShapeDtypeStruct((cfg.h, s, 1), F32),
                   jax.ShapeDtypeStruct((cfg.h, cfg.nb, 1, tb), F32)],
        compiler_params=pltpu.CompilerParams(dimension_semantics=("parallel", "parallel")),
    )(*args)


def _attn_dq(cfg, q, ks, v, sel, o, lse, do, dq_in, name):
    s, tq, tk, dk, tb, nsub = cfg.s, cfg.tq, cfg.tk, cfg.dk, cfg.tb, cfg.nsub
    has_sel = sel is not None
    has_in = dq_in is not None
    nkp = len(ks)

    def kern(*refs):
        refs = list(refs)
        q_ref, k_refs, v_ref = refs[0], refs[1:1 + nkp], refs[1 + nkp]
        p0 = 2 + nkp
        sel_ref = refs[p0] if has_sel else None
        p0 += has_sel
        o_ref, l_ref, do_ref = refs[p0:p0 + 3]
        p0 += 3
        in_ref = refs[p0] if has_in else None
        dq_ref, dr_ref = refs[-2:]
        h, i = pl.program_id(0), pl.program_id(1)
        part = [slice(r * tb, (r + 1) * tb) for r in range(nsub)]
        qs = [q_ref[p, :].astype(BF16) for p in part]
        ts = [i * tq + r * tb + lax.broadcasted_iota(jnp.int32, (tb, 1), 0) for r in range(nsub)]
        sels = [sel_ref[p, :].astype(BF16) for p in part] if has_sel else None
        dvecs, dobs, lses = [], [], []
        for r, p in enumerate(part):
            dov = do_ref[p, :]
            dvec = jnp.sum(dov * o_ref[p, :], axis=1, keepdims=True)
            dr_ref[0, r] = _to_row(dvec)
            dvecs.append(dvec)
            dobs.append(dov.astype(BF16))
            lses.append(l_ref[0, p, :])

        def load(k0):
            rows = pl.ds(k0, tk)
            return _load_keys(k_refs, rows), v_ref[rows, :].astype(BF16)

        def step(r, k0, kk, vv, emat, acc, diag):
            n = k0 + lax.broadcasted_iota(jnp.int32, (1, tk), 1)
            selx = _nn(sels[r], emat) if has_sel else None
            p, _ = _scores(cfg, _nt(qs[r], kk), ts[r], n, h, selx, diag, lses[r])
            ds = p * (_nt(dobs[r], vv) - dvecs[r])
            return acc + _nn(ds.astype(BF16), kk)

        def chunk(k0, accs, diag):
            kk, vv = load(k0)
            emat = _block_of_key(k0, tk, False) if has_sel else None
            return tuple(step(r, k0, kk, vv, emat, accs[r], diag) for r in range(nsub))

        accs = (jnp.zeros((tb, dk), F32),) * nsub
        if cfg.causal:
            full = (i * tq) // tk
            accs = lax.fori_loop(0, full, lambda c, a: chunk(pl.multiple_of(c * tk, tk), a, False), accs)
            accs = chunk(pl.multiple_of(full * tk, tk), accs, True)
        elif cfg.mode == "win":
            starts = [pl.multiple_of(jnp.maximum(i * tq + r * tb - WIN, 0), tb) for r in range(nsub)]
            accs = tuple(step(r, k0, *load(k0), None, accs[r], True) for r, k0 in enumerate(starts))
        else:
            accs = chunk(0, accs, True)
        for r, p in enumerate(part):
            dq_ref[p, :] = accs[r] * cfg.scale + in_ref[p, :] if has_in else accs[r] * cfg.scale

    qs = pl.BlockSpec((tq, dk), lambda h, i: (i, h))
    ins = [pl.BlockSpec((tq, q.width), lambda h, i: (i, q.col(h)))]
    ins += [pl.BlockSpec((cfg.sk, p.width), lambda h, i, p=p: (0, p.col(h))) for p in ks]
    ins += [pl.BlockSpec((cfg.sk, HEAD_V), lambda h, i: (0, v.col(h)))]
    args = [q.arr] + [p.arr for p in ks] + [v.arr]
    if has_sel:
        ins.append(pl.BlockSpec((tq, LANE), lambda h, i: (i, 0)))
        args.append(sel)
    ins += [pl.BlockSpec((tq, HEAD_V), lambda h, i: (i, o.col(h))),
            pl.BlockSpec((1, tq, 1), lambda h, i: (h, i, 0)),
            pl.BlockSpec((tq, HEAD_V), lambda h, i: (i, do.col(h)))]
    args += [o.arr, lse, do.arr]
    if has_in:
        ins.append(qs)
        args.append(dq_in)
    return _pcall(
        kern, name=name, grid=(cfg.h, cfg.nq), in_specs=ins,
        out_specs=[qs, pl.BlockSpec((1, nsub, 1, tb), lambda h, i: (h, i, 0, 0))],
        out_shape=[jax.ShapeDtypeStruct((s, cfg.h * dk), F32),
                   jax.ShapeDtypeStruct((cfg.h, cfg.nb, 1, tb), F32)],
        compiler_params=pltpu.CompilerParams(dimension_semantics=("parallel", "parallel")),
    )(*args)


def _attn_dkv(cfg, q, ks, v, selt, lse_r, d_r, do, name):
    s, tq, tkb, dk, kb, ksub = cfg.s, cfg.tb, cfg.tkb, cfg.dk, cfg.kb, cfg.ksub
    nq = cfg.nb
    has_sel = selt is not None
    nkp = len(ks)
    outs = list(ks) + [v]

    def kern(*refs):
        k_refs, v_ref = refs[:nkp], refs[nkp]
        q_ref, do_ref, lr_ref, dr_ref = refs[nkp + 1:nkp + 5]
        st_ref = refs[nkp + 5] if has_sel else None
        out_refs = refs[-(nkp + 1):]
        j, h = pl.program_id(0), pl.program_id(1)
        k0 = j * tkb
        part = [slice(u * kb, (u + 1) * kb) for u in range(ksub)]
        kks = [_load_keys(k_refs, p) for p in part]
        vvs = [v_ref[p, :].astype(BF16) for p in part]
        ns = [k0 + u * kb + lax.broadcasted_iota(jnp.int32, (kb, 1), 0) for u in range(ksub)]
        emats = [_block_of_key(k0 + u * kb, kb, True) for u in range(ksub)] if has_sel else None

        def qblock(i, carry, diag):
            r0 = pl.multiple_of(i * tq, tq)
            qi = q_ref[pl.ds(r0, tq), :].astype(BF16)
            doi = do_ref[pl.ds(r0, tq), :].astype(BF16)
            t = i * tq + lax.broadcasted_iota(jnp.int32, (1, tq), 1)
            selt_i = st_ref[i].astype(BF16) if has_sel else None
            new = []
            for u in range(ksub):
                dk_acc, dv_acc = carry[u]
                selx = _nn(emats[u], selt_i) if has_sel else None
                pt, _ = _scores(cfg, _nt(kks[u], qi), t, ns[u], h, selx, diag, lr_ref[0, i])
                dv_acc = dv_acc + _nn(pt.astype(BF16), doi)
                dst = pt * (_nt(vvs[u], doi) - dr_ref[0, i])
                new.append((dk_acc + _nn(dst.astype(BF16), qi), dv_acc))
            return tuple(new)

        carry = ((jnp.zeros((kb, dk), F32), jnp.zeros((kb, HEAD_V), F32)),) * ksub
        masked = lambda i, cr: qblock(i, cr, True)
        if cfg.causal:
            first, past = k0 // tq, (k0 + tkb) // tq
            carry = lax.fori_loop(first, past, masked, carry)
            carry = lax.fori_loop(past, nq, lambda i, cr: qblock(i, cr, False), carry)
        elif cfg.mode == "win":
            carry = lax.fori_loop(k0 // tq, jnp.minimum((k0 + tkb + WIN - 2) // tq + 1, nq), masked, carry)
        else:
            carry = lax.fori_loop(0, nq, masked, carry)
        for u, (dk_acc, dv_acc) in enumerate(carry):
            vals, off = [], 0
            for p in ks:
                vals.append(dk_acc[:, off:off + p.width] * cfg.scale)
                off += p.width
            vals.append(dv_acc)
            for src, ref, val in zip(outs, out_refs, vals):
                if src.per_head:
                    ref[part[u], :] = val
                else:
                    @pl.when(h == 0)
                    def _(ref=ref, val=val, u=u):
                        ref[part[u], :] = val

                    @pl.when(h > 0)
                    def _(ref=ref, val=val, u=u):
                        ref[part[u], :] += val

    rowv = pl.BlockSpec((1, nq, 1, tq), lambda j, h: (h, 0, 0, 0))
    ins = [pl.BlockSpec((tkb, p.width), lambda j, h, p=p: (j, p.col(h))) for p in ks]
    ins += [pl.BlockSpec((tkb, HEAD_V), lambda j, h: (j, v.col(h))),
            pl.BlockSpec((s, q.width), lambda j, h: (0, q.col(h))),
            pl.BlockSpec((s, HEAD_V), lambda j, h: (0, do.col(h))), rowv, rowv]
    args = [p.arr for p in ks] + [v.arr, q.arr, do.arr, lse_r, d_r]
    if has_sel:
        ins.append(pl.BlockSpec((nq, LANE, tq), lambda j, h: (0, 0, 0)))
        args.append(selt)
    out_specs = [pl.BlockSpec((tkb, p.width), lambda j, h, p=p: (j, h if p.per_head else 0)) for p in outs]
    out_shape = [jax.ShapeDtypeStruct((cfg.sk, (cfg.h if p.per_head else 1) * p.width), F32) for p in outs]
    return _pcall(
        kern, name=name, grid=(cfg.sk // tkb, cfg.h), in_specs=ins, out_specs=out_specs, out_shape=out_shape,
        compiler_params=pltpu.CompilerParams(dimension_semantics=("parallel", "arbitrary")),
    )(*args)


def _attn_bwd(cfg, q, ks, v, sel, selt, o, lse, lse_r, do, dq_in, name):
    dq, d_r = _attn_dq(cfg, q, ks, v, sel, o, lse, do, dq_in, name + "_dq")
    res = _attn_dkv(cfg, q, ks, v, selt, lse_r, d_r, do, name + "_dkv")
    return dq, res[:-1], res[-1]


def _select(cfg, q, k_cmp, overlap):
    s, tq, sk = cfg.s, cfg.tb, cfg.sk
    n_s = s // SLC_LEN
    top_n = min(SLC_TOPN, n_s)

    def kern(q_ref, k_ref, ov_ref, sel_ref, selt_ref):
        i = pl.program_id(0)
        t = i * tq + lax.broadcasted_iota(jnp.int32, (tq, 1), 0)
        n = lax.broadcasted_iota(jnp.int32, (1, sk), 1)
        kk = k_ref[...]
        imp = jnp.zeros((tq, LANE), F32)
        for h in range(NSA_HEADS):
            sc, mask = _scores(cfg, _nt(q_ref[:, h * 256:(h + 1) * 256], kk), t, n, h, None, True)
            m = jnp.max(sc, axis=1, keepdims=True)
            e = jnp.where(mask, jnp.exp2(sc - m), 0.0)
            p = e / (jnp.sum(e, axis=1, keepdims=True) + 1e-20)
            imp = imp + _nn(p.astype(BF16), ov_ref[...])
        j = lax.broadcasted_iota(jnp.int32, (tq, LANE), 1)
        cur = t >> 6
        forced = (j == 0) | (j == cur) | (j == cur - 1)
        imp = jnp.where(forced, 1e9, imp)
        imp = jnp.where(j > cur, -1e9, imp)
        imp = jnp.where(j >= n_s, -3e38, imp)

        def pick(_, carry):
            work, chosen = carry
            mx = jnp.max(work, axis=1, keepdims=True)
            first = jnp.min(jnp.where(work == mx, j, LANE), axis=1, keepdims=True)
            hit = j == first
            return jnp.where(hit, -3e38, work), jnp.where(hit, 1.0, chosen)

        _, chosen = lax.fori_loop(0, top_n, pick, (imp, jnp.zeros((tq, LANE), F32)))
        chosen = jnp.where(j <= cur, chosen, 0.0)
        sel_ref[...] = chosen
        selt_ref[0] = jnp.transpose(chosen)

    return _pcall(
        kern, name="nsa_select", grid=(cfg.nb,),
        in_specs=[pl.BlockSpec((tq, NSA_HEADS * 256), lambda i: (i, q.col0)),
                  pl.BlockSpec((sk, 256), lambda i: (0, 0)), pl.BlockSpec((sk, LANE), lambda i: (0, 0))],
        out_specs=[pl.BlockSpec((tq, LANE), lambda i: (i, 0)), pl.BlockSpec((1, LANE, tq), lambda i: (i, 0, 0))],
        out_shape=[jax.ShapeDtypeStruct((s, LANE), F32), jax.ShapeDtypeStruct((cfg.nb, LANE, tq), F32)],
    )(q.arr, k_cmp, overlap)


def _silu_grad(pre):
    sg = _sigmoid(pre)
    return sg * (1.0 + pre * (1.0 - sg))


def _compress_fwd(a_lo, a_hi, pe_lo, pe_hi, w1_lo, w1_hi, w2, name):
    n, dp = a_lo.shape[0], w2.shape[1]

    def kern(alo, ahi, plo, phi, w1l, w1h, w2r, out_ref, pre_ref):
        xl = (alo[...] + plo[...]).astype(BF16)
        xh = (ahi[...] + phi[...]).astype(BF16)
        pre = _nn(xl, w1l[...]) + _nn(xh, w1h[...])
        act = pre * _sigmoid(pre)
        out_ref[...] = _nn(act.astype(BF16), w2r[...]).astype(BF16)
        pre_ref[...] = pre

    return _pcall(kern, name=name,
                  out_shape=[jax.ShapeDtypeStruct((n, dp), BF16), jax.ShapeDtypeStruct((n, dp), F32)],
                  )(a_lo, a_hi, pe_lo, pe_hi, w1_lo, w1_hi, w2)


def _compress_bwd(a_lo, a_hi, pe_lo, pe_hi, w1_lo, w1_hi, w2, pre, pre_sh, dout, dout_sh, name):
    n, ln = a_lo.shape
    dp = w2.shape[1]

    def kern(alo, ahi, plo, phi, w1l, w1h, w2r, pre_ref, presh_ref, do_ref, dosh_ref,
             da_ref, dpl_ref, dph_ref, dw1l_ref, dw1h_ref, dw2_ref):
        prev = pre_ref[...]
        act = prev * _sigmoid(prev)
        dob = do_ref[...].astype(BF16)
        w2v = w2r[...]
        dpre = (_nt(dob, w2v) * _silu_grad(prev)).astype(BF16)
        dpre_sh = (_nt(dosh_ref[...].astype(BF16), w2v) * _silu_grad(presh_ref[...])).astype(BF16)
        dw2_ref[...] = _nn(act.T.astype(BF16), dob)
        xl = alo[...] + plo[...]
        xh = ahi[...] + phi[...]
        dw1l_ref[...] = _nn(xl.T.astype(BF16), dpre)
        dw1h_ref[...] = _nn(xh.T.astype(BF16), dpre)
        dal = _nt(dpre, w1l[...])
        dah_sh = _nt(dpre_sh, w1h[...])
        da_ref[...] = dal + dah_sh
        dpl_ref[...] = jnp.sum(dal, axis=0, keepdims=True)
        dph_ref[...] = jnp.sum(dah_sh, axis=0, keepdims=True)

    return _pcall(
        kern, name=name,
        out_shape=[jax.ShapeDtypeStruct((n, ln), F32), jax.ShapeDtypeStruct((1, ln), F32),
                   jax.ShapeDtypeStruct((1, ln), F32), jax.ShapeDtypeStruct((ln, dp), F32),
                   jax.ShapeDtypeStruct((ln, dp), F32), jax.ShapeDtypeStruct((dp, dp), F32)],
    )(a_lo, a_hi, pe_lo, pe_hi, w1_lo, w1_hi, w2, pre, pre_sh, dout, dout_sh)


def _nsa_combine(o_cmp, o_slc, o_win, gl):
    s, w = o_cmp.shape
    tr = _tile(s, 512)

    def kern(a_ref, b_ref, c_ref, g_ref, o_ref):
        g = _sigmoid(g_ref[...])
        for h in range(NSA_HEADS):
            cs = slice(h * HEAD_V, (h + 1) * HEAD_V)
            o_ref[:, cs] = (g[:, 3 * h:3 * h + 1] * a_ref[:, cs] + g[:, 3 * h + 1:3 * h + 2] * b_ref[:, cs]
                            + g[:, 3 * h + 2:3 * h + 3] * c_ref[:, cs])

    row = pl.BlockSpec((tr, w), lambda i: (i, 0))
    return _pcall(kern, name="nsa_combine", grid=(s // tr,),
                  in_specs=[row, row, row, pl.BlockSpec((tr, LANE), lambda i: (i, gl.col0))], out_specs=row,
                  out_shape=jax.ShapeDtypeStruct((s, w), F32))(o_cmp, o_slc, o_win, gl.arr)


def _nsa_combine_bwd(do_cat, o_cmp, o_slc, o_win, gl):
    s, w = o_cmp.shape
    tr = _tile(s, 512)

    def kern(d_ref, a_ref, b_ref, c_ref, g_ref, da_ref, db_ref, dc_ref, dg_ref):
        g = _sigmoid(g_ref[...])
        lane = lax.broadcasted_iota(jnp.int32, (tr, LANE), 1)
        dgl = jnp.zeros((tr, LANE), F32)
        for h in range(NSA_HEADS):
            cs = slice(h * HEAD_V, (h + 1) * HEAD_V)
            dv = d_ref[:, cs]
            for b, (src, dst) in enumerate(((a_ref, da_ref), (b_ref, db_ref), (c_ref, dc_ref))):
                gate = g[:, 3 * h + b:3 * h + b + 1]
                dst[:, cs] = gate * dv
                dgate = jnp.sum(dv * src[:, cs], axis=1, keepdims=True)
                dgl = jnp.where(lane == 3 * h + b, dgate * gate * (1.0 - gate), dgl)
        dg_ref[...] = dgl

    row = pl.BlockSpec((tr, w), lambda i: (i, 0))
    tab = pl.BlockSpec((tr, LANE), lambda i: (i, 0))
    return _pcall(kern, name="nsa_combine_bwd", grid=(s // tr,),
                  in_specs=[pl.BlockSpec((tr, w), lambda i: (i, 2)), row, row, row,
                            pl.BlockSpec((tr, LANE), lambda i: (i, gl.col0))],
                  out_specs=[row, row, row, tab],
                  out_shape=[jax.ShapeDtypeStruct((s, w), F32)] * 3 + [jax.ShapeDtypeStruct((s, LANE), F32)],
                  )(do_cat, o_cmp, o_slc, o_win, gl.arr)


def _gate_fwd(o_mla, o_nsa, o_mem, hp):
    s = o_mla.shape[0]
    tr = _tile(s, 256)

    def kern(a_ref, b_ref, c_ref, z_ref, u_ref):
        z = z_ref[...]
        sz = z * _sigmoid(z)
        u_ref[:, 0:1024] = (a_ref[...] * sz[:, 0:1024]).astype(BF16)
        u_ref[:, 1024:1536] = (b_ref[...] * sz[:, 1024:1536]).astype(BF16)
        u_ref[:, 1536:2048] = (c_ref[...] * sz[:, 1536:2048]).astype(BF16)

    return _pcall(
        kern, name="gate_fwd", grid=(s // tr,),
        in_specs=[pl.BlockSpec((tr, 1024), lambda i: (i, 0)), pl.BlockSpec((tr, 512), lambda i: (i, 0)),
                  pl.BlockSpec((tr, 512), lambda i: (i, 0)), pl.BlockSpec((tr, 2048), lambda i: (i, 2))],
        out_specs=pl.BlockSpec((tr, 2048), lambda i: (i, 0)),
        out_shape=jax.ShapeDtypeStruct((s, 2048), BF16))(o_mla, o_nsa, o_mem, hp)


def _gate_bwd(du, o_mla, o_nsa, o_mem, hp):
    s = du.shape[0]
    tr = _tile(s, 256)

    def kern(d_ref, a_ref, b_ref, c_ref, z_ref, do_ref, dz_ref):
        z = z_ref[...]
        sg = _sigmoid(z)
        sz = z * sg
        dsz = sg * (1.0 + z * (1.0 - sg))
        d = d_ref[...]
        do_ref[...] = d * sz
        dz_ref[:, 0:1024] = d[:, 0:1024] * a_ref[...] * dsz[:, 0:1024]
        dz_ref[:, 1024:1536] = d[:, 1024:1536] * b_ref[...] * dsz[:, 1024:1536]
        dz_ref[:, 1536:2048] = d[:, 1536:2048] * c_ref[...] * dsz[:, 1536:2048]

    wide = pl.BlockSpec((tr, 2048), lambda i: (i, 0))
    return _pcall(
        kern, name="gate_bwd", grid=(s // tr,),
        in_specs=[wide, pl.BlockSpec((tr, 1024), lambda i: (i, 0)), pl.BlockSpec((tr, 512), lambda i: (i, 0)),
                  pl.BlockSpec((tr, 512), lambda i: (i, 0)), pl.BlockSpec((tr, 2048), lambda i: (i, 2))],
        out_specs=[wide, wide],
        out_shape=[jax.ShapeDtypeStruct((s, 2048), F32)] * 2)(du, o_mla, o_nsa, o_mem, hp)


def _tile2d(rows, cols, arrays):
    if rows % 16 == 0:
        return _row_tile(rows, cols * arrays), cols
    want = max(LANE, BLOCK_BYTES // (rows * 4 * arrays) // LANE * LANE)
    tc = LANE
    for t in range(LANE, cols + 1, LANE):
        if cols % t == 0 and t <= want:
            tc = t
    return rows, tc


def _sum_slots(buf, name):
    n, rows, cols = buf.shape
    tr, tc = _tile2d(rows, cols, n)

    def kern(b_ref, o_ref):
        acc = b_ref[0].astype(F32)
        for i in range(1, n):
            acc = acc + b_ref[i].astype(F32)
        o_ref[...] = acc

    return _pcall(kern, name=name, grid=(rows // tr, cols // tc),
                  in_specs=[pl.BlockSpec((n, tr, tc), lambda i, j: (0, i, j))],
                  out_specs=pl.BlockSpec((tr, tc), lambda i, j: (i, j)),
                  out_shape=jax.ShapeDtypeStruct((rows, cols), F32))(buf)


def _pair_sum(g4, theirs, core, axis, name):
    n, rows, cols = theirs.shape
    tr, tc = _tile2d(rows, cols, 1)
    nbr, nbc = rows // tr, cols // tc

    def kern(c_ref, a_ref, b_ref, o_ref):
        o_ref[...] = (a_ref[...] + b_ref[...]).astype(BF16)

    blk = (1, tr, tc)
    mine = ((lambda s, i, j, c: (s, c[0] * nbr + i, j)) if axis == 0
            else (lambda s, i, j, c: (s, i, c[0] * nbc + j)))
    grid_spec = pltpu.PrefetchScalarGridSpec(
        num_scalar_prefetch=1, grid=(n, nbr, nbc),
        in_specs=[pl.BlockSpec(blk, mine), pl.BlockSpec(blk, lambda s, i, j, c: (s, i, j))],
        out_specs=pl.BlockSpec(blk, lambda s, i, j, c: (s, i, j)))
    return _pcall(kern, name=name, grid_spec=grid_spec,
                  out_shape=jax.ShapeDtypeStruct((n, rows, cols), BF16))(core, g4, theirs)


def _adamw(w, g, m, v, name):
    rows, cols = w.shape
    tr, tc = _tile2d(rows, cols, 4)
    bc1 = 1.0 - ADAM_B1 ** ADAM_STEP
    bc2 = 1.0 - ADAM_B2 ** ADAM_STEP

    def kern(w_ref, g_ref, m_ref, v_ref, d_ref, mo_ref, vo_ref):
        gv = g_ref[...]
        mn = ADAM_B1 * m_ref[...] + (1.0 - ADAM_B1) * gv
        vn = ADAM_B2 * v_ref[...] + (1.0 - ADAM_B2) * (gv * gv)
        d_ref[...] = -ADAM_LR * ((mn / bc1) / (jnp.sqrt(vn / bc2) + ADAM_EPS) + ADAM_WD * w_ref[...])
        mo_ref[...] = mn
        vo_ref[...] = vn

    blk = pl.BlockSpec((tr, tc), lambda i, j: (i, j))
    return _pcall(kern, name=name, grid=(rows // tr, cols // tc), in_specs=[blk] * 4, out_specs=[blk] * 3,
                  out_shape=[jax.ShapeDtypeStruct((rows, cols), F32)] * 3)(w, g, m, v)


ANY = pl.BlockSpec(memory_space=pl.ANY)


def _place():
    x, y, c = lax.axis_index("x"), lax.axis_index("y"), lax.axis_index("c")
    chips = [(1 - x, y), (x, 1 - y), (1 - x, 1 - y)]
    return x, y, c, chips


def _remote(src, dst, send_sem, recv_sem, to):
    return pltpu.make_async_remote_copy(src_ref=src, dst_ref=dst, send_sem=send_sem, recv_sem=recv_sem,
                                        device_id=to, device_id_type=MESH)


def _half(ref, lead, core, axis):
    size = ref.shape[len(lead) + axis] // 2
    cut = pl.ds(core * size, size)
    return ref.at[tuple(lead) + ((cut, slice(None)) if axis == 0 else (slice(None), cut))]


def _gather_shards(ws, axes):
    nw = len(ws)

    def body(*refs):
        w_refs, out_refs = refs[:nw], refs[nw:2 * nw]
        send_sems, recv_sems = refs[2 * nw:]
        x, y, c, chips = _place()
        me = 2 * x + y
        sibling = (x, y, 1 - c)

        def part(i, slot, core):
            return _half(out_refs[i], (slot,), core, axes[i])

        def copy(sem, src, dst, to):
            return _remote(src, dst, send_sems.at[sem], recv_sems.at[sem], to)

        first = [copy(j * nw + i, _half(w_refs[i], (), c, axes[i]), part(i, me, c), (*chip, c))
                 for j, chip in enumerate(chips) for i in range(nw)]
        for cp in first:
            cp.start()
        passed = []
        for j, (cx, cy) in enumerate(chips):
            slot = 2 * cx + cy
            for i in range(nw):
                copy(j * nw + i, part(i, slot, c), part(i, slot, c), (x, y, c)).wait_recv()
                fwd = copy((3 + j) * nw + i, part(i, slot, c), part(i, slot, c), sibling)
                fwd.start()
                passed.append(fwd)
        for j, (cx, cy) in enumerate(chips):
            slot = 2 * cx + cy
            for i in range(nw):
                copy((3 + j) * nw + i, part(i, slot, 1 - c), part(i, slot, 1 - c), (x, y, c)).wait_recv()
        for cp in first + passed:
            cp.wait_send()

    return _pcall(
        body, name="gather_shards", in_specs=[ANY] * nw, out_specs=[ANY] * nw,
        out_shape=[jax.ShapeDtypeStruct((4,) + w.shape, w.dtype) for w in ws],
        scratch_shapes=[pltpu.SemaphoreType.DMA((6 * nw,)), pltpu.SemaphoreType.DMA((6 * nw,))],
    )(*ws)


def _half_shape(shape, axis):
    return tuple(d // 2 if k == len(shape) - 2 + axis else d for k, d in enumerate(shape))


def _pair_exchange(gs, axes):
    nw = len(gs)

    def body(*refs):
        g_refs, out_refs = refs[:nw], refs[nw:2 * nw]
        send_sems, recv_sems = refs[2 * nw:]
        x, y, c, _ = _place()
        cps = []
        for i in range(nw):
            cp = _remote(_half(g_refs[i], (slice(None),), 1 - c, axes[i]), out_refs[i],
                         send_sems.at[i], recv_sems.at[i], (x, y, 1 - c))
            cp.start()
            cps.append(cp)
        for cp in cps:
            cp.wait()

    return _pcall(body, name="pair_exchange", in_specs=[ANY] * nw, out_specs=[ANY] * nw,
                  out_shape=[jax.ShapeDtypeStruct(_half_shape(g.shape, a), g.dtype) for g, a in zip(gs, axes)],
                  scratch_shapes=[pltpu.SemaphoreType.DMA((nw,)), pltpu.SemaphoreType.DMA((nw,))])(*gs)


def _chip_exchange(ps):
    nw = len(ps)

    def body(*refs):
        p_refs, out_refs = refs[:nw], refs[nw:2 * nw]
        send_sems, recv_sems, local_sems = refs[2 * nw:]
        x, y, c, chips = _place()
        me = 2 * x + y
        mine = [pltpu.make_async_copy(p_refs[i].at[me], out_refs[i].at[me], local_sems.at[i]) for i in range(nw)]
        for cp in mine:
            cp.start()
        sends = []
        for j, (cx, cy) in enumerate(chips):
            for i in range(nw):
                cp = _remote(p_refs[i].at[2 * cx + cy], out_refs[i].at[me], send_sems.at[j * nw + i],
                             recv_sems.at[j * nw + i], (cx, cy, c))
                cp.start()
                sends.append(cp)
        for j, (cx, cy) in enumerate(chips):
            slot = 2 * cx + cy
            for i in range(nw):
                _remote(out_refs[i].at[slot], out_refs[i].at[slot], send_sems.at[j * nw + i],
                        recv_sems.at[j * nw + i], (x, y, c)).wait_recv()
        for cp in sends:
            cp.wait_send()
        for cp in mine:
            cp.wait()

    return _pcall(body, name="chip_exchange", in_specs=[ANY] * nw, out_specs=[ANY] * nw,
                  out_shape=[jax.ShapeDtypeStruct(p.shape, p.dtype) for p in ps],
                  scratch_shapes=[pltpu.SemaphoreType.DMA((3 * nw,)), pltpu.SemaphoreType.DMA((3 * nw,)),
                                  pltpu.SemaphoreType.DMA((nw,))])(*ps)


def _half_exchange(ts, axes):
    nw = len(ts)

    def body(*refs):
        t_refs, out_refs = refs[:nw], refs[nw:2 * nw]
        send_sems, recv_sems = refs[2 * nw:]
        x, y, c, _ = _place()
        sends = []
        for i in range(nw):
            cp = _remote(t_refs[i], _half(out_refs[i], (), c, axes[i]), send_sems.at[i], recv_sems.at[i],
                         (x, y, 1 - c))
            cp.start()
            sends.append(cp)
        for i in range(nw):
            _remote(t_refs[i], _half(out_refs[i], (), 1 - c, axes[i]), send_sems.at[i], recv_sems.at[i],
                    (x, y, c)).wait_recv()
        for cp in sends:
            cp.wait_send()

    def whole(t, a):
        return tuple(2 * d if k == a else d for k, d in enumerate(t.shape))

    return _pcall(body, name="half_exchange", in_specs=[ANY] * nw, out_specs=[ANY] * nw,
                  out_shape=[jax.ShapeDtypeStruct(whole(t, a), t.dtype) for t, a in zip(ts, axes)],
                  scratch_shapes=[pltpu.SemaphoreType.DMA((nw,)), pltpu.SemaphoreType.DMA((nw,))])(*ts)


def _gather_all(v):
    rows, cols = v.shape

    def body(v_ref, out_ref, send_sems, recv_sems, local_sem):
        x, y, c, _ = _place()
        me = 4 * x + 2 * y + c
        mine = pltpu.make_async_copy(v_ref, out_ref.at[me], local_sem)
        mine.start()
        sends = []
        for d in range(1, 8):
            peer = (x ^ (d >> 2), y ^ ((d >> 1) & 1), c ^ (d & 1))
            cp = _remote(v_ref, out_ref.at[me], send_sems.at[d - 1], recv_sems.at[d - 1], peer)
            cp.start()
            sends.append(cp)
        for d in range(1, 8):
            slot = 4 * (x ^ (d >> 2)) + 2 * (y ^ ((d >> 1) & 1)) + (c ^ (d & 1))
            _remote(v_ref, out_ref.at[slot], send_sems.at[d - 1], recv_sems.at[d - 1], (x, y, c)).wait_recv()
        for cp in sends:
            cp.wait_send()
        mine.wait()

    return _pcall(body, name="gather_all", in_specs=[ANY], out_specs=ANY,
                  out_shape=jax.ShapeDtypeStruct((8, rows, cols), v.dtype),
                  scratch_shapes=[pltpu.SemaphoreType.DMA((7,)), pltpu.SemaphoreType.DMA((7,)),
                                  pltpu.SemaphoreType.DMA])(v)


def _pad_cols(a, width):
    return a if a.shape[1] == width else jnp.pad(a, ((0, 0), (0, width - a.shape[1])))


def _pad_rows(a, height):
    return a if a.shape[0] == height else jnp.pad(a, ((0, height - a.shape[0]), (0, 0)))


def _w_in_padded(wt):
    def seg(name, height=None):
        o, n = ORIG[name]
        return _pad_rows(wt[o:o + n], height or n)

    qn = wt[ORIG["q_nsa"][0]:ORIG["q_nsa"][0] + 768].reshape(NSA_HEADS, NSA_DK, -1)
    qn = jnp.pad(qn, ((0, 0), (0, 256 - NSA_DK), (0, 0))).reshape(NSA_HEADS * 256, -1)
    kr = seg("k_rope")
    zeros = jnp.zeros((PAD["z"] - (PAD["q_mem"] + 512), wt.shape[1]), wt.dtype)
    return jnp.concatenate(
        [seg("c_q"), seg("c_kv"), qn, seg("k_c", 256), seg("k_s", 256), seg("k_w", 256), kr, kr,
         seg("v_c"), seg("v_s"), seg("v_w"), seg("g_nsa", LANE), seg("q_mem"), zeros,
         seg("z_mla"), seg("z_nsa"), seg("z_mem")], axis=0)


def _w_in_unpadded(gt):
    def seg(name, n):
        return gt[PAD[name]:PAD[name] + n]

    qn = gt[PAD["q_nsa"]:PAD["q_nsa"] + 1024].reshape(NSA_HEADS, 256, -1)[:, :NSA_DK].reshape(768, -1)
    z = PAD["z"]
    return jnp.concatenate(
        [seg("c_q", 512), seg("c_kv", 512), seg("k_rope", 64), gt[z:z + 1024], qn, seg("k_c", 192),
         seg("v_c", 128), seg("k_s", 192), seg("v_s", 128), seg("k_w", 192), seg("v_w", 128),
         seg("g_nsa", 12), gt[z + 1024:z + 1536], seg("q_mem", 512), gt[z + 1536:z + 2048]], axis=0)


def _rope_tables(s):
    pos = jnp.arange(s, dtype=F32)
    inv_freq = ROPE_THETA ** (-jnp.arange(0, 64, 2, dtype=F32) / 64)
    ang = pos[:, None] * inv_freq[None, :]
    cos, sin = jnp.cos(ang), jnp.sin(ang)
    z = jnp.zeros((s, 64), F32)
    return jnp.concatenate([cos, cos, z], axis=1), jnp.concatenate([-sin, sin, z], axis=1)


def _overlap_table(s):
    n_c, n_s = s // CMP_STRIDE, s // SLC_LEN
    c0 = np.arange(n_c)[:, None] * CMP_STRIDE
    s0 = np.arange(LANE)[None, :] * SLC_LEN
    ov = (c0 < s0 + SLC_LEN) & (c0 + CMP_LEN > s0) & (np.arange(n_c)[:, None] < n_c - 1) & (np.arange(LANE)[None, :] < n_s)
    return jnp.asarray(ov.astype(np.float32), dtype=BF16)


def _shift_down(a):
    return jnp.concatenate([jnp.zeros((8, a.shape[1]), a.dtype), a], axis=0)[7:7 + a.shape[0]]


def _shift_up(a):
    return jnp.concatenate([a, jnp.zeros((8, a.shape[1]), a.dtype)], axis=0)[1:1 + a.shape[0]]


def _local_step(x, mem, target, w):
    s = x.shape[0]
    cs, sn = _rope_tables(s)
    t_ = jnp.transpose

    w_in_p = _w_in_padded(w["w_in_t"])
    xn, rstd_x = _rms_fwd(_Src(x, D_MODEL), w["norm_g"], "norm_x")
    hp, hpb = _mm(xn, w_in_p, "in_proj", mode="nt", second_dtype=BF16)

    w_uq3 = w["w_uq"].reshape(512, MLA_HEADS, 192)
    w_uq_p = jnp.concatenate([w_uq3, w_uq3[:, :, 128:]], axis=2).reshape(512, MLA_HEADS * 256)
    w_ukv_p = t_(w["w_ukv"].reshape(512, MLA_HEADS, 2, 128), (0, 2, 1, 3)).reshape(512, 2048)
    c_q, c_kv = _Src(hp, 512, 0), _Src(hp, 512, 1)
    cqn, rstd_q = _rms_fwd(c_q, w["q_norm_g"], "norm_q")
    ckvn, rstd_kv = _rms_fwd(c_kv, w["kv_norm_g"], "norm_kv")
    q_lin = _mm(cqn, w_uq_p, "mla_q_proj")
    kvb = _mm(ckvn, w_ukv_p, "mla_kv_proj", out_dtype=BF16)
    q_mla = _rope_fwd(_Src(q_lin, MLA_HEADS * 256), cs, sn, MLA_HEADS, 256, LANE, "rope_q")
    k_pe = _rope_fwd(_Src(hp, LANE, PAD["k_rope"] // LANE), cs, sn, 1, LANE, 0, "rope_k")
    mla = _Attn("mla", s, s, MLA_HEADS, 256)
    mla_q, mla_v = _Src(q_mla, 256), _Src(kvb, LANE, MLA_HEADS)
    mla_k = [_Src(kvb, LANE), _Src(k_pe, LANE, 0, False)]
    o_mla, l_mla, lr_mla = _attn_fwd(mla, mla_q, mla_k, mla_v, None, "mla_fwd")

    sk = s // CMP_STRIDE
    pe_k, pe_v = w["cmp_pe_k"], w["cmp_pe_v"]
    w1k = _pad_cols(w["cmp_w1k"], 256)
    w2k = jnp.pad(w["cmp_w2k"], ((0, 64), (0, 64))).astype(BF16)
    w1v, w2v = w["cmp_w1v"], w["cmp_w2v"].astype(BF16)
    half_k, half_v = CMP_STRIDE * NSA_DK, CMP_STRIDE * HEAD_V
    ak = hp[:, PAD["k_c"]:PAD["k_c"] + NSA_DK].reshape(sk, half_k)
    av = hp[:, PAD["v_c"]:PAD["v_c"] + HEAD_V].reshape(sk, half_v)
    ck_args = (ak, _shift_up(ak), pe_k[:CMP_STRIDE].reshape(1, half_k), pe_k[CMP_STRIDE:].reshape(1, half_k),
               w1k[:half_k], w1k[half_k:], w2k)
    cv_args = (av, _shift_up(av), pe_v[:CMP_STRIDE].reshape(1, half_v), pe_v[CMP_STRIDE:].reshape(1, half_v),
               w1v[:half_v], w1v[half_v:], w2v)
    k_cmp, pre_k = _compress_fwd(*ck_args, "compress_k")
    v_cmp, pre_v = _compress_fwd(*cv_args, "compress_v")
    cmp_ = _Attn("cmp", s, sk, NSA_HEADS, 256)
    slc = _Attn("slc", s, s, NSA_HEADS, 256)
    win = _Attn("win", s, s, NSA_HEADS, 256)
    nsa_q = _Src(hpb, 256, PAD["q_nsa"] // 256)
    cmp_k, cmp_v = [_Src(k_cmp, 256, 0, False)], _Src(v_cmp, HEAD_V, 0, False)
    slc_k, slc_v = [_Src(hpb, 256, PAD["k_s"] // 256, False)], _Src(hpb, HEAD_V, PAD["v_s"] // HEAD_V, False)
    win_k, win_v = [_Src(hpb, 256, PAD["k_w"] // 256, False)], _Src(hpb, HEAD_V, PAD["v_w"] // HEAD_V, False)
    o_cmp, l_cmp, lr_cmp = _attn_fwd(cmp_, nsa_q, cmp_k, cmp_v, None, "cmp_fwd")
    sel, selt = _select(cmp_, _Src(hpb, NSA_HEADS * 256, PAD["q_nsa"] // (NSA_HEADS * 256)), k_cmp,
                        _overlap_table(s))
    o_slc, l_slc, lr_slc = _attn_fwd(slc, nsa_q, slc_k, slc_v, sel, "slc_fwd")
    o_win, l_win, lr_win = _attn_fwd(win, nsa_q, win_k, win_v, None, "win_fwd")
    gl = _Src(hp, LANE, PAD["g_nsa"] // LANE)
    o_nsa = _nsa_combine(o_cmp, o_slc, o_win, gl)

    mn, rstd_m = _rms_fwd(_Src(mem, D_MODEL), w["mem_norm_g"], "norm_mem")
    kvm = _mm(mn, w["w_mem_kv"], "mem_kv_proj", out_dtype=BF16)
    mem_ = _Attn("mem", s, mem.shape[0], MEM_HEADS, LANE)
    mem_q, mem_k, mem_v = _Src(hpb, LANE, PAD["q_mem"] // LANE), [_Src(kvm, LANE)], _Src(kvm, LANE, MEM_HEADS)
    o_mem, l_mem, lr_mem = _attn_fwd(mem_, mem_q, mem_k, mem_v, None, "mem_fwd")

    u = _gate_fwd(o_mla, o_nsa, o_mem, hp)
    proj = _mm(u, w["w_out"], "out_proj")
    dy, g_final, loss = _final_loss(x, proj, w["final_norm_g"].reshape(1, -1), target)

    g_w_out = _mm(u, dy, "out_proj_dw", mode="tn")
    du = _mm(dy, w["w_out"], "out_proj_dx", mode="nt")
    do_cat, dz = _gate_bwd(du, o_mla, o_nsa, o_mem, hp)

    dq_mem, (dk_mem,), dv_mem = _attn_bwd(mem_, mem_q, mem_k, mem_v, None, None, _Src(o_mem, HEAD_V), l_mem,
                                          lr_mem, _Src(do_cat, HEAD_V, 12), None, "mem_bwd")
    dkvm = jnp.concatenate([dk_mem, dv_mem], axis=1)
    g_w_mem_kv = _mm(mn, dkvm, "mem_kv_dw", mode="tn")
    dmn = _mm(dkvm, w["w_mem_kv"], "mem_kv_dx", mode="nt")
    _, g_mem_norm = _rms_bwd(_Src(mem, D_MODEL), w["mem_norm_g"], rstd_m, dmn, None, "norm_mem_bwd")

    do_cmp, do_slc, do_win, dgl = _nsa_combine_bwd(do_cat, o_cmp, o_slc, o_win, gl)
    dq_n, (dk_cmp,), dv_cmp = _attn_bwd(cmp_, nsa_q, cmp_k, cmp_v, None, None, _Src(o_cmp, HEAD_V), l_cmp,
                                        lr_cmp, _Src(do_cmp, HEAD_V), None, "cmp_bwd")
    dq_n, (dk_s,), dv_s = _attn_bwd(slc, nsa_q, slc_k, slc_v, sel, selt, _Src(o_slc, HEAD_V), l_slc, lr_slc,
                                    _Src(do_slc, HEAD_V), dq_n, "slc_bwd")
    dq_n, (dk_w,), dv_w = _attn_bwd(win, nsa_q, win_k, win_v, None, None, _Src(o_win, HEAD_V), l_win, lr_win,
                                    _Src(do_win, HEAD_V), dq_n, "win_bwd")
    dak, dpk_lo, dpk_hi, dw1k_lo, dw1k_hi, g_w2k = _compress_bwd(
        *ck_args, pre_k, _shift_down(pre_k), dk_cmp, _shift_down(dk_cmp), "compress_k_bwd")
    dav, dpv_lo, dpv_hi, dw1v_lo, dw1v_hi, g_w2v = _compress_bwd(
        *cv_args, pre_v, _shift_down(pre_v), dv_cmp, _shift_down(dv_cmp), "compress_v_bwd")
    g_pe_k = jnp.concatenate([dpk_lo.reshape(CMP_STRIDE, NSA_DK), dpk_hi.reshape(CMP_STRIDE, NSA_DK)], axis=0)
    g_pe_v = jnp.concatenate([dpv_lo.reshape(CMP_STRIDE, HEAD_V), dpv_hi.reshape(CMP_STRIDE, HEAD_V)], axis=0)
    g_w1k = jnp.concatenate([dw1k_lo, dw1k_hi], axis=0)[:, :NSA_DK]
    g_w1v = jnp.concatenate([dw1v_lo, dw1v_hi], axis=0)
    dk_c = _pad_cols(dak.reshape(s, NSA_DK), 256)
    dv_c = dav.reshape(s, HEAD_V)

    dq_m, (dk_nope, dk_pe), dv_m = _attn_bwd(mla, mla_q, mla_k, mla_v, None, None, _Src(o_mla, HEAD_V), l_mla,
                                             lr_mla, _Src(do_cat, HEAD_V), None, "mla_bwd")
    dq_lin = _rope_bwd_q(dq_m, cs, sn)
    dkv_lin, d_krope = _rope_bwd_k(dk_nope, dk_pe, dv_m, cs, sn)
    g_w_uq_p = _mm(cqn, dq_lin, "mla_q_dw", mode="tn")
    dcqn = _mm(dq_lin, w_uq_p, "mla_q_dx", mode="nt")
    g_w_ukv_p = _mm(ckvn, dkv_lin, "mla_kv_dw", mode="tn")
    dckvn = _mm(dkv_lin, w_ukv_p, "mla_kv_dx", mode="nt")
    dc_q, g_q_norm = _rms_bwd(c_q, w["q_norm_g"], rstd_q, dcqn, None, "norm_q_bwd")
    dc_kv, g_kv_norm = _rms_bwd(c_kv, w["kv_norm_g"], rstd_kv, dckvn, None, "norm_kv_bwd")
    g_w_uq = g_w_uq_p.reshape(512, MLA_HEADS, 256)[:, :, :192].reshape(512, MLA_HEADS * 192)
    g_w_ukv = t_(g_w_ukv_p.reshape(512, 2, MLA_HEADS, 128), (0, 2, 1, 3)).reshape(512, 2048)

    dhp = jnp.concatenate(
        [dc_q, dc_kv, dq_n, dk_c, dk_s, dk_w, d_krope, dv_c, dv_s, dv_w, dgl, dq_mem,
         jnp.zeros((s, PAD["z"] - (PAD["q_mem"] + 512)), F32), dz], axis=1)
    g_w_in_t = _w_in_unpadded(_mm(dhp, xn, "in_proj_dw", mode="tn"))
    dxn = _mm(dhp, w_in_p, "in_proj_dx")
    grad_x, g_norm = _rms_bwd(_Src(x, D_MODEL), w["norm_g"], rstd_x, dxn, dy, "norm_x_bwd")

    grads = dict(norm_g=g_norm, w_in_t=g_w_in_t, q_norm_g=g_q_norm, w_uq=g_w_uq, kv_norm_g=g_kv_norm,
                 w_ukv=g_w_ukv, cmp_pe_k=g_pe_k, cmp_pe_v=g_pe_v, cmp_w1k=g_w1k, cmp_w2k=g_w2k[:NSA_DK, :NSA_DK],
                 cmp_w1v=g_w1v, cmp_w2v=g_w2v, mem_norm_g=g_mem_norm, w_mem_kv=g_w_mem_kv, w_out=g_w_out,
                 final_norm_g=g_final.reshape(-1))
    return loss[0, 0], grad_x, grads


def kernel(x, mem, norm_g, w_in, q_norm_g, w_uq, kv_norm_g, w_ukv, cmp_pe_k, cmp_pe_v, cmp_w1k, cmp_w2k, cmp_w1v, cmp_w2v, mem_norm_g, w_mem_kv, w_out, final_norm_g, loss_target, m_norm_g, m_w_in, m_q_norm_g, m_w_uq, m_kv_norm_g, m_w_ukv, m_cmp_pe_k, m_cmp_pe_v, m_cmp_w1k, m_cmp_w2k, m_cmp_w1v, m_cmp_w2v, m_mem_norm_g, m_w_mem_kv, m_w_out, m_final_norm_g, v_norm_g, v_w_in, v_q_norm_g, v_w_uq, v_kv_norm_g, v_w_ukv, v_cmp_pe_k, v_cmp_pe_v, v_cmp_w1k, v_cmp_w2k, v_cmp_w1v, v_cmp_w2v, v_mem_norm_g, v_w_mem_kv, v_w_out, v_final_norm_g):
    args = dict(locals())
    wts = {n: args[n] for n in WEIGHTS}
    loc = {n: (a if n == "final_norm_g" else a[0]) for n, a in wts.items()}

    def to_x(n, a):
        return a.T if n == "w_in" else a

    split = [1 if n == "w_in" else 0 for n in SHARDED]

    own = [to_x(n, loc[n]).astype(BF16) for n in SHARDED]
    chip = 2 * lax.axis_index("x") + lax.axis_index("y")
    gathered = [lax.dynamic_update_slice(gw, a[None], (chip, 0, 0))
                for gw, a in zip(_gather_shards(own, split), own)]
    full = {n: loc[n].reshape(1, -1) if loc[n].ndim == 1 else loc[n] for n in REPLICATED}
    for n, gw in zip(SHARDED, gathered):
        if n == "w_in":
            full["w_in_t"] = gw.reshape(4 * gw.shape[1], gw.shape[2])
        elif SHARD_AXIS[n] == 0:
            full[n] = gw.reshape(4 * gw.shape[1], gw.shape[2])
        else:
            full[n] = jnp.concatenate([gw[j] for j in range(4)], axis=1)

    loss, grad_x, g = _local_step(x[0], mem[0], loss_target[0], full)
    loss = lax.psum(loss, ("x", "y", "c"))

    def slots(n):
        a = g["w_in_t"] if n == "w_in" else g[n]
        if n == "w_in" or SHARD_AXIS[n] == 0:
            return a.reshape(4, a.shape[0] // 4, a.shape[1])
        width = a.shape[1] // 4
        return jnp.stack([a[:, j * width:(j + 1) * width] for j in range(4)])

    gs = [slots(n) for n in SHARDED]
    core = lax.axis_index("c").astype(jnp.int32).reshape(1)
    theirs = _pair_exchange(gs, split)
    pairs = [_pair_sum(a, b, core, ax, "pair_sum_" + n) for n, a, b, ax in zip(SHARDED, gs, theirs, split)]
    from_chips = _chip_exchange(pairs)
    mine = [_sum_slots(b, "chip_sum_" + n) for n, b in zip(SHARDED, from_chips)]
    g_sh = [lax.dynamic_update_slice(o, t, (core[0] * t.shape[0], 0) if ax == 0 else (0, core[0] * t.shape[1]))
            for o, t, ax in zip(_half_exchange(mine, split), mine, split)]

    n_rep = sum(int(np.prod(loc[n].shape)) for n in REPLICATED)
    rows_rep = -(-n_rep // (8 * LANE)) * 8

    def rep_pack(parts):
        flat = jnp.concatenate([p.reshape(-1) for p in parts])
        return jnp.pad(flat, (0, rows_rep * LANE - n_rep)).reshape(rows_rep, LANE)

    g_rep = _sum_slots(_gather_all(rep_pack([g[n] for n in REPLICATED])), "replica_sum")
    d_rp, m_rp, v_rp = _adamw(rep_pack([wts[n] for n in REPLICATED]), g_rep,
                              rep_pack([args["m_" + n] for n in REPLICATED]),
                              rep_pack([args["v_" + n] for n in REPLICATED]), "adamw_replicated")

    def rep_unpack(buf):
        flat, out, o = buf.reshape(-1), {}, 0
        for n in REPLICATED:
            size = int(np.prod(wts[n].shape))
            out[n] = flat[o:o + size].reshape(wts[n].shape)
            o += size
        return out

    outs = {k: rep_unpack(b) for k, b in (("g", g_rep), ("d", d_rp), ("m", m_rp), ("v", v_rp))}
    for n, gn in zip(SHARDED, g_sh):
        d, mo, vo = _adamw(to_x(n, loc[n]), gn, to_x(n, args["m_" + n][0]), to_x(n, args["v_" + n][0]),
                           "adamw_" + n)
        for k, a in (("g", gn), ("d", d), ("m", mo), ("v", vo)):
            outs[k][n] = to_x(n, a).reshape(wts[n].shape)

    return (loss, grad_x[None], *[outs["g"][n] for n in WEIGHTS], *[outs["d"][n] for n in WEIGHTS],
            *[outs["m"][n] for n in WEIGHTS], *[outs["v"][n] for n in WEIGHTS])
```

```python
from typing import NamedTuple

import numpy as np
import jax
import jax.numpy as jnp
from jax import lax
from jax.experimental import pallas as pl
from jax.experimental.pallas import tpu as pltpu

F32 = jnp.float32
BF16 = jnp.bfloat16
MESH = pl.DeviceIdType.MESH

D_MODEL = 2048
EPS = 1e-6
LANE = 128
HEAD_V = 128
MLA_HEADS = 8
NSA_HEADS = 4
MEM_HEADS = 4
NSA_DK = 192
CMP_STRIDE = 16
CMP_LEN = 32
SLC_LEN = 64
SLC_TOPN = 16
WIN = 512
NEG = -1e30
LOG2E = 1.4426950408889634
ROPE_THETA = 10000.0
BLOCK_BYTES = 2 << 20

ORIG = dict(c_q=(0, 512), c_kv=(512, 512), k_rope=(1024, 64), z_mla=(1088, 1024),
            q_nsa=(2112, 768), k_c=(2880, 192), v_c=(3072, 128), k_s=(3200, 192),
            v_s=(3392, 128), k_w=(3520, 192), v_w=(3712, 128), g_nsa=(3840, 12),
            z_nsa=(3852, 512), q_mem=(4364, 512), z_mem=(4876, 512))
PAD = dict(c_q=0, c_kv=512, q_nsa=1024, k_c=2048, k_s=2304, k_w=2560, k_rope=2816, v_c=2944,
           v_s=3072, v_w=3200, g_nsa=3328, q_mem=3456, z=4096)
D_PAD = 6144

ADAM_LR, ADAM_B1, ADAM_B2, ADAM_EPS, ADAM_WD, ADAM_STEP = 0.001, 0.9, 0.999, 1e-08, 0.01, 10

SHARDED = ("w_in", "w_uq", "w_ukv", "cmp_w1k", "cmp_w1v", "w_mem_kv", "w_out")
SHARD_AXIS = dict(w_in=1, w_uq=1, w_ukv=1, cmp_w1k=0, cmp_w1v=0, w_mem_kv=0, w_out=0)
REPLICATED = ("norm_g", "q_norm_g", "kv_norm_g", "cmp_pe_k", "cmp_pe_v", "cmp_w2k", "cmp_w2v",
              "mem_norm_g", "final_norm_g")
WEIGHTS = ("norm_g", "w_in", "q_norm_g", "w_uq", "kv_norm_g", "w_ukv", "cmp_pe_k", "cmp_pe_v",
           "cmp_w1k", "cmp_w2k", "cmp_w1v", "cmp_w2v", "mem_norm_g", "w_mem_kv", "w_out",
           "final_norm_g")


def _pcall(kernel, **kw):
    return pl.pallas_call(kernel, **kw)


def _tile(n, pref):
    if n <= pref:
        return n
    for t in range(pref, LANE - 1, -LANE):
        if n % t == 0:
            return t
    raise ValueError((n, pref))


def _row_tile(rows, cols, itemsize=4):
    want = max(16, BLOCK_BYTES // (cols * itemsize))
    if rows <= want:
        return rows
    t = 16
    best = rows
    while t <= want:
        if rows % t == 0:
            best = t
        t *= 2
    return best


def _nt(a, b):
    return lax.dot_general(a, b, (((1,), (1,)), ((), ())), preferred_element_type=F32)


def _tn(a, b):
    return lax.dot_general(a, b, (((0,), (0,)), ((), ())), preferred_element_type=F32)


def _nn(a, b):
    return jnp.dot(a, b, preferred_element_type=F32)


def _sigmoid(x):
    return 1.0 / (1.0 + jnp.exp(-x))


class _Src(NamedTuple):
    arr: jax.Array
    width: int
    col0: int = 0
    per_head: bool = True

    def col(self, h):
        return self.col0 + h if self.per_head else self.col0


def _mm(a, b, name, mode="nn", out_dtype=F32, second_dtype=None):
    if mode == "tn":
        k, m = a.shape
    else:
        m, k = a.shape
    if mode == "nt":
        n, k2 = b.shape
    else:
        k2, n = b.shape
    assert k == k2, (a.shape, b.shape, mode)
    tm, tn, tk = _tile(m, 1024), _tile(n, 1024), _tile(k, 2048)
    nk = k // tk
    assert nk == 1 or (out_dtype == F32 and second_dtype is None)
    dot = {"nn": _nn, "nt": _nt, "tn": _tn}[mode]

    def kern(a_ref, b_ref, o_ref, *more):
        r = dot(a_ref[...].astype(BF16), b_ref[...].astype(BF16))
        if nk == 1:
            o_ref[...] = r.astype(out_dtype)
            if more:
                more[0][...] = r.astype(second_dtype)
        else:
            kk = pl.program_id(2)

            @pl.when(kk == 0)
            def _():
                o_ref[...] = r

            @pl.when(kk > 0)
            def _():
                o_ref[...] += r

    a_spec = (pl.BlockSpec((tk, tm), lambda i, j, kk: (kk, i)) if mode == "tn"
              else pl.BlockSpec((tm, tk), lambda i, j, kk: (i, kk)))
    b_spec = (pl.BlockSpec((tn, tk), lambda i, j, kk: (j, kk)) if mode == "nt"
              else pl.BlockSpec((tk, tn), lambda i, j, kk: (kk, j)))
    o_spec = pl.BlockSpec((tm, tn), lambda i, j, kk: (i, j))
    out_shape = jax.ShapeDtypeStruct((m, n), out_dtype)
    if second_dtype is not None:
        o_spec = [o_spec, o_spec]
        out_shape = [out_shape, jax.ShapeDtypeStruct((m, n), second_dtype)]
    return _pcall(
        kern, name=name, grid=(m // tm, n // tn, nk), in_specs=[a_spec, b_spec], out_specs=o_spec,
        out_shape=out_shape,
        compiler_params=pltpu.CompilerParams(dimension_semantics=("parallel", "parallel", "arbitrary")),
    )(a, b)


def _rms_fwd(x, g, name):
    r, d = x.arr.shape[0], x.width
    tr = _tile(r, 512)

    def kern(x_ref, g_ref, y_ref, r_ref):
        xv = x_ref[...]
        rstd = lax.rsqrt(jnp.mean(xv * xv, axis=-1, keepdims=True) + EPS)
        y_ref[...] = (xv * rstd * g_ref[...]).astype(BF16)
        r_ref[...] = rstd

    return _pcall(
        kern, name=name, grid=(r // tr,),
        in_specs=[pl.BlockSpec((tr, d), lambda i: (i, x.col0)), pl.BlockSpec((1, d), lambda i: (0, 0))],
        out_specs=[pl.BlockSpec((tr, d), lambda i: (i, 0)), pl.BlockSpec((tr, 1), lambda i: (i, 0))],
        out_shape=[jax.ShapeDtypeStruct((r, d), BF16), jax.ShapeDtypeStruct((r, 1), F32)],
    )(x.arr, g)


def _rms_bwd(x, g, rstd, dy, add, name):
    r, d = x.arr.shape[0], x.width
    tr = _tile(r, 256)
    has_add = add is not None

    def kern(*refs):
        if has_add:
            x_ref, g_ref, r_ref, dy_ref, add_ref, dx_ref, dg_ref = refs
        else:
            x_ref, g_ref, r_ref, dy_ref, dx_ref, dg_ref = refs
        rs = r_ref[...]
        xhat = x_ref[...] * rs
        dyv = dy_ref[...]
        dyg = dyv * g_ref[...]
        c = jnp.mean(dyg * xhat, axis=-1, keepdims=True)
        dx = rs * (dyg - xhat * c)
        if has_add:
            dx = dx + add_ref[...]
        dx_ref[...] = dx
        part = jnp.sum(dyv * xhat, axis=0, keepdims=True)

        @pl.when(pl.program_id(0) == 0)
        def _():
            dg_ref[...] = part

        @pl.when(pl.program_id(0) > 0)
        def _():
            dg_ref[...] += part

    row = pl.BlockSpec((tr, d), lambda i: (i, 0))
    vec = pl.BlockSpec((1, d), lambda i: (0, 0))
    ins = [pl.BlockSpec((tr, d), lambda i: (i, x.col0)), vec, pl.BlockSpec((tr, 1), lambda i: (i, 0)), row]
    ins += [row] if has_add else []
    args = (x.arr, g, rstd, dy) + ((add,) if has_add else ())
    return _pcall(
        kern, name=name, grid=(r // tr,), in_specs=ins, out_specs=[row, vec],
        out_shape=[jax.ShapeDtypeStruct((r, d), F32), jax.ShapeDtypeStruct((1, d), F32)],
        compiler_params=pltpu.CompilerParams(dimension_semantics=("arbitrary",)),
    )(*args)


def _final_loss(x, proj, g, target):
    r, d = x.shape
    tr = _tile(r, 256)

    def kern(x_ref, p_ref, g_ref, t_ref, dy_ref, dg_ref, loss_ref):
        y = x_ref[...] + p_ref[...]
        rs = lax.rsqrt(jnp.mean(y * y, axis=-1, keepdims=True) + EPS)
        yhat = y * rs
        gv = g_ref[...]
        e = yhat * gv - t_ref[...]
        lpart = 0.5 * jnp.sum(jnp.mean(e * e, axis=-1, keepdims=True), axis=0, keepdims=True)
        dout = e * (1.0 / d)
        dyg = dout * gv
        c = jnp.mean(dyg * yhat, axis=-1, keepdims=True)
        dy_ref[...] = rs * (dyg - yhat * c)
        gpart = jnp.sum(dout * yhat, axis=0, keepdims=True)
        lrow = jnp.broadcast_to(lpart, (1, LANE))

        @pl.when(pl.program_id(0) == 0)
        def _():
            dg_ref[...] = gpart
            loss_ref[...] = lrow

        @pl.when(pl.program_id(0) > 0)
        def _():
            dg_ref[...] += gpart
            loss_ref[...] += lrow

    row = pl.BlockSpec((tr, d), lambda i: (i, 0))
    vec = pl.BlockSpec((1, d), lambda i: (0, 0))
    return _pcall(
        kern, name="final_loss", grid=(r // tr,), in_specs=[row, row, vec, row],
        out_specs=[row, vec, pl.BlockSpec((1, LANE), lambda i: (0, 0))],
        out_shape=[jax.ShapeDtypeStruct((r, d), F32), jax.ShapeDtypeStruct((1, d), F32),
                   jax.ShapeDtypeStruct((1, LANE), F32)],
        compiler_params=pltpu.CompilerParams(dimension_semantics=("arbitrary",)),
    )(x, proj, g, target)


def _rope_fwd(x, cs, sn, nh, width, off, name):
    s = x.arr.shape[0]
    tr = _tile(s, 512)

    def kern(x_ref, c_ref, s_ref, o_ref):
        cv, sv = c_ref[...], s_ref[...]
        for h in range(nh):
            b = h * width
            if off:
                o_ref[:, b:b + off] = x_ref[:, b:b + off].astype(BF16)
            xr = x_ref[:, b + off:b + off + LANE]
            o_ref[:, b + off:b + off + LANE] = (xr * cv + pltpu.roll(xr, 32, 1) * sv).astype(BF16)

    tab = pl.BlockSpec((tr, LANE), lambda i: (i, 0))
    return _pcall(
        kern, name=name, grid=(s // tr,),
        in_specs=[pl.BlockSpec((tr, nh * width), lambda i: (i, x.col0)), tab, tab],
        out_specs=pl.BlockSpec((tr, nh * width), lambda i: (i, 0)),
        out_shape=jax.ShapeDtypeStruct((s, nh * width), BF16),
    )(x.arr, cs, sn)


def _rope_grad(d, cv, sv):
    g2 = d * sv
    g2 = g2 + pltpu.roll(g2, 64, 1)
    lane = lax.broadcasted_iota(jnp.int32, d.shape, 1)
    return jnp.where(lane < 64, d * cv + pltpu.roll(g2, 32, 1), 0.0)


def _rope_bwd_q(dq, cs, sn):
    s, w = dq.shape
    tr = _tile(s, 512)
    nh = w // 256

    def kern(d_ref, c_ref, s_ref, o_ref):
        cv, sv = c_ref[...], s_ref[...]
        for h in range(nh):
            b = h * 256
            o_ref[:, b:b + LANE] = d_ref[:, b:b + LANE]
            o_ref[:, b + LANE:b + 256] = _rope_grad(d_ref[:, b + LANE:b + 256], cv, sv)

    row = pl.BlockSpec((tr, w), lambda i: (i, 0))
    tab = pl.BlockSpec((tr, LANE), lambda i: (i, 0))
    return _pcall(kern, name="rope_bwd_q", grid=(s // tr,), in_specs=[row, tab, tab], out_specs=row,
                  out_shape=jax.ShapeDtypeStruct((s, w), F32))(dq, cs, sn)


def _rope_bwd_k(dk_nope, dk_pe, dv, cs, sn):
    s, w = dk_nope.shape
    tr = _tile(s, 512)

    def kern(dk_ref, dp_ref, dv_ref, c_ref, s_ref, okv_ref, okr_ref):
        okv_ref[:, :w] = dk_ref[...]
        okv_ref[:, w:] = dv_ref[...]
        okr_ref[...] = _rope_grad(dp_ref[...], c_ref[...], s_ref[...])

    tab = pl.BlockSpec((tr, LANE), lambda i: (i, 0))
    wide = pl.BlockSpec((tr, w), lambda i: (i, 0))
    return _pcall(
        kern, name="rope_bwd_k", grid=(s // tr,), in_specs=[wide, tab, wide, tab, tab],
        out_specs=[pl.BlockSpec((tr, 2 * w), lambda i: (i, 0)), tab],
        out_shape=[jax.ShapeDtypeStruct((s, 2 * w), F32), jax.ShapeDtypeStruct((s, LANE), F32)],
    )(dk_nope, dk_pe, dv, cs, sn)


class _Attn:
    def __init__(self, mode, s, sk, heads, dk):
        self.mode, self.s, self.sk, self.h, self.dk = mode, s, sk, heads, dk
        self.scale = {"mla": 192 ** -0.5, "mem": 128 ** -0.5}.get(mode, NSA_DK ** -0.5)
        self.tb = min(256, s)
        self.nb = s // self.tb
        self.nsub = 2 if self.nb % 2 == 0 else 1
        self.tq = self.tb * self.nsub
        self.nq = s // self.tq
        self.causal = mode in ("mla", "slc")
        if self.causal:
            self.tk = self.tq
        elif mode == "win":
            self.tk = WIN + self.tb
        else:
            self.tk = sk
        self.tkb = min(512, sk)
        self.ksub = 2 if self.tkb == 512 and mode == "mla" else 1
        self.kb = self.tkb // self.ksub
        self.ncmp = s // CMP_STRIDE - 1

    def mask_bias(self, t, n, h, selx, diag):
        m = self.mode
        if m == "mla":
            return (n <= t) if diag else None, None
        if m == "mem":
            return None, None
        slope = jnp.where(h == 0, 0.25, jnp.where(h == 1, 0.0625, jnp.where(h == 2, 0.015625, 0.00390625)))
        slope = slope.astype(F32) * LOG2E
        if m == "cmp":
            mask = (n * CMP_STRIDE + (CMP_LEN - 1) <= t) & (n < self.ncmp)
            pos = n.astype(F32) * float(CMP_STRIDE) + (CMP_LEN - 1) / 2.0
            return mask, slope * pos
        rel = t - n
        if m == "slc":
            mask = selx > 0.5
            return (mask & (rel >= 0)) if diag else mask, slope * n.astype(F32)
        return (rel >= 0) & (rel < WIN), slope * n.astype(F32)


def _scores(cfg, s_raw, t, n, h, selx, diag, lse=None):
    s = s_raw * (cfg.scale * LOG2E)
    mask, key_term = cfg.mask_bias(t, n, h, selx, diag)
    if key_term is not None:
        s = s + key_term
    if lse is None:
        if mask is not None:
            s = jnp.where(mask, s, NEG)
        return s, mask
    p = jnp.exp2(jnp.minimum(s - lse, 0.0))
    if mask is not None:
        p = jnp.where(mask, p, 0.0)
    return p, mask


def _block_of_key(k0, tk, keys_on_rows):
    shape = (tk, LANE) if keys_on_rows else (LANE, tk)
    n = lax.broadcasted_iota(jnp.int32, shape, 0 if keys_on_rows else 1) + k0
    j = lax.broadcasted_iota(jnp.int32, shape, 1 if keys_on_rows else 0)
    return jnp.where((n >> 6) == j, 1.0, 0.0).astype(BF16)


def _to_row(col):
    t = col.shape[0]
    return jnp.transpose(jnp.broadcast_to(col, (t, LANE)))[0:1, :]


def _load_keys(k_refs, rows):
    parts = [r[rows, :].astype(BF16) for r in k_refs]
    return parts[0] if len(parts) == 1 else jnp.concatenate(parts, axis=1)


def _attn_fwd(cfg, q, ks, v, sel, name):
    s, tq, tk, tb, nsub = cfg.s, cfg.tq, cfg.tk, cfg.tb, cfg.nsub
    has_sel = sel is not None
    nkp = len(ks)

    def kern(*refs):
        q_ref, k_refs, v_ref = refs[0], refs[1:1 + nkp], refs[1 + nkp]
        sel_ref = refs[2 + nkp] if has_sel else None
        o_ref, lc_ref, lr_ref = refs[2 + nkp + has_sel:5 + nkp + has_sel]
        h, i = pl.program_id(0), pl.program_id(1)
        part = [slice(r * tb, (r + 1) * tb) for r in range(nsub)]
        qs = [q_ref[p, :].astype(BF16) for p in part]
        ts = [i * tq + r * tb + lax.broadcasted_iota(jnp.int32, (tb, 1), 0) for r in range(nsub)]
        sels = [sel_ref[p, :].astype(BF16) for p in part] if has_sel else None

        def load(k0):
            rows = pl.ds(k0, tk)
            return _load_keys(k_refs, rows), v_ref[rows, :].astype(BF16)

        def soft(r, k0, s_raw, vv, emat, carry, diag):
            m, l, acc = carry
            n = k0 + lax.broadcasted_iota(jnp.int32, (1, tk), 1)
            selx = _nn(sels[r], emat) if has_sel else None
            sc, mask = _scores(cfg, s_raw, ts[r], n, h, selx, diag)
            m_new = jnp.maximum(m, jnp.max(sc, axis=1, keepdims=True))
            alpha = jnp.exp2(m - m_new)
            p = jnp.exp2(sc - m_new)
            if mask is not None:
                p = jnp.where(mask, p, 0.0)
            l = alpha * l + jnp.sum(p, axis=1, keepdims=True)
            acc = alpha * acc + _nn(p.astype(BF16), vv)
            return m_new, l, acc

        def step(r, k0, kk, vv, emat, carry, diag):
            return soft(r, k0, _nt(qs[r], kk), vv, emat, carry, diag)

        def chunk(k0, carry, diag):
            kk, vv = load(k0)
            emat = _block_of_key(k0, tk, False) if has_sel else None
            return tuple(step(r, k0, kk, vv, emat, carry[r], diag) for r in range(nsub))

        init = (jnp.full((tb, 1), NEG, F32), jnp.zeros((tb, 1), F32), jnp.zeros((tb, HEAD_V), F32))
        carry = (init,) * nsub
        if cfg.causal:
            buf_a, buf_b = refs[-2:]
            full = (i * tq) // tk

            def scores_into(buf, c):
                kk = _load_keys(k_refs, pl.ds(pl.multiple_of(c * tk, tk), tk))
                for r in range(nsub):
                    buf[r] = _nt(qs[r], kk)

            def consume(buf, c, cr, diag):
                k0 = pl.multiple_of(c * tk, tk)
                vv = v_ref[pl.ds(k0, tk), :].astype(BF16)
                emat = _block_of_key(k0, tk, False) if has_sel else None
                return tuple(soft(r, k0, buf[r], vv, emat, cr[r], diag) for r in range(nsub))

            def pair(p, cr):
                scores_into(buf_b, 2 * p + 1)
                cr = consume(buf_a, 2 * p, cr, False)
                scores_into(buf_a, 2 * p + 2)
                return consume(buf_b, 2 * p + 1, cr, False)

            def odd_tail(cr):
                scores_into(buf_b, full)
                cr = consume(buf_a, full - 1, cr, False)
                return consume(buf_b, full, cr, True)

            scores_into(buf_a, 0)
            carry = lax.fori_loop(0, full // 2, pair, carry)
            carry = lax.cond(full % 2 == 1, odd_tail, lambda cr: consume(buf_a, full, cr, True), carry)
        elif cfg.mode == "win":
            starts = [pl.multiple_of(jnp.maximum(i * tq + r * tb - WIN, 0), tb) for r in range(nsub)]
            carry = tuple(step(r, k0, *load(k0), None, carry[r], True) for r, k0 in enumerate(starts))
        else:
            carry = chunk(0, carry, True)
        for r, (m, l, acc) in enumerate(carry):
            o_ref[part[r], :] = acc / (l + 1e-20)
            lse = m + jnp.log(l + 1e-20) * LOG2E
            lc_ref[0, part[r], :] = lse
            lr_ref[0, r] = _to_row(lse)

    ins = [pl.BlockSpec((tq, q.width), lambda h, i: (i, q.col(h)))]
    ins += [pl.BlockSpec((cfg.sk, p.width), lambda h, i, p=p: (0, p.col(h))) for p in ks]
    ins += [pl.BlockSpec((cfg.sk, HEAD_V), lambda h, i: (0, v.col(h)))]
    args = [q.arr] + [p.arr for p in ks] + [v.arr]
    if has_sel:
        ins.append(pl.BlockSpec((tq, LANE), lambda h, i: (i, 0)))
        args.append(sel)
    return _pcall(
        kern, name=name, grid=(cfg.h, cfg.nq), in_specs=ins,
        out_specs=[pl.BlockSpec((tq, HEAD_V), lambda h, i: (i, h)),
                   pl.BlockSpec((1, tq, 1), lambda h, i: (h, i, 0)),
                   pl.BlockSpec((1, nsub, 1, tb), lambda h, i: (h, i, 0, 0))],
        out_shape=[jax.ShapeDtypeStruct((s, cfg.h * HEAD_V), F32),
                   jax.ShapeDtypeStruct((cfg.h, s, 1), F32),
                   jax.ShapeDtypeStruct((cfg.h, cfg.nb, 1, tb), F32)],
        scratch_shapes=[pltpu.VMEM((nsub, tb, tk), F32)] * 2 if cfg.causal else [],
        compiler_params=pltpu.CompilerParams(dimension_semantics=("parallel", "parallel")),
    )(*args)


def _attn_dq(cfg, q, ks, v, sel, o, lse, do, dq_in, name):
    s, tq, tk, dk, tb, nsub = cfg.s, cfg.tq, cfg.tk, cfg.dk, cfg.tb, cfg.nsub
    has_sel = sel is not None
    has_in = dq_in is not None
    nkp = len(ks)

    def kern(*refs):
        refs = list(refs)
        q_ref, k_refs, v_ref = refs[0], refs[1:1 + nkp], refs[1 + nkp]
        p0 = 2 + nkp
        sel_ref = refs[p0] if has_sel else None
        p0 += has_sel
        o_ref, l_ref, do_ref = refs[p0:p0 + 3]
        p0 += 3
        in_ref = refs[p0] if has_in else None
        dq_ref, dr_ref = refs[-2:]
        h, i = pl.program_id(0), pl.program_id(1)
        part = [slice(r * tb, (r + 1) * tb) for r in range(nsub)]
        qs = [q_ref[p, :].astype(BF16) for p in part]
        ts = [i * tq + r * tb + lax.broadcasted_iota(jnp.int32, (tb, 1), 0) for r in range(nsub)]
        sels = [sel_ref[p, :].astype(BF16) for p in part] if has_sel else None
        dvecs, dobs, lses = [], [], []
        for r, p in enumerate(part):
            dov = do_ref[p, :]
            dvec = jnp.sum(dov * o_ref[p, :], axis=1, keepdims=True)
            dr_ref[0, r] = _to_row(dvec)
            dvecs.append(dvec)
            dobs.append(dov.astype(BF16))
            lses.append(l_ref[0, p, :])

        def load(k0):
            rows = pl.ds(k0, tk)
            return _load_keys(k_refs, rows), v_ref[rows, :].astype(BF16)

        def step(r, k0, kk, vv, emat, acc, diag):
            n = k0 + lax.broadcasted_iota(jnp.int32, (1, tk), 1)
            selx = _nn(sels[r], emat) if has_sel else None
            p, _ = _scores(cfg, _nt(qs[r], kk), ts[r], n, h, selx, diag, lses[r])
            ds = p * (_nt(dobs[r], vv) - dvecs[r])
            return acc + _nn(ds.astype(BF16), kk)

        def chunk(k0, accs, diag):
            kk, vv = load(k0)
            emat = _block_of_key(k0, tk, False) if has_sel else None
            return tuple(step(r, k0, kk, vv, emat, accs[r], diag) for r in range(nsub))

        accs = (jnp.zeros((tb, dk), F32),) * nsub
        if cfg.causal:
            full = (i * tq) // tk
            accs = lax.fori_loop(0, full, lambda c, a: chunk(pl.multiple_of(c * tk, tk), a, False), accs)
            accs = chunk(pl.multiple_of(full * tk, tk), accs, True)
        elif cfg.mode == "win":
            starts = [pl.multiple_of(jnp.maximum(i * tq + r * tb - WIN, 0), tb) for r in range(nsub)]
            accs = tuple(step(r, k0, *load(k0), None, accs[r], True) for r, k0 in enumerate(starts))
        else:
            accs = chunk(0, accs, True)
        for r, p in enumerate(part):
            dq_ref[p, :] = accs[r] * cfg.scale + in_ref[p, :] if has_in else accs[r] * cfg.scale

    qs = pl.BlockSpec((tq, dk), lambda h, i: (i, h))
    ins = [pl.BlockSpec((tq, q.width), lambda h, i: (i, q.col(h)))]
    ins += [pl.BlockSpec((cfg.sk, p.width), lambda h, i, p=p: (0, p.col(h))) for p in ks]
    ins += [pl.BlockSpec((cfg.sk, HEAD_V), lambda h, i: (0, v.col(h)))]
    args = [q.arr] + [p.arr for p in ks] + [v.arr]
    if has_sel:
        ins.append(pl.BlockSpec((tq, LANE), lambda h, i: (i, 0)))
        args.append(sel)
    ins += [pl.BlockSpec((tq, HEAD_V), lambda h, i: (i, o.col(h))),
            pl.BlockSpec((1, tq, 1), lambda h, i: (h, i, 0)),
            pl.BlockSpec((tq, HEAD_V), lambda h, i: (i, do.col(h)))]
    args += [o.arr, lse, do.arr]
    if has_in:
        ins.append(qs)
        args.append(dq_in)
    return _pcall(
        kern, name=name, grid=(cfg.h, cfg.nq), in_specs=ins,
        out_specs=[qs, pl.BlockSpec((1, nsub, 1, tb), lambda h, i: (h, i, 0, 0))],
        out_shape=[jax.ShapeDtypeStruct((s, cfg.h * dk), F32),
                   jax.ShapeDtypeStruct((cfg.h, cfg.nb, 1, tb), F32)],
        compiler_params=pltpu.CompilerParams(dimension_semantics=("parallel", "parallel")),
    )(*args)


def _attn_dkv(cfg, q, ks, v, selt, lse_r, d_r, do, name):
    s, tq, tkb, dk, kb, ksub = cfg.s, cfg.tb, cfg.tkb, cfg.dk, cfg.kb, cfg.ksub
    nq = cfg.nb
    has_sel = selt is not None
    nkp = len(ks)
    outs = list(ks) + [v]

    def kern(*refs):
        k_refs, v_ref = refs[:nkp], refs[nkp]
        q_ref, do_ref, lr_ref, dr_ref = refs[nkp + 1:nkp + 5]
        st_ref = refs[nkp + 5] if has_sel else None
        out_refs = refs[-(nkp + 1):]
        j, h = pl.program_id(0), pl.program_id(1)
        k0 = j * tkb
        part = [slice(u * kb, (u + 1) * kb) for u in range(ksub)]
        kks = [_load_keys(k_refs, p) for p in part]
        vvs = [v_ref[p, :].astype(BF16) for p in part]
        ns = [k0 + u * kb + lax.broadcasted_iota(jnp.int32, (kb, 1), 0) for u in range(ksub)]
        emats = [_block_of_key(k0 + u * kb, kb, True) for u in range(ksub)] if has_sel else None

        def qblock(i, carry, diag):
            r0 = pl.multiple_of(i * tq, tq)
            qi = q_ref[pl.ds(r0, tq), :].astype(BF16)
            doi = do_ref[pl.ds(r0, tq), :].astype(BF16)
            t = i * tq + lax.broadcasted_iota(jnp.int32, (1, tq), 1)
            selt_i = st_ref[i].astype(BF16) if has_sel else None
            new = []
            for u in range(ksub):
                dk_acc, dv_acc = carry[u]
                selx = _nn(emats[u], selt_i) if has_sel else None
                pt, _ = _scores(cfg, _nt(kks[u], qi), t, ns[u], h, selx, diag, lr_ref[0, i])
                dv_acc = dv_acc + _nn(pt.astype(BF16), doi)
                dst = pt * (_nt(vvs[u], doi) - dr_ref[0, i])
                new.append((dk_acc + _nn(dst.astype(BF16), qi), dv_acc))
            return tuple(new)

        carry = ((jnp.zeros((kb, dk), F32), jnp.zeros((kb, HEAD_V), F32)),) * ksub
        masked = lambda i, cr: qblock(i, cr, True)
        if cfg.causal:
            first, past = k0 // tq, (k0 + tkb) // tq
            carry = lax.fori_loop(first, past, masked, carry)
            carry = lax.fori_loop(past, nq, lambda i, cr: qblock(i, cr, False), carry)
        elif cfg.mode == "win":
            carry = lax.fori_loop(k0 // tq, jnp.minimum((k0 + tkb + WIN - 2) // tq + 1, nq), masked, carry)
        else:
            carry = lax.fori_loop(0, nq, masked, carry)
        for u, (dk_acc, dv_acc) in enumerate(carry):
            vals, off = [], 0
            for p in ks:
                vals.append(dk_acc[:, off:off + p.width] * cfg.scale)
                off += p.width
            vals.append(dv_acc)
            for src, ref, val in zip(outs, out_refs, vals):
                if src.per_head:
                    ref[part[u], :] = val
                else:
                    @pl.when(h == 0)
                    def _(ref=ref, val=val, u=u):
                        ref[part[u], :] = val

                    @pl.when(h > 0)
                    def _(ref=ref, val=val, u=u):
                        ref[part[u], :] += val

    rowv = pl.BlockSpec((1, nq, 1, tq), lambda j, h: (h, 0, 0, 0))
    ins = [pl.BlockSpec((tkb, p.width), lambda j, h, p=p: (j, p.col(h))) for p in ks]
    ins += [pl.BlockSpec((tkb, HEAD_V), lambda j, h: (j, v.col(h))),
            pl.BlockSpec((s, q.width), lambda j, h: (0, q.col(h))),
            pl.BlockSpec((s, HEAD_V), lambda j, h: (0, do.col(h))), rowv, rowv]
    args = [p.arr for p in ks] + [v.arr, q.arr, do.arr, lse_r, d_r]
    if has_sel:
        ins.append(pl.BlockSpec((nq, LANE, tq), lambda j, h: (0, 0, 0)))
        args.append(selt)
    out_specs = [pl.BlockSpec((tkb, p.width), lambda j, h, p=p: (j, h if p.per_head else 0)) for p in outs]
    out_shape = [jax.ShapeDtypeStruct((cfg.sk, (cfg.h if p.per_head else 1) * p.width), F32) for p in outs]
    return _pcall(
        kern, name=name, grid=(cfg.sk // tkb, cfg.h), in_specs=ins, out_specs=out_specs, out_shape=out_shape,
        compiler_params=pltpu.CompilerParams(dimension_semantics=("parallel", "arbitrary")),
    )(*args)


def _attn_bwd(cfg, q, ks, v, sel, selt, o, lse, lse_r, do, dq_in, name):
    dq, d_r = _attn_dq(cfg, q, ks, v, sel, o, lse, do, dq_in, name + "_dq")
    res = _attn_dkv(cfg, q, ks, v, selt, lse_r, d_r, do, name + "_dkv")
    return dq, res[:-1], res[-1]


def _select(cfg, q, k_cmp, overlap):
    s, tq, sk = cfg.s, cfg.tb, cfg.sk
    n_s = s // SLC_LEN
    top_n = min(SLC_TOPN, n_s)

    def kern(q_ref, k_ref, ov_ref, sel_ref, selt_ref):
        i = pl.program_id(0)
        t = i * tq + lax.broadcasted_iota(jnp.int32, (tq, 1), 0)
        n = lax.broadcasted_iota(jnp.int32, (1, sk), 1)
        kk = k_ref[...]
        imp = jnp.zeros((tq, LANE), F32)
        for h in range(NSA_HEADS):
            sc, mask = _scores(cfg, _nt(q_ref[:, h * 256:(h + 1) * 256], kk), t, n, h, None, True)
            m = jnp.max(sc, axis=1, keepdims=True)
            e = jnp.where(mask, jnp.exp2(sc - m), 0.0)
            p = e / (jnp.sum(e, axis=1, keepdims=True) + 1e-20)
            imp = imp + _nn(p.astype(BF16), ov_ref[...])
        j = lax.broadcasted_iota(jnp.int32, (tq, LANE), 1)
        cur = t >> 6
        forced = (j == 0) | (j == cur) | (j == cur - 1)
        imp = jnp.where(forced, 1e9, imp)
        imp = jnp.where(j > cur, -1e9, imp)
        imp = jnp.where(j >= n_s, -3e38, imp)

        def pick(_, carry):
            work, chosen = carry
            mx = jnp.max(work, axis=1, keepdims=True)
            first = jnp.min(jnp.where(work == mx, j, LANE), axis=1, keepdims=True)
            hit = j == first
            return jnp.where(hit, -3e38, work), jnp.where(hit, 1.0, chosen)

        _, chosen = lax.fori_loop(0, top_n, pick, (imp, jnp.zeros((tq, LANE), F32)))
        chosen = jnp.where(j <= cur, chosen, 0.0)
        sel_ref[...] = chosen
        selt_ref[0] = jnp.transpose(chosen)

    return _pcall(
        kern, name="nsa_select", grid=(cfg.nb,),
        in_specs=[pl.BlockSpec((tq, NSA_HEADS * 256), lambda i: (i, q.col0)),
                  pl.BlockSpec((sk, 256), lambda i: (0, 0)), pl.BlockSpec((sk, LANE), lambda i: (0, 0))],
        out_specs=[pl.BlockSpec((tq, LANE), lambda i: (i, 0)), pl.BlockSpec((1, LANE, tq), lambda i: (i, 0, 0))],
        out_shape=[jax.ShapeDtypeStruct((s, LANE), F32), jax.ShapeDtypeStruct((cfg.nb, LANE, tq), F32)],
    )(q.arr, k_cmp, overlap)


def _silu_grad(pre):
    sg = _sigmoid(pre)
    return sg * (1.0 + pre * (1.0 - sg))


def _compress_fwd(a_lo, a_hi, pe_lo, pe_hi, w1_lo, w1_hi, w2, name):
    n, dp = a_lo.shape[0], w2.shape[1]

    def kern(alo, ahi, plo, phi, w1l, w1h, w2r, out_ref, pre_ref):
        xl = (alo[...] + plo[...]).astype(BF16)
        xh = (ahi[...] + phi[...]).astype(BF16)
        pre = _nn(xl, w1l[...]) + _nn(xh, w1h[...])
        act = pre * _sigmoid(pre)
        out_ref[...] = _nn(act.astype(BF16), w2r[...]).astype(BF16)
        pre_ref[...] = pre

    return _pcall(kern, name=name,
                  out_shape=[jax.ShapeDtypeStruct((n, dp), BF16), jax.ShapeDtypeStruct((n, dp), F32)],
                  )(a_lo, a_hi, pe_lo, pe_hi, w1_lo, w1_hi, w2)


def _compress_bwd(a_lo, a_hi, pe_lo, pe_hi, w1_lo, w1_hi, w2, pre, pre_sh, dout, dout_sh, name):
    n, ln = a_lo.shape
    dp = w2.shape[1]

    def kern(alo, ahi, plo, phi, w1l, w1h, w2r, pre_ref, presh_ref, do_ref, dosh_ref,
             da_ref, dpl_ref, dph_ref, dw1l_ref, dw1h_ref, dw2_ref):
        prev = pre_ref[...]
        act = prev * _sigmoid(prev)
        dob = do_ref[...].astype(BF16)
        w2v = w2r[...]
        dpre = (_nt(dob, w2v) * _silu_grad(prev)).astype(BF16)
        dpre_sh = (_nt(dosh_ref[...].astype(BF16), w2v) * _silu_grad(presh_ref[...])).astype(BF16)
        dw2_ref[...] = _nn(act.T.astype(BF16), dob)
        xl = alo[...] + plo[...]
        xh = ahi[...] + phi[...]
        dw1l_ref[...] = _nn(xl.T.astype(BF16), dpre)
        dw1h_ref[...] = _nn(xh.T.astype(BF16), dpre)
        dal = _nt(dpre, w1l[...])
        dah_sh = _nt(dpre_sh, w1h[...])
        da_ref[...] = dal + dah_sh
        dpl_ref[...] = jnp.sum(dal, axis=0, keepdims=True)
        dph_ref[...] = jnp.sum(dah_sh, axis=0, keepdims=True)

    return _pcall(
        kern, name=name,
        out_shape=[jax.ShapeDtypeStruct((n, ln), F32), jax.ShapeDtypeStruct((1, ln), F32),
                   jax.ShapeDtypeStruct((1, ln), F32), jax.ShapeDtypeStruct((ln, dp), F32),
                   jax.ShapeDtypeStruct((ln, dp), F32), jax.ShapeDtypeStruct((dp, dp), F32)],
    )(a_lo, a_hi, pe_lo, pe_hi, w1_lo, w1_hi, w2, pre, pre_sh, dout, dout_sh)


def _nsa_combine(o_cmp, o_slc, o_win, gl):
    s, w = o_cmp.shape
    tr = _tile(s, 512)

    def kern(a_ref, b_ref, c_ref, g_ref, o_ref):
        g = _sigmoid(g_ref[...])
        for h in range(NSA_HEADS):
            cs = slice(h * HEAD_V, (h + 1) * HEAD_V)
            o_ref[:, cs] = (g[:, 3 * h:3 * h + 1] * a_ref[:, cs] + g[:, 3 * h + 1:3 * h + 2] * b_ref[:, cs]
                            + g[:, 3 * h + 2:3 * h + 3] * c_ref[:, cs])

    row = pl.BlockSpec((tr, w), lambda i: (i, 0))
    return _pcall(kern, name="nsa_combine", grid=(s // tr,),
                  in_specs=[row, row, row, pl.BlockSpec((tr, LANE), lambda i: (i, gl.col0))], out_specs=row,
                  out_shape=jax.ShapeDtypeStruct((s, w), F32))(o_cmp, o_slc, o_win, gl.arr)


def _nsa_combine_bwd(do_cat, o_cmp, o_slc, o_win, gl):
    s, w = o_cmp.shape
    tr = _tile(s, 512)

    def kern(d_ref, a_ref, b_ref, c_ref, g_ref, da_ref, db_ref, dc_ref, dg_ref):
        g = _sigmoid(g_ref[...])
        lane = lax.broadcasted_iota(jnp.int32, (tr, LANE), 1)
        dgl = jnp.zeros((tr, LANE), F32)
        for h in range(NSA_HEADS):
            cs = slice(h * HEAD_V, (h + 1) * HEAD_V)
            dv = d_ref[:, cs]
            for b, (src, dst) in enumerate(((a_ref, da_ref), (b_ref, db_ref), (c_ref, dc_ref))):
                gate = g[:, 3 * h + b:3 * h + b + 1]
                dst[:, cs] = gate * dv
                dgate = jnp.sum(dv * src[:, cs], axis=1, keepdims=True)
                dgl = jnp.where(lane == 3 * h + b, dgate * gate * (1.0 - gate), dgl)
        dg_ref[...] = dgl

    row = pl.BlockSpec((tr, w), lambda i: (i, 0))
    tab = pl.BlockSpec((tr, LANE), lambda i: (i, 0))
    return _pcall(kern, name="nsa_combine_bwd", grid=(s // tr,),
                  in_specs=[pl.BlockSpec((tr, w), lambda i: (i, 2)), row, row, row,
                            pl.BlockSpec((tr, LANE), lambda i: (i, gl.col0))],
                  out_specs=[row, row, row, tab],
                  out_shape=[jax.ShapeDtypeStruct((s, w), F32)] * 3 + [jax.ShapeDtypeStruct((s, LANE), F32)],
                  )(do_cat, o_cmp, o_slc, o_win, gl.arr)


def _gate_fwd(o_mla, o_nsa, o_mem, hp):
    s = o_mla.shape[0]
    tr = _tile(s, 256)

    def kern(a_ref, b_ref, c_ref, z_ref, u_ref):
        z = z_ref[...]
        sz = z * _sigmoid(z)
        u_ref[:, 0:1024] = (a_ref[...] * sz[:, 0:1024]).astype(BF16)
        u_ref[:, 1024:1536] = (b_ref[...] * sz[:, 1024:1536]).astype(BF16)
        u_ref[:, 1536:2048] = (c_ref[...] * sz[:, 1536:2048]).astype(BF16)

    return _pcall(
        kern, name="gate_fwd", grid=(s // tr,),
        in_specs=[pl.BlockSpec((tr, 1024), lambda i: (i, 0)), pl.BlockSpec((tr, 512), lambda i: (i, 0)),
                  pl.BlockSpec((tr, 512), lambda i: (i, 0)), pl.BlockSpec((tr, 2048), lambda i: (i, 2))],
        out_specs=pl.BlockSpec((tr, 2048), lambda i: (i, 0)),
        out_shape=jax.ShapeDtypeStruct((s, 2048), BF16))(o_mla, o_nsa, o_mem, hp)


def _gate_bwd(du, o_mla, o_nsa, o_mem, hp):
    s = du.shape[0]
    tr = _tile(s, 256)

    def kern(d_ref, a_ref, b_ref, c_ref, z_ref, do_ref, dz_ref):
        z = z_ref[...]
        sg = _sigmoid(z)
        sz = z * sg
        dsz = sg * (1.0 + z * (1.0 - sg))
        d = d_ref[...]
        do_ref[...] = d * sz
        dz_ref[:, 0:1024] = d[:, 0:1024] * a_ref[...] * dsz[:, 0:1024]
        dz_ref[:, 1024:1536] = d[:, 1024:1536] * b_ref[...] * dsz[:, 1024:1536]
        dz_ref[:, 1536:2048] = d[:, 1536:2048] * c_ref[...] * dsz[:, 1536:2048]

    wide = pl.BlockSpec((tr, 2048), lambda i: (i, 0))
    return _pcall(
        kern, name="gate_bwd", grid=(s // tr,),
        in_specs=[wide, pl.BlockSpec((tr, 1024), lambda i: (i, 0)), pl.BlockSpec((tr, 512), lambda i: (i, 0)),
                  pl.BlockSpec((tr, 512), lambda i: (i, 0)), pl.BlockSpec((tr, 2048), lambda i: (i, 2))],
        out_specs=[wide, wide],
        out_shape=[jax.ShapeDtypeStruct((s, 2048), F32)] * 2)(du, o_mla, o_nsa, o_mem, hp)


def _tile2d(rows, cols, arrays):
    if rows % 16 == 0:
        return _row_tile(rows, cols * arrays), cols
    want = max(LANE, BLOCK_BYTES // (rows * 4 * arrays) // LANE * LANE)
    tc = LANE
    for t in range(LANE, cols + 1, LANE):
        if cols % t == 0 and t <= want:
            tc = t
    return rows, tc


def _sum_slots(buf, name):
    n, rows, cols = buf.shape
    tr, tc = _tile2d(rows, cols, n)

    def kern(b_ref, o_ref):
        acc = b_ref[0].astype(F32)
        for i in range(1, n):
            acc = acc + b_ref[i].astype(F32)
        o_ref[...] = acc

    return _pcall(kern, name=name, grid=(rows // tr, cols // tc),
                  in_specs=[pl.BlockSpec((n, tr, tc), lambda i, j: (0, i, j))],
                  out_specs=pl.BlockSpec((tr, tc), lambda i, j: (i, j)),
                  out_shape=jax.ShapeDtypeStruct((rows, cols), F32))(buf)


def _pair_sum(g4, theirs, core, axis, name):
    n, rows, cols = theirs.shape
    tr, tc = _tile2d(rows, cols, 1)
    nbr, nbc = rows // tr, cols // tc

    def kern(c_ref, a_ref, b_ref, o_ref):
        o_ref[...] = (a_ref[...] + b_ref[...]).astype(BF16)

    blk = (1, tr, tc)
    mine = ((lambda s, i, j, c: (s, c[0] * nbr + i, j)) if axis == 0
            else (lambda s, i, j, c: (s, i, c[0] * nbc + j)))
    grid_spec = pltpu.PrefetchScalarGridSpec(
        num_scalar_prefetch=1, grid=(n, nbr, nbc),
        in_specs=[pl.BlockSpec(blk, mine), pl.BlockSpec(blk, lambda s, i, j, c: (s, i, j))],
        out_specs=pl.BlockSpec(blk, lambda s, i, j, c: (s, i, j)))
    return _pcall(kern, name=name, grid_spec=grid_spec,
                  out_shape=jax.ShapeDtypeStruct((n, rows, cols), BF16))(core, g4, theirs)


def _adamw(w, g, m, v, name):
    rows, cols = w.shape
    tr, tc = _tile2d(rows, cols, 4)
    bc1 = 1.0 - ADAM_B1 ** ADAM_STEP
    bc2 = 1.0 - ADAM_B2 ** ADAM_STEP

    def kern(w_ref, g_ref, m_ref, v_ref, d_ref, mo_ref, vo_ref):
        gv = g_ref[...]
        mn = ADAM_B1 * m_ref[...] + (1.0 - ADAM_B1) * gv
        vn = ADAM_B2 * v_ref[...] + (1.0 - ADAM_B2) * (gv * gv)
        d_ref[...] = -ADAM_LR * ((mn / bc1) / (jnp.sqrt(vn / bc2) + ADAM_EPS) + ADAM_WD * w_ref[...])
        mo_ref[...] = mn
        vo_ref[...] = vn

    blk = pl.BlockSpec((tr, tc), lambda i, j: (i, j))
    return _pcall(kern, name=name, grid=(rows // tr, cols // tc), in_specs=[blk] * 4, out_specs=[blk] * 3,
                  out_shape=[jax.ShapeDtypeStruct((rows, cols), F32)] * 3)(w, g, m, v)


ANY = pl.BlockSpec(memory_space=pl.ANY)


def _place():
    x, y, c = lax.axis_index("x"), lax.axis_index("y"), lax.axis_index("c")
    chips = [(1 - x, y), (x, 1 - y), (1 - x, 1 - y)]
    return x, y, c, chips


def _remote(src, dst, send_sem, recv_sem, to):
    return pltpu.make_async_remote_copy(src_ref=src, dst_ref=dst, send_sem=send_sem, recv_sem=recv_sem,
                                        device_id=to, device_id_type=MESH)


def _half(ref, lead, core, axis):
    size = ref.shape[len(lead) + axis] // 2
    cut = pl.ds(core * size, size)
    return ref.at[tuple(lead) + ((cut, slice(None)) if axis == 0 else (slice(None), cut))]


def _gather_shards(ws, axes):
    nw = len(ws)

    def body(*refs):
        w_refs, out_refs = refs[:nw], refs[nw:2 * nw]
        send_sems, recv_sems = refs[2 * nw:]
        x, y, c, chips = _place()
        me = 2 * x + y
        sibling = (x, y, 1 - c)

        def part(i, slot, core):
            return _half(out_refs[i], (slot,), core, axes[i])

        def copy(sem, src, dst, to):
            return _remote(src, dst, send_sems.at[sem], recv_sems.at[sem], to)

        first = [copy(j * nw + i, _half(w_refs[i], (), c, axes[i]), part(i, me, c), (*chip, c))
                 for j, chip in enumerate(chips) for i in range(nw)]
        for cp in first:
            cp.start()
        passed = []
        for j, (cx, cy) in enumerate(chips):
            slot = 2 * cx + cy
            for i in range(nw):
                copy(j * nw + i, part(i, slot, c), part(i, slot, c), (x, y, c)).wait_recv()
                fwd = copy((3 + j) * nw + i, part(i, slot, c), part(i, slot, c), sibling)
                fwd.start()
                passed.append(fwd)
        for j, (cx, cy) in enumerate(chips):
            slot = 2 * cx + cy
            for i in range(nw):
                copy((3 + j) * nw + i, part(i, slot, 1 - c), part(i, slot, 1 - c), (x, y, c)).wait_recv()
        for cp in first + passed:
            cp.wait_send()

    return _pcall(
        body, name="gather_shards", in_specs=[ANY] * nw, out_specs=[ANY] * nw,
        out_shape=[jax.ShapeDtypeStruct((4,) + w.shape, w.dtype) for w in ws],
        scratch_shapes=[pltpu.SemaphoreType.DMA((6 * nw,)), pltpu.SemaphoreType.DMA((6 * nw,))],
    )(*ws)


def _half_shape(shape, axis):
    return tuple(d // 2 if k == len(shape) - 2 + axis else d for k, d in enumerate(shape))


def _pair_exchange(gs, axes):
    nw = len(gs)

    def body(*refs):
        g_refs, out_refs = refs[:nw], refs[nw:2 * nw]
        send_sems, recv_sems = refs[2 * nw:]
        x, y, c, _ = _place()
        cps = []
        for i in range(nw):
            cp = _remote(_half(g_refs[i], (slice(None),), 1 - c, axes[i]), out_refs[i],
                         send_sems.at[i], recv_sems.at[i], (x, y, 1 - c))
            cp.start()
            cps.append(cp)
        for cp in cps:
            cp.wait()

    return _pcall(body, name="pair_exchange", in_specs=[ANY] * nw, out_specs=[ANY] * nw,
                  out_shape=[jax.ShapeDtypeStruct(_half_shape(g.shape, a), g.dtype) for g, a in zip(gs, axes)],
                  scratch_shapes=[pltpu.SemaphoreType.DMA((nw,)), pltpu.SemaphoreType.DMA((nw,))])(*gs)


def _chip_exchange(ps):
    nw = len(ps)

    def body(*refs):
        p_refs, out_refs = refs[:nw], refs[nw:2 * nw]
        send_sems, recv_sems, local_sems = refs[2 * nw:]
        x, y, c, chips = _place()
        me = 2 * x + y
        mine = [pltpu.make_async_copy(p_refs[i].at[me], out_refs[i].at[me], local_sems.at[i]) for i in range(nw)]
        for cp in mine:
            cp.start()
        sends = []
        for j, (cx, cy) in enumerate(chips):
            for i in range(nw):
                cp = _remote(p_refs[i].at[2 * cx + cy], out_refs[i].at[me], send_sems.at[j * nw + i],
                             recv_sems.at[j * nw + i], (cx, cy, c))
                cp.start()
                sends.append(cp)
        for j, (cx, cy) in enumerate(chips):
            slot = 2 * cx + cy
            for i in range(nw):
                _remote(out_refs[i].at[slot], out_refs[i].at[slot], send_sems.at[j * nw + i],
                        recv_sems.at[j * nw + i], (x, y, c)).wait_recv()
        for cp in sends:
            cp.wait_send()
        for cp in mine:
            cp.wait()

    return _pcall(body, name="chip_exchange", in_specs=[ANY] * nw, out_specs=[ANY] * nw,
                  out_shape=[jax.ShapeDtypeStruct(p.shape, p.dtype) for p in ps],
                  scratch_shapes=[pltpu.SemaphoreType.DMA((3 * nw,)), pltpu.SemaphoreType.DMA((3 * nw,)),
                                  pltpu.SemaphoreType.DMA((nw,))])(*ps)


def _half_exchange(ts, axes):
    nw = len(ts)

    def body(*refs):
        t_refs, out_refs = refs[:nw], refs[nw:2 * nw]
        send_sems, recv_sems = refs[2 * nw:]
        x, y, c, _ = _place()
        sends = []
        for i in range(nw):
            cp = _remote(t_refs[i], _half(out_refs[i], (), c, axes[i]), send_sems.at[i], recv_sems.at[i],
                         (x, y, 1 - c))
            cp.start()
            sends.append(cp)
        for i in range(nw):
            _remote(t_refs[i], _half(out_refs[i], (), 1 - c, axes[i]), send_sems.at[i], recv_sems.at[i],
                    (x, y, c)).wait_recv()
        for cp in sends:
            cp.wait_send()

    def whole(t, a):
        return tuple(2 * d if k == a else d for k, d in enumerate(t.shape))

    return _pcall(body, name="half_exchange", in_specs=[ANY] * nw, out_specs=[ANY] * nw,
                  out_shape=[jax.ShapeDtypeStruct(whole(t, a), t.dtype) for t, a in zip(ts, axes)],
                  scratch_shapes=[pltpu.SemaphoreType.DMA((nw,)), pltpu.SemaphoreType.DMA((nw,))])(*ts)


def _gather_all(v):
    rows, cols = v.shape

    def body(v_ref, out_ref, send_sems, recv_sems, local_sem):
        x, y, c, _ = _place()
        me = 4 * x + 2 * y + c
        mine = pltpu.make_async_copy(v_ref, out_ref.at[me], local_sem)
        mine.start()
        sends = []
        for d in range(1, 8):
            peer = (x ^ (d >> 2), y ^ ((d >> 1) & 1), c ^ (d & 1))
            cp = _remote(v_ref, out_ref.at[me], send_sems.at[d - 1], recv_sems.at[d - 1], peer)
            cp.start()
            sends.append(cp)
        for d in range(1, 8):
            slot = 4 * (x ^ (d >> 2)) + 2 * (y ^ ((d >> 1) & 1)) + (c ^ (d & 1))
            _remote(v_ref, out_ref.at[slot], send_sems.at[d - 1], recv_sems.at[d - 1], (x, y, c)).wait_recv()
        for cp in sends:
            cp.wait_send()
        mine.wait()

    return _pcall(body, name="gather_all", in_specs=[ANY], out_specs=ANY,
                  out_shape=jax.ShapeDtypeStruct((8, rows, cols), v.dtype),
                  scratch_shapes=[pltpu.SemaphoreType.DMA((7,)), pltpu.SemaphoreType.DMA((7,)),
                                  pltpu.SemaphoreType.DMA])(v)


def _pad_cols(a, width):
    return a if a.shape[1] == width else jnp.pad(a, ((0, 0), (0, width - a.shape[1])))


def _pad_rows(a, height):
    return a if a.shape[0] == height else jnp.pad(a, ((0, height - a.shape[0]), (0, 0)))


def _w_in_padded(wt):
    def seg(name, height=None):
        o, n = ORIG[name]
        return _pad_rows(wt[o:o + n], height or n)

    qn = wt[ORIG["q_nsa"][0]:ORIG["q_nsa"][0] + 768].reshape(NSA_HEADS, NSA_DK, -1)
    qn = jnp.pad(qn, ((0, 0), (0, 256 - NSA_DK), (0, 0))).reshape(NSA_HEADS * 256, -1)
    kr = seg("k_rope")
    zeros = jnp.zeros((PAD["z"] - (PAD["q_mem"] + 512), wt.shape[1]), wt.dtype)
    return jnp.concatenate(
        [seg("c_q"), seg("c_kv"), qn, seg("k_c", 256), seg("k_s", 256), seg("k_w", 256), kr, kr,
         seg("v_c"), seg("v_s"), seg("v_w"), seg("g_nsa", LANE), seg("q_mem"), zeros,
         seg("z_mla"), seg("z_nsa"), seg("z_mem")], axis=0)


def _w_in_unpadded(gt):
    def seg(name, n):
        return gt[PAD[name]:PAD[name] + n]

    qn = gt[PAD["q_nsa"]:PAD["q_nsa"] + 1024].reshape(NSA_HEADS, 256, -1)[:, :NSA_DK].reshape(768, -1)
    z = PAD["z"]
    return jnp.concatenate(
        [seg("c_q", 512), seg("c_kv", 512), seg("k_rope", 64), gt[z:z + 1024], qn, seg("k_c", 192),
         seg("v_c", 128), seg("k_s", 192), seg("v_s", 128), seg("k_w", 192), seg("v_w", 128),
         seg("g_nsa", 12), gt[z + 1024:z + 1536], seg("q_mem", 512), gt[z + 1536:z + 2048]], axis=0)


def _rope_tables(s):
    pos = jnp.arange(s, dtype=F32)
    inv_freq = ROPE_THETA ** (-jnp.arange(0, 64, 2, dtype=F32) / 64)
    ang = pos[:, None] * inv_freq[None, :]
    cos, sin = jnp.cos(ang), jnp.sin(ang)
    z = jnp.zeros((s, 64), F32)
    return jnp.concatenate([cos, cos, z], axis=1), jnp.concatenate([-sin, sin, z], axis=1)


def _overlap_table(s):
    n_c, n_s = s // CMP_STRIDE, s // SLC_LEN
    c0 = np.arange(n_c)[:, None] * CMP_STRIDE
    s0 = np.arange(LANE)[None, :] * SLC_LEN
    ov = (c0 < s0 + SLC_LEN) & (c0 + CMP_LEN > s0) & (np.arange(n_c)[:, None] < n_c - 1) & (np.arange(LANE)[None, :] < n_s)
    return jnp.asarray(ov.astype(np.float32), dtype=BF16)


def _shift_down(a):
    return jnp.concatenate([jnp.zeros((8, a.shape[1]), a.dtype), a], axis=0)[7:7 + a.shape[0]]


def _shift_up(a):
    return jnp.concatenate([a, jnp.zeros((8, a.shape[1]), a.dtype)], axis=0)[1:1 + a.shape[0]]


def _local_step(x, mem, target, w):
    s = x.shape[0]
    cs, sn = _rope_tables(s)
    t_ = jnp.transpose

    w_in_p = _w_in_padded(w["w_in_t"])
    xn, rstd_x = _rms_fwd(_Src(x, D_MODEL), w["norm_g"], "norm_x")
    hp, hpb = _mm(xn, w_in_p, "in_proj", mode="nt", second_dtype=BF16)

    w_uq3 = w["w_uq"].reshape(512, MLA_HEADS, 192)
    w_uq_p = jnp.concatenate([w_uq3, w_uq3[:, :, 128:]], axis=2).reshape(512, MLA_HEADS * 256)
    w_ukv_p = t_(w["w_ukv"].reshape(512, MLA_HEADS, 2, 128), (0, 2, 1, 3)).reshape(512, 2048)
    c_q, c_kv = _Src(hp, 512, 0), _Src(hp, 512, 1)
    cqn, rstd_q = _rms_fwd(c_q, w["q_norm_g"], "norm_q")
    ckvn, rstd_kv = _rms_fwd(c_kv, w["kv_norm_g"], "norm_kv")
    q_lin = _mm(cqn, w_uq_p, "mla_q_proj")
    kvb = _mm(ckvn, w_ukv_p, "mla_kv_proj", out_dtype=BF16)
    q_mla = _rope_fwd(_Src(q_lin, MLA_HEADS * 256), cs, sn, MLA_HEADS, 256, LANE, "rope_q")
    k_pe = _rope_fwd(_Src(hp, LANE, PAD["k_rope"] // LANE), cs, sn, 1, LANE, 0, "rope_k")
    mla = _Attn("mla", s, s, MLA_HEADS, 256)
    mla_q, mla_v = _Src(q_mla, 256), _Src(kvb, LANE, MLA_HEADS)
    mla_k = [_Src(kvb, LANE), _Src(k_pe, LANE, 0, False)]
    o_mla, l_mla, lr_mla = _attn_fwd(mla, mla_q, mla_k, mla_v, None, "mla_fwd")

    sk = s // CMP_STRIDE
    pe_k, pe_v = w["cmp_pe_k"], w["cmp_pe_v"]
    w1k = _pad_cols(w["cmp_w1k"], 256)
    w2k = jnp.pad(w["cmp_w2k"], ((0, 64), (0, 64))).astype(BF16)
    w1v, w2v = w["cmp_w1v"], w["cmp_w2v"].astype(BF16)
    half_k, half_v = CMP_STRIDE * NSA_DK, CMP_STRIDE * HEAD_V
    ak = hp[:, PAD["k_c"]:PAD["k_c"] + NSA_DK].reshape(sk, half_k)
    av = hp[:, PAD["v_c"]:PAD["v_c"] + HEAD_V].reshape(sk, half_v)
    ck_args = (ak, _shift_up(ak), pe_k[:CMP_STRIDE].reshape(1, half_k), pe_k[CMP_STRIDE:].reshape(1, half_k),
               w1k[:half_k], w1k[half_k:], w2k)
    cv_args = (av, _shift_up(av), pe_v[:CMP_STRIDE].reshape(1, half_v), pe_v[CMP_STRIDE:].reshape(1, half_v),
               w1v[:half_v], w1v[half_v:], w2v)
    k_cmp, pre_k = _compress_fwd(*ck_args, "compress_k")
    v_cmp, pre_v = _compress_fwd(*cv_args, "compress_v")
    cmp_ = _Attn("cmp", s, sk, NSA_HEADS, 256)
    slc = _Attn("slc", s, s, NSA_HEADS, 256)
    win = _Attn("win", s, s, NSA_HEADS, 256)
    nsa_q = _Src(hpb, 256, PAD["q_nsa"] // 256)
    cmp_k, cmp_v = [_Src(k_cmp, 256, 0, False)], _Src(v_cmp, HEAD_V, 0, False)
    slc_k, slc_v = [_Src(hpb, 256, PAD["k_s"] // 256, False)], _Src(hpb, HEAD_V, PAD["v_s"] // HEAD_V, False)
    win_k, win_v = [_Src(hpb, 256, PAD["k_w"] // 256, False)], _Src(hpb, HEAD_V, PAD["v_w"] // HEAD_V, False)
    o_cmp, l_cmp, lr_cmp = _attn_fwd(cmp_, nsa_q, cmp_k, cmp_v, None, "cmp_fwd")
    sel, selt = _select(cmp_, _Src(hpb, NSA_HEADS * 256, PAD["q_nsa"] // (NSA_HEADS * 256)), k_cmp,
                        _overlap_table(s))
    o_slc, l_slc, lr_slc = _attn_fwd(slc, nsa_q, slc_k, slc_v, sel, "slc_fwd")
    o_win, l_win, lr_win = _attn_fwd(win, nsa_q, win_k, win_v, None, "win_fwd")
    gl = _Src(hp, LANE, PAD["g_nsa"] // LANE)
    o_nsa = _nsa_combine(o_cmp, o_slc, o_win, gl)

    mn, rstd_m = _rms_fwd(_Src(mem, D_MODEL), w["mem_norm_g"], "norm_mem")
    kvm = _mm(mn, w["w_mem_kv"], "mem_kv_proj", out_dtype=BF16)
    mem_ = _Attn("mem", s, mem.shape[0], MEM_HEADS, LANE)
    mem_q, mem_k, mem_v = _Src(hpb, LANE, PAD["q_mem"] // LANE), [_Src(kvm, LANE)], _Src(kvm, LANE, MEM_HEADS)
    o_mem, l_mem, lr_mem = _attn_fwd(mem_, mem_q, mem_k, mem_v, None, "mem_fwd")

    u = _gate_fwd(o_mla, o_nsa, o_mem, hp)
    proj = _mm(u, w["w_out"], "out_proj")
    dy, g_final, loss = _final_loss(x, proj, w["final_norm_g"].reshape(1, -1), target)

    g_w_out = _mm(u, dy, "out_proj_dw", mode="tn")
    du = _mm(dy, w["w_out"], "out_proj_dx", mode="nt")
    do_cat, dz = _gate_bwd(du, o_mla, o_nsa, o_mem, hp)

    dq_mem, (dk_mem,), dv_mem = _attn_bwd(mem_, mem_q, mem_k, mem_v, None, None, _Src(o_mem, HEAD_V), l_mem,
                                          lr_mem, _Src(do_cat, HEAD_V, 12), None, "mem_bwd")
    dkvm = jnp.concatenate([dk_mem, dv_mem], axis=1)
    g_w_mem_kv = _mm(mn, dkvm, "mem_kv_dw", mode="tn")
    dmn = _mm(dkvm, w["w_mem_kv"], "mem_kv_dx", mode="nt")
    _, g_mem_norm = _rms_bwd(_Src(mem, D_MODEL), w["mem_norm_g"], rstd_m, dmn, None, "norm_mem_bwd")

    do_cmp, do_slc, do_win, dgl = _nsa_combine_bwd(do_cat, o_cmp, o_slc, o_win, gl)
    dq_n, (dk_cmp,), dv_cmp = _attn_bwd(cmp_, nsa_q, cmp_k, cmp_v, None, None, _Src(o_cmp, HEAD_V), l_cmp,
                                        lr_cmp, _Src(do_cmp, HEAD_V), None, "cmp_bwd")
    dq_n, (dk_s,), dv_s = _attn_bwd(slc, nsa_q, slc_k, slc_v, sel, selt, _Src(o_slc, HEAD_V), l_slc, lr_slc,
                                    _Src(do_slc, HEAD_V), dq_n, "slc_bwd")
    dq_n, (dk_w,), dv_w = _attn_bwd(win, nsa_q, win_k, win_v, None, None, _Src(o_win, HEAD_V), l_win, lr_win,
                                    _Src(do_win, HEAD_V), dq_n, "win_bwd")
    dak, dpk_lo, dpk_hi, dw1k_lo, dw1k_hi, g_w2k = _compress_bwd(
        *ck_args, pre_k, _shift_down(pre_k), dk_cmp, _shift_down(dk_cmp), "compress_k_bwd")
    dav, dpv_lo, dpv_hi, dw1v_lo, dw1v_hi, g_w2v = _compress_bwd(
        *cv_args, pre_v, _shift_down(pre_v), dv_cmp, _shift_down(dv_cmp), "compress_v_bwd")
    g_pe_k = jnp.concatenate([dpk_lo.reshape(CMP_STRIDE, NSA_DK), dpk_hi.reshape(CMP_STRIDE, NSA_DK)], axis=0)
    g_pe_v = jnp.concatenate([dpv_lo.reshape(CMP_STRIDE, HEAD_V), dpv_hi.reshape(CMP_STRIDE, HEAD_V)], axis=0)
    g_w1k = jnp.concatenate([dw1k_lo, dw1k_hi], axis=0)[:, :NSA_DK]
    g_w1v = jnp.concatenate([dw1v_lo, dw1v_hi], axis=0)
    dk_c = _pad_cols(dak.reshape(s, NSA_DK), 256)
    dv_c = dav.reshape(s, HEAD_V)

    dq_m, (dk_nope, dk_pe), dv_m = _attn_bwd(mla, mla_q, mla_k, mla_v, None, None, _Src(o_mla, HEAD_V), l_mla,
                                             lr_mla, _Src(do_cat, HEAD_V), None, "mla_bwd")
    dq_lin = _rope_bwd_q(dq_m, cs, sn)
    dkv_lin, d_krope = _rope_bwd_k(dk_nope, dk_pe, dv_m, cs, sn)
    g_w_uq_p = _mm(cqn, dq_lin, "mla_q_dw", mode="tn")
    dcqn = _mm(dq_lin, w_uq_p, "mla_q_dx", mode="nt")
    g_w_ukv_p = _mm(ckvn, dkv_lin, "mla_kv_dw", mode="tn")
    dckvn = _mm(dkv_lin, w_ukv_p, "mla_kv_dx", mode="nt")
    dc_q, g_q_norm = _rms_bwd(c_q, w["q_norm_g"], rstd_q, dcqn, None, "norm_q_bwd")
    dc_kv, g_kv_norm = _rms_bwd(c_kv, w["kv_norm_g"], rstd_kv, dckvn, None, "norm_kv_bwd")
    g_w_uq = g_w_uq_p.reshape(512, MLA_HEADS, 256)[:, :, :192].reshape(512, MLA_HEADS * 192)
    g_w_ukv = t_(g_w_ukv_p.reshape(512, 2, MLA_HEADS, 128), (0, 2, 1, 3)).reshape(512, 2048)

    dhp = jnp.concatenate(
        [dc_q, dc_kv, dq_n, dk_c, dk_s, dk_w, d_krope, dv_c, dv_s, dv_w, dgl, dq_mem,
         jnp.zeros((s, PAD["z"] - (PAD["q_mem"] + 512)), F32), dz], axis=1)
    g_w_in_t = _w_in_unpadded(_mm(dhp, xn, "in_proj_dw", mode="tn"))
    dxn = _mm(dhp, w_in_p, "in_proj_dx")
    grad_x, g_norm = _rms_bwd(_Src(x, D_MODEL), w["norm_g"], rstd_x, dxn, dy, "norm_x_bwd")

    grads = dict(norm_g=g_norm, w_in_t=g_w_in_t, q_norm_g=g_q_norm, w_uq=g_w_uq, kv_norm_g=g_kv_norm,
                 w_ukv=g_w_ukv, cmp_pe_k=g_pe_k, cmp_pe_v=g_pe_v, cmp_w1k=g_w1k, cmp_w2k=g_w2k[:NSA_DK, :NSA_DK],
                 cmp_w1v=g_w1v, cmp_w2v=g_w2v, mem_norm_g=g_mem_norm, w_mem_kv=g_w_mem_kv, w_out=g_w_out,
                 final_norm_g=g_final.reshape(-1))
    return loss[0, 0], grad_x, grads


def kernel(x, mem, norm_g, w_in, q_norm_g, w_uq, kv_norm_g, w_ukv, cmp_pe_k, cmp_pe_v, cmp_w1k, cmp_w2k, cmp_w1v, cmp_w2v, mem_norm_g, w_mem_kv, w_out, final_norm_g, loss_target, m_norm_g, m_w_in, m_q_norm_g, m_w_uq, m_kv_norm_g, m_w_ukv, m_cmp_pe_k, m_cmp_pe_v, m_cmp_w1k, m_cmp_w2k, m_cmp_w1v, m_cmp_w2v, m_mem_norm_g, m_w_mem_kv, m_w_out, m_final_norm_g, v_norm_g, v_w_in, v_q_norm_g, v_w_uq, v_kv_norm_g, v_w_ukv, v_cmp_pe_k, v_cmp_pe_v, v_cmp_w1k, v_cmp_w2k, v_cmp_w1v, v_cmp_w2v, v_mem_norm_g, v_w_mem_kv, v_w_out, v_final_norm_g):
    args = dict(locals())
    wts = {n: args[n] for n in WEIGHTS}
    loc = {n: (a if n == "final_norm_g" else a[0]) for n, a in wts.items()}

    def to_x(n, a):
        return a.T if n == "w_in" else a

    split = [1 if n == "w_in" else 0 for n in SHARDED]

    own = [to_x(n, loc[n]).astype(BF16) for n in SHARDED]
    chip = 2 * lax.axis_index("x") + lax.axis_index("y")
    gathered = [lax.dynamic_update_slice(gw, a[None], (chip, 0, 0))
                for gw, a in zip(_gather_shards(own, split), own)]
    full = {n: loc[n].reshape(1, -1) if loc[n].ndim == 1 else loc[n] for n in REPLICATED}
    for n, gw in zip(SHARDED, gathered):
        if n == "w_in":
            full["w_in_t"] = gw.reshape(4 * gw.shape[1], gw.shape[2])
        elif SHARD_AXIS[n] == 0:
            full[n] = gw.reshape(4 * gw.shape[1], gw.shape[2])
        else:
            full[n] = jnp.concatenate([gw[j] for j in range(4)], axis=1)

    loss, grad_x, g = _local_step(x[0], mem[0], loss_target[0], full)
    loss = lax.psum(loss, ("x", "y", "c"))

    def slots(n):
        a = g["w_in_t"] if n == "w_in" else g[n]
        if n == "w_in" or SHARD_AXIS[n] == 0:
            return a.reshape(4, a.shape[0] // 4, a.shape[1])
        width = a.shape[1] // 4
        return jnp.stack([a[:, j * width:(j + 1) * width] for j in range(4)])

    gs = [slots(n) for n in SHARDED]
    core = lax.axis_index("c").astype(jnp.int32).reshape(1)
    theirs = _pair_exchange(gs, split)
    pairs = [_pair_sum(a, b, core, ax, "pair_sum_" + n) for n, a, b, ax in zip(SHARDED, gs, theirs, split)]
    from_chips = _chip_exchange(pairs)
    mine = [_sum_slots(b, "chip_sum_" + n) for n, b in zip(SHARDED, from_chips)]
    g_sh = [lax.dynamic_update_slice(o, t, (core[0] * t.shape[0], 0) if ax == 0 else (0, core[0] * t.shape[1]))
            for o, t, ax in zip(_half_exchange(mine, split), mine, split)]

    n_rep = sum(int(np.prod(loc[n].shape)) for n in REPLICATED)
    rows_rep = -(-n_rep // (8 * LANE)) * 8

    def rep_pack(parts):
        flat = jnp.concatenate([p.reshape(-1) for p in parts])
        return jnp.pad(flat, (0, rows_rep * LANE - n_rep)).reshape(rows_rep, LANE)

    g_rep = _sum_slots(_gather_all(rep_pack([g[n] for n in REPLICATED])), "replica_sum")
    d_rp, m_rp, v_rp = _adamw(rep_pack([wts[n] for n in REPLICATED]), g_rep,
                              rep_pack([args["m_" + n] for n in REPLICATED]),
                              rep_pack([args["v_" + n] for n in REPLICATED]), "adamw_replicated")

    def rep_unpack(buf):
        flat, out, o = buf.reshape(-1), {}, 0
        for n in REPLICATED:
            size = int(np.prod(wts[n].shape))
            out[n] = flat[o:o + size].reshape(wts[n].shape)
            o += size
        return out

    outs = {k: rep_unpack(b) for k, b in (("g", g_rep), ("d", d_rp), ("m", m_rp), ("v", v_rp))}
    for n, gn in zip(SHARDED, g_sh):
        d, mo, vo = _adamw(to_x(n, loc[n]), gn, to_x(n, args["m_" + n][0]), to_x(n, args["v_" + n][0]),
                           "adamw_" + n)
        for k, a in (("g", gn), ("d", d), ("m", mo), ("v", vo)):
            outs[k][n] = to_x(n, a).reshape(wts[n].shape)

    return (loss, grad_x[None], *[outs["g"][n] for n in WEIGHTS], *[outs["d"][n] for n in WEIGHTS],
            *[outs["m"][n] for n in WEIGHTS], *[outs["v"][n] for n in WEIGHTS])
```

```python
from typing import NamedTuple

import numpy as np
import jax
import jax.numpy as jnp
from jax import lax
from jax.experimental import pallas as pl
from jax.experimental.pallas import tpu as pltpu

F32 = jnp.float32
BF16 = jnp.bfloat16
MESH = pl.DeviceIdType.MESH

D_MODEL = 2048
EPS = 1e-6
LANE = 128
HEAD_V = 128
MLA_HEADS = 8
NSA_HEADS = 4
MEM_HEADS = 4
NSA_DK = 192
CMP_STRIDE = 16
CMP_LEN = 32
SLC_LEN = 64
SLC_TOPN = 16
WIN = 512
NEG = -1e30
LOG2E = 1.4426950408889634
ROPE_THETA = 10000.0
BLOCK_BYTES = 2 << 20

ORIG = dict(c_q=(0, 512), c_kv=(512, 512), k_rope=(1024, 64), z_mla=(1088, 1024),
            q_nsa=(2112, 768), k_c=(2880, 192), v_c=(3072, 128), k_s=(3200, 192),
            v_s=(3392, 128), k_w=(3520, 192), v_w=(3712, 128), g_nsa=(3840, 12),
            z_nsa=(3852, 512), q_mem=(4364, 512), z_mem=(4876, 512))
PAD = dict(c_q=0, c_kv=512, q_nsa=1024, k_c=2048, k_s=2304, k_w=2560, k_rope=2816, v_c=2944,
           v_s=3072, v_w=3200, g_nsa=3328, q_mem=3456, z=4096)
D_PAD = 6144

ADAM_LR, ADAM_B1, ADAM_B2, ADAM_EPS, ADAM_WD, ADAM_STEP = 0.001, 0.9, 0.999, 1e-08, 0.01, 10

SHARDED = ("w_in", "w_uq", "w_ukv", "cmp_w1k", "cmp_w1v", "w_mem_kv", "w_out")
SHARD_AXIS = dict(w_in=1, w_uq=1, w_ukv=1, cmp_w1k=0, cmp_w1v=0, w_mem_kv=0, w_out=0)
REPLICATED = ("norm_g", "q_norm_g", "kv_norm_g", "cmp_pe_k", "cmp_pe_v", "cmp_w2k", "cmp_w2v",
              "mem_norm_g", "final_norm_g")
WEIGHTS = ("norm_g", "w_in", "q_norm_g", "w_uq", "kv_norm_g", "w_ukv", "cmp_pe_k", "cmp_pe_v",
           "cmp_w1k", "cmp_w2k", "cmp_w1v", "cmp_w2v", "mem_norm_g", "w_mem_kv", "w_out",
           "final_norm_g")


def _pcall(kernel, **kw):
    return pl.pallas_call(kernel, **kw)


def _tile(n, pref):
    if n <= pref:
        return n
    for t in range(pref, LANE - 1, -LANE):
        if n % t == 0:
            return t
    raise ValueError((n, pref))


def _row_tile(rows, cols, itemsize=4):
    want = max(16, BLOCK_BYTES // (cols * itemsize))
    if rows <= want:
        return rows
    t = 16
    best = rows
    while t <= want:
        if rows % t == 0:
            best = t
        t *= 2
    return best


def _nt(a, b):
    return lax.dot_general(a, b, (((1,), (1,)), ((), ())), preferred_element_type=F32)


def _tn(a, b):
    return lax.dot_general(a, b, (((0,), (0,)), ((), ())), preferred_element_type=F32)


def _nn(a, b):
    return jnp.dot(a, b, preferred_element_type=F32)


def _sigmoid(x):
    return 1.0 / (1.0 + jnp.exp(-x))


class _Src(NamedTuple):
    arr: jax.Array
    width: int
    col0: int = 0
    per_head: bool = True

    def col(self, h):
        return self.col0 + h if self.per_head else self.col0


def _mm(a, b, name, mode="nn", out_dtype=F32, second_dtype=None):
    if mode == "tn":
        k, m = a.shape
    else:
        m, k = a.shape
    if mode == "nt":
        n, k2 = b.shape
    else:
        k2, n = b.shape
    assert k == k2, (a.shape, b.shape, mode)
    tm, tn, tk = _tile(m, 1024), _tile(n, 1024), _tile(k, 2048)
    nk = k // tk
    assert nk == 1 or (out_dtype == F32 and second_dtype is None)
    dot = {"nn": _nn, "nt": _nt, "tn": _tn}[mode]

    def kern(a_ref, b_ref, o_ref, *more):
        r = dot(a_ref[...].astype(BF16), b_ref[...].astype(BF16))
        if nk == 1:
            o_ref[...] = r.astype(out_dtype)
            if more:
                more[0][...] = r.astype(second_dtype)
        else:
            kk = pl.program_id(2)

            @pl.when(kk == 0)
            def _():
                o_ref[...] = r

            @pl.when(kk > 0)
            def _():
                o_ref[...] += r

    a_spec = (pl.BlockSpec((tk, tm), lambda i, j, kk: (kk, i)) if mode == "tn"
              else pl.BlockSpec((tm, tk), lambda i, j, kk: (i, kk)))
    b_spec = (pl.BlockSpec((tn, tk), lambda i, j, kk: (j, kk)) if mode == "nt"
              else pl.BlockSpec((tk, tn), lambda i, j, kk: (kk, j)))
    o_spec = pl.BlockSpec((tm, tn), lambda i, j, kk: (i, j))
    out_shape = jax.ShapeDtypeStruct((m, n), out_dtype)
    if second_dtype is not None:
        o_spec = [o_spec, o_spec]
        out_shape = [out_shape, jax.ShapeDtypeStruct((m, n), second_dtype)]
    return _pcall(
        kern, name=name, grid=(m // tm, n // tn, nk), in_specs=[a_spec, b_spec], out_specs=o_spec,
        out_shape=out_shape,
        compiler_params=pltpu.CompilerParams(dimension_semantics=("parallel", "parallel", "arbitrary")),
    )(a, b)


def _rms_fwd(x, g, name):
    r, d = x.arr.shape[0], x.width
    tr = _tile(r, 512)

    def kern(x_ref, g_ref, y_ref, r_ref):
        xv = x_ref[...]
        rstd = lax.rsqrt(jnp.mean(xv * xv, axis=-1, keepdims=True) + EPS)
        y_ref[...] = (xv * rstd * g_ref[...]).astype(BF16)
        r_ref[...] = rstd

    return _pcall(
        kern, name=name, grid=(r // tr,),
        in_specs=[pl.BlockSpec((tr, d), lambda i: (i, x.col0)), pl.BlockSpec((1, d), lambda i: (0, 0))],
        out_specs=[pl.BlockSpec((tr, d), lambda i: (i, 0)), pl.BlockSpec((tr, 1), lambda i: (i, 0))],
        out_shape=[jax.ShapeDtypeStruct((r, d), BF16), jax.ShapeDtypeStruct((r, 1), F32)],
    )(x.arr, g)


def _rms_bwd(x, g, rstd, dy, add, name):
    r, d = x.arr.shape[0], x.width
    tr = _tile(r, 256)
    has_add = add is not None

    def kern(*refs):
        if has_add:
            x_ref, g_ref, r_ref, dy_ref, add_ref, dx_ref, dg_ref = refs
        else:
            x_ref, g_ref, r_ref, dy_ref, dx_ref, dg_ref = refs
        rs = r_ref[...]
        xhat = x_ref[...] * rs
        dyv = dy_ref[...]
        dyg = dyv * g_ref[...]
        c = jnp.mean(dyg * xhat, axis=-1, keepdims=True)
        dx = rs * (dyg - xhat * c)
        if has_add:
            dx = dx + add_ref[...]
        dx_ref[...] = dx
        part = jnp.sum(dyv * xhat, axis=0, keepdims=True)

        @pl.when(pl.program_id(0) == 0)
        def _():
            dg_ref[...] = part

        @pl.when(pl.program_id(0) > 0)
        def _():
            dg_ref[...] += part

    row = pl.BlockSpec((tr, d), lambda i: (i, 0))
    vec = pl.BlockSpec((1, d), lambda i: (0, 0))
    ins = [pl.BlockSpec((tr, d), lambda i: (i, x.col0)), vec, pl.BlockSpec((tr, 1), lambda i: (i, 0)), row]
    ins += [row] if has_add else []
    args = (x.arr, g, rstd, dy) + ((add,) if has_add else ())
    return _pcall(
        kern, name=name, grid=(r // tr,), in_specs=ins, out_specs=[row, vec],
        out_shape=[jax.ShapeDtypeStruct((r, d), F32), jax.ShapeDtypeStruct((1, d), F32)],
        compiler_params=pltpu.CompilerParams(dimension_semantics=("arbitrary",)),
    )(*args)


def _final_loss(x, proj, g, target):
    r, d = x.shape
    tr = _tile(r, 256)

    def kern(x_ref, p_ref, g_ref, t_ref, dy_ref, dg_ref, loss_ref):
        y = x_ref[...] + p_ref[...]
        rs = lax.rsqrt(jnp.mean(y * y, axis=-1, keepdims=True) + EPS)
        yhat = y * rs
        gv = g_ref[...]
        e = yhat * gv - t_ref[...]
        lpart = 0.5 * jnp.sum(jnp.mean(e * e, axis=-1, keepdims=True), axis=0, keepdims=True)
        dout = e * (1.0 / d)
        dyg = dout * gv
        c = jnp.mean(dyg * yhat, axis=-1, keepdims=True)
        dy_ref[...] = rs * (dyg - yhat * c)
        gpart = jnp.sum(dout * yhat, axis=0, keepdims=True)
        lrow = jnp.broadcast_to(lpart, (1, LANE))

        @pl.when(pl.program_id(0) == 0)
        def _():
            dg_ref[...] = gpart
            loss_ref[...] = lrow

        @pl.when(pl.program_id(0) > 0)
        def _():
            dg_ref[...] += gpart
            loss_ref[...] += lrow

    row = pl.BlockSpec((tr, d), lambda i: (i, 0))
    vec = pl.BlockSpec((1, d), lambda i: (0, 0))
    return _pcall(
        kern, name="final_loss", grid=(r // tr,), in_specs=[row, row, vec, row],
        out_specs=[row, vec, pl.BlockSpec((1, LANE), lambda i: (0, 0))],
        out_shape=[jax.ShapeDtypeStruct((r, d), F32), jax.ShapeDtypeStruct((1, d), F32),
                   jax.ShapeDtypeStruct((1, LANE), F32)],
        compiler_params=pltpu.CompilerParams(dimension_semantics=("arbitrary",)),
    )(x, proj, g, target)


def _rope_fwd(x, cs, sn, nh, width, off, name):
    s = x.arr.shape[0]
    tr = _tile(s, 512)

    def kern(x_ref, c_ref, s_ref, o_ref):
        cv, sv = c_ref[...], s_ref[...]
        for h in range(nh):
            b = h * width
            if off:
                o_ref[:, b:b + off] = x_ref[:, b:b + off].astype(BF16)
            xr = x_ref[:, b + off:b + off + LANE]
            o_ref[:, b + off:b + off + LANE] = (xr * cv + pltpu.roll(xr, 32, 1) * sv).astype(BF16)

    tab = pl.BlockSpec((tr, LANE), lambda i: (i, 0))
    return _pcall(
        kern, name=name, grid=(s // tr,),
        in_specs=[pl.BlockSpec((tr, nh * width), lambda i: (i, x.col0)), tab, tab],
        out_specs=pl.BlockSpec((tr, nh * width), lambda i: (i, 0)),
        out_shape=jax.ShapeDtypeStruct((s, nh * width), BF16),
    )(x.arr, cs, sn)


def _rope_grad(d, cv, sv):
    g2 = d * sv
    g2 = g2 + pltpu.roll(g2, 64, 1)
    lane = lax.broadcasted_iota(jnp.int32, d.shape, 1)
    return jnp.where(lane < 64, d * cv + pltpu.roll(g2, 32, 1), 0.0)


def _rope_bwd_q(dq, cs, sn):
    s, w = dq.shape
    tr = _tile(s, 512)
    nh = w // 256

    def kern(d_ref, c_ref, s_ref, o_ref):
        cv, sv = c_ref[...], s_ref[...]
        for h in range(nh):
            b = h * 256
            o_ref[:, b:b + LANE] = d_ref[:, b:b + LANE]
            o_ref[:, b + LANE:b + 256] = _rope_grad(d_ref[:, b + LANE:b + 256], cv, sv)

    row = pl.BlockSpec((tr, w), lambda i: (i, 0))
    tab = pl.BlockSpec((tr, LANE), lambda i: (i, 0))
    return _pcall(kern, name="rope_bwd_q", grid=(s // tr,), in_specs=[row, tab, tab], out_specs=row,
                  out_shape=jax.ShapeDtypeStruct((s, w), F32))(dq, cs, sn)


def _rope_bwd_k(dk_nope, dk_pe, dv, cs, sn):
    s, w = dk_nope.shape
    tr = _tile(s, 512)

    def kern(dk_ref, dp_ref, dv_ref, c_ref, s_ref, okv_ref, okr_ref):
        okv_ref[:, :w] = dk_ref[...]
        okv_ref[:, w:] = dv_ref[...]
        okr_ref[...] = _rope_grad(dp_ref[...], c_ref[...], s_ref[...])

    tab = pl.BlockSpec((tr, LANE), lambda i: (i, 0))
    wide = pl.BlockSpec((tr, w), lambda i: (i, 0))
    return _pcall(
        kern, name="rope_bwd_k", grid=(s // tr,), in_specs=[wide, tab, wide, tab, tab],
        out_specs=[pl.BlockSpec((tr, 2 * w), lambda i: (i, 0)), tab],
        out_shape=[jax.ShapeDtypeStruct((s, 2 * w), F32), jax.ShapeDtypeStruct((s, LANE), F32)],
    )(dk_nope, dk_pe, dv, cs, sn)


class _Attn:
    def __init__(self, mode, s, sk, heads, dk):
        self.mode, self.s, self.sk, self.h, self.dk = mode, s, sk, heads, dk
        self.scale = {"mla": 192 ** -0.5, "mem": 128 ** -0.5}.get(mode, NSA_DK ** -0.5)
        self.tb = min(256, s)
        self.nb = s // self.tb
        self.nsub = 2 if self.nb % 2 == 0 else 1
        self.tq = self.tb * self.nsub
        self.nq = s // self.tq
        self.causal = mode in ("mla", "slc")
        if self.causal:
            self.tk = self.tq
        elif mode == "win":
            self.tk = WIN + self.tb
        else:
            self.tk = sk
        self.tkb = min(512, sk)
        self.ksub = 2 if self.tkb == 512 and mode == "mla" else 1
        self.kb = self.tkb // self.ksub
        self.ncmp = s // CMP_STRIDE - 1

    def mask_bias(self, t, n, h, selx, diag):
        m = self.mode
        if m == "mla":
            return (n <= t) if diag else None, None
        if m == "mem":
            return None, None
        slope = jnp.where(h == 0, 0.25, jnp.where(h == 1, 0.0625, jnp.where(h == 2, 0.015625, 0.00390625)))
        slope = slope.astype(F32) * LOG2E
        if m == "cmp":
            mask = (n * CMP_STRIDE + (CMP_LEN - 1) <= t) & (n < self.ncmp)
            pos = n.astype(F32) * float(CMP_STRIDE) + (CMP_LEN - 1) / 2.0
            return mask, slope * pos
        rel = t - n
        if m == "slc":
            mask = selx > 0.5
            return (mask & (rel >= 0)) if diag else mask, slope * n.astype(F32)
        return (rel >= 0) & (rel < WIN), slope * n.astype(F32)


def _scores(cfg, s_raw, t, n, h, selx, diag, lse=None):
    s = s_raw * (cfg.scale * LOG2E)
    mask, key_term = cfg.mask_bias(t, n, h, selx, diag)
    if key_term is not None:
        s = s + key_term
    if lse is None:
        if mask is not None:
            s = jnp.where(mask, s, NEG)
        return s, mask
    p = jnp.exp2(jnp.minimum(s - lse, 0.0))
    if mask is not None:
        p = jnp.where(mask, p, 0.0)
    return p, mask


def _block_of_key(k0, tk, keys_on_rows):
    shape = (tk, LANE) if keys_on_rows else (LANE, tk)
    n = lax.broadcasted_iota(jnp.int32, shape, 0 if keys_on_rows else 1) + k0
    j = lax.broadcasted_iota(jnp.int32, shape, 1 if keys_on_rows else 0)
    return jnp.where((n >> 6) == j, 1.0, 0.0).astype(BF16)


def _to_row(col):
    t = col.shape[0]
    return jnp.transpose(jnp.broadcast_to(col, (t, LANE)))[0:1, :]


def _load_keys(k_refs, rows):
    parts = [r[rows, :].astype(BF16) for r in k_refs]
    return parts[0] if len(parts) == 1 else jnp.concatenate(parts, axis=1)


def _attn_fwd(cfg, q, ks, v, sel, name):
    s, tq, tk, tb, nsub = cfg.s, cfg.tq, cfg.tk, cfg.tb, cfg.nsub
    has_sel = sel is not None
    nkp = len(ks)

    def kern(*refs):
        q_ref, k_refs, v_ref = refs[0], refs[1:1 + nkp], refs[1 + nkp]
        sel_ref = refs[2 + nkp] if has_sel else None
        o_ref, lc_ref, lr_ref = refs[2 + nkp + has_sel:5 + nkp + has_sel]
        h, i = pl.program_id(0), pl.program_id(1)
        part = [slice(r * tb, (r + 1) * tb) for r in range(nsub)]
        qs = [q_ref[p, :].astype(BF16) for p in part]
        ts = [i * tq + r * tb + lax.broadcasted_iota(jnp.int32, (tb, 1), 0) for r in range(nsub)]
        sels = [sel_ref[p, :].astype(BF16) for p in part] if has_sel else None

        def load(k0):
            rows = pl.ds(k0, tk)
            return _load_keys(k_refs, rows), v_ref[rows, :].astype(BF16)

        def soft(r, k0, s_raw, vv, emat, carry, diag):
            m, l, acc = carry
            n = k0 + lax.broadcasted_iota(jnp.int32, (1, tk), 1)
            selx = _nn(sels[r], emat) if has_sel else None
            sc, mask = _scores(cfg, s_raw, ts[r], n, h, selx, diag)
            m_new = jnp.maximum(m, jnp.max(sc, axis=1, keepdims=True))
            alpha = jnp.exp2(m - m_new)
            p = jnp.exp2(sc - m_new)
            if mask is not None:
                p = jnp.where(mask, p, 0.0)
            l = alpha * l + jnp.sum(p, axis=1, keepdims=True)
            acc = alpha * acc + _nn(p.astype(BF16), vv)
            return m_new, l, acc

        def step(r, k0, kk, vv, emat, carry, diag):
            return soft(r, k0, _nt(qs[r], kk), vv, emat, carry, diag)

        def chunk(k0, carry, diag):
            kk, vv = load(k0)
            emat = _block_of_key(k0, tk, False) if has_sel else None
            return tuple(step(r, k0, kk, vv, emat, carry[r], diag) for r in range(nsub))

        init = (jnp.full((tb, 1), NEG, F32), jnp.zeros((tb, 1), F32), jnp.zeros((tb, HEAD_V), F32))
        carry = (init,) * nsub
        if cfg.causal:
            buf_a, buf_b = refs[-2:]
            full = (i * tq) // tk

            def scores_into(buf, c):
                kk = _load_keys(k_refs, pl.ds(pl.multiple_of(c * tk, tk), tk))
                for r in range(nsub):
                    buf[r] = _nt(qs[r], kk)

            def consume(buf, c, cr, diag):
                k0 = pl.multiple_of(c * tk, tk)
                vv = v_ref[pl.ds(k0, tk), :].astype(BF16)
                emat = _block_of_key(k0, tk, False) if has_sel else None
                return tuple(soft(r, k0, buf[r], vv, emat, cr[r], diag) for r in range(nsub))

            def pair(p, cr):
                scores_into(buf_b, 2 * p + 1)
                cr = consume(buf_a, 2 * p, cr, False)
                scores_into(buf_a, 2 * p + 2)
                return consume(buf_b, 2 * p + 1, cr, False)

            def odd_tail(cr):
                scores_into(buf_b, full)
                cr = consume(buf_a, full - 1, cr, False)
                return consume(buf_b, full, cr, True)

            scores_into(buf_a, 0)
            carry = lax.fori_loop(0, full // 2, pair, carry)
            carry = lax.cond(full % 2 == 1, odd_tail, lambda cr: consume(buf_a, full, cr, True), carry)
        elif cfg.mode == "win":
            starts = [pl.multiple_of(jnp.maximum(i * tq + r * tb - WIN, 0), tb) for r in range(nsub)]
            carry = tuple(step(r, k0, *load(k0), None, carry[r], True) for r, k0 in enumerate(starts))
        else:
            carry = chunk(0, carry, True)
        for r, (m, l, acc) in enumerate(carry):
            o_ref[part[r], :] = acc / (l + 1e-20)
            lse = m + jnp.log(l + 1e-20) * LOG2E
            lc_ref[0, part[r], :] = lse
            lr_ref[0, r] = _to_row(lse)

    ins = [pl.BlockSpec((tq, q.width), lambda h, i: (i, q.col(h)))]
    ins += [pl.BlockSpec((cfg.sk, p.width), lambda h, i, p=p: (0, p.col(h))) for p in ks]
    ins += [pl.BlockSpec((cfg.sk, HEAD_V), lambda h, i: (0, v.col(h)))]
    args = [q.arr] + [p.arr for p in ks] + [v.arr]
    if has_sel:
        ins.append(pl.BlockSpec((tq, LANE), lambda h, i: (i, 0)))
        args.append(sel)
    return _pcall(
        kern, name=name, grid=(cfg.h, cfg.nq), in_specs=ins,
        out_specs=[pl.BlockSpec((tq, HEAD_V), lambda h, i: (i, h)),
                   pl.BlockSpec((1, tq, 1), lambda h, i: (h, i, 0)),
                   pl.BlockSpec((1, nsub, 1, tb), lambda h, i: (h, i, 0, 0))],
        out_shape=[jax.ShapeDtypeStruct((s, cfg.h * HEAD_V), F32),
                   jax.ShapeDtypeStruct((cfg.h, s, 1), F32),
                   jax.ShapeDtypeStruct((cfg.h, cfg.nb, 1, tb), F32)],
        scratch_shapes=[pltpu.VMEM((nsub, tb, tk), F32)] * 2 if cfg.causal else [],
        compiler_params=pltpu.CompilerParams(dimension_semantics=("parallel", "parallel")),
    )(*args)


def _attn_dq(cfg, q, ks, v, sel, o, lse, do, dq_in, name):
    s, tq, tk, dk, tb, nsub = cfg.s, cfg.tq, cfg.tk, cfg.dk, cfg.tb, cfg.nsub
    has_sel = sel is not None
    has_in = dq_in is not None
    nkp = len(ks)

    def kern(*refs):
        refs = list(refs)
        q_ref, k_refs, v_ref = refs[0], refs[1:1 + nkp], refs[1 + nkp]
        p0 = 2 + nkp
        sel_ref = refs[p0] if has_sel else None
        p0 += has_sel
        o_ref, l_ref, do_ref = refs[p0:p0 + 3]
        p0 += 3
        in_ref = refs[p0] if has_in else None
        p0 += has_in
        dq_ref, dr_ref = refs[p0:p0 + 2]
        h, i = pl.program_id(0), pl.program_id(1)
        part = [slice(r * tb, (r + 1) * tb) for r in range(nsub)]
        qs = [q_ref[p, :].astype(BF16) for p in part]
        ts = [i * tq + r * tb + lax.broadcasted_iota(jnp.int32, (tb, 1), 0) for r in range(nsub)]
        sels = [sel_ref[p, :].astype(BF16) for p in part] if has_sel else None
        dvecs, dobs, lses = [], [], []
        for r, p in enumerate(part):
            dov = do_ref[p, :]
            dvec = jnp.sum(dov * o_ref[p, :], axis=1, keepdims=True)
            dr_ref[0, r] = _to_row(dvec)
            dvecs.append(dvec)
            dobs.append(dov.astype(BF16))
            lses.append(l_ref[0, p, :])

        def load(k0):
            rows = pl.ds(k0, tk)
            return _load_keys(k_refs, rows), v_ref[rows, :].astype(BF16)

        def grad(r, k0, s_raw, dp, kk, emat, acc, diag):
            n = k0 + lax.broadcasted_iota(jnp.int32, (1, tk), 1)
            selx = _nn(sels[r], emat) if has_sel else None
            p, _ = _scores(cfg, s_raw, ts[r], n, h, selx, diag, lses[r])
            ds = p * (dp - dvecs[r])
            return acc + _nn(ds.astype(BF16), kk)

        def step(r, k0, kk, vv, emat, acc, diag):
            return grad(r, k0, _nt(qs[r], kk), _nt(dobs[r], vv), kk, emat, acc, diag)

        def chunk(k0, accs, diag):
            kk, vv = load(k0)
            emat = _block_of_key(k0, tk, False) if has_sel else None
            return tuple(step(r, k0, kk, vv, emat, accs[r], diag) for r in range(nsub))

        accs = (jnp.zeros((tb, dk), F32),) * nsub
        if cfg.causal:
            sa, pa, sb, pb = refs[-4:]
            full = (i * tq) // tk

            def products_into(sbuf, pbuf, c):
                kk, vv = load(pl.multiple_of(c * tk, tk))
                for r in range(nsub):
                    sbuf[r] = _nt(qs[r], kk)
                    pbuf[r] = _nt(dobs[r], vv)

            def consume(sbuf, pbuf, c, ac, diag):
                k0 = pl.multiple_of(c * tk, tk)
                kk = _load_keys(k_refs, pl.ds(k0, tk))
                emat = _block_of_key(k0, tk, False) if has_sel else None
                return tuple(grad(r, k0, sbuf[r], pbuf[r], kk, emat, ac[r], diag) for r in range(nsub))

            def pair(p, ac):
                products_into(sb, pb, 2 * p + 1)
                ac = consume(sa, pa, 2 * p, ac, False)
                products_into(sa, pa, 2 * p + 2)
                return consume(sb, pb, 2 * p + 1, ac, False)

            def odd_tail(ac):
                products_into(sb, pb, full)
                ac = consume(sa, pa, full - 1, ac, False)
                return consume(sb, pb, full, ac, True)

            products_into(sa, pa, 0)
            accs = lax.fori_loop(0, full // 2, pair, accs)
            accs = lax.cond(full % 2 == 1, odd_tail, lambda ac: consume(sa, pa, full, ac, True), accs)
        elif cfg.mode == "win":
            starts = [pl.multiple_of(jnp.maximum(i * tq + r * tb - WIN, 0), tb) for r in range(nsub)]
            accs = tuple(step(r, k0, *load(k0), None, accs[r], True) for r, k0 in enumerate(starts))
        else:
            accs = chunk(0, accs, True)
        for r, p in enumerate(part):
            dq_ref[p, :] = accs[r] * cfg.scale + in_ref[p, :] if has_in else accs[r] * cfg.scale

    qs = pl.BlockSpec((tq, dk), lambda h, i: (i, h))
    ins = [pl.BlockSpec((tq, q.width), lambda h, i: (i, q.col(h)))]
    ins += [pl.BlockSpec((cfg.sk, p.width), lambda h, i, p=p: (0, p.col(h))) for p in ks]
    ins += [pl.BlockSpec((cfg.sk, HEAD_V), lambda h, i: (0, v.col(h)))]
    args = [q.arr] + [p.arr for p in ks] + [v.arr]
    if has_sel:
        ins.append(pl.BlockSpec((tq, LANE), lambda h, i: (i, 0)))
        args.append(sel)
    ins += [pl.BlockSpec((tq, HEAD_V), lambda h, i: (i, o.col(h))),
            pl.BlockSpec((1, tq, 1), lambda h, i: (h, i, 0)),
            pl.BlockSpec((tq, HEAD_V), lambda h, i: (i, do.col(h)))]
    args += [o.arr, lse, do.arr]
    if has_in:
        ins.append(qs)
        args.append(dq_in)
    return _pcall(
        kern, name=name, grid=(cfg.h, cfg.nq), in_specs=ins,
        out_specs=[qs, pl.BlockSpec((1, nsub, 1, tb), lambda h, i: (h, i, 0, 0))],
        out_shape=[jax.ShapeDtypeStruct((s, cfg.h * dk), F32),
                   jax.ShapeDtypeStruct((cfg.h, cfg.nb, 1, tb), F32)],
        scratch_shapes=[pltpu.VMEM((nsub, tb, tk), F32)] * 4 if cfg.causal else [],
        compiler_params=pltpu.CompilerParams(dimension_semantics=("parallel", "parallel")),
    )(*args)


def _attn_dkv(cfg, q, ks, v, selt, lse_r, d_r, do, name):
    s, tq, tkb, dk, kb, ksub = cfg.s, cfg.tb, cfg.tkb, cfg.dk, cfg.kb, cfg.ksub
    nq = cfg.nb
    has_sel = selt is not None
    nkp = len(ks)
    outs = list(ks) + [v]

    def kern(*refs):
        k_refs, v_ref = refs[:nkp], refs[nkp]
        q_ref, do_ref, lr_ref, dr_ref = refs[nkp + 1:nkp + 5]
        st_ref = refs[nkp + 5] if has_sel else None
        out_refs = refs[nkp + 5 + has_sel:2 * nkp + 6 + has_sel]
        sa, pa, sb, pb = refs[-4:]
        j, h = pl.program_id(0), pl.program_id(1)
        k0 = j * tkb
        part = [slice(u * kb, (u + 1) * kb) for u in range(ksub)]
        kks = [_load_keys(k_refs, p) for p in part]
        vvs = [v_ref[p, :].astype(BF16) for p in part]
        ns = [k0 + u * kb + lax.broadcasted_iota(jnp.int32, (kb, 1), 0) for u in range(ksub)]
        emats = [_block_of_key(k0 + u * kb, kb, True) for u in range(ksub)] if has_sel else None

        def load_q(i):
            rows = pl.ds(pl.multiple_of(i * tq, tq), tq)
            return q_ref[rows, :].astype(BF16), do_ref[rows, :].astype(BF16)

        def products_into(sbuf, pbuf, i):
            qi, doi = load_q(i)
            for u in range(ksub):
                sbuf[u] = _nt(kks[u], qi)
                pbuf[u] = _nt(vvs[u], doi)

        def consume(sbuf, pbuf, i, carry):
            qi, doi = load_q(i)
            t = i * tq + lax.broadcasted_iota(jnp.int32, (1, tq), 1)
            selt_i = st_ref[i].astype(BF16) if has_sel else None
            new = []
            for u in range(ksub):
                dk_acc, dv_acc = carry[u]
                selx = _nn(emats[u], selt_i) if has_sel else None
                pt, _ = _scores(cfg, sbuf[u], t, ns[u], h, selx, True, lr_ref[0, i])
                dv_acc = dv_acc + _nn(pt.astype(BF16), doi)
                dst = pt * (pbuf[u] - dr_ref[0, i])
                new.append((dk_acc + _nn(dst.astype(BF16), qi), dv_acc))
            return tuple(new)

        if cfg.causal:
            first, count = k0 // tq, nq - k0 // tq
        elif cfg.mode == "win":
            first = k0 // tq
            count = jnp.minimum((k0 + tkb + WIN - 2) // tq + 1, nq) - first
        else:
            first, count = 0, nq

        def pair(p, cr):
            i0 = first + 2 * p
            products_into(sb, pb, i0 + 1)
            cr = consume(sa, pa, i0, cr)
            products_into(sa, pa, i0 + 2)
            return consume(sb, pb, i0 + 1, cr)

        carry = ((jnp.zeros((kb, dk), F32), jnp.zeros((kb, HEAD_V), F32)),) * ksub
        products_into(sa, pa, first)
        carry = lax.fori_loop(0, count // 2 - 1, pair, carry)
        last = first + count - 2
        products_into(sb, pb, last + 1)
        carry = consume(sa, pa, last, carry)
        carry = consume(sb, pb, last + 1, carry)
        for u, (dk_acc, dv_acc) in enumerate(carry):
            vals, off = [], 0
            for p in ks:
                vals.append(dk_acc[:, off:off + p.width] * cfg.scale)
                off += p.width
            vals.append(dv_acc)
            for src, ref, val in zip(outs, out_refs, vals):
                if src.per_head:
                    ref[part[u], :] = val
                else:
                    @pl.when(h == 0)
                    def _(ref=ref, val=val, u=u):
                        ref[part[u], :] = val

                    @pl.when(h > 0)
                    def _(ref=ref, val=val, u=u):
                        ref[part[u], :] += val

    rowv = pl.BlockSpec((1, nq, 1, tq), lambda j, h: (h, 0, 0, 0))
    ins = [pl.BlockSpec((tkb, p.width), lambda j, h, p=p: (j, p.col(h))) for p in ks]
    ins += [pl.BlockSpec((tkb, HEAD_V), lambda j, h: (j, v.col(h))),
            pl.BlockSpec((s, q.width), lambda j, h: (0, q.col(h))),
            pl.BlockSpec((s, HEAD_V), lambda j, h: (0, do.col(h))), rowv, rowv]
    args = [p.arr for p in ks] + [v.arr, q.arr, do.arr, lse_r, d_r]
    if has_sel:
        ins.append(pl.BlockSpec((nq, LANE, tq), lambda j, h: (0, 0, 0)))
        args.append(selt)
    out_specs = [pl.BlockSpec((tkb, p.width), lambda j, h, p=p: (j, h if p.per_head else 0)) for p in outs]
    out_shape = [jax.ShapeDtypeStruct((cfg.sk, (cfg.h if p.per_head else 1) * p.width), F32) for p in outs]
    assert nq % 2 == 0 and (cfg.mode in ("cmp", "mem") or tkb % (2 * tq) == 0), (nq, tkb, tq)
    return _pcall(
        kern, name=name, grid=(cfg.sk // tkb, cfg.h), in_specs=ins, out_specs=out_specs, out_shape=out_shape,
        scratch_shapes=[pltpu.VMEM((ksub, kb, tq), F32)] * 4,
        compiler_params=pltpu.CompilerParams(dimension_semantics=("parallel", "arbitrary")),
    )(*args)


def _attn_bwd(cfg, q, ks, v, sel, selt, o, lse, lse_r, do, dq_in, name):
    dq, d_r = _attn_dq(cfg, q, ks, v, sel, o, lse, do, dq_in, name + "_dq")
    res = _attn_dkv(cfg, q, ks, v, selt, lse_r, d_r, do, name + "_dkv")
    return dq, res[:-1], res[-1]


def _select(cfg, q, k_cmp, overlap):
    s, tq, sk = cfg.s, cfg.tb, cfg.sk
    n_s = s // SLC_LEN
    top_n = min(SLC_TOPN, n_s)

    def kern(q_ref, k_ref, ov_ref, sel_ref, selt_ref):
        i = pl.program_id(0)
        t = i * tq + lax.broadcasted_iota(jnp.int32, (tq, 1), 0)
        n = lax.broadcasted_iota(jnp.int32, (1, sk), 1)
        kk = k_ref[...]
        imp = jnp.zeros((tq, LANE), F32)
        for h in range(NSA_HEADS):
            sc, mask = _scores(cfg, _nt(q_ref[:, h * 256:(h + 1) * 256], kk), t, n, h, None, True)
            m = jnp.max(sc, axis=1, keepdims=True)
            e = jnp.where(mask, jnp.exp2(sc - m), 0.0)
            p = e / (jnp.sum(e, axis=1, keepdims=True) + 1e-20)
            imp = imp + _nn(p.astype(BF16), ov_ref[...])
        j = lax.broadcasted_iota(jnp.int32, (tq, LANE), 1)
        cur = t >> 6
        forced = (j == 0) | (j == cur) | (j == cur - 1)
        imp = jnp.where(forced, 1e9, imp)
        imp = jnp.where(j > cur, -1e9, imp)
        imp = jnp.where(j >= n_s, -3e38, imp)

        def pick(_, carry):
            work, chosen = carry
            mx = jnp.max(work, axis=1, keepdims=True)
            first = jnp.min(jnp.where(work == mx, j, LANE), axis=1, keepdims=True)
            hit = j == first
            return jnp.where(hit, -3e38, work), jnp.where(hit, 1.0, chosen)

        _, chosen = lax.fori_loop(0, top_n, pick, (imp, jnp.zeros((tq, LANE), F32)))
        chosen = jnp.where(j <= cur, chosen, 0.0)
        sel_ref[...] = chosen
        selt_ref[0] = jnp.transpose(chosen)

    return _pcall(
        kern, name="nsa_select", grid=(cfg.nb,),
        in_specs=[pl.BlockSpec((tq, NSA_HEADS * 256), lambda i: (i, q.col0)),
                  pl.BlockSpec((sk, 256), lambda i: (0, 0)), pl.BlockSpec((sk, LANE), lambda i: (0, 0))],
        out_specs=[pl.BlockSpec((tq, LANE), lambda i: (i, 0)), pl.BlockSpec((1, LANE, tq), lambda i: (i, 0, 0))],
        out_shape=[jax.ShapeDtypeStruct((s, LANE), F32), jax.ShapeDtypeStruct((cfg.nb, LANE, tq), F32)],
    )(q.arr, k_cmp, overlap)


def _silu_grad(pre):
    sg = _sigmoid(pre)
    return sg * (1.0 + pre * (1.0 - sg))


def _compress_fwd(a_lo, a_hi, pe_lo, pe_hi, w1_lo, w1_hi, w2, name):
    n, dp = a_lo.shape[0], w2.shape[1]

    def kern(alo, ahi, plo, phi, w1l, w1h, w2r, out_ref, pre_ref):
        xl = (alo[...] + plo[...]).astype(BF16)
        xh = (ahi[...] + phi[...]).astype(BF16)
        pre = _nn(xl, w1l[...]) + _nn(xh, w1h[...])
        act = pre * _sigmoid(pre)
        out_ref[...] = _nn(act.astype(BF16), w2r[...]).astype(BF16)
        pre_ref[...] = pre

    return _pcall(kern, name=name,
                  out_shape=[jax.ShapeDtypeStruct((n, dp), BF16), jax.ShapeDtypeStruct((n, dp), F32)],
                  )(a_lo, a_hi, pe_lo, pe_hi, w1_lo, w1_hi, w2)


def _compress_bwd(a_lo, a_hi, pe_lo, pe_hi, w1_lo, w1_hi, w2, pre, pre_sh, dout, dout_sh, name):
    n, ln = a_lo.shape
    dp = w2.shape[1]

    def kern(alo, ahi, plo, phi, w1l, w1h, w2r, pre_ref, presh_ref, do_ref, dosh_ref,
             da_ref, dpl_ref, dph_ref, dw1l_ref, dw1h_ref, dw2_ref):
        prev = pre_ref[...]
        act = prev * _sigmoid(prev)
        dob = do_ref[...].astype(BF16)
        w2v = w2r[...]
        dpre = (_nt(dob, w2v) * _silu_grad(prev)).astype(BF16)
        dpre_sh = (_nt(dosh_ref[...].astype(BF16), w2v) * _silu_grad(presh_ref[...])).astype(BF16)
        dw2_ref[...] = _nn(act.T.astype(BF16), dob)
        xl = alo[...] + plo[...]
        xh = ahi[...] + phi[...]
        dw1l_ref[...] = _nn(xl.T.astype(BF16), dpre)
        dw1h_ref[...] = _nn(xh.T.astype(BF16), dpre)
        dal = _nt(dpre, w1l[...])
        dah_sh = _nt(dpre_sh, w1h[...])
        da_ref[...] = dal + dah_sh
        dpl_ref[...] = jnp.sum(dal, axis=0, keepdims=True)
        dph_ref[...] = jnp.sum(dah_sh, axis=0, keepdims=True)

    return _pcall(
        kern, name=name,
        out_shape=[jax.ShapeDtypeStruct((n, ln), F32), jax.ShapeDtypeStruct((1, ln), F32),
                   jax.ShapeDtypeStruct((1, ln), F32), jax.ShapeDtypeStruct((ln, dp), F32),
                   jax.ShapeDtypeStruct((ln, dp), F32), jax.ShapeDtypeStruct((dp, dp), F32)],
    )(a_lo, a_hi, pe_lo, pe_hi, w1_lo, w1_hi, w2, pre, pre_sh, dout, dout_sh)


def _nsa_combine(o_cmp, o_slc, o_win, gl):
    s, w = o_cmp.shape
    tr = _tile(s, 512)

    def kern(a_ref, b_ref, c_ref, g_ref, o_ref):
        g = _sigmoid(g_ref[...])
        for h in range(NSA_HEADS):
            cs = slice(h * HEAD_V, (h + 1) * HEAD_V)
            o_ref[:, cs] = (g[:, 3 * h:3 * h + 1] * a_ref[:, cs] + g[:, 3 * h + 1:3 * h + 2] * b_ref[:, cs]
                            + g[:, 3 * h + 2:3 * h + 3] * c_ref[:, cs])

    row = pl.BlockSpec((tr, w), lambda i: (i, 0))
    return _pcall(kern, name="nsa_combine", grid=(s // tr,),
                  in_specs=[row, row, row, pl.BlockSpec((tr, LANE), lambda i: (i, gl.col0))], out_specs=row,
                  out_shape=jax.ShapeDtypeStruct((s, w), F32))(o_cmp, o_slc, o_win, gl.arr)


def _nsa_combine_bwd(do_cat, o_cmp, o_slc, o_win, gl):
    s, w = o_cmp.shape
    tr = _tile(s, 512)

    def kern(d_ref, a_ref, b_ref, c_ref, g_ref, da_ref, db_ref, dc_ref, dg_ref):
        g = _sigmoid(g_ref[...])
        lane = lax.broadcasted_iota(jnp.int32, (tr, LANE), 1)
        dgl = jnp.zeros((tr, LANE), F32)
        for h in range(NSA_HEADS):
            cs = slice(h * HEAD_V, (h + 1) * HEAD_V)
            dv = d_ref[:, cs]
            for b, (src, dst) in enumerate(((a_ref, da_ref), (b_ref, db_ref), (c_ref, dc_ref))):
                gate = g[:, 3 * h + b:3 * h + b + 1]
                dst[:, cs] = gate * dv
                dgate = jnp.sum(dv * src[:, cs], axis=1, keepdims=True)
                dgl = jnp.where(lane == 3 * h + b, dgate * gate * (1.0 - gate), dgl)
        dg_ref[...] = dgl

    row = pl.BlockSpec((tr, w), lambda i: (i, 0))
    tab = pl.BlockSpec((tr, LANE), lambda i: (i, 0))
    return _pcall(kern, name="nsa_combine_bwd", grid=(s // tr,),
                  in_specs=[pl.BlockSpec((tr, w), lambda i: (i, 2)), row, row, row,
                            pl.BlockSpec((tr, LANE), lambda i: (i, gl.col0))],
                  out_specs=[row, row, row, tab],
                  out_shape=[jax.ShapeDtypeStruct((s, w), F32)] * 3 + [jax.ShapeDtypeStruct((s, LANE), F32)],
                  )(do_cat, o_cmp, o_slc, o_win, gl.arr)


def _gate_fwd(o_mla, o_nsa, o_mem, hp):
    s = o_mla.shape[0]
    tr = _tile(s, 256)

    def kern(a_ref, b_ref, c_ref, z_ref, u_ref):
        z = z_ref[...]
        sz = z * _sigmoid(z)
        u_ref[:, 0:1024] = (a_ref[...] * sz[:, 0:1024]).astype(BF16)
        u_ref[:, 1024:1536] = (b_ref[...] * sz[:, 1024:1536]).astype(BF16)
        u_ref[:, 1536:2048] = (c_ref[...] * sz[:, 1536:2048]).astype(BF16)

    return _pcall(
        kern, name="gate_fwd", grid=(s // tr,),
        in_specs=[pl.BlockSpec((tr, 1024), lambda i: (i, 0)), pl.BlockSpec((tr, 512), lambda i: (i, 0)),
                  pl.BlockSpec((tr, 512), lambda i: (i, 0)), pl.BlockSpec((tr, 2048), lambda i: (i, 2))],
        out_specs=pl.BlockSpec((tr, 2048), lambda i: (i, 0)),
        out_shape=jax.ShapeDtypeStruct((s, 2048), BF16))(o_mla, o_nsa, o_mem, hp)


def _gate_bwd(du, o_mla, o_nsa, o_mem, hp):
    s = du.shape[0]
    tr = _tile(s, 256)

    def kern(d_ref, a_ref, b_ref, c_ref, z_ref, do_ref, dz_ref):
        z = z_ref[...]
        sg = _sigmoid(z)
        sz = z * sg
        dsz = sg * (1.0 + z * (1.0 - sg))
        d = d_ref[...]
        do_ref[...] = d * sz
        dz_ref[:, 0:1024] = d[:, 0:1024] * a_ref[...] * dsz[:, 0:1024]
        dz_ref[:, 1024:1536] = d[:, 1024:1536] * b_ref[...] * dsz[:, 1024:1536]
        dz_ref[:, 1536:2048] = d[:, 1536:2048] * c_ref[...] * dsz[:, 1536:2048]

    wide = pl.BlockSpec((tr, 2048), lambda i: (i, 0))
    return _pcall(
        kern, name="gate_bwd", grid=(s // tr,),
        in_specs=[wide, pl.BlockSpec((tr, 1024), lambda i: (i, 0)), pl.BlockSpec((tr, 512), lambda i: (i, 0)),
                  pl.BlockSpec((tr, 512), lambda i: (i, 0)), pl.BlockSpec((tr, 2048), lambda i: (i, 2))],
        out_specs=[wide, wide],
        out_shape=[jax.ShapeDtypeStruct((s, 2048), F32)] * 2)(du, o_mla, o_nsa, o_mem, hp)


def _tile2d(rows, cols, arrays):
    if rows % 16 == 0:
        return _row_tile(rows, cols * arrays), cols
    want = max(LANE, BLOCK_BYTES // (rows * 4 * arrays) // LANE * LANE)
    tc = LANE
    for t in range(LANE, cols + 1, LANE):
        if cols % t == 0 and t <= want:
            tc = t
    return rows, tc


def _sum_slots(buf, name):
    n, rows, cols = buf.shape
    tr, tc = _tile2d(rows, cols, n)

    def kern(b_ref, o_ref):
        acc = b_ref[0].astype(F32)
        for i in range(1, n):
            acc = acc + b_ref[i].astype(F32)
        o_ref[...] = acc

    return _pcall(kern, name=name, grid=(rows // tr, cols // tc),
                  in_specs=[pl.BlockSpec((n, tr, tc), lambda i, j: (0, i, j))],
                  out_specs=pl.BlockSpec((tr, tc), lambda i, j: (i, j)),
                  out_shape=jax.ShapeDtypeStruct((rows, cols), F32))(buf)


def _pair_sum(g4, theirs, core, axis, name):
    n, rows, cols = theirs.shape
    tr, tc = _tile2d(rows, cols, 1)
    nbr, nbc = rows // tr, cols // tc

    def kern(c_ref, a_ref, b_ref, o_ref):
        o_ref[...] = (a_ref[...] + b_ref[...]).astype(BF16)

    blk = (1, tr, tc)
    mine = ((lambda s, i, j, c: (s, c[0] * nbr + i, j)) if axis == 0
            else (lambda s, i, j, c: (s, i, c[0] * nbc + j)))
    grid_spec = pltpu.PrefetchScalarGridSpec(
        num_scalar_prefetch=1, grid=(n, nbr, nbc),
        in_specs=[pl.BlockSpec(blk, mine), pl.BlockSpec(blk, lambda s, i, j, c: (s, i, j))],
        out_specs=pl.BlockSpec(blk, lambda s, i, j, c: (s, i, j)))
    return _pcall(kern, name=name, grid_spec=grid_spec,
                  out_shape=jax.ShapeDtypeStruct((n, rows, cols), BF16))(core, g4, theirs)


def _adamw(w, g, m, v, name):
    rows, cols = w.shape
    tr, tc = _tile2d(rows, cols, 4)
    bc1 = 1.0 - ADAM_B1 ** ADAM_STEP
    bc2 = 1.0 - ADAM_B2 ** ADAM_STEP

    def kern(w_ref, g_ref, m_ref, v_ref, d_ref, mo_ref, vo_ref):
        gv = g_ref[...]
        mn = ADAM_B1 * m_ref[...] + (1.0 - ADAM_B1) * gv
        vn = ADAM_B2 * v_ref[...] + (1.0 - ADAM_B2) * (gv * gv)
        d_ref[...] = -ADAM_LR * ((mn / bc1) / (jnp.sqrt(vn / bc2) + ADAM_EPS) + ADAM_WD * w_ref[...])
        mo_ref[...] = mn
        vo_ref[...] = vn

    blk = pl.BlockSpec((tr, tc), lambda i, j: (i, j))
    return _pcall(kern, name=name, grid=(rows // tr, cols // tc), in_specs=[blk] * 4, out_specs=[blk] * 3,
                  out_shape=[jax.ShapeDtypeStruct((rows, cols), F32)] * 3)(w, g, m, v)


ANY = pl.BlockSpec(memory_space=pl.ANY)


def _place():
    x, y, c = lax.axis_index("x"), lax.axis_index("y"), lax.axis_index("c")
    chips = [(1 - x, y), (x, 1 - y), (1 - x, 1 - y)]
    return x, y, c, chips


def _remote(src, dst, send_sem, recv_sem, to):
    return pltpu.make_async_remote_copy(src_ref=src, dst_ref=dst, send_sem=send_sem, recv_sem=recv_sem,
                                        device_id=to, device_id_type=MESH)


def _half(ref, lead, core, axis):
    size = ref.shape[len(lead) + axis] // 2
    cut = pl.ds(core * size, size)
    return ref.at[tuple(lead) + ((cut, slice(None)) if axis == 0 else (slice(None), cut))]


def _gather_shards(ws, axes):
    nw = len(ws)

    def body(*refs):
        w_refs, out_refs = refs[:nw], refs[nw:2 * nw]
        send_sems, recv_sems = refs[2 * nw:]
        x, y, c, chips = _place()
        me = 2 * x + y
        sibling = (x, y, 1 - c)

        def part(i, slot, core):
            return _half(out_refs[i], (slot,), core, axes[i])

        def copy(sem, src, dst, to):
            return _remote(src, dst, send_sems.at[sem], recv_sems.at[sem], to)

        first = [copy(j * nw + i, _half(w_refs[i], (), c, axes[i]), part(i, me, c), (*chip, c))
                 for j, chip in enumerate(chips) for i in range(nw)]
        for cp in first:
            cp.start()
        passed = []
        for j, (cx, cy) in enumerate(chips):
            slot = 2 * cx + cy
            for i in range(nw):
                copy(j * nw + i, part(i, slot, c), part(i, slot, c), (x, y, c)).wait_recv()
                fwd = copy((3 + j) * nw + i, part(i, slot, c), part(i, slot, c), sibling)
                fwd.start()
                passed.append(fwd)
        for j, (cx, cy) in enumerate(chips):
            slot = 2 * cx + cy
            for i in range(nw):
                copy((3 + j) * nw + i, part(i, slot, 1 - c), part(i, slot, 1 - c), (x, y, c)).wait_recv()
        for cp in first + passed:
            cp.wait_send()

    return _pcall(
        body, name="gather_shards", in_specs=[ANY] * nw, out_specs=[ANY] * nw,
        out_shape=[jax.ShapeDtypeStruct((4,) + w.shape, w.dtype) for w in ws],
        scratch_shapes=[pltpu.SemaphoreType.DMA((6 * nw,)), pltpu.SemaphoreType.DMA((6 * nw,))],
    )(*ws)


def _half_shape(shape, axis):
    return tuple(d // 2 if k == len(shape) - 2 + axis else d for k, d in enumerate(shape))


def _pair_exchange(gs, axes):
    nw = len(gs)

    def body(*refs):
        g_refs, out_refs = refs[:nw], refs[nw:2 * nw]
        send_sems, recv_sems = refs[2 * nw:]
        x, y, c, _ = _place()
        cps = []
        for i in range(nw):
            cp = _remote(_half(g_refs[i], (slice(None),), 1 - c, axes[i]), out_refs[i],
                         send_sems.at[i], recv_sems.at[i], (x, y, 1 - c))
            cp.start()
            cps.append(cp)
        for cp in cps:
            cp.wait()

    return _pcall(body, name="pair_exchange", in_specs=[ANY] * nw, out_specs=[ANY] * nw,
                  out_shape=[jax.ShapeDtypeStruct(_half_shape(g.shape, a), g.dtype) for g, a in zip(gs, axes)],
                  scratch_shapes=[pltpu.SemaphoreType.DMA((nw,)), pltpu.SemaphoreType.DMA((nw,))])(*gs)


def _chip_exchange(ps):
    nw = len(ps)

    def body(*refs):
        p_refs, out_refs = refs[:nw], refs[nw:2 * nw]
        send_sems, recv_sems, local_sems = refs[2 * nw:]
        x, y, c, chips = _place()
        me = 2 * x + y
        mine = [pltpu.make_async_copy(p_refs[i].at[me], out_refs[i].at[me], local_sems.at[i]) for i in range(nw)]
        for cp in mine:
            cp.start()
        sends = []
        for j, (cx, cy) in enumerate(chips):
            for i in range(nw):
                cp = _remote(p_refs[i].at[2 * cx + cy], out_refs[i].at[me], send_sems.at[j * nw + i],
                             recv_sems.at[j * nw + i], (cx, cy, c))
                cp.start()
                sends.append(cp)
        for j, (cx, cy) in enumerate(chips):
            slot = 2 * cx + cy
            for i in range(nw):
                _remote(out_refs[i].at[slot], out_refs[i].at[slot], send_sems.at[j * nw + i],
                        recv_sems.at[j * nw + i], (x, y, c)).wait_recv()
        for cp in sends:
            cp.wait_send()
        for cp in mine:
            cp.wait()

    return _pcall(body, name="chip_exchange", in_specs=[ANY] * nw, out_specs=[ANY] * nw,
                  out_shape=[jax.ShapeDtypeStruct(p.shape, p.dtype) for p in ps],
                  scratch_shapes=[pltpu.SemaphoreType.DMA((3 * nw,)), pltpu.SemaphoreType.DMA((3 * nw,)),
                                  pltpu.SemaphoreType.DMA((nw,))])(*ps)


def _half_exchange(ts, axes):
    nw = len(ts)

    def body(*refs):
        t_refs, out_refs = refs[:nw], refs[nw:2 * nw]
        send_sems, recv_sems = refs[2 * nw:]
        x, y, c, _ = _place()
        sends = []
        for i in range(nw):
            cp = _remote(t_refs[i], _half(out_refs[i], (), c, axes[i]), send_sems.at[i], recv_sems.at[i],
                         (x, y, 1 - c))
            cp.start()
            sends.append(cp)
        for i in range(nw):
            _remote(t_refs[i], _half(out_refs[i], (), 1 - c, axes[i]), send_sems.at[i], recv_sems.at[i],
                    (x, y, c)).wait_recv()
        for cp in sends:
            cp.wait_send()

    def whole(t, a):
        return tuple(2 * d if k == a else d for k, d in enumerate(t.shape))

    return _pcall(body, name="half_exchange", in_specs=[ANY] * nw, out_specs=[ANY] * nw,
                  out_shape=[jax.ShapeDtypeStruct(whole(t, a), t.dtype) for t, a in zip(ts, axes)],
                  scratch_shapes=[pltpu.SemaphoreType.DMA((nw,)), pltpu.SemaphoreType.DMA((nw,))])(*ts)


def _gather_all(v):
    rows, cols = v.shape

    def body(v_ref, out_ref, send_sems, recv_sems, local_sem):
        x, y, c, _ = _place()
        me = 4 * x + 2 * y + c
        mine = pltpu.make_async_copy(v_ref, out_ref.at[me], local_sem)
        mine.start()
        sends = []
        for d in range(1, 8):
            peer = (x ^ (d >> 2), y ^ ((d >> 1) & 1), c ^ (d & 1))
            cp = _remote(v_ref, out_ref.at[me], send_sems.at[d - 1], recv_sems.at[d - 1], peer)
            cp.start()
            sends.append(cp)
        for d in range(1, 8):
            slot = 4 * (x ^ (d >> 2)) + 2 * (y ^ ((d >> 1) & 1)) + (c ^ (d & 1))
            _remote(v_ref, out_ref.at[slot], send_sems.at[d - 1], recv_sems.at[d - 1], (x, y, c)).wait_recv()
        for cp in sends:
            cp.wait_send()
        mine.wait()

    return _pcall(body, name="gather_all", in_specs=[ANY], out_specs=ANY,
                  out_shape=jax.ShapeDtypeStruct((8, rows, cols), v.dtype),
                  scratch_shapes=[pltpu.SemaphoreType.DMA((7,)), pltpu.SemaphoreType.DMA((7,)),
                                  pltpu.SemaphoreType.DMA])(v)


def _pad_cols(a, width):
    return a if a.shape[1] == width else jnp.pad(a, ((0, 0), (0, width - a.shape[1])))


def _pad_rows(a, height):
    return a if a.shape[0] == height else jnp.pad(a, ((0, height - a.shape[0]), (0, 0)))


def _w_in_padded(wt):
    def seg(name, height=None):
        o, n = ORIG[name]
        return _pad_rows(wt[o:o + n], height or n)

    qn = wt[ORIG["q_nsa"][0]:ORIG["q_nsa"][0] + 768].reshape(NSA_HEADS, NSA_DK, -1)
    qn = jnp.pad(qn, ((0, 0), (0, 256 - NSA_DK), (0, 0))).reshape(NSA_HEADS * 256, -1)
    kr = seg("k_rope")
    zeros = jnp.zeros((PAD["z"] - (PAD["q_mem"] + 512), wt.shape[1]), wt.dtype)
    return jnp.concatenate(
        [seg("c_q"), seg("c_kv"), qn, seg("k_c", 256), seg("k_s", 256), seg("k_w", 256), kr, kr,
         seg("v_c"), seg("v_s"), seg("v_w"), seg("g_nsa", LANE), seg("q_mem"), zeros,
         seg("z_mla"), seg("z_nsa"), seg("z_mem")], axis=0)


def _w_in_unpadded(gt):
    def seg(name, n):
        return gt[PAD[name]:PAD[name] + n]

    qn = gt[PAD["q_nsa"]:PAD["q_nsa"] + 1024].reshape(NSA_HEADS, 256, -1)[:, :NSA_DK].reshape(768, -1)
    z = PAD["z"]
    return jnp.concatenate(
        [seg("c_q", 512), seg("c_kv", 512), seg("k_rope", 64), gt[z:z + 1024], qn, seg("k_c", 192),
         seg("v_c", 128), seg("k_s", 192), seg("v_s", 128), seg("k_w", 192), seg("v_w", 128),
         seg("g_nsa", 12), gt[z + 1024:z + 1536], seg("q_mem", 512), gt[z + 1536:z + 2048]], axis=0)


def _rope_tables(s):
    pos = jnp.arange(s, dtype=F32)
    inv_freq = ROPE_THETA ** (-jnp.arange(0, 64, 2, dtype=F32) / 64)
    ang = pos[:, None] * inv_freq[None, :]
    cos, sin = jnp.cos(ang), jnp.sin(ang)
    z = jnp.zeros((s, 64), F32)
    return jnp.concatenate([cos, cos, z], axis=1), jnp.concatenate([-sin, sin, z], axis=1)


def _overlap_table(s):
    n_c, n_s = s // CMP_STRIDE, s // SLC_LEN
    c0 = np.arange(n_c)[:, None] * CMP_STRIDE
    s0 = np.arange(LANE)[None, :] * SLC_LEN
    ov = (c0 < s0 + SLC_LEN) & (c0 + CMP_LEN > s0) & (np.arange(n_c)[:, None] < n_c - 1) & (np.arange(LANE)[None, :] < n_s)
    return jnp.asarray(ov.astype(np.float32), dtype=BF16)


def _shift_down(a):
    return jnp.concatenate([jnp.zeros((8, a.shape[1]), a.dtype), a], axis=0)[7:7 + a.shape[0]]


def _shift_up(a):
    return jnp.concatenate([a, jnp.zeros((8, a.shape[1]), a.dtype)], axis=0)[1:1 + a.shape[0]]


def _local_step(x, mem, target, w):
    s = x.shape[0]
    cs, sn = _rope_tables(s)
    t_ = jnp.transpose

    w_in_p = _w_in_padded(w["w_in_t"])
    xn, rstd_x = _rms_fwd(_Src(x, D_MODEL), w["norm_g"], "norm_x")
    hp, hpb = _mm(xn, w_in_p, "in_proj", mode="nt", second_dtype=BF16)

    w_uq3 = w["w_uq"].reshape(512, MLA_HEADS, 192)
    w_uq_p = jnp.concatenate([w_uq3, w_uq3[:, :, 128:]], axis=2).reshape(512, MLA_HEADS * 256)
    w_ukv_p = t_(w["w_ukv"].reshape(512, MLA_HEADS, 2, 128), (0, 2, 1, 3)).reshape(512, 2048)
    c_q, c_kv = _Src(hp, 512, 0), _Src(hp, 512, 1)
    cqn, rstd_q = _rms_fwd(c_q, w["q_norm_g"], "norm_q")
    ckvn, rstd_kv = _rms_fwd(c_kv, w["kv_norm_g"], "norm_kv")
    q_lin = _mm(cqn, w_uq_p, "mla_q_proj")
    kvb = _mm(ckvn, w_ukv_p, "mla_kv_proj", out_dtype=BF16)
    q_mla = _rope_fwd(_Src(q_lin, MLA_HEADS * 256), cs, sn, MLA_HEADS, 256, LANE, "rope_q")
    k_pe = _rope_fwd(_Src(hp, LANE, PAD["k_rope"] // LANE), cs, sn, 1, LANE, 0, "rope_k")
    mla = _Attn("mla", s, s, MLA_HEADS, 256)
    mla_q, mla_v = _Src(q_mla, 256), _Src(kvb, LANE, MLA_HEADS)
    mla_k = [_Src(kvb, LANE), _Src(k_pe, LANE, 0, False)]
    o_mla, l_mla, lr_mla = _attn_fwd(mla, mla_q, mla_k, mla_v, None, "mla_fwd")

    sk = s // CMP_STRIDE
    pe_k, pe_v = w["cmp_pe_k"], w["cmp_pe_v"]
    w1k = _pad_cols(w["cmp_w1k"], 256)
    w2k = jnp.pad(w["cmp_w2k"], ((0, 64), (0, 64))).astype(BF16)
    w1v, w2v = w["cmp_w1v"], w["cmp_w2v"].astype(BF16)
    half_k, half_v = CMP_STRIDE * NSA_DK, CMP_STRIDE * HEAD_V
    ak = hp[:, PAD["k_c"]:PAD["k_c"] + NSA_DK].reshape(sk, half_k)
    av = hp[:, PAD["v_c"]:PAD["v_c"] + HEAD_V].reshape(sk, half_v)
    ck_args = (ak, _shift_up(ak), pe_k[:CMP_STRIDE].reshape(1, half_k), pe_k[CMP_STRIDE:].reshape(1, half_k),
               w1k[:half_k], w1k[half_k:], w2k)
    cv_args = (av, _shift_up(av), pe_v[:CMP_STRIDE].reshape(1, half_v), pe_v[CMP_STRIDE:].reshape(1, half_v),
               w1v[:half_v], w1v[half_v:], w2v)
    k_cmp, pre_k = _compress_fwd(*ck_args, "compress_k")
    v_cmp, pre_v = _compress_fwd(*cv_args, "compress_v")
    cmp_ = _Attn("cmp", s, sk, NSA_HEADS, 256)
    slc = _Attn("slc", s, s, NSA_HEADS, 256)
    win = _Attn("win", s, s, NSA_HEADS, 256)
    nsa_q = _Src(hpb, 256, PAD["q_nsa"] // 256)
    cmp_k, cmp_v = [_Src(k_cmp, 256, 0, False)], _Src(v_cmp, HEAD_V, 0, False)
    slc_k, slc_v = [_Src(hpb, 256, PAD["k_s"] // 256, False)], _Src(hpb, HEAD_V, PAD["v_s"] // HEAD_V, False)
    win_k, win_v = [_Src(hpb, 256, PAD["k_w"] // 256, False)], _Src(hpb, HEAD_V, PAD["v_w"] // HEAD_V, False)
    o_cmp, l_cmp, lr_cmp = _attn_fwd(cmp_, nsa_q, cmp_k, cmp_v, None, "cmp_fwd")
    sel, selt = _select(cmp_, _Src(hpb, NSA_HEADS * 256, PAD["q_nsa"] // (NSA_HEADS * 256)), k_cmp,
                        _overlap_table(s))
    o_slc, l_slc, lr_slc = _attn_fwd(slc, nsa_q, slc_k, slc_v, sel, "slc_fwd")
    o_win, l_win, lr_win = _attn_fwd(win, nsa_q, win_k, win_v, None, "win_fwd")
    gl = _Src(hp, LANE, PAD["g_nsa"] // LANE)
    o_nsa = _nsa_combine(o_cmp, o_slc, o_win, gl)

    mn, rstd_m = _rms_fwd(_Src(mem, D_MODEL), w["mem_norm_g"], "norm_mem")
    kvm = _mm(mn, w["w_mem_kv"], "mem_kv_proj", out_dtype=BF16)
    mem_ = _Attn("mem", s, mem.shape[0], MEM_HEADS, LANE)
    mem_q, mem_k, mem_v = _Src(hpb, LANE, PAD["q_mem"] // LANE), [_Src(kvm, LANE)], _Src(kvm, LANE, MEM_HEADS)
    o_mem, l_mem, lr_mem = _attn_fwd(mem_, mem_q, mem_k, mem_v, None, "mem_fwd")

    u = _gate_fwd(o_mla, o_nsa, o_mem, hp)
    proj = _mm(u, w["w_out"], "out_proj")
    dy, g_final, loss = _final_loss(x, proj, w["final_norm_g"].reshape(1, -1), target)

    g_w_out = _mm(u, dy, "out_proj_dw", mode="tn")
    du = _mm(dy, w["w_out"], "out_proj_dx", mode="nt")
    do_cat, dz = _gate_bwd(du, o_mla, o_nsa, o_mem, hp)

    dq_mem, (dk_mem,), dv_mem = _attn_bwd(mem_, mem_q, mem_k, mem_v, None, None, _Src(o_mem, HEAD_V), l_mem,
                                          lr_mem, _Src(do_cat, HEAD_V, 12), None, "mem_bwd")
    dkvm = jnp.concatenate([dk_mem, dv_mem], axis=1)
    g_w_mem_kv = _mm(mn, dkvm, "mem_kv_dw", mode="tn")
    dmn = _mm(dkvm, w["w_mem_kv"], "mem_kv_dx", mode="nt")
    _, g_mem_norm = _rms_bwd(_Src(mem, D_MODEL), w["mem_norm_g"], rstd_m, dmn, None, "norm_mem_bwd")

    do_cmp, do_slc, do_win, dgl = _nsa_combine_bwd(do_cat, o_cmp, o_slc, o_win, gl)
    dq_n, (dk_cmp,), dv_cmp = _attn_bwd(cmp_, nsa_q, cmp_k, cmp_v, None, None, _Src(o_cmp, HEAD_V), l_cmp,
                                        lr_cmp, _Src(do_cmp, HEAD_V), None, "cmp_bwd")
    dq_n, (dk_s,), dv_s = _attn_bwd(slc, nsa_q, slc_k, slc_v, sel, selt, _Src(o_slc, HEAD_V), l_slc, lr_slc,
                                    _Src(do_slc, HEAD_V), dq_n, "slc_bwd")
    dq_n, (dk_w,), dv_w = _attn_bwd(win, nsa_q, win_k, win_v, None, None, _Src(o_win, HEAD_V), l_win, lr_win,
                                    _Src(do_win, HEAD_V), dq_n, "win_bwd")
    dak, dpk_lo, dpk_hi, dw1k_lo, dw1k_hi, g_w2k = _compress_bwd(
        *ck_args, pre_k, _shift_down(pre_k), dk_cmp, _shift_down(dk_cmp), "compress_k_bwd")
    dav, dpv_lo, dpv_hi, dw1v_lo, dw1v_hi, g_w2v = _compress_bwd(
        *cv_args, pre_v, _shift_down(pre_v), dv_cmp, _shift_down(dv_cmp), "compress_v_bwd")
    g_pe_k = jnp.concatenate([dpk_lo.reshape(CMP_STRIDE, NSA_DK), dpk_hi.reshape(CMP_STRIDE, NSA_DK)], axis=0)
    g_pe_v = jnp.concatenate([dpv_lo.reshape(CMP_STRIDE, HEAD_V), dpv_hi.reshape(CMP_STRIDE, HEAD_V)], axis=0)
    g_w1k = jnp.concatenate([dw1k_lo, dw1k_hi], axis=0)[:, :NSA_DK]
    g_w1v = jnp.concatenate([dw1v_lo, dw1v_hi], axis=0)
    dk_c = _pad_cols(dak.reshape(s, NSA_DK), 256)
    dv_c = dav.reshape(s, HEAD_V)

    dq_m, (dk_nope, dk_pe), dv_m = _attn_bwd(mla, mla_q, mla_k, mla_v, None, None, _Src(o_mla, HEAD_V), l_mla,
                                             lr_mla, _Src(do_cat, HEAD_V), None, "mla_bwd")
    dq_lin = _rope_bwd_q(dq_m, cs, sn)
    dkv_lin, d_krope = _rope_bwd_k(dk_nope, dk_pe, dv_m, cs, sn)
    g_w_uq_p = _mm(cqn, dq_lin, "mla_q_dw", mode="tn")
    dcqn = _mm(dq_lin, w_uq_p, "mla_q_dx", mode="nt")
    g_w_ukv_p = _mm(ckvn, dkv_lin, "mla_kv_dw", mode="tn")
    dckvn = _mm(dkv_lin, w_ukv_p, "mla_kv_dx", mode="nt")
    dc_q, g_q_norm = _rms_bwd(c_q, w["q_norm_g"], rstd_q, dcqn, None, "norm_q_bwd")
    dc_kv, g_kv_norm = _rms_bwd(c_kv, w["kv_norm_g"], rstd_kv, dckvn, None, "norm_kv_bwd")
    g_w_uq = g_w_uq_p.reshape(512, MLA_HEADS, 256)[:, :, :192].reshape(512, MLA_HEADS * 192)
    g_w_ukv = t_(g_w_ukv_p.reshape(512, 2, MLA_HEADS, 128), (0, 2, 1, 3)).reshape(512, 2048)

    dhp = jnp.concatenate(
        [dc_q, dc_kv, dq_n, dk_c, dk_s, dk_w, d_krope, dv_c, dv_s, dv_w, dgl, dq_mem,
         jnp.zeros((s, PAD["z"] - (PAD["q_mem"] + 512)), F32), dz], axis=1)
    g_w_in_t = _w_in_unpadded(_mm(dhp, xn, "in_proj_dw", mode="tn"))
    dxn = _mm(dhp, w_in_p, "in_proj_dx")
    grad_x, g_norm = _rms_bwd(_Src(x, D_MODEL), w["norm_g"], rstd_x, dxn, dy, "norm_x_bwd")

    grads = dict(norm_g=g_norm, w_in_t=g_w_in_t, q_norm_g=g_q_norm, w_uq=g_w_uq, kv_norm_g=g_kv_norm,
                 w_ukv=g_w_ukv, cmp_pe_k=g_pe_k, cmp_pe_v=g_pe_v, cmp_w1k=g_w1k, cmp_w2k=g_w2k[:NSA_DK, :NSA_DK],
                 cmp_w1v=g_w1v, cmp_w2v=g_w2v, mem_norm_g=g_mem_norm, w_mem_kv=g_w_mem_kv, w_out=g_w_out,
                 final_norm_g=g_final.reshape(-1))
    return loss[0, 0], grad_x, grads


def kernel(x, mem, norm_g, w_in, q_norm_g, w_uq, kv_norm_g, w_ukv, cmp_pe_k, cmp_pe_v, cmp_w1k, cmp_w2k, cmp_w1v, cmp_w2v, mem_norm_g, w_mem_kv, w_out, final_norm_g, loss_target, m_norm_g, m_w_in, m_q_norm_g, m_w_uq, m_kv_norm_g, m_w_ukv, m_cmp_pe_k, m_cmp_pe_v, m_cmp_w1k, m_cmp_w2k, m_cmp_w1v, m_cmp_w2v, m_mem_norm_g, m_w_mem_kv, m_w_out, m_final_norm_g, v_norm_g, v_w_in, v_q_norm_g, v_w_uq, v_kv_norm_g, v_w_ukv, v_cmp_pe_k, v_cmp_pe_v, v_cmp_w1k, v_cmp_w2k, v_cmp_w1v, v_cmp_w2v, v_mem_norm_g, v_w_mem_kv, v_w_out, v_final_norm_g):
    args = dict(locals())
    wts = {n: args[n] for n in WEIGHTS}
    loc = {n: (a if n == "final_norm_g" else a[0]) for n, a in wts.items()}

    def to_x(n, a):
        return a.T if n == "w_in" else a

    split = [1 if n == "w_in" else 0 for n in SHARDED]

    own = [to_x(n, loc[n]).astype(BF16) for n in SHARDED]
    chip = 2 * lax.axis_index("x") + lax.axis_index("y")
    gathered = [lax.dynamic_update_slice(gw, a[None], (chip, 0, 0))
                for gw, a in zip(_gather_shards(own, split), own)]
    full = {n: loc[n].reshape(1, -1) if loc[n].ndim == 1 else loc[n] for n in REPLICATED}
    for n, gw in zip(SHARDED, gathered):
        if n == "w_in":
            full["w_in_t"] = gw.reshape(4 * gw.shape[1], gw.shape[2])
        elif SHARD_AXIS[n] == 0:
            full[n] = gw.reshape(4 * gw.shape[1], gw.shape[2])
        else:
            full[n] = jnp.concatenate([gw[j] for j in range(4)], axis=1)

    loss, grad_x, g = _local_step(x[0], mem[0], loss_target[0], full)
    loss = lax.psum(loss, ("x", "y", "c"))

    def slots(n):
        a = g["w_in_t"] if n == "w_in" else g[n]
        if n == "w_in" or SHARD_AXIS[n] == 0:
            return a.reshape(4, a.shape[0] // 4, a.shape[1])
        width = a.shape[1] // 4
        return jnp.stack([a[:, j * width:(j + 1) * width] for j in range(4)])

    gs = [slots(n) for n in SHARDED]
    core = lax.axis_index("c").astype(jnp.int32).reshape(1)
    theirs = _pair_exchange(gs, split)
    pairs = [_pair_sum(a, b, core, ax, "pair_sum_" + n) for n, a, b, ax in zip(SHARDED, gs, theirs, split)]
    from_chips = _chip_exchange(pairs)
    mine = [_sum_slots(b, "chip_sum_" + n) for n, b in zip(SHARDED, from_chips)]
    g_sh = [lax.dynamic_update_slice(o, t, (core[0] * t.shape[0], 0) if ax == 0 else (0, core[0] * t.shape[1]))
            for o, t, ax in zip(_half_exchange(mine, split), mine, split)]

    n_rep = sum(int(np.prod(loc[n].shape)) for n in REPLICATED)
    rows_rep = -(-n_rep // (8 * LANE)) * 8

    def rep_pack(parts):
        flat = jnp.concatenate([p.reshape(-1) for p in parts])
        return jnp.pad(flat, (0, rows_rep * LANE - n_rep)).reshape(rows_rep, LANE)

    g_rep = _sum_slots(_gather_all(rep_pack([g[n] for n in REPLICATED])), "replica_sum")
    d_rp, m_rp, v_rp = _adamw(rep_pack([wts[n] for n in REPLICATED]), g_rep,
                              rep_pack([args["m_" + n] for n in REPLICATED]),
                              rep_pack([args["v_" + n] for n in REPLICATED]), "adamw_replicated")

    def rep_unpack(buf):
        flat, out, o = buf.reshape(-1), {}, 0
        for n in REPLICATED:
            size = int(np.prod(wts[n].shape))
            out[n] = flat[o:o + size].reshape(wts[n].shape)
            o += size
        return out

    outs = {k: rep_unpack(b) for k, b in (("g", g_rep), ("d", d_rp), ("m", m_rp), ("v", v_rp))}
    for n, gn in zip(SHARDED, g_sh):
        d, mo, vo = _adamw(to_x(n, loc[n]), gn, to_x(n, args["m_" + n][0]), to_x(n, args["v_" + n][0]),
                           "adamw_" + n)
        for k, a in (("g", gn), ("d", d), ("m", mo), ("v", vo)):
            outs[k][n] = to_x(n, a).reshape(wts[n].shape)

    return (loss, grad_x[None], *[outs["g"][n] for n in WEIGHTS], *[outs["d"][n] for n in WEIGHTS],
            *[outs["m"][n] for n in WEIGHTS], *[outs["v"][n] for n in WEIGHTS])
```

```python
from typing import NamedTuple

import numpy as np
import jax
import jax.numpy as jnp
from jax import lax
from jax.experimental import pallas as pl
from jax.experimental.pallas import tpu as pltpu

F32 = jnp.float32
BF16 = jnp.bfloat16
MESH = pl.DeviceIdType.MESH

D_MODEL = 2048
EPS = 1e-6
LANE = 128
HEAD_V = 128
MLA_HEADS = 8
NSA_HEADS = 4
MEM_HEADS = 4
NSA_DK = 192
CMP_STRIDE = 16
CMP_LEN = 32
SLC_LEN = 64
SLC_TOPN = 16
WIN = 512
NEG = -1e30
LOG2E = 1.4426950408889634
ROPE_THETA = 10000.0
BLOCK_BYTES = 2 << 20

ORIG = dict(c_q=(0, 512), c_kv=(512, 512), k_rope=(1024, 64), z_mla=(1088, 1024),
            q_nsa=(2112, 768), k_c=(2880, 192), v_c=(3072, 128), k_s=(3200, 192),
            v_s=(3392, 128), k_w=(3520, 192), v_w=(3712, 128), g_nsa=(3840, 12),
            z_nsa=(3852, 512), q_mem=(4364, 512), z_mem=(4876, 512))
PAD = dict(c_q=0, c_kv=512, q_nsa=1024, k_c=2048, k_s=2304, k_w=2560, k_rope=2816, v_c=2944,
           v_s=3072, v_w=3200, g_nsa=3328, q_mem=3456, z=4096)
D_PAD = 6144

ADAM_LR, ADAM_B1, ADAM_B2, ADAM_EPS, ADAM_WD, ADAM_STEP = 0.001, 0.9, 0.999, 1e-08, 0.01, 10

SHARDED = ("w_in", "w_uq", "w_ukv", "cmp_w1k", "cmp_w1v", "w_mem_kv", "w_out")
SHARD_AXIS = dict(w_in=1, w_uq=1, w_ukv=1, cmp_w1k=0, cmp_w1v=0, w_mem_kv=0, w_out=0)
REPLICATED = ("norm_g", "q_norm_g", "kv_norm_g", "cmp_pe_k", "cmp_pe_v", "cmp_w2k", "cmp_w2v",
              "mem_norm_g", "final_norm_g")
WEIGHTS = ("norm_g", "w_in", "q_norm_g", "w_uq", "kv_norm_g", "w_ukv", "cmp_pe_k", "cmp_pe_v",
           "cmp_w1k", "cmp_w2k", "cmp_w1v", "cmp_w2v", "mem_norm_g", "w_mem_kv", "w_out",
           "final_norm_g")


def _pcall(kernel, **kw):
    return pl.pallas_call(kernel, **kw)


def _tile(n, pref):
    if n <= pref:
        return n
    for t in range(pref, LANE - 1, -LANE):
        if n % t == 0:
            return t
    raise ValueError((n, pref))


def _row_tile(rows, cols, itemsize=4):
    want = max(16, BLOCK_BYTES // (cols * itemsize))
    if rows <= want:
        return rows
    t = 16
    best = rows
    while t <= want:
        if rows % t == 0:
            best = t
        t *= 2
    return best


def _nt(a, b):
    return lax.dot_general(a, b, (((1,), (1,)), ((), ())), preferred_element_type=F32)


def _tn(a, b):
    return lax.dot_general(a, b, (((0,), (0,)), ((), ())), preferred_element_type=F32)


def _nn(a, b):
    return jnp.dot(a, b, preferred_element_type=F32)


def _sigmoid(x):
    return 1.0 / (1.0 + jnp.exp(-x))


class _Src(NamedTuple):
    arr: jax.Array
    width: int
    col0: int = 0
    per_head: bool = True

    def col(self, h):
        return self.col0 + h if self.per_head else self.col0


def _mm(a, b, name, mode="nn", out_dtype=F32, second_dtype=None):
    if mode == "tn":
        k, m = a.shape
    else:
        m, k = a.shape
    if mode == "nt":
        n, k2 = b.shape
    else:
        k2, n = b.shape
    assert k == k2, (a.shape, b.shape, mode)
    tm, tn, tk = _tile(m, 1024), _tile(n, 1024), _tile(k, 2048)
    nk = k // tk
    assert nk == 1 or (out_dtype == F32 and second_dtype is None)
    dot = {"nn": _nn, "nt": _nt, "tn": _tn}[mode]

    def kern(a_ref, b_ref, o_ref, *more):
        r = dot(a_ref[...].astype(BF16), b_ref[...].astype(BF16))
        if nk == 1:
            o_ref[...] = r.astype(out_dtype)
            if more:
                more[0][...] = r.astype(second_dtype)
        else:
            kk = pl.program_id(2)

            @pl.when(kk == 0)
            def _():
                o_ref[...] = r

            @pl.when(kk > 0)
            def _():
                o_ref[...] += r

    a_spec = (pl.BlockSpec((tk, tm), lambda i, j, kk: (kk, i)) if mode == "tn"
              else pl.BlockSpec((tm, tk), lambda i, j, kk: (i, kk)))
    b_spec = (pl.BlockSpec((tn, tk), lambda i, j, kk: (j, kk)) if mode == "nt"
              else pl.BlockSpec((tk, tn), lambda i, j, kk: (kk, j)))
    o_spec = pl.BlockSpec((tm, tn), lambda i, j, kk: (i, j))
    out_shape = jax.ShapeDtypeStruct((m, n), out_dtype)
    if second_dtype is not None:
        o_spec = [o_spec, o_spec]
        out_shape = [out_shape, jax.ShapeDtypeStruct((m, n), second_dtype)]
    return _pcall(
        kern, name=name, grid=(m // tm, n // tn, nk), in_specs=[a_spec, b_spec], out_specs=o_spec,
        out_shape=out_shape,
        compiler_params=pltpu.CompilerParams(dimension_semantics=("parallel", "parallel", "arbitrary")),
    )(a, b)


def _rms_fwd(x, g, name):
    r, d = x.arr.shape[0], x.width
    tr = _tile(r, 512)

    def kern(x_ref, g_ref, y_ref, r_ref):
        xv = x_ref[...]
        rstd = lax.rsqrt(jnp.mean(xv * xv, axis=-1, keepdims=True) + EPS)
        y_ref[...] = (xv * rstd * g_ref[...]).astype(BF16)
        r_ref[...] = rstd

    return _pcall(
        kern, name=name, grid=(r // tr,),
        in_specs=[pl.BlockSpec((tr, d), lambda i: (i, x.col0)), pl.BlockSpec((1, d), lambda i: (0, 0))],
        out_specs=[pl.BlockSpec((tr, d), lambda i: (i, 0)), pl.BlockSpec((tr, 1), lambda i: (i, 0))],
        out_shape=[jax.ShapeDtypeStruct((r, d), BF16), jax.ShapeDtypeStruct((r, 1), F32)],
    )(x.arr, g)


def _rms_bwd(x, g, rstd, dy, add, name):
    r, d = x.arr.shape[0], x.width
    tr = _tile(r, 256)
    has_add = add is not None

    def kern(*refs):
        if has_add:
            x_ref, g_ref, r_ref, dy_ref, add_ref, dx_ref, dg_ref = refs
        else:
            x_ref, g_ref, r_ref, dy_ref, dx_ref, dg_ref = refs
        rs = r_ref[...]
        xhat = x_ref[...] * rs
        dyv = dy_ref[...]
        dyg = dyv * g_ref[...]
        c = jnp.mean(dyg * xhat, axis=-1, keepdims=True)
        dx = rs * (dyg - xhat * c)
        if has_add:
            dx = dx + add_ref[...]
        dx_ref[...] = dx
        part = jnp.sum(dyv * xhat, axis=0, keepdims=True)

        @pl.when(pl.program_id(0) == 0)
        def _():
            dg_ref[...] = part

        @pl.when(pl.program_id(0) > 0)
        def _():
            dg_ref[...] += part

    row = pl.BlockSpec((tr, d), lambda i: (i, 0))
    vec = pl.BlockSpec((1, d), lambda i: (0, 0))
    ins = [pl.BlockSpec((tr, d), lambda i: (i, x.col0)), vec, pl.BlockSpec((tr, 1), lambda i: (i, 0)), row]
    ins += [row] if has_add else []
    args = (x.arr, g, rstd, dy) + ((add,) if has_add else ())
    return _pcall(
        kern, name=name, grid=(r // tr,), in_specs=ins, out_specs=[row, vec],
        out_shape=[jax.ShapeDtypeStruct((r, d), F32), jax.ShapeDtypeStruct((1, d), F32)],
        compiler_params=pltpu.CompilerParams(dimension_semantics=("arbitrary",)),
    )(*args)


def _final_loss(x, proj, g, target):
    r, d = x.shape
    tr = _tile(r, 256)

    def kern(x_ref, p_ref, g_ref, t_ref, dy_ref, dg_ref, loss_ref):
        y = x_ref[...] + p_ref[...]
        rs = lax.rsqrt(jnp.mean(y * y, axis=-1, keepdims=True) + EPS)
        yhat = y * rs
        gv = g_ref[...]
        e = yhat * gv - t_ref[...]
        lpart = 0.5 * jnp.sum(jnp.mean(e * e, axis=-1, keepdims=True), axis=0, keepdims=True)
        dout = e * (1.0 / d)
        dyg = dout * gv
        c = jnp.mean(dyg * yhat, axis=-1, keepdims=True)
        dy_ref[...] = rs * (dyg - yhat * c)
        gpart = jnp.sum(dout * yhat, axis=0, keepdims=True)
        lrow = jnp.broadcast_to(lpart, (1, LANE))

        @pl.when(pl.program_id(0) == 0)
        def _():
            dg_ref[...] = gpart
            loss_ref[...] = lrow

        @pl.when(pl.program_id(0) > 0)
        def _():
            dg_ref[...] += gpart
            loss_ref[...] += lrow

    row = pl.BlockSpec((tr, d), lambda i: (i, 0))
    vec = pl.BlockSpec((1, d), lambda i: (0, 0))
    return _pcall(
        kern, name="final_loss", grid=(r // tr,), in_specs=[row, row, vec, row],
        out_specs=[row, vec, pl.BlockSpec((1, LANE), lambda i: (0, 0))],
        out_shape=[jax.ShapeDtypeStruct((r, d), F32), jax.ShapeDtypeStruct((1, d), F32),
                   jax.ShapeDtypeStruct((1, LANE), F32)],
        compiler_params=pltpu.CompilerParams(dimension_semantics=("arbitrary",)),
    )(x, proj, g, target)


def _rope_fwd(x, cs, sn, nh, width, off, name):
    s = x.arr.shape[0]
    tr = _tile(s, 512)

    def kern(x_ref, c_ref, s_ref, o_ref):
        cv, sv = c_ref[...], s_ref[...]
        for h in range(nh):
            b = h * width
            if off:
                o_ref[:, b:b + off] = x_ref[:, b:b + off].astype(BF16)
            xr = x_ref[:, b + off:b + off + LANE]
            o_ref[:, b + off:b + off + LANE] = (xr * cv + pltpu.roll(xr, 32, 1) * sv).astype(BF16)

    tab = pl.BlockSpec((tr, LANE), lambda i: (i, 0))
    return _pcall(
        kern, name=name, grid=(s // tr,),
        in_specs=[pl.BlockSpec((tr, nh * width), lambda i: (i, x.col0)), tab, tab],
        out_specs=pl.BlockSpec((tr, nh * width), lambda i: (i, 0)),
        out_shape=jax.ShapeDtypeStruct((s, nh * width), BF16),
    )(x.arr, cs, sn)


def _rope_grad(d, cv, sv):
    g2 = d * sv
    g2 = g2 + pltpu.roll(g2, 64, 1)
    lane = lax.broadcasted_iota(jnp.int32, d.shape, 1)
    return jnp.where(lane < 64, d * cv + pltpu.roll(g2, 32, 1), 0.0)


def _rope_bwd_q(dq, cs, sn):
    s, w = dq.shape
    tr = _tile(s, 512)
    nh = w // 256

    def kern(d_ref, c_ref, s_ref, o_ref):
        cv, sv = c_ref[...], s_ref[...]
        for h in range(nh):
            b = h * 256
            o_ref[:, b:b + LANE] = d_ref[:, b:b + LANE]
            o_ref[:, b + LANE:b + 256] = _rope_grad(d_ref[:, b + LANE:b + 256], cv, sv)

    row = pl.BlockSpec((tr, w), lambda i: (i, 0))
    tab = pl.BlockSpec((tr, LANE), lambda i: (i, 0))
    return _pcall(kern, name="rope_bwd_q", grid=(s // tr,), in_specs=[row, tab, tab], out_specs=row,
                  out_shape=jax.ShapeDtypeStruct((s, w), F32))(dq, cs, sn)


def _rope_bwd_k(dk_nope, dk_pe, dv, cs, sn):
    s, w = dk_nope.shape
    tr = _tile(s, 512)

    def kern(dk_ref, dp_ref, dv_ref, c_ref, s_ref, okv_ref, okr_ref):
        okv_ref[:, :w] = dk_ref[...]
        okv_ref[:, w:] = dv_ref[...]
        okr_ref[...] = _rope_grad(dp_ref[...], c_ref[...], s_ref[...])

    tab = pl.BlockSpec((tr, LANE), lambda i: (i, 0))
    wide = pl.BlockSpec((tr, w), lambda i: (i, 0))
    return _pcall(
        kern, name="rope_bwd_k", grid=(s // tr,), in_specs=[wide, tab, wide, tab, tab],
        out_specs=[pl.BlockSpec((tr, 2 * w), lambda i: (i, 0)), tab],
        out_shape=[jax.ShapeDtypeStruct((s, 2 * w), F32), jax.ShapeDtypeStruct((s, LANE), F32)],
    )(dk_nope, dk_pe, dv, cs, sn)


class _Attn:
    def __init__(self, mode, s, sk, heads, dk):
        self.mode, self.s, self.sk, self.h, self.dk = mode, s, sk, heads, dk
        self.scale = {"mla": 192 ** -0.5, "mem": 128 ** -0.5}.get(mode, NSA_DK ** -0.5)
        self.tb = min(256, s)
        self.nb = s // self.tb
        self.nsub = 2 if self.nb % 2 == 0 else 1
        self.tq = self.tb * self.nsub
        self.nq = s // self.tq
        self.causal = mode in ("mla", "slc")
        if self.causal:
            self.tk = self.tq
        elif mode == "win":
            self.tk = WIN + self.tb
        else:
            self.tk = sk
        self.tkb = min(512, sk)
        self.ksub = 2 if self.tkb == 512 and mode == "mla" else 1
        self.kb = self.tkb // self.ksub
        self.ncmp = s // CMP_STRIDE - 1

    def mask_bias(self, t, n, h, selx, diag):
        m = self.mode
        if m == "mla":
            return (n <= t) if diag else None, None
        if m == "mem":
            return None, None
        slope = jnp.where(h == 0, 0.25, jnp.where(h == 1, 0.0625, jnp.where(h == 2, 0.015625, 0.00390625)))
        slope = slope.astype(F32) * LOG2E
        if m == "cmp":
            mask = (n * CMP_STRIDE + (CMP_LEN - 1) <= t) & (n < self.ncmp)
            pos = n.astype(F32) * float(CMP_STRIDE) + (CMP_LEN - 1) / 2.0
            return mask, slope * pos
        rel = t - n
        if m == "slc":
            return (rel >= 0) if diag else None, slope * n.astype(F32)
        return (rel >= 0) & (rel < WIN), slope * n.astype(F32)


def _scores(cfg, s_raw, t, n, h, selx, diag, lse=None):
    s = s_raw * (cfg.scale * LOG2E)
    mask, key_term = cfg.mask_bias(t, n, h, selx, diag)
    if key_term is not None:
        s = s + key_term
    if selx is not None:
        s = s + selx
    if lse is None:
        if mask is not None:
            s = jnp.where(mask, s, NEG)
        return s, mask
    p = jnp.exp2(jnp.minimum(s - lse, 0.0))
    if mask is not None:
        p = jnp.where(mask, p, 0.0)
    return p, mask


def _block_of_key(k0, tk, keys_on_rows):
    shape = (tk, LANE) if keys_on_rows else (LANE, tk)
    n = lax.broadcasted_iota(jnp.int32, shape, 0 if keys_on_rows else 1) + k0
    j = lax.broadcasted_iota(jnp.int32, shape, 1 if keys_on_rows else 0)
    return jnp.where((n >> 6) == j, NEG, 0.0).astype(BF16)


def _to_row(col):
    t = col.shape[0]
    return jnp.transpose(jnp.broadcast_to(col, (t, LANE)))[0:1, :]


def _load_keys(k_refs, rows):
    parts = [r[rows, :].astype(BF16) for r in k_refs]
    return parts[0] if len(parts) == 1 else jnp.concatenate(parts, axis=1)


def _attn_fwd(cfg, q, ks, v, sel, name):
    s, tq, tk, tb, nsub = cfg.s, cfg.tq, cfg.tk, cfg.tb, cfg.nsub
    has_sel = sel is not None
    nkp = len(ks)

    def kern(*refs):
        q_ref, k_refs, v_ref = refs[0], refs[1:1 + nkp], refs[1 + nkp]
        sel_ref = refs[2 + nkp] if has_sel else None
        o_ref, lc_ref, lr_ref = refs[2 + nkp + has_sel:5 + nkp + has_sel]
        h, i = pl.program_id(0), pl.program_id(1)
        part = [slice(r * tb, (r + 1) * tb) for r in range(nsub)]
        qs = [q_ref[p, :].astype(BF16) for p in part]
        ts = [i * tq + r * tb + lax.broadcasted_iota(jnp.int32, (tb, 1), 0) for r in range(nsub)]
        sels = [sel_ref[p, :].astype(BF16) for p in part] if has_sel else None

        def load(k0):
            rows = pl.ds(k0, tk)
            return _load_keys(k_refs, rows), v_ref[rows, :].astype(BF16)

        def soft(r, k0, s_raw, vv, emat, carry, diag):
            m, l, acc = carry
            n = k0 + lax.broadcasted_iota(jnp.int32, (1, tk), 1)
            selx = _nn(sels[r], emat) if has_sel else None
            sc, mask = _scores(cfg, s_raw, ts[r], n, h, selx, diag)
            m_new = jnp.maximum(m, jnp.max(sc, axis=1, keepdims=True))
            alpha = jnp.exp2(m - m_new)
            p = jnp.exp2(sc - m_new)
            if mask is not None:
                p = jnp.where(mask, p, 0.0)
            l = alpha * l + jnp.sum(p, axis=1, keepdims=True)
            acc = alpha * acc + _nn(p.astype(BF16), vv)
            return m_new, l, acc

        def step(r, k0, kk, vv, emat, carry, diag):
            return soft(r, k0, _nt(qs[r], kk), vv, emat, carry, diag)

        def chunk(k0, carry, diag):
            kk, vv = load(k0)
            emat = _block_of_key(k0, tk, False) if has_sel else None
            return tuple(step(r, k0, kk, vv, emat, carry[r], diag) for r in range(nsub))

        init = (jnp.full((tb, 1), NEG, F32), jnp.zeros((tb, 1), F32), jnp.zeros((tb, HEAD_V), F32))
        carry = (init,) * nsub
        if cfg.causal:
            buf_a, buf_b = refs[-2:]
            full = (i * tq) // tk

            def scores_into(buf, c):
                kk = _load_keys(k_refs, pl.ds(pl.multiple_of(c * tk, tk), tk))
                for r in range(nsub):
                    buf[r] = _nt(qs[r], kk)

            def consume(buf, c, cr, diag):
                k0 = pl.multiple_of(c * tk, tk)
                vv = v_ref[pl.ds(k0, tk), :].astype(BF16)
                emat = _block_of_key(k0, tk, False) if has_sel else None
                return tuple(soft(r, k0, buf[r], vv, emat, cr[r], diag) for r in range(nsub))

            def pair(p, cr):
                scores_into(buf_b, 2 * p + 1)
                cr = consume(buf_a, 2 * p, cr, False)
                scores_into(buf_a, 2 * p + 2)
                return consume(buf_b, 2 * p + 1, cr, False)

            def odd_tail(cr):
                scores_into(buf_b, full)
                cr = consume(buf_a, full - 1, cr, False)
                return consume(buf_b, full, cr, True)

            scores_into(buf_a, 0)
            carry = lax.fori_loop(0, full // 2, pair, carry)
            carry = lax.cond(full % 2 == 1, odd_tail, lambda cr: consume(buf_a, full, cr, True), carry)
        elif cfg.mode == "win":
            starts = [pl.multiple_of(jnp.maximum(i * tq + r * tb - WIN, 0), tb) for r in range(nsub)]
            carry = tuple(step(r, k0, *load(k0), None, carry[r], True) for r, k0 in enumerate(starts))
        else:
            carry = chunk(0, carry, True)
        for r, (m, l, acc) in enumerate(carry):
            o_ref[part[r], :] = acc / (l + 1e-20)
            lse = m + jnp.log(l + 1e-20) * LOG2E
            lc_ref[0, part[r], :] = lse
            lr_ref[0, r] = _to_row(lse)

    ins = [pl.BlockSpec((tq, q.width), lambda h, i: (i, q.col(h)))]
    ins += [pl.BlockSpec((cfg.sk, p.width), lambda h, i, p=p: (0, p.col(h))) for p in ks]
    ins += [pl.BlockSpec((cfg.sk, HEAD_V), lambda h, i: (0, v.col(h)))]
    args = [q.arr] + [p.arr for p in ks] + [v.arr]
    if has_sel:
        ins.append(pl.BlockSpec((tq, LANE), lambda h, i: (i, 0)))
        args.append(sel)
    return _pcall(
        kern, name=name, grid=(cfg.h, cfg.nq), in_specs=ins,
        out_specs=[pl.BlockSpec((tq, HEAD_V), lambda h, i: (i, h)),
                   pl.BlockSpec((1, tq, 1), lambda h, i: (h, i, 0)),
                   pl.BlockSpec((1, nsub, 1, tb), lambda h, i: (h, i, 0, 0))],
        out_shape=[jax.ShapeDtypeStruct((s, cfg.h * HEAD_V), F32),
                   jax.ShapeDtypeStruct((cfg.h, s, 1), F32),
                   jax.ShapeDtypeStruct((cfg.h, cfg.nb, 1, tb), F32)],
        scratch_shapes=[pltpu.VMEM((nsub, tb, tk), F32)] * 2 if cfg.causal else [],
        compiler_params=pltpu.CompilerParams(dimension_semantics=("parallel", "parallel")),
    )(*args)


def _attn_dq(cfg, q, ks, v, sel, o, lse, do, dq_in, name):
    s, tq, tk, dk, tb, nsub = cfg.s, cfg.tq, cfg.tk, cfg.dk, cfg.tb, cfg.nsub
    has_sel = sel is not None
    has_in = dq_in is not None
    nkp = len(ks)

    def kern(*refs):
        refs = list(refs)
        q_ref, k_refs, v_ref = refs[0], refs[1:1 + nkp], refs[1 + nkp]
        p0 = 2 + nkp
        sel_ref = refs[p0] if has_sel else None
        p0 += has_sel
        o_ref, l_ref, do_ref = refs[p0:p0 + 3]
        p0 += 3
        in_ref = refs[p0] if has_in else None
        p0 += has_in
        dq_ref, dr_ref = refs[p0:p0 + 2]
        h, i = pl.program_id(0), pl.program_id(1)
        part = [slice(r * tb, (r + 1) * tb) for r in range(nsub)]
        qs = [q_ref[p, :].astype(BF16) for p in part]
        ts = [i * tq + r * tb + lax.broadcasted_iota(jnp.int32, (tb, 1), 0) for r in range(nsub)]
        sels = [sel_ref[p, :].astype(BF16) for p in part] if has_sel else None
        dvecs, dobs, lses = [], [], []
        for r, p in enumerate(part):
            dov = do_ref[p, :]
            dvec = jnp.sum(dov * o_ref[p, :], axis=1, keepdims=True)
            dr_ref[0, r] = _to_row(dvec)
            dvecs.append(dvec)
            dobs.append(dov.astype(BF16))
            lses.append(l_ref[0, p, :])

        def load(k0):
            rows = pl.ds(k0, tk)
            return _load_keys(k_refs, rows), v_ref[rows, :].astype(BF16)

        def grad(r, k0, s_raw, dp, kk, emat, acc, diag):
            n = k0 + lax.broadcasted_iota(jnp.int32, (1, tk), 1)
            selx = _nn(sels[r], emat) if has_sel else None
            p, _ = _scores(cfg, s_raw, ts[r], n, h, selx, diag, lses[r])
            ds = p * (dp - dvecs[r])
            return acc + _nn(ds.astype(BF16), kk)

        def step(r, k0, kk, vv, emat, acc, diag):
            return grad(r, k0, _nt(qs[r], kk), _nt(dobs[r], vv), kk, emat, acc, diag)

        def chunk(k0, accs, diag):
            kk, vv = load(k0)
            emat = _block_of_key(k0, tk, False) if has_sel else None
            return tuple(step(r, k0, kk, vv, emat, accs[r], diag) for r in range(nsub))

        accs = (jnp.zeros((tb, dk), F32),) * nsub
        if cfg.causal:
            sa, pa, sb, pb = refs[-4:]
            full = (i * tq) // tk

            def products_into(sbuf, pbuf, c):
                kk, vv = load(pl.multiple_of(c * tk, tk))
                for r in range(nsub):
                    sbuf[r] = _nt(qs[r], kk)
                    pbuf[r] = _nt(dobs[r], vv)

            def consume(sbuf, pbuf, c, ac, diag):
                k0 = pl.multiple_of(c * tk, tk)
                kk = _load_keys(k_refs, pl.ds(k0, tk))
                emat = _block_of_key(k0, tk, False) if has_sel else None
                return tuple(grad(r, k0, sbuf[r], pbuf[r], kk, emat, ac[r], diag) for r in range(nsub))

            def pair(p, ac):
                products_into(sb, pb, 2 * p + 1)
                ac = consume(sa, pa, 2 * p, ac, False)
                products_into(sa, pa, 2 * p + 2)
                return consume(sb, pb, 2 * p + 1, ac, False)

            def odd_tail(ac):
                products_into(sb, pb, full)
                ac = consume(sa, pa, full - 1, ac, False)
                return consume(sb, pb, full, ac, True)

            products_into(sa, pa, 0)
            accs = lax.fori_loop(0, full // 2, pair, accs)
            accs = lax.cond(full % 2 == 1, odd_tail, lambda ac: consume(sa, pa, full, ac, True), accs)
        elif cfg.mode == "win":
            starts = [pl.multiple_of(jnp.maximum(i * tq + r * tb - WIN, 0), tb) for r in range(nsub)]
            accs = tuple(step(r, k0, *load(k0), None, accs[r], True) for r, k0 in enumerate(starts))
        else:
            accs = chunk(0, accs, True)
        for r, p in enumerate(part):
            dq_ref[p, :] = accs[r] * cfg.scale + in_ref[p, :] if has_in else accs[r] * cfg.scale

    qs = pl.BlockSpec((tq, dk), lambda h, i: (i, h))
    ins = [pl.BlockSpec((tq, q.width), lambda h, i: (i, q.col(h)))]
    ins += [pl.BlockSpec((cfg.sk, p.width), lambda h, i, p=p: (0, p.col(h))) for p in ks]
    ins += [pl.BlockSpec((cfg.sk, HEAD_V), lambda h, i: (0, v.col(h)))]
    args = [q.arr] + [p.arr for p in ks] + [v.arr]
    if has_sel:
        ins.append(pl.BlockSpec((tq, LANE), lambda h, i: (i, 0)))
        args.append(sel)
    ins += [pl.BlockSpec((tq, HEAD_V), lambda h, i: (i, o.col(h))),
            pl.BlockSpec((1, tq, 1), lambda h, i: (h, i, 0)),
            pl.BlockSpec((tq, HEAD_V), lambda h, i: (i, do.col(h)))]
    args += [o.arr, lse, do.arr]
    if has_in:
        ins.append(qs)
        args.append(dq_in)
    return _pcall(
        kern, name=name, grid=(cfg.h, cfg.nq), in_specs=ins,
        out_specs=[qs, pl.BlockSpec((1, nsub, 1, tb), lambda h, i: (h, i, 0, 0))],
        out_shape=[jax.ShapeDtypeStruct((s, cfg.h * dk), F32),
                   jax.ShapeDtypeStruct((cfg.h, cfg.nb, 1, tb), F32)],
        scratch_shapes=[pltpu.VMEM((nsub, tb, tk), F32)] * 4 if cfg.causal else [],
        compiler_params=pltpu.CompilerParams(dimension_semantics=("parallel", "parallel")),
    )(*args)


def _attn_dkv(cfg, q, ks, v, selt, lse_r, d_r, do, name):
    s, tq, tkb, dk, kb, ksub = cfg.s, cfg.tb, cfg.tkb, cfg.dk, cfg.kb, cfg.ksub
    nq = cfg.nb
    has_sel = selt is not None
    nkp = len(ks)
    outs = list(ks) + [v]

    def kern(*refs):
        k_refs, v_ref = refs[:nkp], refs[nkp]
        q_ref, do_ref, lr_ref, dr_ref = refs[nkp + 1:nkp + 5]
        st_ref = refs[nkp + 5] if has_sel else None
        out_refs = refs[nkp + 5 + has_sel:2 * nkp + 6 + has_sel]
        sa, pa, sb, pb = refs[-4:]
        j, h = pl.program_id(0), pl.program_id(1)
        k0 = j * tkb
        part = [slice(u * kb, (u + 1) * kb) for u in range(ksub)]
        kks = [_load_keys(k_refs, p) for p in part]
        vvs = [v_ref[p, :].astype(BF16) for p in part]
        ns = [k0 + u * kb + lax.broadcasted_iota(jnp.int32, (kb, 1), 0) for u in range(ksub)]
        emats = [_block_of_key(k0 + u * kb, kb, True) for u in range(ksub)] if has_sel else None

        def load_q(i):
            rows = pl.ds(pl.multiple_of(i * tq, tq), tq)
            return q_ref[rows, :].astype(BF16), do_ref[rows, :].astype(BF16)

        def products_into(sbuf, pbuf, i):
            qi, doi = load_q(i)
            for u in range(ksub):
                sbuf[u] = _nt(kks[u], qi)
                pbuf[u] = _nt(vvs[u], doi)

        def consume(sbuf, pbuf, i, carry):
            qi, doi = load_q(i)
            t = i * tq + lax.broadcasted_iota(jnp.int32, (1, tq), 1)
            selt_i = st_ref[i].astype(BF16) if has_sel else None
            new = []
            for u in range(ksub):
                dk_acc, dv_acc = carry[u]
                selx = _nn(emats[u], selt_i) if has_sel else None
                pt, _ = _scores(cfg, sbuf[u], t, ns[u], h, selx, True, lr_ref[0, i])
                dv_acc = dv_acc + _nn(pt.astype(BF16), doi)
                dst = pt * (pbuf[u] - dr_ref[0, i])
                new.append((dk_acc + _nn(dst.astype(BF16), qi), dv_acc))
            return tuple(new)

        if cfg.causal:
            first, count = k0 // tq, nq - k0 // tq
        elif cfg.mode == "win":
            first = k0 // tq
            count = jnp.minimum((k0 + tkb + WIN - 2) // tq + 1, nq) - first
        else:
            first, count = 0, nq

        def pair(p, cr):
            i0 = first + 2 * p
            products_into(sb, pb, i0 + 1)
            cr = consume(sa, pa, i0, cr)
            products_into(sa, pa, i0 + 2)
            return consume(sb, pb, i0 + 1, cr)

        carry = ((jnp.zeros((kb, dk), F32), jnp.zeros((kb, HEAD_V), F32)),) * ksub
        products_into(sa, pa, first)
        carry = lax.fori_loop(0, count // 2 - 1, pair, carry)
        last = first + count - 2
        products_into(sb, pb, last + 1)
        carry = consume(sa, pa, last, carry)
        carry = consume(sb, pb, last + 1, carry)
        for u, (dk_acc, dv_acc) in enumerate(carry):
            vals, off = [], 0
            for p in ks:
                vals.append(dk_acc[:, off:off + p.width] * cfg.scale)
                off += p.width
            vals.append(dv_acc)
            for src, ref, val in zip(outs, out_refs, vals):
                if src.per_head:
                    ref[part[u], :] = val
                else:
                    @pl.when(h == 0)
                    def _(ref=ref, val=val, u=u):
                        ref[part[u], :] = val

                    @pl.when(h > 0)
                    def _(ref=ref, val=val, u=u):
                        ref[part[u], :] += val

    rowv = pl.BlockSpec((1, nq, 1, tq), lambda j, h: (h, 0, 0, 0))
    ins = [pl.BlockSpec((tkb, p.width), lambda j, h, p=p: (j, p.col(h))) for p in ks]
    ins += [pl.BlockSpec((tkb, HEAD_V), lambda j, h: (j, v.col(h))),
            pl.BlockSpec((s, q.width), lambda j, h: (0, q.col(h))),
            pl.BlockSpec((s, HEAD_V), lambda j, h: (0, do.col(h))), rowv, rowv]
    args = [p.arr for p in ks] + [v.arr, q.arr, do.arr, lse_r, d_r]
    if has_sel:
        ins.append(pl.BlockSpec((nq, LANE, tq), lambda j, h: (0, 0, 0)))
        args.append(selt)
    out_specs = [pl.BlockSpec((tkb, p.width), lambda j, h, p=p: (j, h if p.per_head else 0)) for p in outs]
    out_shape = [jax.ShapeDtypeStruct((cfg.sk, (cfg.h if p.per_head else 1) * p.width), F32) for p in outs]
    assert nq % 2 == 0 and (cfg.mode in ("cmp", "mem") or tkb % (2 * tq) == 0), (nq, tkb, tq)
    return _pcall(
        kern, name=name, grid=(cfg.sk // tkb, cfg.h), in_specs=ins, out_specs=out_specs, out_shape=out_shape,
        scratch_shapes=[pltpu.VMEM((ksub, kb, tq), F32)] * 4,
        compiler_params=pltpu.CompilerParams(dimension_semantics=("parallel", "arbitrary")),
    )(*args)


def _attn_bwd(cfg, q, ks, v, sel, selt, o, lse, lse_r, do, dq_in, name):
    dq, d_r = _attn_dq(cfg, q, ks, v, sel, o, lse, do, dq_in, name + "_dq")
    res = _attn_dkv(cfg, q, ks, v, selt, lse_r, d_r, do, name + "_dkv")
    return dq, res[:-1], res[-1]


def _select(cfg, q, k_cmp, overlap):
    s, tq, sk = cfg.s, cfg.tb, cfg.sk
    n_s = s // SLC_LEN
    top_n = min(SLC_TOPN, n_s)

    def kern(q_ref, k_ref, ov_ref, sel_ref, selt_ref):
        i = pl.program_id(0)
        t = i * tq + lax.broadcasted_iota(jnp.int32, (tq, 1), 0)
        n = lax.broadcasted_iota(jnp.int32, (1, sk), 1)
        kk = k_ref[...]
        imp = jnp.zeros((tq, LANE), F32)
        for h in range(NSA_HEADS):
            sc, mask = _scores(cfg, _nt(q_ref[:, h * 256:(h + 1) * 256], kk), t, n, h, None, True)
            m = jnp.max(sc, axis=1, keepdims=True)
            e = jnp.where(mask, jnp.exp2(sc - m), 0.0)
            p = e / (jnp.sum(e, axis=1, keepdims=True) + 1e-20)
            imp = imp + _nn(p.astype(BF16), ov_ref[...])
        j = lax.broadcasted_iota(jnp.int32, (tq, LANE), 1)
        cur = t >> 6
        forced = (j == 0) | (j == cur) | (j == cur - 1)
        imp = jnp.where(forced, 1e9, imp)
        imp = jnp.where(j > cur, -1e9, imp)
        imp = jnp.where(j >= n_s, -3e38, imp)

        def pick(_, carry):
            work, chosen = carry
            mx = jnp.max(work, axis=1, keepdims=True)
            first = jnp.min(jnp.where(work == mx, j, LANE), axis=1, keepdims=True)
            hit = j == first
            return jnp.where(hit, -3e38, work), jnp.where(hit, 1.0, chosen)

        _, chosen = lax.fori_loop(0, top_n, pick, (imp, jnp.zeros((tq, LANE), F32)))
        rejected = 1.0 - jnp.where(j <= cur, chosen, 0.0)
        sel_ref[...] = rejected
        selt_ref[0] = jnp.transpose(rejected)

    return _pcall(
        kern, name="nsa_select", grid=(cfg.nb,),
        in_specs=[pl.BlockSpec((tq, NSA_HEADS * 256), lambda i: (i, q.col0)),
                  pl.BlockSpec((sk, 256), lambda i: (0, 0)), pl.BlockSpec((sk, LANE), lambda i: (0, 0))],
        out_specs=[pl.BlockSpec((tq, LANE), lambda i: (i, 0)), pl.BlockSpec((1, LANE, tq), lambda i: (i, 0, 0))],
        out_shape=[jax.ShapeDtypeStruct((s, LANE), F32), jax.ShapeDtypeStruct((cfg.nb, LANE, tq), F32)],
    )(q.arr, k_cmp, overlap)


def _silu_grad(pre):
    sg = _sigmoid(pre)
    return sg * (1.0 + pre * (1.0 - sg))


def _compress_fwd(a_lo, a_hi, pe_lo, pe_hi, w1_lo, w1_hi, w2, name):
    n, dp = a_lo.shape[0], w2.shape[1]

    def kern(alo, ahi, plo, phi, w1l, w1h, w2r, out_ref, pre_ref):
        xl = (alo[...] + plo[...]).astype(BF16)
        xh = (ahi[...] + phi[...]).astype(BF16)
        pre = _nn(xl, w1l[...]) + _nn(xh, w1h[...])
        act = pre * _sigmoid(pre)
        out_ref[...] = _nn(act.astype(BF16), w2r[...]).astype(BF16)
        pre_ref[...] = pre

    return _pcall(kern, name=name,
                  out_shape=[jax.ShapeDtypeStruct((n, dp), BF16), jax.ShapeDtypeStruct((n, dp), F32)],
                  )(a_lo, a_hi, pe_lo, pe_hi, w1_lo, w1_hi, w2)


def _compress_bwd(a_lo, a_hi, pe_lo, pe_hi, w1_lo, w1_hi, w2, pre, pre_sh, dout, dout_sh, name):
    n, ln = a_lo.shape
    dp = w2.shape[1]

    def kern(alo, ahi, plo, phi, w1l, w1h, w2r, pre_ref, presh_ref, do_ref, dosh_ref,
             da_ref, dpl_ref, dph_ref, dw1l_ref, dw1h_ref, dw2_ref):
        prev = pre_ref[...]
        act = prev * _sigmoid(prev)
        dob = do_ref[...].astype(BF16)
        w2v = w2r[...]
        dpre = (_nt(dob, w2v) * _silu_grad(prev)).astype(BF16)
        dpre_sh = (_nt(dosh_ref[...].astype(BF16), w2v) * _silu_grad(presh_ref[...])).astype(BF16)
        dw2_ref[...] = _nn(act.T.astype(BF16), dob)
        xl = alo[...] + plo[...]
        xh = ahi[...] + phi[...]
        dw1l_ref[...] = _nn(xl.T.astype(BF16), dpre)
        dw1h_ref[...] = _nn(xh.T.astype(BF16), dpre)
        dal = _nt(dpre, w1l[...])
        dah_sh = _nt(dpre_sh, w1h[...])
        da_ref[...] = dal + dah_sh
        dpl_ref[...] = jnp.sum(dal, axis=0, keepdims=True)
        dph_ref[...] = jnp.sum(dah_sh, axis=0, keepdims=True)

    return _pcall(
        kern, name=name,
        out_shape=[jax.ShapeDtypeStruct((n, ln), F32), jax.ShapeDtypeStruct((1, ln), F32),
                   jax.ShapeDtypeStruct((1, ln), F32), jax.ShapeDtypeStruct((ln, dp), F32),
                   jax.ShapeDtypeStruct((ln, dp), F32), jax.ShapeDtypeStruct((dp, dp), F32)],
    )(a_lo, a_hi, pe_lo, pe_hi, w1_lo, w1_hi, w2, pre, pre_sh, dout, dout_sh)


def _nsa_combine(o_cmp, o_slc, o_win, gl):
    s, w = o_cmp.shape
    tr = _tile(s, 512)

    def kern(a_ref, b_ref, c_ref, g_ref, o_ref):
        g = _sigmoid(g_ref[...])
        for h in range(NSA_HEADS):
            cs = slice(h * HEAD_V, (h + 1) * HEAD_V)
            o_ref[:, cs] = (g[:, 3 * h:3 * h + 1] * a_ref[:, cs] + g[:, 3 * h + 1:3 * h + 2] * b_ref[:, cs]
                            + g[:, 3 * h + 2:3 * h + 3] * c_ref[:, cs])

    row = pl.BlockSpec((tr, w), lambda i: (i, 0))
    return _pcall(kern, name="nsa_combine", grid=(s // tr,),
                  in_specs=[row, row, row, pl.BlockSpec((tr, LANE), lambda i: (i, gl.col0))], out_specs=row,
                  out_shape=jax.ShapeDtypeStruct((s, w), F32))(o_cmp, o_slc, o_win, gl.arr)


def _nsa_combine_bwd(do_cat, o_cmp, o_slc, o_win, gl):
    s, w = o_cmp.shape
    tr = _tile(s, 512)

    def kern(d_ref, a_ref, b_ref, c_ref, g_ref, da_ref, db_ref, dc_ref, dg_ref):
        g = _sigmoid(g_ref[...])
        lane = lax.broadcasted_iota(jnp.int32, (tr, LANE), 1)
        dgl = jnp.zeros((tr, LANE), F32)
        for h in range(NSA_HEADS):
            cs = slice(h * HEAD_V, (h + 1) * HEAD_V)
            dv = d_ref[:, cs]
            for b, (src, dst) in enumerate(((a_ref, da_ref), (b_ref, db_ref), (c_ref, dc_ref))):
                gate = g[:, 3 * h + b:3 * h + b + 1]
                dst[:, cs] = gate * dv
                dgate = jnp.sum(dv * src[:, cs], axis=1, keepdims=True)
                dgl = jnp.where(lane == 3 * h + b, dgate * gate * (1.0 - gate), dgl)
        dg_ref[...] = dgl

    row = pl.BlockSpec((tr, w), lambda i: (i, 0))
    tab = pl.BlockSpec((tr, LANE), lambda i: (i, 0))
    return _pcall(kern, name="nsa_combine_bwd", grid=(s // tr,),
                  in_specs=[pl.BlockSpec((tr, w), lambda i: (i, 2)), row, row, row,
                            pl.BlockSpec((tr, LANE), lambda i: (i, gl.col0))],
                  out_specs=[row, row, row, tab],
                  out_shape=[jax.ShapeDtypeStruct((s, w), F32)] * 3 + [jax.ShapeDtypeStruct((s, LANE), F32)],
                  )(do_cat, o_cmp, o_slc, o_win, gl.arr)


def _gate_fwd(o_mla, o_nsa, o_mem, hp):
    s = o_mla.shape[0]
    tr = _tile(s, 256)

    def kern(a_ref, b_ref, c_ref, z_ref, u_ref):
        z = z_ref[...]
        sz = z * _sigmoid(z)
        u_ref[:, 0:1024] = (a_ref[...] * sz[:, 0:1024]).astype(BF16)
        u_ref[:, 1024:1536] = (b_ref[...] * sz[:, 1024:1536]).astype(BF16)
        u_ref[:, 1536:2048] = (c_ref[...] * sz[:, 1536:2048]).astype(BF16)

    return _pcall(
        kern, name="gate_fwd", grid=(s // tr,),
        in_specs=[pl.BlockSpec((tr, 1024), lambda i: (i, 0)), pl.BlockSpec((tr, 512), lambda i: (i, 0)),
                  pl.BlockSpec((tr, 512), lambda i: (i, 0)), pl.BlockSpec((tr, 2048), lambda i: (i, 2))],
        out_specs=pl.BlockSpec((tr, 2048), lambda i: (i, 0)),
        out_shape=jax.ShapeDtypeStruct((s, 2048), BF16))(o_mla, o_nsa, o_mem, hp)


def _gate_bwd(du, o_mla, o_nsa, o_mem, hp):
    s = du.shape[0]
    tr = _tile(s, 256)

    def kern(d_ref, a_ref, b_ref, c_ref, z_ref, do_ref, dz_ref):
        z = z_ref[...]
        sg = _sigmoid(z)
        sz = z * sg
        dsz = sg * (1.0 + z * (1.0 - sg))
        d = d_ref[...]
        do_ref[...] = d * sz
        dz_ref[:, 0:1024] = d[:, 0:1024] * a_ref[...] * dsz[:, 0:1024]
        dz_ref[:, 1024:1536] = d[:, 1024:1536] * b_ref[...] * dsz[:, 1024:1536]
        dz_ref[:, 1536:2048] = d[:, 1536:2048] * c_ref[...] * dsz[:, 1536:2048]

    wide = pl.BlockSpec((tr, 2048), lambda i: (i, 0))
    return _pcall(
        kern, name="gate_bwd", grid=(s // tr,),
        in_specs=[wide, pl.BlockSpec((tr, 1024), lambda i: (i, 0)), pl.BlockSpec((tr, 512), lambda i: (i, 0)),
                  pl.BlockSpec((tr, 512), lambda i: (i, 0)), pl.BlockSpec((tr, 2048), lambda i: (i, 2))],
        out_specs=[wide, wide],
        out_shape=[jax.ShapeDtypeStruct((s, 2048), F32)] * 2)(du, o_mla, o_nsa, o_mem, hp)


def _tile2d(rows, cols, arrays):
    if rows % 16 == 0:
        return _row_tile(rows, cols * arrays), cols
    want = max(LANE, BLOCK_BYTES // (rows * 4 * arrays) // LANE * LANE)
    tc = LANE
    for t in range(LANE, cols + 1, LANE):
        if cols % t == 0 and t <= want:
            tc = t
    return rows, tc


def _sum_slots(buf, name):
    n, rows, cols = buf.shape
    tr, tc = _tile2d(rows, cols, n)

    def kern(b_ref, o_ref):
        acc = b_ref[0].astype(F32)
        for i in range(1, n):
            acc = acc + b_ref[i].astype(F32)
        o_ref[...] = acc

    return _pcall(kern, name=name, grid=(rows // tr, cols // tc),
                  in_specs=[pl.BlockSpec((n, tr, tc), lambda i, j: (0, i, j))],
                  out_specs=pl.BlockSpec((tr, tc), lambda i, j: (i, j)),
                  out_shape=jax.ShapeDtypeStruct((rows, cols), F32))(buf)


def _pair_sum(g4, theirs, core, axis, name):
    n, rows, cols = theirs.shape
    tr, tc = _tile2d(rows, cols, 1)
    nbr, nbc = rows // tr, cols // tc

    def kern(c_ref, a_ref, b_ref, o_ref):
        o_ref[...] = (a_ref[...] + b_ref[...]).astype(BF16)

    blk = (1, tr, tc)
    mine = ((lambda s, i, j, c: (s, c[0] * nbr + i, j)) if axis == 0
            else (lambda s, i, j, c: (s, i, c[0] * nbc + j)))
    grid_spec = pltpu.PrefetchScalarGridSpec(
        num_scalar_prefetch=1, grid=(n, nbr, nbc),
        in_specs=[pl.BlockSpec(blk, mine), pl.BlockSpec(blk, lambda s, i, j, c: (s, i, j))],
        out_specs=pl.BlockSpec(blk, lambda s, i, j, c: (s, i, j)))
    return _pcall(kern, name=name, grid_spec=grid_spec,
                  out_shape=jax.ShapeDtypeStruct((n, rows, cols), BF16))(core, g4, theirs)


def _adamw(w, g, m, v, name):
    rows, cols = w.shape
    tr, tc = _tile2d(rows, cols, 4)
    bc1 = 1.0 - ADAM_B1 ** ADAM_STEP
    bc2 = 1.0 - ADAM_B2 ** ADAM_STEP

    def kern(w_ref, g_ref, m_ref, v_ref, d_ref, mo_ref, vo_ref):
        gv = g_ref[...]
        mn = ADAM_B1 * m_ref[...] + (1.0 - ADAM_B1) * gv
        vn = ADAM_B2 * v_ref[...] + (1.0 - ADAM_B2) * (gv * gv)
        d_ref[...] = -ADAM_LR * ((mn / bc1) / (jnp.sqrt(vn / bc2) + ADAM_EPS) + ADAM_WD * w_ref[...])
        mo_ref[...] = mn
        vo_ref[...] = vn

    blk = pl.BlockSpec((tr, tc), lambda i, j: (i, j))
    return _pcall(kern, name=name, grid=(rows // tr, cols // tc), in_specs=[blk] * 4, out_specs=[blk] * 3,
                  out_shape=[jax.ShapeDtypeStruct((rows, cols), F32)] * 3)(w, g, m, v)


ANY = pl.BlockSpec(memory_space=pl.ANY)


def _place():
    x, y, c = lax.axis_index("x"), lax.axis_index("y"), lax.axis_index("c")
    chips = [(1 - x, y), (x, 1 - y), (1 - x, 1 - y)]
    return x, y, c, chips


def _remote(src, dst, send_sem, recv_sem, to):
    return pltpu.make_async_remote_copy(src_ref=src, dst_ref=dst, send_sem=send_sem, recv_sem=recv_sem,
                                        device_id=to, device_id_type=MESH)


def _half(ref, lead, core, axis):
    size = ref.shape[len(lead) + axis] // 2
    cut = pl.ds(core * size, size)
    return ref.at[tuple(lead) + ((cut, slice(None)) if axis == 0 else (slice(None), cut))]


def _gather_shards(ws, axes):
    nw = len(ws)

    def body(*refs):
        w_refs, out_refs = refs[:nw], refs[nw:2 * nw]
        send_sems, recv_sems = refs[2 * nw:]
        x, y, c, chips = _place()
        me = 2 * x + y
        sibling = (x, y, 1 - c)

        def part(i, slot, core):
            return _half(out_refs[i], (slot,), core, axes[i])

        def copy(sem, src, dst, to):
            return _remote(src, dst, send_sems.at[sem], recv_sems.at[sem], to)

        first = [copy(j * nw + i, _half(w_refs[i], (), c, axes[i]), part(i, me, c), (*chip, c))
                 for j, chip in enumerate(chips) for i in range(nw)]
        for cp in first:
            cp.start()
        passed = []
        for j, (cx, cy) in enumerate(chips):
            slot = 2 * cx + cy
            for i in range(nw):
                copy(j * nw + i, part(i, slot, c), part(i, slot, c), (x, y, c)).wait_recv()
                fwd = copy((3 + j) * nw + i, part(i, slot, c), part(i, slot, c), sibling)
                fwd.start()
                passed.append(fwd)
        for j, (cx, cy) in enumerate(chips):
            slot = 2 * cx + cy
            for i in range(nw):
                copy((3 + j) * nw + i, part(i, slot, 1 - c), part(i, slot, 1 - c), (x, y, c)).wait_recv()
        for cp in first + passed:
            cp.wait_send()

    return _pcall(
        body, name="gather_shards", in_specs=[ANY] * nw, out_specs=[ANY] * nw,
        out_shape=[jax.ShapeDtypeStruct((4,) + w.shape, w.dtype) for w in ws],
        scratch_shapes=[pltpu.SemaphoreType.DMA((6 * nw,)), pltpu.SemaphoreType.DMA((6 * nw,))],
    )(*ws)


def _half_shape(shape, axis):
    return tuple(d // 2 if k == len(shape) - 2 + axis else d for k, d in enumerate(shape))


def _pair_exchange(gs, axes):
    nw = len(gs)

    def body(*refs):
        g_refs, out_refs = refs[:nw], refs[nw:2 * nw]
        send_sems, recv_sems = refs[2 * nw:]
        x, y, c, _ = _place()
        cps = []
        for i in range(nw):
            cp = _remote(_half(g_refs[i], (slice(None),), 1 - c, axes[i]), out_refs[i],
                         send_sems.at[i], recv_sems.at[i], (x, y, 1 - c))
            cp.start()
            cps.append(cp)
        for cp in cps:
            cp.wait()

    return _pcall(body, name="pair_exchange", in_specs=[ANY] * nw, out_specs=[ANY] * nw,
                  out_shape=[jax.ShapeDtypeStruct(_half_shape(g.shape, a), g.dtype) for g, a in zip(gs, axes)],
                  scratch_shapes=[pltpu.SemaphoreType.DMA((nw,)), pltpu.SemaphoreType.DMA((nw,))])(*gs)


def _chip_exchange(ps):
    nw = len(ps)

    def body(*refs):
        p_refs, out_refs = refs[:nw], refs[nw:2 * nw]
        send_sems, recv_sems, local_sems = refs[2 * nw:]
        x, y, c, chips = _place()
        me = 2 * x + y
        mine = [pltpu.make_async_copy(p_refs[i].at[me], out_refs[i].at[me], local_sems.at[i]) for i in range(nw)]
        for cp in mine:
            cp.start()
        sends = []
        for j, (cx, cy) in enumerate(chips):
            for i in range(nw):
                cp = _remote(p_refs[i].at[2 * cx + cy], out_refs[i].at[me], send_sems.at[j * nw + i],
                             recv_sems.at[j * nw + i], (cx, cy, c))
                cp.start()
                sends.append(cp)
        for j, (cx, cy) in enumerate(chips):
            slot = 2 * cx + cy
            for i in range(nw):
                _remote(out_refs[i].at[slot], out_refs[i].at[slot], send_sems.at[j * nw + i],
                        recv_sems.at[j * nw + i], (x, y, c)).wait_recv()
        for cp in sends:
            cp.wait_send()
        for cp in mine:
            cp.wait()

    return _pcall(body, name="chip_exchange", in_specs=[ANY] * nw, out_specs=[ANY] * nw,
                  out_shape=[jax.ShapeDtypeStruct(p.shape, p.dtype) for p in ps],
                  scratch_shapes=[pltpu.SemaphoreType.DMA((3 * nw,)), pltpu.SemaphoreType.DMA((3 * nw,)),
                                  pltpu.SemaphoreType.DMA((nw,))])(*ps)


def _half_exchange(ts, axes):
    nw = len(ts)

    def body(*refs):
        t_refs, out_refs = refs[:nw], refs[nw:2 * nw]
        send_sems, recv_sems = refs[2 * nw:]
        x, y, c, _ = _place()
        sends = []
        for i in range(nw):
            cp = _remote(t_refs[i], _half(out_refs[i], (), c, axes[i]), send_sems.at[i], recv_sems.at[i],
                         (x, y, 1 - c))
            cp.start()
            sends.append(cp)
        for i in range(nw):
            _remote(t_refs[i], _half(out_refs[i], (), 1 - c, axes[i]), send_sems.at[i], recv_sems.at[i],
                    (x, y, c)).wait_recv()
        for cp in sends:
            cp.wait_send()

    def whole(t, a):
        return tuple(2 * d if k == a else d for k, d in enumerate(t.shape))

    return _pcall(body, name="half_exchange", in_specs=[ANY] * nw, out_specs=[ANY] * nw,
                  out_shape=[jax.ShapeDtypeStruct(whole(t, a), t.dtype) for t, a in zip(ts, axes)],
                  scratch_shapes=[pltpu.SemaphoreType.DMA((nw,)), pltpu.SemaphoreType.DMA((nw,))])(*ts)


def _gather_all(v):
    rows, cols = v.shape

    def body(v_ref, out_ref, send_sems, recv_sems, local_sem):
        x, y, c, _ = _place()
        me = 4 * x + 2 * y + c
        mine = pltpu.make_async_copy(v_ref, out_ref.at[me], local_sem)
        mine.start()
        sends = []
        for d in range(1, 8):
            peer = (x ^ (d >> 2), y ^ ((d >> 1) & 1), c ^ (d & 1))
            cp = _remote(v_ref, out_ref.at[me], send_sems.at[d - 1], recv_sems.at[d - 1], peer)
            cp.start()
            sends.append(cp)
        for d in range(1, 8):
            slot = 4 * (x ^ (d >> 2)) + 2 * (y ^ ((d >> 1) & 1)) + (c ^ (d & 1))
            _remote(v_ref, out_ref.at[slot], send_sems.at[d - 1], recv_sems.at[d - 1], (x, y, c)).wait_recv()
        for cp in sends:
            cp.wait_send()
        mine.wait()

    return _pcall(body, name="gather_all", in_specs=[ANY], out_specs=ANY,
                  out_shape=jax.ShapeDtypeStruct((8, rows, cols), v.dtype),
                  scratch_shapes=[pltpu.SemaphoreType.DMA((7,)), pltpu.SemaphoreType.DMA((7,)),
                                  pltpu.SemaphoreType.DMA])(v)


def _pad_cols(a, width):
    return a if a.shape[1] == width else jnp.pad(a, ((0, 0), (0, width - a.shape[1])))


def _pad_rows(a, height):
    return a if a.shape[0] == height else jnp.pad(a, ((0, height - a.shape[0]), (0, 0)))


def _w_in_padded(wt):
    def seg(name, height=None):
        o, n = ORIG[name]
        return _pad_rows(wt[o:o + n], height or n)

    qn = wt[ORIG["q_nsa"][0]:ORIG["q_nsa"][0] + 768].reshape(NSA_HEADS, NSA_DK, -1)
    qn = jnp.pad(qn, ((0, 0), (0, 256 - NSA_DK), (0, 0))).reshape(NSA_HEADS * 256, -1)
    kr = seg("k_rope")
    zeros = jnp.zeros((PAD["z"] - (PAD["q_mem"] + 512), wt.shape[1]), wt.dtype)
    return jnp.concatenate(
        [seg("c_q"), seg("c_kv"), qn, seg("k_c", 256), seg("k_s", 256), seg("k_w", 256), kr, kr,
         seg("v_c"), seg("v_s"), seg("v_w"), seg("g_nsa", LANE), seg("q_mem"), zeros,
         seg("z_mla"), seg("z_nsa"), seg("z_mem")], axis=0)


def _w_in_unpadded(gt):
    def seg(name, n):
        return gt[PAD[name]:PAD[name] + n]

    qn = gt[PAD["q_nsa"]:PAD["q_nsa"] + 1024].reshape(NSA_HEADS, 256, -1)[:, :NSA_DK].reshape(768, -1)
    z = PAD["z"]
    return jnp.concatenate(
        [seg("c_q", 512), seg("c_kv", 512), seg("k_rope", 64), gt[z:z + 1024], qn, seg("k_c", 192),
         seg("v_c", 128), seg("k_s", 192), seg("v_s", 128), seg("k_w", 192), seg("v_w", 128),
         seg("g_nsa", 12), gt[z + 1024:z + 1536], seg("q_mem", 512), gt[z + 1536:z + 2048]], axis=0)


def _rope_tables(s):
    pos = jnp.arange(s, dtype=F32)
    inv_freq = ROPE_THETA ** (-jnp.arange(0, 64, 2, dtype=F32) / 64)
    ang = pos[:, None] * inv_freq[None, :]
    cos, sin = jnp.cos(ang), jnp.sin(ang)
    z = jnp.zeros((s, 64), F32)
    return jnp.concatenate([cos, cos, z], axis=1), jnp.concatenate([-sin, sin, z], axis=1)


def _overlap_table(s):
    n_c, n_s = s // CMP_STRIDE, s // SLC_LEN
    c0 = np.arange(n_c)[:, None] * CMP_STRIDE
    s0 = np.arange(LANE)[None, :] * SLC_LEN
    ov = (c0 < s0 + SLC_LEN) & (c0 + CMP_LEN > s0) & (np.arange(n_c)[:, None] < n_c - 1) & (np.arange(LANE)[None, :] < n_s)
    return jnp.asarray(ov.astype(np.float32), dtype=BF16)


def _shift_down(a):
    return jnp.concatenate([jnp.zeros((8, a.shape[1]), a.dtype), a], axis=0)[7:7 + a.shape[0]]


def _shift_up(a):
    return jnp.concatenate([a, jnp.zeros((8, a.shape[1]), a.dtype)], axis=0)[1:1 + a.shape[0]]


def _local_step(x, mem, target, w):
    s = x.shape[0]
    cs, sn = _rope_tables(s)
    t_ = jnp.transpose

    w_in_p = _w_in_padded(w["w_in_t"])
    xn, rstd_x = _rms_fwd(_Src(x, D_MODEL), w["norm_g"], "norm_x")
    hp, hpb = _mm(xn, w_in_p, "in_proj", mode="nt", second_dtype=BF16)

    w_uq3 = w["w_uq"].reshape(512, MLA_HEADS, 192)
    w_uq_p = jnp.concatenate([w_uq3, w_uq3[:, :, 128:]], axis=2).reshape(512, MLA_HEADS * 256)
    w_ukv_p = t_(w["w_ukv"].reshape(512, MLA_HEADS, 2, 128), (0, 2, 1, 3)).reshape(512, 2048)
    c_q, c_kv = _Src(hp, 512, 0), _Src(hp, 512, 1)
    cqn, rstd_q = _rms_fwd(c_q, w["q_norm_g"], "norm_q")
    ckvn, rstd_kv = _rms_fwd(c_kv, w["kv_norm_g"], "norm_kv")
    q_lin = _mm(cqn, w_uq_p, "mla_q_proj")
    kvb = _mm(ckvn, w_ukv_p, "mla_kv_proj", out_dtype=BF16)
    q_mla = _rope_fwd(_Src(q_lin, MLA_HEADS * 256), cs, sn, MLA_HEADS, 256, LANE, "rope_q")
    k_pe = _rope_fwd(_Src(hp, LANE, PAD["k_rope"] // LANE), cs, sn, 1, LANE, 0, "rope_k")
    mla = _Attn("mla", s, s, MLA_HEADS, 256)
    mla_q, mla_v = _Src(q_mla, 256), _Src(kvb, LANE, MLA_HEADS)
    mla_k = [_Src(kvb, LANE), _Src(k_pe, LANE, 0, False)]
    o_mla, l_mla, lr_mla = _attn_fwd(mla, mla_q, mla_k, mla_v, None, "mla_fwd")

    sk = s // CMP_STRIDE
    pe_k, pe_v = w["cmp_pe_k"], w["cmp_pe_v"]
    w1k = _pad_cols(w["cmp_w1k"], 256)
    w2k = jnp.pad(w["cmp_w2k"], ((0, 64), (0, 64))).astype(BF16)
    w1v, w2v = w["cmp_w1v"], w["cmp_w2v"].astype(BF16)
    half_k, half_v = CMP_STRIDE * NSA_DK, CMP_STRIDE * HEAD_V
    ak = hp[:, PAD["k_c"]:PAD["k_c"] + NSA_DK].reshape(sk, half_k)
    av = hp[:, PAD["v_c"]:PAD["v_c"] + HEAD_V].reshape(sk, half_v)
    ck_args = (ak, _shift_up(ak), pe_k[:CMP_STRIDE].reshape(1, half_k), pe_k[CMP_STRIDE:].reshape(1, half_k),
               w1k[:half_k], w1k[half_k:], w2k)
    cv_args = (av, _shift_up(av), pe_v[:CMP_STRIDE].reshape(1, half_v), pe_v[CMP_STRIDE:].reshape(1, half_v),
               w1v[:half_v], w1v[half_v:], w2v)
    k_cmp, pre_k = _compress_fwd(*ck_args, "compress_k")
    v_cmp, pre_v = _compress_fwd(*cv_args, "compress_v")
    cmp_ = _Attn("cmp", s, sk, NSA_HEADS, 256)
    slc = _Attn("slc", s, s, NSA_HEADS, 256)
    win = _Attn("win", s, s, NSA_HEADS, 256)
    nsa_q = _Src(hpb, 256, PAD["q_nsa"] // 256)
    cmp_k, cmp_v = [_Src(k_cmp, 256, 0, False)], _Src(v_cmp, HEAD_V, 0, False)
    slc_k, slc_v = [_Src(hpb, 256, PAD["k_s"] // 256, False)], _Src(hpb, HEAD_V, PAD["v_s"] // HEAD_V, False)
    win_k, win_v = [_Src(hpb, 256, PAD["k_w"] // 256, False)], _Src(hpb, HEAD_V, PAD["v_w"] // HEAD_V, False)
    o_cmp, l_cmp, lr_cmp = _attn_fwd(cmp_, nsa_q, cmp_k, cmp_v, None, "cmp_fwd")
    sel, selt = _select(cmp_, _Src(hpb, NSA_HEADS * 256, PAD["q_nsa"] // (NSA_HEADS * 256)), k_cmp,
                        _overlap_table(s))
    o_slc, l_slc, lr_slc = _attn_fwd(slc, nsa_q, slc_k, slc_v, sel, "slc_fwd")
    o_win, l_win, lr_win = _attn_fwd(win, nsa_q, win_k, win_v, None, "win_fwd")
    gl = _Src(hp, LANE, PAD["g_nsa"] // LANE)
    o_nsa = _nsa_combine(o_cmp, o_slc, o_win, gl)

    mn, rstd_m = _rms_fwd(_Src(mem, D_MODEL), w["mem_norm_g"], "norm_mem")
    kvm = _mm(mn, w["w_mem_kv"], "mem_kv_proj", out_dtype=BF16)
    mem_ = _Attn("mem", s, mem.shape[0], MEM_HEADS, LANE)
    mem_q, mem_k, mem_v = _Src(hpb, LANE, PAD["q_mem"] // LANE), [_Src(kvm, LANE)], _Src(kvm, LANE, MEM_HEADS)
    o_mem, l_mem, lr_mem = _attn_fwd(mem_, mem_q, mem_k, mem_v, None, "mem_fwd")

    u = _gate_fwd(o_mla, o_nsa, o_mem, hp)
    proj = _mm(u, w["w_out"], "out_proj")
    dy, g_final, loss = _final_loss(x, proj, w["final_norm_g"].reshape(1, -1), target)

    g_w_out = _mm(u, dy, "out_proj_dw", mode="tn")
    du = _mm(dy, w["w_out"], "out_proj_dx", mode="nt")
    do_cat, dz = _gate_bwd(du, o_mla, o_nsa, o_mem, hp)

    dq_mem, (dk_mem,), dv_mem = _attn_bwd(mem_, mem_q, mem_k, mem_v, None, None, _Src(o_mem, HEAD_V), l_mem,
                                          lr_mem, _Src(do_cat, HEAD_V, 12), None, "mem_bwd")
    dkvm = jnp.concatenate([dk_mem, dv_mem], axis=1)
    g_w_mem_kv = _mm(mn, dkvm, "mem_kv_dw", mode="tn")
    dmn = _mm(dkvm, w["w_mem_kv"], "mem_kv_dx", mode="nt")
    _, g_mem_norm = _rms_bwd(_Src(mem, D_MODEL), w["mem_norm_g"], rstd_m, dmn, None, "norm_mem_bwd")

    do_cmp, do_slc, do_win, dgl = _nsa_combine_bwd(do_cat, o_cmp, o_slc, o_win, gl)
    dq_n, (dk_cmp,), dv_cmp = _attn_bwd(cmp_, nsa_q, cmp_k, cmp_v, None, None, _Src(o_cmp, HEAD_V), l_cmp,
                                        lr_cmp, _Src(do_cmp, HEAD_V), None, "cmp_bwd")
    dq_n, (dk_s,), dv_s = _attn_bwd(slc, nsa_q, slc_k, slc_v, sel, selt, _Src(o_slc, HEAD_V), l_slc, lr_slc,
                                    _Src(do_slc, HEAD_V), dq_n, "slc_bwd")
    dq_n, (dk_w,), dv_w = _attn_bwd(win, nsa_q, win_k, win_v, None, None, _Src(o_win, HEAD_V), l_win, lr_win,
                                    _Src(do_win, HEAD_V), dq_n, "win_bwd")
    dak, dpk_lo, dpk_hi, dw1k_lo, dw1k_hi, g_w2k = _compress_bwd(
        *ck_args, pre_k, _shift_down(pre_k), dk_cmp, _shift_down(dk_cmp), "compress_k_bwd")
    dav, dpv_lo, dpv_hi, dw1v_lo, dw1v_hi, g_w2v = _compress_bwd(
        *cv_args, pre_v, _shift_down(pre_v), dv_cmp, _shift_down(dv_cmp), "compress_v_bwd")
    g_pe_k = jnp.concatenate([dpk_lo.reshape(CMP_STRIDE, NSA_DK), dpk_hi.reshape(CMP_STRIDE, NSA_DK)], axis=0)
    g_pe_v = jnp.concatenate([dpv_lo.reshape(CMP_STRIDE, HEAD_V), dpv_hi.reshape(CMP_STRIDE, HEAD_V)], axis=0)
    g_w1k = jnp.concatenate([dw1k_lo, dw1k_hi], axis=0)[:, :NSA_DK]
    g_w1v = jnp.concatenate([dw1v_lo, dw1v_hi], axis=0)
    dk_c = _pad_cols(dak.reshape(s, NSA_DK), 256)
    dv_c = dav.reshape(s, HEAD_V)

    dq_m, (dk_nope, dk_pe), dv_m = _attn_bwd(mla, mla_q, mla_k, mla_v, None, None, _Src(o_mla, HEAD_V), l_mla,
                                             lr_mla, _Src(do_cat, HEAD_V), None, "mla_bwd")
    dq_lin = _rope_bwd_q(dq_m, cs, sn)
    dkv_lin, d_krope = _rope_bwd_k(dk_nope, dk_pe, dv_m, cs, sn)
    g_w_uq_p = _mm(cqn, dq_lin, "mla_q_dw", mode="tn")
    dcqn = _mm(dq_lin, w_uq_p, "mla_q_dx", mode="nt")
    g_w_ukv_p = _mm(ckvn, dkv_lin, "mla_kv_dw", mode="tn")
    dckvn = _mm(dkv_lin, w_ukv_p, "mla_kv_dx", mode="nt")
    dc_q, g_q_norm = _rms_bwd(c_q, w["q_norm_g"], rstd_q, dcqn, None, "norm_q_bwd")
    dc_kv, g_kv_norm = _rms_bwd(c_kv, w["kv_norm_g"], rstd_kv, dckvn, None, "norm_kv_bwd")
    g_w_uq = g_w_uq_p.reshape(512, MLA_HEADS, 256)[:, :, :192].reshape(512, MLA_HEADS * 192)
    g_w_ukv = t_(g_w_ukv_p.reshape(512, 2, MLA_HEADS, 128), (0, 2, 1, 3)).reshape(512, 2048)

    dhp = jnp.concatenate(
        [dc_q, dc_kv, dq_n, dk_c, dk_s, dk_w, d_krope, dv_c, dv_s, dv_w, dgl, dq_mem,
         jnp.zeros((s, PAD["z"] - (PAD["q_mem"] + 512)), F32), dz], axis=1)
    g_w_in_t = _w_in_unpadded(_mm(dhp, xn, "in_proj_dw", mode="tn"))
    dxn = _mm(dhp, w_in_p, "in_proj_dx")
    grad_x, g_norm = _rms_bwd(_Src(x, D_MODEL), w["norm_g"], rstd_x, dxn, dy, "norm_x_bwd")

    grads = dict(norm_g=g_norm, w_in_t=g_w_in_t, q_norm_g=g_q_norm, w_uq=g_w_uq, kv_norm_g=g_kv_norm,
                 w_ukv=g_w_ukv, cmp_pe_k=g_pe_k, cmp_pe_v=g_pe_v, cmp_w1k=g_w1k, cmp_w2k=g_w2k[:NSA_DK, :NSA_DK],
                 cmp_w1v=g_w1v, cmp_w2v=g_w2v, mem_norm_g=g_mem_norm, w_mem_kv=g_w_mem_kv, w_out=g_w_out,
                 final_norm_g=g_final.reshape(-1))
    return loss[0, 0], grad_x, grads


def kernel(x, mem, norm_g, w_in, q_norm_g, w_uq, kv_norm_g, w_ukv, cmp_pe_k, cmp_pe_v, cmp_w1k, cmp_w2k, cmp_w1v, cmp_w2v, mem_norm_g, w_mem_kv, w_out, final_norm_g, loss_target, m_norm_g, m_w_in, m_q_norm_g, m_w_uq, m_kv_norm_g, m_w_ukv, m_cmp_pe_k, m_cmp_pe_v, m_cmp_w1k, m_cmp_w2k, m_cmp_w1v, m_cmp_w2v, m_mem_norm_g, m_w_mem_kv, m_w_out, m_final_norm_g, v_norm_g, v_w_in, v_q_norm_g, v_w_uq, v_kv_norm_g, v_w_ukv, v_cmp_pe_k, v_cmp_pe_v, v_cmp_w1k, v_cmp_w2k, v_cmp_w1v, v_cmp_w2v, v_mem_norm_g, v_w_mem_kv, v_w_out, v_final_norm_g):
    args = dict(locals())
    wts = {n: args[n] for n in WEIGHTS}
    loc = {n: (a if n == "final_norm_g" else a[0]) for n, a in wts.items()}

    def to_x(n, a):
        return a.T if n == "w_in" else a

    split = [1 if n == "w_in" else 0 for n in SHARDED]

    own = [to_x(n, loc[n]).astype(BF16) for n in SHARDED]
    chip = 2 * lax.axis_index("x") + lax.axis_index("y")
    gathered = [lax.dynamic_update_slice(gw, a[None], (chip, 0, 0))
                for gw, a in zip(_gather_shards(own, split), own)]
    full = {n: loc[n].reshape(1, -1) if loc[n].ndim == 1 else loc[n] for n in REPLICATED}
    for n, gw in zip(SHARDED, gathered):
        if n == "w_in":
            full["w_in_t"] = gw.reshape(4 * gw.shape[1], gw.shape[2])
        elif SHARD_AXIS[n] == 0:
            full[n] = gw.reshape(4 * gw.shape[1], gw.shape[2])
        else:
            full[n] = jnp.concatenate([gw[j] for j in range(4)], axis=1)

    loss, grad_x, g = _local_step(x[0], mem[0], loss_target[0], full)
    loss = lax.psum(loss, ("x", "y", "c"))

    def slots(n):
        a = g["w_in_t"] if n == "w_in" else g[n]
        if n == "w_in" or SHARD_AXIS[n] == 0:
            return a.reshape(4, a.shape[0] // 4, a.shape[1])
        width = a.shape[1] // 4
        return jnp.stack([a[:, j * width:(j + 1) * width] for j in range(4)])

    gs = [slots(n) for n in SHARDED]
    core = lax.axis_index("c").astype(jnp.int32).reshape(1)
    theirs = _pair_exchange(gs, split)
    pairs = [_pair_sum(a, b, core, ax, "pair_sum_" + n) for n, a, b, ax in zip(SHARDED, gs, theirs, split)]
    from_chips = _chip_exchange(pairs)
    mine = [_sum_slots(b, "chip_sum_" + n) for n, b in zip(SHARDED, from_chips)]
    g_sh = [lax.dynamic_update_slice(o, t, (core[0] * t.shape[0], 0) if ax == 0 else (0, core[0] * t.shape[1]))
            for o, t, ax in zip(_half_exchange(mine, split), mine, split)]

    n_rep = sum(int(np.prod(loc[n].shape)) for n in REPLICATED)
    rows_rep = -(-n_rep // (8 * LANE)) * 8

    def rep_pack(parts):
        flat = jnp.concatenate([p.reshape(-1) for p in parts])
        return jnp.pad(flat, (0, rows_rep * LANE - n_rep)).reshape(rows_rep, LANE)

    g_rep = _sum_slots(_gather_all(rep_pack([g[n] for n in REPLICATED])), "replica_sum")
    d_rp, m_rp, v_rp = _adamw(rep_pack([wts[n] for n in REPLICATED]), g_rep,
                              rep_pack([args["m_" + n] for n in REPLICATED]),
                              rep_pack([args["v_" + n] for n in REPLICATED]), "adamw_replicated")

    def rep_unpack(buf):
        flat, out, o = buf.reshape(-1), {}, 0
        for n in REPLICATED:
            size = int(np.prod(wts[n].shape))
            out[n] = flat[o:o + size].reshape(wts[n].shape)
            o += size
        return out

    outs = {k: rep_unpack(b) for k, b in (("g", g_rep), ("d", d_rp), ("m", m_rp), ("v", v_rp))}
    for n, gn in zip(SHARDED, g_sh):
        d, mo, vo = _adamw(to_x(n, loc[n]), gn, to_x(n, args["m_" + n][0]), to_x(n, args["v_" + n][0]),
                           "adamw_" + n)
        for k, a in (("g", gn), ("d", d), ("m", mo), ("v", vo)):
            outs[k][n] = to_x(n, a).reshape(wts[n].shape)

    return (loss, grad_x[None], *[outs["g"][n] for n in WEIGHTS], *[outs["d"][n] for n in WEIGHTS],
            *[outs["m"][n] for n in WEIGHTS], *[outs["v"][n] for n in WEIGHTS])
```

```python
from typing import NamedTuple

import numpy as np
import jax
import jax.numpy as jnp
from jax import lax
from jax.experimental import pallas as pl
from jax.experimental.pallas import tpu as pltpu

F32 = jnp.float32
BF16 = jnp.bfloat16
MESH = pl.DeviceIdType.MESH

D_MODEL = 2048
EPS = 1e-6
LANE = 128
HEAD_V = 128
MLA_HEADS = 8
NSA_HEADS = 4
MEM_HEADS = 4
NSA_DK = 192
CMP_STRIDE = 16
CMP_LEN = 32
SLC_LEN = 64
SLC_TOPN = 16
WIN = 512
NEG = -1e30
LOG2E = 1.4426950408889634
ROPE_THETA = 10000.0
BLOCK_BYTES = 2 << 20

ORIG = dict(c_q=(0, 512), c_kv=(512, 512), k_rope=(1024, 64), z_mla=(1088, 1024),
            q_nsa=(2112, 768), k_c=(2880, 192), v_c=(3072, 128), k_s=(3200, 192),
            v_s=(3392, 128), k_w=(3520, 192), v_w=(3712, 128), g_nsa=(3840, 12),
            z_nsa=(3852, 512), q_mem=(4364, 512), z_mem=(4876, 512))
PAD = dict(c_q=0, c_kv=512, q_nsa=1024, k_c=2048, k_s=2304, k_w=2560, k_rope=2816, v_c=2944,
           v_s=3072, v_w=3200, g_nsa=3328, q_mem=3456, z=4096)
D_PAD = 6144

ADAM_LR, ADAM_B1, ADAM_B2, ADAM_EPS, ADAM_WD, ADAM_STEP = 0.001, 0.9, 0.999, 1e-08, 0.01, 10

SHARDED = ("w_in", "w_uq", "w_ukv", "cmp_w1k", "cmp_w1v", "w_mem_kv", "w_out")
SHARD_AXIS = dict(w_in=1, w_uq=1, w_ukv=1, cmp_w1k=0, cmp_w1v=0, w_mem_kv=0, w_out=0)
REPLICATED = ("norm_g", "q_norm_g", "kv_norm_g", "cmp_pe_k", "cmp_pe_v", "cmp_w2k", "cmp_w2v",
              "mem_norm_g", "final_norm_g")
WEIGHTS = ("norm_g", "w_in", "q_norm_g", "w_uq", "kv_norm_g", "w_ukv", "cmp_pe_k", "cmp_pe_v",
           "cmp_w1k", "cmp_w2k", "cmp_w1v", "cmp_w2v", "mem_norm_g", "w_mem_kv", "w_out",
           "final_norm_g")


def _pcall(kernel, **kw):
    return pl.pallas_call(kernel, **kw)


def _tile(n, pref):
    if n <= pref:
        return n
    for t in range(pref, LANE - 1, -LANE):
        if n % t == 0:
            return t
    raise ValueError((n, pref))


def _row_tile(rows, cols, itemsize=4):
    want = max(16, BLOCK_BYTES // (cols * itemsize))
    if rows <= want:
        return rows
    t = 16
    best = rows
    while t <= want:
        if rows % t == 0:
            best = t
        t *= 2
    return best


def _nt(a, b):
    return lax.dot_general(a, b, (((1,), (1,)), ((), ())), preferred_element_type=F32)


def _tn(a, b):
    return lax.dot_general(a, b, (((0,), (0,)), ((), ())), preferred_element_type=F32)


def _nn(a, b):
    return jnp.dot(a, b, preferred_element_type=F32)


def _sigmoid(x):
    return 1.0 / (1.0 + jnp.exp(-x))


class _Src(NamedTuple):
    arr: jax.Array
    width: int
    col0: int = 0
    per_head: bool = True

    def col(self, h):
        return self.col0 + h if self.per_head else self.col0


def _mm(a, b, name, mode="nn", out_dtype=F32, second_dtype=None):
    if mode == "tn":
        k, m = a.shape
    else:
        m, k = a.shape
    if mode == "nt":
        n, k2 = b.shape
    else:
        k2, n = b.shape
    assert k == k2, (a.shape, b.shape, mode)
    tm, tn, tk = _tile(m, 1024), _tile(n, 1024), _tile(k, 2048)
    nk = k // tk
    assert nk == 1 or (out_dtype == F32 and second_dtype is None)
    dot = {"nn": _nn, "nt": _nt, "tn": _tn}[mode]

    def kern(a_ref, b_ref, o_ref, *more):
        r = dot(a_ref[...].astype(BF16), b_ref[...].astype(BF16))
        if nk == 1:
            o_ref[...] = r.astype(out_dtype)
            if more:
                more[0][...] = r.astype(second_dtype)
        else:
            kk = pl.program_id(2)

            @pl.when(kk == 0)
            def _():
                o_ref[...] = r

            @pl.when(kk > 0)
            def _():
                o_ref[...] += r

    a_spec = (pl.BlockSpec((tk, tm), lambda i, j, kk: (kk, i)) if mode == "tn"
              else pl.BlockSpec((tm, tk), lambda i, j, kk: (i, kk)))
    b_spec = (pl.BlockSpec((tn, tk), lambda i, j, kk: (j, kk)) if mode == "nt"
              else pl.BlockSpec((tk, tn), lambda i, j, kk: (kk, j)))
    o_spec = pl.BlockSpec((tm, tn), lambda i, j, kk: (i, j))
    out_shape = jax.ShapeDtypeStruct((m, n), out_dtype)
    if second_dtype is not None:
        o_spec = [o_spec, o_spec]
        out_shape = [out_shape, jax.ShapeDtypeStruct((m, n), second_dtype)]
    return _pcall(
        kern, name=name, grid=(m // tm, n // tn, nk), in_specs=[a_spec, b_spec], out_specs=o_spec,
        out_shape=out_shape,
        compiler_params=pltpu.CompilerParams(dimension_semantics=("parallel", "parallel", "arbitrary")),
    )(a, b)


def _rms_fwd(x, g, name):
    r, d = x.arr.shape[0], x.width
    tr = _tile(r, 512)

    def kern(x_ref, g_ref, y_ref, r_ref):
        xv = x_ref[...]
        rstd = lax.rsqrt(jnp.mean(xv * xv, axis=-1, keepdims=True) + EPS)
        y_ref[...] = (xv * rstd * g_ref[...]).astype(BF16)
        r_ref[...] = rstd

    return _pcall(
        kern, name=name, grid=(r // tr,),
        in_specs=[pl.BlockSpec((tr, d), lambda i: (i, x.col0)), pl.BlockSpec((1, d), lambda i: (0, 0))],
        out_specs=[pl.BlockSpec((tr, d), lambda i: (i, 0)), pl.BlockSpec((tr, 1), lambda i: (i, 0))],
        out_shape=[jax.ShapeDtypeStruct((r, d), BF16), jax.ShapeDtypeStruct((r, 1), F32)],
    )(x.arr, g)


def _rms_bwd(x, g, rstd, dy, add, name):
    r, d = x.arr.shape[0], x.width
    tr = _tile(r, 256)
    has_add = add is not None

    def kern(*refs):
        if has_add:
            x_ref, g_ref, r_ref, dy_ref, add_ref, dx_ref, dg_ref = refs
        else:
            x_ref, g_ref, r_ref, dy_ref, dx_ref, dg_ref = refs
        rs = r_ref[...]
        xhat = x_ref[...] * rs
        dyv = dy_ref[...]
        dyg = dyv * g_ref[...]
        c = jnp.mean(dyg * xhat, axis=-1, keepdims=True)
        dx = rs * (dyg - xhat * c)
        if has_add:
            dx = dx + add_ref[...]
        dx_ref[...] = dx
        part = jnp.sum(dyv * xhat, axis=0, keepdims=True)

        @pl.when(pl.program_id(0) == 0)
        def _():
            dg_ref[...] = part

        @pl.when(pl.program_id(0) > 0)
        def _():
            dg_ref[...] += part

    row = pl.BlockSpec((tr, d), lambda i: (i, 0))
    vec = pl.BlockSpec((1, d), lambda i: (0, 0))
    ins = [pl.BlockSpec((tr, d), lambda i: (i, x.col0)), vec, pl.BlockSpec((tr, 1), lambda i: (i, 0)), row]
    ins += [row] if has_add else []
    args = (x.arr, g, rstd, dy) + ((add,) if has_add else ())
    return _pcall(
        kern, name=name, grid=(r // tr,), in_specs=ins, out_specs=[row, vec],
        out_shape=[jax.ShapeDtypeStruct((r, d), F32), jax.ShapeDtypeStruct((1, d), F32)],
        compiler_params=pltpu.CompilerParams(dimension_semantics=("arbitrary",)),
    )(*args)


def _final_loss(x, proj, g, target):
    r, d = x.shape
    tr = _tile(r, 256)

    def kern(x_ref, p_ref, g_ref, t_ref, dy_ref, dg_ref, loss_ref):
        y = x_ref[...] + p_ref[...]
        rs = lax.rsqrt(jnp.mean(y * y, axis=-1, keepdims=True) + EPS)
        yhat = y * rs
        gv = g_ref[...]
        e = yhat * gv - t_ref[...]
        lpart = 0.5 * jnp.sum(jnp.mean(e * e, axis=-1, keepdims=True), axis=0, keepdims=True)
        dout = e * (1.0 / d)
        dyg = dout * gv
        c = jnp.mean(dyg * yhat, axis=-1, keepdims=True)
        dy_ref[...] = rs * (dyg - yhat * c)
        gpart = jnp.sum(dout * yhat, axis=0, keepdims=True)
        lrow = jnp.broadcast_to(lpart, (1, LANE))

        @pl.when(pl.program_id(0) == 0)
        def _():
            dg_ref[...] = gpart
            loss_ref[...] = lrow

        @pl.when(pl.program_id(0) > 0)
        def _():
            dg_ref[...] += gpart
            loss_ref[...] += lrow

    row = pl.BlockSpec((tr, d), lambda i: (i, 0))
    vec = pl.BlockSpec((1, d), lambda i: (0, 0))
    return _pcall(
        kern, name="final_loss", grid=(r // tr,), in_specs=[row, row, vec, row],
        out_specs=[row, vec, pl.BlockSpec((1, LANE), lambda i: (0, 0))],
        out_shape=[jax.ShapeDtypeStruct((r, d), F32), jax.ShapeDtypeStruct((1, d), F32),
                   jax.ShapeDtypeStruct((1, LANE), F32)],
        compiler_params=pltpu.CompilerParams(dimension_semantics=("arbitrary",)),
    )(x, proj, g, target)


def _rope_fwd(x, cs, sn, nh, width, off, name):
    s = x.arr.shape[0]
    tr = _tile(s, 512)

    def kern(x_ref, c_ref, s_ref, o_ref):
        cv, sv = c_ref[...], s_ref[...]
        for h in range(nh):
            b = h * width
            if off:
                o_ref[:, b:b + off] = x_ref[:, b:b + off].astype(BF16)
            xr = x_ref[:, b + off:b + off + LANE]
            o_ref[:, b + off:b + off + LANE] = (xr * cv + pltpu.roll(xr, 32, 1) * sv).astype(BF16)

    tab = pl.BlockSpec((tr, LANE), lambda i: (i, 0))
    return _pcall(
        kern, name=name, grid=(s // tr,),
        in_specs=[pl.BlockSpec((tr, nh * width), lambda i: (i, x.col0)), tab, tab],
        out_specs=pl.BlockSpec((tr, nh * width), lambda i: (i, 0)),
        out_shape=jax.ShapeDtypeStruct((s, nh * width), BF16),
    )(x.arr, cs, sn)


def _rope_grad(d, cv, sv):
    g2 = d * sv
    g2 = g2 + pltpu.roll(g2, 64, 1)
    lane = lax.broadcasted_iota(jnp.int32, d.shape, 1)
    return jnp.where(lane < 64, d * cv + pltpu.roll(g2, 32, 1), 0.0)


def _rope_bwd_q(dq, cs, sn):
    s, w = dq.shape
    tr = _tile(s, 512)
    nh = w // 256

    def kern(d_ref, c_ref, s_ref, o_ref):
        cv, sv = c_ref[...], s_ref[...]
        for h in range(nh):
            b = h * 256
            o_ref[:, b:b + LANE] = d_ref[:, b:b + LANE]
            o_ref[:, b + LANE:b + 256] = _rope_grad(d_ref[:, b + LANE:b + 256], cv, sv)

    row = pl.BlockSpec((tr, w), lambda i: (i, 0))
    tab = pl.BlockSpec((tr, LANE), lambda i: (i, 0))
    return _pcall(kern, name="rope_bwd_q", grid=(s // tr,), in_specs=[row, tab, tab], out_specs=row,
                  out_shape=jax.ShapeDtypeStruct((s, w), F32))(dq, cs, sn)


def _rope_bwd_k(dk_nope, dk_pe, dv, cs, sn):
    s, w = dk_nope.shape
    tr = _tile(s, 512)

    def kern(dk_ref, dp_ref, dv_ref, c_ref, s_ref, okv_ref, okr_ref):
        okv_ref[:, :w] = dk_ref[...]
        okv_ref[:, w:] = dv_ref[...]
        okr_ref[...] = _rope_grad(dp_ref[...], c_ref[...], s_ref[...])

    tab = pl.BlockSpec((tr, LANE), lambda i: (i, 0))
    wide = pl.BlockSpec((tr, w), lambda i: (i, 0))
    return _pcall(
        kern, name="rope_bwd_k", grid=(s // tr,), in_specs=[wide, tab, wide, tab, tab],
        out_specs=[pl.BlockSpec((tr, 2 * w), lambda i: (i, 0)), tab],
        out_shape=[jax.ShapeDtypeStruct((s, 2 * w), F32), jax.ShapeDtypeStruct((s, LANE), F32)],
    )(dk_nope, dk_pe, dv, cs, sn)


class _Attn:
    def __init__(self, mode, s, sk, heads, dk):
        self.mode, self.s, self.sk, self.h, self.dk = mode, s, sk, heads, dk
        self.scale = {"mla": 192 ** -0.5, "mem": 128 ** -0.5}.get(mode, NSA_DK ** -0.5)
        self.tb = min(256, s)
        self.nb = s // self.tb
        self.nsub = 2 if self.nb % 2 == 0 else 1
        self.tq = self.tb * self.nsub
        self.nq = s // self.tq
        self.causal = mode in ("mla", "slc")
        if self.causal:
            self.tk = self.tq
        elif mode == "win":
            self.tk = WIN + self.tb
        else:
            self.tk = sk
        self.tkb = min(512, sk)
        self.ksub = 2 if self.tkb == 512 and mode == "mla" else 1
        self.kb = self.tkb // self.ksub
        self.ncmp = s // CMP_STRIDE - 1

    def mask_bias(self, t, n, h, selx, diag):
        m = self.mode
        if m == "mla":
            return (n <= t) if diag else None, None
        if m == "mem":
            return None, None
        slope = jnp.where(h == 0, 0.25, jnp.where(h == 1, 0.0625, jnp.where(h == 2, 0.015625, 0.00390625)))
        slope = slope.astype(F32) * LOG2E
        if m == "cmp":
            mask = (n * CMP_STRIDE + (CMP_LEN - 1) <= t) & (n < self.ncmp)
            pos = n.astype(F32) * float(CMP_STRIDE) + (CMP_LEN - 1) / 2.0
            return mask, slope * pos
        rel = t - n
        if m == "slc":
            return (rel >= 0) if diag else None, slope * n.astype(F32)
        return (rel >= 0) & (rel < WIN), slope * n.astype(F32)


def _scores(cfg, s_raw, t, n, h, selx, diag, lse=None):
    s = s_raw * (cfg.scale * LOG2E)
    mask, key_term = cfg.mask_bias(t, n, h, selx, diag)
    if key_term is not None:
        s = s + key_term
    if selx is not None:
        s = s + selx
    if lse is None:
        if mask is not None:
            s = jnp.where(mask, s, NEG)
        return s, mask
    p = jnp.exp2(jnp.minimum(s - lse, 0.0))
    if mask is not None:
        p = jnp.where(mask, p, 0.0)
    return p, mask


def _block_of_key(k0, tk, keys_on_rows):
    shape = (tk, LANE) if keys_on_rows else (LANE, tk)
    n = lax.broadcasted_iota(jnp.int32, shape, 0 if keys_on_rows else 1) + k0
    j = lax.broadcasted_iota(jnp.int32, shape, 1 if keys_on_rows else 0)
    return jnp.where((n >> 6) == j, NEG, 0.0).astype(BF16)


def _to_row(col):
    t = col.shape[0]
    return jnp.transpose(jnp.broadcast_to(col, (t, LANE)))[0:1, :]


def _load_keys(k_refs, rows):
    parts = [r[rows, :].astype(BF16) for r in k_refs]
    return parts[0] if len(parts) == 1 else jnp.concatenate(parts, axis=1)


def _attn_fwd(cfg, q, ks, v, sel, name):
    s, tq, tk, tb, nsub = cfg.s, cfg.tq, cfg.tk, cfg.tb, cfg.nsub
    has_sel = sel is not None
    nkp = len(ks)

    def kern(*refs):
        q_ref, k_refs, v_ref = refs[0], refs[1:1 + nkp], refs[1 + nkp]
        sel_ref = refs[2 + nkp] if has_sel else None
        o_ref, lc_ref, lr_ref = refs[2 + nkp + has_sel:5 + nkp + has_sel]
        h, i = pl.program_id(0), pl.program_id(1)
        part = [slice(r * tb, (r + 1) * tb) for r in range(nsub)]
        qs = [q_ref[p, :].astype(BF16) for p in part]
        ts = [i * tq + r * tb + lax.broadcasted_iota(jnp.int32, (tb, 1), 0) for r in range(nsub)]
        sels = [sel_ref[p, :].astype(BF16) for p in part] if has_sel else None

        def load(k0):
            rows = pl.ds(k0, tk)
            return _load_keys(k_refs, rows), v_ref[rows, :].astype(BF16)

        def soft(r, k0, s_raw, vv, emat, carry, diag):
            m, l, acc = carry
            n = k0 + lax.broadcasted_iota(jnp.int32, (1, tk), 1)
            selx = _nn(sels[r], emat) if has_sel else None
            sc, mask = _scores(cfg, s_raw, ts[r], n, h, selx, diag)
            m_new = jnp.maximum(m, jnp.max(sc, axis=1, keepdims=True))
            alpha = jnp.exp2(m - m_new)
            p = jnp.exp2(sc - m_new)
            if mask is not None:
                p = jnp.where(mask, p, 0.0)
            l = alpha * l + jnp.sum(p, axis=1, keepdims=True)
            acc = alpha * acc + _nn(p.astype(BF16), vv)
            return m_new, l, acc

        def step(r, k0, kk, vv, emat, carry, diag):
            return soft(r, k0, _nt(qs[r], kk), vv, emat, carry, diag)

        def chunk(k0, carry, diag):
            kk, vv = load(k0)
            emat = _block_of_key(k0, tk, False) if has_sel else None
            return tuple(step(r, k0, kk, vv, emat, carry[r], diag) for r in range(nsub))

        init = (jnp.full((tb, 1), NEG, F32), jnp.zeros((tb, 1), F32), jnp.zeros((tb, HEAD_V), F32))
        carry = (init,) * nsub
        if cfg.causal:
            buf_a, buf_b = refs[-2:]
            full = (i * tq) // tk

            def scores_into(buf, c):
                kk = _load_keys(k_refs, pl.ds(pl.multiple_of(c * tk, tk), tk))
                for r in range(nsub):
                    buf[r] = _nt(qs[r], kk)

            def consume(buf, c, cr, diag):
                k0 = pl.multiple_of(c * tk, tk)
                vv = v_ref[pl.ds(k0, tk), :].astype(BF16)
                emat = _block_of_key(k0, tk, False) if has_sel else None
                return tuple(soft(r, k0, buf[r], vv, emat, cr[r], diag) for r in range(nsub))

            def pair(p, cr):
                scores_into(buf_b, 2 * p + 1)
                cr = consume(buf_a, 2 * p, cr, False)
                scores_into(buf_a, 2 * p + 2)
                return consume(buf_b, 2 * p + 1, cr, False)

            def odd_tail(cr):
                scores_into(buf_b, full)
                cr = consume(buf_a, full - 1, cr, False)
                return consume(buf_b, full, cr, True)

            scores_into(buf_a, 0)
            carry = lax.fori_loop(0, full // 2, pair, carry)
            carry = lax.cond(full % 2 == 1, odd_tail, lambda cr: consume(buf_a, full, cr, True), carry)
        elif cfg.mode == "win":
            starts = [pl.multiple_of(jnp.maximum(i * tq + r * tb - WIN, 0), tb) for r in range(nsub)]
            carry = tuple(step(r, k0, *load(k0), None, carry[r], True) for r, k0 in enumerate(starts))
        else:
            carry = chunk(0, carry, True)
        for r, (m, l, acc) in enumerate(carry):
            o_ref[part[r], :] = acc / (l + 1e-20)
            lse = m + jnp.log(l + 1e-20) * LOG2E
            lc_ref[0, part[r], :] = lse
            lr_ref[0, r] = _to_row(lse)

    ins = [pl.BlockSpec((tq, q.width), lambda h, i: (i, q.col(h)))]
    ins += [pl.BlockSpec((cfg.sk, p.width), lambda h, i, p=p: (0, p.col(h))) for p in ks]
    ins += [pl.BlockSpec((cfg.sk, HEAD_V), lambda h, i: (0, v.col(h)))]
    args = [q.arr] + [p.arr for p in ks] + [v.arr]
    if has_sel:
        ins.append(pl.BlockSpec((tq, LANE), lambda h, i: (i, 0)))
        args.append(sel)
    return _pcall(
        kern, name=name, grid=(cfg.h, cfg.nq), in_specs=ins,
        out_specs=[pl.BlockSpec((tq, HEAD_V), lambda h, i: (i, h)),
                   pl.BlockSpec((1, tq, 1), lambda h, i: (h, i, 0)),
                   pl.BlockSpec((1, nsub, 1, tb), lambda h, i: (h, i, 0, 0))],
        out_shape=[jax.ShapeDtypeStruct((s, cfg.h * HEAD_V), F32),
                   jax.ShapeDtypeStruct((cfg.h, s, 1), F32),
                   jax.ShapeDtypeStruct((cfg.h, cfg.nb, 1, tb), F32)],
        scratch_shapes=[pltpu.VMEM((nsub, tb, tk), F32)] * 2 if cfg.causal else [],
        compiler_params=pltpu.CompilerParams(dimension_semantics=("parallel", "parallel")),
    )(*args)


def _attn_dq(cfg, q, ks, v, sel, o, lse, do, dq_in, name):
    s, tq, tk, dk, tb, nsub = cfg.s, cfg.tq, cfg.tk, cfg.dk, cfg.tb, cfg.nsub
    has_sel = sel is not None
    has_in = dq_in is not None
    nkp = len(ks)

    def kern(*refs):
        refs = list(refs)
        q_ref, k_refs, v_ref = refs[0], refs[1:1 + nkp], refs[1 + nkp]
        p0 = 2 + nkp
        sel_ref = refs[p0] if has_sel else None
        p0 += has_sel
        o_ref, l_ref, do_ref = refs[p0:p0 + 3]
        p0 += 3
        in_ref = refs[p0] if has_in else None
        p0 += has_in
        dq_ref, dr_ref = refs[p0:p0 + 2]
        h, i = pl.program_id(0), pl.program_id(1)
        part = [slice(r * tb, (r + 1) * tb) for r in range(nsub)]
        qs = [q_ref[p, :].astype(BF16) for p in part]
        ts = [i * tq + r * tb + lax.broadcasted_iota(jnp.int32, (tb, 1), 0) for r in range(nsub)]
        sels = [sel_ref[p, :].astype(BF16) for p in part] if has_sel else None
        dvecs, dobs, lses = [], [], []
        for r, p in enumerate(part):
            dov = do_ref[p, :]
            dvec = jnp.sum(dov * o_ref[p, :], axis=1, keepdims=True)
            dr_ref[0, r] = _to_row(dvec)
            dvecs.append(dvec)
            dobs.append(dov.astype(BF16))
            lses.append(l_ref[0, p, :])

        def load(k0):
            rows = pl.ds(k0, tk)
            return _load_keys(k_refs, rows), v_ref[rows, :].astype(BF16)

        def grad(r, k0, s_raw, dp, kk, emat, acc, diag):
            n = k0 + lax.broadcasted_iota(jnp.int32, (1, tk), 1)
            selx = _nn(sels[r], emat) if has_sel else None
            p, _ = _scores(cfg, s_raw, ts[r], n, h, selx, diag, lses[r])
            ds = p * (dp - dvecs[r])
            return acc + _nn(ds.astype(BF16), kk)

        def step(r, k0, kk, vv, emat, acc, diag):
            return grad(r, k0, _nt(qs[r], kk), _nt(dobs[r], vv), kk, emat, acc, diag)

        def chunk(k0, accs, diag):
            kk, vv = load(k0)
            emat = _block_of_key(k0, tk, False) if has_sel else None
            return tuple(step(r, k0, kk, vv, emat, accs[r], diag) for r in range(nsub))

        accs = (jnp.zeros((tb, dk), F32),) * nsub
        if cfg.causal:
            sa, pa, sb, pb = refs[-4:]
            full = (i * tq) // tk

            def products_into(sbuf, pbuf, c):
                kk, vv = load(pl.multiple_of(c * tk, tk))
                for r in range(nsub):
                    sbuf[r] = _nt(qs[r], kk)
                    pbuf[r] = _nt(dobs[r], vv)

            def consume(sbuf, pbuf, c, ac, diag):
                k0 = pl.multiple_of(c * tk, tk)
                kk = _load_keys(k_refs, pl.ds(k0, tk))
                emat = _block_of_key(k0, tk, False) if has_sel else None
                return tuple(grad(r, k0, sbuf[r], pbuf[r], kk, emat, ac[r], diag) for r in range(nsub))

            def pair(p, ac):
                products_into(sb, pb, 2 * p + 1)
                ac = consume(sa, pa, 2 * p, ac, False)
                products_into(sa, pa, 2 * p + 2)
                return consume(sb, pb, 2 * p + 1, ac, False)

            def odd_tail(ac):
                products_into(sb, pb, full)
                ac = consume(sa, pa, full - 1, ac, False)
                return consume(sb, pb, full, ac, True)

            products_into(sa, pa, 0)
            accs = lax.fori_loop(0, full // 2, pair, accs)
            accs = lax.cond(full % 2 == 1, odd_tail, lambda ac: consume(sa, pa, full, ac, True), accs)
        elif cfg.mode == "win":
            starts = [pl.multiple_of(jnp.maximum(i * tq + r * tb - WIN, 0), tb) for r in range(nsub)]
            accs = tuple(step(r, k0, *load(k0), None, accs[r], True) for r, k0 in enumerate(starts))
        else:
            accs = chunk(0, accs, True)
        for r, p in enumerate(part):
            dq_ref[p, :] = accs[r] * cfg.scale + in_ref[p, :] if has_in else accs[r] * cfg.scale

    qs = pl.BlockSpec((tq, dk), lambda h, i: (i, h))
    ins = [pl.BlockSpec((tq, q.width), lambda h, i: (i, q.col(h)))]
    ins += [pl.BlockSpec((cfg.sk, p.width), lambda h, i, p=p: (0, p.col(h))) for p in ks]
    ins += [pl.BlockSpec((cfg.sk, HEAD_V), lambda h, i: (0, v.col(h)))]
    args = [q.arr] + [p.arr for p in ks] + [v.arr]
    if has_sel:
        ins.append(pl.BlockSpec((tq, LANE), lambda h, i: (i, 0)))
        args.append(sel)
    ins += [pl.BlockSpec((tq, HEAD_V), lambda h, i: (i, o.col(h))),
            pl.BlockSpec((1, tq, 1), lambda h, i: (h, i, 0)),
            pl.BlockSpec((tq, HEAD_V), lambda h, i: (i, do.col(h)))]
    args += [o.arr, lse, do.arr]
    if has_in:
        ins.append(qs)
        args.append(dq_in)
    return _pcall(
        kern, name=name, grid=(cfg.h, cfg.nq), in_specs=ins,
        out_specs=[qs, pl.BlockSpec((1, nsub, 1, tb), lambda h, i: (h, i, 0, 0))],
        out_shape=[jax.ShapeDtypeStruct((s, cfg.h * dk), F32),
                   jax.ShapeDtypeStruct((cfg.h, cfg.nb, 1, tb), F32)],
        scratch_shapes=[pltpu.VMEM((nsub, tb, tk), F32)] * 4 if cfg.causal else [],
        compiler_params=pltpu.CompilerParams(dimension_semantics=("parallel", "parallel")),
    )(*args)


def _attn_dkv(cfg, q, ks, v, selt, lse_r, d_r, do, name):
    s, tq, tkb, dk, kb, ksub = cfg.s, cfg.tb, cfg.tkb, cfg.dk, cfg.kb, cfg.ksub
    nq = cfg.nb
    has_sel = selt is not None
    nkp = len(ks)
    outs = list(ks) + [v]

    def kern(*refs):
        k_refs, v_ref = refs[:nkp], refs[nkp]
        q_ref, do_ref, lr_ref, dr_ref = refs[nkp + 1:nkp + 5]
        st_ref = refs[nkp + 5] if has_sel else None
        out_refs = refs[nkp + 5 + has_sel:2 * nkp + 6 + has_sel]
        sa, pa, sb, pb = refs[-4:]
        j, h = pl.program_id(0), pl.program_id(1)
        k0 = j * tkb
        part = [slice(u * kb, (u + 1) * kb) for u in range(ksub)]
        kks = [_load_keys(k_refs, p) for p in part]
        vvs = [v_ref[p, :].astype(BF16) for p in part]
        ns = [k0 + u * kb + lax.broadcasted_iota(jnp.int32, (kb, 1), 0) for u in range(ksub)]
        emats = [_block_of_key(k0 + u * kb, kb, True) for u in range(ksub)] if has_sel else None

        def load_q(i):
            rows = pl.ds(pl.multiple_of(i * tq, tq), tq)
            return q_ref[rows, :].astype(BF16), do_ref[rows, :].astype(BF16)

        def products_into(sbuf, pbuf, i):
            qi, doi = load_q(i)
            for u in range(ksub):
                sbuf[u] = _nt(kks[u], qi)
                pbuf[u] = _nt(vvs[u], doi)

        def consume(sbuf, pbuf, i, carry):
            qi, doi = load_q(i)
            t = i * tq + lax.broadcasted_iota(jnp.int32, (1, tq), 1)
            selt_i = st_ref[i].astype(BF16) if has_sel else None
            new = []
            for u in range(ksub):
                dk_acc, dv_acc = carry[u]
                selx = _nn(emats[u], selt_i) if has_sel else None
                pt, _ = _scores(cfg, sbuf[u], t, ns[u], h, selx, True, lr_ref[0, i])
                dv_acc = dv_acc + _nn(pt.astype(BF16), doi)
                dst = pt * (pbuf[u] - dr_ref[0, i])
                new.append((dk_acc + _nn(dst.astype(BF16), qi), dv_acc))
            return tuple(new)

        if cfg.causal:
            first, count = k0 // tq, nq - k0 // tq
        elif cfg.mode == "win":
            first = k0 // tq
            count = jnp.minimum((k0 + tkb + WIN - 2) // tq + 1, nq) - first
        else:
            first, count = 0, nq

        def pair(p, cr):
            i0 = first + 2 * p
            products_into(sb, pb, i0 + 1)
            cr = consume(sa, pa, i0, cr)
            products_into(sa, pa, i0 + 2)
            return consume(sb, pb, i0 + 1, cr)

        carry = ((jnp.zeros((kb, dk), F32), jnp.zeros((kb, HEAD_V), F32)),) * ksub
        products_into(sa, pa, first)
        carry = lax.fori_loop(0, count // 2 - 1, pair, carry)
        last = first + count - 2
        products_into(sb, pb, last + 1)
        carry = consume(sa, pa, last, carry)
        carry = consume(sb, pb, last + 1, carry)
        for u, (dk_acc, dv_acc) in enumerate(carry):
            vals, off = [], 0
            for p in ks:
                vals.append(dk_acc[:, off:off + p.width] * cfg.scale)
                off += p.width
            vals.append(dv_acc)
            for src, ref, val in zip(outs, out_refs, vals):
                if src.per_head:
                    ref[part[u], :] = val
                else:
                    @pl.when(h == 0)
                    def _(ref=ref, val=val, u=u):
                        ref[part[u], :] = val

                    @pl.when(h > 0)
                    def _(ref=ref, val=val, u=u):
                        ref[part[u], :] += val

    rowv = pl.BlockSpec((1, nq, 1, tq), lambda j, h: (h, 0, 0, 0))
    ins = [pl.BlockSpec((tkb, p.width), lambda j, h, p=p: (j, p.col(h))) for p in ks]
    ins += [pl.BlockSpec((tkb, HEAD_V), lambda j, h: (j, v.col(h))),
            pl.BlockSpec((s, q.width), lambda j, h: (0, q.col(h))),
            pl.BlockSpec((s, HEAD_V), lambda j, h: (0, do.col(h))), rowv, rowv]
    args = [p.arr for p in ks] + [v.arr, q.arr, do.arr, lse_r, d_r]
    if has_sel:
        ins.append(pl.BlockSpec((nq, LANE, tq), lambda j, h: (0, 0, 0)))
        args.append(selt)
    out_specs = [pl.BlockSpec((tkb, p.width), lambda j, h, p=p: (j, h if p.per_head else 0)) for p in outs]
    out_shape = [jax.ShapeDtypeStruct((cfg.sk, (cfg.h if p.per_head else 1) * p.width), F32) for p in outs]
    assert nq % 2 == 0 and (cfg.mode in ("cmp", "mem") or tkb % (2 * tq) == 0), (nq, tkb, tq)
    return _pcall(
        kern, name=name, grid=(cfg.sk // tkb, cfg.h), in_specs=ins, out_specs=out_specs, out_shape=out_shape,
        scratch_shapes=[pltpu.VMEM((ksub, kb, tq), F32)] * 4,
        compiler_params=pltpu.CompilerParams(dimension_semantics=("parallel", "arbitrary")),
    )(*args)


def _attn_bwd(cfg, q, ks, v, sel, selt, o, lse, lse_r, do, dq_in, name):
    dq, d_r = _attn_dq(cfg, q, ks, v, sel, o, lse, do, dq_in, name + "_dq")
    res = _attn_dkv(cfg, q, ks, v, selt, lse_r, d_r, do, name + "_dkv")
    return dq, res[:-1], res[-1]


def _select(cfg, q, k_cmp, overlap):
    s, tq, sk = cfg.s, cfg.tb, cfg.sk
    n_s = s // SLC_LEN
    top_n = min(SLC_TOPN, n_s)

    def kern(q_ref, k_ref, ov_ref, sel_ref, selt_ref):
        i = pl.program_id(0)
        t = i * tq + lax.broadcasted_iota(jnp.int32, (tq, 1), 0)
        n = lax.broadcasted_iota(jnp.int32, (1, sk), 1)
        kk = k_ref[...]
        imp = jnp.zeros((tq, LANE), F32)
        for h in range(NSA_HEADS):
            sc, mask = _scores(cfg, _nt(q_ref[:, h * 256:(h + 1) * 256], kk), t, n, h, None, True)
            m = jnp.max(sc, axis=1, keepdims=True)
            e = jnp.where(mask, jnp.exp2(sc - m), 0.0)
            p = e / (jnp.sum(e, axis=1, keepdims=True) + 1e-20)
            imp = imp + _nn(p.astype(BF16), ov_ref[...])
        j = lax.broadcasted_iota(jnp.int32, (tq, LANE), 1)
        cur = t >> 6
        forced = (j == 0) | (j == cur) | (j == cur - 1)
        imp = jnp.where(forced, 1e9, imp)
        imp = jnp.where(j > cur, -1e9, imp)
        imp = jnp.where(j >= n_s, -3e38, imp)

        def pick(_, carry):
            work, chosen = carry
            mx = jnp.max(work, axis=1, keepdims=True)
            first = jnp.min(jnp.where(work == mx, j, LANE), axis=1, keepdims=True)
            hit = j == first
            return jnp.where(hit, -3e38, work), jnp.where(hit, 1.0, chosen)

        _, chosen = lax.fori_loop(0, top_n, pick, (imp, jnp.zeros((tq, LANE), F32)))
        rejected = 1.0 - jnp.where(j <= cur, chosen, 0.0)
        sel_ref[...] = rejected
        selt_ref[0] = jnp.transpose(rejected)

    return _pcall(
        kern, name="nsa_select", grid=(cfg.nb,),
        in_specs=[pl.BlockSpec((tq, NSA_HEADS * 256), lambda i: (i, q.col0)),
                  pl.BlockSpec((sk, 256), lambda i: (0, 0)), pl.BlockSpec((sk, LANE), lambda i: (0, 0))],
        out_specs=[pl.BlockSpec((tq, LANE), lambda i: (i, 0)), pl.BlockSpec((1, LANE, tq), lambda i: (i, 0, 0))],
        out_shape=[jax.ShapeDtypeStruct((s, LANE), F32), jax.ShapeDtypeStruct((cfg.nb, LANE, tq), F32)],
    )(q.arr, k_cmp, overlap)


def _silu_grad(pre):
    sg = _sigmoid(pre)
    return sg * (1.0 + pre * (1.0 - sg))


def _compress_fwd(a_lo, a_hi, pe_lo, pe_hi, w1_lo, w1_hi, w2, name):
    n, dp = a_lo.shape[0], w2.shape[1]

    def kern(alo, ahi, plo, phi, w1l, w1h, w2r, out_ref, pre_ref):
        xl = (alo[...] + plo[...]).astype(BF16)
        xh = (ahi[...] + phi[...]).astype(BF16)
        pre = _nn(xl, w1l[...]) + _nn(xh, w1h[...])
        act = pre * _sigmoid(pre)
        out_ref[...] = _nn(act.astype(BF16), w2r[...]).astype(BF16)
        pre_ref[...] = pre

    return _pcall(kern, name=name,
                  out_shape=[jax.ShapeDtypeStruct((n, dp), BF16), jax.ShapeDtypeStruct((n, dp), F32)],
                  )(a_lo, a_hi, pe_lo, pe_hi, w1_lo, w1_hi, w2)


def _compress_bwd(a_lo, a_hi, pe_lo, pe_hi, w1_lo, w1_hi, w2, pre, pre_sh, dout, dout_sh, name):
    n, ln = a_lo.shape
    dp = w2.shape[1]

    def kern(alo, ahi, plo, phi, w1l, w1h, w2r, pre_ref, presh_ref, do_ref, dosh_ref,
             da_ref, dpl_ref, dph_ref, dw1l_ref, dw1h_ref, dw2_ref):
        prev = pre_ref[...]
        act = prev * _sigmoid(prev)
        dob = do_ref[...].astype(BF16)
        w2v = w2r[...]
        dpre = (_nt(dob, w2v) * _silu_grad(prev)).astype(BF16)
        dpre_sh = (_nt(dosh_ref[...].astype(BF16), w2v) * _silu_grad(presh_ref[...])).astype(BF16)
        dw2_ref[...] = _nn(act.T.astype(BF16), dob)
        xl = alo[...] + plo[...]
        xh = ahi[...] + phi[...]
        dw1l_ref[...] = _nn(xl.T.astype(BF16), dpre)
        dw1h_ref[...] = _nn(xh.T.astype(BF16), dpre)
        dal = _nt(dpre, w1l[...])
        dah_sh = _nt(dpre_sh, w1h[...])
        da_ref[...] = dal + dah_sh
        dpl_ref[...] = jnp.sum(dal, axis=0, keepdims=True)
        dph_ref[...] = jnp.sum(dah_sh, axis=0, keepdims=True)

    return _pcall(
        kern, name=name,
        out_shape=[jax.ShapeDtypeStruct((n, ln), F32), jax.ShapeDtypeStruct((1, ln), F32),
                   jax.ShapeDtypeStruct((1, ln), F32), jax.ShapeDtypeStruct((ln, dp), F32),
                   jax.ShapeDtypeStruct((ln, dp), F32), jax.ShapeDtypeStruct((dp, dp), F32)],
    )(a_lo, a_hi, pe_lo, pe_hi, w1_lo, w1_hi, w2, pre, pre_sh, dout, dout_sh)


def _nsa_combine(o_cmp, o_slc, o_win, gl):
    s, w = o_cmp.shape
    tr = _tile(s, 512)

    def kern(a_ref, b_ref, c_ref, g_ref, o_ref):
        g = _sigmoid(g_ref[...])
        for h in range(NSA_HEADS):
            cs = slice(h * HEAD_V, (h + 1) * HEAD_V)
            o_ref[:, cs] = (g[:, 3 * h:3 * h + 1] * a_ref[:, cs] + g[:, 3 * h + 1:3 * h + 2] * b_ref[:, cs]
                            + g[:, 3 * h + 2:3 * h + 3] * c_ref[:, cs])

    row = pl.BlockSpec((tr, w), lambda i: (i, 0))
    return _pcall(kern, name="nsa_combine", grid=(s // tr,),
                  in_specs=[row, row, row, pl.BlockSpec((tr, LANE), lambda i: (i, gl.col0))], out_specs=row,
                  out_shape=jax.ShapeDtypeStruct((s, w), F32))(o_cmp, o_slc, o_win, gl.arr)


def _nsa_combine_bwd(do_cat, o_cmp, o_slc, o_win, gl):
    s, w = o_cmp.shape
    tr = _tile(s, 512)

    def kern(d_ref, a_ref, b_ref, c_ref, g_ref, da_ref, db_ref, dc_ref, dg_ref):
        g = _sigmoid(g_ref[...])
        lane = lax.broadcasted_iota(jnp.int32, (tr, LANE), 1)
        dgl = jnp.zeros((tr, LANE), F32)
        for h in range(NSA_HEADS):
            cs = slice(h * HEAD_V, (h + 1) * HEAD_V)
            dv = d_ref[:, cs]
            for b, (src, dst) in enumerate(((a_ref, da_ref), (b_ref, db_ref), (c_ref, dc_ref))):
                gate = g[:, 3 * h + b:3 * h + b + 1]
                dst[:, cs] = gate * dv
                dgate = jnp.sum(dv * src[:, cs], axis=1, keepdims=True)
                dgl = jnp.where(lane == 3 * h + b, dgate * gate * (1.0 - gate), dgl)
        dg_ref[...] = dgl

    row = pl.BlockSpec((tr, w), lambda i: (i, 0))
    tab = pl.BlockSpec((tr, LANE), lambda i: (i, 0))
    return _pcall(kern, name="nsa_combine_bwd", grid=(s // tr,),
                  in_specs=[pl.BlockSpec((tr, w), lambda i: (i, 2)), row, row, row,
                            pl.BlockSpec((tr, LANE), lambda i: (i, gl.col0))],
                  out_specs=[row, row, row, tab],
                  out_shape=[jax.ShapeDtypeStruct((s, w), F32)] * 3 + [jax.ShapeDtypeStruct((s, LANE), F32)],
                  )(do_cat, o_cmp, o_slc, o_win, gl.arr)


def _gate_fwd(o_mla, o_nsa, o_mem, hp):
    s = o_mla.shape[0]
    tr = _tile(s, 256)

    def kern(a_ref, b_ref, c_ref, z_ref, u_ref):
        z = z_ref[...]
        sz = z * _sigmoid(z)
        u_ref[:, 0:1024] = (a_ref[...] * sz[:, 0:1024]).astype(BF16)
        u_ref[:, 1024:1536] = (b_ref[...] * sz[:, 1024:1536]).astype(BF16)
        u_ref[:, 1536:2048] = (c_ref[...] * sz[:, 1536:2048]).astype(BF16)

    return _pcall(
        kern, name="gate_fwd", grid=(s // tr,),
        in_specs=[pl.BlockSpec((tr, 1024), lambda i: (i, 0)), pl.BlockSpec((tr, 512), lambda i: (i, 0)),
                  pl.BlockSpec((tr, 512), lambda i: (i, 0)), pl.BlockSpec((tr, 2048), lambda i: (i, 2))],
        out_specs=pl.BlockSpec((tr, 2048), lambda i: (i, 0)),
        out_shape=jax.ShapeDtypeStruct((s, 2048), BF16))(o_mla, o_nsa, o_mem, hp)


def _gate_bwd(du, o_mla, o_nsa, o_mem, hp):
    s = du.shape[0]
    tr = _tile(s, 256)

    def kern(d_ref, a_ref, b_ref, c_ref, z_ref, do_ref, dz_ref):
        z = z_ref[...]
        sg = _sigmoid(z)
        sz = z * sg
        dsz = sg * (1.0 + z * (1.0 - sg))
        d = d_ref[...]
        do_ref[...] = d * sz
        dz_ref[:, 0:1024] = d[:, 0:1024] * a_ref[...] * dsz[:, 0:1024]
        dz_ref[:, 1024:1536] = d[:, 1024:1536] * b_ref[...] * dsz[:, 1024:1536]
        dz_ref[:, 1536:2048] = d[:, 1536:2048] * c_ref[...] * dsz[:, 1536:2048]

    wide = pl.BlockSpec((tr, 2048), lambda i: (i, 0))
    return _pcall(
        kern, name="gate_bwd", grid=(s // tr,),
        in_specs=[wide, pl.BlockSpec((tr, 1024), lambda i: (i, 0)), pl.BlockSpec((tr, 512), lambda i: (i, 0)),
                  pl.BlockSpec((tr, 512), lambda i: (i, 0)), pl.BlockSpec((tr, 2048), lambda i: (i, 2))],
        out_specs=[wide, wide],
        out_shape=[jax.ShapeDtypeStruct((s, 2048), F32)] * 2)(du, o_mla, o_nsa, o_mem, hp)


def _tile2d(rows, cols, arrays):
    if rows % 16 == 0:
        return _row_tile(rows, cols * arrays), cols
    want = max(LANE, BLOCK_BYTES // (rows * 4 * arrays) // LANE * LANE)
    tc = LANE
    for t in range(LANE, cols + 1, LANE):
        if cols % t == 0 and t <= want:
            tc = t
    return rows, tc


def _sum_slots(buf, name):
    n, rows, cols = buf.shape
    tr, tc = _tile2d(rows, cols, n)

    def kern(b_ref, o_ref):
        acc = b_ref[0].astype(F32)
        for i in range(1, n):
            acc = acc + b_ref[i].astype(F32)
        o_ref[...] = acc

    return _pcall(kern, name=name, grid=(rows // tr, cols // tc),
                  in_specs=[pl.BlockSpec((n, tr, tc), lambda i, j: (0, i, j))],
                  out_specs=pl.BlockSpec((tr, tc), lambda i, j: (i, j)),
                  out_shape=jax.ShapeDtypeStruct((rows, cols), F32))(buf)


def _pair_sum(g4, theirs, core, axis, name):
    n, rows, cols = theirs.shape
    tr, tc = _tile2d(rows, cols, 1)
    nbr, nbc = rows // tr, cols // tc

    def kern(c_ref, a_ref, b_ref, o_ref):
        o_ref[...] = (a_ref[...] + b_ref[...]).astype(BF16)

    blk = (1, tr, tc)
    mine = ((lambda s, i, j, c: (s, c[0] * nbr + i, j)) if axis == 0
            else (lambda s, i, j, c: (s, i, c[0] * nbc + j)))
    grid_spec = pltpu.PrefetchScalarGridSpec(
        num_scalar_prefetch=1, grid=(n, nbr, nbc),
        in_specs=[pl.BlockSpec(blk, mine), pl.BlockSpec(blk, lambda s, i, j, c: (s, i, j))],
        out_specs=pl.BlockSpec(blk, lambda s, i, j, c: (s, i, j)))
    return _pcall(kern, name=name, grid_spec=grid_spec,
                  out_shape=jax.ShapeDtypeStruct((n, rows, cols), BF16))(core, g4, theirs)


def _adamw(w, g, m, v, name):
    rows, cols = w.shape
    tr, tc = _tile2d(rows, cols, 4)
    bc1 = 1.0 - ADAM_B1 ** ADAM_STEP
    bc2 = 1.0 - ADAM_B2 ** ADAM_STEP

    def kern(w_ref, g_ref, m_ref, v_ref, d_ref, mo_ref, vo_ref):
        gv = g_ref[...]
        mn = ADAM_B1 * m_ref[...] + (1.0 - ADAM_B1) * gv
        vn = ADAM_B2 * v_ref[...] + (1.0 - ADAM_B2) * (gv * gv)
        d_ref[...] = -ADAM_LR * ((mn / bc1) / (jnp.sqrt(vn / bc2) + ADAM_EPS) + ADAM_WD * w_ref[...])
        mo_ref[...] = mn
        vo_ref[...] = vn

    blk = pl.BlockSpec((tr, tc), lambda i, j: (i, j))
    return _pcall(kern, name=name, grid=(rows // tr, cols // tc), in_specs=[blk] * 4, out_specs=[blk] * 3,
                  out_shape=[jax.ShapeDtypeStruct((rows, cols), F32)] * 3)(w, g, m, v)


ANY = pl.BlockSpec(memory_space=pl.ANY)


def _place():
    x, y, c = lax.axis_index("x"), lax.axis_index("y"), lax.axis_index("c")
    chips = [(1 - x, y), (x, 1 - y), (1 - x, 1 - y)]
    return x, y, c, chips


def _remote(src, dst, send_sem, recv_sem, to):
    return pltpu.make_async_remote_copy(src_ref=src, dst_ref=dst, send_sem=send_sem, recv_sem=recv_sem,
                                        device_id=to, device_id_type=MESH)


def _half(ref, lead, core, axis):
    size = ref.shape[len(lead) + axis] // 2
    cut = pl.ds(core * size, size)
    return ref.at[tuple(lead) + ((cut, slice(None)) if axis == 0 else (slice(None), cut))]


def _gather_shards(ws, axes):
    nw = len(ws)

    def body(*refs):
        w_refs, out_refs = refs[:nw], refs[nw:2 * nw]
        send_sems, recv_sems = refs[2 * nw:]
        x, y, c, chips = _place()
        me = 2 * x + y
        sibling = (x, y, 1 - c)

        def part(i, slot, core):
            return _half(out_refs[i], (slot,), core, axes[i])

        def copy(sem, src, dst, to):
            return _remote(src, dst, send_sems.at[sem], recv_sems.at[sem], to)

        first = [copy(j * nw + i, _half(w_refs[i], (), c, axes[i]), part(i, me, c), (*chip, c))
                 for j, chip in enumerate(chips) for i in range(nw)]
        for cp in first:
            cp.start()
        passed = []
        for j, (cx, cy) in enumerate(chips):
            slot = 2 * cx + cy
            for i in range(nw):
                copy(j * nw + i, part(i, slot, c), part(i, slot, c), (x, y, c)).wait_recv()
                fwd = copy((3 + j) * nw + i, part(i, slot, c), part(i, slot, c), sibling)
                fwd.start()
                passed.append(fwd)
        for j, (cx, cy) in enumerate(chips):
            slot = 2 * cx + cy
            for i in range(nw):
                copy((3 + j) * nw + i, part(i, slot, 1 - c), part(i, slot, 1 - c), (x, y, c)).wait_recv()
        for cp in first + passed:
            cp.wait_send()

    return _pcall(
        body, name="gather_shards", in_specs=[ANY] * nw, out_specs=[ANY] * nw,
        out_shape=[jax.ShapeDtypeStruct((4,) + w.shape, w.dtype) for w in ws],
        scratch_shapes=[pltpu.SemaphoreType.DMA((6 * nw,)), pltpu.SemaphoreType.DMA((6 * nw,))],
    )(*ws)


def _half_shape(shape, axis):
    return tuple(d // 2 if k == len(shape) - 2 + axis else d for k, d in enumerate(shape))


def _pair_exchange(gs, axes):
    nw = len(gs)

    def body(*refs):
        g_refs, out_refs = refs[:nw], refs[nw:2 * nw]
        send_sems, recv_sems = refs[2 * nw:]
        x, y, c, _ = _place()
        cps = []
        for i in range(nw):
            cp = _remote(_half(g_refs[i], (slice(None),), 1 - c, axes[i]), out_refs[i],
                         send_sems.at[i], recv_sems.at[i], (x, y, 1 - c))
            cp.start()
            cps.append(cp)
        for cp in cps:
            cp.wait()

    return _pcall(body, name="pair_exchange", in_specs=[ANY] * nw, out_specs=[ANY] * nw,
                  out_shape=[jax.ShapeDtypeStruct(_half_shape(g.shape, a), g.dtype) for g, a in zip(gs, axes)],
                  scratch_shapes=[pltpu.SemaphoreType.DMA((nw,)), pltpu.SemaphoreType.DMA((nw,))])(*gs)


def _chip_exchange(ps):
    nw = len(ps)

    def body(*refs):
        p_refs, out_refs = refs[:nw], refs[nw:2 * nw]
        send_sems, recv_sems, local_sems = refs[2 * nw:]
        x, y, c, chips = _place()
        me = 2 * x + y
        mine = [pltpu.make_async_copy(p_refs[i].at[me], out_refs[i].at[me], local_sems.at[i]) for i in range(nw)]
        for cp in mine:
            cp.start()
        sends = []
        for j, (cx, cy) in enumerate(chips):
            for i in range(nw):
                cp = _remote(p_refs[i].at[2 * cx + cy], out_refs[i].at[me], send_sems.at[j * nw + i],
                             recv_sems.at[j * nw + i], (cx, cy, c))
                cp.start()
                sends.append(cp)
        for j, (cx, cy) in enumerate(chips):
            slot = 2 * cx + cy
            for i in range(nw):
                _remote(out_refs[i].at[slot], out_refs[i].at[slot], send_sems.at[j * nw + i],
                        recv_sems.at[j * nw + i], (x, y, c)).wait_recv()
        for cp in sends:
            cp.wait_send()
        for cp in mine:
            cp.wait()

    return _pcall(body, name="chip_exchange", in_specs=[ANY] * nw, out_specs=[ANY] * nw,
                  out_shape=[jax.ShapeDtypeStruct(p.shape, p.dtype) for p in ps],
                  scratch_shapes=[pltpu.SemaphoreType.DMA((3 * nw,)), pltpu.SemaphoreType.DMA((3 * nw,)),
                                  pltpu.SemaphoreType.DMA((nw,))])(*ps)


def _half_exchange(ts, axes):
    nw = len(ts)

    def body(*refs):
        t_refs, out_refs = refs[:nw], refs[nw:2 * nw]
        send_sems, recv_sems = refs[2 * nw:]
        x, y, c, _ = _place()
        sends = []
        for i in range(nw):
            cp = _remote(t_refs[i], _half(out_refs[i], (), c, axes[i]), send_sems.at[i], recv_sems.at[i],
                         (x, y, 1 - c))
            cp.start()
            sends.append(cp)
        for i in range(nw):
            _remote(t_refs[i], _half(out_refs[i], (), 1 - c, axes[i]), send_sems.at[i], recv_sems.at[i],
                    (x, y, c)).wait_recv()
        for cp in sends:
            cp.wait_send()

    def whole(t, a):
        return tuple(2 * d if k == a else d for k, d in enumerate(t.shape))

    return _pcall(body, name="half_exchange", in_specs=[ANY] * nw, out_specs=[ANY] * nw,
                  out_shape=[jax.ShapeDtypeStruct(whole(t, a), t.dtype) for t, a in zip(ts, axes)],
                  scratch_shapes=[pltpu.SemaphoreType.DMA((nw,)), pltpu.SemaphoreType.DMA((nw,))])(*ts)


def _gather_all(v):
    rows, cols = v.shape

    def body(v_ref, out_ref, send_sems, recv_sems, local_sem):
        x, y, c, _ = _place()
        me = 4 * x + 2 * y + c
        mine = pltpu.make_async_copy(v_ref, out_ref.at[me], local_sem)
        mine.start()
        sends = []
        for d in range(1, 8):
            peer = (x ^ (d >> 2), y ^ ((d >> 1) & 1), c ^ (d & 1))
            cp = _remote(v_ref, out_ref.at[me], send_sems.at[d - 1], recv_sems.at[d - 1], peer)
            cp.start()
            sends.append(cp)
        for d in range(1, 8):
            slot = 4 * (x ^ (d >> 2)) + 2 * (y ^ ((d >> 1) & 1)) + (c ^ (d & 1))
            _remote(v_ref, out_ref.at[slot], send_sems.at[d - 1], recv_sems.at[d - 1], (x, y, c)).wait_recv()
        for cp in sends:
            cp.wait_send()
        mine.wait()

    return _pcall(body, name="gather_all", in_specs=[ANY], out_specs=ANY,
                  out_shape=jax.ShapeDtypeStruct((8, rows, cols), v.dtype),
                  scratch_shapes=[pltpu.SemaphoreType.DMA((7,)), pltpu.SemaphoreType.DMA((7,)),
                                  pltpu.SemaphoreType.DMA])(v)


def _pad_cols(a, width):
    return a if a.shape[1] == width else jnp.pad(a, ((0, 0), (0, width - a.shape[1])))


def _pad_rows(a, height):
    return a if a.shape[0] == height else jnp.pad(a, ((0, height - a.shape[0]), (0, 0)))


def _w_in_padded(wt):
    def seg(name, height=None):
        o, n = ORIG[name]
        return _pad_rows(wt[o:o + n], height or n)

    qn = wt[ORIG["q_nsa"][0]:ORIG["q_nsa"][0] + 768].reshape(NSA_HEADS, NSA_DK, -1)
    qn = jnp.pad(qn, ((0, 0), (0, 256 - NSA_DK), (0, 0))).reshape(NSA_HEADS * 256, -1)
    kr = seg("k_rope")
    zeros = jnp.zeros((PAD["z"] - (PAD["q_mem"] + 512), wt.shape[1]), wt.dtype)
    return jnp.concatenate(
        [seg("c_q"), seg("c_kv"), qn, seg("k_c", 256), seg("k_s", 256), seg("k_w", 256), kr, kr,
         seg("v_c"), seg("v_s"), seg("v_w"), seg("g_nsa", LANE), seg("q_mem"), zeros,
         seg("z_mla"), seg("z_nsa"), seg("z_mem")], axis=0)


def _w_in_unpadded(gt):
    def seg(name, n):
        return gt[PAD[name]:PAD[name] + n]

    qn = gt[PAD["q_nsa"]:PAD["q_nsa"] + 1024].reshape(NSA_HEADS, 256, -1)[:, :NSA_DK].reshape(768, -1)
    z = PAD["z"]
    return jnp.concatenate(
        [seg("c_q", 512), seg("c_kv", 512), seg("k_rope", 64), gt[z:z + 1024], qn, seg("k_c", 192),
         seg("v_c", 128), seg("k_s", 192), seg("v_s", 128), seg("k_w", 192), seg("v_w", 128),
         seg("g_nsa", 12), gt[z + 1024:z + 1536], seg("q_mem", 512), gt[z + 1536:z + 2048]], axis=0)


def _unpad_segments():
    z = PAD["z"]
    segs = [(PAD["c_q"], 0, 512), (PAD["c_kv"], 512, 512), (PAD["k_rope"], 1024, 64), (z, 1088, 1024)]
    segs += [(PAD["q_nsa"] + 256 * h, 2112 + NSA_DK * h, NSA_DK) for h in range(NSA_HEADS)]
    for name, rows in (("k_c", 192), ("v_c", 128), ("k_s", 192), ("v_s", 128), ("k_w", 192), ("v_w", 128),
                       ("g_nsa", 12)):
        segs.append((PAD[name], ORIG[name][0], rows))
    segs += [(z + 1024, ORIG["z_nsa"][0], 512), (PAD["q_mem"], ORIG["q_mem"][0], 512),
             (z + 1536, ORIG["z_mem"][0], 512)]
    return segs


def _w_in_grad_slots(gt):
    rows, cols = gt.shape
    shard = sum(n for _, _, n in _unpad_segments()) // 4
    tc = 256
    pieces = []
    for src, dst, n in _unpad_segments():
        while n:
            slot, off = divmod(dst, shard)
            take = min(n, shard - off)
            pieces.append((src, slot, off, take))
            src, dst, n = src + take, dst + take, n - take

    def kern(g_ref, o_ref):
        for src, slot, off, take in pieces:
            o_ref[slot, off:off + take, :] = g_ref[src:src + take, :]

    return _pcall(kern, name="w_in_grad_slots", grid=(cols // tc,),
                  in_specs=[pl.BlockSpec((rows, tc), lambda i: (0, i))],
                  out_specs=pl.BlockSpec((4, shard, tc), lambda i: (0, 0, i)),
                  out_shape=jax.ShapeDtypeStruct((4, shard, cols), F32))(gt)


def _w_in_from_slots(ws):
    nslot, shard, cols = ws.shape
    tc = 256
    pieces = []
    for dst, src, n in _unpad_segments() + [(PAD["k_rope"] + 64, ORIG["k_rope"][0], 64)]:
        while n:
            slot, off = divmod(src, shard)
            take = min(n, shard - off)
            pieces.append((dst, slot, off, take))
            src, dst, n = src + take, dst + take, n - take

    def kern(w_ref, o_ref):
        o_ref[...] = jnp.zeros_like(o_ref)
        for dst, slot, off, take in pieces:
            o_ref[dst:dst + take, :] = w_ref[slot, off:off + take, :]

    return _pcall(kern, name="w_in_from_slots", grid=(cols // tc,),
                  in_specs=[pl.BlockSpec((nslot, shard, tc), lambda i: (0, 0, i))],
                  out_specs=pl.BlockSpec((D_PAD, tc), lambda i: (0, i)),
                  out_shape=jax.ShapeDtypeStruct((D_PAD, cols), ws.dtype))(ws)


def _rope_tables(s):
    pos = jnp.arange(s, dtype=F32)
    inv_freq = ROPE_THETA ** (-jnp.arange(0, 64, 2, dtype=F32) / 64)
    ang = pos[:, None] * inv_freq[None, :]
    cos, sin = jnp.cos(ang), jnp.sin(ang)
    z = jnp.zeros((s, 64), F32)
    return jnp.concatenate([cos, cos, z], axis=1), jnp.concatenate([-sin, sin, z], axis=1)


def _overlap_table(s):
    n_c, n_s = s // CMP_STRIDE, s // SLC_LEN
    c0 = np.arange(n_c)[:, None] * CMP_STRIDE
    s0 = np.arange(LANE)[None, :] * SLC_LEN
    ov = (c0 < s0 + SLC_LEN) & (c0 + CMP_LEN > s0) & (np.arange(n_c)[:, None] < n_c - 1) & (np.arange(LANE)[None, :] < n_s)
    return jnp.asarray(ov.astype(np.float32), dtype=BF16)


def _shift_down(a):
    return jnp.concatenate([jnp.zeros((8, a.shape[1]), a.dtype), a], axis=0)[7:7 + a.shape[0]]


def _shift_up(a):
    return jnp.concatenate([a, jnp.zeros((8, a.shape[1]), a.dtype)], axis=0)[1:1 + a.shape[0]]


def _local_step(x, mem, target, w):
    s = x.shape[0]
    cs, sn = _rope_tables(s)
    t_ = jnp.transpose

    w_in_p = _w_in_from_slots(w["w_in_t"])
    xn, rstd_x = _rms_fwd(_Src(x, D_MODEL), w["norm_g"], "norm_x")
    hp, hpb = _mm(xn, w_in_p, "in_proj", mode="nt", second_dtype=BF16)

    w_uq3 = w["w_uq"].reshape(512, MLA_HEADS, 192)
    w_uq_p = jnp.concatenate([w_uq3, w_uq3[:, :, 128:]], axis=2).reshape(512, MLA_HEADS * 256)
    w_ukv_p = t_(w["w_ukv"].reshape(512, MLA_HEADS, 2, 128), (0, 2, 1, 3)).reshape(512, 2048)
    c_q, c_kv = _Src(hp, 512, 0), _Src(hp, 512, 1)
    cqn, rstd_q = _rms_fwd(c_q, w["q_norm_g"], "norm_q")
    ckvn, rstd_kv = _rms_fwd(c_kv, w["kv_norm_g"], "norm_kv")
    q_lin = _mm(cqn, w_uq_p, "mla_q_proj")
    kvb = _mm(ckvn, w_ukv_p, "mla_kv_proj", out_dtype=BF16)
    q_mla = _rope_fwd(_Src(q_lin, MLA_HEADS * 256), cs, sn, MLA_HEADS, 256, LANE, "rope_q")
    k_pe = _rope_fwd(_Src(hp, LANE, PAD["k_rope"] // LANE), cs, sn, 1, LANE, 0, "rope_k")
    mla = _Attn("mla", s, s, MLA_HEADS, 256)
    mla_q, mla_v = _Src(q_mla, 256), _Src(kvb, LANE, MLA_HEADS)
    mla_k = [_Src(kvb, LANE), _Src(k_pe, LANE, 0, False)]
    o_mla, l_mla, lr_mla = _attn_fwd(mla, mla_q, mla_k, mla_v, None, "mla_fwd")

    sk = s // CMP_STRIDE
    pe_k, pe_v = w["cmp_pe_k"], w["cmp_pe_v"]
    w1k = _pad_cols(w["cmp_w1k"], 256)
    w2k = jnp.pad(w["cmp_w2k"], ((0, 64), (0, 64))).astype(BF16)
    w1v, w2v = w["cmp_w1v"], w["cmp_w2v"].astype(BF16)
    half_k, half_v = CMP_STRIDE * NSA_DK, CMP_STRIDE * HEAD_V
    ak = hp[:, PAD["k_c"]:PAD["k_c"] + NSA_DK].reshape(sk, half_k)
    av = hp[:, PAD["v_c"]:PAD["v_c"] + HEAD_V].reshape(sk, half_v)
    ck_args = (ak, _shift_up(ak), pe_k[:CMP_STRIDE].reshape(1, half_k), pe_k[CMP_STRIDE:].reshape(1, half_k),
               w1k[:half_k], w1k[half_k:], w2k)
    cv_args = (av, _shift_up(av), pe_v[:CMP_STRIDE].reshape(1, half_v), pe_v[CMP_STRIDE:].reshape(1, half_v),
               w1v[:half_v], w1v[half_v:], w2v)
    k_cmp, pre_k = _compress_fwd(*ck_args, "compress_k")
    v_cmp, pre_v = _compress_fwd(*cv_args, "compress_v")
    cmp_ = _Attn("cmp", s, sk, NSA_HEADS, 256)
    slc = _Attn("slc", s, s, NSA_HEADS, 256)
    win = _Attn("win", s, s, NSA_HEADS, 256)
    nsa_q = _Src(hpb, 256, PAD["q_nsa"] // 256)
    cmp_k, cmp_v = [_Src(k_cmp, 256, 0, False)], _Src(v_cmp, HEAD_V, 0, False)
    slc_k, slc_v = [_Src(hpb, 256, PAD["k_s"] // 256, False)], _Src(hpb, HEAD_V, PAD["v_s"] // HEAD_V, False)
    win_k, win_v = [_Src(hpb, 256, PAD["k_w"] // 256, False)], _Src(hpb, HEAD_V, PAD["v_w"] // HEAD_V, False)
    o_cmp, l_cmp, lr_cmp = _attn_fwd(cmp_, nsa_q, cmp_k, cmp_v, None, "cmp_fwd")
    sel, selt = _select(cmp_, _Src(hpb, NSA_HEADS * 256, PAD["q_nsa"] // (NSA_HEADS * 256)), k_cmp,
                        _overlap_table(s))
    o_slc, l_slc, lr_slc = _attn_fwd(slc, nsa_q, slc_k, slc_v, sel, "slc_fwd")
    o_win, l_win, lr_win = _attn_fwd(win, nsa_q, win_k, win_v, None, "win_fwd")
    gl = _Src(hp, LANE, PAD["g_nsa"] // LANE)
    o_nsa = _nsa_combine(o_cmp, o_slc, o_win, gl)

    mn, rstd_m = _rms_fwd(_Src(mem, D_MODEL), w["mem_norm_g"], "norm_mem")
    kvm = _mm(mn, w["w_mem_kv"], "mem_kv_proj", out_dtype=BF16)
    mem_ = _Attn("mem", s, mem.shape[0], MEM_HEADS, LANE)
    mem_q, mem_k, mem_v = _Src(hpb, LANE, PAD["q_mem"] // LANE), [_Src(kvm, LANE)], _Src(kvm, LANE, MEM_HEADS)
    o_mem, l_mem, lr_mem = _attn_fwd(mem_, mem_q, mem_k, mem_v, None, "mem_fwd")

    u = _gate_fwd(o_mla, o_nsa, o_mem, hp)
    proj = _mm(u, w["w_out"], "out_proj")
    dy, g_final, loss = _final_loss(x, proj, w["final_norm_g"].reshape(1, -1), target)

    g_w_out = _mm(u, dy, "out_proj_dw", mode="tn")
    du = _mm(dy, w["w_out"], "out_proj_dx", mode="nt")
    do_cat, dz = _gate_bwd(du, o_mla, o_nsa, o_mem, hp)

    dq_mem, (dk_mem,), dv_mem = _attn_bwd(mem_, mem_q, mem_k, mem_v, None, None, _Src(o_mem, HEAD_V), l_mem,
                                          lr_mem, _Src(do_cat, HEAD_V, 12), None, "mem_bwd")
    dkvm = jnp.concatenate([dk_mem, dv_mem], axis=1)
    g_w_mem_kv = _mm(mn, dkvm, "mem_kv_dw", mode="tn")
    dmn = _mm(dkvm, w["w_mem_kv"], "mem_kv_dx", mode="nt")
    _, g_mem_norm = _rms_bwd(_Src(mem, D_MODEL), w["mem_norm_g"], rstd_m, dmn, None, "norm_mem_bwd")

    do_cmp, do_slc, do_win, dgl = _nsa_combine_bwd(do_cat, o_cmp, o_slc, o_win, gl)
    dq_n, (dk_cmp,), dv_cmp = _attn_bwd(cmp_, nsa_q, cmp_k, cmp_v, None, None, _Src(o_cmp, HEAD_V), l_cmp,
                                        lr_cmp, _Src(do_cmp, HEAD_V), None, "cmp_bwd")
    dq_n, (dk_s,), dv_s = _attn_bwd(slc, nsa_q, slc_k, slc_v, sel, selt, _Src(o_slc, HEAD_V), l_slc, lr_slc,
                                    _Src(do_slc, HEAD_V), dq_n, "slc_bwd")
    dq_n, (dk_w,), dv_w = _attn_bwd(win, nsa_q, win_k, win_v, None, None, _Src(o_win, HEAD_V), l_win, lr_win,
                                    _Src(do_win, HEAD_V), dq_n, "win_bwd")
    dak, dpk_lo, dpk_hi, dw1k_lo, dw1k_hi, g_w2k = _compress_bwd(
        *ck_args, pre_k, _shift_down(pre_k), dk_cmp, _shift_down(dk_cmp), "compress_k_bwd")
    dav, dpv_lo, dpv_hi, dw1v_lo, dw1v_hi, g_w2v = _compress_bwd(
        *cv_args, pre_v, _shift_down(pre_v), dv_cmp, _shift_down(dv_cmp), "compress_v_bwd")
    g_pe_k = jnp.concatenate([dpk_lo.reshape(CMP_STRIDE, NSA_DK), dpk_hi.reshape(CMP_STRIDE, NSA_DK)], axis=0)
    g_pe_v = jnp.concatenate([dpv_lo.reshape(CMP_STRIDE, HEAD_V), dpv_hi.reshape(CMP_STRIDE, HEAD_V)], axis=0)
    g_w1k = jnp.concatenate([dw1k_lo, dw1k_hi], axis=0)[:, :NSA_DK]
    g_w1v = jnp.concatenate([dw1v_lo, dw1v_hi], axis=0)
    dk_c = _pad_cols(dak.reshape(s, NSA_DK), 256)
    dv_c = dav.reshape(s, HEAD_V)

    dq_m, (dk_nope, dk_pe), dv_m = _attn_bwd(mla, mla_q, mla_k, mla_v, None, None, _Src(o_mla, HEAD_V), l_mla,
                                             lr_mla, _Src(do_cat, HEAD_V), None, "mla_bwd")
    dq_lin = _rope_bwd_q(dq_m, cs, sn)
    dkv_lin, d_krope = _rope_bwd_k(dk_nope, dk_pe, dv_m, cs, sn)
    g_w_uq_p = _mm(cqn, dq_lin, "mla_q_dw", mode="tn")
    dcqn = _mm(dq_lin, w_uq_p, "mla_q_dx", mode="nt")
    g_w_ukv_p = _mm(ckvn, dkv_lin, "mla_kv_dw", mode="tn")
    dckvn = _mm(dkv_lin, w_ukv_p, "mla_kv_dx", mode="nt")
    dc_q, g_q_norm = _rms_bwd(c_q, w["q_norm_g"], rstd_q, dcqn, None, "norm_q_bwd")
    dc_kv, g_kv_norm = _rms_bwd(c_kv, w["kv_norm_g"], rstd_kv, dckvn, None, "norm_kv_bwd")
    g_w_uq = g_w_uq_p.reshape(512, MLA_HEADS, 256)[:, :, :192].reshape(512, MLA_HEADS * 192)
    g_w_ukv = t_(g_w_ukv_p.reshape(512, 2, MLA_HEADS, 128), (0, 2, 1, 3)).reshape(512, 2048)

    dhp = jnp.concatenate(
        [dc_q, dc_kv, dq_n, dk_c, dk_s, dk_w, d_krope, dv_c, dv_s, dv_w, dgl, dq_mem,
         jnp.zeros((s, PAD["z"] - (PAD["q_mem"] + 512)), F32), dz], axis=1)
    g_w_in_t = _w_in_grad_slots(_mm(dhp, xn, "in_proj_dw", mode="tn"))
    dxn = _mm(dhp, w_in_p, "in_proj_dx")
    grad_x, g_norm = _rms_bwd(_Src(x, D_MODEL), w["norm_g"], rstd_x, dxn, dy, "norm_x_bwd")

    grads = dict(norm_g=g_norm, w_in_t=g_w_in_t, q_norm_g=g_q_norm, w_uq=g_w_uq, kv_norm_g=g_kv_norm,
                 w_ukv=g_w_ukv, cmp_pe_k=g_pe_k, cmp_pe_v=g_pe_v, cmp_w1k=g_w1k, cmp_w2k=g_w2k[:NSA_DK, :NSA_DK],
                 cmp_w1v=g_w1v, cmp_w2v=g_w2v, mem_norm_g=g_mem_norm, w_mem_kv=g_w_mem_kv, w_out=g_w_out,
                 final_norm_g=g_final.reshape(-1))
    return loss[0, 0], grad_x, grads


def kernel(x, mem, norm_g, w_in, q_norm_g, w_uq, kv_norm_g, w_ukv, cmp_pe_k, cmp_pe_v, cmp_w1k, cmp_w2k, cmp_w1v, cmp_w2v, mem_norm_g, w_mem_kv, w_out, final_norm_g, loss_target, m_norm_g, m_w_in, m_q_norm_g, m_w_uq, m_kv_norm_g, m_w_ukv, m_cmp_pe_k, m_cmp_pe_v, m_cmp_w1k, m_cmp_w2k, m_cmp_w1v, m_cmp_w2v, m_mem_norm_g, m_w_mem_kv, m_w_out, m_final_norm_g, v_norm_g, v_w_in, v_q_norm_g, v_w_uq, v_kv_norm_g, v_w_ukv, v_cmp_pe_k, v_cmp_pe_v, v_cmp_w1k, v_cmp_w2k, v_cmp_w1v, v_cmp_w2v, v_mem_norm_g, v_w_mem_kv, v_w_out, v_final_norm_g):
    args = dict(locals())
    wts = {n: args[n] for n in WEIGHTS}
    loc = {n: (a if n == "final_norm_g" else a[0]) for n, a in wts.items()}

    def to_x(n, a):
        return a.T if n == "w_in" else a

    split = [1 if n == "w_in" else 0 for n in SHARDED]

    own = [to_x(n, loc[n]).astype(BF16) for n in SHARDED]
    chip = 2 * lax.axis_index("x") + lax.axis_index("y")
    gathered = [lax.dynamic_update_slice(gw, a[None], (chip, 0, 0))
                for gw, a in zip(_gather_shards(own, split), own)]
    full = {n: loc[n].reshape(1, -1) if loc[n].ndim == 1 else loc[n] for n in REPLICATED}
    for n, gw in zip(SHARDED, gathered):
        if n == "w_in":
            full["w_in_t"] = gw
        elif SHARD_AXIS[n] == 0:
            full[n] = gw.reshape(4 * gw.shape[1], gw.shape[2])
        else:
            full[n] = jnp.concatenate([gw[j] for j in range(4)], axis=1)

    loss, grad_x, g = _local_step(x[0], mem[0], loss_target[0], full)
    loss = lax.psum(loss, ("x", "y", "c"))

    def slots(n):
        if n == "w_in":
            return g["w_in_t"]
        a = g[n]
        if SHARD_AXIS[n] == 0:
            return a.reshape(4, a.shape[0] // 4, a.shape[1])
        width = a.shape[1] // 4
        return jnp.stack([a[:, j * width:(j + 1) * width] for j in range(4)])

    gs = [slots(n) for n in SHARDED]
    core = lax.axis_index("c").astype(jnp.int32).reshape(1)
    theirs = _pair_exchange(gs, split)
    pairs = [_pair_sum(a, b, core, ax, "pair_sum_" + n) for n, a, b, ax in zip(SHARDED, gs, theirs, split)]
    from_chips = _chip_exchange(pairs)
    mine = [_sum_slots(b, "chip_sum_" + n) for n, b in zip(SHARDED, from_chips)]
    g_sh = [lax.dynamic_update_slice(o, t, (core[0] * t.shape[0], 0) if ax == 0 else (0, core[0] * t.shape[1]))
            for o, t, ax in zip(_half_exchange(mine, split), mine, split)]

    n_rep = sum(int(np.prod(loc[n].shape)) for n in REPLICATED)
    rows_rep = -(-n_rep // (8 * LANE)) * 8

    def rep_pack(parts):
        flat = jnp.concatenate([p.reshape(-1) for p in parts])
        return jnp.pad(flat, (0, rows_rep * LANE - n_rep)).reshape(rows_rep, LANE)

    g_rep = _sum_slots(_gather_all(rep_pack([g[n] for n in REPLICATED])), "replica_sum")
    d_rp, m_rp, v_rp = _adamw(rep_pack([wts[n] for n in REPLICATED]), g_rep,
                              rep_pack([args["m_" + n] for n in REPLICATED]),
                              rep_pack([args["v_" + n] for n in REPLICATED]), "adamw_replicated")

    def rep_unpack(buf):
        flat, out, o = buf.reshape(-1), {}, 0
        for n in REPLICATED:
            size = int(np.prod(wts[n].shape))
            out[n] = flat[o:o + size].reshape(wts[n].shape)
            o += size
        return out

    outs = {k: rep_unpack(b) for k, b in (("g", g_rep), ("d", d_rp), ("m", m_rp), ("v", v_rp))}
    for n, gn in zip(SHARDED, g_sh):
        d, mo, vo = _adamw(to_x(n, loc[n]), gn, to_x(n, args["m_" + n][0]), to_x(n, args["v_" + n][0]),
                           "adamw_" + n)
        for k, a in (("g", gn), ("d", d), ("m", mo), ("v", vo)):
            outs[k][n] = to_x(n, a).reshape(wts[n].shape)

    return (loss, grad_x[None], *[outs["g"][n] for n in WEIGHTS], *[outs["d"][n] for n in WEIGHTS],
            *[outs["m"][n] for n in WEIGHTS], *[outs["v"][n] for n in WEIGHTS])
```

```python
from typing import NamedTuple

import numpy as np
import jax
import jax.numpy as jnp
from jax import lax
from jax.experimental import pallas as pl
from jax.experimental.pallas import tpu as pltpu

F32 = jnp.float32
BF16 = jnp.bfloat16
MESH = pl.DeviceIdType.MESH

D_MODEL = 2048
EPS = 1e-6
LANE = 128
HEAD_V = 128
MLA_HEADS = 8
NSA_HEADS = 4
MEM_HEADS = 4
NSA_DK = 192
CMP_STRIDE = 16
CMP_LEN = 32
SLC_LEN = 64
SLC_TOPN = 16
WIN = 512
NEG = -1e30
LOG2E = 1.4426950408889634
ROPE_THETA = 10000.0
BLOCK_BYTES = 2 << 20

ORIG = dict(c_q=(0, 512), c_kv=(512, 512), k_rope=(1024, 64), z_mla=(1088, 1024),
            q_nsa=(2112, 768), k_c=(2880, 192), v_c=(3072, 128), k_s=(3200, 192),
            v_s=(3392, 128), k_w=(3520, 192), v_w=(3712, 128), g_nsa=(3840, 12),
            z_nsa=(3852, 512), q_mem=(4364, 512), z_mem=(4876, 512))
PAD = dict(c_q=0, c_kv=512, q_nsa=1024, k_c=2048, k_s=2304, k_w=2560, k_rope=2816, v_c=2944,
           v_s=3072, v_w=3200, g_nsa=3328, q_mem=3584, z=4096)
D_PAD = 6144

ADAM_LR, ADAM_B1, ADAM_B2, ADAM_EPS, ADAM_WD, ADAM_STEP = 0.001, 0.9, 0.999, 1e-08, 0.01, 10

SHARDED = ("w_in", "w_uq", "w_ukv", "cmp_w1k", "cmp_w1v", "w_mem_kv", "w_out")
SHARD_AXIS = dict(w_in=1, w_uq=1, w_ukv=1, cmp_w1k=0, cmp_w1v=0, w_mem_kv=0, w_out=0)
REPLICATED = ("norm_g", "q_norm_g", "kv_norm_g", "cmp_pe_k", "cmp_pe_v", "cmp_w2k", "cmp_w2v",
              "mem_norm_g", "final_norm_g")
WEIGHTS = ("norm_g", "w_in", "q_norm_g", "w_uq", "kv_norm_g", "w_ukv", "cmp_pe_k", "cmp_pe_v",
           "cmp_w1k", "cmp_w2k", "cmp_w1v", "cmp_w2v", "mem_norm_g", "w_mem_kv", "w_out",
           "final_norm_g")


def _pcall(kernel, **kw):
    return pl.pallas_call(kernel, **kw)


def _tile(n, pref):
    if n <= pref:
        return n
    for t in range(pref, LANE - 1, -LANE):
        if n % t == 0:
            return t
    raise ValueError((n, pref))


def _row_tile(rows, cols, itemsize=4):
    want = max(16, BLOCK_BYTES // (cols * itemsize))
    if rows <= want:
        return rows
    t = 16
    best = rows
    while t <= want:
        if rows % t == 0:
            best = t
        t *= 2
    return best


def _nt(a, b):
    return lax.dot_general(a, b, (((1,), (1,)), ((), ())), preferred_element_type=F32)


def _tn(a, b):
    return lax.dot_general(a, b, (((0,), (0,)), ((), ())), preferred_element_type=F32)


def _nn(a, b):
    return jnp.dot(a, b, preferred_element_type=F32)


def _sigmoid(x):
    return 1.0 / (1.0 + jnp.exp(-x))


class _Src(NamedTuple):
    arr: jax.Array
    width: int
    col0: int = 0
    per_head: bool = True

    def col(self, h):
        return self.col0 + h if self.per_head else self.col0


def _mm(a, b, name, mode="nn", out_dtype=F32, second_dtype=None):
    if mode == "tn":
        k, m = a.shape
    else:
        m, k = a.shape
    if mode == "nt":
        n, k2 = b.shape
    else:
        k2, n = b.shape
    assert k == k2, (a.shape, b.shape, mode)
    tm, tn, tk = _tile(m, 1024), _tile(n, 1024), _tile(k, 2048)
    nk = k // tk
    assert nk == 1 or (out_dtype == F32 and second_dtype is None)
    dot = {"nn": _nn, "nt": _nt, "tn": _tn}[mode]

    def kern(a_ref, b_ref, o_ref, *more):
        r = dot(a_ref[...].astype(BF16), b_ref[...].astype(BF16))
        if nk == 1:
            o_ref[...] = r.astype(out_dtype)
            if more:
                more[0][...] = r.astype(second_dtype)
        else:
            kk = pl.program_id(2)

            @pl.when(kk == 0)
            def _():
                o_ref[...] = r

            @pl.when(kk > 0)
            def _():
                o_ref[...] += r

    a_spec = (pl.BlockSpec((tk, tm), lambda i, j, kk: (kk, i)) if mode == "tn"
              else pl.BlockSpec((tm, tk), lambda i, j, kk: (i, kk)))
    b_spec = (pl.BlockSpec((tn, tk), lambda i, j, kk: (j, kk)) if mode == "nt"
              else pl.BlockSpec((tk, tn), lambda i, j, kk: (kk, j)))
    o_spec = pl.BlockSpec((tm, tn), lambda i, j, kk: (i, j))
    out_shape = jax.ShapeDtypeStruct((m, n), out_dtype)
    if second_dtype is not None:
        o_spec = [o_spec, o_spec]
        out_shape = [out_shape, jax.ShapeDtypeStruct((m, n), second_dtype)]
    return _pcall(
        kern, name=name, grid=(m // tm, n // tn, nk), in_specs=[a_spec, b_spec], out_specs=o_spec,
        out_shape=out_shape,
        compiler_params=pltpu.CompilerParams(dimension_semantics=("parallel", "parallel", "arbitrary")),
    )(a, b)


def _rms_fwd(x, g, name):
    r, d = x.arr.shape[0], x.width
    tr = _tile(r, 512)

    def kern(x_ref, g_ref, y_ref, r_ref):
        xv = x_ref[...]
        rstd = lax.rsqrt(jnp.mean(xv * xv, axis=-1, keepdims=True) + EPS)
        y_ref[...] = (xv * rstd * g_ref[...]).astype(BF16)
        r_ref[...] = rstd

    return _pcall(
        kern, name=name, grid=(r // tr,),
        in_specs=[pl.BlockSpec((tr, d), lambda i: (i, x.col0)), pl.BlockSpec((1, d), lambda i: (0, 0))],
        out_specs=[pl.BlockSpec((tr, d), lambda i: (i, 0)), pl.BlockSpec((tr, 1), lambda i: (i, 0))],
        out_shape=[jax.ShapeDtypeStruct((r, d), BF16), jax.ShapeDtypeStruct((r, 1), F32)],
    )(x.arr, g)


def _rms_bwd(x, g, rstd, dy, add, name):
    r, d = x.arr.shape[0], x.width
    tr = _tile(r, 256)
    has_add = add is not None

    def kern(*refs):
        if has_add:
            x_ref, g_ref, r_ref, dy_ref, add_ref, dx_ref, dg_ref = refs
        else:
            x_ref, g_ref, r_ref, dy_ref, dx_ref, dg_ref = refs
        rs = r_ref[...]
        xhat = x_ref[...] * rs
        dyv = dy_ref[...]
        dyg = dyv * g_ref[...]
        c = jnp.mean(dyg * xhat, axis=-1, keepdims=True)
        dx = rs * (dyg - xhat * c)
        if has_add:
            dx = dx + add_ref[...]
        dx_ref[...] = dx
        part = jnp.sum(dyv * xhat, axis=0, keepdims=True)

        @pl.when(pl.program_id(0) == 0)
        def _():
            dg_ref[...] = part

        @pl.when(pl.program_id(0) > 0)
        def _():
            dg_ref[...] += part

    row = pl.BlockSpec((tr, d), lambda i: (i, 0))
    vec = pl.BlockSpec((1, d), lambda i: (0, 0))
    ins = [pl.BlockSpec((tr, d), lambda i: (i, x.col0)), vec, pl.BlockSpec((tr, 1), lambda i: (i, 0)), row]
    ins += [row] if has_add else []
    args = (x.arr, g, rstd, dy) + ((add,) if has_add else ())
    return _pcall(
        kern, name=name, grid=(r // tr,), in_specs=ins, out_specs=[row, vec],
        out_shape=[jax.ShapeDtypeStruct((r, d), F32), jax.ShapeDtypeStruct((1, d), F32)],
        compiler_params=pltpu.CompilerParams(dimension_semantics=("arbitrary",)),
    )(*args)


def _final_loss(x, proj, g, target):
    r, d = x.shape
    tr = _tile(r, 256)

    def kern(x_ref, p_ref, g_ref, t_ref, dy_ref, dg_ref, loss_ref):
        y = x_ref[...] + p_ref[...]
        rs = lax.rsqrt(jnp.mean(y * y, axis=-1, keepdims=True) + EPS)
        yhat = y * rs
        gv = g_ref[...]
        e = yhat * gv - t_ref[...]
        lpart = 0.5 * jnp.sum(jnp.mean(e * e, axis=-1, keepdims=True), axis=0, keepdims=True)
        dout = e * (1.0 / d)
        dyg = dout * gv
        c = jnp.mean(dyg * yhat, axis=-1, keepdims=True)
        dy_ref[...] = rs * (dyg - yhat * c)
        gpart = jnp.sum(dout * yhat, axis=0, keepdims=True)
        lrow = jnp.broadcast_to(lpart, (1, LANE))

        @pl.when(pl.program_id(0) == 0)
        def _():
            dg_ref[...] = gpart
            loss_ref[...] = lrow

        @pl.when(pl.program_id(0) > 0)
        def _():
            dg_ref[...] += gpart
            loss_ref[...] += lrow

    row = pl.BlockSpec((tr, d), lambda i: (i, 0))
    vec = pl.BlockSpec((1, d), lambda i: (0, 0))
    return _pcall(
        kern, name="final_loss", grid=(r // tr,), in_specs=[row, row, vec, row],
        out_specs=[row, vec, pl.BlockSpec((1, LANE), lambda i: (0, 0))],
        out_shape=[jax.ShapeDtypeStruct((r, d), F32), jax.ShapeDtypeStruct((1, d), F32),
                   jax.ShapeDtypeStruct((1, LANE), F32)],
        compiler_params=pltpu.CompilerParams(dimension_semantics=("arbitrary",)),
    )(x, proj, g, target)


def _rope_fwd(x, cs, sn, nh, width, off, name):
    s = x.arr.shape[0]
    tr = _tile(s, 512)

    def kern(x_ref, c_ref, s_ref, o_ref):
        cv, sv = c_ref[...], s_ref[...]
        for h in range(nh):
            b = h * width
            if off:
                o_ref[:, b:b + off] = x_ref[:, b:b + off].astype(BF16)
            xr = x_ref[:, b + off:b + off + LANE]
            o_ref[:, b + off:b + off + LANE] = (xr * cv + pltpu.roll(xr, 32, 1) * sv).astype(BF16)

    tab = pl.BlockSpec((tr, LANE), lambda i: (i, 0))
    return _pcall(
        kern, name=name, grid=(s // tr,),
        in_specs=[pl.BlockSpec((tr, nh * width), lambda i: (i, x.col0)), tab, tab],
        out_specs=pl.BlockSpec((tr, nh * width), lambda i: (i, 0)),
        out_shape=jax.ShapeDtypeStruct((s, nh * width), BF16),
    )(x.arr, cs, sn)


def _rope_grad(d, cv, sv):
    g2 = d * sv
    g2 = g2 + pltpu.roll(g2, 64, 1)
    lane = lax.broadcasted_iota(jnp.int32, d.shape, 1)
    return jnp.where(lane < 64, d * cv + pltpu.roll(g2, 32, 1), 0.0)


def _rope_bwd_q(dq, cs, sn):
    s, w = dq.shape
    tr = _tile(s, 512)
    nh = w // 256

    def kern(d_ref, c_ref, s_ref, o_ref):
        cv, sv = c_ref[...], s_ref[...]
        for h in range(nh):
            b = h * 256
            o_ref[:, b:b + LANE] = d_ref[:, b:b + LANE]
            o_ref[:, b + LANE:b + 256] = _rope_grad(d_ref[:, b + LANE:b + 256], cv, sv)

    row = pl.BlockSpec((tr, w), lambda i: (i, 0))
    tab = pl.BlockSpec((tr, LANE), lambda i: (i, 0))
    return _pcall(kern, name="rope_bwd_q", grid=(s // tr,), in_specs=[row, tab, tab], out_specs=row,
                  out_shape=jax.ShapeDtypeStruct((s, w), F32))(dq, cs, sn)


def _rope_bwd_k(dk_nope, dk_pe, dv, cs, sn):
    s, w = dk_nope.shape
    tr = _tile(s, 512)

    def kern(dk_ref, dp_ref, dv_ref, c_ref, s_ref, okv_ref, okr_ref):
        okv_ref[:, :w] = dk_ref[...]
        okv_ref[:, w:] = dv_ref[...]
        okr_ref[...] = _rope_grad(dp_ref[...], c_ref[...], s_ref[...])

    tab = pl.BlockSpec((tr, LANE), lambda i: (i, 0))
    wide = pl.BlockSpec((tr, w), lambda i: (i, 0))
    return _pcall(
        kern, name="rope_bwd_k", grid=(s // tr,), in_specs=[wide, tab, wide, tab, tab],
        out_specs=[pl.BlockSpec((tr, 2 * w), lambda i: (i, 0)), tab],
        out_shape=[jax.ShapeDtypeStruct((s, 2 * w), F32), jax.ShapeDtypeStruct((s, LANE), F32)],
    )(dk_nope, dk_pe, dv, cs, sn)


class _Attn:
    def __init__(self, mode, s, sk, heads, dk):
        self.mode, self.s, self.sk, self.h, self.dk = mode, s, sk, heads, dk
        self.scale = {"mla": 192 ** -0.5, "mem": 128 ** -0.5}.get(mode, NSA_DK ** -0.5)
        self.tb = min(256, s)
        self.nb = s // self.tb
        self.nsub = 2 if self.nb % 2 == 0 else 1
        self.tq = self.tb * self.nsub
        self.nq = s // self.tq
        self.causal = mode in ("mla", "slc")
        if self.causal:
            self.tk = self.tq
        elif mode == "win":
            self.tk = WIN + self.tb
        else:
            self.tk = sk
        self.tkb = min(512, sk)
        self.ksub = 2 if self.tkb == 512 and mode == "mla" else 1
        self.kb = self.tkb // self.ksub
        self.ncmp = s // CMP_STRIDE - 1

    def mask_bias(self, t, n, h, selx, diag):
        m = self.mode
        if m == "mla":
            return (n <= t) if diag else None, None
        if m == "mem":
            return None, None
        slope = jnp.where(h == 0, 0.25, jnp.where(h == 1, 0.0625, jnp.where(h == 2, 0.015625, 0.00390625)))
        slope = slope.astype(F32) * LOG2E
        if m == "cmp":
            mask = (n * CMP_STRIDE + (CMP_LEN - 1) <= t) & (n < self.ncmp)
            pos = n.astype(F32) * float(CMP_STRIDE) + (CMP_LEN - 1) / 2.0
            return mask, slope * pos
        rel = t - n
        if m == "slc":
            return (rel >= 0) if diag else None, slope * n.astype(F32)
        return (rel >= 0) & (rel < WIN), slope * n.astype(F32)


def _scores(cfg, s_raw, t, n, h, selx, diag, lse=None):
    s = s_raw * (cfg.scale * LOG2E)
    mask, key_term = cfg.mask_bias(t, n, h, selx, diag)
    if key_term is not None:
        s = s + key_term
    if selx is not None:
        s = s + selx
    if lse is None:
        if mask is not None:
            s = jnp.where(mask, s, NEG)
        return s, mask
    p = jnp.exp2(jnp.minimum(s - lse, 0.0))
    if mask is not None:
        p = jnp.where(mask, p, 0.0)
    return p, mask


def _block_of_key(k0, tk, keys_on_rows, value=NEG):
    shape = (tk, LANE) if keys_on_rows else (LANE, tk)
    n = lax.broadcasted_iota(jnp.int32, shape, 0 if keys_on_rows else 1) + k0
    j = lax.broadcasted_iota(jnp.int32, shape, 1 if keys_on_rows else 0)
    return jnp.where((n >> 6) == j, value, 0.0).astype(BF16)


def _to_row(col):
    t = col.shape[0]
    return jnp.transpose(jnp.broadcast_to(col, (t, LANE)))[0:1, :]


def _load_keys(k_refs, rows):
    parts = [r[rows, :].astype(BF16) for r in k_refs]
    return parts[0] if len(parts) == 1 else jnp.concatenate(parts, axis=1)


def _attn_fwd(cfg, q, ks, v, sel, name):
    s, tq, tk, tb, nsub = cfg.s, cfg.tq, cfg.tk, cfg.tb, cfg.nsub
    has_sel = sel is not None
    nkp = len(ks)

    def kern(*refs):
        q_ref, k_refs, v_ref = refs[0], refs[1:1 + nkp], refs[1 + nkp]
        sel_ref = refs[2 + nkp] if has_sel else None
        o_ref, lc_ref, lr_ref = refs[2 + nkp + has_sel:5 + nkp + has_sel]
        h, i = pl.program_id(0), pl.program_id(1)
        part = [slice(r * tb, (r + 1) * tb) for r in range(nsub)]
        qs = [q_ref[p, :].astype(BF16) for p in part]
        ts = [i * tq + r * tb + lax.broadcasted_iota(jnp.int32, (tb, 1), 0) for r in range(nsub)]
        sels = [sel_ref[p, :].astype(BF16) for p in part] if has_sel else None

        def load(k0):
            rows = pl.ds(k0, tk)
            return _load_keys(k_refs, rows), v_ref[rows, :].astype(BF16)

        def soft(r, k0, s_raw, vv, emat, carry, diag):
            m, l, acc = carry
            n = k0 + lax.broadcasted_iota(jnp.int32, (1, tk), 1)
            selx = _nn(sels[r], emat) if has_sel else None
            sc, mask = _scores(cfg, s_raw, ts[r], n, h, selx, diag)
            m_new = jnp.maximum(m, jnp.max(sc, axis=1, keepdims=True))
            alpha = jnp.exp2(m - m_new)
            p = jnp.exp2(sc - m_new)
            if mask is not None:
                p = jnp.where(mask, p, 0.0)
            l = alpha * l + jnp.sum(p, axis=1, keepdims=True)
            acc = alpha * acc + _nn(p.astype(BF16), vv)
            return m_new, l, acc

        def step(r, k0, kk, vv, emat, carry, diag):
            return soft(r, k0, _nt(qs[r], kk), vv, emat, carry, diag)

        def chunk(k0, carry, diag):
            kk, vv = load(k0)
            emat = _block_of_key(k0, tk, False) if has_sel else None
            return tuple(step(r, k0, kk, vv, emat, carry[r], diag) for r in range(nsub))

        init = (jnp.full((tb, 1), NEG, F32), jnp.zeros((tb, 1), F32), jnp.zeros((tb, HEAD_V), F32))
        carry = (init,) * nsub
        if cfg.causal:
            buf_a, buf_b = refs[-2:]
            full = (i * tq) // tk

            def scores_into(buf, c):
                kk = _load_keys(k_refs, pl.ds(pl.multiple_of(c * tk, tk), tk))
                for r in range(nsub):
                    buf[r] = _nt(qs[r], kk)

            def consume(buf, c, cr, diag):
                k0 = pl.multiple_of(c * tk, tk)
                vv = v_ref[pl.ds(k0, tk), :].astype(BF16)
                emat = _block_of_key(k0, tk, False) if has_sel else None
                return tuple(soft(r, k0, buf[r], vv, emat, cr[r], diag) for r in range(nsub))

            def pair(p, cr):
                scores_into(buf_b, 2 * p + 1)
                cr = consume(buf_a, 2 * p, cr, False)
                scores_into(buf_a, 2 * p + 2)
                return consume(buf_b, 2 * p + 1, cr, False)

            def odd_tail(cr):
                scores_into(buf_b, full)
                cr = consume(buf_a, full - 1, cr, False)
                return consume(buf_b, full, cr, True)

            scores_into(buf_a, 0)
            carry = lax.fori_loop(0, full // 2, pair, carry)
            carry = lax.cond(full % 2 == 1, odd_tail, lambda cr: consume(buf_a, full, cr, True), carry)
        elif cfg.mode == "win":
            starts = [pl.multiple_of(jnp.maximum(i * tq + r * tb - WIN, 0), tb) for r in range(nsub)]
            carry = tuple(step(r, k0, *load(k0), None, carry[r], True) for r, k0 in enumerate(starts))
        else:
            carry = chunk(0, carry, True)
        for r, (m, l, acc) in enumerate(carry):
            o_ref[part[r], :] = acc / (l + 1e-20)
            lse = m + jnp.log(l + 1e-20) * LOG2E
            lc_ref[0, part[r], :] = lse
            lr_ref[0, r] = _to_row(lse)

    ins = [pl.BlockSpec((tq, q.width), lambda h, i: (i, q.col(h)))]
    ins += [pl.BlockSpec((cfg.sk, p.width), lambda h, i, p=p: (0, p.col(h))) for p in ks]
    ins += [pl.BlockSpec((cfg.sk, HEAD_V), lambda h, i: (0, v.col(h)))]
    args = [q.arr] + [p.arr for p in ks] + [v.arr]
    if has_sel:
        ins.append(pl.BlockSpec((tq, LANE), lambda h, i: (i, 0)))
        args.append(sel)
    return _pcall(
        kern, name=name, grid=(cfg.h, cfg.nq), in_specs=ins,
        out_specs=[pl.BlockSpec((tq, HEAD_V), lambda h, i: (i, h)),
                   pl.BlockSpec((1, tq, 1), lambda h, i: (h, i, 0)),
                   pl.BlockSpec((1, nsub, 1, tb), lambda h, i: (h, i, 0, 0))],
        out_shape=[jax.ShapeDtypeStruct((s, cfg.h * HEAD_V), F32),
                   jax.ShapeDtypeStruct((cfg.h, s, 1), F32),
                   jax.ShapeDtypeStruct((cfg.h, cfg.nb, 1, tb), F32)],
        scratch_shapes=[pltpu.VMEM((nsub, tb, tk), F32)] * 2 if cfg.causal else [],
        compiler_params=pltpu.CompilerParams(dimension_semantics=("parallel", "parallel")),
    )(*args)


def _attn_dq(cfg, q, ks, v, sel, o, lse, do, dq_in, name):
    s, tq, tk, dk, tb, nsub = cfg.s, cfg.tq, cfg.tk, cfg.dk, cfg.tb, cfg.nsub
    has_sel = sel is not None
    has_in = dq_in is not None
    nkp = len(ks)

    def kern(*refs):
        refs = list(refs)
        q_ref, k_refs, v_ref = refs[0], refs[1:1 + nkp], refs[1 + nkp]
        p0 = 2 + nkp
        sel_ref = refs[p0] if has_sel else None
        p0 += has_sel
        o_ref, l_ref, do_ref = refs[p0:p0 + 3]
        p0 += 3
        in_ref = refs[p0] if has_in else None
        p0 += has_in
        dq_ref, dr_ref = refs[p0:p0 + 2]
        h, i = pl.program_id(0), pl.program_id(1)
        part = [slice(r * tb, (r + 1) * tb) for r in range(nsub)]
        qs = [q_ref[p, :].astype(BF16) for p in part]
        ts = [i * tq + r * tb + lax.broadcasted_iota(jnp.int32, (tb, 1), 0) for r in range(nsub)]
        sels = [sel_ref[p, :].astype(BF16) for p in part] if has_sel else None
        dvecs, dobs, lses = [], [], []
        for r, p in enumerate(part):
            dov = do_ref[p, :]
            dvec = jnp.sum(dov * o_ref[p, :], axis=1, keepdims=True)
            dr_ref[0, r] = _to_row(dvec)
            dvecs.append(dvec)
            dobs.append(dov.astype(BF16))
            lses.append(l_ref[0, p, :])

        def load(k0):
            rows = pl.ds(k0, tk)
            return _load_keys(k_refs, rows), v_ref[rows, :].astype(BF16)

        def grad(r, k0, s_raw, dp, kk, emat, acc, diag):
            n = k0 + lax.broadcasted_iota(jnp.int32, (1, tk), 1)
            selx = _nn(sels[r], emat) if has_sel else None
            p, _ = _scores(cfg, s_raw, ts[r], n, h, selx, diag, lses[r])
            ds = p * (dp - dvecs[r])
            return acc + _nn(ds.astype(BF16), kk)

        def step(r, k0, kk, vv, emat, acc, diag):
            return grad(r, k0, _nt(qs[r], kk), _nt(dobs[r], vv), kk, emat, acc, diag)

        def chunk(k0, accs, diag):
            kk, vv = load(k0)
            emat = _block_of_key(k0, tk, False) if has_sel else None
            return tuple(step(r, k0, kk, vv, emat, accs[r], diag) for r in range(nsub))

        accs = (jnp.zeros((tb, dk), F32),) * nsub
        if cfg.causal:
            sa, pa, sb, pb = refs[-4:]
            full = (i * tq) // tk

            def products_into(sbuf, pbuf, c):
                kk, vv = load(pl.multiple_of(c * tk, tk))
                for r in range(nsub):
                    sbuf[r] = _nt(qs[r], kk)
                    pbuf[r] = _nt(dobs[r], vv)

            def consume(sbuf, pbuf, c, ac, diag):
                k0 = pl.multiple_of(c * tk, tk)
                kk = _load_keys(k_refs, pl.ds(k0, tk))
                emat = _block_of_key(k0, tk, False) if has_sel else None
                return tuple(grad(r, k0, sbuf[r], pbuf[r], kk, emat, ac[r], diag) for r in range(nsub))

            def pair(p, ac):
                products_into(sb, pb, 2 * p + 1)
                ac = consume(sa, pa, 2 * p, ac, False)
                products_into(sa, pa, 2 * p + 2)
                return consume(sb, pb, 2 * p + 1, ac, False)

            def odd_tail(ac):
                products_into(sb, pb, full)
                ac = consume(sa, pa, full - 1, ac, False)
                return consume(sb, pb, full, ac, True)

            products_into(sa, pa, 0)
            accs = lax.fori_loop(0, full // 2, pair, accs)
            accs = lax.cond(full % 2 == 1, odd_tail, lambda ac: consume(sa, pa, full, ac, True), accs)
        elif cfg.mode == "win":
            starts = [pl.multiple_of(jnp.maximum(i * tq + r * tb - WIN, 0), tb) for r in range(nsub)]
            accs = tuple(step(r, k0, *load(k0), None, accs[r], True) for r, k0 in enumerate(starts))
        else:
            accs = chunk(0, accs, True)
        for r, p in enumerate(part):
            dq_ref[p, :] = accs[r] * cfg.scale + in_ref[p, :] if has_in else accs[r] * cfg.scale

    qs = pl.BlockSpec((tq, dk), lambda h, i: (i, h))
    ins = [pl.BlockSpec((tq, q.width), lambda h, i: (i, q.col(h)))]
    ins += [pl.BlockSpec((cfg.sk, p.width), lambda h, i, p=p: (0, p.col(h))) for p in ks]
    ins += [pl.BlockSpec((cfg.sk, HEAD_V), lambda h, i: (0, v.col(h)))]
    args = [q.arr] + [p.arr for p in ks] + [v.arr]
    if has_sel:
        ins.append(pl.BlockSpec((tq, LANE), lambda h, i: (i, 0)))
        args.append(sel)
    ins += [pl.BlockSpec((tq, HEAD_V), lambda h, i: (i, o.col(h))),
            pl.BlockSpec((1, tq, 1), lambda h, i: (h, i, 0)),
            pl.BlockSpec((tq, HEAD_V), lambda h, i: (i, do.col(h)))]
    args += [o.arr, lse, do.arr]
    if has_in:
        ins.append(qs)
        args.append(dq_in)
    return _pcall(
        kern, name=name, grid=(cfg.h, cfg.nq), in_specs=ins,
        out_specs=[qs, pl.BlockSpec((1, nsub, 1, tb), lambda h, i: (h, i, 0, 0))],
        out_shape=[jax.ShapeDtypeStruct((s, cfg.h * dk), F32),
                   jax.ShapeDtypeStruct((cfg.h, cfg.nb, 1, tb), F32)],
        scratch_shapes=[pltpu.VMEM((nsub, tb, tk), F32)] * 4 if cfg.causal else [],
        compiler_params=pltpu.CompilerParams(dimension_semantics=("parallel", "parallel")),
    )(*args)


def _attn_dkv(cfg, q, ks, v, selt, lse_r, d_r, do, name):
    s, tq, tkb, dk, kb, ksub = cfg.s, cfg.tb, cfg.tkb, cfg.dk, cfg.kb, cfg.ksub
    nq = cfg.nb
    has_sel = selt is not None
    nkp = len(ks)
    outs = list(ks) + [v]

    def kern(*refs):
        k_refs, v_ref = refs[:nkp], refs[nkp]
        q_ref, do_ref, lr_ref, dr_ref = refs[nkp + 1:nkp + 5]
        st_ref = refs[nkp + 5] if has_sel else None
        out_refs = refs[nkp + 5 + has_sel:2 * nkp + 6 + has_sel]
        sa, pa, sb, pb = refs[-4:]
        j, h = pl.program_id(0), pl.program_id(1)
        k0 = j * tkb
        part = [slice(u * kb, (u + 1) * kb) for u in range(ksub)]
        kks = [_load_keys(k_refs, p) for p in part]
        vvs = [v_ref[p, :].astype(BF16) for p in part]
        ns = [k0 + u * kb + lax.broadcasted_iota(jnp.int32, (kb, 1), 0) for u in range(ksub)]
        emats = [_block_of_key(k0 + u * kb, kb, True) for u in range(ksub)] if has_sel else None

        def load_q(i):
            rows = pl.ds(pl.multiple_of(i * tq, tq), tq)
            return q_ref[rows, :].astype(BF16), do_ref[rows, :].astype(BF16)

        def products_into(sbuf, pbuf, i):
            qi, doi = load_q(i)
            for u in range(ksub):
                sbuf[u] = _nt(kks[u], qi)
                pbuf[u] = _nt(vvs[u], doi)

        def consume(sbuf, pbuf, i, carry):
            qi, doi = load_q(i)
            t = i * tq + lax.broadcasted_iota(jnp.int32, (1, tq), 1)
            selt_i = st_ref[i].astype(BF16) if has_sel else None
            new = []
            for u in range(ksub):
                dk_acc, dv_acc = carry[u]
                selx = _nn(emats[u], selt_i) if has_sel else None
                pt, _ = _scores(cfg, sbuf[u], t, ns[u], h, selx, True, lr_ref[0, i])
                dv_acc = dv_acc + _nn(pt.astype(BF16), doi)
                dst = pt * (pbuf[u] - dr_ref[0, i])
                new.append((dk_acc + _nn(dst.astype(BF16), qi), dv_acc))
            return tuple(new)

        if cfg.causal:
            first, count = k0 // tq, nq - k0 // tq
        elif cfg.mode == "win":
            first = k0 // tq
            count = jnp.minimum((k0 + tkb + WIN - 2) // tq + 1, nq) - first
        else:
            first, count = 0, nq

        def pair(p, cr):
            i0 = first + 2 * p
            products_into(sb, pb, i0 + 1)
            cr = consume(sa, pa, i0, cr)
            products_into(sa, pa, i0 + 2)
            return consume(sb, pb, i0 + 1, cr)

        carry = ((jnp.zeros((kb, dk), F32), jnp.zeros((kb, HEAD_V), F32)),) * ksub
        products_into(sa, pa, first)
        carry = lax.fori_loop(0, count // 2 - 1, pair, carry)
        last = first + count - 2
        products_into(sb, pb, last + 1)
        carry = consume(sa, pa, last, carry)
        carry = consume(sb, pb, last + 1, carry)
        for u, (dk_acc, dv_acc) in enumerate(carry):
            vals, off = [], 0
            for p in ks:
                vals.append(dk_acc[:, off:off + p.width] * cfg.scale)
                off += p.width
            vals.append(dv_acc)
            for src, ref, val in zip(outs, out_refs, vals):
                if src.per_head:
                    ref[part[u], :] = val
                else:
                    @pl.when(h == 0)
                    def _(ref=ref, val=val, u=u):
                        ref[part[u], :] = val

                    @pl.when(h > 0)
                    def _(ref=ref, val=val, u=u):
                        ref[part[u], :] += val

    rowv = pl.BlockSpec((1, nq, 1, tq), lambda j, h: (h, 0, 0, 0))
    ins = [pl.BlockSpec((tkb, p.width), lambda j, h, p=p: (j, p.col(h))) for p in ks]
    ins += [pl.BlockSpec((tkb, HEAD_V), lambda j, h: (j, v.col(h))),
            pl.BlockSpec((s, q.width), lambda j, h: (0, q.col(h))),
            pl.BlockSpec((s, HEAD_V), lambda j, h: (0, do.col(h))), rowv, rowv]
    args = [p.arr for p in ks] + [v.arr, q.arr, do.arr, lse_r, d_r]
    if has_sel:
        ins.append(pl.BlockSpec((nq, LANE, tq), lambda j, h: (0, 0, 0)))
        args.append(selt)
    out_specs = [pl.BlockSpec((tkb, p.width), lambda j, h, p=p: (j, h if p.per_head else 0)) for p in outs]
    out_shape = [jax.ShapeDtypeStruct((cfg.sk, (cfg.h if p.per_head else 1) * p.width), F32) for p in outs]
    assert nq % 2 == 0 and (cfg.mode in ("cmp", "mem") or tkb % (2 * tq) == 0), (nq, tkb, tq)
    return _pcall(
        kern, name=name, grid=(cfg.sk // tkb, cfg.h), in_specs=ins, out_specs=out_specs, out_shape=out_shape,
        scratch_shapes=[pltpu.VMEM((ksub, kb, tq), F32)] * 4,
        compiler_params=pltpu.CompilerParams(dimension_semantics=("parallel", "arbitrary")),
    )(*args)


def _all_heads(cfg, src, rows, key=False):
    if src.per_head:
        assert src.col0 % cfg.h == 0
        width, col = cfg.h * src.width, src.col0 // cfg.h
    else:
        width, col = src.width, src.col0
    return pl.BlockSpec((rows, width), (lambda i: (0, col)) if key else (lambda i: (i, col)))


def _head_cols(src, hh):
    return slice(hh * src.width, (hh + 1) * src.width) if src.per_head else slice(None)


def _key_window(cfg, i, r):
    if cfg.mode == "win":
        return pl.ds(pl.multiple_of(jnp.maximum(i * cfg.tq + r * cfg.tb - WIN, 0), cfg.tb), cfg.tk)
    return pl.ds(0, cfg.tk)


def _attn_fwd_small(cfg, q, ks, v, name, overlap=None):
    s, tq, tk, tb, nsub, nh = cfg.s, cfg.tq, cfg.tk, cfg.tb, cfg.nsub, cfg.h
    nkp = len(ks)
    select = overlap is not None
    n_s = s // SLC_LEN
    top_n = min(SLC_TOPN, n_s)

    def kern(*refs):
        q_ref, k_refs, v_ref = refs[0], refs[1:1 + nkp], refs[1 + nkp]
        ov_ref = refs[2 + nkp] if select else None
        o_ref, lc_ref, lr_ref = refs[2 + nkp + select:5 + nkp + select]
        i = pl.program_id(0)
        imps = [jnp.zeros((tb, LANE), F32)] * nsub
        for r in range(nsub):
            rows = slice(r * tb, (r + 1) * tb)
            t = i * tq + r * tb + lax.broadcasted_iota(jnp.int32, (tb, 1), 0)
            win = _key_window(cfg, i, r)
            n = win.start + lax.broadcasted_iota(jnp.int32, (1, tk), 1)
            for hh in range(nh):
                qv = q_ref[rows, hh * cfg.dk:(hh + 1) * cfg.dk].astype(BF16)
                kk = _load_keys([kr.at[:, _head_cols(p, hh)] for kr, p in zip(k_refs, ks)], win)
                vv = v_ref[win, _head_cols(v, hh)].astype(BF16)
                sc, mask = _scores(cfg, _nt(qv, kk), t, n, hh, None, True)
                m = jnp.max(sc, axis=1, keepdims=True)
                e = jnp.exp2(sc - m)
                if mask is not None:
                    e = jnp.where(mask, e, 0.0)
                l = jnp.sum(e, axis=1, keepdims=True)
                o_ref[rows, hh * HEAD_V:(hh + 1) * HEAD_V] = _nn(e.astype(BF16), vv) / (l + 1e-20)
                lse = m + jnp.log(l + 1e-20) * LOG2E
                lc_ref[hh, rows, :] = lse
                lr_ref[hh, r] = _to_row(lse)
                if select:
                    imps[r] = imps[r] + _nn((e / (l + 1e-20)).astype(BF16), ov_ref[...])
        if select:
            sel_ref, selt_ref, imp_t = refs[5 + nkp + select:8 + nkp + select]
            for r in range(nsub):
                t = i * tq + r * tb + lax.broadcasted_iota(jnp.int32, (tb, 1), 0)
                j = lax.broadcasted_iota(jnp.int32, (tb, LANE), 1)
                cur = t >> 6
                imp = jnp.where((j == 0) | (j == cur) | (j == cur - 1), 1e9, imps[r])
                imp = jnp.where(j > cur, -1e9, imp)
                imp_t[r] = jnp.transpose(imp)
                mine = imp_t[r, 0:n_s, :]
                jrow = lax.broadcasted_iota(jnp.int32, (n_s, tb), 0)

                def count(k, rank):
                    other = imp_t[r, pl.ds(k, 1), :]
                    ahead = (other > mine) | ((other == mine) & (k < jrow))
                    return rank + jnp.where(ahead, 1.0, 0.0)

                rank = lax.fori_loop(0, n_s, count, jnp.zeros((n_s, tb), F32))
                cur_t = (i * tq + r * tb + lax.broadcasted_iota(jnp.int32, (1, tb), 1)) >> 6
                rejected = jnp.where((rank < top_n) & (jrow <= cur_t), 0.0, 1.0)
                if n_s < LANE:
                    rejected = jnp.concatenate([rejected, jnp.ones((LANE - n_s, tb), F32)], axis=0)
                selt_ref[r] = rejected
                sel_ref[r * tb:(r + 1) * tb, :] = jnp.transpose(rejected)

    ins = [_all_heads(cfg, q, tq)] + [_all_heads(cfg, p, cfg.sk, True) for p in ks]
    ins += [_all_heads(cfg, v, cfg.sk, True)]
    args = [q.arr] + [p.arr for p in ks] + [v.arr]
    out_specs = [pl.BlockSpec((tq, nh * HEAD_V), lambda i: (i, 0)),
                 pl.BlockSpec((nh, tq, 1), lambda i: (0, i, 0)),
                 pl.BlockSpec((nh, nsub, 1, tb), lambda i: (0, i, 0, 0))]
    out_shape = [jax.ShapeDtypeStruct((s, nh * HEAD_V), F32), jax.ShapeDtypeStruct((nh, s, 1), F32),
                 jax.ShapeDtypeStruct((nh, cfg.nb, 1, tb), F32)]
    scratch = []
    if select:
        ins.append(pl.BlockSpec((cfg.sk, LANE), lambda i: (0, 0)))
        args.append(overlap)
        out_specs += [pl.BlockSpec((tq, LANE), lambda i: (i, 0)), pl.BlockSpec((nsub, LANE, tb), lambda i: (i, 0, 0))]
        out_shape += [jax.ShapeDtypeStruct((s, LANE), F32), jax.ShapeDtypeStruct((cfg.nb, LANE, tb), F32)]
        scratch = [pltpu.VMEM((nsub, LANE, tb), F32)]
    return _pcall(kern, name=name, grid=(cfg.nq,), in_specs=ins, out_specs=out_specs, out_shape=out_shape,
                  scratch_shapes=scratch,
                  compiler_params=pltpu.CompilerParams(dimension_semantics=("parallel",)))(*args)


def _attn_dq_small(cfg, q, ks, v, o, lse, do, dq_in, name):
    s, tq, tk, tb, nsub, nh, dk = cfg.s, cfg.tq, cfg.tk, cfg.tb, cfg.nsub, cfg.h, cfg.dk
    nkp = len(ks)
    has_in = dq_in is not None

    def kern(*refs):
        q_ref, k_refs, v_ref = refs[0], refs[1:1 + nkp], refs[1 + nkp]
        o_ref, l_ref, do_ref = refs[2 + nkp:5 + nkp]
        in_ref = refs[5 + nkp] if has_in else None
        dq_ref, dr_ref = refs[5 + nkp + has_in:7 + nkp + has_in]
        i = pl.program_id(0)
        for r in range(nsub):
            rows = slice(r * tb, (r + 1) * tb)
            t = i * tq + r * tb + lax.broadcasted_iota(jnp.int32, (tb, 1), 0)
            win = _key_window(cfg, i, r)
            n = win.start + lax.broadcasted_iota(jnp.int32, (1, tk), 1)
            for hh in range(nh):
                vcols = slice(hh * HEAD_V, (hh + 1) * HEAD_V)
                qcols = slice(hh * dk, (hh + 1) * dk)
                qv = q_ref[rows, qcols].astype(BF16)
                kk = _load_keys([kr.at[:, _head_cols(p, hh)] for kr, p in zip(k_refs, ks)], win)
                vv = v_ref[win, _head_cols(v, hh)].astype(BF16)
                dov = do_ref[rows, vcols]
                dvec = jnp.sum(dov * o_ref[rows, vcols], axis=1, keepdims=True)
                dr_ref[hh, r] = _to_row(dvec)
                p, _ = _scores(cfg, _nt(qv, kk), t, n, hh, None, True, l_ref[hh, rows, :])
                ds = p * (_nt(dov.astype(BF16), vv) - dvec)
                dq = _nn(ds.astype(BF16), kk) * cfg.scale
                dq_ref[rows, qcols] = dq + in_ref[rows, qcols] if has_in else dq

    qs = pl.BlockSpec((tq, nh * dk), lambda i: (i, 0))
    ins = [_all_heads(cfg, q, tq)] + [_all_heads(cfg, p, cfg.sk, True) for p in ks]
    ins += [_all_heads(cfg, v, cfg.sk, True)]
    ins += [_all_heads(cfg, o, tq), pl.BlockSpec((nh, tq, 1), lambda i: (0, i, 0)), _all_heads(cfg, do, tq)]
    args = [q.arr] + [p.arr for p in ks] + [v.arr, o.arr, lse, do.arr]
    if has_in:
        ins.append(qs)
        args.append(dq_in)
    return _pcall(
        kern, name=name, grid=(cfg.nq,), in_specs=ins,
        out_specs=[qs, pl.BlockSpec((nh, nsub, 1, tb), lambda i: (0, i, 0, 0))],
        out_shape=[jax.ShapeDtypeStruct((s, nh * dk), F32), jax.ShapeDtypeStruct((nh, cfg.nb, 1, tb), F32)],
        compiler_params=pltpu.CompilerParams(dimension_semantics=("parallel",)))(*args)


def _attn_bwd(cfg, q, ks, v, sel, selt, o, lse, lse_r, do, dq_in, name):
    if not cfg.causal:
        dq, d_r = _attn_dq_small(cfg, q, ks, v, o, lse, do, dq_in, name + "_dq")
        res = _attn_dkv(cfg, q, ks, v, selt, lse_r, d_r, do, name + "_dkv")
        return dq, res[:-1], res[-1]
    dq, d_r = _attn_dq(cfg, q, ks, v, sel, o, lse, do, dq_in, name + "_dq")
    res = _attn_dkv(cfg, q, ks, v, selt, lse_r, d_r, do, name + "_dkv")
    return dq, res[:-1], res[-1]


def _silu_grad(pre):
    sg = _sigmoid(pre)
    return sg * (1.0 + pre * (1.0 - sg))


def _compress_fwd(a_lo, a_hi, pe_lo, pe_hi, w1_lo, w1_hi, w2, name):
    n, dp = a_lo.shape[0], w2.shape[1]

    def kern(alo, ahi, plo, phi, w1l, w1h, w2r, out_ref, pre_ref):
        xl = (alo[...] + plo[...]).astype(BF16)
        xh = (ahi[...] + phi[...]).astype(BF16)
        pre = _nn(xl, w1l[...]) + _nn(xh, w1h[...])
        act = pre * _sigmoid(pre)
        out_ref[...] = _nn(act.astype(BF16), w2r[...]).astype(BF16)
        pre_ref[...] = pre

    return _pcall(kern, name=name,
                  out_shape=[jax.ShapeDtypeStruct((n, dp), BF16), jax.ShapeDtypeStruct((n, dp), F32)],
                  )(a_lo, a_hi, pe_lo, pe_hi, w1_lo, w1_hi, w2)


def _compress_bwd(a_lo, a_hi, pe_lo, pe_hi, w1_lo, w1_hi, w2, pre, pre_sh, dout, dout_sh, name):
    n, ln = a_lo.shape
    dp = w2.shape[1]

    def kern(alo, ahi, plo, phi, w1l, w1h, w2r, pre_ref, presh_ref, do_ref, dosh_ref,
             da_ref, dpl_ref, dph_ref, dw1l_ref, dw1h_ref, dw2_ref):
        prev = pre_ref[...]
        act = prev * _sigmoid(prev)
        dob = do_ref[...].astype(BF16)
        w2v = w2r[...]
        dpre = (_nt(dob, w2v) * _silu_grad(prev)).astype(BF16)
        dpre_sh = (_nt(dosh_ref[...].astype(BF16), w2v) * _silu_grad(presh_ref[...])).astype(BF16)
        dw2_ref[...] = _nn(act.T.astype(BF16), dob)
        xl = alo[...] + plo[...]
        xh = ahi[...] + phi[...]
        dw1l_ref[...] = _nn(xl.T.astype(BF16), dpre)
        dw1h_ref[...] = _nn(xh.T.astype(BF16), dpre)
        dal = _nt(dpre, w1l[...])
        dah_sh = _nt(dpre_sh, w1h[...])
        da_ref[...] = dal + dah_sh
        dpl_ref[...] = jnp.sum(dal, axis=0, keepdims=True)
        dph_ref[...] = jnp.sum(dah_sh, axis=0, keepdims=True)

    return _pcall(
        kern, name=name,
        out_shape=[jax.ShapeDtypeStruct((n, ln), F32), jax.ShapeDtypeStruct((1, ln), F32),
                   jax.ShapeDtypeStruct((1, ln), F32), jax.ShapeDtypeStruct((ln, dp), F32),
                   jax.ShapeDtypeStruct((ln, dp), F32), jax.ShapeDtypeStruct((dp, dp), F32)],
    )(a_lo, a_hi, pe_lo, pe_hi, w1_lo, w1_hi, w2, pre, pre_sh, dout, dout_sh)


def _nsa_combine(o_cmp, o_slc, o_win, gl):
    s, w = o_cmp.shape
    tr = _tile(s, 512)

    def kern(a_ref, b_ref, c_ref, g_ref, o_ref):
        g = _sigmoid(g_ref[...])
        for h in range(NSA_HEADS):
            cs = slice(h * HEAD_V, (h + 1) * HEAD_V)
            o_ref[:, cs] = (g[:, 3 * h:3 * h + 1] * a_ref[:, cs] + g[:, 3 * h + 1:3 * h + 2] * b_ref[:, cs]
                            + g[:, 3 * h + 2:3 * h + 3] * c_ref[:, cs])

    row = pl.BlockSpec((tr, w), lambda i: (i, 0))
    return _pcall(kern, name="nsa_combine", grid=(s // tr,),
                  in_specs=[row, row, row, pl.BlockSpec((tr, LANE), lambda i: (i, gl.col0))], out_specs=row,
                  out_shape=jax.ShapeDtypeStruct((s, w), F32))(o_cmp, o_slc, o_win, gl.arr)


def _nsa_combine_bwd(do_cat, o_cmp, o_slc, o_win, gl):
    s, w = o_cmp.shape
    tr = _tile(s, 512)

    def kern(d_ref, a_ref, b_ref, c_ref, g_ref, da_ref, db_ref, dc_ref, dg_ref):
        g = _sigmoid(g_ref[...])
        lane = lax.broadcasted_iota(jnp.int32, (tr, LANE), 1)
        dgl = jnp.zeros((tr, LANE), F32)
        for h in range(NSA_HEADS):
            cs = slice(h * HEAD_V, (h + 1) * HEAD_V)
            dv = d_ref[:, cs]
            for b, (src, dst) in enumerate(((a_ref, da_ref), (b_ref, db_ref), (c_ref, dc_ref))):
                gate = g[:, 3 * h + b:3 * h + b + 1]
                dst[:, cs] = gate * dv
                dgate = jnp.sum(dv * src[:, cs], axis=1, keepdims=True)
                dgl = jnp.where(lane == 3 * h + b, dgate * gate * (1.0 - gate), dgl)
        dg_ref[...] = dgl

    row = pl.BlockSpec((tr, w), lambda i: (i, 0))
    tab = pl.BlockSpec((tr, LANE), lambda i: (i, 0))
    return _pcall(kern, name="nsa_combine_bwd", grid=(s // tr,),
                  in_specs=[pl.BlockSpec((tr, w), lambda i: (i, 2)), row, row, row,
                            pl.BlockSpec((tr, LANE), lambda i: (i, gl.col0))],
                  out_specs=[row, row, row, tab],
                  out_shape=[jax.ShapeDtypeStruct((s, w), F32)] * 3 + [jax.ShapeDtypeStruct((s, LANE), F32)],
                  )(do_cat, o_cmp, o_slc, o_win, gl.arr)


def _gate_fwd(o_mla, o_nsa, o_mem, hp):
    s = o_mla.shape[0]
    tr = _tile(s, 256)

    def kern(a_ref, b_ref, c_ref, z_ref, u_ref):
        z = z_ref[...]
        sz = z * _sigmoid(z)
        u_ref[:, 0:1024] = (a_ref[...] * sz[:, 0:1024]).astype(BF16)
        u_ref[:, 1024:1536] = (b_ref[...] * sz[:, 1024:1536]).astype(BF16)
        u_ref[:, 1536:2048] = (c_ref[...] * sz[:, 1536:2048]).astype(BF16)

    return _pcall(
        kern, name="gate_fwd", grid=(s // tr,),
        in_specs=[pl.BlockSpec((tr, 1024), lambda i: (i, 0)), pl.BlockSpec((tr, 512), lambda i: (i, 0)),
                  pl.BlockSpec((tr, 512), lambda i: (i, 0)), pl.BlockSpec((tr, 2048), lambda i: (i, 2))],
        out_specs=pl.BlockSpec((tr, 2048), lambda i: (i, 0)),
        out_shape=jax.ShapeDtypeStruct((s, 2048), BF16))(o_mla, o_nsa, o_mem, hp)


def _gate_bwd(du, o_mla, o_nsa, o_mem, hp):
    s = du.shape[0]
    tr = _tile(s, 256)

    def kern(d_ref, a_ref, b_ref, c_ref, z_ref, do_ref, dz_ref):
        z = z_ref[...]
        sg = _sigmoid(z)
        sz = z * sg
        dsz = sg * (1.0 + z * (1.0 - sg))
        d = d_ref[...]
        do_ref[...] = d * sz
        dz_ref[:, 0:1024] = d[:, 0:1024] * a_ref[...] * dsz[:, 0:1024]
        dz_ref[:, 1024:1536] = d[:, 1024:1536] * b_ref[...] * dsz[:, 1024:1536]
        dz_ref[:, 1536:2048] = d[:, 1536:2048] * c_ref[...] * dsz[:, 1536:2048]

    wide = pl.BlockSpec((tr, 2048), lambda i: (i, 0))
    return _pcall(
        kern, name="gate_bwd", grid=(s // tr,),
        in_specs=[wide, pl.BlockSpec((tr, 1024), lambda i: (i, 0)), pl.BlockSpec((tr, 512), lambda i: (i, 0)),
                  pl.BlockSpec((tr, 512), lambda i: (i, 0)), pl.BlockSpec((tr, 2048), lambda i: (i, 2))],
        out_specs=[wide, wide],
        out_shape=[jax.ShapeDtypeStruct((s, 2048), F32)] * 2)(du, o_mla, o_nsa, o_mem, hp)


def _tile2d(rows, cols, arrays):
    if rows % 16 == 0:
        return _row_tile(rows, cols * arrays), cols
    want = max(LANE, BLOCK_BYTES // (rows * 4 * arrays) // LANE * LANE)
    tc = LANE
    for t in range(LANE, cols + 1, LANE):
        if cols % t == 0 and t <= want:
            tc = t
    return rows, tc


def _sum_slots(buf, name):
    n, rows, cols = buf.shape
    tr, tc = _tile2d(rows, cols, n)

    def kern(b_ref, o_ref):
        acc = b_ref[0].astype(F32)
        for i in range(1, n):
            acc = acc + b_ref[i].astype(F32)
        o_ref[...] = acc

    return _pcall(kern, name=name, grid=(rows // tr, cols // tc),
                  in_specs=[pl.BlockSpec((n, tr, tc), lambda i, j: (0, i, j))],
                  out_specs=pl.BlockSpec((tr, tc), lambda i, j: (i, j)),
                  out_shape=jax.ShapeDtypeStruct((rows, cols), F32))(buf)


def _pair_sum(g4, theirs, core, axis, name):
    n, rows, cols = theirs.shape
    tr, tc = _tile2d(rows, cols, 1)
    nbr, nbc = rows // tr, cols // tc

    def kern(c_ref, a_ref, b_ref, o_ref):
        o_ref[...] = (a_ref[...] + b_ref[...]).astype(BF16)

    blk = (1, tr, tc)
    mine = ((lambda s, i, j, c: (s, c[0] * nbr + i, j)) if axis == 0
            else (lambda s, i, j, c: (s, i, c[0] * nbc + j)))
    grid_spec = pltpu.PrefetchScalarGridSpec(
        num_scalar_prefetch=1, grid=(n, nbr, nbc),
        in_specs=[pl.BlockSpec(blk, mine), pl.BlockSpec(blk, lambda s, i, j, c: (s, i, j))],
        out_specs=pl.BlockSpec(blk, lambda s, i, j, c: (s, i, j)))
    return _pcall(kern, name=name, grid_spec=grid_spec,
                  out_shape=jax.ShapeDtypeStruct((n, rows, cols), BF16))(core, g4, theirs)


def _adamw(w, g, m, v, name):
    rows, cols = w.shape
    tr, tc = _tile2d(rows, cols, 4)
    bc1 = 1.0 - ADAM_B1 ** ADAM_STEP
    bc2 = 1.0 - ADAM_B2 ** ADAM_STEP

    def kern(w_ref, g_ref, m_ref, v_ref, d_ref, mo_ref, vo_ref):
        gv = g_ref[...]
        mn = ADAM_B1 * m_ref[...] + (1.0 - ADAM_B1) * gv
        vn = ADAM_B2 * v_ref[...] + (1.0 - ADAM_B2) * (gv * gv)
        d_ref[...] = -ADAM_LR * ((mn / bc1) / (jnp.sqrt(vn / bc2) + ADAM_EPS) + ADAM_WD * w_ref[...])
        mo_ref[...] = mn
        vo_ref[...] = vn

    blk = pl.BlockSpec((tr, tc), lambda i, j: (i, j))
    return _pcall(kern, name=name, grid=(rows // tr, cols // tc), in_specs=[blk] * 4, out_specs=[blk] * 3,
                  out_shape=[jax.ShapeDtypeStruct((rows, cols), F32)] * 3)(w, g, m, v)


ANY = pl.BlockSpec(memory_space=pl.ANY)


def _place():
    x, y, c = lax.axis_index("x"), lax.axis_index("y"), lax.axis_index("c")
    chips = [(1 - x, y), (x, 1 - y), (1 - x, 1 - y)]
    return x, y, c, chips


def _remote(src, dst, send_sem, recv_sem, to):
    return pltpu.make_async_remote_copy(src_ref=src, dst_ref=dst, send_sem=send_sem, recv_sem=recv_sem,
                                        device_id=to, device_id_type=MESH)


def _half(ref, lead, core, axis):
    size = ref.shape[len(lead) + axis] // 2
    cut = pl.ds(core * size, size)
    return ref.at[tuple(lead) + ((cut, slice(None)) if axis == 0 else (slice(None), cut))]


def _gather_shards(ws, axes):
    nw = len(ws)

    def body(*refs):
        w_refs, out_refs = refs[:nw], refs[nw:2 * nw]
        send_sems, recv_sems = refs[2 * nw:]
        x, y, c, chips = _place()
        me = 2 * x + y
        sibling = (x, y, 1 - c)

        def part(i, slot, core):
            return _half(out_refs[i], (slot,), core, axes[i])

        def copy(sem, src, dst, to):
            return _remote(src, dst, send_sems.at[sem], recv_sems.at[sem], to)

        first = [copy(j * nw + i, _half(w_refs[i], (), c, axes[i]), part(i, me, c), (*chip, c))
                 for j, chip in enumerate(chips) for i in range(nw)]
        for cp in first:
            cp.start()
        passed = []
        for j, (cx, cy) in enumerate(chips):
            slot = 2 * cx + cy
            for i in range(nw):
                copy(j * nw + i, part(i, slot, c), part(i, slot, c), (x, y, c)).wait_recv()
                fwd = copy((3 + j) * nw + i, part(i, slot, c), part(i, slot, c), sibling)
                fwd.start()
                passed.append(fwd)
        for j, (cx, cy) in enumerate(chips):
            slot = 2 * cx + cy
            for i in range(nw):
                copy((3 + j) * nw + i, part(i, slot, 1 - c), part(i, slot, 1 - c), (x, y, c)).wait_recv()
        for cp in first + passed:
            cp.wait_send()

    return _pcall(
        body, name="gather_shards", in_specs=[ANY] * nw, out_specs=[ANY] * nw,
        out_shape=[jax.ShapeDtypeStruct((4,) + w.shape, w.dtype) for w in ws],
        scratch_shapes=[pltpu.SemaphoreType.DMA((6 * nw,)), pltpu.SemaphoreType.DMA((6 * nw,))],
    )(*ws)


def _half_shape(shape, axis):
    return tuple(d // 2 if k == len(shape) - 2 + axis else d for k, d in enumerate(shape))


def _pair_exchange(gs, axes):
    nw = len(gs)

    def body(*refs):
        g_refs, out_refs = refs[:nw], refs[nw:2 * nw]
        send_sems, recv_sems = refs[2 * nw:]
        x, y, c, _ = _place()
        cps = []
        for i in range(nw):
            cp = _remote(_half(g_refs[i], (slice(None),), 1 - c, axes[i]), out_refs[i],
                         send_sems.at[i], recv_sems.at[i], (x, y, 1 - c))
            cp.start()
            cps.append(cp)
        for cp in cps:
            cp.wait()

    return _pcall(body, name="pair_exchange", in_specs=[ANY] * nw, out_specs=[ANY] * nw,
                  out_shape=[jax.ShapeDtypeStruct(_half_shape(g.shape, a), g.dtype) for g, a in zip(gs, axes)],
                  scratch_shapes=[pltpu.SemaphoreType.DMA((nw,)), pltpu.SemaphoreType.DMA((nw,))])(*gs)


def _chip_exchange(ps):
    nw = len(ps)

    def body(*refs):
        p_refs, out_refs = refs[:nw], refs[nw:2 * nw]
        send_sems, recv_sems, local_sems = refs[2 * nw:]
        x, y, c, chips = _place()
        me = 2 * x + y
        mine = [pltpu.make_async_copy(p_refs[i].at[me], out_refs[i].at[me], local_sems.at[i]) for i in range(nw)]
        for cp in mine:
            cp.start()
        sends = []
        for j, (cx, cy) in enumerate(chips):
            for i in range(nw):
                cp = _remote(p_refs[i].at[2 * cx + cy], out_refs[i].at[me], send_sems.at[j * nw + i],
                             recv_sems.at[j * nw + i], (cx, cy, c))
                cp.start()
                sends.append(cp)
        for j, (cx, cy) in enumerate(chips):
            slot = 2 * cx + cy
            for i in range(nw):
                _remote(out_refs[i].at[slot], out_refs[i].at[slot], send_sems.at[j * nw + i],
                        recv_sems.at[j * nw + i], (x, y, c)).wait_recv()
        for cp in sends:
            cp.wait_send()
        for cp in mine:
            cp.wait()

    return _pcall(body, name="chip_exchange", in_specs=[ANY] * nw, out_specs=[ANY] * nw,
                  out_shape=[jax.ShapeDtypeStruct(p.shape, p.dtype) for p in ps],
                  scratch_shapes=[pltpu.SemaphoreType.DMA((3 * nw,)), pltpu.SemaphoreType.DMA((3 * nw,)),
                                  pltpu.SemaphoreType.DMA((nw,))])(*ps)


def _half_exchange(ts, axes):
    nw = len(ts)

    def body(*refs):
        t_refs, out_refs = refs[:nw], refs[nw:2 * nw]
        send_sems, recv_sems = refs[2 * nw:]
        x, y, c, _ = _place()
        sends = []
        for i in range(nw):
            cp = _remote(t_refs[i], _half(out_refs[i], (), c, axes[i]), send_sems.at[i], recv_sems.at[i],
                         (x, y, 1 - c))
            cp.start()
            sends.append(cp)
        for i in range(nw):
            _remote(t_refs[i], _half(out_refs[i], (), 1 - c, axes[i]), send_sems.at[i], recv_sems.at[i],
                    (x, y, c)).wait_recv()
        for cp in sends:
            cp.wait_send()

    def whole(t, a):
        return tuple(2 * d if k == a else d for k, d in enumerate(t.shape))

    return _pcall(body, name="half_exchange", in_specs=[ANY] * nw, out_specs=[ANY] * nw,
                  out_shape=[jax.ShapeDtypeStruct(whole(t, a), t.dtype) for t, a in zip(ts, axes)],
                  scratch_shapes=[pltpu.SemaphoreType.DMA((nw,)), pltpu.SemaphoreType.DMA((nw,))])(*ts)


def _gather_all(v):
    rows, cols = v.shape

    def body(v_ref, out_ref, send_sems, recv_sems, local_sem):
        x, y, c, _ = _place()
        me = 4 * x + 2 * y + c
        mine = pltpu.make_async_copy(v_ref, out_ref.at[me], local_sem)
        mine.start()
        sends = []
        for d in range(1, 8):
            peer = (x ^ (d >> 2), y ^ ((d >> 1) & 1), c ^ (d & 1))
            cp = _remote(v_ref, out_ref.at[me], send_sems.at[d - 1], recv_sems.at[d - 1], peer)
            cp.start()
            sends.append(cp)
        for d in range(1, 8):
            slot = 4 * (x ^ (d >> 2)) + 2 * (y ^ ((d >> 1) & 1)) + (c ^ (d & 1))
            _remote(v_ref, out_ref.at[slot], send_sems.at[d - 1], recv_sems.at[d - 1], (x, y, c)).wait_recv()
        for cp in sends:
            cp.wait_send()
        mine.wait()

    return _pcall(body, name="gather_all", in_specs=[ANY], out_specs=ANY,
                  out_shape=jax.ShapeDtypeStruct((8, rows, cols), v.dtype),
                  scratch_shapes=[pltpu.SemaphoreType.DMA((7,)), pltpu.SemaphoreType.DMA((7,)),
                                  pltpu.SemaphoreType.DMA])(v)


def _pad_cols(a, width):
    return a if a.shape[1] == width else jnp.pad(a, ((0, 0), (0, width - a.shape[1])))


def _unpad_segments():
    z = PAD["z"]
    segs = [(PAD["c_q"], 0, 512), (PAD["c_kv"], 512, 512), (PAD["k_rope"], 1024, 64), (z, 1088, 1024)]
    segs += [(PAD["q_nsa"] + 256 * h, 2112 + NSA_DK * h, NSA_DK) for h in range(NSA_HEADS)]
    for name, rows in (("k_c", 192), ("v_c", 128), ("k_s", 192), ("v_s", 128), ("k_w", 192), ("v_w", 128),
                       ("g_nsa", 12)):
        segs.append((PAD[name], ORIG[name][0], rows))
    segs += [(z + 1024, ORIG["z_nsa"][0], 512), (PAD["q_mem"], ORIG["q_mem"][0], 512),
             (z + 1536, ORIG["z_mem"][0], 512)]
    return segs


def _w_in_grad_slots(gt):
    rows, cols = gt.shape
    shard = sum(n for _, _, n in _unpad_segments()) // 4
    tc = 256
    pieces = []
    for src, dst, n in _unpad_segments():
        while n:
            slot, off = divmod(dst, shard)
            take = min(n, shard - off)
            pieces.append((src, slot, off, take))
            src, dst, n = src + take, dst + take, n - take

    def kern(g_ref, o_ref):
        for src, slot, off, take in pieces:
            o_ref[slot, off:off + take, :] = g_ref[src:src + take, :]

    return _pcall(kern, name="w_in_grad_slots", grid=(cols // tc,),
                  in_specs=[pl.BlockSpec((rows, tc), lambda i: (0, i))],
                  out_specs=pl.BlockSpec((4, shard, tc), lambda i: (0, 0, i)),
                  out_shape=jax.ShapeDtypeStruct((4, shard, cols), F32))(gt)


def _w_in_from_slots(ws):
    nslot, shard, cols = ws.shape
    tc = 256
    pieces = []
    for dst, src, n in _unpad_segments() + [(PAD["k_rope"] + 64, ORIG["k_rope"][0], 64)]:
        while n:
            slot, off = divmod(src, shard)
            take = min(n, shard - off)
            pieces.append((dst, slot, off, take))
            src, dst, n = src + take, dst + take, n - take

    def kern(w_ref, o_ref):
        o_ref[...] = jnp.zeros_like(o_ref)
        for dst, slot, off, take in pieces:
            o_ref[dst:dst + take, :] = w_ref[slot, off:off + take, :]

    return _pcall(kern, name="w_in_from_slots", grid=(cols // tc,),
                  in_specs=[pl.BlockSpec((nslot, shard, tc), lambda i: (0, 0, i))],
                  out_specs=pl.BlockSpec((D_PAD, tc), lambda i: (0, i)),
                  out_shape=jax.ShapeDtypeStruct((D_PAD, cols), ws.dtype))(ws)


def _rope_tables(s):
    pos = jnp.arange(s, dtype=F32)
    inv_freq = ROPE_THETA ** (-jnp.arange(0, 64, 2, dtype=F32) / 64)
    ang = pos[:, None] * inv_freq[None, :]
    cos, sin = jnp.cos(ang), jnp.sin(ang)
    z = jnp.zeros((s, 64), F32)
    return jnp.concatenate([cos, cos, z], axis=1), jnp.concatenate([-sin, sin, z], axis=1)


def _overlap_table(s):
    n_c, n_s = s // CMP_STRIDE, s // SLC_LEN
    c0 = np.arange(n_c)[:, None] * CMP_STRIDE
    s0 = np.arange(LANE)[None, :] * SLC_LEN
    ov = (c0 < s0 + SLC_LEN) & (c0 + CMP_LEN > s0) & (np.arange(n_c)[:, None] < n_c - 1) & (np.arange(LANE)[None, :] < n_s)
    return jnp.asarray(ov.astype(np.float32), dtype=BF16)


def _shift_down(a):
    return jnp.concatenate([jnp.zeros((8, a.shape[1]), a.dtype), a], axis=0)[7:7 + a.shape[0]]


def _shift_up(a):
    return jnp.concatenate([a, jnp.zeros((8, a.shape[1]), a.dtype)], axis=0)[1:1 + a.shape[0]]


def _local_step(x, mem, target, w):
    s = x.shape[0]
    cs, sn = _rope_tables(s)
    t_ = jnp.transpose

    w_in_p = _w_in_from_slots(w["w_in_t"])
    xn, rstd_x = _rms_fwd(_Src(x, D_MODEL), w["norm_g"], "norm_x")
    hp, hpb = _mm(xn, w_in_p, "in_proj", mode="nt", second_dtype=BF16)

    w_uq3 = w["w_uq"].reshape(512, MLA_HEADS, 192)
    w_uq_p = jnp.concatenate([w_uq3, w_uq3[:, :, 128:]], axis=2).reshape(512, MLA_HEADS * 256)
    w_ukv_p = t_(w["w_ukv"].reshape(512, MLA_HEADS, 2, 128), (0, 2, 1, 3)).reshape(512, 2048)
    c_q, c_kv = _Src(hp, 512, 0), _Src(hp, 512, 1)
    cqn, rstd_q = _rms_fwd(c_q, w["q_norm_g"], "norm_q")
    ckvn, rstd_kv = _rms_fwd(c_kv, w["kv_norm_g"], "norm_kv")
    q_lin = _mm(cqn, w_uq_p, "mla_q_proj")
    kvb = _mm(ckvn, w_ukv_p, "mla_kv_proj", out_dtype=BF16)
    q_mla = _rope_fwd(_Src(q_lin, MLA_HEADS * 256), cs, sn, MLA_HEADS, 256, LANE, "rope_q")
    k_pe = _rope_fwd(_Src(hp, LANE, PAD["k_rope"] // LANE), cs, sn, 1, LANE, 0, "rope_k")
    mla = _Attn("mla", s, s, MLA_HEADS, 256)
    mla_q, mla_v = _Src(q_mla, 256), _Src(kvb, LANE, MLA_HEADS)
    mla_k = [_Src(kvb, LANE), _Src(k_pe, LANE, 0, False)]
    o_mla, l_mla, lr_mla = _attn_fwd(mla, mla_q, mla_k, mla_v, None, "mla_fwd")

    sk = s // CMP_STRIDE
    pe_k, pe_v = w["cmp_pe_k"], w["cmp_pe_v"]
    w1k = _pad_cols(w["cmp_w1k"], 256)
    w2k = jnp.pad(w["cmp_w2k"], ((0, 64), (0, 64))).astype(BF16)
    w1v, w2v = w["cmp_w1v"], w["cmp_w2v"].astype(BF16)
    half_k, half_v = CMP_STRIDE * NSA_DK, CMP_STRIDE * HEAD_V
    ak = hp[:, PAD["k_c"]:PAD["k_c"] + NSA_DK].reshape(sk, half_k)
    av = hp[:, PAD["v_c"]:PAD["v_c"] + HEAD_V].reshape(sk, half_v)
    ck_args = (ak, _shift_up(ak), pe_k[:CMP_STRIDE].reshape(1, half_k), pe_k[CMP_STRIDE:].reshape(1, half_k),
               w1k[:half_k], w1k[half_k:], w2k)
    cv_args = (av, _shift_up(av), pe_v[:CMP_STRIDE].reshape(1, half_v), pe_v[CMP_STRIDE:].reshape(1, half_v),
               w1v[:half_v], w1v[half_v:], w2v)
    k_cmp, pre_k = _compress_fwd(*ck_args, "compress_k")
    v_cmp, pre_v = _compress_fwd(*cv_args, "compress_v")
    cmp_ = _Attn("cmp", s, sk, NSA_HEADS, 256)
    slc = _Attn("slc", s, s, NSA_HEADS, 256)
    win = _Attn("win", s, s, NSA_HEADS, 256)
    nsa_q = _Src(hpb, 256, PAD["q_nsa"] // 256)
    cmp_k, cmp_v = [_Src(k_cmp, 256, 0, False)], _Src(v_cmp, HEAD_V, 0, False)
    slc_k, slc_v = [_Src(hpb, 256, PAD["k_s"] // 256, False)], _Src(hpb, HEAD_V, PAD["v_s"] // HEAD_V, False)
    win_k, win_v = [_Src(hpb, 256, PAD["k_w"] // 256, False)], _Src(hpb, HEAD_V, PAD["v_w"] // HEAD_V, False)
    o_cmp, l_cmp, lr_cmp, sel, selt = _attn_fwd_small(cmp_, nsa_q, cmp_k, cmp_v, "cmp_fwd", _overlap_table(s))
    o_slc, l_slc, lr_slc = _attn_fwd(slc, nsa_q, slc_k, slc_v, sel, "slc_fwd")
    o_win, l_win, lr_win = _attn_fwd_small(win, nsa_q, win_k, win_v, "win_fwd")
    gl = _Src(hp, LANE, PAD["g_nsa"] // LANE)
    o_nsa = _nsa_combine(o_cmp, o_slc, o_win, gl)

    mn, rstd_m = _rms_fwd(_Src(mem, D_MODEL), w["mem_norm_g"], "norm_mem")
    kvm = _mm(mn, w["w_mem_kv"], "mem_kv_proj", out_dtype=BF16)
    mem_ = _Attn("mem", s, mem.shape[0], MEM_HEADS, LANE)
    mem_q, mem_k, mem_v = _Src(hpb, LANE, PAD["q_mem"] // LANE), [_Src(kvm, LANE)], _Src(kvm, LANE, MEM_HEADS)
    o_mem, l_mem, lr_mem = _attn_fwd_small(mem_, mem_q, mem_k, mem_v, "mem_fwd")

    u = _gate_fwd(o_mla, o_nsa, o_mem, hp)
    proj = _mm(u, w["w_out"], "out_proj")
    dy, g_final, loss = _final_loss(x, proj, w["final_norm_g"].reshape(1, -1), target)

    g_w_out = _mm(u, dy, "out_proj_dw", mode="tn")
    du = _mm(dy, w["w_out"], "out_proj_dx", mode="nt")
    do_cat, dz = _gate_bwd(du, o_mla, o_nsa, o_mem, hp)

    dq_mem, (dk_mem,), dv_mem = _attn_bwd(mem_, mem_q, mem_k, mem_v, None, None, _Src(o_mem, HEAD_V), l_mem,
                                          lr_mem, _Src(do_cat, HEAD_V, 12), None, "mem_bwd")
    dkvm = jnp.concatenate([dk_mem, dv_mem], axis=1)
    g_w_mem_kv = _mm(mn, dkvm, "mem_kv_dw", mode="tn")
    dmn = _mm(dkvm, w["w_mem_kv"], "mem_kv_dx", mode="nt")
    _, g_mem_norm = _rms_bwd(_Src(mem, D_MODEL), w["mem_norm_g"], rstd_m, dmn, None, "norm_mem_bwd")

    do_cmp, do_slc, do_win, dgl = _nsa_combine_bwd(do_cat, o_cmp, o_slc, o_win, gl)
    dq_n, (dk_cmp,), dv_cmp = _attn_bwd(cmp_, nsa_q, cmp_k, cmp_v, None, None, _Src(o_cmp, HEAD_V), l_cmp,
                                        lr_cmp, _Src(do_cmp, HEAD_V), None, "cmp_bwd")
    dq_n, (dk_s,), dv_s = _attn_bwd(slc, nsa_q, slc_k, slc_v, sel, selt, _Src(o_slc, HEAD_V), l_slc, lr_slc,
                                    _Src(do_slc, HEAD_V), dq_n, "slc_bwd")
    dq_n, (dk_w,), dv_w = _attn_bwd(win, nsa_q, win_k, win_v, None, None, _Src(o_win, HEAD_V), l_win, lr_win,
                                    _Src(do_win, HEAD_V), dq_n, "win_bwd")
    dak, dpk_lo, dpk_hi, dw1k_lo, dw1k_hi, g_w2k = _compress_bwd(
        *ck_args, pre_k, _shift_down(pre_k), dk_cmp, _shift_down(dk_cmp), "compress_k_bwd")
    dav, dpv_lo, dpv_hi, dw1v_lo, dw1v_hi, g_w2v = _compress_bwd(
        *cv_args, pre_v, _shift_down(pre_v), dv_cmp, _shift_down(dv_cmp), "compress_v_bwd")
    g_pe_k = jnp.concatenate([dpk_lo.reshape(CMP_STRIDE, NSA_DK), dpk_hi.reshape(CMP_STRIDE, NSA_DK)], axis=0)
    g_pe_v = jnp.concatenate([dpv_lo.reshape(CMP_STRIDE, HEAD_V), dpv_hi.reshape(CMP_STRIDE, HEAD_V)], axis=0)
    g_w1k = jnp.concatenate([dw1k_lo, dw1k_hi], axis=0)[:, :NSA_DK]
    g_w1v = jnp.concatenate([dw1v_lo, dw1v_hi], axis=0)
    dk_c = _pad_cols(dak.reshape(s, NSA_DK), 256)
    dv_c = dav.reshape(s, HEAD_V)

    dq_m, (dk_nope, dk_pe), dv_m = _attn_bwd(mla, mla_q, mla_k, mla_v, None, None, _Src(o_mla, HEAD_V), l_mla,
                                             lr_mla, _Src(do_cat, HEAD_V), None, "mla_bwd")
    dq_lin = _rope_bwd_q(dq_m, cs, sn)
    dkv_lin, d_krope = _rope_bwd_k(dk_nope, dk_pe, dv_m, cs, sn)
    g_w_uq_p = _mm(cqn, dq_lin, "mla_q_dw", mode="tn")
    dcqn = _mm(dq_lin, w_uq_p, "mla_q_dx", mode="nt")
    g_w_ukv_p = _mm(ckvn, dkv_lin, "mla_kv_dw", mode="tn")
    dckvn = _mm(dkv_lin, w_ukv_p, "mla_kv_dx", mode="nt")
    dc_q, g_q_norm = _rms_bwd(c_q, w["q_norm_g"], rstd_q, dcqn, None, "norm_q_bwd")
    dc_kv, g_kv_norm = _rms_bwd(c_kv, w["kv_norm_g"], rstd_kv, dckvn, None, "norm_kv_bwd")
    g_w_uq = g_w_uq_p.reshape(512, MLA_HEADS, 256)[:, :, :192].reshape(512, MLA_HEADS * 192)
    g_w_ukv = t_(g_w_ukv_p.reshape(512, 2, MLA_HEADS, 128), (0, 2, 1, 3)).reshape(512, 2048)

    dhp = jnp.concatenate(
        [dc_q, dc_kv, dq_n, dk_c, dk_s, dk_w, d_krope, dv_c, dv_s, dv_w, dgl,
         jnp.zeros((s, PAD["q_mem"] - (PAD["g_nsa"] + LANE)), F32), dq_mem, dz], axis=1)
    g_w_in_t = _w_in_grad_slots(_mm(dhp, xn, "in_proj_dw", mode="tn"))
    dxn = _mm(dhp, w_in_p, "in_proj_dx")
    grad_x, g_norm = _rms_bwd(_Src(x, D_MODEL), w["norm_g"], rstd_x, dxn, dy, "norm_x_bwd")

    grads = dict(norm_g=g_norm, w_in_t=g_w_in_t, q_norm_g=g_q_norm, w_uq=g_w_uq, kv_norm_g=g_kv_norm,
                 w_ukv=g_w_ukv, cmp_pe_k=g_pe_k, cmp_pe_v=g_pe_v, cmp_w1k=g_w1k, cmp_w2k=g_w2k[:NSA_DK, :NSA_DK],
                 cmp_w1v=g_w1v, cmp_w2v=g_w2v, mem_norm_g=g_mem_norm, w_mem_kv=g_w_mem_kv, w_out=g_w_out,
                 final_norm_g=g_final.reshape(-1))
    return loss[0, 0], grad_x, grads


def kernel(x, mem, norm_g, w_in, q_norm_g, w_uq, kv_norm_g, w_ukv, cmp_pe_k, cmp_pe_v, cmp_w1k, cmp_w2k, cmp_w1v, cmp_w2v, mem_norm_g, w_mem_kv, w_out, final_norm_g, loss_target, m_norm_g, m_w_in, m_q_norm_g, m_w_uq, m_kv_norm_g, m_w_ukv, m_cmp_pe_k, m_cmp_pe_v, m_cmp_w1k, m_cmp_w2k, m_cmp_w1v, m_cmp_w2v, m_mem_norm_g, m_w_mem_kv, m_w_out, m_final_norm_g, v_norm_g, v_w_in, v_q_norm_g, v_w_uq, v_kv_norm_g, v_w_ukv, v_cmp_pe_k, v_cmp_pe_v, v_cmp_w1k, v_cmp_w2k, v_cmp_w1v, v_cmp_w2v, v_mem_norm_g, v_w_mem_kv, v_w_out, v_final_norm_g):
    args = dict(locals())
    wts = {n: args[n] for n in WEIGHTS}
    loc = {n: (a if n == "final_norm_g" else a[0]) for n, a in wts.items()}

    def to_x(n, a):
        return a.T if n == "w_in" else a

    split = [1 if n == "w_in" else 0 for n in SHARDED]

    own = [to_x(n, loc[n]).astype(BF16) for n in SHARDED]
    chip = 2 * lax.axis_index("x") + lax.axis_index("y")
    gathered = [lax.dynamic_update_slice(gw, a[None], (chip, 0, 0))
                for gw, a in zip(_gather_shards(own, split), own)]
    full = {n: loc[n].reshape(1, -1) if loc[n].ndim == 1 else loc[n] for n in REPLICATED}
    for n, gw in zip(SHARDED, gathered):
        if n == "w_in":
            full["w_in_t"] = gw
        elif SHARD_AXIS[n] == 0:
            full[n] = gw.reshape(4 * gw.shape[1], gw.shape[2])
        else:
            full[n] = jnp.concatenate([gw[j] for j in range(4)], axis=1)

    loss, grad_x, g = _local_step(x[0], mem[0], loss_target[0], full)
    loss = lax.psum(loss, ("x", "y", "c"))

    def slots(n):
        if n == "w_in":
            return g["w_in_t"]
        a = g[n]
        if SHARD_AXIS[n] == 0:
            return a.reshape(4, a.shape[0] // 4, a.shape[1])
        width = a.shape[1] // 4
        return jnp.stack([a[:, j * width:(j + 1) * width] for j in range(4)])

    gs = [slots(n) for n in SHARDED]
    core = lax.axis_index("c").astype(jnp.int32).reshape(1)
    theirs = _pair_exchange(gs, split)
    pairs = [_pair_sum(a, b, core, ax, "pair_sum_" + n) for n, a, b, ax in zip(SHARDED, gs, theirs, split)]
    from_chips = _chip_exchange(pairs)
    mine = [_sum_slots(b, "chip_sum_" + n) for n, b in zip(SHARDED, from_chips)]
    g_sh = [lax.dynamic_update_slice(o, t, (core[0] * t.shape[0], 0) if ax == 0 else (0, core[0] * t.shape[1]))
            for o, t, ax in zip(_half_exchange(mine, split), mine, split)]

    n_rep = sum(int(np.prod(loc[n].shape)) for n in REPLICATED)
    rows_rep = -(-n_rep // (8 * LANE)) * 8

    def rep_pack(parts):
        flat = jnp.concatenate([p.reshape(-1) for p in parts])
        return jnp.pad(flat, (0, rows_rep * LANE - n_rep)).reshape(rows_rep, LANE)

    g_rep = _sum_slots(_gather_all(rep_pack([g[n] for n in REPLICATED])), "replica_sum")
    d_rp, m_rp, v_rp = _adamw(rep_pack([wts[n] for n in REPLICATED]), g_rep,
                              rep_pack([args["m_" + n] for n in REPLICATED]),
                              rep_pack([args["v_" + n] for n in REPLICATED]), "adamw_replicated")

    def rep_unpack(buf):
        flat, out, o = buf.reshape(-1), {}, 0
        for n in REPLICATED:
            size = int(np.prod(wts[n].shape))
            out[n] = flat[o:o + size].reshape(wts[n].shape)
            o += size
        return out

    outs = {k: rep_unpack(b) for k, b in (("g", g_rep), ("d", d_rp), ("m", m_rp), ("v", v_rp))}
    for n, gn in zip(SHARDED, g_sh):
        d, mo, vo = _adamw(to_x(n, loc[n]), gn, to_x(n, args["m_" + n][0]), to_x(n, args["v_" + n][0]),
                           "adamw_" + n)
        for k, a in (("g", gn), ("d", d), ("m", mo), ("v", vo)):
            outs[k][n] = to_x(n, a).reshape(wts[n].shape)

    return (loss, grad_x[None], *[outs["g"][n] for n in WEIGHTS], *[outs["d"][n] for n in WEIGHTS],
            *[outs["m"][n] for n in WEIGHTS], *[outs["v"][n] for n in WEIGHTS])
```

```python
from typing import NamedTuple

import numpy as np
import jax
import jax.numpy as jnp
from jax import lax
from jax.experimental import pallas as pl
from jax.experimental.pallas import tpu as pltpu

F32 = jnp.float32
BF16 = jnp.bfloat16
MESH = pl.DeviceIdType.MESH

D_MODEL = 2048
EPS = 1e-6
LANE = 128
HEAD_V = 128
MLA_HEADS = 8
NSA_HEADS = 4
MEM_HEADS = 4
NSA_DK = 192
CMP_STRIDE = 16
CMP_LEN = 32
SLC_LEN = 64
SLC_TOPN = 16
WIN = 512
NEG = -1e30
LOG2E = 1.4426950408889634
ROPE_THETA = 10000.0
BLOCK_BYTES = 2 << 20

ORIG = dict(c_q=(0, 512), c_kv=(512, 512), k_rope=(1024, 64), z_mla=(1088, 1024),
            q_nsa=(2112, 768), k_c=(2880, 192), v_c=(3072, 128), k_s=(3200, 192),
            v_s=(3392, 128), k_w=(3520, 192), v_w=(3712, 128), g_nsa=(3840, 12),
            z_nsa=(3852, 512), q_mem=(4364, 512), z_mem=(4876, 512))
PAD = dict(c_q=0, c_kv=512, q_nsa=1024, k_c=2048, k_s=2304, k_w=2560, k_rope=2816, v_c=2944,
           v_s=3072, v_w=3200, g_nsa=3328, q_mem=3584, z=4096)
D_PAD = 6144

ADAM_LR, ADAM_B1, ADAM_B2, ADAM_EPS, ADAM_WD, ADAM_STEP = 0.001, 0.9, 0.999, 1e-08, 0.01, 10

SHARDED = ("w_in", "w_uq", "w_ukv", "cmp_w1k", "cmp_w1v", "w_mem_kv", "w_out")
SHARD_AXIS = dict(w_in=1, w_uq=1, w_ukv=1, cmp_w1k=0, cmp_w1v=0, w_mem_kv=0, w_out=0)
REPLICATED = ("norm_g", "q_norm_g", "kv_norm_g", "cmp_pe_k", "cmp_pe_v", "cmp_w2k", "cmp_w2v",
              "mem_norm_g", "final_norm_g")
WEIGHTS = ("norm_g", "w_in", "q_norm_g", "w_uq", "kv_norm_g", "w_ukv", "cmp_pe_k", "cmp_pe_v",
           "cmp_w1k", "cmp_w2k", "cmp_w1v", "cmp_w2v", "mem_norm_g", "w_mem_kv", "w_out",
           "final_norm_g")


def _pcall(kernel, **kw):
    return pl.pallas_call(kernel, **kw)


def _tile(n, pref):
    if n <= pref:
        return n
    for t in range(pref, LANE - 1, -LANE):
        if n % t == 0:
            return t
    raise ValueError((n, pref))


def _row_tile(rows, cols, itemsize=4):
    want = max(16, BLOCK_BYTES // (cols * itemsize))
    if rows <= want:
        return rows
    t = 16
    best = rows
    while t <= want:
        if rows % t == 0:
            best = t
        t *= 2
    return best


def _nt(a, b):
    return lax.dot_general(a, b, (((1,), (1,)), ((), ())), preferred_element_type=F32)


def _tn(a, b):
    return lax.dot_general(a, b, (((0,), (0,)), ((), ())), preferred_element_type=F32)


def _nn(a, b):
    return jnp.dot(a, b, preferred_element_type=F32)


def _sigmoid(x):
    return 1.0 / (1.0 + jnp.exp(-x))


class _Src(NamedTuple):
    arr: jax.Array
    width: int
    col0: int = 0
    per_head: bool = True

    def col(self, h):
        return self.col0 + h if self.per_head else self.col0


def _mm(a, b, name, mode="nn", out_dtype=F32, second_dtype=None, wide=1024):
    if mode == "tn":
        k, m = a.shape
    else:
        m, k = a.shape
    if mode == "nt":
        n, k2 = b.shape
    else:
        k2, n = b.shape
    assert k == k2, (a.shape, b.shape, mode)
    tm, tn, tk = _tile(m, 1024), _tile(n, wide), _tile(k, 2048)
    nk = k // tk
    assert nk == 1 or (out_dtype == F32 and second_dtype is None)
    dot = {"nn": _nn, "nt": _nt, "tn": _tn}[mode]

    def kern(a_ref, b_ref, o_ref, *more):
        r = dot(a_ref[...].astype(BF16), b_ref[...].astype(BF16))
        if nk == 1:
            o_ref[...] = r.astype(out_dtype)
            if more:
                more[0][...] = r.astype(second_dtype)
        else:
            kk = pl.program_id(2)

            @pl.when(kk == 0)
            def _():
                o_ref[...] = r

            @pl.when(kk > 0)
            def _():
                o_ref[...] += r

    a_spec = (pl.BlockSpec((tk, tm), lambda i, j, kk: (kk, i)) if mode == "tn"
              else pl.BlockSpec((tm, tk), lambda i, j, kk: (i, kk)))
    b_spec = (pl.BlockSpec((tn, tk), lambda i, j, kk: (j, kk)) if mode == "nt"
              else pl.BlockSpec((tk, tn), lambda i, j, kk: (kk, j)))
    o_spec = pl.BlockSpec((tm, tn), lambda i, j, kk: (i, j))
    out_shape = jax.ShapeDtypeStruct((m, n), out_dtype)
    if second_dtype is not None:
        o_spec = [o_spec, o_spec]
        out_shape = [out_shape, jax.ShapeDtypeStruct((m, n), second_dtype)]
    return _pcall(
        kern, name=name, grid=(m // tm, n // tn, nk), in_specs=[a_spec, b_spec], out_specs=o_spec,
        out_shape=out_shape,
        compiler_params=pltpu.CompilerParams(dimension_semantics=("parallel", "parallel", "arbitrary")),
    )(a, b)


def _rms_fwd(x, g, name):
    r, d = x.arr.shape[0], x.width
    tr = _tile(r, 512)

    def kern(x_ref, g_ref, y_ref, r_ref):
        xv = x_ref[...]
        rstd = lax.rsqrt(jnp.mean(xv * xv, axis=-1, keepdims=True) + EPS)
        y_ref[...] = (xv * rstd * g_ref[...]).astype(BF16)
        r_ref[...] = rstd

    return _pcall(
        kern, name=name, grid=(r // tr,),
        in_specs=[pl.BlockSpec((tr, d), lambda i: (i, x.col0)), pl.BlockSpec((1, d), lambda i: (0, 0))],
        out_specs=[pl.BlockSpec((tr, d), lambda i: (i, 0)), pl.BlockSpec((tr, 1), lambda i: (i, 0))],
        out_shape=[jax.ShapeDtypeStruct((r, d), BF16), jax.ShapeDtypeStruct((r, 1), F32)],
    )(x.arr, g)


def _rms_bwd(x, g, rstd, dy, add, name):
    r, d = x.arr.shape[0], x.width
    tr = _tile(r, 256)
    has_add = add is not None

    def kern(*refs):
        if has_add:
            x_ref, g_ref, r_ref, dy_ref, add_ref, dx_ref, dg_ref = refs
        else:
            x_ref, g_ref, r_ref, dy_ref, dx_ref, dg_ref = refs
        rs = r_ref[...]
        xhat = x_ref[...] * rs
        dyv = dy_ref[...]
        dyg = dyv * g_ref[...]
        c = jnp.mean(dyg * xhat, axis=-1, keepdims=True)
        dx = rs * (dyg - xhat * c)
        if has_add:
            dx = dx + add_ref[...]
        dx_ref[...] = dx
        part = jnp.sum(dyv * xhat, axis=0, keepdims=True)

        @pl.when(pl.program_id(0) == 0)
        def _():
            dg_ref[...] = part

        @pl.when(pl.program_id(0) > 0)
        def _():
            dg_ref[...] += part

    row = pl.BlockSpec((tr, d), lambda i: (i, 0))
    vec = pl.BlockSpec((1, d), lambda i: (0, 0))
    ins = [pl.BlockSpec((tr, d), lambda i: (i, x.col0)), vec, pl.BlockSpec((tr, 1), lambda i: (i, 0)), row]
    ins += [row] if has_add else []
    args = (x.arr, g, rstd, dy) + ((add,) if has_add else ())
    return _pcall(
        kern, name=name, grid=(r // tr,), in_specs=ins, out_specs=[row, vec],
        out_shape=[jax.ShapeDtypeStruct((r, d), F32), jax.ShapeDtypeStruct((1, d), F32)],
        compiler_params=pltpu.CompilerParams(dimension_semantics=("arbitrary",)),
    )(*args)


def _final_loss(x, proj, g, target):
    r, d = x.shape
    tr = _tile(r, 256)

    def kern(x_ref, p_ref, g_ref, t_ref, dy_ref, dg_ref, loss_ref):
        y = x_ref[...] + p_ref[...]
        rs = lax.rsqrt(jnp.mean(y * y, axis=-1, keepdims=True) + EPS)
        yhat = y * rs
        gv = g_ref[...]
        e = yhat * gv - t_ref[...]
        lpart = 0.5 * jnp.sum(jnp.mean(e * e, axis=-1, keepdims=True), axis=0, keepdims=True)
        dout = e * (1.0 / d)
        dyg = dout * gv
        c = jnp.mean(dyg * yhat, axis=-1, keepdims=True)
        dy_ref[...] = rs * (dyg - yhat * c)
        gpart = jnp.sum(dout * yhat, axis=0, keepdims=True)
        lrow = jnp.broadcast_to(lpart, (1, LANE))

        @pl.when(pl.program_id(0) == 0)
        def _():
            dg_ref[...] = gpart
            loss_ref[...] = lrow

        @pl.when(pl.program_id(0) > 0)
        def _():
            dg_ref[...] += gpart
            loss_ref[...] += lrow

    row = pl.BlockSpec((tr, d), lambda i: (i, 0))
    vec = pl.BlockSpec((1, d), lambda i: (0, 0))
    return _pcall(
        kern, name="final_loss", grid=(r // tr,), in_specs=[row, row, vec, row],
        out_specs=[row, vec, pl.BlockSpec((1, LANE), lambda i: (0, 0))],
        out_shape=[jax.ShapeDtypeStruct((r, d), F32), jax.ShapeDtypeStruct((1, d), F32),
                   jax.ShapeDtypeStruct((1, LANE), F32)],
        compiler_params=pltpu.CompilerParams(dimension_semantics=("arbitrary",)),
    )(x, proj, g, target)


def _rope_fwd(x, cs, sn, nh, width, off, name):
    s = x.arr.shape[0]
    tr = _tile(s, 512)

    def kern(x_ref, c_ref, s_ref, o_ref):
        cv, sv = c_ref[...], s_ref[...]
        for h in range(nh):
            b = h * width
            if off:
                o_ref[:, b:b + off] = x_ref[:, b:b + off].astype(BF16)
            xr = x_ref[:, b + off:b + off + LANE]
            o_ref[:, b + off:b + off + LANE] = (xr * cv + pltpu.roll(xr, 32, 1) * sv).astype(BF16)

    tab = pl.BlockSpec((tr, LANE), lambda i: (i, 0))
    return _pcall(
        kern, name=name, grid=(s // tr,),
        in_specs=[pl.BlockSpec((tr, nh * width), lambda i: (i, x.col0)), tab, tab],
        out_specs=pl.BlockSpec((tr, nh * width), lambda i: (i, 0)),
        out_shape=jax.ShapeDtypeStruct((s, nh * width), BF16),
    )(x.arr, cs, sn)


def _rope_grad(d, cv, sv):
    g2 = d * sv
    g2 = g2 + pltpu.roll(g2, 64, 1)
    lane = lax.broadcasted_iota(jnp.int32, d.shape, 1)
    return jnp.where(lane < 64, d * cv + pltpu.roll(g2, 32, 1), 0.0)


def _rope_bwd_q(dq, cs, sn):
    s, w = dq.shape
    tr = _tile(s, 512)
    nh = w // 256

    def kern(d_ref, c_ref, s_ref, o_ref):
        cv, sv = c_ref[...], s_ref[...]
        for h in range(nh):
            b = h * 256
            o_ref[:, b:b + LANE] = d_ref[:, b:b + LANE]
            o_ref[:, b + LANE:b + 256] = _rope_grad(d_ref[:, b + LANE:b + 256], cv, sv)

    row = pl.BlockSpec((tr, w), lambda i: (i, 0))
    tab = pl.BlockSpec((tr, LANE), lambda i: (i, 0))
    return _pcall(kern, name="rope_bwd_q", grid=(s // tr,), in_specs=[row, tab, tab], out_specs=row,
                  out_shape=jax.ShapeDtypeStruct((s, w), F32))(dq, cs, sn)


def _rope_bwd_k(dk_nope, dk_pe, dv, cs, sn):
    s, w = dk_nope.shape
    tr = _tile(s, 512)

    def kern(dk_ref, dp_ref, dv_ref, c_ref, s_ref, okv_ref, okr_ref):
        okv_ref[:, :w] = dk_ref[...]
        okv_ref[:, w:] = dv_ref[...]
        okr_ref[...] = _rope_grad(dp_ref[...], c_ref[...], s_ref[...])

    tab = pl.BlockSpec((tr, LANE), lambda i: (i, 0))
    wide = pl.BlockSpec((tr, w), lambda i: (i, 0))
    return _pcall(
        kern, name="rope_bwd_k", grid=(s // tr,), in_specs=[wide, tab, wide, tab, tab],
        out_specs=[pl.BlockSpec((tr, 2 * w), lambda i: (i, 0)), tab],
        out_shape=[jax.ShapeDtypeStruct((s, 2 * w), F32), jax.ShapeDtypeStruct((s, LANE), F32)],
    )(dk_nope, dk_pe, dv, cs, sn)


class _Attn:
    def __init__(self, mode, s, sk, heads, dk):
        self.mode, self.s, self.sk, self.h, self.dk = mode, s, sk, heads, dk
        self.scale = {"mla": 192 ** -0.5, "mem": 128 ** -0.5}.get(mode, NSA_DK ** -0.5)
        self.tb = min(256, s)
        self.nb = s // self.tb
        self.nsub = 2 if self.nb % 2 == 0 else 1
        self.tq = self.tb * self.nsub
        self.nq = s // self.tq
        self.causal = mode in ("mla", "slc")
        if self.causal:
            self.tk = self.tq
        elif mode == "win":
            self.tk = WIN + self.tb
        else:
            self.tk = sk
        self.tkb = min(512, sk)
        self.ksub = 2 if self.tkb == 512 and mode == "mla" else 1
        self.kb = self.tkb // self.ksub
        self.ncmp = s // CMP_STRIDE - 1

    def mask_bias(self, t, n, h, selx, diag):
        m = self.mode
        if m == "mla":
            return (n <= t) if diag else None, None
        if m == "mem":
            return None, None
        slope = jnp.where(h == 0, 0.25, jnp.where(h == 1, 0.0625, jnp.where(h == 2, 0.015625, 0.00390625)))
        slope = slope.astype(F32) * LOG2E
        if m == "cmp":
            mask = (n * CMP_STRIDE + (CMP_LEN - 1) <= t) & (n < self.ncmp)
            pos = n.astype(F32) * float(CMP_STRIDE) + (CMP_LEN - 1) / 2.0
            return mask, slope * pos
        rel = t - n
        if m == "slc":
            return (rel >= 0) if diag else None, slope * n.astype(F32)
        return (rel >= 0) & (rel < WIN), slope * n.astype(F32)


def _scores(cfg, s_raw, t, n, h, selx, diag, lse=None):
    s = s_raw * (cfg.scale * LOG2E)
    mask, key_term = cfg.mask_bias(t, n, h, selx, diag)
    if key_term is not None:
        s = s + key_term
    if selx is not None:
        s = s + selx
    if lse is None:
        if mask is not None:
            s = jnp.where(mask, s, NEG)
        return s, mask
    p = jnp.exp2(jnp.minimum(s - lse, 0.0))
    if mask is not None:
        p = jnp.where(mask, p, 0.0)
    return p, mask


def _block_of_key(k0, tk, keys_on_rows, value=NEG):
    shape = (tk, LANE) if keys_on_rows else (LANE, tk)
    n = lax.broadcasted_iota(jnp.int32, shape, 0 if keys_on_rows else 1) + k0
    j = lax.broadcasted_iota(jnp.int32, shape, 1 if keys_on_rows else 0)
    return jnp.where((n >> 6) == j, value, 0.0).astype(BF16)


def _to_row(col):
    t = col.shape[0]
    return jnp.transpose(jnp.broadcast_to(col, (t, LANE)))[0:1, :]


def _load_keys(k_refs, rows):
    parts = [r[rows, :].astype(BF16) for r in k_refs]
    return parts[0] if len(parts) == 1 else jnp.concatenate(parts, axis=1)


def _attn_fwd(cfg, q, ks, v, sel, name):
    s, tq, tk, tb, nsub = cfg.s, cfg.tq, cfg.tk, cfg.tb, cfg.nsub
    has_sel = sel is not None
    nkp = len(ks)

    def kern(*refs):
        q_ref, k_refs, v_ref = refs[0], refs[1:1 + nkp], refs[1 + nkp]
        sel_ref = refs[2 + nkp] if has_sel else None
        o_ref, lc_ref, lr_ref = refs[2 + nkp + has_sel:5 + nkp + has_sel]
        h, i = pl.program_id(0), pl.program_id(1)
        part = [slice(r * tb, (r + 1) * tb) for r in range(nsub)]
        qs = [q_ref[p, :].astype(BF16) for p in part]
        ts = [i * tq + r * tb + lax.broadcasted_iota(jnp.int32, (tb, 1), 0) for r in range(nsub)]
        sels = [sel_ref[p, :].astype(BF16) for p in part] if has_sel else None

        def load(k0):
            rows = pl.ds(k0, tk)
            return _load_keys(k_refs, rows), v_ref[rows, :].astype(BF16)

        def soft(r, k0, s_raw, vv, emat, carry, diag):
            m, l, acc = carry
            n = k0 + lax.broadcasted_iota(jnp.int32, (1, tk), 1)
            selx = _nn(sels[r], emat) if has_sel else None
            sc, mask = _scores(cfg, s_raw, ts[r], n, h, selx, diag)
            m_new = jnp.maximum(m, jnp.max(sc, axis=1, keepdims=True))
            alpha = jnp.exp2(m - m_new)
            p = jnp.exp2(sc - m_new)
            if mask is not None:
                p = jnp.where(mask, p, 0.0)
            l = alpha * l + jnp.sum(p, axis=1, keepdims=True)
            acc = alpha * acc + _nn(p.astype(BF16), vv)
            return m_new, l, acc

        def step(r, k0, kk, vv, emat, carry, diag):
            return soft(r, k0, _nt(qs[r], kk), vv, emat, carry, diag)

        def chunk(k0, carry, diag):
            kk, vv = load(k0)
            emat = _block_of_key(k0, tk, False) if has_sel else None
            return tuple(step(r, k0, kk, vv, emat, carry[r], diag) for r in range(nsub))

        init = (jnp.full((tb, 1), NEG, F32), jnp.zeros((tb, 1), F32), jnp.zeros((tb, HEAD_V), F32))
        carry = (init,) * nsub
        if cfg.causal:
            buf_a, buf_b = refs[-2:]
            full = (i * tq) // tk

            def scores_into(buf, c):
                kk = _load_keys(k_refs, pl.ds(pl.multiple_of(c * tk, tk), tk))
                for r in range(nsub):
                    buf[r] = _nt(qs[r], kk)

            def consume(buf, c, cr, diag):
                k0 = pl.multiple_of(c * tk, tk)
                vv = v_ref[pl.ds(k0, tk), :].astype(BF16)
                emat = _block_of_key(k0, tk, False) if has_sel else None
                return tuple(soft(r, k0, buf[r], vv, emat, cr[r], diag) for r in range(nsub))

            def pair(p, cr):
                scores_into(buf_b, 2 * p + 1)
                cr = consume(buf_a, 2 * p, cr, False)
                scores_into(buf_a, 2 * p + 2)
                return consume(buf_b, 2 * p + 1, cr, False)

            def odd_tail(cr):
                scores_into(buf_b, full)
                cr = consume(buf_a, full - 1, cr, False)
                return consume(buf_b, full, cr, True)

            scores_into(buf_a, 0)
            carry = lax.fori_loop(0, full // 2, pair, carry)
            carry = lax.cond(full % 2 == 1, odd_tail, lambda cr: consume(buf_a, full, cr, True), carry)
        elif cfg.mode == "win":
            starts = [pl.multiple_of(jnp.maximum(i * tq + r * tb - WIN, 0), tb) for r in range(nsub)]
            carry = tuple(step(r, k0, *load(k0), None, carry[r], True) for r, k0 in enumerate(starts))
        else:
            carry = chunk(0, carry, True)
        for r, (m, l, acc) in enumerate(carry):
            o_ref[part[r], :] = acc / (l + 1e-20)
            lse = m + jnp.log(l + 1e-20) * LOG2E
            lc_ref[0, part[r], :] = lse
            lr_ref[0, r] = _to_row(lse)

    ins = [pl.BlockSpec((tq, q.width), lambda h, i: (i, q.col(h)))]
    ins += [pl.BlockSpec((cfg.sk, p.width), lambda h, i, p=p: (0, p.col(h))) for p in ks]
    ins += [pl.BlockSpec((cfg.sk, HEAD_V), lambda h, i: (0, v.col(h)))]
    args = [q.arr] + [p.arr for p in ks] + [v.arr]
    if has_sel:
        ins.append(pl.BlockSpec((tq, LANE), lambda h, i: (i, 0)))
        args.append(sel)
    return _pcall(
        kern, name=name, grid=(cfg.h, cfg.nq), in_specs=ins,
        out_specs=[pl.BlockSpec((tq, HEAD_V), lambda h, i: (i, h)),
                   pl.BlockSpec((1, tq, 1), lambda h, i: (h, i, 0)),
                   pl.BlockSpec((1, nsub, 1, tb), lambda h, i: (h, i, 0, 0))],
        out_shape=[jax.ShapeDtypeStruct((s, cfg.h * HEAD_V), F32),
                   jax.ShapeDtypeStruct((cfg.h, s, 1), F32),
                   jax.ShapeDtypeStruct((cfg.h, cfg.nb, 1, tb), F32)],
        scratch_shapes=[pltpu.VMEM((nsub, tb, tk), F32)] * 2 if cfg.causal else [],
        compiler_params=pltpu.CompilerParams(dimension_semantics=("parallel", "parallel")),
    )(*args)


def _attn_dq(cfg, q, ks, v, sel, o, lse, do, dq_in, name):
    s, tq, tk, dk, tb, nsub = cfg.s, cfg.tq, cfg.tk, cfg.dk, cfg.tb, cfg.nsub
    has_sel = sel is not None
    has_in = dq_in is not None
    nkp = len(ks)

    def kern(*refs):
        refs = list(refs)
        q_ref, k_refs, v_ref = refs[0], refs[1:1 + nkp], refs[1 + nkp]
        p0 = 2 + nkp
        sel_ref = refs[p0] if has_sel else None
        p0 += has_sel
        o_ref, l_ref, do_ref = refs[p0:p0 + 3]
        p0 += 3
        in_ref = refs[p0] if has_in else None
        p0 += has_in
        dq_ref, dr_ref = refs[p0:p0 + 2]
        h, i = pl.program_id(0), pl.program_id(1)
        part = [slice(r * tb, (r + 1) * tb) for r in range(nsub)]
        qs = [q_ref[p, :].astype(BF16) for p in part]
        ts = [i * tq + r * tb + lax.broadcasted_iota(jnp.int32, (tb, 1), 0) for r in range(nsub)]
        sels = [sel_ref[p, :].astype(BF16) for p in part] if has_sel else None
        dvecs, dobs, lses = [], [], []
        for r, p in enumerate(part):
            dov = do_ref[p, :]
            dvec = jnp.sum(dov * o_ref[p, :], axis=1, keepdims=True)
            dr_ref[0, r] = _to_row(dvec)
            dvecs.append(dvec)
            dobs.append(dov.astype(BF16))
            lses.append(l_ref[0, p, :])

        def load(k0):
            rows = pl.ds(k0, tk)
            return _load_keys(k_refs, rows), v_ref[rows, :].astype(BF16)

        def grad(r, k0, s_raw, dp, kk, emat, acc, diag):
            n = k0 + lax.broadcasted_iota(jnp.int32, (1, tk), 1)
            selx = _nn(sels[r], emat) if has_sel else None
            p, _ = _scores(cfg, s_raw, ts[r], n, h, selx, diag, lses[r])
            ds = p * (dp - dvecs[r])
            return acc + _nn(ds.astype(BF16), kk)

        def step(r, k0, kk, vv, emat, acc, diag):
            return grad(r, k0, _nt(qs[r], kk), _nt(dobs[r], vv), kk, emat, acc, diag)

        def chunk(k0, accs, diag):
            kk, vv = load(k0)
            emat = _block_of_key(k0, tk, False) if has_sel else None
            return tuple(step(r, k0, kk, vv, emat, accs[r], diag) for r in range(nsub))

        accs = (jnp.zeros((tb, dk), F32),) * nsub
        if cfg.causal:
            sa, pa, sb, pb = refs[-4:]
            full = (i * tq) // tk

            def products_into(sbuf, pbuf, c):
                kk, vv = load(pl.multiple_of(c * tk, tk))
                for r in range(nsub):
                    sbuf[r] = _nt(qs[r], kk)
                    pbuf[r] = _nt(dobs[r], vv)

            def consume(sbuf, pbuf, c, ac, diag):
                k0 = pl.multiple_of(c * tk, tk)
                kk = _load_keys(k_refs, pl.ds(k0, tk))
                emat = _block_of_key(k0, tk, False) if has_sel else None
                return tuple(grad(r, k0, sbuf[r], pbuf[r], kk, emat, ac[r], diag) for r in range(nsub))

            def pair(p, ac):
                products_into(sb, pb, 2 * p + 1)
                ac = consume(sa, pa, 2 * p, ac, False)
                products_into(sa, pa, 2 * p + 2)
                return consume(sb, pb, 2 * p + 1, ac, False)

            def odd_tail(ac):
                products_into(sb, pb, full)
                ac = consume(sa, pa, full - 1, ac, False)
                return consume(sb, pb, full, ac, True)

            products_into(sa, pa, 0)
            accs = lax.fori_loop(0, full // 2, pair, accs)
            accs = lax.cond(full % 2 == 1, odd_tail, lambda ac: consume(sa, pa, full, ac, True), accs)
        elif cfg.mode == "win":
            starts = [pl.multiple_of(jnp.maximum(i * tq + r * tb - WIN, 0), tb) for r in range(nsub)]
            accs = tuple(step(r, k0, *load(k0), None, accs[r], True) for r, k0 in enumerate(starts))
        else:
            accs = chunk(0, accs, True)
        for r, p in enumerate(part):
            dq_ref[p, :] = accs[r] * cfg.scale + in_ref[p, :] if has_in else accs[r] * cfg.scale

    qs = pl.BlockSpec((tq, dk), lambda h, i: (i, h))
    ins = [pl.BlockSpec((tq, q.width), lambda h, i: (i, q.col(h)))]
    ins += [pl.BlockSpec((cfg.sk, p.width), lambda h, i, p=p: (0, p.col(h))) for p in ks]
    ins += [pl.BlockSpec((cfg.sk, HEAD_V), lambda h, i: (0, v.col(h)))]
    args = [q.arr] + [p.arr for p in ks] + [v.arr]
    if has_sel:
        ins.append(pl.BlockSpec((tq, LANE), lambda h, i: (i, 0)))
        args.append(sel)
    ins += [pl.BlockSpec((tq, HEAD_V), lambda h, i: (i, o.col(h))),
            pl.BlockSpec((1, tq, 1), lambda h, i: (h, i, 0)),
            pl.BlockSpec((tq, HEAD_V), lambda h, i: (i, do.col(h)))]
    args += [o.arr, lse, do.arr]
    if has_in:
        ins.append(qs)
        args.append(dq_in)
    return _pcall(
        kern, name=name, grid=(cfg.h, cfg.nq), in_specs=ins,
        out_specs=[qs, pl.BlockSpec((1, nsub, 1, tb), lambda h, i: (h, i, 0, 0))],
        out_shape=[jax.ShapeDtypeStruct((s, cfg.h * dk), F32),
                   jax.ShapeDtypeStruct((cfg.h, cfg.nb, 1, tb), F32)],
        scratch_shapes=[pltpu.VMEM((nsub, tb, tk), F32)] * 4 if cfg.causal else [],
        compiler_params=pltpu.CompilerParams(dimension_semantics=("parallel", "parallel")),
    )(*args)


def _attn_dkv(cfg, q, ks, v, selt, lse_r, d_r, do, name):
    s, tq, tkb, dk, kb, ksub = cfg.s, cfg.tb, cfg.tkb, cfg.dk, cfg.kb, cfg.ksub
    nq = cfg.nb
    has_sel = selt is not None
    nkp = len(ks)
    outs = list(ks) + [v]

    def kern(*refs):
        k_refs, v_ref = refs[:nkp], refs[nkp]
        q_ref, do_ref, lr_ref, dr_ref = refs[nkp + 1:nkp + 5]
        st_ref = refs[nkp + 5] if has_sel else None
        out_refs = refs[nkp + 5 + has_sel:2 * nkp + 6 + has_sel]
        sa, pa, sb, pb = refs[-4:]
        j, h = pl.program_id(0), pl.program_id(1)
        k0 = j * tkb
        part = [slice(u * kb, (u + 1) * kb) for u in range(ksub)]
        kks = [_load_keys(k_refs, p) for p in part]
        vvs = [v_ref[p, :].astype(BF16) for p in part]
        ns = [k0 + u * kb + lax.broadcasted_iota(jnp.int32, (kb, 1), 0) for u in range(ksub)]
        emats = [_block_of_key(k0 + u * kb, kb, True) for u in range(ksub)] if has_sel else None

        def load_q(i):
            rows = pl.ds(pl.multiple_of(i * tq, tq), tq)
            return q_ref[rows, :].astype(BF16), do_ref[rows, :].astype(BF16)

        def products_into(sbuf, pbuf, i):
            qi, doi = load_q(i)
            for u in range(ksub):
                sbuf[u] = _nt(kks[u], qi)
                pbuf[u] = _nt(vvs[u], doi)

        def consume(sbuf, pbuf, i, carry):
            qi, doi = load_q(i)
            t = i * tq + lax.broadcasted_iota(jnp.int32, (1, tq), 1)
            selt_i = st_ref[i].astype(BF16) if has_sel else None
            new = []
            for u in range(ksub):
                dk_acc, dv_acc = carry[u]
                selx = _nn(emats[u], selt_i) if has_sel else None
                pt, _ = _scores(cfg, sbuf[u], t, ns[u], h, selx, True, lr_ref[0, i])
                dv_acc = dv_acc + _nn(pt.astype(BF16), doi)
                dst = pt * (pbuf[u] - dr_ref[0, i])
                new.append((dk_acc + _nn(dst.astype(BF16), qi), dv_acc))
            return tuple(new)

        if cfg.causal:
            first, count = k0 // tq, nq - k0 // tq
        elif cfg.mode == "win":
            first = k0 // tq
            count = jnp.minimum((k0 + tkb + WIN - 2) // tq + 1, nq) - first
        else:
            first, count = 0, nq

        def pair(p, cr):
            i0 = first + 2 * p
            products_into(sb, pb, i0 + 1)
            cr = consume(sa, pa, i0, cr)
            products_into(sa, pa, i0 + 2)
            return consume(sb, pb, i0 + 1, cr)

        carry = ((jnp.zeros((kb, dk), F32), jnp.zeros((kb, HEAD_V), F32)),) * ksub
        products_into(sa, pa, first)
        carry = lax.fori_loop(0, count // 2 - 1, pair, carry)
        last = first + count - 2
        products_into(sb, pb, last + 1)
        carry = consume(sa, pa, last, carry)
        carry = consume(sb, pb, last + 1, carry)
        for u, (dk_acc, dv_acc) in enumerate(carry):
            vals, off = [], 0
            for p in ks:
                vals.append(dk_acc[:, off:off + p.width] * cfg.scale)
                off += p.width
            vals.append(dv_acc)
            for src, ref, val in zip(outs, out_refs, vals):
                if src.per_head:
                    ref[part[u], :] = val
                else:
                    @pl.when(h == 0)
                    def _(ref=ref, val=val, u=u):
                        ref[part[u], :] = val

                    @pl.when(h > 0)
                    def _(ref=ref, val=val, u=u):
                        ref[part[u], :] += val

    rowv = pl.BlockSpec((1, nq, 1, tq), lambda j, h: (h, 0, 0, 0))
    ins = [pl.BlockSpec((tkb, p.width), lambda j, h, p=p: (j, p.col(h))) for p in ks]
    ins += [pl.BlockSpec((tkb, HEAD_V), lambda j, h: (j, v.col(h))),
            pl.BlockSpec((s, q.width), lambda j, h: (0, q.col(h))),
            pl.BlockSpec((s, HEAD_V), lambda j, h: (0, do.col(h))), rowv, rowv]
    args = [p.arr for p in ks] + [v.arr, q.arr, do.arr, lse_r, d_r]
    if has_sel:
        ins.append(pl.BlockSpec((nq, LANE, tq), lambda j, h: (0, 0, 0)))
        args.append(selt)
    out_specs = [pl.BlockSpec((tkb, p.width), lambda j, h, p=p: (j, h if p.per_head else 0)) for p in outs]
    out_shape = [jax.ShapeDtypeStruct((cfg.sk, (cfg.h if p.per_head else 1) * p.width), F32) for p in outs]
    assert nq % 2 == 0 and (cfg.mode in ("cmp", "mem") or tkb % (2 * tq) == 0), (nq, tkb, tq)
    return _pcall(
        kern, name=name, grid=(cfg.sk // tkb, cfg.h), in_specs=ins, out_specs=out_specs, out_shape=out_shape,
        scratch_shapes=[pltpu.VMEM((ksub, kb, tq), F32)] * 4,
        compiler_params=pltpu.CompilerParams(dimension_semantics=("parallel", "arbitrary")),
    )(*args)


def _all_heads(cfg, src, rows, key=False):
    if src.per_head:
        assert src.col0 % cfg.h == 0
        width, col = cfg.h * src.width, src.col0 // cfg.h
    else:
        width, col = src.width, src.col0
    return pl.BlockSpec((rows, width), (lambda i: (0, col)) if key else (lambda i: (i, col)))


def _head_cols(src, hh):
    return slice(hh * src.width, (hh + 1) * src.width) if src.per_head else slice(None)


def _key_window(cfg, i, r):
    if cfg.mode == "win":
        return pl.ds(pl.multiple_of(jnp.maximum(i * cfg.tq + r * cfg.tb - WIN, 0), cfg.tb), cfg.tk)
    return pl.ds(0, cfg.tk)


def _attn_fwd_small(cfg, q, ks, v, name, overlap=None):
    s, tq, tk, tb, nsub, nh = cfg.s, cfg.tq, cfg.tk, cfg.tb, cfg.nsub, cfg.h
    nkp = len(ks)
    select = overlap is not None
    n_s = s // SLC_LEN
    top_n = min(SLC_TOPN, n_s)

    def kern(*refs):
        q_ref, k_refs, v_ref = refs[0], refs[1:1 + nkp], refs[1 + nkp]
        ov_ref = refs[2 + nkp] if select else None
        o_ref, lc_ref, lr_ref = refs[2 + nkp + select:5 + nkp + select]
        i = pl.program_id(0)
        imps = [jnp.zeros((tb, LANE), F32)] * nsub
        for r in range(nsub):
            rows = slice(r * tb, (r + 1) * tb)
            t = i * tq + r * tb + lax.broadcasted_iota(jnp.int32, (tb, 1), 0)
            win = _key_window(cfg, i, r)
            n = win.start + lax.broadcasted_iota(jnp.int32, (1, tk), 1)
            for hh in range(nh):
                qv = q_ref[rows, hh * cfg.dk:(hh + 1) * cfg.dk].astype(BF16)
                kk = _load_keys([kr.at[:, _head_cols(p, hh)] for kr, p in zip(k_refs, ks)], win)
                vv = v_ref[win, _head_cols(v, hh)].astype(BF16)
                sc, mask = _scores(cfg, _nt(qv, kk), t, n, hh, None, True)
                m = jnp.max(sc, axis=1, keepdims=True)
                e = jnp.exp2(sc - m)
                if mask is not None:
                    e = jnp.where(mask, e, 0.0)
                l = jnp.sum(e, axis=1, keepdims=True)
                o_ref[rows, hh * HEAD_V:(hh + 1) * HEAD_V] = _nn(e.astype(BF16), vv) / (l + 1e-20)
                lse = m + jnp.log(l + 1e-20) * LOG2E
                lc_ref[hh, rows, :] = lse
                lr_ref[hh, r] = _to_row(lse)
                if select:
                    imps[r] = imps[r] + _nn((e / (l + 1e-20)).astype(BF16), ov_ref[...])
        if select:
            sel_ref, selt_ref, imp_t = refs[5 + nkp + select:8 + nkp + select]
            for r in range(nsub):
                t = i * tq + r * tb + lax.broadcasted_iota(jnp.int32, (tb, 1), 0)
                j = lax.broadcasted_iota(jnp.int32, (tb, LANE), 1)
                cur = t >> 6
                imp = jnp.where((j == 0) | (j == cur) | (j == cur - 1), 1e9, imps[r])
                imp = jnp.where(j > cur, -1e9, imp)
                imp_t[r] = jnp.transpose(imp)
                mine = imp_t[r, 0:n_s, :]
                jrow = lax.broadcasted_iota(jnp.int32, (n_s, tb), 0)

                def count(k, rank):
                    other = imp_t[r, pl.ds(k, 1), :]
                    ahead = (other > mine) | ((other == mine) & (k < jrow))
                    return rank + jnp.where(ahead, 1.0, 0.0)

                rank = lax.fori_loop(0, n_s, count, jnp.zeros((n_s, tb), F32))
                cur_t = (i * tq + r * tb + lax.broadcasted_iota(jnp.int32, (1, tb), 1)) >> 6
                rejected = jnp.where((rank < top_n) & (jrow <= cur_t), 0.0, 1.0)
                if n_s < LANE:
                    rejected = jnp.concatenate([rejected, jnp.ones((LANE - n_s, tb), F32)], axis=0)
                selt_ref[r] = rejected
                sel_ref[r * tb:(r + 1) * tb, :] = jnp.transpose(rejected)

    ins = [_all_heads(cfg, q, tq)] + [_all_heads(cfg, p, cfg.sk, True) for p in ks]
    ins += [_all_heads(cfg, v, cfg.sk, True)]
    args = [q.arr] + [p.arr for p in ks] + [v.arr]
    out_specs = [pl.BlockSpec((tq, nh * HEAD_V), lambda i: (i, 0)),
                 pl.BlockSpec((nh, tq, 1), lambda i: (0, i, 0)),
                 pl.BlockSpec((nh, nsub, 1, tb), lambda i: (0, i, 0, 0))]
    out_shape = [jax.ShapeDtypeStruct((s, nh * HEAD_V), F32), jax.ShapeDtypeStruct((nh, s, 1), F32),
                 jax.ShapeDtypeStruct((nh, cfg.nb, 1, tb), F32)]
    scratch = []
    if select:
        ins.append(pl.BlockSpec((cfg.sk, LANE), lambda i: (0, 0)))
        args.append(overlap)
        out_specs += [pl.BlockSpec((tq, LANE), lambda i: (i, 0)), pl.BlockSpec((nsub, LANE, tb), lambda i: (i, 0, 0))]
        out_shape += [jax.ShapeDtypeStruct((s, LANE), F32), jax.ShapeDtypeStruct((cfg.nb, LANE, tb), F32)]
        scratch = [pltpu.VMEM((nsub, LANE, tb), F32)]
    return _pcall(kern, name=name, grid=(cfg.nq,), in_specs=ins, out_specs=out_specs, out_shape=out_shape,
                  scratch_shapes=scratch,
                  compiler_params=pltpu.CompilerParams(dimension_semantics=("parallel",)))(*args)


def _attn_dq_small(cfg, q, ks, v, o, lse, do, dq_in, name):
    s, tq, tk, tb, nsub, nh, dk = cfg.s, cfg.tq, cfg.tk, cfg.tb, cfg.nsub, cfg.h, cfg.dk
    nkp = len(ks)
    has_in = dq_in is not None

    def kern(*refs):
        q_ref, k_refs, v_ref = refs[0], refs[1:1 + nkp], refs[1 + nkp]
        o_ref, l_ref, do_ref = refs[2 + nkp:5 + nkp]
        in_ref = refs[5 + nkp] if has_in else None
        dq_ref, dr_ref = refs[5 + nkp + has_in:7 + nkp + has_in]
        i = pl.program_id(0)
        for r in range(nsub):
            rows = slice(r * tb, (r + 1) * tb)
            t = i * tq + r * tb + lax.broadcasted_iota(jnp.int32, (tb, 1), 0)
            win = _key_window(cfg, i, r)
            n = win.start + lax.broadcasted_iota(jnp.int32, (1, tk), 1)
            for hh in range(nh):
                vcols = slice(hh * HEAD_V, (hh + 1) * HEAD_V)
                qcols = slice(hh * dk, (hh + 1) * dk)
                qv = q_ref[rows, qcols].astype(BF16)
                kk = _load_keys([kr.at[:, _head_cols(p, hh)] for kr, p in zip(k_refs, ks)], win)
                vv = v_ref[win, _head_cols(v, hh)].astype(BF16)
                dov = do_ref[rows, vcols]
                dvec = jnp.sum(dov * o_ref[rows, vcols], axis=1, keepdims=True)
                dr_ref[hh, r] = _to_row(dvec)
                p, _ = _scores(cfg, _nt(qv, kk), t, n, hh, None, True, l_ref[hh, rows, :])
                ds = p * (_nt(dov.astype(BF16), vv) - dvec)
                dq = _nn(ds.astype(BF16), kk) * cfg.scale
                dq_ref[rows, qcols] = dq + in_ref[rows, qcols] if has_in else dq

    qs = pl.BlockSpec((tq, nh * dk), lambda i: (i, 0))
    ins = [_all_heads(cfg, q, tq)] + [_all_heads(cfg, p, cfg.sk, True) for p in ks]
    ins += [_all_heads(cfg, v, cfg.sk, True)]
    ins += [_all_heads(cfg, o, tq), pl.BlockSpec((nh, tq, 1), lambda i: (0, i, 0)), _all_heads(cfg, do, tq)]
    args = [q.arr] + [p.arr for p in ks] + [v.arr, o.arr, lse, do.arr]
    if has_in:
        ins.append(qs)
        args.append(dq_in)
    return _pcall(
        kern, name=name, grid=(cfg.nq,), in_specs=ins,
        out_specs=[qs, pl.BlockSpec((nh, nsub, 1, tb), lambda i: (0, i, 0, 0))],
        out_shape=[jax.ShapeDtypeStruct((s, nh * dk), F32), jax.ShapeDtypeStruct((nh, cfg.nb, 1, tb), F32)],
        compiler_params=pltpu.CompilerParams(dimension_semantics=("parallel",)))(*args)


def _attn_bwd(cfg, q, ks, v, sel, selt, o, lse, lse_r, do, dq_in, name):
    if not cfg.causal:
        dq, d_r = _attn_dq_small(cfg, q, ks, v, o, lse, do, dq_in, name + "_dq")
        res = _attn_dkv(cfg, q, ks, v, selt, lse_r, d_r, do, name + "_dkv")
        return dq, res[:-1], res[-1]
    dq, d_r = _attn_dq(cfg, q, ks, v, sel, o, lse, do, dq_in, name + "_dq")
    res = _attn_dkv(cfg, q, ks, v, selt, lse_r, d_r, do, name + "_dkv")
    return dq, res[:-1], res[-1]


def _silu_grad(pre):
    sg = _sigmoid(pre)
    return sg * (1.0 + pre * (1.0 - sg))


def _compress_fwd(a_lo, a_hi, pe_lo, pe_hi, w1_lo, w1_hi, w2, name):
    n, dp = a_lo.shape[0], w2.shape[1]

    def kern(alo, ahi, plo, phi, w1l, w1h, w2r, out_ref, pre_ref):
        xl = (alo[...] + plo[...]).astype(BF16)
        xh = (ahi[...] + phi[...]).astype(BF16)
        pre = _nn(xl, w1l[...]) + _nn(xh, w1h[...])
        act = pre * _sigmoid(pre)
        out_ref[...] = _nn(act.astype(BF16), w2r[...]).astype(BF16)
        pre_ref[...] = pre

    return _pcall(kern, name=name,
                  out_shape=[jax.ShapeDtypeStruct((n, dp), BF16), jax.ShapeDtypeStruct((n, dp), F32)],
                  )(a_lo, a_hi, pe_lo, pe_hi, w1_lo, w1_hi, w2)


def _compress_bwd(a_lo, a_hi, pe_lo, pe_hi, w1_lo, w1_hi, w2, pre, pre_sh, dout, dout_sh, name):
    n, ln = a_lo.shape
    dp = w2.shape[1]

    def kern(alo, ahi, plo, phi, w1l, w1h, w2r, pre_ref, presh_ref, do_ref, dosh_ref,
             da_ref, dpl_ref, dph_ref, dw1l_ref, dw1h_ref, dw2_ref):
        prev = pre_ref[...]
        act = prev * _sigmoid(prev)
        dob = do_ref[...].astype(BF16)
        w2v = w2r[...]
        dpre = (_nt(dob, w2v) * _silu_grad(prev)).astype(BF16)
        dpre_sh = (_nt(dosh_ref[...].astype(BF16), w2v) * _silu_grad(presh_ref[...])).astype(BF16)
        dw2_ref[...] = _nn(act.T.astype(BF16), dob)
        xl = alo[...] + plo[...]
        xh = ahi[...] + phi[...]
        dw1l_ref[...] = _nn(xl.T.astype(BF16), dpre)
        dw1h_ref[...] = _nn(xh.T.astype(BF16), dpre)
        dal = _nt(dpre, w1l[...])
        dah_sh = _nt(dpre_sh, w1h[...])
        da_ref[...] = dal + dah_sh
        dpl_ref[...] = jnp.sum(dal, axis=0, keepdims=True)
        dph_ref[...] = jnp.sum(dah_sh, axis=0, keepdims=True)

    return _pcall(
        kern, name=name,
        out_shape=[jax.ShapeDtypeStruct((n, ln), F32), jax.ShapeDtypeStruct((1, ln), F32),
                   jax.ShapeDtypeStruct((1, ln), F32), jax.ShapeDtypeStruct((ln, dp), F32),
                   jax.ShapeDtypeStruct((ln, dp), F32), jax.ShapeDtypeStruct((dp, dp), F32)],
    )(a_lo, a_hi, pe_lo, pe_hi, w1_lo, w1_hi, w2, pre, pre_sh, dout, dout_sh)


def _nsa_combine(o_cmp, o_slc, o_win, gl):
    s, w = o_cmp.shape
    tr = _tile(s, 512)

    def kern(a_ref, b_ref, c_ref, g_ref, o_ref):
        g = _sigmoid(g_ref[...])
        for h in range(NSA_HEADS):
            cs = slice(h * HEAD_V, (h + 1) * HEAD_V)
            o_ref[:, cs] = (g[:, 3 * h:3 * h + 1] * a_ref[:, cs] + g[:, 3 * h + 1:3 * h + 2] * b_ref[:, cs]
                            + g[:, 3 * h + 2:3 * h + 3] * c_ref[:, cs])

    row = pl.BlockSpec((tr, w), lambda i: (i, 0))
    return _pcall(kern, name="nsa_combine", grid=(s // tr,),
                  in_specs=[row, row, row, pl.BlockSpec((tr, LANE), lambda i: (i, gl.col0))], out_specs=row,
                  out_shape=jax.ShapeDtypeStruct((s, w), F32))(o_cmp, o_slc, o_win, gl.arr)


def _nsa_combine_bwd(do_cat, o_cmp, o_slc, o_win, gl):
    s, w = o_cmp.shape
    tr = _tile(s, 512)

    def kern(d_ref, a_ref, b_ref, c_ref, g_ref, da_ref, db_ref, dc_ref, dg_ref):
        g = _sigmoid(g_ref[...])
        lane = lax.broadcasted_iota(jnp.int32, (tr, LANE), 1)
        dgl = jnp.zeros((tr, LANE), F32)
        for h in range(NSA_HEADS):
            cs = slice(h * HEAD_V, (h + 1) * HEAD_V)
            dv = d_ref[:, cs]
            for b, (src, dst) in enumerate(((a_ref, da_ref), (b_ref, db_ref), (c_ref, dc_ref))):
                gate = g[:, 3 * h + b:3 * h + b + 1]
                dst[:, cs] = gate * dv
                dgate = jnp.sum(dv * src[:, cs], axis=1, keepdims=True)
                dgl = jnp.where(lane == 3 * h + b, dgate * gate * (1.0 - gate), dgl)
        dg_ref[...] = dgl

    row = pl.BlockSpec((tr, w), lambda i: (i, 0))
    tab = pl.BlockSpec((tr, LANE), lambda i: (i, 0))
    return _pcall(kern, name="nsa_combine_bwd", grid=(s // tr,),
                  in_specs=[pl.BlockSpec((tr, w), lambda i: (i, 2)), row, row, row,
                            pl.BlockSpec((tr, LANE), lambda i: (i, gl.col0))],
                  out_specs=[row, row, row, tab],
                  out_shape=[jax.ShapeDtypeStruct((s, w), F32)] * 3 + [jax.ShapeDtypeStruct((s, LANE), F32)],
                  )(do_cat, o_cmp, o_slc, o_win, gl.arr)


def _gate_fwd(o_mla, o_nsa, o_mem, hp):
    s = o_mla.shape[0]
    tr = _tile(s, 256)

    def kern(a_ref, b_ref, c_ref, z_ref, u_ref):
        z = z_ref[...]
        sz = z * _sigmoid(z)
        u_ref[:, 0:1024] = (a_ref[...] * sz[:, 0:1024]).astype(BF16)
        u_ref[:, 1024:1536] = (b_ref[...] * sz[:, 1024:1536]).astype(BF16)
        u_ref[:, 1536:2048] = (c_ref[...] * sz[:, 1536:2048]).astype(BF16)

    return _pcall(
        kern, name="gate_fwd", grid=(s // tr,),
        in_specs=[pl.BlockSpec((tr, 1024), lambda i: (i, 0)), pl.BlockSpec((tr, 512), lambda i: (i, 0)),
                  pl.BlockSpec((tr, 512), lambda i: (i, 0)), pl.BlockSpec((tr, 2048), lambda i: (i, 2))],
        out_specs=pl.BlockSpec((tr, 2048), lambda i: (i, 0)),
        out_shape=jax.ShapeDtypeStruct((s, 2048), BF16))(o_mla, o_nsa, o_mem, hp)


def _gate_bwd(du, o_mla, o_nsa, o_mem, hp):
    s = du.shape[0]
    tr = _tile(s, 256)

    def kern(d_ref, a_ref, b_ref, c_ref, z_ref, do_ref, dz_ref):
        z = z_ref[...]
        sg = _sigmoid(z)
        sz = z * sg
        dsz = sg * (1.0 + z * (1.0 - sg))
        d = d_ref[...]
        do_ref[...] = d * sz
        dz_ref[:, 0:1024] = d[:, 0:1024] * a_ref[...] * dsz[:, 0:1024]
        dz_ref[:, 1024:1536] = d[:, 1024:1536] * b_ref[...] * dsz[:, 1024:1536]
        dz_ref[:, 1536:2048] = d[:, 1536:2048] * c_ref[...] * dsz[:, 1536:2048]

    wide = pl.BlockSpec((tr, 2048), lambda i: (i, 0))
    return _pcall(
        kern, name="gate_bwd", grid=(s // tr,),
        in_specs=[wide, pl.BlockSpec((tr, 1024), lambda i: (i, 0)), pl.BlockSpec((tr, 512), lambda i: (i, 0)),
                  pl.BlockSpec((tr, 512), lambda i: (i, 0)), pl.BlockSpec((tr, 2048), lambda i: (i, 2))],
        out_specs=[wide, wide],
        out_shape=[jax.ShapeDtypeStruct((s, 2048), F32)] * 2)(du, o_mla, o_nsa, o_mem, hp)


def _tile2d(rows, cols, arrays):
    if rows % 16 == 0:
        return _row_tile(rows, cols * arrays), cols
    want = max(LANE, BLOCK_BYTES // (rows * 4 * arrays) // LANE * LANE)
    tc = LANE
    for t in range(LANE, cols + 1, LANE):
        if cols % t == 0 and t <= want:
            tc = t
    return rows, tc


def _sum_slots(buf, name):
    n, rows, cols = buf.shape
    tr, tc = _tile2d(rows, cols, n)

    def kern(b_ref, o_ref):
        acc = b_ref[0].astype(F32)
        for i in range(1, n):
            acc = acc + b_ref[i].astype(F32)
        o_ref[...] = acc

    return _pcall(kern, name=name, grid=(rows // tr, cols // tc),
                  in_specs=[pl.BlockSpec((n, tr, tc), lambda i, j: (0, i, j))],
                  out_specs=pl.BlockSpec((tr, tc), lambda i, j: (i, j)),
                  out_shape=jax.ShapeDtypeStruct((rows, cols), F32))(buf)


def _pair_sum(g4, theirs, core, axis, name):
    n, rows, cols = theirs.shape
    tr, tc = _tile2d(rows, cols, 1)
    nbr, nbc = rows // tr, cols // tc

    def kern(c_ref, a_ref, b_ref, o_ref):
        o_ref[...] = (a_ref[...] + b_ref[...]).astype(BF16)

    blk = (1, tr, tc)
    mine = ((lambda s, i, j, c: (s, c[0] * nbr + i, j)) if axis == 0
            else (lambda s, i, j, c: (s, i, c[0] * nbc + j)))
    grid_spec = pltpu.PrefetchScalarGridSpec(
        num_scalar_prefetch=1, grid=(n, nbr, nbc),
        in_specs=[pl.BlockSpec(blk, mine), pl.BlockSpec(blk, lambda s, i, j, c: (s, i, j))],
        out_specs=pl.BlockSpec(blk, lambda s, i, j, c: (s, i, j)))
    return _pcall(kern, name=name, grid_spec=grid_spec,
                  out_shape=jax.ShapeDtypeStruct((n, rows, cols), BF16))(core, g4, theirs)


def _adamw(w, g, m, v, name):
    rows, cols = w.shape
    tr, tc = _tile2d(rows, cols, 4)
    bc1 = 1.0 - ADAM_B1 ** ADAM_STEP
    bc2 = 1.0 - ADAM_B2 ** ADAM_STEP

    def kern(w_ref, g_ref, m_ref, v_ref, d_ref, mo_ref, vo_ref):
        gv = g_ref[...]
        mn = ADAM_B1 * m_ref[...] + (1.0 - ADAM_B1) * gv
        vn = ADAM_B2 * v_ref[...] + (1.0 - ADAM_B2) * (gv * gv)
        d_ref[...] = -ADAM_LR * ((mn / bc1) / (jnp.sqrt(vn / bc2) + ADAM_EPS) + ADAM_WD * w_ref[...])
        mo_ref[...] = mn
        vo_ref[...] = vn

    blk = pl.BlockSpec((tr, tc), lambda i, j: (i, j))
    return _pcall(kern, name=name, grid=(rows // tr, cols // tc), in_specs=[blk] * 4, out_specs=[blk] * 3,
                  out_shape=[jax.ShapeDtypeStruct((rows, cols), F32)] * 3)(w, g, m, v)


ANY = pl.BlockSpec(memory_space=pl.ANY)


def _place():
    x, y, c = lax.axis_index("x"), lax.axis_index("y"), lax.axis_index("c")
    chips = [(1 - x, y), (x, 1 - y), (1 - x, 1 - y)]
    return x, y, c, chips


def _remote(src, dst, send_sem, recv_sem, to):
    return pltpu.make_async_remote_copy(src_ref=src, dst_ref=dst, send_sem=send_sem, recv_sem=recv_sem,
                                        device_id=to, device_id_type=MESH)


def _half(ref, lead, core, axis):
    size = ref.shape[len(lead) + axis] // 2
    cut = pl.ds(core * size, size)
    return ref.at[tuple(lead) + ((cut, slice(None)) if axis == 0 else (slice(None), cut))]


def _gather_shards(ws, axes):
    nw = len(ws)

    def body(*refs):
        w_refs, out_refs = refs[:nw], refs[nw:2 * nw]
        send_sems, recv_sems = refs[2 * nw:]
        x, y, c, chips = _place()
        me = 2 * x + y
        sibling = (x, y, 1 - c)

        def part(i, slot, core):
            return _half(out_refs[i], (slot,), core, axes[i])

        def copy(sem, src, dst, to):
            return _remote(src, dst, send_sems.at[sem], recv_sems.at[sem], to)

        first = [copy(j * nw + i, _half(w_refs[i], (), c, axes[i]), part(i, me, c), (*chip, c))
                 for j, chip in enumerate(chips) for i in range(nw)]
        for cp in first:
            cp.start()
        passed = []
        for j, (cx, cy) in enumerate(chips):
            slot = 2 * cx + cy
            for i in range(nw):
                copy(j * nw + i, part(i, slot, c), part(i, slot, c), (x, y, c)).wait_recv()
                fwd = copy((3 + j) * nw + i, part(i, slot, c), part(i, slot, c), sibling)
                fwd.start()
                passed.append(fwd)
        for j, (cx, cy) in enumerate(chips):
            slot = 2 * cx + cy
            for i in range(nw):
                copy((3 + j) * nw + i, part(i, slot, 1 - c), part(i, slot, 1 - c), (x, y, c)).wait_recv()
        for cp in first + passed:
            cp.wait_send()

    return _pcall(
        body, name="gather_shards", in_specs=[ANY] * nw, out_specs=[ANY] * nw,
        out_shape=[jax.ShapeDtypeStruct((4,) + w.shape, w.dtype) for w in ws],
        scratch_shapes=[pltpu.SemaphoreType.DMA((6 * nw,)), pltpu.SemaphoreType.DMA((6 * nw,))],
    )(*ws)


def _half_shape(shape, axis):
    return tuple(d // 2 if k == len(shape) - 2 + axis else d for k, d in enumerate(shape))


def _pair_exchange(gs, axes):
    nw = len(gs)

    def body(*refs):
        g_refs, out_refs = refs[:nw], refs[nw:2 * nw]
        send_sems, recv_sems = refs[2 * nw:]
        x, y, c, _ = _place()
        cps = []
        for i in range(nw):
            cp = _remote(_half(g_refs[i], (slice(None),), 1 - c, axes[i]), out_refs[i],
                         send_sems.at[i], recv_sems.at[i], (x, y, 1 - c))
            cp.start()
            cps.append(cp)
        for cp in cps:
            cp.wait()

    return _pcall(body, name="pair_exchange", in_specs=[ANY] * nw, out_specs=[ANY] * nw,
                  out_shape=[jax.ShapeDtypeStruct(_half_shape(g.shape, a), g.dtype) for g, a in zip(gs, axes)],
                  scratch_shapes=[pltpu.SemaphoreType.DMA((nw,)), pltpu.SemaphoreType.DMA((nw,))])(*gs)


def _chip_exchange(ps):
    nw = len(ps)

    def body(*refs):
        p_refs, out_refs = refs[:nw], refs[nw:2 * nw]
        send_sems, recv_sems, local_sems = refs[2 * nw:]
        x, y, c, chips = _place()
        me = 2 * x + y
        mine = [pltpu.make_async_copy(p_refs[i].at[me], out_refs[i].at[me], local_sems.at[i]) for i in range(nw)]
        for cp in mine:
            cp.start()
        sends = []
        for j, (cx, cy) in enumerate(chips):
            for i in range(nw):
                cp = _remote(p_refs[i].at[2 * cx + cy], out_refs[i].at[me], send_sems.at[j * nw + i],
                             recv_sems.at[j * nw + i], (cx, cy, c))
                cp.start()
                sends.append(cp)
        for j, (cx, cy) in enumerate(chips):
            slot = 2 * cx + cy
            for i in range(nw):
                _remote(out_refs[i].at[slot], out_refs[i].at[slot], send_sems.at[j * nw + i],
                        recv_sems.at[j * nw + i], (x, y, c)).wait_recv()
        for cp in sends:
            cp.wait_send()
        for cp in mine:
            cp.wait()

    return _pcall(body, name="chip_exchange", in_specs=[ANY] * nw, out_specs=[ANY] * nw,
                  out_shape=[jax.ShapeDtypeStruct(p.shape, p.dtype) for p in ps],
                  scratch_shapes=[pltpu.SemaphoreType.DMA((3 * nw,)), pltpu.SemaphoreType.DMA((3 * nw,)),
                                  pltpu.SemaphoreType.DMA((nw,))])(*ps)


def _half_exchange(ts, axes):
    nw = len(ts)

    def body(*refs):
        t_refs, out_refs = refs[:nw], refs[nw:2 * nw]
        send_sems, recv_sems = refs[2 * nw:]
        x, y, c, _ = _place()
        sends = []
        for i in range(nw):
            cp = _remote(t_refs[i], _half(out_refs[i], (), c, axes[i]), send_sems.at[i], recv_sems.at[i],
                         (x, y, 1 - c))
            cp.start()
            sends.append(cp)
        for i in range(nw):
            _remote(t_refs[i], _half(out_refs[i], (), 1 - c, axes[i]), send_sems.at[i], recv_sems.at[i],
                    (x, y, c)).wait_recv()
        for cp in sends:
            cp.wait_send()

    def whole(t, a):
        return tuple(2 * d if k == a else d for k, d in enumerate(t.shape))

    return _pcall(body, name="half_exchange", in_specs=[ANY] * nw, out_specs=[ANY] * nw,
                  out_shape=[jax.ShapeDtypeStruct(whole(t, a), t.dtype) for t, a in zip(ts, axes)],
                  scratch_shapes=[pltpu.SemaphoreType.DMA((nw,)), pltpu.SemaphoreType.DMA((nw,))])(*ts)


def _gather_all(v):
    rows, cols = v.shape

    def body(v_ref, out_ref, send_sems, recv_sems, local_sem):
        x, y, c, _ = _place()
        me = 4 * x + 2 * y + c
        mine = pltpu.make_async_copy(v_ref, out_ref.at[me], local_sem)
        mine.start()
        sends = []
        for d in range(1, 8):
            peer = (x ^ (d >> 2), y ^ ((d >> 1) & 1), c ^ (d & 1))
            cp = _remote(v_ref, out_ref.at[me], send_sems.at[d - 1], recv_sems.at[d - 1], peer)
            cp.start()
            sends.append(cp)
        for d in range(1, 8):
            slot = 4 * (x ^ (d >> 2)) + 2 * (y ^ ((d >> 1) & 1)) + (c ^ (d & 1))
            _remote(v_ref, out_ref.at[slot], send_sems.at[d - 1], recv_sems.at[d - 1], (x, y, c)).wait_recv()
        for cp in sends:
            cp.wait_send()
        mine.wait()

    return _pcall(body, name="gather_all", in_specs=[ANY], out_specs=ANY,
                  out_shape=jax.ShapeDtypeStruct((8, rows, cols), v.dtype),
                  scratch_shapes=[pltpu.SemaphoreType.DMA((7,)), pltpu.SemaphoreType.DMA((7,)),
                                  pltpu.SemaphoreType.DMA])(v)


def _pad_cols(a, width):
    return a if a.shape[1] == width else jnp.pad(a, ((0, 0), (0, width - a.shape[1])))


def _unpad_segments():
    z = PAD["z"]
    segs = [(PAD["c_q"], 0, 512), (PAD["c_kv"], 512, 512), (PAD["k_rope"], 1024, 64), (z, 1088, 1024)]
    segs += [(PAD["q_nsa"] + 256 * h, 2112 + NSA_DK * h, NSA_DK) for h in range(NSA_HEADS)]
    for name, rows in (("k_c", 192), ("v_c", 128), ("k_s", 192), ("v_s", 128), ("k_w", 192), ("v_w", 128),
                       ("g_nsa", 12)):
        segs.append((PAD[name], ORIG[name][0], rows))
    segs += [(z + 1024, ORIG["z_nsa"][0], 512), (PAD["q_mem"], ORIG["q_mem"][0], 512),
             (z + 1536, ORIG["z_mem"][0], 512)]
    return segs


def _w_in_grad_slots(gt):
    rows, cols = gt.shape
    shard = sum(n for _, _, n in _unpad_segments()) // 4
    tc = 256
    pieces = []
    for src, dst, n in _unpad_segments():
        while n:
            slot, off = divmod(dst, shard)
            take = min(n, shard - off)
            pieces.append((src, slot, off, take))
            src, dst, n = src + take, dst + take, n - take

    def kern(g_ref, o_ref):
        for src, slot, off, take in pieces:
            o_ref[slot, off:off + take, :] = g_ref[src:src + take, :]

    return _pcall(kern, name="w_in_grad_slots", grid=(cols // tc,),
                  in_specs=[pl.BlockSpec((rows, tc), lambda i: (0, i))],
                  out_specs=pl.BlockSpec((4, shard, tc), lambda i: (0, 0, i)),
                  out_shape=jax.ShapeDtypeStruct((4, shard, cols), F32))(gt)


def _w_in_from_slots(ws):
    nslot, shard, cols = ws.shape
    tc = 256
    pieces = []
    for dst, src, n in _unpad_segments() + [(PAD["k_rope"] + 64, ORIG["k_rope"][0], 64)]:
        while n:
            slot, off = divmod(src, shard)
            take = min(n, shard - off)
            pieces.append((dst, slot, off, take))
            src, dst, n = src + take, dst + take, n - take

    def kern(w_ref, o_ref):
        o_ref[...] = jnp.zeros_like(o_ref)
        for dst, slot, off, take in pieces:
            o_ref[dst:dst + take, :] = w_ref[slot, off:off + take, :]

    return _pcall(kern, name="w_in_from_slots", grid=(cols // tc,),
                  in_specs=[pl.BlockSpec((nslot, shard, tc), lambda i: (0, 0, i))],
                  out_specs=pl.BlockSpec((D_PAD, tc), lambda i: (0, i)),
                  out_shape=jax.ShapeDtypeStruct((D_PAD, cols), ws.dtype))(ws)


def _rope_tables(s):
    pos = jnp.arange(s, dtype=F32)
    inv_freq = ROPE_THETA ** (-jnp.arange(0, 64, 2, dtype=F32) / 64)
    ang = pos[:, None] * inv_freq[None, :]
    cos, sin = jnp.cos(ang), jnp.sin(ang)
    z = jnp.zeros((s, 64), F32)
    return jnp.concatenate([cos, cos, z], axis=1), jnp.concatenate([-sin, sin, z], axis=1)


def _overlap_table(s):
    n_c, n_s = s // CMP_STRIDE, s // SLC_LEN
    c0 = np.arange(n_c)[:, None] * CMP_STRIDE
    s0 = np.arange(LANE)[None, :] * SLC_LEN
    ov = (c0 < s0 + SLC_LEN) & (c0 + CMP_LEN > s0) & (np.arange(n_c)[:, None] < n_c - 1) & (np.arange(LANE)[None, :] < n_s)
    return jnp.asarray(ov.astype(np.float32), dtype=BF16)


def _shift_down(a):
    return jnp.concatenate([jnp.zeros((8, a.shape[1]), a.dtype), a], axis=0)[7:7 + a.shape[0]]


def _shift_up(a):
    return jnp.concatenate([a, jnp.zeros((8, a.shape[1]), a.dtype)], axis=0)[1:1 + a.shape[0]]


def _local_step(x, mem, target, w):
    s = x.shape[0]
    cs, sn = _rope_tables(s)
    t_ = jnp.transpose

    w_in_p = _w_in_from_slots(w["w_in_t"])
    xn, rstd_x = _rms_fwd(_Src(x, D_MODEL), w["norm_g"], "norm_x")
    hp, hpb = _mm(xn, w_in_p, "in_proj", mode="nt", second_dtype=BF16)

    w_uq3 = w["w_uq"].reshape(512, MLA_HEADS, 192)
    w_uq_p = jnp.concatenate([w_uq3, w_uq3[:, :, 128:]], axis=2).reshape(512, MLA_HEADS * 256)
    w_ukv_p = t_(w["w_ukv"].reshape(512, MLA_HEADS, 2, 128), (0, 2, 1, 3)).reshape(512, 2048)
    c_q, c_kv = _Src(hp, 512, 0), _Src(hp, 512, 1)
    cqn, rstd_q = _rms_fwd(c_q, w["q_norm_g"], "norm_q")
    ckvn, rstd_kv = _rms_fwd(c_kv, w["kv_norm_g"], "norm_kv")
    q_lin = _mm(cqn, w_uq_p, "mla_q_proj")
    kvb = _mm(ckvn, w_ukv_p, "mla_kv_proj", out_dtype=BF16)
    q_mla = _rope_fwd(_Src(q_lin, MLA_HEADS * 256), cs, sn, MLA_HEADS, 256, LANE, "rope_q")
    k_pe = _rope_fwd(_Src(hp, LANE, PAD["k_rope"] // LANE), cs, sn, 1, LANE, 0, "rope_k")
    mla = _Attn("mla", s, s, MLA_HEADS, 256)
    mla_q, mla_v = _Src(q_mla, 256), _Src(kvb, LANE, MLA_HEADS)
    mla_k = [_Src(kvb, LANE), _Src(k_pe, LANE, 0, False)]
    o_mla, l_mla, lr_mla = _attn_fwd(mla, mla_q, mla_k, mla_v, None, "mla_fwd")

    sk = s // CMP_STRIDE
    pe_k, pe_v = w["cmp_pe_k"], w["cmp_pe_v"]
    w1k = _pad_cols(w["cmp_w1k"], 256)
    w2k = jnp.pad(w["cmp_w2k"], ((0, 64), (0, 64))).astype(BF16)
    w1v, w2v = w["cmp_w1v"], w["cmp_w2v"].astype(BF16)
    half_k, half_v = CMP_STRIDE * NSA_DK, CMP_STRIDE * HEAD_V
    ak = hp[:, PAD["k_c"]:PAD["k_c"] + NSA_DK].reshape(sk, half_k)
    av = hp[:, PAD["v_c"]:PAD["v_c"] + HEAD_V].reshape(sk, half_v)
    ck_args = (ak, _shift_up(ak), pe_k[:CMP_STRIDE].reshape(1, half_k), pe_k[CMP_STRIDE:].reshape(1, half_k),
               w1k[:half_k], w1k[half_k:], w2k)
    cv_args = (av, _shift_up(av), pe_v[:CMP_STRIDE].reshape(1, half_v), pe_v[CMP_STRIDE:].reshape(1, half_v),
               w1v[:half_v], w1v[half_v:], w2v)
    k_cmp, pre_k = _compress_fwd(*ck_args, "compress_k")
    v_cmp, pre_v = _compress_fwd(*cv_args, "compress_v")
    cmp_ = _Attn("cmp", s, sk, NSA_HEADS, 256)
    slc = _Attn("slc", s, s, NSA_HEADS, 256)
    win = _Attn("win", s, s, NSA_HEADS, 256)
    nsa_q = _Src(hpb, 256, PAD["q_nsa"] // 256)
    cmp_k, cmp_v = [_Src(k_cmp, 256, 0, False)], _Src(v_cmp, HEAD_V, 0, False)
    slc_k, slc_v = [_Src(hpb, 256, PAD["k_s"] // 256, False)], _Src(hpb, HEAD_V, PAD["v_s"] // HEAD_V, False)
    win_k, win_v = [_Src(hpb, 256, PAD["k_w"] // 256, False)], _Src(hpb, HEAD_V, PAD["v_w"] // HEAD_V, False)
    o_cmp, l_cmp, lr_cmp, sel, selt = _attn_fwd_small(cmp_, nsa_q, cmp_k, cmp_v, "cmp_fwd", _overlap_table(s))
    o_slc, l_slc, lr_slc = _attn_fwd(slc, nsa_q, slc_k, slc_v, sel, "slc_fwd")
    o_win, l_win, lr_win = _attn_fwd_small(win, nsa_q, win_k, win_v, "win_fwd")
    gl = _Src(hp, LANE, PAD["g_nsa"] // LANE)
    o_nsa = _nsa_combine(o_cmp, o_slc, o_win, gl)

    mn, rstd_m = _rms_fwd(_Src(mem, D_MODEL), w["mem_norm_g"], "norm_mem")
    kvm = _mm(mn, w["w_mem_kv"], "mem_kv_proj", out_dtype=BF16)
    mem_ = _Attn("mem", s, mem.shape[0], MEM_HEADS, LANE)
    mem_q, mem_k, mem_v = _Src(hpb, LANE, PAD["q_mem"] // LANE), [_Src(kvm, LANE)], _Src(kvm, LANE, MEM_HEADS)
    o_mem, l_mem, lr_mem = _attn_fwd_small(mem_, mem_q, mem_k, mem_v, "mem_fwd")

    u = _gate_fwd(o_mla, o_nsa, o_mem, hp)
    proj = _mm(u, w["w_out"], "out_proj")
    dy, g_final, loss = _final_loss(x, proj, w["final_norm_g"].reshape(1, -1), target)

    g_w_out = _mm(u, dy, "out_proj_dw", mode="tn")
    du = _mm(dy, w["w_out"], "out_proj_dx", mode="nt")
    do_cat, dz = _gate_bwd(du, o_mla, o_nsa, o_mem, hp)

    dq_mem, (dk_mem,), dv_mem = _attn_bwd(mem_, mem_q, mem_k, mem_v, None, None, _Src(o_mem, HEAD_V), l_mem,
                                          lr_mem, _Src(do_cat, HEAD_V, 12), None, "mem_bwd")
    dkvm = jnp.concatenate([dk_mem, dv_mem], axis=1)
    g_w_mem_kv = _mm(mn, dkvm, "mem_kv_dw", mode="tn")
    dmn = _mm(dkvm, w["w_mem_kv"], "mem_kv_dx", mode="nt")
    _, g_mem_norm = _rms_bwd(_Src(mem, D_MODEL), w["mem_norm_g"], rstd_m, dmn, None, "norm_mem_bwd")

    do_cmp, do_slc, do_win, dgl = _nsa_combine_bwd(do_cat, o_cmp, o_slc, o_win, gl)
    dq_n, (dk_cmp,), dv_cmp = _attn_bwd(cmp_, nsa_q, cmp_k, cmp_v, None, None, _Src(o_cmp, HEAD_V), l_cmp,
                                        lr_cmp, _Src(do_cmp, HEAD_V), None, "cmp_bwd")
    dq_n, (dk_s,), dv_s = _attn_bwd(slc, nsa_q, slc_k, slc_v, sel, selt, _Src(o_slc, HEAD_V), l_slc, lr_slc,
                                    _Src(do_slc, HEAD_V), dq_n, "slc_bwd")
    dq_n, (dk_w,), dv_w = _attn_bwd(win, nsa_q, win_k, win_v, None, None, _Src(o_win, HEAD_V), l_win, lr_win,
                                    _Src(do_win, HEAD_V), dq_n, "win_bwd")
    dak, dpk_lo, dpk_hi, dw1k_lo, dw1k_hi, g_w2k = _compress_bwd(
        *ck_args, pre_k, _shift_down(pre_k), dk_cmp, _shift_down(dk_cmp), "compress_k_bwd")
    dav, dpv_lo, dpv_hi, dw1v_lo, dw1v_hi, g_w2v = _compress_bwd(
        *cv_args, pre_v, _shift_down(pre_v), dv_cmp, _shift_down(dv_cmp), "compress_v_bwd")
    g_pe_k = jnp.concatenate([dpk_lo.reshape(CMP_STRIDE, NSA_DK), dpk_hi.reshape(CMP_STRIDE, NSA_DK)], axis=0)
    g_pe_v = jnp.concatenate([dpv_lo.reshape(CMP_STRIDE, HEAD_V), dpv_hi.reshape(CMP_STRIDE, HEAD_V)], axis=0)
    g_w1k = jnp.concatenate([dw1k_lo, dw1k_hi], axis=0)[:, :NSA_DK]
    g_w1v = jnp.concatenate([dw1v_lo, dw1v_hi], axis=0)
    dk_c = _pad_cols(dak.reshape(s, NSA_DK), 256)
    dv_c = dav.reshape(s, HEAD_V)

    dq_m, (dk_nope, dk_pe), dv_m = _attn_bwd(mla, mla_q, mla_k, mla_v, None, None, _Src(o_mla, HEAD_V), l_mla,
                                             lr_mla, _Src(do_cat, HEAD_V), None, "mla_bwd")
    dq_lin = _rope_bwd_q(dq_m, cs, sn)
    dkv_lin, d_krope = _rope_bwd_k(dk_nope, dk_pe, dv_m, cs, sn)
    g_w_uq_p = _mm(cqn, dq_lin, "mla_q_dw", mode="tn")
    dcqn = _mm(dq_lin, w_uq_p, "mla_q_dx", mode="nt")
    g_w_ukv_p = _mm(ckvn, dkv_lin, "mla_kv_dw", mode="tn")
    dckvn = _mm(dkv_lin, w_ukv_p, "mla_kv_dx", mode="nt")
    dc_q, g_q_norm = _rms_bwd(c_q, w["q_norm_g"], rstd_q, dcqn, None, "norm_q_bwd")
    dc_kv, g_kv_norm = _rms_bwd(c_kv, w["kv_norm_g"], rstd_kv, dckvn, None, "norm_kv_bwd")
    g_w_uq = g_w_uq_p.reshape(512, MLA_HEADS, 256)[:, :, :192].reshape(512, MLA_HEADS * 192)
    g_w_ukv = t_(g_w_ukv_p.reshape(512, 2, MLA_HEADS, 128), (0, 2, 1, 3)).reshape(512, 2048)

    dhp = jnp.concatenate(
        [dc_q, dc_kv, dq_n, dk_c, dk_s, dk_w, d_krope, dv_c, dv_s, dv_w, dgl,
         jnp.zeros((s, PAD["q_mem"] - (PAD["g_nsa"] + LANE)), F32), dq_mem, dz], axis=1).astype(BF16)
    g_w_in_t = _w_in_grad_slots(_mm(dhp, xn, "in_proj_dw", mode="tn", wide=2048))
    dxn = _mm(dhp, w_in_p, "in_proj_dx", wide=2048)
    grad_x, g_norm = _rms_bwd(_Src(x, D_MODEL), w["norm_g"], rstd_x, dxn, dy, "norm_x_bwd")

    grads = dict(norm_g=g_norm, w_in_t=g_w_in_t, q_norm_g=g_q_norm, w_uq=g_w_uq, kv_norm_g=g_kv_norm,
                 w_ukv=g_w_ukv, cmp_pe_k=g_pe_k, cmp_pe_v=g_pe_v, cmp_w1k=g_w1k, cmp_w2k=g_w2k[:NSA_DK, :NSA_DK],
                 cmp_w1v=g_w1v, cmp_w2v=g_w2v, mem_norm_g=g_mem_norm, w_mem_kv=g_w_mem_kv, w_out=g_w_out,
                 final_norm_g=g_final.reshape(-1))
    return loss[0, 0], grad_x, grads


def kernel(x, mem, norm_g, w_in, q_norm_g, w_uq, kv_norm_g, w_ukv, cmp_pe_k, cmp_pe_v, cmp_w1k, cmp_w2k, cmp_w1v, cmp_w2v, mem_norm_g, w_mem_kv, w_out, final_norm_g, loss_target, m_norm_g, m_w_in, m_q_norm_g, m_w_uq, m_kv_norm_g, m_w_ukv, m_cmp_pe_k, m_cmp_pe_v, m_cmp_w1k, m_cmp_w2k, m_cmp_w1v, m_cmp_w2v, m_mem_norm_g, m_w_mem_kv, m_w_out, m_final_norm_g, v_norm_g, v_w_in, v_q_norm_g, v_w_uq, v_kv_norm_g, v_w_ukv, v_cmp_pe_k, v_cmp_pe_v, v_cmp_w1k, v_cmp_w2k, v_cmp_w1v, v_cmp_w2v, v_mem_norm_g, v_w_mem_kv, v_w_out, v_final_norm_g):
    args = dict(locals())
    wts = {n: args[n] for n in WEIGHTS}
    loc = {n: (a if n == "final_norm_g" else a[0]) for n, a in wts.items()}

    def to_x(n, a):
        return a.T if n == "w_in" else a

    split = [1 if n == "w_in" else 0 for n in SHARDED]

    own = [to_x(n, loc[n]).astype(BF16) for n in SHARDED]
    chip = 2 * lax.axis_index("x") + lax.axis_index("y")
    gathered = [lax.dynamic_update_slice(gw, a[None], (chip, 0, 0))
                for gw, a in zip(_gather_shards(own, split), own)]
    full = {n: loc[n].reshape(1, -1) if loc[n].ndim == 1 else loc[n] for n in REPLICATED}
    for n, gw in zip(SHARDED, gathered):
        if n == "w_in":
            full["w_in_t"] = gw
        elif SHARD_AXIS[n] == 0:
            full[n] = gw.reshape(4 * gw.shape[1], gw.shape[2])
        else:
            full[n] = jnp.concatenate([gw[j] for j in range(4)], axis=1)

    loss, grad_x, g = _local_step(x[0], mem[0], loss_target[0], full)
    loss = lax.psum(loss, ("x", "y", "c"))

    def slots(n):
        if n == "w_in":
            return g["w_in_t"]
        a = g[n]
        if SHARD_AXIS[n] == 0:
            return a.reshape(4, a.shape[0] // 4, a.shape[1])
        width = a.shape[1] // 4
        return jnp.stack([a[:, j * width:(j + 1) * width] for j in range(4)])

    gs = [slots(n) for n in SHARDED]
    core = lax.axis_index("c").astype(jnp.int32).reshape(1)
    theirs = _pair_exchange(gs, split)
    pairs = [_pair_sum(a, b, core, ax, "pair_sum_" + n) for n, a, b, ax in zip(SHARDED, gs, theirs, split)]
    from_chips = _chip_exchange(pairs)
    mine = [_sum_slots(b, "chip_sum_" + n) for n, b in zip(SHARDED, from_chips)]
    g_sh = [lax.dynamic_update_slice(o, t, (core[0] * t.shape[0], 0) if ax == 0 else (0, core[0] * t.shape[1]))
            for o, t, ax in zip(_half_exchange(mine, split), mine, split)]

    n_rep = sum(int(np.prod(loc[n].shape)) for n in REPLICATED)
    rows_rep = -(-n_rep // (8 * LANE)) * 8

    def rep_pack(parts):
        flat = jnp.concatenate([p.reshape(-1) for p in parts])
        return jnp.pad(flat, (0, rows_rep * LANE - n_rep)).reshape(rows_rep, LANE)

    g_rep = _sum_slots(_gather_all(rep_pack([g[n] for n in REPLICATED])), "replica_sum")
    d_rp, m_rp, v_rp = _adamw(rep_pack([wts[n] for n in REPLICATED]), g_rep,
                              rep_pack([args["m_" + n] for n in REPLICATED]),
                              rep_pack([args["v_" + n] for n in REPLICATED]), "adamw_replicated")

    def rep_unpack(buf):
        flat, out, o = buf.reshape(-1), {}, 0
        for n in REPLICATED:
            size = int(np.prod(wts[n].shape))
            out[n] = flat[o:o + size].reshape(wts[n].shape)
            o += size
        return out

    outs = {k: rep_unpack(b) for k, b in (("g", g_rep), ("d", d_rp), ("m", m_rp), ("v", v_rp))}
    for n, gn in zip(SHARDED, g_sh):
        d, mo, vo = _adamw(to_x(n, loc[n]), gn, to_x(n, args["m_" + n][0]), to_x(n, args["v_" + n][0]),
                           "adamw_" + n)
        for k, a in (("g", gn), ("d", d), ("m", mo), ("v", vo)):
            outs[k][n] = to_x(n, a).reshape(wts[n].shape)

    return (loss, grad_x[None], *[outs["g"][n] for n in WEIGHTS], *[outs["d"][n] for n in WEIGHTS],
            *[outs["m"][n] for n in WEIGHTS], *[outs["v"][n] for n in WEIGHTS])
```

```python
from typing import NamedTuple

import numpy as np
import jax
import jax.numpy as jnp
from jax import lax
from jax.experimental import pallas as pl
from jax.experimental.pallas import tpu as pltpu

F32 = jnp.float32
BF16 = jnp.bfloat16
MESH = pl.DeviceIdType.MESH

D_MODEL = 2048
EPS = 1e-6
LANE = 128
HEAD_V = 128
MLA_HEADS = 8
NSA_HEADS = 4
MEM_HEADS = 4
NSA_DK = 192
CMP_STRIDE = 16
CMP_LEN = 32
SLC_LEN = 64
SLC_TOPN = 16
WIN = 512
NEG = -1e30
LOG2E = 1.4426950408889634
ROPE_THETA = 10000.0
BLOCK_BYTES = 2 << 20

ORIG = dict(c_q=(0, 512), c_kv=(512, 512), k_rope=(1024, 64), z_mla=(1088, 1024),
            q_nsa=(2112, 768), k_c=(2880, 192), v_c=(3072, 128), k_s=(3200, 192),
            v_s=(3392, 128), k_w=(3520, 192), v_w=(3712, 128), g_nsa=(3840, 12),
            z_nsa=(3852, 512), q_mem=(4364, 512), z_mem=(4876, 512))
PAD = dict(c_q=0, c_kv=512, q_nsa=1024, k_c=2048, k_s=2304, k_w=2560, k_rope=2816, v_c=2944,
           v_s=3072, v_w=3200, g_nsa=3328, q_mem=3584, z=4096)
D_PAD = 6144

ADAM_LR, ADAM_B1, ADAM_B2, ADAM_EPS, ADAM_WD, ADAM_STEP = 0.001, 0.9, 0.999, 1e-08, 0.01, 10

SHARDED = ("w_in", "w_uq", "w_ukv", "cmp_w1k", "cmp_w1v", "w_mem_kv", "w_out")
SHARD_AXIS = dict(w_in=1, w_uq=1, w_ukv=1, cmp_w1k=0, cmp_w1v=0, w_mem_kv=0, w_out=0)
REPLICATED = ("norm_g", "q_norm_g", "kv_norm_g", "cmp_pe_k", "cmp_pe_v", "cmp_w2k", "cmp_w2v",
              "mem_norm_g", "final_norm_g")
WEIGHTS = ("norm_g", "w_in", "q_norm_g", "w_uq", "kv_norm_g", "w_ukv", "cmp_pe_k", "cmp_pe_v",
           "cmp_w1k", "cmp_w2k", "cmp_w1v", "cmp_w2v", "mem_norm_g", "w_mem_kv", "w_out",
           "final_norm_g")


def _pcall(kernel, **kw):
    return pl.pallas_call(kernel, **kw)


def _tile(n, pref):
    if n <= pref:
        return n
    for t in range(pref, LANE - 1, -LANE):
        if n % t == 0:
            return t
    raise ValueError((n, pref))


def _row_tile(rows, cols, itemsize=4):
    want = max(16, BLOCK_BYTES // (cols * itemsize))
    if rows <= want:
        return rows
    t = 16
    best = rows
    while t <= want:
        if rows % t == 0:
            best = t
        t *= 2
    return best


def _nt(a, b):
    return lax.dot_general(a, b, (((1,), (1,)), ((), ())), preferred_element_type=F32)


def _tn(a, b):
    return lax.dot_general(a, b, (((0,), (0,)), ((), ())), preferred_element_type=F32)


def _nn(a, b):
    return jnp.dot(a, b, preferred_element_type=F32)


def _sigmoid(x):
    return 1.0 / (1.0 + jnp.exp(-x))


class _Src(NamedTuple):
    arr: jax.Array
    width: int
    col0: int = 0
    per_head: bool = True

    def col(self, h):
        return self.col0 + h if self.per_head else self.col0


class _Side(NamedTuple):
    inputs: list
    out_shape: list
    scratch: list
    phase: object


def _mm(a, b, name, mode="nn", out_dtype=F32, second_dtype=None, wide=1024, side=None):
    if mode == "tn":
        k, m = a.shape
    else:
        m, k = a.shape
    if mode == "nt":
        n, k2 = b.shape
    else:
        k2, n = b.shape
    assert k == k2, (a.shape, b.shape, mode)
    tm, tn, tk = _tile(m, 1024), _tile(n, wide), _tile(k, 2048)
    grid = (m // tm, n // tn, k // tk)
    nk = grid[2]
    assert nk == 1 or (out_dtype == F32 and second_dtype is None)
    dot = {"nn": _nn, "nt": _nt, "tn": _tn}[mode]
    n_in = len(side.inputs) if side else 0
    n_out = len(side.out_shape) if side else 0
    n_res = 1 + (second_dtype is not None)

    def kern(*refs):
        a_ref, b_ref = refs[:2]
        res = refs[2 + n_in:2 + n_in + n_res]
        step = [pl.program_id(d) for d in range(3)]
        if side:
            side_refs = (refs[2:2 + n_in], refs[2 + n_in + n_res:2 + n_in + n_res + n_out],
                         refs[2 + n_in + n_res + n_out:])

            @pl.when((step[0] == 0) & (step[1] == 0) & (step[2] == 0))
            def _():
                side.phase("start", *side_refs)

        r = dot(a_ref[...].astype(BF16), b_ref[...].astype(BF16))
        if nk == 1:
            res[0][...] = r.astype(out_dtype)
            if n_res == 2:
                res[1][...] = r.astype(second_dtype)
        else:
            @pl.when(step[2] == 0)
            def _():
                res[0][...] = r

            @pl.when(step[2] > 0)
            def _():
                res[0][...] += r

        if side:
            @pl.when((step[0] == grid[0] - 1) & (step[1] == grid[1] - 1) & (step[2] == nk - 1))
            def _():
                side.phase("finish", *side_refs)

    a_spec = (pl.BlockSpec((tk, tm), lambda i, j, kk: (kk, i)) if mode == "tn"
              else pl.BlockSpec((tm, tk), lambda i, j, kk: (i, kk)))
    b_spec = (pl.BlockSpec((tn, tk), lambda i, j, kk: (j, kk)) if mode == "nt"
              else pl.BlockSpec((tk, tn), lambda i, j, kk: (kk, j)))
    o_spec = pl.BlockSpec((tm, tn), lambda i, j, kk: (i, j))
    out_specs = [o_spec] * n_res + [ANY] * n_out
    out_shape = [jax.ShapeDtypeStruct((m, n), out_dtype)]
    if second_dtype is not None:
        out_shape.append(jax.ShapeDtypeStruct((m, n), second_dtype))
    out_shape += list(side.out_shape) if side else []
    semantics = ("arbitrary",) * 3 if side else ("parallel", "parallel", "arbitrary")
    out = _pcall(
        kern, name=name, grid=grid, in_specs=[a_spec, b_spec] + [ANY] * n_in, out_specs=out_specs,
        out_shape=out_shape, scratch_shapes=list(side.scratch) if side else [],
        compiler_params=pltpu.CompilerParams(dimension_semantics=semantics),
    )(a, b, *(side.inputs if side else []))
    return out[0] if len(out) == 1 else out


def _rms_fwd(x, g, name):
    r, d = x.arr.shape[0], x.width
    tr = _tile(r, 512)

    def kern(x_ref, g_ref, y_ref, r_ref):
        xv = x_ref[...]
        rstd = lax.rsqrt(jnp.mean(xv * xv, axis=-1, keepdims=True) + EPS)
        y_ref[...] = (xv * rstd * g_ref[...]).astype(BF16)
        r_ref[...] = rstd

    return _pcall(
        kern, name=name, grid=(r // tr,),
        in_specs=[pl.BlockSpec((tr, d), lambda i: (i, x.col0)), pl.BlockSpec((1, d), lambda i: (0, 0))],
        out_specs=[pl.BlockSpec((tr, d), lambda i: (i, 0)), pl.BlockSpec((tr, 1), lambda i: (i, 0))],
        out_shape=[jax.ShapeDtypeStruct((r, d), BF16), jax.ShapeDtypeStruct((r, 1), F32)],
    )(x.arr, g)


def _rms_bwd(x, g, rstd, dy, add, name):
    r, d = x.arr.shape[0], x.width
    tr = _tile(r, 256)
    has_add = add is not None

    def kern(*refs):
        if has_add:
            x_ref, g_ref, r_ref, dy_ref, add_ref, dx_ref, dg_ref = refs
        else:
            x_ref, g_ref, r_ref, dy_ref, dx_ref, dg_ref = refs
        rs = r_ref[...]
        xhat = x_ref[...] * rs
        dyv = dy_ref[...]
        dyg = dyv * g_ref[...]
        c = jnp.mean(dyg * xhat, axis=-1, keepdims=True)
        dx = rs * (dyg - xhat * c)
        if has_add:
            dx = dx + add_ref[...]
        dx_ref[...] = dx
        part = jnp.sum(dyv * xhat, axis=0, keepdims=True)

        @pl.when(pl.program_id(0) == 0)
        def _():
            dg_ref[...] = part

        @pl.when(pl.program_id(0) > 0)
        def _():
            dg_ref[...] += part

    row = pl.BlockSpec((tr, d), lambda i: (i, 0))
    vec = pl.BlockSpec((1, d), lambda i: (0, 0))
    ins = [pl.BlockSpec((tr, d), lambda i: (i, x.col0)), vec, pl.BlockSpec((tr, 1), lambda i: (i, 0)), row]
    ins += [row] if has_add else []
    args = (x.arr, g, rstd, dy) + ((add,) if has_add else ())
    return _pcall(
        kern, name=name, grid=(r // tr,), in_specs=ins, out_specs=[row, vec],
        out_shape=[jax.ShapeDtypeStruct((r, d), F32), jax.ShapeDtypeStruct((1, d), F32)],
        compiler_params=pltpu.CompilerParams(dimension_semantics=("arbitrary",)),
    )(*args)


def _final_loss(x, proj, g, target):
    r, d = x.shape
    tr = _tile(r, 256)

    def kern(x_ref, p_ref, g_ref, t_ref, dy_ref, dg_ref, loss_ref):
        y = x_ref[...] + p_ref[...]
        rs = lax.rsqrt(jnp.mean(y * y, axis=-1, keepdims=True) + EPS)
        yhat = y * rs
        gv = g_ref[...]
        e = yhat * gv - t_ref[...]
        lpart = 0.5 * jnp.sum(jnp.mean(e * e, axis=-1, keepdims=True), axis=0, keepdims=True)
        dout = e * (1.0 / d)
        dyg = dout * gv
        c = jnp.mean(dyg * yhat, axis=-1, keepdims=True)
        dy_ref[...] = rs * (dyg - yhat * c)
        gpart = jnp.sum(dout * yhat, axis=0, keepdims=True)
        lrow = jnp.broadcast_to(lpart, (1, LANE))

        @pl.when(pl.program_id(0) == 0)
        def _():
            dg_ref[...] = gpart
            loss_ref[...] = lrow

        @pl.when(pl.program_id(0) > 0)
        def _():
            dg_ref[...] += gpart
            loss_ref[...] += lrow

    row = pl.BlockSpec((tr, d), lambda i: (i, 0))
    vec = pl.BlockSpec((1, d), lambda i: (0, 0))
    return _pcall(
        kern, name="final_loss", grid=(r // tr,), in_specs=[row, row, vec, row],
        out_specs=[row, vec, pl.BlockSpec((1, LANE), lambda i: (0, 0))],
        out_shape=[jax.ShapeDtypeStruct((r, d), F32), jax.ShapeDtypeStruct((1, d), F32),
                   jax.ShapeDtypeStruct((1, LANE), F32)],
        compiler_params=pltpu.CompilerParams(dimension_semantics=("arbitrary",)),
    )(x, proj, g, target)


def _rope_fwd(x, cs, sn, nh, width, off, name):
    s = x.arr.shape[0]
    tr = _tile(s, 512)

    def kern(x_ref, c_ref, s_ref, o_ref):
        cv, sv = c_ref[...], s_ref[...]
        for h in range(nh):
            b = h * width
            if off:
                o_ref[:, b:b + off] = x_ref[:, b:b + off].astype(BF16)
            xr = x_ref[:, b + off:b + off + LANE]
            o_ref[:, b + off:b + off + LANE] = (xr * cv + pltpu.roll(xr, 32, 1) * sv).astype(BF16)

    tab = pl.BlockSpec((tr, LANE), lambda i: (i, 0))
    return _pcall(
        kern, name=name, grid=(s // tr,),
        in_specs=[pl.BlockSpec((tr, nh * width), lambda i: (i, x.col0)), tab, tab],
        out_specs=pl.BlockSpec((tr, nh * width), lambda i: (i, 0)),
        out_shape=jax.ShapeDtypeStruct((s, nh * width), BF16),
    )(x.arr, cs, sn)


def _rope_grad(d, cv, sv):
    g2 = d * sv
    g2 = g2 + pltpu.roll(g2, 64, 1)
    lane = lax.broadcasted_iota(jnp.int32, d.shape, 1)
    return jnp.where(lane < 64, d * cv + pltpu.roll(g2, 32, 1), 0.0)


def _rope_bwd_q(dq, cs, sn):
    s, w = dq.shape
    tr = _tile(s, 512)
    nh = w // 256

    def kern(d_ref, c_ref, s_ref, o_ref):
        cv, sv = c_ref[...], s_ref[...]
        for h in range(nh):
            b = h * 256
            o_ref[:, b:b + LANE] = d_ref[:, b:b + LANE]
            o_ref[:, b + LANE:b + 256] = _rope_grad(d_ref[:, b + LANE:b + 256], cv, sv)

    row = pl.BlockSpec((tr, w), lambda i: (i, 0))
    tab = pl.BlockSpec((tr, LANE), lambda i: (i, 0))
    return _pcall(kern, name="rope_bwd_q", grid=(s // tr,), in_specs=[row, tab, tab], out_specs=row,
                  out_shape=jax.ShapeDtypeStruct((s, w), F32))(dq, cs, sn)


def _rope_bwd_k(dk_nope, dk_pe, dv, cs, sn):
    s, w = dk_nope.shape
    tr = _tile(s, 512)

    def kern(dk_ref, dp_ref, dv_ref, c_ref, s_ref, okv_ref, okr_ref):
        okv_ref[:, :w] = dk_ref[...]
        okv_ref[:, w:] = dv_ref[...]
        okr_ref[...] = _rope_grad(dp_ref[...], c_ref[...], s_ref[...])

    tab = pl.BlockSpec((tr, LANE), lambda i: (i, 0))
    wide = pl.BlockSpec((tr, w), lambda i: (i, 0))
    return _pcall(
        kern, name="rope_bwd_k", grid=(s // tr,), in_specs=[wide, tab, wide, tab, tab],
        out_specs=[pl.BlockSpec((tr, 2 * w), lambda i: (i, 0)), tab],
        out_shape=[jax.ShapeDtypeStruct((s, 2 * w), F32), jax.ShapeDtypeStruct((s, LANE), F32)],
    )(dk_nope, dk_pe, dv, cs, sn)


class _Attn:
    def __init__(self, mode, s, sk, heads, dk):
        self.mode, self.s, self.sk, self.h, self.dk = mode, s, sk, heads, dk
        self.scale = {"mla": 192 ** -0.5, "mem": 128 ** -0.5}.get(mode, NSA_DK ** -0.5)
        self.tb = min(256, s)
        self.nb = s // self.tb
        self.nsub = 2 if self.nb % 2 == 0 else 1
        self.tq = self.tb * self.nsub
        self.nq = s // self.tq
        self.causal = mode in ("mla", "slc")
        if self.causal:
            self.tk = self.tq
        elif mode == "win":
            self.tk = WIN + self.tb
        else:
            self.tk = sk
        self.tkb = min(512, sk)
        self.ksub = 2 if self.tkb == 512 and mode == "mla" else 1
        self.kb = self.tkb // self.ksub
        self.ncmp = s // CMP_STRIDE - 1

    def mask_bias(self, t, n, h, selx, diag):
        m = self.mode
        if m == "mla":
            return (n <= t) if diag else None, None
        if m == "mem":
            return None, None
        slope = jnp.where(h == 0, 0.25, jnp.where(h == 1, 0.0625, jnp.where(h == 2, 0.015625, 0.00390625)))
        slope = slope.astype(F32) * LOG2E
        if m == "cmp":
            mask = (n * CMP_STRIDE + (CMP_LEN - 1) <= t) & (n < self.ncmp)
            pos = n.astype(F32) * float(CMP_STRIDE) + (CMP_LEN - 1) / 2.0
            return mask, slope * pos
        rel = t - n
        if m == "slc":
            return (rel >= 0) if diag else None, slope * n.astype(F32)
        return (rel >= 0) & (rel < WIN), slope * n.astype(F32)


def _scores(cfg, s_raw, t, n, h, selx, diag, lse=None):
    s = s_raw * (cfg.scale * LOG2E)
    mask, key_term = cfg.mask_bias(t, n, h, selx, diag)
    if key_term is not None:
        s = s + key_term
    if selx is not None:
        s = s + selx
    if lse is None:
        if mask is not None:
            s = jnp.where(mask, s, NEG)
        return s, mask
    p = jnp.exp2(jnp.minimum(s - lse, 0.0))
    if mask is not None:
        p = jnp.where(mask, p, 0.0)
    return p, mask


def _block_of_key(k0, tk, keys_on_rows, value=NEG):
    shape = (tk, LANE) if keys_on_rows else (LANE, tk)
    n = lax.broadcasted_iota(jnp.int32, shape, 0 if keys_on_rows else 1) + k0
    j = lax.broadcasted_iota(jnp.int32, shape, 1 if keys_on_rows else 0)
    return jnp.where((n >> 6) == j, value, 0.0).astype(BF16)


def _to_row(col):
    t = col.shape[0]
    return jnp.transpose(jnp.broadcast_to(col, (t, LANE)))[0:1, :]


def _load_keys(k_refs, rows):
    parts = [r[rows, :].astype(BF16) for r in k_refs]
    return parts[0] if len(parts) == 1 else jnp.concatenate(parts, axis=1)


def _attn_fwd(cfg, q, ks, v, sel, name):
    s, tq, tk, tb, nsub = cfg.s, cfg.tq, cfg.tk, cfg.tb, cfg.nsub
    has_sel = sel is not None
    nkp = len(ks)

    def kern(*refs):
        q_ref, k_refs, v_ref = refs[0], refs[1:1 + nkp], refs[1 + nkp]
        sel_ref = refs[2 + nkp] if has_sel else None
        o_ref, lc_ref, lr_ref = refs[2 + nkp + has_sel:5 + nkp + has_sel]
        h, i = pl.program_id(0), pl.program_id(1)
        part = [slice(r * tb, (r + 1) * tb) for r in range(nsub)]
        qs = [q_ref[p, :].astype(BF16) for p in part]
        ts = [i * tq + r * tb + lax.broadcasted_iota(jnp.int32, (tb, 1), 0) for r in range(nsub)]
        sels = [sel_ref[p, :].astype(BF16) for p in part] if has_sel else None

        def load(k0):
            rows = pl.ds(k0, tk)
            return _load_keys(k_refs, rows), v_ref[rows, :].astype(BF16)

        def soft(r, k0, s_raw, vv, emat, carry, diag):
            m, l, acc = carry
            n = k0 + lax.broadcasted_iota(jnp.int32, (1, tk), 1)
            selx = _nn(sels[r], emat) if has_sel else None
            sc, mask = _scores(cfg, s_raw, ts[r], n, h, selx, diag)
            m_new = jnp.maximum(m, jnp.max(sc, axis=1, keepdims=True))
            alpha = jnp.exp2(m - m_new)
            p = jnp.exp2(sc - m_new)
            if mask is not None:
                p = jnp.where(mask, p, 0.0)
            l = alpha * l + jnp.sum(p, axis=1, keepdims=True)
            acc = alpha * acc + _nn(p.astype(BF16), vv)
            return m_new, l, acc

        def step(r, k0, kk, vv, emat, carry, diag):
            return soft(r, k0, _nt(qs[r], kk), vv, emat, carry, diag)

        def chunk(k0, carry, diag):
            kk, vv = load(k0)
            emat = _block_of_key(k0, tk, False) if has_sel else None
            return tuple(step(r, k0, kk, vv, emat, carry[r], diag) for r in range(nsub))

        init = (jnp.full((tb, 1), NEG, F32), jnp.zeros((tb, 1), F32), jnp.zeros((tb, HEAD_V), F32))
        carry = (init,) * nsub
        if cfg.causal:
            buf_a, buf_b = refs[-2:]
            full = (i * tq) // tk

            def scores_into(buf, c):
                kk = _load_keys(k_refs, pl.ds(pl.multiple_of(c * tk, tk), tk))
                for r in range(nsub):
                    buf[r] = _nt(qs[r], kk)

            def consume(buf, c, cr, diag):
                k0 = pl.multiple_of(c * tk, tk)
                vv = v_ref[pl.ds(k0, tk), :].astype(BF16)
                emat = _block_of_key(k0, tk, False) if has_sel else None
                return tuple(soft(r, k0, buf[r], vv, emat, cr[r], diag) for r in range(nsub))

            def pair(p, cr):
                scores_into(buf_b, 2 * p + 1)
                cr = consume(buf_a, 2 * p, cr, False)
                scores_into(buf_a, 2 * p + 2)
                return consume(buf_b, 2 * p + 1, cr, False)

            def odd_tail(cr):
                scores_into(buf_b, full)
                cr = consume(buf_a, full - 1, cr, False)
                return consume(buf_b, full, cr, True)

            scores_into(buf_a, 0)
            carry = lax.fori_loop(0, full // 2, pair, carry)
            carry = lax.cond(full % 2 == 1, odd_tail, lambda cr: consume(buf_a, full, cr, True), carry)
        elif cfg.mode == "win":
            starts = [pl.multiple_of(jnp.maximum(i * tq + r * tb - WIN, 0), tb) for r in range(nsub)]
            carry = tuple(step(r, k0, *load(k0), None, carry[r], True) for r, k0 in enumerate(starts))
        else:
            carry = chunk(0, carry, True)
        for r, (m, l, acc) in enumerate(carry):
            o_ref[part[r], :] = acc / (l + 1e-20)
            lse = m + jnp.log(l + 1e-20) * LOG2E
            lc_ref[0, part[r], :] = lse
            lr_ref[0, r] = _to_row(lse)

    ins = [pl.BlockSpec((tq, q.width), lambda h, i: (i, q.col(h)))]
    ins += [pl.BlockSpec((cfg.sk, p.width), lambda h, i, p=p: (0, p.col(h))) for p in ks]
    ins += [pl.BlockSpec((cfg.sk, HEAD_V), lambda h, i: (0, v.col(h)))]
    args = [q.arr] + [p.arr for p in ks] + [v.arr]
    if has_sel:
        ins.append(pl.BlockSpec((tq, LANE), lambda h, i: (i, 0)))
        args.append(sel)
    return _pcall(
        kern, name=name, grid=(cfg.h, cfg.nq), in_specs=ins,
        out_specs=[pl.BlockSpec((tq, HEAD_V), lambda h, i: (i, h)),
                   pl.BlockSpec((1, tq, 1), lambda h, i: (h, i, 0)),
                   pl.BlockSpec((1, nsub, 1, tb), lambda h, i: (h, i, 0, 0))],
        out_shape=[jax.ShapeDtypeStruct((s, cfg.h * HEAD_V), F32),
                   jax.ShapeDtypeStruct((cfg.h, s, 1), F32),
                   jax.ShapeDtypeStruct((cfg.h, cfg.nb, 1, tb), F32)],
        scratch_shapes=[pltpu.VMEM((nsub, tb, tk), F32)] * 2 if cfg.causal else [],
        compiler_params=pltpu.CompilerParams(dimension_semantics=("parallel", "parallel")),
    )(*args)


def _attn_dq(cfg, q, ks, v, sel, o, lse, do, dq_in, name):
    s, tq, tk, dk, tb, nsub = cfg.s, cfg.tq, cfg.tk, cfg.dk, cfg.tb, cfg.nsub
    has_sel = sel is not None
    has_in = dq_in is not None
    nkp = len(ks)

    def kern(*refs):
        refs = list(refs)
        q_ref, k_refs, v_ref = refs[0], refs[1:1 + nkp], refs[1 + nkp]
        p0 = 2 + nkp
        sel_ref = refs[p0] if has_sel else None
        p0 += has_sel
        o_ref, l_ref, do_ref = refs[p0:p0 + 3]
        p0 += 3
        in_ref = refs[p0] if has_in else None
        p0 += has_in
        dq_ref, dr_ref = refs[p0:p0 + 2]
        h, i = pl.program_id(0), pl.program_id(1)
        part = [slice(r * tb, (r + 1) * tb) for r in range(nsub)]
        qs = [q_ref[p, :].astype(BF16) for p in part]
        ts = [i * tq + r * tb + lax.broadcasted_iota(jnp.int32, (tb, 1), 0) for r in range(nsub)]
        sels = [sel_ref[p, :].astype(BF16) for p in part] if has_sel else None
        dvecs, dobs, lses = [], [], []
        for r, p in enumerate(part):
            dov = do_ref[p, :]
            dvec = jnp.sum(dov * o_ref[p, :], axis=1, keepdims=True)
            dr_ref[0, r] = _to_row(dvec)
            dvecs.append(dvec)
            dobs.append(dov.astype(BF16))
            lses.append(l_ref[0, p, :])

        def load(k0):
            rows = pl.ds(k0, tk)
            return _load_keys(k_refs, rows), v_ref[rows, :].astype(BF16)

        def grad(r, k0, s_raw, dp, kk, emat, acc, diag):
            n = k0 + lax.broadcasted_iota(jnp.int32, (1, tk), 1)
            selx = _nn(sels[r], emat) if has_sel else None
            p, _ = _scores(cfg, s_raw, ts[r], n, h, selx, diag, lses[r])
            ds = p * (dp - dvecs[r])
            return acc + _nn(ds.astype(BF16), kk)

        def step(r, k0, kk, vv, emat, acc, diag):
            return grad(r, k0, _nt(qs[r], kk), _nt(dobs[r], vv), kk, emat, acc, diag)

        def chunk(k0, accs, diag):
            kk, vv = load(k0)
            emat = _block_of_key(k0, tk, False) if has_sel else None
            return tuple(step(r, k0, kk, vv, emat, accs[r], diag) for r in range(nsub))

        accs = (jnp.zeros((tb, dk), F32),) * nsub
        if cfg.causal:
            sa, pa, sb, pb = refs[-4:]
            full = (i * tq) // tk

            def products_into(sbuf, pbuf, c):
                kk, vv = load(pl.multiple_of(c * tk, tk))
                for r in range(nsub):
                    sbuf[r] = _nt(qs[r], kk)
                    pbuf[r] = _nt(dobs[r], vv)

            def consume(sbuf, pbuf, c, ac, diag):
                k0 = pl.multiple_of(c * tk, tk)
                kk = _load_keys(k_refs, pl.ds(k0, tk))
                emat = _block_of_key(k0, tk, False) if has_sel else None
                return tuple(grad(r, k0, sbuf[r], pbuf[r], kk, emat, ac[r], diag) for r in range(nsub))

            def pair(p, ac):
                products_into(sb, pb, 2 * p + 1)
                ac = consume(sa, pa, 2 * p, ac, False)
                products_into(sa, pa, 2 * p + 2)
                return consume(sb, pb, 2 * p + 1, ac, False)

            def odd_tail(ac):
                products_into(sb, pb, full)
                ac = consume(sa, pa, full - 1, ac, False)
                return consume(sb, pb, full, ac, True)

            products_into(sa, pa, 0)
            accs = lax.fori_loop(0, full // 2, pair, accs)
            accs = lax.cond(full % 2 == 1, odd_tail, lambda ac: consume(sa, pa, full, ac, True), accs)
        elif cfg.mode == "win":
            starts = [pl.multiple_of(jnp.maximum(i * tq + r * tb - WIN, 0), tb) for r in range(nsub)]
            accs = tuple(step(r, k0, *load(k0), None, accs[r], True) for r, k0 in enumerate(starts))
        else:
            accs = chunk(0, accs, True)
        for r, p in enumerate(part):
            dq_ref[p, :] = accs[r] * cfg.scale + in_ref[p, :] if has_in else accs[r] * cfg.scale

    qs = pl.BlockSpec((tq, dk), lambda h, i: (i, h))
    ins = [pl.BlockSpec((tq, q.width), lambda h, i: (i, q.col(h)))]
    ins += [pl.BlockSpec((cfg.sk, p.width), lambda h, i, p=p: (0, p.col(h))) for p in ks]
    ins += [pl.BlockSpec((cfg.sk, HEAD_V), lambda h, i: (0, v.col(h)))]
    args = [q.arr] + [p.arr for p in ks] + [v.arr]
    if has_sel:
        ins.append(pl.BlockSpec((tq, LANE), lambda h, i: (i, 0)))
        args.append(sel)
    ins += [pl.BlockSpec((tq, HEAD_V), lambda h, i: (i, o.col(h))),
            pl.BlockSpec((1, tq, 1), lambda h, i: (h, i, 0)),
            pl.BlockSpec((tq, HEAD_V), lambda h, i: (i, do.col(h)))]
    args += [o.arr, lse, do.arr]
    if has_in:
        ins.append(qs)
        args.append(dq_in)
    return _pcall(
        kern, name=name, grid=(cfg.h, cfg.nq), in_specs=ins,
        out_specs=[qs, pl.BlockSpec((1, nsub, 1, tb), lambda h, i: (h, i, 0, 0))],
        out_shape=[jax.ShapeDtypeStruct((s, cfg.h * dk), F32),
                   jax.ShapeDtypeStruct((cfg.h, cfg.nb, 1, tb), F32)],
        scratch_shapes=[pltpu.VMEM((nsub, tb, tk), F32)] * 4 if cfg.causal else [],
        compiler_params=pltpu.CompilerParams(dimension_semantics=("parallel", "parallel")),
    )(*args)


def _attn_dkv(cfg, q, ks, v, selt, lse_r, d_r, do, name):
    s, tq, tkb, dk, kb, ksub = cfg.s, cfg.tb, cfg.tkb, cfg.dk, cfg.kb, cfg.ksub
    nq = cfg.nb
    has_sel = selt is not None
    nkp = len(ks)
    outs = list(ks) + [v]

    def kern(*refs):
        k_refs, v_ref = refs[:nkp], refs[nkp]
        q_ref, do_ref, lr_ref, dr_ref = refs[nkp + 1:nkp + 5]
        st_ref = refs[nkp + 5] if has_sel else None
        out_refs = refs[nkp + 5 + has_sel:2 * nkp + 6 + has_sel]
        sa, pa, sb, pb = refs[-4:]
        j, h = pl.program_id(0), pl.program_id(1)
        k0 = j * tkb
        part = [slice(u * kb, (u + 1) * kb) for u in range(ksub)]
        kks = [_load_keys(k_refs, p) for p in part]
        vvs = [v_ref[p, :].astype(BF16) for p in part]
        ns = [k0 + u * kb + lax.broadcasted_iota(jnp.int32, (kb, 1), 0) for u in range(ksub)]
        emats = [_block_of_key(k0 + u * kb, kb, True) for u in range(ksub)] if has_sel else None

        def load_q(i):
            rows = pl.ds(pl.multiple_of(i * tq, tq), tq)
            return q_ref[rows, :].astype(BF16), do_ref[rows, :].astype(BF16)

        def products_into(sbuf, pbuf, i):
            qi, doi = load_q(i)
            for u in range(ksub):
                sbuf[u] = _nt(kks[u], qi)
                pbuf[u] = _nt(vvs[u], doi)

        def consume(sbuf, pbuf, i, carry):
            qi, doi = load_q(i)
            t = i * tq + lax.broadcasted_iota(jnp.int32, (1, tq), 1)
            selt_i = st_ref[i].astype(BF16) if has_sel else None
            new = []
            for u in range(ksub):
                dk_acc, dv_acc = carry[u]
                selx = _nn(emats[u], selt_i) if has_sel else None
                pt, _ = _scores(cfg, sbuf[u], t, ns[u], h, selx, True, lr_ref[0, i])
                dv_acc = dv_acc + _nn(pt.astype(BF16), doi)
                dst = pt * (pbuf[u] - dr_ref[0, i])
                new.append((dk_acc + _nn(dst.astype(BF16), qi), dv_acc))
            return tuple(new)

        if cfg.causal:
            first, count = k0 // tq, nq - k0 // tq
        elif cfg.mode == "win":
            first = k0 // tq
            count = jnp.minimum((k0 + tkb + WIN - 2) // tq + 1, nq) - first
        else:
            first, count = 0, nq

        def pair(p, cr):
            i0 = first + 2 * p
            products_into(sb, pb, i0 + 1)
            cr = consume(sa, pa, i0, cr)
            products_into(sa, pa, i0 + 2)
            return consume(sb, pb, i0 + 1, cr)

        carry = ((jnp.zeros((kb, dk), F32), jnp.zeros((kb, HEAD_V), F32)),) * ksub
        products_into(sa, pa, first)
        carry = lax.fori_loop(0, count // 2 - 1, pair, carry)
        last = first + count - 2
        products_into(sb, pb, last + 1)
        carry = consume(sa, pa, last, carry)
        carry = consume(sb, pb, last + 1, carry)
        for u, (dk_acc, dv_acc) in enumerate(carry):
            vals, off = [], 0
            for p in ks:
                vals.append(dk_acc[:, off:off + p.width] * cfg.scale)
                off += p.width
            vals.append(dv_acc)
            for src, ref, val in zip(outs, out_refs, vals):
                if src.per_head:
                    ref[part[u], :] = val
                else:
                    @pl.when(h == 0)
                    def _(ref=ref, val=val, u=u):
                        ref[part[u], :] = val

                    @pl.when(h > 0)
                    def _(ref=ref, val=val, u=u):
                        ref[part[u], :] += val

    rowv = pl.BlockSpec((1, nq, 1, tq), lambda j, h: (h, 0, 0, 0))
    ins = [pl.BlockSpec((tkb, p.width), lambda j, h, p=p: (j, p.col(h))) for p in ks]
    ins += [pl.BlockSpec((tkb, HEAD_V), lambda j, h: (j, v.col(h))),
            pl.BlockSpec((s, q.width), lambda j, h: (0, q.col(h))),
            pl.BlockSpec((s, HEAD_V), lambda j, h: (0, do.col(h))), rowv, rowv]
    args = [p.arr for p in ks] + [v.arr, q.arr, do.arr, lse_r, d_r]
    if has_sel:
        ins.append(pl.BlockSpec((nq, LANE, tq), lambda j, h: (0, 0, 0)))
        args.append(selt)
    out_specs = [pl.BlockSpec((tkb, p.width), lambda j, h, p=p: (j, h if p.per_head else 0)) for p in outs]
    out_shape = [jax.ShapeDtypeStruct((cfg.sk, (cfg.h if p.per_head else 1) * p.width), F32) for p in outs]
    assert nq % 2 == 0 and (cfg.mode in ("cmp", "mem") or tkb % (2 * tq) == 0), (nq, tkb, tq)
    return _pcall(
        kern, name=name, grid=(cfg.sk // tkb, cfg.h), in_specs=ins, out_specs=out_specs, out_shape=out_shape,
        scratch_shapes=[pltpu.VMEM((ksub, kb, tq), F32)] * 4,
        compiler_params=pltpu.CompilerParams(dimension_semantics=("parallel", "arbitrary")),
    )(*args)


def _all_heads(cfg, src, rows, key=False):
    if src.per_head:
        assert src.col0 % cfg.h == 0
        width, col = cfg.h * src.width, src.col0 // cfg.h
    else:
        width, col = src.width, src.col0
    return pl.BlockSpec((rows, width), (lambda i: (0, col)) if key else (lambda i: (i, col)))


def _head_cols(src, hh):
    return slice(hh * src.width, (hh + 1) * src.width) if src.per_head else slice(None)


def _key_window(cfg, i, r):
    if cfg.mode == "win":
        return pl.ds(pl.multiple_of(jnp.maximum(i * cfg.tq + r * cfg.tb - WIN, 0), cfg.tb), cfg.tk)
    return pl.ds(0, cfg.tk)


def _attn_fwd_small(cfg, q, ks, v, name, overlap=None):
    s, tq, tk, tb, nsub, nh = cfg.s, cfg.tq, cfg.tk, cfg.tb, cfg.nsub, cfg.h
    nkp = len(ks)
    select = overlap is not None
    n_s = s // SLC_LEN
    top_n = min(SLC_TOPN, n_s)

    def kern(*refs):
        q_ref, k_refs, v_ref = refs[0], refs[1:1 + nkp], refs[1 + nkp]
        ov_ref = refs[2 + nkp] if select else None
        o_ref, lc_ref, lr_ref = refs[2 + nkp + select:5 + nkp + select]
        i = pl.program_id(0)
        imps = [jnp.zeros((tb, LANE), F32)] * nsub
        for r in range(nsub):
            rows = slice(r * tb, (r + 1) * tb)
            t = i * tq + r * tb + lax.broadcasted_iota(jnp.int32, (tb, 1), 0)
            win = _key_window(cfg, i, r)
            n = win.start + lax.broadcasted_iota(jnp.int32, (1, tk), 1)
            for hh in range(nh):
                qv = q_ref[rows, hh * cfg.dk:(hh + 1) * cfg.dk].astype(BF16)
                kk = _load_keys([kr.at[:, _head_cols(p, hh)] for kr, p in zip(k_refs, ks)], win)
                vv = v_ref[win, _head_cols(v, hh)].astype(BF16)
                sc, mask = _scores(cfg, _nt(qv, kk), t, n, hh, None, True)
                m = jnp.max(sc, axis=1, keepdims=True)
                e = jnp.exp2(sc - m)
                if mask is not None:
                    e = jnp.where(mask, e, 0.0)
                l = jnp.sum(e, axis=1, keepdims=True)
                o_ref[rows, hh * HEAD_V:(hh + 1) * HEAD_V] = _nn(e.astype(BF16), vv) / (l + 1e-20)
                lse = m + jnp.log(l + 1e-20) * LOG2E
                lc_ref[hh, rows, :] = lse
                lr_ref[hh, r] = _to_row(lse)
                if select:
                    imps[r] = imps[r] + _nn((e / (l + 1e-20)).astype(BF16), ov_ref[...])
        if select:
            sel_ref, selt_ref, imp_t = refs[5 + nkp + select:8 + nkp + select]
            for r in range(nsub):
                t = i * tq + r * tb + lax.broadcasted_iota(jnp.int32, (tb, 1), 0)
                j = lax.broadcasted_iota(jnp.int32, (tb, LANE), 1)
                cur = t >> 6
                imp = jnp.where((j == 0) | (j == cur) | (j == cur - 1), 1e9, imps[r])
                imp = jnp.where(j > cur, -1e9, imp)
                imp_t[r] = jnp.transpose(imp)
                mine = imp_t[r, 0:n_s, :]
                jrow = lax.broadcasted_iota(jnp.int32, (n_s, tb), 0)

                def count(k, rank):
                    other = imp_t[r, pl.ds(k, 1), :]
                    ahead = (other > mine) | ((other == mine) & (k < jrow))
                    return rank + jnp.where(ahead, 1.0, 0.0)

                rank = lax.fori_loop(0, n_s, count, jnp.zeros((n_s, tb), F32))
                cur_t = (i * tq + r * tb + lax.broadcasted_iota(jnp.int32, (1, tb), 1)) >> 6
                rejected = jnp.where((rank < top_n) & (jrow <= cur_t), 0.0, 1.0)
                if n_s < LANE:
                    rejected = jnp.concatenate([rejected, jnp.ones((LANE - n_s, tb), F32)], axis=0)
                selt_ref[r] = rejected
                sel_ref[r * tb:(r + 1) * tb, :] = jnp.transpose(rejected)

    ins = [_all_heads(cfg, q, tq)] + [_all_heads(cfg, p, cfg.sk, True) for p in ks]
    ins += [_all_heads(cfg, v, cfg.sk, True)]
    args = [q.arr] + [p.arr for p in ks] + [v.arr]
    out_specs = [pl.BlockSpec((tq, nh * HEAD_V), lambda i: (i, 0)),
                 pl.BlockSpec((nh, tq, 1), lambda i: (0, i, 0)),
                 pl.BlockSpec((nh, nsub, 1, tb), lambda i: (0, i, 0, 0))]
    out_shape = [jax.ShapeDtypeStruct((s, nh * HEAD_V), F32), jax.ShapeDtypeStruct((nh, s, 1), F32),
                 jax.ShapeDtypeStruct((nh, cfg.nb, 1, tb), F32)]
    scratch = []
    if select:
        ins.append(pl.BlockSpec((cfg.sk, LANE), lambda i: (0, 0)))
        args.append(overlap)
        out_specs += [pl.BlockSpec((tq, LANE), lambda i: (i, 0)), pl.BlockSpec((nsub, LANE, tb), lambda i: (i, 0, 0))]
        out_shape += [jax.ShapeDtypeStruct((s, LANE), F32), jax.ShapeDtypeStruct((cfg.nb, LANE, tb), F32)]
        scratch = [pltpu.VMEM((nsub, LANE, tb), F32)]
    return _pcall(kern, name=name, grid=(cfg.nq,), in_specs=ins, out_specs=out_specs, out_shape=out_shape,
                  scratch_shapes=scratch,
                  compiler_params=pltpu.CompilerParams(dimension_semantics=("parallel",)))(*args)


def _attn_dq_small(cfg, q, ks, v, o, lse, do, dq_in, name):
    s, tq, tk, tb, nsub, nh, dk = cfg.s, cfg.tq, cfg.tk, cfg.tb, cfg.nsub, cfg.h, cfg.dk
    nkp = len(ks)
    has_in = dq_in is not None

    def kern(*refs):
        q_ref, k_refs, v_ref = refs[0], refs[1:1 + nkp], refs[1 + nkp]
        o_ref, l_ref, do_ref = refs[2 + nkp:5 + nkp]
        in_ref = refs[5 + nkp] if has_in else None
        dq_ref, dr_ref = refs[5 + nkp + has_in:7 + nkp + has_in]
        i = pl.program_id(0)
        for r in range(nsub):
            rows = slice(r * tb, (r + 1) * tb)
            t = i * tq + r * tb + lax.broadcasted_iota(jnp.int32, (tb, 1), 0)
            win = _key_window(cfg, i, r)
            n = win.start + lax.broadcasted_iota(jnp.int32, (1, tk), 1)
            for hh in range(nh):
                vcols = slice(hh * HEAD_V, (hh + 1) * HEAD_V)
                qcols = slice(hh * dk, (hh + 1) * dk)
                qv = q_ref[rows, qcols].astype(BF16)
                kk = _load_keys([kr.at[:, _head_cols(p, hh)] for kr, p in zip(k_refs, ks)], win)
                vv = v_ref[win, _head_cols(v, hh)].astype(BF16)
                dov = do_ref[rows, vcols]
                dvec = jnp.sum(dov * o_ref[rows, vcols], axis=1, keepdims=True)
                dr_ref[hh, r] = _to_row(dvec)
                p, _ = _scores(cfg, _nt(qv, kk), t, n, hh, None, True, l_ref[hh, rows, :])
                ds = p * (_nt(dov.astype(BF16), vv) - dvec)
                dq = _nn(ds.astype(BF16), kk) * cfg.scale
                dq_ref[rows, qcols] = dq + in_ref[rows, qcols] if has_in else dq

    qs = pl.BlockSpec((tq, nh * dk), lambda i: (i, 0))
    ins = [_all_heads(cfg, q, tq)] + [_all_heads(cfg, p, cfg.sk, True) for p in ks]
    ins += [_all_heads(cfg, v, cfg.sk, True)]
    ins += [_all_heads(cfg, o, tq), pl.BlockSpec((nh, tq, 1), lambda i: (0, i, 0)), _all_heads(cfg, do, tq)]
    args = [q.arr] + [p.arr for p in ks] + [v.arr, o.arr, lse, do.arr]
    if has_in:
        ins.append(qs)
        args.append(dq_in)
    return _pcall(
        kern, name=name, grid=(cfg.nq,), in_specs=ins,
        out_specs=[qs, pl.BlockSpec((nh, nsub, 1, tb), lambda i: (0, i, 0, 0))],
        out_shape=[jax.ShapeDtypeStruct((s, nh * dk), F32), jax.ShapeDtypeStruct((nh, cfg.nb, 1, tb), F32)],
        compiler_params=pltpu.CompilerParams(dimension_semantics=("parallel",)))(*args)


def _attn_bwd(cfg, q, ks, v, sel, selt, o, lse, lse_r, do, dq_in, name):
    if not cfg.causal:
        dq, d_r = _attn_dq_small(cfg, q, ks, v, o, lse, do, dq_in, name + "_dq")
        res = _attn_dkv(cfg, q, ks, v, selt, lse_r, d_r, do, name + "_dkv")
        return dq, res[:-1], res[-1]
    dq, d_r = _attn_dq(cfg, q, ks, v, sel, o, lse, do, dq_in, name + "_dq")
    res = _attn_dkv(cfg, q, ks, v, selt, lse_r, d_r, do, name + "_dkv")
    return dq, res[:-1], res[-1]


def _silu_grad(pre):
    sg = _sigmoid(pre)
    return sg * (1.0 + pre * (1.0 - sg))


def _compress_fwd(a_lo, a_hi, pe_lo, pe_hi, w1_lo, w1_hi, w2, name):
    n, dp = a_lo.shape[0], w2.shape[1]

    def kern(alo, ahi, plo, phi, w1l, w1h, w2r, out_ref, pre_ref):
        xl = (alo[...] + plo[...]).astype(BF16)
        xh = (ahi[...] + phi[...]).astype(BF16)
        pre = _nn(xl, w1l[...]) + _nn(xh, w1h[...])
        act = pre * _sigmoid(pre)
        out_ref[...] = _nn(act.astype(BF16), w2r[...]).astype(BF16)
        pre_ref[...] = pre

    return _pcall(kern, name=name,
                  out_shape=[jax.ShapeDtypeStruct((n, dp), BF16), jax.ShapeDtypeStruct((n, dp), F32)],
                  )(a_lo, a_hi, pe_lo, pe_hi, w1_lo, w1_hi, w2)


def _compress_bwd(a_lo, a_hi, pe_lo, pe_hi, w1_lo, w1_hi, w2, pre, pre_sh, dout, dout_sh, name):
    n, ln = a_lo.shape
    dp = w2.shape[1]

    def kern(alo, ahi, plo, phi, w1l, w1h, w2r, pre_ref, presh_ref, do_ref, dosh_ref,
             da_ref, dpl_ref, dph_ref, dw1l_ref, dw1h_ref, dw2_ref):
        prev = pre_ref[...]
        act = prev * _sigmoid(prev)
        dob = do_ref[...].astype(BF16)
        w2v = w2r[...]
        dpre = (_nt(dob, w2v) * _silu_grad(prev)).astype(BF16)
        dpre_sh = (_nt(dosh_ref[...].astype(BF16), w2v) * _silu_grad(presh_ref[...])).astype(BF16)
        dw2_ref[...] = _nn(act.T.astype(BF16), dob)
        xl = alo[...] + plo[...]
        xh = ahi[...] + phi[...]
        dw1l_ref[...] = _nn(xl.T.astype(BF16), dpre)
        dw1h_ref[...] = _nn(xh.T.astype(BF16), dpre)
        dal = _nt(dpre, w1l[...])
        dah_sh = _nt(dpre_sh, w1h[...])
        da_ref[...] = dal + dah_sh
        dpl_ref[...] = jnp.sum(dal, axis=0, keepdims=True)
        dph_ref[...] = jnp.sum(dah_sh, axis=0, keepdims=True)

    return _pcall(
        kern, name=name,
        out_shape=[jax.ShapeDtypeStruct((n, ln), F32), jax.ShapeDtypeStruct((1, ln), F32),
                   jax.ShapeDtypeStruct((1, ln), F32), jax.ShapeDtypeStruct((ln, dp), F32),
                   jax.ShapeDtypeStruct((ln, dp), F32), jax.ShapeDtypeStruct((dp, dp), F32)],
    )(a_lo, a_hi, pe_lo, pe_hi, w1_lo, w1_hi, w2, pre, pre_sh, dout, dout_sh)


def _nsa_combine(o_cmp, o_slc, o_win, gl):
    s, w = o_cmp.shape
    tr = _tile(s, 512)

    def kern(a_ref, b_ref, c_ref, g_ref, o_ref):
        g = _sigmoid(g_ref[...])
        for h in range(NSA_HEADS):
            cs = slice(h * HEAD_V, (h + 1) * HEAD_V)
            o_ref[:, cs] = (g[:, 3 * h:3 * h + 1] * a_ref[:, cs] + g[:, 3 * h + 1:3 * h + 2] * b_ref[:, cs]
                            + g[:, 3 * h + 2:3 * h + 3] * c_ref[:, cs])

    row = pl.BlockSpec((tr, w), lambda i: (i, 0))
    return _pcall(kern, name="nsa_combine", grid=(s // tr,),
                  in_specs=[row, row, row, pl.BlockSpec((tr, LANE), lambda i: (i, gl.col0))], out_specs=row,
                  out_shape=jax.ShapeDtypeStruct((s, w), F32))(o_cmp, o_slc, o_win, gl.arr)


def _nsa_combine_bwd(do_cat, o_cmp, o_slc, o_win, gl):
    s, w = o_cmp.shape
    tr = _tile(s, 512)

    def kern(d_ref, a_ref, b_ref, c_ref, g_ref, da_ref, db_ref, dc_ref, dg_ref):
        g = _sigmoid(g_ref[...])
        lane = lax.broadcasted_iota(jnp.int32, (tr, LANE), 1)
        dgl = jnp.zeros((tr, LANE), F32)
        for h in range(NSA_HEADS):
            cs = slice(h * HEAD_V, (h + 1) * HEAD_V)
            dv = d_ref[:, cs]
            for b, (src, dst) in enumerate(((a_ref, da_ref), (b_ref, db_ref), (c_ref, dc_ref))):
                gate = g[:, 3 * h + b:3 * h + b + 1]
                dst[:, cs] = gate * dv
                dgate = jnp.sum(dv * src[:, cs], axis=1, keepdims=True)
                dgl = jnp.where(lane == 3 * h + b, dgate * gate * (1.0 - gate), dgl)
        dg_ref[...] = dgl

    row = pl.BlockSpec((tr, w), lambda i: (i, 0))
    tab = pl.BlockSpec((tr, LANE), lambda i: (i, 0))
    return _pcall(kern, name="nsa_combine_bwd", grid=(s // tr,),
                  in_specs=[pl.BlockSpec((tr, w), lambda i: (i, 2)), row, row, row,
                            pl.BlockSpec((tr, LANE), lambda i: (i, gl.col0))],
                  out_specs=[row, row, row, tab],
                  out_shape=[jax.ShapeDtypeStruct((s, w), F32)] * 3 + [jax.ShapeDtypeStruct((s, LANE), F32)],
                  )(do_cat, o_cmp, o_slc, o_win, gl.arr)


def _gate_fwd(o_mla, o_nsa, o_mem, hp):
    s = o_mla.shape[0]
    tr = _tile(s, 256)

    def kern(a_ref, b_ref, c_ref, z_ref, u_ref):
        z = z_ref[...]
        sz = z * _sigmoid(z)
        u_ref[:, 0:1024] = (a_ref[...] * sz[:, 0:1024]).astype(BF16)
        u_ref[:, 1024:1536] = (b_ref[...] * sz[:, 1024:1536]).astype(BF16)
        u_ref[:, 1536:2048] = (c_ref[...] * sz[:, 1536:2048]).astype(BF16)

    return _pcall(
        kern, name="gate_fwd", grid=(s // tr,),
        in_specs=[pl.BlockSpec((tr, 1024), lambda i: (i, 0)), pl.BlockSpec((tr, 512), lambda i: (i, 0)),
                  pl.BlockSpec((tr, 512), lambda i: (i, 0)), pl.BlockSpec((tr, 2048), lambda i: (i, 2))],
        out_specs=pl.BlockSpec((tr, 2048), lambda i: (i, 0)),
        out_shape=jax.ShapeDtypeStruct((s, 2048), BF16))(o_mla, o_nsa, o_mem, hp)


def _gate_bwd(du, o_mla, o_nsa, o_mem, hp):
    s = du.shape[0]
    tr = _tile(s, 256)

    def kern(d_ref, a_ref, b_ref, c_ref, z_ref, do_ref, dz_ref):
        z = z_ref[...]
        sg = _sigmoid(z)
        sz = z * sg
        dsz = sg * (1.0 + z * (1.0 - sg))
        d = d_ref[...]
        do_ref[...] = d * sz
        dz_ref[:, 0:1024] = d[:, 0:1024] * a_ref[...] * dsz[:, 0:1024]
        dz_ref[:, 1024:1536] = d[:, 1024:1536] * b_ref[...] * dsz[:, 1024:1536]
        dz_ref[:, 1536:2048] = d[:, 1536:2048] * c_ref[...] * dsz[:, 1536:2048]

    wide = pl.BlockSpec((tr, 2048), lambda i: (i, 0))
    return _pcall(
        kern, name="gate_bwd", grid=(s // tr,),
        in_specs=[wide, pl.BlockSpec((tr, 1024), lambda i: (i, 0)), pl.BlockSpec((tr, 512), lambda i: (i, 0)),
                  pl.BlockSpec((tr, 512), lambda i: (i, 0)), pl.BlockSpec((tr, 2048), lambda i: (i, 2))],
        out_specs=[wide, wide],
        out_shape=[jax.ShapeDtypeStruct((s, 2048), F32)] * 2)(du, o_mla, o_nsa, o_mem, hp)


def _tile2d(rows, cols, arrays):
    if rows % 16 == 0:
        return _row_tile(rows, cols * arrays), cols
    want = max(LANE, BLOCK_BYTES // (rows * 4 * arrays) // LANE * LANE)
    tc = LANE
    for t in range(LANE, cols + 1, LANE):
        if cols % t == 0 and t <= want:
            tc = t
    return rows, tc


def _sum_slots(buf, name):
    n, rows, cols = buf.shape
    tr, tc = _tile2d(rows, cols, n)

    def kern(b_ref, o_ref):
        acc = b_ref[0].astype(F32)
        for i in range(1, n):
            acc = acc + b_ref[i].astype(F32)
        o_ref[...] = acc

    return _pcall(kern, name=name, grid=(rows // tr, cols // tc),
                  in_specs=[pl.BlockSpec((n, tr, tc), lambda i, j: (0, i, j))],
                  out_specs=pl.BlockSpec((tr, tc), lambda i, j: (i, j)),
                  out_shape=jax.ShapeDtypeStruct((rows, cols), F32))(buf)


def _pair_sum(g4, theirs, core, axis, name):
    n, rows, cols = theirs.shape
    tr, tc = _tile2d(rows, cols, 1)
    nbr, nbc = rows // tr, cols // tc

    def kern(c_ref, a_ref, b_ref, o_ref):
        o_ref[...] = (a_ref[...] + b_ref[...]).astype(BF16)

    blk = (1, tr, tc)
    mine = ((lambda s, i, j, c: (s, c[0] * nbr + i, j)) if axis == 0
            else (lambda s, i, j, c: (s, i, c[0] * nbc + j)))
    grid_spec = pltpu.PrefetchScalarGridSpec(
        num_scalar_prefetch=1, grid=(n, nbr, nbc),
        in_specs=[pl.BlockSpec(blk, mine), pl.BlockSpec(blk, lambda s, i, j, c: (s, i, j))],
        out_specs=pl.BlockSpec(blk, lambda s, i, j, c: (s, i, j)))
    return _pcall(kern, name=name, grid_spec=grid_spec,
                  out_shape=jax.ShapeDtypeStruct((n, rows, cols), BF16))(core, g4, theirs)


def _adamw(w, g, m, v, name):
    rows, cols = w.shape
    tr, tc = _tile2d(rows, cols, 4)
    bc1 = 1.0 - ADAM_B1 ** ADAM_STEP
    bc2 = 1.0 - ADAM_B2 ** ADAM_STEP

    def kern(w_ref, g_ref, m_ref, v_ref, d_ref, mo_ref, vo_ref):
        gv = g_ref[...]
        mn = ADAM_B1 * m_ref[...] + (1.0 - ADAM_B1) * gv
        vn = ADAM_B2 * v_ref[...] + (1.0 - ADAM_B2) * (gv * gv)
        d_ref[...] = -ADAM_LR * ((mn / bc1) / (jnp.sqrt(vn / bc2) + ADAM_EPS) + ADAM_WD * w_ref[...])
        mo_ref[...] = mn
        vo_ref[...] = vn

    blk = pl.BlockSpec((tr, tc), lambda i, j: (i, j))
    return _pcall(kern, name=name, grid=(rows // tr, cols // tc), in_specs=[blk] * 4, out_specs=[blk] * 3,
                  out_shape=[jax.ShapeDtypeStruct((rows, cols), F32)] * 3)(w, g, m, v)


ANY = pl.BlockSpec(memory_space=pl.ANY)


def _place():
    x, y, c = lax.axis_index("x"), lax.axis_index("y"), lax.axis_index("c")
    chips = [(1 - x, y), (x, 1 - y), (1 - x, 1 - y)]
    return x, y, c, chips


def _remote(src, dst, send_sem, recv_sem, to):
    return pltpu.make_async_remote_copy(src_ref=src, dst_ref=dst, send_sem=send_sem, recv_sem=recv_sem,
                                        device_id=to, device_id_type=MESH)


def _half(ref, lead, core, axis):
    size = ref.shape[len(lead) + axis] // 2
    cut = pl.ds(core * size, size)
    return ref.at[tuple(lead) + ((cut, slice(None)) if axis == 0 else (slice(None), cut))]


def _gather_shards(ws, axes):
    side = _gather_side(ws, axes)

    def body(*refs):
        nw = len(ws)
        split = (refs[:nw], refs[nw:2 * nw], refs[2 * nw:])
        side.phase("start", *split)
        side.phase("finish", *split)

    return _pcall(body, name="gather_shards", in_specs=[ANY] * len(ws), out_specs=[ANY] * len(ws),
                  out_shape=side.out_shape, scratch_shapes=side.scratch)(*ws)


def _gather_side(ws, axes):
    nw = len(ws)

    def phase(which, w_refs, out_refs, sems):
        send_sems, recv_sems = sems
        x, y, c, chips = _place()
        me = 2 * x + y
        sibling = (x, y, 1 - c)

        def part(i, slot, core):
            return _half(out_refs[i], (slot,), core, axes[i])

        def copy(sem, src, dst, to):
            return _remote(src, dst, send_sems.at[sem], recv_sems.at[sem], to)

        first = [copy(j * nw + i, _half(w_refs[i], (), c, axes[i]), part(i, me, c), (*chip, c))
                 for j, chip in enumerate(chips) for i in range(nw)]
        if which == "start":
            for cp in first:
                cp.start()
            return
        passed = []
        for j, (cx, cy) in enumerate(chips):
            slot = 2 * cx + cy
            for i in range(nw):
                copy(j * nw + i, part(i, slot, c), part(i, slot, c), (x, y, c)).wait_recv()
                fwd = copy((3 + j) * nw + i, part(i, slot, c), part(i, slot, c), sibling)
                fwd.start()
                passed.append(fwd)
        for j, (cx, cy) in enumerate(chips):
            slot = 2 * cx + cy
            for i in range(nw):
                copy((3 + j) * nw + i, part(i, slot, 1 - c), part(i, slot, 1 - c), (x, y, c)).wait_recv()
        for cp in first + passed:
            cp.wait_send()

    return _Side(list(ws), [jax.ShapeDtypeStruct((4,) + w.shape, w.dtype) for w in ws],
                 [pltpu.SemaphoreType.DMA((6 * nw,)), pltpu.SemaphoreType.DMA((6 * nw,))], phase)


def _half_shape(shape, axis):
    return tuple(d // 2 if k == len(shape) - 2 + axis else d for k, d in enumerate(shape))


def _pair_exchange(gs, axes, name):
    nw = len(gs)

    def body(*refs):
        g_refs, out_refs = refs[:nw], refs[nw:2 * nw]
        send_sems, recv_sems = refs[2 * nw:]
        x, y, c, _ = _place()
        cps = []
        for i in range(nw):
            cp = _remote(_half(g_refs[i], (slice(None),), 1 - c, axes[i]), out_refs[i],
                         send_sems.at[i], recv_sems.at[i], (x, y, 1 - c))
            cp.start()
            cps.append(cp)
        for cp in cps:
            cp.wait()

    return _pcall(body, name=name, in_specs=[ANY] * nw, out_specs=[ANY] * nw,
                  out_shape=[jax.ShapeDtypeStruct(_half_shape(g.shape, a), g.dtype) for g, a in zip(gs, axes)],
                  scratch_shapes=[pltpu.SemaphoreType.DMA((nw,)), pltpu.SemaphoreType.DMA((nw,))])(*gs)


def _chip_exchange(ps):
    side = _chip_side(ps)

    def body(*refs):
        nw = len(ps)
        split = (refs[:nw], refs[nw:2 * nw], refs[2 * nw:])
        side.phase("start", *split)
        side.phase("finish", *split)

    return _pcall(body, name="chip_exchange", in_specs=[ANY] * len(ps), out_specs=[ANY] * len(ps),
                  out_shape=side.out_shape, scratch_shapes=side.scratch)(*ps)


def _chip_side(ps):
    nw = len(ps)

    def phase(which, p_refs, out_refs, sems):
        send_sems, recv_sems, local_sems = sems
        x, y, c, chips = _place()
        me = 2 * x + y
        mine = [pltpu.make_async_copy(p_refs[i].at[me], out_refs[i].at[me], local_sems.at[i]) for i in range(nw)]
        sends = [_remote(p_refs[i].at[2 * cx + cy], out_refs[i].at[me], send_sems.at[j * nw + i],
                         recv_sems.at[j * nw + i], (cx, cy, c))
                 for j, (cx, cy) in enumerate(chips) for i in range(nw)]
        if which == "start":
            for cp in mine + sends:
                cp.start()
            return
        for j, (cx, cy) in enumerate(chips):
            slot = 2 * cx + cy
            for i in range(nw):
                _remote(out_refs[i].at[slot], out_refs[i].at[slot], send_sems.at[j * nw + i],
                        recv_sems.at[j * nw + i], (x, y, c)).wait_recv()
        for cp in sends:
            cp.wait_send()
        for cp in mine:
            cp.wait()

    return _Side(list(ps), [jax.ShapeDtypeStruct(p.shape, p.dtype) for p in ps],
                 [pltpu.SemaphoreType.DMA((3 * nw,)), pltpu.SemaphoreType.DMA((3 * nw,)),
                  pltpu.SemaphoreType.DMA((nw,))], phase)


def _half_exchange(ts, axes):
    nw = len(ts)

    def body(*refs):
        t_refs, out_refs = refs[:nw], refs[nw:2 * nw]
        send_sems, recv_sems = refs[2 * nw:]
        x, y, c, _ = _place()
        sends = []
        for i in range(nw):
            cp = _remote(t_refs[i], _half(out_refs[i], (), c, axes[i]), send_sems.at[i], recv_sems.at[i],
                         (x, y, 1 - c))
            cp.start()
            sends.append(cp)
        for i in range(nw):
            _remote(t_refs[i], _half(out_refs[i], (), 1 - c, axes[i]), send_sems.at[i], recv_sems.at[i],
                    (x, y, c)).wait_recv()
        for cp in sends:
            cp.wait_send()

    def whole(t, a):
        return tuple(2 * d if k == a else d for k, d in enumerate(t.shape))

    return _pcall(body, name="half_exchange", in_specs=[ANY] * nw, out_specs=[ANY] * nw,
                  out_shape=[jax.ShapeDtypeStruct(whole(t, a), t.dtype) for t, a in zip(ts, axes)],
                  scratch_shapes=[pltpu.SemaphoreType.DMA((nw,)), pltpu.SemaphoreType.DMA((nw,))])(*ts)


def _gather_all(v):
    rows, cols = v.shape

    def body(v_ref, out_ref, send_sems, recv_sems, local_sem):
        x, y, c, _ = _place()
        me = 4 * x + 2 * y + c
        mine = pltpu.make_async_copy(v_ref, out_ref.at[me], local_sem)
        mine.start()
        sends = []
        for d in range(1, 8):
            peer = (x ^ (d >> 2), y ^ ((d >> 1) & 1), c ^ (d & 1))
            cp = _remote(v_ref, out_ref.at[me], send_sems.at[d - 1], recv_sems.at[d - 1], peer)
            cp.start()
            sends.append(cp)
        for d in range(1, 8):
            slot = 4 * (x ^ (d >> 2)) + 2 * (y ^ ((d >> 1) & 1)) + (c ^ (d & 1))
            _remote(v_ref, out_ref.at[slot], send_sems.at[d - 1], recv_sems.at[d - 1], (x, y, c)).wait_recv()
        for cp in sends:
            cp.wait_send()
        mine.wait()

    return _pcall(body, name="gather_all", in_specs=[ANY], out_specs=ANY,
                  out_shape=jax.ShapeDtypeStruct((8, rows, cols), v.dtype),
                  scratch_shapes=[pltpu.SemaphoreType.DMA((7,)), pltpu.SemaphoreType.DMA((7,)),
                                  pltpu.SemaphoreType.DMA])(v)


def _pad_cols(a, width):
    return a if a.shape[1] == width else jnp.pad(a, ((0, 0), (0, width - a.shape[1])))


def _unpad_segments():
    z = PAD["z"]
    segs = [(PAD["c_q"], 0, 512), (PAD["c_kv"], 512, 512), (PAD["k_rope"], 1024, 64), (z, 1088, 1024)]
    segs += [(PAD["q_nsa"] + 256 * h, 2112 + NSA_DK * h, NSA_DK) for h in range(NSA_HEADS)]
    for name, rows in (("k_c", 192), ("v_c", 128), ("k_s", 192), ("v_s", 128), ("k_w", 192), ("v_w", 128),
                       ("g_nsa", 12)):
        segs.append((PAD[name], ORIG[name][0], rows))
    segs += [(z + 1024, ORIG["z_nsa"][0], 512), (PAD["q_mem"], ORIG["q_mem"][0], 512),
             (z + 1536, ORIG["z_mem"][0], 512)]
    return segs


def _w_in_grad_slots(gt):
    rows, cols = gt.shape
    shard = sum(n for _, _, n in _unpad_segments()) // 4
    tc = 256
    pieces = []
    for src, dst, n in _unpad_segments():
        while n:
            slot, off = divmod(dst, shard)
            take = min(n, shard - off)
            pieces.append((src, slot, off, take))
            src, dst, n = src + take, dst + take, n - take

    def kern(g_ref, o_ref):
        for src, slot, off, take in pieces:
            o_ref[slot, off:off + take, :] = g_ref[src:src + take, :]

    return _pcall(kern, name="w_in_grad_slots", grid=(cols // tc,),
                  in_specs=[pl.BlockSpec((rows, tc), lambda i: (0, i))],
                  out_specs=pl.BlockSpec((4, shard, tc), lambda i: (0, 0, i)),
                  out_shape=jax.ShapeDtypeStruct((4, shard, cols), F32))(gt)


def _w_in_from_slots(ws):
    nslot, shard, cols = ws.shape
    tc = 256
    pieces = []
    for dst, src, n in _unpad_segments() + [(PAD["k_rope"] + 64, ORIG["k_rope"][0], 64)]:
        while n:
            slot, off = divmod(src, shard)
            take = min(n, shard - off)
            pieces.append((dst, slot, off, take))
            src, dst, n = src + take, dst + take, n - take

    def kern(w_ref, o_ref):
        o_ref[...] = jnp.zeros_like(o_ref)
        for dst, slot, off, take in pieces:
            o_ref[dst:dst + take, :] = w_ref[slot, off:off + take, :]

    return _pcall(kern, name="w_in_from_slots", grid=(cols // tc,),
                  in_specs=[pl.BlockSpec((nslot, shard, tc), lambda i: (0, 0, i))],
                  out_specs=pl.BlockSpec((D_PAD, tc), lambda i: (0, i)),
                  out_shape=jax.ShapeDtypeStruct((D_PAD, cols), ws.dtype))(ws)


def _rope_tables(s):
    pos = jnp.arange(s, dtype=F32)
    inv_freq = ROPE_THETA ** (-jnp.arange(0, 64, 2, dtype=F32) / 64)
    ang = pos[:, None] * inv_freq[None, :]
    cos, sin = jnp.cos(ang), jnp.sin(ang)
    z = jnp.zeros((s, 64), F32)
    return jnp.concatenate([cos, cos, z], axis=1), jnp.concatenate([-sin, sin, z], axis=1)


def _overlap_table(s):
    n_c, n_s = s // CMP_STRIDE, s // SLC_LEN
    c0 = np.arange(n_c)[:, None] * CMP_STRIDE
    s0 = np.arange(LANE)[None, :] * SLC_LEN
    ov = (c0 < s0 + SLC_LEN) & (c0 + CMP_LEN > s0) & (np.arange(n_c)[:, None] < n_c - 1) & (np.arange(LANE)[None, :] < n_s)
    return jnp.asarray(ov.astype(np.float32), dtype=BF16)


def _shift_down(a):
    return jnp.concatenate([jnp.zeros((8, a.shape[1]), a.dtype), a], axis=0)[7:7 + a.shape[0]]


def _shift_up(a):
    return jnp.concatenate([a, jnp.zeros((8, a.shape[1]), a.dtype)], axis=0)[1:1 + a.shape[0]]


def _local_step(x, mem, target, w, hooks=None):
    s = x.shape[0]
    cs, sn = _rope_tables(s)
    t_ = jnp.transpose

    w_in_p = _w_in_from_slots(w["w_in_t"])
    xn, rstd_x = _rms_fwd(_Src(x, D_MODEL), w["norm_g"], "norm_x")
    if hooks is None:
        hp, hpb = _mm(xn, w_in_p, "in_proj", mode="nt", second_dtype=BF16)
    else:
        hp, hpb, *gathered = _mm(xn, w_in_p, "in_proj", mode="nt", second_dtype=BF16, side=hooks.gather_side)
        w = {**w, **hooks.weights(gathered)}

    w_uq3 = w["w_uq"].reshape(512, MLA_HEADS, 192)
    w_uq_p = jnp.concatenate([w_uq3, w_uq3[:, :, 128:]], axis=2).reshape(512, MLA_HEADS * 256)
    w_ukv_p = t_(w["w_ukv"].reshape(512, MLA_HEADS, 2, 128), (0, 2, 1, 3)).reshape(512, 2048)
    c_q, c_kv = _Src(hp, 512, 0), _Src(hp, 512, 1)
    cqn, rstd_q = _rms_fwd(c_q, w["q_norm_g"], "norm_q")
    ckvn, rstd_kv = _rms_fwd(c_kv, w["kv_norm_g"], "norm_kv")
    q_lin = _mm(cqn, w_uq_p, "mla_q_proj")
    kvb = _mm(ckvn, w_ukv_p, "mla_kv_proj", out_dtype=BF16)
    q_mla = _rope_fwd(_Src(q_lin, MLA_HEADS * 256), cs, sn, MLA_HEADS, 256, LANE, "rope_q")
    k_pe = _rope_fwd(_Src(hp, LANE, PAD["k_rope"] // LANE), cs, sn, 1, LANE, 0, "rope_k")
    mla = _Attn("mla", s, s, MLA_HEADS, 256)
    mla_q, mla_v = _Src(q_mla, 256), _Src(kvb, LANE, MLA_HEADS)
    mla_k = [_Src(kvb, LANE), _Src(k_pe, LANE, 0, False)]
    o_mla, l_mla, lr_mla = _attn_fwd(mla, mla_q, mla_k, mla_v, None, "mla_fwd")

    sk = s // CMP_STRIDE
    pe_k, pe_v = w["cmp_pe_k"], w["cmp_pe_v"]
    w1k = _pad_cols(w["cmp_w1k"], 256)
    w2k = jnp.pad(w["cmp_w2k"], ((0, 64), (0, 64))).astype(BF16)
    w1v, w2v = w["cmp_w1v"], w["cmp_w2v"].astype(BF16)
    half_k, half_v = CMP_STRIDE * NSA_DK, CMP_STRIDE * HEAD_V
    ak = hp[:, PAD["k_c"]:PAD["k_c"] + NSA_DK].reshape(sk, half_k)
    av = hp[:, PAD["v_c"]:PAD["v_c"] + HEAD_V].reshape(sk, half_v)
    ck_args = (ak, _shift_up(ak), pe_k[:CMP_STRIDE].reshape(1, half_k), pe_k[CMP_STRIDE:].reshape(1, half_k),
               w1k[:half_k], w1k[half_k:], w2k)
    cv_args = (av, _shift_up(av), pe_v[:CMP_STRIDE].reshape(1, half_v), pe_v[CMP_STRIDE:].reshape(1, half_v),
               w1v[:half_v], w1v[half_v:], w2v)
    k_cmp, pre_k = _compress_fwd(*ck_args, "compress_k")
    v_cmp, pre_v = _compress_fwd(*cv_args, "compress_v")
    cmp_ = _Attn("cmp", s, sk, NSA_HEADS, 256)
    slc = _Attn("slc", s, s, NSA_HEADS, 256)
    win = _Attn("win", s, s, NSA_HEADS, 256)
    nsa_q = _Src(hpb, 256, PAD["q_nsa"] // 256)
    cmp_k, cmp_v = [_Src(k_cmp, 256, 0, False)], _Src(v_cmp, HEAD_V, 0, False)
    slc_k, slc_v = [_Src(hpb, 256, PAD["k_s"] // 256, False)], _Src(hpb, HEAD_V, PAD["v_s"] // HEAD_V, False)
    win_k, win_v = [_Src(hpb, 256, PAD["k_w"] // 256, False)], _Src(hpb, HEAD_V, PAD["v_w"] // HEAD_V, False)
    o_cmp, l_cmp, lr_cmp, sel, selt = _attn_fwd_small(cmp_, nsa_q, cmp_k, cmp_v, "cmp_fwd", _overlap_table(s))
    o_slc, l_slc, lr_slc = _attn_fwd(slc, nsa_q, slc_k, slc_v, sel, "slc_fwd")
    o_win, l_win, lr_win = _attn_fwd_small(win, nsa_q, win_k, win_v, "win_fwd")
    gl = _Src(hp, LANE, PAD["g_nsa"] // LANE)
    o_nsa = _nsa_combine(o_cmp, o_slc, o_win, gl)

    mn, rstd_m = _rms_fwd(_Src(mem, D_MODEL), w["mem_norm_g"], "norm_mem")
    kvm = _mm(mn, w["w_mem_kv"], "mem_kv_proj", out_dtype=BF16)
    mem_ = _Attn("mem", s, mem.shape[0], MEM_HEADS, LANE)
    mem_q, mem_k, mem_v = _Src(hpb, LANE, PAD["q_mem"] // LANE), [_Src(kvm, LANE)], _Src(kvm, LANE, MEM_HEADS)
    o_mem, l_mem, lr_mem = _attn_fwd_small(mem_, mem_q, mem_k, mem_v, "mem_fwd")

    u = _gate_fwd(o_mla, o_nsa, o_mem, hp)
    proj = _mm(u, w["w_out"], "out_proj")
    dy, g_final, loss = _final_loss(x, proj, w["final_norm_g"].reshape(1, -1), target)

    g_w_out = _mm(u, dy, "out_proj_dw", mode="tn")
    du = _mm(dy, w["w_out"], "out_proj_dx", mode="nt")
    do_cat, dz = _gate_bwd(du, o_mla, o_nsa, o_mem, hp)

    dq_mem, (dk_mem,), dv_mem = _attn_bwd(mem_, mem_q, mem_k, mem_v, None, None, _Src(o_mem, HEAD_V), l_mem,
                                          lr_mem, _Src(do_cat, HEAD_V, 12), None, "mem_bwd")
    dkvm = jnp.concatenate([dk_mem, dv_mem], axis=1)
    g_w_mem_kv = _mm(mn, dkvm, "mem_kv_dw", mode="tn")
    dmn = _mm(dkvm, w["w_mem_kv"], "mem_kv_dx", mode="nt")
    _, g_mem_norm = _rms_bwd(_Src(mem, D_MODEL), w["mem_norm_g"], rstd_m, dmn, None, "norm_mem_bwd")

    do_cmp, do_slc, do_win, dgl = _nsa_combine_bwd(do_cat, o_cmp, o_slc, o_win, gl)
    dq_n, (dk_cmp,), dv_cmp = _attn_bwd(cmp_, nsa_q, cmp_k, cmp_v, None, None, _Src(o_cmp, HEAD_V), l_cmp,
                                        lr_cmp, _Src(do_cmp, HEAD_V), None, "cmp_bwd")
    dq_n, (dk_s,), dv_s = _attn_bwd(slc, nsa_q, slc_k, slc_v, sel, selt, _Src(o_slc, HEAD_V), l_slc, lr_slc,
                                    _Src(do_slc, HEAD_V), dq_n, "slc_bwd")
    dq_n, (dk_w,), dv_w = _attn_bwd(win, nsa_q, win_k, win_v, None, None, _Src(o_win, HEAD_V), l_win, lr_win,
                                    _Src(do_win, HEAD_V), dq_n, "win_bwd")
    dak, dpk_lo, dpk_hi, dw1k_lo, dw1k_hi, g_w2k = _compress_bwd(
        *ck_args, pre_k, _shift_down(pre_k), dk_cmp, _shift_down(dk_cmp), "compress_k_bwd")
    dav, dpv_lo, dpv_hi, dw1v_lo, dw1v_hi, g_w2v = _compress_bwd(
        *cv_args, pre_v, _shift_down(pre_v), dv_cmp, _shift_down(dv_cmp), "compress_v_bwd")
    g_pe_k = jnp.concatenate([dpk_lo.reshape(CMP_STRIDE, NSA_DK), dpk_hi.reshape(CMP_STRIDE, NSA_DK)], axis=0)
    g_pe_v = jnp.concatenate([dpv_lo.reshape(CMP_STRIDE, HEAD_V), dpv_hi.reshape(CMP_STRIDE, HEAD_V)], axis=0)
    g_w1k = jnp.concatenate([dw1k_lo, dw1k_hi], axis=0)[:, :NSA_DK]
    g_w1v = jnp.concatenate([dw1v_lo, dw1v_hi], axis=0)
    dk_c = _pad_cols(dak.reshape(s, NSA_DK), 256)
    dv_c = dav.reshape(s, HEAD_V)

    dq_m, (dk_nope, dk_pe), dv_m = _attn_bwd(mla, mla_q, mla_k, mla_v, None, None, _Src(o_mla, HEAD_V), l_mla,
                                             lr_mla, _Src(do_cat, HEAD_V), None, "mla_bwd")
    dq_lin = _rope_bwd_q(dq_m, cs, sn)
    dkv_lin, d_krope = _rope_bwd_k(dk_nope, dk_pe, dv_m, cs, sn)
    g_w_uq_p = _mm(cqn, dq_lin, "mla_q_dw", mode="tn")
    dcqn = _mm(dq_lin, w_uq_p, "mla_q_dx", mode="nt")
    g_w_ukv_p = _mm(ckvn, dkv_lin, "mla_kv_dw", mode="tn")
    dckvn = _mm(dkv_lin, w_ukv_p, "mla_kv_dx", mode="nt")
    dc_q, g_q_norm = _rms_bwd(c_q, w["q_norm_g"], rstd_q, dcqn, None, "norm_q_bwd")
    dc_kv, g_kv_norm = _rms_bwd(c_kv, w["kv_norm_g"], rstd_kv, dckvn, None, "norm_kv_bwd")
    g_w_uq = g_w_uq_p.reshape(512, MLA_HEADS, 256)[:, :, :192].reshape(512, MLA_HEADS * 192)
    g_w_ukv = t_(g_w_ukv_p.reshape(512, 2, MLA_HEADS, 128), (0, 2, 1, 3)).reshape(512, 2048)

    dhp = jnp.concatenate(
        [dc_q, dc_kv, dq_n, dk_c, dk_s, dk_w, d_krope, dv_c, dv_s, dv_w, dgl,
         jnp.zeros((s, PAD["q_mem"] - (PAD["g_nsa"] + LANE)), F32), dq_mem, dz], axis=1).astype(BF16)
    grads = dict(q_norm_g=g_q_norm, w_uq=g_w_uq, kv_norm_g=g_kv_norm,
                 w_ukv=g_w_ukv, cmp_pe_k=g_pe_k, cmp_pe_v=g_pe_v, cmp_w1k=g_w1k, cmp_w2k=g_w2k[:NSA_DK, :NSA_DK],
                 cmp_w1v=g_w1v, cmp_w2v=g_w2v, mem_norm_g=g_mem_norm, w_mem_kv=g_w_mem_kv, w_out=g_w_out,
                 final_norm_g=g_final.reshape(-1))
    g_w_in_t = _w_in_grad_slots(_mm(dhp, xn, "in_proj_dw", mode="tn", wide=2048))
    if hooks is None:
        dxn = _mm(dhp, w_in_p, "in_proj_dx", wide=2048)
    else:
        dxn, *received = _mm(dhp, w_in_p, "in_proj_dx", wide=2048, side=hooks.reduce_side(grads))
        hooks.received = received
    grad_x, g_norm = _rms_bwd(_Src(x, D_MODEL), w["norm_g"], rstd_x, dxn, dy, "norm_x_bwd")
    grads.update(norm_g=g_norm, w_in_t=g_w_in_t)
    return loss[0, 0], grad_x, grads


def kernel(x, mem, norm_g, w_in, q_norm_g, w_uq, kv_norm_g, w_ukv, cmp_pe_k, cmp_pe_v, cmp_w1k, cmp_w2k, cmp_w1v, cmp_w2v, mem_norm_g, w_mem_kv, w_out, final_norm_g, loss_target, m_norm_g, m_w_in, m_q_norm_g, m_w_uq, m_kv_norm_g, m_w_ukv, m_cmp_pe_k, m_cmp_pe_v, m_cmp_w1k, m_cmp_w2k, m_cmp_w1v, m_cmp_w2v, m_mem_norm_g, m_w_mem_kv, m_w_out, m_final_norm_g, v_norm_g, v_w_in, v_q_norm_g, v_w_uq, v_kv_norm_g, v_w_ukv, v_cmp_pe_k, v_cmp_pe_v, v_cmp_w1k, v_cmp_w2k, v_cmp_w1v, v_cmp_w2v, v_mem_norm_g, v_w_mem_kv, v_w_out, v_final_norm_g):
    args = dict(locals())
    wts = {n: args[n] for n in WEIGHTS}
    loc = {n: (a if n == "final_norm_g" else a[0]) for n, a in wts.items()}

    def to_x(n, a):
        return a.T if n == "w_in" else a

    split = [1 if n == "w_in" else 0 for n in SHARDED]
    rest = [n for n in SHARDED if n != "w_in"]
    chip = 2 * lax.axis_index("x") + lax.axis_index("y")
    core = lax.axis_index("c").astype(jnp.int32).reshape(1)
    own = {n: to_x(n, loc[n]).astype(BF16) for n in SHARDED}

    def with_own_slot(gw, a):
        return lax.dynamic_update_slice(gw, a[None], (chip, 0, 0))

    def slots(n, a):
        if n == "w_in":
            return a
        if SHARD_AXIS[n] == 0:
            return a.reshape(4, a.shape[0] // 4, a.shape[1])
        width = a.shape[1] // 4
        return jnp.stack([a[:, j * width:(j + 1) * width] for j in range(4)])

    def pair_sums(names, grads, name):
        axes = [1 if n == "w_in" else 0 for n in names]
        gs = [slots(n, a) for n, a in zip(names, grads)]
        theirs = _pair_exchange(gs, axes, name)
        return [_pair_sum(a, b, core, ax, "pair_sum_" + n) for n, a, b, ax in zip(names, gs, theirs, axes)]

    class Hooks:
        gather_side = _gather_side([own[n] for n in rest], [0] * len(rest))
        received = None

        @staticmethod
        def weights(gathered):
            out = {}
            for n, gw in zip(rest, gathered):
                gw = with_own_slot(gw, own[n])
                if SHARD_AXIS[n] == 0:
                    out[n] = gw.reshape(4 * gw.shape[1], gw.shape[2])
                else:
                    out[n] = jnp.concatenate([gw[j] for j in range(4)], axis=1)
            return out

        @staticmethod
        def reduce_side(grads):
            return _chip_side(pair_sums(rest, [grads[n] for n in rest], "pair_exchange_rest"))

    hooks = Hooks()

    start = {n: loc[n].reshape(1, -1) if loc[n].ndim == 1 else loc[n] for n in REPLICATED}
    start["w_in_t"] = with_own_slot(_gather_shards([own["w_in"]], [1])[0], own["w_in"])
    loss, grad_x, g = _local_step(x[0], mem[0], loss_target[0], start, hooks)
    loss = lax.psum(loss, ("x", "y", "c"))

    from_chips = dict(zip(rest, hooks.received))
    from_chips["w_in"] = _chip_exchange(pair_sums(["w_in"], [g["w_in_t"]], "pair_exchange_w_in"))[0]
    mine = [_sum_slots(from_chips[n], "chip_sum_" + n) for n in SHARDED]
    g_sh = [lax.dynamic_update_slice(o, t, (core[0] * t.shape[0], 0) if ax == 0 else (0, core[0] * t.shape[1]))
            for o, t, ax in zip(_half_exchange(mine, split), mine, split)]

    n_rep = sum(int(np.prod(loc[n].shape)) for n in REPLICATED)
    rows_rep = -(-n_rep // (8 * LANE)) * 8

    def rep_pack(parts):
        flat = jnp.concatenate([p.reshape(-1) for p in parts])
        return jnp.pad(flat, (0, rows_rep * LANE - n_rep)).reshape(rows_rep, LANE)

    g_rep = _sum_slots(_gather_all(rep_pack([g[n] for n in REPLICATED])), "replica_sum")
    d_rp, m_rp, v_rp = _adamw(rep_pack([wts[n] for n in REPLICATED]), g_rep,
                              rep_pack([args["m_" + n] for n in REPLICATED]),
                              rep_pack([args["v_" + n] for n in REPLICATED]), "adamw_replicated")

    def rep_unpack(buf):
        flat, out, o = buf.reshape(-1), {}, 0
        for n in REPLICATED:
            size = int(np.prod(wts[n].shape))
            out[n] = flat[o:o + size].reshape(wts[n].shape)
            o += size
        return out

    outs = {k: rep_unpack(b) for k, b in (("g", g_rep), ("d", d_rp), ("m", m_rp), ("v", v_rp))}
    for n, gn in zip(SHARDED, g_sh):
        d, mo, vo = _adamw(to_x(n, loc[n]), gn, to_x(n, args["m_" + n][0]), to_x(n, args["v_" + n][0]),
                           "adamw_" + n)
        for k, a in (("g", gn), ("d", d), ("m", mo), ("v", vo)):
            outs[k][n] = to_x(n, a).reshape(wts[n].shape)

    return (loss, grad_x[None], *[outs["g"][n] for n in WEIGHTS], *[outs["d"][n] for n in WEIGHTS],
            *[outs["m"][n] for n in WEIGHTS], *[outs["v"][n] for n in WEIGHTS])
```

```python
from typing import NamedTuple

import numpy as np
import jax
import jax.numpy as jnp
from jax import lax
from jax.experimental import pallas as pl
from jax.experimental.pallas import tpu as pltpu

F32 = jnp.float32
BF16 = jnp.bfloat16
MESH = pl.DeviceIdType.MESH

D_MODEL = 2048
EPS = 1e-6
LANE = 128
HEAD_V = 128
MLA_HEADS = 8
NSA_HEADS = 4
MEM_HEADS = 4
NSA_DK = 192
CMP_STRIDE = 16
CMP_LEN = 32
SLC_LEN = 64
SLC_TOPN = 16
WIN = 512
NEG = -1e30
LOG2E = 1.4426950408889634
ROPE_THETA = 10000.0
BLOCK_BYTES = 2 << 20

ORIG = dict(c_q=(0, 512), c_kv=(512, 512), k_rope=(1024, 64), z_mla=(1088, 1024),
            q_nsa=(2112, 768), k_c=(2880, 192), v_c=(3072, 128), k_s=(3200, 192),
            v_s=(3392, 128), k_w=(3520, 192), v_w=(3712, 128), g_nsa=(3840, 12),
            z_nsa=(3852, 512), q_mem=(4364, 512), z_mem=(4876, 512))
PAD = dict(c_q=0, c_kv=512, q_nsa=1024, k_c=2048, k_s=2304, k_w=2560, k_rope=2816, v_c=2944,
           v_s=3072, v_w=3200, g_nsa=3328, q_mem=3584, z=4096)
D_PAD = 6144

ADAM_LR, ADAM_B1, ADAM_B2, ADAM_EPS, ADAM_WD, ADAM_STEP = 0.001, 0.9, 0.999, 1e-08, 0.01, 10

SHARDED = ("w_in", "w_uq", "w_ukv", "cmp_w1k", "cmp_w1v", "w_mem_kv", "w_out")
SHARD_AXIS = dict(w_in=1, w_uq=1, w_ukv=1, cmp_w1k=0, cmp_w1v=0, w_mem_kv=0, w_out=0)
REPLICATED = ("norm_g", "q_norm_g", "kv_norm_g", "cmp_pe_k", "cmp_pe_v", "cmp_w2k", "cmp_w2v",
              "mem_norm_g", "final_norm_g")
WEIGHTS = ("norm_g", "w_in", "q_norm_g", "w_uq", "kv_norm_g", "w_ukv", "cmp_pe_k", "cmp_pe_v",
           "cmp_w1k", "cmp_w2k", "cmp_w1v", "cmp_w2v", "mem_norm_g", "w_mem_kv", "w_out",
           "final_norm_g")


def _pcall(kernel, **kw):
    return pl.pallas_call(kernel, **kw)


def _tile(n, pref):
    if n <= pref:
        return n
    for t in range(pref, LANE - 1, -LANE):
        if n % t == 0:
            return t
    raise ValueError((n, pref))


def _row_tile(rows, cols, itemsize=4):
    want = max(16, BLOCK_BYTES // (cols * itemsize))
    if rows <= want:
        return rows
    t = 16
    best = rows
    while t <= want:
        if rows % t == 0:
            best = t
        t *= 2
    return best


def _nt(a, b):
    return lax.dot_general(a, b, (((1,), (1,)), ((), ())), preferred_element_type=F32)


def _tn(a, b):
    return lax.dot_general(a, b, (((0,), (0,)), ((), ())), preferred_element_type=F32)


def _nn(a, b):
    return jnp.dot(a, b, preferred_element_type=F32)


def _sigmoid(x):
    return 1.0 / (1.0 + jnp.exp(-x))


class _Src(NamedTuple):
    arr: jax.Array
    width: int
    col0: int = 0
    per_head: bool = True

    def col(self, h):
        return self.col0 + h if self.per_head else self.col0


class _Side(NamedTuple):
    inputs: list
    out_shape: list
    scratch: list
    phase: object


def _mm(a, b, name, mode="nn", out_dtype=F32, second_dtype=None, wide=1024, side=None):
    if mode == "tn":
        k, m = a.shape
    else:
        m, k = a.shape
    if mode == "nt":
        n, k2 = b.shape
    else:
        k2, n = b.shape
    assert k == k2, (a.shape, b.shape, mode)
    tm, tn, tk = _tile(m, 1024), _tile(n, wide), _tile(k, 2048)
    grid = (m // tm, n // tn, k // tk)
    nk = grid[2]
    assert nk == 1 or (out_dtype == F32 and second_dtype is None)
    dot = {"nn": _nn, "nt": _nt, "tn": _tn}[mode]
    n_in = len(side.inputs) if side else 0
    n_out = len(side.out_shape) if side else 0
    n_res = 1 + (second_dtype is not None)

    def kern(*refs):
        a_ref, b_ref = refs[:2]
        res = refs[2 + n_in:2 + n_in + n_res]
        step = [pl.program_id(d) for d in range(3)]
        if side:
            side_refs = (refs[2:2 + n_in], refs[2 + n_in + n_res:2 + n_in + n_res + n_out],
                         refs[2 + n_in + n_res + n_out:])

            @pl.when((step[0] == 0) & (step[1] == 0) & (step[2] == 0))
            def _():
                side.phase("start", *side_refs)

        r = dot(a_ref[...].astype(BF16), b_ref[...].astype(BF16))
        if nk == 1:
            res[0][...] = r.astype(out_dtype)
            if n_res == 2:
                res[1][...] = r.astype(second_dtype)
        else:
            @pl.when(step[2] == 0)
            def _():
                res[0][...] = r

            @pl.when(step[2] > 0)
            def _():
                res[0][...] += r

        if side:
            @pl.when((step[0] == grid[0] - 1) & (step[1] == grid[1] - 1) & (step[2] == nk - 1))
            def _():
                side.phase("finish", *side_refs)

    a_spec = (pl.BlockSpec((tk, tm), lambda i, j, kk: (kk, i)) if mode == "tn"
              else pl.BlockSpec((tm, tk), lambda i, j, kk: (i, kk)))
    b_spec = (pl.BlockSpec((tn, tk), lambda i, j, kk: (j, kk)) if mode == "nt"
              else pl.BlockSpec((tk, tn), lambda i, j, kk: (kk, j)))
    o_spec = pl.BlockSpec((tm, tn), lambda i, j, kk: (i, j))
    out_specs = [o_spec] * n_res + [ANY] * n_out
    out_shape = [jax.ShapeDtypeStruct((m, n), out_dtype)]
    if second_dtype is not None:
        out_shape.append(jax.ShapeDtypeStruct((m, n), second_dtype))
    out_shape += list(side.out_shape) if side else []
    semantics = ("arbitrary",) * 3 if side else ("parallel", "parallel", "arbitrary")
    out = _pcall(
        kern, name=name, grid=grid, in_specs=[a_spec, b_spec] + [ANY] * n_in, out_specs=out_specs,
        out_shape=out_shape, scratch_shapes=list(side.scratch) if side else [],
        compiler_params=pltpu.CompilerParams(dimension_semantics=semantics),
    )(a, b, *(side.inputs if side else []))
    return out[0] if len(out) == 1 else out


def _rms_fwd(x, g, name):
    r, d = x.arr.shape[0], x.width
    tr = _tile(r, 512)

    def kern(x_ref, g_ref, y_ref, r_ref):
        xv = x_ref[...]
        rstd = lax.rsqrt(jnp.mean(xv * xv, axis=-1, keepdims=True) + EPS)
        y_ref[...] = (xv * rstd * g_ref[...]).astype(BF16)
        r_ref[...] = rstd

    return _pcall(
        kern, name=name, grid=(r // tr,),
        in_specs=[pl.BlockSpec((tr, d), lambda i: (i, x.col0)), pl.BlockSpec((1, d), lambda i: (0, 0))],
        out_specs=[pl.BlockSpec((tr, d), lambda i: (i, 0)), pl.BlockSpec((tr, 1), lambda i: (i, 0))],
        out_shape=[jax.ShapeDtypeStruct((r, d), BF16), jax.ShapeDtypeStruct((r, 1), F32)],
    )(x.arr, g)


def _rms_bwd(x, g, rstd, dy, add, name):
    r, d = x.arr.shape[0], x.width
    tr = _tile(r, 256)
    has_add = add is not None

    def kern(*refs):
        if has_add:
            x_ref, g_ref, r_ref, dy_ref, add_ref, dx_ref, dg_ref = refs
        else:
            x_ref, g_ref, r_ref, dy_ref, dx_ref, dg_ref = refs
        rs = r_ref[...]
        xhat = x_ref[...] * rs
        dyv = dy_ref[...]
        dyg = dyv * g_ref[...]
        c = jnp.mean(dyg * xhat, axis=-1, keepdims=True)
        dx = rs * (dyg - xhat * c)
        if has_add:
            dx = dx + add_ref[...]
        dx_ref[...] = dx
        part = jnp.sum(dyv * xhat, axis=0, keepdims=True)

        @pl.when(pl.program_id(0) == 0)
        def _():
            dg_ref[...] = part

        @pl.when(pl.program_id(0) > 0)
        def _():
            dg_ref[...] += part

    row = pl.BlockSpec((tr, d), lambda i: (i, 0))
    vec = pl.BlockSpec((1, d), lambda i: (0, 0))
    ins = [pl.BlockSpec((tr, d), lambda i: (i, x.col0)), vec, pl.BlockSpec((tr, 1), lambda i: (i, 0)), row]
    ins += [row] if has_add else []
    args = (x.arr, g, rstd, dy) + ((add,) if has_add else ())
    return _pcall(
        kern, name=name, grid=(r // tr,), in_specs=ins, out_specs=[row, vec],
        out_shape=[jax.ShapeDtypeStruct((r, d), F32), jax.ShapeDtypeStruct((1, d), F32)],
        compiler_params=pltpu.CompilerParams(dimension_semantics=("arbitrary",)),
    )(*args)


def _final_loss(x, proj, g, target):
    r, d = x.shape
    tr = _tile(r, 256)

    def kern(x_ref, p_ref, g_ref, t_ref, dy_ref, dg_ref, loss_ref):
        y = x_ref[...] + p_ref[...]
        rs = lax.rsqrt(jnp.mean(y * y, axis=-1, keepdims=True) + EPS)
        yhat = y * rs
        gv = g_ref[...]
        e = yhat * gv - t_ref[...]
        lpart = 0.5 * jnp.sum(jnp.mean(e * e, axis=-1, keepdims=True), axis=0, keepdims=True)
        dout = e * (1.0 / d)
        dyg = dout * gv
        c = jnp.mean(dyg * yhat, axis=-1, keepdims=True)
        dy_ref[...] = rs * (dyg - yhat * c)
        gpart = jnp.sum(dout * yhat, axis=0, keepdims=True)
        lrow = jnp.broadcast_to(lpart, (1, LANE))

        @pl.when(pl.program_id(0) == 0)
        def _():
            dg_ref[...] = gpart
            loss_ref[...] = lrow

        @pl.when(pl.program_id(0) > 0)
        def _():
            dg_ref[...] += gpart
            loss_ref[...] += lrow

    row = pl.BlockSpec((tr, d), lambda i: (i, 0))
    vec = pl.BlockSpec((1, d), lambda i: (0, 0))
    return _pcall(
        kern, name="final_loss", grid=(r // tr,), in_specs=[row, row, vec, row],
        out_specs=[row, vec, pl.BlockSpec((1, LANE), lambda i: (0, 0))],
        out_shape=[jax.ShapeDtypeStruct((r, d), F32), jax.ShapeDtypeStruct((1, d), F32),
                   jax.ShapeDtypeStruct((1, LANE), F32)],
        compiler_params=pltpu.CompilerParams(dimension_semantics=("arbitrary",)),
    )(x, proj, g, target)


def _rope_fwd(x, cs, sn, nh, width, off, name):
    s = x.arr.shape[0]
    tr = _tile(s, 512)

    def kern(x_ref, c_ref, s_ref, o_ref):
        cv, sv = c_ref[...], s_ref[...]
        for h in range(nh):
            b = h * width
            if off:
                o_ref[:, b:b + off] = x_ref[:, b:b + off].astype(BF16)
            xr = x_ref[:, b + off:b + off + LANE]
            o_ref[:, b + off:b + off + LANE] = (xr * cv + pltpu.roll(xr, 32, 1) * sv).astype(BF16)

    tab = pl.BlockSpec((tr, LANE), lambda i: (i, 0))
    return _pcall(
        kern, name=name, grid=(s // tr,),
        in_specs=[pl.BlockSpec((tr, nh * width), lambda i: (i, x.col0)), tab, tab],
        out_specs=pl.BlockSpec((tr, nh * width), lambda i: (i, 0)),
        out_shape=jax.ShapeDtypeStruct((s, nh * width), BF16),
    )(x.arr, cs, sn)


def _rope_grad(d, cv, sv):
    g2 = d * sv
    g2 = g2 + pltpu.roll(g2, 64, 1)
    lane = lax.broadcasted_iota(jnp.int32, d.shape, 1)
    return jnp.where(lane < 64, d * cv + pltpu.roll(g2, 32, 1), 0.0)


def _rope_bwd_q(dq, cs, sn):
    s, w = dq.shape
    tr = _tile(s, 512)
    nh = w // 256

    def kern(d_ref, c_ref, s_ref, o_ref):
        cv, sv = c_ref[...], s_ref[...]
        for h in range(nh):
            b = h * 256
            o_ref[:, b:b + LANE] = d_ref[:, b:b + LANE]
            o_ref[:, b + LANE:b + 256] = _rope_grad(d_ref[:, b + LANE:b + 256], cv, sv)

    row = pl.BlockSpec((tr, w), lambda i: (i, 0))
    tab = pl.BlockSpec((tr, LANE), lambda i: (i, 0))
    return _pcall(kern, name="rope_bwd_q", grid=(s // tr,), in_specs=[row, tab, tab], out_specs=row,
                  out_shape=jax.ShapeDtypeStruct((s, w), F32))(dq, cs, sn)


def _rope_bwd_k(dk_nope, dk_pe, dv, cs, sn):
    s, w = dk_nope.shape
    tr = _tile(s, 512)

    def kern(dk_ref, dp_ref, dv_ref, c_ref, s_ref, okv_ref, okr_ref):
        okv_ref[:, :w] = dk_ref[...]
        okv_ref[:, w:] = dv_ref[...]
        okr_ref[...] = _rope_grad(dp_ref[...], c_ref[...], s_ref[...])

    tab = pl.BlockSpec((tr, LANE), lambda i: (i, 0))
    wide = pl.BlockSpec((tr, w), lambda i: (i, 0))
    return _pcall(
        kern, name="rope_bwd_k", grid=(s // tr,), in_specs=[wide, tab, wide, tab, tab],
        out_specs=[pl.BlockSpec((tr, 2 * w), lambda i: (i, 0)), tab],
        out_shape=[jax.ShapeDtypeStruct((s, 2 * w), F32), jax.ShapeDtypeStruct((s, LANE), F32)],
    )(dk_nope, dk_pe, dv, cs, sn)


class _Attn:
    def __init__(self, mode, s, sk, heads, dk):
        self.mode, self.s, self.sk, self.h, self.dk = mode, s, sk, heads, dk
        self.scale = {"mla": 192 ** -0.5, "mem": 128 ** -0.5}.get(mode, NSA_DK ** -0.5)
        self.tb = min(256, s)
        self.nb = s // self.tb
        self.nsub = 2 if self.nb % 2 == 0 else 1
        self.tq = self.tb * self.nsub
        self.nq = s // self.tq
        self.causal = mode in ("mla", "slc")
        if self.causal:
            self.tk = self.tq
        elif mode == "win":
            self.tk = WIN + self.tb
        else:
            self.tk = sk
        self.tkb = min(512, sk)
        self.ksub = 2 if self.tkb == 512 and mode == "mla" else 1
        self.kb = self.tkb // self.ksub
        self.ncmp = s // CMP_STRIDE - 1

    def mask_bias(self, t, n, h, selx, diag):
        m = self.mode
        if m == "mla":
            return (n <= t) if diag else None, None
        if m == "mem":
            return None, None
        slope = jnp.where(h == 0, 0.25, jnp.where(h == 1, 0.0625, jnp.where(h == 2, 0.015625, 0.00390625)))
        slope = slope.astype(F32) * LOG2E
        if m == "cmp":
            mask = (n * CMP_STRIDE + (CMP_LEN - 1) <= t) & (n < self.ncmp)
            pos = n.astype(F32) * float(CMP_STRIDE) + (CMP_LEN - 1) / 2.0
            return mask, slope * pos
        rel = t - n
        if m == "slc":
            return (rel >= 0) if diag else None, slope * n.astype(F32)
        return (rel >= 0) & (rel < WIN), slope * n.astype(F32)


def _scores(cfg, s_raw, t, n, h, selx, diag, lse=None):
    s = s_raw * (cfg.scale * LOG2E)
    mask, key_term = cfg.mask_bias(t, n, h, selx, diag)
    if key_term is not None:
        s = s + key_term
    if selx is not None:
        s = s + selx
    if lse is None:
        if mask is not None:
            s = jnp.where(mask, s, NEG)
        return s, mask
    p = jnp.exp2(jnp.minimum(s - lse, 0.0))
    if mask is not None:
        p = jnp.where(mask, p, 0.0)
    return p, mask


def _block_of_key(k0, tk, keys_on_rows, value=NEG):
    shape = (tk, LANE) if keys_on_rows else (LANE, tk)
    n = lax.broadcasted_iota(jnp.int32, shape, 0 if keys_on_rows else 1) + k0
    j = lax.broadcasted_iota(jnp.int32, shape, 1 if keys_on_rows else 0)
    return jnp.where((n >> 6) == j, value, 0.0).astype(BF16)


def _to_row(col):
    t = col.shape[0]
    return jnp.transpose(jnp.broadcast_to(col, (t, LANE)))[0:1, :]


def _load_keys(k_refs, rows):
    parts = [r[rows, :].astype(BF16) for r in k_refs]
    return parts[0] if len(parts) == 1 else jnp.concatenate(parts, axis=1)


def _attn_fwd(cfg, q, ks, v, sel, name):
    s, tq, tk, tb, nsub = cfg.s, cfg.tq, cfg.tk, cfg.tb, cfg.nsub
    has_sel = sel is not None
    nkp = len(ks)

    def kern(*refs):
        q_ref, k_refs, v_ref = refs[0], refs[1:1 + nkp], refs[1 + nkp]
        sel_ref = refs[2 + nkp] if has_sel else None
        o_ref, lc_ref, lr_ref = refs[2 + nkp + has_sel:5 + nkp + has_sel]
        h, i = pl.program_id(0), pl.program_id(1)
        part = [slice(r * tb, (r + 1) * tb) for r in range(nsub)]
        qs = [q_ref[p, :].astype(BF16) for p in part]
        ts = [i * tq + r * tb + lax.broadcasted_iota(jnp.int32, (tb, 1), 0) for r in range(nsub)]
        sels = [sel_ref[p, :].astype(BF16) for p in part] if has_sel else None

        def load(k0):
            rows = pl.ds(k0, tk)
            return _load_keys(k_refs, rows), v_ref[rows, :].astype(BF16)

        def soft(r, k0, s_raw, vv, emat, carry, diag):
            m, l, acc = carry
            n = k0 + lax.broadcasted_iota(jnp.int32, (1, tk), 1)
            selx = _nn(sels[r], emat) if has_sel else None
            sc, mask = _scores(cfg, s_raw, ts[r], n, h, selx, diag)
            m_new = jnp.maximum(m, jnp.max(sc, axis=1, keepdims=True))
            alpha = jnp.exp2(m - m_new)
            p = jnp.exp2(sc - m_new)
            if mask is not None:
                p = jnp.where(mask, p, 0.0)
            l = alpha * l + jnp.sum(p, axis=1, keepdims=True)
            acc = alpha * acc + _nn(p.astype(BF16), vv)
            return m_new, l, acc

        def step(r, k0, kk, vv, emat, carry, diag):
            return soft(r, k0, _nt(qs[r], kk), vv, emat, carry, diag)

        def chunk(k0, carry, diag):
            kk, vv = load(k0)
            emat = _block_of_key(k0, tk, False) if has_sel else None
            return tuple(step(r, k0, kk, vv, emat, carry[r], diag) for r in range(nsub))

        init = (jnp.full((tb, 1), NEG, F32), jnp.zeros((tb, 1), F32), jnp.zeros((tb, HEAD_V), F32))
        carry = (init,) * nsub
        if cfg.causal:
            buf_a, buf_b = refs[-2:]
            full = (i * tq) // tk

            def scores_into(buf, c):
                kk = _load_keys(k_refs, pl.ds(pl.multiple_of(c * tk, tk), tk))
                for r in range(nsub):
                    buf[r] = _nt(qs[r], kk)

            def consume(buf, c, cr, diag):
                k0 = pl.multiple_of(c * tk, tk)
                vv = v_ref[pl.ds(k0, tk), :].astype(BF16)
                emat = _block_of_key(k0, tk, False) if has_sel else None
                return tuple(soft(r, k0, buf[r], vv, emat, cr[r], diag) for r in range(nsub))

            def pair(p, cr):
                scores_into(buf_b, 2 * p + 1)
                cr = consume(buf_a, 2 * p, cr, False)
                scores_into(buf_a, 2 * p + 2)
                return consume(buf_b, 2 * p + 1, cr, False)

            def odd_tail(cr):
                scores_into(buf_b, full)
                cr = consume(buf_a, full - 1, cr, False)
                return consume(buf_b, full, cr, True)

            scores_into(buf_a, 0)
            carry = lax.fori_loop(0, full // 2, pair, carry)
            carry = lax.cond(full % 2 == 1, odd_tail, lambda cr: consume(buf_a, full, cr, True), carry)
        elif cfg.mode == "win":
            starts = [pl.multiple_of(jnp.maximum(i * tq + r * tb - WIN, 0), tb) for r in range(nsub)]
            carry = tuple(step(r, k0, *load(k0), None, carry[r], True) for r, k0 in enumerate(starts))
        else:
            carry = chunk(0, carry, True)
        for r, (m, l, acc) in enumerate(carry):
            o_ref[part[r], :] = acc / (l + 1e-20)
            lse = m + jnp.log(l + 1e-20) * LOG2E
            lc_ref[0, part[r], :] = lse
            lr_ref[0, r] = _to_row(lse)

    ins = [pl.BlockSpec((tq, q.width), lambda h, i: (i, q.col(h)))]
    ins += [pl.BlockSpec((cfg.sk, p.width), lambda h, i, p=p: (0, p.col(h))) for p in ks]
    ins += [pl.BlockSpec((cfg.sk, HEAD_V), lambda h, i: (0, v.col(h)))]
    args = [q.arr] + [p.arr for p in ks] + [v.arr]
    if has_sel:
        ins.append(pl.BlockSpec((tq, LANE), lambda h, i: (i, 0)))
        args.append(sel)
    return _pcall(
        kern, name=name, grid=(cfg.h, cfg.nq), in_specs=ins,
        out_specs=[pl.BlockSpec((tq, HEAD_V), lambda h, i: (i, h)),
                   pl.BlockSpec((1, tq, 1), lambda h, i: (h, i, 0)),
                   pl.BlockSpec((1, nsub, 1, tb), lambda h, i: (h, i, 0, 0))],
        out_shape=[jax.ShapeDtypeStruct((s, cfg.h * HEAD_V), F32),
                   jax.ShapeDtypeStruct((cfg.h, s, 1), F32),
                   jax.ShapeDtypeStruct((cfg.h, cfg.nb, 1, tb), F32)],
        scratch_shapes=[pltpu.VMEM((nsub, tb, tk), F32)] * 2 if cfg.causal else [],
        compiler_params=pltpu.CompilerParams(dimension_semantics=("parallel", "parallel")),
    )(*args)


def _attn_dq(cfg, q, ks, v, sel, o, lse, do, dq_in, name):
    s, tq, tk, dk, tb, nsub = cfg.s, cfg.tq, cfg.tk, cfg.dk, cfg.tb, cfg.nsub
    has_sel = sel is not None
    has_in = dq_in is not None
    nkp = len(ks)

    def kern(*refs):
        refs = list(refs)
        q_ref, k_refs, v_ref = refs[0], refs[1:1 + nkp], refs[1 + nkp]
        p0 = 2 + nkp
        sel_ref = refs[p0] if has_sel else None
        p0 += has_sel
        o_ref, l_ref, do_ref = refs[p0:p0 + 3]
        p0 += 3
        in_ref = refs[p0] if has_in else None
        p0 += has_in
        dq_ref, dr_ref = refs[p0:p0 + 2]
        h, i = pl.program_id(0), pl.program_id(1)
        part = [slice(r * tb, (r + 1) * tb) for r in range(nsub)]
        qs = [q_ref[p, :].astype(BF16) for p in part]
        ts = [i * tq + r * tb + lax.broadcasted_iota(jnp.int32, (tb, 1), 0) for r in range(nsub)]
        sels = [sel_ref[p, :].astype(BF16) for p in part] if has_sel else None
        dvecs, dobs, lses = [], [], []
        for r, p in enumerate(part):
            dov = do_ref[p, :]
            dvec = jnp.sum(dov * o_ref[p, :], axis=1, keepdims=True)
            dr_ref[0, r] = _to_row(dvec)
            dvecs.append(dvec)
            dobs.append(dov.astype(BF16))
            lses.append(l_ref[0, p, :])

        def load(k0):
            rows = pl.ds(k0, tk)
            return _load_keys(k_refs, rows), v_ref[rows, :].astype(BF16)

        def grad(r, k0, s_raw, dp, kk, emat, acc, diag):
            n = k0 + lax.broadcasted_iota(jnp.int32, (1, tk), 1)
            selx = _nn(sels[r], emat) if has_sel else None
            p, _ = _scores(cfg, s_raw, ts[r], n, h, selx, diag, lses[r])
            ds = p * (dp - dvecs[r])
            return acc + _nn(ds.astype(BF16), kk)

        def step(r, k0, kk, vv, emat, acc, diag):
            return grad(r, k0, _nt(qs[r], kk), _nt(dobs[r], vv), kk, emat, acc, diag)

        def chunk(k0, accs, diag):
            kk, vv = load(k0)
            emat = _block_of_key(k0, tk, False) if has_sel else None
            return tuple(step(r, k0, kk, vv, emat, accs[r], diag) for r in range(nsub))

        accs = (jnp.zeros((tb, dk), F32),) * nsub
        if cfg.causal:
            sa, pa, sb, pb = refs[-4:]
            full = (i * tq) // tk

            def products_into(sbuf, pbuf, c):
                kk, vv = load(pl.multiple_of(c * tk, tk))
                for r in range(nsub):
                    sbuf[r] = _nt(qs[r], kk)
                    pbuf[r] = _nt(dobs[r], vv)

            def consume(sbuf, pbuf, c, ac, diag):
                k0 = pl.multiple_of(c * tk, tk)
                kk = _load_keys(k_refs, pl.ds(k0, tk))
                emat = _block_of_key(k0, tk, False) if has_sel else None
                return tuple(grad(r, k0, sbuf[r], pbuf[r], kk, emat, ac[r], diag) for r in range(nsub))

            def pair(p, ac):
                products_into(sb, pb, 2 * p + 1)
                ac = consume(sa, pa, 2 * p, ac, False)
                products_into(sa, pa, 2 * p + 2)
                return consume(sb, pb, 2 * p + 1, ac, False)

            def odd_tail(ac):
                products_into(sb, pb, full)
                ac = consume(sa, pa, full - 1, ac, False)
                return consume(sb, pb, full, ac, True)

            products_into(sa, pa, 0)
            accs = lax.fori_loop(0, full // 2, pair, accs)
            accs = lax.cond(full % 2 == 1, odd_tail, lambda ac: consume(sa, pa, full, ac, True), accs)
        elif cfg.mode == "win":
            starts = [pl.multiple_of(jnp.maximum(i * tq + r * tb - WIN, 0), tb) for r in range(nsub)]
            accs = tuple(step(r, k0, *load(k0), None, accs[r], True) for r, k0 in enumerate(starts))
        else:
            accs = chunk(0, accs, True)
        for r, p in enumerate(part):
            dq_ref[p, :] = accs[r] * cfg.scale + in_ref[p, :] if has_in else accs[r] * cfg.scale

    qs = pl.BlockSpec((tq, dk), lambda h, i: (i, h))
    ins = [pl.BlockSpec((tq, q.width), lambda h, i: (i, q.col(h)))]
    ins += [pl.BlockSpec((cfg.sk, p.width), lambda h, i, p=p: (0, p.col(h))) for p in ks]
    ins += [pl.BlockSpec((cfg.sk, HEAD_V), lambda h, i: (0, v.col(h)))]
    args = [q.arr] + [p.arr for p in ks] + [v.arr]
    if has_sel:
        ins.append(pl.BlockSpec((tq, LANE), lambda h, i: (i, 0)))
        args.append(sel)
    ins += [pl.BlockSpec((tq, HEAD_V), lambda h, i: (i, o.col(h))),
            pl.BlockSpec((1, tq, 1), lambda h, i: (h, i, 0)),
            pl.BlockSpec((tq, HEAD_V), lambda h, i: (i, do.col(h)))]
    args += [o.arr, lse, do.arr]
    if has_in:
        ins.append(qs)
        args.append(dq_in)
    return _pcall(
        kern, name=name, grid=(cfg.h, cfg.nq), in_specs=ins,
        out_specs=[qs, pl.BlockSpec((1, nsub, 1, tb), lambda h, i: (h, i, 0, 0))],
        out_shape=[jax.ShapeDtypeStruct((s, cfg.h * dk), F32),
                   jax.ShapeDtypeStruct((cfg.h, cfg.nb, 1, tb), F32)],
        scratch_shapes=[pltpu.VMEM((nsub, tb, tk), F32)] * 4 if cfg.causal else [],
        compiler_params=pltpu.CompilerParams(dimension_semantics=("parallel", "parallel")),
    )(*args)


def _attn_dkv(cfg, q, ks, v, selt, lse_r, d_r, do, name):
    s, tq, tkb, dk, kb, ksub = cfg.s, cfg.tb, cfg.tkb, cfg.dk, cfg.kb, cfg.ksub
    nq = cfg.nb
    has_sel = selt is not None
    nkp = len(ks)
    outs = list(ks) + [v]

    def kern(*refs):
        k_refs, v_ref = refs[:nkp], refs[nkp]
        q_ref, do_ref, lr_ref, dr_ref = refs[nkp + 1:nkp + 5]
        st_ref = refs[nkp + 5] if has_sel else None
        out_refs = refs[nkp + 5 + has_sel:2 * nkp + 6 + has_sel]
        sa, pa, sb, pb = refs[-4:]
        j, h = pl.program_id(0), pl.program_id(1)
        k0 = j * tkb
        part = [slice(u * kb, (u + 1) * kb) for u in range(ksub)]
        kks = [_load_keys(k_refs, p) for p in part]
        vvs = [v_ref[p, :].astype(BF16) for p in part]
        ns = [k0 + u * kb + lax.broadcasted_iota(jnp.int32, (kb, 1), 0) for u in range(ksub)]
        emats = [_block_of_key(k0 + u * kb, kb, True) for u in range(ksub)] if has_sel else None

        def load_q(i):
            rows = pl.ds(pl.multiple_of(i * tq, tq), tq)
            return q_ref[rows, :].astype(BF16), do_ref[rows, :].astype(BF16)

        def products_into(sbuf, pbuf, i):
            qi, doi = load_q(i)
            for u in range(ksub):
                sbuf[u] = _nt(kks[u], qi)
                pbuf[u] = _nt(vvs[u], doi)

        def consume(sbuf, pbuf, i, carry):
            qi, doi = load_q(i)
            t = i * tq + lax.broadcasted_iota(jnp.int32, (1, tq), 1)
            selt_i = st_ref[i].astype(BF16) if has_sel else None
            new = []
            for u in range(ksub):
                dk_acc, dv_acc = carry[u]
                selx = _nn(emats[u], selt_i) if has_sel else None
                pt, _ = _scores(cfg, sbuf[u], t, ns[u], h, selx, True, lr_ref[0, i])
                dv_acc = dv_acc + _nn(pt.astype(BF16), doi)
                dst = pt * (pbuf[u] - dr_ref[0, i])
                new.append((dk_acc + _nn(dst.astype(BF16), qi), dv_acc))
            return tuple(new)

        if cfg.causal:
            first, count = k0 // tq, nq - k0 // tq
        elif cfg.mode == "win":
            first = k0 // tq
            count = jnp.minimum((k0 + tkb + WIN - 2) // tq + 1, nq) - first
        else:
            first, count = 0, nq

        def pair(p, cr):
            i0 = first + 2 * p
            products_into(sb, pb, i0 + 1)
            cr = consume(sa, pa, i0, cr)
            products_into(sa, pa, i0 + 2)
            return consume(sb, pb, i0 + 1, cr)

        carry = ((jnp.zeros((kb, dk), F32), jnp.zeros((kb, HEAD_V), F32)),) * ksub
        products_into(sa, pa, first)
        carry = lax.fori_loop(0, count // 2 - 1, pair, carry)
        last = first + count - 2
        products_into(sb, pb, last + 1)
        carry = consume(sa, pa, last, carry)
        carry = consume(sb, pb, last + 1, carry)
        for u, (dk_acc, dv_acc) in enumerate(carry):
            vals, off = [], 0
            for p in ks:
                vals.append(dk_acc[:, off:off + p.width] * cfg.scale)
                off += p.width
            vals.append(dv_acc)
            for src, ref, val in zip(outs, out_refs, vals):
                if src.per_head:
                    ref[part[u], :] = val
                else:
                    @pl.when(h == 0)
                    def _(ref=ref, val=val, u=u):
                        ref[part[u], :] = val

                    @pl.when(h > 0)
                    def _(ref=ref, val=val, u=u):
                        ref[part[u], :] += val

    rowv = pl.BlockSpec((1, nq, 1, tq), lambda j, h: (h, 0, 0, 0))
    ins = [pl.BlockSpec((tkb, p.width), lambda j, h, p=p: (j, p.col(h))) for p in ks]
    ins += [pl.BlockSpec((tkb, HEAD_V), lambda j, h: (j, v.col(h))),
            pl.BlockSpec((s, q.width), lambda j, h: (0, q.col(h))),
            pl.BlockSpec((s, HEAD_V), lambda j, h: (0, do.col(h))), rowv, rowv]
    args = [p.arr for p in ks] + [v.arr, q.arr, do.arr, lse_r, d_r]
    if has_sel:
        ins.append(pl.BlockSpec((nq, LANE, tq), lambda j, h: (0, 0, 0)))
        args.append(selt)
    out_specs = [pl.BlockSpec((tkb, p.width), lambda j, h, p=p: (j, h if p.per_head else 0)) for p in outs]
    out_shape = [jax.ShapeDtypeStruct((cfg.sk, (cfg.h if p.per_head else 1) * p.width), F32) for p in outs]
    assert nq % 2 == 0 and (cfg.mode in ("cmp", "mem") or tkb % (2 * tq) == 0), (nq, tkb, tq)
    return _pcall(
        kern, name=name, grid=(cfg.sk // tkb, cfg.h), in_specs=ins, out_specs=out_specs, out_shape=out_shape,
        scratch_shapes=[pltpu.VMEM((ksub, kb, tq), F32)] * 4,
        compiler_params=pltpu.CompilerParams(dimension_semantics=("parallel", "arbitrary")),
    )(*args)


def _all_heads(cfg, src, rows, key=False):
    if src.per_head:
        assert src.col0 % cfg.h == 0
        width, col = cfg.h * src.width, src.col0 // cfg.h
    else:
        width, col = src.width, src.col0
    return pl.BlockSpec((rows, width), (lambda i: (0, col)) if key else (lambda i: (i, col)))


def _head_cols(src, hh):
    return slice(hh * src.width, (hh + 1) * src.width) if src.per_head else slice(None)


def _key_window(cfg, i, r):
    if cfg.mode == "win":
        return pl.ds(pl.multiple_of(jnp.maximum(i * cfg.tq + r * cfg.tb - WIN, 0), cfg.tb), cfg.tk)
    return pl.ds(0, cfg.tk)


def _attn_fwd_small(cfg, q, ks, v, name, overlap=None):
    s, tq, tk, tb, nsub, nh = cfg.s, cfg.tq, cfg.tk, cfg.tb, cfg.nsub, cfg.h
    nkp = len(ks)
    select = overlap is not None
    n_s = s // SLC_LEN
    top_n = min(SLC_TOPN, n_s)

    def kern(*refs):
        q_ref, k_refs, v_ref = refs[0], refs[1:1 + nkp], refs[1 + nkp]
        ov_ref = refs[2 + nkp] if select else None
        o_ref, lc_ref, lr_ref = refs[2 + nkp + select:5 + nkp + select]
        i = pl.program_id(0)
        imps = [jnp.zeros((tb, LANE), F32)] * nsub
        for r in range(nsub):
            rows = slice(r * tb, (r + 1) * tb)
            t = i * tq + r * tb + lax.broadcasted_iota(jnp.int32, (tb, 1), 0)
            win = _key_window(cfg, i, r)
            n = win.start + lax.broadcasted_iota(jnp.int32, (1, tk), 1)
            for hh in range(nh):
                qv = q_ref[rows, hh * cfg.dk:(hh + 1) * cfg.dk].astype(BF16)
                kk = _load_keys([kr.at[:, _head_cols(p, hh)] for kr, p in zip(k_refs, ks)], win)
                vv = v_ref[win, _head_cols(v, hh)].astype(BF16)
                sc, mask = _scores(cfg, _nt(qv, kk), t, n, hh, None, True)
                m = jnp.max(sc, axis=1, keepdims=True)
                e = jnp.exp2(sc - m)
                if mask is not None:
                    e = jnp.where(mask, e, 0.0)
                l = jnp.sum(e, axis=1, keepdims=True)
                o_ref[rows, hh * HEAD_V:(hh + 1) * HEAD_V] = _nn(e.astype(BF16), vv) / (l + 1e-20)
                lse = m + jnp.log(l + 1e-20) * LOG2E
                lc_ref[hh, rows, :] = lse
                lr_ref[hh, r] = _to_row(lse)
                if select:
                    imps[r] = imps[r] + _nn((e / (l + 1e-20)).astype(BF16), ov_ref[...])
        if select:
            sel_ref, selt_ref, imp_t = refs[5 + nkp + select:8 + nkp + select]
            for r in range(nsub):
                t = i * tq + r * tb + lax.broadcasted_iota(jnp.int32, (tb, 1), 0)
                j = lax.broadcasted_iota(jnp.int32, (tb, LANE), 1)
                cur = t >> 6
                imp = jnp.where((j == 0) | (j == cur) | (j == cur - 1), 1e9, imps[r])
                imp = jnp.where(j > cur, -1e9, imp)
                imp_t[r] = jnp.transpose(imp)
                mine = imp_t[r, 0:n_s, :]
                jrow = lax.broadcasted_iota(jnp.int32, (n_s, tb), 0)

                def count(k, rank):
                    other = imp_t[r, pl.ds(k, 1), :]
                    ahead = (other > mine) | ((other == mine) & (k < jrow))
                    return rank + jnp.where(ahead, 1.0, 0.0)

                rank = lax.fori_loop(0, n_s, count, jnp.zeros((n_s, tb), F32))
                cur_t = (i * tq + r * tb + lax.broadcasted_iota(jnp.int32, (1, tb), 1)) >> 6
                rejected = jnp.where((rank < top_n) & (jrow <= cur_t), 0.0, 1.0)
                if n_s < LANE:
                    rejected = jnp.concatenate([rejected, jnp.ones((LANE - n_s, tb), F32)], axis=0)
                selt_ref[r] = rejected
                sel_ref[r * tb:(r + 1) * tb, :] = jnp.transpose(rejected)

    ins = [_all_heads(cfg, q, tq)] + [_all_heads(cfg, p, cfg.sk, True) for p in ks]
    ins += [_all_heads(cfg, v, cfg.sk, True)]
    args = [q.arr] + [p.arr for p in ks] + [v.arr]
    out_specs = [pl.BlockSpec((tq, nh * HEAD_V), lambda i: (i, 0)),
                 pl.BlockSpec((nh, tq, 1), lambda i: (0, i, 0)),
                 pl.BlockSpec((nh, nsub, 1, tb), lambda i: (0, i, 0, 0))]
    out_shape = [jax.ShapeDtypeStruct((s, nh * HEAD_V), F32), jax.ShapeDtypeStruct((nh, s, 1), F32),
                 jax.ShapeDtypeStruct((nh, cfg.nb, 1, tb), F32)]
    scratch = []
    if select:
        ins.append(pl.BlockSpec((cfg.sk, LANE), lambda i: (0, 0)))
        args.append(overlap)
        out_specs += [pl.BlockSpec((tq, LANE), lambda i: (i, 0)), pl.BlockSpec((nsub, LANE, tb), lambda i: (i, 0, 0))]
        out_shape += [jax.ShapeDtypeStruct((s, LANE), F32), jax.ShapeDtypeStruct((cfg.nb, LANE, tb), F32)]
        scratch = [pltpu.VMEM((nsub, LANE, tb), F32)]
    return _pcall(kern, name=name, grid=(cfg.nq,), in_specs=ins, out_specs=out_specs, out_shape=out_shape,
                  scratch_shapes=scratch,
                  compiler_params=pltpu.CompilerParams(dimension_semantics=("parallel",)))(*args)


def _attn_dq_small(cfg, q, ks, v, o, lse, do, dq_in, name):
    s, tq, tk, tb, nsub, nh, dk = cfg.s, cfg.tq, cfg.tk, cfg.tb, cfg.nsub, cfg.h, cfg.dk
    nkp = len(ks)
    has_in = dq_in is not None

    def kern(*refs):
        q_ref, k_refs, v_ref = refs[0], refs[1:1 + nkp], refs[1 + nkp]
        o_ref, l_ref, do_ref = refs[2 + nkp:5 + nkp]
        in_ref = refs[5 + nkp] if has_in else None
        dq_ref, dr_ref = refs[5 + nkp + has_in:7 + nkp + has_in]
        i = pl.program_id(0)
        for r in range(nsub):
            rows = slice(r * tb, (r + 1) * tb)
            t = i * tq + r * tb + lax.broadcasted_iota(jnp.int32, (tb, 1), 0)
            win = _key_window(cfg, i, r)
            n = win.start + lax.broadcasted_iota(jnp.int32, (1, tk), 1)
            for hh in range(nh):
                vcols = slice(hh * HEAD_V, (hh + 1) * HEAD_V)
                qcols = slice(hh * dk, (hh + 1) * dk)
                qv = q_ref[rows, qcols].astype(BF16)
                kk = _load_keys([kr.at[:, _head_cols(p, hh)] for kr, p in zip(k_refs, ks)], win)
                vv = v_ref[win, _head_cols(v, hh)].astype(BF16)
                dov = do_ref[rows, vcols]
                dvec = jnp.sum(dov * o_ref[rows, vcols], axis=1, keepdims=True)
                dr_ref[hh, r] = _to_row(dvec)
                p, _ = _scores(cfg, _nt(qv, kk), t, n, hh, None, True, l_ref[hh, rows, :])
                ds = p * (_nt(dov.astype(BF16), vv) - dvec)
                dq = _nn(ds.astype(BF16), kk) * cfg.scale
                dq_ref[rows, qcols] = dq + in_ref[rows, qcols] if has_in else dq

    qs = pl.BlockSpec((tq, nh * dk), lambda i: (i, 0))
    ins = [_all_heads(cfg, q, tq)] + [_all_heads(cfg, p, cfg.sk, True) for p in ks]
    ins += [_all_heads(cfg, v, cfg.sk, True)]
    ins += [_all_heads(cfg, o, tq), pl.BlockSpec((nh, tq, 1), lambda i: (0, i, 0)), _all_heads(cfg, do, tq)]
    args = [q.arr] + [p.arr for p in ks] + [v.arr, o.arr, lse, do.arr]
    if has_in:
        ins.append(qs)
        args.append(dq_in)
    return _pcall(
        kern, name=name, grid=(cfg.nq,), in_specs=ins,
        out_specs=[qs, pl.BlockSpec((nh, nsub, 1, tb), lambda i: (0, i, 0, 0))],
        out_shape=[jax.ShapeDtypeStruct((s, nh * dk), F32), jax.ShapeDtypeStruct((nh, cfg.nb, 1, tb), F32)],
        compiler_params=pltpu.CompilerParams(dimension_semantics=("parallel",)))(*args)


def _attn_bwd(cfg, q, ks, v, sel, selt, o, lse, lse_r, do, dq_in, name):
    if not cfg.causal:
        dq, d_r = _attn_dq_small(cfg, q, ks, v, o, lse, do, dq_in, name + "_dq")
        res = _attn_dkv(cfg, q, ks, v, selt, lse_r, d_r, do, name + "_dkv")
        return dq, res[:-1], res[-1]
    dq, d_r = _attn_dq(cfg, q, ks, v, sel, o, lse, do, dq_in, name + "_dq")
    res = _attn_dkv(cfg, q, ks, v, selt, lse_r, d_r, do, name + "_dkv")
    return dq, res[:-1], res[-1]


def _silu_grad(pre):
    sg = _sigmoid(pre)
    return sg * (1.0 + pre * (1.0 - sg))


def _compress_fwd(a_lo, a_hi, pe_lo, pe_hi, w1_lo, w1_hi, w2, name):
    n, dp = a_lo.shape[0], w2.shape[1]

    def kern(alo, ahi, plo, phi, w1l, w1h, w2r, out_ref, pre_ref):
        xl = (alo[...] + plo[...]).astype(BF16)
        xh = (ahi[...] + phi[...]).astype(BF16)
        pre = _nn(xl, w1l[...]) + _nn(xh, w1h[...])
        act = pre * _sigmoid(pre)
        out_ref[...] = _nn(act.astype(BF16), w2r[...]).astype(BF16)
        pre_ref[...] = pre

    return _pcall(kern, name=name,
                  out_shape=[jax.ShapeDtypeStruct((n, dp), BF16), jax.ShapeDtypeStruct((n, dp), F32)],
                  )(a_lo, a_hi, pe_lo, pe_hi, w1_lo, w1_hi, w2)


def _compress_bwd(a_lo, a_hi, pe_lo, pe_hi, w1_lo, w1_hi, w2, pre, pre_sh, dout, dout_sh, name):
    n, ln = a_lo.shape
    dp = w2.shape[1]

    def kern(alo, ahi, plo, phi, w1l, w1h, w2r, pre_ref, presh_ref, do_ref, dosh_ref,
             da_ref, dpl_ref, dph_ref, dw1l_ref, dw1h_ref, dw2_ref):
        prev = pre_ref[...]
        act = prev * _sigmoid(prev)
        dob = do_ref[...].astype(BF16)
        w2v = w2r[...]
        dpre = (_nt(dob, w2v) * _silu_grad(prev)).astype(BF16)
        dpre_sh = (_nt(dosh_ref[...].astype(BF16), w2v) * _silu_grad(presh_ref[...])).astype(BF16)
        dw2_ref[...] = _nn(act.T.astype(BF16), dob)
        xl = alo[...] + plo[...]
        xh = ahi[...] + phi[...]
        dw1l_ref[...] = _nn(xl.T.astype(BF16), dpre)
        dw1h_ref[...] = _nn(xh.T.astype(BF16), dpre)
        dal = _nt(dpre, w1l[...])
        dah_sh = _nt(dpre_sh, w1h[...])
        da_ref[...] = dal + dah_sh
        dpl_ref[...] = jnp.sum(dal, axis=0, keepdims=True)
        dph_ref[...] = jnp.sum(dah_sh, axis=0, keepdims=True)

    return _pcall(
        kern, name=name,
        out_shape=[jax.ShapeDtypeStruct((n, ln), F32), jax.ShapeDtypeStruct((1, ln), F32),
                   jax.ShapeDtypeStruct((1, ln), F32), jax.ShapeDtypeStruct((ln, dp), F32),
                   jax.ShapeDtypeStruct((ln, dp), F32), jax.ShapeDtypeStruct((dp, dp), F32)],
    )(a_lo, a_hi, pe_lo, pe_hi, w1_lo, w1_hi, w2, pre, pre_sh, dout, dout_sh)


def _nsa_combine(o_cmp, o_slc, o_win, gl):
    s, w = o_cmp.shape
    tr = _tile(s, 512)

    def kern(a_ref, b_ref, c_ref, g_ref, o_ref):
        g = _sigmoid(g_ref[...])
        for h in range(NSA_HEADS):
            cs = slice(h * HEAD_V, (h + 1) * HEAD_V)
            o_ref[:, cs] = (g[:, 3 * h:3 * h + 1] * a_ref[:, cs] + g[:, 3 * h + 1:3 * h + 2] * b_ref[:, cs]
                            + g[:, 3 * h + 2:3 * h + 3] * c_ref[:, cs])

    row = pl.BlockSpec((tr, w), lambda i: (i, 0))
    return _pcall(kern, name="nsa_combine", grid=(s // tr,),
                  in_specs=[row, row, row, pl.BlockSpec((tr, LANE), lambda i: (i, gl.col0))], out_specs=row,
                  out_shape=jax.ShapeDtypeStruct((s, w), F32))(o_cmp, o_slc, o_win, gl.arr)


def _nsa_combine_bwd(do_cat, o_cmp, o_slc, o_win, gl):
    s, w = o_cmp.shape
    tr = _tile(s, 512)

    def kern(d_ref, a_ref, b_ref, c_ref, g_ref, da_ref, db_ref, dc_ref, dg_ref):
        g = _sigmoid(g_ref[...])
        lane = lax.broadcasted_iota(jnp.int32, (tr, LANE), 1)
        dgl = jnp.zeros((tr, LANE), F32)
        for h in range(NSA_HEADS):
            cs = slice(h * HEAD_V, (h + 1) * HEAD_V)
            dv = d_ref[:, cs]
            for b, (src, dst) in enumerate(((a_ref, da_ref), (b_ref, db_ref), (c_ref, dc_ref))):
                gate = g[:, 3 * h + b:3 * h + b + 1]
                dst[:, cs] = gate * dv
                dgate = jnp.sum(dv * src[:, cs], axis=1, keepdims=True)
                dgl = jnp.where(lane == 3 * h + b, dgate * gate * (1.0 - gate), dgl)
        dg_ref[...] = dgl

    row = pl.BlockSpec((tr, w), lambda i: (i, 0))
    tab = pl.BlockSpec((tr, LANE), lambda i: (i, 0))
    return _pcall(kern, name="nsa_combine_bwd", grid=(s // tr,),
                  in_specs=[pl.BlockSpec((tr, w), lambda i: (i, 2)), row, row, row,
                            pl.BlockSpec((tr, LANE), lambda i: (i, gl.col0))],
                  out_specs=[row, row, row, tab],
                  out_shape=[jax.ShapeDtypeStruct((s, w), F32)] * 3 + [jax.ShapeDtypeStruct((s, LANE), F32)],
                  )(do_cat, o_cmp, o_slc, o_win, gl.arr)


def _gate_fwd(o_mla, o_nsa, o_mem, hp):
    s = o_mla.shape[0]
    tr = _tile(s, 256)

    def kern(a_ref, b_ref, c_ref, z_ref, u_ref):
        z = z_ref[...]
        sz = z * _sigmoid(z)
        u_ref[:, 0:1024] = (a_ref[...] * sz[:, 0:1024]).astype(BF16)
        u_ref[:, 1024:1536] = (b_ref[...] * sz[:, 1024:1536]).astype(BF16)
        u_ref[:, 1536:2048] = (c_ref[...] * sz[:, 1536:2048]).astype(BF16)

    return _pcall(
        kern, name="gate_fwd", grid=(s // tr,),
        in_specs=[pl.BlockSpec((tr, 1024), lambda i: (i, 0)), pl.BlockSpec((tr, 512), lambda i: (i, 0)),
                  pl.BlockSpec((tr, 512), lambda i: (i, 0)), pl.BlockSpec((tr, 2048), lambda i: (i, 2))],
        out_specs=pl.BlockSpec((tr, 2048), lambda i: (i, 0)),
        out_shape=jax.ShapeDtypeStruct((s, 2048), BF16))(o_mla, o_nsa, o_mem, hp)


def _gate_bwd(du, o_mla, o_nsa, o_mem, hp):
    s = du.shape[0]
    tr = _tile(s, 256)

    def kern(d_ref, a_ref, b_ref, c_ref, z_ref, do_ref, dz_ref):
        z = z_ref[...]
        sg = _sigmoid(z)
        sz = z * sg
        dsz = sg * (1.0 + z * (1.0 - sg))
        d = d_ref[...]
        do_ref[...] = d * sz
        dz_ref[:, 0:1024] = d[:, 0:1024] * a_ref[...] * dsz[:, 0:1024]
        dz_ref[:, 1024:1536] = d[:, 1024:1536] * b_ref[...] * dsz[:, 1024:1536]
        dz_ref[:, 1536:2048] = d[:, 1536:2048] * c_ref[...] * dsz[:, 1536:2048]

    wide = pl.BlockSpec((tr, 2048), lambda i: (i, 0))
    return _pcall(
        kern, name="gate_bwd", grid=(s // tr,),
        in_specs=[wide, pl.BlockSpec((tr, 1024), lambda i: (i, 0)), pl.BlockSpec((tr, 512), lambda i: (i, 0)),
                  pl.BlockSpec((tr, 512), lambda i: (i, 0)), pl.BlockSpec((tr, 2048), lambda i: (i, 2))],
        out_specs=[wide, wide],
        out_shape=[jax.ShapeDtypeStruct((s, 2048), F32)] * 2)(du, o_mla, o_nsa, o_mem, hp)


def _tile2d(rows, cols, arrays):
    if rows % 16 == 0:
        return _row_tile(rows, cols * arrays), cols
    want = max(LANE, BLOCK_BYTES // (rows * 4 * arrays) // LANE * LANE)
    tc = LANE
    for t in range(LANE, cols + 1, LANE):
        if cols % t == 0 and t <= want:
            tc = t
    return rows, tc


def _sum_slots(buf, name):
    n, rows, cols = buf.shape
    tr, tc = _tile2d(rows, cols, n)

    def kern(b_ref, o_ref):
        acc = b_ref[0].astype(F32)
        for i in range(1, n):
            acc = acc + b_ref[i].astype(F32)
        o_ref[...] = acc

    return _pcall(kern, name=name, grid=(rows // tr, cols // tc),
                  in_specs=[pl.BlockSpec((n, tr, tc), lambda i, j: (0, i, j))],
                  out_specs=pl.BlockSpec((tr, tc), lambda i, j: (i, j)),
                  out_shape=jax.ShapeDtypeStruct((rows, cols), F32))(buf)


def _pair_sum(g4, theirs, core, axis, name):
    n, rows, cols = theirs.shape
    tr, tc = _tile2d(rows, cols, 1)
    nbr, nbc = rows // tr, cols // tc

    def kern(c_ref, a_ref, b_ref, o_ref):
        o_ref[...] = (a_ref[...] + b_ref[...]).astype(BF16)

    blk = (1, tr, tc)
    mine = ((lambda s, i, j, c: (s, c[0] * nbr + i, j)) if axis == 0
            else (lambda s, i, j, c: (s, i, c[0] * nbc + j)))
    grid_spec = pltpu.PrefetchScalarGridSpec(
        num_scalar_prefetch=1, grid=(n, nbr, nbc),
        in_specs=[pl.BlockSpec(blk, mine), pl.BlockSpec(blk, lambda s, i, j, c: (s, i, j))],
        out_specs=pl.BlockSpec(blk, lambda s, i, j, c: (s, i, j)))
    return _pcall(kern, name=name, grid_spec=grid_spec,
                  out_shape=jax.ShapeDtypeStruct((n, rows, cols), BF16))(core, g4, theirs)


def _adamw(w, g, m, v, name):
    rows, cols = w.shape
    tr, tc = _tile2d(rows, cols, 4)
    bc1 = 1.0 - ADAM_B1 ** ADAM_STEP
    bc2 = 1.0 - ADAM_B2 ** ADAM_STEP

    def kern(w_ref, g_ref, m_ref, v_ref, d_ref, mo_ref, vo_ref):
        gv = g_ref[...]
        mn = ADAM_B1 * m_ref[...] + (1.0 - ADAM_B1) * gv
        vn = ADAM_B2 * v_ref[...] + (1.0 - ADAM_B2) * (gv * gv)
        d_ref[...] = -ADAM_LR * ((mn / bc1) / (jnp.sqrt(vn / bc2) + ADAM_EPS) + ADAM_WD * w_ref[...])
        mo_ref[...] = mn
        vo_ref[...] = vn

    blk = pl.BlockSpec((tr, tc), lambda i, j: (i, j))
    return _pcall(kern, name=name, grid=(rows // tr, cols // tc), in_specs=[blk] * 4, out_specs=[blk] * 3,
                  out_shape=[jax.ShapeDtypeStruct((rows, cols), F32)] * 3)(w, g, m, v)


ANY = pl.BlockSpec(memory_space=pl.ANY)


def _place():
    x, y, c = lax.axis_index("x"), lax.axis_index("y"), lax.axis_index("c")
    chips = [(1 - x, y), (x, 1 - y), (1 - x, 1 - y)]
    return x, y, c, chips


def _remote(src, dst, send_sem, recv_sem, to):
    return pltpu.make_async_remote_copy(src_ref=src, dst_ref=dst, send_sem=send_sem, recv_sem=recv_sem,
                                        device_id=to, device_id_type=MESH)


def _half(ref, lead, core, axis):
    size = ref.shape[len(lead) + axis] // 2
    cut = pl.ds(core * size, size)
    return ref.at[tuple(lead) + ((cut, slice(None)) if axis == 0 else (slice(None), cut))]


def _gather_shards(ws, axes):
    side = _gather_side(ws, axes)

    def body(*refs):
        nw = len(ws)
        split = (refs[:nw], refs[nw:2 * nw], refs[2 * nw:])
        side.phase("start", *split)
        side.phase("finish", *split)

    return _pcall(body, name="gather_shards", in_specs=[ANY] * len(ws), out_specs=[ANY] * len(ws),
                  out_shape=side.out_shape, scratch_shapes=side.scratch)(*ws)


def _gather_side(ws, axes):
    nw = len(ws)

    def phase(which, w_refs, out_refs, sems):
        send_sems, recv_sems = sems
        x, y, c, chips = _place()
        me = 2 * x + y
        sibling = (x, y, 1 - c)

        def part(i, slot, core):
            return _half(out_refs[i], (slot,), core, axes[i])

        def copy(sem, src, dst, to):
            return _remote(src, dst, send_sems.at[sem], recv_sems.at[sem], to)

        first = [copy(j * nw + i, _half(w_refs[i], (), c, axes[i]), part(i, me, c), (*chip, c))
                 for j, chip in enumerate(chips) for i in range(nw)]
        if which == "start":
            for cp in first:
                cp.start()
            return
        passed = []
        for j, (cx, cy) in enumerate(chips):
            slot = 2 * cx + cy
            for i in range(nw):
                copy(j * nw + i, part(i, slot, c), part(i, slot, c), (x, y, c)).wait_recv()
                fwd = copy((3 + j) * nw + i, part(i, slot, c), part(i, slot, c), sibling)
                fwd.start()
                passed.append(fwd)
        for j, (cx, cy) in enumerate(chips):
            slot = 2 * cx + cy
            for i in range(nw):
                copy((3 + j) * nw + i, part(i, slot, 1 - c), part(i, slot, 1 - c), (x, y, c)).wait_recv()
        for cp in first + passed:
            cp.wait_send()

    return _Side(list(ws), [jax.ShapeDtypeStruct((4,) + w.shape, w.dtype) for w in ws],
                 [pltpu.SemaphoreType.DMA((6 * nw,)), pltpu.SemaphoreType.DMA((6 * nw,))], phase)


def _half_shape(shape, axis):
    return tuple(d // 2 if k == len(shape) - 2 + axis else d for k, d in enumerate(shape))


def _pair_exchange(gs, axes, name):
    nw = len(gs)

    def body(*refs):
        g_refs, out_refs = refs[:nw], refs[nw:2 * nw]
        send_sems, recv_sems = refs[2 * nw:]
        x, y, c, _ = _place()
        cps = []
        for i in range(nw):
            cp = _remote(_half(g_refs[i], (slice(None),), 1 - c, axes[i]), out_refs[i],
                         send_sems.at[i], recv_sems.at[i], (x, y, 1 - c))
            cp.start()
            cps.append(cp)
        for cp in cps:
            cp.wait()

    return _pcall(body, name=name, in_specs=[ANY] * nw, out_specs=[ANY] * nw,
                  out_shape=[jax.ShapeDtypeStruct(_half_shape(g.shape, a), g.dtype) for g, a in zip(gs, axes)],
                  scratch_shapes=[pltpu.SemaphoreType.DMA((nw,)), pltpu.SemaphoreType.DMA((nw,))])(*gs)


def _chip_exchange(ps):
    side = _chip_side(ps)

    def body(*refs):
        nw = len(ps)
        split = (refs[:nw], refs[nw:2 * nw], refs[2 * nw:])
        side.phase("start", *split)
        side.phase("finish", *split)

    return _pcall(body, name="chip_exchange", in_specs=[ANY] * len(ps), out_specs=[ANY] * len(ps),
                  out_shape=side.out_shape, scratch_shapes=side.scratch)(*ps)


def _chip_side(ps):
    nw = len(ps)

    def phase(which, p_refs, out_refs, sems):
        send_sems, recv_sems, local_sems = sems
        x, y, c, chips = _place()
        me = 2 * x + y
        mine = [pltpu.make_async_copy(p_refs[i].at[me], out_refs[i].at[me], local_sems.at[i]) for i in range(nw)]
        sends = [_remote(p_refs[i].at[2 * cx + cy], out_refs[i].at[me], send_sems.at[j * nw + i],
                         recv_sems.at[j * nw + i], (cx, cy, c))
                 for j, (cx, cy) in enumerate(chips) for i in range(nw)]
        if which == "start":
            for cp in mine + sends:
                cp.start()
            return
        for j, (cx, cy) in enumerate(chips):
            slot = 2 * cx + cy
            for i in range(nw):
                _remote(out_refs[i].at[slot], out_refs[i].at[slot], send_sems.at[j * nw + i],
                        recv_sems.at[j * nw + i], (x, y, c)).wait_recv()
        for cp in sends:
            cp.wait_send()
        for cp in mine:
            cp.wait()

    return _Side(list(ps), [jax.ShapeDtypeStruct(p.shape, p.dtype) for p in ps],
                 [pltpu.SemaphoreType.DMA((3 * nw,)), pltpu.SemaphoreType.DMA((3 * nw,)),
                  pltpu.SemaphoreType.DMA((nw,))], phase)


def _half_exchange(ts, axes):
    nw = len(ts)

    def body(*refs):
        t_refs, out_refs = refs[:nw], refs[nw:2 * nw]
        send_sems, recv_sems = refs[2 * nw:]
        x, y, c, _ = _place()
        sends = []
        for i in range(nw):
            cp = _remote(t_refs[i], _half(out_refs[i], (), c, axes[i]), send_sems.at[i], recv_sems.at[i],
                         (x, y, 1 - c))
            cp.start()
            sends.append(cp)
        for i in range(nw):
            _remote(t_refs[i], _half(out_refs[i], (), 1 - c, axes[i]), send_sems.at[i], recv_sems.at[i],
                    (x, y, c)).wait_recv()
        for cp in sends:
            cp.wait_send()

    def whole(t, a):
        return tuple(2 * d if k == a else d for k, d in enumerate(t.shape))

    return _pcall(body, name="half_exchange", in_specs=[ANY] * nw, out_specs=[ANY] * nw,
                  out_shape=[jax.ShapeDtypeStruct(whole(t, a), t.dtype) for t, a in zip(ts, axes)],
                  scratch_shapes=[pltpu.SemaphoreType.DMA((nw,)), pltpu.SemaphoreType.DMA((nw,))])(*ts)


def _gather_all(v):
    rows, cols = v.shape

    def body(v_ref, out_ref, send_sems, recv_sems, local_sem):
        x, y, c, _ = _place()
        me = 4 * x + 2 * y + c
        mine = pltpu.make_async_copy(v_ref, out_ref.at[me], local_sem)
        mine.start()
        sends = []
        for d in range(1, 8):
            peer = (x ^ (d >> 2), y ^ ((d >> 1) & 1), c ^ (d & 1))
            cp = _remote(v_ref, out_ref.at[me], send_sems.at[d - 1], recv_sems.at[d - 1], peer)
            cp.start()
            sends.append(cp)
        for d in range(1, 8):
            slot = 4 * (x ^ (d >> 2)) + 2 * (y ^ ((d >> 1) & 1)) + (c ^ (d & 1))
            _remote(v_ref, out_ref.at[slot], send_sems.at[d - 1], recv_sems.at[d - 1], (x, y, c)).wait_recv()
        for cp in sends:
            cp.wait_send()
        mine.wait()

    return _pcall(body, name="gather_all", in_specs=[ANY], out_specs=ANY,
                  out_shape=jax.ShapeDtypeStruct((8, rows, cols), v.dtype),
                  scratch_shapes=[pltpu.SemaphoreType.DMA((7,)), pltpu.SemaphoreType.DMA((7,)),
                                  pltpu.SemaphoreType.DMA])(v)


def _pad_cols(a, width):
    return a if a.shape[1] == width else jnp.pad(a, ((0, 0), (0, width - a.shape[1])))


def _unpad_segments():
    z = PAD["z"]
    segs = [(PAD["c_q"], 0, 512), (PAD["c_kv"], 512, 512), (PAD["k_rope"], 1024, 64), (z, 1088, 1024)]
    segs += [(PAD["q_nsa"] + 256 * h, 2112 + NSA_DK * h, NSA_DK) for h in range(NSA_HEADS)]
    for name, rows in (("k_c", 192), ("v_c", 128), ("k_s", 192), ("v_s", 128), ("k_w", 192), ("v_w", 128),
                       ("g_nsa", 12)):
        segs.append((PAD[name], ORIG[name][0], rows))
    segs += [(z + 1024, ORIG["z_nsa"][0], 512), (PAD["q_mem"], ORIG["q_mem"][0], 512),
             (z + 1536, ORIG["z_mem"][0], 512)]
    return segs


def _w_in_grad_slots(gt):
    rows, cols = gt.shape
    shard = sum(n for _, _, n in _unpad_segments()) // 4
    tc = 256
    pieces = []
    for src, dst, n in _unpad_segments():
        while n:
            slot, off = divmod(dst, shard)
            take = min(n, shard - off)
            pieces.append((src, slot, off, take))
            src, dst, n = src + take, dst + take, n - take

    def kern(g_ref, o_ref):
        for src, slot, off, take in pieces:
            o_ref[slot, off:off + take, :] = g_ref[src:src + take, :]

    return _pcall(kern, name="w_in_grad_slots", grid=(cols // tc,),
                  in_specs=[pl.BlockSpec((rows, tc), lambda i: (0, i))],
                  out_specs=pl.BlockSpec((4, shard, tc), lambda i: (0, 0, i)),
                  out_shape=jax.ShapeDtypeStruct((4, shard, cols), F32))(gt)


def _w_in_from_slots(ws):
    nslot, shard, cols = ws.shape
    tc = 256
    pieces = []
    for dst, src, n in _unpad_segments() + [(PAD["k_rope"] + 64, ORIG["k_rope"][0], 64)]:
        while n:
            slot, off = divmod(src, shard)
            take = min(n, shard - off)
            pieces.append((dst, slot, off, take))
            src, dst, n = src + take, dst + take, n - take

    def kern(w_ref, o_ref):
        o_ref[...] = jnp.zeros_like(o_ref)
        for dst, slot, off, take in pieces:
            o_ref[dst:dst + take, :] = w_ref[slot, off:off + take, :]

    return _pcall(kern, name="w_in_from_slots", grid=(cols // tc,),
                  in_specs=[pl.BlockSpec((nslot, shard, tc), lambda i: (0, 0, i))],
                  out_specs=pl.BlockSpec((D_PAD, tc), lambda i: (0, i)),
                  out_shape=jax.ShapeDtypeStruct((D_PAD, cols), ws.dtype))(ws)


def _rope_tables(s):
    pos = jnp.arange(s, dtype=F32)
    inv_freq = ROPE_THETA ** (-jnp.arange(0, 64, 2, dtype=F32) / 64)
    ang = pos[:, None] * inv_freq[None, :]
    cos, sin = jnp.cos(ang), jnp.sin(ang)
    z = jnp.zeros((s, 64), F32)
    return jnp.concatenate([cos, cos, z], axis=1), jnp.concatenate([-sin, sin, z], axis=1)


def _overlap_table(s):
    n_c, n_s = s // CMP_STRIDE, s // SLC_LEN
    c0 = np.arange(n_c)[:, None] * CMP_STRIDE
    s0 = np.arange(LANE)[None, :] * SLC_LEN
    ov = (c0 < s0 + SLC_LEN) & (c0 + CMP_LEN > s0) & (np.arange(n_c)[:, None] < n_c - 1) & (np.arange(LANE)[None, :] < n_s)
    return jnp.asarray(ov.astype(np.float32), dtype=BF16)


def _shift_down(a):
    return jnp.concatenate([jnp.zeros((8, a.shape[1]), a.dtype), a], axis=0)[7:7 + a.shape[0]]


def _shift_up(a):
    return jnp.concatenate([a, jnp.zeros((8, a.shape[1]), a.dtype)], axis=0)[1:1 + a.shape[0]]


def _local_step(x, mem, target, w, hooks=None):
    s = x.shape[0]
    cs, sn = _rope_tables(s)
    t_ = jnp.transpose

    w_in_p = _w_in_from_slots(w["w_in_t"])
    xn, rstd_x = _rms_fwd(_Src(x, D_MODEL), w["norm_g"], "norm_x")
    if hooks is None:
        hp, hpb = _mm(xn, w_in_p, "in_proj", mode="nt", second_dtype=BF16)
    else:
        hp, hpb, *gathered = _mm(xn, w_in_p, "in_proj", mode="nt", second_dtype=BF16, side=hooks.gather_side)
        w = {**w, **hooks.weights(gathered)}

    w_uq3 = w["w_uq"].reshape(512, MLA_HEADS, 192)
    w_uq_p = jnp.concatenate([w_uq3, w_uq3[:, :, 128:]], axis=2).reshape(512, MLA_HEADS * 256)
    w_ukv_p = t_(w["w_ukv"].reshape(512, MLA_HEADS, 2, 128), (0, 2, 1, 3)).reshape(512, 2048)
    c_q, c_kv = _Src(hp, 512, 0), _Src(hp, 512, 1)
    cqn, rstd_q = _rms_fwd(c_q, w["q_norm_g"], "norm_q")
    ckvn, rstd_kv = _rms_fwd(c_kv, w["kv_norm_g"], "norm_kv")
    q_lin = _mm(cqn, w_uq_p, "mla_q_proj")
    kvb = _mm(ckvn, w_ukv_p, "mla_kv_proj", out_dtype=BF16)
    q_mla = _rope_fwd(_Src(q_lin, MLA_HEADS * 256), cs, sn, MLA_HEADS, 256, LANE, "rope_q")
    k_pe = _rope_fwd(_Src(hp, LANE, PAD["k_rope"] // LANE), cs, sn, 1, LANE, 0, "rope_k")
    mla = _Attn("mla", s, s, MLA_HEADS, 256)
    mla_q, mla_v = _Src(q_mla, 256), _Src(kvb, LANE, MLA_HEADS)
    mla_k = [_Src(kvb, LANE), _Src(k_pe, LANE, 0, False)]
    o_mla, l_mla, lr_mla = _attn_fwd(mla, mla_q, mla_k, mla_v, None, "mla_fwd")

    sk = s // CMP_STRIDE
    pe_k, pe_v = w["cmp_pe_k"], w["cmp_pe_v"]
    w1k = _pad_cols(w["cmp_w1k"], 256)
    w2k = jnp.pad(w["cmp_w2k"], ((0, 64), (0, 64))).astype(BF16)
    w1v, w2v = w["cmp_w1v"], w["cmp_w2v"].astype(BF16)
    half_k, half_v = CMP_STRIDE * NSA_DK, CMP_STRIDE * HEAD_V
    ak = hp[:, PAD["k_c"]:PAD["k_c"] + NSA_DK].reshape(sk, half_k)
    av = hp[:, PAD["v_c"]:PAD["v_c"] + HEAD_V].reshape(sk, half_v)
    ck_args = (ak, _shift_up(ak), pe_k[:CMP_STRIDE].reshape(1, half_k), pe_k[CMP_STRIDE:].reshape(1, half_k),
               w1k[:half_k], w1k[half_k:], w2k)
    cv_args = (av, _shift_up(av), pe_v[:CMP_STRIDE].reshape(1, half_v), pe_v[CMP_STRIDE:].reshape(1, half_v),
               w1v[:half_v], w1v[half_v:], w2v)
    k_cmp, pre_k = _compress_fwd(*ck_args, "compress_k")
    v_cmp, pre_v = _compress_fwd(*cv_args, "compress_v")
    cmp_ = _Attn("cmp", s, sk, NSA_HEADS, 256)
    slc = _Attn("slc", s, s, NSA_HEADS, 256)
    win = _Attn("win", s, s, NSA_HEADS, 256)
    nsa_q = _Src(hpb, 256, PAD["q_nsa"] // 256)
    cmp_k, cmp_v = [_Src(k_cmp, 256, 0, False)], _Src(v_cmp, HEAD_V, 0, False)
    slc_k, slc_v = [_Src(hpb, 256, PAD["k_s"] // 256, False)], _Src(hpb, HEAD_V, PAD["v_s"] // HEAD_V, False)
    win_k, win_v = [_Src(hpb, 256, PAD["k_w"] // 256, False)], _Src(hpb, HEAD_V, PAD["v_w"] // HEAD_V, False)
    o_cmp, l_cmp, lr_cmp, sel, selt = _attn_fwd_small(cmp_, nsa_q, cmp_k, cmp_v, "cmp_fwd", _overlap_table(s))
    o_slc, l_slc, lr_slc = _attn_fwd(slc, nsa_q, slc_k, slc_v, sel, "slc_fwd")
    o_win, l_win, lr_win = _attn_fwd_small(win, nsa_q, win_k, win_v, "win_fwd")
    gl = _Src(hp, LANE, PAD["g_nsa"] // LANE)
    o_nsa = _nsa_combine(o_cmp, o_slc, o_win, gl)

    mn, rstd_m = _rms_fwd(_Src(mem, D_MODEL), w["mem_norm_g"], "norm_mem")
    kvm = _mm(mn, w["w_mem_kv"], "mem_kv_proj", out_dtype=BF16)
    mem_ = _Attn("mem", s, mem.shape[0], MEM_HEADS, LANE)
    mem_q, mem_k, mem_v = _Src(hpb, LANE, PAD["q_mem"] // LANE), [_Src(kvm, LANE)], _Src(kvm, LANE, MEM_HEADS)
    o_mem, l_mem, lr_mem = _attn_fwd_small(mem_, mem_q, mem_k, mem_v, "mem_fwd")

    u = _gate_fwd(o_mla, o_nsa, o_mem, hp)
    proj = _mm(u, w["w_out"], "out_proj")
    dy, g_final, loss = _final_loss(x, proj, w["final_norm_g"].reshape(1, -1), target)

    g_w_out = _mm(u, dy, "out_proj_dw", mode="tn")
    du = _mm(dy, w["w_out"], "out_proj_dx", mode="nt")
    do_cat, dz = _gate_bwd(du, o_mla, o_nsa, o_mem, hp)

    dq_mem, (dk_mem,), dv_mem = _attn_bwd(mem_, mem_q, mem_k, mem_v, None, None, _Src(o_mem, HEAD_V), l_mem,
                                          lr_mem, _Src(do_cat, HEAD_V, 12), None, "mem_bwd")
    dkvm = jnp.concatenate([dk_mem, dv_mem], axis=1)
    g_w_mem_kv = _mm(mn, dkvm, "mem_kv_dw", mode="tn")
    dmn = _mm(dkvm, w["w_mem_kv"], "mem_kv_dx", mode="nt")
    _, g_mem_norm = _rms_bwd(_Src(mem, D_MODEL), w["mem_norm_g"], rstd_m, dmn, None, "norm_mem_bwd")

    do_cmp, do_slc, do_win, dgl = _nsa_combine_bwd(do_cat, o_cmp, o_slc, o_win, gl)
    dq_n, (dk_cmp,), dv_cmp = _attn_bwd(cmp_, nsa_q, cmp_k, cmp_v, None, None, _Src(o_cmp, HEAD_V), l_cmp,
                                        lr_cmp, _Src(do_cmp, HEAD_V), None, "cmp_bwd")
    dq_n, (dk_s,), dv_s = _attn_bwd(slc, nsa_q, slc_k, slc_v, sel, selt, _Src(o_slc, HEAD_V), l_slc, lr_slc,
                                    _Src(do_slc, HEAD_V), dq_n, "slc_bwd")
    dq_n, (dk_w,), dv_w = _attn_bwd(win, nsa_q, win_k, win_v, None, None, _Src(o_win, HEAD_V), l_win, lr_win,
                                    _Src(do_win, HEAD_V), dq_n, "win_bwd")
    dak, dpk_lo, dpk_hi, dw1k_lo, dw1k_hi, g_w2k = _compress_bwd(
        *ck_args, pre_k, _shift_down(pre_k), dk_cmp, _shift_down(dk_cmp), "compress_k_bwd")
    dav, dpv_lo, dpv_hi, dw1v_lo, dw1v_hi, g_w2v = _compress_bwd(
        *cv_args, pre_v, _shift_down(pre_v), dv_cmp, _shift_down(dv_cmp), "compress_v_bwd")
    g_pe_k = jnp.concatenate([dpk_lo.reshape(CMP_STRIDE, NSA_DK), dpk_hi.reshape(CMP_STRIDE, NSA_DK)], axis=0)
    g_pe_v = jnp.concatenate([dpv_lo.reshape(CMP_STRIDE, HEAD_V), dpv_hi.reshape(CMP_STRIDE, HEAD_V)], axis=0)
    g_w1k = jnp.concatenate([dw1k_lo, dw1k_hi], axis=0)[:, :NSA_DK]
    g_w1v = jnp.concatenate([dw1v_lo, dw1v_hi], axis=0)
    dk_c = _pad_cols(dak.reshape(s, NSA_DK), 256)
    dv_c = dav.reshape(s, HEAD_V)

    dq_m, (dk_nope, dk_pe), dv_m = _attn_bwd(mla, mla_q, mla_k, mla_v, None, None, _Src(o_mla, HEAD_V), l_mla,
                                             lr_mla, _Src(do_cat, HEAD_V), None, "mla_bwd")
    dq_lin = _rope_bwd_q(dq_m, cs, sn)
    dkv_lin, d_krope = _rope_bwd_k(dk_nope, dk_pe, dv_m, cs, sn)
    g_w_uq_p = _mm(cqn, dq_lin, "mla_q_dw", mode="tn")
    dcqn = _mm(dq_lin, w_uq_p, "mla_q_dx", mode="nt")
    g_w_ukv_p = _mm(ckvn, dkv_lin, "mla_kv_dw", mode="tn")
    dckvn = _mm(dkv_lin, w_ukv_p, "mla_kv_dx", mode="nt")
    dc_q, g_q_norm = _rms_bwd(c_q, w["q_norm_g"], rstd_q, dcqn, None, "norm_q_bwd")
    dc_kv, g_kv_norm = _rms_bwd(c_kv, w["kv_norm_g"], rstd_kv, dckvn, None, "norm_kv_bwd")
    g_w_uq = g_w_uq_p.reshape(512, MLA_HEADS, 256)[:, :, :192].reshape(512, MLA_HEADS * 192)
    g_w_ukv = t_(g_w_ukv_p.reshape(512, 2, MLA_HEADS, 128), (0, 2, 1, 3)).reshape(512, 2048)

    dhp = jnp.concatenate(
        [dc_q, dc_kv, dq_n, dk_c, dk_s, dk_w, d_krope, dv_c, dv_s, dv_w, dgl,
         jnp.zeros((s, PAD["q_mem"] - (PAD["g_nsa"] + LANE)), F32), dq_mem, dz], axis=1).astype(BF16)
    grads = dict(q_norm_g=g_q_norm, w_uq=g_w_uq, kv_norm_g=g_kv_norm,
                 w_ukv=g_w_ukv, cmp_pe_k=g_pe_k, cmp_pe_v=g_pe_v, cmp_w1k=g_w1k, cmp_w2k=g_w2k[:NSA_DK, :NSA_DK],
                 cmp_w1v=g_w1v, cmp_w2v=g_w2v, mem_norm_g=g_mem_norm, w_mem_kv=g_w_mem_kv, w_out=g_w_out,
                 final_norm_g=g_final.reshape(-1))
    if hooks is None:
        g_w_in_t = _w_in_grad_slots(_mm(dhp, xn, "in_proj_dw", mode="tn", wide=2048))
        dxn = _mm(dhp, w_in_p, "in_proj_dx", wide=2048)
    else:
        g_w_in_p, *hooks.received = _mm(dhp, xn, "in_proj_dw", mode="tn", wide=2048, side=hooks.reduce_side(grads))
        g_w_in_t = _w_in_grad_slots(g_w_in_p)
        dxn, hooks.received_w_in = _mm(dhp, w_in_p, "in_proj_dx", wide=2048, side=hooks.reduce_side_w_in(g_w_in_t))
    grad_x, g_norm = _rms_bwd(_Src(x, D_MODEL), w["norm_g"], rstd_x, dxn, dy, "norm_x_bwd")
    grads.update(norm_g=g_norm, w_in_t=g_w_in_t)
    return loss[0, 0], grad_x, grads


def kernel(x, mem, norm_g, w_in, q_norm_g, w_uq, kv_norm_g, w_ukv, cmp_pe_k, cmp_pe_v, cmp_w1k, cmp_w2k, cmp_w1v, cmp_w2v, mem_norm_g, w_mem_kv, w_out, final_norm_g, loss_target, m_norm_g, m_w_in, m_q_norm_g, m_w_uq, m_kv_norm_g, m_w_ukv, m_cmp_pe_k, m_cmp_pe_v, m_cmp_w1k, m_cmp_w2k, m_cmp_w1v, m_cmp_w2v, m_mem_norm_g, m_w_mem_kv, m_w_out, m_final_norm_g, v_norm_g, v_w_in, v_q_norm_g, v_w_uq, v_kv_norm_g, v_w_ukv, v_cmp_pe_k, v_cmp_pe_v, v_cmp_w1k, v_cmp_w2k, v_cmp_w1v, v_cmp_w2v, v_mem_norm_g, v_w_mem_kv, v_w_out, v_final_norm_g):
    args = dict(locals())
    wts = {n: args[n] for n in WEIGHTS}
    loc = {n: (a if n == "final_norm_g" else a[0]) for n, a in wts.items()}

    def to_x(n, a):
        return a.T if n == "w_in" else a

    split = [1 if n == "w_in" else 0 for n in SHARDED]
    rest = [n for n in SHARDED if n != "w_in"]
    chip = 2 * lax.axis_index("x") + lax.axis_index("y")
    core = lax.axis_index("c").astype(jnp.int32).reshape(1)
    own = {n: to_x(n, loc[n]).astype(BF16) for n in SHARDED}

    def with_own_slot(gw, a):
        return lax.dynamic_update_slice(gw, a[None], (chip, 0, 0))

    def slots(n, a):
        if n == "w_in":
            return a
        if SHARD_AXIS[n] == 0:
            return a.reshape(4, a.shape[0] // 4, a.shape[1])
        width = a.shape[1] // 4
        return jnp.stack([a[:, j * width:(j + 1) * width] for j in range(4)])

    def pair_sums(names, grads, name):
        axes = [1 if n == "w_in" else 0 for n in names]
        gs = [slots(n, a) for n, a in zip(names, grads)]
        theirs = _pair_exchange(gs, axes, name)
        return [_pair_sum(a, b, core, ax, "pair_sum_" + n) for n, a, b, ax in zip(names, gs, theirs, axes)]

    class Hooks:
        gather_side = _gather_side([own[n] for n in rest], [0] * len(rest))
        received = None

        @staticmethod
        def weights(gathered):
            out = {}
            for n, gw in zip(rest, gathered):
                gw = with_own_slot(gw, own[n])
                if SHARD_AXIS[n] == 0:
                    out[n] = gw.reshape(4 * gw.shape[1], gw.shape[2])
                else:
                    out[n] = jnp.concatenate([gw[j] for j in range(4)], axis=1)
            return out

        @staticmethod
        def reduce_side(grads):
            return _chip_side(pair_sums(rest, [grads[n] for n in rest], "pair_exchange_rest"))

        @staticmethod
        def reduce_side_w_in(g_w_in_t):
            return _chip_side(pair_sums(["w_in"], [g_w_in_t], "pair_exchange_w_in"))

    hooks = Hooks()

    start = {n: loc[n].reshape(1, -1) if loc[n].ndim == 1 else loc[n] for n in REPLICATED}
    start["w_in_t"] = with_own_slot(_gather_shards([own["w_in"]], [1])[0], own["w_in"])
    loss, grad_x, g = _local_step(x[0], mem[0], loss_target[0], start, hooks)
    loss = lax.psum(loss, ("x", "y", "c"))

    from_chips = dict(zip(rest, hooks.received), w_in=hooks.received_w_in)
    mine = [_sum_slots(from_chips[n], "chip_sum_" + n) for n in SHARDED]
    g_sh = [lax.dynamic_update_slice(o, t, (core[0] * t.shape[0], 0) if ax == 0 else (0, core[0] * t.shape[1]))
            for o, t, ax in zip(_half_exchange(mine, split), mine, split)]

    n_rep = sum(int(np.prod(loc[n].shape)) for n in REPLICATED)
    rows_rep = -(-n_rep // (8 * LANE)) * 8

    def rep_pack(parts):
        flat = jnp.concatenate([p.reshape(-1) for p in parts])
        return jnp.pad(flat, (0, rows_rep * LANE - n_rep)).reshape(rows_rep, LANE)

    g_rep = _sum_slots(_gather_all(rep_pack([g[n] for n in REPLICATED])), "replica_sum")
    d_rp, m_rp, v_rp = _adamw(rep_pack([wts[n] for n in REPLICATED]), g_rep,
                              rep_pack([args["m_" + n] for n in REPLICATED]),
                              rep_pack([args["v_" + n] for n in REPLICATED]), "adamw_replicated")

    def rep_unpack(buf):
        flat, out, o = buf.reshape(-1), {}, 0
        for n in REPLICATED:
            size = int(np.prod(wts[n].shape))
            out[n] = flat[o:o + size].reshape(wts[n].shape)
            o += size
        return out

    outs = {k: rep_unpack(b) for k, b in (("g", g_rep), ("d", d_rp), ("m", m_rp), ("v", v_rp))}
    for n, gn in zip(SHARDED, g_sh):
        d, mo, vo = _adamw(to_x(n, loc[n]), gn, to_x(n, args["m_" + n][0]), to_x(n, args["v_" + n][0]),
                           "adamw_" + n)
        for k, a in (("g", gn), ("d", d), ("m", mo), ("v", vo)):
            outs[k][n] = to_x(n, a).reshape(wts[n].shape)

    return (loss, grad_x[None], *[outs["g"][n] for n in WEIGHTS], *[outs["d"][n] for n in WEIGHTS],
            *[outs["m"][n] for n in WEIGHTS], *[outs["v"][n] for n in WEIGHTS])
```

```python
from typing import NamedTuple

import numpy as np
import jax
import jax.numpy as jnp
from jax import lax
from jax.experimental import pallas as pl
from jax.experimental.pallas import tpu as pltpu

F32 = jnp.float32
BF16 = jnp.bfloat16
MESH = pl.DeviceIdType.MESH

D_MODEL = 2048
EPS = 1e-6
LANE = 128
HEAD_V = 128
MLA_HEADS = 8
NSA_HEADS = 4
MEM_HEADS = 4
NSA_DK = 192
CMP_STRIDE = 16
CMP_LEN = 32
SLC_LEN = 64
SLC_TOPN = 16
WIN = 512
NEG = -1e30
LOG2E = 1.4426950408889634
ROPE_THETA = 10000.0
BLOCK_BYTES = 2 << 20

ORIG = dict(c_q=(0, 512), c_kv=(512, 512), k_rope=(1024, 64), z_mla=(1088, 1024),
            q_nsa=(2112, 768), k_c=(2880, 192), v_c=(3072, 128), k_s=(3200, 192),
            v_s=(3392, 128), k_w=(3520, 192), v_w=(3712, 128), g_nsa=(3840, 12),
            z_nsa=(3852, 512), q_mem=(4364, 512), z_mem=(4876, 512))
PAD = dict(c_q=0, c_kv=512, q_nsa=1024, k_c=2048, k_s=2304, k_w=2560, k_rope=2816, v_c=2944,
           v_s=3072, v_w=3200, g_nsa=3328, q_mem=3584, z=4096)
D_PAD = 6144

ADAM_LR, ADAM_B1, ADAM_B2, ADAM_EPS, ADAM_WD, ADAM_STEP = 0.001, 0.9, 0.999, 1e-08, 0.01, 10

SHARDED = ("w_in", "w_uq", "w_ukv", "cmp_w1k", "cmp_w1v", "w_mem_kv", "w_out")
SHARD_AXIS = dict(w_in=1, w_uq=1, w_ukv=1, cmp_w1k=0, cmp_w1v=0, w_mem_kv=0, w_out=0)
REPLICATED = ("norm_g", "q_norm_g", "kv_norm_g", "cmp_pe_k", "cmp_pe_v", "cmp_w2k", "cmp_w2v",
              "mem_norm_g", "final_norm_g")
WEIGHTS = ("norm_g", "w_in", "q_norm_g", "w_uq", "kv_norm_g", "w_ukv", "cmp_pe_k", "cmp_pe_v",
           "cmp_w1k", "cmp_w2k", "cmp_w1v", "cmp_w2v", "mem_norm_g", "w_mem_kv", "w_out",
           "final_norm_g")


def _pcall(kernel, **kw):
    return pl.pallas_call(kernel, **kw)


def _tile(n, pref):
    if n <= pref:
        return n
    for t in range(pref, LANE - 1, -LANE):
        if n % t == 0:
            return t
    raise ValueError((n, pref))


def _row_tile(rows, cols, itemsize=4):
    want = max(16, BLOCK_BYTES // (cols * itemsize))
    if rows <= want:
        return rows
    t = 16
    best = rows
    while t <= want:
        if rows % t == 0:
            best = t
        t *= 2
    return best


def _nt(a, b):
    return lax.dot_general(a, b, (((1,), (1,)), ((), ())), preferred_element_type=F32)


def _tn(a, b):
    return lax.dot_general(a, b, (((0,), (0,)), ((), ())), preferred_element_type=F32)


def _nn(a, b):
    return jnp.dot(a, b, preferred_element_type=F32)


def _sigmoid(x):
    return 1.0 / (1.0 + jnp.exp(-x))


class _Src(NamedTuple):
    arr: jax.Array
    width: int
    col0: int = 0
    per_head: bool = True

    def col(self, h):
        return self.col0 + h if self.per_head else self.col0


class _Side(NamedTuple):
    inputs: list
    out_shape: list
    scratch: list
    phase: object


def _mm(a, b, name, mode="nn", out_dtype=F32, second_dtype=None, wide=1024, side=None):
    if mode == "tn":
        k, m = a.shape
    else:
        m, k = a.shape
    if mode == "nt":
        n, k2 = b.shape
    else:
        k2, n = b.shape
    assert k == k2, (a.shape, b.shape, mode)
    tm, tn, tk = _tile(m, 1024), _tile(n, wide), _tile(k, 2048)
    grid = (m // tm, n // tn, k // tk)
    nk = grid[2]
    assert nk == 1 or (out_dtype == F32 and second_dtype is None)
    dot = {"nn": _nn, "nt": _nt, "tn": _tn}[mode]
    n_in = len(side.inputs) if side else 0
    n_out = len(side.out_shape) if side else 0
    n_res = 1 + (second_dtype is not None)

    def kern(*refs):
        a_ref, b_ref = refs[:2]
        res = refs[2 + n_in:2 + n_in + n_res]
        step = [pl.program_id(d) for d in range(3)]
        if side:
            side_refs = (refs[2:2 + n_in], refs[2 + n_in + n_res:2 + n_in + n_res + n_out],
                         refs[2 + n_in + n_res + n_out:])

            @pl.when((step[0] == 0) & (step[1] == 0) & (step[2] == 0))
            def _():
                side.phase("start", *side_refs)

        r = dot(a_ref[...].astype(BF16), b_ref[...].astype(BF16))
        if nk == 1:
            res[0][...] = r.astype(out_dtype)
            if n_res == 2:
                res[1][...] = r.astype(second_dtype)
        else:
            @pl.when(step[2] == 0)
            def _():
                res[0][...] = r

            @pl.when(step[2] > 0)
            def _():
                res[0][...] += r

        if side:
            @pl.when((step[0] == grid[0] - 1) & (step[1] == grid[1] - 1) & (step[2] == nk - 1))
            def _():
                side.phase("finish", *side_refs)

    a_spec = (pl.BlockSpec((tk, tm), lambda i, j, kk: (kk, i)) if mode == "tn"
              else pl.BlockSpec((tm, tk), lambda i, j, kk: (i, kk)))
    b_spec = (pl.BlockSpec((tn, tk), lambda i, j, kk: (j, kk)) if mode == "nt"
              else pl.BlockSpec((tk, tn), lambda i, j, kk: (kk, j)))
    o_spec = pl.BlockSpec((tm, tn), lambda i, j, kk: (i, j))
    out_specs = [o_spec] * n_res + [ANY] * n_out
    out_shape = [jax.ShapeDtypeStruct((m, n), out_dtype)]
    if second_dtype is not None:
        out_shape.append(jax.ShapeDtypeStruct((m, n), second_dtype))
    out_shape += list(side.out_shape) if side else []
    semantics = ("arbitrary",) * 3 if side else ("parallel", "parallel", "arbitrary")
    out = _pcall(
        kern, name=name, grid=grid, in_specs=[a_spec, b_spec] + [ANY] * n_in, out_specs=out_specs,
        out_shape=out_shape, scratch_shapes=list(side.scratch) if side else [],
        compiler_params=pltpu.CompilerParams(dimension_semantics=semantics),
    )(a, b, *(side.inputs if side else []))
    return out[0] if len(out) == 1 else out


def _rms_fwd(x, g, name):
    r, d = x.arr.shape[0], x.width
    tr = _tile(r, 512)

    def kern(x_ref, g_ref, y_ref, r_ref):
        xv = x_ref[...]
        rstd = lax.rsqrt(jnp.mean(xv * xv, axis=-1, keepdims=True) + EPS)
        y_ref[...] = (xv * rstd * g_ref[...]).astype(BF16)
        r_ref[...] = rstd

    return _pcall(
        kern, name=name, grid=(r // tr,),
        in_specs=[pl.BlockSpec((tr, d), lambda i: (i, x.col0)), pl.BlockSpec((1, d), lambda i: (0, 0))],
        out_specs=[pl.BlockSpec((tr, d), lambda i: (i, 0)), pl.BlockSpec((tr, 1), lambda i: (i, 0))],
        out_shape=[jax.ShapeDtypeStruct((r, d), BF16), jax.ShapeDtypeStruct((r, 1), F32)],
    )(x.arr, g)


def _rms_bwd(x, g, rstd, dy, add, name):
    r, d = x.arr.shape[0], x.width
    tr = _tile(r, 256)
    has_add = add is not None

    def kern(*refs):
        if has_add:
            x_ref, g_ref, r_ref, dy_ref, add_ref, dx_ref, dg_ref = refs
        else:
            x_ref, g_ref, r_ref, dy_ref, dx_ref, dg_ref = refs
        rs = r_ref[...]
        xhat = x_ref[...] * rs
        dyv = dy_ref[...]
        dyg = dyv * g_ref[...]
        c = jnp.mean(dyg * xhat, axis=-1, keepdims=True)
        dx = rs * (dyg - xhat * c)
        if has_add:
            dx = dx + add_ref[...]
        dx_ref[...] = dx
        part = jnp.sum(dyv * xhat, axis=0, keepdims=True)

        @pl.when(pl.program_id(0) == 0)
        def _():
            dg_ref[...] = part

        @pl.when(pl.program_id(0) > 0)
        def _():
            dg_ref[...] += part

    row = pl.BlockSpec((tr, d), lambda i: (i, 0))
    vec = pl.BlockSpec((1, d), lambda i: (0, 0))
    ins = [pl.BlockSpec((tr, d), lambda i: (i, x.col0)), vec, pl.BlockSpec((tr, 1), lambda i: (i, 0)), row]
    ins += [row] if has_add else []
    args = (x.arr, g, rstd, dy) + ((add,) if has_add else ())
    return _pcall(
        kern, name=name, grid=(r // tr,), in_specs=ins, out_specs=[row, vec],
        out_shape=[jax.ShapeDtypeStruct((r, d), F32), jax.ShapeDtypeStruct((1, d), F32)],
        compiler_params=pltpu.CompilerParams(dimension_semantics=("arbitrary",)),
    )(*args)


def _final_loss(x, proj, g, target):
    r, d = x.shape
    tr = _tile(r, 256)

    def kern(x_ref, p_ref, g_ref, t_ref, dy_ref, dg_ref, loss_ref):
        y = x_ref[...] + p_ref[...]
        rs = lax.rsqrt(jnp.mean(y * y, axis=-1, keepdims=True) + EPS)
        yhat = y * rs
        gv = g_ref[...]
        e = yhat * gv - t_ref[...]
        lpart = 0.5 * jnp.sum(jnp.mean(e * e, axis=-1, keepdims=True), axis=0, keepdims=True)
        dout = e * (1.0 / d)
        dyg = dout * gv
        c = jnp.mean(dyg * yhat, axis=-1, keepdims=True)
        dy_ref[...] = rs * (dyg - yhat * c)
        gpart = jnp.sum(dout * yhat, axis=0, keepdims=True)
        lrow = jnp.broadcast_to(lpart, (1, LANE))

        @pl.when(pl.program_id(0) == 0)
        def _():
            dg_ref[...] = gpart
            loss_ref[...] = lrow

        @pl.when(pl.program_id(0) > 0)
        def _():
            dg_ref[...] += gpart
            loss_ref[...] += lrow

    row = pl.BlockSpec((tr, d), lambda i: (i, 0))
    vec = pl.BlockSpec((1, d), lambda i: (0, 0))
    return _pcall(
        kern, name="final_loss", grid=(r // tr,), in_specs=[row, row, vec, row],
        out_specs=[row, vec, pl.BlockSpec((1, LANE), lambda i: (0, 0))],
        out_shape=[jax.ShapeDtypeStruct((r, d), F32), jax.ShapeDtypeStruct((1, d), F32),
                   jax.ShapeDtypeStruct((1, LANE), F32)],
        compiler_params=pltpu.CompilerParams(dimension_semantics=("arbitrary",)),
    )(x, proj, g, target)


def _rope_fwd(x, cs, sn, nh, width, off, name):
    s = x.arr.shape[0]
    tr = _tile(s, 512)

    def kern(x_ref, c_ref, s_ref, o_ref):
        cv, sv = c_ref[...], s_ref[...]
        for h in range(nh):
            b = h * width
            if off:
                o_ref[:, b:b + off] = x_ref[:, b:b + off].astype(BF16)
            xr = x_ref[:, b + off:b + off + LANE]
            o_ref[:, b + off:b + off + LANE] = (xr * cv + pltpu.roll(xr, 32, 1) * sv).astype(BF16)

    tab = pl.BlockSpec((tr, LANE), lambda i: (i, 0))
    return _pcall(
        kern, name=name, grid=(s // tr,),
        in_specs=[pl.BlockSpec((tr, nh * width), lambda i: (i, x.col0)), tab, tab],
        out_specs=pl.BlockSpec((tr, nh * width), lambda i: (i, 0)),
        out_shape=jax.ShapeDtypeStruct((s, nh * width), BF16),
    )(x.arr, cs, sn)


def _rope_grad(d, cv, sv):
    g2 = d * sv
    g2 = g2 + pltpu.roll(g2, 64, 1)
    lane = lax.broadcasted_iota(jnp.int32, d.shape, 1)
    return jnp.where(lane < 64, d * cv + pltpu.roll(g2, 32, 1), 0.0)


def _rope_bwd_q(dq, cs, sn):
    s, w = dq.shape
    tr = _tile(s, 512)
    nh = w // 256

    def kern(d_ref, c_ref, s_ref, o_ref):
        cv, sv = c_ref[...], s_ref[...]
        for h in range(nh):
            b = h * 256
            o_ref[:, b:b + LANE] = d_ref[:, b:b + LANE]
            o_ref[:, b + LANE:b + 256] = _rope_grad(d_ref[:, b + LANE:b + 256], cv, sv)

    row = pl.BlockSpec((tr, w), lambda i: (i, 0))
    tab = pl.BlockSpec((tr, LANE), lambda i: (i, 0))
    return _pcall(kern, name="rope_bwd_q", grid=(s // tr,), in_specs=[row, tab, tab], out_specs=row,
                  out_shape=jax.ShapeDtypeStruct((s, w), F32))(dq, cs, sn)


def _rope_bwd_k(dk_nope, dk_pe, dv, cs, sn):
    s, w = dk_nope.shape
    tr = _tile(s, 512)

    def kern(dk_ref, dp_ref, dv_ref, c_ref, s_ref, okv_ref, okr_ref):
        okv_ref[:, :w] = dk_ref[...]
        okv_ref[:, w:] = dv_ref[...]
        okr_ref[...] = _rope_grad(dp_ref[...], c_ref[...], s_ref[...])

    tab = pl.BlockSpec((tr, LANE), lambda i: (i, 0))
    wide = pl.BlockSpec((tr, w), lambda i: (i, 0))
    return _pcall(
        kern, name="rope_bwd_k", grid=(s // tr,), in_specs=[wide, tab, wide, tab, tab],
        out_specs=[pl.BlockSpec((tr, 2 * w), lambda i: (i, 0)), tab],
        out_shape=[jax.ShapeDtypeStruct((s, 2 * w), F32), jax.ShapeDtypeStruct((s, LANE), F32)],
    )(dk_nope, dk_pe, dv, cs, sn)


class _Attn:
    def __init__(self, mode, s, sk, heads, dk):
        self.mode, self.s, self.sk, self.h, self.dk = mode, s, sk, heads, dk
        self.scale = {"mla": 192 ** -0.5, "mem": 128 ** -0.5}.get(mode, NSA_DK ** -0.5)
        self.tb = min(256, s)
        self.nb = s // self.tb
        self.nsub = 2 if self.nb % 2 == 0 else 1
        self.tq = self.tb * self.nsub
        self.nq = s // self.tq
        self.causal = mode in ("mla", "slc")
        if self.causal:
            self.tk = self.tq
        elif mode == "win":
            self.tk = WIN + self.tb
        else:
            self.tk = sk
        self.tkb = min(512, sk)
        self.ksub = 2 if self.tkb == 512 and mode == "mla" else 1
        self.kb = self.tkb // self.ksub
        self.ncmp = s // CMP_STRIDE - 1

    def mask_bias(self, t, n, h, selx, diag):
        m = self.mode
        if m == "mla":
            return (n <= t) if diag else None, None
        if m == "mem":
            return None, None
        slope = jnp.where(h == 0, 0.25, jnp.where(h == 1, 0.0625, jnp.where(h == 2, 0.015625, 0.00390625)))
        slope = slope.astype(F32) * LOG2E
        if m == "cmp":
            mask = (n * CMP_STRIDE + (CMP_LEN - 1) <= t) & (n < self.ncmp)
            pos = n.astype(F32) * float(CMP_STRIDE) + (CMP_LEN - 1) / 2.0
            return mask, slope * pos
        rel = t - n
        if m == "slc":
            return (rel >= 0) if diag else None, slope * n.astype(F32)
        return (rel >= 0) & (rel < WIN), slope * n.astype(F32)


def _scores(cfg, s_raw, t, n, h, selx, diag, lse=None):
    s = s_raw * (cfg.scale * LOG2E)
    mask, key_term = cfg.mask_bias(t, n, h, selx, diag)
    if key_term is not None:
        s = s + key_term
    if selx is not None:
        s = s + selx
    if lse is None:
        if mask is not None:
            s = jnp.where(mask, s, NEG)
        return s, mask
    p = jnp.exp2(jnp.minimum(s - lse, 0.0))
    if mask is not None:
        p = jnp.where(mask, p, 0.0)
    return p, mask


def _block_of_key(k0, tk, keys_on_rows, value=NEG):
    shape = (tk, LANE) if keys_on_rows else (LANE, tk)
    n = lax.broadcasted_iota(jnp.int32, shape, 0 if keys_on_rows else 1) + k0
    j = lax.broadcasted_iota(jnp.int32, shape, 1 if keys_on_rows else 0)
    return jnp.where((n >> 6) == j, value, 0.0).astype(BF16)


def _to_row(col):
    t = col.shape[0]
    return jnp.transpose(jnp.broadcast_to(col, (t, LANE)))[0:1, :]


def _load_keys(k_refs, rows):
    parts = [r[rows, :].astype(BF16) for r in k_refs]
    return parts[0] if len(parts) == 1 else jnp.concatenate(parts, axis=1)


def _attn_fwd(cfg, q, ks, v, sel, name):
    s, tq, tk, tb, nsub = cfg.s, cfg.tq, cfg.tk, cfg.tb, cfg.nsub
    has_sel = sel is not None
    nkp = len(ks)

    def kern(*refs):
        q_ref, k_refs, v_ref = refs[0], refs[1:1 + nkp], refs[1 + nkp]
        sel_ref = refs[2 + nkp] if has_sel else None
        o_ref, lc_ref, lr_ref = refs[2 + nkp + has_sel:5 + nkp + has_sel]
        h, i = pl.program_id(0), pl.program_id(1)
        part = [slice(r * tb, (r + 1) * tb) for r in range(nsub)]
        qs = [q_ref[p, :].astype(BF16) for p in part]
        ts = [i * tq + r * tb + lax.broadcasted_iota(jnp.int32, (tb, 1), 0) for r in range(nsub)]
        sels = [sel_ref[p, :].astype(BF16) for p in part] if has_sel else None

        def load(k0):
            rows = pl.ds(k0, tk)
            return _load_keys(k_refs, rows), v_ref[rows, :].astype(BF16)

        def soft(r, k0, s_raw, vv, emat, carry, diag):
            m, l, acc = carry
            n = k0 + lax.broadcasted_iota(jnp.int32, (1, tk), 1)
            selx = _nn(sels[r], emat) if has_sel else None
            sc, mask = _scores(cfg, s_raw, ts[r], n, h, selx, diag)
            m_new = jnp.maximum(m, jnp.max(sc, axis=1, keepdims=True))
            alpha = jnp.exp2(m - m_new)
            p = jnp.exp2(sc - m_new)
            if mask is not None:
                p = jnp.where(mask, p, 0.0)
            l = alpha * l + jnp.sum(p, axis=1, keepdims=True)
            acc = alpha * acc + _nn(p.astype(BF16), vv)
            return m_new, l, acc

        def step(r, k0, kk, vv, emat, carry, diag):
            return soft(r, k0, _nt(qs[r], kk), vv, emat, carry, diag)

        def chunk(k0, carry, diag):
            kk, vv = load(k0)
            emat = _block_of_key(k0, tk, False) if has_sel else None
            return tuple(step(r, k0, kk, vv, emat, carry[r], diag) for r in range(nsub))

        init = (jnp.full((tb, 1), NEG, F32), jnp.zeros((tb, 1), F32), jnp.zeros((tb, HEAD_V), F32))
        carry = (init,) * nsub
        if cfg.causal:
            buf_a, buf_b = refs[-2:]
            full = (i * tq) // tk

            def scores_into(buf, c):
                kk = _load_keys(k_refs, pl.ds(pl.multiple_of(c * tk, tk), tk))
                for r in range(nsub):
                    buf[r] = _nt(qs[r], kk)

            def consume(buf, c, cr, diag):
                k0 = pl.multiple_of(c * tk, tk)
                vv = v_ref[pl.ds(k0, tk), :].astype(BF16)
                emat = _block_of_key(k0, tk, False) if has_sel else None
                return tuple(soft(r, k0, buf[r], vv, emat, cr[r], diag) for r in range(nsub))

            def pair(p, cr):
                scores_into(buf_b, 2 * p + 1)
                cr = consume(buf_a, 2 * p, cr, False)
                scores_into(buf_a, 2 * p + 2)
                return consume(buf_b, 2 * p + 1, cr, False)

            def odd_tail(cr):
                scores_into(buf_b, full)
                cr = consume(buf_a, full - 1, cr, False)
                return consume(buf_b, full, cr, True)

            scores_into(buf_a, 0)
            carry = lax.fori_loop(0, full // 2, pair, carry)
            carry = lax.cond(full % 2 == 1, odd_tail, lambda cr: consume(buf_a, full, cr, True), carry)
        elif cfg.mode == "win":
            starts = [pl.multiple_of(jnp.maximum(i * tq + r * tb - WIN, 0), tb) for r in range(nsub)]
            carry = tuple(step(r, k0, *load(k0), None, carry[r], True) for r, k0 in enumerate(starts))
        else:
            carry = chunk(0, carry, True)
        for r, (m, l, acc) in enumerate(carry):
            o_ref[part[r], :] = acc / (l + 1e-20)
            lse = m + jnp.log(l + 1e-20) * LOG2E
            lc_ref[0, part[r], :] = lse
            lr_ref[0, r] = _to_row(lse)

    ins = [pl.BlockSpec((tq, q.width), lambda h, i: (i, q.col(h)))]
    ins += [pl.BlockSpec((cfg.sk, p.width), lambda h, i, p=p: (0, p.col(h))) for p in ks]
    ins += [pl.BlockSpec((cfg.sk, HEAD_V), lambda h, i: (0, v.col(h)))]
    args = [q.arr] + [p.arr for p in ks] + [v.arr]
    if has_sel:
        ins.append(pl.BlockSpec((tq, LANE), lambda h, i: (i, 0)))
        args.append(sel)
    return _pcall(
        kern, name=name, grid=(cfg.h, cfg.nq), in_specs=ins,
        out_specs=[pl.BlockSpec((tq, HEAD_V), lambda h, i: (i, h)),
                   pl.BlockSpec((1, tq, 1), lambda h, i: (h, i, 0)),
                   pl.BlockSpec((1, nsub, 1, tb), lambda h, i: (h, i, 0, 0))],
        out_shape=[jax.ShapeDtypeStruct((s, cfg.h * HEAD_V), F32),
                   jax.ShapeDtypeStruct((cfg.h, s, 1), F32),
                   jax.ShapeDtypeStruct((cfg.h, cfg.nb, 1, tb), F32)],
        scratch_shapes=[pltpu.VMEM((nsub, tb, tk), F32)] * 2 if cfg.causal else [],
        compiler_params=pltpu.CompilerParams(dimension_semantics=("parallel", "parallel")),
    )(*args)


def _attn_dq(cfg, q, ks, v, sel, o, lse, do, dq_in, name):
    s, tq, tk, dk, tb, nsub = cfg.s, cfg.tq, cfg.tk, cfg.dk, cfg.tb, cfg.nsub
    has_sel = sel is not None
    has_in = dq_in is not None
    nkp = len(ks)

    def kern(*refs):
        refs = list(refs)
        q_ref, k_refs, v_ref = refs[0], refs[1:1 + nkp], refs[1 + nkp]
        p0 = 2 + nkp
        sel_ref = refs[p0] if has_sel else None
        p0 += has_sel
        o_ref, l_ref, do_ref = refs[p0:p0 + 3]
        p0 += 3
        in_ref = refs[p0] if has_in else None
        p0 += has_in
        dq_ref, dr_ref = refs[p0:p0 + 2]
        h, i = pl.program_id(0), pl.program_id(1)
        part = [slice(r * tb, (r + 1) * tb) for r in range(nsub)]
        qs = [q_ref[p, :].astype(BF16) for p in part]
        ts = [i * tq + r * tb + lax.broadcasted_iota(jnp.int32, (tb, 1), 0) for r in range(nsub)]
        sels = [sel_ref[p, :].astype(BF16) for p in part] if has_sel else None
        dvecs, dobs, lses = [], [], []
        for r, p in enumerate(part):
            dov = do_ref[p, :]
            dvec = jnp.sum(dov * o_ref[p, :], axis=1, keepdims=True)
            dr_ref[0, r] = _to_row(dvec)
            dvecs.append(dvec)
            dobs.append(dov.astype(BF16))
            lses.append(l_ref[0, p, :])

        def load(k0):
            rows = pl.ds(k0, tk)
            return _load_keys(k_refs, rows), v_ref[rows, :].astype(BF16)

        def grad(r, k0, s_raw, dp, kk, emat, acc, diag):
            n = k0 + lax.broadcasted_iota(jnp.int32, (1, tk), 1)
            selx = _nn(sels[r], emat) if has_sel else None
            p, _ = _scores(cfg, s_raw, ts[r], n, h, selx, diag, lses[r])
            ds = p * (dp - dvecs[r])
            return acc + _nn(ds.astype(BF16), kk)

        def step(r, k0, kk, vv, emat, acc, diag):
            return grad(r, k0, _nt(qs[r], kk), _nt(dobs[r], vv), kk, emat, acc, diag)

        def chunk(k0, accs, diag):
            kk, vv = load(k0)
            emat = _block_of_key(k0, tk, False) if has_sel else None
            return tuple(step(r, k0, kk, vv, emat, accs[r], diag) for r in range(nsub))

        accs = (jnp.zeros((tb, dk), F32),) * nsub
        if cfg.causal:
            sa, pa, sb, pb = refs[-4:]
            full = (i * tq) // tk

            def products_into(sbuf, pbuf, c):
                kk, vv = load(pl.multiple_of(c * tk, tk))
                for r in range(nsub):
                    sbuf[r] = _nt(qs[r], kk)
                    pbuf[r] = _nt(dobs[r], vv)

            def consume(sbuf, pbuf, c, ac, diag):
                k0 = pl.multiple_of(c * tk, tk)
                kk = _load_keys(k_refs, pl.ds(k0, tk))
                emat = _block_of_key(k0, tk, False) if has_sel else None
                return tuple(grad(r, k0, sbuf[r], pbuf[r], kk, emat, ac[r], diag) for r in range(nsub))

            def pair(p, ac):
                products_into(sb, pb, 2 * p + 1)
                ac = consume(sa, pa, 2 * p, ac, False)
                products_into(sa, pa, 2 * p + 2)
                return consume(sb, pb, 2 * p + 1, ac, False)

            def odd_tail(ac):
                products_into(sb, pb, full)
                ac = consume(sa, pa, full - 1, ac, False)
                return consume(sb, pb, full, ac, True)

            products_into(sa, pa, 0)
            accs = lax.fori_loop(0, full // 2, pair, accs)
            accs = lax.cond(full % 2 == 1, odd_tail, lambda ac: consume(sa, pa, full, ac, True), accs)
        elif cfg.mode == "win":
            starts = [pl.multiple_of(jnp.maximum(i * tq + r * tb - WIN, 0), tb) for r in range(nsub)]
            accs = tuple(step(r, k0, *load(k0), None, accs[r], True) for r, k0 in enumerate(starts))
        else:
            accs = chunk(0, accs, True)
        for r, p in enumerate(part):
            dq_ref[p, :] = accs[r] * cfg.scale + in_ref[p, :] if has_in else accs[r] * cfg.scale

    qs = pl.BlockSpec((tq, dk), lambda h, i: (i, h))
    ins = [pl.BlockSpec((tq, q.width), lambda h, i: (i, q.col(h)))]
    ins += [pl.BlockSpec((cfg.sk, p.width), lambda h, i, p=p: (0, p.col(h))) for p in ks]
    ins += [pl.BlockSpec((cfg.sk, HEAD_V), lambda h, i: (0, v.col(h)))]
    args = [q.arr] + [p.arr for p in ks] + [v.arr]
    if has_sel:
        ins.append(pl.BlockSpec((tq, LANE), lambda h, i: (i, 0)))
        args.append(sel)
    ins += [pl.BlockSpec((tq, HEAD_V), lambda h, i: (i, o.col(h))),
            pl.BlockSpec((1, tq, 1), lambda h, i: (h, i, 0)),
            pl.BlockSpec((tq, HEAD_V), lambda h, i: (i, do.col(h)))]
    args += [o.arr, lse, do.arr]
    if has_in:
        ins.append(qs)
        args.append(dq_in)
    return _pcall(
        kern, name=name, grid=(cfg.h, cfg.nq), in_specs=ins,
        out_specs=[qs, pl.BlockSpec((1, nsub, 1, tb), lambda h, i: (h, i, 0, 0))],
        out_shape=[jax.ShapeDtypeStruct((s, cfg.h * dk), F32),
                   jax.ShapeDtypeStruct((cfg.h, cfg.nb, 1, tb), F32)],
        scratch_shapes=[pltpu.VMEM((nsub, tb, tk), F32)] * 4 if cfg.causal else [],
        compiler_params=pltpu.CompilerParams(dimension_semantics=("parallel", "parallel")),
    )(*args)


def _attn_dkv(cfg, q, ks, v, selt, lse_r, d_r, do, name):
    s, tq, tkb, dk, kb, ksub = cfg.s, cfg.tb, cfg.tkb, cfg.dk, cfg.kb, cfg.ksub
    nq = cfg.nb
    has_sel = selt is not None
    nkp = len(ks)
    outs = list(ks) + [v]

    def kern(*refs):
        k_refs, v_ref = refs[:nkp], refs[nkp]
        q_ref, do_ref, lr_ref, dr_ref = refs[nkp + 1:nkp + 5]
        st_ref = refs[nkp + 5] if has_sel else None
        out_refs = refs[nkp + 5 + has_sel:2 * nkp + 6 + has_sel]
        sa, pa, sb, pb = refs[-4:]
        j, h = pl.program_id(0), pl.program_id(1)
        k0 = j * tkb
        part = [slice(u * kb, (u + 1) * kb) for u in range(ksub)]
        kks = [_load_keys(k_refs, p) for p in part]
        vvs = [v_ref[p, :].astype(BF16) for p in part]
        ns = [k0 + u * kb + lax.broadcasted_iota(jnp.int32, (kb, 1), 0) for u in range(ksub)]
        emats = [_block_of_key(k0 + u * kb, kb, True) for u in range(ksub)] if has_sel else None

        def load_q(i):
            rows = pl.ds(pl.multiple_of(i * tq, tq), tq)
            return q_ref[rows, :].astype(BF16), do_ref[rows, :].astype(BF16)

        def products_into(sbuf, pbuf, i):
            qi, doi = load_q(i)
            for u in range(ksub):
                sbuf[u] = _nt(kks[u], qi)
                pbuf[u] = _nt(vvs[u], doi)

        def consume(sbuf, pbuf, i, carry):
            qi, doi = load_q(i)
            t = i * tq + lax.broadcasted_iota(jnp.int32, (1, tq), 1)
            selt_i = st_ref[i].astype(BF16) if has_sel else None
            new = []
            for u in range(ksub):
                dk_acc, dv_acc = carry[u]
                selx = _nn(emats[u], selt_i) if has_sel else None
                pt, _ = _scores(cfg, sbuf[u], t, ns[u], h, selx, True, lr_ref[0, i])
                dv_acc = dv_acc + _nn(pt.astype(BF16), doi)
                dst = pt * (pbuf[u] - dr_ref[0, i])
                new.append((dk_acc + _nn(dst.astype(BF16), qi), dv_acc))
            return tuple(new)

        if cfg.causal:
            first, count = k0 // tq, nq - k0 // tq
        elif cfg.mode == "win":
            first = k0 // tq
            count = jnp.minimum((k0 + tkb + WIN - 2) // tq + 1, nq) - first
        else:
            first, count = 0, nq

        def pair(p, cr):
            i0 = first + 2 * p
            products_into(sb, pb, i0 + 1)
            cr = consume(sa, pa, i0, cr)
            products_into(sa, pa, i0 + 2)
            return consume(sb, pb, i0 + 1, cr)

        carry = ((jnp.zeros((kb, dk), F32), jnp.zeros((kb, HEAD_V), F32)),) * ksub
        products_into(sa, pa, first)
        carry = lax.fori_loop(0, count // 2 - 1, pair, carry)
        last = first + count - 2
        products_into(sb, pb, last + 1)
        carry = consume(sa, pa, last, carry)
        carry = consume(sb, pb, last + 1, carry)
        for u, (dk_acc, dv_acc) in enumerate(carry):
            vals, off = [], 0
            for p in ks:
                vals.append(dk_acc[:, off:off + p.width] * cfg.scale)
                off += p.width
            vals.append(dv_acc)
            for src, ref, val in zip(outs, out_refs, vals):
                if src.per_head:
                    ref[part[u], :] = val
                else:
                    @pl.when(h == 0)
                    def _(ref=ref, val=val, u=u):
                        ref[part[u], :] = val

                    @pl.when(h > 0)
                    def _(ref=ref, val=val, u=u):
                        ref[part[u], :] += val

    rowv = pl.BlockSpec((1, nq, 1, tq), lambda j, h: (h, 0, 0, 0))
    ins = [pl.BlockSpec((tkb, p.width), lambda j, h, p=p: (j, p.col(h))) for p in ks]
    ins += [pl.BlockSpec((tkb, HEAD_V), lambda j, h: (j, v.col(h))),
            pl.BlockSpec((s, q.width), lambda j, h: (0, q.col(h))),
            pl.BlockSpec((s, HEAD_V), lambda j, h: (0, do.col(h))), rowv, rowv]
    args = [p.arr for p in ks] + [v.arr, q.arr, do.arr, lse_r, d_r]
    if has_sel:
        ins.append(pl.BlockSpec((nq, LANE, tq), lambda j, h: (0, 0, 0)))
        args.append(selt)
    out_specs = [pl.BlockSpec((tkb, p.width), lambda j, h, p=p: (j, h if p.per_head else 0)) for p in outs]
    out_shape = [jax.ShapeDtypeStruct((cfg.sk, (cfg.h if p.per_head else 1) * p.width), F32) for p in outs]
    assert nq % 2 == 0 and (cfg.mode in ("cmp", "mem") or tkb % (2 * tq) == 0), (nq, tkb, tq)
    return _pcall(
        kern, name=name, grid=(cfg.sk // tkb, cfg.h), in_specs=ins, out_specs=out_specs, out_shape=out_shape,
        scratch_shapes=[pltpu.VMEM((ksub, kb, tq), F32)] * 4,
        compiler_params=pltpu.CompilerParams(dimension_semantics=("parallel", "arbitrary")),
    )(*args)


def _attn_dkv_flat(cfg, q, ks, v, selt, lse_r, d_r, do, name):
    s, tq, tkb, dk, kb, ksub = cfg.s, cfg.tb, cfg.tkb, cfg.dk, cfg.kb, cfg.ksub
    nq = cfg.nb
    has_sel = selt is not None
    nkp = len(ks)
    outs = list(ks) + [v]
    assert nq % 2 == 0 and tkb % (2 * tq) == 0, (nq, tkb, tq)
    steps = []
    for j in range(cfg.sk // tkb):
        first = j * tkb // tq
        stop = nq if cfg.causal else min((j * tkb + tkb + WIN - 2) // tq + 1, nq)
        steps += [(j, i0) for i0 in range(first, stop, 2)]
    n_pairs = len(steps)
    steps.append(steps[-1])
    tab_j = jnp.asarray(np.array([p[0] for p in steps], np.int32))
    tab_i = jnp.asarray(np.array([p[1] for p in steps], np.int32))

    def kern(tj_ref, ti_ref, *refs):
        k_refs, v_ref = refs[:nkp], refs[nkp]
        q_ref, do_ref, lr_ref, dr_ref = refs[nkp + 1:nkp + 5]
        st_ref = refs[nkp + 5] if has_sel else None
        out_refs = refs[nkp + 5 + has_sel:2 * nkp + 6 + has_sel]
        sa, pa, sb, pb = refs[-4:]
        h = pl.program_id(0)
        for src, ref in zip(outs, out_refs):
            if src.per_head:
                ref[...] = jnp.zeros_like(ref)
            else:
                @pl.when(h == 0)
                def _(ref=ref):
                    ref[...] = jnp.zeros_like(ref)

        def key_rows(j, u):
            return pl.ds(pl.multiple_of(j * tkb + u * kb, kb), kb)

        def load_q(i):
            rows = pl.ds(pl.multiple_of(i * tq, tq), tq)
            return q_ref[rows, :].astype(BF16), do_ref[rows, :].astype(BF16)

        def products_into(sbuf, pbuf, j, i):
            qi, doi = load_q(i)
            for u in range(ksub):
                rows = key_rows(j, u)
                sbuf[u] = _nt(_load_keys(k_refs, rows), qi)
                pbuf[u] = _nt(v_ref[rows, :].astype(BF16), doi)

        def consume(sbuf, pbuf, j, i):
            qi, doi = load_q(i)
            t = i * tq + lax.broadcasted_iota(jnp.int32, (1, tq), 1)
            selt_i = st_ref[i].astype(BF16) if has_sel else None
            res = []
            for u in range(ksub):
                k0 = j * tkb + u * kb
                n = k0 + lax.broadcasted_iota(jnp.int32, (kb, 1), 0)
                selx = _nn(_block_of_key(k0, kb, True), selt_i) if has_sel else None
                pt, _ = _scores(cfg, sbuf[u], t, n, h, selx, True, lr_ref[0, i])
                dst = pt * (pbuf[u] - dr_ref[0, i])
                res.append((_nn(dst.astype(BF16), qi), _nn(pt.astype(BF16), doi)))
            return res

        def pair(p, carry):
            j, i0 = tj_ref[p], ti_ref[p]
            products_into(sb, pb, j, i0 + 1)
            ca = consume(sa, pa, j, i0)
            products_into(sa, pa, tj_ref[p + 1], ti_ref[p + 1])
            cb = consume(sb, pb, j, i0 + 1)
            for u in range(ksub):
                rows = key_rows(j, u)
                dk_c = (ca[u][0] + cb[u][0]) * cfg.scale
                off = 0
                for src, ref in zip(ks, out_refs):
                    ref[rows, :] += dk_c[:, off:off + src.width]
                    off += src.width
                out_refs[nkp][rows, :] += ca[u][1] + cb[u][1]
            return carry

        products_into(sa, pa, tj_ref[0], ti_ref[0])
        lax.fori_loop(0, n_pairs, pair, 0)

    rowv = pl.BlockSpec((1, nq, 1, tq), lambda h, tj, ti: (h, 0, 0, 0))
    ins = [pl.BlockSpec((cfg.sk, p.width), lambda h, tj, ti, p=p: (0, p.col(h))) for p in ks]
    ins += [pl.BlockSpec((cfg.sk, HEAD_V), lambda h, tj, ti: (0, v.col(h))),
            pl.BlockSpec((s, q.width), lambda h, tj, ti: (0, q.col(h))),
            pl.BlockSpec((s, HEAD_V), lambda h, tj, ti: (0, do.col(h))), rowv, rowv]
    args = [p.arr for p in ks] + [v.arr, q.arr, do.arr, lse_r, d_r]
    if has_sel:
        ins.append(pl.BlockSpec((nq, LANE, tq), lambda h, tj, ti: (0, 0, 0)))
        args.append(selt)
    out_specs = [pl.BlockSpec((cfg.sk, p.width), lambda h, tj, ti, p=p: (0, h if p.per_head else 0))
                 for p in outs]
    out_shape = [jax.ShapeDtypeStruct((cfg.sk, (cfg.h if p.per_head else 1) * p.width), F32) for p in outs]
    grid_spec = pltpu.PrefetchScalarGridSpec(
        num_scalar_prefetch=2, grid=(cfg.h,), in_specs=ins, out_specs=out_specs,
        scratch_shapes=[pltpu.VMEM((ksub, kb, tq), F32)] * 4)
    return _pcall(kern, name=name, grid_spec=grid_spec, out_shape=out_shape,
                  compiler_params=pltpu.CompilerParams(dimension_semantics=("arbitrary",)))(tab_j, tab_i, *args)


def _all_heads(cfg, src, rows, key=False):
    if src.per_head:
        assert src.col0 % cfg.h == 0
        width, col = cfg.h * src.width, src.col0 // cfg.h
    else:
        width, col = src.width, src.col0
    return pl.BlockSpec((rows, width), (lambda i: (0, col)) if key else (lambda i: (i, col)))


def _head_cols(src, hh):
    return slice(hh * src.width, (hh + 1) * src.width) if src.per_head else slice(None)


def _key_window(cfg, i, r):
    if cfg.mode == "win":
        return pl.ds(pl.multiple_of(jnp.maximum(i * cfg.tq + r * cfg.tb - WIN, 0), cfg.tb), cfg.tk)
    return pl.ds(0, cfg.tk)


def _attn_fwd_small(cfg, q, ks, v, name, overlap=None):
    s, tq, tk, tb, nsub, nh = cfg.s, cfg.tq, cfg.tk, cfg.tb, cfg.nsub, cfg.h
    nkp = len(ks)
    select = overlap is not None
    n_s = s // SLC_LEN
    top_n = min(SLC_TOPN, n_s)

    def kern(*refs):
        q_ref, k_refs, v_ref = refs[0], refs[1:1 + nkp], refs[1 + nkp]
        ov_ref = refs[2 + nkp] if select else None
        o_ref, lc_ref, lr_ref = refs[2 + nkp + select:5 + nkp + select]
        i = pl.program_id(0)
        imps = [jnp.zeros((tb, LANE), F32)] * nsub
        for r in range(nsub):
            rows = slice(r * tb, (r + 1) * tb)
            t = i * tq + r * tb + lax.broadcasted_iota(jnp.int32, (tb, 1), 0)
            win = _key_window(cfg, i, r)
            n = win.start + lax.broadcasted_iota(jnp.int32, (1, tk), 1)
            for hh in range(nh):
                qv = q_ref[rows, hh * cfg.dk:(hh + 1) * cfg.dk].astype(BF16)
                kk = _load_keys([kr.at[:, _head_cols(p, hh)] for kr, p in zip(k_refs, ks)], win)
                vv = v_ref[win, _head_cols(v, hh)].astype(BF16)
                sc, mask = _scores(cfg, _nt(qv, kk), t, n, hh, None, True)
                m = jnp.max(sc, axis=1, keepdims=True)
                e = jnp.exp2(sc - m)
                if mask is not None:
                    e = jnp.where(mask, e, 0.0)
                l = jnp.sum(e, axis=1, keepdims=True)
                o_ref[rows, hh * HEAD_V:(hh + 1) * HEAD_V] = _nn(e.astype(BF16), vv) / (l + 1e-20)
                lse = m + jnp.log(l + 1e-20) * LOG2E
                lc_ref[hh, rows, :] = lse
                lr_ref[hh, r] = _to_row(lse)
                if select:
                    imps[r] = imps[r] + _nn((e / (l + 1e-20)).astype(BF16), ov_ref[...])
        if select:
            sel_ref, selt_ref, imp_t = refs[5 + nkp + select:8 + nkp + select]
            for r in range(nsub):
                t = i * tq + r * tb + lax.broadcasted_iota(jnp.int32, (tb, 1), 0)
                j = lax.broadcasted_iota(jnp.int32, (tb, LANE), 1)
                cur = t >> 6
                imp = jnp.where((j == 0) | (j == cur) | (j == cur - 1), 1e9, imps[r])
                imp = jnp.where(j > cur, -1e9, imp)
                imp_t[r] = jnp.transpose(imp)
                mine = imp_t[r, 0:n_s, :]
                jrow = lax.broadcasted_iota(jnp.int32, (n_s, tb), 0)

                def count(k, rank):
                    other = imp_t[r, pl.ds(k, 1), :]
                    ahead = (other > mine) | ((other == mine) & (k < jrow))
                    return rank + jnp.where(ahead, 1.0, 0.0)

                rank = lax.fori_loop(0, n_s, count, jnp.zeros((n_s, tb), F32))
                cur_t = (i * tq + r * tb + lax.broadcasted_iota(jnp.int32, (1, tb), 1)) >> 6
                rejected = jnp.where((rank < top_n) & (jrow <= cur_t), 0.0, 1.0)
                if n_s < LANE:
                    rejected = jnp.concatenate([rejected, jnp.ones((LANE - n_s, tb), F32)], axis=0)
                selt_ref[r] = rejected
                sel_ref[r * tb:(r + 1) * tb, :] = jnp.transpose(rejected)

    ins = [_all_heads(cfg, q, tq)] + [_all_heads(cfg, p, cfg.sk, True) for p in ks]
    ins += [_all_heads(cfg, v, cfg.sk, True)]
    args = [q.arr] + [p.arr for p in ks] + [v.arr]
    out_specs = [pl.BlockSpec((tq, nh * HEAD_V), lambda i: (i, 0)),
                 pl.BlockSpec((nh, tq, 1), lambda i: (0, i, 0)),
                 pl.BlockSpec((nh, nsub, 1, tb), lambda i: (0, i, 0, 0))]
    out_shape = [jax.ShapeDtypeStruct((s, nh * HEAD_V), F32), jax.ShapeDtypeStruct((nh, s, 1), F32),
                 jax.ShapeDtypeStruct((nh, cfg.nb, 1, tb), F32)]
    scratch = []
    if select:
        ins.append(pl.BlockSpec((cfg.sk, LANE), lambda i: (0, 0)))
        args.append(overlap)
        out_specs += [pl.BlockSpec((tq, LANE), lambda i: (i, 0)), pl.BlockSpec((nsub, LANE, tb), lambda i: (i, 0, 0))]
        out_shape += [jax.ShapeDtypeStruct((s, LANE), F32), jax.ShapeDtypeStruct((cfg.nb, LANE, tb), F32)]
        scratch = [pltpu.VMEM((nsub, LANE, tb), F32)]
    return _pcall(kern, name=name, grid=(cfg.nq,), in_specs=ins, out_specs=out_specs, out_shape=out_shape,
                  scratch_shapes=scratch,
                  compiler_params=pltpu.CompilerParams(dimension_semantics=("parallel",)))(*args)


def _attn_dq_small(cfg, q, ks, v, o, lse, do, dq_in, name):
    s, tq, tk, tb, nsub, nh, dk = cfg.s, cfg.tq, cfg.tk, cfg.tb, cfg.nsub, cfg.h, cfg.dk
    nkp = len(ks)
    has_in = dq_in is not None

    def kern(*refs):
        q_ref, k_refs, v_ref = refs[0], refs[1:1 + nkp], refs[1 + nkp]
        o_ref, l_ref, do_ref = refs[2 + nkp:5 + nkp]
        in_ref = refs[5 + nkp] if has_in else None
        dq_ref, dr_ref = refs[5 + nkp + has_in:7 + nkp + has_in]
        i = pl.program_id(0)
        for r in range(nsub):
            rows = slice(r * tb, (r + 1) * tb)
            t = i * tq + r * tb + lax.broadcasted_iota(jnp.int32, (tb, 1), 0)
            win = _key_window(cfg, i, r)
            n = win.start + lax.broadcasted_iota(jnp.int32, (1, tk), 1)
            for hh in range(nh):
                vcols = slice(hh * HEAD_V, (hh + 1) * HEAD_V)
                qcols = slice(hh * dk, (hh + 1) * dk)
                qv = q_ref[rows, qcols].astype(BF16)
                kk = _load_keys([kr.at[:, _head_cols(p, hh)] for kr, p in zip(k_refs, ks)], win)
                vv = v_ref[win, _head_cols(v, hh)].astype(BF16)
                dov = do_ref[rows, vcols]
                dvec = jnp.sum(dov * o_ref[rows, vcols], axis=1, keepdims=True)
                dr_ref[hh, r] = _to_row(dvec)
                p, _ = _scores(cfg, _nt(qv, kk), t, n, hh, None, True, l_ref[hh, rows, :])
                ds = p * (_nt(dov.astype(BF16), vv) - dvec)
                dq = _nn(ds.astype(BF16), kk) * cfg.scale
                dq_ref[rows, qcols] = dq + in_ref[rows, qcols] if has_in else dq

    qs = pl.BlockSpec((tq, nh * dk), lambda i: (i, 0))
    ins = [_all_heads(cfg, q, tq)] + [_all_heads(cfg, p, cfg.sk, True) for p in ks]
    ins += [_all_heads(cfg, v, cfg.sk, True)]
    ins += [_all_heads(cfg, o, tq), pl.BlockSpec((nh, tq, 1), lambda i: (0, i, 0)), _all_heads(cfg, do, tq)]
    args = [q.arr] + [p.arr for p in ks] + [v.arr, o.arr, lse, do.arr]
    if has_in:
        ins.append(qs)
        args.append(dq_in)
    return _pcall(
        kern, name=name, grid=(cfg.nq,), in_specs=ins,
        out_specs=[qs, pl.BlockSpec((nh, nsub, 1, tb), lambda i: (0, i, 0, 0))],
        out_shape=[jax.ShapeDtypeStruct((s, nh * dk), F32), jax.ShapeDtypeStruct((nh, cfg.nb, 1, tb), F32)],
        compiler_params=pltpu.CompilerParams(dimension_semantics=("parallel",)))(*args)


def _attn_bwd(cfg, q, ks, v, sel, selt, o, lse, lse_r, do, dq_in, name):
    if cfg.causal:
        dq, d_r = _attn_dq(cfg, q, ks, v, sel, o, lse, do, dq_in, name + "_dq")
    else:
        dq, d_r = _attn_dq_small(cfg, q, ks, v, o, lse, do, dq_in, name + "_dq")
    dkv = _attn_dkv_flat if cfg.causal or cfg.mode == "win" else _attn_dkv
    res = dkv(cfg, q, ks, v, selt, lse_r, d_r, do, name + "_dkv")
    return dq, res[:-1], res[-1]


def _silu_grad(pre):
    sg = _sigmoid(pre)
    return sg * (1.0 + pre * (1.0 - sg))


def _compress_fwd(a_lo, a_hi, pe_lo, pe_hi, w1_lo, w1_hi, w2, name):
    n, dp = a_lo.shape[0], w2.shape[1]

    def kern(alo, ahi, plo, phi, w1l, w1h, w2r, out_ref, pre_ref):
        xl = (alo[...] + plo[...]).astype(BF16)
        xh = (ahi[...] + phi[...]).astype(BF16)
        pre = _nn(xl, w1l[...]) + _nn(xh, w1h[...])
        act = pre * _sigmoid(pre)
        out_ref[...] = _nn(act.astype(BF16), w2r[...]).astype(BF16)
        pre_ref[...] = pre

    return _pcall(kern, name=name,
                  out_shape=[jax.ShapeDtypeStruct((n, dp), BF16), jax.ShapeDtypeStruct((n, dp), F32)],
                  )(a_lo, a_hi, pe_lo, pe_hi, w1_lo, w1_hi, w2)


def _compress_bwd(a_lo, a_hi, pe_lo, pe_hi, w1_lo, w1_hi, w2, pre, pre_sh, dout, dout_sh, name):
    n, ln = a_lo.shape
    dp = w2.shape[1]

    def kern(alo, ahi, plo, phi, w1l, w1h, w2r, pre_ref, presh_ref, do_ref, dosh_ref,
             da_ref, dpl_ref, dph_ref, dw1l_ref, dw1h_ref, dw2_ref):
        prev = pre_ref[...]
        act = prev * _sigmoid(prev)
        dob = do_ref[...].astype(BF16)
        w2v = w2r[...]
        dpre = (_nt(dob, w2v) * _silu_grad(prev)).astype(BF16)
        dpre_sh = (_nt(dosh_ref[...].astype(BF16), w2v) * _silu_grad(presh_ref[...])).astype(BF16)
        dw2_ref[...] = _nn(act.T.astype(BF16), dob)
        xl = alo[...] + plo[...]
        xh = ahi[...] + phi[...]
        dw1l_ref[...] = _nn(xl.T.astype(BF16), dpre)
        dw1h_ref[...] = _nn(xh.T.astype(BF16), dpre)
        dal = _nt(dpre, w1l[...])
        dah_sh = _nt(dpre_sh, w1h[...])
        da_ref[...] = dal + dah_sh
        dpl_ref[...] = jnp.sum(dal, axis=0, keepdims=True)
        dph_ref[...] = jnp.sum(dah_sh, axis=0, keepdims=True)

    return _pcall(
        kern, name=name,
        out_shape=[jax.ShapeDtypeStruct((n, ln), F32), jax.ShapeDtypeStruct((1, ln), F32),
                   jax.ShapeDtypeStruct((1, ln), F32), jax.ShapeDtypeStruct((ln, dp), F32),
                   jax.ShapeDtypeStruct((ln, dp), F32), jax.ShapeDtypeStruct((dp, dp), F32)],
    )(a_lo, a_hi, pe_lo, pe_hi, w1_lo, w1_hi, w2, pre, pre_sh, dout, dout_sh)


def _nsa_combine(o_cmp, o_slc, o_win, gl):
    s, w = o_cmp.shape
    tr = _tile(s, 512)

    def kern(a_ref, b_ref, c_ref, g_ref, o_ref):
        g = _sigmoid(g_ref[...])
        for h in range(NSA_HEADS):
            cs = slice(h * HEAD_V, (h + 1) * HEAD_V)
            o_ref[:, cs] = (g[:, 3 * h:3 * h + 1] * a_ref[:, cs] + g[:, 3 * h + 1:3 * h + 2] * b_ref[:, cs]
                            + g[:, 3 * h + 2:3 * h + 3] * c_ref[:, cs])

    row = pl.BlockSpec((tr, w), lambda i: (i, 0))
    return _pcall(kern, name="nsa_combine", grid=(s // tr,),
                  in_specs=[row, row, row, pl.BlockSpec((tr, LANE), lambda i: (i, gl.col0))], out_specs=row,
                  out_shape=jax.ShapeDtypeStruct((s, w), F32))(o_cmp, o_slc, o_win, gl.arr)


def _nsa_combine_bwd(do_cat, o_cmp, o_slc, o_win, gl):
    s, w = o_cmp.shape
    tr = _tile(s, 512)

    def kern(d_ref, a_ref, b_ref, c_ref, g_ref, da_ref, db_ref, dc_ref, dg_ref):
        g = _sigmoid(g_ref[...])
        lane = lax.broadcasted_iota(jnp.int32, (tr, LANE), 1)
        dgl = jnp.zeros((tr, LANE), F32)
        for h in range(NSA_HEADS):
            cs = slice(h * HEAD_V, (h + 1) * HEAD_V)
            dv = d_ref[:, cs]
            for b, (src, dst) in enumerate(((a_ref, da_ref), (b_ref, db_ref), (c_ref, dc_ref))):
                gate = g[:, 3 * h + b:3 * h + b + 1]
                dst[:, cs] = gate * dv
                dgate = jnp.sum(dv * src[:, cs], axis=1, keepdims=True)
                dgl = jnp.where(lane == 3 * h + b, dgate * gate * (1.0 - gate), dgl)
        dg_ref[...] = dgl

    row = pl.BlockSpec((tr, w), lambda i: (i, 0))
    tab = pl.BlockSpec((tr, LANE), lambda i: (i, 0))
    return _pcall(kern, name="nsa_combine_bwd", grid=(s // tr,),
                  in_specs=[pl.BlockSpec((tr, w), lambda i: (i, 2)), row, row, row,
                            pl.BlockSpec((tr, LANE), lambda i: (i, gl.col0))],
                  out_specs=[row, row, row, tab],
                  out_shape=[jax.ShapeDtypeStruct((s, w), F32)] * 3 + [jax.ShapeDtypeStruct((s, LANE), F32)],
                  )(do_cat, o_cmp, o_slc, o_win, gl.arr)


def _gate_fwd(o_mla, o_nsa, o_mem, hp):
    s = o_mla.shape[0]
    tr = _tile(s, 256)

    def kern(a_ref, b_ref, c_ref, z_ref, u_ref):
        z = z_ref[...]
        sz = z * _sigmoid(z)
        u_ref[:, 0:1024] = (a_ref[...] * sz[:, 0:1024]).astype(BF16)
        u_ref[:, 1024:1536] = (b_ref[...] * sz[:, 1024:1536]).astype(BF16)
        u_ref[:, 1536:2048] = (c_ref[...] * sz[:, 1536:2048]).astype(BF16)

    return _pcall(
        kern, name="gate_fwd", grid=(s // tr,),
        in_specs=[pl.BlockSpec((tr, 1024), lambda i: (i, 0)), pl.BlockSpec((tr, 512), lambda i: (i, 0)),
                  pl.BlockSpec((tr, 512), lambda i: (i, 0)), pl.BlockSpec((tr, 2048), lambda i: (i, 2))],
        out_specs=pl.BlockSpec((tr, 2048), lambda i: (i, 0)),
        out_shape=jax.ShapeDtypeStruct((s, 2048), BF16))(o_mla, o_nsa, o_mem, hp)


def _gate_bwd(du, o_mla, o_nsa, o_mem, hp):
    s = du.shape[0]
    tr = _tile(s, 256)

    def kern(d_ref, a_ref, b_ref, c_ref, z_ref, do_ref, dz_ref):
        z = z_ref[...]
        sg = _sigmoid(z)
        sz = z * sg
        dsz = sg * (1.0 + z * (1.0 - sg))
        d = d_ref[...]
        do_ref[...] = d * sz
        dz_ref[:, 0:1024] = d[:, 0:1024] * a_ref[...] * dsz[:, 0:1024]
        dz_ref[:, 1024:1536] = d[:, 1024:1536] * b_ref[...] * dsz[:, 1024:1536]
        dz_ref[:, 1536:2048] = d[:, 1536:2048] * c_ref[...] * dsz[:, 1536:2048]

    wide = pl.BlockSpec((tr, 2048), lambda i: (i, 0))
    return _pcall(
        kern, name="gate_bwd", grid=(s // tr,),
        in_specs=[wide, pl.BlockSpec((tr, 1024), lambda i: (i, 0)), pl.BlockSpec((tr, 512), lambda i: (i, 0)),
                  pl.BlockSpec((tr, 512), lambda i: (i, 0)), pl.BlockSpec((tr, 2048), lambda i: (i, 2))],
        out_specs=[wide, wide],
        out_shape=[jax.ShapeDtypeStruct((s, 2048), F32)] * 2)(du, o_mla, o_nsa, o_mem, hp)


def _tile2d(rows, cols, arrays):
    if rows % 16 == 0:
        return _row_tile(rows, cols * arrays), cols
    want = max(LANE, BLOCK_BYTES // (rows * 4 * arrays) // LANE * LANE)
    tc = LANE
    for t in range(LANE, cols + 1, LANE):
        if cols % t == 0 and t <= want:
            tc = t
    return rows, tc


def _sum_slots(buf, name):
    n, rows, cols = buf.shape
    tr, tc = _tile2d(rows, cols, n)

    def kern(b_ref, o_ref):
        acc = b_ref[0].astype(F32)
        for i in range(1, n):
            acc = acc + b_ref[i].astype(F32)
        o_ref[...] = acc

    return _pcall(kern, name=name, grid=(rows // tr, cols // tc),
                  in_specs=[pl.BlockSpec((n, tr, tc), lambda i, j: (0, i, j))],
                  out_specs=pl.BlockSpec((tr, tc), lambda i, j: (i, j)),
                  out_shape=jax.ShapeDtypeStruct((rows, cols), F32))(buf)


def _pair_sum(g4, theirs, core, axis, name):
    n, rows, cols = theirs.shape
    tr, tc = _tile2d(rows, cols, 1)
    nbr, nbc = rows // tr, cols // tc

    def kern(c_ref, a_ref, b_ref, o_ref):
        o_ref[...] = (a_ref[...] + b_ref[...]).astype(BF16)

    blk = (1, tr, tc)
    mine = ((lambda s, i, j, c: (s, c[0] * nbr + i, j)) if axis == 0
            else (lambda s, i, j, c: (s, i, c[0] * nbc + j)))
    grid_spec = pltpu.PrefetchScalarGridSpec(
        num_scalar_prefetch=1, grid=(n, nbr, nbc),
        in_specs=[pl.BlockSpec(blk, mine), pl.BlockSpec(blk, lambda s, i, j, c: (s, i, j))],
        out_specs=pl.BlockSpec(blk, lambda s, i, j, c: (s, i, j)))
    return _pcall(kern, name=name, grid_spec=grid_spec,
                  out_shape=jax.ShapeDtypeStruct((n, rows, cols), BF16))(core, g4, theirs)


def _adamw(w, g, m, v, name):
    rows, cols = w.shape
    tr, tc = _tile2d(rows, cols, 4)
    bc1 = 1.0 - ADAM_B1 ** ADAM_STEP
    bc2 = 1.0 - ADAM_B2 ** ADAM_STEP

    def kern(w_ref, g_ref, m_ref, v_ref, d_ref, mo_ref, vo_ref):
        gv = g_ref[...]
        mn = ADAM_B1 * m_ref[...] + (1.0 - ADAM_B1) * gv
        vn = ADAM_B2 * v_ref[...] + (1.0 - ADAM_B2) * (gv * gv)
        d_ref[...] = -ADAM_LR * ((mn / bc1) / (jnp.sqrt(vn / bc2) + ADAM_EPS) + ADAM_WD * w_ref[...])
        mo_ref[...] = mn
        vo_ref[...] = vn

    blk = pl.BlockSpec((tr, tc), lambda i, j: (i, j))
    return _pcall(kern, name=name, grid=(rows // tr, cols // tc), in_specs=[blk] * 4, out_specs=[blk] * 3,
                  out_shape=[jax.ShapeDtypeStruct((rows, cols), F32)] * 3)(w, g, m, v)


ANY = pl.BlockSpec(memory_space=pl.ANY)


def _place():
    x, y, c = lax.axis_index("x"), lax.axis_index("y"), lax.axis_index("c")
    chips = [(1 - x, y), (x, 1 - y), (1 - x, 1 - y)]
    return x, y, c, chips


def _remote(src, dst, send_sem, recv_sem, to):
    return pltpu.make_async_remote_copy(src_ref=src, dst_ref=dst, send_sem=send_sem, recv_sem=recv_sem,
                                        device_id=to, device_id_type=MESH)


def _half(ref, lead, core, axis):
    size = ref.shape[len(lead) + axis] // 2
    cut = pl.ds(core * size, size)
    return ref.at[tuple(lead) + ((cut, slice(None)) if axis == 0 else (slice(None), cut))]


def _gather_shards(ws, axes):
    side = _gather_side(ws, axes)

    def body(*refs):
        nw = len(ws)
        split = (refs[:nw], refs[nw:2 * nw], refs[2 * nw:])
        side.phase("start", *split)
        side.phase("finish", *split)

    return _pcall(body, name="gather_shards", in_specs=[ANY] * len(ws), out_specs=[ANY] * len(ws),
                  out_shape=side.out_shape, scratch_shapes=side.scratch)(*ws)


def _gather_side(ws, axes):
    nw = len(ws)

    def phase(which, w_refs, out_refs, sems):
        send_sems, recv_sems = sems
        x, y, c, chips = _place()
        me = 2 * x + y
        sibling = (x, y, 1 - c)

        def part(i, slot, core):
            return _half(out_refs[i], (slot,), core, axes[i])

        def copy(sem, src, dst, to):
            return _remote(src, dst, send_sems.at[sem], recv_sems.at[sem], to)

        first = [copy(j * nw + i, _half(w_refs[i], (), c, axes[i]), part(i, me, c), (*chip, c))
                 for j, chip in enumerate(chips) for i in range(nw)]
        if which == "start":
            for cp in first:
                cp.start()
            return
        passed = []
        for j, (cx, cy) in enumerate(chips):
            slot = 2 * cx + cy
            for i in range(nw):
                copy(j * nw + i, part(i, slot, c), part(i, slot, c), (x, y, c)).wait_recv()
                fwd = copy((3 + j) * nw + i, part(i, slot, c), part(i, slot, c), sibling)
                fwd.start()
                passed.append(fwd)
        for j, (cx, cy) in enumerate(chips):
            slot = 2 * cx + cy
            for i in range(nw):
                copy((3 + j) * nw + i, part(i, slot, 1 - c), part(i, slot, 1 - c), (x, y, c)).wait_recv()
        for cp in first + passed:
            cp.wait_send()

    return _Side(list(ws), [jax.ShapeDtypeStruct((4,) + w.shape, w.dtype) for w in ws],
                 [pltpu.SemaphoreType.DMA((6 * nw,)), pltpu.SemaphoreType.DMA((6 * nw,))], phase)


def _half_shape(shape, axis):
    return tuple(d // 2 if k == len(shape) - 2 + axis else d for k, d in enumerate(shape))


def _pair_exchange(gs, axes, name):
    nw = len(gs)

    def body(*refs):
        g_refs, out_refs = refs[:nw], refs[nw:2 * nw]
        send_sems, recv_sems = refs[2 * nw:]
        x, y, c, _ = _place()
        cps = []
        for i in range(nw):
            cp = _remote(_half(g_refs[i], (slice(None),), 1 - c, axes[i]), out_refs[i],
                         send_sems.at[i], recv_sems.at[i], (x, y, 1 - c))
            cp.start()
            cps.append(cp)
        for cp in cps:
            cp.wait()

    return _pcall(body, name=name, in_specs=[ANY] * nw, out_specs=[ANY] * nw,
                  out_shape=[jax.ShapeDtypeStruct(_half_shape(g.shape, a), g.dtype) for g, a in zip(gs, axes)],
                  scratch_shapes=[pltpu.SemaphoreType.DMA((nw,)), pltpu.SemaphoreType.DMA((nw,))])(*gs)


def _chip_exchange(ps):
    side = _chip_side(ps)

    def body(*refs):
        nw = len(ps)
        split = (refs[:nw], refs[nw:2 * nw], refs[2 * nw:])
        side.phase("start", *split)
        side.phase("finish", *split)

    return _pcall(body, name="chip_exchange", in_specs=[ANY] * len(ps), out_specs=[ANY] * len(ps),
                  out_shape=side.out_shape, scratch_shapes=side.scratch)(*ps)


def _chip_side(ps):
    nw = len(ps)

    def phase(which, p_refs, out_refs, sems):
        send_sems, recv_sems, local_sems = sems
        x, y, c, chips = _place()
        me = 2 * x + y
        mine = [pltpu.make_async_copy(p_refs[i].at[me], out_refs[i].at[me], local_sems.at[i]) for i in range(nw)]
        sends = [_remote(p_refs[i].at[2 * cx + cy], out_refs[i].at[me], send_sems.at[j * nw + i],
                         recv_sems.at[j * nw + i], (cx, cy, c))
                 for j, (cx, cy) in enumerate(chips) for i in range(nw)]
        if which == "start":
            for cp in mine + sends:
                cp.start()
            return
        for j, (cx, cy) in enumerate(chips):
            slot = 2 * cx + cy
            for i in range(nw):
                _remote(out_refs[i].at[slot], out_refs[i].at[slot], send_sems.at[j * nw + i],
                        recv_sems.at[j * nw + i], (x, y, c)).wait_recv()
        for cp in sends:
            cp.wait_send()
        for cp in mine:
            cp.wait()

    return _Side(list(ps), [jax.ShapeDtypeStruct(p.shape, p.dtype) for p in ps],
                 [pltpu.SemaphoreType.DMA((3 * nw,)), pltpu.SemaphoreType.DMA((3 * nw,)),
                  pltpu.SemaphoreType.DMA((nw,))], phase)


def _half_exchange(ts, axes):
    nw = len(ts)

    def body(*refs):
        t_refs, out_refs = refs[:nw], refs[nw:2 * nw]
        send_sems, recv_sems = refs[2 * nw:]
        x, y, c, _ = _place()
        sends = []
        for i in range(nw):
            cp = _remote(t_refs[i], _half(out_refs[i], (), c, axes[i]), send_sems.at[i], recv_sems.at[i],
                         (x, y, 1 - c))
            cp.start()
            sends.append(cp)
        for i in range(nw):
            _remote(t_refs[i], _half(out_refs[i], (), 1 - c, axes[i]), send_sems.at[i], recv_sems.at[i],
                    (x, y, c)).wait_recv()
        for cp in sends:
            cp.wait_send()

    def whole(t, a):
        return tuple(2 * d if k == a else d for k, d in enumerate(t.shape))

    return _pcall(body, name="half_exchange", in_specs=[ANY] * nw, out_specs=[ANY] * nw,
                  out_shape=[jax.ShapeDtypeStruct(whole(t, a), t.dtype) for t, a in zip(ts, axes)],
                  scratch_shapes=[pltpu.SemaphoreType.DMA((nw,)), pltpu.SemaphoreType.DMA((nw,))])(*ts)


def _gather_all(v):
    rows, cols = v.shape

    def body(v_ref, out_ref, send_sems, recv_sems, local_sem):
        x, y, c, _ = _place()
        me = 4 * x + 2 * y + c
        mine = pltpu.make_async_copy(v_ref, out_ref.at[me], local_sem)
        mine.start()
        sends = []
        for d in range(1, 8):
            peer = (x ^ (d >> 2), y ^ ((d >> 1) & 1), c ^ (d & 1))
            cp = _remote(v_ref, out_ref.at[me], send_sems.at[d - 1], recv_sems.at[d - 1], peer)
            cp.start()
            sends.append(cp)
        for d in range(1, 8):
            slot = 4 * (x ^ (d >> 2)) + 2 * (y ^ ((d >> 1) & 1)) + (c ^ (d & 1))
            _remote(v_ref, out_ref.at[slot], send_sems.at[d - 1], recv_sems.at[d - 1], (x, y, c)).wait_recv()
        for cp in sends:
            cp.wait_send()
        mine.wait()

    return _pcall(body, name="gather_all", in_specs=[ANY], out_specs=ANY,
                  out_shape=jax.ShapeDtypeStruct((8, rows, cols), v.dtype),
                  scratch_shapes=[pltpu.SemaphoreType.DMA((7,)), pltpu.SemaphoreType.DMA((7,)),
                                  pltpu.SemaphoreType.DMA])(v)


def _pad_cols(a, width):
    return a if a.shape[1] == width else jnp.pad(a, ((0, 0), (0, width - a.shape[1])))


def _unpad_segments():
    z = PAD["z"]
    segs = [(PAD["c_q"], 0, 512), (PAD["c_kv"], 512, 512), (PAD["k_rope"], 1024, 64), (z, 1088, 1024)]
    segs += [(PAD["q_nsa"] + 256 * h, 2112 + NSA_DK * h, NSA_DK) for h in range(NSA_HEADS)]
    for name, rows in (("k_c", 192), ("v_c", 128), ("k_s", 192), ("v_s", 128), ("k_w", 192), ("v_w", 128),
                       ("g_nsa", 12)):
        segs.append((PAD[name], ORIG[name][0], rows))
    segs += [(z + 1024, ORIG["z_nsa"][0], 512), (PAD["q_mem"], ORIG["q_mem"][0], 512),
             (z + 1536, ORIG["z_mem"][0], 512)]
    return segs


def _w_in_grad_slots(gt):
    rows, cols = gt.shape
    shard = sum(n for _, _, n in _unpad_segments()) // 4
    tc = 256
    pieces = []
    for src, dst, n in _unpad_segments():
        while n:
            slot, off = divmod(dst, shard)
            take = min(n, shard - off)
            pieces.append((src, slot, off, take))
            src, dst, n = src + take, dst + take, n - take

    def kern(g_ref, o_ref):
        for src, slot, off, take in pieces:
            o_ref[slot, off:off + take, :] = g_ref[src:src + take, :]

    return _pcall(kern, name="w_in_grad_slots", grid=(cols // tc,),
                  in_specs=[pl.BlockSpec((rows, tc), lambda i: (0, i))],
                  out_specs=pl.BlockSpec((4, shard, tc), lambda i: (0, 0, i)),
                  out_shape=jax.ShapeDtypeStruct((4, shard, cols), F32))(gt)


def _w_in_from_slots(ws):
    nslot, shard, cols = ws.shape
    tc = 256
    pieces = []
    for dst, src, n in _unpad_segments() + [(PAD["k_rope"] + 64, ORIG["k_rope"][0], 64)]:
        while n:
            slot, off = divmod(src, shard)
            take = min(n, shard - off)
            pieces.append((dst, slot, off, take))
            src, dst, n = src + take, dst + take, n - take

    def kern(w_ref, o_ref):
        o_ref[...] = jnp.zeros_like(o_ref)
        for dst, slot, off, take in pieces:
            o_ref[dst:dst + take, :] = w_ref[slot, off:off + take, :]

    return _pcall(kern, name="w_in_from_slots", grid=(cols // tc,),
                  in_specs=[pl.BlockSpec((nslot, shard, tc), lambda i: (0, 0, i))],
                  out_specs=pl.BlockSpec((D_PAD, tc), lambda i: (0, i)),
                  out_shape=jax.ShapeDtypeStruct((D_PAD, cols), ws.dtype))(ws)


def _rope_tables(s):
    pos = jnp.arange(s, dtype=F32)
    inv_freq = ROPE_THETA ** (-jnp.arange(0, 64, 2, dtype=F32) / 64)
    ang = pos[:, None] * inv_freq[None, :]
    cos, sin = jnp.cos(ang), jnp.sin(ang)
    z = jnp.zeros((s, 64), F32)
    return jnp.concatenate([cos, cos, z], axis=1), jnp.concatenate([-sin, sin, z], axis=1)


def _overlap_table(s):
    n_c, n_s = s // CMP_STRIDE, s // SLC_LEN
    c0 = np.arange(n_c)[:, None] * CMP_STRIDE
    s0 = np.arange(LANE)[None, :] * SLC_LEN
    ov = (c0 < s0 + SLC_LEN) & (c0 + CMP_LEN > s0) & (np.arange(n_c)[:, None] < n_c - 1) & (np.arange(LANE)[None, :] < n_s)
    return jnp.asarray(ov.astype(np.float32), dtype=BF16)


def _shift_down(a):
    return jnp.concatenate([jnp.zeros((8, a.shape[1]), a.dtype), a], axis=0)[7:7 + a.shape[0]]


def _shift_up(a):
    return jnp.concatenate([a, jnp.zeros((8, a.shape[1]), a.dtype)], axis=0)[1:1 + a.shape[0]]


def _local_step(x, mem, target, w, hooks=None):
    s = x.shape[0]
    cs, sn = _rope_tables(s)
    t_ = jnp.transpose

    w_in_p = _w_in_from_slots(w["w_in_t"])
    xn, rstd_x = _rms_fwd(_Src(x, D_MODEL), w["norm_g"], "norm_x")
    if hooks is None:
        hp, hpb = _mm(xn, w_in_p, "in_proj", mode="nt", second_dtype=BF16)
    else:
        hp, hpb, *gathered = _mm(xn, w_in_p, "in_proj", mode="nt", second_dtype=BF16, side=hooks.gather_side)
        w = {**w, **hooks.weights(gathered)}

    w_uq3 = w["w_uq"].reshape(512, MLA_HEADS, 192)
    w_uq_p = jnp.concatenate([w_uq3, w_uq3[:, :, 128:]], axis=2).reshape(512, MLA_HEADS * 256)
    w_ukv_p = t_(w["w_ukv"].reshape(512, MLA_HEADS, 2, 128), (0, 2, 1, 3)).reshape(512, 2048)
    c_q, c_kv = _Src(hp, 512, 0), _Src(hp, 512, 1)
    cqn, rstd_q = _rms_fwd(c_q, w["q_norm_g"], "norm_q")
    ckvn, rstd_kv = _rms_fwd(c_kv, w["kv_norm_g"], "norm_kv")
    q_lin = _mm(cqn, w_uq_p, "mla_q_proj")
    kvb = _mm(ckvn, w_ukv_p, "mla_kv_proj", out_dtype=BF16)
    q_mla = _rope_fwd(_Src(q_lin, MLA_HEADS * 256), cs, sn, MLA_HEADS, 256, LANE, "rope_q")
    k_pe = _rope_fwd(_Src(hp, LANE, PAD["k_rope"] // LANE), cs, sn, 1, LANE, 0, "rope_k")
    mla = _Attn("mla", s, s, MLA_HEADS, 256)
    mla_q, mla_v = _Src(q_mla, 256), _Src(kvb, LANE, MLA_HEADS)
    mla_k = [_Src(kvb, LANE), _Src(k_pe, LANE, 0, False)]
    o_mla, l_mla, lr_mla = _attn_fwd(mla, mla_q, mla_k, mla_v, None, "mla_fwd")

    sk = s // CMP_STRIDE
    pe_k, pe_v = w["cmp_pe_k"], w["cmp_pe_v"]
    w1k = _pad_cols(w["cmp_w1k"], 256)
    w2k = jnp.pad(w["cmp_w2k"], ((0, 64), (0, 64))).astype(BF16)
    w1v, w2v = w["cmp_w1v"], w["cmp_w2v"].astype(BF16)
    half_k, half_v = CMP_STRIDE * NSA_DK, CMP_STRIDE * HEAD_V
    ak = hp[:, PAD["k_c"]:PAD["k_c"] + NSA_DK].reshape(sk, half_k)
    av = hp[:, PAD["v_c"]:PAD["v_c"] + HEAD_V].reshape(sk, half_v)
    ck_args = (ak, _shift_up(ak), pe_k[:CMP_STRIDE].reshape(1, half_k), pe_k[CMP_STRIDE:].reshape(1, half_k),
               w1k[:half_k], w1k[half_k:], w2k)
    cv_args = (av, _shift_up(av), pe_v[:CMP_STRIDE].reshape(1, half_v), pe_v[CMP_STRIDE:].reshape(1, half_v),
               w1v[:half_v], w1v[half_v:], w2v)
    k_cmp, pre_k = _compress_fwd(*ck_args, "compress_k")
    v_cmp, pre_v = _compress_fwd(*cv_args, "compress_v")
    cmp_ = _Attn("cmp", s, sk, NSA_HEADS, 256)
    slc = _Attn("slc", s, s, NSA_HEADS, 256)
    win = _Attn("win", s, s, NSA_HEADS, 256)
    nsa_q = _Src(hpb, 256, PAD["q_nsa"] // 256)
    cmp_k, cmp_v = [_Src(k_cmp, 256, 0, False)], _Src(v_cmp, HEAD_V, 0, False)
    slc_k, slc_v = [_Src(hpb, 256, PAD["k_s"] // 256, False)], _Src(hpb, HEAD_V, PAD["v_s"] // HEAD_V, False)
    win_k, win_v = [_Src(hpb, 256, PAD["k_w"] // 256, False)], _Src(hpb, HEAD_V, PAD["v_w"] // HEAD_V, False)
    o_cmp, l_cmp, lr_cmp, sel, selt = _attn_fwd_small(cmp_, nsa_q, cmp_k, cmp_v, "cmp_fwd", _overlap_table(s))
    o_slc, l_slc, lr_slc = _attn_fwd(slc, nsa_q, slc_k, slc_v, sel, "slc_fwd")
    o_win, l_win, lr_win = _attn_fwd_small(win, nsa_q, win_k, win_v, "win_fwd")
    gl = _Src(hp, LANE, PAD["g_nsa"] // LANE)
    o_nsa = _nsa_combine(o_cmp, o_slc, o_win, gl)

    mn, rstd_m = _rms_fwd(_Src(mem, D_MODEL), w["mem_norm_g"], "norm_mem")
    kvm = _mm(mn, w["w_mem_kv"], "mem_kv_proj", out_dtype=BF16)
    mem_ = _Attn("mem", s, mem.shape[0], MEM_HEADS, LANE)
    mem_q, mem_k, mem_v = _Src(hpb, LANE, PAD["q_mem"] // LANE), [_Src(kvm, LANE)], _Src(kvm, LANE, MEM_HEADS)
    o_mem, l_mem, lr_mem = _attn_fwd_small(mem_, mem_q, mem_k, mem_v, "mem_fwd")

    u = _gate_fwd(o_mla, o_nsa, o_mem, hp)
    proj = _mm(u, w["w_out"], "out_proj")
    dy, g_final, loss = _final_loss(x, proj, w["final_norm_g"].reshape(1, -1), target)

    g_w_out = _mm(u, dy, "out_proj_dw", mode="tn")
    du = _mm(dy, w["w_out"], "out_proj_dx", mode="nt")
    do_cat, dz = _gate_bwd(du, o_mla, o_nsa, o_mem, hp)

    dq_mem, (dk_mem,), dv_mem = _attn_bwd(mem_, mem_q, mem_k, mem_v, None, None, _Src(o_mem, HEAD_V), l_mem,
                                          lr_mem, _Src(do_cat, HEAD_V, 12), None, "mem_bwd")
    dkvm = jnp.concatenate([dk_mem, dv_mem], axis=1)
    g_w_mem_kv = _mm(mn, dkvm, "mem_kv_dw", mode="tn")
    dmn = _mm(dkvm, w["w_mem_kv"], "mem_kv_dx", mode="nt")
    _, g_mem_norm = _rms_bwd(_Src(mem, D_MODEL), w["mem_norm_g"], rstd_m, dmn, None, "norm_mem_bwd")

    do_cmp, do_slc, do_win, dgl = _nsa_combine_bwd(do_cat, o_cmp, o_slc, o_win, gl)
    dq_n, (dk_cmp,), dv_cmp = _attn_bwd(cmp_, nsa_q, cmp_k, cmp_v, None, None, _Src(o_cmp, HEAD_V), l_cmp,
                                        lr_cmp, _Src(do_cmp, HEAD_V), None, "cmp_bwd")
    dq_n, (dk_s,), dv_s = _attn_bwd(slc, nsa_q, slc_k, slc_v, sel, selt, _Src(o_slc, HEAD_V), l_slc, lr_slc,
                                    _Src(do_slc, HEAD_V), dq_n, "slc_bwd")
    dq_n, (dk_w,), dv_w = _attn_bwd(win, nsa_q, win_k, win_v, None, None, _Src(o_win, HEAD_V), l_win, lr_win,
                                    _Src(do_win, HEAD_V), dq_n, "win_bwd")
    dak, dpk_lo, dpk_hi, dw1k_lo, dw1k_hi, g_w2k = _compress_bwd(
        *ck_args, pre_k, _shift_down(pre_k), dk_cmp, _shift_down(dk_cmp), "compress_k_bwd")
    dav, dpv_lo, dpv_hi, dw1v_lo, dw1v_hi, g_w2v = _compress_bwd(
        *cv_args, pre_v, _shift_down(pre_v), dv_cmp, _shift_down(dv_cmp), "compress_v_bwd")
    g_pe_k = jnp.concatenate([dpk_lo.reshape(CMP_STRIDE, NSA_DK), dpk_hi.reshape(CMP_STRIDE, NSA_DK)], axis=0)
    g_pe_v = jnp.concatenate([dpv_lo.reshape(CMP_STRIDE, HEAD_V), dpv_hi.reshape(CMP_STRIDE, HEAD_V)], axis=0)
    g_w1k = jnp.concatenate([dw1k_lo, dw1k_hi], axis=0)[:, :NSA_DK]
    g_w1v = jnp.concatenate([dw1v_lo, dw1v_hi], axis=0)
    dk_c = _pad_cols(dak.reshape(s, NSA_DK), 256)
    dv_c = dav.reshape(s, HEAD_V)

    dq_m, (dk_nope, dk_pe), dv_m = _attn_bwd(mla, mla_q, mla_k, mla_v, None, None, _Src(o_mla, HEAD_V), l_mla,
                                             lr_mla, _Src(do_cat, HEAD_V), None, "mla_bwd")
    dq_lin = _rope_bwd_q(dq_m, cs, sn)
    dkv_lin, d_krope = _rope_bwd_k(dk_nope, dk_pe, dv_m, cs, sn)
    g_w_uq_p = _mm(cqn, dq_lin, "mla_q_dw", mode="tn")
    dcqn = _mm(dq_lin, w_uq_p, "mla_q_dx", mode="nt")
    g_w_ukv_p = _mm(ckvn, dkv_lin, "mla_kv_dw", mode="tn")
    dckvn = _mm(dkv_lin, w_ukv_p, "mla_kv_dx", mode="nt")
    dc_q, g_q_norm = _rms_bwd(c_q, w["q_norm_g"], rstd_q, dcqn, None, "norm_q_bwd")
    dc_kv, g_kv_norm = _rms_bwd(c_kv, w["kv_norm_g"], rstd_kv, dckvn, None, "norm_kv_bwd")
    g_w_uq = g_w_uq_p.reshape(512, MLA_HEADS, 256)[:, :, :192].reshape(512, MLA_HEADS * 192)
    g_w_ukv = t_(g_w_ukv_p.reshape(512, 2, MLA_HEADS, 128), (0, 2, 1, 3)).reshape(512, 2048)

    dhp = jnp.concatenate(
        [dc_q, dc_kv, dq_n, dk_c, dk_s, dk_w, d_krope, dv_c, dv_s, dv_w, dgl,
         jnp.zeros((s, PAD["q_mem"] - (PAD["g_nsa"] + LANE)), F32), dq_mem, dz], axis=1).astype(BF16)
    grads = dict(q_norm_g=g_q_norm, w_uq=g_w_uq, kv_norm_g=g_kv_norm,
                 w_ukv=g_w_ukv, cmp_pe_k=g_pe_k, cmp_pe_v=g_pe_v, cmp_w1k=g_w1k, cmp_w2k=g_w2k[:NSA_DK, :NSA_DK],
                 cmp_w1v=g_w1v, cmp_w2v=g_w2v, mem_norm_g=g_mem_norm, w_mem_kv=g_w_mem_kv, w_out=g_w_out,
                 final_norm_g=g_final.reshape(-1))
    if hooks is None:
        g_w_in_t = _w_in_grad_slots(_mm(dhp, xn, "in_proj_dw", mode="tn", wide=2048))
        dxn = _mm(dhp, w_in_p, "in_proj_dx", wide=2048)
    else:
        g_w_in_p, *hooks.received = _mm(dhp, xn, "in_proj_dw", mode="tn", wide=2048, side=hooks.reduce_side(grads))
        g_w_in_t = _w_in_grad_slots(g_w_in_p)
        dxn, hooks.received_w_in = _mm(dhp, w_in_p, "in_proj_dx", wide=2048, side=hooks.reduce_side_w_in(g_w_in_t))
    grad_x, g_norm = _rms_bwd(_Src(x, D_MODEL), w["norm_g"], rstd_x, dxn, dy, "norm_x_bwd")
    grads.update(norm_g=g_norm, w_in_t=g_w_in_t)
    return loss[0, 0], grad_x, grads


def kernel(x, mem, norm_g, w_in, q_norm_g, w_uq, kv_norm_g, w_ukv, cmp_pe_k, cmp_pe_v, cmp_w1k, cmp_w2k, cmp_w1v, cmp_w2v, mem_norm_g, w_mem_kv, w_out, final_norm_g, loss_target, m_norm_g, m_w_in, m_q_norm_g, m_w_uq, m_kv_norm_g, m_w_ukv, m_cmp_pe_k, m_cmp_pe_v, m_cmp_w1k, m_cmp_w2k, m_cmp_w1v, m_cmp_w2v, m_mem_norm_g, m_w_mem_kv, m_w_out, m_final_norm_g, v_norm_g, v_w_in, v_q_norm_g, v_w_uq, v_kv_norm_g, v_w_ukv, v_cmp_pe_k, v_cmp_pe_v, v_cmp_w1k, v_cmp_w2k, v_cmp_w1v, v_cmp_w2v, v_mem_norm_g, v_w_mem_kv, v_w_out, v_final_norm_g):
    args = dict(locals())
    wts = {n: args[n] for n in WEIGHTS}
    loc = {n: (a if n == "final_norm_g" else a[0]) for n, a in wts.items()}

    def to_x(n, a):
        return a.T if n == "w_in" else a

    split = [1 if n == "w_in" else 0 for n in SHARDED]
    rest = [n for n in SHARDED if n != "w_in"]
    chip = 2 * lax.axis_index("x") + lax.axis_index("y")
    core = lax.axis_index("c").astype(jnp.int32).reshape(1)
    own = {n: to_x(n, loc[n]).astype(BF16) for n in SHARDED}

    def with_own_slot(gw, a):
        return lax.dynamic_update_slice(gw, a[None], (chip, 0, 0))

    def slots(n, a):
        if n == "w_in":
            return a
        if SHARD_AXIS[n] == 0:
            return a.reshape(4, a.shape[0] // 4, a.shape[1])
        width = a.shape[1] // 4
        return jnp.stack([a[:, j * width:(j + 1) * width] for j in range(4)])

    def pair_sums(names, grads, name):
        axes = [1 if n == "w_in" else 0 for n in names]
        gs = [slots(n, a) for n, a in zip(names, grads)]
        theirs = _pair_exchange(gs, axes, name)
        return [_pair_sum(a, b, core, ax, "pair_sum_" + n) for n, a, b, ax in zip(names, gs, theirs, axes)]

    class Hooks:
        gather_side = _gather_side([own[n] for n in rest], [0] * len(rest))
        received = None

        @staticmethod
        def weights(gathered):
            out = {}
            for n, gw in zip(rest, gathered):
                gw = with_own_slot(gw, own[n])
                if SHARD_AXIS[n] == 0:
                    out[n] = gw.reshape(4 * gw.shape[1], gw.shape[2])
                else:
                    out[n] = jnp.concatenate([gw[j] for j in range(4)], axis=1)
            return out

        @staticmethod
        def reduce_side(grads):
            return _chip_side(pair_sums(rest, [grads[n] for n in rest], "pair_exchange_rest"))

        @staticmethod
        def reduce_side_w_in(g_w_in_t):
            return _chip_side(pair_sums(["w_in"], [g_w_in_t], "pair_exchange_w_in"))

    hooks = Hooks()

    start = {n: loc[n].reshape(1, -1) if loc[n].ndim == 1 else loc[n] for n in REPLICATED}
    start["w_in_t"] = with_own_slot(_gather_shards([own["w_in"]], [1])[0], own["w_in"])
    loss, grad_x, g = _local_step(x[0], mem[0], loss_target[0], start, hooks)
    loss = lax.psum(loss, ("x", "y", "c"))

    from_chips = dict(zip(rest, hooks.received), w_in=hooks.received_w_in)
    mine = [_sum_slots(from_chips[n], "chip_sum_" + n) for n in SHARDED]
    g_sh = [lax.dynamic_update_slice(o, t, (core[0] * t.shape[0], 0) if ax == 0 else (0, core[0] * t.shape[1]))
            for o, t, ax in zip(_half_exchange(mine, split), mine, split)]

    n_rep = sum(int(np.prod(loc[n].shape)) for n in REPLICATED)
    rows_rep = -(-n_rep // (8 * LANE)) * 8

    def rep_pack(parts):
        flat = jnp.concatenate([p.reshape(-1) for p in parts])
        return jnp.pad(flat, (0, rows_rep * LANE - n_rep)).reshape(rows_rep, LANE)

    g_rep = _sum_slots(_gather_all(rep_pack([g[n] for n in REPLICATED])), "replica_sum")
    d_rp, m_rp, v_rp = _adamw(rep_pack([wts[n] for n in REPLICATED]), g_rep,
                              rep_pack([args["m_" + n] for n in REPLICATED]),
                              rep_pack([args["v_" + n] for n in REPLICATED]), "adamw_replicated")

    def rep_unpack(buf):
        flat, out, o = buf.reshape(-1), {}, 0
        for n in REPLICATED:
            size = int(np.prod(wts[n].shape))
            out[n] = flat[o:o + size].reshape(wts[n].shape)
            o += size
        return out

    outs = {k: rep_unpack(b) for k, b in (("g", g_rep), ("d", d_rp), ("m", m_rp), ("v", v_rp))}
    for n, gn in zip(SHARDED, g_sh):
        d, mo, vo = _adamw(to_x(n, loc[n]), gn, to_x(n, args["m_" + n][0]), to_x(n, args["v_" + n][0]),
                           "adamw_" + n)
        for k, a in (("g", gn), ("d", d), ("m", mo), ("v", vo)):
            outs[k][n] = to_x(n, a).reshape(wts[n].shape)

    return (loss, grad_x[None], *[outs["g"][n] for n in WEIGHTS], *[outs["d"][n] for n in WEIGHTS],
            *[outs["m"][n] for n in WEIGHTS], *[outs["v"][n] for n in WEIGHTS])
```

```python
from typing import NamedTuple

import numpy as np
import jax
import jax.numpy as jnp
from jax import lax
from jax.experimental import pallas as pl
from jax.experimental.pallas import tpu as pltpu

F32 = jnp.float32
BF16 = jnp.bfloat16
MESH = pl.DeviceIdType.MESH

D_MODEL = 2048
EPS = 1e-6
LANE = 128
HEAD_V = 128
MLA_HEADS = 8
NSA_HEADS = 4
MEM_HEADS = 4
NSA_DK = 192
CMP_STRIDE = 16
CMP_LEN = 32
SLC_LEN = 64
SLC_TOPN = 16
WIN = 512
NEG = -1e30
LOG2E = 1.4426950408889634
ROPE_THETA = 10000.0
BLOCK_BYTES = 2 << 20

ORIG = dict(c_q=(0, 512), c_kv=(512, 512), k_rope=(1024, 64), z_mla=(1088, 1024),
            q_nsa=(2112, 768), k_c=(2880, 192), v_c=(3072, 128), k_s=(3200, 192),
            v_s=(3392, 128), k_w=(3520, 192), v_w=(3712, 128), g_nsa=(3840, 12),
            z_nsa=(3852, 512), q_mem=(4364, 512), z_mem=(4876, 512))
PAD = dict(c_q=0, c_kv=512, q_nsa=1024, k_c=2048, k_s=2304, k_w=2560, k_rope=2816, v_c=2944,
           v_s=3072, v_w=3200, g_nsa=3328, q_mem=3584, z=4096)
D_PAD = 6144

ADAM_LR, ADAM_B1, ADAM_B2, ADAM_EPS, ADAM_WD, ADAM_STEP = 0.001, 0.9, 0.999, 1e-08, 0.01, 10

SHARDED = ("w_in", "w_uq", "w_ukv", "cmp_w1k", "cmp_w1v", "w_mem_kv", "w_out")
SHARD_AXIS = dict(w_in=1, w_uq=1, w_ukv=1, cmp_w1k=0, cmp_w1v=0, w_mem_kv=0, w_out=0)
REPLICATED = ("norm_g", "q_norm_g", "kv_norm_g", "cmp_pe_k", "cmp_pe_v", "cmp_w2k", "cmp_w2v",
              "mem_norm_g", "final_norm_g")
WEIGHTS = ("norm_g", "w_in", "q_norm_g", "w_uq", "kv_norm_g", "w_ukv", "cmp_pe_k", "cmp_pe_v",
           "cmp_w1k", "cmp_w2k", "cmp_w1v", "cmp_w2v", "mem_norm_g", "w_mem_kv", "w_out",
           "final_norm_g")


def _pcall(kernel, **kw):
    return pl.pallas_call(kernel, **kw)


def _tile(n, pref):
    if n <= pref:
        return n
    for t in range(pref, LANE - 1, -LANE):
        if n % t == 0:
            return t
    raise ValueError((n, pref))


def _row_tile(rows, cols, itemsize=4):
    want = max(16, BLOCK_BYTES // (cols * itemsize))
    if rows <= want:
        return rows
    t = 16
    best = rows
    while t <= want:
        if rows % t == 0:
            best = t
        t *= 2
    return best


def _nt(a, b):
    return lax.dot_general(a, b, (((1,), (1,)), ((), ())), preferred_element_type=F32)


def _tn(a, b):
    return lax.dot_general(a, b, (((0,), (0,)), ((), ())), preferred_element_type=F32)


def _nn(a, b):
    return jnp.dot(a, b, preferred_element_type=F32)


def _sigmoid(x):
    return 1.0 / (1.0 + jnp.exp(-x))


class _Src(NamedTuple):
    arr: jax.Array
    width: int
    col0: int = 0
    per_head: bool = True

    def col(self, h):
        return self.col0 + h if self.per_head else self.col0


class _Side(NamedTuple):
    inputs: list
    out_shape: list
    scratch: list
    phase: object


def _mm(a, b, name, mode="nn", out_dtype=F32, second_dtype=None, wide=1024, side=None):
    if mode == "tn":
        k, m = a.shape
    else:
        m, k = a.shape
    if mode == "nt":
        n, k2 = b.shape
    else:
        k2, n = b.shape
    assert k == k2, (a.shape, b.shape, mode)
    tm, tn, tk = _tile(m, 1024), _tile(n, wide), _tile(k, 2048)
    grid = (m // tm, n // tn, k // tk)
    nk = grid[2]
    assert nk == 1 or (out_dtype == F32 and second_dtype is None)
    dot = {"nn": _nn, "nt": _nt, "tn": _tn}[mode]
    n_in = len(side.inputs) if side else 0
    n_out = len(side.out_shape) if side else 0
    n_res = 1 + (second_dtype is not None)

    def kern(*refs):
        a_ref, b_ref = refs[:2]
        res = refs[2 + n_in:2 + n_in + n_res]
        step = [pl.program_id(d) for d in range(3)]
        if side:
            side_refs = (refs[2:2 + n_in], refs[2 + n_in + n_res:2 + n_in + n_res + n_out],
                         refs[2 + n_in + n_res + n_out:])

            @pl.when((step[0] == 0) & (step[1] == 0) & (step[2] == 0))
            def _():
                side.phase("start", *side_refs)

        r = dot(a_ref[...].astype(BF16), b_ref[...].astype(BF16))
        if nk == 1:
            res[0][...] = r.astype(out_dtype)
            if n_res == 2:
                res[1][...] = r.astype(second_dtype)
        else:
            @pl.when(step[2] == 0)
            def _():
                res[0][...] = r

            @pl.when(step[2] > 0)
            def _():
                res[0][...] += r

        if side:
            @pl.when((step[0] == grid[0] - 1) & (step[1] == grid[1] - 1) & (step[2] == nk - 1))
            def _():
                side.phase("finish", *side_refs)

    a_spec = (pl.BlockSpec((tk, tm), lambda i, j, kk: (kk, i)) if mode == "tn"
              else pl.BlockSpec((tm, tk), lambda i, j, kk: (i, kk)))
    b_spec = (pl.BlockSpec((tn, tk), lambda i, j, kk: (j, kk)) if mode == "nt"
              else pl.BlockSpec((tk, tn), lambda i, j, kk: (kk, j)))
    o_spec = pl.BlockSpec((tm, tn), lambda i, j, kk: (i, j))
    out_specs = [o_spec] * n_res + [ANY] * n_out
    out_shape = [jax.ShapeDtypeStruct((m, n), out_dtype)]
    if second_dtype is not None:
        out_shape.append(jax.ShapeDtypeStruct((m, n), second_dtype))
    out_shape += list(side.out_shape) if side else []
    semantics = ("arbitrary",) * 3 if side else ("parallel", "parallel", "arbitrary")
    out = _pcall(
        kern, name=name, grid=grid, in_specs=[a_spec, b_spec] + [ANY] * n_in, out_specs=out_specs,
        out_shape=out_shape, scratch_shapes=list(side.scratch) if side else [],
        compiler_params=pltpu.CompilerParams(dimension_semantics=semantics),
    )(a, b, *(side.inputs if side else []))
    return out[0] if len(out) == 1 else out


def _rms_fwd(x, g, name):
    r, d = x.arr.shape[0], x.width
    tr = _tile(r, 512)

    def kern(x_ref, g_ref, y_ref, r_ref):
        xv = x_ref[...]
        rstd = lax.rsqrt(jnp.mean(xv * xv, axis=-1, keepdims=True) + EPS)
        y_ref[...] = (xv * rstd * g_ref[...]).astype(BF16)
        r_ref[...] = rstd

    return _pcall(
        kern, name=name, grid=(r // tr,),
        in_specs=[pl.BlockSpec((tr, d), lambda i: (i, x.col0)), pl.BlockSpec((1, d), lambda i: (0, 0))],
        out_specs=[pl.BlockSpec((tr, d), lambda i: (i, 0)), pl.BlockSpec((tr, 1), lambda i: (i, 0))],
        out_shape=[jax.ShapeDtypeStruct((r, d), BF16), jax.ShapeDtypeStruct((r, 1), F32)],
    )(x.arr, g)


def _rms_bwd(x, g, rstd, dy, add, name):
    r, d = x.arr.shape[0], x.width
    tr = _tile(r, 256)
    has_add = add is not None

    def kern(*refs):
        if has_add:
            x_ref, g_ref, r_ref, dy_ref, add_ref, dx_ref, dg_ref = refs
        else:
            x_ref, g_ref, r_ref, dy_ref, dx_ref, dg_ref = refs
        rs = r_ref[...]
        xhat = x_ref[...] * rs
        dyv = dy_ref[...]
        dyg = dyv * g_ref[...]
        c = jnp.mean(dyg * xhat, axis=-1, keepdims=True)
        dx = rs * (dyg - xhat * c)
        if has_add:
            dx = dx + add_ref[...]
        dx_ref[...] = dx
        part = jnp.sum(dyv * xhat, axis=0, keepdims=True)

        @pl.when(pl.program_id(0) == 0)
        def _():
            dg_ref[...] = part

        @pl.when(pl.program_id(0) > 0)
        def _():
            dg_ref[...] += part

    row = pl.BlockSpec((tr, d), lambda i: (i, 0))
    vec = pl.BlockSpec((1, d), lambda i: (0, 0))
    ins = [pl.BlockSpec((tr, d), lambda i: (i, x.col0)), vec, pl.BlockSpec((tr, 1), lambda i: (i, 0)), row]
    ins += [row] if has_add else []
    args = (x.arr, g, rstd, dy) + ((add,) if has_add else ())
    return _pcall(
        kern, name=name, grid=(r // tr,), in_specs=ins, out_specs=[row, vec],
        out_shape=[jax.ShapeDtypeStruct((r, d), F32), jax.ShapeDtypeStruct((1, d), F32)],
        compiler_params=pltpu.CompilerParams(dimension_semantics=("arbitrary",)),
    )(*args)


def _final_loss(x, proj, g, target):
    r, d = x.shape
    tr = _tile(r, 256)

    def kern(x_ref, p_ref, g_ref, t_ref, dy_ref, dg_ref, loss_ref):
        y = x_ref[...] + p_ref[...]
        rs = lax.rsqrt(jnp.mean(y * y, axis=-1, keepdims=True) + EPS)
        yhat = y * rs
        gv = g_ref[...]
        e = yhat * gv - t_ref[...]
        lpart = 0.5 * jnp.sum(jnp.mean(e * e, axis=-1, keepdims=True), axis=0, keepdims=True)
        dout = e * (1.0 / d)
        dyg = dout * gv
        c = jnp.mean(dyg * yhat, axis=-1, keepdims=True)
        dy_ref[...] = rs * (dyg - yhat * c)
        gpart = jnp.sum(dout * yhat, axis=0, keepdims=True)
        lrow = jnp.broadcast_to(lpart, (1, LANE))

        @pl.when(pl.program_id(0) == 0)
        def _():
            dg_ref[...] = gpart
            loss_ref[...] = lrow

        @pl.when(pl.program_id(0) > 0)
        def _():
            dg_ref[...] += gpart
            loss_ref[...] += lrow

    row = pl.BlockSpec((tr, d), lambda i: (i, 0))
    vec = pl.BlockSpec((1, d), lambda i: (0, 0))
    return _pcall(
        kern, name="final_loss", grid=(r // tr,), in_specs=[row, row, vec, row],
        out_specs=[row, vec, pl.BlockSpec((1, LANE), lambda i: (0, 0))],
        out_shape=[jax.ShapeDtypeStruct((r, d), F32), jax.ShapeDtypeStruct((1, d), F32),
                   jax.ShapeDtypeStruct((1, LANE), F32)],
        compiler_params=pltpu.CompilerParams(dimension_semantics=("arbitrary",)),
    )(x, proj, g, target)


def _rope_fwd(x, cs, sn, nh, width, off, name):
    s = x.arr.shape[0]
    tr = _tile(s, 512)

    def kern(x_ref, c_ref, s_ref, o_ref):
        cv, sv = c_ref[...], s_ref[...]
        for h in range(nh):
            b = h * width
            if off:
                o_ref[:, b:b + off] = x_ref[:, b:b + off].astype(BF16)
            xr = x_ref[:, b + off:b + off + LANE]
            o_ref[:, b + off:b + off + LANE] = (xr * cv + pltpu.roll(xr, 32, 1) * sv).astype(BF16)

    tab = pl.BlockSpec((tr, LANE), lambda i: (i, 0))
    return _pcall(
        kern, name=name, grid=(s // tr,),
        in_specs=[pl.BlockSpec((tr, nh * width), lambda i: (i, x.col0)), tab, tab],
        out_specs=pl.BlockSpec((tr, nh * width), lambda i: (i, 0)),
        out_shape=jax.ShapeDtypeStruct((s, nh * width), BF16),
    )(x.arr, cs, sn)


def _rope_grad(d, cv, sv):
    g2 = d * sv
    g2 = g2 + pltpu.roll(g2, 64, 1)
    lane = lax.broadcasted_iota(jnp.int32, d.shape, 1)
    return jnp.where(lane < 64, d * cv + pltpu.roll(g2, 32, 1), 0.0)


def _rope_bwd_q(dq, cs, sn):
    s, w = dq.shape
    tr = _tile(s, 512)
    nh = w // 256

    def kern(d_ref, c_ref, s_ref, o_ref):
        cv, sv = c_ref[...], s_ref[...]
        for h in range(nh):
            b = h * 256
            o_ref[:, b:b + LANE] = d_ref[:, b:b + LANE]
            o_ref[:, b + LANE:b + 256] = _rope_grad(d_ref[:, b + LANE:b + 256], cv, sv)

    row = pl.BlockSpec((tr, w), lambda i: (i, 0))
    tab = pl.BlockSpec((tr, LANE), lambda i: (i, 0))
    return _pcall(kern, name="rope_bwd_q", grid=(s // tr,), in_specs=[row, tab, tab], out_specs=row,
                  out_shape=jax.ShapeDtypeStruct((s, w), F32))(dq, cs, sn)


def _rope_bwd_k(dk_nope, dk_pe, dv, cs, sn):
    s, w = dk_nope.shape
    tr = _tile(s, 512)

    def kern(dk_ref, dp_ref, dv_ref, c_ref, s_ref, okv_ref, okr_ref):
        okv_ref[:, :w] = dk_ref[...]
        okv_ref[:, w:] = dv_ref[...]
        okr_ref[...] = _rope_grad(dp_ref[...], c_ref[...], s_ref[...])

    tab = pl.BlockSpec((tr, LANE), lambda i: (i, 0))
    wide = pl.BlockSpec((tr, w), lambda i: (i, 0))
    return _pcall(
        kern, name="rope_bwd_k", grid=(s // tr,), in_specs=[wide, tab, wide, tab, tab],
        out_specs=[pl.BlockSpec((tr, 2 * w), lambda i: (i, 0)), tab],
        out_shape=[jax.ShapeDtypeStruct((s, 2 * w), F32), jax.ShapeDtypeStruct((s, LANE), F32)],
    )(dk_nope, dk_pe, dv, cs, sn)


class _Attn:
    def __init__(self, mode, s, sk, heads, dk):
        self.mode, self.s, self.sk, self.h, self.dk = mode, s, sk, heads, dk
        self.scale = {"mla": 192 ** -0.5, "mem": 128 ** -0.5}.get(mode, NSA_DK ** -0.5)
        self.tb = min(256, s)
        self.nb = s // self.tb
        self.nsub = 2 if self.nb % 2 == 0 else 1
        self.tq = self.tb * self.nsub
        self.nq = s // self.tq
        self.causal = mode in ("mla", "slc")
        if self.causal:
            self.tk = self.tq
        elif mode == "win":
            self.tk = WIN + self.tb
        else:
            self.tk = sk
        self.tkb = min(512, sk)
        self.ksub = 2 if self.tkb == 512 and mode == "mla" else 1
        self.kb = self.tkb // self.ksub
        self.ncmp = s // CMP_STRIDE - 1

    def mask_bias(self, t, n, h, selx, diag):
        m = self.mode
        if m == "mla":
            return (n <= t) if diag else None, None
        if m == "mem":
            return None, None
        slope = jnp.where(h == 0, 0.25, jnp.where(h == 1, 0.0625, jnp.where(h == 2, 0.015625, 0.00390625)))
        slope = slope.astype(F32) * LOG2E
        if m == "cmp":
            mask = (n * CMP_STRIDE + (CMP_LEN - 1) <= t) & (n < self.ncmp)
            pos = n.astype(F32) * float(CMP_STRIDE) + (CMP_LEN - 1) / 2.0
            return mask, slope * pos
        rel = t - n
        if m == "slc":
            return (rel >= 0) if diag else None, slope * n.astype(F32)
        return (rel >= 0) & (rel < WIN), slope * n.astype(F32)


def _scores(cfg, s_raw, t, n, h, selx, diag, lse=None):
    s = s_raw * (cfg.scale * LOG2E)
    mask, key_term = cfg.mask_bias(t, n, h, selx, diag)
    if key_term is not None:
        s = s + key_term
    if selx is not None:
        s = s + selx
    if lse is None:
        if mask is not None:
            s = jnp.where(mask, s, NEG)
        return s, mask
    p = jnp.exp2(jnp.minimum(s - lse, 0.0))
    if mask is not None:
        p = jnp.where(mask, p, 0.0)
    return p, mask


def _block_of_key(k0, tk, keys_on_rows, value=NEG):
    shape = (tk, LANE) if keys_on_rows else (LANE, tk)
    n = lax.broadcasted_iota(jnp.int32, shape, 0 if keys_on_rows else 1) + k0
    j = lax.broadcasted_iota(jnp.int32, shape, 1 if keys_on_rows else 0)
    return jnp.where((n >> 6) == j, value, 0.0).astype(BF16)


def _to_row(col):
    t = col.shape[0]
    return jnp.transpose(jnp.broadcast_to(col, (t, LANE)))[0:1, :]


def _load_keys(k_refs, rows):
    parts = [r[rows, :].astype(BF16) for r in k_refs]
    return parts[0] if len(parts) == 1 else jnp.concatenate(parts, axis=1)


def _attn_fwd(cfg, q, ks, v, sel, name):
    s, tq, tk, tb, nsub = cfg.s, cfg.tq, cfg.tk, cfg.tb, cfg.nsub
    has_sel = sel is not None
    nkp = len(ks)
    assert cfg.causal and tq == tk and cfg.nq % 2 == 0

    def kern(*refs):
        q_ref, k_refs, v_ref = refs[0], refs[1:1 + nkp], refs[1 + nkp]
        sel_ref = refs[2 + nkp] if has_sel else None
        o_ref, lc_ref, lr_ref = refs[2 + nkp + has_sel:5 + nkp + has_sel]
        buf_a, buf_b = refs[-2:]
        h, g = pl.program_id(0), pl.program_id(1)

        def block(b):
            rows = [slice(b * tq + r * tb, b * tq + (r + 1) * tb) for r in range(nsub)]
            qs = [q_ref[p, :].astype(BF16) for p in rows]
            ts = [(2 * g + b) * tq + r * tb + lax.broadcasted_iota(jnp.int32, (tb, 1), 0) for r in range(nsub)]
            sels = [sel_ref[p, :].astype(BF16) for p in rows] if has_sel else None
            return rows, qs, ts, sels

        def scores_into(buf, blk, c):
            kk = _load_keys(k_refs, pl.ds(pl.multiple_of(c * tk, tk), tk))
            for r in range(nsub):
                buf[r] = _nt(blk[1][r], kk)

        def consume(buf, blk, c, carry, diag):
            _, _, ts, sels = blk
            k0 = pl.multiple_of(c * tk, tk)
            vv = v_ref[pl.ds(k0, tk), :].astype(BF16)
            emat = _block_of_key(k0, tk, False) if has_sel else None
            n = k0 + lax.broadcasted_iota(jnp.int32, (1, tk), 1)
            new = []
            for r in range(nsub):
                m, l, acc = carry[r]
                selx = _nn(sels[r], emat) if has_sel else None
                sc, mask = _scores(cfg, buf[r], ts[r], n, h, selx, diag)
                m_new = jnp.maximum(m, jnp.max(sc, axis=1, keepdims=True))
                alpha = jnp.exp2(m - m_new)
                p = jnp.exp2(sc - m_new)
                if mask is not None:
                    p = jnp.where(mask, p, 0.0)
                l = alpha * l + jnp.sum(p, axis=1, keepdims=True)
                new.append((m_new, l, alpha * acc + _nn(p.astype(BF16), vv)))
            return tuple(new)

        def finish(blk, b, carry):
            for r, (m, l, acc) in enumerate(carry):
                o_ref[blk[0][r], :] = acc / (l + 1e-20)
                lse = m + jnp.log(l + 1e-20) * LOG2E
                lc_ref[0, blk[0][r], :] = lse
                lr_ref[0, b * nsub + r] = _to_row(lse)

        def pairs(blk, first, other):
            def pair(p, cr):
                scores_into(other, blk, 2 * p + 1)
                cr = consume(first, blk, 2 * p, cr, False)
                scores_into(first, blk, 2 * p + 2)
                return consume(other, blk, 2 * p + 1, cr, False)
            return pair

        init = ((jnp.full((tb, 1), NEG, F32), jnp.zeros((tb, 1), F32), jnp.zeros((tb, HEAD_V), F32)),) * nsub
        blk_a, blk_b = block(0), block(1)
        scores_into(buf_a, blk_a, 0)
        carry = lax.fori_loop(0, g, pairs(blk_a, buf_a, buf_b), init)
        scores_into(buf_b, blk_b, 0)
        finish(blk_a, 0, consume(buf_a, blk_a, 2 * g, carry, True))
        carry = lax.fori_loop(0, g, pairs(blk_b, buf_b, buf_a), init)
        scores_into(buf_a, blk_b, 2 * g + 1)
        carry = consume(buf_b, blk_b, 2 * g, carry, False)
        finish(blk_b, 1, consume(buf_a, blk_b, 2 * g + 1, carry, True))

    ins = [pl.BlockSpec((2 * tq, q.width), lambda h, g: (g, q.col(h)))]
    ins += [pl.BlockSpec((cfg.sk, p.width), lambda h, g, p=p: (0, p.col(h))) for p in ks]
    ins += [pl.BlockSpec((cfg.sk, HEAD_V), lambda h, g: (0, v.col(h)))]
    args = [q.arr] + [p.arr for p in ks] + [v.arr]
    if has_sel:
        ins.append(pl.BlockSpec((2 * tq, LANE), lambda h, g: (g, 0)))
        args.append(sel)
    return _pcall(
        kern, name=name, grid=(cfg.h, cfg.nq // 2), in_specs=ins,
        out_specs=[pl.BlockSpec((2 * tq, HEAD_V), lambda h, g: (g, h)),
                   pl.BlockSpec((1, 2 * tq, 1), lambda h, g: (h, g, 0)),
                   pl.BlockSpec((1, 2 * nsub, 1, tb), lambda h, g: (h, g, 0, 0))],
        out_shape=[jax.ShapeDtypeStruct((s, cfg.h * HEAD_V), F32),
                   jax.ShapeDtypeStruct((cfg.h, s, 1), F32),
                   jax.ShapeDtypeStruct((cfg.h, cfg.nb, 1, tb), F32)],
        scratch_shapes=[pltpu.VMEM((nsub, tb, tk), F32)] * 2,
        compiler_params=pltpu.CompilerParams(dimension_semantics=("parallel", "parallel")),
    )(*args)


def _attn_dq(cfg, q, ks, v, sel, o, lse, do, dq_in, name):
    s, tq, tk, dk, tb, nsub = cfg.s, cfg.tq, cfg.tk, cfg.dk, cfg.tb, cfg.nsub
    has_sel = sel is not None
    has_in = dq_in is not None
    nkp = len(ks)

    def kern(*refs):
        refs = list(refs)
        q_ref, k_refs, v_ref = refs[0], refs[1:1 + nkp], refs[1 + nkp]
        p0 = 2 + nkp
        sel_ref = refs[p0] if has_sel else None
        p0 += has_sel
        o_ref, l_ref, do_ref = refs[p0:p0 + 3]
        p0 += 3
        in_ref = refs[p0] if has_in else None
        p0 += has_in
        dq_ref, dr_ref = refs[p0:p0 + 2]
        sa, pa, sb, pb = refs[-4:]
        h, g = pl.program_id(0), pl.program_id(1)

        def block(b):
            rows = [slice(b * tq + r * tb, b * tq + (r + 1) * tb) for r in range(nsub)]
            qs = [q_ref[p, :].astype(BF16) for p in rows]
            ts = [(2 * g + b) * tq + r * tb + lax.broadcasted_iota(jnp.int32, (tb, 1), 0) for r in range(nsub)]
            sels = [sel_ref[p, :].astype(BF16) for p in rows] if has_sel else None
            dvecs, dobs, lses = [], [], []
            for r, p in enumerate(rows):
                dov = do_ref[p, :]
                dvec = jnp.sum(dov * o_ref[p, :], axis=1, keepdims=True)
                dr_ref[0, b * nsub + r] = _to_row(dvec)
                dvecs.append(dvec)
                dobs.append(dov.astype(BF16))
                lses.append(l_ref[0, p, :])
            return rows, qs, ts, sels, dvecs, dobs, lses

        def products_into(sbuf, pbuf, blk, c):
            rows = pl.ds(pl.multiple_of(c * tk, tk), tk)
            kk, vv = _load_keys(k_refs, rows), v_ref[rows, :].astype(BF16)
            for r in range(nsub):
                sbuf[r] = _nt(blk[1][r], kk)
                pbuf[r] = _nt(blk[5][r], vv)

        def consume(sbuf, pbuf, blk, c, accs, diag):
            _, _, ts, sels, dvecs, _, lses = blk
            k0 = pl.multiple_of(c * tk, tk)
            kk = _load_keys(k_refs, pl.ds(k0, tk))
            emat = _block_of_key(k0, tk, False) if has_sel else None
            n = k0 + lax.broadcasted_iota(jnp.int32, (1, tk), 1)
            new = []
            for r in range(nsub):
                selx = _nn(sels[r], emat) if has_sel else None
                p, _ = _scores(cfg, sbuf[r], ts[r], n, h, selx, diag, lses[r])
                ds = p * (pbuf[r] - dvecs[r])
                new.append(accs[r] + _nn(ds.astype(BF16), kk))
            return tuple(new)

        def finish(blk, accs):
            for r, p in enumerate(blk[0]):
                dq_ref[p, :] = accs[r] * cfg.scale + in_ref[p, :] if has_in else accs[r] * cfg.scale

        def pairs(blk, first, other):
            def pair(p, ac):
                products_into(*other, blk, 2 * p + 1)
                ac = consume(*first, blk, 2 * p, ac, False)
                products_into(*first, blk, 2 * p + 2)
                return consume(*other, blk, 2 * p + 1, ac, False)
            return pair

        zero = (jnp.zeros((tb, dk), F32),) * nsub
        buf_a, buf_b = (sa, pa), (sb, pb)
        blk_a, blk_b = block(0), block(1)
        products_into(*buf_a, blk_a, 0)
        accs = lax.fori_loop(0, g, pairs(blk_a, buf_a, buf_b), zero)
        products_into(*buf_b, blk_b, 0)
        finish(blk_a, consume(*buf_a, blk_a, 2 * g, accs, True))
        accs = lax.fori_loop(0, g, pairs(blk_b, buf_b, buf_a), zero)
        products_into(*buf_a, blk_b, 2 * g + 1)
        accs = consume(*buf_b, blk_b, 2 * g, accs, False)
        finish(blk_b, consume(*buf_a, blk_b, 2 * g + 1, accs, True))

    assert cfg.causal and tq == tk and cfg.nq % 2 == 0
    qs = pl.BlockSpec((2 * tq, dk), lambda h, g: (g, h))
    ins = [pl.BlockSpec((2 * tq, q.width), lambda h, g: (g, q.col(h)))]
    ins += [pl.BlockSpec((cfg.sk, p.width), lambda h, g, p=p: (0, p.col(h))) for p in ks]
    ins += [pl.BlockSpec((cfg.sk, HEAD_V), lambda h, g: (0, v.col(h)))]
    args = [q.arr] + [p.arr for p in ks] + [v.arr]
    if has_sel:
        ins.append(pl.BlockSpec((2 * tq, LANE), lambda h, g: (g, 0)))
        args.append(sel)
    ins += [pl.BlockSpec((2 * tq, HEAD_V), lambda h, g: (g, o.col(h))),
            pl.BlockSpec((1, 2 * tq, 1), lambda h, g: (h, g, 0)),
            pl.BlockSpec((2 * tq, HEAD_V), lambda h, g: (g, do.col(h)))]
    args += [o.arr, lse, do.arr]
    if has_in:
        ins.append(qs)
        args.append(dq_in)
    return _pcall(
        kern, name=name, grid=(cfg.h, cfg.nq // 2), in_specs=ins,
        out_specs=[qs, pl.BlockSpec((1, 2 * nsub, 1, tb), lambda h, g: (h, g, 0, 0))],
        out_shape=[jax.ShapeDtypeStruct((s, cfg.h * dk), F32),
                   jax.ShapeDtypeStruct((cfg.h, cfg.nb, 1, tb), F32)],
        scratch_shapes=[pltpu.VMEM((nsub, tb, tk), F32)] * 4,
        compiler_params=pltpu.CompilerParams(dimension_semantics=("parallel", "parallel")),
    )(*args)


def _attn_dkv(cfg, q, ks, v, selt, lse_r, d_r, do, name):
    s, tq, tkb, dk, kb, ksub = cfg.s, cfg.tb, cfg.tkb, cfg.dk, cfg.kb, cfg.ksub
    nq = cfg.nb
    has_sel = selt is not None
    nkp = len(ks)
    outs = list(ks) + [v]

    def kern(*refs):
        k_refs, v_ref = refs[:nkp], refs[nkp]
        q_ref, do_ref, lr_ref, dr_ref = refs[nkp + 1:nkp + 5]
        st_ref = refs[nkp + 5] if has_sel else None
        out_refs = refs[nkp + 5 + has_sel:2 * nkp + 6 + has_sel]
        sa, pa, sb, pb = refs[-4:]
        j, h = pl.program_id(0), pl.program_id(1)
        k0 = j * tkb
        part = [slice(u * kb, (u + 1) * kb) for u in range(ksub)]
        kks = [_load_keys(k_refs, p) for p in part]
        vvs = [v_ref[p, :].astype(BF16) for p in part]
        ns = [k0 + u * kb + lax.broadcasted_iota(jnp.int32, (kb, 1), 0) for u in range(ksub)]
        emats = [_block_of_key(k0 + u * kb, kb, True) for u in range(ksub)] if has_sel else None

        def load_q(i):
            rows = pl.ds(pl.multiple_of(i * tq, tq), tq)
            return q_ref[rows, :].astype(BF16), do_ref[rows, :].astype(BF16)

        def products_into(sbuf, pbuf, i):
            qi, doi = load_q(i)
            for u in range(ksub):
                sbuf[u] = _nt(kks[u], qi)
                pbuf[u] = _nt(vvs[u], doi)

        def consume(sbuf, pbuf, i, carry):
            qi, doi = load_q(i)
            t = i * tq + lax.broadcasted_iota(jnp.int32, (1, tq), 1)
            selt_i = st_ref[i].astype(BF16) if has_sel else None
            new = []
            for u in range(ksub):
                dk_acc, dv_acc = carry[u]
                selx = _nn(emats[u], selt_i) if has_sel else None
                pt, _ = _scores(cfg, sbuf[u], t, ns[u], h, selx, True, lr_ref[0, i])
                dv_acc = dv_acc + _nn(pt.astype(BF16), doi)
                dst = pt * (pbuf[u] - dr_ref[0, i])
                new.append((dk_acc + _nn(dst.astype(BF16), qi), dv_acc))
            return tuple(new)

        if cfg.causal:
            first, count = k0 // tq, nq - k0 // tq
        elif cfg.mode == "win":
            first = k0 // tq
            count = jnp.minimum((k0 + tkb + WIN - 2) // tq + 1, nq) - first
        else:
            first, count = 0, nq

        def pair(p, cr):
            i0 = first + 2 * p
            products_into(sb, pb, i0 + 1)
            cr = consume(sa, pa, i0, cr)
            products_into(sa, pa, i0 + 2)
            return consume(sb, pb, i0 + 1, cr)

        carry = ((jnp.zeros((kb, dk), F32), jnp.zeros((kb, HEAD_V), F32)),) * ksub
        products_into(sa, pa, first)
        carry = lax.fori_loop(0, count // 2 - 1, pair, carry)
        last = first + count - 2
        products_into(sb, pb, last + 1)
        carry = consume(sa, pa, last, carry)
        carry = consume(sb, pb, last + 1, carry)
        for u, (dk_acc, dv_acc) in enumerate(carry):
            vals, off = [], 0
            for p in ks:
                vals.append(dk_acc[:, off:off + p.width] * cfg.scale)
                off += p.width
            vals.append(dv_acc)
            for src, ref, val in zip(outs, out_refs, vals):
                if src.per_head:
                    ref[part[u], :] = val
                else:
                    @pl.when(h == 0)
                    def _(ref=ref, val=val, u=u):
                        ref[part[u], :] = val

                    @pl.when(h > 0)
                    def _(ref=ref, val=val, u=u):
                        ref[part[u], :] += val

    rowv = pl.BlockSpec((1, nq, 1, tq), lambda j, h: (h, 0, 0, 0))
    ins = [pl.BlockSpec((tkb, p.width), lambda j, h, p=p: (j, p.col(h))) for p in ks]
    ins += [pl.BlockSpec((tkb, HEAD_V), lambda j, h: (j, v.col(h))),
            pl.BlockSpec((s, q.width), lambda j, h: (0, q.col(h))),
            pl.BlockSpec((s, HEAD_V), lambda j, h: (0, do.col(h))), rowv, rowv]
    args = [p.arr for p in ks] + [v.arr, q.arr, do.arr, lse_r, d_r]
    if has_sel:
        ins.append(pl.BlockSpec((nq, LANE, tq), lambda j, h: (0, 0, 0)))
        args.append(selt)
    out_specs = [pl.BlockSpec((tkb, p.width), lambda j, h, p=p: (j, h if p.per_head else 0)) for p in outs]
    out_shape = [jax.ShapeDtypeStruct((cfg.sk, (cfg.h if p.per_head else 1) * p.width), F32) for p in outs]
    assert nq % 2 == 0 and (cfg.mode in ("cmp", "mem") or tkb % (2 * tq) == 0), (nq, tkb, tq)
    return _pcall(
        kern, name=name, grid=(cfg.sk // tkb, cfg.h), in_specs=ins, out_specs=out_specs, out_shape=out_shape,
        scratch_shapes=[pltpu.VMEM((ksub, kb, tq), F32)] * 4,
        compiler_params=pltpu.CompilerParams(dimension_semantics=("parallel", "arbitrary")),
    )(*args)


def _attn_dkv_flat(cfg, q, ks, v, selt, lse_r, d_r, do, name):
    s, tq, tkb, dk, kb, ksub = cfg.s, cfg.tb, cfg.tkb, cfg.dk, cfg.kb, cfg.ksub
    nq = cfg.nb
    has_sel = selt is not None
    nkp = len(ks)
    outs = list(ks) + [v]
    assert nq % 2 == 0 and tkb % (2 * tq) == 0, (nq, tkb, tq)
    steps = []
    for j in range(cfg.sk // tkb):
        first = j * tkb // tq
        stop = nq if cfg.causal else min((j * tkb + tkb + WIN - 2) // tq + 1, nq)
        steps += [(j, i0) for i0 in range(first, stop, 2)]
    n_pairs = len(steps)
    steps.append(steps[-1])
    tab_j = jnp.asarray(np.array([p[0] for p in steps], np.int32))
    tab_i = jnp.asarray(np.array([p[1] for p in steps], np.int32))

    def kern(tj_ref, ti_ref, *refs):
        k_refs, v_ref = refs[:nkp], refs[nkp]
        q_ref, do_ref, lr_ref, dr_ref = refs[nkp + 1:nkp + 5]
        st_ref = refs[nkp + 5] if has_sel else None
        out_refs = refs[nkp + 5 + has_sel:2 * nkp + 6 + has_sel]
        sa, pa, sb, pb = refs[-4:]
        h = pl.program_id(0)
        for src, ref in zip(outs, out_refs):
            if src.per_head:
                ref[...] = jnp.zeros_like(ref)
            else:
                @pl.when(h == 0)
                def _(ref=ref):
                    ref[...] = jnp.zeros_like(ref)

        def key_rows(j, u):
            return pl.ds(pl.multiple_of(j * tkb + u * kb, kb), kb)

        def load_q(i):
            rows = pl.ds(pl.multiple_of(i * tq, tq), tq)
            return q_ref[rows, :].astype(BF16), do_ref[rows, :].astype(BF16)

        def products_into(sbuf, pbuf, j, i):
            qi, doi = load_q(i)
            for u in range(ksub):
                rows = key_rows(j, u)
                sbuf[u] = _nt(_load_keys(k_refs, rows), qi)
                pbuf[u] = _nt(v_ref[rows, :].astype(BF16), doi)

        def consume(sbuf, pbuf, j, i):
            qi, doi = load_q(i)
            t = i * tq + lax.broadcasted_iota(jnp.int32, (1, tq), 1)
            selt_i = st_ref[i].astype(BF16) if has_sel else None
            res = []
            for u in range(ksub):
                k0 = j * tkb + u * kb
                n = k0 + lax.broadcasted_iota(jnp.int32, (kb, 1), 0)
                selx = _nn(_block_of_key(k0, kb, True), selt_i) if has_sel else None
                pt, _ = _scores(cfg, sbuf[u], t, n, h, selx, True, lr_ref[0, i])
                dst = pt * (pbuf[u] - dr_ref[0, i])
                res.append((_nn(dst.astype(BF16), qi), _nn(pt.astype(BF16), doi)))
            return res

        def pair(p, carry):
            j, i0 = tj_ref[p], ti_ref[p]
            products_into(sb, pb, j, i0 + 1)
            ca = consume(sa, pa, j, i0)
            products_into(sa, pa, tj_ref[p + 1], ti_ref[p + 1])
            cb = consume(sb, pb, j, i0 + 1)
            for u in range(ksub):
                rows = key_rows(j, u)
                dk_c = (ca[u][0] + cb[u][0]) * cfg.scale
                off = 0
                for src, ref in zip(ks, out_refs):
                    ref[rows, :] += dk_c[:, off:off + src.width]
                    off += src.width
                out_refs[nkp][rows, :] += ca[u][1] + cb[u][1]
            return carry

        products_into(sa, pa, tj_ref[0], ti_ref[0])
        lax.fori_loop(0, n_pairs, pair, 0)

    rowv = pl.BlockSpec((1, nq, 1, tq), lambda h, tj, ti: (h, 0, 0, 0))
    ins = [pl.BlockSpec((cfg.sk, p.width), lambda h, tj, ti, p=p: (0, p.col(h))) for p in ks]
    ins += [pl.BlockSpec((cfg.sk, HEAD_V), lambda h, tj, ti: (0, v.col(h))),
            pl.BlockSpec((s, q.width), lambda h, tj, ti: (0, q.col(h))),
            pl.BlockSpec((s, HEAD_V), lambda h, tj, ti: (0, do.col(h))), rowv, rowv]
    args = [p.arr for p in ks] + [v.arr, q.arr, do.arr, lse_r, d_r]
    if has_sel:
        ins.append(pl.BlockSpec((nq, LANE, tq), lambda h, tj, ti: (0, 0, 0)))
        args.append(selt)
    out_specs = [pl.BlockSpec((cfg.sk, p.width), lambda h, tj, ti, p=p: (0, h if p.per_head else 0))
                 for p in outs]
    out_shape = [jax.ShapeDtypeStruct((cfg.sk, (cfg.h if p.per_head else 1) * p.width), F32) for p in outs]
    grid_spec = pltpu.PrefetchScalarGridSpec(
        num_scalar_prefetch=2, grid=(cfg.h,), in_specs=ins, out_specs=out_specs,
        scratch_shapes=[pltpu.VMEM((ksub, kb, tq), F32)] * 4)
    return _pcall(kern, name=name, grid_spec=grid_spec, out_shape=out_shape,
                  compiler_params=pltpu.CompilerParams(dimension_semantics=("arbitrary",)))(tab_j, tab_i, *args)


def _all_heads(cfg, src, rows, key=False):
    if src.per_head:
        assert src.col0 % cfg.h == 0
        width, col = cfg.h * src.width, src.col0 // cfg.h
    else:
        width, col = src.width, src.col0
    return pl.BlockSpec((rows, width), (lambda i: (0, col)) if key else (lambda i: (i, col)))


def _head_cols(src, hh):
    return slice(hh * src.width, (hh + 1) * src.width) if src.per_head else slice(None)


def _key_window(cfg, i, r):
    if cfg.mode == "win":
        return pl.ds(pl.multiple_of(jnp.maximum(i * cfg.tq + r * cfg.tb - WIN, 0), cfg.tb), cfg.tk)
    return pl.ds(0, cfg.tk)


def _attn_fwd_small(cfg, q, ks, v, name, overlap=None):
    s, tq, tk, tb, nsub, nh = cfg.s, cfg.tq, cfg.tk, cfg.tb, cfg.nsub, cfg.h
    nkp = len(ks)
    select = overlap is not None
    n_s = s // SLC_LEN
    top_n = min(SLC_TOPN, n_s)

    def kern(*refs):
        q_ref, k_refs, v_ref = refs[0], refs[1:1 + nkp], refs[1 + nkp]
        ov_ref = refs[2 + nkp] if select else None
        o_ref, lc_ref, lr_ref = refs[2 + nkp + select:5 + nkp + select]
        i = pl.program_id(0)
        imps = [jnp.zeros((tb, LANE), F32)] * nsub
        for r in range(nsub):
            rows = slice(r * tb, (r + 1) * tb)
            t = i * tq + r * tb + lax.broadcasted_iota(jnp.int32, (tb, 1), 0)
            win = _key_window(cfg, i, r)
            n = win.start + lax.broadcasted_iota(jnp.int32, (1, tk), 1)
            for hh in range(nh):
                qv = q_ref[rows, hh * cfg.dk:(hh + 1) * cfg.dk].astype(BF16)
                kk = _load_keys([kr.at[:, _head_cols(p, hh)] for kr, p in zip(k_refs, ks)], win)
                vv = v_ref[win, _head_cols(v, hh)].astype(BF16)
                sc, mask = _scores(cfg, _nt(qv, kk), t, n, hh, None, True)
                m = jnp.max(sc, axis=1, keepdims=True)
                e = jnp.exp2(sc - m)
                if mask is not None:
                    e = jnp.where(mask, e, 0.0)
                l = jnp.sum(e, axis=1, keepdims=True)
                o_ref[rows, hh * HEAD_V:(hh + 1) * HEAD_V] = _nn(e.astype(BF16), vv) / (l + 1e-20)
                lse = m + jnp.log(l + 1e-20) * LOG2E
                lc_ref[hh, rows, :] = lse
                lr_ref[hh, r] = _to_row(lse)
                if select:
                    imps[r] = imps[r] + _nn((e / (l + 1e-20)).astype(BF16), ov_ref[...])
        if select:
            sel_ref, selt_ref, imp_t = refs[5 + nkp + select:8 + nkp + select]
            for r in range(nsub):
                t = i * tq + r * tb + lax.broadcasted_iota(jnp.int32, (tb, 1), 0)
                j = lax.broadcasted_iota(jnp.int32, (tb, LANE), 1)
                cur = t >> 6
                imp = jnp.where((j == 0) | (j == cur) | (j == cur - 1), 1e9, imps[r])
                imp = jnp.where(j > cur, -1e9, imp)
                imp_t[r] = jnp.transpose(imp)
                mine = imp_t[r, 0:n_s, :]
                jrow = lax.broadcasted_iota(jnp.int32, (n_s, tb), 0)

                def count(k, rank):
                    other = imp_t[r, pl.ds(k, 1), :]
                    ahead = (other > mine) | ((other == mine) & (k < jrow))
                    return rank + jnp.where(ahead, 1.0, 0.0)

                rank = lax.fori_loop(0, n_s, count, jnp.zeros((n_s, tb), F32))
                cur_t = (i * tq + r * tb + lax.broadcasted_iota(jnp.int32, (1, tb), 1)) >> 6
                rejected = jnp.where((rank < top_n) & (jrow <= cur_t), 0.0, 1.0)
                if n_s < LANE:
                    rejected = jnp.concatenate([rejected, jnp.ones((LANE - n_s, tb), F32)], axis=0)
                selt_ref[r] = rejected
                sel_ref[r * tb:(r + 1) * tb, :] = jnp.transpose(rejected)

    ins = [_all_heads(cfg, q, tq)] + [_all_heads(cfg, p, cfg.sk, True) for p in ks]
    ins += [_all_heads(cfg, v, cfg.sk, True)]
    args = [q.arr] + [p.arr for p in ks] + [v.arr]
    out_specs = [pl.BlockSpec((tq, nh * HEAD_V), lambda i: (i, 0)),
                 pl.BlockSpec((nh, tq, 1), lambda i: (0, i, 0)),
                 pl.BlockSpec((nh, nsub, 1, tb), lambda i: (0, i, 0, 0))]
    out_shape = [jax.ShapeDtypeStruct((s, nh * HEAD_V), F32), jax.ShapeDtypeStruct((nh, s, 1), F32),
                 jax.ShapeDtypeStruct((nh, cfg.nb, 1, tb), F32)]
    scratch = []
    if select:
        ins.append(pl.BlockSpec((cfg.sk, LANE), lambda i: (0, 0)))
        args.append(overlap)
        out_specs += [pl.BlockSpec((tq, LANE), lambda i: (i, 0)), pl.BlockSpec((nsub, LANE, tb), lambda i: (i, 0, 0))]
        out_shape += [jax.ShapeDtypeStruct((s, LANE), F32), jax.ShapeDtypeStruct((cfg.nb, LANE, tb), F32)]
        scratch = [pltpu.VMEM((nsub, LANE, tb), F32)]
    return _pcall(kern, name=name, grid=(cfg.nq,), in_specs=ins, out_specs=out_specs, out_shape=out_shape,
                  scratch_shapes=scratch,
                  compiler_params=pltpu.CompilerParams(dimension_semantics=("parallel",)))(*args)


def _attn_dq_small(cfg, q, ks, v, o, lse, do, dq_in, name):
    s, tq, tk, tb, nsub, nh, dk = cfg.s, cfg.tq, cfg.tk, cfg.tb, cfg.nsub, cfg.h, cfg.dk
    nkp = len(ks)
    has_in = dq_in is not None

    def kern(*refs):
        q_ref, k_refs, v_ref = refs[0], refs[1:1 + nkp], refs[1 + nkp]
        o_ref, l_ref, do_ref = refs[2 + nkp:5 + nkp]
        in_ref = refs[5 + nkp] if has_in else None
        dq_ref, dr_ref = refs[5 + nkp + has_in:7 + nkp + has_in]
        i = pl.program_id(0)
        for r in range(nsub):
            rows = slice(r * tb, (r + 1) * tb)
            t = i * tq + r * tb + lax.broadcasted_iota(jnp.int32, (tb, 1), 0)
            win = _key_window(cfg, i, r)
            n = win.start + lax.broadcasted_iota(jnp.int32, (1, tk), 1)
            for hh in range(nh):
                vcols = slice(hh * HEAD_V, (hh + 1) * HEAD_V)
                qcols = slice(hh * dk, (hh + 1) * dk)
                qv = q_ref[rows, qcols].astype(BF16)
                kk = _load_keys([kr.at[:, _head_cols(p, hh)] for kr, p in zip(k_refs, ks)], win)
                vv = v_ref[win, _head_cols(v, hh)].astype(BF16)
                dov = do_ref[rows, vcols]
                dvec = jnp.sum(dov * o_ref[rows, vcols], axis=1, keepdims=True)
                dr_ref[hh, r] = _to_row(dvec)
                p, _ = _scores(cfg, _nt(qv, kk), t, n, hh, None, True, l_ref[hh, rows, :])
                ds = p * (_nt(dov.astype(BF16), vv) - dvec)
                dq = _nn(ds.astype(BF16), kk) * cfg.scale
                dq_ref[rows, qcols] = dq + in_ref[rows, qcols] if has_in else dq

    qs = pl.BlockSpec((tq, nh * dk), lambda i: (i, 0))
    ins = [_all_heads(cfg, q, tq)] + [_all_heads(cfg, p, cfg.sk, True) for p in ks]
    ins += [_all_heads(cfg, v, cfg.sk, True)]
    ins += [_all_heads(cfg, o, tq), pl.BlockSpec((nh, tq, 1), lambda i: (0, i, 0)), _all_heads(cfg, do, tq)]
    args = [q.arr] + [p.arr for p in ks] + [v.arr, o.arr, lse, do.arr]
    if has_in:
        ins.append(qs)
        args.append(dq_in)
    return _pcall(
        kern, name=name, grid=(cfg.nq,), in_specs=ins,
        out_specs=[qs, pl.BlockSpec((nh, nsub, 1, tb), lambda i: (0, i, 0, 0))],
        out_shape=[jax.ShapeDtypeStruct((s, nh * dk), F32), jax.ShapeDtypeStruct((nh, cfg.nb, 1, tb), F32)],
        compiler_params=pltpu.CompilerParams(dimension_semantics=("parallel",)))(*args)


def _attn_bwd(cfg, q, ks, v, sel, selt, o, lse, lse_r, do, dq_in, name):
    if cfg.causal:
        dq, d_r = _attn_dq(cfg, q, ks, v, sel, o, lse, do, dq_in, name + "_dq")
    else:
        dq, d_r = _attn_dq_small(cfg, q, ks, v, o, lse, do, dq_in, name + "_dq")
    dkv = _attn_dkv_flat if cfg.causal or cfg.mode == "win" else _attn_dkv
    res = dkv(cfg, q, ks, v, selt, lse_r, d_r, do, name + "_dkv")
    return dq, res[:-1], res[-1]


def _silu_grad(pre):
    sg = _sigmoid(pre)
    return sg * (1.0 + pre * (1.0 - sg))


def _compress_fwd(a_lo, a_hi, pe_lo, pe_hi, w1_lo, w1_hi, w2, name):
    n, dp = a_lo.shape[0], w2.shape[1]

    def kern(alo, ahi, plo, phi, w1l, w1h, w2r, out_ref, pre_ref):
        xl = (alo[...] + plo[...]).astype(BF16)
        xh = (ahi[...] + phi[...]).astype(BF16)
        pre = _nn(xl, w1l[...]) + _nn(xh, w1h[...])
        act = pre * _sigmoid(pre)
        out_ref[...] = _nn(act.astype(BF16), w2r[...]).astype(BF16)
        pre_ref[...] = pre

    return _pcall(kern, name=name,
                  out_shape=[jax.ShapeDtypeStruct((n, dp), BF16), jax.ShapeDtypeStruct((n, dp), F32)],
                  )(a_lo, a_hi, pe_lo, pe_hi, w1_lo, w1_hi, w2)


def _compress_bwd(a_lo, a_hi, pe_lo, pe_hi, w1_lo, w1_hi, w2, pre, pre_sh, dout, dout_sh, name):
    n, ln = a_lo.shape
    dp = w2.shape[1]

    def kern(alo, ahi, plo, phi, w1l, w1h, w2r, pre_ref, presh_ref, do_ref, dosh_ref,
             da_ref, dpl_ref, dph_ref, dw1l_ref, dw1h_ref, dw2_ref):
        prev = pre_ref[...]
        act = prev * _sigmoid(prev)
        dob = do_ref[...].astype(BF16)
        w2v = w2r[...]
        dpre = (_nt(dob, w2v) * _silu_grad(prev)).astype(BF16)
        dpre_sh = (_nt(dosh_ref[...].astype(BF16), w2v) * _silu_grad(presh_ref[...])).astype(BF16)
        dw2_ref[...] = _nn(act.T.astype(BF16), dob)
        xl = alo[...] + plo[...]
        xh = ahi[...] + phi[...]
        dw1l_ref[...] = _nn(xl.T.astype(BF16), dpre)
        dw1h_ref[...] = _nn(xh.T.astype(BF16), dpre)
        dal = _nt(dpre, w1l[...])
        dah_sh = _nt(dpre_sh, w1h[...])
        da_ref[...] = dal + dah_sh
        dpl_ref[...] = jnp.sum(dal, axis=0, keepdims=True)
        dph_ref[...] = jnp.sum(dah_sh, axis=0, keepdims=True)

    return _pcall(
        kern, name=name,
        out_shape=[jax.ShapeDtypeStruct((n, ln), F32), jax.ShapeDtypeStruct((1, ln), F32),
                   jax.ShapeDtypeStruct((1, ln), F32), jax.ShapeDtypeStruct((ln, dp), F32),
                   jax.ShapeDtypeStruct((ln, dp), F32), jax.ShapeDtypeStruct((dp, dp), F32)],
    )(a_lo, a_hi, pe_lo, pe_hi, w1_lo, w1_hi, w2, pre, pre_sh, dout, dout_sh)


def _nsa_combine(o_cmp, o_slc, o_win, gl):
    s, w = o_cmp.shape
    tr = _tile(s, 512)

    def kern(a_ref, b_ref, c_ref, g_ref, o_ref):
        g = _sigmoid(g_ref[...])
        for h in range(NSA_HEADS):
            cs = slice(h * HEAD_V, (h + 1) * HEAD_V)
            o_ref[:, cs] = (g[:, 3 * h:3 * h + 1] * a_ref[:, cs] + g[:, 3 * h + 1:3 * h + 2] * b_ref[:, cs]
                            + g[:, 3 * h + 2:3 * h + 3] * c_ref[:, cs])

    row = pl.BlockSpec((tr, w), lambda i: (i, 0))
    return _pcall(kern, name="nsa_combine", grid=(s // tr,),
                  in_specs=[row, row, row, pl.BlockSpec((tr, LANE), lambda i: (i, gl.col0))], out_specs=row,
                  out_shape=jax.ShapeDtypeStruct((s, w), F32))(o_cmp, o_slc, o_win, gl.arr)


def _nsa_combine_bwd(do_cat, o_cmp, o_slc, o_win, gl):
    s, w = o_cmp.shape
    tr = _tile(s, 512)

    def kern(d_ref, a_ref, b_ref, c_ref, g_ref, da_ref, db_ref, dc_ref, dg_ref):
        g = _sigmoid(g_ref[...])
        lane = lax.broadcasted_iota(jnp.int32, (tr, LANE), 1)
        dgl = jnp.zeros((tr, LANE), F32)
        for h in range(NSA_HEADS):
            cs = slice(h * HEAD_V, (h + 1) * HEAD_V)
            dv = d_ref[:, cs]
            for b, (src, dst) in enumerate(((a_ref, da_ref), (b_ref, db_ref), (c_ref, dc_ref))):
                gate = g[:, 3 * h + b:3 * h + b + 1]
                dst[:, cs] = gate * dv
                dgate = jnp.sum(dv * src[:, cs], axis=1, keepdims=True)
                dgl = jnp.where(lane == 3 * h + b, dgate * gate * (1.0 - gate), dgl)
        dg_ref[...] = dgl

    row = pl.BlockSpec((tr, w), lambda i: (i, 0))
    tab = pl.BlockSpec((tr, LANE), lambda i: (i, 0))
    return _pcall(kern, name="nsa_combine_bwd", grid=(s // tr,),
                  in_specs=[pl.BlockSpec((tr, w), lambda i: (i, 2)), row, row, row,
                            pl.BlockSpec((tr, LANE), lambda i: (i, gl.col0))],
                  out_specs=[row, row, row, tab],
                  out_shape=[jax.ShapeDtypeStruct((s, w), F32)] * 3 + [jax.ShapeDtypeStruct((s, LANE), F32)],
                  )(do_cat, o_cmp, o_slc, o_win, gl.arr)


def _gate_fwd(o_mla, o_nsa, o_mem, hp):
    s = o_mla.shape[0]
    tr = _tile(s, 256)

    def kern(a_ref, b_ref, c_ref, z_ref, u_ref):
        z = z_ref[...]
        sz = z * _sigmoid(z)
        u_ref[:, 0:1024] = (a_ref[...] * sz[:, 0:1024]).astype(BF16)
        u_ref[:, 1024:1536] = (b_ref[...] * sz[:, 1024:1536]).astype(BF16)
        u_ref[:, 1536:2048] = (c_ref[...] * sz[:, 1536:2048]).astype(BF16)

    return _pcall(
        kern, name="gate_fwd", grid=(s // tr,),
        in_specs=[pl.BlockSpec((tr, 1024), lambda i: (i, 0)), pl.BlockSpec((tr, 512), lambda i: (i, 0)),
                  pl.BlockSpec((tr, 512), lambda i: (i, 0)), pl.BlockSpec((tr, 2048), lambda i: (i, 2))],
        out_specs=pl.BlockSpec((tr, 2048), lambda i: (i, 0)),
        out_shape=jax.ShapeDtypeStruct((s, 2048), BF16))(o_mla, o_nsa, o_mem, hp)


def _gate_bwd(du, o_mla, o_nsa, o_mem, hp):
    s = du.shape[0]
    tr = _tile(s, 256)

    def kern(d_ref, a_ref, b_ref, c_ref, z_ref, do_ref, dz_ref):
        z = z_ref[...]
        sg = _sigmoid(z)
        sz = z * sg
        dsz = sg * (1.0 + z * (1.0 - sg))
        d = d_ref[...]
        do_ref[...] = d * sz
        dz_ref[:, 0:1024] = d[:, 0:1024] * a_ref[...] * dsz[:, 0:1024]
        dz_ref[:, 1024:1536] = d[:, 1024:1536] * b_ref[...] * dsz[:, 1024:1536]
        dz_ref[:, 1536:2048] = d[:, 1536:2048] * c_ref[...] * dsz[:, 1536:2048]

    wide = pl.BlockSpec((tr, 2048), lambda i: (i, 0))
    return _pcall(
        kern, name="gate_bwd", grid=(s // tr,),
        in_specs=[wide, pl.BlockSpec((tr, 1024), lambda i: (i, 0)), pl.BlockSpec((tr, 512), lambda i: (i, 0)),
                  pl.BlockSpec((tr, 512), lambda i: (i, 0)), pl.BlockSpec((tr, 2048), lambda i: (i, 2))],
        out_specs=[wide, wide],
        out_shape=[jax.ShapeDtypeStruct((s, 2048), F32)] * 2)(du, o_mla, o_nsa, o_mem, hp)


def _tile2d(rows, cols, arrays):
    if rows % 16 == 0:
        return _row_tile(rows, cols * arrays), cols
    want = max(LANE, BLOCK_BYTES // (rows * 4 * arrays) // LANE * LANE)
    tc = LANE
    for t in range(LANE, cols + 1, LANE):
        if cols % t == 0 and t <= want:
            tc = t
    return rows, tc


def _sum_slots(buf, name):
    n, rows, cols = buf.shape
    tr, tc = _tile2d(rows, cols, n)

    def kern(b_ref, o_ref):
        acc = b_ref[0].astype(F32)
        for i in range(1, n):
            acc = acc + b_ref[i].astype(F32)
        o_ref[...] = acc

    return _pcall(kern, name=name, grid=(rows // tr, cols // tc),
                  in_specs=[pl.BlockSpec((n, tr, tc), lambda i, j: (0, i, j))],
                  out_specs=pl.BlockSpec((tr, tc), lambda i, j: (i, j)),
                  out_shape=jax.ShapeDtypeStruct((rows, cols), F32))(buf)


def _pair_sum(g4, theirs, core, axis, name):
    n, rows, cols = theirs.shape
    tr, tc = _tile2d(rows, cols, 1)
    nbr, nbc = rows // tr, cols // tc

    def kern(c_ref, a_ref, b_ref, o_ref):
        o_ref[...] = (a_ref[...] + b_ref[...]).astype(BF16)

    blk = (1, tr, tc)
    mine = ((lambda s, i, j, c: (s, c[0] * nbr + i, j)) if axis == 0
            else (lambda s, i, j, c: (s, i, c[0] * nbc + j)))
    grid_spec = pltpu.PrefetchScalarGridSpec(
        num_scalar_prefetch=1, grid=(n, nbr, nbc),
        in_specs=[pl.BlockSpec(blk, mine), pl.BlockSpec(blk, lambda s, i, j, c: (s, i, j))],
        out_specs=pl.BlockSpec(blk, lambda s, i, j, c: (s, i, j)))
    return _pcall(kern, name=name, grid_spec=grid_spec,
                  out_shape=jax.ShapeDtypeStruct((n, rows, cols), BF16))(core, g4, theirs)


def _adamw(w, g, m, v, name):
    rows, cols = w.shape
    tr, tc = _tile2d(rows, cols, 4)
    bc1 = 1.0 - ADAM_B1 ** ADAM_STEP
    bc2 = 1.0 - ADAM_B2 ** ADAM_STEP

    def kern(w_ref, g_ref, m_ref, v_ref, d_ref, mo_ref, vo_ref):
        gv = g_ref[...]
        mn = ADAM_B1 * m_ref[...] + (1.0 - ADAM_B1) * gv
        vn = ADAM_B2 * v_ref[...] + (1.0 - ADAM_B2) * (gv * gv)
        d_ref[...] = -ADAM_LR * ((mn / bc1) / (jnp.sqrt(vn / bc2) + ADAM_EPS) + ADAM_WD * w_ref[...])
        mo_ref[...] = mn
        vo_ref[...] = vn

    blk = pl.BlockSpec((tr, tc), lambda i, j: (i, j))
    return _pcall(kern, name=name, grid=(rows // tr, cols // tc), in_specs=[blk] * 4, out_specs=[blk] * 3,
                  out_shape=[jax.ShapeDtypeStruct((rows, cols), F32)] * 3)(w, g, m, v)


ANY = pl.BlockSpec(memory_space=pl.ANY)


def _place():
    x, y, c = lax.axis_index("x"), lax.axis_index("y"), lax.axis_index("c")
    chips = [(1 - x, y), (x, 1 - y), (1 - x, 1 - y)]
    return x, y, c, chips


def _remote(src, dst, send_sem, recv_sem, to):
    return pltpu.make_async_remote_copy(src_ref=src, dst_ref=dst, send_sem=send_sem, recv_sem=recv_sem,
                                        device_id=to, device_id_type=MESH)


def _half(ref, lead, core, axis):
    size = ref.shape[len(lead) + axis] // 2
    cut = pl.ds(core * size, size)
    return ref.at[tuple(lead) + ((cut, slice(None)) if axis == 0 else (slice(None), cut))]


def _gather_shards(ws, axes):
    side = _gather_side(ws, axes)

    def body(*refs):
        nw = len(ws)
        split = (refs[:nw], refs[nw:2 * nw], refs[2 * nw:])
        side.phase("start", *split)
        side.phase("finish", *split)

    return _pcall(body, name="gather_shards", in_specs=[ANY] * len(ws), out_specs=[ANY] * len(ws),
                  out_shape=side.out_shape, scratch_shapes=side.scratch)(*ws)


def _gather_side(ws, axes):
    nw = len(ws)

    def phase(which, w_refs, out_refs, sems):
        send_sems, recv_sems = sems
        x, y, c, chips = _place()
        me = 2 * x + y
        sibling = (x, y, 1 - c)

        def part(i, slot, core):
            return _half(out_refs[i], (slot,), core, axes[i])

        def copy(sem, src, dst, to):
            return _remote(src, dst, send_sems.at[sem], recv_sems.at[sem], to)

        first = [copy(j * nw + i, _half(w_refs[i], (), c, axes[i]), part(i, me, c), (*chip, c))
                 for j, chip in enumerate(chips) for i in range(nw)]
        if which == "start":
            for cp in first:
                cp.start()
            return
        passed = []
        for j, (cx, cy) in enumerate(chips):
            slot = 2 * cx + cy
            for i in range(nw):
                copy(j * nw + i, part(i, slot, c), part(i, slot, c), (x, y, c)).wait_recv()
                fwd = copy((3 + j) * nw + i, part(i, slot, c), part(i, slot, c), sibling)
                fwd.start()
                passed.append(fwd)
        for j, (cx, cy) in enumerate(chips):
            slot = 2 * cx + cy
            for i in range(nw):
                copy((3 + j) * nw + i, part(i, slot, 1 - c), part(i, slot, 1 - c), (x, y, c)).wait_recv()
        for cp in first + passed:
            cp.wait_send()

    return _Side(list(ws), [jax.ShapeDtypeStruct((4,) + w.shape, w.dtype) for w in ws],
                 [pltpu.SemaphoreType.DMA((6 * nw,)), pltpu.SemaphoreType.DMA((6 * nw,))], phase)


def _half_shape(shape, axis):
    return tuple(d // 2 if k == len(shape) - 2 + axis else d for k, d in enumerate(shape))


def _pair_exchange(gs, axes, name):
    nw = len(gs)

    def body(*refs):
        g_refs, out_refs = refs[:nw], refs[nw:2 * nw]
        send_sems, recv_sems = refs[2 * nw:]
        x, y, c, _ = _place()
        cps = []
        for i in range(nw):
            cp = _remote(_half(g_refs[i], (slice(None),), 1 - c, axes[i]), out_refs[i],
                         send_sems.at[i], recv_sems.at[i], (x, y, 1 - c))
            cp.start()
            cps.append(cp)
        for cp in cps:
            cp.wait()

    return _pcall(body, name=name, in_specs=[ANY] * nw, out_specs=[ANY] * nw,
                  out_shape=[jax.ShapeDtypeStruct(_half_shape(g.shape, a), g.dtype) for g, a in zip(gs, axes)],
                  scratch_shapes=[pltpu.SemaphoreType.DMA((nw,)), pltpu.SemaphoreType.DMA((nw,))])(*gs)


def _chip_exchange(ps):
    side = _chip_side(ps)

    def body(*refs):
        nw = len(ps)
        split = (refs[:nw], refs[nw:2 * nw], refs[2 * nw:])
        side.phase("start", *split)
        side.phase("finish", *split)

    return _pcall(body, name="chip_exchange", in_specs=[ANY] * len(ps), out_specs=[ANY] * len(ps),
                  out_shape=side.out_shape, scratch_shapes=side.scratch)(*ps)


def _chip_side(ps):
    nw = len(ps)

    def phase(which, p_refs, out_refs, sems):
        send_sems, recv_sems, local_sems = sems
        x, y, c, chips = _place()
        me = 2 * x + y
        mine = [pltpu.make_async_copy(p_refs[i].at[me], out_refs[i].at[me], local_sems.at[i]) for i in range(nw)]
        sends = [_remote(p_refs[i].at[2 * cx + cy], out_refs[i].at[me], send_sems.at[j * nw + i],
                         recv_sems.at[j * nw + i], (cx, cy, c))
                 for j, (cx, cy) in enumerate(chips) for i in range(nw)]
        if which == "start":
            for cp in mine + sends:
                cp.start()
            return
        for j, (cx, cy) in enumerate(chips):
            slot = 2 * cx + cy
            for i in range(nw):
                _remote(out_refs[i].at[slot], out_refs[i].at[slot], send_sems.at[j * nw + i],
                        recv_sems.at[j * nw + i], (x, y, c)).wait_recv()
        for cp in sends:
            cp.wait_send()
        for cp in mine:
            cp.wait()

    return _Side(list(ps), [jax.ShapeDtypeStruct(p.shape, p.dtype) for p in ps],
                 [pltpu.SemaphoreType.DMA((3 * nw,)), pltpu.SemaphoreType.DMA((3 * nw,)),
                  pltpu.SemaphoreType.DMA((nw,))], phase)


def _half_exchange(ts, axes):
    nw = len(ts)

    def body(*refs):
        t_refs, out_refs = refs[:nw], refs[nw:2 * nw]
        send_sems, recv_sems = refs[2 * nw:]
        x, y, c, _ = _place()
        sends = []
        for i in range(nw):
            cp = _remote(t_refs[i], _half(out_refs[i], (), c, axes[i]), send_sems.at[i], recv_sems.at[i],
                         (x, y, 1 - c))
            cp.start()
            sends.append(cp)
        for i in range(nw):
            _remote(t_refs[i], _half(out_refs[i], (), 1 - c, axes[i]), send_sems.at[i], recv_sems.at[i],
                    (x, y, c)).wait_recv()
        for cp in sends:
            cp.wait_send()

    def whole(t, a):
        return tuple(2 * d if k == a else d for k, d in enumerate(t.shape))

    return _pcall(body, name="half_exchange", in_specs=[ANY] * nw, out_specs=[ANY] * nw,
                  out_shape=[jax.ShapeDtypeStruct(whole(t, a), t.dtype) for t, a in zip(ts, axes)],
                  scratch_shapes=[pltpu.SemaphoreType.DMA((nw,)), pltpu.SemaphoreType.DMA((nw,))])(*ts)


def _gather_all(v):
    rows, cols = v.shape

    def body(v_ref, out_ref, send_sems, recv_sems, local_sem):
        x, y, c, _ = _place()
        me = 4 * x + 2 * y + c
        mine = pltpu.make_async_copy(v_ref, out_ref.at[me], local_sem)
        mine.start()
        sends = []
        for d in range(1, 8):
            peer = (x ^ (d >> 2), y ^ ((d >> 1) & 1), c ^ (d & 1))
            cp = _remote(v_ref, out_ref.at[me], send_sems.at[d - 1], recv_sems.at[d - 1], peer)
            cp.start()
            sends.append(cp)
        for d in range(1, 8):
            slot = 4 * (x ^ (d >> 2)) + 2 * (y ^ ((d >> 1) & 1)) + (c ^ (d & 1))
            _remote(v_ref, out_ref.at[slot], send_sems.at[d - 1], recv_sems.at[d - 1], (x, y, c)).wait_recv()
        for cp in sends:
            cp.wait_send()
        mine.wait()

    return _pcall(body, name="gather_all", in_specs=[ANY], out_specs=ANY,
                  out_shape=jax.ShapeDtypeStruct((8, rows, cols), v.dtype),
                  scratch_shapes=[pltpu.SemaphoreType.DMA((7,)), pltpu.SemaphoreType.DMA((7,)),
                                  pltpu.SemaphoreType.DMA])(v)


def _pad_cols(a, width):
    return a if a.shape[1] == width else jnp.pad(a, ((0, 0), (0, width - a.shape[1])))


def _unpad_segments():
    z = PAD["z"]
    segs = [(PAD["c_q"], 0, 512), (PAD["c_kv"], 512, 512), (PAD["k_rope"], 1024, 64), (z, 1088, 1024)]
    segs += [(PAD["q_nsa"] + 256 * h, 2112 + NSA_DK * h, NSA_DK) for h in range(NSA_HEADS)]
    for name, rows in (("k_c", 192), ("v_c", 128), ("k_s", 192), ("v_s", 128), ("k_w", 192), ("v_w", 128),
                       ("g_nsa", 12)):
        segs.append((PAD[name], ORIG[name][0], rows))
    segs += [(z + 1024, ORIG["z_nsa"][0], 512), (PAD["q_mem"], ORIG["q_mem"][0], 512),
             (z + 1536, ORIG["z_mem"][0], 512)]
    return segs


def _w_in_grad_slots(gt):
    rows, cols = gt.shape
    shard = sum(n for _, _, n in _unpad_segments()) // 4
    tc = 256
    pieces = []
    for src, dst, n in _unpad_segments():
        while n:
            slot, off = divmod(dst, shard)
            take = min(n, shard - off)
            pieces.append((src, slot, off, take))
            src, dst, n = src + take, dst + take, n - take

    def kern(g_ref, o_ref):
        for src, slot, off, take in pieces:
            o_ref[slot, off:off + take, :] = g_ref[src:src + take, :]

    return _pcall(kern, name="w_in_grad_slots", grid=(cols // tc,),
                  in_specs=[pl.BlockSpec((rows, tc), lambda i: (0, i))],
                  out_specs=pl.BlockSpec((4, shard, tc), lambda i: (0, 0, i)),
                  out_shape=jax.ShapeDtypeStruct((4, shard, cols), F32))(gt)


def _w_in_from_slots(ws):
    nslot, shard, cols = ws.shape
    tc = 256
    pieces = []
    for dst, src, n in _unpad_segments() + [(PAD["k_rope"] + 64, ORIG["k_rope"][0], 64)]:
        while n:
            slot, off = divmod(src, shard)
            take = min(n, shard - off)
            pieces.append((dst, slot, off, take))
            src, dst, n = src + take, dst + take, n - take

    def kern(w_ref, o_ref):
        o_ref[...] = jnp.zeros_like(o_ref)
        for dst, slot, off, take in pieces:
            o_ref[dst:dst + take, :] = w_ref[slot, off:off + take, :]

    return _pcall(kern, name="w_in_from_slots", grid=(cols // tc,),
                  in_specs=[pl.BlockSpec((nslot, shard, tc), lambda i: (0, 0, i))],
                  out_specs=pl.BlockSpec((D_PAD, tc), lambda i: (0, i)),
                  out_shape=jax.ShapeDtypeStruct((D_PAD, cols), ws.dtype))(ws)


def _rope_tables(s):
    pos = jnp.arange(s, dtype=F32)
    inv_freq = ROPE_THETA ** (-jnp.arange(0, 64, 2, dtype=F32) / 64)
    ang = pos[:, None] * inv_freq[None, :]
    cos, sin = jnp.cos(ang), jnp.sin(ang)
    z = jnp.zeros((s, 64), F32)
    return jnp.concatenate([cos, cos, z], axis=1), jnp.concatenate([-sin, sin, z], axis=1)


def _overlap_table(s):
    n_c, n_s = s // CMP_STRIDE, s // SLC_LEN
    c0 = np.arange(n_c)[:, None] * CMP_STRIDE
    s0 = np.arange(LANE)[None, :] * SLC_LEN
    ov = (c0 < s0 + SLC_LEN) & (c0 + CMP_LEN > s0) & (np.arange(n_c)[:, None] < n_c - 1) & (np.arange(LANE)[None, :] < n_s)
    return jnp.asarray(ov.astype(np.float32), dtype=BF16)


def _shift_down(a):
    return jnp.concatenate([jnp.zeros((8, a.shape[1]), a.dtype), a], axis=0)[7:7 + a.shape[0]]


def _shift_up(a):
    return jnp.concatenate([a, jnp.zeros((8, a.shape[1]), a.dtype)], axis=0)[1:1 + a.shape[0]]


def _local_step(x, mem, target, w, hooks=None):
    s = x.shape[0]
    cs, sn = _rope_tables(s)
    t_ = jnp.transpose

    w_in_p = _w_in_from_slots(w["w_in_t"])
    xn, rstd_x = _rms_fwd(_Src(x, D_MODEL), w["norm_g"], "norm_x")
    if hooks is None:
        hp, hpb = _mm(xn, w_in_p, "in_proj", mode="nt", second_dtype=BF16)
    else:
        hp, hpb, *gathered = _mm(xn, w_in_p, "in_proj", mode="nt", second_dtype=BF16, side=hooks.gather_side)
        w = {**w, **hooks.weights(gathered)}

    w_uq3 = w["w_uq"].reshape(512, MLA_HEADS, 192)
    w_uq_p = jnp.concatenate([w_uq3, w_uq3[:, :, 128:]], axis=2).reshape(512, MLA_HEADS * 256)
    w_ukv_p = t_(w["w_ukv"].reshape(512, MLA_HEADS, 2, 128), (0, 2, 1, 3)).reshape(512, 2048)
    c_q, c_kv = _Src(hp, 512, 0), _Src(hp, 512, 1)
    cqn, rstd_q = _rms_fwd(c_q, w["q_norm_g"], "norm_q")
    ckvn, rstd_kv = _rms_fwd(c_kv, w["kv_norm_g"], "norm_kv")
    q_lin = _mm(cqn, w_uq_p, "mla_q_proj")
    kvb = _mm(ckvn, w_ukv_p, "mla_kv_proj", out_dtype=BF16)
    q_mla = _rope_fwd(_Src(q_lin, MLA_HEADS * 256), cs, sn, MLA_HEADS, 256, LANE, "rope_q")
    k_pe = _rope_fwd(_Src(hp, LANE, PAD["k_rope"] // LANE), cs, sn, 1, LANE, 0, "rope_k")
    mla = _Attn("mla", s, s, MLA_HEADS, 256)
    mla_q, mla_v = _Src(q_mla, 256), _Src(kvb, LANE, MLA_HEADS)
    mla_k = [_Src(kvb, LANE), _Src(k_pe, LANE, 0, False)]
    o_mla, l_mla, lr_mla = _attn_fwd(mla, mla_q, mla_k, mla_v, None, "mla_fwd")

    sk = s // CMP_STRIDE
    pe_k, pe_v = w["cmp_pe_k"], w["cmp_pe_v"]
    w1k = _pad_cols(w["cmp_w1k"], 256)
    w2k = jnp.pad(w["cmp_w2k"], ((0, 64), (0, 64))).astype(BF16)
    w1v, w2v = w["cmp_w1v"], w["cmp_w2v"].astype(BF16)
    half_k, half_v = CMP_STRIDE * NSA_DK, CMP_STRIDE * HEAD_V
    ak = hp[:, PAD["k_c"]:PAD["k_c"] + NSA_DK].reshape(sk, half_k)
    av = hp[:, PAD["v_c"]:PAD["v_c"] + HEAD_V].reshape(sk, half_v)
    ck_args = (ak, _shift_up(ak), pe_k[:CMP_STRIDE].reshape(1, half_k), pe_k[CMP_STRIDE:].reshape(1, half_k),
               w1k[:half_k], w1k[half_k:], w2k)
    cv_args = (av, _shift_up(av), pe_v[:CMP_STRIDE].reshape(1, half_v), pe_v[CMP_STRIDE:].reshape(1, half_v),
               w1v[:half_v], w1v[half_v:], w2v)
    k_cmp, pre_k = _compress_fwd(*ck_args, "compress_k")
    v_cmp, pre_v = _compress_fwd(*cv_args, "compress_v")
    cmp_ = _Attn("cmp", s, sk, NSA_HEADS, 256)
    slc = _Attn("slc", s, s, NSA_HEADS, 256)
    win = _Attn("win", s, s, NSA_HEADS, 256)
    nsa_q = _Src(hpb, 256, PAD["q_nsa"] // 256)
    cmp_k, cmp_v = [_Src(k_cmp, 256, 0, False)], _Src(v_cmp, HEAD_V, 0, False)
    slc_k, slc_v = [_Src(hpb, 256, PAD["k_s"] // 256, False)], _Src(hpb, HEAD_V, PAD["v_s"] // HEAD_V, False)
    win_k, win_v = [_Src(hpb, 256, PAD["k_w"] // 256, False)], _Src(hpb, HEAD_V, PAD["v_w"] // HEAD_V, False)
    o_cmp, l_cmp, lr_cmp, sel, selt = _attn_fwd_small(cmp_, nsa_q, cmp_k, cmp_v, "cmp_fwd", _overlap_table(s))
    o_slc, l_slc, lr_slc = _attn_fwd(slc, nsa_q, slc_k, slc_v, sel, "slc_fwd")
    o_win, l_win, lr_win = _attn_fwd_small(win, nsa_q, win_k, win_v, "win_fwd")
    gl = _Src(hp, LANE, PAD["g_nsa"] // LANE)
    o_nsa = _nsa_combine(o_cmp, o_slc, o_win, gl)

    mn, rstd_m = _rms_fwd(_Src(mem, D_MODEL), w["mem_norm_g"], "norm_mem")
    kvm = _mm(mn, w["w_mem_kv"], "mem_kv_proj", out_dtype=BF16)
    mem_ = _Attn("mem", s, mem.shape[0], MEM_HEADS, LANE)
    mem_q, mem_k, mem_v = _Src(hpb, LANE, PAD["q_mem"] // LANE), [_Src(kvm, LANE)], _Src(kvm, LANE, MEM_HEADS)
    o_mem, l_mem, lr_mem = _attn_fwd_small(mem_, mem_q, mem_k, mem_v, "mem_fwd")

    u = _gate_fwd(o_mla, o_nsa, o_mem, hp)
    proj = _mm(u, w["w_out"], "out_proj")
    dy, g_final, loss = _final_loss(x, proj, w["final_norm_g"].reshape(1, -1), target)

    g_w_out = _mm(u, dy, "out_proj_dw", mode="tn")
    du = _mm(dy, w["w_out"], "out_proj_dx", mode="nt")
    do_cat, dz = _gate_bwd(du, o_mla, o_nsa, o_mem, hp)

    dq_mem, (dk_mem,), dv_mem = _attn_bwd(mem_, mem_q, mem_k, mem_v, None, None, _Src(o_mem, HEAD_V), l_mem,
                                          lr_mem, _Src(do_cat, HEAD_V, 12), None, "mem_bwd")
    dkvm = jnp.concatenate([dk_mem, dv_mem], axis=1)
    g_w_mem_kv = _mm(mn, dkvm, "mem_kv_dw", mode="tn")
    dmn = _mm(dkvm, w["w_mem_kv"], "mem_kv_dx", mode="nt")
    _, g_mem_norm = _rms_bwd(_Src(mem, D_MODEL), w["mem_norm_g"], rstd_m, dmn, None, "norm_mem_bwd")

    do_cmp, do_slc, do_win, dgl = _nsa_combine_bwd(do_cat, o_cmp, o_slc, o_win, gl)
    dq_n, (dk_cmp,), dv_cmp = _attn_bwd(cmp_, nsa_q, cmp_k, cmp_v, None, None, _Src(o_cmp, HEAD_V), l_cmp,
                                        lr_cmp, _Src(do_cmp, HEAD_V), None, "cmp_bwd")
    dq_n, (dk_s,), dv_s = _attn_bwd(slc, nsa_q, slc_k, slc_v, sel, selt, _Src(o_slc, HEAD_V), l_slc, lr_slc,
                                    _Src(do_slc, HEAD_V), dq_n, "slc_bwd")
    dq_n, (dk_w,), dv_w = _attn_bwd(win, nsa_q, win_k, win_v, None, None, _Src(o_win, HEAD_V), l_win, lr_win,
                                    _Src(do_win, HEAD_V), dq_n, "win_bwd")
    dak, dpk_lo, dpk_hi, dw1k_lo, dw1k_hi, g_w2k = _compress_bwd(
        *ck_args, pre_k, _shift_down(pre_k), dk_cmp, _shift_down(dk_cmp), "compress_k_bwd")
    dav, dpv_lo, dpv_hi, dw1v_lo, dw1v_hi, g_w2v = _compress_bwd(
        *cv_args, pre_v, _shift_down(pre_v), dv_cmp, _shift_down(dv_cmp), "compress_v_bwd")
    g_pe_k = jnp.concatenate([dpk_lo.reshape(CMP_STRIDE, NSA_DK), dpk_hi.reshape(CMP_STRIDE, NSA_DK)], axis=0)
    g_pe_v = jnp.concatenate([dpv_lo.reshape(CMP_STRIDE, HEAD_V), dpv_hi.reshape(CMP_STRIDE, HEAD_V)], axis=0)
    g_w1k = jnp.concatenate([dw1k_lo, dw1k_hi], axis=0)[:, :NSA_DK]
    g_w1v = jnp.concatenate([dw1v_lo, dw1v_hi], axis=0)
    dk_c = _pad_cols(dak.reshape(s, NSA_DK), 256)
    dv_c = dav.reshape(s, HEAD_V)

    dq_m, (dk_nope, dk_pe), dv_m = _attn_bwd(mla, mla_q, mla_k, mla_v, None, None, _Src(o_mla, HEAD_V), l_mla,
                                             lr_mla, _Src(do_cat, HEAD_V), None, "mla_bwd")
    dq_lin = _rope_bwd_q(dq_m, cs, sn)
    dkv_lin, d_krope = _rope_bwd_k(dk_nope, dk_pe, dv_m, cs, sn)
    g_w_uq_p = _mm(cqn, dq_lin, "mla_q_dw", mode="tn")
    dcqn = _mm(dq_lin, w_uq_p, "mla_q_dx", mode="nt")
    g_w_ukv_p = _mm(ckvn, dkv_lin, "mla_kv_dw", mode="tn")
    dckvn = _mm(dkv_lin, w_ukv_p, "mla_kv_dx", mode="nt")
    dc_q, g_q_norm = _rms_bwd(c_q, w["q_norm_g"], rstd_q, dcqn, None, "norm_q_bwd")
    dc_kv, g_kv_norm = _rms_bwd(c_kv, w["kv_norm_g"], rstd_kv, dckvn, None, "norm_kv_bwd")
    g_w_uq = g_w_uq_p.reshape(512, MLA_HEADS, 256)[:, :, :192].reshape(512, MLA_HEADS * 192)
    g_w_ukv = t_(g_w_ukv_p.reshape(512, 2, MLA_HEADS, 128), (0, 2, 1, 3)).reshape(512, 2048)

    dhp = jnp.concatenate(
        [dc_q, dc_kv, dq_n, dk_c, dk_s, dk_w, d_krope, dv_c, dv_s, dv_w, dgl,
         jnp.zeros((s, PAD["q_mem"] - (PAD["g_nsa"] + LANE)), F32), dq_mem, dz], axis=1).astype(BF16)
    grads = dict(q_norm_g=g_q_norm, w_uq=g_w_uq, kv_norm_g=g_kv_norm,
                 w_ukv=g_w_ukv, cmp_pe_k=g_pe_k, cmp_pe_v=g_pe_v, cmp_w1k=g_w1k, cmp_w2k=g_w2k[:NSA_DK, :NSA_DK],
                 cmp_w1v=g_w1v, cmp_w2v=g_w2v, mem_norm_g=g_mem_norm, w_mem_kv=g_w_mem_kv, w_out=g_w_out,
                 final_norm_g=g_final.reshape(-1))
    if hooks is None:
        g_w_in_t = _w_in_grad_slots(_mm(dhp, xn, "in_proj_dw", mode="tn", wide=2048))
        dxn = _mm(dhp, w_in_p, "in_proj_dx", wide=2048)
    else:
        g_w_in_p, *hooks.received = _mm(dhp, xn, "in_proj_dw", mode="tn", wide=2048, side=hooks.reduce_side(grads))
        g_w_in_t = _w_in_grad_slots(g_w_in_p)
        dxn, hooks.received_w_in = _mm(dhp, w_in_p, "in_proj_dx", wide=2048, side=hooks.reduce_side_w_in(g_w_in_t))
    grad_x, g_norm = _rms_bwd(_Src(x, D_MODEL), w["norm_g"], rstd_x, dxn, dy, "norm_x_bwd")
    grads.update(norm_g=g_norm, w_in_t=g_w_in_t)
    return loss[0, 0], grad_x, grads


def kernel(x, mem, norm_g, w_in, q_norm_g, w_uq, kv_norm_g, w_ukv, cmp_pe_k, cmp_pe_v, cmp_w1k, cmp_w2k, cmp_w1v, cmp_w2v, mem_norm_g, w_mem_kv, w_out, final_norm_g, loss_target, m_norm_g, m_w_in, m_q_norm_g, m_w_uq, m_kv_norm_g, m_w_ukv, m_cmp_pe_k, m_cmp_pe_v, m_cmp_w1k, m_cmp_w2k, m_cmp_w1v, m_cmp_w2v, m_mem_norm_g, m_w_mem_kv, m_w_out, m_final_norm_g, v_norm_g, v_w_in, v_q_norm_g, v_w_uq, v_kv_norm_g, v_w_ukv, v_cmp_pe_k, v_cmp_pe_v, v_cmp_w1k, v_cmp_w2k, v_cmp_w1v, v_cmp_w2v, v_mem_norm_g, v_w_mem_kv, v_w_out, v_final_norm_g):
    args = dict(locals())
    wts = {n: args[n] for n in WEIGHTS}
    loc = {n: (a if n == "final_norm_g" else a[0]) for n, a in wts.items()}

    def to_x(n, a):
        return a.T if n == "w_in" else a

    split = [1 if n == "w_in" else 0 for n in SHARDED]
    rest = [n for n in SHARDED if n != "w_in"]
    chip = 2 * lax.axis_index("x") + lax.axis_index("y")
    core = lax.axis_index("c").astype(jnp.int32).reshape(1)
    own = {n: to_x(n, loc[n]).astype(BF16) for n in SHARDED}

    def with_own_slot(gw, a):
        return lax.dynamic_update_slice(gw, a[None], (chip, 0, 0))

    def slots(n, a):
        if n == "w_in":
            return a
        if SHARD_AXIS[n] == 0:
            return a.reshape(4, a.shape[0] // 4, a.shape[1])
        width = a.shape[1] // 4
        return jnp.stack([a[:, j * width:(j + 1) * width] for j in range(4)])

    def pair_sums(names, grads, name):
        axes = [1 if n == "w_in" else 0 for n in names]
        gs = [slots(n, a) for n, a in zip(names, grads)]
        theirs = _pair_exchange(gs, axes, name)
        return [_pair_sum(a, b, core, ax, "pair_sum_" + n) for n, a, b, ax in zip(names, gs, theirs, axes)]

    class Hooks:
        gather_side = _gather_side([own[n] for n in rest], [0] * len(rest))
        received = None

        @staticmethod
        def weights(gathered):
            out = {}
            for n, gw in zip(rest, gathered):
                gw = with_own_slot(gw, own[n])
                if SHARD_AXIS[n] == 0:
                    out[n] = gw.reshape(4 * gw.shape[1], gw.shape[2])
                else:
                    out[n] = jnp.concatenate([gw[j] for j in range(4)], axis=1)
            return out

        @staticmethod
        def reduce_side(grads):
            return _chip_side(pair_sums(rest, [grads[n] for n in rest], "pair_exchange_rest"))

        @staticmethod
        def reduce_side_w_in(g_w_in_t):
            return _chip_side(pair_sums(["w_in"], [g_w_in_t], "pair_exchange_w_in"))

    hooks = Hooks()

    start = {n: loc[n].reshape(1, -1) if loc[n].ndim == 1 else loc[n] for n in REPLICATED}
    start["w_in_t"] = with_own_slot(_gather_shards([own["w_in"]], [1])[0], own["w_in"])
    loss, grad_x, g = _local_step(x[0], mem[0], loss_target[0], start, hooks)
    loss = lax.psum(loss, ("x", "y", "c"))

    from_chips = dict(zip(rest, hooks.received), w_in=hooks.received_w_in)
    mine = [_sum_slots(from_chips[n], "chip_sum_" + n) for n in SHARDED]
    g_sh = [lax.dynamic_update_slice(o, t, (core[0] * t.shape[0], 0) if ax == 0 else (0, core[0] * t.shape[1]))
            for o, t, ax in zip(_half_exchange(mine, split), mine, split)]

    n_rep = sum(int(np.prod(loc[n].shape)) for n in REPLICATED)
    rows_rep = -(-n_rep // (8 * LANE)) * 8

    def rep_pack(parts):
        flat = jnp.concatenate([p.reshape(-1) for p in parts])
        return jnp.pad(flat, (0, rows_rep * LANE - n_rep)).reshape(rows_rep, LANE)

    g_rep = _sum_slots(_gather_all(rep_pack([g[n] for n in REPLICATED])), "replica_sum")
    d_rp, m_rp, v_rp = _adamw(rep_pack([wts[n] for n in REPLICATED]), g_rep,
                              rep_pack([args["m_" + n] for n in REPLICATED]),
                              rep_pack([args["v_" + n] for n in REPLICATED]), "adamw_replicated")

    def rep_unpack(buf):
        flat, out, o = buf.reshape(-1), {}, 0
        for n in REPLICATED:
            size = int(np.prod(wts[n].shape))
            out[n] = flat[o:o + size].reshape(wts[n].shape)
            o += size
        return out

    outs = {k: rep_unpack(b) for k, b in (("g", g_rep), ("d", d_rp), ("m", m_rp), ("v", v_rp))}
    for n, gn in zip(SHARDED, g_sh):
        d, mo, vo = _adamw(to_x(n, loc[n]), gn, to_x(n, args["m_" + n][0]), to_x(n, args["v_" + n][0]),
                           "adamw_" + n)
        for k, a in (("g", gn), ("d", d), ("m", mo), ("v", vo)):
            outs[k][n] = to_x(n, a).reshape(wts[n].shape)

    return (loss, grad_x[None], *[outs["g"][n] for n in WEIGHTS], *[outs["d"][n] for n in WEIGHTS],
            *[outs["m"][n] for n in WEIGHTS], *[outs["v"][n] for n in WEIGHTS])
```

```python
from typing import NamedTuple

import numpy as np
import jax
import jax.numpy as jnp
from jax import lax
from jax.experimental import pallas as pl
from jax.experimental.pallas import tpu as pltpu

F32 = jnp.float32
BF16 = jnp.bfloat16
MESH = pl.DeviceIdType.MESH

D_MODEL = 2048
EPS = 1e-6
LANE = 128
HEAD_V = 128
MLA_HEADS = 8
NSA_HEADS = 4
MEM_HEADS = 4
NSA_DK = 192
CMP_STRIDE = 16
CMP_LEN = 32
SLC_LEN = 64
SLC_TOPN = 16
WIN = 512
NEG = -1e30
LOG2E = 1.4426950408889634
ROPE_THETA = 10000.0
BLOCK_BYTES = 2 << 20

ORIG = dict(c_q=(0, 512), c_kv=(512, 512), k_rope=(1024, 64), z_mla=(1088, 1024),
            q_nsa=(2112, 768), k_c=(2880, 192), v_c=(3072, 128), k_s=(3200, 192),
            v_s=(3392, 128), k_w=(3520, 192), v_w=(3712, 128), g_nsa=(3840, 12),
            z_nsa=(3852, 512), q_mem=(4364, 512), z_mem=(4876, 512))
PAD = dict(c_q=0, c_kv=512, q_nsa=1024, k_c=2048, k_s=2304, k_w=2560, k_rope=2816, v_c=2944,
           v_s=3072, v_w=3200, g_nsa=3328, q_mem=3584, z=4096)
D_PAD = 6144

ADAM_LR, ADAM_B1, ADAM_B2, ADAM_EPS, ADAM_WD, ADAM_STEP = 0.001, 0.9, 0.999, 1e-08, 0.01, 10

SHARDED = ("w_in", "w_uq", "w_ukv", "cmp_w1k", "cmp_w1v", "w_mem_kv", "w_out")
SHARD_AXIS = dict(w_in=1, w_uq=1, w_ukv=1, cmp_w1k=0, cmp_w1v=0, w_mem_kv=0, w_out=0)
REPLICATED = ("norm_g", "q_norm_g", "kv_norm_g", "cmp_pe_k", "cmp_pe_v", "cmp_w2k", "cmp_w2v",
              "mem_norm_g", "final_norm_g")
WEIGHTS = ("norm_g", "w_in", "q_norm_g", "w_uq", "kv_norm_g", "w_ukv", "cmp_pe_k", "cmp_pe_v",
           "cmp_w1k", "cmp_w2k", "cmp_w1v", "cmp_w2v", "mem_norm_g", "w_mem_kv", "w_out",
           "final_norm_g")


def _pcall(kernel, **kw):
    return pl.pallas_call(kernel, **kw)


def _tile(n, pref):
    if n <= pref:
        return n
    for t in range(pref, LANE - 1, -LANE):
        if n % t == 0:
            return t
    raise ValueError((n, pref))


def _row_tile(rows, cols, itemsize=4):
    want = max(16, BLOCK_BYTES // (cols * itemsize))
    if rows <= want:
        return rows
    t = 16
    best = rows
    while t <= want:
        if rows % t == 0:
            best = t
        t *= 2
    return best


def _nt(a, b):
    return lax.dot_general(a, b, (((1,), (1,)), ((), ())), preferred_element_type=F32)


def _tn(a, b):
    return lax.dot_general(a, b, (((0,), (0,)), ((), ())), preferred_element_type=F32)


def _nn(a, b):
    return jnp.dot(a, b, preferred_element_type=F32)


def _sigmoid(x):
    return 1.0 / (1.0 + jnp.exp(-x))


class _Src(NamedTuple):
    arr: jax.Array
    width: int
    col0: int = 0
    per_head: bool = True

    def col(self, h):
        return self.col0 + h if self.per_head else self.col0


class _Side(NamedTuple):
    inputs: list
    out_shape: list
    scratch: list
    phase: object


def _mm(a, b, name, mode="nn", out_dtype=F32, second_dtype=None, wide=1024, side=None):
    if mode == "tn":
        k, m = a.shape
    else:
        m, k = a.shape
    if mode == "nt":
        n, k2 = b.shape
    else:
        k2, n = b.shape
    assert k == k2, (a.shape, b.shape, mode)
    tm, tn, tk = _tile(m, 1024), _tile(n, wide), _tile(k, 2048)
    grid = (m // tm, n // tn, k // tk)
    nk = grid[2]
    assert nk == 1 or (out_dtype == F32 and second_dtype is None)
    dot = {"nn": _nn, "nt": _nt, "tn": _tn}[mode]
    n_in = len(side.inputs) if side else 0
    n_out = len(side.out_shape) if side else 0
    n_res = 1 + (second_dtype is not None)

    def kern(*refs):
        a_ref, b_ref = refs[:2]
        res = refs[2 + n_in:2 + n_in + n_res]
        step = [pl.program_id(d) for d in range(3)]
        if side:
            side_refs = (refs[2:2 + n_in], refs[2 + n_in + n_res:2 + n_in + n_res + n_out],
                         refs[2 + n_in + n_res + n_out:])

            @pl.when((step[0] == 0) & (step[1] == 0) & (step[2] == 0))
            def _():
                side.phase("start", *side_refs)

        r = dot(a_ref[...].astype(BF16), b_ref[...].astype(BF16))
        if nk == 1:
            res[0][...] = r.astype(out_dtype)
            if n_res == 2:
                res[1][...] = r.astype(second_dtype)
        else:
            @pl.when(step[2] == 0)
            def _():
                res[0][...] = r

            @pl.when(step[2] > 0)
            def _():
                res[0][...] += r

        if side:
            @pl.when((step[0] == grid[0] - 1) & (step[1] == grid[1] - 1) & (step[2] == nk - 1))
            def _():
                side.phase("finish", *side_refs)

    a_spec = (pl.BlockSpec((tk, tm), lambda i, j, kk: (kk, i)) if mode == "tn"
              else pl.BlockSpec((tm, tk), lambda i, j, kk: (i, kk)))
    b_spec = (pl.BlockSpec((tn, tk), lambda i, j, kk: (j, kk)) if mode == "nt"
              else pl.BlockSpec((tk, tn), lambda i, j, kk: (kk, j)))
    o_spec = pl.BlockSpec((tm, tn), lambda i, j, kk: (i, j))
    out_specs = [o_spec] * n_res + [ANY] * n_out
    out_shape = [jax.ShapeDtypeStruct((m, n), out_dtype)]
    if second_dtype is not None:
        out_shape.append(jax.ShapeDtypeStruct((m, n), second_dtype))
    out_shape += list(side.out_shape) if side else []
    semantics = ("arbitrary",) * 3 if side else ("parallel", "parallel", "arbitrary")
    out = _pcall(
        kern, name=name, grid=grid, in_specs=[a_spec, b_spec] + [ANY] * n_in, out_specs=out_specs,
        out_shape=out_shape, scratch_shapes=list(side.scratch) if side else [],
        compiler_params=pltpu.CompilerParams(dimension_semantics=semantics),
    )(a, b, *(side.inputs if side else []))
    return out[0] if len(out) == 1 else out


def _rms_fwd(x, g, name):
    r, d = x.arr.shape[0], x.width
    tr = _tile(r, 512)

    def kern(x_ref, g_ref, y_ref, r_ref):
        xv = x_ref[...]
        rstd = lax.rsqrt(jnp.mean(xv * xv, axis=-1, keepdims=True) + EPS)
        y_ref[...] = (xv * rstd * g_ref[...]).astype(BF16)
        r_ref[...] = rstd

    return _pcall(
        kern, name=name, grid=(r // tr,),
        in_specs=[pl.BlockSpec((tr, d), lambda i: (i, x.col0)), pl.BlockSpec((1, d), lambda i: (0, 0))],
        out_specs=[pl.BlockSpec((tr, d), lambda i: (i, 0)), pl.BlockSpec((tr, 1), lambda i: (i, 0))],
        out_shape=[jax.ShapeDtypeStruct((r, d), BF16), jax.ShapeDtypeStruct((r, 1), F32)],
    )(x.arr, g)


def _rms_bwd(x, g, rstd, dy, add, name):
    r, d = x.arr.shape[0], x.width
    tr = _tile(r, 256)
    has_add = add is not None

    def kern(*refs):
        if has_add:
            x_ref, g_ref, r_ref, dy_ref, add_ref, dx_ref, dg_ref = refs
        else:
            x_ref, g_ref, r_ref, dy_ref, dx_ref, dg_ref = refs
        rs = r_ref[...]
        xhat = x_ref[...] * rs
        dyv = dy_ref[...]
        dyg = dyv * g_ref[...]
        c = jnp.mean(dyg * xhat, axis=-1, keepdims=True)
        dx = rs * (dyg - xhat * c)
        if has_add:
            dx = dx + add_ref[...]
        dx_ref[...] = dx
        part = jnp.sum(dyv * xhat, axis=0, keepdims=True)

        @pl.when(pl.program_id(0) == 0)
        def _():
            dg_ref[...] = part

        @pl.when(pl.program_id(0) > 0)
        def _():
            dg_ref[...] += part

    row = pl.BlockSpec((tr, d), lambda i: (i, 0))
    vec = pl.BlockSpec((1, d), lambda i: (0, 0))
    ins = [pl.BlockSpec((tr, d), lambda i: (i, x.col0)), vec, pl.BlockSpec((tr, 1), lambda i: (i, 0)), row]
    ins += [row] if has_add else []
    args = (x.arr, g, rstd, dy) + ((add,) if has_add else ())
    return _pcall(
        kern, name=name, grid=(r // tr,), in_specs=ins, out_specs=[row, vec],
        out_shape=[jax.ShapeDtypeStruct((r, d), F32), jax.ShapeDtypeStruct((1, d), F32)],
        compiler_params=pltpu.CompilerParams(dimension_semantics=("arbitrary",)),
    )(*args)


def _final_loss(x, proj, g, target):
    r, d = x.shape
    tr = _tile(r, 256)

    def kern(x_ref, p_ref, g_ref, t_ref, dy_ref, dg_ref, loss_ref):
        y = x_ref[...] + p_ref[...]
        rs = lax.rsqrt(jnp.mean(y * y, axis=-1, keepdims=True) + EPS)
        yhat = y * rs
        gv = g_ref[...]
        e = yhat * gv - t_ref[...]
        lpart = 0.5 * jnp.sum(jnp.mean(e * e, axis=-1, keepdims=True), axis=0, keepdims=True)
        dout = e * (1.0 / d)
        dyg = dout * gv
        c = jnp.mean(dyg * yhat, axis=-1, keepdims=True)
        dy_ref[...] = rs * (dyg - yhat * c)
        gpart = jnp.sum(dout * yhat, axis=0, keepdims=True)
        lrow = jnp.broadcast_to(lpart, (1, LANE))

        @pl.when(pl.program_id(0) == 0)
        def _():
            dg_ref[...] = gpart
            loss_ref[...] = lrow

        @pl.when(pl.program_id(0) > 0)
        def _():
            dg_ref[...] += gpart
            loss_ref[...] += lrow

    row = pl.BlockSpec((tr, d), lambda i: (i, 0))
    vec = pl.BlockSpec((1, d), lambda i: (0, 0))
    return _pcall(
        kern, name="final_loss", grid=(r // tr,), in_specs=[row, row, vec, row],
        out_specs=[row, vec, pl.BlockSpec((1, LANE), lambda i: (0, 0))],
        out_shape=[jax.ShapeDtypeStruct((r, d), F32), jax.ShapeDtypeStruct((1, d), F32),
                   jax.ShapeDtypeStruct((1, LANE), F32)],
        compiler_params=pltpu.CompilerParams(dimension_semantics=("arbitrary",)),
    )(x, proj, g, target)


def _rope_fwd(x, cs, sn, nh, width, off, name):
    s = x.arr.shape[0]
    tr = _tile(s, 512)

    def kern(x_ref, c_ref, s_ref, o_ref):
        cv, sv = c_ref[...], s_ref[...]
        for h in range(nh):
            b = h * width
            if off:
                o_ref[:, b:b + off] = x_ref[:, b:b + off].astype(BF16)
            xr = x_ref[:, b + off:b + off + LANE]
            o_ref[:, b + off:b + off + LANE] = (xr * cv + pltpu.roll(xr, 32, 1) * sv).astype(BF16)

    tab = pl.BlockSpec((tr, LANE), lambda i: (i, 0))
    return _pcall(
        kern, name=name, grid=(s // tr,),
        in_specs=[pl.BlockSpec((tr, nh * width), lambda i: (i, x.col0)), tab, tab],
        out_specs=pl.BlockSpec((tr, nh * width), lambda i: (i, 0)),
        out_shape=jax.ShapeDtypeStruct((s, nh * width), BF16),
    )(x.arr, cs, sn)


def _rope_grad(d, cv, sv):
    g2 = d * sv
    g2 = g2 + pltpu.roll(g2, 64, 1)
    lane = lax.broadcasted_iota(jnp.int32, d.shape, 1)
    return jnp.where(lane < 64, d * cv + pltpu.roll(g2, 32, 1), 0.0)


def _rope_bwd_q(dq, cs, sn):
    s, w = dq.shape
    tr = _tile(s, 512)
    nh = w // 256

    def kern(d_ref, c_ref, s_ref, o_ref):
        cv, sv = c_ref[...], s_ref[...]
        for h in range(nh):
            b = h * 256
            o_ref[:, b:b + LANE] = d_ref[:, b:b + LANE]
            o_ref[:, b + LANE:b + 256] = _rope_grad(d_ref[:, b + LANE:b + 256], cv, sv)

    row = pl.BlockSpec((tr, w), lambda i: (i, 0))
    tab = pl.BlockSpec((tr, LANE), lambda i: (i, 0))
    return _pcall(kern, name="rope_bwd_q", grid=(s // tr,), in_specs=[row, tab, tab], out_specs=row,
                  out_shape=jax.ShapeDtypeStruct((s, w), F32))(dq, cs, sn)


def _rope_bwd_k(dk_nope, dk_pe, dv, cs, sn):
    s, w = dk_nope.shape
    tr = _tile(s, 512)

    def kern(dk_ref, dp_ref, dv_ref, c_ref, s_ref, okv_ref, okr_ref):
        okv_ref[:, :w] = dk_ref[...]
        okv_ref[:, w:] = dv_ref[...]
        okr_ref[...] = _rope_grad(dp_ref[...], c_ref[...], s_ref[...])

    tab = pl.BlockSpec((tr, LANE), lambda i: (i, 0))
    wide = pl.BlockSpec((tr, w), lambda i: (i, 0))
    return _pcall(
        kern, name="rope_bwd_k", grid=(s // tr,), in_specs=[wide, tab, wide, tab, tab],
        out_specs=[pl.BlockSpec((tr, 2 * w), lambda i: (i, 0)), tab],
        out_shape=[jax.ShapeDtypeStruct((s, 2 * w), F32), jax.ShapeDtypeStruct((s, LANE), F32)],
    )(dk_nope, dk_pe, dv, cs, sn)


class _Attn:
    def __init__(self, mode, s, sk, heads, dk):
        self.mode, self.s, self.sk, self.h, self.dk = mode, s, sk, heads, dk
        self.scale = {"mla": 192 ** -0.5, "mem": 128 ** -0.5}.get(mode, NSA_DK ** -0.5)
        self.tb = min(256, s)
        self.nb = s // self.tb
        self.nsub = 2 if self.nb % 2 == 0 else 1
        self.tq = self.tb * self.nsub
        self.nq = s // self.tq
        self.causal = mode in ("mla", "slc")
        if self.causal:
            self.tk = self.tq
        elif mode == "win":
            self.tk = WIN + self.tb
        else:
            self.tk = sk
        self.tkb = min(512, sk)
        self.ksub = 2 if self.tkb == 512 and mode == "mla" else 1
        self.kb = self.tkb // self.ksub
        self.ncmp = s // CMP_STRIDE - 1

    def mask_bias(self, t, n, h, selx, diag):
        m = self.mode
        if m == "mla":
            return (n <= t) if diag else None, None
        if m == "mem":
            return None, None
        slope = jnp.where(h == 0, 0.25, jnp.where(h == 1, 0.0625, jnp.where(h == 2, 0.015625, 0.00390625)))
        slope = slope.astype(F32) * LOG2E
        if m == "cmp":
            mask = (n * CMP_STRIDE + (CMP_LEN - 1) <= t) & (n < self.ncmp)
            pos = n.astype(F32) * float(CMP_STRIDE) + (CMP_LEN - 1) / 2.0
            return mask, slope * pos
        rel = t - n
        if m == "slc":
            return (rel >= 0) if diag else None, slope * n.astype(F32)
        return (rel >= 0) & (rel < WIN), slope * n.astype(F32)


def _scores(cfg, s_raw, t, n, h, selx, diag, lse=None):
    s = s_raw * (cfg.scale * LOG2E)
    mask, key_term = cfg.mask_bias(t, n, h, selx, diag)
    if key_term is not None:
        s = s + key_term
    if selx is not None:
        s = s + selx
    if lse is None:
        if mask is not None:
            s = jnp.where(mask, s, NEG)
        return s, mask
    p = jnp.exp2(jnp.minimum(s - lse, 0.0))
    if mask is not None:
        p = jnp.where(mask, p, 0.0)
    return p, mask


def _block_of_key(k0, tk, keys_on_rows, value=NEG):
    shape = (tk, LANE) if keys_on_rows else (LANE, tk)
    n = lax.broadcasted_iota(jnp.int32, shape, 0 if keys_on_rows else 1) + k0
    j = lax.broadcasted_iota(jnp.int32, shape, 1 if keys_on_rows else 0)
    return jnp.where((n >> 6) == j, value, 0.0).astype(BF16)


def _to_row(col):
    t = col.shape[0]
    return jnp.transpose(jnp.broadcast_to(col, (t, LANE)))[0:1, :]


def _load_keys(k_refs, rows):
    parts = [r[rows, :].astype(BF16) for r in k_refs]
    return parts[0] if len(parts) == 1 else jnp.concatenate(parts, axis=1)


def _attn_fwd(cfg, q, ks, v, sel, name):
    s, tq, tk, tb, nsub = cfg.s, cfg.tq, cfg.tk, cfg.tb, cfg.nsub
    has_sel = sel is not None
    nkp = len(ks)
    assert cfg.causal and tq == tk and cfg.nq % 2 == 0

    def kern(*refs):
        q_ref, k_refs, v_ref = refs[0], refs[1:1 + nkp], refs[1 + nkp]
        sel_ref = refs[2 + nkp] if has_sel else None
        o_ref, lc_ref, lr_ref = refs[2 + nkp + has_sel:5 + nkp + has_sel]
        buf_a, buf_b = refs[-2:]
        h, g = pl.program_id(0), pl.program_id(1)

        def block(b):
            rows = [slice(b * tq + r * tb, b * tq + (r + 1) * tb) for r in range(nsub)]
            qs = [q_ref[p, :].astype(BF16) for p in rows]
            ts = [(2 * g + b) * tq + r * tb + lax.broadcasted_iota(jnp.int32, (tb, 1), 0) for r in range(nsub)]
            sels = [sel_ref[p, :].astype(BF16) for p in rows] if has_sel else None
            return rows, qs, ts, sels

        def scores_into(buf, blk, c):
            kk = _load_keys(k_refs, pl.ds(pl.multiple_of(c * tk, tk), tk))
            for r in range(nsub):
                buf[r] = _nt(blk[1][r], kk)

        def consume(buf, blk, c, carry, diag):
            _, _, ts, sels = blk
            k0 = pl.multiple_of(c * tk, tk)
            vv = v_ref[pl.ds(k0, tk), :].astype(BF16)
            emat = _block_of_key(k0, tk, False) if has_sel else None
            n = k0 + lax.broadcasted_iota(jnp.int32, (1, tk), 1)
            new = []
            for r in range(nsub):
                m, l, acc = carry[r]
                selx = _nn(sels[r], emat) if has_sel else None
                sc, mask = _scores(cfg, buf[r], ts[r], n, h, selx, diag)
                m_new = jnp.maximum(m, jnp.max(sc, axis=1, keepdims=True))
                alpha = jnp.exp2(m - m_new)
                p = jnp.exp2(sc - m_new)
                if mask is not None:
                    p = jnp.where(mask, p, 0.0)
                l = alpha * l + jnp.sum(p, axis=1, keepdims=True)
                new.append((m_new, l, alpha * acc + _nn(p.astype(BF16), vv)))
            return tuple(new)

        def finish(blk, b, carry):
            for r, (m, l, acc) in enumerate(carry):
                o_ref[blk[0][r], :] = acc / (l + 1e-20)
                lse = m + jnp.log(l + 1e-20) * LOG2E
                lc_ref[0, blk[0][r], :] = lse
                lr_ref[0, b * nsub + r] = _to_row(lse)

        def pairs(blk, first, other):
            def pair(p, cr):
                scores_into(other, blk, 2 * p + 1)
                cr = consume(first, blk, 2 * p, cr, False)
                scores_into(first, blk, 2 * p + 2)
                return consume(other, blk, 2 * p + 1, cr, False)
            return pair

        init = ((jnp.full((tb, 1), NEG, F32), jnp.zeros((tb, 1), F32), jnp.zeros((tb, HEAD_V), F32)),) * nsub
        blk_a, blk_b = block(0), block(1)
        scores_into(buf_a, blk_a, 0)
        carry = lax.fori_loop(0, g, pairs(blk_a, buf_a, buf_b), init)
        scores_into(buf_b, blk_b, 0)
        finish(blk_a, 0, consume(buf_a, blk_a, 2 * g, carry, True))
        carry = lax.fori_loop(0, g, pairs(blk_b, buf_b, buf_a), init)
        scores_into(buf_a, blk_b, 2 * g + 1)
        carry = consume(buf_b, blk_b, 2 * g, carry, False)
        finish(blk_b, 1, consume(buf_a, blk_b, 2 * g + 1, carry, True))

    ins = [pl.BlockSpec((2 * tq, q.width), lambda h, g: (g, q.col(h)))]
    ins += [pl.BlockSpec((cfg.sk, p.width), lambda h, g, p=p: (0, p.col(h))) for p in ks]
    ins += [pl.BlockSpec((cfg.sk, HEAD_V), lambda h, g: (0, v.col(h)))]
    args = [q.arr] + [p.arr for p in ks] + [v.arr]
    if has_sel:
        ins.append(pl.BlockSpec((2 * tq, LANE), lambda h, g: (g, 0)))
        args.append(sel)
    return _pcall(
        kern, name=name, grid=(cfg.h, cfg.nq // 2), in_specs=ins,
        out_specs=[pl.BlockSpec((2 * tq, HEAD_V), lambda h, g: (g, h)),
                   pl.BlockSpec((1, 2 * tq, 1), lambda h, g: (h, g, 0)),
                   pl.BlockSpec((1, 2 * nsub, 1, tb), lambda h, g: (h, g, 0, 0))],
        out_shape=[jax.ShapeDtypeStruct((s, cfg.h * HEAD_V), F32),
                   jax.ShapeDtypeStruct((cfg.h, s, 1), F32),
                   jax.ShapeDtypeStruct((cfg.h, cfg.nb, 1, tb), F32)],
        scratch_shapes=[pltpu.VMEM((nsub, tb, tk), F32)] * 2,
        compiler_params=pltpu.CompilerParams(dimension_semantics=("parallel", "parallel")),
    )(*args)


def _attn_dq(cfg, q, ks, v, sel, o, lse, do, dq_in, name):
    s, tq, tk, dk, tb, nsub = cfg.s, cfg.tq, cfg.tk, cfg.dk, cfg.tb, cfg.nsub
    has_sel = sel is not None
    has_in = dq_in is not None
    nkp = len(ks)

    def kern(*refs):
        refs = list(refs)
        q_ref, k_refs, v_ref = refs[0], refs[1:1 + nkp], refs[1 + nkp]
        p0 = 2 + nkp
        sel_ref = refs[p0] if has_sel else None
        p0 += has_sel
        o_ref, l_ref, do_ref = refs[p0:p0 + 3]
        p0 += 3
        in_ref = refs[p0] if has_in else None
        p0 += has_in
        dq_ref, dr_ref = refs[p0:p0 + 2]
        sa, pa, sb, pb = refs[-4:]
        h, g = pl.program_id(0), pl.program_id(1)

        def block(b):
            rows = [slice(b * tq + r * tb, b * tq + (r + 1) * tb) for r in range(nsub)]
            qs = [q_ref[p, :].astype(BF16) for p in rows]
            ts = [(2 * g + b) * tq + r * tb + lax.broadcasted_iota(jnp.int32, (tb, 1), 0) for r in range(nsub)]
            sels = [sel_ref[p, :].astype(BF16) for p in rows] if has_sel else None
            dvecs, dobs, lses = [], [], []
            for r, p in enumerate(rows):
                dov = do_ref[p, :]
                dvec = jnp.sum(dov * o_ref[p, :], axis=1, keepdims=True)
                dr_ref[0, b * nsub + r] = _to_row(dvec)
                dvecs.append(dvec)
                dobs.append(dov.astype(BF16))
                lses.append(l_ref[0, p, :])
            return rows, qs, ts, sels, dvecs, dobs, lses

        def products_into(sbuf, pbuf, blk, c):
            rows = pl.ds(pl.multiple_of(c * tk, tk), tk)
            kk, vv = _load_keys(k_refs, rows), v_ref[rows, :].astype(BF16)
            for r in range(nsub):
                sbuf[r] = _nt(blk[1][r], kk)
                pbuf[r] = _nt(blk[5][r], vv)

        def consume(sbuf, pbuf, blk, c, accs, diag):
            _, _, ts, sels, dvecs, _, lses = blk
            k0 = pl.multiple_of(c * tk, tk)
            kk = _load_keys(k_refs, pl.ds(k0, tk))
            emat = _block_of_key(k0, tk, False) if has_sel else None
            n = k0 + lax.broadcasted_iota(jnp.int32, (1, tk), 1)
            new = []
            for r in range(nsub):
                selx = _nn(sels[r], emat) if has_sel else None
                p, _ = _scores(cfg, sbuf[r], ts[r], n, h, selx, diag, lses[r])
                ds = p * (pbuf[r] - dvecs[r])
                new.append(accs[r] + _nn(ds.astype(BF16), kk))
            return tuple(new)

        def finish(blk, accs):
            for r, p in enumerate(blk[0]):
                dq_ref[p, :] = accs[r] * cfg.scale + in_ref[p, :] if has_in else accs[r] * cfg.scale

        def pairs(blk, first, other):
            def pair(p, ac):
                products_into(*other, blk, 2 * p + 1)
                ac = consume(*first, blk, 2 * p, ac, False)
                products_into(*first, blk, 2 * p + 2)
                return consume(*other, blk, 2 * p + 1, ac, False)
            return pair

        zero = (jnp.zeros((tb, dk), F32),) * nsub
        buf_a, buf_b = (sa, pa), (sb, pb)
        blk_a, blk_b = block(0), block(1)
        products_into(*buf_a, blk_a, 0)
        accs = lax.fori_loop(0, g, pairs(blk_a, buf_a, buf_b), zero)
        products_into(*buf_b, blk_b, 0)
        finish(blk_a, consume(*buf_a, blk_a, 2 * g, accs, True))
        accs = lax.fori_loop(0, g, pairs(blk_b, buf_b, buf_a), zero)
        products_into(*buf_a, blk_b, 2 * g + 1)
        accs = consume(*buf_b, blk_b, 2 * g, accs, False)
        finish(blk_b, consume(*buf_a, blk_b, 2 * g + 1, accs, True))

    assert cfg.causal and tq == tk and cfg.nq % 2 == 0
    qs = pl.BlockSpec((2 * tq, dk), lambda h, g: (g, h))
    ins = [pl.BlockSpec((2 * tq, q.width), lambda h, g: (g, q.col(h)))]
    ins += [pl.BlockSpec((cfg.sk, p.width), lambda h, g, p=p: (0, p.col(h))) for p in ks]
    ins += [pl.BlockSpec((cfg.sk, HEAD_V), lambda h, g: (0, v.col(h)))]
    args = [q.arr] + [p.arr for p in ks] + [v.arr]
    if has_sel:
        ins.append(pl.BlockSpec((2 * tq, LANE), lambda h, g: (g, 0)))
        args.append(sel)
    ins += [pl.BlockSpec((2 * tq, HEAD_V), lambda h, g: (g, o.col(h))),
            pl.BlockSpec((1, 2 * tq, 1), lambda h, g: (h, g, 0)),
            pl.BlockSpec((2 * tq, HEAD_V), lambda h, g: (g, do.col(h)))]
    args += [o.arr, lse, do.arr]
    if has_in:
        ins.append(qs)
        args.append(dq_in)
    return _pcall(
        kern, name=name, grid=(cfg.h, cfg.nq // 2), in_specs=ins,
        out_specs=[qs, pl.BlockSpec((1, 2 * nsub, 1, tb), lambda h, g: (h, g, 0, 0))],
        out_shape=[jax.ShapeDtypeStruct((s, cfg.h * dk), F32),
                   jax.ShapeDtypeStruct((cfg.h, cfg.nb, 1, tb), F32)],
        scratch_shapes=[pltpu.VMEM((nsub, tb, tk), F32)] * 4,
        compiler_params=pltpu.CompilerParams(dimension_semantics=("parallel", "parallel")),
    )(*args)


def _attn_dkv(cfg, q, ks, v, selt, lse_r, d_r, do, name):
    s, tq, tkb, dk, kb, ksub = cfg.s, cfg.tb, cfg.tkb, cfg.dk, cfg.kb, cfg.ksub
    nq = cfg.nb
    has_sel = selt is not None
    nkp = len(ks)
    outs = list(ks) + [v]

    def kern(*refs):
        k_refs, v_ref = refs[:nkp], refs[nkp]
        q_ref, do_ref, lr_ref, dr_ref = refs[nkp + 1:nkp + 5]
        st_ref = refs[nkp + 5] if has_sel else None
        out_refs = refs[nkp + 5 + has_sel:2 * nkp + 6 + has_sel]
        sa, pa, sb, pb = refs[-4:]
        j, h = pl.program_id(0), pl.program_id(1)
        k0 = j * tkb
        part = [slice(u * kb, (u + 1) * kb) for u in range(ksub)]
        kks = [_load_keys(k_refs, p) for p in part]
        vvs = [v_ref[p, :].astype(BF16) for p in part]
        ns = [k0 + u * kb + lax.broadcasted_iota(jnp.int32, (kb, 1), 0) for u in range(ksub)]
        emats = [_block_of_key(k0 + u * kb, kb, True) for u in range(ksub)] if has_sel else None

        def load_q(i):
            rows = pl.ds(pl.multiple_of(i * tq, tq), tq)
            return q_ref[rows, :].astype(BF16), do_ref[rows, :].astype(BF16)

        def products_into(sbuf, pbuf, i):
            qi, doi = load_q(i)
            for u in range(ksub):
                sbuf[u] = _nt(kks[u], qi)
                pbuf[u] = _nt(vvs[u], doi)

        def consume(sbuf, pbuf, i, carry):
            qi, doi = load_q(i)
            t = i * tq + lax.broadcasted_iota(jnp.int32, (1, tq), 1)
            selt_i = st_ref[i].astype(BF16) if has_sel else None
            new = []
            for u in range(ksub):
                dk_acc, dv_acc = carry[u]
                selx = _nn(emats[u], selt_i) if has_sel else None
                pt, _ = _scores(cfg, sbuf[u], t, ns[u], h, selx, True, lr_ref[0, i])
                dv_acc = dv_acc + _nn(pt.astype(BF16), doi)
                dst = pt * (pbuf[u] - dr_ref[0, i])
                new.append((dk_acc + _nn(dst.astype(BF16), qi), dv_acc))
            return tuple(new)

        if cfg.causal:
            first, count = k0 // tq, nq - k0 // tq
        elif cfg.mode == "win":
            first = k0 // tq
            count = jnp.minimum((k0 + tkb + WIN - 2) // tq + 1, nq) - first
        else:
            first, count = 0, nq

        def pair(p, cr):
            i0 = first + 2 * p
            products_into(sb, pb, i0 + 1)
            cr = consume(sa, pa, i0, cr)
            products_into(sa, pa, i0 + 2)
            return consume(sb, pb, i0 + 1, cr)

        carry = ((jnp.zeros((kb, dk), F32), jnp.zeros((kb, HEAD_V), F32)),) * ksub
        products_into(sa, pa, first)
        carry = lax.fori_loop(0, count // 2 - 1, pair, carry)
        last = first + count - 2
        products_into(sb, pb, last + 1)
        carry = consume(sa, pa, last, carry)
        carry = consume(sb, pb, last + 1, carry)
        for u, (dk_acc, dv_acc) in enumerate(carry):
            vals, off = [], 0
            for p in ks:
                vals.append(dk_acc[:, off:off + p.width] * cfg.scale)
                off += p.width
            vals.append(dv_acc)
            for src, ref, val in zip(outs, out_refs, vals):
                if src.per_head:
                    ref[part[u], :] = val
                else:
                    @pl.when(h == 0)
                    def _(ref=ref, val=val, u=u):
                        ref[part[u], :] = val

                    @pl.when(h > 0)
                    def _(ref=ref, val=val, u=u):
                        ref[part[u], :] += val

    rowv = pl.BlockSpec((1, nq, 1, tq), lambda j, h: (h, 0, 0, 0))
    ins = [pl.BlockSpec((tkb, p.width), lambda j, h, p=p: (j, p.col(h))) for p in ks]
    ins += [pl.BlockSpec((tkb, HEAD_V), lambda j, h: (j, v.col(h))),
            pl.BlockSpec((s, q.width), lambda j, h: (0, q.col(h))),
            pl.BlockSpec((s, HEAD_V), lambda j, h: (0, do.col(h))), rowv, rowv]
    args = [p.arr for p in ks] + [v.arr, q.arr, do.arr, lse_r, d_r]
    if has_sel:
        ins.append(pl.BlockSpec((nq, LANE, tq), lambda j, h: (0, 0, 0)))
        args.append(selt)
    out_specs = [pl.BlockSpec((tkb, p.width), lambda j, h, p=p: (j, h if p.per_head else 0)) for p in outs]
    out_shape = [jax.ShapeDtypeStruct((cfg.sk, (cfg.h if p.per_head else 1) * p.width), F32) for p in outs]
    assert nq % 2 == 0 and (cfg.mode in ("cmp", "mem") or tkb % (2 * tq) == 0), (nq, tkb, tq)
    return _pcall(
        kern, name=name, grid=(cfg.sk // tkb, cfg.h), in_specs=ins, out_specs=out_specs, out_shape=out_shape,
        scratch_shapes=[pltpu.VMEM((ksub, kb, tq), F32)] * 4,
        compiler_params=pltpu.CompilerParams(dimension_semantics=("parallel", "arbitrary")),
    )(*args)


def _attn_dkv_flat(cfg, q, ks, v, selt, lse_r, d_r, do, name):
    s, tq, tkb, dk, kb, ksub = cfg.s, cfg.tb, cfg.tkb, cfg.dk, cfg.kb, cfg.ksub
    nq = cfg.nb
    has_sel = selt is not None
    nkp = len(ks)
    outs = list(ks) + [v]
    assert nq % 2 == 0 and tkb % (2 * tq) == 0, (nq, tkb, tq)
    steps = []
    for j in range(cfg.sk // tkb):
        first = j * tkb // tq
        stop = nq if cfg.causal else min((j * tkb + tkb + WIN - 2) // tq + 1, nq)
        steps += [(j, i0) for i0 in range(first, stop, 2)]
    n_pairs = len(steps)
    steps.append(steps[-1])
    tab_j = jnp.asarray(np.array([p[0] for p in steps], np.int32))
    tab_i = jnp.asarray(np.array([p[1] for p in steps], np.int32))

    def kern(tj_ref, ti_ref, *refs):
        k_refs, v_ref = refs[:nkp], refs[nkp]
        q_ref, do_ref, lr_ref, dr_ref = refs[nkp + 1:nkp + 5]
        st_ref = refs[nkp + 5] if has_sel else None
        out_refs = refs[nkp + 5 + has_sel:2 * nkp + 6 + has_sel]
        sa, pa, sb, pb = refs[-4:]
        h = pl.program_id(0)
        for src, ref in zip(outs, out_refs):
            if src.per_head:
                ref[...] = jnp.zeros_like(ref)
            else:
                @pl.when(h == 0)
                def _(ref=ref):
                    ref[...] = jnp.zeros_like(ref)

        def key_rows(j, u):
            return pl.ds(pl.multiple_of(j * tkb + u * kb, kb), kb)

        def load_q(i):
            rows = pl.ds(pl.multiple_of(i * tq, tq), tq)
            return q_ref[rows, :].astype(BF16), do_ref[rows, :].astype(BF16)

        def products_into(sbuf, pbuf, j, i):
            qi, doi = load_q(i)
            for u in range(ksub):
                rows = key_rows(j, u)
                sbuf[u] = _nt(_load_keys(k_refs, rows), qi)
                pbuf[u] = _nt(v_ref[rows, :].astype(BF16), doi)

        def consume(sbuf, pbuf, j, i):
            qi, doi = load_q(i)
            t = i * tq + lax.broadcasted_iota(jnp.int32, (1, tq), 1)
            selt_i = st_ref[i].astype(BF16) if has_sel else None
            res = []
            for u in range(ksub):
                k0 = j * tkb + u * kb
                n = k0 + lax.broadcasted_iota(jnp.int32, (kb, 1), 0)
                selx = _nn(_block_of_key(k0, kb, True), selt_i) if has_sel else None
                pt, _ = _scores(cfg, sbuf[u], t, n, h, selx, True, lr_ref[0, i])
                dst = pt * (pbuf[u] - dr_ref[0, i])
                res.append((_nn(dst.astype(BF16), qi), _nn(pt.astype(BF16), doi)))
            return res

        def pair(p, carry):
            j, i0 = tj_ref[p], ti_ref[p]
            products_into(sb, pb, j, i0 + 1)
            ca = consume(sa, pa, j, i0)
            products_into(sa, pa, tj_ref[p + 1], ti_ref[p + 1])
            cb = consume(sb, pb, j, i0 + 1)
            for u in range(ksub):
                rows = key_rows(j, u)
                dk_c = (ca[u][0] + cb[u][0]) * cfg.scale
                off = 0
                for src, ref in zip(ks, out_refs):
                    ref[rows, :] += dk_c[:, off:off + src.width]
                    off += src.width
                out_refs[nkp][rows, :] += ca[u][1] + cb[u][1]
            return carry

        products_into(sa, pa, tj_ref[0], ti_ref[0])
        lax.fori_loop(0, n_pairs, pair, 0)

    rowv = pl.BlockSpec((1, nq, 1, tq), lambda h, tj, ti: (h, 0, 0, 0))
    ins = [pl.BlockSpec((cfg.sk, p.width), lambda h, tj, ti, p=p: (0, p.col(h))) for p in ks]
    ins += [pl.BlockSpec((cfg.sk, HEAD_V), lambda h, tj, ti: (0, v.col(h))),
            pl.BlockSpec((s, q.width), lambda h, tj, ti: (0, q.col(h))),
            pl.BlockSpec((s, HEAD_V), lambda h, tj, ti: (0, do.col(h))), rowv, rowv]
    args = [p.arr for p in ks] + [v.arr, q.arr, do.arr, lse_r, d_r]
    if has_sel:
        ins.append(pl.BlockSpec((nq, LANE, tq), lambda h, tj, ti: (0, 0, 0)))
        args.append(selt)
    out_specs = [pl.BlockSpec((cfg.sk, p.width), lambda h, tj, ti, p=p: (0, h if p.per_head else 0))
                 for p in outs]
    out_shape = [jax.ShapeDtypeStruct((cfg.sk, (cfg.h if p.per_head else 1) * p.width), F32) for p in outs]
    grid_spec = pltpu.PrefetchScalarGridSpec(
        num_scalar_prefetch=2, grid=(cfg.h,), in_specs=ins, out_specs=out_specs,
        scratch_shapes=[pltpu.VMEM((ksub, kb, tq), F32)] * 4)
    return _pcall(kern, name=name, grid_spec=grid_spec, out_shape=out_shape,
                  compiler_params=pltpu.CompilerParams(dimension_semantics=("arbitrary",)))(tab_j, tab_i, *args)


def _all_heads(cfg, src, rows, key=False):
    if src.per_head:
        assert src.col0 % cfg.h == 0
        width, col = cfg.h * src.width, src.col0 // cfg.h
    else:
        width, col = src.width, src.col0
    return pl.BlockSpec((rows, width), (lambda i: (0, col)) if key else (lambda i: (i, col)))


def _head_cols(src, hh):
    return slice(hh * src.width, (hh + 1) * src.width) if src.per_head else slice(None)


def _key_window(cfg, i, r):
    if cfg.mode == "win":
        return pl.ds(pl.multiple_of(jnp.maximum(i * cfg.tq + r * cfg.tb - WIN, 0), cfg.tb), cfg.tk)
    return pl.ds(0, cfg.tk)


def _attn_fwd_small(cfg, q, ks, v, name, overlap=None):
    s, tq, tk, tb, nsub, nh = cfg.s, cfg.tq, cfg.tk, cfg.tb, cfg.nsub, cfg.h
    nkp = len(ks)
    select = overlap is not None
    n_s = s // SLC_LEN
    top_n = min(SLC_TOPN, n_s)

    def kern(*refs):
        q_ref, k_refs, v_ref = refs[0], refs[1:1 + nkp], refs[1 + nkp]
        ov_ref = refs[2 + nkp] if select else None
        o_ref, lc_ref, lr_ref = refs[2 + nkp + select:5 + nkp + select]
        i = pl.program_id(0)
        imps = [jnp.zeros((tb, LANE), F32)] * nsub
        for r in range(nsub):
            rows = slice(r * tb, (r + 1) * tb)
            t = i * tq + r * tb + lax.broadcasted_iota(jnp.int32, (tb, 1), 0)
            win = _key_window(cfg, i, r)
            n = win.start + lax.broadcasted_iota(jnp.int32, (1, tk), 1)
            for hh in range(nh):
                qv = q_ref[rows, hh * cfg.dk:(hh + 1) * cfg.dk].astype(BF16)
                kk = _load_keys([kr.at[:, _head_cols(p, hh)] for kr, p in zip(k_refs, ks)], win)
                vv = v_ref[win, _head_cols(v, hh)].astype(BF16)
                sc, mask = _scores(cfg, _nt(qv, kk), t, n, hh, None, True)
                m = jnp.max(sc, axis=1, keepdims=True)
                e = jnp.exp2(sc - m)
                if mask is not None:
                    e = jnp.where(mask, e, 0.0)
                l = jnp.sum(e, axis=1, keepdims=True)
                o_ref[rows, hh * HEAD_V:(hh + 1) * HEAD_V] = _nn(e.astype(BF16), vv) / (l + 1e-20)
                lse = m + jnp.log(l + 1e-20) * LOG2E
                lc_ref[hh, rows, :] = lse
                lr_ref[hh, r] = _to_row(lse)
                if select:
                    imps[r] = imps[r] + _nn((e / (l + 1e-20)).astype(BF16), ov_ref[...])
        if select:
            sel_ref, selt_ref, imp_t = refs[5 + nkp + select:8 + nkp + select]
            for r in range(nsub):
                t = i * tq + r * tb + lax.broadcasted_iota(jnp.int32, (tb, 1), 0)
                j = lax.broadcasted_iota(jnp.int32, (tb, LANE), 1)
                cur = t >> 6
                imp = jnp.where((j == 0) | (j == cur) | (j == cur - 1), 1e9, imps[r])
                imp = jnp.where(j > cur, -1e9, imp)
                imp_t[r] = jnp.transpose(imp)
                mine = imp_t[r, 0:n_s, :]
                jrow = lax.broadcasted_iota(jnp.int32, (n_s, tb), 0)

                def count(k, rank):
                    other = imp_t[r, pl.ds(k, 1), :]
                    ahead = (other > mine) | ((other == mine) & (k < jrow))
                    return rank + jnp.where(ahead, 1.0, 0.0)

                rank = lax.fori_loop(0, n_s, count, jnp.zeros((n_s, tb), F32))
                cur_t = (i * tq + r * tb + lax.broadcasted_iota(jnp.int32, (1, tb), 1)) >> 6
                rejected = jnp.where((rank < top_n) & (jrow <= cur_t), 0.0, 1.0)
                if n_s < LANE:
                    rejected = jnp.concatenate([rejected, jnp.ones((LANE - n_s, tb), F32)], axis=0)
                selt_ref[r] = rejected
                sel_ref[r * tb:(r + 1) * tb, :] = jnp.transpose(rejected)

    ins = [_all_heads(cfg, q, tq)] + [_all_heads(cfg, p, cfg.sk, True) for p in ks]
    ins += [_all_heads(cfg, v, cfg.sk, True)]
    args = [q.arr] + [p.arr for p in ks] + [v.arr]
    out_specs = [pl.BlockSpec((tq, nh * HEAD_V), lambda i: (i, 0)),
                 pl.BlockSpec((nh, tq, 1), lambda i: (0, i, 0)),
                 pl.BlockSpec((nh, nsub, 1, tb), lambda i: (0, i, 0, 0))]
    out_shape = [jax.ShapeDtypeStruct((s, nh * HEAD_V), F32), jax.ShapeDtypeStruct((nh, s, 1), F32),
                 jax.ShapeDtypeStruct((nh, cfg.nb, 1, tb), F32)]
    scratch = []
    if select:
        ins.append(pl.BlockSpec((cfg.sk, LANE), lambda i: (0, 0)))
        args.append(overlap)
        out_specs += [pl.BlockSpec((tq, LANE), lambda i: (i, 0)), pl.BlockSpec((nsub, LANE, tb), lambda i: (i, 0, 0))]
        out_shape += [jax.ShapeDtypeStruct((s, LANE), F32), jax.ShapeDtypeStruct((cfg.nb, LANE, tb), F32)]
        scratch = [pltpu.VMEM((nsub, LANE, tb), F32)]
    return _pcall(kern, name=name, grid=(cfg.nq,), in_specs=ins, out_specs=out_specs, out_shape=out_shape,
                  scratch_shapes=scratch,
                  compiler_params=pltpu.CompilerParams(dimension_semantics=("parallel",)))(*args)


def _attn_dq_small(cfg, q, ks, v, o, lse, do, dq_in, name):
    s, tq, tk, tb, nsub, nh, dk = cfg.s, cfg.tq, cfg.tk, cfg.tb, cfg.nsub, cfg.h, cfg.dk
    nkp = len(ks)
    has_in = dq_in is not None

    def kern(*refs):
        q_ref, k_refs, v_ref = refs[0], refs[1:1 + nkp], refs[1 + nkp]
        o_ref, l_ref, do_ref = refs[2 + nkp:5 + nkp]
        in_ref = refs[5 + nkp] if has_in else None
        dq_ref, dr_ref = refs[5 + nkp + has_in:7 + nkp + has_in]
        i = pl.program_id(0)
        for r in range(nsub):
            rows = slice(r * tb, (r + 1) * tb)
            t = i * tq + r * tb + lax.broadcasted_iota(jnp.int32, (tb, 1), 0)
            win = _key_window(cfg, i, r)
            n = win.start + lax.broadcasted_iota(jnp.int32, (1, tk), 1)
            for hh in range(nh):
                vcols = slice(hh * HEAD_V, (hh + 1) * HEAD_V)
                qcols = slice(hh * dk, (hh + 1) * dk)
                qv = q_ref[rows, qcols].astype(BF16)
                kk = _load_keys([kr.at[:, _head_cols(p, hh)] for kr, p in zip(k_refs, ks)], win)
                vv = v_ref[win, _head_cols(v, hh)].astype(BF16)
                dov = do_ref[rows, vcols]
                dvec = jnp.sum(dov * o_ref[rows, vcols], axis=1, keepdims=True)
                dr_ref[hh, r] = _to_row(dvec)
                p, _ = _scores(cfg, _nt(qv, kk), t, n, hh, None, True, l_ref[hh, rows, :])
                ds = p * (_nt(dov.astype(BF16), vv) - dvec)
                dq = _nn(ds.astype(BF16), kk) * cfg.scale
                dq_ref[rows, qcols] = dq + in_ref[rows, qcols] if has_in else dq

    qs = pl.BlockSpec((tq, nh * dk), lambda i: (i, 0))
    ins = [_all_heads(cfg, q, tq)] + [_all_heads(cfg, p, cfg.sk, True) for p in ks]
    ins += [_all_heads(cfg, v, cfg.sk, True)]
    ins += [_all_heads(cfg, o, tq), pl.BlockSpec((nh, tq, 1), lambda i: (0, i, 0)), _all_heads(cfg, do, tq)]
    args = [q.arr] + [p.arr for p in ks] + [v.arr, o.arr, lse, do.arr]
    if has_in:
        ins.append(qs)
        args.append(dq_in)
    return _pcall(
        kern, name=name, grid=(cfg.nq,), in_specs=ins,
        out_specs=[qs, pl.BlockSpec((nh, nsub, 1, tb), lambda i: (0, i, 0, 0))],
        out_shape=[jax.ShapeDtypeStruct((s, nh * dk), F32), jax.ShapeDtypeStruct((nh, cfg.nb, 1, tb), F32)],
        compiler_params=pltpu.CompilerParams(dimension_semantics=("parallel",)))(*args)


def _attn_bwd(cfg, q, ks, v, sel, selt, o, lse, lse_r, do, dq_in, name):
    if cfg.causal:
        dq, d_r = _attn_dq(cfg, q, ks, v, sel, o, lse, do, dq_in, name + "_dq")
    else:
        dq, d_r = _attn_dq_small(cfg, q, ks, v, o, lse, do, dq_in, name + "_dq")
    dkv = _attn_dkv_flat if cfg.causal or cfg.mode == "win" else _attn_dkv
    res = dkv(cfg, q, ks, v, selt, lse_r, d_r, do, name + "_dkv")
    return dq, res[:-1], res[-1]


def _silu_grad(pre):
    sg = _sigmoid(pre)
    return sg * (1.0 + pre * (1.0 - sg))


def _compress_fwd(a_lo, a_hi, pe_lo, pe_hi, w1_lo, w1_hi, w2, name):
    n, dp = a_lo.shape[0], w2.shape[1]

    def kern(alo, ahi, plo, phi, w1l, w1h, w2r, out_ref, pre_ref):
        xl = (alo[...] + plo[...]).astype(BF16)
        xh = (ahi[...] + phi[...]).astype(BF16)
        pre = _nn(xl, w1l[...]) + _nn(xh, w1h[...])
        act = pre * _sigmoid(pre)
        out_ref[...] = _nn(act.astype(BF16), w2r[...]).astype(BF16)
        pre_ref[...] = pre

    return _pcall(kern, name=name,
                  out_shape=[jax.ShapeDtypeStruct((n, dp), BF16), jax.ShapeDtypeStruct((n, dp), F32)],
                  )(a_lo, a_hi, pe_lo, pe_hi, w1_lo, w1_hi, w2)


def _compress_bwd(a_lo, a_hi, pe_lo, pe_hi, w1_lo, w1_hi, w2, pre, pre_sh, dout, dout_sh, name):
    n, ln = a_lo.shape
    dp = w2.shape[1]

    def kern(alo, ahi, plo, phi, w1l, w1h, w2r, pre_ref, presh_ref, do_ref, dosh_ref,
             da_ref, dpl_ref, dph_ref, dw1l_ref, dw1h_ref, dw2_ref):
        prev = pre_ref[...]
        act = prev * _sigmoid(prev)
        dob = do_ref[...].astype(BF16)
        w2v = w2r[...]
        dpre = (_nt(dob, w2v) * _silu_grad(prev)).astype(BF16)
        dpre_sh = (_nt(dosh_ref[...].astype(BF16), w2v) * _silu_grad(presh_ref[...])).astype(BF16)
        dw2_ref[...] = _nn(act.T.astype(BF16), dob)
        xl = alo[...] + plo[...]
        xh = ahi[...] + phi[...]
        dw1l_ref[...] = _nn(xl.T.astype(BF16), dpre)
        dw1h_ref[...] = _nn(xh.T.astype(BF16), dpre)
        dal = _nt(dpre, w1l[...])
        dah_sh = _nt(dpre_sh, w1h[...])
        da_ref[...] = dal + dah_sh
        dpl_ref[...] = jnp.sum(dal, axis=0, keepdims=True)
        dph_ref[...] = jnp.sum(dah_sh, axis=0, keepdims=True)

    return _pcall(
        kern, name=name,
        out_shape=[jax.ShapeDtypeStruct((n, ln), F32), jax.ShapeDtypeStruct((1, ln), F32),
                   jax.ShapeDtypeStruct((1, ln), F32), jax.ShapeDtypeStruct((ln, dp), F32),
                   jax.ShapeDtypeStruct((ln, dp), F32), jax.ShapeDtypeStruct((dp, dp), F32)],
    )(a_lo, a_hi, pe_lo, pe_hi, w1_lo, w1_hi, w2, pre, pre_sh, dout, dout_sh)


def _nsa_combine(o_cmp, o_slc, o_win, gl):
    s, w = o_cmp.shape
    tr = _tile(s, 512)

    def kern(a_ref, b_ref, c_ref, g_ref, o_ref):
        g = _sigmoid(g_ref[...])
        for h in range(NSA_HEADS):
            cs = slice(h * HEAD_V, (h + 1) * HEAD_V)
            o_ref[:, cs] = (g[:, 3 * h:3 * h + 1] * a_ref[:, cs] + g[:, 3 * h + 1:3 * h + 2] * b_ref[:, cs]
                            + g[:, 3 * h + 2:3 * h + 3] * c_ref[:, cs])

    row = pl.BlockSpec((tr, w), lambda i: (i, 0))
    return _pcall(kern, name="nsa_combine", grid=(s // tr,),
                  in_specs=[row, row, row, pl.BlockSpec((tr, LANE), lambda i: (i, gl.col0))], out_specs=row,
                  out_shape=jax.ShapeDtypeStruct((s, w), F32))(o_cmp, o_slc, o_win, gl.arr)


def _nsa_combine_bwd(do_cat, o_cmp, o_slc, o_win, gl):
    s, w = o_cmp.shape
    tr = _tile(s, 512)

    def kern(d_ref, a_ref, b_ref, c_ref, g_ref, da_ref, db_ref, dc_ref, dg_ref):
        g = _sigmoid(g_ref[...])
        lane = lax.broadcasted_iota(jnp.int32, (tr, LANE), 1)
        dgl = jnp.zeros((tr, LANE), F32)
        for h in range(NSA_HEADS):
            cs = slice(h * HEAD_V, (h + 1) * HEAD_V)
            dv = d_ref[:, cs]
            for b, (src, dst) in enumerate(((a_ref, da_ref), (b_ref, db_ref), (c_ref, dc_ref))):
                gate = g[:, 3 * h + b:3 * h + b + 1]
                dst[:, cs] = gate * dv
                dgate = jnp.sum(dv * src[:, cs], axis=1, keepdims=True)
                dgl = jnp.where(lane == 3 * h + b, dgate * gate * (1.0 - gate), dgl)
        dg_ref[...] = dgl

    row = pl.BlockSpec((tr, w), lambda i: (i, 0))
    tab = pl.BlockSpec((tr, LANE), lambda i: (i, 0))
    return _pcall(kern, name="nsa_combine_bwd", grid=(s // tr,),
                  in_specs=[pl.BlockSpec((tr, w), lambda i: (i, 2)), row, row, row,
                            pl.BlockSpec((tr, LANE), lambda i: (i, gl.col0))],
                  out_specs=[row, row, row, tab],
                  out_shape=[jax.ShapeDtypeStruct((s, w), F32)] * 3 + [jax.ShapeDtypeStruct((s, LANE), F32)],
                  )(do_cat, o_cmp, o_slc, o_win, gl.arr)


def _gate_fwd(o_mla, o_nsa, o_mem, hp):
    s = o_mla.shape[0]
    tr = _tile(s, 256)

    def kern(a_ref, b_ref, c_ref, z_ref, u_ref):
        z = z_ref[...]
        sz = z * _sigmoid(z)
        u_ref[:, 0:1024] = (a_ref[...] * sz[:, 0:1024]).astype(BF16)
        u_ref[:, 1024:1536] = (b_ref[...] * sz[:, 1024:1536]).astype(BF16)
        u_ref[:, 1536:2048] = (c_ref[...] * sz[:, 1536:2048]).astype(BF16)

    return _pcall(
        kern, name="gate_fwd", grid=(s // tr,),
        in_specs=[pl.BlockSpec((tr, 1024), lambda i: (i, 0)), pl.BlockSpec((tr, 512), lambda i: (i, 0)),
                  pl.BlockSpec((tr, 512), lambda i: (i, 0)), pl.BlockSpec((tr, 2048), lambda i: (i, 2))],
        out_specs=pl.BlockSpec((tr, 2048), lambda i: (i, 0)),
        out_shape=jax.ShapeDtypeStruct((s, 2048), BF16))(o_mla, o_nsa, o_mem, hp)


def _gate_bwd(du, o_mla, o_nsa, o_mem, hp):
    s = du.shape[0]
    tr = _tile(s, 256)

    def kern(d_ref, a_ref, b_ref, c_ref, z_ref, do_ref, dz_ref):
        z = z_ref[...]
        sg = _sigmoid(z)
        sz = z * sg
        dsz = sg * (1.0 + z * (1.0 - sg))
        d = d_ref[...]
        do_ref[...] = d * sz
        dz_ref[:, 0:1024] = d[:, 0:1024] * a_ref[...] * dsz[:, 0:1024]
        dz_ref[:, 1024:1536] = d[:, 1024:1536] * b_ref[...] * dsz[:, 1024:1536]
        dz_ref[:, 1536:2048] = d[:, 1536:2048] * c_ref[...] * dsz[:, 1536:2048]

    wide = pl.BlockSpec((tr, 2048), lambda i: (i, 0))
    return _pcall(
        kern, name="gate_bwd", grid=(s // tr,),
        in_specs=[wide, pl.BlockSpec((tr, 1024), lambda i: (i, 0)), pl.BlockSpec((tr, 512), lambda i: (i, 0)),
                  pl.BlockSpec((tr, 512), lambda i: (i, 0)), pl.BlockSpec((tr, 2048), lambda i: (i, 2))],
        out_specs=[wide, wide],
        out_shape=[jax.ShapeDtypeStruct((s, 2048), F32)] * 2)(du, o_mla, o_nsa, o_mem, hp)


def _tile2d(rows, cols, arrays):
    if rows % 16 == 0:
        return _row_tile(rows, cols * arrays), cols
    want = max(LANE, BLOCK_BYTES // (rows * 4 * arrays) // LANE * LANE)
    tc = LANE
    for t in range(LANE, cols + 1, LANE):
        if cols % t == 0 and t <= want:
            tc = t
    return rows, tc


def _sum_slots(buf, name):
    n, rows, cols = buf.shape
    tr, tc = _tile2d(rows, cols, n)

    def kern(b_ref, o_ref):
        acc = b_ref[0].astype(F32)
        for i in range(1, n):
            acc = acc + b_ref[i].astype(F32)
        o_ref[...] = acc

    return _pcall(kern, name=name, grid=(rows // tr, cols // tc),
                  in_specs=[pl.BlockSpec((n, tr, tc), lambda i, j: (0, i, j))],
                  out_specs=pl.BlockSpec((tr, tc), lambda i, j: (i, j)),
                  out_shape=jax.ShapeDtypeStruct((rows, cols), F32))(buf)


def _chip_sum(buf, core, axis, name):
    n, rows, cols = buf.shape
    tr, tc = _tile2d(rows, cols, n)
    nbr, nbc = rows // tr, cols // tc

    def kern(c_ref, b_ref, o_ref, w_ref):
        acc = b_ref[0].astype(F32)
        for i in range(1, n):
            acc = acc + b_ref[i].astype(F32)
        o_ref[...] = acc
        w_ref[...] = acc

    place = ((lambda i, j, c: (c[0] * nbr + i, j)) if axis == 0 else (lambda i, j, c: (i, c[0] * nbc + j)))
    whole = (2 * rows, cols) if axis == 0 else (rows, 2 * cols)
    grid_spec = pltpu.PrefetchScalarGridSpec(
        num_scalar_prefetch=1, grid=(nbr, nbc),
        in_specs=[pl.BlockSpec((n, tr, tc), lambda i, j, c: (0, i, j))],
        out_specs=[pl.BlockSpec((tr, tc), lambda i, j, c: (i, j)), pl.BlockSpec((tr, tc), place)])
    return _pcall(kern, name=name, grid_spec=grid_spec,
                  out_shape=[jax.ShapeDtypeStruct((rows, cols), F32), jax.ShapeDtypeStruct(whole, F32)])(core, buf)


def _pair_sum(g4, theirs, core, axis, name):
    n, rows, cols = theirs.shape
    tr, tc = _tile2d(rows, cols, 1)
    nbr, nbc = rows // tr, cols // tc

    def kern(c_ref, a_ref, b_ref, o_ref):
        o_ref[...] = (a_ref[...] + b_ref[...]).astype(BF16)

    blk = (1, tr, tc)
    mine = ((lambda s, i, j, c: (s, c[0] * nbr + i, j)) if axis == 0
            else (lambda s, i, j, c: (s, i, c[0] * nbc + j)))
    grid_spec = pltpu.PrefetchScalarGridSpec(
        num_scalar_prefetch=1, grid=(n, nbr, nbc),
        in_specs=[pl.BlockSpec(blk, mine), pl.BlockSpec(blk, lambda s, i, j, c: (s, i, j))],
        out_specs=pl.BlockSpec(blk, lambda s, i, j, c: (s, i, j)))
    return _pcall(kern, name=name, grid_spec=grid_spec,
                  out_shape=jax.ShapeDtypeStruct((n, rows, cols), BF16))(core, g4, theirs)


def _adamw(w, g, m, v, name):
    rows, cols = w.shape
    tr, tc = _tile2d(rows, cols, 4)
    bc1 = 1.0 - ADAM_B1 ** ADAM_STEP
    bc2 = 1.0 - ADAM_B2 ** ADAM_STEP

    def kern(w_ref, g_ref, m_ref, v_ref, d_ref, mo_ref, vo_ref):
        gv = g_ref[...]
        mn = ADAM_B1 * m_ref[...] + (1.0 - ADAM_B1) * gv
        vn = ADAM_B2 * v_ref[...] + (1.0 - ADAM_B2) * (gv * gv)
        d_ref[...] = -ADAM_LR * ((mn / bc1) / (jnp.sqrt(vn / bc2) + ADAM_EPS) + ADAM_WD * w_ref[...])
        mo_ref[...] = mn
        vo_ref[...] = vn

    blk = pl.BlockSpec((tr, tc), lambda i, j: (i, j))
    return _pcall(kern, name=name, grid=(rows // tr, cols // tc), in_specs=[blk] * 4, out_specs=[blk] * 3,
                  out_shape=[jax.ShapeDtypeStruct((rows, cols), F32)] * 3)(w, g, m, v)


ANY = pl.BlockSpec(memory_space=pl.ANY)


def _place():
    x, y, c = lax.axis_index("x"), lax.axis_index("y"), lax.axis_index("c")
    chips = [(1 - x, y), (x, 1 - y), (1 - x, 1 - y)]
    return x, y, c, chips


def _remote(src, dst, send_sem, recv_sem, to):
    return pltpu.make_async_remote_copy(src_ref=src, dst_ref=dst, send_sem=send_sem, recv_sem=recv_sem,
                                        device_id=to, device_id_type=MESH)


def _half(ref, lead, core, axis):
    size = ref.shape[len(lead) + axis] // 2
    cut = pl.ds(core * size, size)
    return ref.at[tuple(lead) + ((cut, slice(None)) if axis == 0 else (slice(None), cut))]


def _gather_shards(ws, axes):
    side = _gather_side(ws, axes)

    def body(*refs):
        nw = len(ws)
        split = (refs[:nw], refs[nw:2 * nw], refs[2 * nw:])
        side.phase("start", *split)
        side.phase("finish", *split)

    return _pcall(body, name="gather_shards", in_specs=[ANY] * len(ws), out_specs=[ANY] * len(ws),
                  out_shape=side.out_shape, scratch_shapes=side.scratch)(*ws)


def _gather_side(ws, axes):
    nw = len(ws)

    def phase(which, w_refs, out_refs, sems):
        send_sems, recv_sems = sems
        x, y, c, chips = _place()
        me = 2 * x + y
        sibling = (x, y, 1 - c)

        def part(i, slot, core):
            return _half(out_refs[i], (slot,), core, axes[i])

        def copy(sem, src, dst, to):
            return _remote(src, dst, send_sems.at[sem], recv_sems.at[sem], to)

        first = [copy(j * nw + i, _half(w_refs[i], (), c, axes[i]), part(i, me, c), (*chip, c))
                 for j, chip in enumerate(chips) for i in range(nw)]
        if which == "start":
            for cp in first:
                cp.start()
            return
        passed = []
        for j, (cx, cy) in enumerate(chips):
            slot = 2 * cx + cy
            for i in range(nw):
                copy(j * nw + i, part(i, slot, c), part(i, slot, c), (x, y, c)).wait_recv()
                fwd = copy((3 + j) * nw + i, part(i, slot, c), part(i, slot, c), sibling)
                fwd.start()
                passed.append(fwd)
        for j, (cx, cy) in enumerate(chips):
            slot = 2 * cx + cy
            for i in range(nw):
                copy((3 + j) * nw + i, part(i, slot, 1 - c), part(i, slot, 1 - c), (x, y, c)).wait_recv()
        for cp in first + passed:
            cp.wait_send()

    return _Side(list(ws), [jax.ShapeDtypeStruct((4,) + w.shape, w.dtype) for w in ws],
                 [pltpu.SemaphoreType.DMA((6 * nw,)), pltpu.SemaphoreType.DMA((6 * nw,))], phase)


def _half_shape(shape, axis):
    return tuple(d // 2 if k == len(shape) - 2 + axis else d for k, d in enumerate(shape))


def _pair_exchange(gs, axes, name):
    nw = len(gs)

    def body(*refs):
        g_refs, out_refs = refs[:nw], refs[nw:2 * nw]
        send_sems, recv_sems = refs[2 * nw:]
        x, y, c, _ = _place()
        cps = []
        for i in range(nw):
            cp = _remote(_half(g_refs[i], (slice(None),), 1 - c, axes[i]), out_refs[i],
                         send_sems.at[i], recv_sems.at[i], (x, y, 1 - c))
            cp.start()
            cps.append(cp)
        for cp in cps:
            cp.wait()

    return _pcall(body, name=name, in_specs=[ANY] * nw, out_specs=[ANY] * nw,
                  out_shape=[jax.ShapeDtypeStruct(_half_shape(g.shape, a), g.dtype) for g, a in zip(gs, axes)],
                  scratch_shapes=[pltpu.SemaphoreType.DMA((nw,)), pltpu.SemaphoreType.DMA((nw,))])(*gs)


def _chip_exchange(ps):
    side = _chip_side(ps)

    def body(*refs):
        nw = len(ps)
        split = (refs[:nw], refs[nw:2 * nw], refs[2 * nw:])
        side.phase("start", *split)
        side.phase("finish", *split)

    return _pcall(body, name="chip_exchange", in_specs=[ANY] * len(ps), out_specs=[ANY] * len(ps),
                  out_shape=side.out_shape, scratch_shapes=side.scratch)(*ps)


def _chip_side(ps):
    nw = len(ps)

    def phase(which, p_refs, out_refs, sems):
        send_sems, recv_sems, local_sems = sems
        x, y, c, chips = _place()
        me = 2 * x + y
        mine = [pltpu.make_async_copy(p_refs[i].at[me], out_refs[i].at[me], local_sems.at[i]) for i in range(nw)]
        sends = [_remote(p_refs[i].at[2 * cx + cy], out_refs[i].at[me], send_sems.at[j * nw + i],
                         recv_sems.at[j * nw + i], (cx, cy, c))
                 for j, (cx, cy) in enumerate(chips) for i in range(nw)]
        if which == "start":
            for cp in mine + sends:
                cp.start()
            return
        for j, (cx, cy) in enumerate(chips):
            slot = 2 * cx + cy
            for i in range(nw):
                _remote(out_refs[i].at[slot], out_refs[i].at[slot], send_sems.at[j * nw + i],
                        recv_sems.at[j * nw + i], (x, y, c)).wait_recv()
        for cp in sends:
            cp.wait_send()
        for cp in mine:
            cp.wait()

    return _Side(list(ps), [jax.ShapeDtypeStruct(p.shape, p.dtype) for p in ps],
                 [pltpu.SemaphoreType.DMA((3 * nw,)), pltpu.SemaphoreType.DMA((3 * nw,)),
                  pltpu.SemaphoreType.DMA((nw,))], phase)


def _half_exchange(ts, wholes, axes):
    nw = len(ts)

    def body(*refs):
        t_refs, out_refs = refs[:nw], refs[2 * nw:3 * nw]
        send_sems, recv_sems = refs[3 * nw:]
        x, y, c, _ = _place()
        sends = []
        for i in range(nw):
            cp = _remote(t_refs[i], _half(out_refs[i], (), c, axes[i]), send_sems.at[i], recv_sems.at[i],
                         (x, y, 1 - c))
            cp.start()
            sends.append(cp)
        for i in range(nw):
            _remote(t_refs[i], _half(out_refs[i], (), 1 - c, axes[i]), send_sems.at[i], recv_sems.at[i],
                    (x, y, c)).wait_recv()
        for cp in sends:
            cp.wait_send()

    return _pcall(body, name="half_exchange", in_specs=[ANY] * (2 * nw), out_specs=[ANY] * nw,
                  out_shape=[jax.ShapeDtypeStruct(w.shape, w.dtype) for w in wholes],
                  input_output_aliases={nw + i: i for i in range(nw)},
                  scratch_shapes=[pltpu.SemaphoreType.DMA((nw,)), pltpu.SemaphoreType.DMA((nw,))])(*ts, *wholes)


def _gather_all(v):
    rows, cols = v.shape

    def body(v_ref, out_ref, send_sems, recv_sems, local_sem):
        x, y, c, _ = _place()
        me = 4 * x + 2 * y + c
        mine = pltpu.make_async_copy(v_ref, out_ref.at[me], local_sem)
        mine.start()
        sends = []
        for d in range(1, 8):
            peer = (x ^ (d >> 2), y ^ ((d >> 1) & 1), c ^ (d & 1))
            cp = _remote(v_ref, out_ref.at[me], send_sems.at[d - 1], recv_sems.at[d - 1], peer)
            cp.start()
            sends.append(cp)
        for d in range(1, 8):
            slot = 4 * (x ^ (d >> 2)) + 2 * (y ^ ((d >> 1) & 1)) + (c ^ (d & 1))
            _remote(v_ref, out_ref.at[slot], send_sems.at[d - 1], recv_sems.at[d - 1], (x, y, c)).wait_recv()
        for cp in sends:
            cp.wait_send()
        mine.wait()

    return _pcall(body, name="gather_all", in_specs=[ANY], out_specs=ANY,
                  out_shape=jax.ShapeDtypeStruct((8, rows, cols), v.dtype),
                  scratch_shapes=[pltpu.SemaphoreType.DMA((7,)), pltpu.SemaphoreType.DMA((7,)),
                                  pltpu.SemaphoreType.DMA])(v)


def _pad_cols(a, width):
    return a if a.shape[1] == width else jnp.pad(a, ((0, 0), (0, width - a.shape[1])))


def _unpad_segments():
    z = PAD["z"]
    segs = [(PAD["c_q"], 0, 512), (PAD["c_kv"], 512, 512), (PAD["k_rope"], 1024, 64), (z, 1088, 1024)]
    segs += [(PAD["q_nsa"] + 256 * h, 2112 + NSA_DK * h, NSA_DK) for h in range(NSA_HEADS)]
    for name, rows in (("k_c", 192), ("v_c", 128), ("k_s", 192), ("v_s", 128), ("k_w", 192), ("v_w", 128),
                       ("g_nsa", 12)):
        segs.append((PAD[name], ORIG[name][0], rows))
    segs += [(z + 1024, ORIG["z_nsa"][0], 512), (PAD["q_mem"], ORIG["q_mem"][0], 512),
             (z + 1536, ORIG["z_mem"][0], 512)]
    return segs


def _w_in_grad_slots(gt):
    rows, cols = gt.shape
    shard = sum(n for _, _, n in _unpad_segments()) // 4
    tc = 256
    pieces = []
    for src, dst, n in _unpad_segments():
        while n:
            slot, off = divmod(dst, shard)
            take = min(n, shard - off)
            pieces.append((src, slot, off, take))
            src, dst, n = src + take, dst + take, n - take

    def kern(g_ref, o_ref):
        for src, slot, off, take in pieces:
            o_ref[slot, off:off + take, :] = g_ref[src:src + take, :]

    return _pcall(kern, name="w_in_grad_slots", grid=(cols // tc,),
                  in_specs=[pl.BlockSpec((rows, tc), lambda i: (0, i))],
                  out_specs=pl.BlockSpec((4, shard, tc), lambda i: (0, 0, i)),
                  out_shape=jax.ShapeDtypeStruct((4, shard, cols), F32))(gt)


def _w_in_from_slots(ws):
    nslot, shard, cols = ws.shape
    tc = 256
    pieces = []
    for dst, src, n in _unpad_segments() + [(PAD["k_rope"] + 64, ORIG["k_rope"][0], 64)]:
        while n:
            slot, off = divmod(src, shard)
            take = min(n, shard - off)
            pieces.append((dst, slot, off, take))
            src, dst, n = src + take, dst + take, n - take

    def kern(w_ref, o_ref):
        o_ref[...] = jnp.zeros_like(o_ref)
        for dst, slot, off, take in pieces:
            o_ref[dst:dst + take, :] = w_ref[slot, off:off + take, :]

    return _pcall(kern, name="w_in_from_slots", grid=(cols // tc,),
                  in_specs=[pl.BlockSpec((nslot, shard, tc), lambda i: (0, 0, i))],
                  out_specs=pl.BlockSpec((D_PAD, tc), lambda i: (0, i)),
                  out_shape=jax.ShapeDtypeStruct((D_PAD, cols), ws.dtype))(ws)


def _rope_tables(s):
    pos = jnp.arange(s, dtype=F32)
    inv_freq = ROPE_THETA ** (-jnp.arange(0, 64, 2, dtype=F32) / 64)
    ang = pos[:, None] * inv_freq[None, :]
    cos, sin = jnp.cos(ang), jnp.sin(ang)
    z = jnp.zeros((s, 64), F32)
    return jnp.concatenate([cos, cos, z], axis=1), jnp.concatenate([-sin, sin, z], axis=1)


def _overlap_table(s):
    n_c, n_s = s // CMP_STRIDE, s // SLC_LEN
    c0 = np.arange(n_c)[:, None] * CMP_STRIDE
    s0 = np.arange(LANE)[None, :] * SLC_LEN
    ov = (c0 < s0 + SLC_LEN) & (c0 + CMP_LEN > s0) & (np.arange(n_c)[:, None] < n_c - 1) & (np.arange(LANE)[None, :] < n_s)
    return jnp.asarray(ov.astype(np.float32), dtype=BF16)


def _shift_down(a):
    return jnp.concatenate([jnp.zeros((8, a.shape[1]), a.dtype), a], axis=0)[7:7 + a.shape[0]]


def _shift_up(a):
    return jnp.concatenate([a, jnp.zeros((8, a.shape[1]), a.dtype)], axis=0)[1:1 + a.shape[0]]


def _local_step(x, mem, target, w, hooks=None):
    s = x.shape[0]
    cs, sn = _rope_tables(s)
    t_ = jnp.transpose

    w_in_p = _w_in_from_slots(w["w_in_t"])
    xn, rstd_x = _rms_fwd(_Src(x, D_MODEL), w["norm_g"], "norm_x")
    if hooks is None:
        hp, hpb = _mm(xn, w_in_p, "in_proj", mode="nt", second_dtype=BF16)
    else:
        hp, hpb, *gathered = _mm(xn, w_in_p, "in_proj", mode="nt", second_dtype=BF16, side=hooks.gather_side)
        w = {**w, **hooks.weights(gathered)}

    w_uq3 = w["w_uq"].reshape(512, MLA_HEADS, 192)
    w_uq_p = jnp.concatenate([w_uq3, w_uq3[:, :, 128:]], axis=2).reshape(512, MLA_HEADS * 256)
    w_ukv_p = t_(w["w_ukv"].reshape(512, MLA_HEADS, 2, 128), (0, 2, 1, 3)).reshape(512, 2048)
    c_q, c_kv = _Src(hp, 512, 0), _Src(hp, 512, 1)
    cqn, rstd_q = _rms_fwd(c_q, w["q_norm_g"], "norm_q")
    ckvn, rstd_kv = _rms_fwd(c_kv, w["kv_norm_g"], "norm_kv")
    q_lin = _mm(cqn, w_uq_p, "mla_q_proj")
    kvb = _mm(ckvn, w_ukv_p, "mla_kv_proj", out_dtype=BF16)
    q_mla = _rope_fwd(_Src(q_lin, MLA_HEADS * 256), cs, sn, MLA_HEADS, 256, LANE, "rope_q")
    k_pe = _rope_fwd(_Src(hp, LANE, PAD["k_rope"] // LANE), cs, sn, 1, LANE, 0, "rope_k")
    mla = _Attn("mla", s, s, MLA_HEADS, 256)
    mla_q, mla_v = _Src(q_mla, 256), _Src(kvb, LANE, MLA_HEADS)
    mla_k = [_Src(kvb, LANE), _Src(k_pe, LANE, 0, False)]
    o_mla, l_mla, lr_mla = _attn_fwd(mla, mla_q, mla_k, mla_v, None, "mla_fwd")

    sk = s // CMP_STRIDE
    pe_k, pe_v = w["cmp_pe_k"], w["cmp_pe_v"]
    w1k = _pad_cols(w["cmp_w1k"], 256)
    w2k = jnp.pad(w["cmp_w2k"], ((0, 64), (0, 64))).astype(BF16)
    w1v, w2v = w["cmp_w1v"], w["cmp_w2v"].astype(BF16)
    half_k, half_v = CMP_STRIDE * NSA_DK, CMP_STRIDE * HEAD_V
    ak = hp[:, PAD["k_c"]:PAD["k_c"] + NSA_DK].reshape(sk, half_k)
    av = hp[:, PAD["v_c"]:PAD["v_c"] + HEAD_V].reshape(sk, half_v)
    ck_args = (ak, _shift_up(ak), pe_k[:CMP_STRIDE].reshape(1, half_k), pe_k[CMP_STRIDE:].reshape(1, half_k),
               w1k[:half_k], w1k[half_k:], w2k)
    cv_args = (av, _shift_up(av), pe_v[:CMP_STRIDE].reshape(1, half_v), pe_v[CMP_STRIDE:].reshape(1, half_v),
               w1v[:half_v], w1v[half_v:], w2v)
    k_cmp, pre_k = _compress_fwd(*ck_args, "compress_k")
    v_cmp, pre_v = _compress_fwd(*cv_args, "compress_v")
    cmp_ = _Attn("cmp", s, sk, NSA_HEADS, 256)
    slc = _Attn("slc", s, s, NSA_HEADS, 256)
    win = _Attn("win", s, s, NSA_HEADS, 256)
    nsa_q = _Src(hpb, 256, PAD["q_nsa"] // 256)
    cmp_k, cmp_v = [_Src(k_cmp, 256, 0, False)], _Src(v_cmp, HEAD_V, 0, False)
    slc_k, slc_v = [_Src(hpb, 256, PAD["k_s"] // 256, False)], _Src(hpb, HEAD_V, PAD["v_s"] // HEAD_V, False)
    win_k, win_v = [_Src(hpb, 256, PAD["k_w"] // 256, False)], _Src(hpb, HEAD_V, PAD["v_w"] // HEAD_V, False)
    o_cmp, l_cmp, lr_cmp, sel, selt = _attn_fwd_small(cmp_, nsa_q, cmp_k, cmp_v, "cmp_fwd", _overlap_table(s))
    o_slc, l_slc, lr_slc = _attn_fwd(slc, nsa_q, slc_k, slc_v, sel, "slc_fwd")
    o_win, l_win, lr_win = _attn_fwd_small(win, nsa_q, win_k, win_v, "win_fwd")
    gl = _Src(hp, LANE, PAD["g_nsa"] // LANE)
    o_nsa = _nsa_combine(o_cmp, o_slc, o_win, gl)

    mn, rstd_m = _rms_fwd(_Src(mem, D_MODEL), w["mem_norm_g"], "norm_mem")
    kvm = _mm(mn, w["w_mem_kv"], "mem_kv_proj", out_dtype=BF16)
    mem_ = _Attn("mem", s, mem.shape[0], MEM_HEADS, LANE)
    mem_q, mem_k, mem_v = _Src(hpb, LANE, PAD["q_mem"] // LANE), [_Src(kvm, LANE)], _Src(kvm, LANE, MEM_HEADS)
    o_mem, l_mem, lr_mem = _attn_fwd_small(mem_, mem_q, mem_k, mem_v, "mem_fwd")

    u = _gate_fwd(o_mla, o_nsa, o_mem, hp)
    proj = _mm(u, w["w_out"], "out_proj")
    dy, g_final, loss = _final_loss(x, proj, w["final_norm_g"].reshape(1, -1), target)

    g_w_out = _mm(u, dy, "out_proj_dw", mode="tn")
    du = _mm(dy, w["w_out"], "out_proj_dx", mode="nt")
    do_cat, dz = _gate_bwd(du, o_mla, o_nsa, o_mem, hp)

    dq_mem, (dk_mem,), dv_mem = _attn_bwd(mem_, mem_q, mem_k, mem_v, None, None, _Src(o_mem, HEAD_V), l_mem,
                                          lr_mem, _Src(do_cat, HEAD_V, 12), None, "mem_bwd")
    dkvm = jnp.concatenate([dk_mem, dv_mem], axis=1)
    g_w_mem_kv = _mm(mn, dkvm, "mem_kv_dw", mode="tn")
    dmn = _mm(dkvm, w["w_mem_kv"], "mem_kv_dx", mode="nt")
    _, g_mem_norm = _rms_bwd(_Src(mem, D_MODEL), w["mem_norm_g"], rstd_m, dmn, None, "norm_mem_bwd")

    do_cmp, do_slc, do_win, dgl = _nsa_combine_bwd(do_cat, o_cmp, o_slc, o_win, gl)
    dq_n, (dk_cmp,), dv_cmp = _attn_bwd(cmp_, nsa_q, cmp_k, cmp_v, None, None, _Src(o_cmp, HEAD_V), l_cmp,
                                        lr_cmp, _Src(do_cmp, HEAD_V), None, "cmp_bwd")
    dq_n, (dk_s,), dv_s = _attn_bwd(slc, nsa_q, slc_k, slc_v, sel, selt, _Src(o_slc, HEAD_V), l_slc, lr_slc,
                                    _Src(do_slc, HEAD_V), dq_n, "slc_bwd")
    dq_n, (dk_w,), dv_w = _attn_bwd(win, nsa_q, win_k, win_v, None, None, _Src(o_win, HEAD_V), l_win, lr_win,
                                    _Src(do_win, HEAD_V), dq_n, "win_bwd")
    dak, dpk_lo, dpk_hi, dw1k_lo, dw1k_hi, g_w2k = _compress_bwd(
        *ck_args, pre_k, _shift_down(pre_k), dk_cmp, _shift_down(dk_cmp), "compress_k_bwd")
    dav, dpv_lo, dpv_hi, dw1v_lo, dw1v_hi, g_w2v = _compress_bwd(
        *cv_args, pre_v, _shift_down(pre_v), dv_cmp, _shift_down(dv_cmp), "compress_v_bwd")
    g_pe_k = jnp.concatenate([dpk_lo.reshape(CMP_STRIDE, NSA_DK), dpk_hi.reshape(CMP_STRIDE, NSA_DK)], axis=0)
    g_pe_v = jnp.concatenate([dpv_lo.reshape(CMP_STRIDE, HEAD_V), dpv_hi.reshape(CMP_STRIDE, HEAD_V)], axis=0)
    g_w1k = jnp.concatenate([dw1k_lo, dw1k_hi], axis=0)[:, :NSA_DK]
    g_w1v = jnp.concatenate([dw1v_lo, dw1v_hi], axis=0)
    dk_c = _pad_cols(dak.reshape(s, NSA_DK), 256)
    dv_c = dav.reshape(s, HEAD_V)

    dq_m, (dk_nope, dk_pe), dv_m = _attn_bwd(mla, mla_q, mla_k, mla_v, None, None, _Src(o_mla, HEAD_V), l_mla,
                                             lr_mla, _Src(do_cat, HEAD_V), None, "mla_bwd")
    dq_lin = _rope_bwd_q(dq_m, cs, sn)
    dkv_lin, d_krope = _rope_bwd_k(dk_nope, dk_pe, dv_m, cs, sn)
    g_w_uq_p = _mm(cqn, dq_lin, "mla_q_dw", mode="tn")
    dcqn = _mm(dq_lin, w_uq_p, "mla_q_dx", mode="nt")
    g_w_ukv_p = _mm(ckvn, dkv_lin, "mla_kv_dw", mode="tn")
    dckvn = _mm(dkv_lin, w_ukv_p, "mla_kv_dx", mode="nt")
    dc_q, g_q_norm = _rms_bwd(c_q, w["q_norm_g"], rstd_q, dcqn, None, "norm_q_bwd")
    dc_kv, g_kv_norm = _rms_bwd(c_kv, w["kv_norm_g"], rstd_kv, dckvn, None, "norm_kv_bwd")
    g_w_uq = g_w_uq_p.reshape(512, MLA_HEADS, 256)[:, :, :192].reshape(512, MLA_HEADS * 192)
    g_w_ukv = t_(g_w_ukv_p.reshape(512, 2, MLA_HEADS, 128), (0, 2, 1, 3)).reshape(512, 2048)

    dhp = jnp.concatenate(
        [dc_q, dc_kv, dq_n, dk_c, dk_s, dk_w, d_krope, dv_c, dv_s, dv_w, dgl,
         jnp.zeros((s, PAD["q_mem"] - (PAD["g_nsa"] + LANE)), F32), dq_mem, dz], axis=1).astype(BF16)
    grads = dict(q_norm_g=g_q_norm, w_uq=g_w_uq, kv_norm_g=g_kv_norm,
                 w_ukv=g_w_ukv, cmp_pe_k=g_pe_k, cmp_pe_v=g_pe_v, cmp_w1k=g_w1k, cmp_w2k=g_w2k[:NSA_DK, :NSA_DK],
                 cmp_w1v=g_w1v, cmp_w2v=g_w2v, mem_norm_g=g_mem_norm, w_mem_kv=g_w_mem_kv, w_out=g_w_out,
                 final_norm_g=g_final.reshape(-1))
    if hooks is None:
        g_w_in_t = _w_in_grad_slots(_mm(dhp, xn, "in_proj_dw", mode="tn", wide=2048))
        dxn = _mm(dhp, w_in_p, "in_proj_dx", wide=2048)
    else:
        g_w_in_p, *hooks.received = _mm(dhp, xn, "in_proj_dw", mode="tn", wide=2048, side=hooks.reduce_side(grads))
        g_w_in_t = _w_in_grad_slots(g_w_in_p)
        dxn, hooks.received_w_in = _mm(dhp, w_in_p, "in_proj_dx", wide=2048, side=hooks.reduce_side_w_in(g_w_in_t))
    grad_x, g_norm = _rms_bwd(_Src(x, D_MODEL), w["norm_g"], rstd_x, dxn, dy, "norm_x_bwd")
    grads.update(norm_g=g_norm, w_in_t=g_w_in_t)
    return loss[0, 0], grad_x, grads


def kernel(x, mem, norm_g, w_in, q_norm_g, w_uq, kv_norm_g, w_ukv, cmp_pe_k, cmp_pe_v, cmp_w1k, cmp_w2k, cmp_w1v, cmp_w2v, mem_norm_g, w_mem_kv, w_out, final_norm_g, loss_target, m_norm_g, m_w_in, m_q_norm_g, m_w_uq, m_kv_norm_g, m_w_ukv, m_cmp_pe_k, m_cmp_pe_v, m_cmp_w1k, m_cmp_w2k, m_cmp_w1v, m_cmp_w2v, m_mem_norm_g, m_w_mem_kv, m_w_out, m_final_norm_g, v_norm_g, v_w_in, v_q_norm_g, v_w_uq, v_kv_norm_g, v_w_ukv, v_cmp_pe_k, v_cmp_pe_v, v_cmp_w1k, v_cmp_w2k, v_cmp_w1v, v_cmp_w2v, v_mem_norm_g, v_w_mem_kv, v_w_out, v_final_norm_g):
    args = dict(locals())
    wts = {n: args[n] for n in WEIGHTS}
    loc = {n: (a if n == "final_norm_g" else a[0]) for n, a in wts.items()}

    def to_x(n, a):
        return a.T if n == "w_in" else a

    split = [1 if n == "w_in" else 0 for n in SHARDED]
    rest = [n for n in SHARDED if n != "w_in"]
    chip = 2 * lax.axis_index("x") + lax.axis_index("y")
    core = lax.axis_index("c").astype(jnp.int32).reshape(1)
    own = {n: to_x(n, loc[n]).astype(BF16) for n in SHARDED}

    def with_own_slot(gw, a):
        return lax.dynamic_update_slice(gw, a[None], (chip, 0, 0))

    def slots(n, a):
        if n == "w_in":
            return a
        if SHARD_AXIS[n] == 0:
            return a.reshape(4, a.shape[0] // 4, a.shape[1])
        width = a.shape[1] // 4
        return jnp.stack([a[:, j * width:(j + 1) * width] for j in range(4)])

    def pair_sums(names, grads, name):
        axes = [1 if n == "w_in" else 0 for n in names]
        gs = [slots(n, a) for n, a in zip(names, grads)]
        theirs = _pair_exchange(gs, axes, name)
        return [_pair_sum(a, b, core, ax, "pair_sum_" + n) for n, a, b, ax in zip(names, gs, theirs, axes)]

    class Hooks:
        gather_side = _gather_side([own[n] for n in rest], [0] * len(rest))
        received = None

        @staticmethod
        def weights(gathered):
            out = {}
            for n, gw in zip(rest, gathered):
                gw = with_own_slot(gw, own[n])
                if SHARD_AXIS[n] == 0:
                    out[n] = gw.reshape(4 * gw.shape[1], gw.shape[2])
                else:
                    out[n] = jnp.concatenate([gw[j] for j in range(4)], axis=1)
            return out

        @staticmethod
        def reduce_side(grads):
            return _chip_side(pair_sums(rest, [grads[n] for n in rest], "pair_exchange_rest"))

        @staticmethod
        def reduce_side_w_in(g_w_in_t):
            return _chip_side(pair_sums(["w_in"], [g_w_in_t], "pair_exchange_w_in"))

    hooks = Hooks()

    start = {n: loc[n].reshape(1, -1) if loc[n].ndim == 1 else loc[n] for n in REPLICATED}
    start["w_in_t"] = with_own_slot(_gather_shards([own["w_in"]], [1])[0], own["w_in"])
    loss, grad_x, g = _local_step(x[0], mem[0], loss_target[0], start, hooks)
    loss = lax.psum(loss, ("x", "y", "c"))

    from_chips = dict(zip(rest, hooks.received), w_in=hooks.received_w_in)
    sums = [_chip_sum(from_chips[n], core, ax, "chip_sum_" + n) for n, ax in zip(SHARDED, split)]
    g_sh = _half_exchange([a for a, _ in sums], [b for _, b in sums], split)

    n_rep = sum(int(np.prod(loc[n].shape)) for n in REPLICATED)
    rows_rep = -(-n_rep // (8 * LANE)) * 8

    def rep_pack(parts):
        flat = jnp.concatenate([p.reshape(-1) for p in parts])
        return jnp.pad(flat, (0, rows_rep * LANE - n_rep)).reshape(rows_rep, LANE)

    g_rep = _sum_slots(_gather_all(rep_pack([g[n] for n in REPLICATED])), "replica_sum")
    d_rp, m_rp, v_rp = _adamw(rep_pack([wts[n] for n in REPLICATED]), g_rep,
                              rep_pack([args["m_" + n] for n in REPLICATED]),
                              rep_pack([args["v_" + n] for n in REPLICATED]), "adamw_replicated")

    def rep_unpack(buf):
        flat, out, o = buf.reshape(-1), {}, 0
        for n in REPLICATED:
            size = int(np.prod(wts[n].shape))
            out[n] = flat[o:o + size].reshape(wts[n].shape)
            o += size
        return out

    outs = {k: rep_unpack(b) for k, b in (("g", g_rep), ("d", d_rp), ("m", m_rp), ("v", v_rp))}
    for n, gn in zip(SHARDED, g_sh):
        d, mo, vo = _adamw(to_x(n, loc[n]), gn, to_x(n, args["m_" + n][0]), to_x(n, args["v_" + n][0]),
                           "adamw_" + n)
        for k, a in (("g", gn), ("d", d), ("m", mo), ("v", vo)):
            outs[k][n] = to_x(n, a).reshape(wts[n].shape)

    return (loss, grad_x[None], *[outs["g"][n] for n in WEIGHTS], *[outs["d"][n] for n in WEIGHTS],
            *[outs["m"][n] for n in WEIGHTS], *[outs["v"][n] for n in WEIGHTS])
```

```python
from typing import NamedTuple

import numpy as np
import jax
import jax.numpy as jnp
from jax import lax
from jax.experimental import pallas as pl
from jax.experimental.pallas import tpu as pltpu

F32 = jnp.float32
BF16 = jnp.bfloat16
MESH = pl.DeviceIdType.MESH

D_MODEL = 2048
EPS = 1e-6
LANE = 128
HEAD_V = 128
MLA_HEADS = 8
NSA_HEADS = 4
MEM_HEADS = 4
NSA_DK = 192
CMP_STRIDE = 16
CMP_LEN = 32
SLC_LEN = 64
SLC_TOPN = 16
WIN = 512
NEG = -1e30
LOG2E = 1.4426950408889634
ROPE_THETA = 10000.0
BLOCK_BYTES = 2 << 20

ORIG = dict(c_q=(0, 512), c_kv=(512, 512), k_rope=(1024, 64), z_mla=(1088, 1024),
            q_nsa=(2112, 768), k_c=(2880, 192), v_c=(3072, 128), k_s=(3200, 192),
            v_s=(3392, 128), k_w=(3520, 192), v_w=(3712, 128), g_nsa=(3840, 12),
            z_nsa=(3852, 512), q_mem=(4364, 512), z_mem=(4876, 512))
PAD = dict(c_q=0, c_kv=512, q_nsa=1024, k_c=2048, k_s=2304, k_w=2560, k_rope=2816, v_c=2944,
           v_s=3072, v_w=3200, g_nsa=3328, q_mem=3584, z=4096)
D_PAD = 6144

ADAM_LR, ADAM_B1, ADAM_B2, ADAM_EPS, ADAM_WD, ADAM_STEP = 0.001, 0.9, 0.999, 1e-08, 0.01, 10

SHARDED = ("w_in", "w_uq", "w_ukv", "cmp_w1k", "cmp_w1v", "w_mem_kv", "w_out")
SHARD_AXIS = dict(w_in=1, w_uq=1, w_ukv=1, cmp_w1k=0, cmp_w1v=0, w_mem_kv=0, w_out=0)
REPLICATED = ("norm_g", "q_norm_g", "kv_norm_g", "cmp_pe_k", "cmp_pe_v", "cmp_w2k", "cmp_w2v",
              "mem_norm_g", "final_norm_g")
WEIGHTS = ("norm_g", "w_in", "q_norm_g", "w_uq", "kv_norm_g", "w_ukv", "cmp_pe_k", "cmp_pe_v",
           "cmp_w1k", "cmp_w2k", "cmp_w1v", "cmp_w2v", "mem_norm_g", "w_mem_kv", "w_out",
           "final_norm_g")


def _pcall(kernel, **kw):
    return pl.pallas_call(kernel, **kw)


def _tile(n, pref):
    if n <= pref:
        return n
    for t in range(pref, LANE - 1, -LANE):
        if n % t == 0:
            return t
    raise ValueError((n, pref))


def _row_tile(rows, cols, itemsize=4):
    want = max(16, BLOCK_BYTES // (cols * itemsize))
    if rows <= want:
        return rows
    t = 16
    best = rows
    while t <= want:
        if rows % t == 0:
            best = t
        t *= 2
    return best


def _nt(a, b):
    return lax.dot_general(a, b, (((1,), (1,)), ((), ())), preferred_element_type=F32)


def _tn(a, b):
    return lax.dot_general(a, b, (((0,), (0,)), ((), ())), preferred_element_type=F32)


def _nn(a, b):
    return jnp.dot(a, b, preferred_element_type=F32)


def _sigmoid(x):
    return 1.0 / (1.0 + jnp.exp(-x))


class _Src(NamedTuple):
    arr: jax.Array
    width: int
    col0: int = 0
    per_head: bool = True

    def col(self, h):
        return self.col0 + h if self.per_head else self.col0


class _Side(NamedTuple):
    inputs: list
    out_shape: list
    scratch: list
    phase: object


def _mm(a, b, name, mode="nn", out_dtype=F32, second_dtype=None, wide=1024, side=None):
    if mode == "tn":
        k, m = a.shape
    else:
        m, k = a.shape
    if mode == "nt":
        n, k2 = b.shape
    else:
        k2, n = b.shape
    assert k == k2, (a.shape, b.shape, mode)
    tm, tn, tk = _tile(m, 1024), _tile(n, wide), _tile(k, 2048)
    grid = (m // tm, n // tn, k // tk)
    nk = grid[2]
    assert nk == 1 or (out_dtype == F32 and second_dtype is None)
    dot = {"nn": _nn, "nt": _nt, "tn": _tn}[mode]
    n_in = len(side.inputs) if side else 0
    n_out = len(side.out_shape) if side else 0
    n_res = 1 + (second_dtype is not None)

    def kern(*refs):
        a_ref, b_ref = refs[:2]
        res = refs[2 + n_in:2 + n_in + n_res]
        step = [pl.program_id(d) for d in range(3)]
        if side:
            side_refs = (refs[2:2 + n_in], refs[2 + n_in + n_res:2 + n_in + n_res + n_out],
                         refs[2 + n_in + n_res + n_out:])

            @pl.when((step[0] == 0) & (step[1] == 0) & (step[2] == 0))
            def _():
                side.phase("start", *side_refs)

        r = dot(a_ref[...].astype(BF16), b_ref[...].astype(BF16))
        if nk == 1:
            res[0][...] = r.astype(out_dtype)
            if n_res == 2:
                res[1][...] = r.astype(second_dtype)
        else:
            @pl.when(step[2] == 0)
            def _():
                res[0][...] = r

            @pl.when(step[2] > 0)
            def _():
                res[0][...] += r

        if side:
            @pl.when((step[0] == grid[0] - 1) & (step[1] == grid[1] - 1) & (step[2] == nk - 1))
            def _():
                side.phase("finish", *side_refs)

    a_spec = (pl.BlockSpec((tk, tm), lambda i, j, kk: (kk, i)) if mode == "tn"
              else pl.BlockSpec((tm, tk), lambda i, j, kk: (i, kk)))
    b_spec = (pl.BlockSpec((tn, tk), lambda i, j, kk: (j, kk)) if mode == "nt"
              else pl.BlockSpec((tk, tn), lambda i, j, kk: (kk, j)))
    o_spec = pl.BlockSpec((tm, tn), lambda i, j, kk: (i, j))
    out_specs = [o_spec] * n_res + [ANY] * n_out
    out_shape = [jax.ShapeDtypeStruct((m, n), out_dtype)]
    if second_dtype is not None:
        out_shape.append(jax.ShapeDtypeStruct((m, n), second_dtype))
    out_shape += list(side.out_shape) if side else []
    semantics = ("arbitrary",) * 3 if side else ("parallel", "parallel", "arbitrary")
    out = _pcall(
        kern, name=name, grid=grid, in_specs=[a_spec, b_spec] + [ANY] * n_in, out_specs=out_specs,
        out_shape=out_shape, scratch_shapes=list(side.scratch) if side else [],
        compiler_params=pltpu.CompilerParams(dimension_semantics=semantics),
    )(a, b, *(side.inputs if side else []))
    return out[0] if len(out) == 1 else out


def _rms_fwd(x, g, name):
    r, d = x.arr.shape[0], x.width
    tr = _tile(r, 512)

    def kern(x_ref, g_ref, y_ref, r_ref):
        xv = x_ref[...]
        rstd = lax.rsqrt(jnp.mean(xv * xv, axis=-1, keepdims=True) + EPS)
        y_ref[...] = (xv * rstd * g_ref[...]).astype(BF16)
        r_ref[...] = rstd

    return _pcall(
        kern, name=name, grid=(r // tr,),
        in_specs=[pl.BlockSpec((tr, d), lambda i: (i, x.col0)), pl.BlockSpec((1, d), lambda i: (0, 0))],
        out_specs=[pl.BlockSpec((tr, d), lambda i: (i, 0)), pl.BlockSpec((tr, 1), lambda i: (i, 0))],
        out_shape=[jax.ShapeDtypeStruct((r, d), BF16), jax.ShapeDtypeStruct((r, 1), F32)],
    )(x.arr, g)


def _rms_bwd(x, g, rstd, dy, add, name):
    r, d = x.arr.shape[0], x.width
    tr = _tile(r, 256)
    has_add = add is not None

    def kern(*refs):
        if has_add:
            x_ref, g_ref, r_ref, dy_ref, add_ref, dx_ref, dg_ref = refs
        else:
            x_ref, g_ref, r_ref, dy_ref, dx_ref, dg_ref = refs
        rs = r_ref[...]
        xhat = x_ref[...] * rs
        dyv = dy_ref[...]
        dyg = dyv * g_ref[...]
        c = jnp.mean(dyg * xhat, axis=-1, keepdims=True)
        dx = rs * (dyg - xhat * c)
        if has_add:
            dx = dx + add_ref[...]
        dx_ref[...] = dx
        part = jnp.sum(dyv * xhat, axis=0, keepdims=True)

        @pl.when(pl.program_id(0) == 0)
        def _():
            dg_ref[...] = part

        @pl.when(pl.program_id(0) > 0)
        def _():
            dg_ref[...] += part

    row = pl.BlockSpec((tr, d), lambda i: (i, 0))
    vec = pl.BlockSpec((1, d), lambda i: (0, 0))
    ins = [pl.BlockSpec((tr, d), lambda i: (i, x.col0)), vec, pl.BlockSpec((tr, 1), lambda i: (i, 0)), row]
    ins += [row] if has_add else []
    args = (x.arr, g, rstd, dy) + ((add,) if has_add else ())
    return _pcall(
        kern, name=name, grid=(r // tr,), in_specs=ins, out_specs=[row, vec],
        out_shape=[jax.ShapeDtypeStruct((r, d), F32), jax.ShapeDtypeStruct((1, d), F32)],
        compiler_params=pltpu.CompilerParams(dimension_semantics=("arbitrary",)),
    )(*args)


def _final_loss(x, proj, g, target):
    r, d = x.shape
    tr = _tile(r, 256)

    def kern(x_ref, p_ref, g_ref, t_ref, dy_ref, dg_ref, loss_ref):
        y = x_ref[...] + p_ref[...]
        rs = lax.rsqrt(jnp.mean(y * y, axis=-1, keepdims=True) + EPS)
        yhat = y * rs
        gv = g_ref[...]
        e = yhat * gv - t_ref[...]
        lpart = 0.5 * jnp.sum(jnp.mean(e * e, axis=-1, keepdims=True), axis=0, keepdims=True)
        dout = e * (1.0 / d)
        dyg = dout * gv
        c = jnp.mean(dyg * yhat, axis=-1, keepdims=True)
        dy_ref[...] = rs * (dyg - yhat * c)
        gpart = jnp.sum(dout * yhat, axis=0, keepdims=True)
        lrow = jnp.broadcast_to(lpart, (1, LANE))

        @pl.when(pl.program_id(0) == 0)
        def _():
            dg_ref[...] = gpart
            loss_ref[...] = lrow

        @pl.when(pl.program_id(0) > 0)
        def _():
            dg_ref[...] += gpart
            loss_ref[...] += lrow

    row = pl.BlockSpec((tr, d), lambda i: (i, 0))
    vec = pl.BlockSpec((1, d), lambda i: (0, 0))
    return _pcall(
        kern, name="final_loss", grid=(r // tr,), in_specs=[row, row, vec, row],
        out_specs=[row, vec, pl.BlockSpec((1, LANE), lambda i: (0, 0))],
        out_shape=[jax.ShapeDtypeStruct((r, d), F32), jax.ShapeDtypeStruct((1, d), F32),
                   jax.ShapeDtypeStruct((1, LANE), F32)],
        compiler_params=pltpu.CompilerParams(dimension_semantics=("arbitrary",)),
    )(x, proj, g, target)


def _rope_fwd(x, cs, sn, nh, width, off, name):
    s = x.arr.shape[0]
    tr = _tile(s, 512)

    def kern(x_ref, c_ref, s_ref, o_ref):
        cv, sv = c_ref[...], s_ref[...]
        for h in range(nh):
            b = h * width
            if off:
                o_ref[:, b:b + off] = x_ref[:, b:b + off].astype(BF16)
            xr = x_ref[:, b + off:b + off + LANE]
            o_ref[:, b + off:b + off + LANE] = (xr * cv + pltpu.roll(xr, 32, 1) * sv).astype(BF16)

    tab = pl.BlockSpec((tr, LANE), lambda i: (i, 0))
    return _pcall(
        kern, name=name, grid=(s // tr,),
        in_specs=[pl.BlockSpec((tr, nh * width), lambda i: (i, x.col0)), tab, tab],
        out_specs=pl.BlockSpec((tr, nh * width), lambda i: (i, 0)),
        out_shape=jax.ShapeDtypeStruct((s, nh * width), BF16),
    )(x.arr, cs, sn)


def _rope_grad(d, cv, sv):
    g2 = d * sv
    g2 = g2 + pltpu.roll(g2, 64, 1)
    lane = lax.broadcasted_iota(jnp.int32, d.shape, 1)
    return jnp.where(lane < 64, d * cv + pltpu.roll(g2, 32, 1), 0.0)


def _rope_bwd_q(dq, cs, sn):
    s, w = dq.shape
    tr = _tile(s, 512)
    nh = w // 256

    def kern(d_ref, c_ref, s_ref, o_ref):
        cv, sv = c_ref[...], s_ref[...]
        for h in range(nh):
            b = h * 256
            o_ref[:, b:b + LANE] = d_ref[:, b:b + LANE]
            o_ref[:, b + LANE:b + 256] = _rope_grad(d_ref[:, b + LANE:b + 256], cv, sv)

    row = pl.BlockSpec((tr, w), lambda i: (i, 0))
    tab = pl.BlockSpec((tr, LANE), lambda i: (i, 0))
    return _pcall(kern, name="rope_bwd_q", grid=(s // tr,), in_specs=[row, tab, tab], out_specs=row,
                  out_shape=jax.ShapeDtypeStruct((s, w), F32))(dq, cs, sn)


def _rope_bwd_k(dk_nope, dk_pe, dv, cs, sn):
    s, w = dk_nope.shape
    tr = _tile(s, 512)

    def kern(dk_ref, dp_ref, dv_ref, c_ref, s_ref, okv_ref, okr_ref):
        okv_ref[:, :w] = dk_ref[...]
        okv_ref[:, w:] = dv_ref[...]
        okr_ref[...] = _rope_grad(dp_ref[...], c_ref[...], s_ref[...])

    tab = pl.BlockSpec((tr, LANE), lambda i: (i, 0))
    wide = pl.BlockSpec((tr, w), lambda i: (i, 0))
    return _pcall(
        kern, name="rope_bwd_k", grid=(s // tr,), in_specs=[wide, tab, wide, tab, tab],
        out_specs=[pl.BlockSpec((tr, 2 * w), lambda i: (i, 0)), tab],
        out_shape=[jax.ShapeDtypeStruct((s, 2 * w), F32), jax.ShapeDtypeStruct((s, LANE), F32)],
    )(dk_nope, dk_pe, dv, cs, sn)


class _Attn:
    def __init__(self, mode, s, sk, heads, dk):
        self.mode, self.s, self.sk, self.h, self.dk = mode, s, sk, heads, dk
        self.scale = {"mla": 192 ** -0.5, "mem": 128 ** -0.5}.get(mode, NSA_DK ** -0.5)
        self.tb = min(256, s)
        self.nb = s // self.tb
        self.nsub = 2 if self.nb % 2 == 0 else 1
        self.tq = self.tb * self.nsub
        self.nq = s // self.tq
        self.causal = mode in ("mla", "slc")
        if self.causal:
            self.tk = self.tq
        elif mode == "win":
            self.tk = WIN + self.tb
        else:
            self.tk = sk
        self.tkb = min(512, sk)
        self.ksub = 1
        self.kb = self.tkb // self.ksub
        self.ncmp = s // CMP_STRIDE - 1

    def mask_bias(self, t, n, h, selx, diag):
        m = self.mode
        if m == "mla":
            return (n <= t) if diag else None, None
        if m == "mem":
            return None, None
        slope = jnp.where(h == 0, 0.25, jnp.where(h == 1, 0.0625, jnp.where(h == 2, 0.015625, 0.00390625)))
        slope = slope.astype(F32) * LOG2E
        if m == "cmp":
            mask = (n * CMP_STRIDE + (CMP_LEN - 1) <= t) & (n < self.ncmp)
            pos = n.astype(F32) * float(CMP_STRIDE) + (CMP_LEN - 1) / 2.0
            return mask, slope * pos
        rel = t - n
        if m == "slc":
            return (rel >= 0) if diag else None, slope * n.astype(F32)
        return (rel >= 0) & (rel < WIN), slope * n.astype(F32)


def _scores(cfg, s_raw, t, n, h, selx, diag, lse=None):
    s = s_raw * (cfg.scale * LOG2E)
    mask, key_term = cfg.mask_bias(t, n, h, selx, diag)
    if key_term is not None:
        s = s + key_term
    if selx is not None:
        s = s + selx
    if lse is None:
        if mask is not None:
            s = jnp.where(mask, s, NEG)
        return s, mask
    p = jnp.exp2(jnp.minimum(s - lse, 0.0))
    if mask is not None:
        p = jnp.where(mask, p, 0.0)
    return p, mask


def _block_of_key(k0, tk, keys_on_rows, value=NEG):
    shape = (tk, LANE) if keys_on_rows else (LANE, tk)
    n = lax.broadcasted_iota(jnp.int32, shape, 0 if keys_on_rows else 1) + k0
    j = lax.broadcasted_iota(jnp.int32, shape, 1 if keys_on_rows else 0)
    return jnp.where((n >> 6) == j, value, 0.0).astype(BF16)


def _to_row(col):
    t = col.shape[0]
    return jnp.transpose(jnp.broadcast_to(col, (t, LANE)))[0:1, :]


def _load_keys(k_refs, rows):
    parts = [r[rows, :].astype(BF16) for r in k_refs]
    return parts[0] if len(parts) == 1 else jnp.concatenate(parts, axis=1)


def _attn_fwd(cfg, q, ks, v, sel, name):
    s, tq, tk, tb, nsub = cfg.s, cfg.tq, cfg.tk, cfg.tb, cfg.nsub
    has_sel = sel is not None
    nkp = len(ks)
    assert cfg.causal and tq == tk and cfg.nq % 2 == 0

    def kern(*refs):
        q_ref, k_refs, v_ref = refs[0], refs[1:1 + nkp], refs[1 + nkp]
        sel_ref = refs[2 + nkp] if has_sel else None
        o_ref, lc_ref, lr_ref = refs[2 + nkp + has_sel:5 + nkp + has_sel]
        buf_a, buf_b = refs[-2:]
        h, g = pl.program_id(0), pl.program_id(1)

        def block(b):
            rows = [slice(b * tq + r * tb, b * tq + (r + 1) * tb) for r in range(nsub)]
            qs = [q_ref[p, :].astype(BF16) for p in rows]
            ts = [(2 * g + b) * tq + r * tb + lax.broadcasted_iota(jnp.int32, (tb, 1), 0) for r in range(nsub)]
            sels = [sel_ref[p, :].astype(BF16) for p in rows] if has_sel else None
            return rows, qs, ts, sels

        def scores_into(buf, blk, c):
            kk = _load_keys(k_refs, pl.ds(pl.multiple_of(c * tk, tk), tk))
            for r in range(nsub):
                buf[r] = _nt(blk[1][r], kk)

        def consume(buf, blk, c, carry, diag):
            _, _, ts, sels = blk
            k0 = pl.multiple_of(c * tk, tk)
            vv = v_ref[pl.ds(k0, tk), :].astype(BF16)
            emat = _block_of_key(k0, tk, False) if has_sel else None
            n = k0 + lax.broadcasted_iota(jnp.int32, (1, tk), 1)
            new = []
            for r in range(nsub):
                m, l, acc = carry[r]
                selx = _nn(sels[r], emat) if has_sel else None
                sc, mask = _scores(cfg, buf[r], ts[r], n, h, selx, diag)
                m_new = jnp.maximum(m, jnp.max(sc, axis=1, keepdims=True))
                alpha = jnp.exp2(m - m_new)
                p = jnp.exp2(sc - m_new)
                if mask is not None:
                    p = jnp.where(mask, p, 0.0)
                l = alpha * l + jnp.sum(p, axis=1, keepdims=True)
                new.append((m_new, l, alpha * acc + _nn(p.astype(BF16), vv)))
            return tuple(new)

        def finish(blk, b, carry):
            for r, (m, l, acc) in enumerate(carry):
                o_ref[blk[0][r], :] = acc / (l + 1e-20)
                lse = m + jnp.log(l + 1e-20) * LOG2E
                lc_ref[0, blk[0][r], :] = lse
                lr_ref[0, b * nsub + r] = _to_row(lse)

        def pairs(blk, first, other):
            def pair(p, cr):
                scores_into(other, blk, 2 * p + 1)
                cr = consume(first, blk, 2 * p, cr, False)
                scores_into(first, blk, 2 * p + 2)
                return consume(other, blk, 2 * p + 1, cr, False)
            return pair

        init = ((jnp.full((tb, 1), NEG, F32), jnp.zeros((tb, 1), F32), jnp.zeros((tb, HEAD_V), F32)),) * nsub
        blk_a, blk_b = block(0), block(1)
        scores_into(buf_a, blk_a, 0)
        carry = lax.fori_loop(0, g, pairs(blk_a, buf_a, buf_b), init)
        scores_into(buf_b, blk_b, 0)
        finish(blk_a, 0, consume(buf_a, blk_a, 2 * g, carry, True))
        carry = lax.fori_loop(0, g, pairs(blk_b, buf_b, buf_a), init)
        scores_into(buf_a, blk_b, 2 * g + 1)
        carry = consume(buf_b, blk_b, 2 * g, carry, False)
        finish(blk_b, 1, consume(buf_a, blk_b, 2 * g + 1, carry, True))

    ins = [pl.BlockSpec((2 * tq, q.width), lambda h, g: (g, q.col(h)))]
    ins += [pl.BlockSpec((cfg.sk, p.width), lambda h, g, p=p: (0, p.col(h))) for p in ks]
    ins += [pl.BlockSpec((cfg.sk, HEAD_V), lambda h, g: (0, v.col(h)))]
    args = [q.arr] + [p.arr for p in ks] + [v.arr]
    if has_sel:
        ins.append(pl.BlockSpec((2 * tq, LANE), lambda h, g: (g, 0)))
        args.append(sel)
    return _pcall(
        kern, name=name, grid=(cfg.h, cfg.nq // 2), in_specs=ins,
        out_specs=[pl.BlockSpec((2 * tq, HEAD_V), lambda h, g: (g, h)),
                   pl.BlockSpec((1, 2 * tq, 1), lambda h, g: (h, g, 0)),
                   pl.BlockSpec((1, 2 * nsub, 1, tb), lambda h, g: (h, g, 0, 0))],
        out_shape=[jax.ShapeDtypeStruct((s, cfg.h * HEAD_V), F32),
                   jax.ShapeDtypeStruct((cfg.h, s, 1), F32),
                   jax.ShapeDtypeStruct((cfg.h, cfg.nb, 1, tb), F32)],
        scratch_shapes=[pltpu.VMEM((nsub, tb, tk), F32)] * 2,
        compiler_params=pltpu.CompilerParams(dimension_semantics=("parallel", "parallel")),
    )(*args)


def _attn_dq(cfg, q, ks, v, sel, o, lse, do, dq_in, name):
    s, tq, tk, dk, tb, nsub = cfg.s, cfg.tq, cfg.tk, cfg.dk, cfg.tb, cfg.nsub
    has_sel = sel is not None
    has_in = dq_in is not None
    nkp = len(ks)

    def kern(*refs):
        refs = list(refs)
        q_ref, k_refs, v_ref = refs[0], refs[1:1 + nkp], refs[1 + nkp]
        p0 = 2 + nkp
        sel_ref = refs[p0] if has_sel else None
        p0 += has_sel
        o_ref, l_ref, do_ref = refs[p0:p0 + 3]
        p0 += 3
        in_ref = refs[p0] if has_in else None
        p0 += has_in
        dq_ref, dr_ref = refs[p0:p0 + 2]
        sa, pa, sb, pb = refs[-4:]
        h, g = pl.program_id(0), pl.program_id(1)

        def block(b):
            rows = [slice(b * tq + r * tb, b * tq + (r + 1) * tb) for r in range(nsub)]
            qs = [q_ref[p, :].astype(BF16) for p in rows]
            ts = [(2 * g + b) * tq + r * tb + lax.broadcasted_iota(jnp.int32, (tb, 1), 0) for r in range(nsub)]
            sels = [sel_ref[p, :].astype(BF16) for p in rows] if has_sel else None
            dvecs, dobs, lses = [], [], []
            for r, p in enumerate(rows):
                dov = do_ref[p, :]
                dvec = jnp.sum(dov * o_ref[p, :], axis=1, keepdims=True)
                dr_ref[0, b * nsub + r] = _to_row(dvec)
                dvecs.append(dvec)
                dobs.append(dov.astype(BF16))
                lses.append(l_ref[0, p, :])
            return rows, qs, ts, sels, dvecs, dobs, lses

        def products_into(sbuf, pbuf, blk, c):
            rows = pl.ds(pl.multiple_of(c * tk, tk), tk)
            kk, vv = _load_keys(k_refs, rows), v_ref[rows, :].astype(BF16)
            for r in range(nsub):
                sbuf[r] = _nt(blk[1][r], kk)
                pbuf[r] = _nt(blk[5][r], vv)

        def consume(sbuf, pbuf, blk, c, accs, diag):
            _, _, ts, sels, dvecs, _, lses = blk
            k0 = pl.multiple_of(c * tk, tk)
            kk = _load_keys(k_refs, pl.ds(k0, tk))
            emat = _block_of_key(k0, tk, False) if has_sel else None
            n = k0 + lax.broadcasted_iota(jnp.int32, (1, tk), 1)
            new = []
            for r in range(nsub):
                selx = _nn(sels[r], emat) if has_sel else None
                p, _ = _scores(cfg, sbuf[r], ts[r], n, h, selx, diag, lses[r])
                ds = p * (pbuf[r] - dvecs[r])
                new.append(accs[r] + _nn(ds.astype(BF16), kk))
            return tuple(new)

        def finish(blk, accs):
            for r, p in enumerate(blk[0]):
                dq_ref[p, :] = accs[r] * cfg.scale + in_ref[p, :] if has_in else accs[r] * cfg.scale

        def pairs(blk, first, other):
            def pair(p, ac):
                products_into(*other, blk, 2 * p + 1)
                ac = consume(*first, blk, 2 * p, ac, False)
                products_into(*first, blk, 2 * p + 2)
                return consume(*other, blk, 2 * p + 1, ac, False)
            return pair

        zero = (jnp.zeros((tb, dk), F32),) * nsub
        buf_a, buf_b = (sa, pa), (sb, pb)
        blk_a, blk_b = block(0), block(1)
        products_into(*buf_a, blk_a, 0)
        accs = lax.fori_loop(0, g, pairs(blk_a, buf_a, buf_b), zero)
        products_into(*buf_b, blk_b, 0)
        finish(blk_a, consume(*buf_a, blk_a, 2 * g, accs, True))
        accs = lax.fori_loop(0, g, pairs(blk_b, buf_b, buf_a), zero)
        products_into(*buf_a, blk_b, 2 * g + 1)
        accs = consume(*buf_b, blk_b, 2 * g, accs, False)
        finish(blk_b, consume(*buf_a, blk_b, 2 * g + 1, accs, True))

    assert cfg.causal and tq == tk and cfg.nq % 2 == 0
    qs = pl.BlockSpec((2 * tq, dk), lambda h, g: (g, h))
    ins = [pl.BlockSpec((2 * tq, q.width), lambda h, g: (g, q.col(h)))]
    ins += [pl.BlockSpec((cfg.sk, p.width), lambda h, g, p=p: (0, p.col(h))) for p in ks]
    ins += [pl.BlockSpec((cfg.sk, HEAD_V), lambda h, g: (0, v.col(h)))]
    args = [q.arr] + [p.arr for p in ks] + [v.arr]
    if has_sel:
        ins.append(pl.BlockSpec((2 * tq, LANE), lambda h, g: (g, 0)))
        args.append(sel)
    ins += [pl.BlockSpec((2 * tq, HEAD_V), lambda h, g: (g, o.col(h))),
            pl.BlockSpec((1, 2 * tq, 1), lambda h, g: (h, g, 0)),
            pl.BlockSpec((2 * tq, HEAD_V), lambda h, g: (g, do.col(h)))]
    args += [o.arr, lse, do.arr]
    if has_in:
        ins.append(qs)
        args.append(dq_in)
    return _pcall(
        kern, name=name, grid=(cfg.h, cfg.nq // 2), in_specs=ins,
        out_specs=[qs, pl.BlockSpec((1, 2 * nsub, 1, tb), lambda h, g: (h, g, 0, 0))],
        out_shape=[jax.ShapeDtypeStruct((s, cfg.h * dk), F32),
                   jax.ShapeDtypeStruct((cfg.h, cfg.nb, 1, tb), F32)],
        scratch_shapes=[pltpu.VMEM((nsub, tb, tk), F32)] * 4,
        compiler_params=pltpu.CompilerParams(dimension_semantics=("parallel", "parallel")),
    )(*args)


def _attn_dkv(cfg, q, ks, v, selt, lse_r, d_r, do, name):
    s, tq, tkb, dk, kb, ksub = cfg.s, cfg.tb, cfg.tkb, cfg.dk, cfg.kb, cfg.ksub
    nq = cfg.nb
    has_sel = selt is not None
    nkp = len(ks)
    outs = list(ks) + [v]

    def kern(*refs):
        k_refs, v_ref = refs[:nkp], refs[nkp]
        q_ref, do_ref, lr_ref, dr_ref = refs[nkp + 1:nkp + 5]
        st_ref = refs[nkp + 5] if has_sel else None
        out_refs = refs[nkp + 5 + has_sel:2 * nkp + 6 + has_sel]
        sa, pa, sb, pb = refs[-4:]
        j, h = pl.program_id(0), pl.program_id(1)
        k0 = j * tkb
        part = [slice(u * kb, (u + 1) * kb) for u in range(ksub)]
        kks = [_load_keys(k_refs, p) for p in part]
        vvs = [v_ref[p, :].astype(BF16) for p in part]
        ns = [k0 + u * kb + lax.broadcasted_iota(jnp.int32, (kb, 1), 0) for u in range(ksub)]
        emats = [_block_of_key(k0 + u * kb, kb, True) for u in range(ksub)] if has_sel else None

        def load_q(i):
            rows = pl.ds(pl.multiple_of(i * tq, tq), tq)
            return q_ref[rows, :].astype(BF16), do_ref[rows, :].astype(BF16)

        def products_into(sbuf, pbuf, i):
            qi, doi = load_q(i)
            for u in range(ksub):
                sbuf[u] = _nt(kks[u], qi)
                pbuf[u] = _nt(vvs[u], doi)

        def consume(sbuf, pbuf, i, carry):
            qi, doi = load_q(i)
            t = i * tq + lax.broadcasted_iota(jnp.int32, (1, tq), 1)
            selt_i = st_ref[i].astype(BF16) if has_sel else None
            new = []
            for u in range(ksub):
                dk_acc, dv_acc = carry[u]
                selx = _nn(emats[u], selt_i) if has_sel else None
                pt, _ = _scores(cfg, sbuf[u], t, ns[u], h, selx, True, lr_ref[0, i])
                dv_acc = dv_acc + _nn(pt.astype(BF16), doi)
                dst = pt * (pbuf[u] - dr_ref[0, i])
                new.append((dk_acc + _nn(dst.astype(BF16), qi), dv_acc))
            return tuple(new)

        if cfg.causal:
            first, count = k0 // tq, nq - k0 // tq
        elif cfg.mode == "win":
            first = k0 // tq
            count = jnp.minimum((k0 + tkb + WIN - 2) // tq + 1, nq) - first
        else:
            first, count = 0, nq

        def pair(p, cr):
            i0 = first + 2 * p
            products_into(sb, pb, i0 + 1)
            cr = consume(sa, pa, i0, cr)
            products_into(sa, pa, i0 + 2)
            return consume(sb, pb, i0 + 1, cr)

        carry = ((jnp.zeros((kb, dk), F32), jnp.zeros((kb, HEAD_V), F32)),) * ksub
        products_into(sa, pa, first)
        carry = lax.fori_loop(0, count // 2 - 1, pair, carry)
        last = first + count - 2
        products_into(sb, pb, last + 1)
        carry = consume(sa, pa, last, carry)
        carry = consume(sb, pb, last + 1, carry)
        for u, (dk_acc, dv_acc) in enumerate(carry):
            vals, off = [], 0
            for p in ks:
                vals.append(dk_acc[:, off:off + p.width] * cfg.scale)
                off += p.width
            vals.append(dv_acc)
            for src, ref, val in zip(outs, out_refs, vals):
                if src.per_head:
                    ref[part[u], :] = val
                else:
                    @pl.when(h == 0)
                    def _(ref=ref, val=val, u=u):
                        ref[part[u], :] = val

                    @pl.when(h > 0)
                    def _(ref=ref, val=val, u=u):
                        ref[part[u], :] += val

    rowv = pl.BlockSpec((1, nq, 1, tq), lambda j, h: (h, 0, 0, 0))
    ins = [pl.BlockSpec((tkb, p.width), lambda j, h, p=p: (j, p.col(h))) for p in ks]
    ins += [pl.BlockSpec((tkb, HEAD_V), lambda j, h: (j, v.col(h))),
            pl.BlockSpec((s, q.width), lambda j, h: (0, q.col(h))),
            pl.BlockSpec((s, HEAD_V), lambda j, h: (0, do.col(h))), rowv, rowv]
    args = [p.arr for p in ks] + [v.arr, q.arr, do.arr, lse_r, d_r]
    if has_sel:
        ins.append(pl.BlockSpec((nq, LANE, tq), lambda j, h: (0, 0, 0)))
        args.append(selt)
    out_specs = [pl.BlockSpec((tkb, p.width), lambda j, h, p=p: (j, h if p.per_head else 0)) for p in outs]
    out_shape = [jax.ShapeDtypeStruct((cfg.sk, (cfg.h if p.per_head else 1) * p.width), F32) for p in outs]
    assert nq % 2 == 0 and (cfg.mode in ("cmp", "mem") or tkb % (2 * tq) == 0), (nq, tkb, tq)
    return _pcall(
        kern, name=name, grid=(cfg.sk // tkb, cfg.h), in_specs=ins, out_specs=out_specs, out_shape=out_shape,
        scratch_shapes=[pltpu.VMEM((ksub, kb, tq), F32)] * 4,
        compiler_params=pltpu.CompilerParams(dimension_semantics=("parallel", "arbitrary")),
    )(*args)


def _attn_dkv_flat(cfg, q, ks, v, selt, lse_r, d_r, do, name):
    s, tq, tkb, dk, kb, ksub = cfg.s, cfg.tb, cfg.tkb, cfg.dk, cfg.kb, cfg.ksub
    nq = cfg.nb
    has_sel = selt is not None
    nkp = len(ks)
    outs = list(ks) + [v]
    assert nq % 2 == 0 and tkb % (2 * tq) == 0, (nq, tkb, tq)
    steps = []
    for j in range(cfg.sk // tkb):
        first = j * tkb // tq
        stop = nq if cfg.causal else min((j * tkb + tkb + WIN - 2) // tq + 1, nq)
        steps += [(j, i0) for i0 in range(first, stop, 2)]
    n_pairs = len(steps)
    steps.append(steps[-1])
    tab_j = jnp.asarray(np.array([p[0] for p in steps], np.int32))
    tab_i = jnp.asarray(np.array([p[1] for p in steps], np.int32))

    def kern(tj_ref, ti_ref, *refs):
        k_refs, v_ref = refs[:nkp], refs[nkp]
        q_ref, do_ref, lr_ref, dr_ref = refs[nkp + 1:nkp + 5]
        st_ref = refs[nkp + 5] if has_sel else None
        out_refs = refs[nkp + 5 + has_sel:2 * nkp + 6 + has_sel]
        sa, pa, sb, pb = refs[-4:]
        h = pl.program_id(0)
        for src, ref in zip(outs, out_refs):
            if src.per_head:
                ref[...] = jnp.zeros_like(ref)
            else:
                @pl.when(h == 0)
                def _(ref=ref):
                    ref[...] = jnp.zeros_like(ref)

        def key_rows(j, u):
            return pl.ds(pl.multiple_of(j * tkb + u * kb, kb), kb)

        def load_q(i):
            rows = pl.ds(pl.multiple_of(i * tq, tq), tq)
            return q_ref[rows, :].astype(BF16), do_ref[rows, :].astype(BF16)

        def products_into(sbuf, pbuf, j, i):
            qi, doi = load_q(i)
            for u in range(ksub):
                rows = key_rows(j, u)
                sbuf[u] = _nt(_load_keys(k_refs, rows), qi)
                pbuf[u] = _nt(v_ref[rows, :].astype(BF16), doi)

        def consume(sbuf, pbuf, j, i):
            qi, doi = load_q(i)
            t = i * tq + lax.broadcasted_iota(jnp.int32, (1, tq), 1)
            selt_i = st_ref[i].astype(BF16) if has_sel else None
            res = []
            for u in range(ksub):
                k0 = j * tkb + u * kb
                n = k0 + lax.broadcasted_iota(jnp.int32, (kb, 1), 0)
                selx = _nn(_block_of_key(k0, kb, True), selt_i) if has_sel else None
                pt, _ = _scores(cfg, sbuf[u], t, n, h, selx, True, lr_ref[0, i])
                dst = pt * (pbuf[u] - dr_ref[0, i])
                res.append((_nn(dst.astype(BF16), qi), _nn(pt.astype(BF16), doi)))
            return res

        def pair(p, carry):
            j, i0 = tj_ref[p], ti_ref[p]
            products_into(sb, pb, j, i0 + 1)
            ca = consume(sa, pa, j, i0)
            products_into(sa, pa, tj_ref[p + 1], ti_ref[p + 1])
            cb = consume(sb, pb, j, i0 + 1)
            for u in range(ksub):
                rows = key_rows(j, u)
                dk_c = (ca[u][0] + cb[u][0]) * cfg.scale
                off = 0
                for src, ref in zip(ks, out_refs):
                    ref[rows, :] += dk_c[:, off:off + src.width]
                    off += src.width
                out_refs[nkp][rows, :] += ca[u][1] + cb[u][1]
            return carry

        products_into(sa, pa, tj_ref[0], ti_ref[0])
        lax.fori_loop(0, n_pairs, pair, 0)

    rowv = pl.BlockSpec((1, nq, 1, tq), lambda h, tj, ti: (h, 0, 0, 0))
    ins = [pl.BlockSpec((cfg.sk, p.width), lambda h, tj, ti, p=p: (0, p.col(h))) for p in ks]
    ins += [pl.BlockSpec((cfg.sk, HEAD_V), lambda h, tj, ti: (0, v.col(h))),
            pl.BlockSpec((s, q.width), lambda h, tj, ti: (0, q.col(h))),
            pl.BlockSpec((s, HEAD_V), lambda h, tj, ti: (0, do.col(h))), rowv, rowv]
    args = [p.arr for p in ks] + [v.arr, q.arr, do.arr, lse_r, d_r]
    if has_sel:
        ins.append(pl.BlockSpec((nq, LANE, tq), lambda h, tj, ti: (0, 0, 0)))
        args.append(selt)
    out_specs = [pl.BlockSpec((cfg.sk, p.width), lambda h, tj, ti, p=p: (0, h if p.per_head else 0))
                 for p in outs]
    out_shape = [jax.ShapeDtypeStruct((cfg.sk, (cfg.h if p.per_head else 1) * p.width), F32) for p in outs]
    grid_spec = pltpu.PrefetchScalarGridSpec(
        num_scalar_prefetch=2, grid=(cfg.h,), in_specs=ins, out_specs=out_specs,
        scratch_shapes=[pltpu.VMEM((ksub, kb, tq), F32)] * 4)
    return _pcall(kern, name=name, grid_spec=grid_spec, out_shape=out_shape,
                  compiler_params=pltpu.CompilerParams(dimension_semantics=("arbitrary",)))(tab_j, tab_i, *args)


def _all_heads(cfg, src, rows, key=False):
    if src.per_head:
        assert src.col0 % cfg.h == 0
        width, col = cfg.h * src.width, src.col0 // cfg.h
    else:
        width, col = src.width, src.col0
    return pl.BlockSpec((rows, width), (lambda i: (0, col)) if key else (lambda i: (i, col)))


def _head_cols(src, hh):
    return slice(hh * src.width, (hh + 1) * src.width) if src.per_head else slice(None)


def _key_window(cfg, i, r):
    if cfg.mode == "win":
        return pl.ds(pl.multiple_of(jnp.maximum(i * cfg.tq + r * cfg.tb - WIN, 0), cfg.tb), cfg.tk)
    return pl.ds(0, cfg.tk)


def _attn_fwd_small(cfg, q, ks, v, name, overlap=None):
    s, tq, tk, tb, nsub, nh = cfg.s, cfg.tq, cfg.tk, cfg.tb, cfg.nsub, cfg.h
    nkp = len(ks)
    select = overlap is not None
    n_s = s // SLC_LEN
    top_n = min(SLC_TOPN, n_s)

    def kern(*refs):
        q_ref, k_refs, v_ref = refs[0], refs[1:1 + nkp], refs[1 + nkp]
        ov_ref = refs[2 + nkp] if select else None
        o_ref, lc_ref, lr_ref = refs[2 + nkp + select:5 + nkp + select]
        i = pl.program_id(0)
        imps = [jnp.zeros((tb, LANE), F32)] * nsub
        for r in range(nsub):
            rows = slice(r * tb, (r + 1) * tb)
            t = i * tq + r * tb + lax.broadcasted_iota(jnp.int32, (tb, 1), 0)
            win = _key_window(cfg, i, r)
            n = win.start + lax.broadcasted_iota(jnp.int32, (1, tk), 1)
            for hh in range(nh):
                qv = q_ref[rows, hh * cfg.dk:(hh + 1) * cfg.dk].astype(BF16)
                kk = _load_keys([kr.at[:, _head_cols(p, hh)] for kr, p in zip(k_refs, ks)], win)
                vv = v_ref[win, _head_cols(v, hh)].astype(BF16)
                sc, mask = _scores(cfg, _nt(qv, kk), t, n, hh, None, True)
                m = jnp.max(sc, axis=1, keepdims=True)
                e = jnp.exp2(sc - m)
                if mask is not None:
                    e = jnp.where(mask, e, 0.0)
                l = jnp.sum(e, axis=1, keepdims=True)
                o_ref[rows, hh * HEAD_V:(hh + 1) * HEAD_V] = _nn(e.astype(BF16), vv) / (l + 1e-20)
                lse = m + jnp.log(l + 1e-20) * LOG2E
                lc_ref[hh, rows, :] = lse
                lr_ref[hh, r] = _to_row(lse)
                if select:
                    imps[r] = imps[r] + _nn((e / (l + 1e-20)).astype(BF16), ov_ref[...])
        if select:
            sel_ref, selt_ref, imp_t = refs[5 + nkp + select:8 + nkp + select]
            for r in range(nsub):
                t = i * tq + r * tb + lax.broadcasted_iota(jnp.int32, (tb, 1), 0)
                j = lax.broadcasted_iota(jnp.int32, (tb, LANE), 1)
                cur = t >> 6
                imp = jnp.where((j == 0) | (j == cur) | (j == cur - 1), 1e9, imps[r])
                imp = jnp.where(j > cur, -1e9, imp)
                imp_t[r] = jnp.transpose(imp)
                mine = imp_t[r, 0:n_s, :]
                jrow = lax.broadcasted_iota(jnp.int32, (n_s, tb), 0)

                def count(k, rank):
                    other = imp_t[r, pl.ds(k, 1), :]
                    ahead = (other > mine) | ((other == mine) & (k < jrow))
                    return rank + jnp.where(ahead, 1.0, 0.0)

                rank = lax.fori_loop(0, n_s, count, jnp.zeros((n_s, tb), F32))
                cur_t = (i * tq + r * tb + lax.broadcasted_iota(jnp.int32, (1, tb), 1)) >> 6
                rejected = jnp.where((rank < top_n) & (jrow <= cur_t), 0.0, 1.0)
                if n_s < LANE:
                    rejected = jnp.concatenate([rejected, jnp.ones((LANE - n_s, tb), F32)], axis=0)
                selt_ref[r] = rejected
                sel_ref[r * tb:(r + 1) * tb, :] = jnp.transpose(rejected)

    ins = [_all_heads(cfg, q, tq)] + [_all_heads(cfg, p, cfg.sk, True) for p in ks]
    ins += [_all_heads(cfg, v, cfg.sk, True)]
    args = [q.arr] + [p.arr for p in ks] + [v.arr]
    out_specs = [pl.BlockSpec((tq, nh * HEAD_V), lambda i: (i, 0)),
                 pl.BlockSpec((nh, tq, 1), lambda i: (0, i, 0)),
                 pl.BlockSpec((nh, nsub, 1, tb), lambda i: (0, i, 0, 0))]
    out_shape = [jax.ShapeDtypeStruct((s, nh * HEAD_V), F32), jax.ShapeDtypeStruct((nh, s, 1), F32),
                 jax.ShapeDtypeStruct((nh, cfg.nb, 1, tb), F32)]
    scratch = []
    if select:
        ins.append(pl.BlockSpec((cfg.sk, LANE), lambda i: (0, 0)))
        args.append(overlap)
        out_specs += [pl.BlockSpec((tq, LANE), lambda i: (i, 0)), pl.BlockSpec((nsub, LANE, tb), lambda i: (i, 0, 0))]
        out_shape += [jax.ShapeDtypeStruct((s, LANE), F32), jax.ShapeDtypeStruct((cfg.nb, LANE, tb), F32)]
        scratch = [pltpu.VMEM((nsub, LANE, tb), F32)]
    return _pcall(kern, name=name, grid=(cfg.nq,), in_specs=ins, out_specs=out_specs, out_shape=out_shape,
                  scratch_shapes=scratch,
                  compiler_params=pltpu.CompilerParams(dimension_semantics=("parallel",)))(*args)


def _attn_dq_small(cfg, q, ks, v, o, lse, do, dq_in, name):
    s, tq, tk, tb, nsub, nh, dk = cfg.s, cfg.tq, cfg.tk, cfg.tb, cfg.nsub, cfg.h, cfg.dk
    nkp = len(ks)
    has_in = dq_in is not None

    def kern(*refs):
        q_ref, k_refs, v_ref = refs[0], refs[1:1 + nkp], refs[1 + nkp]
        o_ref, l_ref, do_ref = refs[2 + nkp:5 + nkp]
        in_ref = refs[5 + nkp] if has_in else None
        dq_ref, dr_ref = refs[5 + nkp + has_in:7 + nkp + has_in]
        i = pl.program_id(0)
        for r in range(nsub):
            rows = slice(r * tb, (r + 1) * tb)
            t = i * tq + r * tb + lax.broadcasted_iota(jnp.int32, (tb, 1), 0)
            win = _key_window(cfg, i, r)
            n = win.start + lax.broadcasted_iota(jnp.int32, (1, tk), 1)
            for hh in range(nh):
                vcols = slice(hh * HEAD_V, (hh + 1) * HEAD_V)
                qcols = slice(hh * dk, (hh + 1) * dk)
                qv = q_ref[rows, qcols].astype(BF16)
                kk = _load_keys([kr.at[:, _head_cols(p, hh)] for kr, p in zip(k_refs, ks)], win)
                vv = v_ref[win, _head_cols(v, hh)].astype(BF16)
                dov = do_ref[rows, vcols]
                dvec = jnp.sum(dov * o_ref[rows, vcols], axis=1, keepdims=True)
                dr_ref[hh, r] = _to_row(dvec)
                p, _ = _scores(cfg, _nt(qv, kk), t, n, hh, None, True, l_ref[hh, rows, :])
                ds = p * (_nt(dov.astype(BF16), vv) - dvec)
                dq = _nn(ds.astype(BF16), kk) * cfg.scale
                dq_ref[rows, qcols] = dq + in_ref[rows, qcols] if has_in else dq

    qs = pl.BlockSpec((tq, nh * dk), lambda i: (i, 0))
    ins = [_all_heads(cfg, q, tq)] + [_all_heads(cfg, p, cfg.sk, True) for p in ks]
    ins += [_all_heads(cfg, v, cfg.sk, True)]
    ins += [_all_heads(cfg, o, tq), pl.BlockSpec((nh, tq, 1), lambda i: (0, i, 0)), _all_heads(cfg, do, tq)]
    args = [q.arr] + [p.arr for p in ks] + [v.arr, o.arr, lse, do.arr]
    if has_in:
        ins.append(qs)
        args.append(dq_in)
    return _pcall(
        kern, name=name, grid=(cfg.nq,), in_specs=ins,
        out_specs=[qs, pl.BlockSpec((nh, nsub, 1, tb), lambda i: (0, i, 0, 0))],
        out_shape=[jax.ShapeDtypeStruct((s, nh * dk), F32), jax.ShapeDtypeStruct((nh, cfg.nb, 1, tb), F32)],
        compiler_params=pltpu.CompilerParams(dimension_semantics=("parallel",)))(*args)


def _attn_bwd(cfg, q, ks, v, sel, selt, o, lse, lse_r, do, dq_in, name):
    if cfg.causal:
        dq, d_r = _attn_dq(cfg, q, ks, v, sel, o, lse, do, dq_in, name + "_dq")
    else:
        dq, d_r = _attn_dq_small(cfg, q, ks, v, o, lse, do, dq_in, name + "_dq")
    dkv = _attn_dkv_flat if cfg.causal or cfg.mode == "win" else _attn_dkv
    res = dkv(cfg, q, ks, v, selt, lse_r, d_r, do, name + "_dkv")
    return dq, res[:-1], res[-1]


def _silu_grad(pre):
    sg = _sigmoid(pre)
    return sg * (1.0 + pre * (1.0 - sg))


def _compress_fwd(a_lo, a_hi, pe_lo, pe_hi, w1_lo, w1_hi, w2, name):
    n, dp = a_lo.shape[0], w2.shape[1]

    def kern(alo, ahi, plo, phi, w1l, w1h, w2r, out_ref, pre_ref):
        xl = (alo[...] + plo[...]).astype(BF16)
        xh = (ahi[...] + phi[...]).astype(BF16)
        pre = _nn(xl, w1l[...]) + _nn(xh, w1h[...])
        act = pre * _sigmoid(pre)
        out_ref[...] = _nn(act.astype(BF16), w2r[...]).astype(BF16)
        pre_ref[...] = pre

    return _pcall(kern, name=name,
                  out_shape=[jax.ShapeDtypeStruct((n, dp), BF16), jax.ShapeDtypeStruct((n, dp), F32)],
                  )(a_lo, a_hi, pe_lo, pe_hi, w1_lo, w1_hi, w2)


def _compress_bwd(a_lo, a_hi, pe_lo, pe_hi, w1_lo, w1_hi, w2, pre, pre_sh, dout, dout_sh, name):
    n, ln = a_lo.shape
    dp = w2.shape[1]

    def kern(alo, ahi, plo, phi, w1l, w1h, w2r, pre_ref, presh_ref, do_ref, dosh_ref,
             da_ref, dpl_ref, dph_ref, dw1l_ref, dw1h_ref, dw2_ref):
        prev = pre_ref[...]
        act = prev * _sigmoid(prev)
        dob = do_ref[...].astype(BF16)
        w2v = w2r[...]
        dpre = (_nt(dob, w2v) * _silu_grad(prev)).astype(BF16)
        dpre_sh = (_nt(dosh_ref[...].astype(BF16), w2v) * _silu_grad(presh_ref[...])).astype(BF16)
        dw2_ref[...] = _nn(act.T.astype(BF16), dob)
        xl = alo[...] + plo[...]
        xh = ahi[...] + phi[...]
        dw1l_ref[...] = _nn(xl.T.astype(BF16), dpre)
        dw1h_ref[...] = _nn(xh.T.astype(BF16), dpre)
        dal = _nt(dpre, w1l[...])
        dah_sh = _nt(dpre_sh, w1h[...])
        da_ref[...] = dal + dah_sh
        dpl_ref[...] = jnp.sum(dal, axis=0, keepdims=True)
        dph_ref[...] = jnp.sum(dah_sh, axis=0, keepdims=True)

    return _pcall(
        kern, name=name,
        out_shape=[jax.ShapeDtypeStruct((n, ln), F32), jax.ShapeDtypeStruct((1, ln), F32),
                   jax.ShapeDtypeStruct((1, ln), F32), jax.ShapeDtypeStruct((ln, dp), F32),
                   jax.ShapeDtypeStruct((ln, dp), F32), jax.ShapeDtypeStruct((dp, dp), F32)],
    )(a_lo, a_hi, pe_lo, pe_hi, w1_lo, w1_hi, w2, pre, pre_sh, dout, dout_sh)


def _nsa_combine(o_cmp, o_slc, o_win, gl):
    s, w = o_cmp.shape
    tr = _tile(s, 512)

    def kern(a_ref, b_ref, c_ref, g_ref, o_ref):
        g = _sigmoid(g_ref[...])
        for h in range(NSA_HEADS):
            cs = slice(h * HEAD_V, (h + 1) * HEAD_V)
            o_ref[:, cs] = (g[:, 3 * h:3 * h + 1] * a_ref[:, cs] + g[:, 3 * h + 1:3 * h + 2] * b_ref[:, cs]
                            + g[:, 3 * h + 2:3 * h + 3] * c_ref[:, cs])

    row = pl.BlockSpec((tr, w), lambda i: (i, 0))
    return _pcall(kern, name="nsa_combine", grid=(s // tr,),
                  in_specs=[row, row, row, pl.BlockSpec((tr, LANE), lambda i: (i, gl.col0))], out_specs=row,
                  out_shape=jax.ShapeDtypeStruct((s, w), F32))(o_cmp, o_slc, o_win, gl.arr)


def _nsa_combine_bwd(do_cat, o_cmp, o_slc, o_win, gl):
    s, w = o_cmp.shape
    tr = _tile(s, 512)

    def kern(d_ref, a_ref, b_ref, c_ref, g_ref, da_ref, db_ref, dc_ref, dg_ref):
        g = _sigmoid(g_ref[...])
        lane = lax.broadcasted_iota(jnp.int32, (tr, LANE), 1)
        dgl = jnp.zeros((tr, LANE), F32)
        for h in range(NSA_HEADS):
            cs = slice(h * HEAD_V, (h + 1) * HEAD_V)
            dv = d_ref[:, cs]
            for b, (src, dst) in enumerate(((a_ref, da_ref), (b_ref, db_ref), (c_ref, dc_ref))):
                gate = g[:, 3 * h + b:3 * h + b + 1]
                dst[:, cs] = gate * dv
                dgate = jnp.sum(dv * src[:, cs], axis=1, keepdims=True)
                dgl = jnp.where(lane == 3 * h + b, dgate * gate * (1.0 - gate), dgl)
        dg_ref[...] = dgl

    row = pl.BlockSpec((tr, w), lambda i: (i, 0))
    tab = pl.BlockSpec((tr, LANE), lambda i: (i, 0))
    return _pcall(kern, name="nsa_combine_bwd", grid=(s // tr,),
                  in_specs=[pl.BlockSpec((tr, w), lambda i: (i, 2)), row, row, row,
                            pl.BlockSpec((tr, LANE), lambda i: (i, gl.col0))],
                  out_specs=[row, row, row, tab],
                  out_shape=[jax.ShapeDtypeStruct((s, w), F32)] * 3 + [jax.ShapeDtypeStruct((s, LANE), F32)],
                  )(do_cat, o_cmp, o_slc, o_win, gl.arr)


def _gate_fwd(o_mla, o_nsa, o_mem, hp):
    s = o_mla.shape[0]
    tr = _tile(s, 256)

    def kern(a_ref, b_ref, c_ref, z_ref, u_ref):
        z = z_ref[...]
        sz = z * _sigmoid(z)
        u_ref[:, 0:1024] = (a_ref[...] * sz[:, 0:1024]).astype(BF16)
        u_ref[:, 1024:1536] = (b_ref[...] * sz[:, 1024:1536]).astype(BF16)
        u_ref[:, 1536:2048] = (c_ref[...] * sz[:, 1536:2048]).astype(BF16)

    return _pcall(
        kern, name="gate_fwd", grid=(s // tr,),
        in_specs=[pl.BlockSpec((tr, 1024), lambda i: (i, 0)), pl.BlockSpec((tr, 512), lambda i: (i, 0)),
                  pl.BlockSpec((tr, 512), lambda i: (i, 0)), pl.BlockSpec((tr, 2048), lambda i: (i, 2))],
        out_specs=pl.BlockSpec((tr, 2048), lambda i: (i, 0)),
        out_shape=jax.ShapeDtypeStruct((s, 2048), BF16))(o_mla, o_nsa, o_mem, hp)


def _gate_bwd(du, o_mla, o_nsa, o_mem, hp):
    s = du.shape[0]
    tr = _tile(s, 256)

    def kern(d_ref, a_ref, b_ref, c_ref, z_ref, do_ref, dz_ref):
        z = z_ref[...]
        sg = _sigmoid(z)
        sz = z * sg
        dsz = sg * (1.0 + z * (1.0 - sg))
        d = d_ref[...]
        do_ref[...] = d * sz
        dz_ref[:, 0:1024] = d[:, 0:1024] * a_ref[...] * dsz[:, 0:1024]
        dz_ref[:, 1024:1536] = d[:, 1024:1536] * b_ref[...] * dsz[:, 1024:1536]
        dz_ref[:, 1536:2048] = d[:, 1536:2048] * c_ref[...] * dsz[:, 1536:2048]

    wide = pl.BlockSpec((tr, 2048), lambda i: (i, 0))
    return _pcall(
        kern, name="gate_bwd", grid=(s // tr,),
        in_specs=[wide, pl.BlockSpec((tr, 1024), lambda i: (i, 0)), pl.BlockSpec((tr, 512), lambda i: (i, 0)),
                  pl.BlockSpec((tr, 512), lambda i: (i, 0)), pl.BlockSpec((tr, 2048), lambda i: (i, 2))],
        out_specs=[wide, wide],
        out_shape=[jax.ShapeDtypeStruct((s, 2048), F32)] * 2)(du, o_mla, o_nsa, o_mem, hp)


def _tile2d(rows, cols, arrays):
    if rows % 16 == 0:
        return _row_tile(rows, cols * arrays), cols
    want = max(LANE, BLOCK_BYTES // (rows * 4 * arrays) // LANE * LANE)
    tc = LANE
    for t in range(LANE, cols + 1, LANE):
        if cols % t == 0 and t <= want:
            tc = t
    return rows, tc


def _sum_slots(buf, name):
    n, rows, cols = buf.shape
    tr, tc = _tile2d(rows, cols, n)

    def kern(b_ref, o_ref):
        acc = b_ref[0].astype(F32)
        for i in range(1, n):
            acc = acc + b_ref[i].astype(F32)
        o_ref[...] = acc

    return _pcall(kern, name=name, grid=(rows // tr, cols // tc),
                  in_specs=[pl.BlockSpec((n, tr, tc), lambda i, j: (0, i, j))],
                  out_specs=pl.BlockSpec((tr, tc), lambda i, j: (i, j)),
                  out_shape=jax.ShapeDtypeStruct((rows, cols), F32))(buf)


def _chip_sum(buf, core, axis, name):
    n, rows, cols = buf.shape
    tr, tc = _tile2d(rows, cols, n)
    nbr, nbc = rows // tr, cols // tc

    def kern(c_ref, b_ref, o_ref, w_ref):
        acc = b_ref[0].astype(F32)
        for i in range(1, n):
            acc = acc + b_ref[i].astype(F32)
        o_ref[...] = acc
        w_ref[...] = acc

    place = ((lambda i, j, c: (c[0] * nbr + i, j)) if axis == 0 else (lambda i, j, c: (i, c[0] * nbc + j)))
    whole = (2 * rows, cols) if axis == 0 else (rows, 2 * cols)
    grid_spec = pltpu.PrefetchScalarGridSpec(
        num_scalar_prefetch=1, grid=(nbr, nbc),
        in_specs=[pl.BlockSpec((n, tr, tc), lambda i, j, c: (0, i, j))],
        out_specs=[pl.BlockSpec((tr, tc), lambda i, j, c: (i, j)), pl.BlockSpec((tr, tc), place)])
    return _pcall(kern, name=name, grid_spec=grid_spec,
                  out_shape=[jax.ShapeDtypeStruct((rows, cols), F32), jax.ShapeDtypeStruct(whole, F32)])(core, buf)


def _pair_sum(g4, theirs, core, axis, name):
    n, rows, cols = theirs.shape
    tr, tc = _tile2d(rows, cols, 1)
    nbr, nbc = rows // tr, cols // tc

    def kern(c_ref, a_ref, b_ref, o_ref):
        o_ref[...] = (a_ref[...] + b_ref[...]).astype(BF16)

    blk = (1, tr, tc)
    mine = ((lambda s, i, j, c: (s, c[0] * nbr + i, j)) if axis == 0
            else (lambda s, i, j, c: (s, i, c[0] * nbc + j)))
    grid_spec = pltpu.PrefetchScalarGridSpec(
        num_scalar_prefetch=1, grid=(n, nbr, nbc),
        in_specs=[pl.BlockSpec(blk, mine), pl.BlockSpec(blk, lambda s, i, j, c: (s, i, j))],
        out_specs=pl.BlockSpec(blk, lambda s, i, j, c: (s, i, j)))
    return _pcall(kern, name=name, grid_spec=grid_spec,
                  out_shape=jax.ShapeDtypeStruct((n, rows, cols), BF16))(core, g4, theirs)


def _adamw(w, g, m, v, name):
    rows, cols = w.shape
    tr, tc = _tile2d(rows, cols, 4)
    bc1 = 1.0 - ADAM_B1 ** ADAM_STEP
    bc2 = 1.0 - ADAM_B2 ** ADAM_STEP

    def kern(w_ref, g_ref, m_ref, v_ref, d_ref, mo_ref, vo_ref):
        gv = g_ref[...]
        mn = ADAM_B1 * m_ref[...] + (1.0 - ADAM_B1) * gv
        vn = ADAM_B2 * v_ref[...] + (1.0 - ADAM_B2) * (gv * gv)
        d_ref[...] = -ADAM_LR * ((mn / bc1) / (jnp.sqrt(vn / bc2) + ADAM_EPS) + ADAM_WD * w_ref[...])
        mo_ref[...] = mn
        vo_ref[...] = vn

    blk = pl.BlockSpec((tr, tc), lambda i, j: (i, j))
    return _pcall(kern, name=name, grid=(rows // tr, cols // tc), in_specs=[blk] * 4, out_specs=[blk] * 3,
                  out_shape=[jax.ShapeDtypeStruct((rows, cols), F32)] * 3)(w, g, m, v)


ANY = pl.BlockSpec(memory_space=pl.ANY)


def _place():
    x, y, c = lax.axis_index("x"), lax.axis_index("y"), lax.axis_index("c")
    chips = [(1 - x, y), (x, 1 - y), (1 - x, 1 - y)]
    return x, y, c, chips


def _remote(src, dst, send_sem, recv_sem, to):
    return pltpu.make_async_remote_copy(src_ref=src, dst_ref=dst, send_sem=send_sem, recv_sem=recv_sem,
                                        device_id=to, device_id_type=MESH)


def _half(ref, lead, core, axis):
    size = ref.shape[len(lead) + axis] // 2
    cut = pl.ds(core * size, size)
    return ref.at[tuple(lead) + ((cut, slice(None)) if axis == 0 else (slice(None), cut))]


def _gather_shards(ws, axes):
    side = _gather_side(ws, axes)

    def body(*refs):
        nw = len(ws)
        split = (refs[:nw], refs[nw:2 * nw], refs[2 * nw:])
        side.phase("start", *split)
        side.phase("finish", *split)

    return _pcall(body, name="gather_shards", in_specs=[ANY] * len(ws), out_specs=[ANY] * len(ws),
                  out_shape=side.out_shape, scratch_shapes=side.scratch)(*ws)


def _gather_side(ws, axes):
    nw = len(ws)

    def phase(which, w_refs, out_refs, sems):
        send_sems, recv_sems = sems
        x, y, c, chips = _place()
        me = 2 * x + y
        sibling = (x, y, 1 - c)

        def part(i, slot, core):
            return _half(out_refs[i], (slot,), core, axes[i])

        def copy(sem, src, dst, to):
            return _remote(src, dst, send_sems.at[sem], recv_sems.at[sem], to)

        first = [copy(j * nw + i, _half(w_refs[i], (), c, axes[i]), part(i, me, c), (*chip, c))
                 for j, chip in enumerate(chips) for i in range(nw)]
        if which == "start":
            for cp in first:
                cp.start()
            return
        passed = []
        for j, (cx, cy) in enumerate(chips):
            slot = 2 * cx + cy
            for i in range(nw):
                copy(j * nw + i, part(i, slot, c), part(i, slot, c), (x, y, c)).wait_recv()
                fwd = copy((3 + j) * nw + i, part(i, slot, c), part(i, slot, c), sibling)
                fwd.start()
                passed.append(fwd)
        for j, (cx, cy) in enumerate(chips):
            slot = 2 * cx + cy
            for i in range(nw):
                copy((3 + j) * nw + i, part(i, slot, 1 - c), part(i, slot, 1 - c), (x, y, c)).wait_recv()
        for cp in first + passed:
            cp.wait_send()

    return _Side(list(ws), [jax.ShapeDtypeStruct((4,) + w.shape, w.dtype) for w in ws],
                 [pltpu.SemaphoreType.DMA((6 * nw,)), pltpu.SemaphoreType.DMA((6 * nw,))], phase)


def _half_shape(shape, axis):
    return tuple(d // 2 if k == len(shape) - 2 + axis else d for k, d in enumerate(shape))


def _pair_exchange(gs, axes, name):
    nw = len(gs)

    def body(*refs):
        g_refs, out_refs = refs[:nw], refs[nw:2 * nw]
        send_sems, recv_sems = refs[2 * nw:]
        x, y, c, _ = _place()
        cps = []
        for i in range(nw):
            cp = _remote(_half(g_refs[i], (slice(None),), 1 - c, axes[i]), out_refs[i],
                         send_sems.at[i], recv_sems.at[i], (x, y, 1 - c))
            cp.start()
            cps.append(cp)
        for cp in cps:
            cp.wait()

    return _pcall(body, name=name, in_specs=[ANY] * nw, out_specs=[ANY] * nw,
                  out_shape=[jax.ShapeDtypeStruct(_half_shape(g.shape, a), g.dtype) for g, a in zip(gs, axes)],
                  scratch_shapes=[pltpu.SemaphoreType.DMA((nw,)), pltpu.SemaphoreType.DMA((nw,))])(*gs)


def _chip_exchange(ps):
    side = _chip_side(ps)

    def body(*refs):
        nw = len(ps)
        split = (refs[:nw], refs[nw:2 * nw], refs[2 * nw:])
        side.phase("start", *split)
        side.phase("finish", *split)

    return _pcall(body, name="chip_exchange", in_specs=[ANY] * len(ps), out_specs=[ANY] * len(ps),
                  out_shape=side.out_shape, scratch_shapes=side.scratch)(*ps)


def _chip_side(ps):
    nw = len(ps)

    def phase(which, p_refs, out_refs, sems):
        send_sems, recv_sems, local_sems = sems
        x, y, c, chips = _place()
        me = 2 * x + y
        mine = [pltpu.make_async_copy(p_refs[i].at[me], out_refs[i].at[me], local_sems.at[i]) for i in range(nw)]
        sends = [_remote(p_refs[i].at[2 * cx + cy], out_refs[i].at[me], send_sems.at[j * nw + i],
                         recv_sems.at[j * nw + i], (cx, cy, c))
                 for j, (cx, cy) in enumerate(chips) for i in range(nw)]
        if which == "start":
            for cp in mine + sends:
                cp.start()
            return
        for j, (cx, cy) in enumerate(chips):
            slot = 2 * cx + cy
            for i in range(nw):
                _remote(out_refs[i].at[slot], out_refs[i].at[slot], send_sems.at[j * nw + i],
                        recv_sems.at[j * nw + i], (x, y, c)).wait_recv()
        for cp in sends:
            cp.wait_send()
        for cp in mine:
            cp.wait()

    return _Side(list(ps), [jax.ShapeDtypeStruct(p.shape, p.dtype) for p in ps],
                 [pltpu.SemaphoreType.DMA((3 * nw,)), pltpu.SemaphoreType.DMA((3 * nw,)),
                  pltpu.SemaphoreType.DMA((nw,))], phase)


def _half_exchange(ts, wholes, axes):
    nw = len(ts)

    def body(*refs):
        t_refs, out_refs = refs[:nw], refs[2 * nw:3 * nw]
        send_sems, recv_sems = refs[3 * nw:]
        x, y, c, _ = _place()
        sends = []
        for i in range(nw):
            cp = _remote(t_refs[i], _half(out_refs[i], (), c, axes[i]), send_sems.at[i], recv_sems.at[i],
                         (x, y, 1 - c))
            cp.start()
            sends.append(cp)
        for i in range(nw):
            _remote(t_refs[i], _half(out_refs[i], (), 1 - c, axes[i]), send_sems.at[i], recv_sems.at[i],
                    (x, y, c)).wait_recv()
        for cp in sends:
            cp.wait_send()

    return _pcall(body, name="half_exchange", in_specs=[ANY] * (2 * nw), out_specs=[ANY] * nw,
                  out_shape=[jax.ShapeDtypeStruct(w.shape, w.dtype) for w in wholes],
                  input_output_aliases={nw + i: i for i in range(nw)},
                  scratch_shapes=[pltpu.SemaphoreType.DMA((nw,)), pltpu.SemaphoreType.DMA((nw,))])(*ts, *wholes)


def _gather_all(v):
    rows, cols = v.shape

    def body(v_ref, out_ref, send_sems, recv_sems, local_sem):
        x, y, c, _ = _place()
        me = 4 * x + 2 * y + c
        mine = pltpu.make_async_copy(v_ref, out_ref.at[me], local_sem)
        mine.start()
        sends = []
        for d in range(1, 8):
            peer = (x ^ (d >> 2), y ^ ((d >> 1) & 1), c ^ (d & 1))
            cp = _remote(v_ref, out_ref.at[me], send_sems.at[d - 1], recv_sems.at[d - 1], peer)
            cp.start()
            sends.append(cp)
        for d in range(1, 8):
            slot = 4 * (x ^ (d >> 2)) + 2 * (y ^ ((d >> 1) & 1)) + (c ^ (d & 1))
            _remote(v_ref, out_ref.at[slot], send_sems.at[d - 1], recv_sems.at[d - 1], (x, y, c)).wait_recv()
        for cp in sends:
            cp.wait_send()
        mine.wait()

    return _pcall(body, name="gather_all", in_specs=[ANY], out_specs=ANY,
                  out_shape=jax.ShapeDtypeStruct((8, rows, cols), v.dtype),
                  scratch_shapes=[pltpu.SemaphoreType.DMA((7,)), pltpu.SemaphoreType.DMA((7,)),
                                  pltpu.SemaphoreType.DMA])(v)


def _pad_cols(a, width):
    return a if a.shape[1] == width else jnp.pad(a, ((0, 0), (0, width - a.shape[1])))


def _unpad_segments():
    z = PAD["z"]
    segs = [(PAD["c_q"], 0, 512), (PAD["c_kv"], 512, 512), (PAD["k_rope"], 1024, 64), (z, 1088, 1024)]
    segs += [(PAD["q_nsa"] + 256 * h, 2112 + NSA_DK * h, NSA_DK) for h in range(NSA_HEADS)]
    for name, rows in (("k_c", 192), ("v_c", 128), ("k_s", 192), ("v_s", 128), ("k_w", 192), ("v_w", 128),
                       ("g_nsa", 12)):
        segs.append((PAD[name], ORIG[name][0], rows))
    segs += [(z + 1024, ORIG["z_nsa"][0], 512), (PAD["q_mem"], ORIG["q_mem"][0], 512),
             (z + 1536, ORIG["z_mem"][0], 512)]
    return segs


def _w_in_grad_slots(gt):
    rows, cols = gt.shape
    shard = sum(n for _, _, n in _unpad_segments()) // 4
    tc = 256
    pieces = []
    for src, dst, n in _unpad_segments():
        while n:
            slot, off = divmod(dst, shard)
            take = min(n, shard - off)
            pieces.append((src, slot, off, take))
            src, dst, n = src + take, dst + take, n - take

    def kern(g_ref, o_ref):
        for src, slot, off, take in pieces:
            o_ref[slot, off:off + take, :] = g_ref[src:src + take, :]

    return _pcall(kern, name="w_in_grad_slots", grid=(cols // tc,),
                  in_specs=[pl.BlockSpec((rows, tc), lambda i: (0, i))],
                  out_specs=pl.BlockSpec((4, shard, tc), lambda i: (0, 0, i)),
                  out_shape=jax.ShapeDtypeStruct((4, shard, cols), F32))(gt)


def _w_in_from_slots(ws):
    nslot, shard, cols = ws.shape
    tc = 256
    pieces = []
    for dst, src, n in _unpad_segments() + [(PAD["k_rope"] + 64, ORIG["k_rope"][0], 64)]:
        while n:
            slot, off = divmod(src, shard)
            take = min(n, shard - off)
            pieces.append((dst, slot, off, take))
            src, dst, n = src + take, dst + take, n - take

    def kern(w_ref, o_ref):
        o_ref[...] = jnp.zeros_like(o_ref)
        for dst, slot, off, take in pieces:
            o_ref[dst:dst + take, :] = w_ref[slot, off:off + take, :]

    return _pcall(kern, name="w_in_from_slots", grid=(cols // tc,),
                  in_specs=[pl.BlockSpec((nslot, shard, tc), lambda i: (0, 0, i))],
                  out_specs=pl.BlockSpec((D_PAD, tc), lambda i: (0, i)),
                  out_shape=jax.ShapeDtypeStruct((D_PAD, cols), ws.dtype))(ws)


def _rope_tables(s):
    pos = jnp.arange(s, dtype=F32)
    inv_freq = ROPE_THETA ** (-jnp.arange(0, 64, 2, dtype=F32) / 64)
    ang = pos[:, None] * inv_freq[None, :]
    cos, sin = jnp.cos(ang), jnp.sin(ang)
    z = jnp.zeros((s, 64), F32)
    return jnp.concatenate([cos, cos, z], axis=1), jnp.concatenate([-sin, sin, z], axis=1)


def _overlap_table(s):
    n_c, n_s = s // CMP_STRIDE, s // SLC_LEN
    c0 = np.arange(n_c)[:, None] * CMP_STRIDE
    s0 = np.arange(LANE)[None, :] * SLC_LEN
    ov = (c0 < s0 + SLC_LEN) & (c0 + CMP_LEN > s0) & (np.arange(n_c)[:, None] < n_c - 1) & (np.arange(LANE)[None, :] < n_s)
    return jnp.asarray(ov.astype(np.float32), dtype=BF16)


def _shift_down(a):
    return jnp.concatenate([jnp.zeros((8, a.shape[1]), a.dtype), a], axis=0)[7:7 + a.shape[0]]


def _shift_up(a):
    return jnp.concatenate([a, jnp.zeros((8, a.shape[1]), a.dtype)], axis=0)[1:1 + a.shape[0]]


def _local_step(x, mem, target, w, hooks=None):
    s = x.shape[0]
    cs, sn = _rope_tables(s)
    t_ = jnp.transpose

    w_in_p = _w_in_from_slots(w["w_in_t"])
    xn, rstd_x = _rms_fwd(_Src(x, D_MODEL), w["norm_g"], "norm_x")
    if hooks is None:
        hp, hpb = _mm(xn, w_in_p, "in_proj", mode="nt", second_dtype=BF16)
    else:
        hp, hpb, *gathered = _mm(xn, w_in_p, "in_proj", mode="nt", second_dtype=BF16, side=hooks.gather_side)
        w = {**w, **hooks.weights(gathered)}

    w_uq3 = w["w_uq"].reshape(512, MLA_HEADS, 192)
    w_uq_p = jnp.concatenate([w_uq3, w_uq3[:, :, 128:]], axis=2).reshape(512, MLA_HEADS * 256)
    w_ukv_p = t_(w["w_ukv"].reshape(512, MLA_HEADS, 2, 128), (0, 2, 1, 3)).reshape(512, 2048)
    c_q, c_kv = _Src(hp, 512, 0), _Src(hp, 512, 1)
    cqn, rstd_q = _rms_fwd(c_q, w["q_norm_g"], "norm_q")
    ckvn, rstd_kv = _rms_fwd(c_kv, w["kv_norm_g"], "norm_kv")
    q_lin = _mm(cqn, w_uq_p, "mla_q_proj")
    kvb = _mm(ckvn, w_ukv_p, "mla_kv_proj", out_dtype=BF16)
    q_mla = _rope_fwd(_Src(q_lin, MLA_HEADS * 256), cs, sn, MLA_HEADS, 256, LANE, "rope_q")
    k_pe = _rope_fwd(_Src(hp, LANE, PAD["k_rope"] // LANE), cs, sn, 1, LANE, 0, "rope_k")
    mla = _Attn("mla", s, s, MLA_HEADS, 256)
    mla_q, mla_v = _Src(q_mla, 256), _Src(kvb, LANE, MLA_HEADS)
    mla_k = [_Src(kvb, LANE), _Src(k_pe, LANE, 0, False)]
    o_mla, l_mla, lr_mla = _attn_fwd(mla, mla_q, mla_k, mla_v, None, "mla_fwd")

    sk = s // CMP_STRIDE
    pe_k, pe_v = w["cmp_pe_k"], w["cmp_pe_v"]
    w1k = _pad_cols(w["cmp_w1k"], 256)
    w2k = jnp.pad(w["cmp_w2k"], ((0, 64), (0, 64))).astype(BF16)
    w1v, w2v = w["cmp_w1v"], w["cmp_w2v"].astype(BF16)
    half_k, half_v = CMP_STRIDE * NSA_DK, CMP_STRIDE * HEAD_V
    ak = hp[:, PAD["k_c"]:PAD["k_c"] + NSA_DK].reshape(sk, half_k)
    av = hp[:, PAD["v_c"]:PAD["v_c"] + HEAD_V].reshape(sk, half_v)
    ck_args = (ak, _shift_up(ak), pe_k[:CMP_STRIDE].reshape(1, half_k), pe_k[CMP_STRIDE:].reshape(1, half_k),
               w1k[:half_k], w1k[half_k:], w2k)
    cv_args = (av, _shift_up(av), pe_v[:CMP_STRIDE].reshape(1, half_v), pe_v[CMP_STRIDE:].reshape(1, half_v),
               w1v[:half_v], w1v[half_v:], w2v)
    k_cmp, pre_k = _compress_fwd(*ck_args, "compress_k")
    v_cmp, pre_v = _compress_fwd(*cv_args, "compress_v")
    cmp_ = _Attn("cmp", s, sk, NSA_HEADS, 256)
    slc = _Attn("slc", s, s, NSA_HEADS, 256)
    win = _Attn("win", s, s, NSA_HEADS, 256)
    nsa_q = _Src(hpb, 256, PAD["q_nsa"] // 256)
    cmp_k, cmp_v = [_Src(k_cmp, 256, 0, False)], _Src(v_cmp, HEAD_V, 0, False)
    slc_k, slc_v = [_Src(hpb, 256, PAD["k_s"] // 256, False)], _Src(hpb, HEAD_V, PAD["v_s"] // HEAD_V, False)
    win_k, win_v = [_Src(hpb, 256, PAD["k_w"] // 256, False)], _Src(hpb, HEAD_V, PAD["v_w"] // HEAD_V, False)
    o_cmp, l_cmp, lr_cmp, sel, selt = _attn_fwd_small(cmp_, nsa_q, cmp_k, cmp_v, "cmp_fwd", _overlap_table(s))
    o_slc, l_slc, lr_slc = _attn_fwd(slc, nsa_q, slc_k, slc_v, sel, "slc_fwd")
    o_win, l_win, lr_win = _attn_fwd_small(win, nsa_q, win_k, win_v, "win_fwd")
    gl = _Src(hp, LANE, PAD["g_nsa"] // LANE)
    o_nsa = _nsa_combine(o_cmp, o_slc, o_win, gl)

    mn, rstd_m = _rms_fwd(_Src(mem, D_MODEL), w["mem_norm_g"], "norm_mem")
    kvm = _mm(mn, w["w_mem_kv"], "mem_kv_proj", out_dtype=BF16)
    mem_ = _Attn("mem", s, mem.shape[0], MEM_HEADS, LANE)
    mem_q, mem_k, mem_v = _Src(hpb, LANE, PAD["q_mem"] // LANE), [_Src(kvm, LANE)], _Src(kvm, LANE, MEM_HEADS)
    o_mem, l_mem, lr_mem = _attn_fwd_small(mem_, mem_q, mem_k, mem_v, "mem_fwd")

    u = _gate_fwd(o_mla, o_nsa, o_mem, hp)
    proj = _mm(u, w["w_out"], "out_proj")
    dy, g_final, loss = _final_loss(x, proj, w["final_norm_g"].reshape(1, -1), target)

    g_w_out = _mm(u, dy, "out_proj_dw", mode="tn")
    du = _mm(dy, w["w_out"], "out_proj_dx", mode="nt")
    do_cat, dz = _gate_bwd(du, o_mla, o_nsa, o_mem, hp)

    dq_mem, (dk_mem,), dv_mem = _attn_bwd(mem_, mem_q, mem_k, mem_v, None, None, _Src(o_mem, HEAD_V), l_mem,
                                          lr_mem, _Src(do_cat, HEAD_V, 12), None, "mem_bwd")
    dkvm = jnp.concatenate([dk_mem, dv_mem], axis=1)
    g_w_mem_kv = _mm(mn, dkvm, "mem_kv_dw", mode="tn")
    dmn = _mm(dkvm, w["w_mem_kv"], "mem_kv_dx", mode="nt")
    _, g_mem_norm = _rms_bwd(_Src(mem, D_MODEL), w["mem_norm_g"], rstd_m, dmn, None, "norm_mem_bwd")

    do_cmp, do_slc, do_win, dgl = _nsa_combine_bwd(do_cat, o_cmp, o_slc, o_win, gl)
    dq_n, (dk_cmp,), dv_cmp = _attn_bwd(cmp_, nsa_q, cmp_k, cmp_v, None, None, _Src(o_cmp, HEAD_V), l_cmp,
                                        lr_cmp, _Src(do_cmp, HEAD_V), None, "cmp_bwd")
    dq_n, (dk_s,), dv_s = _attn_bwd(slc, nsa_q, slc_k, slc_v, sel, selt, _Src(o_slc, HEAD_V), l_slc, lr_slc,
                                    _Src(do_slc, HEAD_V), dq_n, "slc_bwd")
    dq_n, (dk_w,), dv_w = _attn_bwd(win, nsa_q, win_k, win_v, None, None, _Src(o_win, HEAD_V), l_win, lr_win,
                                    _Src(do_win, HEAD_V), dq_n, "win_bwd")
    dak, dpk_lo, dpk_hi, dw1k_lo, dw1k_hi, g_w2k = _compress_bwd(
        *ck_args, pre_k, _shift_down(pre_k), dk_cmp, _shift_down(dk_cmp), "compress_k_bwd")
    dav, dpv_lo, dpv_hi, dw1v_lo, dw1v_hi, g_w2v = _compress_bwd(
        *cv_args, pre_v, _shift_down(pre_v), dv_cmp, _shift_down(dv_cmp), "compress_v_bwd")
    g_pe_k = jnp.concatenate([dpk_lo.reshape(CMP_STRIDE, NSA_DK), dpk_hi.reshape(CMP_STRIDE, NSA_DK)], axis=0)
    g_pe_v = jnp.concatenate([dpv_lo.reshape(CMP_STRIDE, HEAD_V), dpv_hi.reshape(CMP_STRIDE, HEAD_V)], axis=0)
    g_w1k = jnp.concatenate([dw1k_lo, dw1k_hi], axis=0)[:, :NSA_DK]
    g_w1v = jnp.concatenate([dw1v_lo, dw1v_hi], axis=0)
    dk_c = _pad_cols(dak.reshape(s, NSA_DK), 256)
    dv_c = dav.reshape(s, HEAD_V)

    dq_m, (dk_nope, dk_pe), dv_m = _attn_bwd(mla, mla_q, mla_k, mla_v, None, None, _Src(o_mla, HEAD_V), l_mla,
                                             lr_mla, _Src(do_cat, HEAD_V), None, "mla_bwd")
    dq_lin = _rope_bwd_q(dq_m, cs, sn)
    dkv_lin, d_krope = _rope_bwd_k(dk_nope, dk_pe, dv_m, cs, sn)
    g_w_uq_p = _mm(cqn, dq_lin, "mla_q_dw", mode="tn")
    dcqn = _mm(dq_lin, w_uq_p, "mla_q_dx", mode="nt")
    g_w_ukv_p = _mm(ckvn, dkv_lin, "mla_kv_dw", mode="tn")
    dckvn = _mm(dkv_lin, w_ukv_p, "mla_kv_dx", mode="nt")
    dc_q, g_q_norm = _rms_bwd(c_q, w["q_norm_g"], rstd_q, dcqn, None, "norm_q_bwd")
    dc_kv, g_kv_norm = _rms_bwd(c_kv, w["kv_norm_g"], rstd_kv, dckvn, None, "norm_kv_bwd")
    g_w_uq = g_w_uq_p.reshape(512, MLA_HEADS, 256)[:, :, :192].reshape(512, MLA_HEADS * 192)
    g_w_ukv = t_(g_w_ukv_p.reshape(512, 2, MLA_HEADS, 128), (0, 2, 1, 3)).reshape(512, 2048)

    dhp = jnp.concatenate(
        [dc_q, dc_kv, dq_n, dk_c, dk_s, dk_w, d_krope, dv_c, dv_s, dv_w, dgl,
         jnp.zeros((s, PAD["q_mem"] - (PAD["g_nsa"] + LANE)), F32), dq_mem, dz], axis=1).astype(BF16)
    grads = dict(q_norm_g=g_q_norm, w_uq=g_w_uq, kv_norm_g=g_kv_norm,
                 w_ukv=g_w_ukv, cmp_pe_k=g_pe_k, cmp_pe_v=g_pe_v, cmp_w1k=g_w1k, cmp_w2k=g_w2k[:NSA_DK, :NSA_DK],
                 cmp_w1v=g_w1v, cmp_w2v=g_w2v, mem_norm_g=g_mem_norm, w_mem_kv=g_w_mem_kv, w_out=g_w_out,
                 final_norm_g=g_final.reshape(-1))
    if hooks is None:
        g_w_in_t = _w_in_grad_slots(_mm(dhp, xn, "in_proj_dw", mode="tn", wide=2048))
        dxn = _mm(dhp, w_in_p, "in_proj_dx", wide=2048)
    else:
        g_w_in_p, *hooks.received = _mm(dhp, xn, "in_proj_dw", mode="tn", wide=2048, side=hooks.reduce_side(grads))
        g_w_in_t = _w_in_grad_slots(g_w_in_p)
        dxn, hooks.received_w_in = _mm(dhp, w_in_p, "in_proj_dx", wide=2048, side=hooks.reduce_side_w_in(g_w_in_t))
    grad_x, g_norm = _rms_bwd(_Src(x, D_MODEL), w["norm_g"], rstd_x, dxn, dy, "norm_x_bwd")
    grads.update(norm_g=g_norm, w_in_t=g_w_in_t)
    return loss[0, 0], grad_x, grads


def kernel(x, mem, norm_g, w_in, q_norm_g, w_uq, kv_norm_g, w_ukv, cmp_pe_k, cmp_pe_v, cmp_w1k, cmp_w2k, cmp_w1v, cmp_w2v, mem_norm_g, w_mem_kv, w_out, final_norm_g, loss_target, m_norm_g, m_w_in, m_q_norm_g, m_w_uq, m_kv_norm_g, m_w_ukv, m_cmp_pe_k, m_cmp_pe_v, m_cmp_w1k, m_cmp_w2k, m_cmp_w1v, m_cmp_w2v, m_mem_norm_g, m_w_mem_kv, m_w_out, m_final_norm_g, v_norm_g, v_w_in, v_q_norm_g, v_w_uq, v_kv_norm_g, v_w_ukv, v_cmp_pe_k, v_cmp_pe_v, v_cmp_w1k, v_cmp_w2k, v_cmp_w1v, v_cmp_w2v, v_mem_norm_g, v_w_mem_kv, v_w_out, v_final_norm_g):
    args = dict(locals())
    wts = {n: args[n] for n in WEIGHTS}
    loc = {n: (a if n == "final_norm_g" else a[0]) for n, a in wts.items()}

    def to_x(n, a):
        return a.T if n == "w_in" else a

    split = [1 if n == "w_in" else 0 for n in SHARDED]
    rest = [n for n in SHARDED if n != "w_in"]
    chip = 2 * lax.axis_index("x") + lax.axis_index("y")
    core = lax.axis_index("c").astype(jnp.int32).reshape(1)
    own = {n: to_x(n, loc[n]).astype(BF16) for n in SHARDED}

    def with_own_slot(gw, a):
        return lax.dynamic_update_slice(gw, a[None], (chip, 0, 0))

    def slots(n, a):
        if n == "w_in":
            return a
        if SHARD_AXIS[n] == 0:
            return a.reshape(4, a.shape[0] // 4, a.shape[1])
        width = a.shape[1] // 4
        return jnp.stack([a[:, j * width:(j + 1) * width] for j in range(4)])

    def pair_sums(names, grads, name):
        axes = [1 if n == "w_in" else 0 for n in names]
        gs = [slots(n, a) for n, a in zip(names, grads)]
        theirs = _pair_exchange(gs, axes, name)
        return [_pair_sum(a, b, core, ax, "pair_sum_" + n) for n, a, b, ax in zip(names, gs, theirs, axes)]

    class Hooks:
        gather_side = _gather_side([own[n] for n in rest], [0] * len(rest))
        received = None

        @staticmethod
        def weights(gathered):
            out = {}
            for n, gw in zip(rest, gathered):
                gw = with_own_slot(gw, own[n])
                if SHARD_AXIS[n] == 0:
                    out[n] = gw.reshape(4 * gw.shape[1], gw.shape[2])
                else:
                    out[n] = jnp.concatenate([gw[j] for j in range(4)], axis=1)
            return out

        @staticmethod
        def reduce_side(grads):
            return _chip_side(pair_sums(rest, [grads[n] for n in rest], "pair_exchange_rest"))

        @staticmethod
        def reduce_side_w_in(g_w_in_t):
            return _chip_side(pair_sums(["w_in"], [g_w_in_t], "pair_exchange_w_in"))

    hooks = Hooks()

    start = {n: loc[n].reshape(1, -1) if loc[n].ndim == 1 else loc[n] for n in REPLICATED}
    start["w_in_t"] = with_own_slot(_gather_shards([own["w_in"]], [1])[0], own["w_in"])
    loss, grad_x, g = _local_step(x[0], mem[0], loss_target[0], start, hooks)
    loss = lax.psum(loss, ("x", "y", "c"))

    from_chips = dict(zip(rest, hooks.received), w_in=hooks.received_w_in)
    sums = [_chip_sum(from_chips[n], core, ax, "chip_sum_" + n) for n, ax in zip(SHARDED, split)]
    g_sh = _half_exchange([a for a, _ in sums], [b for _, b in sums], split)

    n_rep = sum(int(np.prod(loc[n].shape)) for n in REPLICATED)
    rows_rep = -(-n_rep // (8 * LANE)) * 8

    def rep_pack(parts):
        flat = jnp.concatenate([p.reshape(-1) for p in parts])
        return jnp.pad(flat, (0, rows_rep * LANE - n_rep)).reshape(rows_rep, LANE)

    g_rep = _sum_slots(_gather_all(rep_pack([g[n] for n in REPLICATED])), "replica_sum")
    d_rp, m_rp, v_rp = _adamw(rep_pack([wts[n] for n in REPLICATED]), g_rep,
                              rep_pack([args["m_" + n] for n in REPLICATED]),
                              rep_pack([args["v_" + n] for n in REPLICATED]), "adamw_replicated")

    def rep_unpack(buf):
        flat, out, o = buf.reshape(-1), {}, 0
        for n in REPLICATED:
            size = int(np.prod(wts[n].shape))
            out[n] = flat[o:o + size].reshape(wts[n].shape)
            o += size
        return out

    outs = {k: rep_unpack(b) for k, b in (("g", g_rep), ("d", d_rp), ("m", m_rp), ("v", v_rp))}
    for n, gn in zip(SHARDED, g_sh):
        d, mo, vo = _adamw(to_x(n, loc[n]), gn, to_x(n, args["m_" + n][0]), to_x(n, args["v_" + n][0]),
                           "adamw_" + n)
        for k, a in (("g", gn), ("d", d), ("m", mo), ("v", vo)):
            outs[k][n] = to_x(n, a).reshape(wts[n].shape)

    return (loss, grad_x[None], *[outs["g"][n] for n in WEIGHTS], *[outs["d"][n] for n in WEIGHTS],
            *[outs["m"][n] for n in WEIGHTS], *[outs["v"][n] for n in WEIGHTS])
```

```python
from typing import NamedTuple

import numpy as np
import jax
import jax.numpy as jnp
from jax import lax
from jax.experimental import pallas as pl
from jax.experimental.pallas import tpu as pltpu

F32 = jnp.float32
BF16 = jnp.bfloat16
MESH = pl.DeviceIdType.MESH

D_MODEL = 2048
EPS = 1e-6
LANE = 128
HEAD_V = 128
MLA_HEADS = 8
NSA_HEADS = 4
MEM_HEADS = 4
NSA_DK = 192
CMP_STRIDE = 16
CMP_LEN = 32
SLC_LEN = 64
SLC_TOPN = 16
WIN = 512
NEG = -1e30
LOG2E = 1.4426950408889634
ROPE_THETA = 10000.0
BLOCK_BYTES = 2 << 20

ORIG = dict(c_q=(0, 512), c_kv=(512, 512), k_rope=(1024, 64), z_mla=(1088, 1024),
            q_nsa=(2112, 768), k_c=(2880, 192), v_c=(3072, 128), k_s=(3200, 192),
            v_s=(3392, 128), k_w=(3520, 192), v_w=(3712, 128), g_nsa=(3840, 12),
            z_nsa=(3852, 512), q_mem=(4364, 512), z_mem=(4876, 512))
PAD = dict(c_q=0, c_kv=512, q_nsa=1024, k_c=2048, k_s=2304, k_w=2560, k_rope=2816, v_c=2944,
           v_s=3072, v_w=3200, g_nsa=3328, q_mem=3584, z=4096)
D_PAD = 6144

ADAM_LR, ADAM_B1, ADAM_B2, ADAM_EPS, ADAM_WD, ADAM_STEP = 0.001, 0.9, 0.999, 1e-08, 0.01, 10

SHARDED = ("w_in", "w_uq", "w_ukv", "cmp_w1k", "cmp_w1v", "w_mem_kv", "w_out")
SHARD_AXIS = dict(w_in=1, w_uq=1, w_ukv=1, cmp_w1k=0, cmp_w1v=0, w_mem_kv=0, w_out=0)
REPLICATED = ("norm_g", "q_norm_g", "kv_norm_g", "cmp_pe_k", "cmp_pe_v", "cmp_w2k", "cmp_w2v",
              "mem_norm_g", "final_norm_g")
WEIGHTS = ("norm_g", "w_in", "q_norm_g", "w_uq", "kv_norm_g", "w_ukv", "cmp_pe_k", "cmp_pe_v",
           "cmp_w1k", "cmp_w2k", "cmp_w1v", "cmp_w2v", "mem_norm_g", "w_mem_kv", "w_out",
           "final_norm_g")


def _pcall(kernel, **kw):
    return pl.pallas_call(kernel, **kw)


def _tile(n, pref):
    if n <= pref:
        return n
    for t in range(pref, LANE - 1, -LANE):
        if n % t == 0:
            return t
    raise ValueError((n, pref))


def _row_tile(rows, cols, itemsize=4):
    want = max(16, BLOCK_BYTES // (cols * itemsize))
    if rows <= want:
        return rows
    t = 16
    best = rows
    while t <= want:
        if rows % t == 0:
            best = t
        t *= 2
    return best


def _nt(a, b):
    return lax.dot_general(a, b, (((1,), (1,)), ((), ())), preferred_element_type=F32)


def _tn(a, b):
    return lax.dot_general(a, b, (((0,), (0,)), ((), ())), preferred_element_type=F32)


def _nn(a, b):
    return jnp.dot(a, b, preferred_element_type=F32)


def _sigmoid(x):
    return 1.0 / (1.0 + jnp.exp(-x))


class _Src(NamedTuple):
    arr: jax.Array
    width: int
    col0: int = 0
    per_head: bool = True

    def col(self, h):
        return self.col0 + h if self.per_head else self.col0


class _Side(NamedTuple):
    inputs: list
    out_shape: list
    scratch: list
    phase: object


def _mm(a, b, name, mode="nn", out_dtype=F32, second_dtype=None, wide=1024, side=None):
    if mode == "tn":
        k, m = a.shape
    else:
        m, k = a.shape
    if mode == "nt":
        n, k2 = b.shape
    else:
        k2, n = b.shape
    assert k == k2, (a.shape, b.shape, mode)
    tm, tn, tk = _tile(m, 1024), _tile(n, wide), _tile(k, 2048)
    grid = (m // tm, n // tn, k // tk)
    nk = grid[2]
    assert nk == 1 or (out_dtype == F32 and second_dtype is None)
    dot = {"nn": _nn, "nt": _nt, "tn": _tn}[mode]
    n_in = len(side.inputs) if side else 0
    n_out = len(side.out_shape) if side else 0
    n_res = 1 + (second_dtype is not None)

    def kern(*refs):
        a_ref, b_ref = refs[:2]
        res = refs[2 + n_in:2 + n_in + n_res]
        step = [pl.program_id(d) for d in range(3)]
        if side:
            side_refs = (refs[2:2 + n_in], refs[2 + n_in + n_res:2 + n_in + n_res + n_out],
                         refs[2 + n_in + n_res + n_out:])

            @pl.when((step[0] == 0) & (step[1] == 0) & (step[2] == 0))
            def _():
                side.phase("start", *side_refs)

        r = dot(a_ref[...].astype(BF16), b_ref[...].astype(BF16))
        if nk == 1:
            res[0][...] = r.astype(out_dtype)
            if n_res == 2:
                res[1][...] = r.astype(second_dtype)
        else:
            @pl.when(step[2] == 0)
            def _():
                res[0][...] = r

            @pl.when(step[2] > 0)
            def _():
                res[0][...] += r

        if side:
            @pl.when((step[0] == grid[0] - 1) & (step[1] == grid[1] - 1) & (step[2] == nk - 1))
            def _():
                side.phase("finish", *side_refs)

    a_spec = (pl.BlockSpec((tk, tm), lambda i, j, kk: (kk, i)) if mode == "tn"
              else pl.BlockSpec((tm, tk), lambda i, j, kk: (i, kk)))
    b_spec = (pl.BlockSpec((tn, tk), lambda i, j, kk: (j, kk)) if mode == "nt"
              else pl.BlockSpec((tk, tn), lambda i, j, kk: (kk, j)))
    o_spec = pl.BlockSpec((tm, tn), lambda i, j, kk: (i, j))
    out_specs = [o_spec] * n_res + [ANY] * n_out
    out_shape = [jax.ShapeDtypeStruct((m, n), out_dtype)]
    if second_dtype is not None:
        out_shape.append(jax.ShapeDtypeStruct((m, n), second_dtype))
    out_shape += list(side.out_shape) if side else []
    semantics = ("arbitrary",) * 3 if side else ("parallel", "parallel", "arbitrary")
    out = _pcall(
        kern, name=name, grid=grid, in_specs=[a_spec, b_spec] + [ANY] * n_in, out_specs=out_specs,
        out_shape=out_shape, scratch_shapes=list(side.scratch) if side else [],
        compiler_params=pltpu.CompilerParams(dimension_semantics=semantics),
    )(a, b, *(side.inputs if side else []))
    return out[0] if len(out) == 1 else out


def _rms_fwd(x, g, name):
    r, d = x.arr.shape[0], x.width
    tr = _tile(r, 512)

    def kern(x_ref, g_ref, y_ref, r_ref):
        xv = x_ref[...]
        rstd = lax.rsqrt(jnp.mean(xv * xv, axis=-1, keepdims=True) + EPS)
        y_ref[...] = (xv * rstd * g_ref[...]).astype(BF16)
        r_ref[...] = rstd

    return _pcall(
        kern, name=name, grid=(r // tr,),
        in_specs=[pl.BlockSpec((tr, d), lambda i: (i, x.col0)), pl.BlockSpec((1, d), lambda i: (0, 0))],
        out_specs=[pl.BlockSpec((tr, d), lambda i: (i, 0)), pl.BlockSpec((tr, 1), lambda i: (i, 0))],
        out_shape=[jax.ShapeDtypeStruct((r, d), BF16), jax.ShapeDtypeStruct((r, 1), F32)],
    )(x.arr, g)


def _rms_bwd(x, g, rstd, dy, add, name):
    r, d = x.arr.shape[0], x.width
    tr = _tile(r, 256)
    has_add = add is not None

    def kern(*refs):
        if has_add:
            x_ref, g_ref, r_ref, dy_ref, add_ref, dx_ref, dg_ref = refs
        else:
            x_ref, g_ref, r_ref, dy_ref, dx_ref, dg_ref = refs
        rs = r_ref[...]
        xhat = x_ref[...] * rs
        dyv = dy_ref[...]
        dyg = dyv * g_ref[...]
        c = jnp.mean(dyg * xhat, axis=-1, keepdims=True)
        dx = rs * (dyg - xhat * c)
        if has_add:
            dx = dx + add_ref[...]
        dx_ref[...] = dx
        part = jnp.sum(dyv * xhat, axis=0, keepdims=True)

        @pl.when(pl.program_id(0) == 0)
        def _():
            dg_ref[...] = part

        @pl.when(pl.program_id(0) > 0)
        def _():
            dg_ref[...] += part

    row = pl.BlockSpec((tr, d), lambda i: (i, 0))
    vec = pl.BlockSpec((1, d), lambda i: (0, 0))
    ins = [pl.BlockSpec((tr, d), lambda i: (i, x.col0)), vec, pl.BlockSpec((tr, 1), lambda i: (i, 0)), row]
    ins += [row] if has_add else []
    args = (x.arr, g, rstd, dy) + ((add,) if has_add else ())
    return _pcall(
        kern, name=name, grid=(r // tr,), in_specs=ins, out_specs=[row, vec],
        out_shape=[jax.ShapeDtypeStruct((r, d), F32), jax.ShapeDtypeStruct((1, d), F32)],
        compiler_params=pltpu.CompilerParams(dimension_semantics=("arbitrary",)),
    )(*args)


def _final_loss(x, proj, g, target):
    r, d = x.shape
    tr = _tile(r, 256)

    def kern(x_ref, p_ref, g_ref, t_ref, dy_ref, dg_ref, loss_ref):
        y = x_ref[...] + p_ref[...]
        rs = lax.rsqrt(jnp.mean(y * y, axis=-1, keepdims=True) + EPS)
        yhat = y * rs
        gv = g_ref[...]
        e = yhat * gv - t_ref[...]
        lpart = 0.5 * jnp.sum(jnp.mean(e * e, axis=-1, keepdims=True), axis=0, keepdims=True)
        dout = e * (1.0 / d)
        dyg = dout * gv
        c = jnp.mean(dyg * yhat, axis=-1, keepdims=True)
        dy_ref[...] = rs * (dyg - yhat * c)
        gpart = jnp.sum(dout * yhat, axis=0, keepdims=True)
        lrow = jnp.broadcast_to(lpart, (1, LANE))

        @pl.when(pl.program_id(0) == 0)
        def _():
            dg_ref[...] = gpart
            loss_ref[...] = lrow

        @pl.when(pl.program_id(0) > 0)
        def _():
            dg_ref[...] += gpart
            loss_ref[...] += lrow

    row = pl.BlockSpec((tr, d), lambda i: (i, 0))
    vec = pl.BlockSpec((1, d), lambda i: (0, 0))
    return _pcall(
        kern, name="final_loss", grid=(r // tr,), in_specs=[row, row, vec, row],
        out_specs=[row, vec, pl.BlockSpec((1, LANE), lambda i: (0, 0))],
        out_shape=[jax.ShapeDtypeStruct((r, d), F32), jax.ShapeDtypeStruct((1, d), F32),
                   jax.ShapeDtypeStruct((1, LANE), F32)],
        compiler_params=pltpu.CompilerParams(dimension_semantics=("arbitrary",)),
    )(x, proj, g, target)


def _rope_fwd(x, cs, sn, nh, width, off, name):
    s = x.arr.shape[0]
    tr = _tile(s, 512)

    def kern(x_ref, c_ref, s_ref, o_ref):
        cv, sv = c_ref[...], s_ref[...]
        for h in range(nh):
            b = h * width
            if off:
                o_ref[:, b:b + off] = x_ref[:, b:b + off].astype(BF16)
            xr = x_ref[:, b + off:b + off + LANE]
            o_ref[:, b + off:b + off + LANE] = (xr * cv + pltpu.roll(xr, 32, 1) * sv).astype(BF16)

    tab = pl.BlockSpec((tr, LANE), lambda i: (i, 0))
    return _pcall(
        kern, name=name, grid=(s // tr,),
        in_specs=[pl.BlockSpec((tr, nh * width), lambda i: (i, x.col0)), tab, tab],
        out_specs=pl.BlockSpec((tr, nh * width), lambda i: (i, 0)),
        out_shape=jax.ShapeDtypeStruct((s, nh * width), BF16),
    )(x.arr, cs, sn)


def _rope_grad(d, cv, sv):
    g2 = d * sv
    g2 = g2 + pltpu.roll(g2, 64, 1)
    lane = lax.broadcasted_iota(jnp.int32, d.shape, 1)
    return jnp.where(lane < 64, d * cv + pltpu.roll(g2, 32, 1), 0.0)


def _rope_bwd_q(dq, cs, sn):
    s, w = dq.shape
    tr = _tile(s, 512)
    nh = w // 256

    def kern(d_ref, c_ref, s_ref, o_ref):
        cv, sv = c_ref[...], s_ref[...]
        for h in range(nh):
            b = h * 256
            o_ref[:, b:b + LANE] = d_ref[:, b:b + LANE]
            o_ref[:, b + LANE:b + 256] = _rope_grad(d_ref[:, b + LANE:b + 256], cv, sv)

    row = pl.BlockSpec((tr, w), lambda i: (i, 0))
    tab = pl.BlockSpec((tr, LANE), lambda i: (i, 0))
    return _pcall(kern, name="rope_bwd_q", grid=(s // tr,), in_specs=[row, tab, tab], out_specs=row,
                  out_shape=jax.ShapeDtypeStruct((s, w), F32))(dq, cs, sn)


def _rope_bwd_k(dk_nope, dk_pe, dv, cs, sn):
    s, w = dk_nope.shape
    tr = _tile(s, 512)

    def kern(dk_ref, dp_ref, dv_ref, c_ref, s_ref, okv_ref, okr_ref):
        okv_ref[:, :w] = dk_ref[...]
        okv_ref[:, w:] = dv_ref[...]
        okr_ref[...] = _rope_grad(dp_ref[...], c_ref[...], s_ref[...])

    tab = pl.BlockSpec((tr, LANE), lambda i: (i, 0))
    wide = pl.BlockSpec((tr, w), lambda i: (i, 0))
    return _pcall(
        kern, name="rope_bwd_k", grid=(s // tr,), in_specs=[wide, tab, wide, tab, tab],
        out_specs=[pl.BlockSpec((tr, 2 * w), lambda i: (i, 0)), tab],
        out_shape=[jax.ShapeDtypeStruct((s, 2 * w), F32), jax.ShapeDtypeStruct((s, LANE), F32)],
    )(dk_nope, dk_pe, dv, cs, sn)


class _Attn:
    def __init__(self, mode, s, sk, heads, dk):
        self.mode, self.s, self.sk, self.h, self.dk = mode, s, sk, heads, dk
        self.scale = {"mla": 192 ** -0.5, "mem": 128 ** -0.5}.get(mode, NSA_DK ** -0.5)
        self.tb = min(256, s)
        self.nb = s // self.tb
        self.nsub = 2 if self.nb % 2 == 0 else 1
        self.tq = self.tb * self.nsub
        self.fchains = 1
        self.nq = s // self.tq
        self.causal = mode in ("mla", "slc")
        if self.causal:
            self.tk = self.tq
        elif mode == "win":
            self.tk = WIN + self.tb
        else:
            self.tk = sk
        self.tkb = min(512, sk)
        self.ksub = 1
        self.kb = self.tkb // self.ksub
        self.ncmp = s // CMP_STRIDE - 1

    def mask_bias(self, t, n, h, selx, diag):
        m = self.mode
        if m == "mla":
            return (n <= t) if diag else None, None
        if m == "mem":
            return None, None
        slope = jnp.where(h == 0, 0.25, jnp.where(h == 1, 0.0625, jnp.where(h == 2, 0.015625, 0.00390625)))
        slope = slope.astype(F32) * LOG2E
        if m == "cmp":
            mask = (n * CMP_STRIDE + (CMP_LEN - 1) <= t) & (n < self.ncmp)
            pos = n.astype(F32) * float(CMP_STRIDE) + (CMP_LEN - 1) / 2.0
            return mask, slope * pos
        rel = t - n
        if m == "slc":
            return (rel >= 0) if diag else None, slope * n.astype(F32)
        return (rel >= 0) & (rel < WIN), slope * n.astype(F32)


def _scores(cfg, s_raw, t, n, h, selx, diag, lse=None):
    s = s_raw * (cfg.scale * LOG2E)
    mask, key_term = cfg.mask_bias(t, n, h, selx, diag)
    if key_term is not None:
        s = s + key_term
    if selx is not None:
        s = s + selx
    if lse is None:
        if mask is not None:
            s = jnp.where(mask, s, NEG)
        return s, mask
    p = jnp.exp2(jnp.minimum(s - lse, 0.0))
    if mask is not None:
        p = jnp.where(mask, p, 0.0)
    return p, mask


def _block_of_key(k0, tk, keys_on_rows, value=NEG):
    shape = (tk, LANE) if keys_on_rows else (LANE, tk)
    n = lax.broadcasted_iota(jnp.int32, shape, 0 if keys_on_rows else 1) + k0
    j = lax.broadcasted_iota(jnp.int32, shape, 1 if keys_on_rows else 0)
    return jnp.where((n >> 6) == j, value, 0.0).astype(BF16)


def _to_row(col):
    t = col.shape[0]
    return jnp.transpose(jnp.broadcast_to(col, (t, LANE)))[0:1, :]


def _load_keys(k_refs, rows):
    parts = [r[rows, :].astype(BF16) for r in k_refs]
    return parts[0] if len(parts) == 1 else jnp.concatenate(parts, axis=1)


def _attn_fwd(cfg, q, ks, v, sel, name):
    s, tq, tk, nsub = cfg.s, cfg.tq, cfg.tk, cfg.fchains
    tb = tq // nsub
    per = tb // cfg.tb
    has_sel = sel is not None
    nkp = len(ks)
    assert cfg.causal and tq == tk and cfg.nq % 2 == 0

    def kern(*refs):
        q_ref, k_refs, v_ref = refs[0], refs[1:1 + nkp], refs[1 + nkp]
        sel_ref = refs[2 + nkp] if has_sel else None
        o_ref, lc_ref, lr_ref = refs[2 + nkp + has_sel:5 + nkp + has_sel]
        buf_a, buf_b = refs[-2:]
        h, g = pl.program_id(0), pl.program_id(1)

        def block(b):
            rows = [slice(b * tq + r * tb, b * tq + (r + 1) * tb) for r in range(nsub)]
            qs = [q_ref[p, :].astype(BF16) for p in rows]
            ts = [(2 * g + b) * tq + r * tb + lax.broadcasted_iota(jnp.int32, (tb, 1), 0) for r in range(nsub)]
            sels = [sel_ref[p, :].astype(BF16) for p in rows] if has_sel else None
            return rows, qs, ts, sels

        def scores_into(buf, blk, c):
            kk = _load_keys(k_refs, pl.ds(pl.multiple_of(c * tk, tk), tk))
            for r in range(nsub):
                buf[r] = _nt(blk[1][r], kk)

        def consume(buf, blk, c, carry, diag):
            _, _, ts, sels = blk
            k0 = pl.multiple_of(c * tk, tk)
            vv = v_ref[pl.ds(k0, tk), :].astype(BF16)
            emat = _block_of_key(k0, tk, False) if has_sel else None
            n = k0 + lax.broadcasted_iota(jnp.int32, (1, tk), 1)
            new = []
            for r in range(nsub):
                m, l, acc = carry[r]
                selx = _nn(sels[r], emat) if has_sel else None
                sc, mask = _scores(cfg, buf[r], ts[r], n, h, selx, diag)
                m_new = jnp.maximum(m, jnp.max(sc, axis=1, keepdims=True))
                alpha = jnp.exp2(m - m_new)
                p = jnp.exp2(sc - m_new)
                if mask is not None:
                    p = jnp.where(mask, p, 0.0)
                l = alpha * l + jnp.sum(p, axis=1, keepdims=True)
                new.append((m_new, l, alpha * acc + _nn(p.astype(BF16), vv)))
            return tuple(new)

        def finish(blk, b, carry):
            for r, (m, l, acc) in enumerate(carry):
                o_ref[blk[0][r], :] = acc / (l + 1e-20)
                lse = m + jnp.log(l + 1e-20) * LOG2E
                lc_ref[0, blk[0][r], :] = lse
                for u in range(per):
                    lr_ref[0, (b * nsub + r) * per + u] = _to_row(lse[u * cfg.tb:(u + 1) * cfg.tb])

        def pairs(blk, first, other):
            def pair(p, cr):
                scores_into(other, blk, 2 * p + 1)
                cr = consume(first, blk, 2 * p, cr, False)
                scores_into(first, blk, 2 * p + 2)
                return consume(other, blk, 2 * p + 1, cr, False)
            return pair

        init = ((jnp.full((tb, 1), NEG, F32), jnp.zeros((tb, 1), F32), jnp.zeros((tb, HEAD_V), F32)),) * nsub
        blk_a, blk_b = block(0), block(1)
        scores_into(buf_a, blk_a, 0)
        carry = lax.fori_loop(0, g, pairs(blk_a, buf_a, buf_b), init)
        scores_into(buf_b, blk_b, 0)
        finish(blk_a, 0, consume(buf_a, blk_a, 2 * g, carry, True))
        carry = lax.fori_loop(0, g, pairs(blk_b, buf_b, buf_a), init)
        scores_into(buf_a, blk_b, 2 * g + 1)
        carry = consume(buf_b, blk_b, 2 * g, carry, False)
        finish(blk_b, 1, consume(buf_a, blk_b, 2 * g + 1, carry, True))

    ins = [pl.BlockSpec((2 * tq, q.width), lambda h, g: (g, q.col(h)))]
    ins += [pl.BlockSpec((cfg.sk, p.width), lambda h, g, p=p: (0, p.col(h))) for p in ks]
    ins += [pl.BlockSpec((cfg.sk, HEAD_V), lambda h, g: (0, v.col(h)))]
    args = [q.arr] + [p.arr for p in ks] + [v.arr]
    if has_sel:
        ins.append(pl.BlockSpec((2 * tq, LANE), lambda h, g: (g, 0)))
        args.append(sel)
    return _pcall(
        kern, name=name, grid=(cfg.h, cfg.nq // 2), in_specs=ins,
        out_specs=[pl.BlockSpec((2 * tq, HEAD_V), lambda h, g: (g, h)),
                   pl.BlockSpec((1, 2 * tq, 1), lambda h, g: (h, g, 0)),
                   pl.BlockSpec((1, 2 * tq // cfg.tb, 1, cfg.tb), lambda h, g: (h, g, 0, 0))],
        out_shape=[jax.ShapeDtypeStruct((s, cfg.h * HEAD_V), F32),
                   jax.ShapeDtypeStruct((cfg.h, s, 1), F32),
                   jax.ShapeDtypeStruct((cfg.h, cfg.nb, 1, cfg.tb), F32)],
        scratch_shapes=[pltpu.VMEM((nsub, tb, tk), F32)] * 2,
        compiler_params=pltpu.CompilerParams(dimension_semantics=("parallel", "parallel")),
    )(*args)


def _attn_dq(cfg, q, ks, v, sel, o, lse, do, dq_in, name):
    s, tq, tk, dk, nsub = cfg.s, cfg.tq, cfg.tk, cfg.dk, cfg.fchains
    tb = tq // nsub
    per = tb // cfg.tb
    has_sel = sel is not None
    has_in = dq_in is not None
    nkp = len(ks)

    def kern(*refs):
        refs = list(refs)
        q_ref, k_refs, v_ref = refs[0], refs[1:1 + nkp], refs[1 + nkp]
        p0 = 2 + nkp
        sel_ref = refs[p0] if has_sel else None
        p0 += has_sel
        o_ref, l_ref, do_ref = refs[p0:p0 + 3]
        p0 += 3
        in_ref = refs[p0] if has_in else None
        p0 += has_in
        dq_ref, dr_ref = refs[p0:p0 + 2]
        sa, pa, sb, pb = refs[-4:]
        h, g = pl.program_id(0), pl.program_id(1)

        def block(b):
            rows = [slice(b * tq + r * tb, b * tq + (r + 1) * tb) for r in range(nsub)]
            qs = [q_ref[p, :].astype(BF16) for p in rows]
            ts = [(2 * g + b) * tq + r * tb + lax.broadcasted_iota(jnp.int32, (tb, 1), 0) for r in range(nsub)]
            sels = [sel_ref[p, :].astype(BF16) for p in rows] if has_sel else None
            dvecs, dobs, lses = [], [], []
            for r, p in enumerate(rows):
                dov = do_ref[p, :]
                dvec = jnp.sum(dov * o_ref[p, :], axis=1, keepdims=True)
                for u in range(per):
                    dr_ref[0, (b * nsub + r) * per + u] = _to_row(dvec[u * cfg.tb:(u + 1) * cfg.tb])
                dvecs.append(dvec)
                dobs.append(dov.astype(BF16))
                lses.append(l_ref[0, p, :])
            return rows, qs, ts, sels, dvecs, dobs, lses

        def products_into(sbuf, pbuf, blk, c):
            rows = pl.ds(pl.multiple_of(c * tk, tk), tk)
            kk, vv = _load_keys(k_refs, rows), v_ref[rows, :].astype(BF16)
            for r in range(nsub):
                sbuf[r] = _nt(blk[1][r], kk)
                pbuf[r] = _nt(blk[5][r], vv)

        def consume(sbuf, pbuf, blk, c, accs, diag):
            _, _, ts, sels, dvecs, _, lses = blk
            k0 = pl.multiple_of(c * tk, tk)
            kk = _load_keys(k_refs, pl.ds(k0, tk))
            emat = _block_of_key(k0, tk, False) if has_sel else None
            n = k0 + lax.broadcasted_iota(jnp.int32, (1, tk), 1)
            new = []
            for r in range(nsub):
                selx = _nn(sels[r], emat) if has_sel else None
                p, _ = _scores(cfg, sbuf[r], ts[r], n, h, selx, diag, lses[r])
                ds = p * (pbuf[r] - dvecs[r])
                new.append(accs[r] + _nn(ds.astype(BF16), kk))
            return tuple(new)

        def finish(blk, accs):
            for r, p in enumerate(blk[0]):
                dq_ref[p, :] = accs[r] * cfg.scale + in_ref[p, :] if has_in else accs[r] * cfg.scale

        def pairs(blk, first, other):
            def pair(p, ac):
                products_into(*other, blk, 2 * p + 1)
                ac = consume(*first, blk, 2 * p, ac, False)
                products_into(*first, blk, 2 * p + 2)
                return consume(*other, blk, 2 * p + 1, ac, False)
            return pair

        zero = (jnp.zeros((tb, dk), F32),) * nsub
        buf_a, buf_b = (sa, pa), (sb, pb)
        blk_a, blk_b = block(0), block(1)
        products_into(*buf_a, blk_a, 0)
        accs = lax.fori_loop(0, g, pairs(blk_a, buf_a, buf_b), zero)
        products_into(*buf_b, blk_b, 0)
        finish(blk_a, consume(*buf_a, blk_a, 2 * g, accs, True))
        accs = lax.fori_loop(0, g, pairs(blk_b, buf_b, buf_a), zero)
        products_into(*buf_a, blk_b, 2 * g + 1)
        accs = consume(*buf_b, blk_b, 2 * g, accs, False)
        finish(blk_b, consume(*buf_a, blk_b, 2 * g + 1, accs, True))

    assert cfg.causal and tq == tk and cfg.nq % 2 == 0
    qs = pl.BlockSpec((2 * tq, dk), lambda h, g: (g, h))
    ins = [pl.BlockSpec((2 * tq, q.width), lambda h, g: (g, q.col(h)))]
    ins += [pl.BlockSpec((cfg.sk, p.width), lambda h, g, p=p: (0, p.col(h))) for p in ks]
    ins += [pl.BlockSpec((cfg.sk, HEAD_V), lambda h, g: (0, v.col(h)))]
    args = [q.arr] + [p.arr for p in ks] + [v.arr]
    if has_sel:
        ins.append(pl.BlockSpec((2 * tq, LANE), lambda h, g: (g, 0)))
        args.append(sel)
    ins += [pl.BlockSpec((2 * tq, HEAD_V), lambda h, g: (g, o.col(h))),
            pl.BlockSpec((1, 2 * tq, 1), lambda h, g: (h, g, 0)),
            pl.BlockSpec((2 * tq, HEAD_V), lambda h, g: (g, do.col(h)))]
    args += [o.arr, lse, do.arr]
    if has_in:
        ins.append(qs)
        args.append(dq_in)
    return _pcall(
        kern, name=name, grid=(cfg.h, cfg.nq // 2), in_specs=ins,
        out_specs=[qs, pl.BlockSpec((1, 2 * tq // cfg.tb, 1, cfg.tb), lambda h, g: (h, g, 0, 0))],
        out_shape=[jax.ShapeDtypeStruct((s, cfg.h * dk), F32),
                   jax.ShapeDtypeStruct((cfg.h, cfg.nb, 1, cfg.tb), F32)],
        scratch_shapes=[pltpu.VMEM((nsub, tb, tk), F32)] * 4,
        compiler_params=pltpu.CompilerParams(dimension_semantics=("parallel", "parallel")),
    )(*args)


def _attn_dkv(cfg, q, ks, v, selt, lse_r, d_r, do, name):
    s, tq, tkb, dk, kb, ksub = cfg.s, cfg.tb, cfg.tkb, cfg.dk, cfg.kb, cfg.ksub
    nq = cfg.nb
    has_sel = selt is not None
    nkp = len(ks)
    outs = list(ks) + [v]

    def kern(*refs):
        k_refs, v_ref = refs[:nkp], refs[nkp]
        q_ref, do_ref, lr_ref, dr_ref = refs[nkp + 1:nkp + 5]
        st_ref = refs[nkp + 5] if has_sel else None
        out_refs = refs[nkp + 5 + has_sel:2 * nkp + 6 + has_sel]
        sa, pa, sb, pb = refs[-4:]
        j, h = pl.program_id(0), pl.program_id(1)
        k0 = j * tkb
        part = [slice(u * kb, (u + 1) * kb) for u in range(ksub)]
        kks = [_load_keys(k_refs, p) for p in part]
        vvs = [v_ref[p, :].astype(BF16) for p in part]
        ns = [k0 + u * kb + lax.broadcasted_iota(jnp.int32, (kb, 1), 0) for u in range(ksub)]
        emats = [_block_of_key(k0 + u * kb, kb, True) for u in range(ksub)] if has_sel else None

        def load_q(i):
            rows = pl.ds(pl.multiple_of(i * tq, tq), tq)
            return q_ref[rows, :].astype(BF16), do_ref[rows, :].astype(BF16)

        def products_into(sbuf, pbuf, i):
            qi, doi = load_q(i)
            for u in range(ksub):
                sbuf[u] = _nt(kks[u], qi)
                pbuf[u] = _nt(vvs[u], doi)

        def consume(sbuf, pbuf, i, carry):
            qi, doi = load_q(i)
            t = i * tq + lax.broadcasted_iota(jnp.int32, (1, tq), 1)
            selt_i = st_ref[i].astype(BF16) if has_sel else None
            new = []
            for u in range(ksub):
                dk_acc, dv_acc = carry[u]
                selx = _nn(emats[u], selt_i) if has_sel else None
                pt, _ = _scores(cfg, sbuf[u], t, ns[u], h, selx, True, lr_ref[0, i])
                dv_acc = dv_acc + _nn(pt.astype(BF16), doi)
                dst = pt * (pbuf[u] - dr_ref[0, i])
                new.append((dk_acc + _nn(dst.astype(BF16), qi), dv_acc))
            return tuple(new)

        if cfg.causal:
            first, count = k0 // tq, nq - k0 // tq
        elif cfg.mode == "win":
            first = k0 // tq
            count = jnp.minimum((k0 + tkb + WIN - 2) // tq + 1, nq) - first
        else:
            first, count = 0, nq

        def pair(p, cr):
            i0 = first + 2 * p
            products_into(sb, pb, i0 + 1)
            cr = consume(sa, pa, i0, cr)
            products_into(sa, pa, i0 + 2)
            return consume(sb, pb, i0 + 1, cr)

        carry = ((jnp.zeros((kb, dk), F32), jnp.zeros((kb, HEAD_V), F32)),) * ksub
        products_into(sa, pa, first)
        carry = lax.fori_loop(0, count // 2 - 1, pair, carry)
        last = first + count - 2
        products_into(sb, pb, last + 1)
        carry = consume(sa, pa, last, carry)
        carry = consume(sb, pb, last + 1, carry)
        for u, (dk_acc, dv_acc) in enumerate(carry):
            vals, off = [], 0
            for p in ks:
                vals.append(dk_acc[:, off:off + p.width] * cfg.scale)
                off += p.width
            vals.append(dv_acc)
            for src, ref, val in zip(outs, out_refs, vals):
                if src.per_head:
                    ref[part[u], :] = val
                else:
                    @pl.when(h == 0)
                    def _(ref=ref, val=val, u=u):
                        ref[part[u], :] = val

                    @pl.when(h > 0)
                    def _(ref=ref, val=val, u=u):
                        ref[part[u], :] += val

    rowv = pl.BlockSpec((1, nq, 1, tq), lambda j, h: (h, 0, 0, 0))
    ins = [pl.BlockSpec((tkb, p.width), lambda j, h, p=p: (j, p.col(h))) for p in ks]
    ins += [pl.BlockSpec((tkb, HEAD_V), lambda j, h: (j, v.col(h))),
            pl.BlockSpec((s, q.width), lambda j, h: (0, q.col(h))),
            pl.BlockSpec((s, HEAD_V), lambda j, h: (0, do.col(h))), rowv, rowv]
    args = [p.arr for p in ks] + [v.arr, q.arr, do.arr, lse_r, d_r]
    if has_sel:
        ins.append(pl.BlockSpec((nq, LANE, tq), lambda j, h: (0, 0, 0)))
        args.append(selt)
    out_specs = [pl.BlockSpec((tkb, p.width), lambda j, h, p=p: (j, h if p.per_head else 0)) for p in outs]
    out_shape = [jax.ShapeDtypeStruct((cfg.sk, (cfg.h if p.per_head else 1) * p.width), F32) for p in outs]
    assert nq % 2 == 0 and (cfg.mode in ("cmp", "mem") or tkb % (2 * tq) == 0), (nq, tkb, tq)
    return _pcall(
        kern, name=name, grid=(cfg.sk // tkb, cfg.h), in_specs=ins, out_specs=out_specs, out_shape=out_shape,
        scratch_shapes=[pltpu.VMEM((ksub, kb, tq), F32)] * 4,
        compiler_params=pltpu.CompilerParams(dimension_semantics=("parallel", "arbitrary")),
    )(*args)


def _attn_dkv_flat(cfg, q, ks, v, selt, lse_r, d_r, do, name):
    s, tq, tkb, dk, kb, ksub = cfg.s, cfg.tb, cfg.tkb, cfg.dk, cfg.kb, cfg.ksub
    nq = cfg.nb
    has_sel = selt is not None
    nkp = len(ks)
    outs = list(ks) + [v]
    assert nq % 2 == 0 and tkb % (2 * tq) == 0, (nq, tkb, tq)
    steps = []
    for j in range(cfg.sk // tkb):
        first = j * tkb // tq
        stop = nq if cfg.causal else min((j * tkb + tkb + WIN - 2) // tq + 1, nq)
        steps += [(j, i0) for i0 in range(first, stop, 2)]
    n_pairs = len(steps)
    steps.append(steps[-1])
    tab_j = jnp.asarray(np.array([p[0] for p in steps], np.int32))
    tab_i = jnp.asarray(np.array([p[1] for p in steps], np.int32))

    def kern(tj_ref, ti_ref, *refs):
        k_refs, v_ref = refs[:nkp], refs[nkp]
        q_ref, do_ref, lr_ref, dr_ref = refs[nkp + 1:nkp + 5]
        st_ref = refs[nkp + 5] if has_sel else None
        out_refs = refs[nkp + 5 + has_sel:2 * nkp + 6 + has_sel]
        sa, pa, sb, pb = refs[-4:]
        h = pl.program_id(0)
        for src, ref in zip(outs, out_refs):
            if src.per_head:
                ref[...] = jnp.zeros_like(ref)
            else:
                @pl.when(h == 0)
                def _(ref=ref):
                    ref[...] = jnp.zeros_like(ref)

        def key_rows(j, u):
            return pl.ds(pl.multiple_of(j * tkb + u * kb, kb), kb)

        def load_q(i):
            rows = pl.ds(pl.multiple_of(i * tq, tq), tq)
            return q_ref[rows, :].astype(BF16), do_ref[rows, :].astype(BF16)

        def products_into(sbuf, pbuf, j, i):
            qi, doi = load_q(i)
            for u in range(ksub):
                rows = key_rows(j, u)
                sbuf[u] = _nt(_load_keys(k_refs, rows), qi)
                pbuf[u] = _nt(v_ref[rows, :].astype(BF16), doi)

        def consume(sbuf, pbuf, j, i):
            qi, doi = load_q(i)
            t = i * tq + lax.broadcasted_iota(jnp.int32, (1, tq), 1)
            selt_i = st_ref[i].astype(BF16) if has_sel else None
            res = []
            for u in range(ksub):
                k0 = j * tkb + u * kb
                n = k0 + lax.broadcasted_iota(jnp.int32, (kb, 1), 0)
                selx = _nn(_block_of_key(k0, kb, True), selt_i) if has_sel else None
                pt, _ = _scores(cfg, sbuf[u], t, n, h, selx, True, lr_ref[0, i])
                dst = pt * (pbuf[u] - dr_ref[0, i])
                res.append((_nn(dst.astype(BF16), qi), _nn(pt.astype(BF16), doi)))
            return res

        def pair(p, carry):
            j, i0 = tj_ref[p], ti_ref[p]
            products_into(sb, pb, j, i0 + 1)
            ca = consume(sa, pa, j, i0)
            products_into(sa, pa, tj_ref[p + 1], ti_ref[p + 1])
            cb = consume(sb, pb, j, i0 + 1)
            for u in range(ksub):
                rows = key_rows(j, u)
                dk_c = (ca[u][0] + cb[u][0]) * cfg.scale
                off = 0
                for src, ref in zip(ks, out_refs):
                    ref[rows, :] += dk_c[:, off:off + src.width]
                    off += src.width
                out_refs[nkp][rows, :] += ca[u][1] + cb[u][1]
            return carry

        products_into(sa, pa, tj_ref[0], ti_ref[0])
        lax.fori_loop(0, n_pairs, pair, 0)

    rowv = pl.BlockSpec((1, nq, 1, tq), lambda h, tj, ti: (h, 0, 0, 0))
    ins = [pl.BlockSpec((cfg.sk, p.width), lambda h, tj, ti, p=p: (0, p.col(h))) for p in ks]
    ins += [pl.BlockSpec((cfg.sk, HEAD_V), lambda h, tj, ti: (0, v.col(h))),
            pl.BlockSpec((s, q.width), lambda h, tj, ti: (0, q.col(h))),
            pl.BlockSpec((s, HEAD_V), lambda h, tj, ti: (0, do.col(h))), rowv, rowv]
    args = [p.arr for p in ks] + [v.arr, q.arr, do.arr, lse_r, d_r]
    if has_sel:
        ins.append(pl.BlockSpec((nq, LANE, tq), lambda h, tj, ti: (0, 0, 0)))
        args.append(selt)
    out_specs = [pl.BlockSpec((cfg.sk, p.width), lambda h, tj, ti, p=p: (0, h if p.per_head else 0))
                 for p in outs]
    out_shape = [jax.ShapeDtypeStruct((cfg.sk, (cfg.h if p.per_head else 1) * p.width), F32) for p in outs]
    grid_spec = pltpu.PrefetchScalarGridSpec(
        num_scalar_prefetch=2, grid=(cfg.h,), in_specs=ins, out_specs=out_specs,
        scratch_shapes=[pltpu.VMEM((ksub, kb, tq), F32)] * 4)
    return _pcall(kern, name=name, grid_spec=grid_spec, out_shape=out_shape,
                  compiler_params=pltpu.CompilerParams(dimension_semantics=("arbitrary",)))(tab_j, tab_i, *args)


def _all_heads(cfg, src, rows, key=False):
    if src.per_head:
        assert src.col0 % cfg.h == 0
        width, col = cfg.h * src.width, src.col0 // cfg.h
    else:
        width, col = src.width, src.col0
    return pl.BlockSpec((rows, width), (lambda i: (0, col)) if key else (lambda i: (i, col)))


def _head_cols(src, hh):
    return slice(hh * src.width, (hh + 1) * src.width) if src.per_head else slice(None)


def _key_window(cfg, i, r):
    if cfg.mode == "win":
        return pl.ds(pl.multiple_of(jnp.maximum(i * cfg.tq + r * cfg.tb - WIN, 0), cfg.tb), cfg.tk)
    return pl.ds(0, cfg.tk)


def _attn_fwd_small(cfg, q, ks, v, name, overlap=None):
    s, tq, tk, tb, nsub, nh = cfg.s, cfg.tq, cfg.tk, cfg.tb, cfg.nsub, cfg.h
    nkp = len(ks)
    select = overlap is not None
    n_s = s // SLC_LEN
    top_n = min(SLC_TOPN, n_s)

    def kern(*refs):
        q_ref, k_refs, v_ref = refs[0], refs[1:1 + nkp], refs[1 + nkp]
        ov_ref = refs[2 + nkp] if select else None
        o_ref, lc_ref, lr_ref = refs[2 + nkp + select:5 + nkp + select]
        i = pl.program_id(0)
        imps = [jnp.zeros((tb, LANE), F32)] * nsub
        for r in range(nsub):
            rows = slice(r * tb, (r + 1) * tb)
            t = i * tq + r * tb + lax.broadcasted_iota(jnp.int32, (tb, 1), 0)
            win = _key_window(cfg, i, r)
            n = win.start + lax.broadcasted_iota(jnp.int32, (1, tk), 1)
            for hh in range(nh):
                qv = q_ref[rows, hh * cfg.dk:(hh + 1) * cfg.dk].astype(BF16)
                kk = _load_keys([kr.at[:, _head_cols(p, hh)] for kr, p in zip(k_refs, ks)], win)
                vv = v_ref[win, _head_cols(v, hh)].astype(BF16)
                sc, mask = _scores(cfg, _nt(qv, kk), t, n, hh, None, True)
                m = jnp.max(sc, axis=1, keepdims=True)
                e = jnp.exp2(sc - m)
                if mask is not None:
                    e = jnp.where(mask, e, 0.0)
                l = jnp.sum(e, axis=1, keepdims=True)
                o_ref[rows, hh * HEAD_V:(hh + 1) * HEAD_V] = _nn(e.astype(BF16), vv) / (l + 1e-20)
                lse = m + jnp.log(l + 1e-20) * LOG2E
                lc_ref[hh, rows, :] = lse
                lr_ref[hh, r] = _to_row(lse)
                if select:
                    imps[r] = imps[r] + _nn((e / (l + 1e-20)).astype(BF16), ov_ref[...])
        if select:
            sel_ref, selt_ref, imp_t = refs[5 + nkp + select:8 + nkp + select]
            for r in range(nsub):
                t = i * tq + r * tb + lax.broadcasted_iota(jnp.int32, (tb, 1), 0)
                j = lax.broadcasted_iota(jnp.int32, (tb, LANE), 1)
                cur = t >> 6
                imp = jnp.where((j == 0) | (j == cur) | (j == cur - 1), 1e9, imps[r])
                imp = jnp.where(j > cur, -1e9, imp)
                imp_t[r] = jnp.transpose(imp)
                mine = imp_t[r, 0:n_s, :]
                jrow = lax.broadcasted_iota(jnp.int32, (n_s, tb), 0)

                def count(k, rank):
                    other = imp_t[r, pl.ds(k, 1), :]
                    ahead = (other > mine) | ((other == mine) & (k < jrow))
                    return rank + jnp.where(ahead, 1.0, 0.0)

                rank = lax.fori_loop(0, n_s, count, jnp.zeros((n_s, tb), F32))
                cur_t = (i * tq + r * tb + lax.broadcasted_iota(jnp.int32, (1, tb), 1)) >> 6
                rejected = jnp.where((rank < top_n) & (jrow <= cur_t), 0.0, 1.0)
                if n_s < LANE:
                    rejected = jnp.concatenate([rejected, jnp.ones((LANE - n_s, tb), F32)], axis=0)
                selt_ref[r] = rejected
                sel_ref[r * tb:(r + 1) * tb, :] = jnp.transpose(rejected)

    ins = [_all_heads(cfg, q, tq)] + [_all_heads(cfg, p, cfg.sk, True) for p in ks]
    ins += [_all_heads(cfg, v, cfg.sk, True)]
    args = [q.arr] + [p.arr for p in ks] + [v.arr]
    out_specs = [pl.BlockSpec((tq, nh * HEAD_V), lambda i: (i, 0)),
                 pl.BlockSpec((nh, tq, 1), lambda i: (0, i, 0)),
                 pl.BlockSpec((nh, nsub, 1, tb), lambda i: (0, i, 0, 0))]
    out_shape = [jax.ShapeDtypeStruct((s, nh * HEAD_V), F32), jax.ShapeDtypeStruct((nh, s, 1), F32),
                 jax.ShapeDtypeStruct((nh, cfg.nb, 1, tb), F32)]
    scratch = []
    if select:
        ins.append(pl.BlockSpec((cfg.sk, LANE), lambda i: (0, 0)))
        args.append(overlap)
        out_specs += [pl.BlockSpec((tq, LANE), lambda i: (i, 0)), pl.BlockSpec((nsub, LANE, tb), lambda i: (i, 0, 0))]
        out_shape += [jax.ShapeDtypeStruct((s, LANE), F32), jax.ShapeDtypeStruct((cfg.nb, LANE, tb), F32)]
        scratch = [pltpu.VMEM((nsub, LANE, tb), F32)]
    return _pcall(kern, name=name, grid=(cfg.nq,), in_specs=ins, out_specs=out_specs, out_shape=out_shape,
                  scratch_shapes=scratch,
                  compiler_params=pltpu.CompilerParams(dimension_semantics=("parallel",)))(*args)


def _attn_dq_small(cfg, q, ks, v, o, lse, do, dq_in, name):
    s, tq, tk, tb, nsub, nh, dk = cfg.s, cfg.tq, cfg.tk, cfg.tb, cfg.nsub, cfg.h, cfg.dk
    nkp = len(ks)
    has_in = dq_in is not None

    def kern(*refs):
        q_ref, k_refs, v_ref = refs[0], refs[1:1 + nkp], refs[1 + nkp]
        o_ref, l_ref, do_ref = refs[2 + nkp:5 + nkp]
        in_ref = refs[5 + nkp] if has_in else None
        dq_ref, dr_ref = refs[5 + nkp + has_in:7 + nkp + has_in]
        i = pl.program_id(0)
        for r in range(nsub):
            rows = slice(r * tb, (r + 1) * tb)
            t = i * tq + r * tb + lax.broadcasted_iota(jnp.int32, (tb, 1), 0)
            win = _key_window(cfg, i, r)
            n = win.start + lax.broadcasted_iota(jnp.int32, (1, tk), 1)
            for hh in range(nh):
                vcols = slice(hh * HEAD_V, (hh + 1) * HEAD_V)
                qcols = slice(hh * dk, (hh + 1) * dk)
                qv = q_ref[rows, qcols].astype(BF16)
                kk = _load_keys([kr.at[:, _head_cols(p, hh)] for kr, p in zip(k_refs, ks)], win)
                vv = v_ref[win, _head_cols(v, hh)].astype(BF16)
                dov = do_ref[rows, vcols]
                dvec = jnp.sum(dov * o_ref[rows, vcols], axis=1, keepdims=True)
                dr_ref[hh, r] = _to_row(dvec)
                p, _ = _scores(cfg, _nt(qv, kk), t, n, hh, None, True, l_ref[hh, rows, :])
                ds = p * (_nt(dov.astype(BF16), vv) - dvec)
                dq = _nn(ds.astype(BF16), kk) * cfg.scale
                dq_ref[rows, qcols] = dq + in_ref[rows, qcols] if has_in else dq

    qs = pl.BlockSpec((tq, nh * dk), lambda i: (i, 0))
    ins = [_all_heads(cfg, q, tq)] + [_all_heads(cfg, p, cfg.sk, True) for p in ks]
    ins += [_all_heads(cfg, v, cfg.sk, True)]
    ins += [_all_heads(cfg, o, tq), pl.BlockSpec((nh, tq, 1), lambda i: (0, i, 0)), _all_heads(cfg, do, tq)]
    args = [q.arr] + [p.arr for p in ks] + [v.arr, o.arr, lse, do.arr]
    if has_in:
        ins.append(qs)
        args.append(dq_in)
    return _pcall(
        kern, name=name, grid=(cfg.nq,), in_specs=ins,
        out_specs=[qs, pl.BlockSpec((nh, nsub, 1, tb), lambda i: (0, i, 0, 0))],
        out_shape=[jax.ShapeDtypeStruct((s, nh * dk), F32), jax.ShapeDtypeStruct((nh, cfg.nb, 1, tb), F32)],
        compiler_params=pltpu.CompilerParams(dimension_semantics=("parallel",)))(*args)


def _attn_bwd(cfg, q, ks, v, sel, selt, o, lse, lse_r, do, dq_in, name):
    if cfg.causal:
        dq, d_r = _attn_dq(cfg, q, ks, v, sel, o, lse, do, dq_in, name + "_dq")
    else:
        dq, d_r = _attn_dq_small(cfg, q, ks, v, o, lse, do, dq_in, name + "_dq")
    dkv = _attn_dkv_flat if cfg.causal or cfg.mode == "win" else _attn_dkv
    res = dkv(cfg, q, ks, v, selt, lse_r, d_r, do, name + "_dkv")
    return dq, res[:-1], res[-1]


def _silu_grad(pre):
    sg = _sigmoid(pre)
    return sg * (1.0 + pre * (1.0 - sg))


def _compress_fwd(a_lo, a_hi, pe_lo, pe_hi, w1_lo, w1_hi, w2, name):
    n, dp = a_lo.shape[0], w2.shape[1]

    def kern(alo, ahi, plo, phi, w1l, w1h, w2r, out_ref, pre_ref):
        xl = (alo[...] + plo[...]).astype(BF16)
        xh = (ahi[...] + phi[...]).astype(BF16)
        pre = _nn(xl, w1l[...]) + _nn(xh, w1h[...])
        act = pre * _sigmoid(pre)
        out_ref[...] = _nn(act.astype(BF16), w2r[...]).astype(BF16)
        pre_ref[...] = pre

    return _pcall(kern, name=name,
                  out_shape=[jax.ShapeDtypeStruct((n, dp), BF16), jax.ShapeDtypeStruct((n, dp), F32)],
                  )(a_lo, a_hi, pe_lo, pe_hi, w1_lo, w1_hi, w2)


def _compress_bwd(a_lo, a_hi, pe_lo, pe_hi, w1_lo, w1_hi, w2, pre, pre_sh, dout, dout_sh, name):
    n, ln = a_lo.shape
    dp = w2.shape[1]

    def kern(alo, ahi, plo, phi, w1l, w1h, w2r, pre_ref, presh_ref, do_ref, dosh_ref,
             da_ref, dpl_ref, dph_ref, dw1l_ref, dw1h_ref, dw2_ref):
        prev = pre_ref[...]
        act = prev * _sigmoid(prev)
        dob = do_ref[...].astype(BF16)
        w2v = w2r[...]
        dpre = (_nt(dob, w2v) * _silu_grad(prev)).astype(BF16)
        dpre_sh = (_nt(dosh_ref[...].astype(BF16), w2v) * _silu_grad(presh_ref[...])).astype(BF16)
        dw2_ref[...] = _nn(act.T.astype(BF16), dob)
        xl = alo[...] + plo[...]
        xh = ahi[...] + phi[...]
        dw1l_ref[...] = _nn(xl.T.astype(BF16), dpre)
        dw1h_ref[...] = _nn(xh.T.astype(BF16), dpre)
        dal = _nt(dpre, w1l[...])
        dah_sh = _nt(dpre_sh, w1h[...])
        da_ref[...] = dal + dah_sh
        dpl_ref[...] = jnp.sum(dal, axis=0, keepdims=True)
        dph_ref[...] = jnp.sum(dah_sh, axis=0, keepdims=True)

    return _pcall(
        kern, name=name,
        out_shape=[jax.ShapeDtypeStruct((n, ln), F32), jax.ShapeDtypeStruct((1, ln), F32),
                   jax.ShapeDtypeStruct((1, ln), F32), jax.ShapeDtypeStruct((ln, dp), F32),
                   jax.ShapeDtypeStruct((ln, dp), F32), jax.ShapeDtypeStruct((dp, dp), F32)],
    )(a_lo, a_hi, pe_lo, pe_hi, w1_lo, w1_hi, w2, pre, pre_sh, dout, dout_sh)


def _nsa_combine(o_cmp, o_slc, o_win, gl):
    s, w = o_cmp.shape
    tr = _tile(s, 512)

    def kern(a_ref, b_ref, c_ref, g_ref, o_ref):
        g = _sigmoid(g_ref[...])
        for h in range(NSA_HEADS):
            cs = slice(h * HEAD_V, (h + 1) * HEAD_V)
            o_ref[:, cs] = (g[:, 3 * h:3 * h + 1] * a_ref[:, cs] + g[:, 3 * h + 1:3 * h + 2] * b_ref[:, cs]
                            + g[:, 3 * h + 2:3 * h + 3] * c_ref[:, cs])

    row = pl.BlockSpec((tr, w), lambda i: (i, 0))
    return _pcall(kern, name="nsa_combine", grid=(s // tr,),
                  in_specs=[row, row, row, pl.BlockSpec((tr, LANE), lambda i: (i, gl.col0))], out_specs=row,
                  out_shape=jax.ShapeDtypeStruct((s, w), F32))(o_cmp, o_slc, o_win, gl.arr)


def _nsa_combine_bwd(do_cat, o_cmp, o_slc, o_win, gl):
    s, w = o_cmp.shape
    tr = _tile(s, 512)

    def kern(d_ref, a_ref, b_ref, c_ref, g_ref, da_ref, db_ref, dc_ref, dg_ref):
        g = _sigmoid(g_ref[...])
        lane = lax.broadcasted_iota(jnp.int32, (tr, LANE), 1)
        dgl = jnp.zeros((tr, LANE), F32)
        for h in range(NSA_HEADS):
            cs = slice(h * HEAD_V, (h + 1) * HEAD_V)
            dv = d_ref[:, cs]
            for b, (src, dst) in enumerate(((a_ref, da_ref), (b_ref, db_ref), (c_ref, dc_ref))):
                gate = g[:, 3 * h + b:3 * h + b + 1]
                dst[:, cs] = gate * dv
                dgate = jnp.sum(dv * src[:, cs], axis=1, keepdims=True)
                dgl = jnp.where(lane == 3 * h + b, dgate * gate * (1.0 - gate), dgl)
        dg_ref[...] = dgl

    row = pl.BlockSpec((tr, w), lambda i: (i, 0))
    tab = pl.BlockSpec((tr, LANE), lambda i: (i, 0))
    return _pcall(kern, name="nsa_combine_bwd", grid=(s // tr,),
                  in_specs=[pl.BlockSpec((tr, w), lambda i: (i, 2)), row, row, row,
                            pl.BlockSpec((tr, LANE), lambda i: (i, gl.col0))],
                  out_specs=[row, row, row, tab],
                  out_shape=[jax.ShapeDtypeStruct((s, w), F32)] * 3 + [jax.ShapeDtypeStruct((s, LANE), F32)],
                  )(do_cat, o_cmp, o_slc, o_win, gl.arr)


def _gate_fwd(o_mla, o_nsa, o_mem, hp):
    s = o_mla.shape[0]
    tr = _tile(s, 256)

    def kern(a_ref, b_ref, c_ref, z_ref, u_ref):
        z = z_ref[...]
        sz = z * _sigmoid(z)
        u_ref[:, 0:1024] = (a_ref[...] * sz[:, 0:1024]).astype(BF16)
        u_ref[:, 1024:1536] = (b_ref[...] * sz[:, 1024:1536]).astype(BF16)
        u_ref[:, 1536:2048] = (c_ref[...] * sz[:, 1536:2048]).astype(BF16)

    return _pcall(
        kern, name="gate_fwd", grid=(s // tr,),
        in_specs=[pl.BlockSpec((tr, 1024), lambda i: (i, 0)), pl.BlockSpec((tr, 512), lambda i: (i, 0)),
                  pl.BlockSpec((tr, 512), lambda i: (i, 0)), pl.BlockSpec((tr, 2048), lambda i: (i, 2))],
        out_specs=pl.BlockSpec((tr, 2048), lambda i: (i, 0)),
        out_shape=jax.ShapeDtypeStruct((s, 2048), BF16))(o_mla, o_nsa, o_mem, hp)


def _gate_bwd(du, o_mla, o_nsa, o_mem, hp):
    s = du.shape[0]
    tr = _tile(s, 256)

    def kern(d_ref, a_ref, b_ref, c_ref, z_ref, do_ref, dz_ref):
        z = z_ref[...]
        sg = _sigmoid(z)
        sz = z * sg
        dsz = sg * (1.0 + z * (1.0 - sg))
        d = d_ref[...]
        do_ref[...] = d * sz
        dz_ref[:, 0:1024] = d[:, 0:1024] * a_ref[...] * dsz[:, 0:1024]
        dz_ref[:, 1024:1536] = d[:, 1024:1536] * b_ref[...] * dsz[:, 1024:1536]
        dz_ref[:, 1536:2048] = d[:, 1536:2048] * c_ref[...] * dsz[:, 1536:2048]

    wide = pl.BlockSpec((tr, 2048), lambda i: (i, 0))
    return _pcall(
        kern, name="gate_bwd", grid=(s // tr,),
        in_specs=[wide, pl.BlockSpec((tr, 1024), lambda i: (i, 0)), pl.BlockSpec((tr, 512), lambda i: (i, 0)),
                  pl.BlockSpec((tr, 512), lambda i: (i, 0)), pl.BlockSpec((tr, 2048), lambda i: (i, 2))],
        out_specs=[wide, wide],
        out_shape=[jax.ShapeDtypeStruct((s, 2048), F32)] * 2)(du, o_mla, o_nsa, o_mem, hp)


def _tile2d(rows, cols, arrays):
    if rows % 16 == 0:
        return _row_tile(rows, cols * arrays), cols
    want = max(LANE, BLOCK_BYTES // (rows * 4 * arrays) // LANE * LANE)
    tc = LANE
    for t in range(LANE, cols + 1, LANE):
        if cols % t == 0 and t <= want:
            tc = t
    return rows, tc


def _sum_slots(buf, name):
    n, rows, cols = buf.shape
    tr, tc = _tile2d(rows, cols, n)

    def kern(b_ref, o_ref):
        acc = b_ref[0].astype(F32)
        for i in range(1, n):
            acc = acc + b_ref[i].astype(F32)
        o_ref[...] = acc

    return _pcall(kern, name=name, grid=(rows // tr, cols // tc),
                  in_specs=[pl.BlockSpec((n, tr, tc), lambda i, j: (0, i, j))],
                  out_specs=pl.BlockSpec((tr, tc), lambda i, j: (i, j)),
                  out_shape=jax.ShapeDtypeStruct((rows, cols), F32))(buf)


def _chip_sum(buf, core, axis, name):
    n, rows, cols = buf.shape
    tr, tc = _tile2d(rows, cols, n)
    nbr, nbc = rows // tr, cols // tc

    def kern(c_ref, b_ref, o_ref, w_ref):
        acc = b_ref[0].astype(F32)
        for i in range(1, n):
            acc = acc + b_ref[i].astype(F32)
        o_ref[...] = acc
        w_ref[...] = acc

    place = ((lambda i, j, c: (c[0] * nbr + i, j)) if axis == 0 else (lambda i, j, c: (i, c[0] * nbc + j)))
    whole = (2 * rows, cols) if axis == 0 else (rows, 2 * cols)
    grid_spec = pltpu.PrefetchScalarGridSpec(
        num_scalar_prefetch=1, grid=(nbr, nbc),
        in_specs=[pl.BlockSpec((n, tr, tc), lambda i, j, c: (0, i, j))],
        out_specs=[pl.BlockSpec((tr, tc), lambda i, j, c: (i, j)), pl.BlockSpec((tr, tc), place)])
    return _pcall(kern, name=name, grid_spec=grid_spec,
                  out_shape=[jax.ShapeDtypeStruct((rows, cols), F32), jax.ShapeDtypeStruct(whole, F32)])(core, buf)


def _pair_sum(g4, theirs, core, axis, name):
    n, rows, cols = theirs.shape
    tr, tc = _tile2d(rows, cols, 1)
    nbr, nbc = rows // tr, cols // tc

    def kern(c_ref, a_ref, b_ref, o_ref):
        o_ref[...] = (a_ref[...] + b_ref[...]).astype(BF16)

    blk = (1, tr, tc)
    mine = ((lambda s, i, j, c: (s, c[0] * nbr + i, j)) if axis == 0
            else (lambda s, i, j, c: (s, i, c[0] * nbc + j)))
    grid_spec = pltpu.PrefetchScalarGridSpec(
        num_scalar_prefetch=1, grid=(n, nbr, nbc),
        in_specs=[pl.BlockSpec(blk, mine), pl.BlockSpec(blk, lambda s, i, j, c: (s, i, j))],
        out_specs=pl.BlockSpec(blk, lambda s, i, j, c: (s, i, j)))
    return _pcall(kern, name=name, grid_spec=grid_spec,
                  out_shape=jax.ShapeDtypeStruct((n, rows, cols), BF16))(core, g4, theirs)


def _adamw(w, g, m, v, name):
    rows, cols = w.shape
    tr, tc = _tile2d(rows, cols, 4)
    bc1 = 1.0 - ADAM_B1 ** ADAM_STEP
    bc2 = 1.0 - ADAM_B2 ** ADAM_STEP

    def kern(w_ref, g_ref, m_ref, v_ref, d_ref, mo_ref, vo_ref):
        gv = g_ref[...]
        mn = ADAM_B1 * m_ref[...] + (1.0 - ADAM_B1) * gv
        vn = ADAM_B2 * v_ref[...] + (1.0 - ADAM_B2) * (gv * gv)
        d_ref[...] = -ADAM_LR * ((mn / bc1) / (jnp.sqrt(vn / bc2) + ADAM_EPS) + ADAM_WD * w_ref[...])
        mo_ref[...] = mn
        vo_ref[...] = vn

    blk = pl.BlockSpec((tr, tc), lambda i, j: (i, j))
    return _pcall(kern, name=name, grid=(rows // tr, cols // tc), in_specs=[blk] * 4, out_specs=[blk] * 3,
                  out_shape=[jax.ShapeDtypeStruct((rows, cols), F32)] * 3)(w, g, m, v)


ANY = pl.BlockSpec(memory_space=pl.ANY)


def _place():
    x, y, c = lax.axis_index("x"), lax.axis_index("y"), lax.axis_index("c")
    chips = [(1 - x, y), (x, 1 - y), (1 - x, 1 - y)]
    return x, y, c, chips


def _remote(src, dst, send_sem, recv_sem, to):
    return pltpu.make_async_remote_copy(src_ref=src, dst_ref=dst, send_sem=send_sem, recv_sem=recv_sem,
                                        device_id=to, device_id_type=MESH)


def _half(ref, lead, core, axis):
    size = ref.shape[len(lead) + axis] // 2
    cut = pl.ds(core * size, size)
    return ref.at[tuple(lead) + ((cut, slice(None)) if axis == 0 else (slice(None), cut))]


def _gather_shards(ws, axes):
    side = _gather_side(ws, axes)

    def body(*refs):
        nw = len(ws)
        split = (refs[:nw], refs[nw:2 * nw], refs[2 * nw:])
        side.phase("start", *split)
        side.phase("finish", *split)

    return _pcall(body, name="gather_shards", in_specs=[ANY] * len(ws), out_specs=[ANY] * len(ws),
                  out_shape=side.out_shape, scratch_shapes=side.scratch)(*ws)


def _gather_side(ws, axes):
    nw = len(ws)

    def phase(which, w_refs, out_refs, sems):
        send_sems, recv_sems = sems
        x, y, c, chips = _place()
        me = 2 * x + y
        sibling = (x, y, 1 - c)

        def part(i, slot, core):
            return _half(out_refs[i], (slot,), core, axes[i])

        def copy(sem, src, dst, to):
            return _remote(src, dst, send_sems.at[sem], recv_sems.at[sem], to)

        first = [copy(j * nw + i, _half(w_refs[i], (), c, axes[i]), part(i, me, c), (*chip, c))
                 for j, chip in enumerate(chips) for i in range(nw)]
        if which == "start":
            for cp in first:
                cp.start()
            return
        passed = []
        for j, (cx, cy) in enumerate(chips):
            slot = 2 * cx + cy
            for i in range(nw):
                copy(j * nw + i, part(i, slot, c), part(i, slot, c), (x, y, c)).wait_recv()
                fwd = copy((3 + j) * nw + i, part(i, slot, c), part(i, slot, c), sibling)
                fwd.start()
                passed.append(fwd)
        for j, (cx, cy) in enumerate(chips):
            slot = 2 * cx + cy
            for i in range(nw):
                copy((3 + j) * nw + i, part(i, slot, 1 - c), part(i, slot, 1 - c), (x, y, c)).wait_recv()
        for cp in first + passed:
            cp.wait_send()

    return _Side(list(ws), [jax.ShapeDtypeStruct((4,) + w.shape, w.dtype) for w in ws],
                 [pltpu.SemaphoreType.DMA((6 * nw,)), pltpu.SemaphoreType.DMA((6 * nw,))], phase)


def _half_shape(shape, axis):
    return tuple(d // 2 if k == len(shape) - 2 + axis else d for k, d in enumerate(shape))


def _pair_exchange(gs, axes, name):
    nw = len(gs)

    def body(*refs):
        g_refs, out_refs = refs[:nw], refs[nw:2 * nw]
        send_sems, recv_sems = refs[2 * nw:]
        x, y, c, _ = _place()
        cps = []
        for i in range(nw):
            cp = _remote(_half(g_refs[i], (slice(None),), 1 - c, axes[i]), out_refs[i],
                         send_sems.at[i], recv_sems.at[i], (x, y, 1 - c))
            cp.start()
            cps.append(cp)
        for cp in cps:
            cp.wait()

    return _pcall(body, name=name, in_specs=[ANY] * nw, out_specs=[ANY] * nw,
                  out_shape=[jax.ShapeDtypeStruct(_half_shape(g.shape, a), g.dtype) for g, a in zip(gs, axes)],
                  scratch_shapes=[pltpu.SemaphoreType.DMA((nw,)), pltpu.SemaphoreType.DMA((nw,))])(*gs)


def _chip_exchange(ps):
    side = _chip_side(ps)

    def body(*refs):
        nw = len(ps)
        split = (refs[:nw], refs[nw:2 * nw], refs[2 * nw:])
        side.phase("start", *split)
        side.phase("finish", *split)

    return _pcall(body, name="chip_exchange", in_specs=[ANY] * len(ps), out_specs=[ANY] * len(ps),
                  out_shape=side.out_shape, scratch_shapes=side.scratch)(*ps)


def _chip_side(ps):
    nw = len(ps)

    def phase(which, p_refs, out_refs, sems):
        send_sems, recv_sems, local_sems = sems
        x, y, c, chips = _place()
        me = 2 * x + y
        mine = [pltpu.make_async_copy(p_refs[i].at[me], out_refs[i].at[me], local_sems.at[i]) for i in range(nw)]
        sends = [_remote(p_refs[i].at[2 * cx + cy], out_refs[i].at[me], send_sems.at[j * nw + i],
                         recv_sems.at[j * nw + i], (cx, cy, c))
                 for j, (cx, cy) in enumerate(chips) for i in range(nw)]
        if which == "start":
            for cp in mine + sends:
                cp.start()
            return
        for j, (cx, cy) in enumerate(chips):
            slot = 2 * cx + cy
            for i in range(nw):
                _remote(out_refs[i].at[slot], out_refs[i].at[slot], send_sems.at[j * nw + i],
                        recv_sems.at[j * nw + i], (x, y, c)).wait_recv()
        for cp in sends:
            cp.wait_send()
        for cp in mine:
            cp.wait()

    return _Side(list(ps), [jax.ShapeDtypeStruct(p.shape, p.dtype) for p in ps],
                 [pltpu.SemaphoreType.DMA((3 * nw,)), pltpu.SemaphoreType.DMA((3 * nw,)),
                  pltpu.SemaphoreType.DMA((nw,))], phase)


def _half_exchange(ts, wholes, axes):
    nw = len(ts)

    def body(*refs):
        t_refs, out_refs = refs[:nw], refs[2 * nw:3 * nw]
        send_sems, recv_sems = refs[3 * nw:]
        x, y, c, _ = _place()
        sends = []
        for i in range(nw):
            cp = _remote(t_refs[i], _half(out_refs[i], (), c, axes[i]), send_sems.at[i], recv_sems.at[i],
                         (x, y, 1 - c))
            cp.start()
            sends.append(cp)
        for i in range(nw):
            _remote(t_refs[i], _half(out_refs[i], (), 1 - c, axes[i]), send_sems.at[i], recv_sems.at[i],
                    (x, y, c)).wait_recv()
        for cp in sends:
            cp.wait_send()

    return _pcall(body, name="half_exchange", in_specs=[ANY] * (2 * nw), out_specs=[ANY] * nw,
                  out_shape=[jax.ShapeDtypeStruct(w.shape, w.dtype) for w in wholes],
                  input_output_aliases={nw + i: i for i in range(nw)},
                  scratch_shapes=[pltpu.SemaphoreType.DMA((nw,)), pltpu.SemaphoreType.DMA((nw,))])(*ts, *wholes)


def _gather_all(v):
    rows, cols = v.shape

    def body(v_ref, out_ref, send_sems, recv_sems, local_sem):
        x, y, c, _ = _place()
        me = 4 * x + 2 * y + c
        mine = pltpu.make_async_copy(v_ref, out_ref.at[me], local_sem)
        mine.start()
        sends = []
        for d in range(1, 8):
            peer = (x ^ (d >> 2), y ^ ((d >> 1) & 1), c ^ (d & 1))
            cp = _remote(v_ref, out_ref.at[me], send_sems.at[d - 1], recv_sems.at[d - 1], peer)
            cp.start()
            sends.append(cp)
        for d in range(1, 8):
            slot = 4 * (x ^ (d >> 2)) + 2 * (y ^ ((d >> 1) & 1)) + (c ^ (d & 1))
            _remote(v_ref, out_ref.at[slot], send_sems.at[d - 1], recv_sems.at[d - 1], (x, y, c)).wait_recv()
        for cp in sends:
            cp.wait_send()
        mine.wait()

    return _pcall(body, name="gather_all", in_specs=[ANY], out_specs=ANY,
                  out_shape=jax.ShapeDtypeStruct((8, rows, cols), v.dtype),
                  scratch_shapes=[pltpu.SemaphoreType.DMA((7,)), pltpu.SemaphoreType.DMA((7,)),
                                  pltpu.SemaphoreType.DMA])(v)


def _pad_cols(a, width):
    return a if a.shape[1] == width else jnp.pad(a, ((0, 0), (0, width - a.shape[1])))


def _unpad_segments():
    z = PAD["z"]
    segs = [(PAD["c_q"], 0, 512), (PAD["c_kv"], 512, 512), (PAD["k_rope"], 1024, 64), (z, 1088, 1024)]
    segs += [(PAD["q_nsa"] + 256 * h, 2112 + NSA_DK * h, NSA_DK) for h in range(NSA_HEADS)]
    for name, rows in (("k_c", 192), ("v_c", 128), ("k_s", 192), ("v_s", 128), ("k_w", 192), ("v_w", 128),
                       ("g_nsa", 12)):
        segs.append((PAD[name], ORIG[name][0], rows))
    segs += [(z + 1024, ORIG["z_nsa"][0], 512), (PAD["q_mem"], ORIG["q_mem"][0], 512),
             (z + 1536, ORIG["z_mem"][0], 512)]
    return segs


def _w_in_grad_slots(gt):
    rows, cols = gt.shape
    shard = sum(n for _, _, n in _unpad_segments()) // 4
    tc = 256
    pieces = []
    for src, dst, n in _unpad_segments():
        while n:
            slot, off = divmod(dst, shard)
            take = min(n, shard - off)
            pieces.append((src, slot, off, take))
            src, dst, n = src + take, dst + take, n - take

    def kern(g_ref, o_ref):
        for src, slot, off, take in pieces:
            o_ref[slot, off:off + take, :] = g_ref[src:src + take, :]

    return _pcall(kern, name="w_in_grad_slots", grid=(cols // tc,),
                  in_specs=[pl.BlockSpec((rows, tc), lambda i: (0, i))],
                  out_specs=pl.BlockSpec((4, shard, tc), lambda i: (0, 0, i)),
                  out_shape=jax.ShapeDtypeStruct((4, shard, cols), F32))(gt)


def _w_in_from_slots(ws):
    nslot, shard, cols = ws.shape
    tc = 256
    pieces = []
    for dst, src, n in _unpad_segments() + [(PAD["k_rope"] + 64, ORIG["k_rope"][0], 64)]:
        while n:
            slot, off = divmod(src, shard)
            take = min(n, shard - off)
            pieces.append((dst, slot, off, take))
            src, dst, n = src + take, dst + take, n - take

    def kern(w_ref, o_ref):
        o_ref[...] = jnp.zeros_like(o_ref)
        for dst, slot, off, take in pieces:
            o_ref[dst:dst + take, :] = w_ref[slot, off:off + take, :]

    return _pcall(kern, name="w_in_from_slots", grid=(cols // tc,),
                  in_specs=[pl.BlockSpec((nslot, shard, tc), lambda i: (0, 0, i))],
                  out_specs=pl.BlockSpec((D_PAD, tc), lambda i: (0, i)),
                  out_shape=jax.ShapeDtypeStruct((D_PAD, cols), ws.dtype))(ws)


def _rope_tables(s):
    pos = jnp.arange(s, dtype=F32)
    inv_freq = ROPE_THETA ** (-jnp.arange(0, 64, 2, dtype=F32) / 64)
    ang = pos[:, None] * inv_freq[None, :]
    cos, sin = jnp.cos(ang), jnp.sin(ang)
    z = jnp.zeros((s, 64), F32)
    return jnp.concatenate([cos, cos, z], axis=1), jnp.concatenate([-sin, sin, z], axis=1)


def _overlap_table(s):
    n_c, n_s = s // CMP_STRIDE, s // SLC_LEN
    c0 = np.arange(n_c)[:, None] * CMP_STRIDE
    s0 = np.arange(LANE)[None, :] * SLC_LEN
    ov = (c0 < s0 + SLC_LEN) & (c0 + CMP_LEN > s0) & (np.arange(n_c)[:, None] < n_c - 1) & (np.arange(LANE)[None, :] < n_s)
    return jnp.asarray(ov.astype(np.float32), dtype=BF16)


def _shift_down(a):
    return jnp.concatenate([jnp.zeros((8, a.shape[1]), a.dtype), a], axis=0)[7:7 + a.shape[0]]


def _shift_up(a):
    return jnp.concatenate([a, jnp.zeros((8, a.shape[1]), a.dtype)], axis=0)[1:1 + a.shape[0]]


def _local_step(x, mem, target, w, hooks=None):
    s = x.shape[0]
    cs, sn = _rope_tables(s)
    t_ = jnp.transpose

    w_in_p = _w_in_from_slots(w["w_in_t"])
    xn, rstd_x = _rms_fwd(_Src(x, D_MODEL), w["norm_g"], "norm_x")
    if hooks is None:
        hp, hpb = _mm(xn, w_in_p, "in_proj", mode="nt", second_dtype=BF16)
    else:
        hp, hpb, *gathered = _mm(xn, w_in_p, "in_proj", mode="nt", second_dtype=BF16, side=hooks.gather_side)
        w = {**w, **hooks.weights(gathered)}

    w_uq3 = w["w_uq"].reshape(512, MLA_HEADS, 192)
    w_uq_p = jnp.concatenate([w_uq3, w_uq3[:, :, 128:]], axis=2).reshape(512, MLA_HEADS * 256)
    w_ukv_p = t_(w["w_ukv"].reshape(512, MLA_HEADS, 2, 128), (0, 2, 1, 3)).reshape(512, 2048)
    c_q, c_kv = _Src(hp, 512, 0), _Src(hp, 512, 1)
    cqn, rstd_q = _rms_fwd(c_q, w["q_norm_g"], "norm_q")
    ckvn, rstd_kv = _rms_fwd(c_kv, w["kv_norm_g"], "norm_kv")
    q_lin = _mm(cqn, w_uq_p, "mla_q_proj")
    kvb = _mm(ckvn, w_ukv_p, "mla_kv_proj", out_dtype=BF16)
    q_mla = _rope_fwd(_Src(q_lin, MLA_HEADS * 256), cs, sn, MLA_HEADS, 256, LANE, "rope_q")
    k_pe = _rope_fwd(_Src(hp, LANE, PAD["k_rope"] // LANE), cs, sn, 1, LANE, 0, "rope_k")
    mla = _Attn("mla", s, s, MLA_HEADS, 256)
    mla_q, mla_v = _Src(q_mla, 256), _Src(kvb, LANE, MLA_HEADS)
    mla_k = [_Src(kvb, LANE), _Src(k_pe, LANE, 0, False)]
    o_mla, l_mla, lr_mla = _attn_fwd(mla, mla_q, mla_k, mla_v, None, "mla_fwd")

    sk = s // CMP_STRIDE
    pe_k, pe_v = w["cmp_pe_k"], w["cmp_pe_v"]
    w1k = _pad_cols(w["cmp_w1k"], 256)
    w2k = jnp.pad(w["cmp_w2k"], ((0, 64), (0, 64))).astype(BF16)
    w1v, w2v = w["cmp_w1v"], w["cmp_w2v"].astype(BF16)
    half_k, half_v = CMP_STRIDE * NSA_DK, CMP_STRIDE * HEAD_V
    ak = hp[:, PAD["k_c"]:PAD["k_c"] + NSA_DK].reshape(sk, half_k)
    av = hp[:, PAD["v_c"]:PAD["v_c"] + HEAD_V].reshape(sk, half_v)
    ck_args = (ak, _shift_up(ak), pe_k[:CMP_STRIDE].reshape(1, half_k), pe_k[CMP_STRIDE:].reshape(1, half_k),
               w1k[:half_k], w1k[half_k:], w2k)
    cv_args = (av, _shift_up(av), pe_v[:CMP_STRIDE].reshape(1, half_v), pe_v[CMP_STRIDE:].reshape(1, half_v),
               w1v[:half_v], w1v[half_v:], w2v)
    k_cmp, pre_k = _compress_fwd(*ck_args, "compress_k")
    v_cmp, pre_v = _compress_fwd(*cv_args, "compress_v")
    cmp_ = _Attn("cmp", s, sk, NSA_HEADS, 256)
    slc = _Attn("slc", s, s, NSA_HEADS, 256)
    win = _Attn("win", s, s, NSA_HEADS, 256)
    nsa_q = _Src(hpb, 256, PAD["q_nsa"] // 256)
    cmp_k, cmp_v = [_Src(k_cmp, 256, 0, False)], _Src(v_cmp, HEAD_V, 0, False)
    slc_k, slc_v = [_Src(hpb, 256, PAD["k_s"] // 256, False)], _Src(hpb, HEAD_V, PAD["v_s"] // HEAD_V, False)
    win_k, win_v = [_Src(hpb, 256, PAD["k_w"] // 256, False)], _Src(hpb, HEAD_V, PAD["v_w"] // HEAD_V, False)
    o_cmp, l_cmp, lr_cmp, sel, selt = _attn_fwd_small(cmp_, nsa_q, cmp_k, cmp_v, "cmp_fwd", _overlap_table(s))
    o_slc, l_slc, lr_slc = _attn_fwd(slc, nsa_q, slc_k, slc_v, sel, "slc_fwd")
    o_win, l_win, lr_win = _attn_fwd_small(win, nsa_q, win_k, win_v, "win_fwd")
    gl = _Src(hp, LANE, PAD["g_nsa"] // LANE)
    o_nsa = _nsa_combine(o_cmp, o_slc, o_win, gl)

    mn, rstd_m = _rms_fwd(_Src(mem, D_MODEL), w["mem_norm_g"], "norm_mem")
    kvm = _mm(mn, w["w_mem_kv"], "mem_kv_proj", out_dtype=BF16)
    mem_ = _Attn("mem", s, mem.shape[0], MEM_HEADS, LANE)
    mem_q, mem_k, mem_v = _Src(hpb, LANE, PAD["q_mem"] // LANE), [_Src(kvm, LANE)], _Src(kvm, LANE, MEM_HEADS)
    o_mem, l_mem, lr_mem = _attn_fwd_small(mem_, mem_q, mem_k, mem_v, "mem_fwd")

    u = _gate_fwd(o_mla, o_nsa, o_mem, hp)
    proj = _mm(u, w["w_out"], "out_proj")
    dy, g_final, loss = _final_loss(x, proj, w["final_norm_g"].reshape(1, -1), target)

    g_w_out = _mm(u, dy, "out_proj_dw", mode="tn")
    du = _mm(dy, w["w_out"], "out_proj_dx", mode="nt")
    do_cat, dz = _gate_bwd(du, o_mla, o_nsa, o_mem, hp)

    dq_mem, (dk_mem,), dv_mem = _attn_bwd(mem_, mem_q, mem_k, mem_v, None, None, _Src(o_mem, HEAD_V), l_mem,
                                          lr_mem, _Src(do_cat, HEAD_V, 12), None, "mem_bwd")
    dkvm = jnp.concatenate([dk_mem, dv_mem], axis=1)
    g_w_mem_kv = _mm(mn, dkvm, "mem_kv_dw", mode="tn")
    dmn = _mm(dkvm, w["w_mem_kv"], "mem_kv_dx", mode="nt")
    _, g_mem_norm = _rms_bwd(_Src(mem, D_MODEL), w["mem_norm_g"], rstd_m, dmn, None, "norm_mem_bwd")

    do_cmp, do_slc, do_win, dgl = _nsa_combine_bwd(do_cat, o_cmp, o_slc, o_win, gl)
    dq_n, (dk_cmp,), dv_cmp = _attn_bwd(cmp_, nsa_q, cmp_k, cmp_v, None, None, _Src(o_cmp, HEAD_V), l_cmp,
                                        lr_cmp, _Src(do_cmp, HEAD_V), None, "cmp_bwd")
    dq_n, (dk_s,), dv_s = _attn_bwd(slc, nsa_q, slc_k, slc_v, sel, selt, _Src(o_slc, HEAD_V), l_slc, lr_slc,
                                    _Src(do_slc, HEAD_V), dq_n, "slc_bwd")
    dq_n, (dk_w,), dv_w = _attn_bwd(win, nsa_q, win_k, win_v, None, None, _Src(o_win, HEAD_V), l_win, lr_win,
                                    _Src(do_win, HEAD_V), dq_n, "win_bwd")
    dak, dpk_lo, dpk_hi, dw1k_lo, dw1k_hi, g_w2k = _compress_bwd(
        *ck_args, pre_k, _shift_down(pre_k), dk_cmp, _shift_down(dk_cmp), "compress_k_bwd")
    dav, dpv_lo, dpv_hi, dw1v_lo, dw1v_hi, g_w2v = _compress_bwd(
        *cv_args, pre_v, _shift_down(pre_v), dv_cmp, _shift_down(dv_cmp), "compress_v_bwd")
    g_pe_k = jnp.concatenate([dpk_lo.reshape(CMP_STRIDE, NSA_DK), dpk_hi.reshape(CMP_STRIDE, NSA_DK)], axis=0)
    g_pe_v = jnp.concatenate([dpv_lo.reshape(CMP_STRIDE, HEAD_V), dpv_hi.reshape(CMP_STRIDE, HEAD_V)], axis=0)
    g_w1k = jnp.concatenate([dw1k_lo, dw1k_hi], axis=0)[:, :NSA_DK]
    g_w1v = jnp.concatenate([dw1v_lo, dw1v_hi], axis=0)
    dk_c = _pad_cols(dak.reshape(s, NSA_DK), 256)
    dv_c = dav.reshape(s, HEAD_V)

    dq_m, (dk_nope, dk_pe), dv_m = _attn_bwd(mla, mla_q, mla_k, mla_v, None, None, _Src(o_mla, HEAD_V), l_mla,
                                             lr_mla, _Src(do_cat, HEAD_V), None, "mla_bwd")
    dq_lin = _rope_bwd_q(dq_m, cs, sn)
    dkv_lin, d_krope = _rope_bwd_k(dk_nope, dk_pe, dv_m, cs, sn)
    g_w_uq_p = _mm(cqn, dq_lin, "mla_q_dw", mode="tn")
    dcqn = _mm(dq_lin, w_uq_p, "mla_q_dx", mode="nt")
    g_w_ukv_p = _mm(ckvn, dkv_lin, "mla_kv_dw", mode="tn")
    dckvn = _mm(dkv_lin, w_ukv_p, "mla_kv_dx", mode="nt")
    dc_q, g_q_norm = _rms_bwd(c_q, w["q_norm_g"], rstd_q, dcqn, None, "norm_q_bwd")
    dc_kv, g_kv_norm = _rms_bwd(c_kv, w["kv_norm_g"], rstd_kv, dckvn, None, "norm_kv_bwd")
    g_w_uq = g_w_uq_p.reshape(512, MLA_HEADS, 256)[:, :, :192].reshape(512, MLA_HEADS * 192)
    g_w_ukv = t_(g_w_ukv_p.reshape(512, 2, MLA_HEADS, 128), (0, 2, 1, 3)).reshape(512, 2048)

    dhp = jnp.concatenate(
        [dc_q, dc_kv, dq_n, dk_c, dk_s, dk_w, d_krope, dv_c, dv_s, dv_w, dgl,
         jnp.zeros((s, PAD["q_mem"] - (PAD["g_nsa"] + LANE)), F32), dq_mem, dz], axis=1).astype(BF16)
    grads = dict(q_norm_g=g_q_norm, w_uq=g_w_uq, kv_norm_g=g_kv_norm,
                 w_ukv=g_w_ukv, cmp_pe_k=g_pe_k, cmp_pe_v=g_pe_v, cmp_w1k=g_w1k, cmp_w2k=g_w2k[:NSA_DK, :NSA_DK],
                 cmp_w1v=g_w1v, cmp_w2v=g_w2v, mem_norm_g=g_mem_norm, w_mem_kv=g_w_mem_kv, w_out=g_w_out,
                 final_norm_g=g_final.reshape(-1))
    if hooks is None:
        g_w_in_t = _w_in_grad_slots(_mm(dhp, xn, "in_proj_dw", mode="tn", wide=2048))
        dxn = _mm(dhp, w_in_p, "in_proj_dx", wide=2048)
    else:
        g_w_in_p, *hooks.received = _mm(dhp, xn, "in_proj_dw", mode="tn", wide=2048, side=hooks.reduce_side(grads))
        g_w_in_t = _w_in_grad_slots(g_w_in_p)
        dxn, hooks.received_w_in = _mm(dhp, w_in_p, "in_proj_dx", wide=2048, side=hooks.reduce_side_w_in(g_w_in_t))
    grad_x, g_norm = _rms_bwd(_Src(x, D_MODEL), w["norm_g"], rstd_x, dxn, dy, "norm_x_bwd")
    grads.update(norm_g=g_norm, w_in_t=g_w_in_t)
    return loss[0, 0], grad_x, grads


def kernel(x, mem, norm_g, w_in, q_norm_g, w_uq, kv_norm_g, w_ukv, cmp_pe_k, cmp_pe_v, cmp_w1k, cmp_w2k, cmp_w1v, cmp_w2v, mem_norm_g, w_mem_kv, w_out, final_norm_g, loss_target, m_norm_g, m_w_in, m_q_norm_g, m_w_uq, m_kv_norm_g, m_w_ukv, m_cmp_pe_k, m_cmp_pe_v, m_cmp_w1k, m_cmp_w2k, m_cmp_w1v, m_cmp_w2v, m_mem_norm_g, m_w_mem_kv, m_w_out, m_final_norm_g, v_norm_g, v_w_in, v_q_norm_g, v_w_uq, v_kv_norm_g, v_w_ukv, v_cmp_pe_k, v_cmp_pe_v, v_cmp_w1k, v_cmp_w2k, v_cmp_w1v, v_cmp_w2v, v_mem_norm_g, v_w_mem_kv, v_w_out, v_final_norm_g):
    args = dict(locals())
    wts = {n: args[n] for n in WEIGHTS}
    loc = {n: (a if n == "final_norm_g" else a[0]) for n, a in wts.items()}

    def to_x(n, a):
        return a.T if n == "w_in" else a

    split = [1 if n == "w_in" else 0 for n in SHARDED]
    rest = [n for n in SHARDED if n != "w_in"]
    chip = 2 * lax.axis_index("x") + lax.axis_index("y")
    core = lax.axis_index("c").astype(jnp.int32).reshape(1)
    own = {n: to_x(n, loc[n]).astype(BF16) for n in SHARDED}

    def with_own_slot(gw, a):
        return lax.dynamic_update_slice(gw, a[None], (chip, 0, 0))

    def slots(n, a):
        if n == "w_in":
            return a
        if SHARD_AXIS[n] == 0:
            return a.reshape(4, a.shape[0] // 4, a.shape[1])
        width = a.shape[1] // 4
        return jnp.stack([a[:, j * width:(j + 1) * width] for j in range(4)])

    def pair_sums(names, grads, name):
        axes = [1 if n == "w_in" else 0 for n in names]
        gs = [slots(n, a) for n, a in zip(names, grads)]
        theirs = _pair_exchange(gs, axes, name)
        return [_pair_sum(a, b, core, ax, "pair_sum_" + n) for n, a, b, ax in zip(names, gs, theirs, axes)]

    class Hooks:
        gather_side = _gather_side([own[n] for n in rest], [0] * len(rest))
        received = None

        @staticmethod
        def weights(gathered):
            out = {}
            for n, gw in zip(rest, gathered):
                gw = with_own_slot(gw, own[n])
                if SHARD_AXIS[n] == 0:
                    out[n] = gw.reshape(4 * gw.shape[1], gw.shape[2])
                else:
                    out[n] = jnp.concatenate([gw[j] for j in range(4)], axis=1)
            return out

        @staticmethod
        def reduce_side(grads):
            return _chip_side(pair_sums(rest, [grads[n] for n in rest], "pair_exchange_rest"))

        @staticmethod
        def reduce_side_w_in(g_w_in_t):
            return _chip_side(pair_sums(["w_in"], [g_w_in_t], "pair_exchange_w_in"))

    hooks = Hooks()

    start = {n: loc[n].reshape(1, -1) if loc[n].ndim == 1 else loc[n] for n in REPLICATED}
    start["w_in_t"] = with_own_slot(_gather_shards([own["w_in"]], [1])[0], own["w_in"])
    loss, grad_x, g = _local_step(x[0], mem[0], loss_target[0], start, hooks)
    loss = lax.psum(loss, ("x", "y", "c"))

    from_chips = dict(zip(rest, hooks.received), w_in=hooks.received_w_in)
    sums = [_chip_sum(from_chips[n], core, ax, "chip_sum_" + n) for n, ax in zip(SHARDED, split)]
    g_sh = _half_exchange([a for a, _ in sums], [b for _, b in sums], split)

    n_rep = sum(int(np.prod(loc[n].shape)) for n in REPLICATED)
    rows_rep = -(-n_rep // (8 * LANE)) * 8

    def rep_pack(parts):
        flat = jnp.concatenate([p.reshape(-1) for p in parts])
        return jnp.pad(flat, (0, rows_rep * LANE - n_rep)).reshape(rows_rep, LANE)

    g_rep = _sum_slots(_gather_all(rep_pack([g[n] for n in REPLICATED])), "replica_sum")
    d_rp, m_rp, v_rp = _adamw(rep_pack([wts[n] for n in REPLICATED]), g_rep,
                              rep_pack([args["m_" + n] for n in REPLICATED]),
                              rep_pack([args["v_" + n] for n in REPLICATED]), "adamw_replicated")

    def rep_unpack(buf):
        flat, out, o = buf.reshape(-1), {}, 0
        for n in REPLICATED:
            size = int(np.prod(wts[n].shape))
            out[n] = flat[o:o + size].reshape(wts[n].shape)
            o += size
        return out

    outs = {k: rep_unpack(b) for k, b in (("g", g_rep), ("d", d_rp), ("m", m_rp), ("v", v_rp))}
    for n, gn in zip(SHARDED, g_sh):
        d, mo, vo = _adamw(to_x(n, loc[n]), gn, to_x(n, args["m_" + n][0]), to_x(n, args["v_" + n][0]),
                           "adamw_" + n)
        for k, a in (("g", gn), ("d", d), ("m", mo), ("v", vo)):
            outs[k][n] = to_x(n, a).reshape(wts[n].shape)

    return (loss, grad_x[None], *[outs["g"][n] for n in WEIGHTS], *[outs["d"][n] for n in WEIGHTS],
            *[outs["m"][n] for n in WEIGHTS], *[outs["v"][n] for n in WEIGHTS])
```

```python
from typing import NamedTuple

import numpy as np
import jax
import jax.numpy as jnp
from jax import lax
from jax.experimental import pallas as pl
from jax.experimental.pallas import tpu as pltpu

F32 = jnp.float32
BF16 = jnp.bfloat16
MESH = pl.DeviceIdType.MESH

D_MODEL = 2048
EPS = 1e-6
LANE = 128
HEAD_V = 128
MLA_HEADS = 8
NSA_HEADS = 4
MEM_HEADS = 4
NSA_DK = 192
CMP_STRIDE = 16
CMP_LEN = 32
SLC_LEN = 64
SLC_TOPN = 16
WIN = 512
NEG = -1e30
LOG2E = 1.4426950408889634
ROPE_THETA = 10000.0
BLOCK_BYTES = 2 << 20

ORIG = dict(c_q=(0, 512), c_kv=(512, 512), k_rope=(1024, 64), z_mla=(1088, 1024),
            q_nsa=(2112, 768), k_c=(2880, 192), v_c=(3072, 128), k_s=(3200, 192),
            v_s=(3392, 128), k_w=(3520, 192), v_w=(3712, 128), g_nsa=(3840, 12),
            z_nsa=(3852, 512), q_mem=(4364, 512), z_mem=(4876, 512))
PAD = dict(c_q=0, c_kv=512, q_nsa=1024, k_c=2048, k_s=2304, k_w=2560, k_rope=2816, v_c=2944,
           v_s=3072, v_w=3200, g_nsa=3328, q_mem=3584, z=4096)
D_PAD = 6144

ADAM_LR, ADAM_B1, ADAM_B2, ADAM_EPS, ADAM_WD, ADAM_STEP = 0.001, 0.9, 0.999, 1e-08, 0.01, 10

SHARDED = ("w_in", "w_uq", "w_ukv", "cmp_w1k", "cmp_w1v", "w_mem_kv", "w_out")
SHARD_AXIS = dict(w_in=1, w_uq=1, w_ukv=1, cmp_w1k=0, cmp_w1v=0, w_mem_kv=0, w_out=0)
REPLICATED = ("norm_g", "q_norm_g", "kv_norm_g", "cmp_pe_k", "cmp_pe_v", "cmp_w2k", "cmp_w2v",
              "mem_norm_g", "final_norm_g")
WEIGHTS = ("norm_g", "w_in", "q_norm_g", "w_uq", "kv_norm_g", "w_ukv", "cmp_pe_k", "cmp_pe_v",
           "cmp_w1k", "cmp_w2k", "cmp_w1v", "cmp_w2v", "mem_norm_g", "w_mem_kv", "w_out",
           "final_norm_g")


def _pcall(kernel, **kw):
    return pl.pallas_call(kernel, **kw)


def _tile(n, pref):
    if n <= pref:
        return n
    for t in range(pref, LANE - 1, -LANE):
        if n % t == 0:
            return t
    raise ValueError((n, pref))


def _row_tile(rows, cols, itemsize=4):
    want = max(16, BLOCK_BYTES // (cols * itemsize))
    if rows <= want:
        return rows
    t = 16
    best = rows
    while t <= want:
        if rows % t == 0:
            best = t
        t *= 2
    return best


def _nt(a, b):
    return lax.dot_general(a, b, (((1,), (1,)), ((), ())), preferred_element_type=F32)


def _tn(a, b):
    return lax.dot_general(a, b, (((0,), (0,)), ((), ())), preferred_element_type=F32)


def _nn(a, b):
    return jnp.dot(a, b, preferred_element_type=F32)


def _sigmoid(x):
    return 1.0 / (1.0 + jnp.exp(-x))


class _Src(NamedTuple):
    arr: jax.Array
    width: int
    col0: int = 0
    per_head: bool = True

    def col(self, h):
        return self.col0 + h if self.per_head else self.col0


class _Side(NamedTuple):
    inputs: list
    out_shape: list
    scratch: list
    phase: object


def _mm(a, b, name, mode="nn", out_dtype=F32, second_dtype=None, wide=1024, side=None):
    if mode == "tn":
        k, m = a.shape
    else:
        m, k = a.shape
    if mode == "nt":
        n, k2 = b.shape
    else:
        k2, n = b.shape
    assert k == k2, (a.shape, b.shape, mode)
    tm, tn, tk = _tile(m, 1024), _tile(n, wide), _tile(k, 2048)
    grid = (m // tm, n // tn, k // tk)
    nk = grid[2]
    assert nk == 1 or (out_dtype == F32 and second_dtype is None)
    dot = {"nn": _nn, "nt": _nt, "tn": _tn}[mode]
    n_in = len(side.inputs) if side else 0
    n_out = len(side.out_shape) if side else 0
    n_res = 1 + (second_dtype is not None)

    def kern(*refs):
        a_ref, b_ref = refs[:2]
        res = refs[2 + n_in:2 + n_in + n_res]
        step = [pl.program_id(d) for d in range(3)]
        if side:
            side_refs = (refs[2:2 + n_in], refs[2 + n_in + n_res:2 + n_in + n_res + n_out],
                         refs[2 + n_in + n_res + n_out:])

            @pl.when((step[0] == 0) & (step[1] == 0) & (step[2] == 0))
            def _():
                side.phase("start", *side_refs)

        r = dot(a_ref[...].astype(BF16), b_ref[...].astype(BF16))
        if nk == 1:
            res[0][...] = r.astype(out_dtype)
            if n_res == 2:
                res[1][...] = r.astype(second_dtype)
        else:
            @pl.when(step[2] == 0)
            def _():
                res[0][...] = r

            @pl.when(step[2] > 0)
            def _():
                res[0][...] += r

        if side:
            @pl.when((step[0] == grid[0] - 1) & (step[1] == grid[1] - 1) & (step[2] == nk - 1))
            def _():
                side.phase("finish", *side_refs)

    a_spec = (pl.BlockSpec((tk, tm), lambda i, j, kk: (kk, i)) if mode == "tn"
              else pl.BlockSpec((tm, tk), lambda i, j, kk: (i, kk)))
    b_spec = (pl.BlockSpec((tn, tk), lambda i, j, kk: (j, kk)) if mode == "nt"
              else pl.BlockSpec((tk, tn), lambda i, j, kk: (kk, j)))
    o_spec = pl.BlockSpec((tm, tn), lambda i, j, kk: (i, j))
    out_specs = [o_spec] * n_res + [ANY] * n_out
    out_shape = [jax.ShapeDtypeStruct((m, n), out_dtype)]
    if second_dtype is not None:
        out_shape.append(jax.ShapeDtypeStruct((m, n), second_dtype))
    out_shape += list(side.out_shape) if side else []
    semantics = ("arbitrary",) * 3 if side else ("parallel", "parallel", "arbitrary")
    out = _pcall(
        kern, name=name, grid=grid, in_specs=[a_spec, b_spec] + [ANY] * n_in, out_specs=out_specs,
        out_shape=out_shape, scratch_shapes=list(side.scratch) if side else [],
        compiler_params=pltpu.CompilerParams(dimension_semantics=semantics),
    )(a, b, *(side.inputs if side else []))
    return out[0] if len(out) == 1 else out


def _rms_fwd(x, g, name):
    r, d = x.arr.shape[0], x.width
    tr = _tile(r, 512)

    def kern(x_ref, g_ref, y_ref, r_ref):
        xv = x_ref[...]
        rstd = lax.rsqrt(jnp.mean(xv * xv, axis=-1, keepdims=True) + EPS)
        y_ref[...] = (xv * rstd * g_ref[...]).astype(BF16)
        r_ref[...] = rstd

    return _pcall(
        kern, name=name, grid=(r // tr,),
        in_specs=[pl.BlockSpec((tr, d), lambda i: (i, x.col0)), pl.BlockSpec((1, d), lambda i: (0, 0))],
        out_specs=[pl.BlockSpec((tr, d), lambda i: (i, 0)), pl.BlockSpec((tr, 1), lambda i: (i, 0))],
        out_shape=[jax.ShapeDtypeStruct((r, d), BF16), jax.ShapeDtypeStruct((r, 1), F32)],
    )(x.arr, g)


def _rms_bwd(x, g, rstd, dy, add, name):
    r, d = x.arr.shape[0], x.width
    tr = _tile(r, 256)
    has_add = add is not None

    def kern(*refs):
        if has_add:
            x_ref, g_ref, r_ref, dy_ref, add_ref, dx_ref, dg_ref = refs
        else:
            x_ref, g_ref, r_ref, dy_ref, dx_ref, dg_ref = refs
        rs = r_ref[...]
        xhat = x_ref[...] * rs
        dyv = dy_ref[...]
        dyg = dyv * g_ref[...]
        c = jnp.mean(dyg * xhat, axis=-1, keepdims=True)
        dx = rs * (dyg - xhat * c)
        if has_add:
            dx = dx + add_ref[...]
        dx_ref[...] = dx
        part = jnp.sum(dyv * xhat, axis=0, keepdims=True)

        @pl.when(pl.program_id(0) == 0)
        def _():
            dg_ref[...] = part

        @pl.when(pl.program_id(0) > 0)
        def _():
            dg_ref[...] += part

    row = pl.BlockSpec((tr, d), lambda i: (i, 0))
    vec = pl.BlockSpec((1, d), lambda i: (0, 0))
    ins = [pl.BlockSpec((tr, d), lambda i: (i, x.col0)), vec, pl.BlockSpec((tr, 1), lambda i: (i, 0)), row]
    ins += [row] if has_add else []
    args = (x.arr, g, rstd, dy) + ((add,) if has_add else ())
    return _pcall(
        kern, name=name, grid=(r // tr,), in_specs=ins, out_specs=[row, vec],
        out_shape=[jax.ShapeDtypeStruct((r, d), F32), jax.ShapeDtypeStruct((1, d), F32)],
        compiler_params=pltpu.CompilerParams(dimension_semantics=("arbitrary",)),
    )(*args)


def _final_loss(x, proj, g, target):
    r, d = x.shape
    tr = _tile(r, 256)

    def kern(x_ref, p_ref, g_ref, t_ref, dy_ref, dg_ref, loss_ref):
        y = x_ref[...] + p_ref[...]
        rs = lax.rsqrt(jnp.mean(y * y, axis=-1, keepdims=True) + EPS)
        yhat = y * rs
        gv = g_ref[...]
        e = yhat * gv - t_ref[...]
        lpart = 0.5 * jnp.sum(jnp.mean(e * e, axis=-1, keepdims=True), axis=0, keepdims=True)
        dout = e * (1.0 / d)
        dyg = dout * gv
        c = jnp.mean(dyg * yhat, axis=-1, keepdims=True)
        dy_ref[...] = rs * (dyg - yhat * c)
        gpart = jnp.sum(dout * yhat, axis=0, keepdims=True)
        lrow = jnp.broadcast_to(lpart, (1, LANE))

        @pl.when(pl.program_id(0) == 0)
        def _():
            dg_ref[...] = gpart
            loss_ref[...] = lrow

        @pl.when(pl.program_id(0) > 0)
        def _():
            dg_ref[...] += gpart
            loss_ref[...] += lrow

    row = pl.BlockSpec((tr, d), lambda i: (i, 0))
    vec = pl.BlockSpec((1, d), lambda i: (0, 0))
    return _pcall(
        kern, name="final_loss", grid=(r // tr,), in_specs=[row, row, vec, row],
        out_specs=[row, vec, pl.BlockSpec((1, LANE), lambda i: (0, 0))],
        out_shape=[jax.ShapeDtypeStruct((r, d), F32), jax.ShapeDtypeStruct((1, d), F32),
                   jax.ShapeDtypeStruct((1, LANE), F32)],
        compiler_params=pltpu.CompilerParams(dimension_semantics=("arbitrary",)),
    )(x, proj, g, target)


def _rope_fwd(x, cs, sn, nh, width, off, name):
    s = x.arr.shape[0]
    tr = _tile(s, 512)

    def kern(x_ref, c_ref, s_ref, o_ref):
        cv, sv = c_ref[...], s_ref[...]
        for h in range(nh):
            b = h * width
            if off:
                o_ref[:, b:b + off] = x_ref[:, b:b + off].astype(BF16)
            xr = x_ref[:, b + off:b + off + LANE]
            o_ref[:, b + off:b + off + LANE] = (xr * cv + pltpu.roll(xr, 32, 1) * sv).astype(BF16)

    tab = pl.BlockSpec((tr, LANE), lambda i: (i, 0))
    return _pcall(
        kern, name=name, grid=(s // tr,),
        in_specs=[pl.BlockSpec((tr, nh * width), lambda i: (i, x.col0)), tab, tab],
        out_specs=pl.BlockSpec((tr, nh * width), lambda i: (i, 0)),
        out_shape=jax.ShapeDtypeStruct((s, nh * width), BF16),
    )(x.arr, cs, sn)


def _rope_grad(d, cv, sv):
    g2 = d * sv
    g2 = g2 + pltpu.roll(g2, 64, 1)
    lane = lax.broadcasted_iota(jnp.int32, d.shape, 1)
    return jnp.where(lane < 64, d * cv + pltpu.roll(g2, 32, 1), 0.0)


def _rope_bwd_q(dq, cs, sn):
    s, w = dq.shape
    tr = _tile(s, 512)
    nh = w // 256

    def kern(d_ref, c_ref, s_ref, o_ref):
        cv, sv = c_ref[...], s_ref[...]
        for h in range(nh):
            b = h * 256
            o_ref[:, b:b + LANE] = d_ref[:, b:b + LANE]
            o_ref[:, b + LANE:b + 256] = _rope_grad(d_ref[:, b + LANE:b + 256], cv, sv)

    row = pl.BlockSpec((tr, w), lambda i: (i, 0))
    tab = pl.BlockSpec((tr, LANE), lambda i: (i, 0))
    return _pcall(kern, name="rope_bwd_q", grid=(s // tr,), in_specs=[row, tab, tab], out_specs=row,
                  out_shape=jax.ShapeDtypeStruct((s, w), F32))(dq, cs, sn)


def _rope_bwd_k(dk_nope, dk_pe, dv, cs, sn):
    s, w = dk_nope.shape
    tr = _tile(s, 512)

    def kern(dk_ref, dp_ref, dv_ref, c_ref, s_ref, okv_ref, okr_ref):
        okv_ref[:, :w] = dk_ref[...]
        okv_ref[:, w:] = dv_ref[...]
        okr_ref[...] = _rope_grad(dp_ref[...], c_ref[...], s_ref[...])

    tab = pl.BlockSpec((tr, LANE), lambda i: (i, 0))
    wide = pl.BlockSpec((tr, w), lambda i: (i, 0))
    return _pcall(
        kern, name="rope_bwd_k", grid=(s // tr,), in_specs=[wide, tab, wide, tab, tab],
        out_specs=[pl.BlockSpec((tr, 2 * w), lambda i: (i, 0)), tab],
        out_shape=[jax.ShapeDtypeStruct((s, 2 * w), F32), jax.ShapeDtypeStruct((s, LANE), F32)],
    )(dk_nope, dk_pe, dv, cs, sn)


class _Attn:
    def __init__(self, mode, s, sk, heads, dk):
        self.mode, self.s, self.sk, self.h, self.dk = mode, s, sk, heads, dk
        self.scale = {"mla": 192 ** -0.5, "mem": 128 ** -0.5}.get(mode, NSA_DK ** -0.5)
        self.tb = min(256, s)
        self.nb = s // self.tb
        self.nsub = 2 if self.nb % 2 == 0 else 1
        self.tq = self.tb * self.nsub
        self.fchains = 1
        self.nq = s // self.tq
        self.qpb = 4 if self.nq % 4 == 0 else 2
        self.causal = mode in ("mla", "slc")
        if self.causal:
            self.tk = self.tq
        elif mode == "win":
            self.tk = WIN + self.tb
        else:
            self.tk = sk
        self.tkb = min(512, sk)
        self.ksub = 1
        self.kb = self.tkb // self.ksub
        self.ncmp = s // CMP_STRIDE - 1

    def mask_bias(self, t, n, h, selx, diag):
        m = self.mode
        if m == "mla":
            return (n <= t) if diag else None, None
        if m == "mem":
            return None, None
        slope = jnp.where(h == 0, 0.25, jnp.where(h == 1, 0.0625, jnp.where(h == 2, 0.015625, 0.00390625)))
        slope = slope.astype(F32) * LOG2E
        if m == "cmp":
            mask = (n * CMP_STRIDE + (CMP_LEN - 1) <= t) & (n < self.ncmp)
            pos = n.astype(F32) * float(CMP_STRIDE) + (CMP_LEN - 1) / 2.0
            return mask, slope * pos
        rel = t - n
        if m == "slc":
            return (rel >= 0) if diag else None, slope * n.astype(F32)
        return (rel >= 0) & (rel < WIN), slope * n.astype(F32)


def _scores(cfg, s_raw, t, n, h, selx, diag, lse=None):
    s = s_raw * (cfg.scale * LOG2E)
    mask, key_term = cfg.mask_bias(t, n, h, selx, diag)
    if key_term is not None:
        s = s + key_term
    if selx is not None:
        s = s + selx
    if lse is None:
        if mask is not None:
            s = jnp.where(mask, s, NEG)
        return s, mask
    p = jnp.exp2(jnp.minimum(s - lse, 0.0))
    if mask is not None:
        p = jnp.where(mask, p, 0.0)
    return p, mask


def _block_of_key(k0, tk, keys_on_rows, value=NEG):
    shape = (tk, LANE) if keys_on_rows else (LANE, tk)
    n = lax.broadcasted_iota(jnp.int32, shape, 0 if keys_on_rows else 1) + k0
    j = lax.broadcasted_iota(jnp.int32, shape, 1 if keys_on_rows else 0)
    return jnp.where((n >> 6) == j, value, 0.0).astype(BF16)


def _to_row(col):
    t = col.shape[0]
    return jnp.transpose(jnp.broadcast_to(col, (t, LANE)))[0:1, :]


def _load_keys(k_refs, rows):
    parts = [r[rows, :].astype(BF16) for r in k_refs]
    return parts[0] if len(parts) == 1 else jnp.concatenate(parts, axis=1)


def _attn_fwd(cfg, q, ks, v, sel, name):
    s, tq, tk, nsub = cfg.s, cfg.tq, cfg.tk, cfg.fchains
    tb = tq // nsub
    per = tb // cfg.tb
    has_sel = sel is not None
    nkp = len(ks)
    qpb = cfg.qpb
    assert cfg.causal and tq == tk and qpb % 2 == 0 and cfg.nq % qpb == 0

    def kern(*refs):
        q_ref, k_refs, v_ref = refs[0], refs[1:1 + nkp], refs[1 + nkp]
        sel_ref = refs[2 + nkp] if has_sel else None
        o_ref, lc_ref, lr_ref = refs[2 + nkp + has_sel:5 + nkp + has_sel]
        buf_a, buf_b = refs[-2:]
        h, g = pl.program_id(0), pl.program_id(1)

        def block(b):
            rows = [slice(b * tq + r * tb, b * tq + (r + 1) * tb) for r in range(nsub)]
            qs = [q_ref[p, :].astype(BF16) for p in rows]
            ts = [(qpb * g + b) * tq + r * tb + lax.broadcasted_iota(jnp.int32, (tb, 1), 0) for r in range(nsub)]
            sels = [sel_ref[p, :].astype(BF16) for p in rows] if has_sel else None
            return rows, qs, ts, sels

        def scores_into(buf, blk, c):
            kk = _load_keys(k_refs, pl.ds(pl.multiple_of(c * tk, tk), tk))
            for r in range(nsub):
                buf[r] = _nt(blk[1][r], kk)

        def consume(buf, blk, c, carry, diag):
            _, _, ts, sels = blk
            k0 = pl.multiple_of(c * tk, tk)
            vv = v_ref[pl.ds(k0, tk), :].astype(BF16)
            emat = _block_of_key(k0, tk, False) if has_sel else None
            n = k0 + lax.broadcasted_iota(jnp.int32, (1, tk), 1)
            new = []
            for r in range(nsub):
                m, l, acc = carry[r]
                selx = _nn(sels[r], emat) if has_sel else None
                sc, mask = _scores(cfg, buf[r], ts[r], n, h, selx, diag)
                m_new = jnp.maximum(m, jnp.max(sc, axis=1, keepdims=True))
                alpha = jnp.exp2(m - m_new)
                p = jnp.exp2(sc - m_new)
                if mask is not None:
                    p = jnp.where(mask, p, 0.0)
                l = alpha * l + jnp.sum(p, axis=1, keepdims=True)
                new.append((m_new, l, alpha * acc + _nn(p.astype(BF16), vv)))
            return tuple(new)

        def finish(blk, b, carry):
            for r, (m, l, acc) in enumerate(carry):
                o_ref[blk[0][r], :] = acc / (l + 1e-20)
                lse = m + jnp.log(l + 1e-20) * LOG2E
                lc_ref[0, blk[0][r], :] = lse
                for u in range(per):
                    lr_ref[0, (b * nsub + r) * per + u] = _to_row(lse[u * cfg.tb:(u + 1) * cfg.tb])

        def pairs(blk, first, other):
            def pair(p, cr):
                scores_into(other, blk, 2 * p + 1)
                cr = consume(first, blk, 2 * p, cr, False)
                scores_into(first, blk, 2 * p + 2)
                return consume(other, blk, 2 * p + 1, cr, False)
            return pair

        init = ((jnp.full((tb, 1), NEG, F32), jnp.zeros((tb, 1), F32), jnp.zeros((tb, HEAD_V), F32)),) * nsub
        cur, oth = buf_a, buf_b
        blk = block(0)
        scores_into(cur, blk, 0)
        for b in range(qpb):
            full = qpb * g + b
            carry = lax.fori_loop(0, (qpb // 2) * g + b // 2, pairs(blk, cur, oth), init)
            nxt = block(b + 1) if b + 1 < qpb else None
            if b % 2 == 0:
                if nxt:
                    scores_into(oth, nxt, 0)
                finish(blk, b, consume(cur, blk, full, carry, True))
                cur, oth = oth, cur
            else:
                scores_into(oth, blk, full)
                carry = consume(cur, blk, full - 1, carry, False)
                if nxt:
                    scores_into(cur, nxt, 0)
                finish(blk, b, consume(oth, blk, full, carry, True))
            blk = nxt

    rows_step = qpb * tq
    ins = [pl.BlockSpec((rows_step, q.width), lambda h, g: (g, q.col(h)))]
    ins += [pl.BlockSpec((cfg.sk, p.width), lambda h, g, p=p: (0, p.col(h))) for p in ks]
    ins += [pl.BlockSpec((cfg.sk, HEAD_V), lambda h, g: (0, v.col(h)))]
    args = [q.arr] + [p.arr for p in ks] + [v.arr]
    if has_sel:
        ins.append(pl.BlockSpec((rows_step, LANE), lambda h, g: (g, 0)))
        args.append(sel)
    return _pcall(
        kern, name=name, grid=(cfg.h, cfg.nq // qpb), in_specs=ins,
        out_specs=[pl.BlockSpec((rows_step, HEAD_V), lambda h, g: (g, h)),
                   pl.BlockSpec((1, rows_step, 1), lambda h, g: (h, g, 0)),
                   pl.BlockSpec((1, rows_step // cfg.tb, 1, cfg.tb), lambda h, g: (h, g, 0, 0))],
        out_shape=[jax.ShapeDtypeStruct((s, cfg.h * HEAD_V), F32),
                   jax.ShapeDtypeStruct((cfg.h, s, 1), F32),
                   jax.ShapeDtypeStruct((cfg.h, cfg.nb, 1, cfg.tb), F32)],
        scratch_shapes=[pltpu.VMEM((nsub, tb, tk), F32)] * 2,
        compiler_params=pltpu.CompilerParams(dimension_semantics=("parallel", "parallel")),
    )(*args)


def _attn_dq(cfg, q, ks, v, sel, o, lse, do, dq_in, name):
    s, tq, tk, dk, nsub = cfg.s, cfg.tq, cfg.tk, cfg.dk, cfg.fchains
    tb = tq // nsub
    per = tb // cfg.tb
    has_sel = sel is not None
    has_in = dq_in is not None
    nkp = len(ks)

    def kern(*refs):
        refs = list(refs)
        q_ref, k_refs, v_ref = refs[0], refs[1:1 + nkp], refs[1 + nkp]
        p0 = 2 + nkp
        sel_ref = refs[p0] if has_sel else None
        p0 += has_sel
        o_ref, l_ref, do_ref = refs[p0:p0 + 3]
        p0 += 3
        in_ref = refs[p0] if has_in else None
        p0 += has_in
        dq_ref, dr_ref = refs[p0:p0 + 2]
        sa, pa, sb, pb = refs[-4:]
        h, g = pl.program_id(0), pl.program_id(1)

        def block(b):
            rows = [slice(b * tq + r * tb, b * tq + (r + 1) * tb) for r in range(nsub)]
            qs = [q_ref[p, :].astype(BF16) for p in rows]
            ts = [(qpb * g + b) * tq + r * tb + lax.broadcasted_iota(jnp.int32, (tb, 1), 0) for r in range(nsub)]
            sels = [sel_ref[p, :].astype(BF16) for p in rows] if has_sel else None
            dvecs, dobs, lses = [], [], []
            for r, p in enumerate(rows):
                dov = do_ref[p, :]
                dvec = jnp.sum(dov * o_ref[p, :], axis=1, keepdims=True)
                for u in range(per):
                    dr_ref[0, (b * nsub + r) * per + u] = _to_row(dvec[u * cfg.tb:(u + 1) * cfg.tb])
                dvecs.append(dvec)
                dobs.append(dov.astype(BF16))
                lses.append(l_ref[0, p, :])
            return rows, qs, ts, sels, dvecs, dobs, lses

        def products_into(sbuf, pbuf, blk, c):
            rows = pl.ds(pl.multiple_of(c * tk, tk), tk)
            kk, vv = _load_keys(k_refs, rows), v_ref[rows, :].astype(BF16)
            for r in range(nsub):
                sbuf[r] = _nt(blk[1][r], kk)
                pbuf[r] = _nt(blk[5][r], vv)

        def consume(sbuf, pbuf, blk, c, accs, diag):
            _, _, ts, sels, dvecs, _, lses = blk
            k0 = pl.multiple_of(c * tk, tk)
            kk = _load_keys(k_refs, pl.ds(k0, tk))
            emat = _block_of_key(k0, tk, False) if has_sel else None
            n = k0 + lax.broadcasted_iota(jnp.int32, (1, tk), 1)
            new = []
            for r in range(nsub):
                selx = _nn(sels[r], emat) if has_sel else None
                p, _ = _scores(cfg, sbuf[r], ts[r], n, h, selx, diag, lses[r])
                ds = p * (pbuf[r] - dvecs[r])
                new.append(accs[r] + _nn(ds.astype(BF16), kk))
            return tuple(new)

        def finish(blk, accs):
            for r, p in enumerate(blk[0]):
                dq_ref[p, :] = accs[r] * cfg.scale + in_ref[p, :] if has_in else accs[r] * cfg.scale

        def pairs(blk, first, other):
            def pair(p, ac):
                products_into(*other, blk, 2 * p + 1)
                ac = consume(*first, blk, 2 * p, ac, False)
                products_into(*first, blk, 2 * p + 2)
                return consume(*other, blk, 2 * p + 1, ac, False)
            return pair

        zero = (jnp.zeros((tb, dk), F32),) * nsub
        cur, oth = (sa, pa), (sb, pb)
        blk = block(0)
        products_into(*cur, blk, 0)
        for b in range(qpb):
            full = qpb * g + b
            accs = lax.fori_loop(0, (qpb // 2) * g + b // 2, pairs(blk, cur, oth), zero)
            nxt = block(b + 1) if b + 1 < qpb else None
            if b % 2 == 0:
                if nxt:
                    products_into(*oth, nxt, 0)
                finish(blk, consume(*cur, blk, full, accs, True))
                cur, oth = oth, cur
            else:
                products_into(*oth, blk, full)
                accs = consume(*cur, blk, full - 1, accs, False)
                if nxt:
                    products_into(*cur, nxt, 0)
                finish(blk, consume(*oth, blk, full, accs, True))
            blk = nxt

    qpb = cfg.qpb
    assert cfg.causal and tq == tk and qpb % 2 == 0 and cfg.nq % qpb == 0
    rows_step = qpb * tq
    qs = pl.BlockSpec((rows_step, dk), lambda h, g: (g, h))
    ins = [pl.BlockSpec((rows_step, q.width), lambda h, g: (g, q.col(h)))]
    ins += [pl.BlockSpec((cfg.sk, p.width), lambda h, g, p=p: (0, p.col(h))) for p in ks]
    ins += [pl.BlockSpec((cfg.sk, HEAD_V), lambda h, g: (0, v.col(h)))]
    args = [q.arr] + [p.arr for p in ks] + [v.arr]
    if has_sel:
        ins.append(pl.BlockSpec((rows_step, LANE), lambda h, g: (g, 0)))
        args.append(sel)
    ins += [pl.BlockSpec((rows_step, HEAD_V), lambda h, g: (g, o.col(h))),
            pl.BlockSpec((1, rows_step, 1), lambda h, g: (h, g, 0)),
            pl.BlockSpec((rows_step, HEAD_V), lambda h, g: (g, do.col(h)))]
    args += [o.arr, lse, do.arr]
    if has_in:
        ins.append(qs)
        args.append(dq_in)
    return _pcall(
        kern, name=name, grid=(cfg.h, cfg.nq // qpb), in_specs=ins,
        out_specs=[qs, pl.BlockSpec((1, rows_step // cfg.tb, 1, cfg.tb), lambda h, g: (h, g, 0, 0))],
        out_shape=[jax.ShapeDtypeStruct((s, cfg.h * dk), F32),
                   jax.ShapeDtypeStruct((cfg.h, cfg.nb, 1, cfg.tb), F32)],
        scratch_shapes=[pltpu.VMEM((nsub, tb, tk), F32)] * 4,
        compiler_params=pltpu.CompilerParams(dimension_semantics=("parallel", "parallel")),
    )(*args)


def _attn_dkv(cfg, q, ks, v, selt, lse_r, d_r, do, name):
    s, tq, tkb, dk, kb, ksub = cfg.s, cfg.tb, cfg.tkb, cfg.dk, cfg.kb, cfg.ksub
    nq = cfg.nb
    has_sel = selt is not None
    nkp = len(ks)
    outs = list(ks) + [v]

    def kern(*refs):
        k_refs, v_ref = refs[:nkp], refs[nkp]
        q_ref, do_ref, lr_ref, dr_ref = refs[nkp + 1:nkp + 5]
        st_ref = refs[nkp + 5] if has_sel else None
        out_refs = refs[nkp + 5 + has_sel:2 * nkp + 6 + has_sel]
        sa, pa, sb, pb = refs[-4:]
        j, h = pl.program_id(0), pl.program_id(1)
        k0 = j * tkb
        part = [slice(u * kb, (u + 1) * kb) for u in range(ksub)]
        kks = [_load_keys(k_refs, p) for p in part]
        vvs = [v_ref[p, :].astype(BF16) for p in part]
        ns = [k0 + u * kb + lax.broadcasted_iota(jnp.int32, (kb, 1), 0) for u in range(ksub)]
        emats = [_block_of_key(k0 + u * kb, kb, True) for u in range(ksub)] if has_sel else None

        def load_q(i):
            rows = pl.ds(pl.multiple_of(i * tq, tq), tq)
            return q_ref[rows, :].astype(BF16), do_ref[rows, :].astype(BF16)

        def products_into(sbuf, pbuf, i):
            qi, doi = load_q(i)
            for u in range(ksub):
                sbuf[u] = _nt(kks[u], qi)
                pbuf[u] = _nt(vvs[u], doi)

        def consume(sbuf, pbuf, i, carry):
            qi, doi = load_q(i)
            t = i * tq + lax.broadcasted_iota(jnp.int32, (1, tq), 1)
            selt_i = st_ref[i].astype(BF16) if has_sel else None
            new = []
            for u in range(ksub):
                dk_acc, dv_acc = carry[u]
                selx = _nn(emats[u], selt_i) if has_sel else None
                pt, _ = _scores(cfg, sbuf[u], t, ns[u], h, selx, True, lr_ref[0, i])
                dv_acc = dv_acc + _nn(pt.astype(BF16), doi)
                dst = pt * (pbuf[u] - dr_ref[0, i])
                new.append((dk_acc + _nn(dst.astype(BF16), qi), dv_acc))
            return tuple(new)

        if cfg.causal:
            first, count = k0 // tq, nq - k0 // tq
        elif cfg.mode == "win":
            first = k0 // tq
            count = jnp.minimum((k0 + tkb + WIN - 2) // tq + 1, nq) - first
        else:
            first, count = 0, nq

        def pair(p, cr):
            i0 = first + 2 * p
            products_into(sb, pb, i0 + 1)
            cr = consume(sa, pa, i0, cr)
            products_into(sa, pa, i0 + 2)
            return consume(sb, pb, i0 + 1, cr)

        carry = ((jnp.zeros((kb, dk), F32), jnp.zeros((kb, HEAD_V), F32)),) * ksub
        products_into(sa, pa, first)
        carry = lax.fori_loop(0, count // 2 - 1, pair, carry)
        last = first + count - 2
        products_into(sb, pb, last + 1)
        carry = consume(sa, pa, last, carry)
        carry = consume(sb, pb, last + 1, carry)
        for u, (dk_acc, dv_acc) in enumerate(carry):
            vals, off = [], 0
            for p in ks:
                vals.append(dk_acc[:, off:off + p.width] * cfg.scale)
                off += p.width
            vals.append(dv_acc)
            for src, ref, val in zip(outs, out_refs, vals):
                if src.per_head:
                    ref[part[u], :] = val
                else:
                    @pl.when(h == 0)
                    def _(ref=ref, val=val, u=u):
                        ref[part[u], :] = val

                    @pl.when(h > 0)
                    def _(ref=ref, val=val, u=u):
                        ref[part[u], :] += val

    rowv = pl.BlockSpec((1, nq, 1, tq), lambda j, h: (h, 0, 0, 0))
    ins = [pl.BlockSpec((tkb, p.width), lambda j, h, p=p: (j, p.col(h))) for p in ks]
    ins += [pl.BlockSpec((tkb, HEAD_V), lambda j, h: (j, v.col(h))),
            pl.BlockSpec((s, q.width), lambda j, h: (0, q.col(h))),
            pl.BlockSpec((s, HEAD_V), lambda j, h: (0, do.col(h))), rowv, rowv]
    args = [p.arr for p in ks] + [v.arr, q.arr, do.arr, lse_r, d_r]
    if has_sel:
        ins.append(pl.BlockSpec((nq, LANE, tq), lambda j, h: (0, 0, 0)))
        args.append(selt)
    out_specs = [pl.BlockSpec((tkb, p.width), lambda j, h, p=p: (j, h if p.per_head else 0)) for p in outs]
    out_shape = [jax.ShapeDtypeStruct((cfg.sk, (cfg.h if p.per_head else 1) * p.width), F32) for p in outs]
    assert nq % 2 == 0 and (cfg.mode in ("cmp", "mem") or tkb % (2 * tq) == 0), (nq, tkb, tq)
    return _pcall(
        kern, name=name, grid=(cfg.sk // tkb, cfg.h), in_specs=ins, out_specs=out_specs, out_shape=out_shape,
        scratch_shapes=[pltpu.VMEM((ksub, kb, tq), F32)] * 4,
        compiler_params=pltpu.CompilerParams(dimension_semantics=("parallel", "arbitrary")),
    )(*args)


def _attn_dkv_flat(cfg, q, ks, v, selt, lse_r, d_r, do, name):
    s, tq, tkb, dk, kb, ksub = cfg.s, cfg.tb, cfg.tkb, cfg.dk, cfg.kb, cfg.ksub
    nq = cfg.nb
    has_sel = selt is not None
    nkp = len(ks)
    outs = list(ks) + [v]
    assert nq % 2 == 0 and tkb % (2 * tq) == 0, (nq, tkb, tq)
    steps = []
    for j in range(cfg.sk // tkb):
        first = j * tkb // tq
        stop = nq if cfg.causal else min((j * tkb + tkb + WIN - 2) // tq + 1, nq)
        steps += [(j, i0) for i0 in range(first, stop, 2)]
    n_pairs = len(steps)
    steps.append(steps[-1])
    tab_j = jnp.asarray(np.array([p[0] for p in steps], np.int32))
    tab_i = jnp.asarray(np.array([p[1] for p in steps], np.int32))

    def kern(tj_ref, ti_ref, *refs):
        k_refs, v_ref = refs[:nkp], refs[nkp]
        q_ref, do_ref, lr_ref, dr_ref = refs[nkp + 1:nkp + 5]
        st_ref = refs[nkp + 5] if has_sel else None
        out_refs = refs[nkp + 5 + has_sel:2 * nkp + 6 + has_sel]
        sa, pa, sb, pb = refs[-4:]
        h = pl.program_id(0)
        for src, ref in zip(outs, out_refs):
            if src.per_head:
                ref[...] = jnp.zeros_like(ref)
            else:
                @pl.when(h == 0)
                def _(ref=ref):
                    ref[...] = jnp.zeros_like(ref)

        def key_rows(j, u):
            return pl.ds(pl.multiple_of(j * tkb + u * kb, kb), kb)

        def load_q(i):
            rows = pl.ds(pl.multiple_of(i * tq, tq), tq)
            return q_ref[rows, :].astype(BF16), do_ref[rows, :].astype(BF16)

        def products_into(sbuf, pbuf, j, i):
            qi, doi = load_q(i)
            for u in range(ksub):
                rows = key_rows(j, u)
                sbuf[u] = _nt(_load_keys(k_refs, rows), qi)
                pbuf[u] = _nt(v_ref[rows, :].astype(BF16), doi)

        def consume(sbuf, pbuf, j, i):
            qi, doi = load_q(i)
            t = i * tq + lax.broadcasted_iota(jnp.int32, (1, tq), 1)
            selt_i = st_ref[i].astype(BF16) if has_sel else None
            res = []
            for u in range(ksub):
                k0 = j * tkb + u * kb
                n = k0 + lax.broadcasted_iota(jnp.int32, (kb, 1), 0)
                selx = _nn(_block_of_key(k0, kb, True), selt_i) if has_sel else None
                pt, _ = _scores(cfg, sbuf[u], t, n, h, selx, True, lr_ref[0, i])
                dst = pt * (pbuf[u] - dr_ref[0, i])
                res.append((_nn(dst.astype(BF16), qi), _nn(pt.astype(BF16), doi)))
            return res

        def pair(p, carry):
            j, i0 = tj_ref[p], ti_ref[p]
            products_into(sb, pb, j, i0 + 1)
            ca = consume(sa, pa, j, i0)
            products_into(sa, pa, tj_ref[p + 1], ti_ref[p + 1])
            cb = consume(sb, pb, j, i0 + 1)
            for u in range(ksub):
                rows = key_rows(j, u)
                dk_c = (ca[u][0] + cb[u][0]) * cfg.scale
                off = 0
                for src, ref in zip(ks, out_refs):
                    ref[rows, :] += dk_c[:, off:off + src.width]
                    off += src.width
                out_refs[nkp][rows, :] += ca[u][1] + cb[u][1]
            return carry

        products_into(sa, pa, tj_ref[0], ti_ref[0])
        lax.fori_loop(0, n_pairs, pair, 0)

    rowv = pl.BlockSpec((1, nq, 1, tq), lambda h, tj, ti: (h, 0, 0, 0))
    ins = [pl.BlockSpec((cfg.sk, p.width), lambda h, tj, ti, p=p: (0, p.col(h))) for p in ks]
    ins += [pl.BlockSpec((cfg.sk, HEAD_V), lambda h, tj, ti: (0, v.col(h))),
            pl.BlockSpec((s, q.width), lambda h, tj, ti: (0, q.col(h))),
            pl.BlockSpec((s, HEAD_V), lambda h, tj, ti: (0, do.col(h))), rowv, rowv]
    args = [p.arr for p in ks] + [v.arr, q.arr, do.arr, lse_r, d_r]
    if has_sel:
        ins.append(pl.BlockSpec((nq, LANE, tq), lambda h, tj, ti: (0, 0, 0)))
        args.append(selt)
    out_specs = [pl.BlockSpec((cfg.sk, p.width), lambda h, tj, ti, p=p: (0, h if p.per_head else 0))
                 for p in outs]
    out_shape = [jax.ShapeDtypeStruct((cfg.sk, (cfg.h if p.per_head else 1) * p.width), F32) for p in outs]
    grid_spec = pltpu.PrefetchScalarGridSpec(
        num_scalar_prefetch=2, grid=(cfg.h,), in_specs=ins, out_specs=out_specs,
        scratch_shapes=[pltpu.VMEM((ksub, kb, tq), F32)] * 4)
    return _pcall(kern, name=name, grid_spec=grid_spec, out_shape=out_shape,
                  compiler_params=pltpu.CompilerParams(dimension_semantics=("arbitrary",)))(tab_j, tab_i, *args)


def _all_heads(cfg, src, rows, key=False):
    if src.per_head:
        assert src.col0 % cfg.h == 0
        width, col = cfg.h * src.width, src.col0 // cfg.h
    else:
        width, col = src.width, src.col0
    return pl.BlockSpec((rows, width), (lambda i: (0, col)) if key else (lambda i: (i, col)))


def _head_cols(src, hh):
    return slice(hh * src.width, (hh + 1) * src.width) if src.per_head else slice(None)


def _key_window(cfg, i, r):
    if cfg.mode == "win":
        return pl.ds(pl.multiple_of(jnp.maximum(i * cfg.tq + r * cfg.tb - WIN, 0), cfg.tb), cfg.tk)
    return pl.ds(0, cfg.tk)


def _attn_fwd_small(cfg, q, ks, v, name, overlap=None):
    s, tq, tk, tb, nsub, nh = cfg.s, cfg.tq, cfg.tk, cfg.tb, cfg.nsub, cfg.h
    nkp = len(ks)
    select = overlap is not None
    n_s = s // SLC_LEN
    top_n = min(SLC_TOPN, n_s)

    def kern(*refs):
        q_ref, k_refs, v_ref = refs[0], refs[1:1 + nkp], refs[1 + nkp]
        ov_ref = refs[2 + nkp] if select else None
        o_ref, lc_ref, lr_ref = refs[2 + nkp + select:5 + nkp + select]
        i = pl.program_id(0)
        imps = [jnp.zeros((tb, LANE), F32)] * nsub
        for r in range(nsub):
            rows = slice(r * tb, (r + 1) * tb)
            t = i * tq + r * tb + lax.broadcasted_iota(jnp.int32, (tb, 1), 0)
            win = _key_window(cfg, i, r)
            n = win.start + lax.broadcasted_iota(jnp.int32, (1, tk), 1)
            for hh in range(nh):
                qv = q_ref[rows, hh * cfg.dk:(hh + 1) * cfg.dk].astype(BF16)
                kk = _load_keys([kr.at[:, _head_cols(p, hh)] for kr, p in zip(k_refs, ks)], win)
                vv = v_ref[win, _head_cols(v, hh)].astype(BF16)
                sc, mask = _scores(cfg, _nt(qv, kk), t, n, hh, None, True)
                m = jnp.max(sc, axis=1, keepdims=True)
                e = jnp.exp2(sc - m)
                if mask is not None:
                    e = jnp.where(mask, e, 0.0)
                l = jnp.sum(e, axis=1, keepdims=True)
                o_ref[rows, hh * HEAD_V:(hh + 1) * HEAD_V] = _nn(e.astype(BF16), vv) / (l + 1e-20)
                lse = m + jnp.log(l + 1e-20) * LOG2E
                lc_ref[hh, rows, :] = lse
                lr_ref[hh, r] = _to_row(lse)
                if select:
                    imps[r] = imps[r] + _nn((e / (l + 1e-20)).astype(BF16), ov_ref[...])
        if select:
            sel_ref, selt_ref, imp_t = refs[5 + nkp + select:8 + nkp + select]
            for r in range(nsub):
                t = i * tq + r * tb + lax.broadcasted_iota(jnp.int32, (tb, 1), 0)
                j = lax.broadcasted_iota(jnp.int32, (tb, LANE), 1)
                cur = t >> 6
                imp = jnp.where((j == 0) | (j == cur) | (j == cur - 1), 1e9, imps[r])
                imp = jnp.where(j > cur, -1e9, imp)
                imp_t[r] = jnp.transpose(imp)
                mine = imp_t[r, 0:n_s, :]
                jrow = lax.broadcasted_iota(jnp.int32, (n_s, tb), 0)

                def count(k, rank):
                    other = imp_t[r, pl.ds(k, 1), :]
                    ahead = (other > mine) | ((other == mine) & (k < jrow))
                    return rank + jnp.where(ahead, 1.0, 0.0)

                rank = lax.fori_loop(0, n_s, count, jnp.zeros((n_s, tb), F32))
                cur_t = (i * tq + r * tb + lax.broadcasted_iota(jnp.int32, (1, tb), 1)) >> 6
                rejected = jnp.where((rank < top_n) & (jrow <= cur_t), 0.0, 1.0)
                if n_s < LANE:
                    rejected = jnp.concatenate([rejected, jnp.ones((LANE - n_s, tb), F32)], axis=0)
                selt_ref[r] = rejected
                sel_ref[r * tb:(r + 1) * tb, :] = jnp.transpose(rejected)

    ins = [_all_heads(cfg, q, tq)] + [_all_heads(cfg, p, cfg.sk, True) for p in ks]
    ins += [_all_heads(cfg, v, cfg.sk, True)]
    args = [q.arr] + [p.arr for p in ks] + [v.arr]
    out_specs = [pl.BlockSpec((tq, nh * HEAD_V), lambda i: (i, 0)),
                 pl.BlockSpec((nh, tq, 1), lambda i: (0, i, 0)),
                 pl.BlockSpec((nh, nsub, 1, tb), lambda i: (0, i, 0, 0))]
    out_shape = [jax.ShapeDtypeStruct((s, nh * HEAD_V), F32), jax.ShapeDtypeStruct((nh, s, 1), F32),
                 jax.ShapeDtypeStruct((nh, cfg.nb, 1, tb), F32)]
    scratch = []
    if select:
        ins.append(pl.BlockSpec((cfg.sk, LANE), lambda i: (0, 0)))
        args.append(overlap)
        out_specs += [pl.BlockSpec((tq, LANE), lambda i: (i, 0)), pl.BlockSpec((nsub, LANE, tb), lambda i: (i, 0, 0))]
        out_shape += [jax.ShapeDtypeStruct((s, LANE), F32), jax.ShapeDtypeStruct((cfg.nb, LANE, tb), F32)]
        scratch = [pltpu.VMEM((nsub, LANE, tb), F32)]
    return _pcall(kern, name=name, grid=(cfg.nq,), in_specs=ins, out_specs=out_specs, out_shape=out_shape,
                  scratch_shapes=scratch,
                  compiler_params=pltpu.CompilerParams(dimension_semantics=("parallel",)))(*args)


def _attn_dq_small(cfg, q, ks, v, o, lse, do, dq_in, name):
    s, tq, tk, tb, nsub, nh, dk = cfg.s, cfg.tq, cfg.tk, cfg.tb, cfg.nsub, cfg.h, cfg.dk
    nkp = len(ks)
    has_in = dq_in is not None

    def kern(*refs):
        q_ref, k_refs, v_ref = refs[0], refs[1:1 + nkp], refs[1 + nkp]
        o_ref, l_ref, do_ref = refs[2 + nkp:5 + nkp]
        in_ref = refs[5 + nkp] if has_in else None
        dq_ref, dr_ref = refs[5 + nkp + has_in:7 + nkp + has_in]
        i = pl.program_id(0)
        for r in range(nsub):
            rows = slice(r * tb, (r + 1) * tb)
            t = i * tq + r * tb + lax.broadcasted_iota(jnp.int32, (tb, 1), 0)
            win = _key_window(cfg, i, r)
            n = win.start + lax.broadcasted_iota(jnp.int32, (1, tk), 1)
            for hh in range(nh):
                vcols = slice(hh * HEAD_V, (hh + 1) * HEAD_V)
                qcols = slice(hh * dk, (hh + 1) * dk)
                qv = q_ref[rows, qcols].astype(BF16)
                kk = _load_keys([kr.at[:, _head_cols(p, hh)] for kr, p in zip(k_refs, ks)], win)
                vv = v_ref[win, _head_cols(v, hh)].astype(BF16)
                dov = do_ref[rows, vcols]
                dvec = jnp.sum(dov * o_ref[rows, vcols], axis=1, keepdims=True)
                dr_ref[hh, r] = _to_row(dvec)
                p, _ = _scores(cfg, _nt(qv, kk), t, n, hh, None, True, l_ref[hh, rows, :])
                ds = p * (_nt(dov.astype(BF16), vv) - dvec)
                dq = _nn(ds.astype(BF16), kk) * cfg.scale
                dq_ref[rows, qcols] = dq + in_ref[rows, qcols] if has_in else dq

    qs = pl.BlockSpec((tq, nh * dk), lambda i: (i, 0))
    ins = [_all_heads(cfg, q, tq)] + [_all_heads(cfg, p, cfg.sk, True) for p in ks]
    ins += [_all_heads(cfg, v, cfg.sk, True)]
    ins += [_all_heads(cfg, o, tq), pl.BlockSpec((nh, tq, 1), lambda i: (0, i, 0)), _all_heads(cfg, do, tq)]
    args = [q.arr] + [p.arr for p in ks] + [v.arr, o.arr, lse, do.arr]
    if has_in:
        ins.append(qs)
        args.append(dq_in)
    return _pcall(
        kern, name=name, grid=(cfg.nq,), in_specs=ins,
        out_specs=[qs, pl.BlockSpec((nh, nsub, 1, tb), lambda i: (0, i, 0, 0))],
        out_shape=[jax.ShapeDtypeStruct((s, nh * dk), F32), jax.ShapeDtypeStruct((nh, cfg.nb, 1, tb), F32)],
        compiler_params=pltpu.CompilerParams(dimension_semantics=("parallel",)))(*args)


def _attn_bwd(cfg, q, ks, v, sel, selt, o, lse, lse_r, do, dq_in, name):
    if cfg.causal:
        dq, d_r = _attn_dq(cfg, q, ks, v, sel, o, lse, do, dq_in, name + "_dq")
    else:
        dq, d_r = _attn_dq_small(cfg, q, ks, v, o, lse, do, dq_in, name + "_dq")
    dkv = _attn_dkv_flat if cfg.causal or cfg.mode == "win" else _attn_dkv
    res = dkv(cfg, q, ks, v, selt, lse_r, d_r, do, name + "_dkv")
    return dq, res[:-1], res[-1]


def _silu_grad(pre):
    sg = _sigmoid(pre)
    return sg * (1.0 + pre * (1.0 - sg))


def _compress_fwd(a_lo, a_hi, pe_lo, pe_hi, w1_lo, w1_hi, w2, name):
    n, dp = a_lo.shape[0], w2.shape[1]

    def kern(alo, ahi, plo, phi, w1l, w1h, w2r, out_ref, pre_ref):
        xl = (alo[...] + plo[...]).astype(BF16)
        xh = (ahi[...] + phi[...]).astype(BF16)
        pre = _nn(xl, w1l[...]) + _nn(xh, w1h[...])
        act = pre * _sigmoid(pre)
        out_ref[...] = _nn(act.astype(BF16), w2r[...]).astype(BF16)
        pre_ref[...] = pre

    return _pcall(kern, name=name,
                  out_shape=[jax.ShapeDtypeStruct((n, dp), BF16), jax.ShapeDtypeStruct((n, dp), F32)],
                  )(a_lo, a_hi, pe_lo, pe_hi, w1_lo, w1_hi, w2)


def _compress_bwd(a_lo, a_hi, pe_lo, pe_hi, w1_lo, w1_hi, w2, pre, pre_sh, dout, dout_sh, name):
    n, ln = a_lo.shape
    dp = w2.shape[1]

    def kern(alo, ahi, plo, phi, w1l, w1h, w2r, pre_ref, presh_ref, do_ref, dosh_ref,
             da_ref, dpl_ref, dph_ref, dw1l_ref, dw1h_ref, dw2_ref):
        prev = pre_ref[...]
        act = prev * _sigmoid(prev)
        dob = do_ref[...].astype(BF16)
        w2v = w2r[...]
        dpre = (_nt(dob, w2v) * _silu_grad(prev)).astype(BF16)
        dpre_sh = (_nt(dosh_ref[...].astype(BF16), w2v) * _silu_grad(presh_ref[...])).astype(BF16)
        dw2_ref[...] = _nn(act.T.astype(BF16), dob)
        xl = alo[...] + plo[...]
        xh = ahi[...] + phi[...]
        dw1l_ref[...] = _nn(xl.T.astype(BF16), dpre)
        dw1h_ref[...] = _nn(xh.T.astype(BF16), dpre)
        dal = _nt(dpre, w1l[...])
        dah_sh = _nt(dpre_sh, w1h[...])
        da_ref[...] = dal + dah_sh
        dpl_ref[...] = jnp.sum(dal, axis=0, keepdims=True)
        dph_ref[...] = jnp.sum(dah_sh, axis=0, keepdims=True)

    return _pcall(
        kern, name=name,
        out_shape=[jax.ShapeDtypeStruct((n, ln), F32), jax.ShapeDtypeStruct((1, ln), F32),
                   jax.ShapeDtypeStruct((1, ln), F32), jax.ShapeDtypeStruct((ln, dp), F32),
                   jax.ShapeDtypeStruct((ln, dp), F32), jax.ShapeDtypeStruct((dp, dp), F32)],
    )(a_lo, a_hi, pe_lo, pe_hi, w1_lo, w1_hi, w2, pre, pre_sh, dout, dout_sh)


def _nsa_combine(o_cmp, o_slc, o_win, gl):
    s, w = o_cmp.shape
    tr = _tile(s, 512)

    def kern(a_ref, b_ref, c_ref, g_ref, o_ref):
        g = _sigmoid(g_ref[...])
        for h in range(NSA_HEADS):
            cs = slice(h * HEAD_V, (h + 1) * HEAD_V)
            o_ref[:, cs] = (g[:, 3 * h:3 * h + 1] * a_ref[:, cs] + g[:, 3 * h + 1:3 * h + 2] * b_ref[:, cs]
                            + g[:, 3 * h + 2:3 * h + 3] * c_ref[:, cs])

    row = pl.BlockSpec((tr, w), lambda i: (i, 0))
    return _pcall(kern, name="nsa_combine", grid=(s // tr,),
                  in_specs=[row, row, row, pl.BlockSpec((tr, LANE), lambda i: (i, gl.col0))], out_specs=row,
                  out_shape=jax.ShapeDtypeStruct((s, w), F32))(o_cmp, o_slc, o_win, gl.arr)


def _nsa_combine_bwd(do_cat, o_cmp, o_slc, o_win, gl):
    s, w = o_cmp.shape
    tr = _tile(s, 512)

    def kern(d_ref, a_ref, b_ref, c_ref, g_ref, da_ref, db_ref, dc_ref, dg_ref):
        g = _sigmoid(g_ref[...])
        lane = lax.broadcasted_iota(jnp.int32, (tr, LANE), 1)
        dgl = jnp.zeros((tr, LANE), F32)
        for h in range(NSA_HEADS):
            cs = slice(h * HEAD_V, (h + 1) * HEAD_V)
            dv = d_ref[:, cs]
            for b, (src, dst) in enumerate(((a_ref, da_ref), (b_ref, db_ref), (c_ref, dc_ref))):
                gate = g[:, 3 * h + b:3 * h + b + 1]
                dst[:, cs] = gate * dv
                dgate = jnp.sum(dv * src[:, cs], axis=1, keepdims=True)
                dgl = jnp.where(lane == 3 * h + b, dgate * gate * (1.0 - gate), dgl)
        dg_ref[...] = dgl

    row = pl.BlockSpec((tr, w), lambda i: (i, 0))
    tab = pl.BlockSpec((tr, LANE), lambda i: (i, 0))
    return _pcall(kern, name="nsa_combine_bwd", grid=(s // tr,),
                  in_specs=[pl.BlockSpec((tr, w), lambda i: (i, 2)), row, row, row,
                            pl.BlockSpec((tr, LANE), lambda i: (i, gl.col0))],
                  out_specs=[row, row, row, tab],
                  out_shape=[jax.ShapeDtypeStruct((s, w), F32)] * 3 + [jax.ShapeDtypeStruct((s, LANE), F32)],
                  )(do_cat, o_cmp, o_slc, o_win, gl.arr)


def _gate_fwd(o_mla, o_nsa, o_mem, hp):
    s = o_mla.shape[0]
    tr = _tile(s, 256)

    def kern(a_ref, b_ref, c_ref, z_ref, u_ref):
        z = z_ref[...]
        sz = z * _sigmoid(z)
        u_ref[:, 0:1024] = (a_ref[...] * sz[:, 0:1024]).astype(BF16)
        u_ref[:, 1024:1536] = (b_ref[...] * sz[:, 1024:1536]).astype(BF16)
        u_ref[:, 1536:2048] = (c_ref[...] * sz[:, 1536:2048]).astype(BF16)

    return _pcall(
        kern, name="gate_fwd", grid=(s // tr,),
        in_specs=[pl.BlockSpec((tr, 1024), lambda i: (i, 0)), pl.BlockSpec((tr, 512), lambda i: (i, 0)),
                  pl.BlockSpec((tr, 512), lambda i: (i, 0)), pl.BlockSpec((tr, 2048), lambda i: (i, 2))],
        out_specs=pl.BlockSpec((tr, 2048), lambda i: (i, 0)),
        out_shape=jax.ShapeDtypeStruct((s, 2048), BF16))(o_mla, o_nsa, o_mem, hp)


def _gate_bwd(du, o_mla, o_nsa, o_mem, hp):
    s = du.shape[0]
    tr = _tile(s, 256)

    def kern(d_ref, a_ref, b_ref, c_ref, z_ref, do_ref, dz_ref):
        z = z_ref[...]
        sg = _sigmoid(z)
        sz = z * sg
        dsz = sg * (1.0 + z * (1.0 - sg))
        d = d_ref[...]
        do_ref[...] = d * sz
        dz_ref[:, 0:1024] = d[:, 0:1024] * a_ref[...] * dsz[:, 0:1024]
        dz_ref[:, 1024:1536] = d[:, 1024:1536] * b_ref[...] * dsz[:, 1024:1536]
        dz_ref[:, 1536:2048] = d[:, 1536:2048] * c_ref[...] * dsz[:, 1536:2048]

    wide = pl.BlockSpec((tr, 2048), lambda i: (i, 0))
    return _pcall(
        kern, name="gate_bwd", grid=(s // tr,),
        in_specs=[wide, pl.BlockSpec((tr, 1024), lambda i: (i, 0)), pl.BlockSpec((tr, 512), lambda i: (i, 0)),
                  pl.BlockSpec((tr, 512), lambda i: (i, 0)), pl.BlockSpec((tr, 2048), lambda i: (i, 2))],
        out_specs=[wide, wide],
        out_shape=[jax.ShapeDtypeStruct((s, 2048), F32)] * 2)(du, o_mla, o_nsa, o_mem, hp)


def _tile2d(rows, cols, arrays):
    if rows % 16 == 0:
        return _row_tile(rows, cols * arrays), cols
    want = max(LANE, BLOCK_BYTES // (rows * 4 * arrays) // LANE * LANE)
    tc = LANE
    for t in range(LANE, cols + 1, LANE):
        if cols % t == 0 and t <= want:
            tc = t
    return rows, tc


def _sum_slots(buf, name):
    n, rows, cols = buf.shape
    tr, tc = _tile2d(rows, cols, n)

    def kern(b_ref, o_ref):
        acc = b_ref[0].astype(F32)
        for i in range(1, n):
            acc = acc + b_ref[i].astype(F32)
        o_ref[...] = acc

    return _pcall(kern, name=name, grid=(rows // tr, cols // tc),
                  in_specs=[pl.BlockSpec((n, tr, tc), lambda i, j: (0, i, j))],
                  out_specs=pl.BlockSpec((tr, tc), lambda i, j: (i, j)),
                  out_shape=jax.ShapeDtypeStruct((rows, cols), F32))(buf)


def _chip_sum(buf, core, axis, name):
    n, rows, cols = buf.shape
    tr, tc = _tile2d(rows, cols, n)
    nbr, nbc = rows // tr, cols // tc

    def kern(c_ref, b_ref, o_ref, w_ref):
        acc = b_ref[0].astype(F32)
        for i in range(1, n):
            acc = acc + b_ref[i].astype(F32)
        o_ref[...] = acc
        w_ref[...] = acc

    place = ((lambda i, j, c: (c[0] * nbr + i, j)) if axis == 0 else (lambda i, j, c: (i, c[0] * nbc + j)))
    whole = (2 * rows, cols) if axis == 0 else (rows, 2 * cols)
    grid_spec = pltpu.PrefetchScalarGridSpec(
        num_scalar_prefetch=1, grid=(nbr, nbc),
        in_specs=[pl.BlockSpec((n, tr, tc), lambda i, j, c: (0, i, j))],
        out_specs=[pl.BlockSpec((tr, tc), lambda i, j, c: (i, j)), pl.BlockSpec((tr, tc), place)])
    return _pcall(kern, name=name, grid_spec=grid_spec,
                  out_shape=[jax.ShapeDtypeStruct((rows, cols), F32), jax.ShapeDtypeStruct(whole, F32)])(core, buf)


def _pair_sum(g4, theirs, core, axis, name):
    n, rows, cols = theirs.shape
    tr, tc = _tile2d(rows, cols, 1)
    nbr, nbc = rows // tr, cols // tc

    def kern(c_ref, a_ref, b_ref, o_ref):
        o_ref[...] = (a_ref[...] + b_ref[...]).astype(BF16)

    blk = (1, tr, tc)
    mine = ((lambda s, i, j, c: (s, c[0] * nbr + i, j)) if axis == 0
            else (lambda s, i, j, c: (s, i, c[0] * nbc + j)))
    grid_spec = pltpu.PrefetchScalarGridSpec(
        num_scalar_prefetch=1, grid=(n, nbr, nbc),
        in_specs=[pl.BlockSpec(blk, mine), pl.BlockSpec(blk, lambda s, i, j, c: (s, i, j))],
        out_specs=pl.BlockSpec(blk, lambda s, i, j, c: (s, i, j)))
    return _pcall(kern, name=name, grid_spec=grid_spec,
                  out_shape=jax.ShapeDtypeStruct((n, rows, cols), BF16))(core, g4, theirs)


def _adamw(w, g, m, v, name):
    rows, cols = w.shape
    tr, tc = _tile2d(rows, cols, 4)
    bc1 = 1.0 - ADAM_B1 ** ADAM_STEP
    bc2 = 1.0 - ADAM_B2 ** ADAM_STEP

    def kern(w_ref, g_ref, m_ref, v_ref, d_ref, mo_ref, vo_ref):
        gv = g_ref[...]
        mn = ADAM_B1 * m_ref[...] + (1.0 - ADAM_B1) * gv
        vn = ADAM_B2 * v_ref[...] + (1.0 - ADAM_B2) * (gv * gv)
        d_ref[...] = -ADAM_LR * ((mn / bc1) / (jnp.sqrt(vn / bc2) + ADAM_EPS) + ADAM_WD * w_ref[...])
        mo_ref[...] = mn
        vo_ref[...] = vn

    blk = pl.BlockSpec((tr, tc), lambda i, j: (i, j))
    return _pcall(kern, name=name, grid=(rows // tr, cols // tc), in_specs=[blk] * 4, out_specs=[blk] * 3,
                  out_shape=[jax.ShapeDtypeStruct((rows, cols), F32)] * 3)(w, g, m, v)


ANY = pl.BlockSpec(memory_space=pl.ANY)


def _place():
    x, y, c = lax.axis_index("x"), lax.axis_index("y"), lax.axis_index("c")
    chips = [(1 - x, y), (x, 1 - y), (1 - x, 1 - y)]
    return x, y, c, chips


def _remote(src, dst, send_sem, recv_sem, to):
    return pltpu.make_async_remote_copy(src_ref=src, dst_ref=dst, send_sem=send_sem, recv_sem=recv_sem,
                                        device_id=to, device_id_type=MESH)


def _half(ref, lead, core, axis):
    size = ref.shape[len(lead) + axis] // 2
    cut = pl.ds(core * size, size)
    return ref.at[tuple(lead) + ((cut, slice(None)) if axis == 0 else (slice(None), cut))]


def _gather_shards(ws, axes):
    side = _gather_side(ws, axes)

    def body(*refs):
        nw = len(ws)
        split = (refs[:nw], refs[nw:2 * nw], refs[2 * nw:])
        side.phase("start", *split)
        side.phase("finish", *split)

    return _pcall(body, name="gather_shards", in_specs=[ANY] * len(ws), out_specs=[ANY] * len(ws),
                  out_shape=side.out_shape, scratch_shapes=side.scratch)(*ws)


def _gather_side(ws, axes):
    nw = len(ws)

    def phase(which, w_refs, out_refs, sems):
        send_sems, recv_sems = sems
        x, y, c, chips = _place()
        me = 2 * x + y
        sibling = (x, y, 1 - c)

        def part(i, slot, core):
            return _half(out_refs[i], (slot,), core, axes[i])

        def copy(sem, src, dst, to):
            return _remote(src, dst, send_sems.at[sem], recv_sems.at[sem], to)

        first = [copy(j * nw + i, _half(w_refs[i], (), c, axes[i]), part(i, me, c), (*chip, c))
                 for j, chip in enumerate(chips) for i in range(nw)]
        if which == "start":
            for cp in first:
                cp.start()
            return
        passed = []
        for j, (cx, cy) in enumerate(chips):
            slot = 2 * cx + cy
            for i in range(nw):
                copy(j * nw + i, part(i, slot, c), part(i, slot, c), (x, y, c)).wait_recv()
                fwd = copy((3 + j) * nw + i, part(i, slot, c), part(i, slot, c), sibling)
                fwd.start()
                passed.append(fwd)
        for j, (cx, cy) in enumerate(chips):
            slot = 2 * cx + cy
            for i in range(nw):
                copy((3 + j) * nw + i, part(i, slot, 1 - c), part(i, slot, 1 - c), (x, y, c)).wait_recv()
        for cp in first + passed:
            cp.wait_send()

    return _Side(list(ws), [jax.ShapeDtypeStruct((4,) + w.shape, w.dtype) for w in ws],
                 [pltpu.SemaphoreType.DMA((6 * nw,)), pltpu.SemaphoreType.DMA((6 * nw,))], phase)


def _half_shape(shape, axis):
    return tuple(d // 2 if k == len(shape) - 2 + axis else d for k, d in enumerate(shape))


def _pair_exchange(gs, axes, name):
    nw = len(gs)

    def body(*refs):
        g_refs, out_refs = refs[:nw], refs[nw:2 * nw]
        send_sems, recv_sems = refs[2 * nw:]
        x, y, c, _ = _place()
        cps = []
        for i in range(nw):
            cp = _remote(_half(g_refs[i], (slice(None),), 1 - c, axes[i]), out_refs[i],
                         send_sems.at[i], recv_sems.at[i], (x, y, 1 - c))
            cp.start()
            cps.append(cp)
        for cp in cps:
            cp.wait()

    return _pcall(body, name=name, in_specs=[ANY] * nw, out_specs=[ANY] * nw,
                  out_shape=[jax.ShapeDtypeStruct(_half_shape(g.shape, a), g.dtype) for g, a in zip(gs, axes)],
                  scratch_shapes=[pltpu.SemaphoreType.DMA((nw,)), pltpu.SemaphoreType.DMA((nw,))])(*gs)


def _chip_exchange(ps):
    side = _chip_side(ps)

    def body(*refs):
        nw = len(ps)
        split = (refs[:nw], refs[nw:2 * nw], refs[2 * nw:])
        side.phase("start", *split)
        side.phase("finish", *split)

    return _pcall(body, name="chip_exchange", in_specs=[ANY] * len(ps), out_specs=[ANY] * len(ps),
                  out_shape=side.out_shape, scratch_shapes=side.scratch)(*ps)


def _chip_side(ps):
    nw = len(ps)

    def phase(which, p_refs, out_refs, sems):
        send_sems, recv_sems, local_sems = sems
        x, y, c, chips = _place()
        me = 2 * x + y
        mine = [pltpu.make_async_copy(p_refs[i].at[me], out_refs[i].at[me], local_sems.at[i]) for i in range(nw)]
        sends = [_remote(p_refs[i].at[2 * cx + cy], out_refs[i].at[me], send_sems.at[j * nw + i],
                         recv_sems.at[j * nw + i], (cx, cy, c))
                 for j, (cx, cy) in enumerate(chips) for i in range(nw)]
        if which == "start":
            for cp in mine + sends:
                cp.start()
            return
        for j, (cx, cy) in enumerate(chips):
            slot = 2 * cx + cy
            for i in range(nw):
                _remote(out_refs[i].at[slot], out_refs[i].at[slot], send_sems.at[j * nw + i],
                        recv_sems.at[j * nw + i], (x, y, c)).wait_recv()
        for cp in sends:
            cp.wait_send()
        for cp in mine:
            cp.wait()

    return _Side(list(ps), [jax.ShapeDtypeStruct(p.shape, p.dtype) for p in ps],
                 [pltpu.SemaphoreType.DMA((3 * nw,)), pltpu.SemaphoreType.DMA((3 * nw,)),
                  pltpu.SemaphoreType.DMA((nw,))], phase)


def _half_exchange(ts, wholes, axes):
    nw = len(ts)

    def body(*refs):
        t_refs, out_refs = refs[:nw], refs[2 * nw:3 * nw]
        send_sems, recv_sems = refs[3 * nw:]
        x, y, c, _ = _place()
        sends = []
        for i in range(nw):
            cp = _remote(t_refs[i], _half(out_refs[i], (), c, axes[i]), send_sems.at[i], recv_sems.at[i],
                         (x, y, 1 - c))
            cp.start()
            sends.append(cp)
        for i in range(nw):
            _remote(t_refs[i], _half(out_refs[i], (), 1 - c, axes[i]), send_sems.at[i], recv_sems.at[i],
                    (x, y, c)).wait_recv()
        for cp in sends:
            cp.wait_send()

    return _pcall(body, name="half_exchange", in_specs=[ANY] * (2 * nw), out_specs=[ANY] * nw,
                  out_shape=[jax.ShapeDtypeStruct(w.shape, w.dtype) for w in wholes],
                  input_output_aliases={nw + i: i for i in range(nw)},
                  scratch_shapes=[pltpu.SemaphoreType.DMA((nw,)), pltpu.SemaphoreType.DMA((nw,))])(*ts, *wholes)


def _gather_all(v):
    rows, cols = v.shape

    def body(v_ref, out_ref, send_sems, recv_sems, local_sem):
        x, y, c, _ = _place()
        me = 4 * x + 2 * y + c
        mine = pltpu.make_async_copy(v_ref, out_ref.at[me], local_sem)
        mine.start()
        sends = []
        for d in range(1, 8):
            peer = (x ^ (d >> 2), y ^ ((d >> 1) & 1), c ^ (d & 1))
            cp = _remote(v_ref, out_ref.at[me], send_sems.at[d - 1], recv_sems.at[d - 1], peer)
            cp.start()
            sends.append(cp)
        for d in range(1, 8):
            slot = 4 * (x ^ (d >> 2)) + 2 * (y ^ ((d >> 1) & 1)) + (c ^ (d & 1))
            _remote(v_ref, out_ref.at[slot], send_sems.at[d - 1], recv_sems.at[d - 1], (x, y, c)).wait_recv()
        for cp in sends:
            cp.wait_send()
        mine.wait()

    return _pcall(body, name="gather_all", in_specs=[ANY], out_specs=ANY,
                  out_shape=jax.ShapeDtypeStruct((8, rows, cols), v.dtype),
                  scratch_shapes=[pltpu.SemaphoreType.DMA((7,)), pltpu.SemaphoreType.DMA((7,)),
                                  pltpu.SemaphoreType.DMA])(v)


def _pad_cols(a, width):
    return a if a.shape[1] == width else jnp.pad(a, ((0, 0), (0, width - a.shape[1])))


def _unpad_segments():
    z = PAD["z"]
    segs = [(PAD["c_q"], 0, 512), (PAD["c_kv"], 512, 512), (PAD["k_rope"], 1024, 64), (z, 1088, 1024)]
    segs += [(PAD["q_nsa"] + 256 * h, 2112 + NSA_DK * h, NSA_DK) for h in range(NSA_HEADS)]
    for name, rows in (("k_c", 192), ("v_c", 128), ("k_s", 192), ("v_s", 128), ("k_w", 192), ("v_w", 128),
                       ("g_nsa", 12)):
        segs.append((PAD[name], ORIG[name][0], rows))
    segs += [(z + 1024, ORIG["z_nsa"][0], 512), (PAD["q_mem"], ORIG["q_mem"][0], 512),
             (z + 1536, ORIG["z_mem"][0], 512)]
    return segs


def _w_in_grad_slots(gt):
    rows, cols = gt.shape
    shard = sum(n for _, _, n in _unpad_segments()) // 4
    tc = 256
    pieces = []
    for src, dst, n in _unpad_segments():
        while n:
            slot, off = divmod(dst, shard)
            take = min(n, shard - off)
            pieces.append((src, slot, off, take))
            src, dst, n = src + take, dst + take, n - take

    def kern(g_ref, o_ref):
        for src, slot, off, take in pieces:
            o_ref[slot, off:off + take, :] = g_ref[src:src + take, :]

    return _pcall(kern, name="w_in_grad_slots", grid=(cols // tc,),
                  in_specs=[pl.BlockSpec((rows, tc), lambda i: (0, i))],
                  out_specs=pl.BlockSpec((4, shard, tc), lambda i: (0, 0, i)),
                  out_shape=jax.ShapeDtypeStruct((4, shard, cols), F32))(gt)


def _w_in_from_slots(ws):
    nslot, shard, cols = ws.shape
    tc = 256
    pieces = []
    for dst, src, n in _unpad_segments() + [(PAD["k_rope"] + 64, ORIG["k_rope"][0], 64)]:
        while n:
            slot, off = divmod(src, shard)
            take = min(n, shard - off)
            pieces.append((dst, slot, off, take))
            src, dst, n = src + take, dst + take, n - take

    def kern(w_ref, o_ref):
        o_ref[...] = jnp.zeros_like(o_ref)
        for dst, slot, off, take in pieces:
            o_ref[dst:dst + take, :] = w_ref[slot, off:off + take, :]

    return _pcall(kern, name="w_in_from_slots", grid=(cols // tc,),
                  in_specs=[pl.BlockSpec((nslot, shard, tc), lambda i: (0, 0, i))],
                  out_specs=pl.BlockSpec((D_PAD, tc), lambda i: (0, i)),
                  out_shape=jax.ShapeDtypeStruct((D_PAD, cols), ws.dtype))(ws)


def _rope_tables(s):
    pos = jnp.arange(s, dtype=F32)
    inv_freq = ROPE_THETA ** (-jnp.arange(0, 64, 2, dtype=F32) / 64)
    ang = pos[:, None] * inv_freq[None, :]
    cos, sin = jnp.cos(ang), jnp.sin(ang)
    z = jnp.zeros((s, 64), F32)
    return jnp.concatenate([cos, cos, z], axis=1), jnp.concatenate([-sin, sin, z], axis=1)


def _overlap_table(s):
    n_c, n_s = s // CMP_STRIDE, s // SLC_LEN
    c0 = np.arange(n_c)[:, None] * CMP_STRIDE
    s0 = np.arange(LANE)[None, :] * SLC_LEN
    ov = (c0 < s0 + SLC_LEN) & (c0 + CMP_LEN > s0) & (np.arange(n_c)[:, None] < n_c - 1) & (np.arange(LANE)[None, :] < n_s)
    return jnp.asarray(ov.astype(np.float32), dtype=BF16)


def _shift_down(a):
    return jnp.concatenate([jnp.zeros((8, a.shape[1]), a.dtype), a], axis=0)[7:7 + a.shape[0]]


def _shift_up(a):
    return jnp.concatenate([a, jnp.zeros((8, a.shape[1]), a.dtype)], axis=0)[1:1 + a.shape[0]]


def _local_step(x, mem, target, w, hooks=None):
    s = x.shape[0]
    cs, sn = _rope_tables(s)
    t_ = jnp.transpose

    w_in_p = _w_in_from_slots(w["w_in_t"])
    xn, rstd_x = _rms_fwd(_Src(x, D_MODEL), w["norm_g"], "norm_x")
    if hooks is None:
        hp, hpb = _mm(xn, w_in_p, "in_proj", mode="nt", second_dtype=BF16)
    else:
        hp, hpb, *gathered = _mm(xn, w_in_p, "in_proj", mode="nt", second_dtype=BF16, side=hooks.gather_side)
        w = {**w, **hooks.weights(gathered)}

    w_uq3 = w["w_uq"].reshape(512, MLA_HEADS, 192)
    w_uq_p = jnp.concatenate([w_uq3, w_uq3[:, :, 128:]], axis=2).reshape(512, MLA_HEADS * 256)
    w_ukv_p = t_(w["w_ukv"].reshape(512, MLA_HEADS, 2, 128), (0, 2, 1, 3)).reshape(512, 2048)
    c_q, c_kv = _Src(hp, 512, 0), _Src(hp, 512, 1)
    cqn, rstd_q = _rms_fwd(c_q, w["q_norm_g"], "norm_q")
    ckvn, rstd_kv = _rms_fwd(c_kv, w["kv_norm_g"], "norm_kv")
    q_lin = _mm(cqn, w_uq_p, "mla_q_proj")
    kvb = _mm(ckvn, w_ukv_p, "mla_kv_proj", out_dtype=BF16)
    q_mla = _rope_fwd(_Src(q_lin, MLA_HEADS * 256), cs, sn, MLA_HEADS, 256, LANE, "rope_q")
    k_pe = _rope_fwd(_Src(hp, LANE, PAD["k_rope"] // LANE), cs, sn, 1, LANE, 0, "rope_k")
    mla = _Attn("mla", s, s, MLA_HEADS, 256)
    mla_q, mla_v = _Src(q_mla, 256), _Src(kvb, LANE, MLA_HEADS)
    mla_k = [_Src(kvb, LANE), _Src(k_pe, LANE, 0, False)]
    o_mla, l_mla, lr_mla = _attn_fwd(mla, mla_q, mla_k, mla_v, None, "mla_fwd")

    sk = s // CMP_STRIDE
    pe_k, pe_v = w["cmp_pe_k"], w["cmp_pe_v"]
    w1k = _pad_cols(w["cmp_w1k"], 256)
    w2k = jnp.pad(w["cmp_w2k"], ((0, 64), (0, 64))).astype(BF16)
    w1v, w2v = w["cmp_w1v"], w["cmp_w2v"].astype(BF16)
    half_k, half_v = CMP_STRIDE * NSA_DK, CMP_STRIDE * HEAD_V
    ak = hp[:, PAD["k_c"]:PAD["k_c"] + NSA_DK].reshape(sk, half_k)
    av = hp[:, PAD["v_c"]:PAD["v_c"] + HEAD_V].reshape(sk, half_v)
    ck_args = (ak, _shift_up(ak), pe_k[:CMP_STRIDE].reshape(1, half_k), pe_k[CMP_STRIDE:].reshape(1, half_k),
               w1k[:half_k], w1k[half_k:], w2k)
    cv_args = (av, _shift_up(av), pe_v[:CMP_STRIDE].reshape(1, half_v), pe_v[CMP_STRIDE:].reshape(1, half_v),
               w1v[:half_v], w1v[half_v:], w2v)
    k_cmp, pre_k = _compress_fwd(*ck_args, "compress_k")
    v_cmp, pre_v = _compress_fwd(*cv_args, "compress_v")
    cmp_ = _Attn("cmp", s, sk, NSA_HEADS, 256)
    slc = _Attn("slc", s, s, NSA_HEADS, 256)
    win = _Attn("win", s, s, NSA_HEADS, 256)
    nsa_q = _Src(hpb, 256, PAD["q_nsa"] // 256)
    cmp_k, cmp_v = [_Src(k_cmp, 256, 0, False)], _Src(v_cmp, HEAD_V, 0, False)
    slc_k, slc_v = [_Src(hpb, 256, PAD["k_s"] // 256, False)], _Src(hpb, HEAD_V, PAD["v_s"] // HEAD_V, False)
    win_k, win_v = [_Src(hpb, 256, PAD["k_w"] // 256, False)], _Src(hpb, HEAD_V, PAD["v_w"] // HEAD_V, False)
    o_cmp, l_cmp, lr_cmp, sel, selt = _attn_fwd_small(cmp_, nsa_q, cmp_k, cmp_v, "cmp_fwd", _overlap_table(s))
    o_slc, l_slc, lr_slc = _attn_fwd(slc, nsa_q, slc_k, slc_v, sel, "slc_fwd")
    o_win, l_win, lr_win = _attn_fwd_small(win, nsa_q, win_k, win_v, "win_fwd")
    gl = _Src(hp, LANE, PAD["g_nsa"] // LANE)
    o_nsa = _nsa_combine(o_cmp, o_slc, o_win, gl)

    mn, rstd_m = _rms_fwd(_Src(mem, D_MODEL), w["mem_norm_g"], "norm_mem")
    kvm = _mm(mn, w["w_mem_kv"], "mem_kv_proj", out_dtype=BF16)
    mem_ = _Attn("mem", s, mem.shape[0], MEM_HEADS, LANE)
    mem_q, mem_k, mem_v = _Src(hpb, LANE, PAD["q_mem"] // LANE), [_Src(kvm, LANE)], _Src(kvm, LANE, MEM_HEADS)
    o_mem, l_mem, lr_mem = _attn_fwd_small(mem_, mem_q, mem_k, mem_v, "mem_fwd")

    u = _gate_fwd(o_mla, o_nsa, o_mem, hp)
    proj = _mm(u, w["w_out"], "out_proj")
    dy, g_final, loss = _final_loss(x, proj, w["final_norm_g"].reshape(1, -1), target)

    g_w_out = _mm(u, dy, "out_proj_dw", mode="tn")
    du = _mm(dy, w["w_out"], "out_proj_dx", mode="nt")
    do_cat, dz = _gate_bwd(du, o_mla, o_nsa, o_mem, hp)

    dq_mem, (dk_mem,), dv_mem = _attn_bwd(mem_, mem_q, mem_k, mem_v, None, None, _Src(o_mem, HEAD_V), l_mem,
                                          lr_mem, _Src(do_cat, HEAD_V, 12), None, "mem_bwd")
    dkvm = jnp.concatenate([dk_mem, dv_mem], axis=1)
    g_w_mem_kv = _mm(mn, dkvm, "mem_kv_dw", mode="tn")
    dmn = _mm(dkvm, w["w_mem_kv"], "mem_kv_dx", mode="nt")
    _, g_mem_norm = _rms_bwd(_Src(mem, D_MODEL), w["mem_norm_g"], rstd_m, dmn, None, "norm_mem_bwd")

    do_cmp, do_slc, do_win, dgl = _nsa_combine_bwd(do_cat, o_cmp, o_slc, o_win, gl)
    dq_n, (dk_cmp,), dv_cmp = _attn_bwd(cmp_, nsa_q, cmp_k, cmp_v, None, None, _Src(o_cmp, HEAD_V), l_cmp,
                                        lr_cmp, _Src(do_cmp, HEAD_V), None, "cmp_bwd")
    dq_n, (dk_s,), dv_s = _attn_bwd(slc, nsa_q, slc_k, slc_v, sel, selt, _Src(o_slc, HEAD_V), l_slc, lr_slc,
                                    _Src(do_slc, HEAD_V), dq_n, "slc_bwd")
    dq_n, (dk_w,), dv_w = _attn_bwd(win, nsa_q, win_k, win_v, None, None, _Src(o_win, HEAD_V), l_win, lr_win,
                                    _Src(do_win, HEAD_V), dq_n, "win_bwd")
    dak, dpk_lo, dpk_hi, dw1k_lo, dw1k_hi, g_w2k = _compress_bwd(
        *ck_args, pre_k, _shift_down(pre_k), dk_cmp, _shift_down(dk_cmp), "compress_k_bwd")
    dav, dpv_lo, dpv_hi, dw1v_lo, dw1v_hi, g_w2v = _compress_bwd(
        *cv_args, pre_v, _shift_down(pre_v), dv_cmp, _shift_down(dv_cmp), "compress_v_bwd")
    g_pe_k = jnp.concatenate([dpk_lo.reshape(CMP_STRIDE, NSA_DK), dpk_hi.reshape(CMP_STRIDE, NSA_DK)], axis=0)
    g_pe_v = jnp.concatenate([dpv_lo.reshape(CMP_STRIDE, HEAD_V), dpv_hi.reshape(CMP_STRIDE, HEAD_V)], axis=0)
    g_w1k = jnp.concatenate([dw1k_lo, dw1k_hi], axis=0)[:, :NSA_DK]
    g_w1v = jnp.concatenate([dw1v_lo, dw1v_hi], axis=0)
    dk_c = _pad_cols(dak.reshape(s, NSA_DK), 256)
    dv_c = dav.reshape(s, HEAD_V)

    dq_m, (dk_nope, dk_pe), dv_m = _attn_bwd(mla, mla_q, mla_k, mla_v, None, None, _Src(o_mla, HEAD_V), l_mla,
                                             lr_mla, _Src(do_cat, HEAD_V), None, "mla_bwd")
    dq_lin = _rope_bwd_q(dq_m, cs, sn)
    dkv_lin, d_krope = _rope_bwd_k(dk_nope, dk_pe, dv_m, cs, sn)
    g_w_uq_p = _mm(cqn, dq_lin, "mla_q_dw", mode="tn")
    dcqn = _mm(dq_lin, w_uq_p, "mla_q_dx", mode="nt")
    g_w_ukv_p = _mm(ckvn, dkv_lin, "mla_kv_dw", mode="tn")
    dckvn = _mm(dkv_lin, w_ukv_p, "mla_kv_dx", mode="nt")
    dc_q, g_q_norm = _rms_bwd(c_q, w["q_norm_g"], rstd_q, dcqn, None, "norm_q_bwd")
    dc_kv, g_kv_norm = _rms_bwd(c_kv, w["kv_norm_g"], rstd_kv, dckvn, None, "norm_kv_bwd")
    g_w_uq = g_w_uq_p.reshape(512, MLA_HEADS, 256)[:, :, :192].reshape(512, MLA_HEADS * 192)
    g_w_ukv = t_(g_w_ukv_p.reshape(512, 2, MLA_HEADS, 128), (0, 2, 1, 3)).reshape(512, 2048)

    dhp = jnp.concatenate(
        [dc_q, dc_kv, dq_n, dk_c, dk_s, dk_w, d_krope, dv_c, dv_s, dv_w, dgl,
         jnp.zeros((s, PAD["q_mem"] - (PAD["g_nsa"] + LANE)), F32), dq_mem, dz], axis=1).astype(BF16)
    grads = dict(q_norm_g=g_q_norm, w_uq=g_w_uq, kv_norm_g=g_kv_norm,
                 w_ukv=g_w_ukv, cmp_pe_k=g_pe_k, cmp_pe_v=g_pe_v, cmp_w1k=g_w1k, cmp_w2k=g_w2k[:NSA_DK, :NSA_DK],
                 cmp_w1v=g_w1v, cmp_w2v=g_w2v, mem_norm_g=g_mem_norm, w_mem_kv=g_w_mem_kv, w_out=g_w_out,
                 final_norm_g=g_final.reshape(-1))
    if hooks is None:
        g_w_in_t = _w_in_grad_slots(_mm(dhp, xn, "in_proj_dw", mode="tn", wide=2048))
        dxn = _mm(dhp, w_in_p, "in_proj_dx", wide=2048)
    else:
        g_w_in_p, *hooks.received = _mm(dhp, xn, "in_proj_dw", mode="tn", wide=2048, side=hooks.reduce_side(grads))
        g_w_in_t = _w_in_grad_slots(g_w_in_p)
        dxn, hooks.received_w_in = _mm(dhp, w_in_p, "in_proj_dx", wide=2048, side=hooks.reduce_side_w_in(g_w_in_t))
    grad_x, g_norm = _rms_bwd(_Src(x, D_MODEL), w["norm_g"], rstd_x, dxn, dy, "norm_x_bwd")
    grads.update(norm_g=g_norm, w_in_t=g_w_in_t)
    return loss[0, 0], grad_x, grads


def kernel(x, mem, norm_g, w_in, q_norm_g, w_uq, kv_norm_g, w_ukv, cmp_pe_k, cmp_pe_v, cmp_w1k, cmp_w2k, cmp_w1v, cmp_w2v, mem_norm_g, w_mem_kv, w_out, final_norm_g, loss_target, m_norm_g, m_w_in, m_q_norm_g, m_w_uq, m_kv_norm_g, m_w_ukv, m_cmp_pe_k, m_cmp_pe_v, m_cmp_w1k, m_cmp_w2k, m_cmp_w1v, m_cmp_w2v, m_mem_norm_g, m_w_mem_kv, m_w_out, m_final_norm_g, v_norm_g, v_w_in, v_q_norm_g, v_w_uq, v_kv_norm_g, v_w_ukv, v_cmp_pe_k, v_cmp_pe_v, v_cmp_w1k, v_cmp_w2k, v_cmp_w1v, v_cmp_w2v, v_mem_norm_g, v_w_mem_kv, v_w_out, v_final_norm_g):
    args = dict(locals())
    wts = {n: args[n] for n in WEIGHTS}
    loc = {n: (a if n == "final_norm_g" else a[0]) for n, a in wts.items()}

    def to_x(n, a):
        return a.T if n == "w_in" else a

    split = [1 if n == "w_in" else 0 for n in SHARDED]
    rest = [n for n in SHARDED if n != "w_in"]
    chip = 2 * lax.axis_index("x") + lax.axis_index("y")
    core = lax.axis_index("c").astype(jnp.int32).reshape(1)
    own = {n: to_x(n, loc[n]).astype(BF16) for n in SHARDED}

    def with_own_slot(gw, a):
        return lax.dynamic_update_slice(gw, a[None], (chip, 0, 0))

    def slots(n, a):
        if n == "w_in":
            return a
        if SHARD_AXIS[n] == 0:
            return a.reshape(4, a.shape[0] // 4, a.shape[1])
        width = a.shape[1] // 4
        return jnp.stack([a[:, j * width:(j + 1) * width] for j in range(4)])

    def pair_sums(names, grads, name):
        axes = [1 if n == "w_in" else 0 for n in names]
        gs = [slots(n, a) for n, a in zip(names, grads)]
        theirs = _pair_exchange(gs, axes, name)
        return [_pair_sum(a, b, core, ax, "pair_sum_" + n) for n, a, b, ax in zip(names, gs, theirs, axes)]

    class Hooks:
        gather_side = _gather_side([own[n] for n in rest], [0] * len(rest))
        received = None

        @staticmethod
        def weights(gathered):
            out = {}
            for n, gw in zip(rest, gathered):
                gw = with_own_slot(gw, own[n])
                if SHARD_AXIS[n] == 0:
                    out[n] = gw.reshape(4 * gw.shape[1], gw.shape[2])
                else:
                    out[n] = jnp.concatenate([gw[j] for j in range(4)], axis=1)
            return out

        @staticmethod
        def reduce_side(grads):
            return _chip_side(pair_sums(rest, [grads[n] for n in rest], "pair_exchange_rest"))

        @staticmethod
        def reduce_side_w_in(g_w_in_t):
            return _chip_side(pair_sums(["w_in"], [g_w_in_t], "pair_exchange_w_in"))

    hooks = Hooks()

    start = {n: loc[n].reshape(1, -1) if loc[n].ndim == 1 else loc[n] for n in REPLICATED}
    start["w_in_t"] = with_own_slot(_gather_shards([own["w_in"]], [1])[0], own["w_in"])
    loss, grad_x, g = _local_step(x[0], mem[0], loss_target[0], start, hooks)
    loss = lax.psum(loss, ("x", "y", "c"))

    from_chips = dict(zip(rest, hooks.received), w_in=hooks.received_w_in)
    sums = [_chip_sum(from_chips[n], core, ax, "chip_sum_" + n) for n, ax in zip(SHARDED, split)]
    g_sh = _half_exchange([a for a, _ in sums], [b for _, b in sums], split)

    n_rep = sum(int(np.prod(loc[n].shape)) for n in REPLICATED)
    rows_rep = -(-n_rep // (8 * LANE)) * 8

    def rep_pack(parts):
        flat = jnp.concatenate([p.reshape(-1) for p in parts])
        return jnp.pad(flat, (0, rows_rep * LANE - n_rep)).reshape(rows_rep, LANE)

    g_rep = _sum_slots(_gather_all(rep_pack([g[n] for n in REPLICATED])), "replica_sum")
    d_rp, m_rp, v_rp = _adamw(rep_pack([wts[n] for n in REPLICATED]), g_rep,
                              rep_pack([args["m_" + n] for n in REPLICATED]),
                              rep_pack([args["v_" + n] for n in REPLICATED]), "adamw_replicated")

    def rep_unpack(buf):
        flat, out, o = buf.reshape(-1), {}, 0
        for n in REPLICATED:
            size = int(np.prod(wts[n].shape))
            out[n] = flat[o:o + size].reshape(wts[n].shape)
            o += size
        return out

    outs = {k: rep_unpack(b) for k, b in (("g", g_rep), ("d", d_rp), ("m", m_rp), ("v", v_rp))}
    for n, gn in zip(SHARDED, g_sh):
        d, mo, vo = _adamw(to_x(n, loc[n]), gn, to_x(n, args["m_" + n][0]), to_x(n, args["v_" + n][0]),
                           "adamw_" + n)
        for k, a in (("g", gn), ("d", d), ("m", mo), ("v", vo)):
            outs[k][n] = to_x(n, a).reshape(wts[n].shape)

    return (loss, grad_x[None], *[outs["g"][n] for n in WEIGHTS], *[outs["d"][n] for n in WEIGHTS],
            *[outs["m"][n] for n in WEIGHTS], *[outs["v"][n] for n in WEIGHTS])
```

```python
from typing import NamedTuple

import numpy as np
import jax
import jax.numpy as jnp
from jax import lax
from jax.experimental import pallas as pl
from jax.experimental.pallas import tpu as pltpu

F32 = jnp.float32
BF16 = jnp.bfloat16
MESH = pl.DeviceIdType.MESH

D_MODEL = 2048
EPS = 1e-6
LANE = 128
HEAD_V = 128
MLA_HEADS = 8
NSA_HEADS = 4
MEM_HEADS = 4
NSA_DK = 192
CMP_STRIDE = 16
CMP_LEN = 32
SLC_LEN = 64
SLC_TOPN = 16
WIN = 512
NEG = -1e30
LOG2E = 1.4426950408889634
ROPE_THETA = 10000.0
BLOCK_BYTES = 2 << 20

ORIG = dict(c_q=(0, 512), c_kv=(512, 512), k_rope=(1024, 64), z_mla=(1088, 1024),
            q_nsa=(2112, 768), k_c=(2880, 192), v_c=(3072, 128), k_s=(3200, 192),
            v_s=(3392, 128), k_w=(3520, 192), v_w=(3712, 128), g_nsa=(3840, 12),
            z_nsa=(3852, 512), q_mem=(4364, 512), z_mem=(4876, 512))
PAD = dict(c_q=0, c_kv=512, q_nsa=1024, k_c=2048, k_s=2304, k_w=2560, k_rope=2816, v_c=2944,
           v_s=3072, v_w=3200, g_nsa=3328, q_mem=3584, z=4096)
D_PAD = 6144

ADAM_LR, ADAM_B1, ADAM_B2, ADAM_EPS, ADAM_WD, ADAM_STEP = 0.001, 0.9, 0.999, 1e-08, 0.01, 10

SHARDED = ("w_in", "w_uq", "w_ukv", "cmp_w1k", "cmp_w1v", "w_mem_kv", "w_out")
SHARD_AXIS = dict(w_in=1, w_uq=1, w_ukv=1, cmp_w1k=0, cmp_w1v=0, w_mem_kv=0, w_out=0)
REPLICATED = ("norm_g", "q_norm_g", "kv_norm_g", "cmp_pe_k", "cmp_pe_v", "cmp_w2k", "cmp_w2v",
              "mem_norm_g", "final_norm_g")
WEIGHTS = ("norm_g", "w_in", "q_norm_g", "w_uq", "kv_norm_g", "w_ukv", "cmp_pe_k", "cmp_pe_v",
           "cmp_w1k", "cmp_w2k", "cmp_w1v", "cmp_w2v", "mem_norm_g", "w_mem_kv", "w_out",
           "final_norm_g")


def _pcall(kernel, **kw):
    return pl.pallas_call(kernel, **kw)


def _tile(n, pref):
    if n <= pref:
        return n
    for t in range(pref, LANE - 1, -LANE):
        if n % t == 0:
            return t
    raise ValueError((n, pref))


def _row_tile(rows, cols, itemsize=4):
    want = max(16, BLOCK_BYTES // (cols * itemsize))
    if rows <= want:
        return rows
    t = 16
    best = rows
    while t <= want:
        if rows % t == 0:
            best = t
        t *= 2
    return best


def _nt(a, b):
    return lax.dot_general(a, b, (((1,), (1,)), ((), ())), preferred_element_type=F32)


def _tn(a, b):
    return lax.dot_general(a, b, (((0,), (0,)), ((), ())), preferred_element_type=F32)


def _nn(a, b):
    return jnp.dot(a, b, preferred_element_type=F32)


def _sigmoid(x):
    return 1.0 / (1.0 + jnp.exp(-x))


class _Src(NamedTuple):
    arr: jax.Array
    width: int
    col0: int = 0
    per_head: bool = True

    def col(self, h):
        return self.col0 + h if self.per_head else self.col0


class _Side(NamedTuple):
    inputs: list
    out_shape: list
    scratch: list
    phase: object


def _mm(a, b, name, mode="nn", out_dtype=F32, second_dtype=None, wide=1024, side=None):
    if mode == "tn":
        k, m = a.shape
    else:
        m, k = a.shape
    if mode == "nt":
        n, k2 = b.shape
    else:
        k2, n = b.shape
    assert k == k2, (a.shape, b.shape, mode)
    tm, tn, tk = _tile(m, 1024), _tile(n, wide), _tile(k, 2048)
    grid = (m // tm, n // tn, k // tk)
    nk = grid[2]
    assert nk == 1 or (out_dtype == F32 and second_dtype is None)
    dot = {"nn": _nn, "nt": _nt, "tn": _tn}[mode]
    n_in = len(side.inputs) if side else 0
    n_out = len(side.out_shape) if side else 0
    n_res = 1 + (second_dtype is not None)

    def kern(*refs):
        a_ref, b_ref = refs[:2]
        res = refs[2 + n_in:2 + n_in + n_res]
        step = [pl.program_id(d) for d in range(3)]
        if side:
            side_refs = (refs[2:2 + n_in], refs[2 + n_in + n_res:2 + n_in + n_res + n_out],
                         refs[2 + n_in + n_res + n_out:])

            @pl.when((step[0] == 0) & (step[1] == 0) & (step[2] == 0))
            def _():
                side.phase("start", *side_refs)

        r = dot(a_ref[...].astype(BF16), b_ref[...].astype(BF16))
        if nk == 1:
            res[0][...] = r.astype(out_dtype)
            if n_res == 2:
                res[1][...] = r.astype(second_dtype)
        else:
            @pl.when(step[2] == 0)
            def _():
                res[0][...] = r

            @pl.when(step[2] > 0)
            def _():
                res[0][...] += r

        if side:
            @pl.when((step[0] == grid[0] - 1) & (step[1] == grid[1] - 1) & (step[2] == nk - 1))
            def _():
                side.phase("finish", *side_refs)

    a_spec = (pl.BlockSpec((tk, tm), lambda i, j, kk: (kk, i)) if mode == "tn"
              else pl.BlockSpec((tm, tk), lambda i, j, kk: (i, kk)))
    b_spec = (pl.BlockSpec((tn, tk), lambda i, j, kk: (j, kk)) if mode == "nt"
              else pl.BlockSpec((tk, tn), lambda i, j, kk: (kk, j)))
    o_spec = pl.BlockSpec((tm, tn), lambda i, j, kk: (i, j))
    out_specs = [o_spec] * n_res + [ANY] * n_out
    out_shape = [jax.ShapeDtypeStruct((m, n), out_dtype)]
    if second_dtype is not None:
        out_shape.append(jax.ShapeDtypeStruct((m, n), second_dtype))
    out_shape += list(side.out_shape) if side else []
    semantics = ("arbitrary",) * 3 if side else ("parallel", "parallel", "arbitrary")
    out = _pcall(
        kern, name=name, grid=grid, in_specs=[a_spec, b_spec] + [ANY] * n_in, out_specs=out_specs,
        out_shape=out_shape, scratch_shapes=list(side.scratch) if side else [],
        compiler_params=pltpu.CompilerParams(dimension_semantics=semantics),
    )(a, b, *(side.inputs if side else []))
    return out[0] if len(out) == 1 else out


def _rms_fwd(x, g, name):
    r, d = x.arr.shape[0], x.width
    tr = _tile(r, 512)

    def kern(x_ref, g_ref, y_ref, r_ref):
        xv = x_ref[...]
        rstd = lax.rsqrt(jnp.mean(xv * xv, axis=-1, keepdims=True) + EPS)
        y_ref[...] = (xv * rstd * g_ref[...]).astype(BF16)
        r_ref[...] = rstd

    return _pcall(
        kern, name=name, grid=(r // tr,),
        in_specs=[pl.BlockSpec((tr, d), lambda i: (i, x.col0)), pl.BlockSpec((1, d), lambda i: (0, 0))],
        out_specs=[pl.BlockSpec((tr, d), lambda i: (i, 0)), pl.BlockSpec((tr, 1), lambda i: (i, 0))],
        out_shape=[jax.ShapeDtypeStruct((r, d), BF16), jax.ShapeDtypeStruct((r, 1), F32)],
    )(x.arr, g)


def _rms_bwd(x, g, rstd, dy, add, name):
    r, d = x.arr.shape[0], x.width
    tr = _tile(r, 256)
    has_add = add is not None

    def kern(*refs):
        if has_add:
            x_ref, g_ref, r_ref, dy_ref, add_ref, dx_ref, dg_ref = refs
        else:
            x_ref, g_ref, r_ref, dy_ref, dx_ref, dg_ref = refs
        rs = r_ref[...]
        xhat = x_ref[...] * rs
        dyv = dy_ref[...]
        dyg = dyv * g_ref[...]
        c = jnp.mean(dyg * xhat, axis=-1, keepdims=True)
        dx = rs * (dyg - xhat * c)
        if has_add:
            dx = dx + add_ref[...]
        dx_ref[...] = dx
        part = jnp.sum(dyv * xhat, axis=0, keepdims=True)

        @pl.when(pl.program_id(0) == 0)
        def _():
            dg_ref[...] = part

        @pl.when(pl.program_id(0) > 0)
        def _():
            dg_ref[...] += part

    row = pl.BlockSpec((tr, d), lambda i: (i, 0))
    vec = pl.BlockSpec((1, d), lambda i: (0, 0))
    ins = [pl.BlockSpec((tr, d), lambda i: (i, x.col0)), vec, pl.BlockSpec((tr, 1), lambda i: (i, 0)), row]
    ins += [row] if has_add else []
    args = (x.arr, g, rstd, dy) + ((add,) if has_add else ())
    return _pcall(
        kern, name=name, grid=(r // tr,), in_specs=ins, out_specs=[row, vec],
        out_shape=[jax.ShapeDtypeStruct((r, d), F32), jax.ShapeDtypeStruct((1, d), F32)],
        compiler_params=pltpu.CompilerParams(dimension_semantics=("arbitrary",)),
    )(*args)


def _final_loss(x, proj, g, target):
    r, d = x.shape
    tr = _tile(r, 256)

    def kern(x_ref, p_ref, g_ref, t_ref, dy_ref, dg_ref, loss_ref):
        y = x_ref[...] + p_ref[...]
        rs = lax.rsqrt(jnp.mean(y * y, axis=-1, keepdims=True) + EPS)
        yhat = y * rs
        gv = g_ref[...]
        e = yhat * gv - t_ref[...]
        lpart = 0.5 * jnp.sum(jnp.mean(e * e, axis=-1, keepdims=True), axis=0, keepdims=True)
        dout = e * (1.0 / d)
        dyg = dout * gv
        c = jnp.mean(dyg * yhat, axis=-1, keepdims=True)
        dy_ref[...] = rs * (dyg - yhat * c)
        gpart = jnp.sum(dout * yhat, axis=0, keepdims=True)
        lrow = jnp.broadcast_to(lpart, (1, LANE))

        @pl.when(pl.program_id(0) == 0)
        def _():
            dg_ref[...] = gpart
            loss_ref[...] = lrow

        @pl.when(pl.program_id(0) > 0)
        def _():
            dg_ref[...] += gpart
            loss_ref[...] += lrow

    row = pl.BlockSpec((tr, d), lambda i: (i, 0))
    vec = pl.BlockSpec((1, d), lambda i: (0, 0))
    return _pcall(
        kern, name="final_loss", grid=(r // tr,), in_specs=[row, row, vec, row],
        out_specs=[row, vec, pl.BlockSpec((1, LANE), lambda i: (0, 0))],
        out_shape=[jax.ShapeDtypeStruct((r, d), F32), jax.ShapeDtypeStruct((1, d), F32),
                   jax.ShapeDtypeStruct((1, LANE), F32)],
        compiler_params=pltpu.CompilerParams(dimension_semantics=("arbitrary",)),
    )(x, proj, g, target)


def _rope_fwd(x, cs, sn, nh, width, off, name):
    s = x.arr.shape[0]
    tr = _tile(s, 512)

    def kern(x_ref, c_ref, s_ref, o_ref):
        cv, sv = c_ref[...], s_ref[...]
        for h in range(nh):
            b = h * width
            if off:
                o_ref[:, b:b + off] = x_ref[:, b:b + off].astype(BF16)
            xr = x_ref[:, b + off:b + off + LANE]
            o_ref[:, b + off:b + off + LANE] = (xr * cv + pltpu.roll(xr, 32, 1) * sv).astype(BF16)

    tab = pl.BlockSpec((tr, LANE), lambda i: (i, 0))
    return _pcall(
        kern, name=name, grid=(s // tr,),
        in_specs=[pl.BlockSpec((tr, nh * width), lambda i: (i, x.col0)), tab, tab],
        out_specs=pl.BlockSpec((tr, nh * width), lambda i: (i, 0)),
        out_shape=jax.ShapeDtypeStruct((s, nh * width), BF16),
    )(x.arr, cs, sn)


def _rope_grad(d, cv, sv):
    g2 = d * sv
    g2 = g2 + pltpu.roll(g2, 64, 1)
    lane = lax.broadcasted_iota(jnp.int32, d.shape, 1)
    return jnp.where(lane < 64, d * cv + pltpu.roll(g2, 32, 1), 0.0)


def _rope_bwd_q(dq, cs, sn):
    s, w = dq.shape
    tr = _tile(s, 512)
    nh = w // 256

    def kern(d_ref, c_ref, s_ref, o_ref):
        cv, sv = c_ref[...], s_ref[...]
        for h in range(nh):
            b = h * 256
            o_ref[:, b:b + LANE] = d_ref[:, b:b + LANE]
            o_ref[:, b + LANE:b + 256] = _rope_grad(d_ref[:, b + LANE:b + 256], cv, sv)

    row = pl.BlockSpec((tr, w), lambda i: (i, 0))
    tab = pl.BlockSpec((tr, LANE), lambda i: (i, 0))
    return _pcall(kern, name="rope_bwd_q", grid=(s // tr,), in_specs=[row, tab, tab], out_specs=row,
                  out_shape=jax.ShapeDtypeStruct((s, w), F32))(dq, cs, sn)


def _rope_bwd_k(dk_nope, dk_pe, dv, cs, sn):
    s, w = dk_nope.shape
    tr = _tile(s, 512)

    def kern(dk_ref, dp_ref, dv_ref, c_ref, s_ref, okv_ref, okr_ref):
        okv_ref[:, :w] = dk_ref[...]
        okv_ref[:, w:] = dv_ref[...]
        okr_ref[...] = _rope_grad(dp_ref[...], c_ref[...], s_ref[...])

    tab = pl.BlockSpec((tr, LANE), lambda i: (i, 0))
    wide = pl.BlockSpec((tr, w), lambda i: (i, 0))
    return _pcall(
        kern, name="rope_bwd_k", grid=(s // tr,), in_specs=[wide, tab, wide, tab, tab],
        out_specs=[pl.BlockSpec((tr, 2 * w), lambda i: (i, 0)), tab],
        out_shape=[jax.ShapeDtypeStruct((s, 2 * w), F32), jax.ShapeDtypeStruct((s, LANE), F32)],
    )(dk_nope, dk_pe, dv, cs, sn)


class _Attn:
    def __init__(self, mode, s, sk, heads, dk):
        self.mode, self.s, self.sk, self.h, self.dk = mode, s, sk, heads, dk
        self.scale = {"mla": 192 ** -0.5, "mem": 128 ** -0.5}.get(mode, NSA_DK ** -0.5)
        self.tb = min(256, s)
        self.nb = s // self.tb
        self.nsub = 2 if self.nb % 2 == 0 else 1
        self.tq = self.tb * self.nsub
        self.fchains = 1
        self.nq = s // self.tq
        self.qpb = 4 if self.nq % 4 == 0 and mode == "mla" else 2
        self.causal = mode in ("mla", "slc")
        if self.causal:
            self.tk = self.tq
        elif mode == "win":
            self.tk = WIN + self.tb
        else:
            self.tk = sk
        self.tkb = min(512, sk)
        self.ksub = 1
        self.kb = self.tkb // self.ksub
        self.ncmp = s // CMP_STRIDE - 1

    def mask_bias(self, t, n, h, selx, diag):
        m = self.mode
        if m == "mla":
            return (n <= t) if diag else None, None
        if m == "mem":
            return None, None
        slope = jnp.where(h == 0, 0.25, jnp.where(h == 1, 0.0625, jnp.where(h == 2, 0.015625, 0.00390625)))
        slope = slope.astype(F32) * LOG2E
        if m == "cmp":
            mask = (n * CMP_STRIDE + (CMP_LEN - 1) <= t) & (n < self.ncmp)
            pos = n.astype(F32) * float(CMP_STRIDE) + (CMP_LEN - 1) / 2.0
            return mask, slope * pos
        rel = t - n
        if m == "slc":
            return (rel >= 0) if diag else None, slope * n.astype(F32)
        return (rel >= 0) & (rel < WIN), slope * n.astype(F32)


def _scores(cfg, s_raw, t, n, h, selx, diag, lse=None):
    s = s_raw * (cfg.scale * LOG2E)
    mask, key_term = cfg.mask_bias(t, n, h, selx, diag)
    if key_term is not None:
        s = s + key_term
    if selx is not None:
        s = s + selx
    if lse is None:
        if mask is not None:
            s = jnp.where(mask, s, NEG)
        return s, mask
    p = jnp.exp2(jnp.minimum(s - lse, 0.0))
    if mask is not None:
        p = jnp.where(mask, p, 0.0)
    return p, mask


def _block_of_key(k0, tk, keys_on_rows, value=NEG):
    shape = (tk, LANE) if keys_on_rows else (LANE, tk)
    n = lax.broadcasted_iota(jnp.int32, shape, 0 if keys_on_rows else 1) + k0
    j = lax.broadcasted_iota(jnp.int32, shape, 1 if keys_on_rows else 0)
    return jnp.where((n >> 6) == j, value, 0.0).astype(BF16)


def _to_row(col):
    t = col.shape[0]
    return jnp.transpose(jnp.broadcast_to(col, (t, LANE)))[0:1, :]


def _load_keys(k_refs, rows):
    parts = [r[rows, :].astype(BF16) for r in k_refs]
    return parts[0] if len(parts) == 1 else jnp.concatenate(parts, axis=1)


def _attn_fwd(cfg, q, ks, v, sel, name):
    s, tq, tk, nsub = cfg.s, cfg.tq, cfg.tk, cfg.fchains
    tb = tq // nsub
    per = tb // cfg.tb
    has_sel = sel is not None
    nkp = len(ks)
    qpb = cfg.qpb
    assert cfg.causal and tq == tk and qpb % 2 == 0 and cfg.nq % qpb == 0

    def kern(*refs):
        q_ref, k_refs, v_ref = refs[0], refs[1:1 + nkp], refs[1 + nkp]
        sel_ref = refs[2 + nkp] if has_sel else None
        o_ref, lc_ref, lr_ref = refs[2 + nkp + has_sel:5 + nkp + has_sel]
        buf_a, buf_b = refs[-2:]
        h, g = pl.program_id(0), pl.program_id(1)

        def block(b):
            rows = [slice(b * tq + r * tb, b * tq + (r + 1) * tb) for r in range(nsub)]
            qs = [q_ref[p, :].astype(BF16) for p in rows]
            ts = [(qpb * g + b) * tq + r * tb + lax.broadcasted_iota(jnp.int32, (tb, 1), 0) for r in range(nsub)]
            sels = [sel_ref[p, :].astype(BF16) for p in rows] if has_sel else None
            return rows, qs, ts, sels

        def scores_into(buf, blk, c):
            kk = _load_keys(k_refs, pl.ds(pl.multiple_of(c * tk, tk), tk))
            for r in range(nsub):
                buf[r] = _nt(blk[1][r], kk)

        def consume(buf, blk, c, carry, diag):
            _, _, ts, sels = blk
            k0 = pl.multiple_of(c * tk, tk)
            vv = v_ref[pl.ds(k0, tk), :].astype(BF16)
            emat = _block_of_key(k0, tk, False) if has_sel else None
            n = k0 + lax.broadcasted_iota(jnp.int32, (1, tk), 1)
            new = []
            for r in range(nsub):
                m, l, acc = carry[r]
                selx = _nn(sels[r], emat) if has_sel else None
                sc, mask = _scores(cfg, buf[r], ts[r], n, h, selx, diag)
                m_new = jnp.maximum(m, jnp.max(sc, axis=1, keepdims=True))
                alpha = jnp.exp2(m - m_new)
                p = jnp.exp2(sc - m_new)
                if mask is not None:
                    p = jnp.where(mask, p, 0.0)
                l = alpha * l + jnp.sum(p, axis=1, keepdims=True)
                new.append((m_new, l, alpha * acc + _nn(p.astype(BF16), vv)))
            return tuple(new)

        def finish(blk, b, carry):
            for r, (m, l, acc) in enumerate(carry):
                o_ref[blk[0][r], :] = acc / (l + 1e-20)
                lse = m + jnp.log(l + 1e-20) * LOG2E
                lc_ref[0, blk[0][r], :] = lse
                for u in range(per):
                    lr_ref[0, (b * nsub + r) * per + u] = _to_row(lse[u * cfg.tb:(u + 1) * cfg.tb])

        def pairs(blk, first, other):
            def pair(p, cr):
                scores_into(other, blk, 2 * p + 1)
                cr = consume(first, blk, 2 * p, cr, False)
                scores_into(first, blk, 2 * p + 2)
                return consume(other, blk, 2 * p + 1, cr, False)
            return pair

        init = ((jnp.full((tb, 1), NEG, F32), jnp.zeros((tb, 1), F32), jnp.zeros((tb, HEAD_V), F32)),) * nsub
        cur, oth = buf_a, buf_b
        blk = block(0)
        scores_into(cur, blk, 0)
        for b in range(qpb):
            full = qpb * g + b
            carry = lax.fori_loop(0, (qpb // 2) * g + b // 2, pairs(blk, cur, oth), init)
            nxt = block(b + 1) if b + 1 < qpb else None
            if b % 2 == 0:
                if nxt:
                    scores_into(oth, nxt, 0)
                finish(blk, b, consume(cur, blk, full, carry, True))
                cur, oth = oth, cur
            else:
                scores_into(oth, blk, full)
                carry = consume(cur, blk, full - 1, carry, False)
                if nxt:
                    scores_into(cur, nxt, 0)
                finish(blk, b, consume(oth, blk, full, carry, True))
            blk = nxt

    rows_step = qpb * tq
    ins = [pl.BlockSpec((rows_step, q.width), lambda h, g: (g, q.col(h)))]
    ins += [pl.BlockSpec((cfg.sk, p.width), lambda h, g, p=p: (0, p.col(h))) for p in ks]
    ins += [pl.BlockSpec((cfg.sk, HEAD_V), lambda h, g: (0, v.col(h)))]
    args = [q.arr] + [p.arr for p in ks] + [v.arr]
    if has_sel:
        ins.append(pl.BlockSpec((rows_step, LANE), lambda h, g: (g, 0)))
        args.append(sel)
    return _pcall(
        kern, name=name, grid=(cfg.h, cfg.nq // qpb), in_specs=ins,
        out_specs=[pl.BlockSpec((rows_step, HEAD_V), lambda h, g: (g, h)),
                   pl.BlockSpec((1, rows_step, 1), lambda h, g: (h, g, 0)),
                   pl.BlockSpec((1, rows_step // cfg.tb, 1, cfg.tb), lambda h, g: (h, g, 0, 0))],
        out_shape=[jax.ShapeDtypeStruct((s, cfg.h * HEAD_V), F32),
                   jax.ShapeDtypeStruct((cfg.h, s, 1), F32),
                   jax.ShapeDtypeStruct((cfg.h, cfg.nb, 1, cfg.tb), F32)],
        scratch_shapes=[pltpu.VMEM((nsub, tb, tk), F32)] * 2,
        compiler_params=pltpu.CompilerParams(dimension_semantics=("parallel", "parallel")),
    )(*args)


def _attn_dq(cfg, q, ks, v, sel, o, lse, do, dq_in, name):
    s, tq, tk, dk, nsub = cfg.s, cfg.tq, cfg.tk, cfg.dk, cfg.fchains
    tb = tq // nsub
    per = tb // cfg.tb
    has_sel = sel is not None
    has_in = dq_in is not None
    nkp = len(ks)

    def kern(*refs):
        refs = list(refs)
        q_ref, k_refs, v_ref = refs[0], refs[1:1 + nkp], refs[1 + nkp]
        p0 = 2 + nkp
        sel_ref = refs[p0] if has_sel else None
        p0 += has_sel
        o_ref, l_ref, do_ref = refs[p0:p0 + 3]
        p0 += 3
        in_ref = refs[p0] if has_in else None
        p0 += has_in
        dq_ref, dr_ref = refs[p0:p0 + 2]
        sa, pa, sb, pb = refs[-4:]
        h, g = pl.program_id(0), pl.program_id(1)

        def block(b):
            rows = [slice(b * tq + r * tb, b * tq + (r + 1) * tb) for r in range(nsub)]
            qs = [q_ref[p, :].astype(BF16) for p in rows]
            ts = [(qpb * g + b) * tq + r * tb + lax.broadcasted_iota(jnp.int32, (tb, 1), 0) for r in range(nsub)]
            sels = [sel_ref[p, :].astype(BF16) for p in rows] if has_sel else None
            dvecs, dobs, lses = [], [], []
            for r, p in enumerate(rows):
                dov = do_ref[p, :]
                dvec = jnp.sum(dov * o_ref[p, :], axis=1, keepdims=True)
                for u in range(per):
                    dr_ref[0, (b * nsub + r) * per + u] = _to_row(dvec[u * cfg.tb:(u + 1) * cfg.tb])
                dvecs.append(dvec)
                dobs.append(dov.astype(BF16))
                lses.append(l_ref[0, p, :])
            return rows, qs, ts, sels, dvecs, dobs, lses

        def products_into(sbuf, pbuf, blk, c):
            rows = pl.ds(pl.multiple_of(c * tk, tk), tk)
            kk, vv = _load_keys(k_refs, rows), v_ref[rows, :].astype(BF16)
            for r in range(nsub):
                sbuf[r] = _nt(blk[1][r], kk)
                pbuf[r] = _nt(blk[5][r], vv)

        def consume(sbuf, pbuf, blk, c, accs, diag):
            _, _, ts, sels, dvecs, _, lses = blk
            k0 = pl.multiple_of(c * tk, tk)
            kk = _load_keys(k_refs, pl.ds(k0, tk))
            emat = _block_of_key(k0, tk, False) if has_sel else None
            n = k0 + lax.broadcasted_iota(jnp.int32, (1, tk), 1)
            new = []
            for r in range(nsub):
                selx = _nn(sels[r], emat) if has_sel else None
                p, _ = _scores(cfg, sbuf[r], ts[r], n, h, selx, diag, lses[r])
                ds = p * (pbuf[r] - dvecs[r])
                new.append(accs[r] + _nn(ds.astype(BF16), kk))
            return tuple(new)

        def finish(blk, accs):
            for r, p in enumerate(blk[0]):
                dq_ref[p, :] = accs[r] * cfg.scale + in_ref[p, :] if has_in else accs[r] * cfg.scale

        def pairs(blk, first, other):
            def pair(p, ac):
                products_into(*other, blk, 2 * p + 1)
                ac = consume(*first, blk, 2 * p, ac, False)
                products_into(*first, blk, 2 * p + 2)
                return consume(*other, blk, 2 * p + 1, ac, False)
            return pair

        zero = (jnp.zeros((tb, dk), F32),) * nsub
        cur, oth = (sa, pa), (sb, pb)
        blk = block(0)
        products_into(*cur, blk, 0)
        for b in range(qpb):
            full = qpb * g + b
            accs = lax.fori_loop(0, (qpb // 2) * g + b // 2, pairs(blk, cur, oth), zero)
            nxt = block(b + 1) if b + 1 < qpb else None
            if b % 2 == 0:
                if nxt:
                    products_into(*oth, nxt, 0)
                finish(blk, consume(*cur, blk, full, accs, True))
                cur, oth = oth, cur
            else:
                products_into(*oth, blk, full)
                accs = consume(*cur, blk, full - 1, accs, False)
                if nxt:
                    products_into(*cur, nxt, 0)
                finish(blk, consume(*oth, blk, full, accs, True))
            blk = nxt

    qpb = cfg.qpb
    assert cfg.causal and tq == tk and qpb % 2 == 0 and cfg.nq % qpb == 0
    rows_step = qpb * tq
    qs = pl.BlockSpec((rows_step, dk), lambda h, g: (g, h))
    ins = [pl.BlockSpec((rows_step, q.width), lambda h, g: (g, q.col(h)))]
    ins += [pl.BlockSpec((cfg.sk, p.width), lambda h, g, p=p: (0, p.col(h))) for p in ks]
    ins += [pl.BlockSpec((cfg.sk, HEAD_V), lambda h, g: (0, v.col(h)))]
    args = [q.arr] + [p.arr for p in ks] + [v.arr]
    if has_sel:
        ins.append(pl.BlockSpec((rows_step, LANE), lambda h, g: (g, 0)))
        args.append(sel)
    ins += [pl.BlockSpec((rows_step, HEAD_V), lambda h, g: (g, o.col(h))),
            pl.BlockSpec((1, rows_step, 1), lambda h, g: (h, g, 0)),
            pl.BlockSpec((rows_step, HEAD_V), lambda h, g: (g, do.col(h)))]
    args += [o.arr, lse, do.arr]
    if has_in:
        ins.append(qs)
        args.append(dq_in)
    return _pcall(
        kern, name=name, grid=(cfg.h, cfg.nq // qpb), in_specs=ins,
        out_specs=[qs, pl.BlockSpec((1, rows_step // cfg.tb, 1, cfg.tb), lambda h, g: (h, g, 0, 0))],
        out_shape=[jax.ShapeDtypeStruct((s, cfg.h * dk), F32),
                   jax.ShapeDtypeStruct((cfg.h, cfg.nb, 1, cfg.tb), F32)],
        scratch_shapes=[pltpu.VMEM((nsub, tb, tk), F32)] * 4,
        compiler_params=pltpu.CompilerParams(dimension_semantics=("parallel", "parallel")),
    )(*args)


def _attn_dkv(cfg, q, ks, v, selt, lse_r, d_r, do, name):
    s, tq, tkb, dk, kb, ksub = cfg.s, cfg.tb, cfg.tkb, cfg.dk, cfg.kb, cfg.ksub
    nq = cfg.nb
    has_sel = selt is not None
    nkp = len(ks)
    outs = list(ks) + [v]

    def kern(*refs):
        k_refs, v_ref = refs[:nkp], refs[nkp]
        q_ref, do_ref, lr_ref, dr_ref = refs[nkp + 1:nkp + 5]
        st_ref = refs[nkp + 5] if has_sel else None
        out_refs = refs[nkp + 5 + has_sel:2 * nkp + 6 + has_sel]
        sa, pa, sb, pb = refs[-4:]
        j, h = pl.program_id(0), pl.program_id(1)
        k0 = j * tkb
        part = [slice(u * kb, (u + 1) * kb) for u in range(ksub)]
        kks = [_load_keys(k_refs, p) for p in part]
        vvs = [v_ref[p, :].astype(BF16) for p in part]
        ns = [k0 + u * kb + lax.broadcasted_iota(jnp.int32, (kb, 1), 0) for u in range(ksub)]
        emats = [_block_of_key(k0 + u * kb, kb, True) for u in range(ksub)] if has_sel else None

        def load_q(i):
            rows = pl.ds(pl.multiple_of(i * tq, tq), tq)
            return q_ref[rows, :].astype(BF16), do_ref[rows, :].astype(BF16)

        def products_into(sbuf, pbuf, i):
            qi, doi = load_q(i)
            for u in range(ksub):
                sbuf[u] = _nt(kks[u], qi)
                pbuf[u] = _nt(vvs[u], doi)

        def consume(sbuf, pbuf, i, carry):
            qi, doi = load_q(i)
            t = i * tq + lax.broadcasted_iota(jnp.int32, (1, tq), 1)
            selt_i = st_ref[i].astype(BF16) if has_sel else None
            new = []
            for u in range(ksub):
                dk_acc, dv_acc = carry[u]
                selx = _nn(emats[u], selt_i) if has_sel else None
                pt, _ = _scores(cfg, sbuf[u], t, ns[u], h, selx, True, lr_ref[0, i])
                dv_acc = dv_acc + _nn(pt.astype(BF16), doi)
                dst = pt * (pbuf[u] - dr_ref[0, i])
                new.append((dk_acc + _nn(dst.astype(BF16), qi), dv_acc))
            return tuple(new)

        if cfg.causal:
            first, count = k0 // tq, nq - k0 // tq
        elif cfg.mode == "win":
            first = k0 // tq
            count = jnp.minimum((k0 + tkb + WIN - 2) // tq + 1, nq) - first
        else:
            first, count = 0, nq

        def pair(p, cr):
            i0 = first + 2 * p
            products_into(sb, pb, i0 + 1)
            cr = consume(sa, pa, i0, cr)
            products_into(sa, pa, i0 + 2)
            return consume(sb, pb, i0 + 1, cr)

        carry = ((jnp.zeros((kb, dk), F32), jnp.zeros((kb, HEAD_V), F32)),) * ksub
        products_into(sa, pa, first)
        carry = lax.fori_loop(0, count // 2 - 1, pair, carry)
        last = first + count - 2
        products_into(sb, pb, last + 1)
        carry = consume(sa, pa, last, carry)
        carry = consume(sb, pb, last + 1, carry)
        for u, (dk_acc, dv_acc) in enumerate(carry):
            vals, off = [], 0
            for p in ks:
                vals.append(dk_acc[:, off:off + p.width] * cfg.scale)
                off += p.width
            vals.append(dv_acc)
            for src, ref, val in zip(outs, out_refs, vals):
                if src.per_head:
                    ref[part[u], :] = val
                else:
                    @pl.when(h == 0)
                    def _(ref=ref, val=val, u=u):
                        ref[part[u], :] = val

                    @pl.when(h > 0)
                    def _(ref=ref, val=val, u=u):
                        ref[part[u], :] += val

    rowv = pl.BlockSpec((1, nq, 1, tq), lambda j, h: (h, 0, 0, 0))
    ins = [pl.BlockSpec((tkb, p.width), lambda j, h, p=p: (j, p.col(h))) for p in ks]
    ins += [pl.BlockSpec((tkb, HEAD_V), lambda j, h: (j, v.col(h))),
            pl.BlockSpec((s, q.width), lambda j, h: (0, q.col(h))),
            pl.BlockSpec((s, HEAD_V), lambda j, h: (0, do.col(h))), rowv, rowv]
    args = [p.arr for p in ks] + [v.arr, q.arr, do.arr, lse_r, d_r]
    if has_sel:
        ins.append(pl.BlockSpec((nq, LANE, tq), lambda j, h: (0, 0, 0)))
        args.append(selt)
    out_specs = [pl.BlockSpec((tkb, p.width), lambda j, h, p=p: (j, h if p.per_head else 0)) for p in outs]
    out_shape = [jax.ShapeDtypeStruct((cfg.sk, (cfg.h if p.per_head else 1) * p.width), F32) for p in outs]
    assert nq % 2 == 0 and (cfg.mode in ("cmp", "mem") or tkb % (2 * tq) == 0), (nq, tkb, tq)
    return _pcall(
        kern, name=name, grid=(cfg.sk // tkb, cfg.h), in_specs=ins, out_specs=out_specs, out_shape=out_shape,
        scratch_shapes=[pltpu.VMEM((ksub, kb, tq), F32)] * 4,
        compiler_params=pltpu.CompilerParams(dimension_semantics=("parallel", "arbitrary")),
    )(*args)


def _attn_dkv_flat(cfg, q, ks, v, selt, lse_r, d_r, do, name):
    s, tq, tkb, dk, kb, ksub = cfg.s, cfg.tb, cfg.tkb, cfg.dk, cfg.kb, cfg.ksub
    nq = cfg.nb
    has_sel = selt is not None
    nkp = len(ks)
    outs = list(ks) + [v]
    assert nq % 2 == 0 and tkb % (2 * tq) == 0, (nq, tkb, tq)
    steps = []
    for j in range(cfg.sk // tkb):
        first = j * tkb // tq
        stop = nq if cfg.causal else min((j * tkb + tkb + WIN - 2) // tq + 1, nq)
        steps += [(j, i0) for i0 in range(first, stop, 2)]
    n_pairs = len(steps)
    steps.append(steps[-1])
    tab_j = jnp.asarray(np.array([p[0] for p in steps], np.int32))
    tab_i = jnp.asarray(np.array([p[1] for p in steps], np.int32))

    def kern(tj_ref, ti_ref, *refs):
        k_refs, v_ref = refs[:nkp], refs[nkp]
        q_ref, do_ref, lr_ref, dr_ref = refs[nkp + 1:nkp + 5]
        st_ref = refs[nkp + 5] if has_sel else None
        out_refs = refs[nkp + 5 + has_sel:2 * nkp + 6 + has_sel]
        sa, pa, sb, pb = refs[-4:]
        h = pl.program_id(0)
        for src, ref in zip(outs, out_refs):
            if src.per_head:
                ref[...] = jnp.zeros_like(ref)
            else:
                @pl.when(h == 0)
                def _(ref=ref):
                    ref[...] = jnp.zeros_like(ref)

        def key_rows(j, u):
            return pl.ds(pl.multiple_of(j * tkb + u * kb, kb), kb)

        def load_q(i):
            rows = pl.ds(pl.multiple_of(i * tq, tq), tq)
            return q_ref[rows, :].astype(BF16), do_ref[rows, :].astype(BF16)

        def products_into(sbuf, pbuf, j, i):
            qi, doi = load_q(i)
            for u in range(ksub):
                rows = key_rows(j, u)
                sbuf[u] = _nt(_load_keys(k_refs, rows), qi)
                pbuf[u] = _nt(v_ref[rows, :].astype(BF16), doi)

        def consume(sbuf, pbuf, j, i):
            qi, doi = load_q(i)
            t = i * tq + lax.broadcasted_iota(jnp.int32, (1, tq), 1)
            selt_i = st_ref[i].astype(BF16) if has_sel else None
            res = []
            for u in range(ksub):
                k0 = j * tkb + u * kb
                n = k0 + lax.broadcasted_iota(jnp.int32, (kb, 1), 0)
                selx = _nn(_block_of_key(k0, kb, True), selt_i) if has_sel else None
                pt, _ = _scores(cfg, sbuf[u], t, n, h, selx, True, lr_ref[0, i])
                dst = pt * (pbuf[u] - dr_ref[0, i])
                res.append((_nn(dst.astype(BF16), qi), _nn(pt.astype(BF16), doi)))
            return res

        def pair(p, carry):
            j, i0 = tj_ref[p], ti_ref[p]
            products_into(sb, pb, j, i0 + 1)
            ca = consume(sa, pa, j, i0)
            products_into(sa, pa, tj_ref[p + 1], ti_ref[p + 1])
            cb = consume(sb, pb, j, i0 + 1)
            for u in range(ksub):
                rows = key_rows(j, u)
                dk_c = (ca[u][0] + cb[u][0]) * cfg.scale
                off = 0
                for src, ref in zip(ks, out_refs):
                    ref[rows, :] += dk_c[:, off:off + src.width]
                    off += src.width
                out_refs[nkp][rows, :] += ca[u][1] + cb[u][1]
            return carry

        products_into(sa, pa, tj_ref[0], ti_ref[0])
        lax.fori_loop(0, n_pairs, pair, 0)

    rowv = pl.BlockSpec((1, nq, 1, tq), lambda h, tj, ti: (h, 0, 0, 0))
    ins = [pl.BlockSpec((cfg.sk, p.width), lambda h, tj, ti, p=p: (0, p.col(h))) for p in ks]
    ins += [pl.BlockSpec((cfg.sk, HEAD_V), lambda h, tj, ti: (0, v.col(h))),
            pl.BlockSpec((s, q.width), lambda h, tj, ti: (0, q.col(h))),
            pl.BlockSpec((s, HEAD_V), lambda h, tj, ti: (0, do.col(h))), rowv, rowv]
    args = [p.arr for p in ks] + [v.arr, q.arr, do.arr, lse_r, d_r]
    if has_sel:
        ins.append(pl.BlockSpec((nq, LANE, tq), lambda h, tj, ti: (0, 0, 0)))
        args.append(selt)
    out_specs = [pl.BlockSpec((cfg.sk, p.width), lambda h, tj, ti, p=p: (0, h if p.per_head else 0))
                 for p in outs]
    out_shape = [jax.ShapeDtypeStruct((cfg.sk, (cfg.h if p.per_head else 1) * p.width), F32) for p in outs]
    grid_spec = pltpu.PrefetchScalarGridSpec(
        num_scalar_prefetch=2, grid=(cfg.h,), in_specs=ins, out_specs=out_specs,
        scratch_shapes=[pltpu.VMEM((ksub, kb, tq), F32)] * 4)
    return _pcall(kern, name=name, grid_spec=grid_spec, out_shape=out_shape,
                  compiler_params=pltpu.CompilerParams(dimension_semantics=("arbitrary",)))(tab_j, tab_i, *args)


def _all_heads(cfg, src, rows, key=False):
    if src.per_head:
        assert src.col0 % cfg.h == 0
        width, col = cfg.h * src.width, src.col0 // cfg.h
    else:
        width, col = src.width, src.col0
    return pl.BlockSpec((rows, width), (lambda i: (0, col)) if key else (lambda i: (i, col)))


def _head_cols(src, hh):
    return slice(hh * src.width, (hh + 1) * src.width) if src.per_head else slice(None)


def _key_window(cfg, i, r):
    if cfg.mode == "win":
        return pl.ds(pl.multiple_of(jnp.maximum(i * cfg.tq + r * cfg.tb - WIN, 0), cfg.tb), cfg.tk)
    return pl.ds(0, cfg.tk)


def _attn_fwd_small(cfg, q, ks, v, name, overlap=None):
    s, tq, tk, tb, nsub, nh = cfg.s, cfg.tq, cfg.tk, cfg.tb, cfg.nsub, cfg.h
    nkp = len(ks)
    select = overlap is not None
    n_s = s // SLC_LEN
    top_n = min(SLC_TOPN, n_s)

    def kern(*refs):
        q_ref, k_refs, v_ref = refs[0], refs[1:1 + nkp], refs[1 + nkp]
        ov_ref = refs[2 + nkp] if select else None
        o_ref, lc_ref, lr_ref = refs[2 + nkp + select:5 + nkp + select]
        i = pl.program_id(0)
        imps = [jnp.zeros((tb, LANE), F32)] * nsub
        for r in range(nsub):
            rows = slice(r * tb, (r + 1) * tb)
            t = i * tq + r * tb + lax.broadcasted_iota(jnp.int32, (tb, 1), 0)
            win = _key_window(cfg, i, r)
            n = win.start + lax.broadcasted_iota(jnp.int32, (1, tk), 1)
            for hh in range(nh):
                qv = q_ref[rows, hh * cfg.dk:(hh + 1) * cfg.dk].astype(BF16)
                kk = _load_keys([kr.at[:, _head_cols(p, hh)] for kr, p in zip(k_refs, ks)], win)
                vv = v_ref[win, _head_cols(v, hh)].astype(BF16)
                sc, mask = _scores(cfg, _nt(qv, kk), t, n, hh, None, True)
                m = jnp.max(sc, axis=1, keepdims=True)
                e = jnp.exp2(sc - m)
                if mask is not None:
                    e = jnp.where(mask, e, 0.0)
                l = jnp.sum(e, axis=1, keepdims=True)
                o_ref[rows, hh * HEAD_V:(hh + 1) * HEAD_V] = _nn(e.astype(BF16), vv) / (l + 1e-20)
                lse = m + jnp.log(l + 1e-20) * LOG2E
                lc_ref[hh, rows, :] = lse
                lr_ref[hh, r] = _to_row(lse)
                if select:
                    imps[r] = imps[r] + _nn((e / (l + 1e-20)).astype(BF16), ov_ref[...])
        if select:
            sel_ref, selt_ref, imp_t = refs[5 + nkp + select:8 + nkp + select]
            for r in range(nsub):
                t = i * tq + r * tb + lax.broadcasted_iota(jnp.int32, (tb, 1), 0)
                j = lax.broadcasted_iota(jnp.int32, (tb, LANE), 1)
                cur = t >> 6
                imp = jnp.where((j == 0) | (j == cur) | (j == cur - 1), 1e9, imps[r])
                imp = jnp.where(j > cur, -1e9, imp)
                imp_t[r] = jnp.transpose(imp)
                mine = imp_t[r, 0:n_s, :]
                jrow = lax.broadcasted_iota(jnp.int32, (n_s, tb), 0)

                def count(k, rank):
                    other = imp_t[r, pl.ds(k, 1), :]
                    ahead = (other > mine) | ((other == mine) & (k < jrow))
                    return rank + jnp.where(ahead, 1.0, 0.0)

                rank = lax.fori_loop(0, n_s, count, jnp.zeros((n_s, tb), F32))
                cur_t = (i * tq + r * tb + lax.broadcasted_iota(jnp.int32, (1, tb), 1)) >> 6
                rejected = jnp.where((rank < top_n) & (jrow <= cur_t), 0.0, 1.0)
                if n_s < LANE:
                    rejected = jnp.concatenate([rejected, jnp.ones((LANE - n_s, tb), F32)], axis=0)
                selt_ref[r] = rejected
                sel_ref[r * tb:(r + 1) * tb, :] = jnp.transpose(rejected)

    ins = [_all_heads(cfg, q, tq)] + [_all_heads(cfg, p, cfg.sk, True) for p in ks]
    ins += [_all_heads(cfg, v, cfg.sk, True)]
    args = [q.arr] + [p.arr for p in ks] + [v.arr]
    out_specs = [pl.BlockSpec((tq, nh * HEAD_V), lambda i: (i, 0)),
                 pl.BlockSpec((nh, tq, 1), lambda i: (0, i, 0)),
                 pl.BlockSpec((nh, nsub, 1, tb), lambda i: (0, i, 0, 0))]
    out_shape = [jax.ShapeDtypeStruct((s, nh * HEAD_V), F32), jax.ShapeDtypeStruct((nh, s, 1), F32),
                 jax.ShapeDtypeStruct((nh, cfg.nb, 1, tb), F32)]
    scratch = []
    if select:
        ins.append(pl.BlockSpec((cfg.sk, LANE), lambda i: (0, 0)))
        args.append(overlap)
        out_specs += [pl.BlockSpec((tq, LANE), lambda i: (i, 0)), pl.BlockSpec((nsub, LANE, tb), lambda i: (i, 0, 0))]
        out_shape += [jax.ShapeDtypeStruct((s, LANE), F32), jax.ShapeDtypeStruct((cfg.nb, LANE, tb), F32)]
        scratch = [pltpu.VMEM((nsub, LANE, tb), F32)]
    return _pcall(kern, name=name, grid=(cfg.nq,), in_specs=ins, out_specs=out_specs, out_shape=out_shape,
                  scratch_shapes=scratch,
                  compiler_params=pltpu.CompilerParams(dimension_semantics=("parallel",)))(*args)


def _attn_dq_small(cfg, q, ks, v, o, lse, do, dq_in, name):
    s, tq, tk, tb, nsub, nh, dk = cfg.s, cfg.tq, cfg.tk, cfg.tb, cfg.nsub, cfg.h, cfg.dk
    nkp = len(ks)
    has_in = dq_in is not None

    def kern(*refs):
        q_ref, k_refs, v_ref = refs[0], refs[1:1 + nkp], refs[1 + nkp]
        o_ref, l_ref, do_ref = refs[2 + nkp:5 + nkp]
        in_ref = refs[5 + nkp] if has_in else None
        dq_ref, dr_ref = refs[5 + nkp + has_in:7 + nkp + has_in]
        i = pl.program_id(0)
        for r in range(nsub):
            rows = slice(r * tb, (r + 1) * tb)
            t = i * tq + r * tb + lax.broadcasted_iota(jnp.int32, (tb, 1), 0)
            win = _key_window(cfg, i, r)
            n = win.start + lax.broadcasted_iota(jnp.int32, (1, tk), 1)
            for hh in range(nh):
                vcols = slice(hh * HEAD_V, (hh + 1) * HEAD_V)
                qcols = slice(hh * dk, (hh + 1) * dk)
                qv = q_ref[rows, qcols].astype(BF16)
                kk = _load_keys([kr.at[:, _head_cols(p, hh)] for kr, p in zip(k_refs, ks)], win)
                vv = v_ref[win, _head_cols(v, hh)].astype(BF16)
                dov = do_ref[rows, vcols]
                dvec = jnp.sum(dov * o_ref[rows, vcols], axis=1, keepdims=True)
                dr_ref[hh, r] = _to_row(dvec)
                p, _ = _scores(cfg, _nt(qv, kk), t, n, hh, None, True, l_ref[hh, rows, :])
                ds = p * (_nt(dov.astype(BF16), vv) - dvec)
                dq = _nn(ds.astype(BF16), kk) * cfg.scale
                dq_ref[rows, qcols] = dq + in_ref[rows, qcols] if has_in else dq

    qs = pl.BlockSpec((tq, nh * dk), lambda i: (i, 0))
    ins = [_all_heads(cfg, q, tq)] + [_all_heads(cfg, p, cfg.sk, True) for p in ks]
    ins += [_all_heads(cfg, v, cfg.sk, True)]
    ins += [_all_heads(cfg, o, tq), pl.BlockSpec((nh, tq, 1), lambda i: (0, i, 0)), _all_heads(cfg, do, tq)]
    args = [q.arr] + [p.arr for p in ks] + [v.arr, o.arr, lse, do.arr]
    if has_in:
        ins.append(qs)
        args.append(dq_in)
    return _pcall(
        kern, name=name, grid=(cfg.nq,), in_specs=ins,
        out_specs=[qs, pl.BlockSpec((nh, nsub, 1, tb), lambda i: (0, i, 0, 0))],
        out_shape=[jax.ShapeDtypeStruct((s, nh * dk), F32), jax.ShapeDtypeStruct((nh, cfg.nb, 1, tb), F32)],
        compiler_params=pltpu.CompilerParams(dimension_semantics=("parallel",)))(*args)


def _attn_bwd(cfg, q, ks, v, sel, selt, o, lse, lse_r, do, dq_in, name):
    if cfg.causal:
        dq, d_r = _attn_dq(cfg, q, ks, v, sel, o, lse, do, dq_in, name + "_dq")
    else:
        dq, d_r = _attn_dq_small(cfg, q, ks, v, o, lse, do, dq_in, name + "_dq")
    dkv = _attn_dkv_flat if cfg.causal or cfg.mode == "win" else _attn_dkv
    res = dkv(cfg, q, ks, v, selt, lse_r, d_r, do, name + "_dkv")
    return dq, res[:-1], res[-1]


def _silu_grad(pre):
    sg = _sigmoid(pre)
    return sg * (1.0 + pre * (1.0 - sg))


def _compress_fwd(a_lo, a_hi, pe_lo, pe_hi, w1_lo, w1_hi, w2, name):
    n, dp = a_lo.shape[0], w2.shape[1]

    def kern(alo, ahi, plo, phi, w1l, w1h, w2r, out_ref, pre_ref):
        xl = (alo[...] + plo[...]).astype(BF16)
        xh = (ahi[...] + phi[...]).astype(BF16)
        pre = _nn(xl, w1l[...]) + _nn(xh, w1h[...])
        act = pre * _sigmoid(pre)
        out_ref[...] = _nn(act.astype(BF16), w2r[...]).astype(BF16)
        pre_ref[...] = pre

    return _pcall(kern, name=name,
                  out_shape=[jax.ShapeDtypeStruct((n, dp), BF16), jax.ShapeDtypeStruct((n, dp), F32)],
                  )(a_lo, a_hi, pe_lo, pe_hi, w1_lo, w1_hi, w2)


def _compress_bwd(a_lo, a_hi, pe_lo, pe_hi, w1_lo, w1_hi, w2, pre, pre_sh, dout, dout_sh, name):
    n, ln = a_lo.shape
    dp = w2.shape[1]

    def kern(alo, ahi, plo, phi, w1l, w1h, w2r, pre_ref, presh_ref, do_ref, dosh_ref,
             da_ref, dpl_ref, dph_ref, dw1l_ref, dw1h_ref, dw2_ref):
        prev = pre_ref[...]
        act = prev * _sigmoid(prev)
        dob = do_ref[...].astype(BF16)
        w2v = w2r[...]
        dpre = (_nt(dob, w2v) * _silu_grad(prev)).astype(BF16)
        dpre_sh = (_nt(dosh_ref[...].astype(BF16), w2v) * _silu_grad(presh_ref[...])).astype(BF16)
        dw2_ref[...] = _nn(act.T.astype(BF16), dob)
        xl = alo[...] + plo[...]
        xh = ahi[...] + phi[...]
        dw1l_ref[...] = _nn(xl.T.astype(BF16), dpre)
        dw1h_ref[...] = _nn(xh.T.astype(BF16), dpre)
        dal = _nt(dpre, w1l[...])
        dah_sh = _nt(dpre_sh, w1h[...])
        da_ref[...] = dal + dah_sh
        dpl_ref[...] = jnp.sum(dal, axis=0, keepdims=True)
        dph_ref[...] = jnp.sum(dah_sh, axis=0, keepdims=True)

    return _pcall(
        kern, name=name,
        out_shape=[jax.ShapeDtypeStruct((n, ln), F32), jax.ShapeDtypeStruct((1, ln), F32),
                   jax.ShapeDtypeStruct((1, ln), F32), jax.ShapeDtypeStruct((ln, dp), F32),
                   jax.ShapeDtypeStruct((ln, dp), F32), jax.ShapeDtypeStruct((dp, dp), F32)],
    )(a_lo, a_hi, pe_lo, pe_hi, w1_lo, w1_hi, w2, pre, pre_sh, dout, dout_sh)


def _nsa_combine(o_cmp, o_slc, o_win, gl):
    s, w = o_cmp.shape
    tr = _tile(s, 512)

    def kern(a_ref, b_ref, c_ref, g_ref, o_ref):
        g = _sigmoid(g_ref[...])
        for h in range(NSA_HEADS):
            cs = slice(h * HEAD_V, (h + 1) * HEAD_V)
            o_ref[:, cs] = (g[:, 3 * h:3 * h + 1] * a_ref[:, cs] + g[:, 3 * h + 1:3 * h + 2] * b_ref[:, cs]
                            + g[:, 3 * h + 2:3 * h + 3] * c_ref[:, cs])

    row = pl.BlockSpec((tr, w), lambda i: (i, 0))
    return _pcall(kern, name="nsa_combine", grid=(s // tr,),
                  in_specs=[row, row, row, pl.BlockSpec((tr, LANE), lambda i: (i, gl.col0))], out_specs=row,
                  out_shape=jax.ShapeDtypeStruct((s, w), F32))(o_cmp, o_slc, o_win, gl.arr)


def _nsa_combine_bwd(do_cat, o_cmp, o_slc, o_win, gl):
    s, w = o_cmp.shape
    tr = _tile(s, 512)

    def kern(d_ref, a_ref, b_ref, c_ref, g_ref, da_ref, db_ref, dc_ref, dg_ref):
        g = _sigmoid(g_ref[...])
        lane = lax.broadcasted_iota(jnp.int32, (tr, LANE), 1)
        dgl = jnp.zeros((tr, LANE), F32)
        for h in range(NSA_HEADS):
            cs = slice(h * HEAD_V, (h + 1) * HEAD_V)
            dv = d_ref[:, cs]
            for b, (src, dst) in enumerate(((a_ref, da_ref), (b_ref, db_ref), (c_ref, dc_ref))):
                gate = g[:, 3 * h + b:3 * h + b + 1]
                dst[:, cs] = gate * dv
                dgate = jnp.sum(dv * src[:, cs], axis=1, keepdims=True)
                dgl = jnp.where(lane == 3 * h + b, dgate * gate * (1.0 - gate), dgl)
        dg_ref[...] = dgl

    row = pl.BlockSpec((tr, w), lambda i: (i, 0))
    tab = pl.BlockSpec((tr, LANE), lambda i: (i, 0))
    return _pcall(kern, name="nsa_combine_bwd", grid=(s // tr,),
                  in_specs=[pl.BlockSpec((tr, w), lambda i: (i, 2)), row, row, row,
                            pl.BlockSpec((tr, LANE), lambda i: (i, gl.col0))],
                  out_specs=[row, row, row, tab],
                  out_shape=[jax.ShapeDtypeStruct((s, w), F32)] * 3 + [jax.ShapeDtypeStruct((s, LANE), F32)],
                  )(do_cat, o_cmp, o_slc, o_win, gl.arr)


def _gate_fwd(o_mla, o_nsa, o_mem, hp):
    s = o_mla.shape[0]
    tr = _tile(s, 256)

    def kern(a_ref, b_ref, c_ref, z_ref, u_ref):
        z = z_ref[...]
        sz = z * _sigmoid(z)
        u_ref[:, 0:1024] = (a_ref[...] * sz[:, 0:1024]).astype(BF16)
        u_ref[:, 1024:1536] = (b_ref[...] * sz[:, 1024:1536]).astype(BF16)
        u_ref[:, 1536:2048] = (c_ref[...] * sz[:, 1536:2048]).astype(BF16)

    return _pcall(
        kern, name="gate_fwd", grid=(s // tr,),
        in_specs=[pl.BlockSpec((tr, 1024), lambda i: (i, 0)), pl.BlockSpec((tr, 512), lambda i: (i, 0)),
                  pl.BlockSpec((tr, 512), lambda i: (i, 0)), pl.BlockSpec((tr, 2048), lambda i: (i, 2))],
        out_specs=pl.BlockSpec((tr, 2048), lambda i: (i, 0)),
        out_shape=jax.ShapeDtypeStruct((s, 2048), BF16))(o_mla, o_nsa, o_mem, hp)


def _gate_bwd(du, o_mla, o_nsa, o_mem, hp):
    s = du.shape[0]
    tr = _tile(s, 256)

    def kern(d_ref, a_ref, b_ref, c_ref, z_ref, do_ref, dz_ref):
        z = z_ref[...]
        sg = _sigmoid(z)
        sz = z * sg
        dsz = sg * (1.0 + z * (1.0 - sg))
        d = d_ref[...]
        do_ref[...] = d * sz
        dz_ref[:, 0:1024] = d[:, 0:1024] * a_ref[...] * dsz[:, 0:1024]
        dz_ref[:, 1024:1536] = d[:, 1024:1536] * b_ref[...] * dsz[:, 1024:1536]
        dz_ref[:, 1536:2048] = d[:, 1536:2048] * c_ref[...] * dsz[:, 1536:2048]

    wide = pl.BlockSpec((tr, 2048), lambda i: (i, 0))
    return _pcall(
        kern, name="gate_bwd", grid=(s // tr,),
        in_specs=[wide, pl.BlockSpec((tr, 1024), lambda i: (i, 0)), pl.BlockSpec((tr, 512), lambda i: (i, 0)),
                  pl.BlockSpec((tr, 512), lambda i: (i, 0)), pl.BlockSpec((tr, 2048), lambda i: (i, 2))],
        out_specs=[wide, wide],
        out_shape=[jax.ShapeDtypeStruct((s, 2048), F32)] * 2)(du, o_mla, o_nsa, o_mem, hp)


def _tile2d(rows, cols, arrays):
    if rows % 16 == 0:
        return _row_tile(rows, cols * arrays), cols
    want = max(LANE, BLOCK_BYTES // (rows * 4 * arrays) // LANE * LANE)
    tc = LANE
    for t in range(LANE, cols + 1, LANE):
        if cols % t == 0 and t <= want:
            tc = t
    return rows, tc


def _sum_slots(buf, name):
    n, rows, cols = buf.shape
    tr, tc = _tile2d(rows, cols, n)

    def kern(b_ref, o_ref):
        acc = b_ref[0].astype(F32)
        for i in range(1, n):
            acc = acc + b_ref[i].astype(F32)
        o_ref[...] = acc

    return _pcall(kern, name=name, grid=(rows // tr, cols // tc),
                  in_specs=[pl.BlockSpec((n, tr, tc), lambda i, j: (0, i, j))],
                  out_specs=pl.BlockSpec((tr, tc), lambda i, j: (i, j)),
                  out_shape=jax.ShapeDtypeStruct((rows, cols), F32))(buf)


def _chip_sum(buf, core, axis, name):
    n, rows, cols = buf.shape
    tr, tc = _tile2d(rows, cols, n)
    nbr, nbc = rows // tr, cols // tc

    def kern(c_ref, b_ref, o_ref, w_ref):
        acc = b_ref[0].astype(F32)
        for i in range(1, n):
            acc = acc + b_ref[i].astype(F32)
        o_ref[...] = acc
        w_ref[...] = acc

    place = ((lambda i, j, c: (c[0] * nbr + i, j)) if axis == 0 else (lambda i, j, c: (i, c[0] * nbc + j)))
    whole = (2 * rows, cols) if axis == 0 else (rows, 2 * cols)
    grid_spec = pltpu.PrefetchScalarGridSpec(
        num_scalar_prefetch=1, grid=(nbr, nbc),
        in_specs=[pl.BlockSpec((n, tr, tc), lambda i, j, c: (0, i, j))],
        out_specs=[pl.BlockSpec((tr, tc), lambda i, j, c: (i, j)), pl.BlockSpec((tr, tc), place)])
    return _pcall(kern, name=name, grid_spec=grid_spec,
                  out_shape=[jax.ShapeDtypeStruct((rows, cols), F32), jax.ShapeDtypeStruct(whole, F32)])(core, buf)


def _pair_sum(g4, theirs, core, axis, name):
    n, rows, cols = theirs.shape
    tr, tc = _tile2d(rows, cols, 1)
    nbr, nbc = rows // tr, cols // tc

    def kern(c_ref, a_ref, b_ref, o_ref):
        o_ref[...] = (a_ref[...] + b_ref[...]).astype(BF16)

    blk = (1, tr, tc)
    mine = ((lambda s, i, j, c: (s, c[0] * nbr + i, j)) if axis == 0
            else (lambda s, i, j, c: (s, i, c[0] * nbc + j)))
    grid_spec = pltpu.PrefetchScalarGridSpec(
        num_scalar_prefetch=1, grid=(n, nbr, nbc),
        in_specs=[pl.BlockSpec(blk, mine), pl.BlockSpec(blk, lambda s, i, j, c: (s, i, j))],
        out_specs=pl.BlockSpec(blk, lambda s, i, j, c: (s, i, j)))
    return _pcall(kern, name=name, grid_spec=grid_spec,
                  out_shape=jax.ShapeDtypeStruct((n, rows, cols), BF16))(core, g4, theirs)


def _adamw(w, g, m, v, name):
    rows, cols = w.shape
    tr, tc = _tile2d(rows, cols, 4)
    bc1 = 1.0 - ADAM_B1 ** ADAM_STEP
    bc2 = 1.0 - ADAM_B2 ** ADAM_STEP

    def kern(w_ref, g_ref, m_ref, v_ref, d_ref, mo_ref, vo_ref):
        gv = g_ref[...]
        mn = ADAM_B1 * m_ref[...] + (1.0 - ADAM_B1) * gv
        vn = ADAM_B2 * v_ref[...] + (1.0 - ADAM_B2) * (gv * gv)
        d_ref[...] = -ADAM_LR * ((mn / bc1) / (jnp.sqrt(vn / bc2) + ADAM_EPS) + ADAM_WD * w_ref[...])
        mo_ref[...] = mn
        vo_ref[...] = vn

    blk = pl.BlockSpec((tr, tc), lambda i, j: (i, j))
    return _pcall(kern, name=name, grid=(rows // tr, cols // tc), in_specs=[blk] * 4, out_specs=[blk] * 3,
                  out_shape=[jax.ShapeDtypeStruct((rows, cols), F32)] * 3)(w, g, m, v)


ANY = pl.BlockSpec(memory_space=pl.ANY)


def _place():
    x, y, c = lax.axis_index("x"), lax.axis_index("y"), lax.axis_index("c")
    chips = [(1 - x, y), (x, 1 - y), (1 - x, 1 - y)]
    return x, y, c, chips


def _remote(src, dst, send_sem, recv_sem, to):
    return pltpu.make_async_remote_copy(src_ref=src, dst_ref=dst, send_sem=send_sem, recv_sem=recv_sem,
                                        device_id=to, device_id_type=MESH)


def _half(ref, lead, core, axis):
    size = ref.shape[len(lead) + axis] // 2
    cut = pl.ds(core * size, size)
    return ref.at[tuple(lead) + ((cut, slice(None)) if axis == 0 else (slice(None), cut))]


def _gather_shards(ws, axes):
    side = _gather_side(ws, axes)

    def body(*refs):
        nw = len(ws)
        split = (refs[:nw], refs[nw:2 * nw], refs[2 * nw:])
        side.phase("start", *split)
        side.phase("finish", *split)

    return _pcall(body, name="gather_shards", in_specs=[ANY] * len(ws), out_specs=[ANY] * len(ws),
                  out_shape=side.out_shape, scratch_shapes=side.scratch)(*ws)


def _gather_side(ws, axes):
    nw = len(ws)

    def phase(which, w_refs, out_refs, sems):
        send_sems, recv_sems = sems
        x, y, c, chips = _place()
        me = 2 * x + y
        sibling = (x, y, 1 - c)

        def part(i, slot, core):
            return _half(out_refs[i], (slot,), core, axes[i])

        def copy(sem, src, dst, to):
            return _remote(src, dst, send_sems.at[sem], recv_sems.at[sem], to)

        first = [copy(j * nw + i, _half(w_refs[i], (), c, axes[i]), part(i, me, c), (*chip, c))
                 for j, chip in enumerate(chips) for i in range(nw)]
        if which == "start":
            for cp in first:
                cp.start()
            return
        passed = []
        for j, (cx, cy) in enumerate(chips):
            slot = 2 * cx + cy
            for i in range(nw):
                copy(j * nw + i, part(i, slot, c), part(i, slot, c), (x, y, c)).wait_recv()
                fwd = copy((3 + j) * nw + i, part(i, slot, c), part(i, slot, c), sibling)
                fwd.start()
                passed.append(fwd)
        for j, (cx, cy) in enumerate(chips):
            slot = 2 * cx + cy
            for i in range(nw):
                copy((3 + j) * nw + i, part(i, slot, 1 - c), part(i, slot, 1 - c), (x, y, c)).wait_recv()
        for cp in first + passed:
            cp.wait_send()

    return _Side(list(ws), [jax.ShapeDtypeStruct((4,) + w.shape, w.dtype) for w in ws],
                 [pltpu.SemaphoreType.DMA((6 * nw,)), pltpu.SemaphoreType.DMA((6 * nw,))], phase)


def _half_shape(shape, axis):
    return tuple(d // 2 if k == len(shape) - 2 + axis else d for k, d in enumerate(shape))


def _pair_exchange(gs, axes, name):
    nw = len(gs)

    def body(*refs):
        g_refs, out_refs = refs[:nw], refs[nw:2 * nw]
        send_sems, recv_sems = refs[2 * nw:]
        x, y, c, _ = _place()
        cps = []
        for i in range(nw):
            cp = _remote(_half(g_refs[i], (slice(None),), 1 - c, axes[i]), out_refs[i],
                         send_sems.at[i], recv_sems.at[i], (x, y, 1 - c))
            cp.start()
            cps.append(cp)
        for cp in cps:
            cp.wait()

    return _pcall(body, name=name, in_specs=[ANY] * nw, out_specs=[ANY] * nw,
                  out_shape=[jax.ShapeDtypeStruct(_half_shape(g.shape, a), g.dtype) for g, a in zip(gs, axes)],
                  scratch_shapes=[pltpu.SemaphoreType.DMA((nw,)), pltpu.SemaphoreType.DMA((nw,))])(*gs)


def _chip_side(ps):
    nw = len(ps)

    def phase(which, p_refs, out_refs, sems):
        send_sems, recv_sems, local_sems = sems
        x, y, c, chips = _place()
        me = 2 * x + y
        mine = [pltpu.make_async_copy(p_refs[i].at[me], out_refs[i].at[me], local_sems.at[i]) for i in range(nw)]
        sends = [_remote(p_refs[i].at[2 * cx + cy], out_refs[i].at[me], send_sems.at[j * nw + i],
                         recv_sems.at[j * nw + i], (cx, cy, c))
                 for j, (cx, cy) in enumerate(chips) for i in range(nw)]
        if which == "start":
            for cp in mine + sends:
                cp.start()
            return
        for j, (cx, cy) in enumerate(chips):
            slot = 2 * cx + cy
            for i in range(nw):
                _remote(out_refs[i].at[slot], out_refs[i].at[slot], send_sems.at[j * nw + i],
                        recv_sems.at[j * nw + i], (x, y, c)).wait_recv()
        for cp in sends:
            cp.wait_send()
        for cp in mine:
            cp.wait()

    return _Side(list(ps), [jax.ShapeDtypeStruct(p.shape, p.dtype) for p in ps],
                 [pltpu.SemaphoreType.DMA((3 * nw,)), pltpu.SemaphoreType.DMA((3 * nw,)),
                  pltpu.SemaphoreType.DMA((nw,))], phase)


def _half_exchange(ts, wholes, axes):
    nw = len(ts)

    def body(*refs):
        t_refs, out_refs = refs[:nw], refs[2 * nw:3 * nw]
        send_sems, recv_sems = refs[3 * nw:]
        x, y, c, _ = _place()
        sends = []
        for i in range(nw):
            cp = _remote(t_refs[i], _half(out_refs[i], (), c, axes[i]), send_sems.at[i], recv_sems.at[i],
                         (x, y, 1 - c))
            cp.start()
            sends.append(cp)
        for i in range(nw):
            _remote(t_refs[i], _half(out_refs[i], (), 1 - c, axes[i]), send_sems.at[i], recv_sems.at[i],
                    (x, y, c)).wait_recv()
        for cp in sends:
            cp.wait_send()

    return _pcall(body, name="half_exchange", in_specs=[ANY] * (2 * nw), out_specs=[ANY] * nw,
                  out_shape=[jax.ShapeDtypeStruct(w.shape, w.dtype) for w in wholes],
                  input_output_aliases={nw + i: i for i in range(nw)},
                  scratch_shapes=[pltpu.SemaphoreType.DMA((nw,)), pltpu.SemaphoreType.DMA((nw,))])(*ts, *wholes)


def _gather_all(v):
    rows, cols = v.shape

    def body(v_ref, out_ref, send_sems, recv_sems, local_sem):
        x, y, c, _ = _place()
        me = 4 * x + 2 * y + c
        mine = pltpu.make_async_copy(v_ref, out_ref.at[me], local_sem)
        mine.start()
        sends = []
        for d in range(1, 8):
            peer = (x ^ (d >> 2), y ^ ((d >> 1) & 1), c ^ (d & 1))
            cp = _remote(v_ref, out_ref.at[me], send_sems.at[d - 1], recv_sems.at[d - 1], peer)
            cp.start()
            sends.append(cp)
        for d in range(1, 8):
            slot = 4 * (x ^ (d >> 2)) + 2 * (y ^ ((d >> 1) & 1)) + (c ^ (d & 1))
            _remote(v_ref, out_ref.at[slot], send_sems.at[d - 1], recv_sems.at[d - 1], (x, y, c)).wait_recv()
        for cp in sends:
            cp.wait_send()
        mine.wait()

    return _pcall(body, name="gather_all", in_specs=[ANY], out_specs=ANY,
                  out_shape=jax.ShapeDtypeStruct((8, rows, cols), v.dtype),
                  scratch_shapes=[pltpu.SemaphoreType.DMA((7,)), pltpu.SemaphoreType.DMA((7,)),
                                  pltpu.SemaphoreType.DMA])(v)


def _pad_cols(a, width):
    return a if a.shape[1] == width else jnp.pad(a, ((0, 0), (0, width - a.shape[1])))


def _unpad_segments():
    z = PAD["z"]
    segs = [(PAD["c_q"], 0, 512), (PAD["c_kv"], 512, 512), (PAD["k_rope"], 1024, 64), (z, 1088, 1024)]
    segs += [(PAD["q_nsa"] + 256 * h, 2112 + NSA_DK * h, NSA_DK) for h in range(NSA_HEADS)]
    for name, rows in (("k_c", 192), ("v_c", 128), ("k_s", 192), ("v_s", 128), ("k_w", 192), ("v_w", 128),
                       ("g_nsa", 12)):
        segs.append((PAD[name], ORIG[name][0], rows))
    segs += [(z + 1024, ORIG["z_nsa"][0], 512), (PAD["q_mem"], ORIG["q_mem"][0], 512),
             (z + 1536, ORIG["z_mem"][0], 512)]
    return segs


def _w_in_grad_slots(gt):
    rows, cols = gt.shape
    shard = sum(n for _, _, n in _unpad_segments()) // 4
    tc = 256
    pieces = []
    for src, dst, n in _unpad_segments():
        while n:
            slot, off = divmod(dst, shard)
            take = min(n, shard - off)
            pieces.append((src, slot, off, take))
            src, dst, n = src + take, dst + take, n - take

    def kern(g_ref, o_ref):
        for src, slot, off, take in pieces:
            o_ref[slot, off:off + take, :] = g_ref[src:src + take, :]

    return _pcall(kern, name="w_in_grad_slots", grid=(cols // tc,),
                  in_specs=[pl.BlockSpec((rows, tc), lambda i: (0, i))],
                  out_specs=pl.BlockSpec((4, shard, tc), lambda i: (0, 0, i)),
                  out_shape=jax.ShapeDtypeStruct((4, shard, cols), F32))(gt)


def _w_in_from_slots(ws):
    nslot, shard, cols = ws.shape
    tc = 256
    pieces = []
    for dst, src, n in _unpad_segments() + [(PAD["k_rope"] + 64, ORIG["k_rope"][0], 64)]:
        while n:
            slot, off = divmod(src, shard)
            take = min(n, shard - off)
            pieces.append((dst, slot, off, take))
            src, dst, n = src + take, dst + take, n - take

    def kern(w_ref, o_ref):
        o_ref[...] = jnp.zeros_like(o_ref)
        for dst, slot, off, take in pieces:
            o_ref[dst:dst + take, :] = w_ref[slot, off:off + take, :]

    return _pcall(kern, name="w_in_from_slots", grid=(cols // tc,),
                  in_specs=[pl.BlockSpec((nslot, shard, tc), lambda i: (0, 0, i))],
                  out_specs=pl.BlockSpec((D_PAD, tc), lambda i: (0, i)),
                  out_shape=jax.ShapeDtypeStruct((D_PAD, cols), ws.dtype))(ws)


def _rope_tables(s):
    pos = jnp.arange(s, dtype=F32)
    inv_freq = ROPE_THETA ** (-jnp.arange(0, 64, 2, dtype=F32) / 64)
    ang = pos[:, None] * inv_freq[None, :]
    cos, sin = jnp.cos(ang), jnp.sin(ang)
    z = jnp.zeros((s, 64), F32)
    return jnp.concatenate([cos, cos, z], axis=1), jnp.concatenate([-sin, sin, z], axis=1)


def _overlap_table(s):
    n_c, n_s = s // CMP_STRIDE, s // SLC_LEN
    c0 = np.arange(n_c)[:, None] * CMP_STRIDE
    s0 = np.arange(LANE)[None, :] * SLC_LEN
    ov = (c0 < s0 + SLC_LEN) & (c0 + CMP_LEN > s0) & (np.arange(n_c)[:, None] < n_c - 1) & (np.arange(LANE)[None, :] < n_s)
    return jnp.asarray(ov.astype(np.float32), dtype=BF16)


def _shift_down(a):
    return jnp.concatenate([jnp.zeros((8, a.shape[1]), a.dtype), a], axis=0)[7:7 + a.shape[0]]


def _shift_up(a):
    return jnp.concatenate([a, jnp.zeros((8, a.shape[1]), a.dtype)], axis=0)[1:1 + a.shape[0]]


def _local_step(x, mem, target, w, hooks=None):
    s = x.shape[0]
    cs, sn = _rope_tables(s)
    t_ = jnp.transpose

    w_in_p = _w_in_from_slots(w["w_in_t"])
    xn, rstd_x = _rms_fwd(_Src(x, D_MODEL), w["norm_g"], "norm_x")
    if hooks is None:
        hp, hpb = _mm(xn, w_in_p, "in_proj", mode="nt", second_dtype=BF16)
    else:
        hp, hpb, *gathered = _mm(xn, w_in_p, "in_proj", mode="nt", second_dtype=BF16, side=hooks.gather_side)
        w = {**w, **hooks.weights(gathered)}

    w_uq3 = w["w_uq"].reshape(512, MLA_HEADS, 192)
    w_uq_p = jnp.concatenate([w_uq3, w_uq3[:, :, 128:]], axis=2).reshape(512, MLA_HEADS * 256)
    w_ukv_p = t_(w["w_ukv"].reshape(512, MLA_HEADS, 2, 128), (0, 2, 1, 3)).reshape(512, 2048)
    c_q, c_kv = _Src(hp, 512, 0), _Src(hp, 512, 1)
    cqn, rstd_q = _rms_fwd(c_q, w["q_norm_g"], "norm_q")
    ckvn, rstd_kv = _rms_fwd(c_kv, w["kv_norm_g"], "norm_kv")
    q_lin = _mm(cqn, w_uq_p, "mla_q_proj")
    kvb = _mm(ckvn, w_ukv_p, "mla_kv_proj", out_dtype=BF16)
    q_mla = _rope_fwd(_Src(q_lin, MLA_HEADS * 256), cs, sn, MLA_HEADS, 256, LANE, "rope_q")
    k_pe = _rope_fwd(_Src(hp, LANE, PAD["k_rope"] // LANE), cs, sn, 1, LANE, 0, "rope_k")
    mla = _Attn("mla", s, s, MLA_HEADS, 256)
    mla_q, mla_v = _Src(q_mla, 256), _Src(kvb, LANE, MLA_HEADS)
    mla_k = [_Src(kvb, LANE), _Src(k_pe, LANE, 0, False)]
    o_mla, l_mla, lr_mla = _attn_fwd(mla, mla_q, mla_k, mla_v, None, "mla_fwd")

    sk = s // CMP_STRIDE
    pe_k, pe_v = w["cmp_pe_k"], w["cmp_pe_v"]
    w1k = _pad_cols(w["cmp_w1k"], 256)
    w2k = jnp.pad(w["cmp_w2k"], ((0, 64), (0, 64))).astype(BF16)
    w1v, w2v = w["cmp_w1v"], w["cmp_w2v"].astype(BF16)
    half_k, half_v = CMP_STRIDE * NSA_DK, CMP_STRIDE * HEAD_V
    ak = hp[:, PAD["k_c"]:PAD["k_c"] + NSA_DK].reshape(sk, half_k)
    av = hp[:, PAD["v_c"]:PAD["v_c"] + HEAD_V].reshape(sk, half_v)
    ck_args = (ak, _shift_up(ak), pe_k[:CMP_STRIDE].reshape(1, half_k), pe_k[CMP_STRIDE:].reshape(1, half_k),
               w1k[:half_k], w1k[half_k:], w2k)
    cv_args = (av, _shift_up(av), pe_v[:CMP_STRIDE].reshape(1, half_v), pe_v[CMP_STRIDE:].reshape(1, half_v),
               w1v[:half_v], w1v[half_v:], w2v)
    k_cmp, pre_k = _compress_fwd(*ck_args, "compress_k")
    v_cmp, pre_v = _compress_fwd(*cv_args, "compress_v")
    cmp_ = _Attn("cmp", s, sk, NSA_HEADS, 256)
    slc = _Attn("slc", s, s, NSA_HEADS, 256)
    win = _Attn("win", s, s, NSA_HEADS, 256)
    nsa_q = _Src(hpb, 256, PAD["q_nsa"] // 256)
    cmp_k, cmp_v = [_Src(k_cmp, 256, 0, False)], _Src(v_cmp, HEAD_V, 0, False)
    slc_k, slc_v = [_Src(hpb, 256, PAD["k_s"] // 256, False)], _Src(hpb, HEAD_V, PAD["v_s"] // HEAD_V, False)
    win_k, win_v = [_Src(hpb, 256, PAD["k_w"] // 256, False)], _Src(hpb, HEAD_V, PAD["v_w"] // HEAD_V, False)
    o_cmp, l_cmp, lr_cmp, sel, selt = _attn_fwd_small(cmp_, nsa_q, cmp_k, cmp_v, "cmp_fwd", _overlap_table(s))
    o_slc, l_slc, lr_slc = _attn_fwd(slc, nsa_q, slc_k, slc_v, sel, "slc_fwd")
    o_win, l_win, lr_win = _attn_fwd_small(win, nsa_q, win_k, win_v, "win_fwd")
    gl = _Src(hp, LANE, PAD["g_nsa"] // LANE)
    o_nsa = _nsa_combine(o_cmp, o_slc, o_win, gl)

    mn, rstd_m = _rms_fwd(_Src(mem, D_MODEL), w["mem_norm_g"], "norm_mem")
    kvm = _mm(mn, w["w_mem_kv"], "mem_kv_proj", out_dtype=BF16)
    mem_ = _Attn("mem", s, mem.shape[0], MEM_HEADS, LANE)
    mem_q, mem_k, mem_v = _Src(hpb, LANE, PAD["q_mem"] // LANE), [_Src(kvm, LANE)], _Src(kvm, LANE, MEM_HEADS)
    o_mem, l_mem, lr_mem = _attn_fwd_small(mem_, mem_q, mem_k, mem_v, "mem_fwd")

    u = _gate_fwd(o_mla, o_nsa, o_mem, hp)
    proj = _mm(u, w["w_out"], "out_proj", wide=2048)
    dy, g_final, loss = _final_loss(x, proj, w["final_norm_g"].reshape(1, -1), target)

    g_w_out = _mm(u, dy, "out_proj_dw", mode="tn")
    du = _mm(dy, w["w_out"], "out_proj_dx", mode="nt", wide=2048)
    do_cat, dz = _gate_bwd(du, o_mla, o_nsa, o_mem, hp)

    dq_mem, (dk_mem,), dv_mem = _attn_bwd(mem_, mem_q, mem_k, mem_v, None, None, _Src(o_mem, HEAD_V), l_mem,
                                          lr_mem, _Src(do_cat, HEAD_V, 12), None, "mem_bwd")
    dkvm = jnp.concatenate([dk_mem, dv_mem], axis=1)
    g_w_mem_kv = _mm(mn, dkvm, "mem_kv_dw", mode="tn")
    dmn = _mm(dkvm, w["w_mem_kv"], "mem_kv_dx", mode="nt")
    _, g_mem_norm = _rms_bwd(_Src(mem, D_MODEL), w["mem_norm_g"], rstd_m, dmn, None, "norm_mem_bwd")

    do_cmp, do_slc, do_win, dgl = _nsa_combine_bwd(do_cat, o_cmp, o_slc, o_win, gl)
    dq_n, (dk_cmp,), dv_cmp = _attn_bwd(cmp_, nsa_q, cmp_k, cmp_v, None, None, _Src(o_cmp, HEAD_V), l_cmp,
                                        lr_cmp, _Src(do_cmp, HEAD_V), None, "cmp_bwd")
    dq_n, (dk_s,), dv_s = _attn_bwd(slc, nsa_q, slc_k, slc_v, sel, selt, _Src(o_slc, HEAD_V), l_slc, lr_slc,
                                    _Src(do_slc, HEAD_V), dq_n, "slc_bwd")
    dq_n, (dk_w,), dv_w = _attn_bwd(win, nsa_q, win_k, win_v, None, None, _Src(o_win, HEAD_V), l_win, lr_win,
                                    _Src(do_win, HEAD_V), dq_n, "win_bwd")
    dak, dpk_lo, dpk_hi, dw1k_lo, dw1k_hi, g_w2k = _compress_bwd(
        *ck_args, pre_k, _shift_down(pre_k), dk_cmp, _shift_down(dk_cmp), "compress_k_bwd")
    dav, dpv_lo, dpv_hi, dw1v_lo, dw1v_hi, g_w2v = _compress_bwd(
        *cv_args, pre_v, _shift_down(pre_v), dv_cmp, _shift_down(dv_cmp), "compress_v_bwd")
    g_pe_k = jnp.concatenate([dpk_lo.reshape(CMP_STRIDE, NSA_DK), dpk_hi.reshape(CMP_STRIDE, NSA_DK)], axis=0)
    g_pe_v = jnp.concatenate([dpv_lo.reshape(CMP_STRIDE, HEAD_V), dpv_hi.reshape(CMP_STRIDE, HEAD_V)], axis=0)
    g_w1k = jnp.concatenate([dw1k_lo, dw1k_hi], axis=0)[:, :NSA_DK]
    g_w1v = jnp.concatenate([dw1v_lo, dw1v_hi], axis=0)
    dk_c = _pad_cols(dak.reshape(s, NSA_DK), 256)
    dv_c = dav.reshape(s, HEAD_V)

    dq_m, (dk_nope, dk_pe), dv_m = _attn_bwd(mla, mla_q, mla_k, mla_v, None, None, _Src(o_mla, HEAD_V), l_mla,
                                             lr_mla, _Src(do_cat, HEAD_V), None, "mla_bwd")
    dq_lin = _rope_bwd_q(dq_m, cs, sn)
    dkv_lin, d_krope = _rope_bwd_k(dk_nope, dk_pe, dv_m, cs, sn)
    g_w_uq_p = _mm(cqn, dq_lin, "mla_q_dw", mode="tn")
    dcqn = _mm(dq_lin, w_uq_p, "mla_q_dx", mode="nt")
    g_w_ukv_p = _mm(ckvn, dkv_lin, "mla_kv_dw", mode="tn")
    dckvn = _mm(dkv_lin, w_ukv_p, "mla_kv_dx", mode="nt")
    dc_q, g_q_norm = _rms_bwd(c_q, w["q_norm_g"], rstd_q, dcqn, None, "norm_q_bwd")
    dc_kv, g_kv_norm = _rms_bwd(c_kv, w["kv_norm_g"], rstd_kv, dckvn, None, "norm_kv_bwd")
    g_w_uq = g_w_uq_p.reshape(512, MLA_HEADS, 256)[:, :, :192].reshape(512, MLA_HEADS * 192)
    g_w_ukv = t_(g_w_ukv_p.reshape(512, 2, MLA_HEADS, 128), (0, 2, 1, 3)).reshape(512, 2048)

    dhp = jnp.concatenate(
        [dc_q, dc_kv, dq_n, dk_c, dk_s, dk_w, d_krope, dv_c, dv_s, dv_w, dgl,
         jnp.zeros((s, PAD["q_mem"] - (PAD["g_nsa"] + LANE)), F32), dq_mem, dz], axis=1).astype(BF16)
    grads = dict(q_norm_g=g_q_norm, w_uq=g_w_uq, kv_norm_g=g_kv_norm,
                 w_ukv=g_w_ukv, cmp_pe_k=g_pe_k, cmp_pe_v=g_pe_v, cmp_w1k=g_w1k, cmp_w2k=g_w2k[:NSA_DK, :NSA_DK],
                 cmp_w1v=g_w1v, cmp_w2v=g_w2v, mem_norm_g=g_mem_norm, w_mem_kv=g_w_mem_kv, w_out=g_w_out,
                 final_norm_g=g_final.reshape(-1))
    if hooks is None:
        g_w_in_t = _w_in_grad_slots(_mm(dhp, xn, "in_proj_dw", mode="tn", wide=2048))
        dxn = _mm(dhp, w_in_p, "in_proj_dx", wide=2048)
    else:
        g_w_in_p, *hooks.received = _mm(dhp, xn, "in_proj_dw", mode="tn", wide=2048, side=hooks.reduce_side(grads))
        g_w_in_t = _w_in_grad_slots(g_w_in_p)
        dxn, hooks.received_w_in = _mm(dhp, w_in_p, "in_proj_dx", wide=2048, side=hooks.reduce_side_w_in(g_w_in_t))
    grad_x, g_norm = _rms_bwd(_Src(x, D_MODEL), w["norm_g"], rstd_x, dxn, dy, "norm_x_bwd")
    grads.update(norm_g=g_norm, w_in_t=g_w_in_t)
    return loss[0, 0], grad_x, grads


def kernel(x, mem, norm_g, w_in, q_norm_g, w_uq, kv_norm_g, w_ukv, cmp_pe_k, cmp_pe_v, cmp_w1k, cmp_w2k, cmp_w1v, cmp_w2v, mem_norm_g, w_mem_kv, w_out, final_norm_g, loss_target, m_norm_g, m_w_in, m_q_norm_g, m_w_uq, m_kv_norm_g, m_w_ukv, m_cmp_pe_k, m_cmp_pe_v, m_cmp_w1k, m_cmp_w2k, m_cmp_w1v, m_cmp_w2v, m_mem_norm_g, m_w_mem_kv, m_w_out, m_final_norm_g, v_norm_g, v_w_in, v_q_norm_g, v_w_uq, v_kv_norm_g, v_w_ukv, v_cmp_pe_k, v_cmp_pe_v, v_cmp_w1k, v_cmp_w2k, v_cmp_w1v, v_cmp_w2v, v_mem_norm_g, v_w_mem_kv, v_w_out, v_final_norm_g):
    args = dict(locals())
    wts = {n: args[n] for n in WEIGHTS}
    loc = {n: (a if n == "final_norm_g" else a[0]) for n, a in wts.items()}

    def to_x(n, a):
        return a.T if n == "w_in" else a

    split = [1 if n == "w_in" else 0 for n in SHARDED]
    rest = [n for n in SHARDED if n != "w_in"]
    chip = 2 * lax.axis_index("x") + lax.axis_index("y")
    core = lax.axis_index("c").astype(jnp.int32).reshape(1)
    own = {n: to_x(n, loc[n]).astype(BF16) for n in SHARDED}

    def with_own_slot(gw, a):
        return lax.dynamic_update_slice(gw, a[None], (chip, 0, 0))

    def slots(n, a):
        if n == "w_in":
            return a
        if SHARD_AXIS[n] == 0:
            return a.reshape(4, a.shape[0] // 4, a.shape[1])
        width = a.shape[1] // 4
        return jnp.stack([a[:, j * width:(j + 1) * width] for j in range(4)])

    def pair_sums(names, grads, name):
        axes = [1 if n == "w_in" else 0 for n in names]
        gs = [slots(n, a) for n, a in zip(names, grads)]
        theirs = _pair_exchange(gs, axes, name)
        return [_pair_sum(a, b, core, ax, "pair_sum_" + n) for n, a, b, ax in zip(names, gs, theirs, axes)]

    class Hooks:
        gather_side = _gather_side([own[n] for n in rest], [0] * len(rest))
        received = None

        @staticmethod
        def weights(gathered):
            out = {}
            for n, gw in zip(rest, gathered):
                gw = with_own_slot(gw, own[n])
                if SHARD_AXIS[n] == 0:
                    out[n] = gw.reshape(4 * gw.shape[1], gw.shape[2])
                else:
                    out[n] = jnp.concatenate([gw[j] for j in range(4)], axis=1)
            return out

        @staticmethod
        def reduce_side(grads):
            return _chip_side(pair_sums(rest, [grads[n] for n in rest], "pair_exchange_rest"))

        @staticmethod
        def reduce_side_w_in(g_w_in_t):
            return _chip_side(pair_sums(["w_in"], [g_w_in_t], "pair_exchange_w_in"))

    hooks = Hooks()

    start = {n: loc[n].reshape(1, -1) if loc[n].ndim == 1 else loc[n] for n in REPLICATED}
    start["w_in_t"] = with_own_slot(_gather_shards([own["w_in"]], [1])[0], own["w_in"])
    loss, grad_x, g = _local_step(x[0], mem[0], loss_target[0], start, hooks)
    loss = lax.psum(loss, ("x", "y", "c"))

    from_chips = dict(zip(rest, hooks.received), w_in=hooks.received_w_in)
    sums = [_chip_sum(from_chips[n], core, ax, "chip_sum_" + n) for n, ax in zip(SHARDED, split)]
    g_sh = _half_exchange([a for a, _ in sums], [b for _, b in sums], split)

    n_rep = sum(int(np.prod(loc[n].shape)) for n in REPLICATED)
    rows_rep = -(-n_rep // (8 * LANE)) * 8

    def rep_pack(parts):
        flat = jnp.concatenate([p.reshape(-1) for p in parts])
        return jnp.pad(flat, (0, rows_rep * LANE - n_rep)).reshape(rows_rep, LANE)

    g_rep = _sum_slots(_gather_all(rep_pack([g[n] for n in REPLICATED])), "replica_sum")
    d_rp, m_rp, v_rp = _adamw(rep_pack([wts[n] for n in REPLICATED]), g_rep,
                              rep_pack([args["m_" + n] for n in REPLICATED]),
                              rep_pack([args["v_" + n] for n in REPLICATED]), "adamw_replicated")

    def rep_unpack(buf):
        flat, out, o = buf.reshape(-1), {}, 0
        for n in REPLICATED:
            size = int(np.prod(wts[n].shape))
            out[n] = flat[o:o + size].reshape(wts[n].shape)
            o += size
        return out

    outs = {k: rep_unpack(b) for k, b in (("g", g_rep), ("d", d_rp), ("m", m_rp), ("v", v_rp))}
    for n, gn in zip(SHARDED, g_sh):
        d, mo, vo = _adamw(to_x(n, loc[n]), gn, to_x(n, args["m_" + n][0]), to_x(n, args["v_" + n][0]),
                           "adamw_" + n)
        for k, a in (("g", gn), ("d", d), ("m", mo), ("v", vo)):
            outs[k][n] = to_x(n, a).reshape(wts[n].shape)

    return (loss, grad_x[None], *[outs["g"][n] for n in WEIGHTS], *[outs["d"][n] for n in WEIGHTS],
            *[outs["m"][n] for n in WEIGHTS], *[outs["v"][n] for n in WEIGHTS])
```

```python
from typing import NamedTuple

import numpy as np
import jax
import jax.numpy as jnp
from jax import lax
from jax.experimental import pallas as pl
from jax.experimental.pallas import tpu as pltpu

F32 = jnp.float32
BF16 = jnp.bfloat16
MESH = pl.DeviceIdType.MESH

D_MODEL = 2048
EPS = 1e-6
LANE = 128
HEAD_V = 128
MLA_HEADS = 8
NSA_HEADS = 4
MEM_HEADS = 4
NSA_DK = 192
CMP_STRIDE = 16
CMP_LEN = 32
SLC_LEN = 64
SLC_TOPN = 16
WIN = 512
NEG = -1e30
LOG2E = 1.4426950408889634
ROPE_THETA = 10000.0
BLOCK_BYTES = 2 << 20

ORIG = dict(c_q=(0, 512), c_kv=(512, 512), k_rope=(1024, 64), z_mla=(1088, 1024),
            q_nsa=(2112, 768), k_c=(2880, 192), v_c=(3072, 128), k_s=(3200, 192),
            v_s=(3392, 128), k_w=(3520, 192), v_w=(3712, 128), g_nsa=(3840, 12),
            z_nsa=(3852, 512), q_mem=(4364, 512), z_mem=(4876, 512))
PAD = dict(c_q=0, c_kv=512, q_nsa=1024, k_c=2048, k_s=2304, k_w=2560, k_rope=2816, v_c=2944,
           v_s=3072, v_w=3200, g_nsa=3328, q_mem=3584, z=4096)
D_PAD = 6144

ADAM_LR, ADAM_B1, ADAM_B2, ADAM_EPS, ADAM_WD, ADAM_STEP = 0.001, 0.9, 0.999, 1e-08, 0.01, 10

SHARDED = ("w_in", "w_uq", "w_ukv", "cmp_w1k", "cmp_w1v", "w_mem_kv", "w_out")
SHARD_AXIS = dict(w_in=1, w_uq=1, w_ukv=1, cmp_w1k=0, cmp_w1v=0, w_mem_kv=0, w_out=0)
REPLICATED = ("norm_g", "q_norm_g", "kv_norm_g", "cmp_pe_k", "cmp_pe_v", "cmp_w2k", "cmp_w2v",
              "mem_norm_g", "final_norm_g")
WEIGHTS = ("norm_g", "w_in", "q_norm_g", "w_uq", "kv_norm_g", "w_ukv", "cmp_pe_k", "cmp_pe_v",
           "cmp_w1k", "cmp_w2k", "cmp_w1v", "cmp_w2v", "mem_norm_g", "w_mem_kv", "w_out",
           "final_norm_g")


def _pcall(kernel, **kw):
    return pl.pallas_call(kernel, **kw)


def _tile(n, pref):
    if n <= pref:
        return n
    for t in range(pref, LANE - 1, -LANE):
        if n % t == 0:
            return t
    raise ValueError((n, pref))


def _row_tile(rows, cols, itemsize=4):
    want = max(16, BLOCK_BYTES // (cols * itemsize))
    if rows <= want:
        return rows
    t = 16
    best = rows
    while t <= want:
        if rows % t == 0:
            best = t
        t *= 2
    return best


def _nt(a, b):
    return lax.dot_general(a, b, (((1,), (1,)), ((), ())), preferred_element_type=F32)


def _tn(a, b):
    return lax.dot_general(a, b, (((0,), (0,)), ((), ())), preferred_element_type=F32)


def _nn(a, b):
    return jnp.dot(a, b, preferred_element_type=F32)


def _sigmoid(x):
    return 1.0 / (1.0 + jnp.exp(-x))


class _Src(NamedTuple):
    arr: jax.Array
    width: int
    col0: int = 0
    per_head: bool = True

    def col(self, h):
        return self.col0 + h if self.per_head else self.col0


class _Side(NamedTuple):
    inputs: list
    out_shape: list
    scratch: list
    phase: object


def _mm(a, b, name, mode="nn", out_dtype=F32, second_dtype=None, wide=1024, side=None):
    if mode == "tn":
        k, m = a.shape
    else:
        m, k = a.shape
    if mode == "nt":
        n, k2 = b.shape
    else:
        k2, n = b.shape
    assert k == k2, (a.shape, b.shape, mode)
    tm, tn, tk = _tile(m, 1024), _tile(n, wide), _tile(k, 2048)
    grid = (m // tm, n // tn, k // tk)
    nk = grid[2]
    assert nk == 1 or (out_dtype == F32 and second_dtype is None)
    dot = {"nn": _nn, "nt": _nt, "tn": _tn}[mode]
    n_in = len(side.inputs) if side else 0
    n_out = len(side.out_shape) if side else 0
    n_res = 1 + (second_dtype is not None)

    def kern(*refs):
        a_ref, b_ref = refs[:2]
        res = refs[2 + n_in:2 + n_in + n_res]
        step = [pl.program_id(d) for d in range(3)]
        if side:
            side_refs = (refs[2:2 + n_in], refs[2 + n_in + n_res:2 + n_in + n_res + n_out],
                         refs[2 + n_in + n_res + n_out:])

            @pl.when((step[0] == 0) & (step[1] == 0) & (step[2] == 0))
            def _():
                side.phase("start", *side_refs)

        r = dot(a_ref[...].astype(BF16), b_ref[...].astype(BF16))
        if nk == 1:
            res[0][...] = r.astype(out_dtype)
            if n_res == 2:
                res[1][...] = r.astype(second_dtype)
        else:
            @pl.when(step[2] == 0)
            def _():
                res[0][...] = r

            @pl.when(step[2] > 0)
            def _():
                res[0][...] += r

        if side:
            @pl.when((step[0] == grid[0] - 1) & (step[1] == grid[1] - 1) & (step[2] == nk - 1))
            def _():
                side.phase("finish", *side_refs)

    a_spec = (pl.BlockSpec((tk, tm), lambda i, j, kk: (kk, i)) if mode == "tn"
              else pl.BlockSpec((tm, tk), lambda i, j, kk: (i, kk)))
    b_spec = (pl.BlockSpec((tn, tk), lambda i, j, kk: (j, kk)) if mode == "nt"
              else pl.BlockSpec((tk, tn), lambda i, j, kk: (kk, j)))
    o_spec = pl.BlockSpec((tm, tn), lambda i, j, kk: (i, j))
    out_specs = [o_spec] * n_res + [ANY] * n_out
    out_shape = [jax.ShapeDtypeStruct((m, n), out_dtype)]
    if second_dtype is not None:
        out_shape.append(jax.ShapeDtypeStruct((m, n), second_dtype))
    out_shape += list(side.out_shape) if side else []
    semantics = ("arbitrary",) * 3 if side else ("parallel", "parallel", "arbitrary")
    out = _pcall(
        kern, name=name, grid=grid, in_specs=[a_spec, b_spec] + [ANY] * n_in, out_specs=out_specs,
        out_shape=out_shape, scratch_shapes=list(side.scratch) if side else [],
        compiler_params=pltpu.CompilerParams(dimension_semantics=semantics),
    )(a, b, *(side.inputs if side else []))
    return out[0] if len(out) == 1 else out


def _rms_fwd(x, g, name):
    r, d = x.arr.shape[0], x.width
    tr = _tile(r, 512)

    def kern(x_ref, g_ref, y_ref, r_ref):
        xv = x_ref[...]
        rstd = lax.rsqrt(jnp.mean(xv * xv, axis=-1, keepdims=True) + EPS)
        y_ref[...] = (xv * rstd * g_ref[...]).astype(BF16)
        r_ref[...] = rstd

    return _pcall(
        kern, name=name, grid=(r // tr,),
        in_specs=[pl.BlockSpec((tr, d), lambda i: (i, x.col0)), pl.BlockSpec((1, d), lambda i: (0, 0))],
        out_specs=[pl.BlockSpec((tr, d), lambda i: (i, 0)), pl.BlockSpec((tr, 1), lambda i: (i, 0))],
        out_shape=[jax.ShapeDtypeStruct((r, d), BF16), jax.ShapeDtypeStruct((r, 1), F32)],
    )(x.arr, g)


def _rms_bwd(x, g, rstd, dy, add, name):
    r, d = x.arr.shape[0], x.width
    tr = _tile(r, 256)
    has_add = add is not None

    def kern(*refs):
        if has_add:
            x_ref, g_ref, r_ref, dy_ref, add_ref, dx_ref, dg_ref = refs
        else:
            x_ref, g_ref, r_ref, dy_ref, dx_ref, dg_ref = refs
        rs = r_ref[...]
        xhat = x_ref[...] * rs
        dyv = dy_ref[...]
        dyg = dyv * g_ref[...]
        c = jnp.mean(dyg * xhat, axis=-1, keepdims=True)
        dx = rs * (dyg - xhat * c)
        if has_add:
            dx = dx + add_ref[...]
        dx_ref[...] = dx
        part = jnp.sum(dyv * xhat, axis=0, keepdims=True)

        @pl.when(pl.program_id(0) == 0)
        def _():
            dg_ref[...] = part

        @pl.when(pl.program_id(0) > 0)
        def _():
            dg_ref[...] += part

    row = pl.BlockSpec((tr, d), lambda i: (i, 0))
    vec = pl.BlockSpec((1, d), lambda i: (0, 0))
    ins = [pl.BlockSpec((tr, d), lambda i: (i, x.col0)), vec, pl.BlockSpec((tr, 1), lambda i: (i, 0)), row]
    ins += [row] if has_add else []
    args = (x.arr, g, rstd, dy) + ((add,) if has_add else ())
    return _pcall(
        kern, name=name, grid=(r // tr,), in_specs=ins, out_specs=[row, vec],
        out_shape=[jax.ShapeDtypeStruct((r, d), F32), jax.ShapeDtypeStruct((1, d), F32)],
        compiler_params=pltpu.CompilerParams(dimension_semantics=("arbitrary",)),
    )(*args)


def _final_loss(x, proj, g, target):
    r, d = x.shape
    tr = _tile(r, 256)

    def kern(x_ref, p_ref, g_ref, t_ref, dy_ref, dg_ref, loss_ref):
        y = x_ref[...] + p_ref[...]
        rs = lax.rsqrt(jnp.mean(y * y, axis=-1, keepdims=True) + EPS)
        yhat = y * rs
        gv = g_ref[...]
        e = yhat * gv - t_ref[...]
        lpart = 0.5 * jnp.sum(jnp.mean(e * e, axis=-1, keepdims=True), axis=0, keepdims=True)
        dout = e * (1.0 / d)
        dyg = dout * gv
        c = jnp.mean(dyg * yhat, axis=-1, keepdims=True)
        dy_ref[...] = rs * (dyg - yhat * c)
        gpart = jnp.sum(dout * yhat, axis=0, keepdims=True)
        lrow = jnp.broadcast_to(lpart, (1, LANE))

        @pl.when(pl.program_id(0) == 0)
        def _():
            dg_ref[...] = gpart
            loss_ref[...] = lrow

        @pl.when(pl.program_id(0) > 0)
        def _():
            dg_ref[...] += gpart
            loss_ref[...] += lrow

    row = pl.BlockSpec((tr, d), lambda i: (i, 0))
    vec = pl.BlockSpec((1, d), lambda i: (0, 0))
    return _pcall(
        kern, name="final_loss", grid=(r // tr,), in_specs=[row, row, vec, row],
        out_specs=[row, vec, pl.BlockSpec((1, LANE), lambda i: (0, 0))],
        out_shape=[jax.ShapeDtypeStruct((r, d), F32), jax.ShapeDtypeStruct((1, d), F32),
                   jax.ShapeDtypeStruct((1, LANE), F32)],
        compiler_params=pltpu.CompilerParams(dimension_semantics=("arbitrary",)),
    )(x, proj, g, target)


def _rope_fwd(x, cs, sn, nh, width, off, name):
    s = x.arr.shape[0]
    tr = _tile(s, 512)

    def kern(x_ref, c_ref, s_ref, o_ref):
        cv, sv = c_ref[...], s_ref[...]
        for h in range(nh):
            b = h * width
            if off:
                o_ref[:, b:b + off] = x_ref[:, b:b + off].astype(BF16)
            xr = x_ref[:, b + off:b + off + LANE]
            o_ref[:, b + off:b + off + LANE] = (xr * cv + pltpu.roll(xr, 32, 1) * sv).astype(BF16)

    tab = pl.BlockSpec((tr, LANE), lambda i: (i, 0))
    return _pcall(
        kern, name=name, grid=(s // tr,),
        in_specs=[pl.BlockSpec((tr, nh * width), lambda i: (i, x.col0)), tab, tab],
        out_specs=pl.BlockSpec((tr, nh * width), lambda i: (i, 0)),
        out_shape=jax.ShapeDtypeStruct((s, nh * width), BF16),
    )(x.arr, cs, sn)


def _rope_grad(d, cv, sv):
    g2 = d * sv
    g2 = g2 + pltpu.roll(g2, 64, 1)
    lane = lax.broadcasted_iota(jnp.int32, d.shape, 1)
    return jnp.where(lane < 64, d * cv + pltpu.roll(g2, 32, 1), 0.0)


def _rope_bwd_q(dq, cs, sn):
    s, w = dq.shape
    tr = _tile(s, 512)
    nh = w // 256

    def kern(d_ref, c_ref, s_ref, o_ref):
        cv, sv = c_ref[...], s_ref[...]
        for h in range(nh):
            b = h * 256
            o_ref[:, b:b + LANE] = d_ref[:, b:b + LANE]
            o_ref[:, b + LANE:b + 256] = _rope_grad(d_ref[:, b + LANE:b + 256], cv, sv)

    row = pl.BlockSpec((tr, w), lambda i: (i, 0))
    tab = pl.BlockSpec((tr, LANE), lambda i: (i, 0))
    return _pcall(kern, name="rope_bwd_q", grid=(s // tr,), in_specs=[row, tab, tab], out_specs=row,
                  out_shape=jax.ShapeDtypeStruct((s, w), F32))(dq, cs, sn)


def _rope_bwd_k(dk_nope, dk_pe, dv, cs, sn):
    s, w = dk_nope.shape
    tr = _tile(s, 512)

    def kern(dk_ref, dp_ref, dv_ref, c_ref, s_ref, okv_ref, okr_ref):
        okv_ref[:, :w] = dk_ref[...]
        okv_ref[:, w:] = dv_ref[...]
        okr_ref[...] = _rope_grad(dp_ref[...], c_ref[...], s_ref[...])

    tab = pl.BlockSpec((tr, LANE), lambda i: (i, 0))
    wide = pl.BlockSpec((tr, w), lambda i: (i, 0))
    return _pcall(
        kern, name="rope_bwd_k", grid=(s // tr,), in_specs=[wide, tab, wide, tab, tab],
        out_specs=[pl.BlockSpec((tr, 2 * w), lambda i: (i, 0)), tab],
        out_shape=[jax.ShapeDtypeStruct((s, 2 * w), F32), jax.ShapeDtypeStruct((s, LANE), F32)],
    )(dk_nope, dk_pe, dv, cs, sn)


class _Attn:
    def __init__(self, mode, s, sk, heads, dk):
        self.mode, self.s, self.sk, self.h, self.dk = mode, s, sk, heads, dk
        self.scale = {"mla": 192 ** -0.5, "mem": 128 ** -0.5}.get(mode, NSA_DK ** -0.5)
        self.tb = min(256, s)
        self.nb = s // self.tb
        self.nsub = 2 if self.nb % 2 == 0 else 1
        self.tq = self.tb * self.nsub
        self.fchains = 1
        self.nq = s // self.tq
        self.qpb = 4 if self.nq % 4 == 0 and mode == "mla" else 2
        self.causal = mode in ("mla", "slc")
        if self.causal:
            self.tk = self.tq
        elif mode == "win":
            self.tk = WIN + self.tb
        else:
            self.tk = sk
        self.tkb = min(512, sk)
        self.ksub = 1
        self.kb = self.tkb // self.ksub
        self.ncmp = s // CMP_STRIDE - 1

    def mask_bias(self, t, n, h, selx, diag):
        m = self.mode
        if m == "mla":
            return (n <= t) if diag else None, None
        if m == "mem":
            return None, None
        slope = jnp.where(h == 0, 0.25, jnp.where(h == 1, 0.0625, jnp.where(h == 2, 0.015625, 0.00390625)))
        slope = slope.astype(F32) * LOG2E
        if m == "cmp":
            mask = (n * CMP_STRIDE + (CMP_LEN - 1) <= t) & (n < self.ncmp)
            pos = n.astype(F32) * float(CMP_STRIDE) + (CMP_LEN - 1) / 2.0
            return mask, slope * pos
        rel = t - n
        if m == "slc":
            return (rel >= 0) if diag else None, slope * n.astype(F32)
        return (rel >= 0) & (rel < WIN), slope * n.astype(F32)


def _scores(cfg, s_raw, t, n, h, selx, diag, lse=None):
    s = s_raw * (cfg.scale * LOG2E)
    mask, key_term = cfg.mask_bias(t, n, h, selx, diag)
    if key_term is not None:
        s = s + key_term
    if selx is not None:
        s = s + selx
    if lse is None:
        if mask is not None:
            s = jnp.where(mask, s, NEG)
        return s, mask
    p = jnp.exp2(jnp.minimum(s - lse, 0.0))
    if mask is not None:
        p = jnp.where(mask, p, 0.0)
    return p, mask


def _block_of_key(k0, tk, keys_on_rows, value=NEG):
    shape = (tk, LANE) if keys_on_rows else (LANE, tk)
    n = lax.broadcasted_iota(jnp.int32, shape, 0 if keys_on_rows else 1) + k0
    j = lax.broadcasted_iota(jnp.int32, shape, 1 if keys_on_rows else 0)
    return jnp.where((n >> 6) == j, value, 0.0).astype(BF16)


def _to_row(col):
    t = col.shape[0]
    return jnp.transpose(jnp.broadcast_to(col, (t, LANE)))[0:1, :]


def _load_keys(k_refs, rows):
    parts = [r[rows, :].astype(BF16) for r in k_refs]
    return parts[0] if len(parts) == 1 else jnp.concatenate(parts, axis=1)


def _attn_fwd(cfg, q, ks, v, sel, name):
    s, tq, tk, nsub = cfg.s, cfg.tq, cfg.tk, cfg.fchains
    tb = tq // nsub
    per = tb // cfg.tb
    has_sel = sel is not None
    nkp = len(ks)
    qpb = cfg.qpb
    assert cfg.causal and tq == tk and qpb % 2 == 0 and cfg.nq % qpb == 0

    def kern(*refs):
        q_ref, k_refs, v_ref = refs[0], refs[1:1 + nkp], refs[1 + nkp]
        sel_ref = refs[2 + nkp] if has_sel else None
        o_ref, lc_ref, lr_ref = refs[2 + nkp + has_sel:5 + nkp + has_sel]
        buf_a, buf_b = refs[-2:]
        h, g = pl.program_id(0), pl.program_id(1)

        def block(b):
            rows = [slice(b * tq + r * tb, b * tq + (r + 1) * tb) for r in range(nsub)]
            qs = [q_ref[p, :].astype(BF16) for p in rows]
            ts = [(qpb * g + b) * tq + r * tb + lax.broadcasted_iota(jnp.int32, (tb, 1), 0) for r in range(nsub)]
            sels = [sel_ref[p, :].astype(BF16) for p in rows] if has_sel else None
            return rows, qs, ts, sels

        def scores_into(buf, blk, c):
            kk = _load_keys(k_refs, pl.ds(pl.multiple_of(c * tk, tk), tk))
            for r in range(nsub):
                buf[r] = _nt(blk[1][r], kk)

        def consume(buf, blk, c, carry, diag):
            _, _, ts, sels = blk
            k0 = pl.multiple_of(c * tk, tk)
            vv = v_ref[pl.ds(k0, tk), :].astype(BF16)
            emat = _block_of_key(k0, tk, False) if has_sel else None
            n = k0 + lax.broadcasted_iota(jnp.int32, (1, tk), 1)
            new = []
            for r in range(nsub):
                m, l, acc = carry[r]
                selx = _nn(sels[r], emat) if has_sel else None
                sc, mask = _scores(cfg, buf[r], ts[r], n, h, selx, diag)
                m_new = jnp.maximum(m, jnp.max(sc, axis=1, keepdims=True))
                alpha = jnp.exp2(m - m_new)
                p = jnp.exp2(sc - m_new)
                if mask is not None:
                    p = jnp.where(mask, p, 0.0)
                l = alpha * l + jnp.sum(p, axis=1, keepdims=True)
                new.append((m_new, l, alpha * acc + _nn(p.astype(BF16), vv)))
            return tuple(new)

        def finish(blk, b, carry):
            for r, (m, l, acc) in enumerate(carry):
                o_ref[blk[0][r], :] = acc / (l + 1e-20)
                lse = m + jnp.log(l + 1e-20) * LOG2E
                lc_ref[0, blk[0][r], :] = lse
                for u in range(per):
                    lr_ref[0, (b * nsub + r) * per + u] = _to_row(lse[u * cfg.tb:(u + 1) * cfg.tb])

        def pairs(blk, first, other):
            def pair(p, cr):
                scores_into(other, blk, 2 * p + 1)
                cr = consume(first, blk, 2 * p, cr, False)
                scores_into(first, blk, 2 * p + 2)
                return consume(other, blk, 2 * p + 1, cr, False)
            return pair

        init = ((jnp.full((tb, 1), NEG, F32), jnp.zeros((tb, 1), F32), jnp.zeros((tb, HEAD_V), F32)),) * nsub
        cur, oth = buf_a, buf_b
        blk = block(0)
        scores_into(cur, blk, 0)
        for b in range(qpb):
            full = qpb * g + b
            carry = lax.fori_loop(0, (qpb // 2) * g + b // 2, pairs(blk, cur, oth), init)
            nxt = block(b + 1) if b + 1 < qpb else None
            if b % 2 == 0:
                if nxt:
                    scores_into(oth, nxt, 0)
                finish(blk, b, consume(cur, blk, full, carry, True))
                cur, oth = oth, cur
            else:
                scores_into(oth, blk, full)
                carry = consume(cur, blk, full - 1, carry, False)
                if nxt:
                    scores_into(cur, nxt, 0)
                finish(blk, b, consume(oth, blk, full, carry, True))
            blk = nxt

    rows_step = qpb * tq
    ins = [pl.BlockSpec((rows_step, q.width), lambda h, g: (g, q.col(h)))]
    ins += [pl.BlockSpec((cfg.sk, p.width), lambda h, g, p=p: (0, p.col(h))) for p in ks]
    ins += [pl.BlockSpec((cfg.sk, HEAD_V), lambda h, g: (0, v.col(h)))]
    args = [q.arr] + [p.arr for p in ks] + [v.arr]
    if has_sel:
        ins.append(pl.BlockSpec((rows_step, LANE), lambda h, g: (g, 0)))
        args.append(sel)
    return _pcall(
        kern, name=name, grid=(cfg.h, cfg.nq // qpb), in_specs=ins,
        out_specs=[pl.BlockSpec((rows_step, HEAD_V), lambda h, g: (g, h)),
                   pl.BlockSpec((1, rows_step, 1), lambda h, g: (h, g, 0)),
                   pl.BlockSpec((1, rows_step // cfg.tb, 1, cfg.tb), lambda h, g: (h, g, 0, 0))],
        out_shape=[jax.ShapeDtypeStruct((s, cfg.h * HEAD_V), F32),
                   jax.ShapeDtypeStruct((cfg.h, s, 1), F32),
                   jax.ShapeDtypeStruct((cfg.h, cfg.nb, 1, cfg.tb), F32)],
        scratch_shapes=[pltpu.VMEM((nsub, tb, tk), F32)] * 2,
        compiler_params=pltpu.CompilerParams(dimension_semantics=("parallel", "parallel")),
    )(*args)


def _attn_dq(cfg, q, ks, v, sel, o, lse, do, dq_in, name):
    s, tq, tk, dk, nsub = cfg.s, cfg.tq, cfg.tk, cfg.dk, cfg.fchains
    tb = tq // nsub
    per = tb // cfg.tb
    has_sel = sel is not None
    has_in = dq_in is not None
    nkp = len(ks)

    def kern(*refs):
        refs = list(refs)
        q_ref, k_refs, v_ref = refs[0], refs[1:1 + nkp], refs[1 + nkp]
        p0 = 2 + nkp
        sel_ref = refs[p0] if has_sel else None
        p0 += has_sel
        o_ref, l_ref, do_ref = refs[p0:p0 + 3]
        p0 += 3
        in_ref = refs[p0] if has_in else None
        p0 += has_in
        dq_ref, dr_ref = refs[p0:p0 + 2]
        sa, pa, sb, pb = refs[-4:]
        h, g = pl.program_id(0), pl.program_id(1)

        def block(b):
            rows = [slice(b * tq + r * tb, b * tq + (r + 1) * tb) for r in range(nsub)]
            qs = [q_ref[p, :].astype(BF16) for p in rows]
            ts = [(qpb * g + b) * tq + r * tb + lax.broadcasted_iota(jnp.int32, (tb, 1), 0) for r in range(nsub)]
            sels = [sel_ref[p, :].astype(BF16) for p in rows] if has_sel else None
            dvecs, dobs, lses = [], [], []
            for r, p in enumerate(rows):
                dov = do_ref[p, :]
                dvec = jnp.sum(dov * o_ref[p, :], axis=1, keepdims=True)
                for u in range(per):
                    dr_ref[0, (b * nsub + r) * per + u] = _to_row(dvec[u * cfg.tb:(u + 1) * cfg.tb])
                dvecs.append(dvec)
                dobs.append(dov.astype(BF16))
                lses.append(l_ref[0, p, :])
            return rows, qs, ts, sels, dvecs, dobs, lses

        def products_into(sbuf, pbuf, blk, c):
            rows = pl.ds(pl.multiple_of(c * tk, tk), tk)
            kk, vv = _load_keys(k_refs, rows), v_ref[rows, :].astype(BF16)
            for r in range(nsub):
                sbuf[r] = _nt(blk[1][r], kk)
                pbuf[r] = _nt(blk[5][r], vv)

        def consume(sbuf, pbuf, blk, c, accs, diag):
            _, _, ts, sels, dvecs, _, lses = blk
            k0 = pl.multiple_of(c * tk, tk)
            kk = _load_keys(k_refs, pl.ds(k0, tk))
            emat = _block_of_key(k0, tk, False) if has_sel else None
            n = k0 + lax.broadcasted_iota(jnp.int32, (1, tk), 1)
            new = []
            for r in range(nsub):
                selx = _nn(sels[r], emat) if has_sel else None
                p, _ = _scores(cfg, sbuf[r], ts[r], n, h, selx, diag, lses[r])
                ds = p * (pbuf[r] - dvecs[r])
                new.append(accs[r] + _nn(ds.astype(BF16), kk))
            return tuple(new)

        def finish(blk, accs):
            for r, p in enumerate(blk[0]):
                dq_ref[p, :] = accs[r] * cfg.scale + in_ref[p, :] if has_in else accs[r] * cfg.scale

        def pairs(blk, first, other):
            def pair(p, ac):
                products_into(*other, blk, 2 * p + 1)
                ac = consume(*first, blk, 2 * p, ac, False)
                products_into(*first, blk, 2 * p + 2)
                return consume(*other, blk, 2 * p + 1, ac, False)
            return pair

        zero = (jnp.zeros((tb, dk), F32),) * nsub
        cur, oth = (sa, pa), (sb, pb)
        blk = block(0)
        products_into(*cur, blk, 0)
        for b in range(qpb):
            full = qpb * g + b
            accs = lax.fori_loop(0, (qpb // 2) * g + b // 2, pairs(blk, cur, oth), zero)
            nxt = block(b + 1) if b + 1 < qpb else None
            if b % 2 == 0:
                if nxt:
                    products_into(*oth, nxt, 0)
                finish(blk, consume(*cur, blk, full, accs, True))
                cur, oth = oth, cur
            else:
                products_into(*oth, blk, full)
                accs = consume(*cur, blk, full - 1, accs, False)
                if nxt:
                    products_into(*cur, nxt, 0)
                finish(blk, consume(*oth, blk, full, accs, True))
            blk = nxt

    qpb = cfg.qpb
    assert cfg.causal and tq == tk and qpb % 2 == 0 and cfg.nq % qpb == 0
    rows_step = qpb * tq
    qs = pl.BlockSpec((rows_step, dk), lambda h, g: (g, h))
    ins = [pl.BlockSpec((rows_step, q.width), lambda h, g: (g, q.col(h)))]
    ins += [pl.BlockSpec((cfg.sk, p.width), lambda h, g, p=p: (0, p.col(h))) for p in ks]
    ins += [pl.BlockSpec((cfg.sk, HEAD_V), lambda h, g: (0, v.col(h)))]
    args = [q.arr] + [p.arr for p in ks] + [v.arr]
    if has_sel:
        ins.append(pl.BlockSpec((rows_step, LANE), lambda h, g: (g, 0)))
        args.append(sel)
    ins += [pl.BlockSpec((rows_step, HEAD_V), lambda h, g: (g, o.col(h))),
            pl.BlockSpec((1, rows_step, 1), lambda h, g: (h, g, 0)),
            pl.BlockSpec((rows_step, HEAD_V), lambda h, g: (g, do.col(h)))]
    args += [o.arr, lse, do.arr]
    if has_in:
        ins.append(qs)
        args.append(dq_in)
    return _pcall(
        kern, name=name, grid=(cfg.h, cfg.nq // qpb), in_specs=ins,
        out_specs=[qs, pl.BlockSpec((1, rows_step // cfg.tb, 1, cfg.tb), lambda h, g: (h, g, 0, 0))],
        out_shape=[jax.ShapeDtypeStruct((s, cfg.h * dk), F32),
                   jax.ShapeDtypeStruct((cfg.h, cfg.nb, 1, cfg.tb), F32)],
        scratch_shapes=[pltpu.VMEM((nsub, tb, tk), F32)] * 4,
        compiler_params=pltpu.CompilerParams(dimension_semantics=("parallel", "parallel")),
    )(*args)


def _attn_dkv(cfg, q, ks, v, selt, lse_r, d_r, do, name):
    s, tq, tkb, dk, kb, ksub = cfg.s, cfg.tb, cfg.tkb, cfg.dk, cfg.kb, cfg.ksub
    nq = cfg.nb
    has_sel = selt is not None
    nkp = len(ks)
    outs = list(ks) + [v]

    def kern(*refs):
        k_refs, v_ref = refs[:nkp], refs[nkp]
        q_ref, do_ref, lr_ref, dr_ref = refs[nkp + 1:nkp + 5]
        st_ref = refs[nkp + 5] if has_sel else None
        out_refs = refs[nkp + 5 + has_sel:2 * nkp + 6 + has_sel]
        sa, pa, sb, pb = refs[-4:]
        j, h = pl.program_id(0), pl.program_id(1)
        k0 = j * tkb
        part = [slice(u * kb, (u + 1) * kb) for u in range(ksub)]
        kks = [_load_keys(k_refs, p) for p in part]
        vvs = [v_ref[p, :].astype(BF16) for p in part]
        ns = [k0 + u * kb + lax.broadcasted_iota(jnp.int32, (kb, 1), 0) for u in range(ksub)]
        emats = [_block_of_key(k0 + u * kb, kb, True) for u in range(ksub)] if has_sel else None

        def load_q(i):
            rows = pl.ds(pl.multiple_of(i * tq, tq), tq)
            return q_ref[rows, :].astype(BF16), do_ref[rows, :].astype(BF16)

        def products_into(sbuf, pbuf, i):
            qi, doi = load_q(i)
            for u in range(ksub):
                sbuf[u] = _nt(kks[u], qi)
                pbuf[u] = _nt(vvs[u], doi)

        def consume(sbuf, pbuf, i, carry):
            qi, doi = load_q(i)
            t = i * tq + lax.broadcasted_iota(jnp.int32, (1, tq), 1)
            selt_i = st_ref[i].astype(BF16) if has_sel else None
            new = []
            for u in range(ksub):
                dk_acc, dv_acc = carry[u]
                selx = _nn(emats[u], selt_i) if has_sel else None
                pt, _ = _scores(cfg, sbuf[u], t, ns[u], h, selx, True, lr_ref[0, i])
                dv_acc = dv_acc + _nn(pt.astype(BF16), doi)
                dst = pt * (pbuf[u] - dr_ref[0, i])
                new.append((dk_acc + _nn(dst.astype(BF16), qi), dv_acc))
            return tuple(new)

        if cfg.causal:
            first, count = k0 // tq, nq - k0 // tq
        elif cfg.mode == "win":
            first = k0 // tq
            count = jnp.minimum((k0 + tkb + WIN - 2) // tq + 1, nq) - first
        else:
            first, count = 0, nq

        def pair(p, cr):
            i0 = first + 2 * p
            products_into(sb, pb, i0 + 1)
            cr = consume(sa, pa, i0, cr)
            products_into(sa, pa, i0 + 2)
            return consume(sb, pb, i0 + 1, cr)

        carry = ((jnp.zeros((kb, dk), F32), jnp.zeros((kb, HEAD_V), F32)),) * ksub
        products_into(sa, pa, first)
        carry = lax.fori_loop(0, count // 2 - 1, pair, carry)
        last = first + count - 2
        products_into(sb, pb, last + 1)
        carry = consume(sa, pa, last, carry)
        carry = consume(sb, pb, last + 1, carry)
        for u, (dk_acc, dv_acc) in enumerate(carry):
            vals, off = [], 0
            for p in ks:
                vals.append(dk_acc[:, off:off + p.width] * cfg.scale)
                off += p.width
            vals.append(dv_acc)
            for src, ref, val in zip(outs, out_refs, vals):
                if src.per_head:
                    ref[part[u], :] = val
                else:
                    @pl.when(h == 0)
                    def _(ref=ref, val=val, u=u):
                        ref[part[u], :] = val

                    @pl.when(h > 0)
                    def _(ref=ref, val=val, u=u):
                        ref[part[u], :] += val

    rowv = pl.BlockSpec((1, nq, 1, tq), lambda j, h: (h, 0, 0, 0))
    ins = [pl.BlockSpec((tkb, p.width), lambda j, h, p=p: (j, p.col(h))) for p in ks]
    ins += [pl.BlockSpec((tkb, HEAD_V), lambda j, h: (j, v.col(h))),
            pl.BlockSpec((s, q.width), lambda j, h: (0, q.col(h))),
            pl.BlockSpec((s, HEAD_V), lambda j, h: (0, do.col(h))), rowv, rowv]
    args = [p.arr for p in ks] + [v.arr, q.arr, do.arr, lse_r, d_r]
    if has_sel:
        ins.append(pl.BlockSpec((nq, LANE, tq), lambda j, h: (0, 0, 0)))
        args.append(selt)
    out_specs = [pl.BlockSpec((tkb, p.width), lambda j, h, p=p: (j, h if p.per_head else 0)) for p in outs]
    out_shape = [jax.ShapeDtypeStruct((cfg.sk, (cfg.h if p.per_head else 1) * p.width), F32) for p in outs]
    assert nq % 2 == 0 and (cfg.mode in ("cmp", "mem") or tkb % (2 * tq) == 0), (nq, tkb, tq)
    return _pcall(
        kern, name=name, grid=(cfg.sk // tkb, cfg.h), in_specs=ins, out_specs=out_specs, out_shape=out_shape,
        scratch_shapes=[pltpu.VMEM((ksub, kb, tq), F32)] * 4,
        compiler_params=pltpu.CompilerParams(dimension_semantics=("parallel", "arbitrary")),
    )(*args)


def _attn_dkv_flat(cfg, q, ks, v, selt, lse_r, d_r, do, name):
    s, tq, tkb, dk, kb, ksub = cfg.s, cfg.tb, cfg.tkb, cfg.dk, cfg.kb, cfg.ksub
    nq = cfg.nb
    has_sel = selt is not None
    nkp = len(ks)
    outs = list(ks) + [v]
    assert nq % 2 == 0 and tkb % (2 * tq) == 0, (nq, tkb, tq)
    steps = []
    for j in range(cfg.sk // tkb):
        first = j * tkb // tq
        stop = nq if cfg.causal else min((j * tkb + tkb + WIN - 2) // tq + 1, nq)
        steps += [(j, i0) for i0 in range(first, stop, 2)]
    n_pairs = len(steps)
    steps.append(steps[-1])
    tab_j = jnp.asarray(np.array([p[0] for p in steps], np.int32))
    tab_i = jnp.asarray(np.array([p[1] for p in steps], np.int32))

    def kern(tj_ref, ti_ref, *refs):
        k_refs, v_ref = refs[:nkp], refs[nkp]
        q_ref, do_ref, lr_ref, dr_ref = refs[nkp + 1:nkp + 5]
        st_ref = refs[nkp + 5] if has_sel else None
        out_refs = refs[nkp + 5 + has_sel:2 * nkp + 6 + has_sel]
        sa, pa, sb, pb = refs[-4:]
        h = pl.program_id(0)
        for src, ref in zip(outs, out_refs):
            if src.per_head:
                ref[...] = jnp.zeros_like(ref)
            else:
                @pl.when(h == 0)
                def _(ref=ref):
                    ref[...] = jnp.zeros_like(ref)

        def key_rows(j, u):
            return pl.ds(pl.multiple_of(j * tkb + u * kb, kb), kb)

        def load_q(i):
            rows = pl.ds(pl.multiple_of(i * tq, tq), tq)
            return q_ref[rows, :].astype(BF16), do_ref[rows, :].astype(BF16)

        def products_into(sbuf, pbuf, j, i):
            qi, doi = load_q(i)
            for u in range(ksub):
                rows = key_rows(j, u)
                sbuf[u] = _nt(_load_keys(k_refs, rows), qi)
                pbuf[u] = _nt(v_ref[rows, :].astype(BF16), doi)

        def consume(sbuf, pbuf, j, i):
            qi, doi = load_q(i)
            t = i * tq + lax.broadcasted_iota(jnp.int32, (1, tq), 1)
            selt_i = st_ref[i].astype(BF16) if has_sel else None
            res = []
            for u in range(ksub):
                k0 = j * tkb + u * kb
                n = k0 + lax.broadcasted_iota(jnp.int32, (kb, 1), 0)
                selx = _nn(_block_of_key(k0, kb, True), selt_i) if has_sel else None
                pt, _ = _scores(cfg, sbuf[u], t, n, h, selx, True, lr_ref[0, i])
                dst = pt * (pbuf[u] - dr_ref[0, i])
                res.append((_nn(dst.astype(BF16), qi), _nn(pt.astype(BF16), doi)))
            return res

        def pair(p, carry):
            j, i0 = tj_ref[p], ti_ref[p]
            products_into(sb, pb, j, i0 + 1)
            ca = consume(sa, pa, j, i0)
            products_into(sa, pa, tj_ref[p + 1], ti_ref[p + 1])
            cb = consume(sb, pb, j, i0 + 1)
            for u in range(ksub):
                rows = key_rows(j, u)
                dk_c = (ca[u][0] + cb[u][0]) * cfg.scale
                off = 0
                for src, ref in zip(ks, out_refs):
                    ref[rows, :] += dk_c[:, off:off + src.width]
                    off += src.width
                out_refs[nkp][rows, :] += ca[u][1] + cb[u][1]
            return carry

        products_into(sa, pa, tj_ref[0], ti_ref[0])
        lax.fori_loop(0, n_pairs, pair, 0)

    rowv = pl.BlockSpec((1, nq, 1, tq), lambda h, tj, ti: (h, 0, 0, 0))
    ins = [pl.BlockSpec((cfg.sk, p.width), lambda h, tj, ti, p=p: (0, p.col(h))) for p in ks]
    ins += [pl.BlockSpec((cfg.sk, HEAD_V), lambda h, tj, ti: (0, v.col(h))),
            pl.BlockSpec((s, q.width), lambda h, tj, ti: (0, q.col(h))),
            pl.BlockSpec((s, HEAD_V), lambda h, tj, ti: (0, do.col(h))), rowv, rowv]
    args = [p.arr for p in ks] + [v.arr, q.arr, do.arr, lse_r, d_r]
    if has_sel:
        ins.append(pl.BlockSpec((nq, LANE, tq), lambda h, tj, ti: (0, 0, 0)))
        args.append(selt)
    out_specs = [pl.BlockSpec((cfg.sk, p.width), lambda h, tj, ti, p=p: (0, h if p.per_head else 0))
                 for p in outs]
    out_shape = [jax.ShapeDtypeStruct((cfg.sk, (cfg.h if p.per_head else 1) * p.width), F32) for p in outs]
    grid_spec = pltpu.PrefetchScalarGridSpec(
        num_scalar_prefetch=2, grid=(cfg.h,), in_specs=ins, out_specs=out_specs,
        scratch_shapes=[pltpu.VMEM((ksub, kb, tq), F32)] * 4)
    return _pcall(kern, name=name, grid_spec=grid_spec, out_shape=out_shape,
                  compiler_params=pltpu.CompilerParams(dimension_semantics=("arbitrary",)))(tab_j, tab_i, *args)


def _all_heads(cfg, src, rows, key=False):
    if src.per_head:
        assert src.col0 % cfg.h == 0
        width, col = cfg.h * src.width, src.col0 // cfg.h
    else:
        width, col = src.width, src.col0
    return pl.BlockSpec((rows, width), (lambda i: (0, col)) if key else (lambda i: (i, col)))


def _head_cols(src, hh):
    return slice(hh * src.width, (hh + 1) * src.width) if src.per_head else slice(None)


def _key_window(cfg, i, r):
    if cfg.mode == "win":
        return pl.ds(pl.multiple_of(jnp.maximum(i * cfg.tq + r * cfg.tb - WIN, 0), cfg.tb), cfg.tk)
    return pl.ds(0, cfg.tk)


def _attn_fwd_small(cfg, q, ks, v, name, overlap=None):
    s, tq, tk, tb, nsub, nh = cfg.s, cfg.tq, cfg.tk, cfg.tb, cfg.nsub, cfg.h
    nkp = len(ks)
    select = overlap is not None
    n_s = s // SLC_LEN
    top_n = min(SLC_TOPN, n_s)

    def kern(*refs):
        q_ref, k_refs, v_ref = refs[0], refs[1:1 + nkp], refs[1 + nkp]
        ov_ref = refs[2 + nkp] if select else None
        o_ref, lc_ref, lr_ref = refs[2 + nkp + select:5 + nkp + select]
        i = pl.program_id(0)
        imps = [jnp.zeros((tb, LANE), F32)] * nsub
        for r in range(nsub):
            rows = slice(r * tb, (r + 1) * tb)
            t = i * tq + r * tb + lax.broadcasted_iota(jnp.int32, (tb, 1), 0)
            win = _key_window(cfg, i, r)
            n = win.start + lax.broadcasted_iota(jnp.int32, (1, tk), 1)
            for hh in range(nh):
                qv = q_ref[rows, hh * cfg.dk:(hh + 1) * cfg.dk].astype(BF16)
                kk = _load_keys([kr.at[:, _head_cols(p, hh)] for kr, p in zip(k_refs, ks)], win)
                vv = v_ref[win, _head_cols(v, hh)].astype(BF16)
                sc, mask = _scores(cfg, _nt(qv, kk), t, n, hh, None, True)
                m = jnp.max(sc, axis=1, keepdims=True)
                e = jnp.exp2(sc - m)
                if mask is not None:
                    e = jnp.where(mask, e, 0.0)
                l = jnp.sum(e, axis=1, keepdims=True)
                o_ref[rows, hh * HEAD_V:(hh + 1) * HEAD_V] = _nn(e.astype(BF16), vv) / (l + 1e-20)
                lse = m + jnp.log(l + 1e-20) * LOG2E
                lc_ref[hh, rows, :] = lse
                lr_ref[hh, r] = _to_row(lse)
                if select:
                    imps[r] = imps[r] + _nn((e / (l + 1e-20)).astype(BF16), ov_ref[...])
        if select:
            sel_ref, selt_ref, imp_t = refs[5 + nkp + select:8 + nkp + select]
            for r in range(nsub):
                t = i * tq + r * tb + lax.broadcasted_iota(jnp.int32, (tb, 1), 0)
                j = lax.broadcasted_iota(jnp.int32, (tb, LANE), 1)
                cur = t >> 6
                imp = jnp.where((j == 0) | (j == cur) | (j == cur - 1), 1e9, imps[r])
                imp = jnp.where(j > cur, -1e9, imp)
                imp_t[r] = jnp.transpose(imp)
                mine = imp_t[r, 0:n_s, :]
                jrow = lax.broadcasted_iota(jnp.int32, (n_s, tb), 0)

                def count(k, rank):
                    other = imp_t[r, pl.ds(k, 1), :]
                    ahead = (other > mine) | ((other == mine) & (k < jrow))
                    return rank + jnp.where(ahead, 1.0, 0.0)

                rank = lax.fori_loop(0, n_s, count, jnp.zeros((n_s, tb), F32))
                cur_t = (i * tq + r * tb + lax.broadcasted_iota(jnp.int32, (1, tb), 1)) >> 6
                rejected = jnp.where((rank < top_n) & (jrow <= cur_t), 0.0, 1.0)
                if n_s < LANE:
                    rejected = jnp.concatenate([rejected, jnp.ones((LANE - n_s, tb), F32)], axis=0)
                selt_ref[r] = rejected
                sel_ref[r * tb:(r + 1) * tb, :] = jnp.transpose(rejected)

    ins = [_all_heads(cfg, q, tq)] + [_all_heads(cfg, p, cfg.sk, True) for p in ks]
    ins += [_all_heads(cfg, v, cfg.sk, True)]
    args = [q.arr] + [p.arr for p in ks] + [v.arr]
    out_specs = [pl.BlockSpec((tq, nh * HEAD_V), lambda i: (i, 0)),
                 pl.BlockSpec((nh, tq, 1), lambda i: (0, i, 0)),
                 pl.BlockSpec((nh, nsub, 1, tb), lambda i: (0, i, 0, 0))]
    out_shape = [jax.ShapeDtypeStruct((s, nh * HEAD_V), F32), jax.ShapeDtypeStruct((nh, s, 1), F32),
                 jax.ShapeDtypeStruct((nh, cfg.nb, 1, tb), F32)]
    scratch = []
    if select:
        ins.append(pl.BlockSpec((cfg.sk, LANE), lambda i: (0, 0)))
        args.append(overlap)
        out_specs += [pl.BlockSpec((tq, LANE), lambda i: (i, 0)), pl.BlockSpec((nsub, LANE, tb), lambda i: (i, 0, 0))]
        out_shape += [jax.ShapeDtypeStruct((s, LANE), F32), jax.ShapeDtypeStruct((cfg.nb, LANE, tb), F32)]
        scratch = [pltpu.VMEM((nsub, LANE, tb), F32)]
    return _pcall(kern, name=name, grid=(cfg.nq,), in_specs=ins, out_specs=out_specs, out_shape=out_shape,
                  scratch_shapes=scratch,
                  compiler_params=pltpu.CompilerParams(dimension_semantics=("parallel",)))(*args)


def _attn_dq_small(cfg, q, ks, v, o, lse, do, dq_in, name):
    s, tq, tk, tb, nsub, nh, dk = cfg.s, cfg.tq, cfg.tk, cfg.tb, cfg.nsub, cfg.h, cfg.dk
    nkp = len(ks)
    has_in = dq_in is not None

    def kern(*refs):
        q_ref, k_refs, v_ref = refs[0], refs[1:1 + nkp], refs[1 + nkp]
        o_ref, l_ref, do_ref = refs[2 + nkp:5 + nkp]
        in_ref = refs[5 + nkp] if has_in else None
        dq_ref, dr_ref = refs[5 + nkp + has_in:7 + nkp + has_in]
        i = pl.program_id(0)
        for r in range(nsub):
            rows = slice(r * tb, (r + 1) * tb)
            t = i * tq + r * tb + lax.broadcasted_iota(jnp.int32, (tb, 1), 0)
            win = _key_window(cfg, i, r)
            n = win.start + lax.broadcasted_iota(jnp.int32, (1, tk), 1)
            for hh in range(nh):
                vcols = slice(hh * HEAD_V, (hh + 1) * HEAD_V)
                qcols = slice(hh * dk, (hh + 1) * dk)
                qv = q_ref[rows, qcols].astype(BF16)
                kk = _load_keys([kr.at[:, _head_cols(p, hh)] for kr, p in zip(k_refs, ks)], win)
                vv = v_ref[win, _head_cols(v, hh)].astype(BF16)
                dov = do_ref[rows, vcols]
                dvec = jnp.sum(dov * o_ref[rows, vcols], axis=1, keepdims=True)
                dr_ref[hh, r] = _to_row(dvec)
                p, _ = _scores(cfg, _nt(qv, kk), t, n, hh, None, True, l_ref[hh, rows, :])
                ds = p * (_nt(dov.astype(BF16), vv) - dvec)
                dq = _nn(ds.astype(BF16), kk) * cfg.scale
                dq_ref[rows, qcols] = dq + in_ref[rows, qcols] if has_in else dq

    qs = pl.BlockSpec((tq, nh * dk), lambda i: (i, 0))
    ins = [_all_heads(cfg, q, tq)] + [_all_heads(cfg, p, cfg.sk, True) for p in ks]
    ins += [_all_heads(cfg, v, cfg.sk, True)]
    ins += [_all_heads(cfg, o, tq), pl.BlockSpec((nh, tq, 1), lambda i: (0, i, 0)), _all_heads(cfg, do, tq)]
    args = [q.arr] + [p.arr for p in ks] + [v.arr, o.arr, lse, do.arr]
    if has_in:
        ins.append(qs)
        args.append(dq_in)
    return _pcall(
        kern, name=name, grid=(cfg.nq,), in_specs=ins,
        out_specs=[qs, pl.BlockSpec((nh, nsub, 1, tb), lambda i: (0, i, 0, 0))],
        out_shape=[jax.ShapeDtypeStruct((s, nh * dk), F32), jax.ShapeDtypeStruct((nh, cfg.nb, 1, tb), F32)],
        compiler_params=pltpu.CompilerParams(dimension_semantics=("parallel",)))(*args)


def _attn_bwd(cfg, q, ks, v, sel, selt, o, lse, lse_r, do, dq_in, name):
    if cfg.causal:
        dq, d_r = _attn_dq(cfg, q, ks, v, sel, o, lse, do, dq_in, name + "_dq")
    else:
        dq, d_r = _attn_dq_small(cfg, q, ks, v, o, lse, do, dq_in, name + "_dq")
    dkv = _attn_dkv_flat if cfg.causal or cfg.mode == "win" else _attn_dkv
    res = dkv(cfg, q, ks, v, selt, lse_r, d_r, do, name + "_dkv")
    return dq, res[:-1], res[-1]


def _silu_grad(pre):
    sg = _sigmoid(pre)
    return sg * (1.0 + pre * (1.0 - sg))


def _compress_fwd(a_lo, a_hi, pe_lo, pe_hi, w1_lo, w1_hi, w2, name):
    n, dp = a_lo.shape[0], w2.shape[1]

    def kern(alo, ahi, plo, phi, w1l, w1h, w2r, out_ref, pre_ref):
        xl = (alo[...] + plo[...]).astype(BF16)
        xh = (ahi[...] + phi[...]).astype(BF16)
        pre = _nn(xl, w1l[...]) + _nn(xh, w1h[...])
        act = pre * _sigmoid(pre)
        out_ref[...] = _nn(act.astype(BF16), w2r[...]).astype(BF16)
        pre_ref[...] = pre

    return _pcall(kern, name=name,
                  out_shape=[jax.ShapeDtypeStruct((n, dp), BF16), jax.ShapeDtypeStruct((n, dp), F32)],
                  )(a_lo, a_hi, pe_lo, pe_hi, w1_lo, w1_hi, w2)


def _compress_bwd(a_lo, a_hi, pe_lo, pe_hi, w1_lo, w1_hi, w2, pre, pre_sh, dout, dout_sh, name):
    n, ln = a_lo.shape
    dp = w2.shape[1]

    def kern(alo, ahi, plo, phi, w1l, w1h, w2r, pre_ref, presh_ref, do_ref, dosh_ref,
             da_ref, dpl_ref, dph_ref, dw1l_ref, dw1h_ref, dw2_ref):
        prev = pre_ref[...]
        act = prev * _sigmoid(prev)
        dob = do_ref[...].astype(BF16)
        w2v = w2r[...]
        dpre = (_nt(dob, w2v) * _silu_grad(prev)).astype(BF16)
        dpre_sh = (_nt(dosh_ref[...].astype(BF16), w2v) * _silu_grad(presh_ref[...])).astype(BF16)
        dw2_ref[...] = _nn(act.T.astype(BF16), dob)
        xl = alo[...] + plo[...]
        xh = ahi[...] + phi[...]
        dw1l_ref[...] = _nn(xl.T.astype(BF16), dpre)
        dw1h_ref[...] = _nn(xh.T.astype(BF16), dpre)
        dal = _nt(dpre, w1l[...])
        dah_sh = _nt(dpre_sh, w1h[...])
        da_ref[...] = dal + dah_sh
        dpl_ref[...] = jnp.sum(dal, axis=0, keepdims=True)
        dph_ref[...] = jnp.sum(dah_sh, axis=0, keepdims=True)

    return _pcall(
        kern, name=name,
        out_shape=[jax.ShapeDtypeStruct((n, ln), F32), jax.ShapeDtypeStruct((1, ln), F32),
                   jax.ShapeDtypeStruct((1, ln), F32), jax.ShapeDtypeStruct((ln, dp), F32),
                   jax.ShapeDtypeStruct((ln, dp), F32), jax.ShapeDtypeStruct((dp, dp), F32)],
    )(a_lo, a_hi, pe_lo, pe_hi, w1_lo, w1_hi, w2, pre, pre_sh, dout, dout_sh)


def _nsa_combine(o_cmp, o_slc, o_win, gl):
    s, w = o_cmp.shape
    tr = _tile(s, 512)

    def kern(a_ref, b_ref, c_ref, g_ref, o_ref):
        g = _sigmoid(g_ref[...])
        for h in range(NSA_HEADS):
            cs = slice(h * HEAD_V, (h + 1) * HEAD_V)
            o_ref[:, cs] = (g[:, 3 * h:3 * h + 1] * a_ref[:, cs] + g[:, 3 * h + 1:3 * h + 2] * b_ref[:, cs]
                            + g[:, 3 * h + 2:3 * h + 3] * c_ref[:, cs])

    row = pl.BlockSpec((tr, w), lambda i: (i, 0))
    return _pcall(kern, name="nsa_combine", grid=(s // tr,),
                  in_specs=[row, row, row, pl.BlockSpec((tr, LANE), lambda i: (i, gl.col0))], out_specs=row,
                  out_shape=jax.ShapeDtypeStruct((s, w), F32))(o_cmp, o_slc, o_win, gl.arr)


def _nsa_combine_bwd(do_cat, o_cmp, o_slc, o_win, gl):
    s, w = o_cmp.shape
    tr = _tile(s, 512)

    def kern(d_ref, a_ref, b_ref, c_ref, g_ref, da_ref, db_ref, dc_ref, dg_ref):
        g = _sigmoid(g_ref[...])
        lane = lax.broadcasted_iota(jnp.int32, (tr, LANE), 1)
        dgl = jnp.zeros((tr, LANE), F32)
        for h in range(NSA_HEADS):
            cs = slice(h * HEAD_V, (h + 1) * HEAD_V)
            dv = d_ref[:, cs]
            for b, (src, dst) in enumerate(((a_ref, da_ref), (b_ref, db_ref), (c_ref, dc_ref))):
                gate = g[:, 3 * h + b:3 * h + b + 1]
                dst[:, cs] = gate * dv
                dgate = jnp.sum(dv * src[:, cs], axis=1, keepdims=True)
                dgl = jnp.where(lane == 3 * h + b, dgate * gate * (1.0 - gate), dgl)
        dg_ref[...] = dgl

    row = pl.BlockSpec((tr, w), lambda i: (i, 0))
    tab = pl.BlockSpec((tr, LANE), lambda i: (i, 0))
    return _pcall(kern, name="nsa_combine_bwd", grid=(s // tr,),
                  in_specs=[pl.BlockSpec((tr, w), lambda i: (i, 2)), row, row, row,
                            pl.BlockSpec((tr, LANE), lambda i: (i, gl.col0))],
                  out_specs=[row, row, row, tab],
                  out_shape=[jax.ShapeDtypeStruct((s, w), F32)] * 3 + [jax.ShapeDtypeStruct((s, LANE), F32)],
                  )(do_cat, o_cmp, o_slc, o_win, gl.arr)


def _gate_fwd(o_mla, o_nsa, o_mem, hp):
    s = o_mla.shape[0]
    tr = _tile(s, 256)

    def kern(a_ref, b_ref, c_ref, z_ref, u_ref):
        z = z_ref[...]
        sz = z * _sigmoid(z)
        u_ref[:, 0:1024] = (a_ref[...] * sz[:, 0:1024]).astype(BF16)
        u_ref[:, 1024:1536] = (b_ref[...] * sz[:, 1024:1536]).astype(BF16)
        u_ref[:, 1536:2048] = (c_ref[...] * sz[:, 1536:2048]).astype(BF16)

    return _pcall(
        kern, name="gate_fwd", grid=(s // tr,),
        in_specs=[pl.BlockSpec((tr, 1024), lambda i: (i, 0)), pl.BlockSpec((tr, 512), lambda i: (i, 0)),
                  pl.BlockSpec((tr, 512), lambda i: (i, 0)), pl.BlockSpec((tr, 2048), lambda i: (i, 2))],
        out_specs=pl.BlockSpec((tr, 2048), lambda i: (i, 0)),
        out_shape=jax.ShapeDtypeStruct((s, 2048), BF16))(o_mla, o_nsa, o_mem, hp)


def _gate_bwd(du, o_mla, o_nsa, o_mem, hp):
    s = du.shape[0]
    tr = _tile(s, 256)

    def kern(d_ref, a_ref, b_ref, c_ref, z_ref, do_ref, dz_ref):
        z = z_ref[...]
        sg = _sigmoid(z)
        sz = z * sg
        dsz = sg * (1.0 + z * (1.0 - sg))
        d = d_ref[...]
        do_ref[...] = d * sz
        dz_ref[:, 0:1024] = d[:, 0:1024] * a_ref[...] * dsz[:, 0:1024]
        dz_ref[:, 1024:1536] = d[:, 1024:1536] * b_ref[...] * dsz[:, 1024:1536]
        dz_ref[:, 1536:2048] = d[:, 1536:2048] * c_ref[...] * dsz[:, 1536:2048]

    wide = pl.BlockSpec((tr, 2048), lambda i: (i, 0))
    return _pcall(
        kern, name="gate_bwd", grid=(s // tr,),
        in_specs=[wide, pl.BlockSpec((tr, 1024), lambda i: (i, 0)), pl.BlockSpec((tr, 512), lambda i: (i, 0)),
                  pl.BlockSpec((tr, 512), lambda i: (i, 0)), pl.BlockSpec((tr, 2048), lambda i: (i, 2))],
        out_specs=[wide, wide],
        out_shape=[jax.ShapeDtypeStruct((s, 2048), F32)] * 2)(du, o_mla, o_nsa, o_mem, hp)


def _tile2d(rows, cols, arrays):
    if rows % 16 == 0:
        return _row_tile(rows, cols * arrays), cols
    want = max(LANE, BLOCK_BYTES // (rows * 4 * arrays) // LANE * LANE)
    tc = LANE
    for t in range(LANE, cols + 1, LANE):
        if cols % t == 0 and t <= want:
            tc = t
    return rows, tc


def _sum_slots(buf, name):
    n, rows, cols = buf.shape
    tr, tc = _tile2d(rows, cols, n)

    def kern(b_ref, o_ref):
        acc = b_ref[0].astype(F32)
        for i in range(1, n):
            acc = acc + b_ref[i].astype(F32)
        o_ref[...] = acc

    return _pcall(kern, name=name, grid=(rows // tr, cols // tc),
                  in_specs=[pl.BlockSpec((n, tr, tc), lambda i, j: (0, i, j))],
                  out_specs=pl.BlockSpec((tr, tc), lambda i, j: (i, j)),
                  out_shape=jax.ShapeDtypeStruct((rows, cols), F32))(buf)


def _chip_sum(buf, core, axis, name):
    n, rows, cols = buf.shape
    tr, tc = _tile2d(rows, cols, n)
    nbr, nbc = rows // tr, cols // tc

    def kern(c_ref, b_ref, o_ref, w_ref):
        acc = b_ref[0].astype(F32)
        for i in range(1, n):
            acc = acc + b_ref[i].astype(F32)
        o_ref[...] = acc
        w_ref[...] = acc

    place = ((lambda i, j, c: (c[0] * nbr + i, j)) if axis == 0 else (lambda i, j, c: (i, c[0] * nbc + j)))
    whole = (2 * rows, cols) if axis == 0 else (rows, 2 * cols)
    grid_spec = pltpu.PrefetchScalarGridSpec(
        num_scalar_prefetch=1, grid=(nbr, nbc),
        in_specs=[pl.BlockSpec((n, tr, tc), lambda i, j, c: (0, i, j))],
        out_specs=[pl.BlockSpec((tr, tc), lambda i, j, c: (i, j)), pl.BlockSpec((tr, tc), place)])
    return _pcall(kern, name=name, grid_spec=grid_spec,
                  out_shape=[jax.ShapeDtypeStruct((rows, cols), F32), jax.ShapeDtypeStruct(whole, F32)])(core, buf)


def _pair_sum(g4, theirs, core, axis, name):
    n, rows, cols = theirs.shape
    tr, tc = _tile2d(rows, cols, 1)
    nbr, nbc = rows // tr, cols // tc

    def kern(c_ref, a_ref, b_ref, o_ref):
        o_ref[...] = (a_ref[...] + b_ref[...]).astype(BF16)

    blk = (1, tr, tc)
    mine = ((lambda s, i, j, c: (s, c[0] * nbr + i, j)) if axis == 0
            else (lambda s, i, j, c: (s, i, c[0] * nbc + j)))
    grid_spec = pltpu.PrefetchScalarGridSpec(
        num_scalar_prefetch=1, grid=(n, nbr, nbc),
        in_specs=[pl.BlockSpec(blk, mine), pl.BlockSpec(blk, lambda s, i, j, c: (s, i, j))],
        out_specs=pl.BlockSpec(blk, lambda s, i, j, c: (s, i, j)))
    return _pcall(kern, name=name, grid_spec=grid_spec,
                  out_shape=jax.ShapeDtypeStruct((n, rows, cols), BF16))(core, g4, theirs)


def _adamw(w, g, m, v, name):
    rows, cols = w.shape
    tr, tc = _tile2d(rows, cols, 4)
    bc1 = 1.0 - ADAM_B1 ** ADAM_STEP
    bc2 = 1.0 - ADAM_B2 ** ADAM_STEP

    def kern(w_ref, g_ref, m_ref, v_ref, d_ref, mo_ref, vo_ref):
        gv = g_ref[...]
        mn = ADAM_B1 * m_ref[...] + (1.0 - ADAM_B1) * gv
        vn = ADAM_B2 * v_ref[...] + (1.0 - ADAM_B2) * (gv * gv)
        d_ref[...] = -ADAM_LR * ((mn / bc1) / (jnp.sqrt(vn / bc2) + ADAM_EPS) + ADAM_WD * w_ref[...])
        mo_ref[...] = mn
        vo_ref[...] = vn

    blk = pl.BlockSpec((tr, tc), lambda i, j: (i, j))
    return _pcall(kern, name=name, grid=(rows // tr, cols // tc), in_specs=[blk] * 4, out_specs=[blk] * 3,
                  out_shape=[jax.ShapeDtypeStruct((rows, cols), F32)] * 3)(w, g, m, v)


ANY = pl.BlockSpec(memory_space=pl.ANY)


def _place():
    x, y, c = lax.axis_index("x"), lax.axis_index("y"), lax.axis_index("c")
    chips = [(1 - x, y), (x, 1 - y), (1 - x, 1 - y)]
    return x, y, c, chips


def _remote(src, dst, send_sem, recv_sem, to):
    return pltpu.make_async_remote_copy(src_ref=src, dst_ref=dst, send_sem=send_sem, recv_sem=recv_sem,
                                        device_id=to, device_id_type=MESH)


def _half(ref, lead, core, axis):
    size = ref.shape[len(lead) + axis] // 2
    cut = pl.ds(core * size, size)
    return ref.at[tuple(lead) + ((cut, slice(None)) if axis == 0 else (slice(None), cut))]


def _gather_shards(ws, axes):
    side = _gather_side(ws, axes)

    def body(*refs):
        nw = len(ws)
        split = (refs[:nw], refs[nw:2 * nw], refs[2 * nw:])
        side.phase("start", *split)
        side.phase("finish", *split)

    return _pcall(body, name="gather_shards", in_specs=[ANY] * len(ws), out_specs=[ANY] * len(ws),
                  out_shape=side.out_shape, scratch_shapes=side.scratch)(*ws)


def _gather_side(ws, axes):
    nw = len(ws)

    def phase(which, w_refs, out_refs, sems):
        send_sems, recv_sems = sems
        x, y, c, chips = _place()
        me = 2 * x + y
        sibling = (x, y, 1 - c)

        def part(i, slot, core):
            return _half(out_refs[i], (slot,), core, axes[i])

        def copy(sem, src, dst, to):
            return _remote(src, dst, send_sems.at[sem], recv_sems.at[sem], to)

        first = [copy(j * nw + i, _half(w_refs[i], (), c, axes[i]), part(i, me, c), (*chip, c))
                 for j, chip in enumerate(chips) for i in range(nw)]
        if which == "start":
            for cp in first:
                cp.start()
            return
        passed = []
        for j, (cx, cy) in enumerate(chips):
            slot = 2 * cx + cy
            for i in range(nw):
                copy(j * nw + i, part(i, slot, c), part(i, slot, c), (x, y, c)).wait_recv()
                fwd = copy((3 + j) * nw + i, part(i, slot, c), part(i, slot, c), sibling)
                fwd.start()
                passed.append(fwd)
        for j, (cx, cy) in enumerate(chips):
            slot = 2 * cx + cy
            for i in range(nw):
                copy((3 + j) * nw + i, part(i, slot, 1 - c), part(i, slot, 1 - c), (x, y, c)).wait_recv()
        for cp in first + passed:
            cp.wait_send()

    return _Side(list(ws), [jax.ShapeDtypeStruct((4,) + w.shape, w.dtype) for w in ws],
                 [pltpu.SemaphoreType.DMA((6 * nw,)), pltpu.SemaphoreType.DMA((6 * nw,))], phase)


def _half_shape(shape, axis):
    return tuple(d // 2 if k == len(shape) - 2 + axis else d for k, d in enumerate(shape))


def _pair_exchange(gs, axes, name):
    nw = len(gs)

    def body(*refs):
        g_refs, out_refs = refs[:nw], refs[nw:2 * nw]
        send_sems, recv_sems = refs[2 * nw:]
        x, y, c, _ = _place()
        cps = []
        for i in range(nw):
            cp = _remote(_half(g_refs[i], (slice(None),), 1 - c, axes[i]), out_refs[i],
                         send_sems.at[i], recv_sems.at[i], (x, y, 1 - c))
            cp.start()
            cps.append(cp)
        for cp in cps:
            cp.wait()

    return _pcall(body, name=name, in_specs=[ANY] * nw, out_specs=[ANY] * nw,
                  out_shape=[jax.ShapeDtypeStruct(_half_shape(g.shape, a), g.dtype) for g, a in zip(gs, axes)],
                  scratch_shapes=[pltpu.SemaphoreType.DMA((nw,)), pltpu.SemaphoreType.DMA((nw,))])(*gs)


def _chip_side(ps):
    nw = len(ps)

    def phase(which, p_refs, out_refs, sems):
        send_sems, recv_sems, local_sems = sems
        x, y, c, chips = _place()
        me = 2 * x + y
        mine = [pltpu.make_async_copy(p_refs[i].at[me], out_refs[i].at[me], local_sems.at[i]) for i in range(nw)]
        sends = [_remote(p_refs[i].at[2 * cx + cy], out_refs[i].at[me], send_sems.at[j * nw + i],
                         recv_sems.at[j * nw + i], (cx, cy, c))
                 for j, (cx, cy) in enumerate(chips) for i in range(nw)]
        if which == "start":
            for cp in mine + sends:
                cp.start()
            return
        for j, (cx, cy) in enumerate(chips):
            slot = 2 * cx + cy
            for i in range(nw):
                _remote(out_refs[i].at[slot], out_refs[i].at[slot], send_sems.at[j * nw + i],
                        recv_sems.at[j * nw + i], (x, y, c)).wait_recv()
        for cp in sends:
            cp.wait_send()
        for cp in mine:
            cp.wait()

    return _Side(list(ps), [jax.ShapeDtypeStruct(p.shape, p.dtype) for p in ps],
                 [pltpu.SemaphoreType.DMA((3 * nw,)), pltpu.SemaphoreType.DMA((3 * nw,)),
                  pltpu.SemaphoreType.DMA((nw,))], phase)


def _half_exchange(ts, wholes, axes):
    nw = len(ts)

    def body(*refs):
        t_refs, out_refs = refs[:nw], refs[2 * nw:3 * nw]
        send_sems, recv_sems = refs[3 * nw:]
        x, y, c, _ = _place()
        sends = []
        for i in range(nw):
            cp = _remote(t_refs[i], _half(out_refs[i], (), c, axes[i]), send_sems.at[i], recv_sems.at[i],
                         (x, y, 1 - c))
            cp.start()
            sends.append(cp)
        for i in range(nw):
            _remote(t_refs[i], _half(out_refs[i], (), 1 - c, axes[i]), send_sems.at[i], recv_sems.at[i],
                    (x, y, c)).wait_recv()
        for cp in sends:
            cp.wait_send()

    return _pcall(body, name="half_exchange", in_specs=[ANY] * (2 * nw), out_specs=[ANY] * nw,
                  out_shape=[jax.ShapeDtypeStruct(w.shape, w.dtype) for w in wholes],
                  input_output_aliases={nw + i: i for i in range(nw)},
                  scratch_shapes=[pltpu.SemaphoreType.DMA((nw,)), pltpu.SemaphoreType.DMA((nw,))])(*ts, *wholes)


def _gather_all(v):
    rows, cols = v.shape

    def body(v_ref, out_ref, send_sems, recv_sems, local_sem):
        x, y, c, _ = _place()
        me = 4 * x + 2 * y + c
        mine = pltpu.make_async_copy(v_ref, out_ref.at[me], local_sem)
        mine.start()
        sends = []
        for d in range(1, 8):
            peer = (x ^ (d >> 2), y ^ ((d >> 1) & 1), c ^ (d & 1))
            cp = _remote(v_ref, out_ref.at[me], send_sems.at[d - 1], recv_sems.at[d - 1], peer)
            cp.start()
            sends.append(cp)
        for d in range(1, 8):
            slot = 4 * (x ^ (d >> 2)) + 2 * (y ^ ((d >> 1) & 1)) + (c ^ (d & 1))
            _remote(v_ref, out_ref.at[slot], send_sems.at[d - 1], recv_sems.at[d - 1], (x, y, c)).wait_recv()
        for cp in sends:
            cp.wait_send()
        mine.wait()

    return _pcall(body, name="gather_all", in_specs=[ANY], out_specs=ANY,
                  out_shape=jax.ShapeDtypeStruct((8, rows, cols), v.dtype),
                  scratch_shapes=[pltpu.SemaphoreType.DMA((7,)), pltpu.SemaphoreType.DMA((7,)),
                                  pltpu.SemaphoreType.DMA])(v)


def _pad_cols(a, width):
    return a if a.shape[1] == width else jnp.pad(a, ((0, 0), (0, width - a.shape[1])))


def _unpad_segments():
    z = PAD["z"]
    segs = [(PAD["c_q"], 0, 512), (PAD["c_kv"], 512, 512), (PAD["k_rope"], 1024, 64), (z, 1088, 1024)]
    segs += [(PAD["q_nsa"] + 256 * h, 2112 + NSA_DK * h, NSA_DK) for h in range(NSA_HEADS)]
    for name, rows in (("k_c", 192), ("v_c", 128), ("k_s", 192), ("v_s", 128), ("k_w", 192), ("v_w", 128),
                       ("g_nsa", 12)):
        segs.append((PAD[name], ORIG[name][0], rows))
    segs += [(z + 1024, ORIG["z_nsa"][0], 512), (PAD["q_mem"], ORIG["q_mem"][0], 512),
             (z + 1536, ORIG["z_mem"][0], 512)]
    return segs


def _w_in_grad_slots(gt):
    rows, cols = gt.shape
    shard = sum(n for _, _, n in _unpad_segments()) // 4
    tc = 256
    pieces = []
    for src, dst, n in _unpad_segments():
        while n:
            slot, off = divmod(dst, shard)
            take = min(n, shard - off)
            pieces.append((src, slot, off, take))
            src, dst, n = src + take, dst + take, n - take

    def kern(g_ref, o_ref):
        for src, slot, off, take in pieces:
            o_ref[slot, off:off + take, :] = g_ref[src:src + take, :]

    return _pcall(kern, name="w_in_grad_slots", grid=(cols // tc,),
                  in_specs=[pl.BlockSpec((rows, tc), lambda i: (0, i))],
                  out_specs=pl.BlockSpec((4, shard, tc), lambda i: (0, 0, i)),
                  out_shape=jax.ShapeDtypeStruct((4, shard, cols), F32))(gt)


def _w_in_from_slots(ws):
    nslot, shard, cols = ws.shape
    tc = 256
    pieces = []
    for dst, src, n in _unpad_segments() + [(PAD["k_rope"] + 64, ORIG["k_rope"][0], 64)]:
        while n:
            slot, off = divmod(src, shard)
            take = min(n, shard - off)
            pieces.append((dst, slot, off, take))
            src, dst, n = src + take, dst + take, n - take

    def kern(w_ref, o_ref):
        o_ref[...] = jnp.zeros_like(o_ref)
        for dst, slot, off, take in pieces:
            o_ref[dst:dst + take, :] = w_ref[slot, off:off + take, :]

    return _pcall(kern, name="w_in_from_slots", grid=(cols // tc,),
                  in_specs=[pl.BlockSpec((nslot, shard, tc), lambda i: (0, 0, i))],
                  out_specs=pl.BlockSpec((D_PAD, tc), lambda i: (0, i)),
                  out_shape=jax.ShapeDtypeStruct((D_PAD, cols), ws.dtype))(ws)


def _rope_tables(s):
    pos = jnp.arange(s, dtype=F32)
    inv_freq = ROPE_THETA ** (-jnp.arange(0, 64, 2, dtype=F32) / 64)
    ang = pos[:, None] * inv_freq[None, :]
    cos, sin = jnp.cos(ang), jnp.sin(ang)
    z = jnp.zeros((s, 64), F32)
    return jnp.concatenate([cos, cos, z], axis=1), jnp.concatenate([-sin, sin, z], axis=1)


def _overlap_table(s):
    n_c, n_s = s // CMP_STRIDE, s // SLC_LEN
    c0 = np.arange(n_c)[:, None] * CMP_STRIDE
    s0 = np.arange(LANE)[None, :] * SLC_LEN
    ov = (c0 < s0 + SLC_LEN) & (c0 + CMP_LEN > s0) & (np.arange(n_c)[:, None] < n_c - 1) & (np.arange(LANE)[None, :] < n_s)
    return jnp.asarray(ov.astype(np.float32), dtype=BF16)


def _shift_down(a):
    return jnp.concatenate([jnp.zeros((8, a.shape[1]), a.dtype), a], axis=0)[7:7 + a.shape[0]]


def _shift_up(a):
    return jnp.concatenate([a, jnp.zeros((8, a.shape[1]), a.dtype)], axis=0)[1:1 + a.shape[0]]


def _local_step(x, mem, target, w, hooks=None):
    s = x.shape[0]
    cs, sn = _rope_tables(s)
    t_ = jnp.transpose

    w_in_p = _w_in_from_slots(w["w_in_t"])
    xn, rstd_x = _rms_fwd(_Src(x, D_MODEL), w["norm_g"], "norm_x")
    if hooks is None:
        hp, hpb = _mm(xn, w_in_p, "in_proj", mode="nt", second_dtype=BF16, wide=2048)
    else:
        hp, hpb, *gathered = _mm(xn, w_in_p, "in_proj", mode="nt", second_dtype=BF16, wide=2048,
                                 side=hooks.gather_side)
        w = {**w, **hooks.weights(gathered)}

    w_uq3 = w["w_uq"].reshape(512, MLA_HEADS, 192)
    w_uq_p = jnp.concatenate([w_uq3, w_uq3[:, :, 128:]], axis=2).reshape(512, MLA_HEADS * 256)
    w_ukv_p = t_(w["w_ukv"].reshape(512, MLA_HEADS, 2, 128), (0, 2, 1, 3)).reshape(512, 2048)
    c_q, c_kv = _Src(hp, 512, 0), _Src(hp, 512, 1)
    cqn, rstd_q = _rms_fwd(c_q, w["q_norm_g"], "norm_q")
    ckvn, rstd_kv = _rms_fwd(c_kv, w["kv_norm_g"], "norm_kv")
    q_lin = _mm(cqn, w_uq_p, "mla_q_proj")
    kvb = _mm(ckvn, w_ukv_p, "mla_kv_proj", out_dtype=BF16)
    q_mla = _rope_fwd(_Src(q_lin, MLA_HEADS * 256), cs, sn, MLA_HEADS, 256, LANE, "rope_q")
    k_pe = _rope_fwd(_Src(hp, LANE, PAD["k_rope"] // LANE), cs, sn, 1, LANE, 0, "rope_k")
    mla = _Attn("mla", s, s, MLA_HEADS, 256)
    mla_q, mla_v = _Src(q_mla, 256), _Src(kvb, LANE, MLA_HEADS)
    mla_k = [_Src(kvb, LANE), _Src(k_pe, LANE, 0, False)]
    o_mla, l_mla, lr_mla = _attn_fwd(mla, mla_q, mla_k, mla_v, None, "mla_fwd")

    sk = s // CMP_STRIDE
    pe_k, pe_v = w["cmp_pe_k"], w["cmp_pe_v"]
    w1k = _pad_cols(w["cmp_w1k"], 256)
    w2k = jnp.pad(w["cmp_w2k"], ((0, 64), (0, 64))).astype(BF16)
    w1v, w2v = w["cmp_w1v"], w["cmp_w2v"].astype(BF16)
    half_k, half_v = CMP_STRIDE * NSA_DK, CMP_STRIDE * HEAD_V
    ak = hp[:, PAD["k_c"]:PAD["k_c"] + NSA_DK].reshape(sk, half_k)
    av = hp[:, PAD["v_c"]:PAD["v_c"] + HEAD_V].reshape(sk, half_v)
    ck_args = (ak, _shift_up(ak), pe_k[:CMP_STRIDE].reshape(1, half_k), pe_k[CMP_STRIDE:].reshape(1, half_k),
               w1k[:half_k], w1k[half_k:], w2k)
    cv_args = (av, _shift_up(av), pe_v[:CMP_STRIDE].reshape(1, half_v), pe_v[CMP_STRIDE:].reshape(1, half_v),
               w1v[:half_v], w1v[half_v:], w2v)
    k_cmp, pre_k = _compress_fwd(*ck_args, "compress_k")
    v_cmp, pre_v = _compress_fwd(*cv_args, "compress_v")
    cmp_ = _Attn("cmp", s, sk, NSA_HEADS, 256)
    slc = _Attn("slc", s, s, NSA_HEADS, 256)
    win = _Attn("win", s, s, NSA_HEADS, 256)
    nsa_q = _Src(hpb, 256, PAD["q_nsa"] // 256)
    cmp_k, cmp_v = [_Src(k_cmp, 256, 0, False)], _Src(v_cmp, HEAD_V, 0, False)
    slc_k, slc_v = [_Src(hpb, 256, PAD["k_s"] // 256, False)], _Src(hpb, HEAD_V, PAD["v_s"] // HEAD_V, False)
    win_k, win_v = [_Src(hpb, 256, PAD["k_w"] // 256, False)], _Src(hpb, HEAD_V, PAD["v_w"] // HEAD_V, False)
    o_cmp, l_cmp, lr_cmp, sel, selt = _attn_fwd_small(cmp_, nsa_q, cmp_k, cmp_v, "cmp_fwd", _overlap_table(s))
    o_slc, l_slc, lr_slc = _attn_fwd(slc, nsa_q, slc_k, slc_v, sel, "slc_fwd")
    o_win, l_win, lr_win = _attn_fwd_small(win, nsa_q, win_k, win_v, "win_fwd")
    gl = _Src(hp, LANE, PAD["g_nsa"] // LANE)
    o_nsa = _nsa_combine(o_cmp, o_slc, o_win, gl)

    mn, rstd_m = _rms_fwd(_Src(mem, D_MODEL), w["mem_norm_g"], "norm_mem")
    kvm = _mm(mn, w["w_mem_kv"], "mem_kv_proj", out_dtype=BF16)
    mem_ = _Attn("mem", s, mem.shape[0], MEM_HEADS, LANE)
    mem_q, mem_k, mem_v = _Src(hpb, LANE, PAD["q_mem"] // LANE), [_Src(kvm, LANE)], _Src(kvm, LANE, MEM_HEADS)
    o_mem, l_mem, lr_mem = _attn_fwd_small(mem_, mem_q, mem_k, mem_v, "mem_fwd")

    u = _gate_fwd(o_mla, o_nsa, o_mem, hp)
    proj = _mm(u, w["w_out"], "out_proj", wide=2048)
    dy, g_final, loss = _final_loss(x, proj, w["final_norm_g"].reshape(1, -1), target)

    g_w_out = _mm(u, dy, "out_proj_dw", mode="tn")
    du = _mm(dy, w["w_out"], "out_proj_dx", mode="nt", wide=2048)
    do_cat, dz = _gate_bwd(du, o_mla, o_nsa, o_mem, hp)

    dq_mem, (dk_mem,), dv_mem = _attn_bwd(mem_, mem_q, mem_k, mem_v, None, None, _Src(o_mem, HEAD_V), l_mem,
                                          lr_mem, _Src(do_cat, HEAD_V, 12), None, "mem_bwd")
    dkvm = jnp.concatenate([dk_mem, dv_mem], axis=1)
    g_w_mem_kv = _mm(mn, dkvm, "mem_kv_dw", mode="tn")
    dmn = _mm(dkvm, w["w_mem_kv"], "mem_kv_dx", mode="nt")
    _, g_mem_norm = _rms_bwd(_Src(mem, D_MODEL), w["mem_norm_g"], rstd_m, dmn, None, "norm_mem_bwd")

    do_cmp, do_slc, do_win, dgl = _nsa_combine_bwd(do_cat, o_cmp, o_slc, o_win, gl)
    dq_n, (dk_cmp,), dv_cmp = _attn_bwd(cmp_, nsa_q, cmp_k, cmp_v, None, None, _Src(o_cmp, HEAD_V), l_cmp,
                                        lr_cmp, _Src(do_cmp, HEAD_V), None, "cmp_bwd")
    dq_n, (dk_s,), dv_s = _attn_bwd(slc, nsa_q, slc_k, slc_v, sel, selt, _Src(o_slc, HEAD_V), l_slc, lr_slc,
                                    _Src(do_slc, HEAD_V), dq_n, "slc_bwd")
    dq_n, (dk_w,), dv_w = _attn_bwd(win, nsa_q, win_k, win_v, None, None, _Src(o_win, HEAD_V), l_win, lr_win,
                                    _Src(do_win, HEAD_V), dq_n, "win_bwd")
    dak, dpk_lo, dpk_hi, dw1k_lo, dw1k_hi, g_w2k = _compress_bwd(
        *ck_args, pre_k, _shift_down(pre_k), dk_cmp, _shift_down(dk_cmp), "compress_k_bwd")
    dav, dpv_lo, dpv_hi, dw1v_lo, dw1v_hi, g_w2v = _compress_bwd(
        *cv_args, pre_v, _shift_down(pre_v), dv_cmp, _shift_down(dv_cmp), "compress_v_bwd")
    g_pe_k = jnp.concatenate([dpk_lo.reshape(CMP_STRIDE, NSA_DK), dpk_hi.reshape(CMP_STRIDE, NSA_DK)], axis=0)
    g_pe_v = jnp.concatenate([dpv_lo.reshape(CMP_STRIDE, HEAD_V), dpv_hi.reshape(CMP_STRIDE, HEAD_V)], axis=0)
    g_w1k = jnp.concatenate([dw1k_lo, dw1k_hi], axis=0)[:, :NSA_DK]
    g_w1v = jnp.concatenate([dw1v_lo, dw1v_hi], axis=0)
    dk_c = _pad_cols(dak.reshape(s, NSA_DK), 256)
    dv_c = dav.reshape(s, HEAD_V)

    dq_m, (dk_nope, dk_pe), dv_m = _attn_bwd(mla, mla_q, mla_k, mla_v, None, None, _Src(o_mla, HEAD_V), l_mla,
                                             lr_mla, _Src(do_cat, HEAD_V), None, "mla_bwd")
    dq_lin = _rope_bwd_q(dq_m, cs, sn)
    dkv_lin, d_krope = _rope_bwd_k(dk_nope, dk_pe, dv_m, cs, sn)
    g_w_uq_p = _mm(cqn, dq_lin, "mla_q_dw", mode="tn")
    dcqn = _mm(dq_lin, w_uq_p, "mla_q_dx", mode="nt")
    g_w_ukv_p = _mm(ckvn, dkv_lin, "mla_kv_dw", mode="tn")
    dckvn = _mm(dkv_lin, w_ukv_p, "mla_kv_dx", mode="nt")
    dc_q, g_q_norm = _rms_bwd(c_q, w["q_norm_g"], rstd_q, dcqn, None, "norm_q_bwd")
    dc_kv, g_kv_norm = _rms_bwd(c_kv, w["kv_norm_g"], rstd_kv, dckvn, None, "norm_kv_bwd")
    g_w_uq = g_w_uq_p.reshape(512, MLA_HEADS, 256)[:, :, :192].reshape(512, MLA_HEADS * 192)
    g_w_ukv = t_(g_w_ukv_p.reshape(512, 2, MLA_HEADS, 128), (0, 2, 1, 3)).reshape(512, 2048)

    dhp = jnp.concatenate(
        [dc_q, dc_kv, dq_n, dk_c, dk_s, dk_w, d_krope, dv_c, dv_s, dv_w, dgl,
         jnp.zeros((s, PAD["q_mem"] - (PAD["g_nsa"] + LANE)), F32), dq_mem, dz], axis=1).astype(BF16)
    grads = dict(q_norm_g=g_q_norm, w_uq=g_w_uq, kv_norm_g=g_kv_norm,
                 w_ukv=g_w_ukv, cmp_pe_k=g_pe_k, cmp_pe_v=g_pe_v, cmp_w1k=g_w1k, cmp_w2k=g_w2k[:NSA_DK, :NSA_DK],
                 cmp_w1v=g_w1v, cmp_w2v=g_w2v, mem_norm_g=g_mem_norm, w_mem_kv=g_w_mem_kv, w_out=g_w_out,
                 final_norm_g=g_final.reshape(-1))
    if hooks is None:
        g_w_in_t = _w_in_grad_slots(_mm(dhp, xn, "in_proj_dw", mode="tn", wide=2048))
        dxn = _mm(dhp, w_in_p, "in_proj_dx", wide=2048)
    else:
        g_w_in_p, *hooks.received = _mm(dhp, xn, "in_proj_dw", mode="tn", wide=2048, side=hooks.reduce_side(grads))
        g_w_in_t = _w_in_grad_slots(g_w_in_p)
        dxn, hooks.received_w_in = _mm(dhp, w_in_p, "in_proj_dx", wide=2048, side=hooks.reduce_side_w_in(g_w_in_t))
    grad_x, g_norm = _rms_bwd(_Src(x, D_MODEL), w["norm_g"], rstd_x, dxn, dy, "norm_x_bwd")
    grads.update(norm_g=g_norm, w_in_t=g_w_in_t)
    return loss[0, 0], grad_x, grads


def kernel(x, mem, norm_g, w_in, q_norm_g, w_uq, kv_norm_g, w_ukv, cmp_pe_k, cmp_pe_v, cmp_w1k, cmp_w2k, cmp_w1v, cmp_w2v, mem_norm_g, w_mem_kv, w_out, final_norm_g, loss_target, m_norm_g, m_w_in, m_q_norm_g, m_w_uq, m_kv_norm_g, m_w_ukv, m_cmp_pe_k, m_cmp_pe_v, m_cmp_w1k, m_cmp_w2k, m_cmp_w1v, m_cmp_w2v, m_mem_norm_g, m_w_mem_kv, m_w_out, m_final_norm_g, v_norm_g, v_w_in, v_q_norm_g, v_w_uq, v_kv_norm_g, v_w_ukv, v_cmp_pe_k, v_cmp_pe_v, v_cmp_w1k, v_cmp_w2k, v_cmp_w1v, v_cmp_w2v, v_mem_norm_g, v_w_mem_kv, v_w_out, v_final_norm_g):
    args = dict(locals())
    wts = {n: args[n] for n in WEIGHTS}
    loc = {n: (a if n == "final_norm_g" else a[0]) for n, a in wts.items()}

    def to_x(n, a):
        return a.T if n == "w_in" else a

    split = [1 if n == "w_in" else 0 for n in SHARDED]
    rest = [n for n in SHARDED if n != "w_in"]
    chip = 2 * lax.axis_index("x") + lax.axis_index("y")
    core = lax.axis_index("c").astype(jnp.int32).reshape(1)
    own = {n: to_x(n, loc[n]).astype(BF16) for n in SHARDED}

    def with_own_slot(gw, a):
        return lax.dynamic_update_slice(gw, a[None], (chip, 0, 0))

    def slots(n, a):
        if n == "w_in":
            return a
        if SHARD_AXIS[n] == 0:
            return a.reshape(4, a.shape[0] // 4, a.shape[1])
        width = a.shape[1] // 4
        return jnp.stack([a[:, j * width:(j + 1) * width] for j in range(4)])

    def pair_sums(names, grads, name):
        axes = [1 if n == "w_in" else 0 for n in names]
        gs = [slots(n, a) for n, a in zip(names, grads)]
        theirs = _pair_exchange(gs, axes, name)
        return [_pair_sum(a, b, core, ax, "pair_sum_" + n) for n, a, b, ax in zip(names, gs, theirs, axes)]

    class Hooks:
        gather_side = _gather_side([own[n] for n in rest], [0] * len(rest))
        received = None

        @staticmethod
        def weights(gathered):
            out = {}
            for n, gw in zip(rest, gathered):
                gw = with_own_slot(gw, own[n])
                if SHARD_AXIS[n] == 0:
                    out[n] = gw.reshape(4 * gw.shape[1], gw.shape[2])
                else:
                    out[n] = jnp.concatenate([gw[j] for j in range(4)], axis=1)
            return out

        @staticmethod
        def reduce_side(grads):
            return _chip_side(pair_sums(rest, [grads[n] for n in rest], "pair_exchange_rest"))

        @staticmethod
        def reduce_side_w_in(g_w_in_t):
            return _chip_side(pair_sums(["w_in"], [g_w_in_t], "pair_exchange_w_in"))

    hooks = Hooks()

    start = {n: loc[n].reshape(1, -1) if loc[n].ndim == 1 else loc[n] for n in REPLICATED}
    start["w_in_t"] = with_own_slot(_gather_shards([own["w_in"]], [1])[0], own["w_in"])
    loss, grad_x, g = _local_step(x[0], mem[0], loss_target[0], start, hooks)
    loss = lax.psum(loss, ("x", "y", "c"))

    from_chips = dict(zip(rest, hooks.received), w_in=hooks.received_w_in)
    sums = [_chip_sum(from_chips[n], core, ax, "chip_sum_" + n) for n, ax in zip(SHARDED, split)]
    g_sh = _half_exchange([a for a, _ in sums], [b for _, b in sums], split)

    n_rep = sum(int(np.prod(loc[n].shape)) for n in REPLICATED)
    rows_rep = -(-n_rep // (8 * LANE)) * 8

    def rep_pack(parts):
        flat = jnp.concatenate([p.reshape(-1) for p in parts])
        return jnp.pad(flat, (0, rows_rep * LANE - n_rep)).reshape(rows_rep, LANE)

    g_rep = _sum_slots(_gather_all(rep_pack([g[n] for n in REPLICATED])), "replica_sum")
    d_rp, m_rp, v_rp = _adamw(rep_pack([wts[n] for n in REPLICATED]), g_rep,
                              rep_pack([args["m_" + n] for n in REPLICATED]),
                              rep_pack([args["v_" + n] for n in REPLICATED]), "adamw_replicated")

    def rep_unpack(buf):
        flat, out, o = buf.reshape(-1), {}, 0
        for n in REPLICATED:
            size = int(np.prod(wts[n].shape))
            out[n] = flat[o:o + size].reshape(wts[n].shape)
            o += size
        return out

    outs = {k: rep_unpack(b) for k, b in (("g", g_rep), ("d", d_rp), ("m", m_rp), ("v", v_rp))}
    for n, gn in zip(SHARDED, g_sh):
        d, mo, vo = _adamw(to_x(n, loc[n]), gn, to_x(n, args["m_" + n][0]), to_x(n, args["v_" + n][0]),
                           "adamw_" + n)
        for k, a in (("g", gn), ("d", d), ("m", mo), ("v", vo)):
            outs[k][n] = to_x(n, a).reshape(wts[n].shape)

    return (loss, grad_x[None], *[outs["g"][n] for n in WEIGHTS], *[outs["d"][n] for n in WEIGHTS],
            *[outs["m"][n] for n in WEIGHTS], *[outs["v"][n] for n in WEIGHTS])
```

```python
from typing import NamedTuple

import numpy as np
import jax
import jax.numpy as jnp
from jax import lax
from jax.experimental import pallas as pl
from jax.experimental.pallas import tpu as pltpu

F32 = jnp.float32
BF16 = jnp.bfloat16
MESH = pl.DeviceIdType.MESH

D_MODEL = 2048
EPS = 1e-6
LANE = 128
HEAD_V = 128
MLA_HEADS = 8
NSA_HEADS = 4
MEM_HEADS = 4
NSA_DK = 192
CMP_STRIDE = 16
CMP_LEN = 32
SLC_LEN = 64
SLC_TOPN = 16
WIN = 512
NEG = -1e30
LOG2E = 1.4426950408889634
ROPE_THETA = 10000.0
BLOCK_BYTES = 2 << 20

ORIG = dict(c_q=(0, 512), c_kv=(512, 512), k_rope=(1024, 64), z_mla=(1088, 1024),
            q_nsa=(2112, 768), k_c=(2880, 192), v_c=(3072, 128), k_s=(3200, 192),
            v_s=(3392, 128), k_w=(3520, 192), v_w=(3712, 128), g_nsa=(3840, 12),
            z_nsa=(3852, 512), q_mem=(4364, 512), z_mem=(4876, 512))
PAD = dict(c_q=0, c_kv=512, q_nsa=1024, k_c=2048, k_s=2304, k_w=2560, k_rope=2816, v_c=2944,
           v_s=3072, v_w=3200, g_nsa=3328, q_mem=3584, z=4096)
D_PAD = 6144

ADAM_LR, ADAM_B1, ADAM_B2, ADAM_EPS, ADAM_WD, ADAM_STEP = 0.001, 0.9, 0.999, 1e-08, 0.01, 10

SHARDED = ("w_in", "w_uq", "w_ukv", "cmp_w1k", "cmp_w1v", "w_mem_kv", "w_out")
SHARD_AXIS = dict(w_in=1, w_uq=1, w_ukv=1, cmp_w1k=0, cmp_w1v=0, w_mem_kv=0, w_out=0)
REPLICATED = ("norm_g", "q_norm_g", "kv_norm_g", "cmp_pe_k", "cmp_pe_v", "cmp_w2k", "cmp_w2v",
              "mem_norm_g", "final_norm_g")
WEIGHTS = ("norm_g", "w_in", "q_norm_g", "w_uq", "kv_norm_g", "w_ukv", "cmp_pe_k", "cmp_pe_v",
           "cmp_w1k", "cmp_w2k", "cmp_w1v", "cmp_w2v", "mem_norm_g", "w_mem_kv", "w_out",
           "final_norm_g")


def _pcall(kernel, **kw):
    return pl.pallas_call(kernel, **kw)


def _tile(n, pref):
    if n <= pref:
        return n
    for t in range(pref, LANE - 1, -LANE):
        if n % t == 0:
            return t
    raise ValueError((n, pref))


def _row_tile(rows, cols, itemsize=4):
    want = max(16, BLOCK_BYTES // (cols * itemsize))
    if rows <= want:
        return rows
    t = 16
    best = rows
    while t <= want:
        if rows % t == 0:
            best = t
        t *= 2
    return best


def _nt(a, b):
    return lax.dot_general(a, b, (((1,), (1,)), ((), ())), preferred_element_type=F32)


def _tn(a, b):
    return lax.dot_general(a, b, (((0,), (0,)), ((), ())), preferred_element_type=F32)


def _nn(a, b):
    return jnp.dot(a, b, preferred_element_type=F32)


def _sigmoid(x):
    return 1.0 / (1.0 + jnp.exp(-x))


class _Src(NamedTuple):
    arr: jax.Array
    width: int
    col0: int = 0
    per_head: bool = True

    def col(self, h):
        return self.col0 + h if self.per_head else self.col0


class _Side(NamedTuple):
    inputs: list
    out_shape: list
    scratch: list
    phase: object


def _mm(a, b, name, mode="nn", out_dtype=F32, second_dtype=None, wide=1024, side=None):
    if mode == "tn":
        k, m = a.shape
    else:
        m, k = a.shape
    if mode == "nt":
        n, k2 = b.shape
    else:
        k2, n = b.shape
    assert k == k2, (a.shape, b.shape, mode)
    tm, tn, tk = _tile(m, 1024), _tile(n, wide), _tile(k, 2048)
    grid = (m // tm, n // tn, k // tk)
    nk = grid[2]
    assert nk == 1 or (out_dtype == F32 and second_dtype is None)
    dot = {"nn": _nn, "nt": _nt, "tn": _tn}[mode]
    n_in = len(side.inputs) if side else 0
    n_out = len(side.out_shape) if side else 0
    n_res = 1 + (second_dtype is not None)

    def kern(*refs):
        a_ref, b_ref = refs[:2]
        res = refs[2 + n_in:2 + n_in + n_res]
        step = [pl.program_id(d) for d in range(3)]
        if side:
            side_refs = (refs[2:2 + n_in], refs[2 + n_in + n_res:2 + n_in + n_res + n_out],
                         refs[2 + n_in + n_res + n_out:])

            @pl.when((step[0] == 0) & (step[1] == 0) & (step[2] == 0))
            def _():
                side.phase("start", *side_refs)

        r = dot(a_ref[...].astype(BF16), b_ref[...].astype(BF16))
        if nk == 1:
            res[0][...] = r.astype(out_dtype)
            if n_res == 2:
                res[1][...] = r.astype(second_dtype)
        else:
            @pl.when(step[2] == 0)
            def _():
                res[0][...] = r

            @pl.when(step[2] > 0)
            def _():
                res[0][...] += r

        if side:
            @pl.when((step[0] == grid[0] - 1) & (step[1] == grid[1] - 1) & (step[2] == nk - 1))
            def _():
                side.phase("finish", *side_refs)

    a_spec = (pl.BlockSpec((tk, tm), lambda i, j, kk: (kk, i)) if mode == "tn"
              else pl.BlockSpec((tm, tk), lambda i, j, kk: (i, kk)))
    b_spec = (pl.BlockSpec((tn, tk), lambda i, j, kk: (j, kk)) if mode == "nt"
              else pl.BlockSpec((tk, tn), lambda i, j, kk: (kk, j)))
    o_spec = pl.BlockSpec((tm, tn), lambda i, j, kk: (i, j))
    out_specs = [o_spec] * n_res + [ANY] * n_out
    out_shape = [jax.ShapeDtypeStruct((m, n), out_dtype)]
    if second_dtype is not None:
        out_shape.append(jax.ShapeDtypeStruct((m, n), second_dtype))
    out_shape += list(side.out_shape) if side else []
    semantics = ("arbitrary",) * 3 if side else ("parallel", "parallel", "arbitrary")
    out = _pcall(
        kern, name=name, grid=grid, in_specs=[a_spec, b_spec] + [ANY] * n_in, out_specs=out_specs,
        out_shape=out_shape, scratch_shapes=list(side.scratch) if side else [],
        compiler_params=pltpu.CompilerParams(dimension_semantics=semantics),
    )(a, b, *(side.inputs if side else []))
    return out[0] if len(out) == 1 else out


def _rms_fwd(x, g, name):
    r, d = x.arr.shape[0], x.width
    tr = _tile(r, 512)

    def kern(x_ref, g_ref, y_ref, r_ref):
        xv = x_ref[...]
        rstd = lax.rsqrt(jnp.mean(xv * xv, axis=-1, keepdims=True) + EPS)
        y_ref[...] = (xv * rstd * g_ref[...]).astype(BF16)
        r_ref[...] = rstd

    return _pcall(
        kern, name=name, grid=(r // tr,),
        in_specs=[pl.BlockSpec((tr, d), lambda i: (i, x.col0)), pl.BlockSpec((1, d), lambda i: (0, 0))],
        out_specs=[pl.BlockSpec((tr, d), lambda i: (i, 0)), pl.BlockSpec((tr, 1), lambda i: (i, 0))],
        out_shape=[jax.ShapeDtypeStruct((r, d), BF16), jax.ShapeDtypeStruct((r, 1), F32)],
    )(x.arr, g)


def _rms_bwd(x, g, rstd, dy, add, name, dx_dtype=F32):
    r, d = x.arr.shape[0], x.width
    tr = _tile(r, 256)
    has_add = add is not None

    def kern(*refs):
        if has_add:
            x_ref, g_ref, r_ref, dy_ref, add_ref, dx_ref, dg_ref = refs
        else:
            x_ref, g_ref, r_ref, dy_ref, dx_ref, dg_ref = refs
        rs = r_ref[...]
        xhat = x_ref[...] * rs
        dyv = dy_ref[...]
        dyg = dyv * g_ref[...]
        c = jnp.mean(dyg * xhat, axis=-1, keepdims=True)
        dx = rs * (dyg - xhat * c)
        if has_add:
            dx = dx + add_ref[...]
        dx_ref[...] = dx.astype(dx_dtype)
        part = jnp.sum(dyv * xhat, axis=0, keepdims=True)

        @pl.when(pl.program_id(0) == 0)
        def _():
            dg_ref[...] = part

        @pl.when(pl.program_id(0) > 0)
        def _():
            dg_ref[...] += part

    row = pl.BlockSpec((tr, d), lambda i: (i, 0))
    vec = pl.BlockSpec((1, d), lambda i: (0, 0))
    ins = [pl.BlockSpec((tr, d), lambda i: (i, x.col0)), vec, pl.BlockSpec((tr, 1), lambda i: (i, 0)), row]
    ins += [row] if has_add else []
    args = (x.arr, g, rstd, dy) + ((add,) if has_add else ())
    return _pcall(
        kern, name=name, grid=(r // tr,), in_specs=ins, out_specs=[row, vec],
        out_shape=[jax.ShapeDtypeStruct((r, d), dx_dtype), jax.ShapeDtypeStruct((1, d), F32)],
        compiler_params=pltpu.CompilerParams(dimension_semantics=("arbitrary",)),
    )(*args)


def _final_loss(x, proj, g, target):
    r, d = x.shape
    tr = _tile(r, 256)

    def kern(x_ref, p_ref, g_ref, t_ref, dy_ref, dg_ref, loss_ref):
        y = x_ref[...] + p_ref[...]
        rs = lax.rsqrt(jnp.mean(y * y, axis=-1, keepdims=True) + EPS)
        yhat = y * rs
        gv = g_ref[...]
        e = yhat * gv - t_ref[...]
        lpart = 0.5 * jnp.sum(jnp.mean(e * e, axis=-1, keepdims=True), axis=0, keepdims=True)
        dout = e * (1.0 / d)
        dyg = dout * gv
        c = jnp.mean(dyg * yhat, axis=-1, keepdims=True)
        dy_ref[...] = rs * (dyg - yhat * c)
        gpart = jnp.sum(dout * yhat, axis=0, keepdims=True)
        lrow = jnp.broadcast_to(lpart, (1, LANE))

        @pl.when(pl.program_id(0) == 0)
        def _():
            dg_ref[...] = gpart
            loss_ref[...] = lrow

        @pl.when(pl.program_id(0) > 0)
        def _():
            dg_ref[...] += gpart
            loss_ref[...] += lrow

    row = pl.BlockSpec((tr, d), lambda i: (i, 0))
    vec = pl.BlockSpec((1, d), lambda i: (0, 0))
    return _pcall(
        kern, name="final_loss", grid=(r // tr,), in_specs=[row, row, vec, row],
        out_specs=[row, vec, pl.BlockSpec((1, LANE), lambda i: (0, 0))],
        out_shape=[jax.ShapeDtypeStruct((r, d), F32), jax.ShapeDtypeStruct((1, d), F32),
                   jax.ShapeDtypeStruct((1, LANE), F32)],
        compiler_params=pltpu.CompilerParams(dimension_semantics=("arbitrary",)),
    )(x, proj, g, target)


def _rope_fwd(x, cs, sn, nh, width, off, name):
    s = x.arr.shape[0]
    tr = _tile(s, 512)

    def kern(x_ref, c_ref, s_ref, o_ref):
        cv, sv = c_ref[...], s_ref[...]
        for h in range(nh):
            b = h * width
            if off:
                o_ref[:, b:b + off] = x_ref[:, b:b + off].astype(BF16)
            xr = x_ref[:, b + off:b + off + LANE]
            o_ref[:, b + off:b + off + LANE] = (xr * cv + pltpu.roll(xr, 32, 1) * sv).astype(BF16)

    tab = pl.BlockSpec((tr, LANE), lambda i: (i, 0))
    return _pcall(
        kern, name=name, grid=(s // tr,),
        in_specs=[pl.BlockSpec((tr, nh * width), lambda i: (i, x.col0)), tab, tab],
        out_specs=pl.BlockSpec((tr, nh * width), lambda i: (i, 0)),
        out_shape=jax.ShapeDtypeStruct((s, nh * width), BF16),
    )(x.arr, cs, sn)


def _rope_grad(d, cv, sv):
    g2 = d * sv
    g2 = g2 + pltpu.roll(g2, 64, 1)
    lane = lax.broadcasted_iota(jnp.int32, d.shape, 1)
    return jnp.where(lane < 64, d * cv + pltpu.roll(g2, 32, 1), 0.0)


def _rope_bwd_q(dq, cs, sn):
    s, w = dq.shape
    tr = _tile(s, 512)
    nh = w // 256

    def kern(d_ref, c_ref, s_ref, o_ref):
        cv, sv = c_ref[...], s_ref[...]
        for h in range(nh):
            b = h * 256
            o_ref[:, b:b + LANE] = d_ref[:, b:b + LANE]
            o_ref[:, b + LANE:b + 256] = _rope_grad(d_ref[:, b + LANE:b + 256], cv, sv)

    row = pl.BlockSpec((tr, w), lambda i: (i, 0))
    tab = pl.BlockSpec((tr, LANE), lambda i: (i, 0))
    return _pcall(kern, name="rope_bwd_q", grid=(s // tr,), in_specs=[row, tab, tab], out_specs=row,
                  out_shape=jax.ShapeDtypeStruct((s, w), F32))(dq, cs, sn)


def _rope_bwd_k(dk_nope, dk_pe, dv, cs, sn):
    s, w = dk_nope.shape
    tr = _tile(s, 512)

    def kern(dk_ref, dp_ref, dv_ref, c_ref, s_ref, okv_ref, okr_ref):
        okv_ref[:, :w] = dk_ref[...]
        okv_ref[:, w:] = dv_ref[...]
        okr_ref[...] = _rope_grad(dp_ref[...], c_ref[...], s_ref[...])

    tab = pl.BlockSpec((tr, LANE), lambda i: (i, 0))
    wide = pl.BlockSpec((tr, w), lambda i: (i, 0))
    return _pcall(
        kern, name="rope_bwd_k", grid=(s // tr,), in_specs=[wide, tab, wide, tab, tab],
        out_specs=[pl.BlockSpec((tr, 2 * w), lambda i: (i, 0)), tab],
        out_shape=[jax.ShapeDtypeStruct((s, 2 * w), F32), jax.ShapeDtypeStruct((s, LANE), F32)],
    )(dk_nope, dk_pe, dv, cs, sn)


class _Attn:
    def __init__(self, mode, s, sk, heads, dk):
        self.mode, self.s, self.sk, self.h, self.dk = mode, s, sk, heads, dk
        self.scale = {"mla": 192 ** -0.5, "mem": 128 ** -0.5}.get(mode, NSA_DK ** -0.5)
        self.tb = min(256, s)
        self.nb = s // self.tb
        self.nsub = 2 if self.nb % 2 == 0 else 1
        self.tq = self.tb * self.nsub
        self.fchains = 1
        self.nq = s // self.tq
        self.qpb = 4 if self.nq % 4 == 0 and mode == "mla" else 2
        self.causal = mode in ("mla", "slc")
        if self.causal:
            self.tk = self.tq
        elif mode == "win":
            self.tk = WIN + self.tb
        else:
            self.tk = sk
        self.tkb = min(512, sk)
        self.ksub = 1
        self.kb = self.tkb // self.ksub
        self.ncmp = s // CMP_STRIDE - 1

    def mask_bias(self, t, n, h, selx, diag):
        m = self.mode
        if m == "mla":
            return (n <= t) if diag else None, None
        if m == "mem":
            return None, None
        slope = jnp.where(h == 0, 0.25, jnp.where(h == 1, 0.0625, jnp.where(h == 2, 0.015625, 0.00390625)))
        slope = slope.astype(F32) * LOG2E
        if m == "cmp":
            mask = (n * CMP_STRIDE + (CMP_LEN - 1) <= t) & (n < self.ncmp)
            pos = n.astype(F32) * float(CMP_STRIDE) + (CMP_LEN - 1) / 2.0
            return mask, slope * pos
        rel = t - n
        if m == "slc":
            return (rel >= 0) if diag else None, slope * n.astype(F32)
        return (rel >= 0) & (rel < WIN), slope * n.astype(F32)


def _scores(cfg, s_raw, t, n, h, selx, diag, lse=None):
    s = s_raw * (cfg.scale * LOG2E)
    mask, key_term = cfg.mask_bias(t, n, h, selx, diag)
    if key_term is not None:
        s = s + key_term
    if selx is not None:
        s = s + selx
    if lse is None:
        if mask is not None:
            s = jnp.where(mask, s, NEG)
        return s, mask
    p = jnp.exp2(jnp.minimum(s - lse, 0.0))
    if mask is not None:
        p = jnp.where(mask, p, 0.0)
    return p, mask


def _block_of_key(k0, tk, keys_on_rows, value=NEG):
    shape = (tk, LANE) if keys_on_rows else (LANE, tk)
    n = lax.broadcasted_iota(jnp.int32, shape, 0 if keys_on_rows else 1) + k0
    j = lax.broadcasted_iota(jnp.int32, shape, 1 if keys_on_rows else 0)
    return jnp.where((n >> 6) == j, value, 0.0).astype(BF16)


def _to_row(col):
    t = col.shape[0]
    return jnp.transpose(jnp.broadcast_to(col, (t, LANE)))[0:1, :]


def _load_keys(k_refs, rows):
    parts = [r[rows, :].astype(BF16) for r in k_refs]
    return parts[0] if len(parts) == 1 else jnp.concatenate(parts, axis=1)


def _attn_fwd(cfg, q, ks, v, sel, name):
    s, tq, tk, nsub = cfg.s, cfg.tq, cfg.tk, cfg.fchains
    tb = tq // nsub
    per = tb // cfg.tb
    has_sel = sel is not None
    nkp = len(ks)
    qpb = cfg.qpb
    assert cfg.causal and tq == tk and qpb % 2 == 0 and cfg.nq % qpb == 0

    def kern(*refs):
        q_ref, k_refs, v_ref = refs[0], refs[1:1 + nkp], refs[1 + nkp]
        sel_ref = refs[2 + nkp] if has_sel else None
        o_ref, lc_ref, lr_ref = refs[2 + nkp + has_sel:5 + nkp + has_sel]
        buf_a, buf_b = refs[-2:]
        h, g = pl.program_id(0), pl.program_id(1)

        def block(b):
            rows = [slice(b * tq + r * tb, b * tq + (r + 1) * tb) for r in range(nsub)]
            qs = [q_ref[p, :].astype(BF16) for p in rows]
            ts = [(qpb * g + b) * tq + r * tb + lax.broadcasted_iota(jnp.int32, (tb, 1), 0) for r in range(nsub)]
            sels = [sel_ref[p, :].astype(BF16) for p in rows] if has_sel else None
            return rows, qs, ts, sels

        def scores_into(buf, blk, c):
            kk = _load_keys(k_refs, pl.ds(pl.multiple_of(c * tk, tk), tk))
            for r in range(nsub):
                buf[r] = _nt(blk[1][r], kk)

        def consume(buf, blk, c, carry, diag):
            _, _, ts, sels = blk
            k0 = pl.multiple_of(c * tk, tk)
            vv = v_ref[pl.ds(k0, tk), :].astype(BF16)
            emat = _block_of_key(k0, tk, False) if has_sel else None
            n = k0 + lax.broadcasted_iota(jnp.int32, (1, tk), 1)
            new = []
            for r in range(nsub):
                m, l, acc = carry[r]
                selx = _nn(sels[r], emat) if has_sel else None
                sc, mask = _scores(cfg, buf[r], ts[r], n, h, selx, diag)
                m_new = jnp.maximum(m, jnp.max(sc, axis=1, keepdims=True))
                alpha = jnp.exp2(m - m_new)
                p = jnp.exp2(sc - m_new)
                if mask is not None:
                    p = jnp.where(mask, p, 0.0)
                l = alpha * l + jnp.sum(p, axis=1, keepdims=True)
                new.append((m_new, l, alpha * acc + _nn(p.astype(BF16), vv)))
            return tuple(new)

        def finish(blk, b, carry):
            for r, (m, l, acc) in enumerate(carry):
                o_ref[blk[0][r], :] = acc / (l + 1e-20)
                lse = m + jnp.log(l + 1e-20) * LOG2E
                lc_ref[0, blk[0][r], :] = lse
                for u in range(per):
                    lr_ref[0, (b * nsub + r) * per + u] = _to_row(lse[u * cfg.tb:(u + 1) * cfg.tb])

        def pairs(blk, first, other):
            def pair(p, cr):
                scores_into(other, blk, 2 * p + 1)
                cr = consume(first, blk, 2 * p, cr, False)
                scores_into(first, blk, 2 * p + 2)
                return consume(other, blk, 2 * p + 1, cr, False)
            return pair

        init = ((jnp.full((tb, 1), NEG, F32), jnp.zeros((tb, 1), F32), jnp.zeros((tb, HEAD_V), F32)),) * nsub
        cur, oth = buf_a, buf_b
        blk = block(0)
        scores_into(cur, blk, 0)
        for b in range(qpb):
            full = qpb * g + b
            carry = lax.fori_loop(0, (qpb // 2) * g + b // 2, pairs(blk, cur, oth), init)
            nxt = block(b + 1) if b + 1 < qpb else None
            if b % 2 == 0:
                if nxt:
                    scores_into(oth, nxt, 0)
                finish(blk, b, consume(cur, blk, full, carry, True))
                cur, oth = oth, cur
            else:
                scores_into(oth, blk, full)
                carry = consume(cur, blk, full - 1, carry, False)
                if nxt:
                    scores_into(cur, nxt, 0)
                finish(blk, b, consume(oth, blk, full, carry, True))
            blk = nxt

    rows_step = qpb * tq
    ins = [pl.BlockSpec((rows_step, q.width), lambda h, g: (g, q.col(h)))]
    ins += [pl.BlockSpec((cfg.sk, p.width), lambda h, g, p=p: (0, p.col(h))) for p in ks]
    ins += [pl.BlockSpec((cfg.sk, HEAD_V), lambda h, g: (0, v.col(h)))]
    args = [q.arr] + [p.arr for p in ks] + [v.arr]
    if has_sel:
        ins.append(pl.BlockSpec((rows_step, LANE), lambda h, g: (g, 0)))
        args.append(sel)
    return _pcall(
        kern, name=name, grid=(cfg.h, cfg.nq // qpb), in_specs=ins,
        out_specs=[pl.BlockSpec((rows_step, HEAD_V), lambda h, g: (g, h)),
                   pl.BlockSpec((1, rows_step, 1), lambda h, g: (h, g, 0)),
                   pl.BlockSpec((1, rows_step // cfg.tb, 1, cfg.tb), lambda h, g: (h, g, 0, 0))],
        out_shape=[jax.ShapeDtypeStruct((s, cfg.h * HEAD_V), F32),
                   jax.ShapeDtypeStruct((cfg.h, s, 1), F32),
                   jax.ShapeDtypeStruct((cfg.h, cfg.nb, 1, cfg.tb), F32)],
        scratch_shapes=[pltpu.VMEM((nsub, tb, tk), F32)] * 2,
        compiler_params=pltpu.CompilerParams(dimension_semantics=("parallel", "parallel")),
    )(*args)


def _attn_dq(cfg, q, ks, v, sel, o, lse, do, dq_in, name):
    s, tq, tk, dk, nsub = cfg.s, cfg.tq, cfg.tk, cfg.dk, cfg.fchains
    tb = tq // nsub
    per = tb // cfg.tb
    has_sel = sel is not None
    has_in = dq_in is not None
    nkp = len(ks)

    def kern(*refs):
        refs = list(refs)
        q_ref, k_refs, v_ref = refs[0], refs[1:1 + nkp], refs[1 + nkp]
        p0 = 2 + nkp
        sel_ref = refs[p0] if has_sel else None
        p0 += has_sel
        o_ref, l_ref, do_ref = refs[p0:p0 + 3]
        p0 += 3
        in_ref = refs[p0] if has_in else None
        p0 += has_in
        dq_ref, dr_ref = refs[p0:p0 + 2]
        sa, pa, sb, pb = refs[-4:]
        h, g = pl.program_id(0), pl.program_id(1)

        def block(b):
            rows = [slice(b * tq + r * tb, b * tq + (r + 1) * tb) for r in range(nsub)]
            qs = [q_ref[p, :].astype(BF16) for p in rows]
            ts = [(qpb * g + b) * tq + r * tb + lax.broadcasted_iota(jnp.int32, (tb, 1), 0) for r in range(nsub)]
            sels = [sel_ref[p, :].astype(BF16) for p in rows] if has_sel else None
            dvecs, dobs, lses = [], [], []
            for r, p in enumerate(rows):
                dov = do_ref[p, :]
                dvec = jnp.sum(dov * o_ref[p, :], axis=1, keepdims=True)
                for u in range(per):
                    dr_ref[0, (b * nsub + r) * per + u] = _to_row(dvec[u * cfg.tb:(u + 1) * cfg.tb])
                dvecs.append(dvec)
                dobs.append(dov.astype(BF16))
                lses.append(l_ref[0, p, :])
            return rows, qs, ts, sels, dvecs, dobs, lses

        def products_into(sbuf, pbuf, blk, c):
            rows = pl.ds(pl.multiple_of(c * tk, tk), tk)
            kk, vv = _load_keys(k_refs, rows), v_ref[rows, :].astype(BF16)
            for r in range(nsub):
                sbuf[r] = _nt(blk[1][r], kk)
                pbuf[r] = _nt(blk[5][r], vv)

        def consume(sbuf, pbuf, blk, c, accs, diag):
            _, _, ts, sels, dvecs, _, lses = blk
            k0 = pl.multiple_of(c * tk, tk)
            kk = _load_keys(k_refs, pl.ds(k0, tk))
            emat = _block_of_key(k0, tk, False) if has_sel else None
            n = k0 + lax.broadcasted_iota(jnp.int32, (1, tk), 1)
            new = []
            for r in range(nsub):
                selx = _nn(sels[r], emat) if has_sel else None
                p, _ = _scores(cfg, sbuf[r], ts[r], n, h, selx, diag, lses[r])
                ds = p * (pbuf[r] - dvecs[r])
                new.append(accs[r] + _nn(ds.astype(BF16), kk))
            return tuple(new)

        def finish(blk, accs):
            for r, p in enumerate(blk[0]):
                dq_ref[p, :] = accs[r] * cfg.scale + in_ref[p, :] if has_in else accs[r] * cfg.scale

        def pairs(blk, first, other):
            def pair(p, ac):
                products_into(*other, blk, 2 * p + 1)
                ac = consume(*first, blk, 2 * p, ac, False)
                products_into(*first, blk, 2 * p + 2)
                return consume(*other, blk, 2 * p + 1, ac, False)
            return pair

        zero = (jnp.zeros((tb, dk), F32),) * nsub
        cur, oth = (sa, pa), (sb, pb)
        blk = block(0)
        products_into(*cur, blk, 0)
        for b in range(qpb):
            full = qpb * g + b
            accs = lax.fori_loop(0, (qpb // 2) * g + b // 2, pairs(blk, cur, oth), zero)
            nxt = block(b + 1) if b + 1 < qpb else None
            if b % 2 == 0:
                if nxt:
                    products_into(*oth, nxt, 0)
                finish(blk, consume(*cur, blk, full, accs, True))
                cur, oth = oth, cur
            else:
                products_into(*oth, blk, full)
                accs = consume(*cur, blk, full - 1, accs, False)
                if nxt:
                    products_into(*cur, nxt, 0)
                finish(blk, consume(*oth, blk, full, accs, True))
            blk = nxt

    qpb = cfg.qpb
    assert cfg.causal and tq == tk and qpb % 2 == 0 and cfg.nq % qpb == 0
    rows_step = qpb * tq
    qs = pl.BlockSpec((rows_step, dk), lambda h, g: (g, h))
    ins = [pl.BlockSpec((rows_step, q.width), lambda h, g: (g, q.col(h)))]
    ins += [pl.BlockSpec((cfg.sk, p.width), lambda h, g, p=p: (0, p.col(h))) for p in ks]
    ins += [pl.BlockSpec((cfg.sk, HEAD_V), lambda h, g: (0, v.col(h)))]
    args = [q.arr] + [p.arr for p in ks] + [v.arr]
    if has_sel:
        ins.append(pl.BlockSpec((rows_step, LANE), lambda h, g: (g, 0)))
        args.append(sel)
    ins += [pl.BlockSpec((rows_step, HEAD_V), lambda h, g: (g, o.col(h))),
            pl.BlockSpec((1, rows_step, 1), lambda h, g: (h, g, 0)),
            pl.BlockSpec((rows_step, HEAD_V), lambda h, g: (g, do.col(h)))]
    args += [o.arr, lse, do.arr]
    if has_in:
        ins.append(qs)
        args.append(dq_in)
    return _pcall(
        kern, name=name, grid=(cfg.h, cfg.nq // qpb), in_specs=ins,
        out_specs=[qs, pl.BlockSpec((1, rows_step // cfg.tb, 1, cfg.tb), lambda h, g: (h, g, 0, 0))],
        out_shape=[jax.ShapeDtypeStruct((s, cfg.h * dk), F32),
                   jax.ShapeDtypeStruct((cfg.h, cfg.nb, 1, cfg.tb), F32)],
        scratch_shapes=[pltpu.VMEM((nsub, tb, tk), F32)] * 4,
        compiler_params=pltpu.CompilerParams(dimension_semantics=("parallel", "parallel")),
    )(*args)


def _attn_dkv(cfg, q, ks, v, selt, lse_r, d_r, do, name):
    s, tq, tkb, dk, kb, ksub = cfg.s, cfg.tb, cfg.tkb, cfg.dk, cfg.kb, cfg.ksub
    nq = cfg.nb
    has_sel = selt is not None
    nkp = len(ks)
    outs = list(ks) + [v]

    def kern(*refs):
        k_refs, v_ref = refs[:nkp], refs[nkp]
        q_ref, do_ref, lr_ref, dr_ref = refs[nkp + 1:nkp + 5]
        st_ref = refs[nkp + 5] if has_sel else None
        out_refs = refs[nkp + 5 + has_sel:2 * nkp + 6 + has_sel]
        sa, pa, sb, pb = refs[-4:]
        j, h = pl.program_id(0), pl.program_id(1)
        k0 = j * tkb
        part = [slice(u * kb, (u + 1) * kb) for u in range(ksub)]
        kks = [_load_keys(k_refs, p) for p in part]
        vvs = [v_ref[p, :].astype(BF16) for p in part]
        ns = [k0 + u * kb + lax.broadcasted_iota(jnp.int32, (kb, 1), 0) for u in range(ksub)]
        emats = [_block_of_key(k0 + u * kb, kb, True) for u in range(ksub)] if has_sel else None

        def load_q(i):
            rows = pl.ds(pl.multiple_of(i * tq, tq), tq)
            return q_ref[rows, :].astype(BF16), do_ref[rows, :].astype(BF16)

        def products_into(sbuf, pbuf, i):
            qi, doi = load_q(i)
            for u in range(ksub):
                sbuf[u] = _nt(kks[u], qi)
                pbuf[u] = _nt(vvs[u], doi)

        def consume(sbuf, pbuf, i, carry):
            qi, doi = load_q(i)
            t = i * tq + lax.broadcasted_iota(jnp.int32, (1, tq), 1)
            selt_i = st_ref[i].astype(BF16) if has_sel else None
            new = []
            for u in range(ksub):
                dk_acc, dv_acc = carry[u]
                selx = _nn(emats[u], selt_i) if has_sel else None
                pt, _ = _scores(cfg, sbuf[u], t, ns[u], h, selx, True, lr_ref[0, i])
                dv_acc = dv_acc + _nn(pt.astype(BF16), doi)
                dst = pt * (pbuf[u] - dr_ref[0, i])
                new.append((dk_acc + _nn(dst.astype(BF16), qi), dv_acc))
            return tuple(new)

        if cfg.causal:
            first, count = k0 // tq, nq - k0 // tq
        elif cfg.mode == "win":
            first = k0 // tq
            count = jnp.minimum((k0 + tkb + WIN - 2) // tq + 1, nq) - first
        else:
            first, count = 0, nq

        def pair(p, cr):
            i0 = first + 2 * p
            products_into(sb, pb, i0 + 1)
            cr = consume(sa, pa, i0, cr)
            products_into(sa, pa, i0 + 2)
            return consume(sb, pb, i0 + 1, cr)

        carry = ((jnp.zeros((kb, dk), F32), jnp.zeros((kb, HEAD_V), F32)),) * ksub
        products_into(sa, pa, first)
        carry = lax.fori_loop(0, count // 2 - 1, pair, carry)
        last = first + count - 2
        products_into(sb, pb, last + 1)
        carry = consume(sa, pa, last, carry)
        carry = consume(sb, pb, last + 1, carry)
        for u, (dk_acc, dv_acc) in enumerate(carry):
            vals, off = [], 0
            for p in ks:
                vals.append(dk_acc[:, off:off + p.width] * cfg.scale)
                off += p.width
            vals.append(dv_acc)
            for src, ref, val in zip(outs, out_refs, vals):
                if src.per_head:
                    ref[part[u], :] = val
                else:
                    @pl.when(h == 0)
                    def _(ref=ref, val=val, u=u):
                        ref[part[u], :] = val

                    @pl.when(h > 0)
                    def _(ref=ref, val=val, u=u):
                        ref[part[u], :] += val

    rowv = pl.BlockSpec((1, nq, 1, tq), lambda j, h: (h, 0, 0, 0))
    ins = [pl.BlockSpec((tkb, p.width), lambda j, h, p=p: (j, p.col(h))) for p in ks]
    ins += [pl.BlockSpec((tkb, HEAD_V), lambda j, h: (j, v.col(h))),
            pl.BlockSpec((s, q.width), lambda j, h: (0, q.col(h))),
            pl.BlockSpec((s, HEAD_V), lambda j, h: (0, do.col(h))), rowv, rowv]
    args = [p.arr for p in ks] + [v.arr, q.arr, do.arr, lse_r, d_r]
    if has_sel:
        ins.append(pl.BlockSpec((nq, LANE, tq), lambda j, h: (0, 0, 0)))
        args.append(selt)
    out_specs = [pl.BlockSpec((tkb, p.width), lambda j, h, p=p: (j, h if p.per_head else 0)) for p in outs]
    out_shape = [jax.ShapeDtypeStruct((cfg.sk, (cfg.h if p.per_head else 1) * p.width), F32) for p in outs]
    assert nq % 2 == 0 and (cfg.mode in ("cmp", "mem") or tkb % (2 * tq) == 0), (nq, tkb, tq)
    return _pcall(
        kern, name=name, grid=(cfg.sk // tkb, cfg.h), in_specs=ins, out_specs=out_specs, out_shape=out_shape,
        scratch_shapes=[pltpu.VMEM((ksub, kb, tq), F32)] * 4,
        compiler_params=pltpu.CompilerParams(dimension_semantics=("parallel", "arbitrary")),
    )(*args)


def _attn_dkv_flat(cfg, q, ks, v, selt, lse_r, d_r, do, name):
    s, tq, tkb, dk, kb, ksub = cfg.s, cfg.tb, cfg.tkb, cfg.dk, cfg.kb, cfg.ksub
    nq = cfg.nb
    has_sel = selt is not None
    nkp = len(ks)
    outs = list(ks) + [v]
    assert nq % 2 == 0 and tkb % (2 * tq) == 0, (nq, tkb, tq)
    steps = []
    for j in range(cfg.sk // tkb):
        first = j * tkb // tq
        stop = nq if cfg.causal else min((j * tkb + tkb + WIN - 2) // tq + 1, nq)
        steps += [(j, i0) for i0 in range(first, stop, 2)]
    n_pairs = len(steps)
    steps.append(steps[-1])
    tab_j = jnp.asarray(np.array([p[0] for p in steps], np.int32))
    tab_i = jnp.asarray(np.array([p[1] for p in steps], np.int32))

    def kern(tj_ref, ti_ref, *refs):
        k_refs, v_ref = refs[:nkp], refs[nkp]
        q_ref, do_ref, lr_ref, dr_ref = refs[nkp + 1:nkp + 5]
        st_ref = refs[nkp + 5] if has_sel else None
        out_refs = refs[nkp + 5 + has_sel:2 * nkp + 6 + has_sel]
        sa, pa, sb, pb = refs[-4:]
        h = pl.program_id(0)
        for src, ref in zip(outs, out_refs):
            if src.per_head:
                ref[...] = jnp.zeros_like(ref)
            else:
                @pl.when(h == 0)
                def _(ref=ref):
                    ref[...] = jnp.zeros_like(ref)

        def key_rows(j, u):
            return pl.ds(pl.multiple_of(j * tkb + u * kb, kb), kb)

        def load_q(i):
            rows = pl.ds(pl.multiple_of(i * tq, tq), tq)
            return q_ref[rows, :].astype(BF16), do_ref[rows, :].astype(BF16)

        def products_into(sbuf, pbuf, j, i):
            qi, doi = load_q(i)
            for u in range(ksub):
                rows = key_rows(j, u)
                sbuf[u] = _nt(_load_keys(k_refs, rows), qi)
                pbuf[u] = _nt(v_ref[rows, :].astype(BF16), doi)

        def consume(sbuf, pbuf, j, i):
            qi, doi = load_q(i)
            t = i * tq + lax.broadcasted_iota(jnp.int32, (1, tq), 1)
            selt_i = st_ref[i].astype(BF16) if has_sel else None
            res = []
            for u in range(ksub):
                k0 = j * tkb + u * kb
                n = k0 + lax.broadcasted_iota(jnp.int32, (kb, 1), 0)
                selx = _nn(_block_of_key(k0, kb, True), selt_i) if has_sel else None
                pt, _ = _scores(cfg, sbuf[u], t, n, h, selx, True, lr_ref[0, i])
                dst = pt * (pbuf[u] - dr_ref[0, i])
                res.append((_nn(dst.astype(BF16), qi), _nn(pt.astype(BF16), doi)))
            return res

        def pair(p, carry):
            j, i0 = tj_ref[p], ti_ref[p]
            products_into(sb, pb, j, i0 + 1)
            ca = consume(sa, pa, j, i0)
            products_into(sa, pa, tj_ref[p + 1], ti_ref[p + 1])
            cb = consume(sb, pb, j, i0 + 1)
            for u in range(ksub):
                rows = key_rows(j, u)
                dk_c = (ca[u][0] + cb[u][0]) * cfg.scale
                off = 0
                for src, ref in zip(ks, out_refs):
                    ref[rows, :] += dk_c[:, off:off + src.width]
                    off += src.width
                out_refs[nkp][rows, :] += ca[u][1] + cb[u][1]
            return carry

        products_into(sa, pa, tj_ref[0], ti_ref[0])
        lax.fori_loop(0, n_pairs, pair, 0)

    rowv = pl.BlockSpec((1, nq, 1, tq), lambda h, tj, ti: (h, 0, 0, 0))
    ins = [pl.BlockSpec((cfg.sk, p.width), lambda h, tj, ti, p=p: (0, p.col(h))) for p in ks]
    ins += [pl.BlockSpec((cfg.sk, HEAD_V), lambda h, tj, ti: (0, v.col(h))),
            pl.BlockSpec((s, q.width), lambda h, tj, ti: (0, q.col(h))),
            pl.BlockSpec((s, HEAD_V), lambda h, tj, ti: (0, do.col(h))), rowv, rowv]
    args = [p.arr for p in ks] + [v.arr, q.arr, do.arr, lse_r, d_r]
    if has_sel:
        ins.append(pl.BlockSpec((nq, LANE, tq), lambda h, tj, ti: (0, 0, 0)))
        args.append(selt)
    out_specs = [pl.BlockSpec((cfg.sk, p.width), lambda h, tj, ti, p=p: (0, h if p.per_head else 0))
                 for p in outs]
    out_shape = [jax.ShapeDtypeStruct((cfg.sk, (cfg.h if p.per_head else 1) * p.width), F32) for p in outs]
    grid_spec = pltpu.PrefetchScalarGridSpec(
        num_scalar_prefetch=2, grid=(cfg.h,), in_specs=ins, out_specs=out_specs,
        scratch_shapes=[pltpu.VMEM((ksub, kb, tq), F32)] * 4)
    return _pcall(kern, name=name, grid_spec=grid_spec, out_shape=out_shape,
                  compiler_params=pltpu.CompilerParams(dimension_semantics=("arbitrary",)))(tab_j, tab_i, *args)


def _all_heads(cfg, src, rows, key=False):
    if src.per_head:
        assert src.col0 % cfg.h == 0
        width, col = cfg.h * src.width, src.col0 // cfg.h
    else:
        width, col = src.width, src.col0
    return pl.BlockSpec((rows, width), (lambda i: (0, col)) if key else (lambda i: (i, col)))


def _head_cols(src, hh):
    return slice(hh * src.width, (hh + 1) * src.width) if src.per_head else slice(None)


def _key_window(cfg, i, r):
    if cfg.mode == "win":
        return pl.ds(pl.multiple_of(jnp.maximum(i * cfg.tq + r * cfg.tb - WIN, 0), cfg.tb), cfg.tk)
    return pl.ds(0, cfg.tk)


def _attn_fwd_small(cfg, q, ks, v, name, overlap=None):
    s, tq, tk, tb, nsub, nh = cfg.s, cfg.tq, cfg.tk, cfg.tb, cfg.nsub, cfg.h
    nkp = len(ks)
    select = overlap is not None
    n_s = s // SLC_LEN
    top_n = min(SLC_TOPN, n_s)

    def kern(*refs):
        q_ref, k_refs, v_ref = refs[0], refs[1:1 + nkp], refs[1 + nkp]
        ov_ref = refs[2 + nkp] if select else None
        o_ref, lc_ref, lr_ref = refs[2 + nkp + select:5 + nkp + select]
        i = pl.program_id(0)
        imps = [jnp.zeros((tb, LANE), F32)] * nsub
        for r in range(nsub):
            rows = slice(r * tb, (r + 1) * tb)
            t = i * tq + r * tb + lax.broadcasted_iota(jnp.int32, (tb, 1), 0)
            win = _key_window(cfg, i, r)
            n = win.start + lax.broadcasted_iota(jnp.int32, (1, tk), 1)
            for hh in range(nh):
                qv = q_ref[rows, hh * cfg.dk:(hh + 1) * cfg.dk].astype(BF16)
                kk = _load_keys([kr.at[:, _head_cols(p, hh)] for kr, p in zip(k_refs, ks)], win)
                vv = v_ref[win, _head_cols(v, hh)].astype(BF16)
                sc, mask = _scores(cfg, _nt(qv, kk), t, n, hh, None, True)
                m = jnp.max(sc, axis=1, keepdims=True)
                e = jnp.exp2(sc - m)
                if mask is not None:
                    e = jnp.where(mask, e, 0.0)
                l = jnp.sum(e, axis=1, keepdims=True)
                o_ref[rows, hh * HEAD_V:(hh + 1) * HEAD_V] = _nn(e.astype(BF16), vv) / (l + 1e-20)
                lse = m + jnp.log(l + 1e-20) * LOG2E
                lc_ref[hh, rows, :] = lse
                lr_ref[hh, r] = _to_row(lse)
                if select:
                    imps[r] = imps[r] + _nn((e / (l + 1e-20)).astype(BF16), ov_ref[...])
        if select:
            sel_ref, selt_ref, imp_t = refs[5 + nkp + select:8 + nkp + select]
            for r in range(nsub):
                t = i * tq + r * tb + lax.broadcasted_iota(jnp.int32, (tb, 1), 0)
                j = lax.broadcasted_iota(jnp.int32, (tb, LANE), 1)
                cur = t >> 6
                imp = jnp.where((j == 0) | (j == cur) | (j == cur - 1), 1e9, imps[r])
                imp = jnp.where(j > cur, -1e9, imp)
                imp_t[r] = jnp.transpose(imp)
                mine = imp_t[r, 0:n_s, :]
                jrow = lax.broadcasted_iota(jnp.int32, (n_s, tb), 0)

                def count(k, rank):
                    other = imp_t[r, pl.ds(k, 1), :]
                    ahead = (other > mine) | ((other == mine) & (k < jrow))
                    return rank + jnp.where(ahead, 1.0, 0.0)

                rank = lax.fori_loop(0, n_s, count, jnp.zeros((n_s, tb), F32))
                cur_t = (i * tq + r * tb + lax.broadcasted_iota(jnp.int32, (1, tb), 1)) >> 6
                rejected = jnp.where((rank < top_n) & (jrow <= cur_t), 0.0, 1.0)
                if n_s < LANE:
                    rejected = jnp.concatenate([rejected, jnp.ones((LANE - n_s, tb), F32)], axis=0)
                selt_ref[r] = rejected
                sel_ref[r * tb:(r + 1) * tb, :] = jnp.transpose(rejected)

    ins = [_all_heads(cfg, q, tq)] + [_all_heads(cfg, p, cfg.sk, True) for p in ks]
    ins += [_all_heads(cfg, v, cfg.sk, True)]
    args = [q.arr] + [p.arr for p in ks] + [v.arr]
    out_specs = [pl.BlockSpec((tq, nh * HEAD_V), lambda i: (i, 0)),
                 pl.BlockSpec((nh, tq, 1), lambda i: (0, i, 0)),
                 pl.BlockSpec((nh, nsub, 1, tb), lambda i: (0, i, 0, 0))]
    out_shape = [jax.ShapeDtypeStruct((s, nh * HEAD_V), F32), jax.ShapeDtypeStruct((nh, s, 1), F32),
                 jax.ShapeDtypeStruct((nh, cfg.nb, 1, tb), F32)]
    scratch = []
    if select:
        ins.append(pl.BlockSpec((cfg.sk, LANE), lambda i: (0, 0)))
        args.append(overlap)
        out_specs += [pl.BlockSpec((tq, LANE), lambda i: (i, 0)), pl.BlockSpec((nsub, LANE, tb), lambda i: (i, 0, 0))]
        out_shape += [jax.ShapeDtypeStruct((s, LANE), F32), jax.ShapeDtypeStruct((cfg.nb, LANE, tb), F32)]
        scratch = [pltpu.VMEM((nsub, LANE, tb), F32)]
    return _pcall(kern, name=name, grid=(cfg.nq,), in_specs=ins, out_specs=out_specs, out_shape=out_shape,
                  scratch_shapes=scratch,
                  compiler_params=pltpu.CompilerParams(dimension_semantics=("parallel",)))(*args)


def _attn_dq_small(cfg, q, ks, v, o, lse, do, dq_in, name):
    s, tq, tk, tb, nsub, nh, dk = cfg.s, cfg.tq, cfg.tk, cfg.tb, cfg.nsub, cfg.h, cfg.dk
    nkp = len(ks)
    has_in = dq_in is not None

    def kern(*refs):
        q_ref, k_refs, v_ref = refs[0], refs[1:1 + nkp], refs[1 + nkp]
        o_ref, l_ref, do_ref = refs[2 + nkp:5 + nkp]
        in_ref = refs[5 + nkp] if has_in else None
        dq_ref, dr_ref = refs[5 + nkp + has_in:7 + nkp + has_in]
        i = pl.program_id(0)
        for r in range(nsub):
            rows = slice(r * tb, (r + 1) * tb)
            t = i * tq + r * tb + lax.broadcasted_iota(jnp.int32, (tb, 1), 0)
            win = _key_window(cfg, i, r)
            n = win.start + lax.broadcasted_iota(jnp.int32, (1, tk), 1)
            for hh in range(nh):
                vcols = slice(hh * HEAD_V, (hh + 1) * HEAD_V)
                qcols = slice(hh * dk, (hh + 1) * dk)
                qv = q_ref[rows, qcols].astype(BF16)
                kk = _load_keys([kr.at[:, _head_cols(p, hh)] for kr, p in zip(k_refs, ks)], win)
                vv = v_ref[win, _head_cols(v, hh)].astype(BF16)
                dov = do_ref[rows, vcols]
                dvec = jnp.sum(dov * o_ref[rows, vcols], axis=1, keepdims=True)
                dr_ref[hh, r] = _to_row(dvec)
                p, _ = _scores(cfg, _nt(qv, kk), t, n, hh, None, True, l_ref[hh, rows, :])
                ds = p * (_nt(dov.astype(BF16), vv) - dvec)
                dq = _nn(ds.astype(BF16), kk) * cfg.scale
                dq_ref[rows, qcols] = dq + in_ref[rows, qcols] if has_in else dq

    qs = pl.BlockSpec((tq, nh * dk), lambda i: (i, 0))
    ins = [_all_heads(cfg, q, tq)] + [_all_heads(cfg, p, cfg.sk, True) for p in ks]
    ins += [_all_heads(cfg, v, cfg.sk, True)]
    ins += [_all_heads(cfg, o, tq), pl.BlockSpec((nh, tq, 1), lambda i: (0, i, 0)), _all_heads(cfg, do, tq)]
    args = [q.arr] + [p.arr for p in ks] + [v.arr, o.arr, lse, do.arr]
    if has_in:
        ins.append(qs)
        args.append(dq_in)
    return _pcall(
        kern, name=name, grid=(cfg.nq,), in_specs=ins,
        out_specs=[qs, pl.BlockSpec((nh, nsub, 1, tb), lambda i: (0, i, 0, 0))],
        out_shape=[jax.ShapeDtypeStruct((s, nh * dk), F32), jax.ShapeDtypeStruct((nh, cfg.nb, 1, tb), F32)],
        compiler_params=pltpu.CompilerParams(dimension_semantics=("parallel",)))(*args)


def _attn_bwd(cfg, q, ks, v, sel, selt, o, lse, lse_r, do, dq_in, name):
    if cfg.causal:
        dq, d_r = _attn_dq(cfg, q, ks, v, sel, o, lse, do, dq_in, name + "_dq")
    else:
        dq, d_r = _attn_dq_small(cfg, q, ks, v, o, lse, do, dq_in, name + "_dq")
    dkv = _attn_dkv_flat if cfg.causal or cfg.mode == "win" else _attn_dkv
    res = dkv(cfg, q, ks, v, selt, lse_r, d_r, do, name + "_dkv")
    return dq, res[:-1], res[-1]


def _silu_grad(pre):
    sg = _sigmoid(pre)
    return sg * (1.0 + pre * (1.0 - sg))


def _compress_fwd(a_lo, a_hi, pe_lo, pe_hi, w1_lo, w1_hi, w2, name):
    n, dp = a_lo.shape[0], w2.shape[1]

    def kern(alo, ahi, plo, phi, w1l, w1h, w2r, out_ref, pre_ref):
        xl = (alo[...] + plo[...]).astype(BF16)
        xh = (ahi[...] + phi[...]).astype(BF16)
        pre = _nn(xl, w1l[...]) + _nn(xh, w1h[...])
        act = pre * _sigmoid(pre)
        out_ref[...] = _nn(act.astype(BF16), w2r[...]).astype(BF16)
        pre_ref[...] = pre

    return _pcall(kern, name=name,
                  out_shape=[jax.ShapeDtypeStruct((n, dp), BF16), jax.ShapeDtypeStruct((n, dp), F32)],
                  )(a_lo, a_hi, pe_lo, pe_hi, w1_lo, w1_hi, w2)


def _compress_bwd(a_lo, a_hi, pe_lo, pe_hi, w1_lo, w1_hi, w2, pre, pre_sh, dout, dout_sh, name):
    n, ln = a_lo.shape
    dp = w2.shape[1]

    def kern(alo, ahi, plo, phi, w1l, w1h, w2r, pre_ref, presh_ref, do_ref, dosh_ref,
             da_ref, dpl_ref, dph_ref, dw1l_ref, dw1h_ref, dw2_ref):
        prev = pre_ref[...]
        act = prev * _sigmoid(prev)
        dob = do_ref[...].astype(BF16)
        w2v = w2r[...]
        dpre = (_nt(dob, w2v) * _silu_grad(prev)).astype(BF16)
        dpre_sh = (_nt(dosh_ref[...].astype(BF16), w2v) * _silu_grad(presh_ref[...])).astype(BF16)
        dw2_ref[...] = _nn(act.T.astype(BF16), dob)
        xl = alo[...] + plo[...]
        xh = ahi[...] + phi[...]
        dw1l_ref[...] = _nn(xl.T.astype(BF16), dpre)
        dw1h_ref[...] = _nn(xh.T.astype(BF16), dpre)
        dal = _nt(dpre, w1l[...])
        dah_sh = _nt(dpre_sh, w1h[...])
        da_ref[...] = dal + dah_sh
        dpl_ref[...] = jnp.sum(dal, axis=0, keepdims=True)
        dph_ref[...] = jnp.sum(dah_sh, axis=0, keepdims=True)

    return _pcall(
        kern, name=name,
        out_shape=[jax.ShapeDtypeStruct((n, ln), F32), jax.ShapeDtypeStruct((1, ln), F32),
                   jax.ShapeDtypeStruct((1, ln), F32), jax.ShapeDtypeStruct((ln, dp), F32),
                   jax.ShapeDtypeStruct((ln, dp), F32), jax.ShapeDtypeStruct((dp, dp), F32)],
    )(a_lo, a_hi, pe_lo, pe_hi, w1_lo, w1_hi, w2, pre, pre_sh, dout, dout_sh)


def _nsa_combine(o_cmp, o_slc, o_win, gl):
    s, w = o_cmp.shape
    tr = _tile(s, 512)

    def kern(a_ref, b_ref, c_ref, g_ref, o_ref):
        g = _sigmoid(g_ref[...])
        for h in range(NSA_HEADS):
            cs = slice(h * HEAD_V, (h + 1) * HEAD_V)
            o_ref[:, cs] = (g[:, 3 * h:3 * h + 1] * a_ref[:, cs] + g[:, 3 * h + 1:3 * h + 2] * b_ref[:, cs]
                            + g[:, 3 * h + 2:3 * h + 3] * c_ref[:, cs])

    row = pl.BlockSpec((tr, w), lambda i: (i, 0))
    return _pcall(kern, name="nsa_combine", grid=(s // tr,),
                  in_specs=[row, row, row, pl.BlockSpec((tr, LANE), lambda i: (i, gl.col0))], out_specs=row,
                  out_shape=jax.ShapeDtypeStruct((s, w), F32))(o_cmp, o_slc, o_win, gl.arr)


def _nsa_combine_bwd(do_cat, o_cmp, o_slc, o_win, gl):
    s, w = o_cmp.shape
    tr = _tile(s, 512)

    def kern(d_ref, a_ref, b_ref, c_ref, g_ref, da_ref, db_ref, dc_ref, dg_ref):
        g = _sigmoid(g_ref[...])
        lane = lax.broadcasted_iota(jnp.int32, (tr, LANE), 1)
        dgl = jnp.zeros((tr, LANE), F32)
        for h in range(NSA_HEADS):
            cs = slice(h * HEAD_V, (h + 1) * HEAD_V)
            dv = d_ref[:, cs]
            for b, (src, dst) in enumerate(((a_ref, da_ref), (b_ref, db_ref), (c_ref, dc_ref))):
                gate = g[:, 3 * h + b:3 * h + b + 1]
                dst[:, cs] = gate * dv
                dgate = jnp.sum(dv * src[:, cs], axis=1, keepdims=True)
                dgl = jnp.where(lane == 3 * h + b, dgate * gate * (1.0 - gate), dgl)
        dg_ref[...] = dgl

    row = pl.BlockSpec((tr, w), lambda i: (i, 0))
    tab = pl.BlockSpec((tr, LANE), lambda i: (i, 0))
    return _pcall(kern, name="nsa_combine_bwd", grid=(s // tr,),
                  in_specs=[pl.BlockSpec((tr, w), lambda i: (i, 2)), row, row, row,
                            pl.BlockSpec((tr, LANE), lambda i: (i, gl.col0))],
                  out_specs=[row, row, row, tab],
                  out_shape=[jax.ShapeDtypeStruct((s, w), F32)] * 3 + [jax.ShapeDtypeStruct((s, LANE), F32)],
                  )(do_cat, o_cmp, o_slc, o_win, gl.arr)


def _gate_fwd(o_mla, o_nsa, o_mem, hp):
    s = o_mla.shape[0]
    tr = _tile(s, 256)

    def kern(a_ref, b_ref, c_ref, z_ref, u_ref):
        z = z_ref[...]
        sz = z * _sigmoid(z)
        u_ref[:, 0:1024] = (a_ref[...] * sz[:, 0:1024]).astype(BF16)
        u_ref[:, 1024:1536] = (b_ref[...] * sz[:, 1024:1536]).astype(BF16)
        u_ref[:, 1536:2048] = (c_ref[...] * sz[:, 1536:2048]).astype(BF16)

    return _pcall(
        kern, name="gate_fwd", grid=(s // tr,),
        in_specs=[pl.BlockSpec((tr, 1024), lambda i: (i, 0)), pl.BlockSpec((tr, 512), lambda i: (i, 0)),
                  pl.BlockSpec((tr, 512), lambda i: (i, 0)), pl.BlockSpec((tr, 2048), lambda i: (i, 2))],
        out_specs=pl.BlockSpec((tr, 2048), lambda i: (i, 0)),
        out_shape=jax.ShapeDtypeStruct((s, 2048), BF16))(o_mla, o_nsa, o_mem, hp)


def _gate_bwd(du, o_mla, o_nsa, o_mem, hp):
    s = du.shape[0]
    tr = _tile(s, 256)

    def kern(d_ref, a_ref, b_ref, c_ref, z_ref, do_ref, dz_ref):
        z = z_ref[...]
        sg = _sigmoid(z)
        sz = z * sg
        dsz = sg * (1.0 + z * (1.0 - sg))
        d = d_ref[...]
        do_ref[...] = d * sz
        dz_ref[:, 0:1024] = (d[:, 0:1024] * a_ref[...] * dsz[:, 0:1024]).astype(BF16)
        dz_ref[:, 1024:1536] = (d[:, 1024:1536] * b_ref[...] * dsz[:, 1024:1536]).astype(BF16)
        dz_ref[:, 1536:2048] = (d[:, 1536:2048] * c_ref[...] * dsz[:, 1536:2048]).astype(BF16)

    wide = pl.BlockSpec((tr, 2048), lambda i: (i, 0))
    return _pcall(
        kern, name="gate_bwd", grid=(s // tr,),
        in_specs=[wide, pl.BlockSpec((tr, 1024), lambda i: (i, 0)), pl.BlockSpec((tr, 512), lambda i: (i, 0)),
                  pl.BlockSpec((tr, 512), lambda i: (i, 0)), pl.BlockSpec((tr, 2048), lambda i: (i, 2))],
        out_specs=[wide, wide],
        out_shape=[jax.ShapeDtypeStruct((s, 2048), F32), jax.ShapeDtypeStruct((s, 2048), BF16)],
    )(du, o_mla, o_nsa, o_mem, hp)


def _tile2d(rows, cols, arrays):
    if rows % 16 == 0:
        return _row_tile(rows, cols * arrays), cols
    want = max(LANE, BLOCK_BYTES // (rows * 4 * arrays) // LANE * LANE)
    tc = LANE
    for t in range(LANE, cols + 1, LANE):
        if cols % t == 0 and t <= want:
            tc = t
    return rows, tc


def _sum_slots(buf, name):
    n, rows, cols = buf.shape
    tr, tc = _tile2d(rows, cols, n)

    def kern(b_ref, o_ref):
        acc = b_ref[0].astype(F32)
        for i in range(1, n):
            acc = acc + b_ref[i].astype(F32)
        o_ref[...] = acc

    return _pcall(kern, name=name, grid=(rows // tr, cols // tc),
                  in_specs=[pl.BlockSpec((n, tr, tc), lambda i, j: (0, i, j))],
                  out_specs=pl.BlockSpec((tr, tc), lambda i, j: (i, j)),
                  out_shape=jax.ShapeDtypeStruct((rows, cols), F32))(buf)


def _chip_sum(buf, core, axis, name):
    n, rows, cols = buf.shape
    tr, tc = _tile2d(rows, cols, n)
    nbr, nbc = rows // tr, cols // tc

    def kern(c_ref, b_ref, o_ref, w_ref):
        acc = b_ref[0].astype(F32)
        for i in range(1, n):
            acc = acc + b_ref[i].astype(F32)
        o_ref[...] = acc
        w_ref[...] = acc

    place = ((lambda i, j, c: (c[0] * nbr + i, j)) if axis == 0 else (lambda i, j, c: (i, c[0] * nbc + j)))
    whole = (2 * rows, cols) if axis == 0 else (rows, 2 * cols)
    grid_spec = pltpu.PrefetchScalarGridSpec(
        num_scalar_prefetch=1, grid=(nbr, nbc),
        in_specs=[pl.BlockSpec((n, tr, tc), lambda i, j, c: (0, i, j))],
        out_specs=[pl.BlockSpec((tr, tc), lambda i, j, c: (i, j)), pl.BlockSpec((tr, tc), place)])
    return _pcall(kern, name=name, grid_spec=grid_spec,
                  out_shape=[jax.ShapeDtypeStruct((rows, cols), F32), jax.ShapeDtypeStruct(whole, F32)])(core, buf)


def _pair_sum(g4, theirs, core, axis, name):
    n, rows, cols = theirs.shape
    tr, tc = _tile2d(rows, cols, 1)
    nbr, nbc = rows // tr, cols // tc

    def kern(c_ref, a_ref, b_ref, o_ref):
        o_ref[...] = (a_ref[...] + b_ref[...]).astype(BF16)

    blk = (1, tr, tc)
    mine = ((lambda s, i, j, c: (s, c[0] * nbr + i, j)) if axis == 0
            else (lambda s, i, j, c: (s, i, c[0] * nbc + j)))
    grid_spec = pltpu.PrefetchScalarGridSpec(
        num_scalar_prefetch=1, grid=(n, nbr, nbc),
        in_specs=[pl.BlockSpec(blk, mine), pl.BlockSpec(blk, lambda s, i, j, c: (s, i, j))],
        out_specs=pl.BlockSpec(blk, lambda s, i, j, c: (s, i, j)))
    return _pcall(kern, name=name, grid_spec=grid_spec,
                  out_shape=jax.ShapeDtypeStruct((n, rows, cols), BF16))(core, g4, theirs)


def _adamw(w, g, m, v, name):
    rows, cols = w.shape
    tr, tc = _tile2d(rows, cols, 4)
    bc1 = 1.0 - ADAM_B1 ** ADAM_STEP
    bc2 = 1.0 - ADAM_B2 ** ADAM_STEP

    def kern(w_ref, g_ref, m_ref, v_ref, d_ref, mo_ref, vo_ref):
        gv = g_ref[...]
        mn = ADAM_B1 * m_ref[...] + (1.0 - ADAM_B1) * gv
        vn = ADAM_B2 * v_ref[...] + (1.0 - ADAM_B2) * (gv * gv)
        d_ref[...] = -ADAM_LR * ((mn / bc1) / (jnp.sqrt(vn / bc2) + ADAM_EPS) + ADAM_WD * w_ref[...])
        mo_ref[...] = mn
        vo_ref[...] = vn

    blk = pl.BlockSpec((tr, tc), lambda i, j: (i, j))
    return _pcall(kern, name=name, grid=(rows // tr, cols // tc), in_specs=[blk] * 4, out_specs=[blk] * 3,
                  out_shape=[jax.ShapeDtypeStruct((rows, cols), F32)] * 3)(w, g, m, v)


ANY = pl.BlockSpec(memory_space=pl.ANY)


def _place():
    x, y, c = lax.axis_index("x"), lax.axis_index("y"), lax.axis_index("c")
    chips = [(1 - x, y), (x, 1 - y), (1 - x, 1 - y)]
    return x, y, c, chips


def _remote(src, dst, send_sem, recv_sem, to):
    return pltpu.make_async_remote_copy(src_ref=src, dst_ref=dst, send_sem=send_sem, recv_sem=recv_sem,
                                        device_id=to, device_id_type=MESH)


def _half(ref, lead, core, axis):
    size = ref.shape[len(lead) + axis] // 2
    cut = pl.ds(core * size, size)
    return ref.at[tuple(lead) + ((cut, slice(None)) if axis == 0 else (slice(None), cut))]


def _gather_shards(ws, axes):
    side = _gather_side(ws, axes)

    def body(*refs):
        nw = len(ws)
        split = (refs[:nw], refs[nw:2 * nw], refs[2 * nw:])
        side.phase("start", *split)
        side.phase("finish", *split)

    return _pcall(body, name="gather_shards", in_specs=[ANY] * len(ws), out_specs=[ANY] * len(ws),
                  out_shape=side.out_shape, scratch_shapes=side.scratch)(*ws)


def _gather_side(ws, axes):
    nw = len(ws)

    def phase(which, w_refs, out_refs, sems):
        send_sems, recv_sems = sems
        x, y, c, chips = _place()
        me = 2 * x + y
        sibling = (x, y, 1 - c)

        def part(i, slot, core):
            return _half(out_refs[i], (slot,), core, axes[i])

        def copy(sem, src, dst, to):
            return _remote(src, dst, send_sems.at[sem], recv_sems.at[sem], to)

        first = [copy(j * nw + i, _half(w_refs[i], (), c, axes[i]), part(i, me, c), (*chip, c))
                 for j, chip in enumerate(chips) for i in range(nw)]
        if which == "start":
            for cp in first:
                cp.start()
            return
        passed = []
        for j, (cx, cy) in enumerate(chips):
            slot = 2 * cx + cy
            for i in range(nw):
                copy(j * nw + i, part(i, slot, c), part(i, slot, c), (x, y, c)).wait_recv()
                fwd = copy((3 + j) * nw + i, part(i, slot, c), part(i, slot, c), sibling)
                fwd.start()
                passed.append(fwd)
        for j, (cx, cy) in enumerate(chips):
            slot = 2 * cx + cy
            for i in range(nw):
                copy((3 + j) * nw + i, part(i, slot, 1 - c), part(i, slot, 1 - c), (x, y, c)).wait_recv()
        for cp in first + passed:
            cp.wait_send()

    return _Side(list(ws), [jax.ShapeDtypeStruct((4,) + w.shape, w.dtype) for w in ws],
                 [pltpu.SemaphoreType.DMA((6 * nw,)), pltpu.SemaphoreType.DMA((6 * nw,))], phase)


def _half_shape(shape, axis):
    return tuple(d // 2 if k == len(shape) - 2 + axis else d for k, d in enumerate(shape))


def _pair_exchange(gs, axes, name):
    nw = len(gs)

    def body(*refs):
        g_refs, out_refs = refs[:nw], refs[nw:2 * nw]
        send_sems, recv_sems = refs[2 * nw:]
        x, y, c, _ = _place()
        cps = []
        for i in range(nw):
            cp = _remote(_half(g_refs[i], (slice(None),), 1 - c, axes[i]), out_refs[i],
                         send_sems.at[i], recv_sems.at[i], (x, y, 1 - c))
            cp.start()
            cps.append(cp)
        for cp in cps:
            cp.wait()

    return _pcall(body, name=name, in_specs=[ANY] * nw, out_specs=[ANY] * nw,
                  out_shape=[jax.ShapeDtypeStruct(_half_shape(g.shape, a), g.dtype) for g, a in zip(gs, axes)],
                  scratch_shapes=[pltpu.SemaphoreType.DMA((nw,)), pltpu.SemaphoreType.DMA((nw,))])(*gs)


def _chip_side(ps):
    nw = len(ps)

    def phase(which, p_refs, out_refs, sems):
        send_sems, recv_sems, local_sems = sems
        x, y, c, chips = _place()
        me = 2 * x + y
        mine = [pltpu.make_async_copy(p_refs[i].at[me], out_refs[i].at[me], local_sems.at[i]) for i in range(nw)]
        sends = [_remote(p_refs[i].at[2 * cx + cy], out_refs[i].at[me], send_sems.at[j * nw + i],
                         recv_sems.at[j * nw + i], (cx, cy, c))
                 for j, (cx, cy) in enumerate(chips) for i in range(nw)]
        if which == "start":
            for cp in mine + sends:
                cp.start()
            return
        for j, (cx, cy) in enumerate(chips):
            slot = 2 * cx + cy
            for i in range(nw):
                _remote(out_refs[i].at[slot], out_refs[i].at[slot], send_sems.at[j * nw + i],
                        recv_sems.at[j * nw + i], (x, y, c)).wait_recv()
        for cp in sends:
            cp.wait_send()
        for cp in mine:
            cp.wait()

    return _Side(list(ps), [jax.ShapeDtypeStruct(p.shape, p.dtype) for p in ps],
                 [pltpu.SemaphoreType.DMA((3 * nw,)), pltpu.SemaphoreType.DMA((3 * nw,)),
                  pltpu.SemaphoreType.DMA((nw,))], phase)


def _half_exchange(ts, wholes, axes):
    nw = len(ts)

    def body(*refs):
        t_refs, out_refs = refs[:nw], refs[2 * nw:3 * nw]
        send_sems, recv_sems = refs[3 * nw:]
        x, y, c, _ = _place()
        sends = []
        for i in range(nw):
            cp = _remote(t_refs[i], _half(out_refs[i], (), c, axes[i]), send_sems.at[i], recv_sems.at[i],
                         (x, y, 1 - c))
            cp.start()
            sends.append(cp)
        for i in range(nw):
            _remote(t_refs[i], _half(out_refs[i], (), 1 - c, axes[i]), send_sems.at[i], recv_sems.at[i],
                    (x, y, c)).wait_recv()
        for cp in sends:
            cp.wait_send()

    return _pcall(body, name="half_exchange", in_specs=[ANY] * (2 * nw), out_specs=[ANY] * nw,
                  out_shape=[jax.ShapeDtypeStruct(w.shape, w.dtype) for w in wholes],
                  input_output_aliases={nw + i: i for i in range(nw)},
                  scratch_shapes=[pltpu.SemaphoreType.DMA((nw,)), pltpu.SemaphoreType.DMA((nw,))])(*ts, *wholes)


def _gather_all(v):
    rows, cols = v.shape

    def body(v_ref, out_ref, send_sems, recv_sems, local_sem):
        x, y, c, _ = _place()
        me = 4 * x + 2 * y + c
        mine = pltpu.make_async_copy(v_ref, out_ref.at[me], local_sem)
        mine.start()
        sends = []
        for d in range(1, 8):
            peer = (x ^ (d >> 2), y ^ ((d >> 1) & 1), c ^ (d & 1))
            cp = _remote(v_ref, out_ref.at[me], send_sems.at[d - 1], recv_sems.at[d - 1], peer)
            cp.start()
            sends.append(cp)
        for d in range(1, 8):
            slot = 4 * (x ^ (d >> 2)) + 2 * (y ^ ((d >> 1) & 1)) + (c ^ (d & 1))
            _remote(v_ref, out_ref.at[slot], send_sems.at[d - 1], recv_sems.at[d - 1], (x, y, c)).wait_recv()
        for cp in sends:
            cp.wait_send()
        mine.wait()

    return _pcall(body, name="gather_all", in_specs=[ANY], out_specs=ANY,
                  out_shape=jax.ShapeDtypeStruct((8, rows, cols), v.dtype),
                  scratch_shapes=[pltpu.SemaphoreType.DMA((7,)), pltpu.SemaphoreType.DMA((7,)),
                                  pltpu.SemaphoreType.DMA])(v)


def _pad_cols(a, width):
    return a if a.shape[1] == width else jnp.pad(a, ((0, 0), (0, width - a.shape[1])))


def _unpad_segments():
    z = PAD["z"]
    segs = [(PAD["c_q"], 0, 512), (PAD["c_kv"], 512, 512), (PAD["k_rope"], 1024, 64), (z, 1088, 1024)]
    segs += [(PAD["q_nsa"] + 256 * h, 2112 + NSA_DK * h, NSA_DK) for h in range(NSA_HEADS)]
    for name, rows in (("k_c", 192), ("v_c", 128), ("k_s", 192), ("v_s", 128), ("k_w", 192), ("v_w", 128),
                       ("g_nsa", 12)):
        segs.append((PAD[name], ORIG[name][0], rows))
    segs += [(z + 1024, ORIG["z_nsa"][0], 512), (PAD["q_mem"], ORIG["q_mem"][0], 512),
             (z + 1536, ORIG["z_mem"][0], 512)]
    return segs


def _w_in_grad_slots(gt):
    rows, cols = gt.shape
    shard = sum(n for _, _, n in _unpad_segments()) // 4
    tc = 256
    pieces = []
    for src, dst, n in _unpad_segments():
        while n:
            slot, off = divmod(dst, shard)
            take = min(n, shard - off)
            pieces.append((src, slot, off, take))
            src, dst, n = src + take, dst + take, n - take

    def kern(g_ref, o_ref):
        for src, slot, off, take in pieces:
            o_ref[slot, off:off + take, :] = g_ref[src:src + take, :]

    return _pcall(kern, name="w_in_grad_slots", grid=(cols // tc,),
                  in_specs=[pl.BlockSpec((rows, tc), lambda i: (0, i))],
                  out_specs=pl.BlockSpec((4, shard, tc), lambda i: (0, 0, i)),
                  out_shape=jax.ShapeDtypeStruct((4, shard, cols), F32))(gt)


def _w_in_from_slots(ws):
    nslot, shard, cols = ws.shape
    tc = 256
    pieces = []
    for dst, src, n in _unpad_segments() + [(PAD["k_rope"] + 64, ORIG["k_rope"][0], 64)]:
        while n:
            slot, off = divmod(src, shard)
            take = min(n, shard - off)
            pieces.append((dst, slot, off, take))
            src, dst, n = src + take, dst + take, n - take

    def kern(w_ref, o_ref):
        o_ref[...] = jnp.zeros_like(o_ref)
        for dst, slot, off, take in pieces:
            o_ref[dst:dst + take, :] = w_ref[slot, off:off + take, :]

    return _pcall(kern, name="w_in_from_slots", grid=(cols // tc,),
                  in_specs=[pl.BlockSpec((nslot, shard, tc), lambda i: (0, 0, i))],
                  out_specs=pl.BlockSpec((D_PAD, tc), lambda i: (0, i)),
                  out_shape=jax.ShapeDtypeStruct((D_PAD, cols), ws.dtype))(ws)


def _rope_tables(s):
    pos = jnp.arange(s, dtype=F32)
    inv_freq = ROPE_THETA ** (-jnp.arange(0, 64, 2, dtype=F32) / 64)
    ang = pos[:, None] * inv_freq[None, :]
    cos, sin = jnp.cos(ang), jnp.sin(ang)
    z = jnp.zeros((s, 64), F32)
    return jnp.concatenate([cos, cos, z], axis=1), jnp.concatenate([-sin, sin, z], axis=1)


def _overlap_table(s):
    n_c, n_s = s // CMP_STRIDE, s // SLC_LEN
    c0 = np.arange(n_c)[:, None] * CMP_STRIDE
    s0 = np.arange(LANE)[None, :] * SLC_LEN
    ov = (c0 < s0 + SLC_LEN) & (c0 + CMP_LEN > s0) & (np.arange(n_c)[:, None] < n_c - 1) & (np.arange(LANE)[None, :] < n_s)
    return jnp.asarray(ov.astype(np.float32), dtype=BF16)


def _shift_down(a):
    return jnp.concatenate([jnp.zeros((8, a.shape[1]), a.dtype), a], axis=0)[7:7 + a.shape[0]]


def _shift_up(a):
    return jnp.concatenate([a, jnp.zeros((8, a.shape[1]), a.dtype)], axis=0)[1:1 + a.shape[0]]


def _local_step(x, mem, target, w, hooks=None):
    s = x.shape[0]
    cs, sn = _rope_tables(s)
    t_ = jnp.transpose

    w_in_p = _w_in_from_slots(w["w_in_t"])
    xn, rstd_x = _rms_fwd(_Src(x, D_MODEL), w["norm_g"], "norm_x")
    if hooks is None:
        hp, hpb = _mm(xn, w_in_p, "in_proj", mode="nt", second_dtype=BF16)
    else:
        hp, hpb, *gathered = _mm(xn, w_in_p, "in_proj", mode="nt", second_dtype=BF16, side=hooks.gather_side)
        w = {**w, **hooks.weights(gathered)}

    w_uq3 = w["w_uq"].reshape(512, MLA_HEADS, 192)
    w_uq_p = jnp.concatenate([w_uq3, w_uq3[:, :, 128:]], axis=2).reshape(512, MLA_HEADS * 256)
    w_ukv_p = t_(w["w_ukv"].reshape(512, MLA_HEADS, 2, 128), (0, 2, 1, 3)).reshape(512, 2048)
    c_q, c_kv = _Src(hp, 512, 0), _Src(hp, 512, 1)
    cqn, rstd_q = _rms_fwd(c_q, w["q_norm_g"], "norm_q")
    ckvn, rstd_kv = _rms_fwd(c_kv, w["kv_norm_g"], "norm_kv")
    q_lin = _mm(cqn, w_uq_p, "mla_q_proj")
    kvb = _mm(ckvn, w_ukv_p, "mla_kv_proj", out_dtype=BF16)
    q_mla = _rope_fwd(_Src(q_lin, MLA_HEADS * 256), cs, sn, MLA_HEADS, 256, LANE, "rope_q")
    k_pe = _rope_fwd(_Src(hp, LANE, PAD["k_rope"] // LANE), cs, sn, 1, LANE, 0, "rope_k")
    mla = _Attn("mla", s, s, MLA_HEADS, 256)
    mla_q, mla_v = _Src(q_mla, 256), _Src(kvb, LANE, MLA_HEADS)
    mla_k = [_Src(kvb, LANE), _Src(k_pe, LANE, 0, False)]
    o_mla, l_mla, lr_mla = _attn_fwd(mla, mla_q, mla_k, mla_v, None, "mla_fwd")

    sk = s // CMP_STRIDE
    pe_k, pe_v = w["cmp_pe_k"], w["cmp_pe_v"]
    w1k = _pad_cols(w["cmp_w1k"], 256)
    w2k = jnp.pad(w["cmp_w2k"], ((0, 64), (0, 64))).astype(BF16)
    w1v, w2v = w["cmp_w1v"], w["cmp_w2v"].astype(BF16)
    half_k, half_v = CMP_STRIDE * NSA_DK, CMP_STRIDE * HEAD_V
    ak = hp[:, PAD["k_c"]:PAD["k_c"] + NSA_DK].reshape(sk, half_k)
    av = hp[:, PAD["v_c"]:PAD["v_c"] + HEAD_V].reshape(sk, half_v)
    ck_args = (ak, _shift_up(ak), pe_k[:CMP_STRIDE].reshape(1, half_k), pe_k[CMP_STRIDE:].reshape(1, half_k),
               w1k[:half_k], w1k[half_k:], w2k)
    cv_args = (av, _shift_up(av), pe_v[:CMP_STRIDE].reshape(1, half_v), pe_v[CMP_STRIDE:].reshape(1, half_v),
               w1v[:half_v], w1v[half_v:], w2v)
    k_cmp, pre_k = _compress_fwd(*ck_args, "compress_k")
    v_cmp, pre_v = _compress_fwd(*cv_args, "compress_v")
    cmp_ = _Attn("cmp", s, sk, NSA_HEADS, 256)
    slc = _Attn("slc", s, s, NSA_HEADS, 256)
    win = _Attn("win", s, s, NSA_HEADS, 256)
    nsa_q = _Src(hpb, 256, PAD["q_nsa"] // 256)
    cmp_k, cmp_v = [_Src(k_cmp, 256, 0, False)], _Src(v_cmp, HEAD_V, 0, False)
    slc_k, slc_v = [_Src(hpb, 256, PAD["k_s"] // 256, False)], _Src(hpb, HEAD_V, PAD["v_s"] // HEAD_V, False)
    win_k, win_v = [_Src(hpb, 256, PAD["k_w"] // 256, False)], _Src(hpb, HEAD_V, PAD["v_w"] // HEAD_V, False)
    o_cmp, l_cmp, lr_cmp, sel, selt = _attn_fwd_small(cmp_, nsa_q, cmp_k, cmp_v, "cmp_fwd", _overlap_table(s))
    o_slc, l_slc, lr_slc = _attn_fwd(slc, nsa_q, slc_k, slc_v, sel, "slc_fwd")
    o_win, l_win, lr_win = _attn_fwd_small(win, nsa_q, win_k, win_v, "win_fwd")
    gl = _Src(hp, LANE, PAD["g_nsa"] // LANE)
    o_nsa = _nsa_combine(o_cmp, o_slc, o_win, gl)

    mn, rstd_m = _rms_fwd(_Src(mem, D_MODEL), w["mem_norm_g"], "norm_mem")
    kvm = _mm(mn, w["w_mem_kv"], "mem_kv_proj", out_dtype=BF16)
    mem_ = _Attn("mem", s, mem.shape[0], MEM_HEADS, LANE)
    mem_q, mem_k, mem_v = _Src(hpb, LANE, PAD["q_mem"] // LANE), [_Src(kvm, LANE)], _Src(kvm, LANE, MEM_HEADS)
    o_mem, l_mem, lr_mem = _attn_fwd_small(mem_, mem_q, mem_k, mem_v, "mem_fwd")

    u = _gate_fwd(o_mla, o_nsa, o_mem, hp)
    proj = _mm(u, w["w_out"], "out_proj", wide=2048)
    dy, g_final, loss = _final_loss(x, proj, w["final_norm_g"].reshape(1, -1), target)

    g_w_out = _mm(u, dy, "out_proj_dw", mode="tn")
    du = _mm(dy, w["w_out"], "out_proj_dx", mode="nt", wide=2048)
    do_cat, dz = _gate_bwd(du, o_mla, o_nsa, o_mem, hp)

    dq_mem, (dk_mem,), dv_mem = _attn_bwd(mem_, mem_q, mem_k, mem_v, None, None, _Src(o_mem, HEAD_V), l_mem,
                                          lr_mem, _Src(do_cat, HEAD_V, 12), None, "mem_bwd")
    dkvm = jnp.concatenate([dk_mem, dv_mem], axis=1)
    g_w_mem_kv = _mm(mn, dkvm, "mem_kv_dw", mode="tn")
    dmn = _mm(dkvm, w["w_mem_kv"], "mem_kv_dx", mode="nt")
    _, g_mem_norm = _rms_bwd(_Src(mem, D_MODEL), w["mem_norm_g"], rstd_m, dmn, None, "norm_mem_bwd")

    do_cmp, do_slc, do_win, dgl = _nsa_combine_bwd(do_cat, o_cmp, o_slc, o_win, gl)
    dq_n, (dk_cmp,), dv_cmp = _attn_bwd(cmp_, nsa_q, cmp_k, cmp_v, None, None, _Src(o_cmp, HEAD_V), l_cmp,
                                        lr_cmp, _Src(do_cmp, HEAD_V), None, "cmp_bwd")
    dq_n, (dk_s,), dv_s = _attn_bwd(slc, nsa_q, slc_k, slc_v, sel, selt, _Src(o_slc, HEAD_V), l_slc, lr_slc,
                                    _Src(do_slc, HEAD_V), dq_n, "slc_bwd")
    dq_n, (dk_w,), dv_w = _attn_bwd(win, nsa_q, win_k, win_v, None, None, _Src(o_win, HEAD_V), l_win, lr_win,
                                    _Src(do_win, HEAD_V), dq_n, "win_bwd")
    dak, dpk_lo, dpk_hi, dw1k_lo, dw1k_hi, g_w2k = _compress_bwd(
        *ck_args, pre_k, _shift_down(pre_k), dk_cmp, _shift_down(dk_cmp), "compress_k_bwd")
    dav, dpv_lo, dpv_hi, dw1v_lo, dw1v_hi, g_w2v = _compress_bwd(
        *cv_args, pre_v, _shift_down(pre_v), dv_cmp, _shift_down(dv_cmp), "compress_v_bwd")
    g_pe_k = jnp.concatenate([dpk_lo.reshape(CMP_STRIDE, NSA_DK), dpk_hi.reshape(CMP_STRIDE, NSA_DK)], axis=0)
    g_pe_v = jnp.concatenate([dpv_lo.reshape(CMP_STRIDE, HEAD_V), dpv_hi.reshape(CMP_STRIDE, HEAD_V)], axis=0)
    g_w1k = jnp.concatenate([dw1k_lo, dw1k_hi], axis=0)[:, :NSA_DK]
    g_w1v = jnp.concatenate([dw1v_lo, dw1v_hi], axis=0)
    dk_c = _pad_cols(dak.reshape(s, NSA_DK), 256)
    dv_c = dav.reshape(s, HEAD_V)

    dq_m, (dk_nope, dk_pe), dv_m = _attn_bwd(mla, mla_q, mla_k, mla_v, None, None, _Src(o_mla, HEAD_V), l_mla,
                                             lr_mla, _Src(do_cat, HEAD_V), None, "mla_bwd")
    dq_lin = _rope_bwd_q(dq_m, cs, sn)
    dkv_lin, d_krope = _rope_bwd_k(dk_nope, dk_pe, dv_m, cs, sn)
    g_w_uq_p = _mm(cqn, dq_lin, "mla_q_dw", mode="tn")
    dcqn = _mm(dq_lin, w_uq_p, "mla_q_dx", mode="nt")
    g_w_ukv_p = _mm(ckvn, dkv_lin, "mla_kv_dw", mode="tn")
    dckvn = _mm(dkv_lin, w_ukv_p, "mla_kv_dx", mode="nt")
    dc_q, g_q_norm = _rms_bwd(c_q, w["q_norm_g"], rstd_q, dcqn, None, "norm_q_bwd", BF16)
    dc_kv, g_kv_norm = _rms_bwd(c_kv, w["kv_norm_g"], rstd_kv, dckvn, None, "norm_kv_bwd", BF16)
    g_w_uq = g_w_uq_p.reshape(512, MLA_HEADS, 256)[:, :, :192].reshape(512, MLA_HEADS * 192)
    g_w_ukv = t_(g_w_ukv_p.reshape(512, 2, MLA_HEADS, 128), (0, 2, 1, 3)).reshape(512, 2048)

    pieces = [dc_q, dc_kv, dq_n, dk_c, dk_s, dk_w, d_krope, dv_c, dv_s, dv_w, dgl,
              jnp.zeros((s, PAD["q_mem"] - (PAD["g_nsa"] + LANE)), BF16), dq_mem, dz]
    dhp = jnp.concatenate([p.astype(BF16) for p in pieces], axis=1)
    grads = dict(q_norm_g=g_q_norm, w_uq=g_w_uq, kv_norm_g=g_kv_norm,
                 w_ukv=g_w_ukv, cmp_pe_k=g_pe_k, cmp_pe_v=g_pe_v, cmp_w1k=g_w1k, cmp_w2k=g_w2k[:NSA_DK, :NSA_DK],
                 cmp_w1v=g_w1v, cmp_w2v=g_w2v, mem_norm_g=g_mem_norm, w_mem_kv=g_w_mem_kv, w_out=g_w_out,
                 final_norm_g=g_final.reshape(-1))
    if hooks is None:
        g_w_in_t = _w_in_grad_slots(_mm(dhp, xn, "in_proj_dw", mode="tn", wide=2048))
        dxn = _mm(dhp, w_in_p, "in_proj_dx", wide=2048)
    else:
        g_w_in_p, *hooks.received = _mm(dhp, xn, "in_proj_dw", mode="tn", wide=2048, side=hooks.reduce_side(grads))
        g_w_in_t = _w_in_grad_slots(g_w_in_p)
        dxn, hooks.received_w_in = _mm(dhp, w_in_p, "in_proj_dx", wide=2048, side=hooks.reduce_side_w_in(g_w_in_t))
    grad_x, g_norm = _rms_bwd(_Src(x, D_MODEL), w["norm_g"], rstd_x, dxn, dy, "norm_x_bwd")
    grads.update(norm_g=g_norm, w_in_t=g_w_in_t)
    return loss[0, 0], grad_x, grads


def kernel(x, mem, norm_g, w_in, q_norm_g, w_uq, kv_norm_g, w_ukv, cmp_pe_k, cmp_pe_v, cmp_w1k, cmp_w2k, cmp_w1v, cmp_w2v, mem_norm_g, w_mem_kv, w_out, final_norm_g, loss_target, m_norm_g, m_w_in, m_q_norm_g, m_w_uq, m_kv_norm_g, m_w_ukv, m_cmp_pe_k, m_cmp_pe_v, m_cmp_w1k, m_cmp_w2k, m_cmp_w1v, m_cmp_w2v, m_mem_norm_g, m_w_mem_kv, m_w_out, m_final_norm_g, v_norm_g, v_w_in, v_q_norm_g, v_w_uq, v_kv_norm_g, v_w_ukv, v_cmp_pe_k, v_cmp_pe_v, v_cmp_w1k, v_cmp_w2k, v_cmp_w1v, v_cmp_w2v, v_mem_norm_g, v_w_mem_kv, v_w_out, v_final_norm_g):
    args = dict(locals())
    wts = {n: args[n] for n in WEIGHTS}
    loc = {n: (a if n == "final_norm_g" else a[0]) for n, a in wts.items()}

    def to_x(n, a):
        return a.T if n == "w_in" else a

    split = [1 if n == "w_in" else 0 for n in SHARDED]
    rest = [n for n in SHARDED if n != "w_in"]
    chip = 2 * lax.axis_index("x") + lax.axis_index("y")
    core = lax.axis_index("c").astype(jnp.int32).reshape(1)
    own = {n: to_x(n, loc[n]).astype(BF16) for n in SHARDED}

    def with_own_slot(gw, a):
        return lax.dynamic_update_slice(gw, a[None], (chip, 0, 0))

    def slots(n, a):
        if n == "w_in":
            return a
        if SHARD_AXIS[n] == 0:
            return a.reshape(4, a.shape[0] // 4, a.shape[1])
        width = a.shape[1] // 4
        return jnp.stack([a[:, j * width:(j + 1) * width] for j in range(4)])

    def pair_sums(names, grads, name):
        axes = [1 if n == "w_in" else 0 for n in names]
        gs = [slots(n, a) for n, a in zip(names, grads)]
        theirs = _pair_exchange(gs, axes, name)
        return [_pair_sum(a, b, core, ax, "pair_sum_" + n) for n, a, b, ax in zip(names, gs, theirs, axes)]

    class Hooks:
        gather_side = _gather_side([own[n] for n in rest], [0] * len(rest))
        received = None

        @staticmethod
        def weights(gathered):
            out = {}
            for n, gw in zip(rest, gathered):
                gw = with_own_slot(gw, own[n])
                if SHARD_AXIS[n] == 0:
                    out[n] = gw.reshape(4 * gw.shape[1], gw.shape[2])
                else:
                    out[n] = jnp.concatenate([gw[j] for j in range(4)], axis=1)
            return out

        @staticmethod
        def reduce_side(grads):
            return _chip_side(pair_sums(rest, [grads[n] for n in rest], "pair_exchange_rest"))

        @staticmethod
        def reduce_side_w_in(g_w_in_t):
            return _chip_side(pair_sums(["w_in"], [g_w_in_t], "pair_exchange_w_in"))

    hooks = Hooks()

    start = {n: loc[n].reshape(1, -1) if loc[n].ndim == 1 else loc[n] for n in REPLICATED}
    start["w_in_t"] = with_own_slot(_gather_shards([own["w_in"]], [1])[0], own["w_in"])
    loss, grad_x, g = _local_step(x[0], mem[0], loss_target[0], start, hooks)
    loss = lax.psum(loss, ("x", "y", "c"))

    from_chips = dict(zip(rest, hooks.received), w_in=hooks.received_w_in)
    sums = [_chip_sum(from_chips[n], core, ax, "chip_sum_" + n) for n, ax in zip(SHARDED, split)]
    g_sh = _half_exchange([a for a, _ in sums], [b for _, b in sums], split)

    n_rep = sum(int(np.prod(loc[n].shape)) for n in REPLICATED)
    rows_rep = -(-n_rep // (8 * LANE)) * 8

    def rep_pack(parts):
        flat = jnp.concatenate([p.reshape(-1) for p in parts])
        return jnp.pad(flat, (0, rows_rep * LANE - n_rep)).reshape(rows_rep, LANE)

    g_rep = _sum_slots(_gather_all(rep_pack([g[n] for n in REPLICATED])), "replica_sum")
    d_rp, m_rp, v_rp = _adamw(rep_pack([wts[n] for n in REPLICATED]), g_rep,
                              rep_pack([args["m_" + n] for n in REPLICATED]),
                              rep_pack([args["v_" + n] for n in REPLICATED]), "adamw_replicated")

    def rep_unpack(buf):
        flat, out, o = buf.reshape(-1), {}, 0
        for n in REPLICATED:
            size = int(np.prod(wts[n].shape))
            out[n] = flat[o:o + size].reshape(wts[n].shape)
            o += size
        return out

    outs = {k: rep_unpack(b) for k, b in (("g", g_rep), ("d", d_rp), ("m", m_rp), ("v", v_rp))}
    for n, gn in zip(SHARDED, g_sh):
        d, mo, vo = _adamw(to_x(n, loc[n]), gn, to_x(n, args["m_" + n][0]), to_x(n, args["v_" + n][0]),
                           "adamw_" + n)
        for k, a in (("g", gn), ("d", d), ("m", mo), ("v", vo)):
            outs[k][n] = to_x(n, a).reshape(wts[n].shape)

    return (loss, grad_x[None], *[outs["g"][n] for n in WEIGHTS], *[outs["d"][n] for n in WEIGHTS],
            *[outs["m"][n] for n in WEIGHTS], *[outs["v"][n] for n in WEIGHTS])
```

```python
from typing import NamedTuple

import numpy as np
import jax
import jax.numpy as jnp
from jax import lax
from jax.experimental import pallas as pl
from jax.experimental.pallas import tpu as pltpu

F32 = jnp.float32
BF16 = jnp.bfloat16
MESH = pl.DeviceIdType.MESH

D_MODEL = 2048
EPS = 1e-6
LANE = 128
HEAD_V = 128
MLA_HEADS = 8
NSA_HEADS = 4
MEM_HEADS = 4
NSA_DK = 192
CMP_STRIDE = 16
CMP_LEN = 32
SLC_LEN = 64
SLC_TOPN = 16
WIN = 512
NEG = -1e30
LOG2E = 1.4426950408889634
ROPE_THETA = 10000.0
BLOCK_BYTES = 2 << 20

ORIG = dict(c_q=(0, 512), c_kv=(512, 512), k_rope=(1024, 64), z_mla=(1088, 1024),
            q_nsa=(2112, 768), k_c=(2880, 192), v_c=(3072, 128), k_s=(3200, 192),
            v_s=(3392, 128), k_w=(3520, 192), v_w=(3712, 128), g_nsa=(3840, 12),
            z_nsa=(3852, 512), q_mem=(4364, 512), z_mem=(4876, 512))
PAD = dict(c_q=0, c_kv=512, q_nsa=1024, k_c=2048, k_s=2304, k_w=2560, k_rope=2816, v_c=2944,
           v_s=3072, v_w=3200, g_nsa=3328, q_mem=3584, z=4096)
D_PAD = 6144

ADAM_LR, ADAM_B1, ADAM_B2, ADAM_EPS, ADAM_WD, ADAM_STEP = 0.001, 0.9, 0.999, 1e-08, 0.01, 10

SHARDED = ("w_in", "w_uq", "w_ukv", "cmp_w1k", "cmp_w1v", "w_mem_kv", "w_out")
SHARD_AXIS = dict(w_in=1, w_uq=1, w_ukv=1, cmp_w1k=0, cmp_w1v=0, w_mem_kv=0, w_out=0)
REPLICATED = ("norm_g", "q_norm_g", "kv_norm_g", "cmp_pe_k", "cmp_pe_v", "cmp_w2k", "cmp_w2v",
              "mem_norm_g", "final_norm_g")
WEIGHTS = ("norm_g", "w_in", "q_norm_g", "w_uq", "kv_norm_g", "w_ukv", "cmp_pe_k", "cmp_pe_v",
           "cmp_w1k", "cmp_w2k", "cmp_w1v", "cmp_w2v", "mem_norm_g", "w_mem_kv", "w_out",
           "final_norm_g")


def _pcall(kernel, **kw):
    return pl.pallas_call(kernel, **kw)


def _tile(n, pref):
    if n <= pref:
        return n
    for t in range(pref, LANE - 1, -LANE):
        if n % t == 0:
            return t
    raise ValueError((n, pref))


def _row_tile(rows, cols, itemsize=4):
    want = max(16, BLOCK_BYTES // (cols * itemsize))
    if rows <= want:
        return rows
    t = 16
    best = rows
    while t <= want:
        if rows % t == 0:
            best = t
        t *= 2
    return best


def _nt(a, b):
    return lax.dot_general(a, b, (((1,), (1,)), ((), ())), preferred_element_type=F32)


def _tn(a, b):
    return lax.dot_general(a, b, (((0,), (0,)), ((), ())), preferred_element_type=F32)


def _nn(a, b):
    return jnp.dot(a, b, preferred_element_type=F32)


def _sigmoid(x):
    return 1.0 / (1.0 + jnp.exp(-x))


class _Src(NamedTuple):
    arr: jax.Array
    width: int
    col0: int = 0
    per_head: bool = True

    def col(self, h):
        return self.col0 + h if self.per_head else self.col0


class _Side(NamedTuple):
    inputs: list
    out_shape: list
    scratch: list
    phase: object


def _mm(a, b, name, mode="nn", out_dtype=F32, second_dtype=None, wide=1024, side=None):
    if mode == "tn":
        k, m = a.shape
    else:
        m, k = a.shape
    if mode == "nt":
        n, k2 = b.shape
    else:
        k2, n = b.shape
    assert k == k2, (a.shape, b.shape, mode)
    tm, tn, tk = _tile(m, 1024), _tile(n, wide), _tile(k, 2048)
    grid = (m // tm, n // tn, k // tk)
    nk = grid[2]
    assert nk == 1 or (out_dtype == F32 and second_dtype is None)
    dot = {"nn": _nn, "nt": _nt, "tn": _tn}[mode]
    n_in = len(side.inputs) if side else 0
    n_out = len(side.out_shape) if side else 0
    n_res = 1 + (second_dtype is not None)

    def kern(*refs):
        a_ref, b_ref = refs[:2]
        res = refs[2 + n_in:2 + n_in + n_res]
        step = [pl.program_id(d) for d in range(3)]
        if side:
            side_refs = (refs[2:2 + n_in], refs[2 + n_in + n_res:2 + n_in + n_res + n_out],
                         refs[2 + n_in + n_res + n_out:])

            @pl.when((step[0] == 0) & (step[1] == 0) & (step[2] == 0))
            def _():
                side.phase("start", *side_refs)

        r = dot(a_ref[...].astype(BF16), b_ref[...].astype(BF16))
        if nk == 1:
            res[0][...] = r.astype(out_dtype)
            if n_res == 2:
                res[1][...] = r.astype(second_dtype)
        else:
            @pl.when(step[2] == 0)
            def _():
                res[0][...] = r

            @pl.when(step[2] > 0)
            def _():
                res[0][...] += r

        if side:
            @pl.when((step[0] == grid[0] - 1) & (step[1] == grid[1] - 1) & (step[2] == nk - 1))
            def _():
                side.phase("finish", *side_refs)

    a_spec = (pl.BlockSpec((tk, tm), lambda i, j, kk: (kk, i)) if mode == "tn"
              else pl.BlockSpec((tm, tk), lambda i, j, kk: (i, kk)))
    b_spec = (pl.BlockSpec((tn, tk), lambda i, j, kk: (j, kk)) if mode == "nt"
              else pl.BlockSpec((tk, tn), lambda i, j, kk: (kk, j)))
    o_spec = pl.BlockSpec((tm, tn), lambda i, j, kk: (i, j))
    out_specs = [o_spec] * n_res + [ANY] * n_out
    out_shape = [jax.ShapeDtypeStruct((m, n), out_dtype)]
    if second_dtype is not None:
        out_shape.append(jax.ShapeDtypeStruct((m, n), second_dtype))
    out_shape += list(side.out_shape) if side else []
    semantics = ("arbitrary",) * 3 if side else ("parallel", "parallel", "arbitrary")
    out = _pcall(
        kern, name=name, grid=grid, in_specs=[a_spec, b_spec] + [ANY] * n_in, out_specs=out_specs,
        out_shape=out_shape, scratch_shapes=list(side.scratch) if side else [],
        compiler_params=pltpu.CompilerParams(dimension_semantics=semantics),
    )(a, b, *(side.inputs if side else []))
    return out[0] if len(out) == 1 else out


def _rms_fwd(x, g, name):
    r, d = x.arr.shape[0], x.width
    tr = _tile(r, 512)

    def kern(x_ref, g_ref, y_ref, r_ref):
        xv = x_ref[...]
        rstd = lax.rsqrt(jnp.mean(xv * xv, axis=-1, keepdims=True) + EPS)
        y_ref[...] = (xv * rstd * g_ref[...]).astype(BF16)
        r_ref[...] = rstd

    return _pcall(
        kern, name=name, grid=(r // tr,),
        in_specs=[pl.BlockSpec((tr, d), lambda i: (i, x.col0)), pl.BlockSpec((1, d), lambda i: (0, 0))],
        out_specs=[pl.BlockSpec((tr, d), lambda i: (i, 0)), pl.BlockSpec((tr, 1), lambda i: (i, 0))],
        out_shape=[jax.ShapeDtypeStruct((r, d), BF16), jax.ShapeDtypeStruct((r, 1), F32)],
    )(x.arr, g)


def _rms_bwd(x, g, rstd, dy, add, name, dx_dtype=F32):
    r, d = x.arr.shape[0], x.width
    tr = _tile(r, 256)
    has_add = add is not None

    def kern(*refs):
        if has_add:
            x_ref, g_ref, r_ref, dy_ref, add_ref, dx_ref, dg_ref = refs
        else:
            x_ref, g_ref, r_ref, dy_ref, dx_ref, dg_ref = refs
        rs = r_ref[...]
        xhat = x_ref[...] * rs
        dyv = dy_ref[...]
        dyg = dyv * g_ref[...]
        c = jnp.mean(dyg * xhat, axis=-1, keepdims=True)
        dx = rs * (dyg - xhat * c)
        if has_add:
            dx = dx + add_ref[...]
        dx_ref[...] = dx.astype(dx_dtype)
        part = jnp.sum(dyv * xhat, axis=0, keepdims=True)

        @pl.when(pl.program_id(0) == 0)
        def _():
            dg_ref[...] = part

        @pl.when(pl.program_id(0) > 0)
        def _():
            dg_ref[...] += part

    row = pl.BlockSpec((tr, d), lambda i: (i, 0))
    vec = pl.BlockSpec((1, d), lambda i: (0, 0))
    ins = [pl.BlockSpec((tr, d), lambda i: (i, x.col0)), vec, pl.BlockSpec((tr, 1), lambda i: (i, 0)), row]
    ins += [row] if has_add else []
    args = (x.arr, g, rstd, dy) + ((add,) if has_add else ())
    return _pcall(
        kern, name=name, grid=(r // tr,), in_specs=ins, out_specs=[row, vec],
        out_shape=[jax.ShapeDtypeStruct((r, d), dx_dtype), jax.ShapeDtypeStruct((1, d), F32)],
        compiler_params=pltpu.CompilerParams(dimension_semantics=("arbitrary",)),
    )(*args)


def _final_loss(x, proj, g, target):
    r, d = x.shape
    tr = _tile(r, 256)

    def kern(x_ref, p_ref, g_ref, t_ref, dy_ref, dg_ref, loss_ref):
        y = x_ref[...] + p_ref[...]
        rs = lax.rsqrt(jnp.mean(y * y, axis=-1, keepdims=True) + EPS)
        yhat = y * rs
        gv = g_ref[...]
        e = yhat * gv - t_ref[...]
        lpart = 0.5 * jnp.sum(jnp.mean(e * e, axis=-1, keepdims=True), axis=0, keepdims=True)
        dout = e * (1.0 / d)
        dyg = dout * gv
        c = jnp.mean(dyg * yhat, axis=-1, keepdims=True)
        dy_ref[...] = rs * (dyg - yhat * c)
        gpart = jnp.sum(dout * yhat, axis=0, keepdims=True)
        lrow = jnp.broadcast_to(lpart, (1, LANE))

        @pl.when(pl.program_id(0) == 0)
        def _():
            dg_ref[...] = gpart
            loss_ref[...] = lrow

        @pl.when(pl.program_id(0) > 0)
        def _():
            dg_ref[...] += gpart
            loss_ref[...] += lrow

    row = pl.BlockSpec((tr, d), lambda i: (i, 0))
    vec = pl.BlockSpec((1, d), lambda i: (0, 0))
    return _pcall(
        kern, name="final_loss", grid=(r // tr,), in_specs=[row, row, vec, row],
        out_specs=[row, vec, pl.BlockSpec((1, LANE), lambda i: (0, 0))],
        out_shape=[jax.ShapeDtypeStruct((r, d), F32), jax.ShapeDtypeStruct((1, d), F32),
                   jax.ShapeDtypeStruct((1, LANE), F32)],
        compiler_params=pltpu.CompilerParams(dimension_semantics=("arbitrary",)),
    )(x, proj, g, target)


def _rope_fwd(x, cs, sn, nh, width, off, name):
    s = x.arr.shape[0]
    tr = _tile(s, 512)

    def kern(x_ref, c_ref, s_ref, o_ref):
        cv, sv = c_ref[...], s_ref[...]
        for h in range(nh):
            b = h * width
            if off:
                o_ref[:, b:b + off] = x_ref[:, b:b + off].astype(BF16)
            xr = x_ref[:, b + off:b + off + LANE]
            o_ref[:, b + off:b + off + LANE] = (xr * cv + pltpu.roll(xr, 32, 1) * sv).astype(BF16)

    tab = pl.BlockSpec((tr, LANE), lambda i: (i, 0))
    return _pcall(
        kern, name=name, grid=(s // tr,),
        in_specs=[pl.BlockSpec((tr, nh * width), lambda i: (i, x.col0)), tab, tab],
        out_specs=pl.BlockSpec((tr, nh * width), lambda i: (i, 0)),
        out_shape=jax.ShapeDtypeStruct((s, nh * width), BF16),
    )(x.arr, cs, sn)


def _rope_grad(d, cv, sv):
    g2 = d * sv
    g2 = g2 + pltpu.roll(g2, 64, 1)
    lane = lax.broadcasted_iota(jnp.int32, d.shape, 1)
    return jnp.where(lane < 64, d * cv + pltpu.roll(g2, 32, 1), 0.0)


def _rope_bwd_q(dq, cs, sn):
    s, w = dq.shape
    tr = _tile(s, 512)
    nh = w // 256

    def kern(d_ref, c_ref, s_ref, o_ref):
        cv, sv = c_ref[...], s_ref[...]
        for h in range(nh):
            b = h * 256
            o_ref[:, b:b + LANE] = d_ref[:, b:b + LANE].astype(BF16)
            o_ref[:, b + LANE:b + 256] = _rope_grad(d_ref[:, b + LANE:b + 256], cv, sv).astype(BF16)

    row = pl.BlockSpec((tr, w), lambda i: (i, 0))
    tab = pl.BlockSpec((tr, LANE), lambda i: (i, 0))
    return _pcall(kern, name="rope_bwd_q", grid=(s // tr,), in_specs=[row, tab, tab], out_specs=row,
                  out_shape=jax.ShapeDtypeStruct((s, w), BF16))(dq, cs, sn)


def _rope_bwd_k(dk_nope, dk_pe, dv, cs, sn):
    s, w = dk_nope.shape
    tr = _tile(s, 512)

    def kern(dk_ref, dp_ref, dv_ref, c_ref, s_ref, okv_ref, okr_ref):
        okv_ref[:, :w] = dk_ref[...].astype(BF16)
        okv_ref[:, w:] = dv_ref[...].astype(BF16)
        okr_ref[...] = _rope_grad(dp_ref[...], c_ref[...], s_ref[...])

    tab = pl.BlockSpec((tr, LANE), lambda i: (i, 0))
    wide = pl.BlockSpec((tr, w), lambda i: (i, 0))
    return _pcall(
        kern, name="rope_bwd_k", grid=(s // tr,), in_specs=[wide, tab, wide, tab, tab],
        out_specs=[pl.BlockSpec((tr, 2 * w), lambda i: (i, 0)), tab],
        out_shape=[jax.ShapeDtypeStruct((s, 2 * w), BF16), jax.ShapeDtypeStruct((s, LANE), F32)],
    )(dk_nope, dk_pe, dv, cs, sn)


class _Attn:
    def __init__(self, mode, s, sk, heads, dk):
        self.mode, self.s, self.sk, self.h, self.dk = mode, s, sk, heads, dk
        self.scale = {"mla": 192 ** -0.5, "mem": 128 ** -0.5}.get(mode, NSA_DK ** -0.5)
        self.tb = min(256, s)
        self.nb = s // self.tb
        self.nsub = 2 if self.nb % 2 == 0 else 1
        self.tq = self.tb * self.nsub
        self.fchains = 1
        self.nq = s // self.tq
        self.qpb = 4 if self.nq % 4 == 0 and mode == "mla" else 2
        self.causal = mode in ("mla", "slc")
        if self.causal:
            self.tk = self.tq
        elif mode == "win":
            self.tk = WIN + self.tb
        else:
            self.tk = sk
        self.tkb = min(512, sk)
        self.ksub = 1
        self.kb = self.tkb // self.ksub
        self.ncmp = s // CMP_STRIDE - 1

    def mask_bias(self, t, n, h, selx, diag):
        m = self.mode
        if m == "mla":
            return (n <= t) if diag else None, None
        if m == "mem":
            return None, None
        slope = jnp.where(h == 0, 0.25, jnp.where(h == 1, 0.0625, jnp.where(h == 2, 0.015625, 0.00390625)))
        slope = slope.astype(F32) * LOG2E
        if m == "cmp":
            mask = (n * CMP_STRIDE + (CMP_LEN - 1) <= t) & (n < self.ncmp)
            pos = n.astype(F32) * float(CMP_STRIDE) + (CMP_LEN - 1) / 2.0
            return mask, slope * pos
        rel = t - n
        if m == "slc":
            return (rel >= 0) if diag else None, slope * n.astype(F32)
        return (rel >= 0) & (rel < WIN), slope * n.astype(F32)


def _scores(cfg, s_raw, t, n, h, selx, diag, lse=None):
    s = s_raw * (cfg.scale * LOG2E)
    mask, key_term = cfg.mask_bias(t, n, h, selx, diag)
    if key_term is not None:
        s = s + key_term
    if selx is not None:
        s = s + selx
    if lse is None:
        if mask is not None:
            s = jnp.where(mask, s, NEG)
        return s, mask
    p = jnp.exp2(jnp.minimum(s - lse, 0.0))
    if mask is not None:
        p = jnp.where(mask, p, 0.0)
    return p, mask


def _block_of_key(k0, tk, keys_on_rows, value=NEG):
    shape = (tk, LANE) if keys_on_rows else (LANE, tk)
    n = lax.broadcasted_iota(jnp.int32, shape, 0 if keys_on_rows else 1) + k0
    j = lax.broadcasted_iota(jnp.int32, shape, 1 if keys_on_rows else 0)
    return jnp.where((n >> 6) == j, value, 0.0).astype(BF16)


def _to_row(col):
    t = col.shape[0]
    return jnp.transpose(jnp.broadcast_to(col, (t, LANE)))[0:1, :]


def _load_keys(k_refs, rows):
    parts = [r[rows, :].astype(BF16) for r in k_refs]
    return parts[0] if len(parts) == 1 else jnp.concatenate(parts, axis=1)


def _attn_fwd(cfg, q, ks, v, sel, name):
    s, tq, tk, nsub = cfg.s, cfg.tq, cfg.tk, cfg.fchains
    tb = tq // nsub
    per = tb // cfg.tb
    has_sel = sel is not None
    nkp = len(ks)
    qpb = cfg.qpb
    assert cfg.causal and tq == tk and qpb % 2 == 0 and cfg.nq % qpb == 0

    def kern(*refs):
        q_ref, k_refs, v_ref = refs[0], refs[1:1 + nkp], refs[1 + nkp]
        sel_ref = refs[2 + nkp] if has_sel else None
        o_ref, lc_ref, lr_ref = refs[2 + nkp + has_sel:5 + nkp + has_sel]
        buf_a, buf_b = refs[-2:]
        h, g = pl.program_id(0), pl.program_id(1)

        def block(b):
            rows = [slice(b * tq + r * tb, b * tq + (r + 1) * tb) for r in range(nsub)]
            qs = [q_ref[p, :].astype(BF16) for p in rows]
            ts = [(qpb * g + b) * tq + r * tb + lax.broadcasted_iota(jnp.int32, (tb, 1), 0) for r in range(nsub)]
            sels = [sel_ref[p, :].astype(BF16) for p in rows] if has_sel else None
            return rows, qs, ts, sels

        def scores_into(buf, blk, c):
            kk = _load_keys(k_refs, pl.ds(pl.multiple_of(c * tk, tk), tk))
            for r in range(nsub):
                buf[r] = _nt(blk[1][r], kk)

        def consume(buf, blk, c, carry, diag):
            _, _, ts, sels = blk
            k0 = pl.multiple_of(c * tk, tk)
            vv = v_ref[pl.ds(k0, tk), :].astype(BF16)
            emat = _block_of_key(k0, tk, False) if has_sel else None
            n = k0 + lax.broadcasted_iota(jnp.int32, (1, tk), 1)
            new = []
            for r in range(nsub):
                m, l, acc = carry[r]
                selx = _nn(sels[r], emat) if has_sel else None
                sc, mask = _scores(cfg, buf[r], ts[r], n, h, selx, diag)
                m_new = jnp.maximum(m, jnp.max(sc, axis=1, keepdims=True))
                alpha = jnp.exp2(m - m_new)
                p = jnp.exp2(sc - m_new)
                if mask is not None:
                    p = jnp.where(mask, p, 0.0)
                l = alpha * l + jnp.sum(p, axis=1, keepdims=True)
                new.append((m_new, l, alpha * acc + _nn(p.astype(BF16), vv)))
            return tuple(new)

        def finish(blk, b, carry):
            for r, (m, l, acc) in enumerate(carry):
                o_ref[blk[0][r], :] = acc / (l + 1e-20)
                lse = m + jnp.log(l + 1e-20) * LOG2E
                lc_ref[0, blk[0][r], :] = lse
                for u in range(per):
                    lr_ref[0, (b * nsub + r) * per + u] = _to_row(lse[u * cfg.tb:(u + 1) * cfg.tb])

        def pairs(blk, first, other):
            def pair(p, cr):
                scores_into(other, blk, 2 * p + 1)
                cr = consume(first, blk, 2 * p, cr, False)
                scores_into(first, blk, 2 * p + 2)
                return consume(other, blk, 2 * p + 1, cr, False)
            return pair

        init = ((jnp.full((tb, 1), NEG, F32), jnp.zeros((tb, 1), F32), jnp.zeros((tb, HEAD_V), F32)),) * nsub
        cur, oth = buf_a, buf_b
        blk = block(0)
        scores_into(cur, blk, 0)
        for b in range(qpb):
            full = qpb * g + b
            carry = lax.fori_loop(0, (qpb // 2) * g + b // 2, pairs(blk, cur, oth), init)
            nxt = block(b + 1) if b + 1 < qpb else None
            if b % 2 == 0:
                if nxt:
                    scores_into(oth, nxt, 0)
                finish(blk, b, consume(cur, blk, full, carry, True))
                cur, oth = oth, cur
            else:
                scores_into(oth, blk, full)
                carry = consume(cur, blk, full - 1, carry, False)
                if nxt:
                    scores_into(cur, nxt, 0)
                finish(blk, b, consume(oth, blk, full, carry, True))
            blk = nxt

    rows_step = qpb * tq
    ins = [pl.BlockSpec((rows_step, q.width), lambda h, g: (g, q.col(h)))]
    ins += [pl.BlockSpec((cfg.sk, p.width), lambda h, g, p=p: (0, p.col(h))) for p in ks]
    ins += [pl.BlockSpec((cfg.sk, HEAD_V), lambda h, g: (0, v.col(h)))]
    args = [q.arr] + [p.arr for p in ks] + [v.arr]
    if has_sel:
        ins.append(pl.BlockSpec((rows_step, LANE), lambda h, g: (g, 0)))
        args.append(sel)
    return _pcall(
        kern, name=name, grid=(cfg.h, cfg.nq // qpb), in_specs=ins,
        out_specs=[pl.BlockSpec((rows_step, HEAD_V), lambda h, g: (g, h)),
                   pl.BlockSpec((1, rows_step, 1), lambda h, g: (h, g, 0)),
                   pl.BlockSpec((1, rows_step // cfg.tb, 1, cfg.tb), lambda h, g: (h, g, 0, 0))],
        out_shape=[jax.ShapeDtypeStruct((s, cfg.h * HEAD_V), F32),
                   jax.ShapeDtypeStruct((cfg.h, s, 1), F32),
                   jax.ShapeDtypeStruct((cfg.h, cfg.nb, 1, cfg.tb), F32)],
        scratch_shapes=[pltpu.VMEM((nsub, tb, tk), F32)] * 2,
        compiler_params=pltpu.CompilerParams(dimension_semantics=("parallel", "parallel")),
    )(*args)


def _attn_dq(cfg, q, ks, v, sel, o, lse, do, dq_in, name):
    s, tq, tk, dk, nsub = cfg.s, cfg.tq, cfg.tk, cfg.dk, cfg.fchains
    tb = tq // nsub
    per = tb // cfg.tb
    has_sel = sel is not None
    has_in = dq_in is not None
    nkp = len(ks)

    def kern(*refs):
        refs = list(refs)
        q_ref, k_refs, v_ref = refs[0], refs[1:1 + nkp], refs[1 + nkp]
        p0 = 2 + nkp
        sel_ref = refs[p0] if has_sel else None
        p0 += has_sel
        o_ref, l_ref, do_ref = refs[p0:p0 + 3]
        p0 += 3
        in_ref = refs[p0] if has_in else None
        p0 += has_in
        dq_ref, dr_ref = refs[p0:p0 + 2]
        sa, pa, sb, pb = refs[-4:]
        h, g = pl.program_id(0), pl.program_id(1)

        def block(b):
            rows = [slice(b * tq + r * tb, b * tq + (r + 1) * tb) for r in range(nsub)]
            qs = [q_ref[p, :].astype(BF16) for p in rows]
            ts = [(qpb * g + b) * tq + r * tb + lax.broadcasted_iota(jnp.int32, (tb, 1), 0) for r in range(nsub)]
            sels = [sel_ref[p, :].astype(BF16) for p in rows] if has_sel else None
            dvecs, dobs, lses = [], [], []
            for r, p in enumerate(rows):
                dov = do_ref[p, :]
                dvec = jnp.sum(dov * o_ref[p, :], axis=1, keepdims=True)
                for u in range(per):
                    dr_ref[0, (b * nsub + r) * per + u] = _to_row(dvec[u * cfg.tb:(u + 1) * cfg.tb])
                dvecs.append(dvec)
                dobs.append(dov.astype(BF16))
                lses.append(l_ref[0, p, :])
            return rows, qs, ts, sels, dvecs, dobs, lses

        def products_into(sbuf, pbuf, blk, c):
            rows = pl.ds(pl.multiple_of(c * tk, tk), tk)
            kk, vv = _load_keys(k_refs, rows), v_ref[rows, :].astype(BF16)
            for r in range(nsub):
                sbuf[r] = _nt(blk[1][r], kk)
                pbuf[r] = _nt(blk[5][r], vv)

        def consume(sbuf, pbuf, blk, c, accs, diag):
            _, _, ts, sels, dvecs, _, lses = blk
            k0 = pl.multiple_of(c * tk, tk)
            kk = _load_keys(k_refs, pl.ds(k0, tk))
            emat = _block_of_key(k0, tk, False) if has_sel else None
            n = k0 + lax.broadcasted_iota(jnp.int32, (1, tk), 1)
            new = []
            for r in range(nsub):
                selx = _nn(sels[r], emat) if has_sel else None
                p, _ = _scores(cfg, sbuf[r], ts[r], n, h, selx, diag, lses[r])
                ds = p * (pbuf[r] - dvecs[r])
                new.append(accs[r] + _nn(ds.astype(BF16), kk))
            return tuple(new)

        def finish(blk, accs):
            for r, p in enumerate(blk[0]):
                dq_ref[p, :] = accs[r] * cfg.scale + in_ref[p, :] if has_in else accs[r] * cfg.scale

        def pairs(blk, first, other):
            def pair(p, ac):
                products_into(*other, blk, 2 * p + 1)
                ac = consume(*first, blk, 2 * p, ac, False)
                products_into(*first, blk, 2 * p + 2)
                return consume(*other, blk, 2 * p + 1, ac, False)
            return pair

        zero = (jnp.zeros((tb, dk), F32),) * nsub
        cur, oth = (sa, pa), (sb, pb)
        blk = block(0)
        products_into(*cur, blk, 0)
        for b in range(qpb):
            full = qpb * g + b
            accs = lax.fori_loop(0, (qpb // 2) * g + b // 2, pairs(blk, cur, oth), zero)
            nxt = block(b + 1) if b + 1 < qpb else None
            if b % 2 == 0:
                if nxt:
                    products_into(*oth, nxt, 0)
                finish(blk, consume(*cur, blk, full, accs, True))
                cur, oth = oth, cur
            else:
                products_into(*oth, blk, full)
                accs = consume(*cur, blk, full - 1, accs, False)
                if nxt:
                    products_into(*cur, nxt, 0)
                finish(blk, consume(*oth, blk, full, accs, True))
            blk = nxt

    qpb = cfg.qpb
    assert cfg.causal and tq == tk and qpb % 2 == 0 and cfg.nq % qpb == 0
    rows_step = qpb * tq
    qs = pl.BlockSpec((rows_step, dk), lambda h, g: (g, h))
    ins = [pl.BlockSpec((rows_step, q.width), lambda h, g: (g, q.col(h)))]
    ins += [pl.BlockSpec((cfg.sk, p.width), lambda h, g, p=p: (0, p.col(h))) for p in ks]
    ins += [pl.BlockSpec((cfg.sk, HEAD_V), lambda h, g: (0, v.col(h)))]
    args = [q.arr] + [p.arr for p in ks] + [v.arr]
    if has_sel:
        ins.append(pl.BlockSpec((rows_step, LANE), lambda h, g: (g, 0)))
        args.append(sel)
    ins += [pl.BlockSpec((rows_step, HEAD_V), lambda h, g: (g, o.col(h))),
            pl.BlockSpec((1, rows_step, 1), lambda h, g: (h, g, 0)),
            pl.BlockSpec((rows_step, HEAD_V), lambda h, g: (g, do.col(h)))]
    args += [o.arr, lse, do.arr]
    if has_in:
        ins.append(qs)
        args.append(dq_in)
    return _pcall(
        kern, name=name, grid=(cfg.h, cfg.nq // qpb), in_specs=ins,
        out_specs=[qs, pl.BlockSpec((1, rows_step // cfg.tb, 1, cfg.tb), lambda h, g: (h, g, 0, 0))],
        out_shape=[jax.ShapeDtypeStruct((s, cfg.h * dk), F32),
                   jax.ShapeDtypeStruct((cfg.h, cfg.nb, 1, cfg.tb), F32)],
        scratch_shapes=[pltpu.VMEM((nsub, tb, tk), F32)] * 4,
        compiler_params=pltpu.CompilerParams(dimension_semantics=("parallel", "parallel")),
    )(*args)


def _attn_dkv(cfg, q, ks, v, selt, lse_r, d_r, do, name):
    s, tq, tkb, dk, kb, ksub = cfg.s, cfg.tb, cfg.tkb, cfg.dk, cfg.kb, cfg.ksub
    nq = cfg.nb
    has_sel = selt is not None
    nkp = len(ks)
    outs = list(ks) + [v]

    def kern(*refs):
        k_refs, v_ref = refs[:nkp], refs[nkp]
        q_ref, do_ref, lr_ref, dr_ref = refs[nkp + 1:nkp + 5]
        st_ref = refs[nkp + 5] if has_sel else None
        out_refs = refs[nkp + 5 + has_sel:2 * nkp + 6 + has_sel]
        sa, pa, sb, pb = refs[-4:]
        j, h = pl.program_id(0), pl.program_id(1)
        k0 = j * tkb
        part = [slice(u * kb, (u + 1) * kb) for u in range(ksub)]
        kks = [_load_keys(k_refs, p) for p in part]
        vvs = [v_ref[p, :].astype(BF16) for p in part]
        ns = [k0 + u * kb + lax.broadcasted_iota(jnp.int32, (kb, 1), 0) for u in range(ksub)]
        emats = [_block_of_key(k0 + u * kb, kb, True) for u in range(ksub)] if has_sel else None

        def load_q(i):
            rows = pl.ds(pl.multiple_of(i * tq, tq), tq)
            return q_ref[rows, :].astype(BF16), do_ref[rows, :].astype(BF16)

        def products_into(sbuf, pbuf, i):
            qi, doi = load_q(i)
            for u in range(ksub):
                sbuf[u] = _nt(kks[u], qi)
                pbuf[u] = _nt(vvs[u], doi)

        def consume(sbuf, pbuf, i, carry):
            qi, doi = load_q(i)
            t = i * tq + lax.broadcasted_iota(jnp.int32, (1, tq), 1)
            selt_i = st_ref[i].astype(BF16) if has_sel else None
            new = []
            for u in range(ksub):
                dk_acc, dv_acc = carry[u]
                selx = _nn(emats[u], selt_i) if has_sel else None
                pt, _ = _scores(cfg, sbuf[u], t, ns[u], h, selx, True, lr_ref[0, i])
                dv_acc = dv_acc + _nn(pt.astype(BF16), doi)
                dst = pt * (pbuf[u] - dr_ref[0, i])
                new.append((dk_acc + _nn(dst.astype(BF16), qi), dv_acc))
            return tuple(new)

        if cfg.causal:
            first, count = k0 // tq, nq - k0 // tq
        elif cfg.mode == "win":
            first = k0 // tq
            count = jnp.minimum((k0 + tkb + WIN - 2) // tq + 1, nq) - first
        else:
            first, count = 0, nq

        def pair(p, cr):
            i0 = first + 2 * p
            products_into(sb, pb, i0 + 1)
            cr = consume(sa, pa, i0, cr)
            products_into(sa, pa, i0 + 2)
            return consume(sb, pb, i0 + 1, cr)

        carry = ((jnp.zeros((kb, dk), F32), jnp.zeros((kb, HEAD_V), F32)),) * ksub
        products_into(sa, pa, first)
        carry = lax.fori_loop(0, count // 2 - 1, pair, carry)
        last = first + count - 2
        products_into(sb, pb, last + 1)
        carry = consume(sa, pa, last, carry)
        carry = consume(sb, pb, last + 1, carry)
        for u, (dk_acc, dv_acc) in enumerate(carry):
            vals, off = [], 0
            for p in ks:
                vals.append(dk_acc[:, off:off + p.width] * cfg.scale)
                off += p.width
            vals.append(dv_acc)
            for src, ref, val in zip(outs, out_refs, vals):
                if src.per_head:
                    ref[part[u], :] = val
                else:
                    @pl.when(h == 0)
                    def _(ref=ref, val=val, u=u):
                        ref[part[u], :] = val

                    @pl.when(h > 0)
                    def _(ref=ref, val=val, u=u):
                        ref[part[u], :] += val

    rowv = pl.BlockSpec((1, nq, 1, tq), lambda j, h: (h, 0, 0, 0))
    ins = [pl.BlockSpec((tkb, p.width), lambda j, h, p=p: (j, p.col(h))) for p in ks]
    ins += [pl.BlockSpec((tkb, HEAD_V), lambda j, h: (j, v.col(h))),
            pl.BlockSpec((s, q.width), lambda j, h: (0, q.col(h))),
            pl.BlockSpec((s, HEAD_V), lambda j, h: (0, do.col(h))), rowv, rowv]
    args = [p.arr for p in ks] + [v.arr, q.arr, do.arr, lse_r, d_r]
    if has_sel:
        ins.append(pl.BlockSpec((nq, LANE, tq), lambda j, h: (0, 0, 0)))
        args.append(selt)
    out_specs = [pl.BlockSpec((tkb, p.width), lambda j, h, p=p: (j, h if p.per_head else 0)) for p in outs]
    out_shape = [jax.ShapeDtypeStruct((cfg.sk, (cfg.h if p.per_head else 1) * p.width), F32) for p in outs]
    assert nq % 2 == 0 and (cfg.mode in ("cmp", "mem") or tkb % (2 * tq) == 0), (nq, tkb, tq)
    return _pcall(
        kern, name=name, grid=(cfg.sk // tkb, cfg.h), in_specs=ins, out_specs=out_specs, out_shape=out_shape,
        scratch_shapes=[pltpu.VMEM((ksub, kb, tq), F32)] * 4,
        compiler_params=pltpu.CompilerParams(dimension_semantics=("parallel", "arbitrary")),
    )(*args)


def _attn_dkv_flat(cfg, q, ks, v, selt, lse_r, d_r, do, name):
    s, tq, tkb, dk, kb, ksub = cfg.s, cfg.tb, cfg.tkb, cfg.dk, cfg.kb, cfg.ksub
    nq = cfg.nb
    has_sel = selt is not None
    nkp = len(ks)
    outs = list(ks) + [v]
    assert nq % 2 == 0 and tkb % (2 * tq) == 0, (nq, tkb, tq)
    steps = []
    for j in range(cfg.sk // tkb):
        first = j * tkb // tq
        stop = nq if cfg.causal else min((j * tkb + tkb + WIN - 2) // tq + 1, nq)
        steps += [(j, i0) for i0 in range(first, stop, 2)]
    n_pairs = len(steps)
    steps.append(steps[-1])
    tab_j = jnp.asarray(np.array([p[0] for p in steps], np.int32))
    tab_i = jnp.asarray(np.array([p[1] for p in steps], np.int32))

    def kern(tj_ref, ti_ref, *refs):
        k_refs, v_ref = refs[:nkp], refs[nkp]
        q_ref, do_ref, lr_ref, dr_ref = refs[nkp + 1:nkp + 5]
        st_ref = refs[nkp + 5] if has_sel else None
        out_refs = refs[nkp + 5 + has_sel:2 * nkp + 6 + has_sel]
        sa, pa, sb, pb = refs[-4:]
        h = pl.program_id(0)
        for src, ref in zip(outs, out_refs):
            if src.per_head:
                ref[...] = jnp.zeros_like(ref)
            else:
                @pl.when(h == 0)
                def _(ref=ref):
                    ref[...] = jnp.zeros_like(ref)

        def key_rows(j, u):
            return pl.ds(pl.multiple_of(j * tkb + u * kb, kb), kb)

        def load_q(i):
            rows = pl.ds(pl.multiple_of(i * tq, tq), tq)
            return q_ref[rows, :].astype(BF16), do_ref[rows, :].astype(BF16)

        def products_into(sbuf, pbuf, j, i):
            qi, doi = load_q(i)
            for u in range(ksub):
                rows = key_rows(j, u)
                sbuf[u] = _nt(_load_keys(k_refs, rows), qi)
                pbuf[u] = _nt(v_ref[rows, :].astype(BF16), doi)

        def consume(sbuf, pbuf, j, i):
            qi, doi = load_q(i)
            t = i * tq + lax.broadcasted_iota(jnp.int32, (1, tq), 1)
            selt_i = st_ref[i].astype(BF16) if has_sel else None
            res = []
            for u in range(ksub):
                k0 = j * tkb + u * kb
                n = k0 + lax.broadcasted_iota(jnp.int32, (kb, 1), 0)
                selx = _nn(_block_of_key(k0, kb, True), selt_i) if has_sel else None
                pt, _ = _scores(cfg, sbuf[u], t, n, h, selx, True, lr_ref[0, i])
                dst = pt * (pbuf[u] - dr_ref[0, i])
                res.append((_nn(dst.astype(BF16), qi), _nn(pt.astype(BF16), doi)))
            return res

        def pair(p, carry):
            j, i0 = tj_ref[p], ti_ref[p]
            products_into(sb, pb, j, i0 + 1)
            ca = consume(sa, pa, j, i0)
            products_into(sa, pa, tj_ref[p + 1], ti_ref[p + 1])
            cb = consume(sb, pb, j, i0 + 1)
            for u in range(ksub):
                rows = key_rows(j, u)
                dk_c = (ca[u][0] + cb[u][0]) * cfg.scale
                off = 0
                for src, ref in zip(ks, out_refs):
                    ref[rows, :] += dk_c[:, off:off + src.width]
                    off += src.width
                out_refs[nkp][rows, :] += ca[u][1] + cb[u][1]
            return carry

        products_into(sa, pa, tj_ref[0], ti_ref[0])
        lax.fori_loop(0, n_pairs, pair, 0)

    rowv = pl.BlockSpec((1, nq, 1, tq), lambda h, tj, ti: (h, 0, 0, 0))
    ins = [pl.BlockSpec((cfg.sk, p.width), lambda h, tj, ti, p=p: (0, p.col(h))) for p in ks]
    ins += [pl.BlockSpec((cfg.sk, HEAD_V), lambda h, tj, ti: (0, v.col(h))),
            pl.BlockSpec((s, q.width), lambda h, tj, ti: (0, q.col(h))),
            pl.BlockSpec((s, HEAD_V), lambda h, tj, ti: (0, do.col(h))), rowv, rowv]
    args = [p.arr for p in ks] + [v.arr, q.arr, do.arr, lse_r, d_r]
    if has_sel:
        ins.append(pl.BlockSpec((nq, LANE, tq), lambda h, tj, ti: (0, 0, 0)))
        args.append(selt)
    out_specs = [pl.BlockSpec((cfg.sk, p.width), lambda h, tj, ti, p=p: (0, h if p.per_head else 0))
                 for p in outs]
    out_shape = [jax.ShapeDtypeStruct((cfg.sk, (cfg.h if p.per_head else 1) * p.width), F32) for p in outs]
    grid_spec = pltpu.PrefetchScalarGridSpec(
        num_scalar_prefetch=2, grid=(cfg.h,), in_specs=ins, out_specs=out_specs,
        scratch_shapes=[pltpu.VMEM((ksub, kb, tq), F32)] * 4)
    return _pcall(kern, name=name, grid_spec=grid_spec, out_shape=out_shape,
                  compiler_params=pltpu.CompilerParams(dimension_semantics=("arbitrary",)))(tab_j, tab_i, *args)


def _all_heads(cfg, src, rows, key=False):
    if src.per_head:
        assert src.col0 % cfg.h == 0
        width, col = cfg.h * src.width, src.col0 // cfg.h
    else:
        width, col = src.width, src.col0
    return pl.BlockSpec((rows, width), (lambda i: (0, col)) if key else (lambda i: (i, col)))


def _head_cols(src, hh):
    return slice(hh * src.width, (hh + 1) * src.width) if src.per_head else slice(None)


def _key_window(cfg, i, r):
    if cfg.mode == "win":
        return pl.ds(pl.multiple_of(jnp.maximum(i * cfg.tq + r * cfg.tb - WIN, 0), cfg.tb), cfg.tk)
    return pl.ds(0, cfg.tk)


def _attn_fwd_small(cfg, q, ks, v, name, overlap=None):
    s, tq, tk, tb, nsub, nh = cfg.s, cfg.tq, cfg.tk, cfg.tb, cfg.nsub, cfg.h
    nkp = len(ks)
    select = overlap is not None
    n_s = s // SLC_LEN
    top_n = min(SLC_TOPN, n_s)

    def kern(*refs):
        q_ref, k_refs, v_ref = refs[0], refs[1:1 + nkp], refs[1 + nkp]
        ov_ref = refs[2 + nkp] if select else None
        o_ref, lc_ref, lr_ref = refs[2 + nkp + select:5 + nkp + select]
        i = pl.program_id(0)
        imps = [jnp.zeros((tb, LANE), F32)] * nsub
        for r in range(nsub):
            rows = slice(r * tb, (r + 1) * tb)
            t = i * tq + r * tb + lax.broadcasted_iota(jnp.int32, (tb, 1), 0)
            win = _key_window(cfg, i, r)
            n = win.start + lax.broadcasted_iota(jnp.int32, (1, tk), 1)
            for hh in range(nh):
                qv = q_ref[rows, hh * cfg.dk:(hh + 1) * cfg.dk].astype(BF16)
                kk = _load_keys([kr.at[:, _head_cols(p, hh)] for kr, p in zip(k_refs, ks)], win)
                vv = v_ref[win, _head_cols(v, hh)].astype(BF16)
                sc, mask = _scores(cfg, _nt(qv, kk), t, n, hh, None, True)
                m = jnp.max(sc, axis=1, keepdims=True)
                e = jnp.exp2(sc - m)
                if mask is not None:
                    e = jnp.where(mask, e, 0.0)
                l = jnp.sum(e, axis=1, keepdims=True)
                o_ref[rows, hh * HEAD_V:(hh + 1) * HEAD_V] = _nn(e.astype(BF16), vv) / (l + 1e-20)
                lse = m + jnp.log(l + 1e-20) * LOG2E
                lc_ref[hh, rows, :] = lse
                lr_ref[hh, r] = _to_row(lse)
                if select:
                    imps[r] = imps[r] + _nn((e / (l + 1e-20)).astype(BF16), ov_ref[...])
        if select:
            sel_ref, selt_ref, imp_t = refs[5 + nkp + select:8 + nkp + select]
            for r in range(nsub):
                t = i * tq + r * tb + lax.broadcasted_iota(jnp.int32, (tb, 1), 0)
                j = lax.broadcasted_iota(jnp.int32, (tb, LANE), 1)
                cur = t >> 6
                imp = jnp.where((j == 0) | (j == cur) | (j == cur - 1), 1e9, imps[r])
                imp = jnp.where(j > cur, -1e9, imp)
                imp_t[r] = jnp.transpose(imp)
                mine = imp_t[r, 0:n_s, :]
                jrow = lax.broadcasted_iota(jnp.int32, (n_s, tb), 0)

                def count(k, rank):
                    other = imp_t[r, pl.ds(k, 1), :]
                    ahead = (other > mine) | ((other == mine) & (k < jrow))
                    return rank + jnp.where(ahead, 1.0, 0.0)

                rank = lax.fori_loop(0, n_s, count, jnp.zeros((n_s, tb), F32))
                cur_t = (i * tq + r * tb + lax.broadcasted_iota(jnp.int32, (1, tb), 1)) >> 6
                rejected = jnp.where((rank < top_n) & (jrow <= cur_t), 0.0, 1.0)
                if n_s < LANE:
                    rejected = jnp.concatenate([rejected, jnp.ones((LANE - n_s, tb), F32)], axis=0)
                selt_ref[r] = rejected
                sel_ref[r * tb:(r + 1) * tb, :] = jnp.transpose(rejected)

    ins = [_all_heads(cfg, q, tq)] + [_all_heads(cfg, p, cfg.sk, True) for p in ks]
    ins += [_all_heads(cfg, v, cfg.sk, True)]
    args = [q.arr] + [p.arr for p in ks] + [v.arr]
    out_specs = [pl.BlockSpec((tq, nh * HEAD_V), lambda i: (i, 0)),
                 pl.BlockSpec((nh, tq, 1), lambda i: (0, i, 0)),
                 pl.BlockSpec((nh, nsub, 1, tb), lambda i: (0, i, 0, 0))]
    out_shape = [jax.ShapeDtypeStruct((s, nh * HEAD_V), F32), jax.ShapeDtypeStruct((nh, s, 1), F32),
                 jax.ShapeDtypeStruct((nh, cfg.nb, 1, tb), F32)]
    scratch = []
    if select:
        ins.append(pl.BlockSpec((cfg.sk, LANE), lambda i: (0, 0)))
        args.append(overlap)
        out_specs += [pl.BlockSpec((tq, LANE), lambda i: (i, 0)), pl.BlockSpec((nsub, LANE, tb), lambda i: (i, 0, 0))]
        out_shape += [jax.ShapeDtypeStruct((s, LANE), F32), jax.ShapeDtypeStruct((cfg.nb, LANE, tb), F32)]
        scratch = [pltpu.VMEM((nsub, LANE, tb), F32)]
    return _pcall(kern, name=name, grid=(cfg.nq,), in_specs=ins, out_specs=out_specs, out_shape=out_shape,
                  scratch_shapes=scratch,
                  compiler_params=pltpu.CompilerParams(dimension_semantics=("parallel",)))(*args)


def _attn_dq_small(cfg, q, ks, v, o, lse, do, dq_in, name):
    s, tq, tk, tb, nsub, nh, dk = cfg.s, cfg.tq, cfg.tk, cfg.tb, cfg.nsub, cfg.h, cfg.dk
    nkp = len(ks)
    has_in = dq_in is not None

    def kern(*refs):
        q_ref, k_refs, v_ref = refs[0], refs[1:1 + nkp], refs[1 + nkp]
        o_ref, l_ref, do_ref = refs[2 + nkp:5 + nkp]
        in_ref = refs[5 + nkp] if has_in else None
        dq_ref, dr_ref = refs[5 + nkp + has_in:7 + nkp + has_in]
        i = pl.program_id(0)
        for r in range(nsub):
            rows = slice(r * tb, (r + 1) * tb)
            t = i * tq + r * tb + lax.broadcasted_iota(jnp.int32, (tb, 1), 0)
            win = _key_window(cfg, i, r)
            n = win.start + lax.broadcasted_iota(jnp.int32, (1, tk), 1)
            for hh in range(nh):
                vcols = slice(hh * HEAD_V, (hh + 1) * HEAD_V)
                qcols = slice(hh * dk, (hh + 1) * dk)
                qv = q_ref[rows, qcols].astype(BF16)
                kk = _load_keys([kr.at[:, _head_cols(p, hh)] for kr, p in zip(k_refs, ks)], win)
                vv = v_ref[win, _head_cols(v, hh)].astype(BF16)
                dov = do_ref[rows, vcols]
                dvec = jnp.sum(dov * o_ref[rows, vcols], axis=1, keepdims=True)
                dr_ref[hh, r] = _to_row(dvec)
                p, _ = _scores(cfg, _nt(qv, kk), t, n, hh, None, True, l_ref[hh, rows, :])
                ds = p * (_nt(dov.astype(BF16), vv) - dvec)
                dq = _nn(ds.astype(BF16), kk) * cfg.scale
                dq_ref[rows, qcols] = dq + in_ref[rows, qcols] if has_in else dq

    qs = pl.BlockSpec((tq, nh * dk), lambda i: (i, 0))
    ins = [_all_heads(cfg, q, tq)] + [_all_heads(cfg, p, cfg.sk, True) for p in ks]
    ins += [_all_heads(cfg, v, cfg.sk, True)]
    ins += [_all_heads(cfg, o, tq), pl.BlockSpec((nh, tq, 1), lambda i: (0, i, 0)), _all_heads(cfg, do, tq)]
    args = [q.arr] + [p.arr for p in ks] + [v.arr, o.arr, lse, do.arr]
    if has_in:
        ins.append(qs)
        args.append(dq_in)
    return _pcall(
        kern, name=name, grid=(cfg.nq,), in_specs=ins,
        out_specs=[qs, pl.BlockSpec((nh, nsub, 1, tb), lambda i: (0, i, 0, 0))],
        out_shape=[jax.ShapeDtypeStruct((s, nh * dk), F32), jax.ShapeDtypeStruct((nh, cfg.nb, 1, tb), F32)],
        compiler_params=pltpu.CompilerParams(dimension_semantics=("parallel",)))(*args)


def _attn_bwd(cfg, q, ks, v, sel, selt, o, lse, lse_r, do, dq_in, name):
    if cfg.causal:
        dq, d_r = _attn_dq(cfg, q, ks, v, sel, o, lse, do, dq_in, name + "_dq")
    else:
        dq, d_r = _attn_dq_small(cfg, q, ks, v, o, lse, do, dq_in, name + "_dq")
    dkv = _attn_dkv_flat if cfg.causal or cfg.mode == "win" else _attn_dkv
    res = dkv(cfg, q, ks, v, selt, lse_r, d_r, do, name + "_dkv")
    return dq, res[:-1], res[-1]


def _silu_grad(pre):
    sg = _sigmoid(pre)
    return sg * (1.0 + pre * (1.0 - sg))


def _compress_fwd(a_lo, a_hi, pe_lo, pe_hi, w1_lo, w1_hi, w2, name):
    n, dp = a_lo.shape[0], w2.shape[1]

    def kern(alo, ahi, plo, phi, w1l, w1h, w2r, out_ref, pre_ref):
        xl = (alo[...] + plo[...]).astype(BF16)
        xh = (ahi[...] + phi[...]).astype(BF16)
        pre = _nn(xl, w1l[...]) + _nn(xh, w1h[...])
        act = pre * _sigmoid(pre)
        out_ref[...] = _nn(act.astype(BF16), w2r[...]).astype(BF16)
        pre_ref[...] = pre

    return _pcall(kern, name=name,
                  out_shape=[jax.ShapeDtypeStruct((n, dp), BF16), jax.ShapeDtypeStruct((n, dp), F32)],
                  )(a_lo, a_hi, pe_lo, pe_hi, w1_lo, w1_hi, w2)


def _compress_bwd(a_lo, a_hi, pe_lo, pe_hi, w1_lo, w1_hi, w2, pre, pre_sh, dout, dout_sh, name):
    n, ln = a_lo.shape
    dp = w2.shape[1]

    def kern(alo, ahi, plo, phi, w1l, w1h, w2r, pre_ref, presh_ref, do_ref, dosh_ref,
             da_ref, dpl_ref, dph_ref, dw1l_ref, dw1h_ref, dw2_ref):
        prev = pre_ref[...]
        act = prev * _sigmoid(prev)
        dob = do_ref[...].astype(BF16)
        w2v = w2r[...]
        dpre = (_nt(dob, w2v) * _silu_grad(prev)).astype(BF16)
        dpre_sh = (_nt(dosh_ref[...].astype(BF16), w2v) * _silu_grad(presh_ref[...])).astype(BF16)
        dw2_ref[...] = _nn(act.T.astype(BF16), dob)
        xl = alo[...] + plo[...]
        xh = ahi[...] + phi[...]
        dw1l_ref[...] = _nn(xl.T.astype(BF16), dpre)
        dw1h_ref[...] = _nn(xh.T.astype(BF16), dpre)
        dal = _nt(dpre, w1l[...])
        dah_sh = _nt(dpre_sh, w1h[...])
        da_ref[...] = dal + dah_sh
        dpl_ref[...] = jnp.sum(dal, axis=0, keepdims=True)
        dph_ref[...] = jnp.sum(dah_sh, axis=0, keepdims=True)

    return _pcall(
        kern, name=name,
        out_shape=[jax.ShapeDtypeStruct((n, ln), F32), jax.ShapeDtypeStruct((1, ln), F32),
                   jax.ShapeDtypeStruct((1, ln), F32), jax.ShapeDtypeStruct((ln, dp), F32),
                   jax.ShapeDtypeStruct((ln, dp), F32), jax.ShapeDtypeStruct((dp, dp), F32)],
    )(a_lo, a_hi, pe_lo, pe_hi, w1_lo, w1_hi, w2, pre, pre_sh, dout, dout_sh)


def _nsa_combine(o_cmp, o_slc, o_win, gl):
    s, w = o_cmp.shape
    tr = _tile(s, 512)

    def kern(a_ref, b_ref, c_ref, g_ref, o_ref):
        g = _sigmoid(g_ref[...])
        for h in range(NSA_HEADS):
            cs = slice(h * HEAD_V, (h + 1) * HEAD_V)
            o_ref[:, cs] = (g[:, 3 * h:3 * h + 1] * a_ref[:, cs] + g[:, 3 * h + 1:3 * h + 2] * b_ref[:, cs]
                            + g[:, 3 * h + 2:3 * h + 3] * c_ref[:, cs])

    row = pl.BlockSpec((tr, w), lambda i: (i, 0))
    return _pcall(kern, name="nsa_combine", grid=(s // tr,),
                  in_specs=[row, row, row, pl.BlockSpec((tr, LANE), lambda i: (i, gl.col0))], out_specs=row,
                  out_shape=jax.ShapeDtypeStruct((s, w), F32))(o_cmp, o_slc, o_win, gl.arr)


def _nsa_combine_bwd(do_cat, o_cmp, o_slc, o_win, gl):
    s, w = o_cmp.shape
    tr = _tile(s, 512)

    def kern(d_ref, a_ref, b_ref, c_ref, g_ref, da_ref, db_ref, dc_ref, dg_ref):
        g = _sigmoid(g_ref[...])
        lane = lax.broadcasted_iota(jnp.int32, (tr, LANE), 1)
        dgl = jnp.zeros((tr, LANE), F32)
        for h in range(NSA_HEADS):
            cs = slice(h * HEAD_V, (h + 1) * HEAD_V)
            dv = d_ref[:, cs]
            for b, (src, dst) in enumerate(((a_ref, da_ref), (b_ref, db_ref), (c_ref, dc_ref))):
                gate = g[:, 3 * h + b:3 * h + b + 1]
                dst[:, cs] = gate * dv
                dgate = jnp.sum(dv * src[:, cs], axis=1, keepdims=True)
                dgl = jnp.where(lane == 3 * h + b, dgate * gate * (1.0 - gate), dgl)
        dg_ref[...] = dgl

    row = pl.BlockSpec((tr, w), lambda i: (i, 0))
    tab = pl.BlockSpec((tr, LANE), lambda i: (i, 0))
    return _pcall(kern, name="nsa_combine_bwd", grid=(s // tr,),
                  in_specs=[pl.BlockSpec((tr, w), lambda i: (i, 2)), row, row, row,
                            pl.BlockSpec((tr, LANE), lambda i: (i, gl.col0))],
                  out_specs=[row, row, row, tab],
                  out_shape=[jax.ShapeDtypeStruct((s, w), F32)] * 3 + [jax.ShapeDtypeStruct((s, LANE), F32)],
                  )(do_cat, o_cmp, o_slc, o_win, gl.arr)


def _gate_fwd(o_mla, o_nsa, o_mem, hp):
    s = o_mla.shape[0]
    tr = _tile(s, 256)

    def kern(a_ref, b_ref, c_ref, z_ref, u_ref):
        z = z_ref[...]
        sz = z * _sigmoid(z)
        u_ref[:, 0:1024] = (a_ref[...] * sz[:, 0:1024]).astype(BF16)
        u_ref[:, 1024:1536] = (b_ref[...] * sz[:, 1024:1536]).astype(BF16)
        u_ref[:, 1536:2048] = (c_ref[...] * sz[:, 1536:2048]).astype(BF16)

    return _pcall(
        kern, name="gate_fwd", grid=(s // tr,),
        in_specs=[pl.BlockSpec((tr, 1024), lambda i: (i, 0)), pl.BlockSpec((tr, 512), lambda i: (i, 0)),
                  pl.BlockSpec((tr, 512), lambda i: (i, 0)), pl.BlockSpec((tr, 2048), lambda i: (i, 2))],
        out_specs=pl.BlockSpec((tr, 2048), lambda i: (i, 0)),
        out_shape=jax.ShapeDtypeStruct((s, 2048), BF16))(o_mla, o_nsa, o_mem, hp)


def _gate_bwd(du, o_mla, o_nsa, o_mem, hp):
    s = du.shape[0]
    tr = _tile(s, 256)

    def kern(d_ref, a_ref, b_ref, c_ref, z_ref, do_ref, dz_ref):
        z = z_ref[...]
        sg = _sigmoid(z)
        sz = z * sg
        dsz = sg * (1.0 + z * (1.0 - sg))
        d = d_ref[...]
        do_ref[...] = d * sz
        dz_ref[:, 0:1024] = (d[:, 0:1024] * a_ref[...] * dsz[:, 0:1024]).astype(BF16)
        dz_ref[:, 1024:1536] = (d[:, 1024:1536] * b_ref[...] * dsz[:, 1024:1536]).astype(BF16)
        dz_ref[:, 1536:2048] = (d[:, 1536:2048] * c_ref[...] * dsz[:, 1536:2048]).astype(BF16)

    wide = pl.BlockSpec((tr, 2048), lambda i: (i, 0))
    return _pcall(
        kern, name="gate_bwd", grid=(s // tr,),
        in_specs=[wide, pl.BlockSpec((tr, 1024), lambda i: (i, 0)), pl.BlockSpec((tr, 512), lambda i: (i, 0)),
                  pl.BlockSpec((tr, 512), lambda i: (i, 0)), pl.BlockSpec((tr, 2048), lambda i: (i, 2))],
        out_specs=[wide, wide],
        out_shape=[jax.ShapeDtypeStruct((s, 2048), F32), jax.ShapeDtypeStruct((s, 2048), BF16)],
    )(du, o_mla, o_nsa, o_mem, hp)


def _tile2d(rows, cols, arrays):
    if rows % 16 == 0:
        return _row_tile(rows, cols * arrays), cols
    want = max(LANE, BLOCK_BYTES // (rows * 4 * arrays) // LANE * LANE)
    tc = LANE
    for t in range(LANE, cols + 1, LANE):
        if cols % t == 0 and t <= want:
            tc = t
    return rows, tc


def _sum_slots(buf, name):
    n, rows, cols = buf.shape
    tr, tc = _tile2d(rows, cols, n)

    def kern(b_ref, o_ref):
        acc = b_ref[0].astype(F32)
        for i in range(1, n):
            acc = acc + b_ref[i].astype(F32)
        o_ref[...] = acc

    return _pcall(kern, name=name, grid=(rows // tr, cols // tc),
                  in_specs=[pl.BlockSpec((n, tr, tc), lambda i, j: (0, i, j))],
                  out_specs=pl.BlockSpec((tr, tc), lambda i, j: (i, j)),
                  out_shape=jax.ShapeDtypeStruct((rows, cols), F32))(buf)


def _chip_sum(buf, core, axis, name):
    n, rows, cols = buf.shape
    tr, tc = _tile2d(rows, cols, n)
    nbr, nbc = rows // tr, cols // tc

    def kern(c_ref, b_ref, o_ref, w_ref):
        acc = b_ref[0].astype(F32)
        for i in range(1, n):
            acc = acc + b_ref[i].astype(F32)
        o_ref[...] = acc
        w_ref[...] = acc

    place = ((lambda i, j, c: (c[0] * nbr + i, j)) if axis == 0 else (lambda i, j, c: (i, c[0] * nbc + j)))
    whole = (2 * rows, cols) if axis == 0 else (rows, 2 * cols)
    grid_spec = pltpu.PrefetchScalarGridSpec(
        num_scalar_prefetch=1, grid=(nbr, nbc),
        in_specs=[pl.BlockSpec((n, tr, tc), lambda i, j, c: (0, i, j))],
        out_specs=[pl.BlockSpec((tr, tc), lambda i, j, c: (i, j)), pl.BlockSpec((tr, tc), place)])
    return _pcall(kern, name=name, grid_spec=grid_spec,
                  out_shape=[jax.ShapeDtypeStruct((rows, cols), F32), jax.ShapeDtypeStruct(whole, F32)])(core, buf)


def _pair_sum(g4, theirs, core, axis, name):
    n, rows, cols = theirs.shape
    tr, tc = _tile2d(rows, cols, 1)
    nbr, nbc = rows // tr, cols // tc

    def kern(c_ref, a_ref, b_ref, o_ref):
        o_ref[...] = (a_ref[...] + b_ref[...]).astype(BF16)

    blk = (1, tr, tc)
    mine = ((lambda s, i, j, c: (s, c[0] * nbr + i, j)) if axis == 0
            else (lambda s, i, j, c: (s, i, c[0] * nbc + j)))
    grid_spec = pltpu.PrefetchScalarGridSpec(
        num_scalar_prefetch=1, grid=(n, nbr, nbc),
        in_specs=[pl.BlockSpec(blk, mine), pl.BlockSpec(blk, lambda s, i, j, c: (s, i, j))],
        out_specs=pl.BlockSpec(blk, lambda s, i, j, c: (s, i, j)))
    return _pcall(kern, name=name, grid_spec=grid_spec,
                  out_shape=jax.ShapeDtypeStruct((n, rows, cols), BF16))(core, g4, theirs)


def _adamw(w, g, m, v, name):
    rows, cols = w.shape
    tr, tc = _tile2d(rows, cols, 4)
    bc1 = 1.0 - ADAM_B1 ** ADAM_STEP
    bc2 = 1.0 - ADAM_B2 ** ADAM_STEP

    def kern(w_ref, g_ref, m_ref, v_ref, d_ref, mo_ref, vo_ref):
        gv = g_ref[...]
        mn = ADAM_B1 * m_ref[...] + (1.0 - ADAM_B1) * gv
        vn = ADAM_B2 * v_ref[...] + (1.0 - ADAM_B2) * (gv * gv)
        d_ref[...] = -ADAM_LR * ((mn / bc1) / (jnp.sqrt(vn / bc2) + ADAM_EPS) + ADAM_WD * w_ref[...])
        mo_ref[...] = mn
        vo_ref[...] = vn

    blk = pl.BlockSpec((tr, tc), lambda i, j: (i, j))
    return _pcall(kern, name=name, grid=(rows // tr, cols // tc), in_specs=[blk] * 4, out_specs=[blk] * 3,
                  out_shape=[jax.ShapeDtypeStruct((rows, cols), F32)] * 3)(w, g, m, v)


ANY = pl.BlockSpec(memory_space=pl.ANY)


def _place():
    x, y, c = lax.axis_index("x"), lax.axis_index("y"), lax.axis_index("c")
    chips = [(1 - x, y), (x, 1 - y), (1 - x, 1 - y)]
    return x, y, c, chips


def _remote(src, dst, send_sem, recv_sem, to):
    return pltpu.make_async_remote_copy(src_ref=src, dst_ref=dst, send_sem=send_sem, recv_sem=recv_sem,
                                        device_id=to, device_id_type=MESH)


def _half(ref, lead, core, axis):
    size = ref.shape[len(lead) + axis] // 2
    cut = pl.ds(core * size, size)
    return ref.at[tuple(lead) + ((cut, slice(None)) if axis == 0 else (slice(None), cut))]


def _gather_shards(ws, axes):
    side = _gather_side(ws, axes)

    def body(*refs):
        nw = len(ws)
        split = (refs[:nw], refs[nw:2 * nw], refs[2 * nw:])
        side.phase("start", *split)
        side.phase("finish", *split)

    return _pcall(body, name="gather_shards", in_specs=[ANY] * len(ws), out_specs=[ANY] * len(ws),
                  out_shape=side.out_shape, scratch_shapes=side.scratch)(*ws)


def _gather_side(ws, axes):
    nw = len(ws)

    def phase(which, w_refs, out_refs, sems):
        send_sems, recv_sems = sems
        x, y, c, chips = _place()
        me = 2 * x + y
        sibling = (x, y, 1 - c)

        def part(i, slot, core):
            return _half(out_refs[i], (slot,), core, axes[i])

        def copy(sem, src, dst, to):
            return _remote(src, dst, send_sems.at[sem], recv_sems.at[sem], to)

        first = [copy(j * nw + i, _half(w_refs[i], (), c, axes[i]), part(i, me, c), (*chip, c))
                 for j, chip in enumerate(chips) for i in range(nw)]
        if which == "start":
            for cp in first:
                cp.start()
            return
        passed = []
        for j, (cx, cy) in enumerate(chips):
            slot = 2 * cx + cy
            for i in range(nw):
                copy(j * nw + i, part(i, slot, c), part(i, slot, c), (x, y, c)).wait_recv()
                fwd = copy((3 + j) * nw + i, part(i, slot, c), part(i, slot, c), sibling)
                fwd.start()
                passed.append(fwd)
        for j, (cx, cy) in enumerate(chips):
            slot = 2 * cx + cy
            for i in range(nw):
                copy((3 + j) * nw + i, part(i, slot, 1 - c), part(i, slot, 1 - c), (x, y, c)).wait_recv()
        for cp in first + passed:
            cp.wait_send()

    return _Side(list(ws), [jax.ShapeDtypeStruct((4,) + w.shape, w.dtype) for w in ws],
                 [pltpu.SemaphoreType.DMA((6 * nw,)), pltpu.SemaphoreType.DMA((6 * nw,))], phase)


def _half_shape(shape, axis):
    return tuple(d // 2 if k == len(shape) - 2 + axis else d for k, d in enumerate(shape))


def _pair_exchange(gs, axes, name):
    nw = len(gs)

    def body(*refs):
        g_refs, out_refs = refs[:nw], refs[nw:2 * nw]
        send_sems, recv_sems = refs[2 * nw:]
        x, y, c, _ = _place()
        cps = []
        for i in range(nw):
            cp = _remote(_half(g_refs[i], (slice(None),), 1 - c, axes[i]), out_refs[i],
                         send_sems.at[i], recv_sems.at[i], (x, y, 1 - c))
            cp.start()
            cps.append(cp)
        for cp in cps:
            cp.wait()

    return _pcall(body, name=name, in_specs=[ANY] * nw, out_specs=[ANY] * nw,
                  out_shape=[jax.ShapeDtypeStruct(_half_shape(g.shape, a), g.dtype) for g, a in zip(gs, axes)],
                  scratch_shapes=[pltpu.SemaphoreType.DMA((nw,)), pltpu.SemaphoreType.DMA((nw,))])(*gs)


def _chip_side(ps):
    nw = len(ps)

    def phase(which, p_refs, out_refs, sems):
        send_sems, recv_sems, local_sems = sems
        x, y, c, chips = _place()
        me = 2 * x + y
        mine = [pltpu.make_async_copy(p_refs[i].at[me], out_refs[i].at[me], local_sems.at[i]) for i in range(nw)]
        sends = [_remote(p_refs[i].at[2 * cx + cy], out_refs[i].at[me], send_sems.at[j * nw + i],
                         recv_sems.at[j * nw + i], (cx, cy, c))
                 for j, (cx, cy) in enumerate(chips) for i in range(nw)]
        if which == "start":
            for cp in mine + sends:
                cp.start()
            return
        for j, (cx, cy) in enumerate(chips):
            slot = 2 * cx + cy
            for i in range(nw):
                _remote(out_refs[i].at[slot], out_refs[i].at[slot], send_sems.at[j * nw + i],
                        recv_sems.at[j * nw + i], (x, y, c)).wait_recv()
        for cp in sends:
            cp.wait_send()
        for cp in mine:
            cp.wait()

    return _Side(list(ps), [jax.ShapeDtypeStruct(p.shape, p.dtype) for p in ps],
                 [pltpu.SemaphoreType.DMA((3 * nw,)), pltpu.SemaphoreType.DMA((3 * nw,)),
                  pltpu.SemaphoreType.DMA((nw,))], phase)


def _half_exchange(ts, wholes, axes):
    nw = len(ts)

    def body(*refs):
        t_refs, out_refs = refs[:nw], refs[2 * nw:3 * nw]
        send_sems, recv_sems = refs[3 * nw:]
        x, y, c, _ = _place()
        sends = []
        for i in range(nw):
            cp = _remote(t_refs[i], _half(out_refs[i], (), c, axes[i]), send_sems.at[i], recv_sems.at[i],
                         (x, y, 1 - c))
            cp.start()
            sends.append(cp)
        for i in range(nw):
            _remote(t_refs[i], _half(out_refs[i], (), 1 - c, axes[i]), send_sems.at[i], recv_sems.at[i],
                    (x, y, c)).wait_recv()
        for cp in sends:
            cp.wait_send()

    return _pcall(body, name="half_exchange", in_specs=[ANY] * (2 * nw), out_specs=[ANY] * nw,
                  out_shape=[jax.ShapeDtypeStruct(w.shape, w.dtype) for w in wholes],
                  input_output_aliases={nw + i: i for i in range(nw)},
                  scratch_shapes=[pltpu.SemaphoreType.DMA((nw,)), pltpu.SemaphoreType.DMA((nw,))])(*ts, *wholes)


def _gather_all(v):
    rows, cols = v.shape

    def body(v_ref, out_ref, send_sems, recv_sems, local_sem):
        x, y, c, _ = _place()
        me = 4 * x + 2 * y + c
        mine = pltpu.make_async_copy(v_ref, out_ref.at[me], local_sem)
        mine.start()
        sends = []
        for d in range(1, 8):
            peer = (x ^ (d >> 2), y ^ ((d >> 1) & 1), c ^ (d & 1))
            cp = _remote(v_ref, out_ref.at[me], send_sems.at[d - 1], recv_sems.at[d - 1], peer)
            cp.start()
            sends.append(cp)
        for d in range(1, 8):
            slot = 4 * (x ^ (d >> 2)) + 2 * (y ^ ((d >> 1) & 1)) + (c ^ (d & 1))
            _remote(v_ref, out_ref.at[slot], send_sems.at[d - 1], recv_sems.at[d - 1], (x, y, c)).wait_recv()
        for cp in sends:
            cp.wait_send()
        mine.wait()

    return _pcall(body, name="gather_all", in_specs=[ANY], out_specs=ANY,
                  out_shape=jax.ShapeDtypeStruct((8, rows, cols), v.dtype),
                  scratch_shapes=[pltpu.SemaphoreType.DMA((7,)), pltpu.SemaphoreType.DMA((7,)),
                                  pltpu.SemaphoreType.DMA])(v)


def _pad_cols(a, width):
    return a if a.shape[1] == width else jnp.pad(a, ((0, 0), (0, width - a.shape[1])))


def _unpad_segments():
    z = PAD["z"]
    segs = [(PAD["c_q"], 0, 512), (PAD["c_kv"], 512, 512), (PAD["k_rope"], 1024, 64), (z, 1088, 1024)]
    segs += [(PAD["q_nsa"] + 256 * h, 2112 + NSA_DK * h, NSA_DK) for h in range(NSA_HEADS)]
    for name, rows in (("k_c", 192), ("v_c", 128), ("k_s", 192), ("v_s", 128), ("k_w", 192), ("v_w", 128),
                       ("g_nsa", 12)):
        segs.append((PAD[name], ORIG[name][0], rows))
    segs += [(z + 1024, ORIG["z_nsa"][0], 512), (PAD["q_mem"], ORIG["q_mem"][0], 512),
             (z + 1536, ORIG["z_mem"][0], 512)]
    return segs


def _w_in_grad_slots(gt):
    rows, cols = gt.shape
    shard = sum(n for _, _, n in _unpad_segments()) // 4
    tc = 256
    pieces = []
    for src, dst, n in _unpad_segments():
        while n:
            slot, off = divmod(dst, shard)
            take = min(n, shard - off)
            pieces.append((src, slot, off, take))
            src, dst, n = src + take, dst + take, n - take

    def kern(g_ref, o_ref):
        for src, slot, off, take in pieces:
            o_ref[slot, off:off + take, :] = g_ref[src:src + take, :]

    return _pcall(kern, name="w_in_grad_slots", grid=(cols // tc,),
                  in_specs=[pl.BlockSpec((rows, tc), lambda i: (0, i))],
                  out_specs=pl.BlockSpec((4, shard, tc), lambda i: (0, 0, i)),
                  out_shape=jax.ShapeDtypeStruct((4, shard, cols), F32))(gt)


def _w_in_from_slots(ws):
    nslot, shard, cols = ws.shape
    tc = 256
    pieces = []
    for dst, src, n in _unpad_segments() + [(PAD["k_rope"] + 64, ORIG["k_rope"][0], 64)]:
        while n:
            slot, off = divmod(src, shard)
            take = min(n, shard - off)
            pieces.append((dst, slot, off, take))
            src, dst, n = src + take, dst + take, n - take

    def kern(w_ref, o_ref):
        o_ref[...] = jnp.zeros_like(o_ref)
        for dst, slot, off, take in pieces:
            o_ref[dst:dst + take, :] = w_ref[slot, off:off + take, :]

    return _pcall(kern, name="w_in_from_slots", grid=(cols // tc,),
                  in_specs=[pl.BlockSpec((nslot, shard, tc), lambda i: (0, 0, i))],
                  out_specs=pl.BlockSpec((D_PAD, tc), lambda i: (0, i)),
                  out_shape=jax.ShapeDtypeStruct((D_PAD, cols), ws.dtype))(ws)


def _rope_tables(s):
    pos = jnp.arange(s, dtype=F32)
    inv_freq = ROPE_THETA ** (-jnp.arange(0, 64, 2, dtype=F32) / 64)
    ang = pos[:, None] * inv_freq[None, :]
    cos, sin = jnp.cos(ang), jnp.sin(ang)
    z = jnp.zeros((s, 64), F32)
    return jnp.concatenate([cos, cos, z], axis=1), jnp.concatenate([-sin, sin, z], axis=1)


def _overlap_table(s):
    n_c, n_s = s // CMP_STRIDE, s // SLC_LEN
    c0 = np.arange(n_c)[:, None] * CMP_STRIDE
    s0 = np.arange(LANE)[None, :] * SLC_LEN
    ov = (c0 < s0 + SLC_LEN) & (c0 + CMP_LEN > s0) & (np.arange(n_c)[:, None] < n_c - 1) & (np.arange(LANE)[None, :] < n_s)
    return jnp.asarray(ov.astype(np.float32), dtype=BF16)


def _shift_down(a):
    return jnp.concatenate([jnp.zeros((8, a.shape[1]), a.dtype), a], axis=0)[7:7 + a.shape[0]]


def _shift_up(a):
    return jnp.concatenate([a, jnp.zeros((8, a.shape[1]), a.dtype)], axis=0)[1:1 + a.shape[0]]


def _local_step(x, mem, target, w, hooks=None):
    s = x.shape[0]
    cs, sn = _rope_tables(s)
    t_ = jnp.transpose

    w_in_p = _w_in_from_slots(w["w_in_t"])
    xn, rstd_x = _rms_fwd(_Src(x, D_MODEL), w["norm_g"], "norm_x")
    if hooks is None:
        hp, hpb = _mm(xn, w_in_p, "in_proj", mode="nt", second_dtype=BF16)
    else:
        hp, hpb, *gathered = _mm(xn, w_in_p, "in_proj", mode="nt", second_dtype=BF16, side=hooks.gather_side)
        w = {**w, **hooks.weights(gathered)}

    w_uq3 = w["w_uq"].reshape(512, MLA_HEADS, 192)
    w_uq_p = jnp.concatenate([w_uq3, w_uq3[:, :, 128:]], axis=2).reshape(512, MLA_HEADS * 256)
    w_ukv_p = t_(w["w_ukv"].reshape(512, MLA_HEADS, 2, 128), (0, 2, 1, 3)).reshape(512, 2048)
    c_q, c_kv = _Src(hp, 512, 0), _Src(hp, 512, 1)
    cqn, rstd_q = _rms_fwd(c_q, w["q_norm_g"], "norm_q")
    ckvn, rstd_kv = _rms_fwd(c_kv, w["kv_norm_g"], "norm_kv")
    q_lin = _mm(cqn, w_uq_p, "mla_q_proj")
    kvb = _mm(ckvn, w_ukv_p, "mla_kv_proj", out_dtype=BF16)
    q_mla = _rope_fwd(_Src(q_lin, MLA_HEADS * 256), cs, sn, MLA_HEADS, 256, LANE, "rope_q")
    k_pe = _rope_fwd(_Src(hp, LANE, PAD["k_rope"] // LANE), cs, sn, 1, LANE, 0, "rope_k")
    mla = _Attn("mla", s, s, MLA_HEADS, 256)
    mla_q, mla_v = _Src(q_mla, 256), _Src(kvb, LANE, MLA_HEADS)
    mla_k = [_Src(kvb, LANE), _Src(k_pe, LANE, 0, False)]
    o_mla, l_mla, lr_mla = _attn_fwd(mla, mla_q, mla_k, mla_v, None, "mla_fwd")

    sk = s // CMP_STRIDE
    pe_k, pe_v = w["cmp_pe_k"], w["cmp_pe_v"]
    w1k = _pad_cols(w["cmp_w1k"], 256)
    w2k = jnp.pad(w["cmp_w2k"], ((0, 64), (0, 64))).astype(BF16)
    w1v, w2v = w["cmp_w1v"], w["cmp_w2v"].astype(BF16)
    half_k, half_v = CMP_STRIDE * NSA_DK, CMP_STRIDE * HEAD_V
    ak = hp[:, PAD["k_c"]:PAD["k_c"] + NSA_DK].reshape(sk, half_k)
    av = hp[:, PAD["v_c"]:PAD["v_c"] + HEAD_V].reshape(sk, half_v)
    ck_args = (ak, _shift_up(ak), pe_k[:CMP_STRIDE].reshape(1, half_k), pe_k[CMP_STRIDE:].reshape(1, half_k),
               w1k[:half_k], w1k[half_k:], w2k)
    cv_args = (av, _shift_up(av), pe_v[:CMP_STRIDE].reshape(1, half_v), pe_v[CMP_STRIDE:].reshape(1, half_v),
               w1v[:half_v], w1v[half_v:], w2v)
    k_cmp, pre_k = _compress_fwd(*ck_args, "compress_k")
    v_cmp, pre_v = _compress_fwd(*cv_args, "compress_v")
    cmp_ = _Attn("cmp", s, sk, NSA_HEADS, 256)
    slc = _Attn("slc", s, s, NSA_HEADS, 256)
    win = _Attn("win", s, s, NSA_HEADS, 256)
    nsa_q = _Src(hpb, 256, PAD["q_nsa"] // 256)
    cmp_k, cmp_v = [_Src(k_cmp, 256, 0, False)], _Src(v_cmp, HEAD_V, 0, False)
    slc_k, slc_v = [_Src(hpb, 256, PAD["k_s"] // 256, False)], _Src(hpb, HEAD_V, PAD["v_s"] // HEAD_V, False)
    win_k, win_v = [_Src(hpb, 256, PAD["k_w"] // 256, False)], _Src(hpb, HEAD_V, PAD["v_w"] // HEAD_V, False)
    o_cmp, l_cmp, lr_cmp, sel, selt = _attn_fwd_small(cmp_, nsa_q, cmp_k, cmp_v, "cmp_fwd", _overlap_table(s))
    o_slc, l_slc, lr_slc = _attn_fwd(slc, nsa_q, slc_k, slc_v, sel, "slc_fwd")
    o_win, l_win, lr_win = _attn_fwd_small(win, nsa_q, win_k, win_v, "win_fwd")
    gl = _Src(hp, LANE, PAD["g_nsa"] // LANE)
    o_nsa = _nsa_combine(o_cmp, o_slc, o_win, gl)

    mn, rstd_m = _rms_fwd(_Src(mem, D_MODEL), w["mem_norm_g"], "norm_mem")
    kvm = _mm(mn, w["w_mem_kv"], "mem_kv_proj", out_dtype=BF16)
    mem_ = _Attn("mem", s, mem.shape[0], MEM_HEADS, LANE)
    mem_q, mem_k, mem_v = _Src(hpb, LANE, PAD["q_mem"] // LANE), [_Src(kvm, LANE)], _Src(kvm, LANE, MEM_HEADS)
    o_mem, l_mem, lr_mem = _attn_fwd_small(mem_, mem_q, mem_k, mem_v, "mem_fwd")

    u = _gate_fwd(o_mla, o_nsa, o_mem, hp)
    proj = _mm(u, w["w_out"], "out_proj", wide=2048)
    dy, g_final, loss = _final_loss(x, proj, w["final_norm_g"].reshape(1, -1), target)

    g_w_out = _mm(u, dy, "out_proj_dw", mode="tn")
    du = _mm(dy, w["w_out"], "out_proj_dx", mode="nt", wide=2048)
    do_cat, dz = _gate_bwd(du, o_mla, o_nsa, o_mem, hp)

    dq_mem, (dk_mem,), dv_mem = _attn_bwd(mem_, mem_q, mem_k, mem_v, None, None, _Src(o_mem, HEAD_V), l_mem,
                                          lr_mem, _Src(do_cat, HEAD_V, 12), None, "mem_bwd")
    dkvm = jnp.concatenate([dk_mem, dv_mem], axis=1)
    g_w_mem_kv = _mm(mn, dkvm, "mem_kv_dw", mode="tn")
    dmn = _mm(dkvm, w["w_mem_kv"], "mem_kv_dx", mode="nt")
    _, g_mem_norm = _rms_bwd(_Src(mem, D_MODEL), w["mem_norm_g"], rstd_m, dmn, None, "norm_mem_bwd")

    do_cmp, do_slc, do_win, dgl = _nsa_combine_bwd(do_cat, o_cmp, o_slc, o_win, gl)
    dq_n, (dk_cmp,), dv_cmp = _attn_bwd(cmp_, nsa_q, cmp_k, cmp_v, None, None, _Src(o_cmp, HEAD_V), l_cmp,
                                        lr_cmp, _Src(do_cmp, HEAD_V), None, "cmp_bwd")
    dq_n, (dk_s,), dv_s = _attn_bwd(slc, nsa_q, slc_k, slc_v, sel, selt, _Src(o_slc, HEAD_V), l_slc, lr_slc,
                                    _Src(do_slc, HEAD_V), dq_n, "slc_bwd")
    dq_n, (dk_w,), dv_w = _attn_bwd(win, nsa_q, win_k, win_v, None, None, _Src(o_win, HEAD_V), l_win, lr_win,
                                    _Src(do_win, HEAD_V), dq_n, "win_bwd")
    dak, dpk_lo, dpk_hi, dw1k_lo, dw1k_hi, g_w2k = _compress_bwd(
        *ck_args, pre_k, _shift_down(pre_k), dk_cmp, _shift_down(dk_cmp), "compress_k_bwd")
    dav, dpv_lo, dpv_hi, dw1v_lo, dw1v_hi, g_w2v = _compress_bwd(
        *cv_args, pre_v, _shift_down(pre_v), dv_cmp, _shift_down(dv_cmp), "compress_v_bwd")
    g_pe_k = jnp.concatenate([dpk_lo.reshape(CMP_STRIDE, NSA_DK), dpk_hi.reshape(CMP_STRIDE, NSA_DK)], axis=0)
    g_pe_v = jnp.concatenate([dpv_lo.reshape(CMP_STRIDE, HEAD_V), dpv_hi.reshape(CMP_STRIDE, HEAD_V)], axis=0)
    g_w1k = jnp.concatenate([dw1k_lo, dw1k_hi], axis=0)[:, :NSA_DK]
    g_w1v = jnp.concatenate([dw1v_lo, dw1v_hi], axis=0)
    dk_c = _pad_cols(dak.reshape(s, NSA_DK), 256)
    dv_c = dav.reshape(s, HEAD_V)

    dq_m, (dk_nope, dk_pe), dv_m = _attn_bwd(mla, mla_q, mla_k, mla_v, None, None, _Src(o_mla, HEAD_V), l_mla,
                                             lr_mla, _Src(do_cat, HEAD_V), None, "mla_bwd")
    dq_lin = _rope_bwd_q(dq_m, cs, sn)
    dkv_lin, d_krope = _rope_bwd_k(dk_nope, dk_pe, dv_m, cs, sn)
    g_w_uq_p = _mm(cqn, dq_lin, "mla_q_dw", mode="tn")
    dcqn = _mm(dq_lin, w_uq_p, "mla_q_dx", mode="nt")
    g_w_ukv_p = _mm(ckvn, dkv_lin, "mla_kv_dw", mode="tn")
    dckvn = _mm(dkv_lin, w_ukv_p, "mla_kv_dx", mode="nt")
    dc_q, g_q_norm = _rms_bwd(c_q, w["q_norm_g"], rstd_q, dcqn, None, "norm_q_bwd", BF16)
    dc_kv, g_kv_norm = _rms_bwd(c_kv, w["kv_norm_g"], rstd_kv, dckvn, None, "norm_kv_bwd", BF16)
    g_w_uq = g_w_uq_p.reshape(512, MLA_HEADS, 256)[:, :, :192].reshape(512, MLA_HEADS * 192)
    g_w_ukv = t_(g_w_ukv_p.reshape(512, 2, MLA_HEADS, 128), (0, 2, 1, 3)).reshape(512, 2048)

    pieces = [dc_q, dc_kv, dq_n, dk_c, dk_s, dk_w, d_krope, dv_c, dv_s, dv_w, dgl,
              jnp.zeros((s, PAD["q_mem"] - (PAD["g_nsa"] + LANE)), BF16), dq_mem, dz]
    dhp = jnp.concatenate([p.astype(BF16) for p in pieces], axis=1)
    grads = dict(q_norm_g=g_q_norm, w_uq=g_w_uq, kv_norm_g=g_kv_norm,
                 w_ukv=g_w_ukv, cmp_pe_k=g_pe_k, cmp_pe_v=g_pe_v, cmp_w1k=g_w1k, cmp_w2k=g_w2k[:NSA_DK, :NSA_DK],
                 cmp_w1v=g_w1v, cmp_w2v=g_w2v, mem_norm_g=g_mem_norm, w_mem_kv=g_w_mem_kv, w_out=g_w_out,
                 final_norm_g=g_final.reshape(-1))
    if hooks is None:
        g_w_in_t = _w_in_grad_slots(_mm(dhp, xn, "in_proj_dw", mode="tn", wide=2048))
        dxn = _mm(dhp, w_in_p, "in_proj_dx", wide=2048)
    else:
        g_w_in_p, *hooks.received = _mm(dhp, xn, "in_proj_dw", mode="tn", wide=2048, side=hooks.reduce_side(grads))
        g_w_in_t = _w_in_grad_slots(g_w_in_p)
        dxn, hooks.received_w_in = _mm(dhp, w_in_p, "in_proj_dx", wide=2048, side=hooks.reduce_side_w_in(g_w_in_t))
    grad_x, g_norm = _rms_bwd(_Src(x, D_MODEL), w["norm_g"], rstd_x, dxn, dy, "norm_x_bwd")
    grads.update(norm_g=g_norm, w_in_t=g_w_in_t)
    return loss[0, 0], grad_x, grads


def kernel(x, mem, norm_g, w_in, q_norm_g, w_uq, kv_norm_g, w_ukv, cmp_pe_k, cmp_pe_v, cmp_w1k, cmp_w2k, cmp_w1v, cmp_w2v, mem_norm_g, w_mem_kv, w_out, final_norm_g, loss_target, m_norm_g, m_w_in, m_q_norm_g, m_w_uq, m_kv_norm_g, m_w_ukv, m_cmp_pe_k, m_cmp_pe_v, m_cmp_w1k, m_cmp_w2k, m_cmp_w1v, m_cmp_w2v, m_mem_norm_g, m_w_mem_kv, m_w_out, m_final_norm_g, v_norm_g, v_w_in, v_q_norm_g, v_w_uq, v_kv_norm_g, v_w_ukv, v_cmp_pe_k, v_cmp_pe_v, v_cmp_w1k, v_cmp_w2k, v_cmp_w1v, v_cmp_w2v, v_mem_norm_g, v_w_mem_kv, v_w_out, v_final_norm_g):
    args = dict(locals())
    wts = {n: args[n] for n in WEIGHTS}
    loc = {n: (a if n == "final_norm_g" else a[0]) for n, a in wts.items()}

    def to_x(n, a):
        return a.T if n == "w_in" else a

    split = [1 if n == "w_in" else 0 for n in SHARDED]
    rest = [n for n in SHARDED if n != "w_in"]
    chip = 2 * lax.axis_index("x") + lax.axis_index("y")
    core = lax.axis_index("c").astype(jnp.int32).reshape(1)
    own = {n: to_x(n, loc[n]).astype(BF16) for n in SHARDED}

    def with_own_slot(gw, a):
        return lax.dynamic_update_slice(gw, a[None], (chip, 0, 0))

    def slots(n, a):
        if n == "w_in":
            return a
        if SHARD_AXIS[n] == 0:
            return a.reshape(4, a.shape[0] // 4, a.shape[1])
        width = a.shape[1] // 4
        return jnp.stack([a[:, j * width:(j + 1) * width] for j in range(4)])

    def pair_sums(names, grads, name):
        axes = [1 if n == "w_in" else 0 for n in names]
        gs = [slots(n, a) for n, a in zip(names, grads)]
        theirs = _pair_exchange(gs, axes, name)
        return [_pair_sum(a, b, core, ax, "pair_sum_" + n) for n, a, b, ax in zip(names, gs, theirs, axes)]

    class Hooks:
        gather_side = _gather_side([own[n] for n in rest], [0] * len(rest))
        received = None

        @staticmethod
        def weights(gathered):
            out = {}
            for n, gw in zip(rest, gathered):
                gw = with_own_slot(gw, own[n])
                if SHARD_AXIS[n] == 0:
                    out[n] = gw.reshape(4 * gw.shape[1], gw.shape[2])
                else:
                    out[n] = jnp.concatenate([gw[j] for j in range(4)], axis=1)
            return out

        @staticmethod
        def reduce_side(grads):
            return _chip_side(pair_sums(rest, [grads[n] for n in rest], "pair_exchange_rest"))

        @staticmethod
        def reduce_side_w_in(g_w_in_t):
            return _chip_side(pair_sums(["w_in"], [g_w_in_t], "pair_exchange_w_in"))

    hooks = Hooks()

    start = {n: loc[n].reshape(1, -1) if loc[n].ndim == 1 else loc[n] for n in REPLICATED}
    start["w_in_t"] = with_own_slot(_gather_shards([own["w_in"]], [1])[0], own["w_in"])
    loss, grad_x, g = _local_step(x[0], mem[0], loss_target[0], start, hooks)
    loss = lax.psum(loss, ("x", "y", "c"))

    from_chips = dict(zip(rest, hooks.received), w_in=hooks.received_w_in)
    sums = [_chip_sum(from_chips[n], core, ax, "chip_sum_" + n) for n, ax in zip(SHARDED, split)]
    g_sh = _half_exchange([a for a, _ in sums], [b for _, b in sums], split)

    n_rep = sum(int(np.prod(loc[n].shape)) for n in REPLICATED)
    rows_rep = -(-n_rep // (8 * LANE)) * 8

    def rep_pack(parts):
        flat = jnp.concatenate([p.reshape(-1) for p in parts])
        return jnp.pad(flat, (0, rows_rep * LANE - n_rep)).reshape(rows_rep, LANE)

    g_rep = _sum_slots(_gather_all(rep_pack([g[n] for n in REPLICATED])), "replica_sum")
    d_rp, m_rp, v_rp = _adamw(rep_pack([wts[n] for n in REPLICATED]), g_rep,
                              rep_pack([args["m_" + n] for n in REPLICATED]),
                              rep_pack([args["v_" + n] for n in REPLICATED]), "adamw_replicated")

    def rep_unpack(buf):
        flat, out, o = buf.reshape(-1), {}, 0
        for n in REPLICATED:
            size = int(np.prod(wts[n].shape))
            out[n] = flat[o:o + size].reshape(wts[n].shape)
            o += size
        return out

    outs = {k: rep_unpack(b) for k, b in (("g", g_rep), ("d", d_rp), ("m", m_rp), ("v", v_rp))}
    for n, gn in zip(SHARDED, g_sh):
        d, mo, vo = _adamw(to_x(n, loc[n]), gn, to_x(n, args["m_" + n][0]), to_x(n, args["v_" + n][0]),
                           "adamw_" + n)
        for k, a in (("g", gn), ("d", d), ("m", mo), ("v", vo)):
            outs[k][n] = to_x(n, a).reshape(wts[n].shape)

    return (loss, grad_x[None], *[outs["g"][n] for n in WEIGHTS], *[outs["d"][n] for n in WEIGHTS],
            *[outs["m"][n] for n in WEIGHTS], *[outs["v"][n] for n in WEIGHTS])
```

```python
from typing import NamedTuple

import numpy as np
import jax
import jax.numpy as jnp
from jax import lax
from jax.experimental import pallas as pl
from jax.experimental.pallas import tpu as pltpu

F32 = jnp.float32
BF16 = jnp.bfloat16
MESH = pl.DeviceIdType.MESH

D_MODEL = 2048
EPS = 1e-6
LANE = 128
HEAD_V = 128
MLA_HEADS = 8
NSA_HEADS = 4
MEM_HEADS = 4
NSA_DK = 192
CMP_STRIDE = 16
CMP_LEN = 32
SLC_LEN = 64
SLC_TOPN = 16
WIN = 512
NEG = -1e30
LOG2E = 1.4426950408889634
ROPE_THETA = 10000.0
BLOCK_BYTES = 2 << 20

ORIG = dict(c_q=(0, 512), c_kv=(512, 512), k_rope=(1024, 64), z_mla=(1088, 1024),
            q_nsa=(2112, 768), k_c=(2880, 192), v_c=(3072, 128), k_s=(3200, 192),
            v_s=(3392, 128), k_w=(3520, 192), v_w=(3712, 128), g_nsa=(3840, 12),
            z_nsa=(3852, 512), q_mem=(4364, 512), z_mem=(4876, 512))
PAD = dict(c_q=0, c_kv=512, q_nsa=1024, k_c=2048, k_s=2304, k_w=2560, k_rope=2816, v_c=2944,
           v_s=3072, v_w=3200, g_nsa=3328, q_mem=3584, z=4096)
D_PAD = 6144

ADAM_LR, ADAM_B1, ADAM_B2, ADAM_EPS, ADAM_WD, ADAM_STEP = 0.001, 0.9, 0.999, 1e-08, 0.01, 10

SHARDED = ("w_in", "w_uq", "w_ukv", "cmp_w1k", "cmp_w1v", "w_mem_kv", "w_out")
SHARD_AXIS = dict(w_in=1, w_uq=1, w_ukv=1, cmp_w1k=0, cmp_w1v=0, w_mem_kv=0, w_out=0)
REPLICATED = ("norm_g", "q_norm_g", "kv_norm_g", "cmp_pe_k", "cmp_pe_v", "cmp_w2k", "cmp_w2v",
              "mem_norm_g", "final_norm_g")
WEIGHTS = ("norm_g", "w_in", "q_norm_g", "w_uq", "kv_norm_g", "w_ukv", "cmp_pe_k", "cmp_pe_v",
           "cmp_w1k", "cmp_w2k", "cmp_w1v", "cmp_w2v", "mem_norm_g", "w_mem_kv", "w_out",
           "final_norm_g")


def _pcall(kernel, **kw):
    return pl.pallas_call(kernel, **kw)


def _tile(n, pref):
    if n <= pref:
        return n
    for t in range(pref, LANE - 1, -LANE):
        if n % t == 0:
            return t
    raise ValueError((n, pref))


def _row_tile(rows, cols, itemsize=4):
    want = max(16, BLOCK_BYTES // (cols * itemsize))
    if rows <= want:
        return rows
    t = 16
    best = rows
    while t <= want:
        if rows % t == 0:
            best = t
        t *= 2
    return best


def _nt(a, b):
    return lax.dot_general(a, b, (((1,), (1,)), ((), ())), preferred_element_type=F32)


def _tn(a, b):
    return lax.dot_general(a, b, (((0,), (0,)), ((), ())), preferred_element_type=F32)


def _nn(a, b):
    return jnp.dot(a, b, preferred_element_type=F32)


def _sigmoid(x):
    return 1.0 / (1.0 + jnp.exp(-x))


class _Src(NamedTuple):
    arr: jax.Array
    width: int
    col0: int = 0
    per_head: bool = True

    def col(self, h):
        return self.col0 + h if self.per_head else self.col0


class _Side(NamedTuple):
    inputs: list
    out_shape: list
    scratch: list
    phase: object


def _mm(a, b, name, mode="nn", out_dtype=F32, second_dtype=None, wide=1024, side=None):
    if mode == "tn":
        k, m = a.shape
    else:
        m, k = a.shape
    if mode == "nt":
        n, k2 = b.shape
    else:
        k2, n = b.shape
    assert k == k2, (a.shape, b.shape, mode)
    tm, tn, tk = _tile(m, 1024), _tile(n, wide), _tile(k, 2048)
    grid = (m // tm, n // tn, k // tk)
    nk = grid[2]
    assert nk == 1 or (out_dtype == F32 and second_dtype is None)
    dot = {"nn": _nn, "nt": _nt, "tn": _tn}[mode]
    n_in = len(side.inputs) if side else 0
    n_out = len(side.out_shape) if side else 0
    n_res = 1 + (second_dtype is not None)

    def kern(*refs):
        a_ref, b_ref = refs[:2]
        res = refs[2 + n_in:2 + n_in + n_res]
        step = [pl.program_id(d) for d in range(3)]
        if side:
            side_refs = (refs[2:2 + n_in], refs[2 + n_in + n_res:2 + n_in + n_res + n_out],
                         refs[2 + n_in + n_res + n_out:])

            @pl.when((step[0] == 0) & (step[1] == 0) & (step[2] == 0))
            def _():
                side.phase("start", *side_refs)

        r = dot(a_ref[...].astype(BF16), b_ref[...].astype(BF16))
        if nk == 1:
            res[0][...] = r.astype(out_dtype)
            if n_res == 2:
                res[1][...] = r.astype(second_dtype)
        else:
            @pl.when(step[2] == 0)
            def _():
                res[0][...] = r

            @pl.when(step[2] > 0)
            def _():
                res[0][...] += r

        if side:
            @pl.when((step[0] == grid[0] - 1) & (step[1] == grid[1] - 1) & (step[2] == nk - 1))
            def _():
                side.phase("finish", *side_refs)

    a_spec = (pl.BlockSpec((tk, tm), lambda i, j, kk: (kk, i)) if mode == "tn"
              else pl.BlockSpec((tm, tk), lambda i, j, kk: (i, kk)))
    b_spec = (pl.BlockSpec((tn, tk), lambda i, j, kk: (j, kk)) if mode == "nt"
              else pl.BlockSpec((tk, tn), lambda i, j, kk: (kk, j)))
    o_spec = pl.BlockSpec((tm, tn), lambda i, j, kk: (i, j))
    out_specs = [o_spec] * n_res + [ANY] * n_out
    out_shape = [jax.ShapeDtypeStruct((m, n), out_dtype)]
    if second_dtype is not None:
        out_shape.append(jax.ShapeDtypeStruct((m, n), second_dtype))
    out_shape += list(side.out_shape) if side else []
    semantics = ("arbitrary",) * 3 if side else ("parallel", "parallel", "arbitrary")
    out = _pcall(
        kern, name=name, grid=grid, in_specs=[a_spec, b_spec] + [ANY] * n_in, out_specs=out_specs,
        out_shape=out_shape, scratch_shapes=list(side.scratch) if side else [],
        compiler_params=pltpu.CompilerParams(dimension_semantics=semantics),
    )(a, b, *(side.inputs if side else []))
    return out[0] if len(out) == 1 else out


def _rms_fwd(x, g, name, side=None):
    r, d = x.arr.shape[0], x.width
    tr = _tile(r, 512)
    steps = r // tr
    n_in = len(side.inputs) if side else 0
    n_out = len(side.out_shape) if side else 0

    def kern(*refs):
        x_ref, g_ref = refs[:2]
        y_ref, r_ref = refs[2 + n_in:4 + n_in]
        if side:
            side_refs = (refs[2:2 + n_in], refs[4 + n_in:4 + n_in + n_out], refs[4 + n_in + n_out:])

            @pl.when(pl.program_id(0) == 0)
            def _():
                side.phase("start", *side_refs)

        xv = x_ref[...]
        rstd = lax.rsqrt(jnp.mean(xv * xv, axis=-1, keepdims=True) + EPS)
        y_ref[...] = (xv * rstd * g_ref[...]).astype(BF16)
        r_ref[...] = rstd
        if side:
            @pl.when(pl.program_id(0) == steps - 1)
            def _():
                side.phase("finish", *side_refs)

    return _pcall(
        kern, name=name, grid=(steps,),
        in_specs=[pl.BlockSpec((tr, d), lambda i: (i, x.col0)), pl.BlockSpec((1, d), lambda i: (0, 0))] + [ANY] * n_in,
        out_specs=[pl.BlockSpec((tr, d), lambda i: (i, 0)), pl.BlockSpec((tr, 1), lambda i: (i, 0))] + [ANY] * n_out,
        out_shape=[jax.ShapeDtypeStruct((r, d), BF16), jax.ShapeDtypeStruct((r, 1), F32)]
        + (list(side.out_shape) if side else []),
        scratch_shapes=list(side.scratch) if side else [],
        compiler_params=pltpu.CompilerParams(dimension_semantics=("arbitrary",)),
    )(x.arr, g, *(side.inputs if side else []))


def _rms_bwd(x, g, rstd, dy, add, name, dx_dtype=F32):
    r, d = x.arr.shape[0], x.width
    tr = _tile(r, 256)
    has_add = add is not None

    def kern(*refs):
        if has_add:
            x_ref, g_ref, r_ref, dy_ref, add_ref, dx_ref, dg_ref = refs
        else:
            x_ref, g_ref, r_ref, dy_ref, dx_ref, dg_ref = refs
        rs = r_ref[...]
        xhat = x_ref[...] * rs
        dyv = dy_ref[...]
        dyg = dyv * g_ref[...]
        c = jnp.mean(dyg * xhat, axis=-1, keepdims=True)
        dx = rs * (dyg - xhat * c)
        if has_add:
            dx = dx + add_ref[...]
        dx_ref[...] = dx.astype(dx_dtype)
        part = jnp.sum(dyv * xhat, axis=0, keepdims=True)

        @pl.when(pl.program_id(0) == 0)
        def _():
            dg_ref[...] = part

        @pl.when(pl.program_id(0) > 0)
        def _():
            dg_ref[...] += part

    row = pl.BlockSpec((tr, d), lambda i: (i, 0))
    vec = pl.BlockSpec((1, d), lambda i: (0, 0))
    ins = [pl.BlockSpec((tr, d), lambda i: (i, x.col0)), vec, pl.BlockSpec((tr, 1), lambda i: (i, 0)), row]
    ins += [row] if has_add else []
    args = (x.arr, g, rstd, dy) + ((add,) if has_add else ())
    return _pcall(
        kern, name=name, grid=(r // tr,), in_specs=ins, out_specs=[row, vec],
        out_shape=[jax.ShapeDtypeStruct((r, d), dx_dtype), jax.ShapeDtypeStruct((1, d), F32)],
        compiler_params=pltpu.CompilerParams(dimension_semantics=("arbitrary",)),
    )(*args)


def _final_loss(x, proj, g, target):
    r, d = x.shape
    tr = _tile(r, 256)

    def kern(x_ref, p_ref, g_ref, t_ref, dy_ref, dg_ref, loss_ref):
        y = x_ref[...] + p_ref[...]
        rs = lax.rsqrt(jnp.mean(y * y, axis=-1, keepdims=True) + EPS)
        yhat = y * rs
        gv = g_ref[...]
        e = yhat * gv - t_ref[...]
        lpart = 0.5 * jnp.sum(jnp.mean(e * e, axis=-1, keepdims=True), axis=0, keepdims=True)
        dout = e * (1.0 / d)
        dyg = dout * gv
        c = jnp.mean(dyg * yhat, axis=-1, keepdims=True)
        dy_ref[...] = rs * (dyg - yhat * c)
        gpart = jnp.sum(dout * yhat, axis=0, keepdims=True)
        lrow = jnp.broadcast_to(lpart, (1, LANE))

        @pl.when(pl.program_id(0) == 0)
        def _():
            dg_ref[...] = gpart
            loss_ref[...] = lrow

        @pl.when(pl.program_id(0) > 0)
        def _():
            dg_ref[...] += gpart
            loss_ref[...] += lrow

    row = pl.BlockSpec((tr, d), lambda i: (i, 0))
    vec = pl.BlockSpec((1, d), lambda i: (0, 0))
    return _pcall(
        kern, name="final_loss", grid=(r // tr,), in_specs=[row, row, vec, row],
        out_specs=[row, vec, pl.BlockSpec((1, LANE), lambda i: (0, 0))],
        out_shape=[jax.ShapeDtypeStruct((r, d), F32), jax.ShapeDtypeStruct((1, d), F32),
                   jax.ShapeDtypeStruct((1, LANE), F32)],
        compiler_params=pltpu.CompilerParams(dimension_semantics=("arbitrary",)),
    )(x, proj, g, target)


def _rope_fwd(x, cs, sn, nh, width, off, name):
    s = x.arr.shape[0]
    tr = _tile(s, 512)

    def kern(x_ref, c_ref, s_ref, o_ref):
        cv, sv = c_ref[...], s_ref[...]
        for h in range(nh):
            b = h * width
            if off:
                o_ref[:, b:b + off] = x_ref[:, b:b + off].astype(BF16)
            xr = x_ref[:, b + off:b + off + LANE]
            o_ref[:, b + off:b + off + LANE] = (xr * cv + pltpu.roll(xr, 32, 1) * sv).astype(BF16)

    tab = pl.BlockSpec((tr, LANE), lambda i: (i, 0))
    return _pcall(
        kern, name=name, grid=(s // tr,),
        in_specs=[pl.BlockSpec((tr, nh * width), lambda i: (i, x.col0)), tab, tab],
        out_specs=pl.BlockSpec((tr, nh * width), lambda i: (i, 0)),
        out_shape=jax.ShapeDtypeStruct((s, nh * width), BF16),
    )(x.arr, cs, sn)


def _rope_grad(d, cv, sv):
    g2 = d * sv
    g2 = g2 + pltpu.roll(g2, 64, 1)
    lane = lax.broadcasted_iota(jnp.int32, d.shape, 1)
    return jnp.where(lane < 64, d * cv + pltpu.roll(g2, 32, 1), 0.0)


def _rope_bwd_q(dq, cs, sn):
    s, w = dq.shape
    tr = _tile(s, 512)
    nh = w // 256

    def kern(d_ref, c_ref, s_ref, o_ref):
        cv, sv = c_ref[...], s_ref[...]
        for h in range(nh):
            b = h * 256
            o_ref[:, b:b + LANE] = d_ref[:, b:b + LANE].astype(BF16)
            o_ref[:, b + LANE:b + 256] = _rope_grad(d_ref[:, b + LANE:b + 256], cv, sv).astype(BF16)

    row = pl.BlockSpec((tr, w), lambda i: (i, 0))
    tab = pl.BlockSpec((tr, LANE), lambda i: (i, 0))
    return _pcall(kern, name="rope_bwd_q", grid=(s // tr,), in_specs=[row, tab, tab], out_specs=row,
                  out_shape=jax.ShapeDtypeStruct((s, w), BF16))(dq, cs, sn)


def _rope_bwd_k(dk_nope, dk_pe, dv, cs, sn):
    s, w = dk_nope.shape
    tr = _tile(s, 512)

    def kern(dk_ref, dp_ref, dv_ref, c_ref, s_ref, okv_ref, okr_ref):
        okv_ref[:, :w] = dk_ref[...].astype(BF16)
        okv_ref[:, w:] = dv_ref[...].astype(BF16)
        okr_ref[...] = _rope_grad(dp_ref[...], c_ref[...], s_ref[...])

    tab = pl.BlockSpec((tr, LANE), lambda i: (i, 0))
    wide = pl.BlockSpec((tr, w), lambda i: (i, 0))
    return _pcall(
        kern, name="rope_bwd_k", grid=(s // tr,), in_specs=[wide, tab, wide, tab, tab],
        out_specs=[pl.BlockSpec((tr, 2 * w), lambda i: (i, 0)), tab],
        out_shape=[jax.ShapeDtypeStruct((s, 2 * w), BF16), jax.ShapeDtypeStruct((s, LANE), F32)],
    )(dk_nope, dk_pe, dv, cs, sn)


class _Attn:
    def __init__(self, mode, s, sk, heads, dk):
        self.mode, self.s, self.sk, self.h, self.dk = mode, s, sk, heads, dk
        self.scale = {"mla": 192 ** -0.5, "mem": 128 ** -0.5}.get(mode, NSA_DK ** -0.5)
        self.tb = min(256, s)
        self.nb = s // self.tb
        self.nsub = 2 if self.nb % 2 == 0 else 1
        self.tq = self.tb * self.nsub
        self.fchains = 1
        self.nq = s // self.tq
        self.qpb = 4 if self.nq % 4 == 0 and mode == "mla" else 2
        self.causal = mode in ("mla", "slc")
        if self.causal:
            self.tk = self.tq
        elif mode == "win":
            self.tk = WIN + self.tb
        else:
            self.tk = sk
        self.tkb = min(512, sk)
        self.ksub = 1
        self.kb = self.tkb // self.ksub
        self.ncmp = s // CMP_STRIDE - 1

    def mask_bias(self, t, n, h, selx, diag):
        m = self.mode
        if m == "mla":
            return (n <= t) if diag else None, None
        if m == "mem":
            return None, None
        slope = jnp.where(h == 0, 0.25, jnp.where(h == 1, 0.0625, jnp.where(h == 2, 0.015625, 0.00390625)))
        slope = slope.astype(F32) * LOG2E
        if m == "cmp":
            mask = (n * CMP_STRIDE + (CMP_LEN - 1) <= t) & (n < self.ncmp)
            pos = n.astype(F32) * float(CMP_STRIDE) + (CMP_LEN - 1) / 2.0
            return mask, slope * pos
        rel = t - n
        if m == "slc":
            return (rel >= 0) if diag else None, slope * n.astype(F32)
        return (rel >= 0) & (rel < WIN), slope * n.astype(F32)


def _scores(cfg, s_raw, t, n, h, selx, diag, lse=None):
    s = s_raw * (cfg.scale * LOG2E)
    mask, key_term = cfg.mask_bias(t, n, h, selx, diag)
    if key_term is not None:
        s = s + key_term
    if selx is not None:
        s = s + selx
    if lse is None:
        if mask is not None:
            s = jnp.where(mask, s, NEG)
        return s, mask
    p = jnp.exp2(jnp.minimum(s - lse, 0.0))
    if mask is not None:
        p = jnp.where(mask, p, 0.0)
    return p, mask


def _block_of_key(k0, tk, keys_on_rows, value=NEG):
    shape = (tk, LANE) if keys_on_rows else (LANE, tk)
    n = lax.broadcasted_iota(jnp.int32, shape, 0 if keys_on_rows else 1) + k0
    j = lax.broadcasted_iota(jnp.int32, shape, 1 if keys_on_rows else 0)
    return jnp.where((n >> 6) == j, value, 0.0).astype(BF16)


def _to_row(col):
    t = col.shape[0]
    return jnp.transpose(jnp.broadcast_to(col, (t, LANE)))[0:1, :]


def _load_keys(k_refs, rows):
    parts = [r[rows, :].astype(BF16) for r in k_refs]
    return parts[0] if len(parts) == 1 else jnp.concatenate(parts, axis=1)


def _attn_fwd(cfg, q, ks, v, sel, name):
    s, tq, tk, nsub = cfg.s, cfg.tq, cfg.tk, cfg.fchains
    tb = tq // nsub
    per = tb // cfg.tb
    has_sel = sel is not None
    nkp = len(ks)
    qpb = cfg.qpb
    assert cfg.causal and tq == tk and qpb % 2 == 0 and cfg.nq % qpb == 0

    def kern(*refs):
        q_ref, k_refs, v_ref = refs[0], refs[1:1 + nkp], refs[1 + nkp]
        sel_ref = refs[2 + nkp] if has_sel else None
        o_ref, lc_ref, lr_ref = refs[2 + nkp + has_sel:5 + nkp + has_sel]
        buf_a, buf_b = refs[-2:]
        h, g = pl.program_id(0), pl.program_id(1)

        def block(b):
            rows = [slice(b * tq + r * tb, b * tq + (r + 1) * tb) for r in range(nsub)]
            qs = [q_ref[p, :].astype(BF16) for p in rows]
            ts = [(qpb * g + b) * tq + r * tb + lax.broadcasted_iota(jnp.int32, (tb, 1), 0) for r in range(nsub)]
            sels = [sel_ref[p, :].astype(BF16) for p in rows] if has_sel else None
            return rows, qs, ts, sels

        def scores_into(buf, blk, c):
            kk = _load_keys(k_refs, pl.ds(pl.multiple_of(c * tk, tk), tk))
            for r in range(nsub):
                buf[r] = _nt(blk[1][r], kk)

        def consume(buf, blk, c, carry, diag):
            _, _, ts, sels = blk
            k0 = pl.multiple_of(c * tk, tk)
            vv = v_ref[pl.ds(k0, tk), :].astype(BF16)
            emat = _block_of_key(k0, tk, False) if has_sel else None
            n = k0 + lax.broadcasted_iota(jnp.int32, (1, tk), 1)
            new = []
            for r in range(nsub):
                m, l, acc = carry[r]
                selx = _nn(sels[r], emat) if has_sel else None
                sc, mask = _scores(cfg, buf[r], ts[r], n, h, selx, diag)
                m_new = jnp.maximum(m, jnp.max(sc, axis=1, keepdims=True))
                alpha = jnp.exp2(m - m_new)
                p = jnp.exp2(sc - m_new)
                if mask is not None:
                    p = jnp.where(mask, p, 0.0)
                l = alpha * l + jnp.sum(p, axis=1, keepdims=True)
                new.append((m_new, l, alpha * acc + _nn(p.astype(BF16), vv)))
            return tuple(new)

        def finish(blk, b, carry):
            for r, (m, l, acc) in enumerate(carry):
                o_ref[blk[0][r], :] = acc / (l + 1e-20)
                lse = m + jnp.log(l + 1e-20) * LOG2E
                lc_ref[0, blk[0][r], :] = lse
                for u in range(per):
                    lr_ref[0, (b * nsub + r) * per + u] = _to_row(lse[u * cfg.tb:(u + 1) * cfg.tb])

        def pairs(blk, first, other):
            def pair(p, cr):
                scores_into(other, blk, 2 * p + 1)
                cr = consume(first, blk, 2 * p, cr, False)
                scores_into(first, blk, 2 * p + 2)
                return consume(other, blk, 2 * p + 1, cr, False)
            return pair

        init = ((jnp.full((tb, 1), NEG, F32), jnp.zeros((tb, 1), F32), jnp.zeros((tb, HEAD_V), F32)),) * nsub
        cur, oth = buf_a, buf_b
        blk = block(0)
        scores_into(cur, blk, 0)
        for b in range(qpb):
            full = qpb * g + b
            carry = lax.fori_loop(0, (qpb // 2) * g + b // 2, pairs(blk, cur, oth), init)
            nxt = block(b + 1) if b + 1 < qpb else None
            if b % 2 == 0:
                if nxt:
                    scores_into(oth, nxt, 0)
                finish(blk, b, consume(cur, blk, full, carry, True))
                cur, oth = oth, cur
            else:
                scores_into(oth, blk, full)
                carry = consume(cur, blk, full - 1, carry, False)
                if nxt:
                    scores_into(cur, nxt, 0)
                finish(blk, b, consume(oth, blk, full, carry, True))
            blk = nxt

    rows_step = qpb * tq
    ins = [pl.BlockSpec((rows_step, q.width), lambda h, g: (g, q.col(h)))]
    ins += [pl.BlockSpec((cfg.sk, p.width), lambda h, g, p=p: (0, p.col(h))) for p in ks]
    ins += [pl.BlockSpec((cfg.sk, HEAD_V), lambda h, g: (0, v.col(h)))]
    args = [q.arr] + [p.arr for p in ks] + [v.arr]
    if has_sel:
        ins.append(pl.BlockSpec((rows_step, LANE), lambda h, g: (g, 0)))
        args.append(sel)
    return _pcall(
        kern, name=name, grid=(cfg.h, cfg.nq // qpb), in_specs=ins,
        out_specs=[pl.BlockSpec((rows_step, HEAD_V), lambda h, g: (g, h)),
                   pl.BlockSpec((1, rows_step, 1), lambda h, g: (h, g, 0)),
                   pl.BlockSpec((1, rows_step // cfg.tb, 1, cfg.tb), lambda h, g: (h, g, 0, 0))],
        out_shape=[jax.ShapeDtypeStruct((s, cfg.h * HEAD_V), F32),
                   jax.ShapeDtypeStruct((cfg.h, s, 1), F32),
                   jax.ShapeDtypeStruct((cfg.h, cfg.nb, 1, cfg.tb), F32)],
        scratch_shapes=[pltpu.VMEM((nsub, tb, tk), F32)] * 2,
        compiler_params=pltpu.CompilerParams(dimension_semantics=("parallel", "parallel")),
    )(*args)


def _attn_dq(cfg, q, ks, v, sel, o, lse, do, dq_in, name):
    s, tq, tk, dk, nsub = cfg.s, cfg.tq, cfg.tk, cfg.dk, cfg.fchains
    tb = tq // nsub
    per = tb // cfg.tb
    has_sel = sel is not None
    has_in = dq_in is not None
    nkp = len(ks)

    def kern(*refs):
        refs = list(refs)
        q_ref, k_refs, v_ref = refs[0], refs[1:1 + nkp], refs[1 + nkp]
        p0 = 2 + nkp
        sel_ref = refs[p0] if has_sel else None
        p0 += has_sel
        o_ref, l_ref, do_ref = refs[p0:p0 + 3]
        p0 += 3
        in_ref = refs[p0] if has_in else None
        p0 += has_in
        dq_ref, dr_ref = refs[p0:p0 + 2]
        sa, pa, sb, pb = refs[-4:]
        h, g = pl.program_id(0), pl.program_id(1)

        def block(b):
            rows = [slice(b * tq + r * tb, b * tq + (r + 1) * tb) for r in range(nsub)]
            qs = [q_ref[p, :].astype(BF16) for p in rows]
            ts = [(qpb * g + b) * tq + r * tb + lax.broadcasted_iota(jnp.int32, (tb, 1), 0) for r in range(nsub)]
            sels = [sel_ref[p, :].astype(BF16) for p in rows] if has_sel else None
            dvecs, dobs, lses = [], [], []
            for r, p in enumerate(rows):
                dov = do_ref[p, :]
                dvec = jnp.sum(dov * o_ref[p, :], axis=1, keepdims=True)
                for u in range(per):
                    dr_ref[0, (b * nsub + r) * per + u] = _to_row(dvec[u * cfg.tb:(u + 1) * cfg.tb])
                dvecs.append(dvec)
                dobs.append(dov.astype(BF16))
                lses.append(l_ref[0, p, :])
            return rows, qs, ts, sels, dvecs, dobs, lses

        def products_into(sbuf, pbuf, blk, c):
            rows = pl.ds(pl.multiple_of(c * tk, tk), tk)
            kk, vv = _load_keys(k_refs, rows), v_ref[rows, :].astype(BF16)
            for r in range(nsub):
                sbuf[r] = _nt(blk[1][r], kk)
                pbuf[r] = _nt(blk[5][r], vv)

        def consume(sbuf, pbuf, blk, c, accs, diag):
            _, _, ts, sels, dvecs, _, lses = blk
            k0 = pl.multiple_of(c * tk, tk)
            kk = _load_keys(k_refs, pl.ds(k0, tk))
            emat = _block_of_key(k0, tk, False) if has_sel else None
            n = k0 + lax.broadcasted_iota(jnp.int32, (1, tk), 1)
            new = []
            for r in range(nsub):
                selx = _nn(sels[r], emat) if has_sel else None
                p, _ = _scores(cfg, sbuf[r], ts[r], n, h, selx, diag, lses[r])
                ds = p * (pbuf[r] - dvecs[r])
                new.append(accs[r] + _nn(ds.astype(BF16), kk))
            return tuple(new)

        def finish(blk, accs):
            for r, p in enumerate(blk[0]):
                dq_ref[p, :] = accs[r] * cfg.scale + in_ref[p, :] if has_in else accs[r] * cfg.scale

        def pairs(blk, first, other):
            def pair(p, ac):
                products_into(*other, blk, 2 * p + 1)
                ac = consume(*first, blk, 2 * p, ac, False)
                products_into(*first, blk, 2 * p + 2)
                return consume(*other, blk, 2 * p + 1, ac, False)
            return pair

        zero = (jnp.zeros((tb, dk), F32),) * nsub
        cur, oth = (sa, pa), (sb, pb)
        blk = block(0)
        products_into(*cur, blk, 0)
        for b in range(qpb):
            full = qpb * g + b
            accs = lax.fori_loop(0, (qpb // 2) * g + b // 2, pairs(blk, cur, oth), zero)
            nxt = block(b + 1) if b + 1 < qpb else None
            if b % 2 == 0:
                if nxt:
                    products_into(*oth, nxt, 0)
                finish(blk, consume(*cur, blk, full, accs, True))
                cur, oth = oth, cur
            else:
                products_into(*oth, blk, full)
                accs = consume(*cur, blk, full - 1, accs, False)
                if nxt:
                    products_into(*cur, nxt, 0)
                finish(blk, consume(*oth, blk, full, accs, True))
            blk = nxt

    qpb = cfg.qpb
    assert cfg.causal and tq == tk and qpb % 2 == 0 and cfg.nq % qpb == 0
    rows_step = qpb * tq
    qs = pl.BlockSpec((rows_step, dk), lambda h, g: (g, h))
    ins = [pl.BlockSpec((rows_step, q.width), lambda h, g: (g, q.col(h)))]
    ins += [pl.BlockSpec((cfg.sk, p.width), lambda h, g, p=p: (0, p.col(h))) for p in ks]
    ins += [pl.BlockSpec((cfg.sk, HEAD_V), lambda h, g: (0, v.col(h)))]
    args = [q.arr] + [p.arr for p in ks] + [v.arr]
    if has_sel:
        ins.append(pl.BlockSpec((rows_step, LANE), lambda h, g: (g, 0)))
        args.append(sel)
    ins += [pl.BlockSpec((rows_step, HEAD_V), lambda h, g: (g, o.col(h))),
            pl.BlockSpec((1, rows_step, 1), lambda h, g: (h, g, 0)),
            pl.BlockSpec((rows_step, HEAD_V), lambda h, g: (g, do.col(h)))]
    args += [o.arr, lse, do.arr]
    if has_in:
        ins.append(qs)
        args.append(dq_in)
    return _pcall(
        kern, name=name, grid=(cfg.h, cfg.nq // qpb), in_specs=ins,
        out_specs=[qs, pl.BlockSpec((1, rows_step // cfg.tb, 1, cfg.tb), lambda h, g: (h, g, 0, 0))],
        out_shape=[jax.ShapeDtypeStruct((s, cfg.h * dk), F32),
                   jax.ShapeDtypeStruct((cfg.h, cfg.nb, 1, cfg.tb), F32)],
        scratch_shapes=[pltpu.VMEM((nsub, tb, tk), F32)] * 4,
        compiler_params=pltpu.CompilerParams(dimension_semantics=("parallel", "parallel")),
    )(*args)


def _attn_dkv(cfg, q, ks, v, selt, lse_r, d_r, do, name):
    s, tq, tkb, dk, kb, ksub = cfg.s, cfg.tb, cfg.tkb, cfg.dk, cfg.kb, cfg.ksub
    nq = cfg.nb
    has_sel = selt is not None
    nkp = len(ks)
    outs = list(ks) + [v]

    def kern(*refs):
        k_refs, v_ref = refs[:nkp], refs[nkp]
        q_ref, do_ref, lr_ref, dr_ref = refs[nkp + 1:nkp + 5]
        st_ref = refs[nkp + 5] if has_sel else None
        out_refs = refs[nkp + 5 + has_sel:2 * nkp + 6 + has_sel]
        sa, pa, sb, pb = refs[-4:]
        j, h = pl.program_id(0), pl.program_id(1)
        k0 = j * tkb
        part = [slice(u * kb, (u + 1) * kb) for u in range(ksub)]
        kks = [_load_keys(k_refs, p) for p in part]
        vvs = [v_ref[p, :].astype(BF16) for p in part]
        ns = [k0 + u * kb + lax.broadcasted_iota(jnp.int32, (kb, 1), 0) for u in range(ksub)]
        emats = [_block_of_key(k0 + u * kb, kb, True) for u in range(ksub)] if has_sel else None

        def load_q(i):
            rows = pl.ds(pl.multiple_of(i * tq, tq), tq)
            return q_ref[rows, :].astype(BF16), do_ref[rows, :].astype(BF16)

        def products_into(sbuf, pbuf, i):
            qi, doi = load_q(i)
            for u in range(ksub):
                sbuf[u] = _nt(kks[u], qi)
                pbuf[u] = _nt(vvs[u], doi)

        def consume(sbuf, pbuf, i, carry):
            qi, doi = load_q(i)
            t = i * tq + lax.broadcasted_iota(jnp.int32, (1, tq), 1)
            selt_i = st_ref[i].astype(BF16) if has_sel else None
            new = []
            for u in range(ksub):
                dk_acc, dv_acc = carry[u]
                selx = _nn(emats[u], selt_i) if has_sel else None
                pt, _ = _scores(cfg, sbuf[u], t, ns[u], h, selx, True, lr_ref[0, i])
                dv_acc = dv_acc + _nn(pt.astype(BF16), doi)
                dst = pt * (pbuf[u] - dr_ref[0, i])
                new.append((dk_acc + _nn(dst.astype(BF16), qi), dv_acc))
            return tuple(new)

        if cfg.causal:
            first, count = k0 // tq, nq - k0 // tq
        elif cfg.mode == "win":
            first = k0 // tq
            count = jnp.minimum((k0 + tkb + WIN - 2) // tq + 1, nq) - first
        else:
            first, count = 0, nq

        def pair(p, cr):
            i0 = first + 2 * p
            products_into(sb, pb, i0 + 1)
            cr = consume(sa, pa, i0, cr)
            products_into(sa, pa, i0 + 2)
            return consume(sb, pb, i0 + 1, cr)

        carry = ((jnp.zeros((kb, dk), F32), jnp.zeros((kb, HEAD_V), F32)),) * ksub
        products_into(sa, pa, first)
        carry = lax.fori_loop(0, count // 2 - 1, pair, carry)
        last = first + count - 2
        products_into(sb, pb, last + 1)
        carry = consume(sa, pa, last, carry)
        carry = consume(sb, pb, last + 1, carry)
        for u, (dk_acc, dv_acc) in enumerate(carry):
            vals, off = [], 0
            for p in ks:
                vals.append(dk_acc[:, off:off + p.width] * cfg.scale)
                off += p.width
            vals.append(dv_acc)
            for src, ref, val in zip(outs, out_refs, vals):
                if src.per_head:
                    ref[part[u], :] = val
                else:
                    @pl.when(h == 0)
                    def _(ref=ref, val=val, u=u):
                        ref[part[u], :] = val

                    @pl.when(h > 0)
                    def _(ref=ref, val=val, u=u):
                        ref[part[u], :] += val

    rowv = pl.BlockSpec((1, nq, 1, tq), lambda j, h: (h, 0, 0, 0))
    ins = [pl.BlockSpec((tkb, p.width), lambda j, h, p=p: (j, p.col(h))) for p in ks]
    ins += [pl.BlockSpec((tkb, HEAD_V), lambda j, h: (j, v.col(h))),
            pl.BlockSpec((s, q.width), lambda j, h: (0, q.col(h))),
            pl.BlockSpec((s, HEAD_V), lambda j, h: (0, do.col(h))), rowv, rowv]
    args = [p.arr for p in ks] + [v.arr, q.arr, do.arr, lse_r, d_r]
    if has_sel:
        ins.append(pl.BlockSpec((nq, LANE, tq), lambda j, h: (0, 0, 0)))
        args.append(selt)
    out_specs = [pl.BlockSpec((tkb, p.width), lambda j, h, p=p: (j, h if p.per_head else 0)) for p in outs]
    out_shape = [jax.ShapeDtypeStruct((cfg.sk, (cfg.h if p.per_head else 1) * p.width), F32) for p in outs]
    assert nq % 2 == 0 and (cfg.mode in ("cmp", "mem") or tkb % (2 * tq) == 0), (nq, tkb, tq)
    return _pcall(
        kern, name=name, grid=(cfg.sk // tkb, cfg.h), in_specs=ins, out_specs=out_specs, out_shape=out_shape,
        scratch_shapes=[pltpu.VMEM((ksub, kb, tq), F32)] * 4,
        compiler_params=pltpu.CompilerParams(dimension_semantics=("parallel", "arbitrary")),
    )(*args)


def _attn_dkv_flat(cfg, q, ks, v, selt, lse_r, d_r, do, name):
    s, tq, tkb, dk, kb, ksub = cfg.s, cfg.tb, cfg.tkb, cfg.dk, cfg.kb, cfg.ksub
    nq = cfg.nb
    has_sel = selt is not None
    nkp = len(ks)
    outs = list(ks) + [v]
    assert nq % 2 == 0 and tkb % (2 * tq) == 0, (nq, tkb, tq)
    steps = []
    for j in range(cfg.sk // tkb):
        first = j * tkb // tq
        stop = nq if cfg.causal else min((j * tkb + tkb + WIN - 2) // tq + 1, nq)
        steps += [(j, i0) for i0 in range(first, stop, 2)]
    n_pairs = len(steps)
    steps.append(steps[-1])
    tab_j = jnp.asarray(np.array([p[0] for p in steps], np.int32))
    tab_i = jnp.asarray(np.array([p[1] for p in steps], np.int32))

    def kern(tj_ref, ti_ref, *refs):
        k_refs, v_ref = refs[:nkp], refs[nkp]
        q_ref, do_ref, lr_ref, dr_ref = refs[nkp + 1:nkp + 5]
        st_ref = refs[nkp + 5] if has_sel else None
        out_refs = refs[nkp + 5 + has_sel:2 * nkp + 6 + has_sel]
        sa, pa, sb, pb = refs[-4:]
        h = pl.program_id(0)
        for src, ref in zip(outs, out_refs):
            if src.per_head:
                ref[...] = jnp.zeros_like(ref)
            else:
                @pl.when(h == 0)
                def _(ref=ref):
                    ref[...] = jnp.zeros_like(ref)

        def key_rows(j, u):
            return pl.ds(pl.multiple_of(j * tkb + u * kb, kb), kb)

        def load_q(i):
            rows = pl.ds(pl.multiple_of(i * tq, tq), tq)
            return q_ref[rows, :].astype(BF16), do_ref[rows, :].astype(BF16)

        def products_into(sbuf, pbuf, j, i):
            qi, doi = load_q(i)
            for u in range(ksub):
                rows = key_rows(j, u)
                sbuf[u] = _nt(_load_keys(k_refs, rows), qi)
                pbuf[u] = _nt(v_ref[rows, :].astype(BF16), doi)

        def consume(sbuf, pbuf, j, i):
            qi, doi = load_q(i)
            t = i * tq + lax.broadcasted_iota(jnp.int32, (1, tq), 1)
            selt_i = st_ref[i].astype(BF16) if has_sel else None
            res = []
            for u in range(ksub):
                k0 = j * tkb + u * kb
                n = k0 + lax.broadcasted_iota(jnp.int32, (kb, 1), 0)
                selx = _nn(_block_of_key(k0, kb, True), selt_i) if has_sel else None
                pt, _ = _scores(cfg, sbuf[u], t, n, h, selx, True, lr_ref[0, i])
                dst = pt * (pbuf[u] - dr_ref[0, i])
                res.append((_nn(dst.astype(BF16), qi), _nn(pt.astype(BF16), doi)))
            return res

        def pair(p, carry):
            j, i0 = tj_ref[p], ti_ref[p]
            products_into(sb, pb, j, i0 + 1)
            ca = consume(sa, pa, j, i0)
            products_into(sa, pa, tj_ref[p + 1], ti_ref[p + 1])
            cb = consume(sb, pb, j, i0 + 1)
            for u in range(ksub):
                rows = key_rows(j, u)
                dk_c = (ca[u][0] + cb[u][0]) * cfg.scale
                off = 0
                for src, ref in zip(ks, out_refs):
                    ref[rows, :] += dk_c[:, off:off + src.width]
                    off += src.width
                out_refs[nkp][rows, :] += ca[u][1] + cb[u][1]
            return carry

        products_into(sa, pa, tj_ref[0], ti_ref[0])
        lax.fori_loop(0, n_pairs, pair, 0)

    rowv = pl.BlockSpec((1, nq, 1, tq), lambda h, tj, ti: (h, 0, 0, 0))
    ins = [pl.BlockSpec((cfg.sk, p.width), lambda h, tj, ti, p=p: (0, p.col(h))) for p in ks]
    ins += [pl.BlockSpec((cfg.sk, HEAD_V), lambda h, tj, ti: (0, v.col(h))),
            pl.BlockSpec((s, q.width), lambda h, tj, ti: (0, q.col(h))),
            pl.BlockSpec((s, HEAD_V), lambda h, tj, ti: (0, do.col(h))), rowv, rowv]
    args = [p.arr for p in ks] + [v.arr, q.arr, do.arr, lse_r, d_r]
    if has_sel:
        ins.append(pl.BlockSpec((nq, LANE, tq), lambda h, tj, ti: (0, 0, 0)))
        args.append(selt)
    out_specs = [pl.BlockSpec((cfg.sk, p.width), lambda h, tj, ti, p=p: (0, h if p.per_head else 0))
                 for p in outs]
    out_shape = [jax.ShapeDtypeStruct((cfg.sk, (cfg.h if p.per_head else 1) * p.width), F32) for p in outs]
    grid_spec = pltpu.PrefetchScalarGridSpec(
        num_scalar_prefetch=2, grid=(cfg.h,), in_specs=ins, out_specs=out_specs,
        scratch_shapes=[pltpu.VMEM((ksub, kb, tq), F32)] * 4)
    return _pcall(kern, name=name, grid_spec=grid_spec, out_shape=out_shape,
                  compiler_params=pltpu.CompilerParams(dimension_semantics=("arbitrary",)))(tab_j, tab_i, *args)


def _all_heads(cfg, src, rows, key=False):
    if src.per_head:
        assert src.col0 % cfg.h == 0
        width, col = cfg.h * src.width, src.col0 // cfg.h
    else:
        width, col = src.width, src.col0
    return pl.BlockSpec((rows, width), (lambda i: (0, col)) if key else (lambda i: (i, col)))


def _head_cols(src, hh):
    return slice(hh * src.width, (hh + 1) * src.width) if src.per_head else slice(None)


def _key_window(cfg, i, r):
    if cfg.mode == "win":
        return pl.ds(pl.multiple_of(jnp.maximum(i * cfg.tq + r * cfg.tb - WIN, 0), cfg.tb), cfg.tk)
    return pl.ds(0, cfg.tk)


def _attn_fwd_small(cfg, q, ks, v, name, overlap=None):
    s, tq, tk, tb, nsub, nh = cfg.s, cfg.tq, cfg.tk, cfg.tb, cfg.nsub, cfg.h
    nkp = len(ks)
    select = overlap is not None
    n_s = s // SLC_LEN
    top_n = min(SLC_TOPN, n_s)

    def kern(*refs):
        q_ref, k_refs, v_ref = refs[0], refs[1:1 + nkp], refs[1 + nkp]
        ov_ref = refs[2 + nkp] if select else None
        o_ref, lc_ref, lr_ref = refs[2 + nkp + select:5 + nkp + select]
        i = pl.program_id(0)
        imps = [jnp.zeros((tb, LANE), F32)] * nsub
        for r in range(nsub):
            rows = slice(r * tb, (r + 1) * tb)
            t = i * tq + r * tb + lax.broadcasted_iota(jnp.int32, (tb, 1), 0)
            win = _key_window(cfg, i, r)
            n = win.start + lax.broadcasted_iota(jnp.int32, (1, tk), 1)
            for hh in range(nh):
                qv = q_ref[rows, hh * cfg.dk:(hh + 1) * cfg.dk].astype(BF16)
                kk = _load_keys([kr.at[:, _head_cols(p, hh)] for kr, p in zip(k_refs, ks)], win)
                vv = v_ref[win, _head_cols(v, hh)].astype(BF16)
                sc, mask = _scores(cfg, _nt(qv, kk), t, n, hh, None, True)
                m = jnp.max(sc, axis=1, keepdims=True)
                e = jnp.exp2(sc - m)
                if mask is not None:
                    e = jnp.where(mask, e, 0.0)
                l = jnp.sum(e, axis=1, keepdims=True)
                o_ref[rows, hh * HEAD_V:(hh + 1) * HEAD_V] = _nn(e.astype(BF16), vv) / (l + 1e-20)
                lse = m + jnp.log(l + 1e-20) * LOG2E
                lc_ref[hh, rows, :] = lse
                lr_ref[hh, r] = _to_row(lse)
                if select:
                    imps[r] = imps[r] + _nn((e / (l + 1e-20)).astype(BF16), ov_ref[...])
        if select:
            sel_ref, selt_ref, imp_t = refs[5 + nkp + select:8 + nkp + select]
            for r in range(nsub):
                t = i * tq + r * tb + lax.broadcasted_iota(jnp.int32, (tb, 1), 0)
                j = lax.broadcasted_iota(jnp.int32, (tb, LANE), 1)
                cur = t >> 6
                imp = jnp.where((j == 0) | (j == cur) | (j == cur - 1), 1e9, imps[r])
                imp = jnp.where(j > cur, -1e9, imp)
                imp_t[r] = jnp.transpose(imp)
                mine = imp_t[r, 0:n_s, :]
                jrow = lax.broadcasted_iota(jnp.int32, (n_s, tb), 0)

                def count(k, rank):
                    other = imp_t[r, pl.ds(k, 1), :]
                    ahead = (other > mine) | ((other == mine) & (k < jrow))
                    return rank + jnp.where(ahead, 1.0, 0.0)

                rank = lax.fori_loop(0, n_s, count, jnp.zeros((n_s, tb), F32))
                cur_t = (i * tq + r * tb + lax.broadcasted_iota(jnp.int32, (1, tb), 1)) >> 6
                rejected = jnp.where((rank < top_n) & (jrow <= cur_t), 0.0, 1.0)
                if n_s < LANE:
                    rejected = jnp.concatenate([rejected, jnp.ones((LANE - n_s, tb), F32)], axis=0)
                selt_ref[r] = rejected
                sel_ref[r * tb:(r + 1) * tb, :] = jnp.transpose(rejected)

    ins = [_all_heads(cfg, q, tq)] + [_all_heads(cfg, p, cfg.sk, True) for p in ks]
    ins += [_all_heads(cfg, v, cfg.sk, True)]
    args = [q.arr] + [p.arr for p in ks] + [v.arr]
    out_specs = [pl.BlockSpec((tq, nh * HEAD_V), lambda i: (i, 0)),
                 pl.BlockSpec((nh, tq, 1), lambda i: (0, i, 0)),
                 pl.BlockSpec((nh, nsub, 1, tb), lambda i: (0, i, 0, 0))]
    out_shape = [jax.ShapeDtypeStruct((s, nh * HEAD_V), F32), jax.ShapeDtypeStruct((nh, s, 1), F32),
                 jax.ShapeDtypeStruct((nh, cfg.nb, 1, tb), F32)]
    scratch = []
    if select:
        ins.append(pl.BlockSpec((cfg.sk, LANE), lambda i: (0, 0)))
        args.append(overlap)
        out_specs += [pl.BlockSpec((tq, LANE), lambda i: (i, 0)), pl.BlockSpec((nsub, LANE, tb), lambda i: (i, 0, 0))]
        out_shape += [jax.ShapeDtypeStruct((s, LANE), F32), jax.ShapeDtypeStruct((cfg.nb, LANE, tb), F32)]
        scratch = [pltpu.VMEM((nsub, LANE, tb), F32)]
    return _pcall(kern, name=name, grid=(cfg.nq,), in_specs=ins, out_specs=out_specs, out_shape=out_shape,
                  scratch_shapes=scratch,
                  compiler_params=pltpu.CompilerParams(dimension_semantics=("parallel",)))(*args)


def _attn_dq_small(cfg, q, ks, v, o, lse, do, dq_in, name):
    s, tq, tk, tb, nsub, nh, dk = cfg.s, cfg.tq, cfg.tk, cfg.tb, cfg.nsub, cfg.h, cfg.dk
    nkp = len(ks)
    has_in = dq_in is not None

    def kern(*refs):
        q_ref, k_refs, v_ref = refs[0], refs[1:1 + nkp], refs[1 + nkp]
        o_ref, l_ref, do_ref = refs[2 + nkp:5 + nkp]
        in_ref = refs[5 + nkp] if has_in else None
        dq_ref, dr_ref = refs[5 + nkp + has_in:7 + nkp + has_in]
        i = pl.program_id(0)
        for r in range(nsub):
            rows = slice(r * tb, (r + 1) * tb)
            t = i * tq + r * tb + lax.broadcasted_iota(jnp.int32, (tb, 1), 0)
            win = _key_window(cfg, i, r)
            n = win.start + lax.broadcasted_iota(jnp.int32, (1, tk), 1)
            for hh in range(nh):
                vcols = slice(hh * HEAD_V, (hh + 1) * HEAD_V)
                qcols = slice(hh * dk, (hh + 1) * dk)
                qv = q_ref[rows, qcols].astype(BF16)
                kk = _load_keys([kr.at[:, _head_cols(p, hh)] for kr, p in zip(k_refs, ks)], win)
                vv = v_ref[win, _head_cols(v, hh)].astype(BF16)
                dov = do_ref[rows, vcols]
                dvec = jnp.sum(dov * o_ref[rows, vcols], axis=1, keepdims=True)
                dr_ref[hh, r] = _to_row(dvec)
                p, _ = _scores(cfg, _nt(qv, kk), t, n, hh, None, True, l_ref[hh, rows, :])
                ds = p * (_nt(dov.astype(BF16), vv) - dvec)
                dq = _nn(ds.astype(BF16), kk) * cfg.scale
                dq_ref[rows, qcols] = dq + in_ref[rows, qcols] if has_in else dq

    qs = pl.BlockSpec((tq, nh * dk), lambda i: (i, 0))
    ins = [_all_heads(cfg, q, tq)] + [_all_heads(cfg, p, cfg.sk, True) for p in ks]
    ins += [_all_heads(cfg, v, cfg.sk, True)]
    ins += [_all_heads(cfg, o, tq), pl.BlockSpec((nh, tq, 1), lambda i: (0, i, 0)), _all_heads(cfg, do, tq)]
    args = [q.arr] + [p.arr for p in ks] + [v.arr, o.arr, lse, do.arr]
    if has_in:
        ins.append(qs)
        args.append(dq_in)
    return _pcall(
        kern, name=name, grid=(cfg.nq,), in_specs=ins,
        out_specs=[qs, pl.BlockSpec((nh, nsub, 1, tb), lambda i: (0, i, 0, 0))],
        out_shape=[jax.ShapeDtypeStruct((s, nh * dk), F32), jax.ShapeDtypeStruct((nh, cfg.nb, 1, tb), F32)],
        compiler_params=pltpu.CompilerParams(dimension_semantics=("parallel",)))(*args)


def _attn_bwd(cfg, q, ks, v, sel, selt, o, lse, lse_r, do, dq_in, name):
    if cfg.causal:
        dq, d_r = _attn_dq(cfg, q, ks, v, sel, o, lse, do, dq_in, name + "_dq")
    else:
        dq, d_r = _attn_dq_small(cfg, q, ks, v, o, lse, do, dq_in, name + "_dq")
    dkv = _attn_dkv_flat if cfg.causal or cfg.mode == "win" else _attn_dkv
    res = dkv(cfg, q, ks, v, selt, lse_r, d_r, do, name + "_dkv")
    return dq, res[:-1], res[-1]


def _silu_grad(pre):
    sg = _sigmoid(pre)
    return sg * (1.0 + pre * (1.0 - sg))


def _compress_fwd(a_lo, a_hi, pe_lo, pe_hi, w1_lo, w1_hi, w2, name):
    n, dp = a_lo.shape[0], w2.shape[1]

    def kern(alo, ahi, plo, phi, w1l, w1h, w2r, out_ref, pre_ref):
        xl = (alo[...] + plo[...]).astype(BF16)
        xh = (ahi[...] + phi[...]).astype(BF16)
        pre = _nn(xl, w1l[...]) + _nn(xh, w1h[...])
        act = pre * _sigmoid(pre)
        out_ref[...] = _nn(act.astype(BF16), w2r[...]).astype(BF16)
        pre_ref[...] = pre

    return _pcall(kern, name=name,
                  out_shape=[jax.ShapeDtypeStruct((n, dp), BF16), jax.ShapeDtypeStruct((n, dp), F32)],
                  )(a_lo, a_hi, pe_lo, pe_hi, w1_lo, w1_hi, w2)


def _compress_bwd(a_lo, a_hi, pe_lo, pe_hi, w1_lo, w1_hi, w2, pre, pre_sh, dout, dout_sh, name):
    n, ln = a_lo.shape
    dp = w2.shape[1]

    def kern(alo, ahi, plo, phi, w1l, w1h, w2r, pre_ref, presh_ref, do_ref, dosh_ref,
             da_ref, dpl_ref, dph_ref, dw1l_ref, dw1h_ref, dw2_ref):
        prev = pre_ref[...]
        act = prev * _sigmoid(prev)
        dob = do_ref[...].astype(BF16)
        w2v = w2r[...]
        dpre = (_nt(dob, w2v) * _silu_grad(prev)).astype(BF16)
        dpre_sh = (_nt(dosh_ref[...].astype(BF16), w2v) * _silu_grad(presh_ref[...])).astype(BF16)
        dw2_ref[...] = _nn(act.T.astype(BF16), dob)
        xl = alo[...] + plo[...]
        xh = ahi[...] + phi[...]
        dw1l_ref[...] = _nn(xl.T.astype(BF16), dpre)
        dw1h_ref[...] = _nn(xh.T.astype(BF16), dpre)
        dal = _nt(dpre, w1l[...])
        dah_sh = _nt(dpre_sh, w1h[...])
        da_ref[...] = dal + dah_sh
        dpl_ref[...] = jnp.sum(dal, axis=0, keepdims=True)
        dph_ref[...] = jnp.sum(dah_sh, axis=0, keepdims=True)

    return _pcall(
        kern, name=name,
        out_shape=[jax.ShapeDtypeStruct((n, ln), F32), jax.ShapeDtypeStruct((1, ln), F32),
                   jax.ShapeDtypeStruct((1, ln), F32), jax.ShapeDtypeStruct((ln, dp), F32),
                   jax.ShapeDtypeStruct((ln, dp), F32), jax.ShapeDtypeStruct((dp, dp), F32)],
    )(a_lo, a_hi, pe_lo, pe_hi, w1_lo, w1_hi, w2, pre, pre_sh, dout, dout_sh)


def _nsa_combine(o_cmp, o_slc, o_win, gl):
    s, w = o_cmp.shape
    tr = _tile(s, 512)

    def kern(a_ref, b_ref, c_ref, g_ref, o_ref):
        g = _sigmoid(g_ref[...])
        for h in range(NSA_HEADS):
            cs = slice(h * HEAD_V, (h + 1) * HEAD_V)
            o_ref[:, cs] = (g[:, 3 * h:3 * h + 1] * a_ref[:, cs] + g[:, 3 * h + 1:3 * h + 2] * b_ref[:, cs]
                            + g[:, 3 * h + 2:3 * h + 3] * c_ref[:, cs])

    row = pl.BlockSpec((tr, w), lambda i: (i, 0))
    return _pcall(kern, name="nsa_combine", grid=(s // tr,),
                  in_specs=[row, row, row, pl.BlockSpec((tr, LANE), lambda i: (i, gl.col0))], out_specs=row,
                  out_shape=jax.ShapeDtypeStruct((s, w), F32))(o_cmp, o_slc, o_win, gl.arr)


def _nsa_combine_bwd(do_cat, o_cmp, o_slc, o_win, gl):
    s, w = o_cmp.shape
    tr = _tile(s, 512)

    def kern(d_ref, a_ref, b_ref, c_ref, g_ref, da_ref, db_ref, dc_ref, dg_ref):
        g = _sigmoid(g_ref[...])
        lane = lax.broadcasted_iota(jnp.int32, (tr, LANE), 1)
        dgl = jnp.zeros((tr, LANE), F32)
        for h in range(NSA_HEADS):
            cs = slice(h * HEAD_V, (h + 1) * HEAD_V)
            dv = d_ref[:, cs]
            for b, (src, dst) in enumerate(((a_ref, da_ref), (b_ref, db_ref), (c_ref, dc_ref))):
                gate = g[:, 3 * h + b:3 * h + b + 1]
                dst[:, cs] = gate * dv
                dgate = jnp.sum(dv * src[:, cs], axis=1, keepdims=True)
                dgl = jnp.where(lane == 3 * h + b, dgate * gate * (1.0 - gate), dgl)
        dg_ref[...] = dgl

    row = pl.BlockSpec((tr, w), lambda i: (i, 0))
    tab = pl.BlockSpec((tr, LANE), lambda i: (i, 0))
    return _pcall(kern, name="nsa_combine_bwd", grid=(s // tr,),
                  in_specs=[pl.BlockSpec((tr, w), lambda i: (i, 2)), row, row, row,
                            pl.BlockSpec((tr, LANE), lambda i: (i, gl.col0))],
                  out_specs=[row, row, row, tab],
                  out_shape=[jax.ShapeDtypeStruct((s, w), F32)] * 3 + [jax.ShapeDtypeStruct((s, LANE), F32)],
                  )(do_cat, o_cmp, o_slc, o_win, gl.arr)


def _gate_fwd(o_mla, o_nsa, o_mem, hp):
    s = o_mla.shape[0]
    tr = _tile(s, 256)

    def kern(a_ref, b_ref, c_ref, z_ref, u_ref):
        z = z_ref[...]
        sz = z * _sigmoid(z)
        u_ref[:, 0:1024] = (a_ref[...] * sz[:, 0:1024]).astype(BF16)
        u_ref[:, 1024:1536] = (b_ref[...] * sz[:, 1024:1536]).astype(BF16)
        u_ref[:, 1536:2048] = (c_ref[...] * sz[:, 1536:2048]).astype(BF16)

    return _pcall(
        kern, name="gate_fwd", grid=(s // tr,),
        in_specs=[pl.BlockSpec((tr, 1024), lambda i: (i, 0)), pl.BlockSpec((tr, 512), lambda i: (i, 0)),
                  pl.BlockSpec((tr, 512), lambda i: (i, 0)), pl.BlockSpec((tr, 2048), lambda i: (i, 2))],
        out_specs=pl.BlockSpec((tr, 2048), lambda i: (i, 0)),
        out_shape=jax.ShapeDtypeStruct((s, 2048), BF16))(o_mla, o_nsa, o_mem, hp)


def _gate_bwd(du, o_mla, o_nsa, o_mem, hp):
    s = du.shape[0]
    tr = _tile(s, 256)

    def kern(d_ref, a_ref, b_ref, c_ref, z_ref, do_ref, dz_ref):
        z = z_ref[...]
        sg = _sigmoid(z)
        sz = z * sg
        dsz = sg * (1.0 + z * (1.0 - sg))
        d = d_ref[...]
        do_ref[...] = d * sz
        dz_ref[:, 0:1024] = (d[:, 0:1024] * a_ref[...] * dsz[:, 0:1024]).astype(BF16)
        dz_ref[:, 1024:1536] = (d[:, 1024:1536] * b_ref[...] * dsz[:, 1024:1536]).astype(BF16)
        dz_ref[:, 1536:2048] = (d[:, 1536:2048] * c_ref[...] * dsz[:, 1536:2048]).astype(BF16)

    wide = pl.BlockSpec((tr, 2048), lambda i: (i, 0))
    return _pcall(
        kern, name="gate_bwd", grid=(s // tr,),
        in_specs=[wide, pl.BlockSpec((tr, 1024), lambda i: (i, 0)), pl.BlockSpec((tr, 512), lambda i: (i, 0)),
                  pl.BlockSpec((tr, 512), lambda i: (i, 0)), pl.BlockSpec((tr, 2048), lambda i: (i, 2))],
        out_specs=[wide, wide],
        out_shape=[jax.ShapeDtypeStruct((s, 2048), F32), jax.ShapeDtypeStruct((s, 2048), BF16)],
    )(du, o_mla, o_nsa, o_mem, hp)


def _tile2d(rows, cols, arrays):
    if rows % 16 == 0:
        return _row_tile(rows, cols * arrays), cols
    want = max(LANE, BLOCK_BYTES // (rows * 4 * arrays) // LANE * LANE)
    tc = LANE
    for t in range(LANE, cols + 1, LANE):
        if cols % t == 0 and t <= want:
            tc = t
    return rows, tc


def _sum_slots(buf, name):
    n, rows, cols = buf.shape
    tr, tc = _tile2d(rows, cols, n)

    def kern(b_ref, o_ref):
        acc = b_ref[0].astype(F32)
        for i in range(1, n):
            acc = acc + b_ref[i].astype(F32)
        o_ref[...] = acc

    return _pcall(kern, name=name, grid=(rows // tr, cols // tc),
                  in_specs=[pl.BlockSpec((n, tr, tc), lambda i, j: (0, i, j))],
                  out_specs=pl.BlockSpec((tr, tc), lambda i, j: (i, j)),
                  out_shape=jax.ShapeDtypeStruct((rows, cols), F32))(buf)


def _chip_sum(buf, core, axis, name):
    n, rows, cols = buf.shape
    tr, tc = _tile2d(rows, cols, n)
    nbr, nbc = rows // tr, cols // tc

    def kern(c_ref, b_ref, o_ref, w_ref):
        acc = b_ref[0].astype(F32)
        for i in range(1, n):
            acc = acc + b_ref[i].astype(F32)
        o_ref[...] = acc
        w_ref[...] = acc

    place = ((lambda i, j, c: (c[0] * nbr + i, j)) if axis == 0 else (lambda i, j, c: (i, c[0] * nbc + j)))
    whole = (2 * rows, cols) if axis == 0 else (rows, 2 * cols)
    grid_spec = pltpu.PrefetchScalarGridSpec(
        num_scalar_prefetch=1, grid=(nbr, nbc),
        in_specs=[pl.BlockSpec((n, tr, tc), lambda i, j, c: (0, i, j))],
        out_specs=[pl.BlockSpec((tr, tc), lambda i, j, c: (i, j)), pl.BlockSpec((tr, tc), place)])
    return _pcall(kern, name=name, grid_spec=grid_spec,
                  out_shape=[jax.ShapeDtypeStruct((rows, cols), F32), jax.ShapeDtypeStruct(whole, F32)])(core, buf)


def _pair_sum(g4, theirs, core, axis, name):
    n, rows, cols = theirs.shape
    tr, tc = _tile2d(rows, cols, 1)
    nbr, nbc = rows // tr, cols // tc

    def kern(c_ref, a_ref, b_ref, o_ref):
        o_ref[...] = (a_ref[...] + b_ref[...]).astype(BF16)

    blk = (1, tr, tc)
    mine = ((lambda s, i, j, c: (s, c[0] * nbr + i, j)) if axis == 0
            else (lambda s, i, j, c: (s, i, c[0] * nbc + j)))
    grid_spec = pltpu.PrefetchScalarGridSpec(
        num_scalar_prefetch=1, grid=(n, nbr, nbc),
        in_specs=[pl.BlockSpec(blk, mine), pl.BlockSpec(blk, lambda s, i, j, c: (s, i, j))],
        out_specs=pl.BlockSpec(blk, lambda s, i, j, c: (s, i, j)))
    return _pcall(kern, name=name, grid_spec=grid_spec,
                  out_shape=jax.ShapeDtypeStruct((n, rows, cols), BF16))(core, g4, theirs)


def _adamw(w, g, m, v, name):
    rows, cols = w.shape
    tr, tc = _tile2d(rows, cols, 4)
    bc1 = 1.0 - ADAM_B1 ** ADAM_STEP
    bc2 = 1.0 - ADAM_B2 ** ADAM_STEP

    def kern(w_ref, g_ref, m_ref, v_ref, d_ref, mo_ref, vo_ref):
        gv = g_ref[...]
        mn = ADAM_B1 * m_ref[...] + (1.0 - ADAM_B1) * gv
        vn = ADAM_B2 * v_ref[...] + (1.0 - ADAM_B2) * (gv * gv)
        d_ref[...] = -ADAM_LR * ((mn / bc1) / (jnp.sqrt(vn / bc2) + ADAM_EPS) + ADAM_WD * w_ref[...])
        mo_ref[...] = mn
        vo_ref[...] = vn

    blk = pl.BlockSpec((tr, tc), lambda i, j: (i, j))
    return _pcall(kern, name=name, grid=(rows // tr, cols // tc), in_specs=[blk] * 4, out_specs=[blk] * 3,
                  out_shape=[jax.ShapeDtypeStruct((rows, cols), F32)] * 3)(w, g, m, v)


ANY = pl.BlockSpec(memory_space=pl.ANY)


def _place():
    x, y, c = lax.axis_index("x"), lax.axis_index("y"), lax.axis_index("c")
    chips = [(1 - x, y), (x, 1 - y), (1 - x, 1 - y)]
    return x, y, c, chips


def _remote(src, dst, send_sem, recv_sem, to):
    return pltpu.make_async_remote_copy(src_ref=src, dst_ref=dst, send_sem=send_sem, recv_sem=recv_sem,
                                        device_id=to, device_id_type=MESH)


def _half(ref, lead, core, axis):
    size = ref.shape[len(lead) + axis] // 2
    cut = pl.ds(core * size, size)
    return ref.at[tuple(lead) + ((cut, slice(None)) if axis == 0 else (slice(None), cut))]


def _gather_shards(ws, axes):
    side = _gather_side(ws, axes)

    def body(*refs):
        nw = len(ws)
        split = (refs[:nw], refs[nw:2 * nw], refs[2 * nw:])
        side.phase("start", *split)
        side.phase("finish", *split)

    return _pcall(body, name="gather_shards", in_specs=[ANY] * len(ws), out_specs=[ANY] * len(ws),
                  out_shape=side.out_shape, scratch_shapes=side.scratch)(*ws)


def _gather_side(ws, axes):
    nw = len(ws)

    def phase(which, w_refs, out_refs, sems):
        send_sems, recv_sems = sems
        x, y, c, chips = _place()
        me = 2 * x + y
        sibling = (x, y, 1 - c)

        def part(i, slot, core):
            return _half(out_refs[i], (slot,), core, axes[i])

        def copy(sem, src, dst, to):
            return _remote(src, dst, send_sems.at[sem], recv_sems.at[sem], to)

        first = [copy(j * nw + i, _half(w_refs[i], (), c, axes[i]), part(i, me, c), (*chip, c))
                 for j, chip in enumerate(chips) for i in range(nw)]
        if which == "start":
            for cp in first:
                cp.start()
            return
        passed = []
        for j, (cx, cy) in enumerate(chips):
            slot = 2 * cx + cy
            for i in range(nw):
                copy(j * nw + i, part(i, slot, c), part(i, slot, c), (x, y, c)).wait_recv()
                fwd = copy((3 + j) * nw + i, part(i, slot, c), part(i, slot, c), sibling)
                fwd.start()
                passed.append(fwd)
        for j, (cx, cy) in enumerate(chips):
            slot = 2 * cx + cy
            for i in range(nw):
                copy((3 + j) * nw + i, part(i, slot, 1 - c), part(i, slot, 1 - c), (x, y, c)).wait_recv()
        for cp in first + passed:
            cp.wait_send()

    return _Side(list(ws), [jax.ShapeDtypeStruct((4,) + w.shape, w.dtype) for w in ws],
                 [pltpu.SemaphoreType.DMA((6 * nw,)), pltpu.SemaphoreType.DMA((6 * nw,))], phase)


def _half_shape(shape, axis):
    return tuple(d // 2 if k == len(shape) - 2 + axis else d for k, d in enumerate(shape))


def _pair_exchange(gs, axes, name):
    nw = len(gs)

    def body(*refs):
        g_refs, out_refs = refs[:nw], refs[nw:2 * nw]
        send_sems, recv_sems = refs[2 * nw:]
        x, y, c, _ = _place()
        cps = []
        for i in range(nw):
            cp = _remote(_half(g_refs[i], (slice(None),), 1 - c, axes[i]), out_refs[i],
                         send_sems.at[i], recv_sems.at[i], (x, y, 1 - c))
            cp.start()
            cps.append(cp)
        for cp in cps:
            cp.wait()

    return _pcall(body, name=name, in_specs=[ANY] * nw, out_specs=[ANY] * nw,
                  out_shape=[jax.ShapeDtypeStruct(_half_shape(g.shape, a), g.dtype) for g, a in zip(gs, axes)],
                  scratch_shapes=[pltpu.SemaphoreType.DMA((nw,)), pltpu.SemaphoreType.DMA((nw,))])(*gs)


def _chip_side(ps):
    nw = len(ps)

    def phase(which, p_refs, out_refs, sems):
        send_sems, recv_sems, local_sems = sems
        x, y, c, chips = _place()
        me = 2 * x + y
        mine = [pltpu.make_async_copy(p_refs[i].at[me], out_refs[i].at[me], local_sems.at[i]) for i in range(nw)]
        sends = [_remote(p_refs[i].at[2 * cx + cy], out_refs[i].at[me], send_sems.at[j * nw + i],
                         recv_sems.at[j * nw + i], (cx, cy, c))
                 for j, (cx, cy) in enumerate(chips) for i in range(nw)]
        if which == "start":
            for cp in mine + sends:
                cp.start()
            return
        for j, (cx, cy) in enumerate(chips):
            slot = 2 * cx + cy
            for i in range(nw):
                _remote(out_refs[i].at[slot], out_refs[i].at[slot], send_sems.at[j * nw + i],
                        recv_sems.at[j * nw + i], (x, y, c)).wait_recv()
        for cp in sends:
            cp.wait_send()
        for cp in mine:
            cp.wait()

    return _Side(list(ps), [jax.ShapeDtypeStruct(p.shape, p.dtype) for p in ps],
                 [pltpu.SemaphoreType.DMA((3 * nw,)), pltpu.SemaphoreType.DMA((3 * nw,)),
                  pltpu.SemaphoreType.DMA((nw,))], phase)


def _half_exchange(ts, wholes, axes):
    nw = len(ts)

    def body(*refs):
        t_refs, out_refs = refs[:nw], refs[2 * nw:3 * nw]
        send_sems, recv_sems = refs[3 * nw:]
        x, y, c, _ = _place()
        sends = []
        for i in range(nw):
            cp = _remote(t_refs[i], _half(out_refs[i], (), c, axes[i]), send_sems.at[i], recv_sems.at[i],
                         (x, y, 1 - c))
            cp.start()
            sends.append(cp)
        for i in range(nw):
            _remote(t_refs[i], _half(out_refs[i], (), 1 - c, axes[i]), send_sems.at[i], recv_sems.at[i],
                    (x, y, c)).wait_recv()
        for cp in sends:
            cp.wait_send()

    return _pcall(body, name="half_exchange", in_specs=[ANY] * (2 * nw), out_specs=[ANY] * nw,
                  out_shape=[jax.ShapeDtypeStruct(w.shape, w.dtype) for w in wholes],
                  input_output_aliases={nw + i: i for i in range(nw)},
                  scratch_shapes=[pltpu.SemaphoreType.DMA((nw,)), pltpu.SemaphoreType.DMA((nw,))])(*ts, *wholes)


def _gather_all(v):
    rows, cols = v.shape

    def body(v_ref, out_ref, send_sems, recv_sems, local_sem):
        x, y, c, _ = _place()
        me = 4 * x + 2 * y + c
        mine = pltpu.make_async_copy(v_ref, out_ref.at[me], local_sem)
        mine.start()
        sends = []
        for d in range(1, 8):
            peer = (x ^ (d >> 2), y ^ ((d >> 1) & 1), c ^ (d & 1))
            cp = _remote(v_ref, out_ref.at[me], send_sems.at[d - 1], recv_sems.at[d - 1], peer)
            cp.start()
            sends.append(cp)
        for d in range(1, 8):
            slot = 4 * (x ^ (d >> 2)) + 2 * (y ^ ((d >> 1) & 1)) + (c ^ (d & 1))
            _remote(v_ref, out_ref.at[slot], send_sems.at[d - 1], recv_sems.at[d - 1], (x, y, c)).wait_recv()
        for cp in sends:
            cp.wait_send()
        mine.wait()

    return _pcall(body, name="gather_all", in_specs=[ANY], out_specs=ANY,
                  out_shape=jax.ShapeDtypeStruct((8, rows, cols), v.dtype),
                  scratch_shapes=[pltpu.SemaphoreType.DMA((7,)), pltpu.SemaphoreType.DMA((7,)),
                                  pltpu.SemaphoreType.DMA])(v)


def _pad_cols(a, width):
    return a if a.shape[1] == width else jnp.pad(a, ((0, 0), (0, width - a.shape[1])))


def _unpad_segments():
    z = PAD["z"]
    segs = [(PAD["c_q"], 0, 512), (PAD["c_kv"], 512, 512), (PAD["k_rope"], 1024, 64), (z, 1088, 1024)]
    segs += [(PAD["q_nsa"] + 256 * h, 2112 + NSA_DK * h, NSA_DK) for h in range(NSA_HEADS)]
    for name, rows in (("k_c", 192), ("v_c", 128), ("k_s", 192), ("v_s", 128), ("k_w", 192), ("v_w", 128),
                       ("g_nsa", 12)):
        segs.append((PAD[name], ORIG[name][0], rows))
    segs += [(z + 1024, ORIG["z_nsa"][0], 512), (PAD["q_mem"], ORIG["q_mem"][0], 512),
             (z + 1536, ORIG["z_mem"][0], 512)]
    return segs


def _w_in_grad_slots(gt):
    rows, cols = gt.shape
    shard = sum(n for _, _, n in _unpad_segments()) // 4
    tc = 256
    pieces = []
    for src, dst, n in _unpad_segments():
        while n:
            slot, off = divmod(dst, shard)
            take = min(n, shard - off)
            pieces.append((src, slot, off, take))
            src, dst, n = src + take, dst + take, n - take

    def kern(g_ref, o_ref):
        for src, slot, off, take in pieces:
            o_ref[slot, off:off + take, :] = g_ref[src:src + take, :]

    return _pcall(kern, name="w_in_grad_slots", grid=(cols // tc,),
                  in_specs=[pl.BlockSpec((rows, tc), lambda i: (0, i))],
                  out_specs=pl.BlockSpec((4, shard, tc), lambda i: (0, 0, i)),
                  out_shape=jax.ShapeDtypeStruct((4, shard, cols), F32))(gt)


def _w_in_from_slots(ws):
    nslot, shard, cols = ws.shape
    tc = 256
    pieces = []
    for dst, src, n in _unpad_segments() + [(PAD["k_rope"] + 64, ORIG["k_rope"][0], 64)]:
        while n:
            slot, off = divmod(src, shard)
            take = min(n, shard - off)
            pieces.append((dst, slot, off, take))
            src, dst, n = src + take, dst + take, n - take

    def kern(w_ref, o_ref):
        o_ref[...] = jnp.zeros_like(o_ref)
        for dst, slot, off, take in pieces:
            o_ref[dst:dst + take, :] = w_ref[slot, off:off + take, :]

    return _pcall(kern, name="w_in_from_slots", grid=(cols // tc,),
                  in_specs=[pl.BlockSpec((nslot, shard, tc), lambda i: (0, 0, i))],
                  out_specs=pl.BlockSpec((D_PAD, tc), lambda i: (0, i)),
                  out_shape=jax.ShapeDtypeStruct((D_PAD, cols), ws.dtype))(ws)


def _rope_tables(s):
    pos = jnp.arange(s, dtype=F32)
    inv_freq = ROPE_THETA ** (-jnp.arange(0, 64, 2, dtype=F32) / 64)
    ang = pos[:, None] * inv_freq[None, :]
    cos, sin = jnp.cos(ang), jnp.sin(ang)
    z = jnp.zeros((s, 64), F32)
    return jnp.concatenate([cos, cos, z], axis=1), jnp.concatenate([-sin, sin, z], axis=1)


def _overlap_table(s):
    n_c, n_s = s // CMP_STRIDE, s // SLC_LEN
    c0 = np.arange(n_c)[:, None] * CMP_STRIDE
    s0 = np.arange(LANE)[None, :] * SLC_LEN
    ov = (c0 < s0 + SLC_LEN) & (c0 + CMP_LEN > s0) & (np.arange(n_c)[:, None] < n_c - 1) & (np.arange(LANE)[None, :] < n_s)
    return jnp.asarray(ov.astype(np.float32), dtype=BF16)


def _shift_down(a):
    return jnp.concatenate([jnp.zeros((8, a.shape[1]), a.dtype), a], axis=0)[7:7 + a.shape[0]]


def _shift_up(a):
    return jnp.concatenate([a, jnp.zeros((8, a.shape[1]), a.dtype)], axis=0)[1:1 + a.shape[0]]


def _local_step(x, mem, target, w, hooks=None):
    s = x.shape[0]
    cs, sn = _rope_tables(s)
    t_ = jnp.transpose

    if hooks is None:
        xn, rstd_x = _rms_fwd(_Src(x, D_MODEL), w["norm_g"], "norm_x")
    else:
        xn, rstd_x, w_in_slots = _rms_fwd(_Src(x, D_MODEL), w["norm_g"], "norm_x", hooks.w_in_side)
        w = {**w, "w_in_t": hooks.w_in(w_in_slots)}
    w_in_p = _w_in_from_slots(w["w_in_t"])
    if hooks is None:
        hp, hpb = _mm(xn, w_in_p, "in_proj", mode="nt", second_dtype=BF16)
    else:
        hp, hpb, *gathered = _mm(xn, w_in_p, "in_proj", mode="nt", second_dtype=BF16, side=hooks.gather_side)
        w = {**w, **hooks.weights(gathered)}

    w_uq3 = w["w_uq"].reshape(512, MLA_HEADS, 192)
    w_uq_p = jnp.concatenate([w_uq3, w_uq3[:, :, 128:]], axis=2).reshape(512, MLA_HEADS * 256)
    w_ukv_p = t_(w["w_ukv"].reshape(512, MLA_HEADS, 2, 128), (0, 2, 1, 3)).reshape(512, 2048)
    c_q, c_kv = _Src(hp, 512, 0), _Src(hp, 512, 1)
    cqn, rstd_q = _rms_fwd(c_q, w["q_norm_g"], "norm_q")
    ckvn, rstd_kv = _rms_fwd(c_kv, w["kv_norm_g"], "norm_kv")
    q_lin = _mm(cqn, w_uq_p, "mla_q_proj")
    kvb = _mm(ckvn, w_ukv_p, "mla_kv_proj", out_dtype=BF16)
    q_mla = _rope_fwd(_Src(q_lin, MLA_HEADS * 256), cs, sn, MLA_HEADS, 256, LANE, "rope_q")
    k_pe = _rope_fwd(_Src(hp, LANE, PAD["k_rope"] // LANE), cs, sn, 1, LANE, 0, "rope_k")
    mla = _Attn("mla", s, s, MLA_HEADS, 256)
    mla_q, mla_v = _Src(q_mla, 256), _Src(kvb, LANE, MLA_HEADS)
    mla_k = [_Src(kvb, LANE), _Src(k_pe, LANE, 0, False)]
    o_mla, l_mla, lr_mla = _attn_fwd(mla, mla_q, mla_k, mla_v, None, "mla_fwd")

    sk = s // CMP_STRIDE
    pe_k, pe_v = w["cmp_pe_k"], w["cmp_pe_v"]
    w1k = _pad_cols(w["cmp_w1k"], 256)
    w2k = jnp.pad(w["cmp_w2k"], ((0, 64), (0, 64))).astype(BF16)
    w1v, w2v = w["cmp_w1v"], w["cmp_w2v"].astype(BF16)
    half_k, half_v = CMP_STRIDE * NSA_DK, CMP_STRIDE * HEAD_V
    ak = hp[:, PAD["k_c"]:PAD["k_c"] + NSA_DK].reshape(sk, half_k)
    av = hp[:, PAD["v_c"]:PAD["v_c"] + HEAD_V].reshape(sk, half_v)
    ck_args = (ak, _shift_up(ak), pe_k[:CMP_STRIDE].reshape(1, half_k), pe_k[CMP_STRIDE:].reshape(1, half_k),
               w1k[:half_k], w1k[half_k:], w2k)
    cv_args = (av, _shift_up(av), pe_v[:CMP_STRIDE].reshape(1, half_v), pe_v[CMP_STRIDE:].reshape(1, half_v),
               w1v[:half_v], w1v[half_v:], w2v)
    k_cmp, pre_k = _compress_fwd(*ck_args, "compress_k")
    v_cmp, pre_v = _compress_fwd(*cv_args, "compress_v")
    cmp_ = _Attn("cmp", s, sk, NSA_HEADS, 256)
    slc = _Attn("slc", s, s, NSA_HEADS, 256)
    win = _Attn("win", s, s, NSA_HEADS, 256)
    nsa_q = _Src(hpb, 256, PAD["q_nsa"] // 256)
    cmp_k, cmp_v = [_Src(k_cmp, 256, 0, False)], _Src(v_cmp, HEAD_V, 0, False)
    slc_k, slc_v = [_Src(hpb, 256, PAD["k_s"] // 256, False)], _Src(hpb, HEAD_V, PAD["v_s"] // HEAD_V, False)
    win_k, win_v = [_Src(hpb, 256, PAD["k_w"] // 256, False)], _Src(hpb, HEAD_V, PAD["v_w"] // HEAD_V, False)
    o_cmp, l_cmp, lr_cmp, sel, selt = _attn_fwd_small(cmp_, nsa_q, cmp_k, cmp_v, "cmp_fwd", _overlap_table(s))
    o_slc, l_slc, lr_slc = _attn_fwd(slc, nsa_q, slc_k, slc_v, sel, "slc_fwd")
    o_win, l_win, lr_win = _attn_fwd_small(win, nsa_q, win_k, win_v, "win_fwd")
    gl = _Src(hp, LANE, PAD["g_nsa"] // LANE)
    o_nsa = _nsa_combine(o_cmp, o_slc, o_win, gl)

    mn, rstd_m = _rms_fwd(_Src(mem, D_MODEL), w["mem_norm_g"], "norm_mem")
    kvm = _mm(mn, w["w_mem_kv"], "mem_kv_proj", out_dtype=BF16)
    mem_ = _Attn("mem", s, mem.shape[0], MEM_HEADS, LANE)
    mem_q, mem_k, mem_v = _Src(hpb, LANE, PAD["q_mem"] // LANE), [_Src(kvm, LANE)], _Src(kvm, LANE, MEM_HEADS)
    o_mem, l_mem, lr_mem = _attn_fwd_small(mem_, mem_q, mem_k, mem_v, "mem_fwd")

    u = _gate_fwd(o_mla, o_nsa, o_mem, hp)
    proj = _mm(u, w["w_out"], "out_proj", wide=2048)
    dy, g_final, loss = _final_loss(x, proj, w["final_norm_g"].reshape(1, -1), target)

    g_w_out = _mm(u, dy, "out_proj_dw", mode="tn")
    du = _mm(dy, w["w_out"], "out_proj_dx", mode="nt", wide=2048)
    do_cat, dz = _gate_bwd(du, o_mla, o_nsa, o_mem, hp)

    dq_mem, (dk_mem,), dv_mem = _attn_bwd(mem_, mem_q, mem_k, mem_v, None, None, _Src(o_mem, HEAD_V), l_mem,
                                          lr_mem, _Src(do_cat, HEAD_V, 12), None, "mem_bwd")
    dkvm = jnp.concatenate([dk_mem, dv_mem], axis=1)
    g_w_mem_kv = _mm(mn, dkvm, "mem_kv_dw", mode="tn")
    dmn = _mm(dkvm, w["w_mem_kv"], "mem_kv_dx", mode="nt")
    _, g_mem_norm = _rms_bwd(_Src(mem, D_MODEL), w["mem_norm_g"], rstd_m, dmn, None, "norm_mem_bwd")

    do_cmp, do_slc, do_win, dgl = _nsa_combine_bwd(do_cat, o_cmp, o_slc, o_win, gl)
    dq_n, (dk_cmp,), dv_cmp = _attn_bwd(cmp_, nsa_q, cmp_k, cmp_v, None, None, _Src(o_cmp, HEAD_V), l_cmp,
                                        lr_cmp, _Src(do_cmp, HEAD_V), None, "cmp_bwd")
    dq_n, (dk_s,), dv_s = _attn_bwd(slc, nsa_q, slc_k, slc_v, sel, selt, _Src(o_slc, HEAD_V), l_slc, lr_slc,
                                    _Src(do_slc, HEAD_V), dq_n, "slc_bwd")
    dq_n, (dk_w,), dv_w = _attn_bwd(win, nsa_q, win_k, win_v, None, None, _Src(o_win, HEAD_V), l_win, lr_win,
                                    _Src(do_win, HEAD_V), dq_n, "win_bwd")
    dak, dpk_lo, dpk_hi, dw1k_lo, dw1k_hi, g_w2k = _compress_bwd(
        *ck_args, pre_k, _shift_down(pre_k), dk_cmp, _shift_down(dk_cmp), "compress_k_bwd")
    dav, dpv_lo, dpv_hi, dw1v_lo, dw1v_hi, g_w2v = _compress_bwd(
        *cv_args, pre_v, _shift_down(pre_v), dv_cmp, _shift_down(dv_cmp), "compress_v_bwd")
    g_pe_k = jnp.concatenate([dpk_lo.reshape(CMP_STRIDE, NSA_DK), dpk_hi.reshape(CMP_STRIDE, NSA_DK)], axis=0)
    g_pe_v = jnp.concatenate([dpv_lo.reshape(CMP_STRIDE, HEAD_V), dpv_hi.reshape(CMP_STRIDE, HEAD_V)], axis=0)
    g_w1k = jnp.concatenate([dw1k_lo, dw1k_hi], axis=0)[:, :NSA_DK]
    g_w1v = jnp.concatenate([dw1v_lo, dw1v_hi], axis=0)
    dk_c = _pad_cols(dak.reshape(s, NSA_DK), 256)
    dv_c = dav.reshape(s, HEAD_V)

    dq_m, (dk_nope, dk_pe), dv_m = _attn_bwd(mla, mla_q, mla_k, mla_v, None, None, _Src(o_mla, HEAD_V), l_mla,
                                             lr_mla, _Src(do_cat, HEAD_V), None, "mla_bwd")
    dq_lin = _rope_bwd_q(dq_m, cs, sn)
    dkv_lin, d_krope = _rope_bwd_k(dk_nope, dk_pe, dv_m, cs, sn)
    g_w_uq_p = _mm(cqn, dq_lin, "mla_q_dw", mode="tn")
    dcqn = _mm(dq_lin, w_uq_p, "mla_q_dx", mode="nt")
    g_w_ukv_p = _mm(ckvn, dkv_lin, "mla_kv_dw", mode="tn")
    dckvn = _mm(dkv_lin, w_ukv_p, "mla_kv_dx", mode="nt")
    dc_q, g_q_norm = _rms_bwd(c_q, w["q_norm_g"], rstd_q, dcqn, None, "norm_q_bwd", BF16)
    dc_kv, g_kv_norm = _rms_bwd(c_kv, w["kv_norm_g"], rstd_kv, dckvn, None, "norm_kv_bwd", BF16)
    g_w_uq = g_w_uq_p.reshape(512, MLA_HEADS, 256)[:, :, :192].reshape(512, MLA_HEADS * 192)
    g_w_ukv = t_(g_w_ukv_p.reshape(512, 2, MLA_HEADS, 128), (0, 2, 1, 3)).reshape(512, 2048)

    pieces = [dc_q, dc_kv, dq_n, dk_c, dk_s, dk_w, d_krope, dv_c, dv_s, dv_w, dgl,
              jnp.zeros((s, PAD["q_mem"] - (PAD["g_nsa"] + LANE)), BF16), dq_mem, dz]
    dhp = jnp.concatenate([p.astype(BF16) for p in pieces], axis=1)
    grads = dict(q_norm_g=g_q_norm, w_uq=g_w_uq, kv_norm_g=g_kv_norm,
                 w_ukv=g_w_ukv, cmp_pe_k=g_pe_k, cmp_pe_v=g_pe_v, cmp_w1k=g_w1k, cmp_w2k=g_w2k[:NSA_DK, :NSA_DK],
                 cmp_w1v=g_w1v, cmp_w2v=g_w2v, mem_norm_g=g_mem_norm, w_mem_kv=g_w_mem_kv, w_out=g_w_out,
                 final_norm_g=g_final.reshape(-1))
    if hooks is None:
        g_w_in_t = _w_in_grad_slots(_mm(dhp, xn, "in_proj_dw", mode="tn", wide=2048))
        dxn = _mm(dhp, w_in_p, "in_proj_dx", wide=2048)
    else:
        g_w_in_p, *hooks.received = _mm(dhp, xn, "in_proj_dw", mode="tn", wide=2048, side=hooks.reduce_side(grads))
        g_w_in_t = _w_in_grad_slots(g_w_in_p)
        dxn, hooks.received_w_in = _mm(dhp, w_in_p, "in_proj_dx", wide=2048, side=hooks.reduce_side_w_in(g_w_in_t))
    grad_x, g_norm = _rms_bwd(_Src(x, D_MODEL), w["norm_g"], rstd_x, dxn, dy, "norm_x_bwd")
    grads.update(norm_g=g_norm, w_in_t=g_w_in_t)
    return loss[0, 0], grad_x, grads


def kernel(x, mem, norm_g, w_in, q_norm_g, w_uq, kv_norm_g, w_ukv, cmp_pe_k, cmp_pe_v, cmp_w1k, cmp_w2k, cmp_w1v, cmp_w2v, mem_norm_g, w_mem_kv, w_out, final_norm_g, loss_target, m_norm_g, m_w_in, m_q_norm_g, m_w_uq, m_kv_norm_g, m_w_ukv, m_cmp_pe_k, m_cmp_pe_v, m_cmp_w1k, m_cmp_w2k, m_cmp_w1v, m_cmp_w2v, m_mem_norm_g, m_w_mem_kv, m_w_out, m_final_norm_g, v_norm_g, v_w_in, v_q_norm_g, v_w_uq, v_kv_norm_g, v_w_ukv, v_cmp_pe_k, v_cmp_pe_v, v_cmp_w1k, v_cmp_w2k, v_cmp_w1v, v_cmp_w2v, v_mem_norm_g, v_w_mem_kv, v_w_out, v_final_norm_g):
    args = dict(locals())
    wts = {n: args[n] for n in WEIGHTS}
    loc = {n: (a if n == "final_norm_g" else a[0]) for n, a in wts.items()}

    def to_x(n, a):
        return a.T if n == "w_in" else a

    split = [1 if n == "w_in" else 0 for n in SHARDED]
    rest = [n for n in SHARDED if n != "w_in"]
    chip = 2 * lax.axis_index("x") + lax.axis_index("y")
    core = lax.axis_index("c").astype(jnp.int32).reshape(1)
    own = {n: to_x(n, loc[n]).astype(BF16) for n in SHARDED}

    def with_own_slot(gw, a):
        return lax.dynamic_update_slice(gw, a[None], (chip, 0, 0))

    def slots(n, a):
        if n == "w_in":
            return a
        if SHARD_AXIS[n] == 0:
            return a.reshape(4, a.shape[0] // 4, a.shape[1])
        width = a.shape[1] // 4
        return jnp.stack([a[:, j * width:(j + 1) * width] for j in range(4)])

    def pair_sums(names, grads, name):
        axes = [1 if n == "w_in" else 0 for n in names]
        gs = [slots(n, a) for n, a in zip(names, grads)]
        theirs = _pair_exchange(gs, axes, name)
        return [_pair_sum(a, b, core, ax, "pair_sum_" + n) for n, a, b, ax in zip(names, gs, theirs, axes)]

    class Hooks:
        gather_side = _gather_side([own[n] for n in rest], [0] * len(rest))
        w_in_side = _gather_side([own["w_in"]], [1])
        received = None

        @staticmethod
        def w_in(slots):
            return with_own_slot(slots, own["w_in"])

        @staticmethod
        def weights(gathered):
            out = {}
            for n, gw in zip(rest, gathered):
                gw = with_own_slot(gw, own[n])
                if SHARD_AXIS[n] == 0:
                    out[n] = gw.reshape(4 * gw.shape[1], gw.shape[2])
                else:
                    out[n] = jnp.concatenate([gw[j] for j in range(4)], axis=1)
            return out

        @staticmethod
        def reduce_side(grads):
            return _chip_side(pair_sums(rest, [grads[n] for n in rest], "pair_exchange_rest"))

        @staticmethod
        def reduce_side_w_in(g_w_in_t):
            return _chip_side(pair_sums(["w_in"], [g_w_in_t], "pair_exchange_w_in"))

    hooks = Hooks()

    start = {n: loc[n].reshape(1, -1) if loc[n].ndim == 1 else loc[n] for n in REPLICATED}
    loss, grad_x, g = _local_step(x[0], mem[0], loss_target[0], start, hooks)
    loss = lax.psum(loss, ("x", "y", "c"))

    from_chips = dict(zip(rest, hooks.received), w_in=hooks.received_w_in)
    sums = [_chip_sum(from_chips[n], core, ax, "chip_sum_" + n) for n, ax in zip(SHARDED, split)]
    g_sh = _half_exchange([a for a, _ in sums], [b for _, b in sums], split)

    n_rep = sum(int(np.prod(loc[n].shape)) for n in REPLICATED)
    rows_rep = -(-n_rep // (8 * LANE)) * 8

    def rep_pack(parts):
        flat = jnp.concatenate([p.reshape(-1) for p in parts])
        return jnp.pad(flat, (0, rows_rep * LANE - n_rep)).reshape(rows_rep, LANE)

    g_rep = _sum_slots(_gather_all(rep_pack([g[n] for n in REPLICATED])), "replica_sum")
    d_rp, m_rp, v_rp = _adamw(rep_pack([wts[n] for n in REPLICATED]), g_rep,
                              rep_pack([args["m_" + n] for n in REPLICATED]),
                              rep_pack([args["v_" + n] for n in REPLICATED]), "adamw_replicated")

    def rep_unpack(buf):
        flat, out, o = buf.reshape(-1), {}, 0
        for n in REPLICATED:
            size = int(np.prod(wts[n].shape))
            out[n] = flat[o:o + size].reshape(wts[n].shape)
            o += size
        return out

    outs = {k: rep_unpack(b) for k, b in (("g", g_rep), ("d", d_rp), ("m", m_rp), ("v", v_rp))}
    for n, gn in zip(SHARDED, g_sh):
        d, mo, vo = _adamw(to_x(n, loc[n]), gn, to_x(n, args["m_" + n][0]), to_x(n, args["v_" + n][0]),
                           "adamw_" + n)
        for k, a in (("g", gn), ("d", d), ("m", mo), ("v", vo)):
            outs[k][n] = to_x(n, a).reshape(wts[n].shape)

    return (loss, grad_x[None], *[outs["g"][n] for n in WEIGHTS], *[outs["d"][n] for n in WEIGHTS],
            *[outs["m"][n] for n in WEIGHTS], *[outs["v"][n] for n in WEIGHTS])
```
